```python
import jax, jax.numpy as jnp
from jax import lax
import numpy as np

D_MODEL = 1024
BATCH = 8
SEQ = 4096
DEPTH = 1

HG_WIDTH = D_MODEL // 2
HG_HEAD_DIM = 128
HG_HEADS = HG_WIDTH // HG_HEAD_DIM
CONV_WIDTH = D_MODEL // 2
CONV_GROUPS = 8
CONV_K = 3
D_FF = ((-(-8 * D_MODEL // 3) + 255) // 256) * 256
CHUNK = 32
EPS = 1e-6
SPLIT_SIZES = (HG_WIDTH, HG_WIDTH, HG_WIDTH, HG_WIDTH, CONV_WIDTH, CONV_WIDTH, CONV_WIDTH, D_MODEL, D_MODEL)
N_IN = 4 * HG_WIDTH + 3 * CONV_WIDTH + 2 * D_MODEL

kernel_name = 'hybrid_hgrn2_shortconv_gated_block'


def rmsnorm(x, g):
    xf = x.astype(jnp.float32)
    y = xf * lax.rsqrt(jnp.mean(xf * xf, axis=-1, keepdims=True) + EPS)
    return (y * g.astype(jnp.float32)).astype(x.dtype)


def hgrn2_chunkwise(q, k, v, log_f):
    B, L, H, dk = q.shape
    dv = v.shape[-1]
    n = L // CHUNK

    def to_chunks(t):
        return t.reshape(B, n, CHUNK, H, t.shape[-1]).transpose(1, 0, 3, 2, 4)

    qc, kc, vc, gc = to_chunks(q), to_chunks(k), to_chunks(v), to_chunks(log_f)
    b = jnp.cumsum(gc, axis=3)
    anchor = b[:, :, :, CHUNK // 2 - 1:CHUNK // 2, :]
    q_hat = qc * jnp.exp(b - anchor)
    k_hat = kc * jnp.exp(anchor - b)
    scores = jnp.einsum('nbhid,nbhjd->nbhij', q_hat, k_hat)
    causal = jnp.tril(jnp.ones((CHUNK, CHUNK), dtype=bool))
    scores = jnp.where(causal, scores, 0.0)
    o_intra = jnp.einsum('nbhij,nbhjv->nbhiv', scores, vc)
    b_last = b[:, :, :, -1:, :]
    q_in = qc * jnp.exp(b)
    k_out = kc * jnp.exp(b_last - b)
    chunk_decay = jnp.exp(b_last[:, :, :, 0, :])

    def step(S, inp):
        q_i, k_o, v_c, dec = inp
        o = jnp.einsum('bhid,bhdv->bhiv', q_i, S)
        S = dec[..., None] * S + jnp.einsum('bhjd,bhjv->bhdv', k_o, v_c)
        return S, o

    S0 = jnp.zeros((B, H, dk, dv), jnp.float32)
    _, o_inter = lax.scan(step, S0, (q_in, k_out, vc, chunk_decay))
    o = o_intra + o_inter
    return o.transpose(1, 0, 3, 2, 4).reshape(B, L, H, dv)


def hgrn2_mixer(q_raw, f_raw, i_raw, g_raw, lb, norm_g):
    B, L, _ = q_raw.shape

    def heads(t):
        return t.reshape(B, L, HG_HEADS, HG_HEAD_DIM).astype(jnp.float32)

    q = jax.nn.silu(heads(q_raw)) * (HG_HEAD_DIM ** -0.5)
    lbh = lb.astype(jnp.float32).reshape(HG_HEADS, HG_HEAD_DIM)
    f = lbh + (1.0 - lbh) * jax.nn.sigmoid(heads(f_raw))
    o = hgrn2_chunkwise(q, 1.0 - f, heads(i_raw), jnp.log(f))
    o = rmsnorm(o, norm_g) * jax.nn.silu(heads(g_raw))
    return o.reshape(B, L, HG_WIDTH).astype(q_raw.dtype)


def short_conv_mixer(c_gate, b_gate, xb, conv_w):
    u = c_gate * xb
    rhs = conv_w.astype(u.dtype)[:, None, :]
    y = lax.conv_general_dilated(u, rhs, window_strides=(1,), padding=[(CONV_K - 1, 0)],
                                 dimension_numbers=('NWC', 'WIO', 'NWC'),
                                 feature_group_count=CONV_WIDTH)
    return b_gate * y


def swiglu(h, w_gate, w_up, w_down):
    return (jax.nn.silu(h @ w_gate) * (h @ w_up)) @ w_down


def _fwd_setup_inputs(seed: int = 0) -> dict:
    key = jax.random.key(seed)
    ks = jax.random.split(key, 14)

    def nrm(k, shape, fan):
        return jax.random.normal(k, shape, jnp.float32) * (fan ** -0.5)

    def gain(k, shape):
        return 1.0 + 0.02 * jax.random.normal(k, shape, jnp.float32)

    return {
        'x': jax.random.normal(ks[0], (BATCH, SEQ, D_MODEL), jnp.float32),
        'norm_mix_g': gain(ks[1], (DEPTH, D_MODEL)),
        'w_in': nrm(ks[2], (DEPTH, D_MODEL, N_IN), D_MODEL),
        'lower_bounds': 0.1 * jax.random.normal(ks[3], (DEPTH + 1, HG_WIDTH), jnp.float32),
        'hg_norm_g': gain(ks[4], (DEPTH, HG_HEAD_DIM)),
        'conv_w': nrm(ks[5], (DEPTH, CONV_K, CONV_WIDTH), CONV_K),
        'w_branch_a': nrm(ks[6], (DEPTH, HG_WIDTH, D_MODEL), HG_WIDTH),
        'w_branch_b': nrm(ks[7], (DEPTH, CONV_WIDTH, D_MODEL), CONV_WIDTH),
        'w_out': nrm(ks[8], (DEPTH, D_MODEL, D_MODEL), D_MODEL),
        'norm_ffn_g': gain(ks[9], (DEPTH, D_MODEL)),
        'w_ffn_gate': nrm(ks[10], (DEPTH, D_MODEL, D_FF), D_MODEL),
        'w_ffn_up': nrm(ks[11], (DEPTH, D_MODEL, D_FF), D_MODEL),
        'w_ffn_down': nrm(ks[12], (DEPTH, D_FF, D_MODEL), D_FF),
        'norm_final_g': gain(ks[13], (D_MODEL,)),
    }


def _fwd_reference(x, norm_mix_g, w_in, lower_bounds, hg_norm_g, conv_w, w_branch_a, w_branch_b,
              w_out, norm_ffn_g, w_ffn_gate, w_ffn_up, w_ffn_down, norm_final_g):
    offsets = [int(o) for o in np.cumsum(SPLIT_SIZES)[:-1]]
    lb_all = jnp.cumsum(jax.nn.softmax(lower_bounds.astype(jnp.float32), axis=0), axis=0)
    for l in range(DEPTH):
        h = rmsnorm(x, norm_mix_g[l])
        proj = h @ w_in[l]
        q_raw, f_raw, i_raw, g_raw, c_gate, b_gate, xb, gate_a, gate_b = jnp.split(proj, offsets, axis=-1)
        y_a = hgrn2_mixer(q_raw, f_raw, i_raw, g_raw, lb_all[l], hg_norm_g[l]) @ w_branch_a[l]
        y_b = short_conv_mixer(c_gate, b_gate, xb, conv_w[l]) @ w_branch_b[l]
        merged = jax.nn.sigmoid(gate_a) * y_a + jax.nn.sigmoid(gate_b) * y_b
        x = x + merged @ w_out[l]
        h2 = rmsnorm(x, norm_ffn_g[l])
        x = x + swiglu(h2, w_ffn_gate[l], w_ffn_up[l], w_ffn_down[l])
    return rmsnorm(x, norm_final_g)


import jax as _jax
import jax.numpy as _jnp

TWIN_FORMAT = 'train_step'
FWD_PARAMS = ['x', 'norm_mix_g', 'w_in', 'lower_bounds', 'hg_norm_g', 'conv_w', 'w_branch_a', 'w_branch_b', 'w_out', 'norm_ffn_g', 'w_ffn_gate', 'w_ffn_up', 'w_ffn_down', 'norm_final_g']
TWIN_WEIGHTS = ['norm_mix_g', 'w_in', 'lower_bounds', 'hg_norm_g', 'conv_w', 'w_branch_a', 'w_branch_b', 'w_out', 'norm_ffn_g', 'w_ffn_gate', 'w_ffn_up', 'w_ffn_down', 'norm_final_g']
TWIN_DIFF_INPUT = 'x'
TWIN_INPUTS = ['x', 'norm_mix_g', 'w_in', 'lower_bounds', 'hg_norm_g', 'conv_w', 'w_branch_a', 'w_branch_b', 'w_out', 'norm_ffn_g', 'w_ffn_gate', 'w_ffn_up', 'w_ffn_down', 'norm_final_g', 'loss_target', 'm_norm_mix_g', 'm_w_in', 'm_lower_bounds', 'm_hg_norm_g', 'm_conv_w', 'm_w_branch_a', 'm_w_branch_b', 'm_w_out', 'm_norm_ffn_g', 'm_w_ffn_gate', 'm_w_ffn_up', 'm_w_ffn_down', 'm_norm_final_g', 'v_norm_mix_g', 'v_w_in', 'v_lower_bounds', 'v_hg_norm_g', 'v_conv_w', 'v_w_branch_a', 'v_w_branch_b', 'v_w_out', 'v_norm_ffn_g', 'v_w_ffn_gate', 'v_w_ffn_up', 'v_w_ffn_down', 'v_norm_final_g']
TWIN_OUTPUTS = ['loss', 'grad_x', 'grad_norm_mix_g', 'grad_w_in', 'grad_lower_bounds', 'grad_hg_norm_g', 'grad_conv_w', 'grad_w_branch_a', 'grad_w_branch_b', 'grad_w_out', 'grad_norm_ffn_g', 'grad_w_ffn_gate', 'grad_w_ffn_up', 'grad_w_ffn_down', 'grad_norm_final_g', 'delta_norm_mix_g', 'delta_w_in', 'delta_lower_bounds', 'delta_hg_norm_g', 'delta_conv_w', 'delta_w_branch_a', 'delta_w_branch_b', 'delta_w_out', 'delta_norm_ffn_g', 'delta_w_ffn_gate', 'delta_w_ffn_up', 'delta_w_ffn_down', 'delta_norm_final_g', 'new_m_norm_mix_g', 'new_m_w_in', 'new_m_lower_bounds', 'new_m_hg_norm_g', 'new_m_conv_w', 'new_m_w_branch_a', 'new_m_w_branch_b', 'new_m_w_out', 'new_m_norm_ffn_g', 'new_m_w_ffn_gate', 'new_m_w_ffn_up', 'new_m_w_ffn_down', 'new_m_norm_final_g', 'new_v_norm_mix_g', 'new_v_w_in', 'new_v_lower_bounds', 'new_v_hg_norm_g', 'new_v_conv_w', 'new_v_w_branch_a', 'new_v_w_branch_b', 'new_v_w_out', 'new_v_norm_ffn_g', 'new_v_w_ffn_gate', 'new_v_w_ffn_up', 'new_v_w_ffn_down', 'new_v_norm_final_g']
TWIN_LEAF_KINDS = {'loss': 'loss', 'grad_x': 'grad_x', 'grad_norm_mix_g': 'grad_w', 'grad_w_in': 'grad_w', 'grad_lower_bounds': 'grad_w', 'grad_hg_norm_g': 'grad_w', 'grad_conv_w': 'grad_w', 'grad_w_branch_a': 'grad_w', 'grad_w_branch_b': 'grad_w', 'grad_w_out': 'grad_w', 'grad_norm_ffn_g': 'grad_w', 'grad_w_ffn_gate': 'grad_w', 'grad_w_ffn_up': 'grad_w', 'grad_w_ffn_down': 'grad_w', 'grad_norm_final_g': 'grad_w', 'delta_norm_mix_g': 'delta_w', 'delta_w_in': 'delta_w', 'delta_lower_bounds': 'delta_w', 'delta_hg_norm_g': 'delta_w', 'delta_conv_w': 'delta_w', 'delta_w_branch_a': 'delta_w', 'delta_w_branch_b': 'delta_w', 'delta_w_out': 'delta_w', 'delta_norm_ffn_g': 'delta_w', 'delta_w_ffn_gate': 'delta_w', 'delta_w_ffn_up': 'delta_w', 'delta_w_ffn_down': 'delta_w', 'delta_norm_final_g': 'delta_w', 'new_m_norm_mix_g': 'new_m', 'new_m_w_in': 'new_m', 'new_m_lower_bounds': 'new_m', 'new_m_hg_norm_g': 'new_m', 'new_m_conv_w': 'new_m', 'new_m_w_branch_a': 'new_m', 'new_m_w_branch_b': 'new_m', 'new_m_w_out': 'new_m', 'new_m_norm_ffn_g': 'new_m', 'new_m_w_ffn_gate': 'new_m', 'new_m_w_ffn_up': 'new_m', 'new_m_w_ffn_down': 'new_m', 'new_m_norm_final_g': 'new_m', 'new_v_norm_mix_g': 'new_v', 'new_v_w_in': 'new_v', 'new_v_lower_bounds': 'new_v', 'new_v_hg_norm_g': 'new_v', 'new_v_conv_w': 'new_v', 'new_v_w_branch_a': 'new_v', 'new_v_w_branch_b': 'new_v', 'new_v_w_out': 'new_v', 'new_v_norm_ffn_g': 'new_v', 'new_v_w_ffn_gate': 'new_v', 'new_v_w_ffn_up': 'new_v', 'new_v_w_ffn_down': 'new_v', 'new_v_norm_final_g': 'new_v'}


def _forward(args):
    return _fwd_reference(*[args[k] for k in FWD_PARAMS])


def _output_shape():
    def fwd():
        inp = _fwd_setup_inputs(0)
        return _fwd_reference(*[inp[k] for k in FWD_PARAMS])
    out = _jax.eval_shape(fwd)
    return out.shape, out.dtype

N_MICROBATCH = 1
ADAM_LR = 0.001
ADAM_B1 = 0.9
ADAM_B2 = 0.999
ADAM_EPS = 1e-08
ADAM_WD = 0.01
ADAM_STEP = 10
PER_EXAMPLE_BATCH_AXIS = {'x': 0, 'loss_target': 0}
SHARED_INPUTS = []
_WEIGHT_DTYPES = {'norm_mix_g': _jnp.float32, 'w_in': _jnp.float32, 'lower_bounds': _jnp.float32, 'hg_norm_g': _jnp.float32, 'conv_w': _jnp.float32, 'w_branch_a': _jnp.float32, 'w_branch_b': _jnp.float32, 'w_out': _jnp.float32, 'norm_ffn_g': _jnp.float32, 'w_ffn_gate': _jnp.float32, 'w_ffn_up': _jnp.float32, 'w_ffn_down': _jnp.float32, 'norm_final_g': _jnp.float32}
MOMENT_SCALE = {'norm_mix_g': 2.000844e-01, 'w_in': 7.642050e-02, 'lower_bounds': 7.424871e-03, 'hg_norm_g': 1.517246e-01, 'conv_w': 1.236545e-01, 'w_branch_a': 5.320222e-02, 'w_branch_b': 9.077112e-02, 'w_out': 1.052422e-01, 'norm_ffn_g': 1.194553e-01, 'w_ffn_gate': 4.958688e-02, 'w_ffn_up': 4.802350e-02, 'w_ffn_down': 7.965186e-02, 'norm_final_g': 3.200549e+01}


def _to_microbatches(a, axis):
    t = _jnp.moveaxis(a, axis, 0)
    t = t.reshape((N_MICROBATCH, t.shape[0] // N_MICROBATCH) + t.shape[1:])
    return _jnp.moveaxis(t, 1, axis + 1)


def setup_inputs(seed: int = 0) -> dict:
    inp = _fwd_setup_inputs(seed)
    key = _jax.random.fold_in(_jax.random.key(seed), 7919)
    shape, _ = _output_shape()
    out = dict(inp)
    out["loss_target"] = _jax.random.normal(_jax.random.fold_in(key, 0), shape, _jnp.float32)
    for i, name in enumerate(TWIN_WEIGHTS):
        w = inp[name].astype(_jnp.float32)
        if MOMENT_SCALE is None:
            s = _jnp.sqrt(_jnp.mean(_jnp.square(w)) + 1e-30)
        else:
            s = MOMENT_SCALE[name]
        km, kv = _jax.random.split(_jax.random.fold_in(key, i + 1))
        out[name] = w
        out["m_" + name] = s * _jax.random.normal(km, w.shape, _jnp.float32)
        out["v_" + name] = (s * s) * _jax.random.uniform(kv, w.shape, _jnp.float32, 0.5, 1.5)
    if N_MICROBATCH > 1:
        for name, axis in PER_EXAMPLE_BATCH_AXIS.items():
            out[name] = _to_microbatches(out[name], axis)
    return {'x': out['x'], 'norm_mix_g': out['norm_mix_g'], 'w_in': out['w_in'], 'lower_bounds': out['lower_bounds'], 'hg_norm_g': out['hg_norm_g'], 'conv_w': out['conv_w'], 'w_branch_a': out['w_branch_a'], 'w_branch_b': out['w_branch_b'], 'w_out': out['w_out'], 'norm_ffn_g': out['norm_ffn_g'], 'w_ffn_gate': out['w_ffn_gate'], 'w_ffn_up': out['w_ffn_up'], 'w_ffn_down': out['w_ffn_down'], 'norm_final_g': out['norm_final_g'], 'loss_target': out['loss_target'], 'm_norm_mix_g': out['m_norm_mix_g'], 'm_w_in': out['m_w_in'], 'm_lower_bounds': out['m_lower_bounds'], 'm_hg_norm_g': out['m_hg_norm_g'], 'm_conv_w': out['m_conv_w'], 'm_w_branch_a': out['m_w_branch_a'], 'm_w_branch_b': out['m_w_branch_b'], 'm_w_out': out['m_w_out'], 'm_norm_ffn_g': out['m_norm_ffn_g'], 'm_w_ffn_gate': out['m_w_ffn_gate'], 'm_w_ffn_up': out['m_w_ffn_up'], 'm_w_ffn_down': out['m_w_ffn_down'], 'm_norm_final_g': out['m_norm_final_g'], 'v_norm_mix_g': out['v_norm_mix_g'], 'v_w_in': out['v_w_in'], 'v_lower_bounds': out['v_lower_bounds'], 'v_hg_norm_g': out['v_hg_norm_g'], 'v_conv_w': out['v_conv_w'], 'v_w_branch_a': out['v_w_branch_a'], 'v_w_branch_b': out['v_w_branch_b'], 'v_w_out': out['v_w_out'], 'v_norm_ffn_g': out['v_norm_ffn_g'], 'v_w_ffn_gate': out['v_w_ffn_gate'], 'v_w_ffn_up': out['v_w_ffn_up'], 'v_w_ffn_down': out['v_w_ffn_down'], 'v_norm_final_g': out['v_norm_final_g']}


def _loss(weights, diff, rest, loss_target):
    with _jax.named_scope("forward"):
        args = {**rest, TWIN_DIFF_INPUT: diff, **{k: w.astype(_WEIGHT_DTYPES[k]) for k, w in weights.items()}}
        y = _forward(args)
    with _jax.named_scope("loss_head"):
        err = _jnp.square(y.astype(_jnp.float32) - loss_target)
        return 0.5 * _jnp.sum(_jnp.mean(err, axis=-1)) if err.ndim else 0.5 * err


def _adamw(w, g, m, v):
    m = ADAM_B1 * m + (1.0 - ADAM_B1) * g
    v = ADAM_B2 * v + (1.0 - ADAM_B2) * _jnp.square(g)
    m_hat = m / (1.0 - ADAM_B1 ** ADAM_STEP)
    v_hat = v / (1.0 - ADAM_B2 ** ADAM_STEP)
    delta = -ADAM_LR * (m_hat / (_jnp.sqrt(v_hat) + ADAM_EPS) + ADAM_WD * w)
    return delta, m, v


def reference(x, norm_mix_g, w_in, lower_bounds, hg_norm_g, conv_w, w_branch_a, w_branch_b, w_out, norm_ffn_g, w_ffn_gate, w_ffn_up, w_ffn_down, norm_final_g, loss_target, m_norm_mix_g, m_w_in, m_lower_bounds, m_hg_norm_g, m_conv_w, m_w_branch_a, m_w_branch_b, m_w_out, m_norm_ffn_g, m_w_ffn_gate, m_w_ffn_up, m_w_ffn_down, m_norm_final_g, v_norm_mix_g, v_w_in, v_lower_bounds, v_hg_norm_g, v_conv_w, v_w_branch_a, v_w_branch_b, v_w_out, v_norm_ffn_g, v_w_ffn_gate, v_w_ffn_up, v_w_ffn_down, v_norm_final_g):
    given = dict(x=x, norm_mix_g=norm_mix_g, w_in=w_in, lower_bounds=lower_bounds, hg_norm_g=hg_norm_g, conv_w=conv_w, w_branch_a=w_branch_a, w_branch_b=w_branch_b, w_out=w_out, norm_ffn_g=norm_ffn_g, w_ffn_gate=w_ffn_gate, w_ffn_up=w_ffn_up, w_ffn_down=w_ffn_down, norm_final_g=norm_final_g, loss_target=loss_target, m_norm_mix_g=m_norm_mix_g, m_w_in=m_w_in, m_lower_bounds=m_lower_bounds, m_hg_norm_g=m_hg_norm_g, m_conv_w=m_conv_w, m_w_branch_a=m_w_branch_a, m_w_branch_b=m_w_branch_b, m_w_out=m_w_out, m_norm_ffn_g=m_norm_ffn_g, m_w_ffn_gate=m_w_ffn_gate, m_w_ffn_up=m_w_ffn_up, m_w_ffn_down=m_w_ffn_down, m_norm_final_g=m_norm_final_g, v_norm_mix_g=v_norm_mix_g, v_w_in=v_w_in, v_lower_bounds=v_lower_bounds, v_hg_norm_g=v_hg_norm_g, v_conv_w=v_conv_w, v_w_branch_a=v_w_branch_a, v_w_branch_b=v_w_branch_b, v_w_out=v_w_out, v_norm_ffn_g=v_norm_ffn_g, v_w_ffn_gate=v_w_ffn_gate, v_w_ffn_up=v_w_ffn_up, v_w_ffn_down=v_w_ffn_down, v_norm_final_g=v_norm_final_g)
    weights = {n: given[n] for n in TWIN_WEIGHTS}
    shared = {n: given[n] for n in SHARED_INPUTS}
    per_example = {n: given[n] for n in ['x']}
    grad_fn = _jax.value_and_grad(_loss, argnums=(0, 1))

    def one_microbatch(ex, loss_target):
        ex = dict(ex)
        diff = ex.pop(TWIN_DIFF_INPUT)
        return grad_fn(weights, diff, {**shared, **ex}, loss_target)

    if N_MICROBATCH == 1:
        loss, (grad_w, grad_x) = one_microbatch(per_example, given["loss_target"])
    else:
        def body(carry, xs):
            loss_sum, grad_sum = carry
            l_k, (gw_k, gx_k) = one_microbatch(xs[0], xs[1])
            with _jax.named_scope("update"):
                return (loss_sum + l_k, _jax.tree.map(_jnp.add, grad_sum, gw_k)), gx_k

        init = (_jnp.zeros((), _jnp.float32), _jax.tree.map(_jnp.zeros_like, weights))
        (loss, grad_w), grad_x = _jax.lax.scan(body, init, (per_example, given["loss_target"]))
    with _jax.named_scope("update"):
        delta_w, new_m, new_v = {}, {}, {}
        for n in TWIN_WEIGHTS:
            delta_w[n], new_m[n], new_v[n] = _adamw(weights[n], grad_w[n], given["m_" + n], given["v_" + n])
    return (loss, grad_x, *[grad_w[n] for n in TWIN_WEIGHTS], *[delta_w[n] for n in TWIN_WEIGHTS],
            *[new_m[n] for n in TWIN_WEIGHTS], *[new_v[n] for n in TWIN_WEIGHTS])
```

```python
import functools

import jax
import jax.numpy as jnp
from jax import lax
from jax.experimental import pallas as pl
from jax.experimental.pallas import tpu as pltpu

F32 = jnp.float32
BF16 = jnp.bfloat16

D_MODEL = 1024
HG_WIDTH = 512
HEAD_DIM = 128
N_HEADS = 4
CONV_WIDTH = 512
D_FF = 2816
CHUNK = 32
EPS = 1e-6
Q_SCALE = HEAD_DIM ** -0.5
N_DEV = 8

ADAM_LR = 0.001
ADAM_B1 = 0.9
ADAM_B2 = 0.999
ADAM_EPS = 1e-08
ADAM_WD = 0.01
ADAM_STEP = 10

VMEM_LIMIT_V7X = 56 * 1024 * 1024

PACK_SPLITS = (("w_in", 704), ("w_branch_a", 64), ("w_branch_b", 64), ("w_out", 128),
               ("w_ffn_gate", 352), ("w_ffn_up", 352), ("w_ffn_down", 352))
PACK_ROWS = sum(r for _, r in PACK_SPLITS)
PACK_OFF = {}
_o = 0
for _n, _r in PACK_SPLITS:
    PACK_OFF[_n] = (_o, _r)
    _o += _r

SMALL_ROWS = 64


def _params(sem, vmem=VMEM_LIMIT_V7X):
    return pltpu.CompilerParams(dimension_semantics=sem, vmem_limit_bytes=vmem)


def _mm(a, b):
    return jnp.dot(a.astype(BF16), b.astype(BF16), preferred_element_type=F32)


def _mm_nt(a, b):
    return lax.dot_general(a.astype(BF16), b.astype(BF16), (((1,), (1,)), ((), ())), preferred_element_type=F32)


def _mm_tn(a, b):
    return lax.dot_general(a.astype(BF16), b.astype(BF16), (((0,), (0,)), ((), ())), preferred_element_type=F32)


def _sigmoid(x):
    return 1.0 / (1.0 + jnp.exp(-x))


def _resident(shape):
    nd = len(shape)
    return pl.BlockSpec(shape, lambda *_: (0,) * nd, pipeline_mode=pl.Buffered(1))


def _full(shape):
    nd = len(shape)
    return pl.BlockSpec(shape, lambda *_: (0,) * nd)


def _fwd_in(x, g, w_hg, w_cv, w_gt):
    T = x.shape[0]
    tm = min(256, T)

    def body(x_ref, g_ref, whg_ref, wcv_ref, wgt_ref, h_ref, hg_ref, cv_ref, gt_ref):
        xv = x_ref[...]
        r = lax.rsqrt(jnp.mean(xv * xv, axis=-1, keepdims=True) + EPS)
        h = (xv * r * g_ref[...]).astype(BF16)
        h_ref[...] = h
        hg_ref[...] = jnp.dot(h, whg_ref[...], preferred_element_type=F32)
        cv_ref[...] = jnp.dot(h, wcv_ref[...], preferred_element_type=F32)
        gt_ref[...] = jnp.dot(h, wgt_ref[...], preferred_element_type=F32)

    row = lambda n: pl.BlockSpec((tm, n), lambda i: (i, 0))
    return pl.pallas_call(
        body, name="fwd_in", grid=(T // tm,),
        in_specs=[row(D_MODEL), _full((1, D_MODEL)), _resident(w_hg.shape), _resident(w_cv.shape), _resident(w_gt.shape)],
        out_specs=[row(D_MODEL), row(w_hg.shape[1]), row(w_cv.shape[1]), row(w_gt.shape[1])],
        out_shape=[jax.ShapeDtypeStruct((T, D_MODEL), BF16), jax.ShapeDtypeStruct((T, w_hg.shape[1]), F32),
                   jax.ShapeDtypeStruct((T, w_cv.shape[1]), F32), jax.ShapeDtypeStruct((T, w_gt.shape[1]), F32)],
        compiler_params=_params(("parallel",)),
    )(x, g, w_hg, w_cv, w_gt)


def _chunk_pos(shape):
    return lax.broadcasted_iota(jnp.int32, shape, 0) & (CHUNK - 1)


def _chunk_cumsum(x, pos):
    s = 1
    while s < CHUNK:
        x = x + jnp.where(pos >= s, pltpu.roll(x, s, 0), 0.0)
        s *= 2
    return x


def _chunk_rev_cumsum(x, pos):
    n = x.shape[0]
    s = 1
    while s < CHUNK:
        x = x + jnp.where(pos + s < CHUNK, pltpu.roll(x, n - s, 0), 0.0)
        s *= 2
    return x


def _chunk_bcast(x3, row, tb):
    return jnp.broadcast_to(x3[:, row:row + 1, :], x3.shape).reshape(tb, x3.shape[-1])


def _lower_bound(low_ref):
    l0 = low_ref[0:1, :]
    l1 = low_ref[1:2, :]
    m = jnp.maximum(l0, l1)
    e0 = jnp.exp(l0 - m)
    e1 = jnp.exp(l1 - m)
    return e0 / (e0 + e1), e1 / (e0 + e1)


def _hg_gates(qr, fr, lb, pos, tb):
    sq = _sigmoid(qr)
    q = qr * sq * Q_SCALE
    sg = _sigmoid(fr)
    f = lb + (1.0 - lb) * sg
    k = 1.0 - f
    b = _chunk_cumsum(jnp.log(f), pos)
    b3 = b.reshape(tb // CHUNK, CHUNK, HEAD_DIM)
    anc = _chunk_bcast(b3, CHUNK // 2 - 1, tb)
    blb = _chunk_bcast(b3, CHUNK - 1, tb)
    e_qa = jnp.exp(b - anc)
    e_ka = jnp.exp(anc - b)
    e_b = jnp.exp(b)
    e_ko = jnp.exp(blb - b)
    dec = jnp.exp(blb)
    return sq, q, sg, f, k, e_qa, e_ka, e_b, e_ko, dec


def _intra_mask(sb):
    r = lax.broadcasted_iota(jnp.int32, (sb, sb), 0)
    c = lax.broadcasted_iota(jnp.int32, (sb, sb), 1)
    return ((r // CHUNK) == (c // CHUNK)) & (c <= r)


def _hg_fwd(hg, low, gn):
    T = hg.shape[0]
    tb = min(512, T)
    sb = min(256, tb)
    nb = T // tb
    nc = tb // CHUNK

    def body(q_ref, f_ref, i_ref, g_ref, low_ref, gn_ref, o_ref, og_ref, st_ref, s_scr):
        t = pl.program_id(1)

        @pl.when(t == 0)
        def _():
            s_scr[...] = jnp.zeros_like(s_scr)

        pos = _chunk_pos((tb, HEAD_DIM))
        lb, _ = _lower_bound(low_ref)
        v = i_ref[...]
        _, q, _, _, k, e_qa, e_ka, e_b, e_ko, dec = _hg_gates(q_ref[...], f_ref[...], lb, pos, tb)
        qh = (q * e_qa).astype(BF16)
        kh = (k * e_ka).astype(BF16)
        qi = (q * e_b).astype(BF16)
        ko = (k * e_ko).astype(BF16)
        vb = v.astype(BF16)
        mask = _intra_mask(sb)
        for s in range(tb // sb):
            sl = slice(s * sb, (s + 1) * sb)
            p = jnp.where(mask, _mm_nt(qh[sl], kh[sl]), 0.0)
            o_ref[sl, :] = _mm(p, vb[sl])
        st = s_scr[...]
        for c in range(nc):
            sl = slice(c * CHUNK, (c + 1) * CHUNK)
            st_ref[c] = st
            o_ref[sl, :] = o_ref[sl, :] + _mm_nt(qi[sl], st)
            st = dec[c * CHUNK:c * CHUNK + 1, :] * st + _mm_tn(vb[sl], ko[sl])
        s_scr[...] = st
        o = o_ref[...]
        r = lax.rsqrt(jnp.mean(o * o, axis=-1, keepdims=True) + EPS)
        gr = g_ref[...]
        og_ref[...] = ((o * r * gn_ref[...]) * (gr * _sigmoid(gr))).astype(BF16)

    col = lambda p: pl.BlockSpec((tb, HEAD_DIM), lambda h, t: (t, p * N_HEADS + h))
    hcol = pl.BlockSpec((tb, HEAD_DIM), lambda h, t: (t, h))
    return pl.pallas_call(
        body, name="hg_fwd", grid=(N_HEADS, nb),
        in_specs=[col(0), col(1), col(2), col(3), pl.BlockSpec((2, HEAD_DIM), lambda h, t: (0, h)),
                  pl.BlockSpec((1, HEAD_DIM), lambda h, t: (0, 0))],
        out_specs=[hcol, hcol, pl.BlockSpec((None, nc, HEAD_DIM, HEAD_DIM), lambda h, t: (h, t, 0, 0))],
        out_shape=[jax.ShapeDtypeStruct((T, HG_WIDTH), F32), jax.ShapeDtypeStruct((T, HG_WIDTH), BF16),
                   jax.ShapeDtypeStruct((N_HEADS, T // CHUNK, HEAD_DIM, HEAD_DIM), F32)],
        scratch_shapes=[pltpu.VMEM((HEAD_DIM, HEAD_DIM), F32)],
        compiler_params=_params(("parallel", "arbitrary")),
    )(hg, hg, hg, hg, low, gn)


def _conv_fwd(cv, conv_w):
    T = cv.shape[0]
    nj = CONV_WIDTH // 128

    def body(c_ref, b_ref, x_ref, w_ref, o_ref):
        row = lax.broadcasted_iota(jnp.int32, (T, 128), 0)
        u = c_ref[...] * x_ref[...]
        u1 = jnp.where(row >= 1, pltpu.roll(u, 1, 0), 0.0)
        u2 = jnp.where(row >= 2, pltpu.roll(u, 2, 0), 0.0)
        y = w_ref[0:1, :] * u2 + w_ref[1:2, :] * u1 + w_ref[2:3, :] * u
        o_ref[...] = (b_ref[...] * y).astype(BF16)

    col = lambda p: pl.BlockSpec((T, 128), lambda j: (0, p * nj + j))
    return pl.pallas_call(
        body, name="conv_fwd", grid=(nj,),
        in_specs=[col(0), col(1), col(2), pl.BlockSpec((3, 128), lambda j: (0, j))],
        out_specs=pl.BlockSpec((T, 128), lambda j: (0, j)),
        out_shape=jax.ShapeDtypeStruct((T, CONV_WIDTH), BF16),
        compiler_params=_params(("parallel",)),
    )(cv, cv, cv, conv_w)


def _merge_fwd(og, cvo, gt, x, wa, wb, wo):
    T = x.shape[0]
    tm = min(512, T)

    def body(og_ref, cvo_ref, gt_ref, x_ref, wa_ref, wb_ref, wo_ref, x1_ref, mg_ref):
        ya = jnp.dot(og_ref[...], wa_ref[...], preferred_element_type=F32)
        yb = jnp.dot(cvo_ref[...], wb_ref[...], preferred_element_type=F32)
        m = _sigmoid(gt_ref[:, :D_MODEL]) * ya + _sigmoid(gt_ref[:, D_MODEL:]) * yb
        mb = m.astype(BF16)
        mg_ref[...] = mb
        x1_ref[...] = x_ref[...] + jnp.dot(mb, wo_ref[...], preferred_element_type=F32)

    row = lambda n: pl.BlockSpec((tm, n), lambda i: (i, 0))
    return pl.pallas_call(
        body, name="merge_fwd", grid=(T // tm,),
        in_specs=[row(HG_WIDTH), row(CONV_WIDTH), row(2 * D_MODEL), row(D_MODEL),
                  _resident(wa.shape), _resident(wb.shape), _resident(wo.shape)],
        out_specs=[row(D_MODEL), row(D_MODEL)],
        out_shape=[jax.ShapeDtypeStruct((T, D_MODEL), F32), jax.ShapeDtypeStruct((T, D_MODEL), BF16)],
        compiler_params=_params(("parallel",)),
    )(og, cvo, gt, x, wa, wb, wo)


def _ffn_fwd(x1, g, wg, wu, wd):
    T = x1.shape[0]
    tm = min(256, T)

    def body(x_ref, g_ref, wg_ref, wu_ref, wd_ref, h_ref, gate_ref, up_ref, act_ref, x2_ref):
        xv = x_ref[...]
        r = lax.rsqrt(jnp.mean(xv * xv, axis=-1, keepdims=True) + EPS)
        h = (xv * r * g_ref[...]).astype(BF16)
        h_ref[...] = h
        gate = jnp.dot(h, wg_ref[...], preferred_element_type=F32)
        up = jnp.dot(h, wu_ref[...], preferred_element_type=F32)
        gate_ref[...] = gate
        up_ref[...] = up
        act = (gate * _sigmoid(gate) * up).astype(BF16)
        act_ref[...] = act
        x2_ref[...] = xv + jnp.dot(act, wd_ref[...], preferred_element_type=F32)

    row = lambda n: pl.BlockSpec((tm, n), lambda i: (i, 0))
    return pl.pallas_call(
        body, name="ffn_fwd", grid=(T // tm,),
        in_specs=[row(D_MODEL), _full((1, D_MODEL)), _resident(wg.shape), _resident(wu.shape), _resident(wd.shape)],
        out_specs=[row(D_MODEL), row(D_FF), row(D_FF), row(D_FF), row(D_MODEL)],
        out_shape=[jax.ShapeDtypeStruct((T, D_MODEL), BF16), jax.ShapeDtypeStruct((T, D_FF), F32),
                   jax.ShapeDtypeStruct((T, D_FF), F32), jax.ShapeDtypeStruct((T, D_FF), BF16),
                   jax.ShapeDtypeStruct((T, D_MODEL), F32)],
        compiler_params=_params(("parallel",)),
    )(x1, g, wg, wu, wd)


def _final_fwd_bwd(x2, target, g):
    T = x2.shape[0]
    tm = min(512, T)

    def body(x_ref, t_ref, g_ref, loss_ref, dg_ref, dx_ref):
        @pl.when(pl.program_id(0) == 0)
        def _():
            loss_ref[...] = jnp.zeros_like(loss_ref)
            dg_ref[...] = jnp.zeros_like(dg_ref)

        xv = x_ref[...]
        gv = g_ref[...]
        r = lax.rsqrt(jnp.mean(xv * xv, axis=-1, keepdims=True) + EPS)
        xh = xv * r
        err = xh * gv - t_ref[...]
        loss_ref[...] += 0.5 * jnp.sum(jnp.mean(err * err, axis=-1, keepdims=True), axis=0, keepdims=True)
        dy = err * (1.0 / D_MODEL)
        dg_ref[...] += jnp.sum(dy * xh, axis=0, keepdims=True)
        w = dy * gv
        dx_ref[...] = r * (w - xh * jnp.mean(w * xh, axis=-1, keepdims=True))

    row = pl.BlockSpec((tm, D_MODEL), lambda i: (i, 0))
    return pl.pallas_call(
        body, name="final_fwd_bwd", grid=(T // tm,),
        in_specs=[row, row, _full((1, D_MODEL))],
        out_specs=[_full((1, 128)), _full((1, D_MODEL)), row],
        out_shape=[jax.ShapeDtypeStruct((1, 128), F32), jax.ShapeDtypeStruct((1, D_MODEL), F32),
                   jax.ShapeDtypeStruct((T, D_MODEL), F32)],
        compiler_params=_params(("arbitrary",)),
    )(x2, target, g)


def _ffn_bwd(dx2, x1, gate, up, g, wg, wu, wd):
    T = x1.shape[0]
    tm = min(256, T)

    def body(dx2_ref, x_ref, gate_ref, up_ref, g_ref, wg_ref, wu_ref, wd_ref, dgate_ref, dup_ref, dx1_ref, dgn_ref):
        @pl.when(pl.program_id(0) == 0)
        def _():
            dgn_ref[...] = jnp.zeros_like(dgn_ref)

        dx2 = dx2_ref[...]
        dact = _mm_nt(dx2, wd_ref[...])
        gate = gate_ref[...]
        s = _sigmoid(gate)
        dgate = (dact * up_ref[...] * (s * (1.0 + gate * (1.0 - s)))).astype(BF16)
        dup = (dact * (gate * s)).astype(BF16)
        dgate_ref[...] = dgate
        dup_ref[...] = dup
        dh = _mm_nt(dgate, wg_ref[...]) + _mm_nt(dup, wu_ref[...])
        xv = x_ref[...]
        r = lax.rsqrt(jnp.mean(xv * xv, axis=-1, keepdims=True) + EPS)
        xh = xv * r
        dgn_ref[...] += jnp.sum(dh * xh, axis=0, keepdims=True)
        w = dh * g_ref[...]
        dx1_ref[...] = dx2 + r * (w - xh * jnp.mean(w * xh, axis=-1, keepdims=True))

    row = lambda n: pl.BlockSpec((tm, n), lambda i: (i, 0))
    return pl.pallas_call(
        body, name="ffn_bwd", grid=(T // tm,),
        in_specs=[row(D_MODEL), row(D_MODEL), row(D_FF), row(D_FF), _full((1, D_MODEL)),
                  _resident(wg.shape), _resident(wu.shape), _resident(wd.shape)],
        out_specs=[row(D_FF), row(D_FF), row(D_MODEL), _full((1, D_MODEL))],
        out_shape=[jax.ShapeDtypeStruct((T, D_FF), BF16), jax.ShapeDtypeStruct((T, D_FF), BF16),
                   jax.ShapeDtypeStruct((T, D_MODEL), F32), jax.ShapeDtypeStruct((1, D_MODEL), F32)],
        compiler_params=_params(("arbitrary",)),
    )(dx2, x1, gate, up, g, wg, wu, wd)


def _merge_bwd(dx1, og, cvo, gt, wa, wb, wo):
    T = dx1.shape[0]
    tm = min(512, T)

    def body(dx_ref, og_ref, cvo_ref, gt_ref, wa_ref, wb_ref, wo_ref, dgt_ref, dya_ref, dyb_ref, dog_ref, dcvo_ref):
        dm = _mm_nt(dx_ref[...], wo_ref[...])
        ya = jnp.dot(og_ref[...], wa_ref[...], preferred_element_type=F32)
        yb = jnp.dot(cvo_ref[...], wb_ref[...], preferred_element_type=F32)
        sa = _sigmoid(gt_ref[:, :D_MODEL])
        sb = _sigmoid(gt_ref[:, D_MODEL:])
        dgt_ref[:, :D_MODEL] = (dm * ya * (sa * (1.0 - sa))).astype(BF16)
        dgt_ref[:, D_MODEL:] = (dm * yb * (sb * (1.0 - sb))).astype(BF16)
        dya = (dm * sa).astype(BF16)
        dyb = (dm * sb).astype(BF16)
        dya_ref[...] = dya
        dyb_ref[...] = dyb
        dog_ref[...] = _mm_nt(dya, wa_ref[...])
        dcvo_ref[...] = _mm_nt(dyb, wb_ref[...])

    row = lambda n: pl.BlockSpec((tm, n), lambda i: (i, 0))
    return pl.pallas_call(
        body, name="merge_bwd", grid=(T // tm,),
        in_specs=[row(D_MODEL), row(HG_WIDTH), row(CONV_WIDTH), row(2 * D_MODEL),
                  _resident(wa.shape), _resident(wb.shape), _resident(wo.shape)],
        out_specs=[row(2 * D_MODEL), row(D_MODEL), row(D_MODEL), row(HG_WIDTH), row(CONV_WIDTH)],
        out_shape=[jax.ShapeDtypeStruct((T, 2 * D_MODEL), BF16), jax.ShapeDtypeStruct((T, D_MODEL), BF16),
                   jax.ShapeDtypeStruct((T, D_MODEL), BF16), jax.ShapeDtypeStruct((T, HG_WIDTH), F32),
                   jax.ShapeDtypeStruct((T, CONV_WIDTH), F32)],
        compiler_params=_params(("parallel",)),
    )(dx1, og, cvo, gt, wa, wb, wo)


def _conv_bwd(dcvo, cv, conv_w):
    T = cv.shape[0]
    nj = CONV_WIDTH // 128

    def body(do_ref, c_ref, b_ref, x_ref, w_ref, dc_ref, db_ref, dx_ref, dw_ref):
        row = lax.broadcasted_iota(jnp.int32, (T, 128), 0)
        c = c_ref[...]
        xb = x_ref[...]
        do = do_ref[...]
        u = c * xb
        u1 = jnp.where(row >= 1, pltpu.roll(u, 1, 0), 0.0)
        u2 = jnp.where(row >= 2, pltpu.roll(u, 2, 0), 0.0)
        w0, w1, w2 = w_ref[0:1, :], w_ref[1:2, :], w_ref[2:3, :]
        y = w0 * u2 + w1 * u1 + w2 * u
        db_ref[...] = (do * y).astype(BF16)
        dy = do * b_ref[...]
        dw_ref[0:1, :] = jnp.sum(dy * u2, axis=0, keepdims=True)
        dw_ref[1:2, :] = jnp.sum(dy * u1, axis=0, keepdims=True)
        dw_ref[2:3, :] = jnp.sum(dy * u, axis=0, keepdims=True)
        dy1 = jnp.where(row < T - 1, pltpu.roll(dy, T - 1, 0), 0.0)
        dy2 = jnp.where(row < T - 2, pltpu.roll(dy, T - 2, 0), 0.0)
        du = w2 * dy + w1 * dy1 + w0 * dy2
        dc_ref[...] = (du * xb).astype(BF16)
        dx_ref[...] = (du * c).astype(BF16)

    col = lambda p: pl.BlockSpec((T, 128), lambda j: (0, p * nj + j))
    one = pl.BlockSpec((T, 128), lambda j: (0, j))
    wspec = pl.BlockSpec((3, 128), lambda j: (0, j))
    out = jax.ShapeDtypeStruct((T, CONV_WIDTH), BF16)
    return pl.pallas_call(
        body, name="conv_bwd", grid=(nj,),
        in_specs=[one, col(0), col(1), col(2), wspec],
        out_specs=[one, one, one, wspec],
        out_shape=[out, out, out, jax.ShapeDtypeStruct((3, CONV_WIDTH), F32)],
        compiler_params=_params(("parallel",)),
    )(dcvo, cv, cv, cv, conv_w)


def _hg_bwd(dog, hg, o, st, low, gn):
    T = hg.shape[0]
    tb = min(512, T)
    sb = min(256, tb)
    nb = T // tb
    nc = tb // CHUNK

    def body(q_ref, f_ref, i_ref, g_ref, low_ref, gn_ref, o_ref, dog_ref, st_ref,
             dq_ref, df_ref, di_ref, dg_ref, dlow_ref, dgn_ref,
             ds_scr, dqi_scr, dko_scr, dv_scr, dd_scr, dqh_scr, dkh_scr):
        h = pl.program_id(0)
        t = pl.program_id(1)

        @pl.when(t == 0)
        def _():
            ds_scr[...] = jnp.zeros_like(ds_scr)
            dlow_ref[...] = jnp.zeros_like(dlow_ref)

        @pl.when((t == 0) & (h == 0))
        def _():
            dgn_ref[...] = jnp.zeros_like(dgn_ref)

        pos = _chunk_pos((tb, HEAD_DIM))
        lb, lb1 = _lower_bound(low_ref)
        qr = q_ref[...]
        v = i_ref[...]
        sq, q, sg, f, k, e_qa, e_ka, e_b, e_ko, dec = _hg_gates(qr, f_ref[...], lb, pos, tb)

        gr = g_ref[...]
        gnv = gn_ref[...]
        o = o_ref[...]
        dog_v = dog_ref[...]
        sgr = _sigmoid(gr)
        r = lax.rsqrt(jnp.mean(o * o, axis=-1, keepdims=True) + EPS)
        oh = o * r
        dg_ref[...] = (dog_v * (oh * gnv) * (sgr * (1.0 + gr * (1.0 - sgr)))).astype(BF16)
        don = dog_v * (gr * sgr)
        dgn_ref[...] += jnp.sum(don * oh, axis=0, keepdims=True)
        w = don * gnv
        do = (r * (w - oh * jnp.mean(w * oh, axis=-1, keepdims=True))).astype(BF16)

        qh = (q * e_qa).astype(BF16)
        kh = (k * e_ka).astype(BF16)
        qi = (q * e_b).astype(BF16)
        ko = (k * e_ko).astype(BF16)
        vb = v.astype(BF16)

        mask = _intra_mask(sb)
        for s in range(tb // sb):
            sl = slice(s * sb, (s + 1) * sb)
            p = jnp.where(mask, _mm_nt(qh[sl], kh[sl]), 0.0).astype(BF16)
            dp = jnp.where(mask, _mm_nt(do[sl], vb[sl]), 0.0).astype(BF16)
            dv_scr[sl, :] = _mm_tn(p, do[sl])
            dqh_scr[sl, :] = _mm(dp, kh[sl])
            dkh_scr[sl, :] = _mm_tn(dp, qh[sl])

        ds = ds_scr[...]
        for c in reversed(range(nc)):
            sl = slice(c * CHUNK, (c + 1) * CHUNK)
            st_c = st_ref[c]
            dqi_scr[sl, :] = _mm(do[sl], st_c)
            dko_scr[sl, :] = _mm(vb[sl], ds)
            dv_scr[sl, :] = dv_scr[sl, :] + _mm_nt(ko[sl], ds)
            dd_scr[sl, :] = jnp.broadcast_to(jnp.sum(ds * st_c, axis=0, keepdims=True), (CHUNK, HEAD_DIM))
            ds = dec[c * CHUNK:c * CHUNK + 1, :] * ds + _mm_tn(do[sl], qi[sl])
        ds_scr[...] = ds

        dko_e = dko_scr[...] * e_ko
        dq = dqh_scr[...] * e_qa + dqi_scr[...] * e_b
        dk = dkh_scr[...] * e_ka + dko_e
        kd3 = (k * dko_e).reshape(nc, CHUNK, HEAD_DIM)
        last = jnp.broadcast_to(jnp.sum(kd3, axis=1, keepdims=True), kd3.shape).reshape(tb, HEAD_DIM)
        db = q * dq - k * dk + jnp.where(pos == CHUNK - 1, dec * dd_scr[...] + last, 0.0)
        dlg = _chunk_rev_cumsum(db, pos)
        dfv = dlg / f - dk
        s_low = jnp.sum(dfv * (1.0 - sg), axis=0, keepdims=True)
        dlow_ref[0:1, :] += s_low * lb * (1.0 - lb)
        dlow_ref[1:2, :] += -s_low * lb * lb1
        df_ref[...] = (dfv * (1.0 - lb) * sg * (1.0 - sg)).astype(BF16)
        dq_ref[...] = (dq * Q_SCALE * (sq * (1.0 + qr * (1.0 - sq)))).astype(BF16)
        di_ref[...] = dv_scr[...].astype(BF16)

    rt = lambda t: nb - 1 - t
    col = lambda p: pl.BlockSpec((tb, HEAD_DIM), lambda h, t: (rt(t), p * N_HEADS + h))
    hcol = pl.BlockSpec((tb, HEAD_DIM), lambda h, t: (rt(t), h))
    piece = jax.ShapeDtypeStruct((T, HG_WIDTH), BF16)
    tile = pltpu.VMEM((tb, HEAD_DIM), F32)
    return pl.pallas_call(
        body, name="hg_bwd", grid=(N_HEADS, nb),
        in_specs=[col(0), col(1), col(2), col(3), pl.BlockSpec((2, HEAD_DIM), lambda h, t: (0, h)),
                  pl.BlockSpec((1, HEAD_DIM), lambda h, t: (0, 0)), hcol, hcol,
                  pl.BlockSpec((None, nc, HEAD_DIM, HEAD_DIM), lambda h, t: (h, rt(t), 0, 0))],
        out_specs=[hcol, hcol, hcol, hcol, pl.BlockSpec((2, HEAD_DIM), lambda h, t: (0, h)),
                   pl.BlockSpec((1, HEAD_DIM), lambda h, t: (0, 0))],
        out_shape=[piece, piece, piece, piece, jax.ShapeDtypeStruct((2, HG_WIDTH), F32),
                   jax.ShapeDtypeStruct((1, HEAD_DIM), F32)],
        scratch_shapes=[pltpu.VMEM((HEAD_DIM, HEAD_DIM), F32), tile, tile, tile, tile, tile, tile],
        compiler_params=_params(("arbitrary", "arbitrary")),
    )(hg, hg, hg, hg, low, gn, o, dog, st)


def _in_bwd(dparts, w_in, x, dx1, g):
    T = x.shape[0]
    tm = min(256, T)
    widths = [p.shape[1] for p in dparts]
    offs = [sum(widths[:i]) for i in range(len(widths))]
    n = len(dparts)

    def body(*refs):
        d_refs = refs[:n]
        w_ref, x_ref, dx1_ref, g_ref, dx_ref, dgn_ref = refs[n:]

        @pl.when(pl.program_id(0) == 0)
        def _():
            dgn_ref[...] = jnp.zeros_like(dgn_ref)

        dh = None
        for d_ref, off, wd in zip(d_refs, offs, widths):
            part = _mm_nt(d_ref[...], w_ref[:, off:off + wd])
            dh = part if dh is None else dh + part
        xv = x_ref[...]
        r = lax.rsqrt(jnp.mean(xv * xv, axis=-1, keepdims=True) + EPS)
        xh = xv * r
        dgn_ref[...] += jnp.sum(dh * xh, axis=0, keepdims=True)
        w = dh * g_ref[...]
        dx_ref[...] = dx1_ref[...] + r * (w - xh * jnp.mean(w * xh, axis=-1, keepdims=True))

    row = lambda m: pl.BlockSpec((tm, m), lambda i: (i, 0))
    return pl.pallas_call(
        body, name="in_bwd", grid=(T // tm,),
        in_specs=[row(wd) for wd in widths] + [_resident(w_in.shape), row(D_MODEL), row(D_MODEL), _full((1, D_MODEL))],
        out_specs=[row(D_MODEL), _full((1, D_MODEL))],
        out_shape=[jax.ShapeDtypeStruct((T, D_MODEL), F32), jax.ShapeDtypeStruct((1, D_MODEL), F32)],
        compiler_params=_params(("arbitrary",)),
    )(*dparts, w_in, x, dx1, g)


def _wgrad(name, a, b, tn):
    T, M = a.shape
    N = b.shape[1]
    tk = min(512, T)
    nk = T // tk

    def body(a_ref, b_ref, o_ref):
        k = pl.program_id(1)
        part = _mm_tn(a_ref[...], b_ref[...])

        @pl.when(k == 0)
        def _():
            o_ref[...] = part

        @pl.when(k != 0)
        def _():
            o_ref[...] += part

    return pl.pallas_call(
        body, name=name, grid=(N // tn, nk),
        in_specs=[pl.BlockSpec((tk, M), lambda j, k: (k, 0)), pl.BlockSpec((tk, tn), lambda j, k: (k, j))],
        out_specs=pl.BlockSpec((M, tn), lambda j, k: (0, j)),
        out_shape=jax.ShapeDtypeStruct((M, N), F32),
        compiler_params=_params(("parallel", "arbitrary")),
    )(a, b)


def _adamw_math(w, g, m, v):
    m = ADAM_B1 * m + (1.0 - ADAM_B1) * g
    v = ADAM_B2 * v + (1.0 - ADAM_B2) * (g * g)
    m_hat = m / (1.0 - ADAM_B1 ** ADAM_STEP)
    v_hat = v / (1.0 - ADAM_B2 ** ADAM_STEP)
    delta = -ADAM_LR * (m_hat / (jnp.sqrt(v_hat) + ADAM_EPS) + ADAM_WD * w)
    return delta, m, v


def _adamw_sum(name, w, parts, m, v):
    R, C = w.shape
    tr = R
    for cand in (256, 128, 64, 32, 16, 8):
        if R % cand == 0:
            tr = cand
            break

    def body(w_ref, p_ref, m_ref, v_ref, g_out, d_out, m_out, v_out):
        g = p_ref[0].astype(F32)
        for k in range(1, 4):
            g = g + p_ref[k].astype(F32)
        g_out[...] = g
        d_out[...], m_out[...], v_out[...] = _adamw_math(w_ref[...], g, m_ref[...], v_ref[...])

    blk = pl.BlockSpec((tr, C), lambda i: (i, 0))
    out = jax.ShapeDtypeStruct((R, C), F32)
    return pl.pallas_call(
        body, name=name, grid=(R // tr,),
        in_specs=[blk, pl.BlockSpec((4, tr, C), lambda i: (0, i, 0)), blk, blk],
        out_specs=[blk, blk, blk, blk], out_shape=[out, out, out, out],
        compiler_params=_params(("parallel",)),
    )(w, parts, m, v)


def _small_sum(gathered):
    R = gathered.shape[1]

    def body(p_ref, o_ref):
        g = p_ref[0]
        for k in range(1, N_DEV):
            g = g + p_ref[k]
        o_ref[...] = g

    return pl.pallas_call(
        body, name="small_sum", in_specs=[_full(gathered.shape)], out_specs=_full((R, 128)), grid=(1,),
        out_shape=jax.ShapeDtypeStruct((R, 128), F32),
    )(gathered)


def _small_adamw(w, g, m, v):
    def body(w_ref, g_ref, m_ref, v_ref, d_out, m_out, v_out):
        d_out[...], m_out[...], v_out[...] = _adamw_math(w_ref[...], g_ref[...], m_ref[...], v_ref[...])

    out = jax.ShapeDtypeStruct(w.shape, F32)
    spec = _full(w.shape)
    return pl.pallas_call(
        body, name="small_adamw", grid=(1,), in_specs=[spec] * 4, out_specs=[spec] * 3, out_shape=[out, out, out],
    )(w, g, m, v)


def _pair_sum(keep, got):
    _, R, C = keep.shape
    tr = 672

    def body(a_ref, b_ref, o_ref):
        o_ref[...] = (a_ref[...].astype(F32) + b_ref[...].astype(F32)).astype(BF16)

    blk = pl.BlockSpec((None, tr, C), lambda k, i: (k, i, 0))
    return pl.pallas_call(
        body, name="pair_sum", grid=(4, R // tr), in_specs=[blk, blk], out_specs=blk,
        out_shape=jax.ShapeDtypeStruct(keep.shape, BF16),
        compiler_params=_params(("parallel", "parallel")),
    )(keep, got)


MESH = pl.DeviceIdType.MESH
HBM_SPEC = pl.BlockSpec(memory_space=pl.ANY)


def _all_gather(name, block):
    R, C = block.shape

    def body(x_ref, out_ref, send_sems, recv_sems, local_sem):
        x, y, c = lax.axis_index("x"), lax.axis_index("y"), lax.axis_index("c")
        me, sibling = (x, y, c), (x, y, 1 - c)
        chips = [(1 - x, y), (x, 1 - y), (1 - x, 1 - y)]

        def slot(px, py, pc):
            return out_ref.at[4 * px + 2 * py + pc]

        def copy(k, blk, to, src=None):
            return pltpu.make_async_remote_copy(
                src_ref=slot(*blk) if src is None else src, dst_ref=slot(*blk),
                send_sem=send_sems.at[k], recv_sem=recv_sems.at[k], device_id=to, device_id_type=MESH)

        mine = pltpu.make_async_copy(x_ref, slot(*me), local_sem)
        mine.start()
        first = [copy(0, me, sibling, src=x_ref)]
        first += [copy(1 + j, me, (*chip, c), src=x_ref) for j, chip in enumerate(chips)]
        for cp in first:
            cp.start()
        passed = [copy(4 + j, (*chip, c), sibling) for j, chip in enumerate(chips)]
        for j, chip in enumerate(chips):
            copy(1 + j, (*chip, c), me).wait_recv()
            passed[j].start()
        copy(0, sibling, me).wait_recv()
        for j, chip in enumerate(chips):
            copy(4 + j, (*chip, 1 - c), me).wait_recv()
        for cp in first + passed:
            cp.wait_send()
        mine.wait()

    return pl.pallas_call(
        body, name=name, in_specs=[HBM_SPEC], out_specs=HBM_SPEC,
        out_shape=jax.ShapeDtypeStruct((N_DEV, R, C), block.dtype),
        scratch_shapes=[pltpu.SemaphoreType.DMA((7,)), pltpu.SemaphoreType.DMA((7,)), pltpu.SemaphoreType.DMA],
    )(block)


def _sibling_swap(give):
    def body(x_ref, out_ref, send_sem, recv_sem):
        x, y, c = lax.axis_index("x"), lax.axis_index("y"), lax.axis_index("c")
        cp = pltpu.make_async_remote_copy(src_ref=x_ref, dst_ref=out_ref, send_sem=send_sem, recv_sem=recv_sem,
                                          device_id=(x, y, 1 - c), device_id_type=MESH)
        cp.start()
        cp.wait()

    return pl.pallas_call(
        body, name="sibling_swap", in_specs=[HBM_SPEC], out_specs=HBM_SPEC,
        out_shape=jax.ShapeDtypeStruct(give.shape, give.dtype),
        scratch_shapes=[pltpu.SemaphoreType.DMA, pltpu.SemaphoreType.DMA],
    )(give)


def _chip_exchange(sums):
    def body(x_ref, out_ref, send_sems, recv_sems, local_sem):
        x, y, c = lax.axis_index("x"), lax.axis_index("y"), lax.axis_index("c")
        chips = [(1 - x, y), (x, 1 - y), (1 - x, 1 - y)]
        my_chip = 2 * x + y
        mine = pltpu.make_async_copy(x_ref.at[my_chip], out_ref.at[my_chip], local_sem)
        mine.start()
        sends = []
        for j, (cx, cy) in enumerate(chips):
            sends.append(pltpu.make_async_remote_copy(
                src_ref=x_ref.at[2 * cx + cy], dst_ref=out_ref.at[my_chip],
                send_sem=send_sems.at[j], recv_sem=recv_sems.at[j], device_id=(cx, cy, c), device_id_type=MESH))
        for cp in sends:
            cp.start()
        for j, (cx, cy) in enumerate(chips):
            pltpu.make_async_remote_copy(
                src_ref=x_ref.at[my_chip], dst_ref=out_ref.at[2 * cx + cy],
                send_sem=send_sems.at[j], recv_sem=recv_sems.at[j], device_id=(cx, cy, c),
                device_id_type=MESH).wait_recv()
        for cp in sends:
            cp.wait_send()
        mine.wait()

    return pl.pallas_call(
        body, name="chip_exchange", in_specs=[HBM_SPEC], out_specs=HBM_SPEC,
        out_shape=jax.ShapeDtypeStruct(sums.shape, sums.dtype),
        scratch_shapes=[pltpu.SemaphoreType.DMA((3,)), pltpu.SemaphoreType.DMA((3,)), pltpu.SemaphoreType.DMA],
    )(sums)


def _pack_shards(shards):
    return jnp.concatenate([shards[n].reshape(r, 1024) for n, r in PACK_SPLITS], axis=0)


def _col_sharded_full(g, name, rows):
    off, r = PACK_OFF[name]
    cols = r * 1024 // rows
    return g[:, off:off + r].reshape(N_DEV, rows, cols).transpose(1, 0, 2).reshape(rows, N_DEV * cols)


def _row_sharded_full(g, name):
    off, r = PACK_OFF[name]
    return g[:, off:off + r].reshape(N_DEV * r, 1024)


def _col_sharded_pack(full, n_out=N_DEV):
    rows, allc = full.shape
    cols = allc // N_DEV
    return full.reshape(rows, N_DEV, cols).transpose(1, 0, 2).reshape(N_DEV, rows * cols // 1024, 1024)


def _row_sharded_pack(full):
    rows = full.shape[0] // N_DEV
    return full.reshape(N_DEV, rows, 1024)


def _local_step(x, target, gains, low, conv_w, wfull):
    g_mix, g_hg, g_ffn, g_fin = gains
    w_in = wfull["w_in"]
    w_hg, w_cv, w_gt = w_in[:, :2048], w_in[:, 2048:3584], w_in[:, 3584:]
    wa, wb, wo = wfull["w_branch_a"], wfull["w_branch_b"], wfull["w_out"]
    wg, wu, wd = wfull["w_ffn_gate"], wfull["w_ffn_up"], wfull["w_ffn_down"]

    h, hg, cv, gt = _fwd_in(x, g_mix, w_hg, w_cv, w_gt)
    o, og, st = _hg_fwd(hg, low, g_hg)
    cvo = _conv_fwd(cv, conv_w)
    x1, merged = _merge_fwd(og, cvo, gt, x, wa, wb, wo)
    h2, gate, up, act, x2 = _ffn_fwd(x1, g_ffn, wg, wu, wd)
    loss, d_gfin, dx2 = _final_fwd_bwd(x2, target, g_fin)

    dgate, dup, dx1, d_gffn = _ffn_bwd(dx2, x1, gate, up, g_ffn, wg, wu, wd)
    gw = {}
    gw["w_ffn_down"] = _wgrad("wgrad_ffn_down", act, dx2, 512)
    gw["w_ffn_gate"] = _wgrad("wgrad_ffn_gate", h2, dgate, 1408)
    gw["w_ffn_up"] = _wgrad("wgrad_ffn_up", h2, dup, 1408)
    dgt, dya, dyb, dog, dcvo = _merge_bwd(dx1, og, cvo, gt, wa, wb, wo)
    gw["w_out"] = _wgrad("wgrad_out", merged, dx1, 512)
    gw["w_branch_a"] = _wgrad("wgrad_branch_a", og, dya, 512)
    gw["w_branch_b"] = _wgrad("wgrad_branch_b", cvo, dyb, 512)
    dc, db, dxb, d_conv = _conv_bwd(dcvo, cv, conv_w)
    dq, df, di, dg, d_low, d_ghg = _hg_bwd(dog, hg, o, st, low, g_hg)
    dparts = [dq, df, di, dg, dc, db, dxb, dgt]
    grad_x, d_gmix = _in_bwd(dparts, w_in, x, dx1, g_mix)
    gw["w_in"] = jnp.concatenate(
        [_wgrad("wgrad_in_%d" % i, h, p, 512) for i, p in enumerate(dparts)], axis=1)
    small = dict(norm_mix_g=d_gmix, norm_ffn_g=d_gffn, norm_final_g=d_gfin, lower_bounds=d_low, hg_norm_g=d_ghg,
                 conv_w=d_conv, loss=loss)
    return grad_x, gw, small


_SMALL_LAYOUT = (("norm_mix_g", 0, 8), ("norm_ffn_g", 8, 8), ("norm_final_g", 16, 8), ("lower_bounds", 24, 8),
                 ("hg_norm_g", 32, 1))
_LOSS_ROW = 40
_CONV_ROW = 48


def _pad_rows(a, rows):
    return jnp.pad(a, ((0, rows - a.shape[0]), (0, 0)))


def _pack_small(vals, conv_rows):
    parts = [_pad_rows(vals[name].reshape(rows, 128), 8) for name, _, rows in _SMALL_LAYOUT]
    loss = vals["loss"][:, :128] if "loss" in vals else jnp.zeros((1, 128), F32)
    parts.append(_pad_rows(loss, 8))
    parts.append(_pad_rows(conv_rows, SMALL_ROWS - _CONV_ROW))
    return jnp.concatenate(parts, axis=0)


def _conv_shard_rows(a):
    return jnp.pad(a, ((0, 5), (0, 64)))


def kernel(x, norm_mix_g, w_in, lower_bounds, hg_norm_g, conv_w, w_branch_a, w_branch_b, w_out, norm_ffn_g, w_ffn_gate, w_ffn_up, w_ffn_down, norm_final_g, loss_target, m_norm_mix_g, m_w_in, m_lower_bounds, m_hg_norm_g, m_conv_w, m_w_branch_a, m_w_branch_b, m_w_out, m_norm_ffn_g, m_w_ffn_gate, m_w_ffn_up, m_w_ffn_down, m_norm_final_g, v_norm_mix_g, v_w_in, v_lower_bounds, v_hg_norm_g, v_conv_w, v_w_branch_a, v_w_branch_b, v_w_out, v_norm_ffn_g, v_w_ffn_gate, v_w_ffn_up, v_w_ffn_down, v_norm_final_g):
    cx, cy, cc = lax.axis_index("x"), lax.axis_index("y"), lax.axis_index("c")
    my_dev = 4 * cx + 2 * cy + cc

    big = dict(w_in=w_in[0], w_branch_a=w_branch_a[0], w_branch_b=w_branch_b[0], w_out=w_out[0],
               w_ffn_gate=w_ffn_gate[0], w_ffn_up=w_ffn_up[0], w_ffn_down=w_ffn_down[0])
    big_m = dict(w_in=m_w_in[0], w_branch_a=m_w_branch_a[0], w_branch_b=m_w_branch_b[0], w_out=m_w_out[0],
                 w_ffn_gate=m_w_ffn_gate[0], w_ffn_up=m_w_ffn_up[0], w_ffn_down=m_w_ffn_down[0])
    big_v = dict(w_in=v_w_in[0], w_branch_a=v_w_branch_a[0], w_branch_b=v_w_branch_b[0], w_out=v_w_out[0],
                 w_ffn_gate=v_w_ffn_gate[0], w_ffn_up=v_w_ffn_up[0], w_ffn_down=v_w_ffn_down[0])

    packed = _pack_shards({n: a.astype(BF16) for n, a in big.items()})
    gathered = _all_gather("gather_weights", packed)
    conv_all = _all_gather("gather_conv_w", _conv_shard_rows(conv_w[0]))
    conv_full = conv_all[:, :3, :64].transpose(1, 0, 2).reshape(3, CONV_WIDTH)
    wfull = dict(
        w_in=_col_sharded_full(gathered, "w_in", D_MODEL),
        w_branch_a=_col_sharded_full(gathered, "w_branch_a", HG_WIDTH),
        w_branch_b=_col_sharded_full(gathered, "w_branch_b", CONV_WIDTH),
        w_out=_row_sharded_full(gathered, "w_out"),
        w_ffn_gate=_col_sharded_full(gathered, "w_ffn_gate", D_MODEL),
        w_ffn_up=_col_sharded_full(gathered, "w_ffn_up", D_MODEL),
        w_ffn_down=_row_sharded_full(gathered, "w_ffn_down"),
    )

    gains = (norm_mix_g, hg_norm_g, norm_ffn_g, norm_final_g.reshape(1, D_MODEL))
    grad_x, gw, small = _local_step(x[0], loss_target[0], gains, lower_bounds, conv_full, wfull)

    by_owner = jnp.concatenate([
        _col_sharded_pack(gw["w_in"]), _col_sharded_pack(gw["w_branch_a"]), _col_sharded_pack(gw["w_branch_b"]),
        _row_sharded_pack(gw["w_out"]), _col_sharded_pack(gw["w_ffn_gate"]), _col_sharded_pack(gw["w_ffn_up"]),
        _row_sharded_pack(gw["w_ffn_down"])], axis=1).astype(BF16)
    by_core = by_owner.reshape(4, 2, PACK_ROWS, 1024)
    keep = lax.dynamic_index_in_dim(by_core, cc, axis=1, keepdims=False)
    give = lax.dynamic_index_in_dim(by_core, 1 - cc, axis=1, keepdims=False)
    chip_sums = _pair_sum(keep, _sibling_swap(give))
    parts = _chip_exchange(chip_sums)

    outs = {}
    for n, r in PACK_SPLITS:
        off, _ = PACK_OFF[n]
        shp = big[n].shape
        outs[n] = _adamw_sum("adamw_" + n, big[n], parts[:, off:off + r].reshape(4, *shp), big_m[n], big_v[n])

    small_all = _all_gather("gather_small", _pack_small(small, small["conv_w"].reshape(12, 128)))
    ssum = _small_sum(small_all)
    conv_g_full = ssum[_CONV_ROW:_CONV_ROW + 12].reshape(3, CONV_WIDTH)
    conv_g = lax.dynamic_slice_in_dim(conv_g_full, my_dev * 64, 64, axis=1)
    loss = ssum[_LOSS_ROW, 0]
    g_rows = jnp.concatenate([ssum[:_CONV_ROW], _pad_rows(_conv_shard_rows(conv_g), SMALL_ROWS - _CONV_ROW)], axis=0)

    def pack_state(a):
        vals = dict(norm_mix_g=a[0], norm_ffn_g=a[1], norm_final_g=a[2], lower_bounds=a[3], hg_norm_g=a[4])
        return _pack_small(vals, _conv_shard_rows(a[5][0]))

    sw = pack_state((norm_mix_g, norm_ffn_g, norm_final_g, lower_bounds, hg_norm_g, conv_w))
    sm = pack_state((m_norm_mix_g, m_norm_ffn_g, m_norm_final_g, m_lower_bounds, m_hg_norm_g, m_conv_w))
    sv = pack_state((v_norm_mix_g, v_norm_ffn_g, v_norm_final_g, v_lower_bounds, v_hg_norm_g, v_conv_w))
    s_delta, s_m, s_v = _small_adamw(sw, g_rows, sm, sv)

    shapes = dict(norm_mix_g=(1, D_MODEL), norm_ffn_g=(1, D_MODEL), norm_final_g=(D_MODEL,),
                  lower_bounds=(2, HG_WIDTH), hg_norm_g=(1, HEAD_DIM))

    def unpack(buf, name):
        if name == "conv_w":
            return buf[_CONV_ROW:_CONV_ROW + 3, :64].reshape(1, 3, 64)
        for nm, off, rows in _SMALL_LAYOUT:
            if nm == name:
                return buf[off:off + rows].reshape(shapes[name])
        raise KeyError(name)

    order = ["norm_mix_g", "w_in", "lower_bounds", "hg_norm_g", "conv_w", "w_branch_a", "w_branch_b", "w_out",
             "norm_ffn_g", "w_ffn_gate", "w_ffn_up", "w_ffn_down", "norm_final_g"]
    result = [loss, grad_x[None]]
    for k, sbuf in enumerate((g_rows, s_delta, s_m, s_v)):
        for n in order:
            if n in outs:
                result.append(outs[n][k][None])
            else:
                result.append(unpack(sbuf, n))
    return tuple(result)
```

```python
import functools

import jax
import jax.numpy as jnp
from jax import lax
from jax.experimental import pallas as pl
from jax.experimental.pallas import tpu as pltpu

F32 = jnp.float32
BF16 = jnp.bfloat16

D_MODEL = 1024
HG_WIDTH = 512
HEAD_DIM = 128
N_HEADS = 4
CONV_WIDTH = 512
D_FF = 2816
CHUNK = 32
EPS = 1e-6
Q_SCALE = HEAD_DIM ** -0.5
N_DEV = 8

ADAM_LR = 0.001
ADAM_B1 = 0.9
ADAM_B2 = 0.999
ADAM_EPS = 1e-08
ADAM_WD = 0.01
ADAM_STEP = 10

VMEM_LIMIT_V7X = 56 * 1024 * 1024

PACK_SPLITS = (("w_in", 704), ("w_branch_a", 64), ("w_branch_b", 64), ("w_out", 128),
               ("w_ffn_gate", 352), ("w_ffn_up", 352), ("w_ffn_down", 352))
PACK_ROWS = sum(r for _, r in PACK_SPLITS)
PACK_OFF = {}
_o = 0
for _n, _r in PACK_SPLITS:
    PACK_OFF[_n] = (_o, _r)
    _o += _r

SMALL_ROWS = 64


def _params(sem, vmem=VMEM_LIMIT_V7X):
    return pltpu.CompilerParams(dimension_semantics=sem, vmem_limit_bytes=vmem)


def _mm(a, b):
    return jnp.dot(a.astype(BF16), b.astype(BF16), preferred_element_type=F32)


def _mm_nt(a, b):
    return lax.dot_general(a.astype(BF16), b.astype(BF16), (((1,), (1,)), ((), ())), preferred_element_type=F32)


def _mm_tn(a, b):
    return lax.dot_general(a.astype(BF16), b.astype(BF16), (((0,), (0,)), ((), ())), preferred_element_type=F32)


def _sigmoid(x):
    return 1.0 / (1.0 + jnp.exp(-x))


def _resident(shape):
    nd = len(shape)
    return pl.BlockSpec(shape, lambda *_: (0,) * nd, pipeline_mode=pl.Buffered(1))


def _full(shape):
    nd = len(shape)
    return pl.BlockSpec(shape, lambda *_: (0,) * nd)


def _shard_cols(w_ref):
    return jnp.concatenate([w_ref[s] for s in range(N_DEV)], axis=1)


N_HG = 4 * HG_WIDTH
N_CV = 3 * CONV_WIDTH
N_GT = 2 * D_MODEL
N_IN = N_HG + N_CV + N_GT


def _col(tm, n):
    return pl.BlockSpec((n, tm), lambda i: (0, i))


def _fwd_in(x, g, w_in):
    T = x.shape[0]
    tm = min(256, T)

    def body(x_ref, g_ref, w_ref, ht_ref, hg_ref, cv_ref, gt_ref):
        xv = x_ref[...]
        r = lax.rsqrt(jnp.mean(xv * xv, axis=-1, keepdims=True) + EPS)
        hf = xv * r * g_ref[...]
        h = hf.astype(BF16)
        ht_ref[...] = hf.T.astype(BF16)
        hg_ref[...] = jnp.dot(h, w_ref[:, :N_HG], preferred_element_type=F32)
        cv_ref[...] = jnp.dot(h, w_ref[:, N_HG:N_HG + N_CV], preferred_element_type=F32)
        gt_ref[...] = jnp.dot(h, w_ref[:, N_HG + N_CV:], preferred_element_type=F32)

    row = lambda n: pl.BlockSpec((tm, n), lambda i: (i, 0))
    return pl.pallas_call(
        body, name="fwd_in", grid=(T // tm,),
        in_specs=[row(D_MODEL), _full((1, D_MODEL)), _resident(w_in.shape)],
        out_specs=[_col(tm, D_MODEL), row(N_HG), row(N_CV), row(N_GT)],
        out_shape=[jax.ShapeDtypeStruct((D_MODEL, T), BF16), jax.ShapeDtypeStruct((T, N_HG), F32),
                   jax.ShapeDtypeStruct((T, N_CV), F32), jax.ShapeDtypeStruct((T, N_GT), F32)],
        compiler_params=_params(("parallel",)),
    )(x, g, w_in)


def _chunk_pos(shape):
    return lax.broadcasted_iota(jnp.int32, shape, 0) & (CHUNK - 1)


def _chunk_cumsum(x, pos):
    s = 1
    while s < CHUNK:
        x = x + jnp.where(pos >= s, pltpu.roll(x, s, 0), 0.0)
        s *= 2
    return x


def _chunk_rev_cumsum(x, pos):
    n = x.shape[0]
    s = 1
    while s < CHUNK:
        x = x + jnp.where(pos + s < CHUNK, pltpu.roll(x, n - s, 0), 0.0)
        s *= 2
    return x


def _chunk_bcast(x3, row, tb):
    return jnp.broadcast_to(x3[:, row:row + 1, :], x3.shape).reshape(tb, x3.shape[-1])


def _lower_bound(low_ref):
    l0 = low_ref[0:1, :]
    l1 = low_ref[1:2, :]
    m = jnp.maximum(l0, l1)
    e0 = jnp.exp(l0 - m)
    e1 = jnp.exp(l1 - m)
    return e0 / (e0 + e1), e1 / (e0 + e1)


def _hg_gates(qr, fr, lb, pos, tb):
    sq = _sigmoid(qr)
    q = qr * sq * Q_SCALE
    sg = _sigmoid(fr)
    f = lb + (1.0 - lb) * sg
    k = 1.0 - f
    b = _chunk_cumsum(jnp.log(f), pos)
    b3 = b.reshape(tb // CHUNK, CHUNK, HEAD_DIM)
    anc = _chunk_bcast(b3, CHUNK // 2 - 1, tb)
    blb = _chunk_bcast(b3, CHUNK - 1, tb)
    e_qa = jnp.exp(b - anc)
    e_ka = jnp.exp(anc - b)
    e_b = jnp.exp(b)
    e_ko = jnp.exp(blb - b)
    dec = jnp.exp(blb)
    return sq, q, sg, f, k, e_qa, e_ka, e_b, e_ko, dec


def _intra_mask(sb):
    r = lax.broadcasted_iota(jnp.int32, (sb, sb), 0)
    c = lax.broadcasted_iota(jnp.int32, (sb, sb), 1)
    return ((r // CHUNK) == (c // CHUNK)) & (c <= r)


def _hg_fwd(hg, low, gn):
    T = hg.shape[0]
    tb = min(512, T)
    sb = min(256, tb)
    nb = T // tb
    nc = tb // CHUNK

    def body(q_ref, f_ref, i_ref, g_ref, low_ref, gn_ref, o_ref, og_ref, ogt_ref, st_ref, s_scr):
        t = pl.program_id(1)

        @pl.when(t == 0)
        def _():
            s_scr[...] = jnp.zeros_like(s_scr)

        pos = _chunk_pos((tb, HEAD_DIM))
        lb, _ = _lower_bound(low_ref)
        v = i_ref[...]
        _, q, _, _, k, e_qa, e_ka, e_b, e_ko, dec = _hg_gates(q_ref[...], f_ref[...], lb, pos, tb)
        qh = (q * e_qa).astype(BF16)
        kh = (k * e_ka).astype(BF16)
        qi = (q * e_b).astype(BF16)
        ko = (k * e_ko).astype(BF16)
        vb = v.astype(BF16)
        mask = _intra_mask(sb)
        for s in range(tb // sb):
            sl = slice(s * sb, (s + 1) * sb)
            p = jnp.where(mask, _mm_nt(qh[sl], kh[sl]), 0.0)
            o_ref[sl, :] = _mm(p, vb[sl])
        st = s_scr[...]
        for c in range(nc):
            sl = slice(c * CHUNK, (c + 1) * CHUNK)
            st_ref[c] = st
            o_ref[sl, :] = o_ref[sl, :] + _mm_nt(qi[sl], st)
            st = dec[c * CHUNK:c * CHUNK + 1, :] * st + _mm_tn(vb[sl], ko[sl])
        s_scr[...] = st
        o = o_ref[...]
        r = lax.rsqrt(jnp.mean(o * o, axis=-1, keepdims=True) + EPS)
        gr = g_ref[...]
        og = (o * r * gn_ref[...]) * (gr * _sigmoid(gr))
        og_ref[...] = og.astype(BF16)
        ogt_ref[...] = og.T.astype(BF16)

    col = lambda p: pl.BlockSpec((tb, HEAD_DIM), lambda h, t: (t, p * N_HEADS + h))
    hcol = pl.BlockSpec((tb, HEAD_DIM), lambda h, t: (t, h))
    return pl.pallas_call(
        body, name="hg_fwd", grid=(N_HEADS, nb),
        in_specs=[col(0), col(1), col(2), col(3), pl.BlockSpec((2, HEAD_DIM), lambda h, t: (0, h)),
                  pl.BlockSpec((1, HEAD_DIM), lambda h, t: (0, 0))],
        out_specs=[hcol, hcol, pl.BlockSpec((HEAD_DIM, tb), lambda h, t: (h, t)),
                   pl.BlockSpec((None, nc, HEAD_DIM, HEAD_DIM), lambda h, t: (h, t, 0, 0))],
        out_shape=[jax.ShapeDtypeStruct((T, HG_WIDTH), F32), jax.ShapeDtypeStruct((T, HG_WIDTH), BF16),
                   jax.ShapeDtypeStruct((HG_WIDTH, T), BF16),
                   jax.ShapeDtypeStruct((N_HEADS, T // CHUNK, HEAD_DIM, HEAD_DIM), F32)],
        scratch_shapes=[pltpu.VMEM((HEAD_DIM, HEAD_DIM), F32)],
        compiler_params=_params(("parallel", "arbitrary")),
    )(hg, hg, hg, hg, low, gn)


def _conv_fwd(cv, conv_w):
    T = cv.shape[0]
    nj = CONV_WIDTH // 128

    def body(c_ref, b_ref, x_ref, w_ref, o_ref, ot_ref):
        row = lax.broadcasted_iota(jnp.int32, (T, 128), 0)
        u = c_ref[...] * x_ref[...]
        u1 = jnp.where(row >= 1, pltpu.roll(u, 1, 0), 0.0)
        u2 = jnp.where(row >= 2, pltpu.roll(u, 2, 0), 0.0)
        y = w_ref[0:1, :] * u2 + w_ref[1:2, :] * u1 + w_ref[2:3, :] * u
        out = b_ref[...] * y
        o_ref[...] = out.astype(BF16)
        ot_ref[...] = out.T.astype(BF16)

    col = lambda p: pl.BlockSpec((T, 128), lambda j: (0, p * nj + j))
    return pl.pallas_call(
        body, name="conv_fwd", grid=(nj,),
        in_specs=[col(0), col(1), col(2), pl.BlockSpec((3, 128), lambda j: (0, j))],
        out_specs=[pl.BlockSpec((T, 128), lambda j: (0, j)), pl.BlockSpec((128, T), lambda j: (j, 0))],
        out_shape=[jax.ShapeDtypeStruct((T, CONV_WIDTH), BF16), jax.ShapeDtypeStruct((CONV_WIDTH, T), BF16)],
        compiler_params=_params(("parallel",)),
    )(cv, cv, cv, conv_w)


def _merge_fwd(og, cvo, gt, x, wa, wb, wo):
    T = x.shape[0]
    tm = min(512, T)

    def body(og_ref, cvo_ref, gt_ref, x_ref, wa_ref, wb_ref, wo_ref, x1_ref, mgt_ref):
        ya = jnp.dot(og_ref[...], _shard_cols(wa_ref), preferred_element_type=F32)
        yb = jnp.dot(cvo_ref[...], _shard_cols(wb_ref), preferred_element_type=F32)
        m = _sigmoid(gt_ref[:, :D_MODEL]) * ya + _sigmoid(gt_ref[:, D_MODEL:]) * yb
        mgt_ref[...] = m.T.astype(BF16)
        x1_ref[...] = x_ref[...] + jnp.dot(m.astype(BF16), wo_ref[...], preferred_element_type=F32)

    row = lambda n: pl.BlockSpec((tm, n), lambda i: (i, 0))
    return pl.pallas_call(
        body, name="merge_fwd", grid=(T // tm,),
        in_specs=[row(HG_WIDTH), row(CONV_WIDTH), row(2 * D_MODEL), row(D_MODEL),
                  _resident(wa.shape), _resident(wb.shape), _resident(wo.shape)],
        out_specs=[row(D_MODEL), _col(tm, D_MODEL)],
        out_shape=[jax.ShapeDtypeStruct((T, D_MODEL), F32), jax.ShapeDtypeStruct((D_MODEL, T), BF16)],
        compiler_params=_params(("parallel",)),
    )(og, cvo, gt, x, wa, wb, wo)


def _ffn_fwd(x1, g, wg, wu, wd):
    T = x1.shape[0]
    tm = min(256, T)

    def body(x_ref, g_ref, wg_ref, wu_ref, wd_ref, ht_ref, gate_ref, up_ref, actt_ref, x2_ref):
        xv = x_ref[...]
        r = lax.rsqrt(jnp.mean(xv * xv, axis=-1, keepdims=True) + EPS)
        hf = xv * r * g_ref[...]
        h = hf.astype(BF16)
        ht_ref[...] = hf.T.astype(BF16)
        gate = jnp.dot(h, wg_ref[...], preferred_element_type=F32)
        up = jnp.dot(h, wu_ref[...], preferred_element_type=F32)
        gate_ref[...] = gate
        up_ref[...] = up
        act = gate * _sigmoid(gate) * up
        actt_ref[...] = act.T.astype(BF16)
        x2_ref[...] = xv + jnp.dot(act.astype(BF16), wd_ref[...], preferred_element_type=F32)

    row = lambda n: pl.BlockSpec((tm, n), lambda i: (i, 0))
    return pl.pallas_call(
        body, name="ffn_fwd", grid=(T // tm,),
        in_specs=[row(D_MODEL), _full((1, D_MODEL)), _resident(wg.shape), _resident(wu.shape), _resident(wd.shape)],
        out_specs=[_col(tm, D_MODEL), row(D_FF), row(D_FF), _col(tm, D_FF), row(D_MODEL)],
        out_shape=[jax.ShapeDtypeStruct((D_MODEL, T), BF16), jax.ShapeDtypeStruct((T, D_FF), F32),
                   jax.ShapeDtypeStruct((T, D_FF), F32), jax.ShapeDtypeStruct((D_FF, T), BF16),
                   jax.ShapeDtypeStruct((T, D_MODEL), F32)],
        compiler_params=_params(("parallel",)),
    )(x1, g, wg, wu, wd)


def _final_fwd_bwd(x2, target, g):
    T = x2.shape[0]
    tm = min(512, T)

    def body(x_ref, t_ref, g_ref, loss_ref, dg_ref, dx_ref):
        @pl.when(pl.program_id(0) == 0)
        def _():
            loss_ref[...] = jnp.zeros_like(loss_ref)
            dg_ref[...] = jnp.zeros_like(dg_ref)

        xv = x_ref[...]
        gv = g_ref[...]
        r = lax.rsqrt(jnp.mean(xv * xv, axis=-1, keepdims=True) + EPS)
        xh = xv * r
        err = xh * gv - t_ref[...]
        loss_ref[...] += 0.5 * jnp.sum(jnp.mean(err * err, axis=-1, keepdims=True), axis=0, keepdims=True)
        dy = err * (1.0 / D_MODEL)
        dg_ref[...] += jnp.sum(dy * xh, axis=0, keepdims=True)
        w = dy * gv
        dx_ref[...] = r * (w - xh * jnp.mean(w * xh, axis=-1, keepdims=True))

    row = pl.BlockSpec((tm, D_MODEL), lambda i: (i, 0))
    return pl.pallas_call(
        body, name="final_fwd_bwd", grid=(T // tm,),
        in_specs=[row, row, _full((1, D_MODEL))],
        out_specs=[_full((1, 128)), _full((1, D_MODEL)), row],
        out_shape=[jax.ShapeDtypeStruct((1, 128), F32), jax.ShapeDtypeStruct((1, D_MODEL), F32),
                   jax.ShapeDtypeStruct((T, D_MODEL), F32)],
        compiler_params=_params(("arbitrary",)),
    )(x2, target, g)


def _ffn_bwd(dx2, x1, gate, up, g, wg, wu, wd):
    T = x1.shape[0]
    tm = min(256, T)

    def body(dx2_ref, x_ref, gate_ref, up_ref, g_ref, wg_ref, wu_ref, wd_ref, dgate_ref, dup_ref, dx1_ref, dgn_ref):
        @pl.when(pl.program_id(0) == 0)
        def _():
            dgn_ref[...] = jnp.zeros_like(dgn_ref)

        dx2 = dx2_ref[...]
        dact = _mm_nt(dx2, wd_ref[...])
        gate = gate_ref[...]
        s = _sigmoid(gate)
        dgate = (dact * up_ref[...] * (s * (1.0 + gate * (1.0 - s)))).astype(BF16)
        dup = (dact * (gate * s)).astype(BF16)
        dgate_ref[...] = dgate
        dup_ref[...] = dup
        dh = _mm_nt(dgate, wg_ref[...]) + _mm_nt(dup, wu_ref[...])
        xv = x_ref[...]
        r = lax.rsqrt(jnp.mean(xv * xv, axis=-1, keepdims=True) + EPS)
        xh = xv * r
        dgn_ref[...] += jnp.sum(dh * xh, axis=0, keepdims=True)
        w = dh * g_ref[...]
        dx1_ref[...] = dx2 + r * (w - xh * jnp.mean(w * xh, axis=-1, keepdims=True))

    row = lambda n: pl.BlockSpec((tm, n), lambda i: (i, 0))
    return pl.pallas_call(
        body, name="ffn_bwd", grid=(T // tm,),
        in_specs=[row(D_MODEL), row(D_MODEL), row(D_FF), row(D_FF), _full((1, D_MODEL)),
                  _resident(wg.shape), _resident(wu.shape), _resident(wd.shape)],
        out_specs=[row(D_FF), row(D_FF), row(D_MODEL), _full((1, D_MODEL))],
        out_shape=[jax.ShapeDtypeStruct((T, D_FF), BF16), jax.ShapeDtypeStruct((T, D_FF), BF16),
                   jax.ShapeDtypeStruct((T, D_MODEL), F32), jax.ShapeDtypeStruct((1, D_MODEL), F32)],
        compiler_params=_params(("arbitrary",)),
    )(dx2, x1, gate, up, g, wg, wu, wd)


def _merge_bwd(dx1, og, cvo, gt, wa, wb, wo):
    T = dx1.shape[0]
    tm = min(512, T)

    def body(dx_ref, og_ref, cvo_ref, gt_ref, wa_ref, wb_ref, wo_ref, dgt_ref, dya_ref, dyb_ref, dog_ref, dcvo_ref):
        dm = _mm_nt(dx_ref[...], wo_ref[...])
        wa = _shard_cols(wa_ref)
        wb = _shard_cols(wb_ref)
        ya = jnp.dot(og_ref[...], wa, preferred_element_type=F32)
        yb = jnp.dot(cvo_ref[...], wb, preferred_element_type=F32)
        sa = _sigmoid(gt_ref[:, :D_MODEL])
        sb = _sigmoid(gt_ref[:, D_MODEL:])
        dgt_ref[:, :D_MODEL] = (dm * ya * (sa * (1.0 - sa))).astype(BF16)
        dgt_ref[:, D_MODEL:] = (dm * yb * (sb * (1.0 - sb))).astype(BF16)
        dya = (dm * sa).astype(BF16)
        dyb = (dm * sb).astype(BF16)
        dya_ref[...] = dya
        dyb_ref[...] = dyb
        dog_ref[...] = _mm_nt(dya, wa)
        dcvo_ref[...] = _mm_nt(dyb, wb)

    row = lambda n: pl.BlockSpec((tm, n), lambda i: (i, 0))
    return pl.pallas_call(
        body, name="merge_bwd", grid=(T // tm,),
        in_specs=[row(D_MODEL), row(HG_WIDTH), row(CONV_WIDTH), row(2 * D_MODEL),
                  _resident(wa.shape), _resident(wb.shape), _resident(wo.shape)],
        out_specs=[row(2 * D_MODEL), row(D_MODEL), row(D_MODEL), row(HG_WIDTH), row(CONV_WIDTH)],
        out_shape=[jax.ShapeDtypeStruct((T, 2 * D_MODEL), BF16), jax.ShapeDtypeStruct((T, D_MODEL), BF16),
                   jax.ShapeDtypeStruct((T, D_MODEL), BF16), jax.ShapeDtypeStruct((T, HG_WIDTH), F32),
                   jax.ShapeDtypeStruct((T, CONV_WIDTH), F32)],
        compiler_params=_params(("parallel",)),
    )(dx1, og, cvo, gt, wa, wb, wo)


def _conv_bwd(dcvo, cv, conv_w):
    T = cv.shape[0]
    nj = CONV_WIDTH // 128

    def body(do_ref, c_ref, b_ref, x_ref, w_ref, dc_ref, db_ref, dx_ref, dw_ref):
        row = lax.broadcasted_iota(jnp.int32, (T, 128), 0)
        c = c_ref[...]
        xb = x_ref[...]
        do = do_ref[...]
        u = c * xb
        u1 = jnp.where(row >= 1, pltpu.roll(u, 1, 0), 0.0)
        u2 = jnp.where(row >= 2, pltpu.roll(u, 2, 0), 0.0)
        w0, w1, w2 = w_ref[0:1, :], w_ref[1:2, :], w_ref[2:3, :]
        y = w0 * u2 + w1 * u1 + w2 * u
        db_ref[...] = (do * y).astype(BF16)
        dy = do * b_ref[...]
        dw_ref[0:1, :] = jnp.sum(dy * u2, axis=0, keepdims=True)
        dw_ref[1:2, :] = jnp.sum(dy * u1, axis=0, keepdims=True)
        dw_ref[2:3, :] = jnp.sum(dy * u, axis=0, keepdims=True)
        dy1 = jnp.where(row < T - 1, pltpu.roll(dy, T - 1, 0), 0.0)
        dy2 = jnp.where(row < T - 2, pltpu.roll(dy, T - 2, 0), 0.0)
        du = w2 * dy + w1 * dy1 + w0 * dy2
        dc_ref[...] = (du * xb).astype(BF16)
        dx_ref[...] = (du * c).astype(BF16)

    col = lambda p: pl.BlockSpec((T, 128), lambda j: (0, p * nj + j))
    one = pl.BlockSpec((T, 128), lambda j: (0, j))
    wspec = pl.BlockSpec((3, 128), lambda j: (0, j))
    out = jax.ShapeDtypeStruct((T, CONV_WIDTH), BF16)
    return pl.pallas_call(
        body, name="conv_bwd", grid=(nj,),
        in_specs=[one, col(0), col(1), col(2), wspec],
        out_specs=[one, one, one, wspec],
        out_shape=[out, out, out, jax.ShapeDtypeStruct((3, CONV_WIDTH), F32)],
        compiler_params=_params(("parallel",)),
    )(dcvo, cv, cv, cv, conv_w)


def _hg_bwd(dog, hg, o, st, low, gn):
    T = hg.shape[0]
    tb = min(512, T)
    sb = min(256, tb)
    nb = T // tb
    nc = tb // CHUNK

    def body(q_ref, f_ref, i_ref, g_ref, low_ref, gn_ref, o_ref, dog_ref, st_ref,
             dq_ref, df_ref, di_ref, dg_ref, dlow_ref, dgn_ref,
             ds_scr, dqi_scr, dko_scr, dv_scr, dd_scr, dqh_scr, dkh_scr):
        h = pl.program_id(0)
        t = pl.program_id(1)

        @pl.when(t == 0)
        def _():
            ds_scr[...] = jnp.zeros_like(ds_scr)
            dlow_ref[...] = jnp.zeros_like(dlow_ref)

        @pl.when((t == 0) & (h == 0))
        def _():
            dgn_ref[...] = jnp.zeros_like(dgn_ref)

        pos = _chunk_pos((tb, HEAD_DIM))
        lb, lb1 = _lower_bound(low_ref)
        qr = q_ref[...]
        v = i_ref[...]
        sq, q, sg, f, k, e_qa, e_ka, e_b, e_ko, dec = _hg_gates(qr, f_ref[...], lb, pos, tb)

        gr = g_ref[...]
        gnv = gn_ref[...]
        o = o_ref[...]
        dog_v = dog_ref[...]
        sgr = _sigmoid(gr)
        r = lax.rsqrt(jnp.mean(o * o, axis=-1, keepdims=True) + EPS)
        oh = o * r
        dg_ref[...] = (dog_v * (oh * gnv) * (sgr * (1.0 + gr * (1.0 - sgr)))).astype(BF16)
        don = dog_v * (gr * sgr)
        dgn_ref[...] += jnp.sum(don * oh, axis=0, keepdims=True)
        w = don * gnv
        do = (r * (w - oh * jnp.mean(w * oh, axis=-1, keepdims=True))).astype(BF16)

        qh = (q * e_qa).astype(BF16)
        kh = (k * e_ka).astype(BF16)
        qi = (q * e_b).astype(BF16)
        ko = (k * e_ko).astype(BF16)
        vb = v.astype(BF16)

        mask = _intra_mask(sb)
        for s in range(tb // sb):
            sl = slice(s * sb, (s + 1) * sb)
            p = jnp.where(mask, _mm_nt(qh[sl], kh[sl]), 0.0).astype(BF16)
            dp = jnp.where(mask, _mm_nt(do[sl], vb[sl]), 0.0).astype(BF16)
            dv_scr[sl, :] = _mm_tn(p, do[sl])
            dqh_scr[sl, :] = _mm(dp, kh[sl])
            dkh_scr[sl, :] = _mm_tn(dp, qh[sl])

        ds = ds_scr[...]
        for c in reversed(range(nc)):
            sl = slice(c * CHUNK, (c + 1) * CHUNK)
            st_c = st_ref[c]
            dqi_scr[sl, :] = _mm(do[sl], st_c)
            dko_scr[sl, :] = _mm(vb[sl], ds)
            dv_scr[sl, :] = dv_scr[sl, :] + _mm_nt(ko[sl], ds)
            dd_scr[sl, :] = jnp.broadcast_to(jnp.sum(ds * st_c, axis=0, keepdims=True), (CHUNK, HEAD_DIM))
            ds = dec[c * CHUNK:c * CHUNK + 1, :] * ds + _mm_tn(do[sl], qi[sl])
        ds_scr[...] = ds

        dko_e = dko_scr[...] * e_ko
        dq = dqh_scr[...] * e_qa + dqi_scr[...] * e_b
        dk = dkh_scr[...] * e_ka + dko_e
        kd3 = (k * dko_e).reshape(nc, CHUNK, HEAD_DIM)
        last = jnp.broadcast_to(jnp.sum(kd3, axis=1, keepdims=True), kd3.shape).reshape(tb, HEAD_DIM)
        db = q * dq - k * dk + jnp.where(pos == CHUNK - 1, dec * dd_scr[...] + last, 0.0)
        dlg = _chunk_rev_cumsum(db, pos)
        dfv = dlg / f - dk
        s_low = jnp.sum(dfv * (1.0 - sg), axis=0, keepdims=True)
        dlow_ref[0:1, :] += s_low * lb * (1.0 - lb)
        dlow_ref[1:2, :] += -s_low * lb * lb1
        df_ref[...] = (dfv * (1.0 - lb) * sg * (1.0 - sg)).astype(BF16)
        dq_ref[...] = (dq * Q_SCALE * (sq * (1.0 + qr * (1.0 - sq)))).astype(BF16)
        di_ref[...] = dv_scr[...].astype(BF16)

    rt = lambda t: nb - 1 - t
    col = lambda p: pl.BlockSpec((tb, HEAD_DIM), lambda h, t: (rt(t), p * N_HEADS + h))
    hcol = pl.BlockSpec((tb, HEAD_DIM), lambda h, t: (rt(t), h))
    piece = jax.ShapeDtypeStruct((T, HG_WIDTH), BF16)
    tile = pltpu.VMEM((tb, HEAD_DIM), F32)
    return pl.pallas_call(
        body, name="hg_bwd", grid=(N_HEADS, nb),
        in_specs=[col(0), col(1), col(2), col(3), pl.BlockSpec((2, HEAD_DIM), lambda h, t: (0, h)),
                  pl.BlockSpec((1, HEAD_DIM), lambda h, t: (0, 0)), hcol, hcol,
                  pl.BlockSpec((None, nc, HEAD_DIM, HEAD_DIM), lambda h, t: (h, rt(t), 0, 0))],
        out_specs=[hcol, hcol, hcol, hcol, pl.BlockSpec((2, HEAD_DIM), lambda h, t: (0, h)),
                   pl.BlockSpec((1, HEAD_DIM), lambda h, t: (0, 0))],
        out_shape=[piece, piece, piece, piece, jax.ShapeDtypeStruct((2, HG_WIDTH), F32),
                   jax.ShapeDtypeStruct((1, HEAD_DIM), F32)],
        scratch_shapes=[pltpu.VMEM((HEAD_DIM, HEAD_DIM), F32), tile, tile, tile, tile, tile, tile],
        compiler_params=_params(("arbitrary", "arbitrary")),
    )(hg, hg, hg, hg, low, gn, o, dog, st)


def _in_bwd(dparts, w_in, x, dx1, g):
    T = x.shape[0]
    tm = min(256, T)
    widths = [p.shape[1] for p in dparts]
    offs = [sum(widths[:i]) for i in range(len(widths))]
    n = len(dparts)

    def body(*refs):
        d_refs = refs[:n]
        w_ref, x_ref, dx1_ref, g_ref, dx_ref, dgn_ref = refs[n:]

        @pl.when(pl.program_id(0) == 0)
        def _():
            dgn_ref[...] = jnp.zeros_like(dgn_ref)

        dh = None
        for d_ref, off, wd in zip(d_refs, offs, widths):
            part = _mm_nt(d_ref[...], w_ref[:, off:off + wd])
            dh = part if dh is None else dh + part
        xv = x_ref[...]
        r = lax.rsqrt(jnp.mean(xv * xv, axis=-1, keepdims=True) + EPS)
        xh = xv * r
        dgn_ref[...] += jnp.sum(dh * xh, axis=0, keepdims=True)
        w = dh * g_ref[...]
        dx_ref[...] = dx1_ref[...] + r * (w - xh * jnp.mean(w * xh, axis=-1, keepdims=True))

    row = lambda m: pl.BlockSpec((tm, m), lambda i: (i, 0))
    return pl.pallas_call(
        body, name="in_bwd", grid=(T // tm,),
        in_specs=[row(wd) for wd in widths] + [_resident(w_in.shape), row(D_MODEL), row(D_MODEL), _full((1, D_MODEL))],
        out_specs=[row(D_MODEL), _full((1, D_MODEL))],
        out_shape=[jax.ShapeDtypeStruct((T, D_MODEL), F32), jax.ShapeDtypeStruct((1, D_MODEL), F32)],
        compiler_params=_params(("arbitrary",)),
    )(*dparts, w_in, x, dx1, g)


def _wgrad(name, at, b, tn):
    M, T = at.shape
    N = b.shape[1]
    tk = min(1024, T)
    nk = T // tk

    def body(a_ref, b_ref, o_ref, acc):
        k = pl.program_id(1)
        part = _mm(a_ref[...], b_ref[...])

        @pl.when(k == 0)
        def _():
            acc[...] = part

        @pl.when(k != 0)
        def _():
            acc[...] += part

        @pl.when(k == nk - 1)
        def _():
            o_ref[...] = acc[...].astype(BF16)

    return pl.pallas_call(
        body, name=name, grid=(N // tn, nk),
        in_specs=[pl.BlockSpec((M, tk), lambda j, k: (0, k)), pl.BlockSpec((tk, tn), lambda j, k: (k, j))],
        out_specs=pl.BlockSpec((M, tn), lambda j, k: (0, j)),
        out_shape=jax.ShapeDtypeStruct((M, N), BF16),
        scratch_shapes=[pltpu.VMEM((M, tn), F32)],
        compiler_params=_params(("parallel", "arbitrary")),
    )(at, b)


def _wgrad_in(ht, dparts):
    M, T = ht.shape
    tn = 512
    tk = min(1024, T)
    nk = T // tk
    nblk = [p.shape[1] // tn for p in dparts]
    start = [sum(nblk[:i]) for i in range(len(nblk))]
    n = len(dparts)

    def body(a_ref, *refs):
        d_refs, o_ref, acc = refs[:n], refs[n], refs[n + 1]
        j = pl.program_id(0)
        k = pl.program_id(1)

        @pl.when(k == 0)
        def _():
            acc[...] = jnp.zeros_like(acc)

        for d_ref, s, nb in zip(d_refs, start, nblk):
            @pl.when((j >= s) & (j < s + nb))
            def _():
                acc[...] += _mm(a_ref[...], d_ref[...])

        @pl.when(k == nk - 1)
        def _():
            o_ref[...] = acc[...].astype(BF16)

    def piece_spec(s, nb):
        def index(j, k):
            inside = (j >= s) & (j < s + nb)
            return jnp.where(inside, k, 0), jnp.clip(j - s, 0, nb - 1)
        return pl.BlockSpec((tk, tn), index)

    return pl.pallas_call(
        body, name="wgrad_in", grid=(sum(nblk), nk),
        in_specs=[pl.BlockSpec((M, tk), lambda j, k: (0, k))] + [piece_spec(s, nb) for s, nb in zip(start, nblk)],
        out_specs=pl.BlockSpec((M, tn), lambda j, k: (0, j)),
        out_shape=jax.ShapeDtypeStruct((M, sum(nblk) * tn), BF16),
        scratch_shapes=[pltpu.VMEM((M, tn), F32)],
        compiler_params=_params(("parallel", "arbitrary")),
    )(ht, *dparts)


def _adamw_math(w, g, m, v):
    m = ADAM_B1 * m + (1.0 - ADAM_B1) * g
    v = ADAM_B2 * v + (1.0 - ADAM_B2) * (g * g)
    m_hat = m / (1.0 - ADAM_B1 ** ADAM_STEP)
    v_hat = v / (1.0 - ADAM_B2 ** ADAM_STEP)
    delta = -ADAM_LR * (m_hat / (jnp.sqrt(v_hat) + ADAM_EPS) + ADAM_WD * w)
    return delta, m, v


def _adamw_sum(name, w, parts, m, v):
    R, C = w.shape
    tr = R
    for cand in (256, 128, 64, 32, 16, 8):
        if R % cand == 0:
            tr = cand
            break

    def body(w_ref, p_ref, m_ref, v_ref, g_out, d_out, m_out, v_out):
        g = p_ref[0].astype(F32)
        for k in range(1, 4):
            g = g + p_ref[k].astype(F32)
        g_out[...] = g
        d_out[...], m_out[...], v_out[...] = _adamw_math(w_ref[...], g, m_ref[...], v_ref[...])

    blk = pl.BlockSpec((tr, C), lambda i: (i, 0))
    out = jax.ShapeDtypeStruct((R, C), F32)
    return pl.pallas_call(
        body, name=name, grid=(R // tr,),
        in_specs=[blk, pl.BlockSpec((4, tr, C), lambda i: (0, i, 0)), blk, blk],
        out_specs=[blk, blk, blk, blk], out_shape=[out, out, out, out],
        compiler_params=_params(("parallel",)),
    )(w, parts, m, v)


def _small_sum(gathered):
    R = gathered.shape[1]

    def body(p_ref, o_ref):
        g = p_ref[0]
        for k in range(1, N_DEV):
            g = g + p_ref[k]
        o_ref[...] = g

    return pl.pallas_call(
        body, name="small_sum", in_specs=[_full(gathered.shape)], out_specs=_full((R, 128)), grid=(1,),
        out_shape=jax.ShapeDtypeStruct((R, 128), F32),
    )(gathered)


def _small_adamw(w, g, m, v):
    def body(w_ref, g_ref, m_ref, v_ref, d_out, m_out, v_out):
        d_out[...], m_out[...], v_out[...] = _adamw_math(w_ref[...], g_ref[...], m_ref[...], v_ref[...])

    out = jax.ShapeDtypeStruct(w.shape, F32)
    spec = _full(w.shape)
    return pl.pallas_call(
        body, name="small_adamw", grid=(1,), in_specs=[spec] * 4, out_specs=[spec] * 3, out_shape=[out, out, out],
    )(w, g, m, v)


def _row_tile(rows):
    for cand in (256, 128, 64, 32, 16):
        if rows % cand == 0:
            return cand
    return rows


def _pair_sum(name, by_owner, got, core):
    _, R, C = got.shape
    tr = _row_tile(R)

    def body(core_ref, a_ref, b_ref, o_ref):
        o_ref[...] = (a_ref[...].astype(F32) + b_ref[...].astype(F32)).astype(BF16)

    blk = pl.BlockSpec((None, tr, C), lambda k, i, core_ref: (k, i, 0))
    mine = pl.BlockSpec((None, tr, C), lambda k, i, core_ref: (2 * k + core_ref[0], i, 0))
    return pl.pallas_call(
        body, name=name,
        grid_spec=pltpu.PrefetchScalarGridSpec(num_scalar_prefetch=1, grid=(4, R // tr), in_specs=[mine, blk],
                                               out_specs=blk),
        out_shape=jax.ShapeDtypeStruct(got.shape, BF16),
        compiler_params=_params(("parallel", "parallel")),
    )(core, by_owner, got)


def _cols_from_shards(name, g):
    _, R, c = g.shape
    tr = _row_tile(R)

    def body(g_ref, o_ref):
        for s in range(N_DEV):
            o_ref[:, s * c:(s + 1) * c] = g_ref[s]

    return pl.pallas_call(
        body, name=name, grid=(R // tr,),
        in_specs=[pl.BlockSpec((N_DEV, tr, c), lambda i: (0, i, 0))],
        out_specs=pl.BlockSpec((tr, N_DEV * c), lambda i: (i, 0)),
        out_shape=jax.ShapeDtypeStruct((R, N_DEV * c), g.dtype),
        compiler_params=_params(("parallel",)),
    )(g)


def _shards_from_cols(name, full):
    R, allc = full.shape
    c = allc // N_DEV
    tr = _row_tile(R)

    def body(f_ref, o_ref):
        for s in range(N_DEV):
            o_ref[s] = f_ref[:, s * c:(s + 1) * c]

    return pl.pallas_call(
        body, name=name, grid=(R // tr,),
        in_specs=[pl.BlockSpec((tr, allc), lambda i: (i, 0))],
        out_specs=pl.BlockSpec((N_DEV, tr, c), lambda i: (0, i, 0)),
        out_shape=jax.ShapeDtypeStruct((N_DEV, R, c), full.dtype),
        compiler_params=_params(("parallel",)),
    )(full)


MESH = pl.DeviceIdType.MESH
HBM_SPEC = pl.BlockSpec(memory_space=pl.ANY)


def _all_gather(name, blocks):
    n = len(blocks)

    def body(*refs):
        x_refs, out_refs = refs[:n], refs[n:2 * n]
        send_sems, recv_sems, local_sems = refs[2 * n:]
        x, y, c = lax.axis_index("x"), lax.axis_index("y"), lax.axis_index("c")
        me, sibling = (x, y, c), (x, y, 1 - c)
        chips = [(1 - x, y), (x, 1 - y), (1 - x, 1 - y)]

        def slot(i, px, py, pc):
            return out_refs[i].at[4 * px + 2 * py + pc]

        def copy(i, k, blk, to, src=None):
            return pltpu.make_async_remote_copy(
                src_ref=slot(i, *blk) if src is None else src, dst_ref=slot(i, *blk),
                send_sem=send_sems.at[7 * i + k], recv_sem=recv_sems.at[7 * i + k], device_id=to, device_id_type=MESH)

        mine = [pltpu.make_async_copy(x_refs[i], slot(i, *me), local_sems.at[i]) for i in range(n)]
        for cp in mine:
            cp.start()
        first = []
        for i in range(n):
            first.append(copy(i, 0, me, sibling, src=x_refs[i]))
            first += [copy(i, 1 + j, me, (*chip, c), src=x_refs[i]) for j, chip in enumerate(chips)]
        for cp in first:
            cp.start()
        passed = []
        for i in range(n):
            for j, chip in enumerate(chips):
                copy(i, 1 + j, (*chip, c), me).wait_recv()
                passed.append(copy(i, 4 + j, (*chip, c), sibling))
                passed[-1].start()
        for i in range(n):
            copy(i, 0, sibling, me).wait_recv()
            for j, chip in enumerate(chips):
                copy(i, 4 + j, (*chip, 1 - c), me).wait_recv()
        for cp in first + passed:
            cp.wait_send()
        for cp in mine:
            cp.wait()

    return pl.pallas_call(
        body, name=name, in_specs=[HBM_SPEC] * n, out_specs=[HBM_SPEC] * n,
        out_shape=[jax.ShapeDtypeStruct((N_DEV,) + b.shape, b.dtype) for b in blocks],
        scratch_shapes=[pltpu.SemaphoreType.DMA((7 * n,)), pltpu.SemaphoreType.DMA((7 * n,)),
                        pltpu.SemaphoreType.DMA((n,))],
    )(*blocks)


def _sibling_swap(by_owner):
    n = len(by_owner)

    def body(*refs):
        x_refs, out_refs = refs[:n], refs[n:2 * n]
        send_sems, recv_sems = refs[2 * n:]
        x, y, c = lax.axis_index("x"), lax.axis_index("y"), lax.axis_index("c")
        copies = []
        for i in range(n):
            for k in range(4):
                copies.append(pltpu.make_async_remote_copy(
                    src_ref=x_refs[i].at[2 * k + 1 - c], dst_ref=out_refs[i].at[k],
                    send_sem=send_sems.at[4 * i + k], recv_sem=recv_sems.at[4 * i + k],
                    device_id=(x, y, 1 - c), device_id_type=MESH))
        for cp in copies:
            cp.start()
        for cp in copies:
            cp.wait()

    return pl.pallas_call(
        body, name="sibling_swap", in_specs=[HBM_SPEC] * n, out_specs=[HBM_SPEC] * n,
        out_shape=[jax.ShapeDtypeStruct((4,) + b.shape[1:], b.dtype) for b in by_owner],
        scratch_shapes=[pltpu.SemaphoreType.DMA((4 * n,)), pltpu.SemaphoreType.DMA((4 * n,))],
    )(*by_owner)


def _chip_exchange(sums):
    n = len(sums)

    def body(*refs):
        x_refs, out_refs = refs[:n], refs[n:2 * n]
        send_sems, recv_sems, local_sems = refs[2 * n:]
        x, y, c = lax.axis_index("x"), lax.axis_index("y"), lax.axis_index("c")
        chips = [(1 - x, y), (x, 1 - y), (1 - x, 1 - y)]
        my_chip = 2 * x + y
        mine = [pltpu.make_async_copy(x_refs[i].at[my_chip], out_refs[i].at[my_chip], local_sems.at[i])
                for i in range(n)]
        for cp in mine:
            cp.start()
        sends = []
        for i in range(n):
            for j, (cx, cy) in enumerate(chips):
                sends.append(pltpu.make_async_remote_copy(
                    src_ref=x_refs[i].at[2 * cx + cy], dst_ref=out_refs[i].at[my_chip],
                    send_sem=send_sems.at[3 * i + j], recv_sem=recv_sems.at[3 * i + j],
                    device_id=(cx, cy, c), device_id_type=MESH))
        for cp in sends:
            cp.start()
        for i in range(n):
            for j, (cx, cy) in enumerate(chips):
                pltpu.make_async_remote_copy(
                    src_ref=x_refs[i].at[my_chip], dst_ref=out_refs[i].at[2 * cx + cy],
                    send_sem=send_sems.at[3 * i + j], recv_sem=recv_sems.at[3 * i + j],
                    device_id=(cx, cy, c), device_id_type=MESH).wait_recv()
        for cp in sends:
            cp.wait_send()
        for cp in mine:
            cp.wait()

    return pl.pallas_call(
        body, name="chip_exchange", in_specs=[HBM_SPEC] * n, out_specs=[HBM_SPEC] * n,
        out_shape=[jax.ShapeDtypeStruct(s.shape, s.dtype) for s in sums],
        scratch_shapes=[pltpu.SemaphoreType.DMA((3 * n,)), pltpu.SemaphoreType.DMA((3 * n,)),
                        pltpu.SemaphoreType.DMA((n,))],
    )(*sums)


def _cast_shards(shards):
    n = len(shards)

    def body(*refs):
        for i in range(n):
            refs[n + i][...] = refs[i][...].astype(BF16)

    vmem = pl.BlockSpec(memory_space=pltpu.VMEM)
    return pl.pallas_call(
        body, name="cast_shards", in_specs=[vmem] * n, out_specs=[vmem] * n,
        out_shape=[jax.ShapeDtypeStruct(s.shape, BF16) for s in shards],
        compiler_params=pltpu.CompilerParams(vmem_limit_bytes=VMEM_LIMIT_V7X),
    )(*shards)


BIG = ("w_in", "w_branch_a", "w_branch_b", "w_out", "w_ffn_gate", "w_ffn_up", "w_ffn_down")


def _local_step(x, target, gains, low, conv_w, wg8):
    g_mix, g_hg, g_ffn, g_fin = gains
    w_in = _cols_from_shards("join_w_in", wg8["w_in"])
    wg = _cols_from_shards("join_w_ffn_gate", wg8["w_ffn_gate"])
    wu = _cols_from_shards("join_w_ffn_up", wg8["w_ffn_up"])
    wa, wb = wg8["w_branch_a"], wg8["w_branch_b"]
    wo = wg8["w_out"].reshape(D_MODEL, D_MODEL)
    wd = wg8["w_ffn_down"].reshape(D_FF, D_MODEL)

    ht, hg, cv, gt = _fwd_in(x, g_mix, w_in)
    o, og, ogt, st = _hg_fwd(hg, low, g_hg)
    cvo, cvot = _conv_fwd(cv, conv_w)
    x1, mgt = _merge_fwd(og, cvo, gt, x, wa, wb, wo)
    h2t, gate, up, actt, x2 = _ffn_fwd(x1, g_ffn, wg, wu, wd)
    loss, d_gfin, dx2 = _final_fwd_bwd(x2, target, g_fin)

    dgate, dup, dx1, d_gffn = _ffn_bwd(dx2, x1, gate, up, g_ffn, wg, wu, wd)
    gw = {}
    gw["w_ffn_down"] = _wgrad("wgrad_ffn_down", actt, dx2, 512).reshape(N_DEV, D_FF // N_DEV, D_MODEL)
    gw["w_ffn_gate"] = _shards_from_cols("split_w_ffn_gate", _wgrad("wgrad_ffn_gate", h2t, dgate, 1408))
    gw["w_ffn_up"] = _shards_from_cols("split_w_ffn_up", _wgrad("wgrad_ffn_up", h2t, dup, 1408))
    dgt, dya, dyb, dog, dcvo = _merge_bwd(dx1, og, cvo, gt, wa, wb, wo)
    gw["w_out"] = _wgrad("wgrad_out", mgt, dx1, 512).reshape(N_DEV, D_MODEL // N_DEV, D_MODEL)
    gw["w_branch_a"] = _shards_from_cols("split_w_branch_a", _wgrad("wgrad_branch_a", ogt, dya, 512))
    gw["w_branch_b"] = _shards_from_cols("split_w_branch_b", _wgrad("wgrad_branch_b", cvot, dyb, 512))
    dc, db, dxb, d_conv = _conv_bwd(dcvo, cv, conv_w)
    dq, df, di, dg, d_low, d_ghg = _hg_bwd(dog, hg, o, st, low, g_hg)
    dparts = [dq, df, di, dg, dc, db, dxb, dgt]
    grad_x, d_gmix = _in_bwd(dparts, w_in, x, dx1, g_mix)
    gw["w_in"] = _shards_from_cols("split_w_in", _wgrad_in(ht, dparts))
    small = dict(norm_mix_g=d_gmix, norm_ffn_g=d_gffn, norm_final_g=d_gfin, lower_bounds=d_low, hg_norm_g=d_ghg,
                 conv_w=d_conv, loss=loss)
    return grad_x, gw, small


_SMALL_LAYOUT = (("norm_mix_g", 0, 8), ("norm_ffn_g", 8, 8), ("norm_final_g", 16, 8), ("lower_bounds", 24, 8),
                 ("hg_norm_g", 32, 1))
_LOSS_ROW = 40
_CONV_ROW = 48


def _pad_rows(a, rows):
    return jnp.pad(a, ((0, rows - a.shape[0]), (0, 0)))


def _pack_small(vals, conv_rows):
    parts = [_pad_rows(vals[name].reshape(rows, 128), 8) for name, _, rows in _SMALL_LAYOUT]
    loss = vals["loss"][:, :128] if "loss" in vals else jnp.zeros((1, 128), F32)
    parts.append(_pad_rows(loss, 8))
    parts.append(_pad_rows(conv_rows, SMALL_ROWS - _CONV_ROW))
    return jnp.concatenate(parts, axis=0)


def _conv_shard_rows(a):
    return jnp.pad(a, ((0, 5), (0, 64)))


def kernel(x, norm_mix_g, w_in, lower_bounds, hg_norm_g, conv_w, w_branch_a, w_branch_b, w_out, norm_ffn_g, w_ffn_gate, w_ffn_up, w_ffn_down, norm_final_g, loss_target, m_norm_mix_g, m_w_in, m_lower_bounds, m_hg_norm_g, m_conv_w, m_w_branch_a, m_w_branch_b, m_w_out, m_norm_ffn_g, m_w_ffn_gate, m_w_ffn_up, m_w_ffn_down, m_norm_final_g, v_norm_mix_g, v_w_in, v_lower_bounds, v_hg_norm_g, v_conv_w, v_w_branch_a, v_w_branch_b, v_w_out, v_norm_ffn_g, v_w_ffn_gate, v_w_ffn_up, v_w_ffn_down, v_norm_final_g):
    cx, cy, cc = lax.axis_index("x"), lax.axis_index("y"), lax.axis_index("c")
    my_dev = 4 * cx + 2 * cy + cc

    big = dict(w_in=w_in[0], w_branch_a=w_branch_a[0], w_branch_b=w_branch_b[0], w_out=w_out[0],
               w_ffn_gate=w_ffn_gate[0], w_ffn_up=w_ffn_up[0], w_ffn_down=w_ffn_down[0])
    big_m = dict(w_in=m_w_in[0], w_branch_a=m_w_branch_a[0], w_branch_b=m_w_branch_b[0], w_out=m_w_out[0],
                 w_ffn_gate=m_w_ffn_gate[0], w_ffn_up=m_w_ffn_up[0], w_ffn_down=m_w_ffn_down[0])
    big_v = dict(w_in=v_w_in[0], w_branch_a=v_w_branch_a[0], w_branch_b=v_w_branch_b[0], w_out=v_w_out[0],
                 w_ffn_gate=v_w_ffn_gate[0], w_ffn_up=v_w_ffn_up[0], w_ffn_down=v_w_ffn_down[0])

    shards = _cast_shards([big[n] for n in BIG])
    gathered = _all_gather("gather_weights", list(shards) + [_conv_shard_rows(conv_w[0])])
    wg8 = dict(zip(BIG, gathered[:-1]))
    conv_full = gathered[-1][:, :3, :64].transpose(1, 0, 2).reshape(3, CONV_WIDTH)

    gains = (norm_mix_g, hg_norm_g, norm_ffn_g, norm_final_g.reshape(1, D_MODEL))
    grad_x, gw, small = _local_step(x[0], loss_target[0], gains, lower_bounds, conv_full, wg8)

    core = cc.reshape(1).astype(jnp.int32)
    by_owner = [gw[n] for n in BIG]
    got = _sibling_swap(by_owner)
    chip_sums = [_pair_sum("pair_sum_" + n, a, b, core) for n, a, b in zip(BIG, by_owner, got)]
    parts = _chip_exchange(chip_sums)
    outs = {n: _adamw_sum("adamw_" + n, big[n], p, big_m[n], big_v[n]) for n, p in zip(BIG, parts)}

    small_all = _all_gather("gather_small", [_pack_small(small, small["conv_w"].reshape(12, 128))])
    ssum = _small_sum(small_all[0])
    conv_g_full = ssum[_CONV_ROW:_CONV_ROW + 12].reshape(3, CONV_WIDTH)
    conv_g = lax.dynamic_slice_in_dim(conv_g_full, my_dev * 64, 64, axis=1)
    loss = ssum[_LOSS_ROW, 0]
    g_rows = jnp.concatenate([ssum[:_CONV_ROW], _pad_rows(_conv_shard_rows(conv_g), SMALL_ROWS - _CONV_ROW)], axis=0)

    def pack_state(a):
        vals = dict(norm_mix_g=a[0], norm_ffn_g=a[1], norm_final_g=a[2], lower_bounds=a[3], hg_norm_g=a[4])
        return _pack_small(vals, _conv_shard_rows(a[5][0]))

    sw = pack_state((norm_mix_g, norm_ffn_g, norm_final_g, lower_bounds, hg_norm_g, conv_w))
    sm = pack_state((m_norm_mix_g, m_norm_ffn_g, m_norm_final_g, m_lower_bounds, m_hg_norm_g, m_conv_w))
    sv = pack_state((v_norm_mix_g, v_norm_ffn_g, v_norm_final_g, v_lower_bounds, v_hg_norm_g, v_conv_w))
    s_delta, s_m, s_v = _small_adamw(sw, g_rows, sm, sv)

    shapes = dict(norm_mix_g=(1, D_MODEL), norm_ffn_g=(1, D_MODEL), norm_final_g=(D_MODEL,),
                  lower_bounds=(2, HG_WIDTH), hg_norm_g=(1, HEAD_DIM))

    def unpack(buf, name):
        if name == "conv_w":
            return buf[_CONV_ROW:_CONV_ROW + 3, :64].reshape(1, 3, 64)
        for nm, off, rows in _SMALL_LAYOUT:
            if nm == name:
                return buf[off:off + rows].reshape(shapes[name])
        raise KeyError(name)

    order = ["norm_mix_g", "w_in", "lower_bounds", "hg_norm_g", "conv_w", "w_branch_a", "w_branch_b", "w_out",
             "norm_ffn_g", "w_ffn_gate", "w_ffn_up", "w_ffn_down", "norm_final_g"]
    result = [loss, grad_x[None]]
    for k, sbuf in enumerate((g_rows, s_delta, s_m, s_v)):
        for n in order:
            if n in outs:
                result.append(outs[n][k][None])
            else:
                result.append(unpack(sbuf, n))
    return tuple(result)
```

```python
import functools

import jax
import jax.numpy as jnp
from jax import lax
from jax.experimental import pallas as pl
from jax.experimental.pallas import tpu as pltpu
from jax.experimental.pallas import tpu_sc as plsc

F32 = jnp.float32
BF16 = jnp.bfloat16

D_MODEL = 1024
HG_WIDTH = 512
HEAD_DIM = 128
N_HEADS = 4
CONV_WIDTH = 512
D_FF = 2816
CHUNK = 32
EPS = 1e-6
Q_SCALE = HEAD_DIM ** -0.5
N_DEV = 8

ADAM_LR = 0.001
ADAM_B1 = 0.9
ADAM_B2 = 0.999
ADAM_EPS = 1e-08
ADAM_WD = 0.01
ADAM_STEP = 10

VMEM_LIMIT_V7X = 56 * 1024 * 1024

PACK_SPLITS = (("w_in", 704), ("w_branch_a", 64), ("w_branch_b", 64), ("w_out", 128),
               ("w_ffn_gate", 352), ("w_ffn_up", 352), ("w_ffn_down", 352))
PACK_ROWS = sum(r for _, r in PACK_SPLITS)
PACK_OFF = {}
_o = 0
for _n, _r in PACK_SPLITS:
    PACK_OFF[_n] = (_o, _r)
    _o += _r

SMALL_ROWS = 64


def _params(sem, vmem=VMEM_LIMIT_V7X):
    return pltpu.CompilerParams(dimension_semantics=sem, vmem_limit_bytes=vmem)


def _mm(a, b):
    return jnp.dot(a.astype(BF16), b.astype(BF16), preferred_element_type=F32)


def _mm_nt(a, b):
    return lax.dot_general(a.astype(BF16), b.astype(BF16), (((1,), (1,)), ((), ())), preferred_element_type=F32)


def _mm_tn(a, b):
    return lax.dot_general(a.astype(BF16), b.astype(BF16), (((0,), (0,)), ((), ())), preferred_element_type=F32)


def _sigmoid(x):
    return 1.0 / (1.0 + jnp.exp(-x))


def _resident(shape):
    nd = len(shape)
    return pl.BlockSpec(shape, lambda *_: (0,) * nd, pipeline_mode=pl.Buffered(1))


def _full(shape):
    nd = len(shape)
    return pl.BlockSpec(shape, lambda *_: (0,) * nd)


def _shard_cols(w_ref):
    return jnp.concatenate([w_ref[s] for s in range(N_DEV)], axis=1)


N_HG = 4 * HG_WIDTH
N_CV = 3 * CONV_WIDTH
N_GT = 2 * D_MODEL
N_IN = N_HG + N_CV + N_GT


def _col(tm, n):
    return pl.BlockSpec((n, tm), lambda i: (0, i))


def _fwd_in(x, g, w_in):
    T = x.shape[0]
    tm = min(256, T)

    def body(x_ref, g_ref, w_ref, ht_ref, hg_ref, cv_ref, gt_ref):
        xv = x_ref[...]
        r = lax.rsqrt(jnp.mean(xv * xv, axis=-1, keepdims=True) + EPS)
        hf = xv * r * g_ref[...]
        h = hf.astype(BF16)
        ht_ref[...] = hf.T.astype(BF16)
        hg_ref[...] = jnp.dot(h, w_ref[:, :N_HG], preferred_element_type=F32)
        cv_ref[...] = jnp.dot(h, w_ref[:, N_HG:N_HG + N_CV], preferred_element_type=F32)
        gt_ref[...] = jnp.dot(h, w_ref[:, N_HG + N_CV:], preferred_element_type=F32)

    row = lambda n: pl.BlockSpec((tm, n), lambda i: (i, 0))
    return pl.pallas_call(
        body, name="fwd_in", grid=(T // tm,),
        in_specs=[row(D_MODEL), _full((1, D_MODEL)), _resident(w_in.shape)],
        out_specs=[_col(tm, D_MODEL), row(N_HG), row(N_CV), row(N_GT)],
        out_shape=[jax.ShapeDtypeStruct((D_MODEL, T), BF16), jax.ShapeDtypeStruct((T, N_HG), F32),
                   jax.ShapeDtypeStruct((T, N_CV), F32), jax.ShapeDtypeStruct((T, N_GT), F32)],
        compiler_params=_params(("parallel",)),
    )(x, g, w_in)


def _chunk_pos(shape):
    return lax.broadcasted_iota(jnp.int32, shape, 0) & (CHUNK - 1)


def _chunk_cumsum(x, pos):
    s = 1
    while s < CHUNK:
        x = x + jnp.where(pos >= s, pltpu.roll(x, s, 0), 0.0)
        s *= 2
    return x


def _chunk_rev_cumsum(x, pos):
    n = x.shape[0]
    s = 1
    while s < CHUNK:
        x = x + jnp.where(pos + s < CHUNK, pltpu.roll(x, n - s, 0), 0.0)
        s *= 2
    return x


def _chunk_bcast(x3, row, tb):
    return jnp.broadcast_to(x3[:, row:row + 1, :], x3.shape).reshape(tb, x3.shape[-1])


def _lower_bound(low_ref):
    l0 = low_ref[0:1, :]
    l1 = low_ref[1:2, :]
    m = jnp.maximum(l0, l1)
    e0 = jnp.exp(l0 - m)
    e1 = jnp.exp(l1 - m)
    return e0 / (e0 + e1), e1 / (e0 + e1)


def _hg_gates(qr, fr, lb, pos, tb):
    sq = _sigmoid(qr)
    q = qr * sq * Q_SCALE
    sg = _sigmoid(fr)
    f = lb + (1.0 - lb) * sg
    k = 1.0 - f
    b = _chunk_cumsum(jnp.log(f), pos)
    b3 = b.reshape(tb // CHUNK, CHUNK, HEAD_DIM)
    anc = _chunk_bcast(b3, CHUNK // 2 - 1, tb)
    blb = _chunk_bcast(b3, CHUNK - 1, tb)
    e_qa = jnp.exp(b - anc)
    e_ka = jnp.exp(anc - b)
    e_b = jnp.exp(b)
    e_ko = jnp.exp(blb - b)
    dec = jnp.exp(blb)
    return sq, q, sg, f, k, e_qa, e_ka, e_b, e_ko, dec


def _intra_mask(sb):
    r = lax.broadcasted_iota(jnp.int32, (sb, sb), 0)
    c = lax.broadcasted_iota(jnp.int32, (sb, sb), 1)
    return ((r // CHUNK) == (c // CHUNK)) & (c <= r)


def _hg_fwd(hg, low, gn):
    T = hg.shape[0]
    tb = min(512, T)
    sb = min(256, tb)
    nb = T // tb
    nc = tb // CHUNK

    def body(q_ref, f_ref, i_ref, g_ref, low_ref, gn_ref, o_ref, og_ref, ogt_ref, st_ref, s_scr):
        t = pl.program_id(1)

        @pl.when(t == 0)
        def _():
            s_scr[...] = jnp.zeros_like(s_scr)

        pos = _chunk_pos((tb, HEAD_DIM))
        lb, _ = _lower_bound(low_ref)
        v = i_ref[...]
        _, q, _, _, k, e_qa, e_ka, e_b, e_ko, dec = _hg_gates(q_ref[...], f_ref[...], lb, pos, tb)
        qh = (q * e_qa).astype(BF16)
        kh = (k * e_ka).astype(BF16)
        qi = (q * e_b).astype(BF16)
        ko = (k * e_ko).astype(BF16)
        vb = v.astype(BF16)
        mask = _intra_mask(sb)
        for s in range(tb // sb):
            sl = slice(s * sb, (s + 1) * sb)
            p = jnp.where(mask, _mm_nt(qh[sl], kh[sl]), 0.0)
            o_ref[sl, :] = _mm(p, vb[sl])
        st = s_scr[...]
        for c in range(nc):
            sl = slice(c * CHUNK, (c + 1) * CHUNK)
            st_ref[c] = st
            o_ref[sl, :] = o_ref[sl, :] + _mm_nt(qi[sl], st)
            st = dec[c * CHUNK:c * CHUNK + 1, :] * st + _mm_tn(vb[sl], ko[sl])
        s_scr[...] = st
        o = o_ref[...]
        r = lax.rsqrt(jnp.mean(o * o, axis=-1, keepdims=True) + EPS)
        gr = g_ref[...]
        og = (o * r * gn_ref[...]) * (gr * _sigmoid(gr))
        og_ref[...] = og.astype(BF16)
        ogt_ref[...] = og.T.astype(BF16)

    col = lambda p: pl.BlockSpec((tb, HEAD_DIM), lambda h, t: (t, p * N_HEADS + h))
    hcol = pl.BlockSpec((tb, HEAD_DIM), lambda h, t: (t, h))
    return pl.pallas_call(
        body, name="hg_fwd", grid=(N_HEADS, nb),
        in_specs=[col(0), col(1), col(2), col(3), pl.BlockSpec((2, HEAD_DIM), lambda h, t: (0, h)),
                  pl.BlockSpec((1, HEAD_DIM), lambda h, t: (0, 0))],
        out_specs=[hcol, hcol, pl.BlockSpec((HEAD_DIM, tb), lambda h, t: (h, t)),
                   pl.BlockSpec((None, nc, HEAD_DIM, HEAD_DIM), lambda h, t: (h, t, 0, 0))],
        out_shape=[jax.ShapeDtypeStruct((T, HG_WIDTH), F32), jax.ShapeDtypeStruct((T, HG_WIDTH), BF16),
                   jax.ShapeDtypeStruct((HG_WIDTH, T), BF16),
                   jax.ShapeDtypeStruct((N_HEADS, T // CHUNK, HEAD_DIM, HEAD_DIM), F32)],
        scratch_shapes=[pltpu.VMEM((HEAD_DIM, HEAD_DIM), F32)],
        compiler_params=_params(("parallel", "arbitrary")),
    )(hg, hg, hg, hg, low, gn)


def _conv_fwd(cv, conv_w):
    T = cv.shape[0]
    nj = CONV_WIDTH // 128

    def body(c_ref, b_ref, x_ref, w_ref, o_ref, ot_ref):
        row = lax.broadcasted_iota(jnp.int32, (T, 128), 0)
        u = c_ref[...] * x_ref[...]
        u1 = jnp.where(row >= 1, pltpu.roll(u, 1, 0), 0.0)
        u2 = jnp.where(row >= 2, pltpu.roll(u, 2, 0), 0.0)
        y = w_ref[0:1, :] * u2 + w_ref[1:2, :] * u1 + w_ref[2:3, :] * u
        out = b_ref[...] * y
        o_ref[...] = out.astype(BF16)
        ot_ref[...] = out.T.astype(BF16)

    col = lambda p: pl.BlockSpec((T, 128), lambda j: (0, p * nj + j))
    return pl.pallas_call(
        body, name="conv_fwd", grid=(nj,),
        in_specs=[col(0), col(1), col(2), pl.BlockSpec((3, 128), lambda j: (0, j))],
        out_specs=[pl.BlockSpec((T, 128), lambda j: (0, j)), pl.BlockSpec((128, T), lambda j: (j, 0))],
        out_shape=[jax.ShapeDtypeStruct((T, CONV_WIDTH), BF16), jax.ShapeDtypeStruct((CONV_WIDTH, T), BF16)],
        compiler_params=_params(("parallel",)),
    )(cv, cv, cv, conv_w)


def _merge_fwd(og, cvo, gt, x, wa, wb, wo):
    T = x.shape[0]
    tm = min(512, T)

    def body(og_ref, cvo_ref, gt_ref, x_ref, wa_ref, wb_ref, wo_ref, x1_ref, mgt_ref):
        ya = jnp.dot(og_ref[...], _shard_cols(wa_ref), preferred_element_type=F32)
        yb = jnp.dot(cvo_ref[...], _shard_cols(wb_ref), preferred_element_type=F32)
        m = _sigmoid(gt_ref[:, :D_MODEL]) * ya + _sigmoid(gt_ref[:, D_MODEL:]) * yb
        mgt_ref[...] = m.T.astype(BF16)
        x1_ref[...] = x_ref[...] + jnp.dot(m.astype(BF16), wo_ref[...], preferred_element_type=F32)

    row = lambda n: pl.BlockSpec((tm, n), lambda i: (i, 0))
    return pl.pallas_call(
        body, name="merge_fwd", grid=(T // tm,),
        in_specs=[row(HG_WIDTH), row(CONV_WIDTH), row(2 * D_MODEL), row(D_MODEL),
                  _resident(wa.shape), _resident(wb.shape), _resident(wo.shape)],
        out_specs=[row(D_MODEL), _col(tm, D_MODEL)],
        out_shape=[jax.ShapeDtypeStruct((T, D_MODEL), F32), jax.ShapeDtypeStruct((D_MODEL, T), BF16)],
        compiler_params=_params(("parallel",)),
    )(og, cvo, gt, x, wa, wb, wo)


def _ffn_fwd(x1, g, wg, wu, wd):
    T = x1.shape[0]
    tm = min(256, T)

    def body(x_ref, g_ref, wg_ref, wu_ref, wd_ref, ht_ref, gate_ref, up_ref, actt_ref, x2_ref):
        xv = x_ref[...]
        r = lax.rsqrt(jnp.mean(xv * xv, axis=-1, keepdims=True) + EPS)
        hf = xv * r * g_ref[...]
        h = hf.astype(BF16)
        ht_ref[...] = hf.T.astype(BF16)
        gate = jnp.dot(h, wg_ref[...], preferred_element_type=F32)
        up = jnp.dot(h, wu_ref[...], preferred_element_type=F32)
        gate_ref[...] = gate
        up_ref[...] = up
        act = gate * _sigmoid(gate) * up
        actt_ref[...] = act.T.astype(BF16)
        x2_ref[...] = xv + jnp.dot(act.astype(BF16), wd_ref[...], preferred_element_type=F32)

    row = lambda n: pl.BlockSpec((tm, n), lambda i: (i, 0))
    return pl.pallas_call(
        body, name="ffn_fwd", grid=(T // tm,),
        in_specs=[row(D_MODEL), _full((1, D_MODEL)), _resident(wg.shape), _resident(wu.shape), _resident(wd.shape)],
        out_specs=[_col(tm, D_MODEL), row(D_FF), row(D_FF), _col(tm, D_FF), row(D_MODEL)],
        out_shape=[jax.ShapeDtypeStruct((D_MODEL, T), BF16), jax.ShapeDtypeStruct((T, D_FF), F32),
                   jax.ShapeDtypeStruct((T, D_FF), F32), jax.ShapeDtypeStruct((D_FF, T), BF16),
                   jax.ShapeDtypeStruct((T, D_MODEL), F32)],
        compiler_params=_params(("parallel",)),
    )(x1, g, wg, wu, wd)


def _final_fwd_bwd(x2, target, g):
    T = x2.shape[0]
    tm = min(512, T)

    def body(x_ref, t_ref, g_ref, loss_ref, dg_ref, dx_ref):
        @pl.when(pl.program_id(0) == 0)
        def _():
            loss_ref[...] = jnp.zeros_like(loss_ref)
            dg_ref[...] = jnp.zeros_like(dg_ref)

        xv = x_ref[...]
        gv = g_ref[...]
        r = lax.rsqrt(jnp.mean(xv * xv, axis=-1, keepdims=True) + EPS)
        xh = xv * r
        err = xh * gv - t_ref[...]
        loss_ref[...] += 0.5 * jnp.sum(jnp.mean(err * err, axis=-1, keepdims=True), axis=0, keepdims=True)
        dy = err * (1.0 / D_MODEL)
        dg_ref[...] += jnp.sum(dy * xh, axis=0, keepdims=True)
        w = dy * gv
        dx_ref[...] = r * (w - xh * jnp.mean(w * xh, axis=-1, keepdims=True))

    row = pl.BlockSpec((tm, D_MODEL), lambda i: (i, 0))
    return pl.pallas_call(
        body, name="final_fwd_bwd", grid=(T // tm,),
        in_specs=[row, row, _full((1, D_MODEL))],
        out_specs=[_full((1, 128)), _full((1, D_MODEL)), row],
        out_shape=[jax.ShapeDtypeStruct((1, 128), F32), jax.ShapeDtypeStruct((1, D_MODEL), F32),
                   jax.ShapeDtypeStruct((T, D_MODEL), F32)],
        compiler_params=_params(("arbitrary",)),
    )(x2, target, g)


def _ffn_bwd(dx2, x1, gate, up, g, wg, wu, wd):
    T = x1.shape[0]
    tm = min(256, T)

    def body(dx2_ref, x_ref, gate_ref, up_ref, g_ref, wg_ref, wu_ref, wd_ref, dgate_ref, dup_ref, dx1_ref, dgn_ref):
        @pl.when(pl.program_id(0) == 0)
        def _():
            dgn_ref[...] = jnp.zeros_like(dgn_ref)

        dx2 = dx2_ref[...]
        dact = _mm_nt(dx2, wd_ref[...])
        gate = gate_ref[...]
        s = _sigmoid(gate)
        dgate = (dact * up_ref[...] * (s * (1.0 + gate * (1.0 - s)))).astype(BF16)
        dup = (dact * (gate * s)).astype(BF16)
        dgate_ref[...] = dgate
        dup_ref[...] = dup
        dh = _mm_nt(dgate, wg_ref[...]) + _mm_nt(dup, wu_ref[...])
        xv = x_ref[...]
        r = lax.rsqrt(jnp.mean(xv * xv, axis=-1, keepdims=True) + EPS)
        xh = xv * r
        dgn_ref[...] += jnp.sum(dh * xh, axis=0, keepdims=True)
        w = dh * g_ref[...]
        dx1_ref[...] = dx2 + r * (w - xh * jnp.mean(w * xh, axis=-1, keepdims=True))

    row = lambda n: pl.BlockSpec((tm, n), lambda i: (i, 0))
    return pl.pallas_call(
        body, name="ffn_bwd", grid=(T // tm,),
        in_specs=[row(D_MODEL), row(D_MODEL), row(D_FF), row(D_FF), _full((1, D_MODEL)),
                  _resident(wg.shape), _resident(wu.shape), _resident(wd.shape)],
        out_specs=[row(D_FF), row(D_FF), row(D_MODEL), _full((1, D_MODEL))],
        out_shape=[jax.ShapeDtypeStruct((T, D_FF), BF16), jax.ShapeDtypeStruct((T, D_FF), BF16),
                   jax.ShapeDtypeStruct((T, D_MODEL), F32), jax.ShapeDtypeStruct((1, D_MODEL), F32)],
        compiler_params=_params(("arbitrary",)),
    )(dx2, x1, gate, up, g, wg, wu, wd)


def _merge_bwd(dx1, og, cvo, gt, wa, wb, wo):
    T = dx1.shape[0]
    tm = min(512, T)

    def body(dx_ref, og_ref, cvo_ref, gt_ref, wa_ref, wb_ref, wo_ref, dgt_ref, dya_ref, dyb_ref, dog_ref, dcvo_ref):
        dm = _mm_nt(dx_ref[...], wo_ref[...])
        wa = _shard_cols(wa_ref)
        wb = _shard_cols(wb_ref)
        ya = jnp.dot(og_ref[...], wa, preferred_element_type=F32)
        yb = jnp.dot(cvo_ref[...], wb, preferred_element_type=F32)
        sa = _sigmoid(gt_ref[:, :D_MODEL])
        sb = _sigmoid(gt_ref[:, D_MODEL:])
        dgt_ref[:, :D_MODEL] = (dm * ya * (sa * (1.0 - sa))).astype(BF16)
        dgt_ref[:, D_MODEL:] = (dm * yb * (sb * (1.0 - sb))).astype(BF16)
        dya = (dm * sa).astype(BF16)
        dyb = (dm * sb).astype(BF16)
        dya_ref[...] = dya
        dyb_ref[...] = dyb
        dog_ref[...] = _mm_nt(dya, wa)
        dcvo_ref[...] = _mm_nt(dyb, wb)

    row = lambda n: pl.BlockSpec((tm, n), lambda i: (i, 0))
    return pl.pallas_call(
        body, name="merge_bwd", grid=(T // tm,),
        in_specs=[row(D_MODEL), row(HG_WIDTH), row(CONV_WIDTH), row(2 * D_MODEL),
                  _resident(wa.shape), _resident(wb.shape), _resident(wo.shape)],
        out_specs=[row(2 * D_MODEL), row(D_MODEL), row(D_MODEL), row(HG_WIDTH), row(CONV_WIDTH)],
        out_shape=[jax.ShapeDtypeStruct((T, 2 * D_MODEL), BF16), jax.ShapeDtypeStruct((T, D_MODEL), BF16),
                   jax.ShapeDtypeStruct((T, D_MODEL), BF16), jax.ShapeDtypeStruct((T, HG_WIDTH), F32),
                   jax.ShapeDtypeStruct((T, CONV_WIDTH), F32)],
        compiler_params=_params(("parallel",)),
    )(dx1, og, cvo, gt, wa, wb, wo)


def _conv_bwd(dcvo, cv, conv_w):
    T = cv.shape[0]
    nj = CONV_WIDTH // 128

    def body(do_ref, c_ref, b_ref, x_ref, w_ref, dc_ref, db_ref, dx_ref, dw_ref):
        row = lax.broadcasted_iota(jnp.int32, (T, 128), 0)
        c = c_ref[...]
        xb = x_ref[...]
        do = do_ref[...]
        u = c * xb
        u1 = jnp.where(row >= 1, pltpu.roll(u, 1, 0), 0.0)
        u2 = jnp.where(row >= 2, pltpu.roll(u, 2, 0), 0.0)
        w0, w1, w2 = w_ref[0:1, :], w_ref[1:2, :], w_ref[2:3, :]
        y = w0 * u2 + w1 * u1 + w2 * u
        db_ref[...] = (do * y).astype(BF16)
        dy = do * b_ref[...]
        dw_ref[0:1, :] = jnp.sum(dy * u2, axis=0, keepdims=True)
        dw_ref[1:2, :] = jnp.sum(dy * u1, axis=0, keepdims=True)
        dw_ref[2:3, :] = jnp.sum(dy * u, axis=0, keepdims=True)
        dy1 = jnp.where(row < T - 1, pltpu.roll(dy, T - 1, 0), 0.0)
        dy2 = jnp.where(row < T - 2, pltpu.roll(dy, T - 2, 0), 0.0)
        du = w2 * dy + w1 * dy1 + w0 * dy2
        dc_ref[...] = (du * xb).astype(BF16)
        dx_ref[...] = (du * c).astype(BF16)

    col = lambda p: pl.BlockSpec((T, 128), lambda j: (0, p * nj + j))
    one = pl.BlockSpec((T, 128), lambda j: (0, j))
    wspec = pl.BlockSpec((3, 128), lambda j: (0, j))
    out = jax.ShapeDtypeStruct((T, CONV_WIDTH), BF16)
    return pl.pallas_call(
        body, name="conv_bwd", grid=(nj,),
        in_specs=[one, col(0), col(1), col(2), wspec],
        out_specs=[one, one, one, wspec],
        out_shape=[out, out, out, jax.ShapeDtypeStruct((3, CONV_WIDTH), F32)],
        compiler_params=_params(("parallel",)),
    )(dcvo, cv, cv, cv, conv_w)


def _hg_bwd(dog, hg, o, st, low, gn):
    T = hg.shape[0]
    tb = min(512, T)
    sb = min(256, tb)
    nb = T // tb
    nc = tb // CHUNK

    def body(q_ref, f_ref, i_ref, g_ref, low_ref, gn_ref, o_ref, dog_ref, st_ref,
             dq_ref, df_ref, di_ref, dg_ref, dlow_ref, dgn_ref,
             ds_scr, dqi_scr, dko_scr, dv_scr, dd_scr, dqh_scr, dkh_scr):
        h = pl.program_id(0)
        t = pl.program_id(1)

        @pl.when(t == 0)
        def _():
            ds_scr[...] = jnp.zeros_like(ds_scr)
            dlow_ref[...] = jnp.zeros_like(dlow_ref)

        @pl.when((t == 0) & (h == 0))
        def _():
            dgn_ref[...] = jnp.zeros_like(dgn_ref)

        pos = _chunk_pos((tb, HEAD_DIM))
        lb, lb1 = _lower_bound(low_ref)
        qr = q_ref[...]
        v = i_ref[...]
        sq, q, sg, f, k, e_qa, e_ka, e_b, e_ko, dec = _hg_gates(qr, f_ref[...], lb, pos, tb)

        gr = g_ref[...]
        gnv = gn_ref[...]
        o = o_ref[...]
        dog_v = dog_ref[...]
        sgr = _sigmoid(gr)
        r = lax.rsqrt(jnp.mean(o * o, axis=-1, keepdims=True) + EPS)
        oh = o * r
        dg_ref[...] = (dog_v * (oh * gnv) * (sgr * (1.0 + gr * (1.0 - sgr)))).astype(BF16)
        don = dog_v * (gr * sgr)
        dgn_ref[...] += jnp.sum(don * oh, axis=0, keepdims=True)
        w = don * gnv
        do = (r * (w - oh * jnp.mean(w * oh, axis=-1, keepdims=True))).astype(BF16)

        qh = (q * e_qa).astype(BF16)
        kh = (k * e_ka).astype(BF16)
        qi = (q * e_b).astype(BF16)
        ko = (k * e_ko).astype(BF16)
        vb = v.astype(BF16)

        mask = _intra_mask(sb)
        for s in range(tb // sb):
            sl = slice(s * sb, (s + 1) * sb)
            p = jnp.where(mask, _mm_nt(qh[sl], kh[sl]), 0.0).astype(BF16)
            dp = jnp.where(mask, _mm_nt(do[sl], vb[sl]), 0.0).astype(BF16)
            dv_scr[sl, :] = _mm_tn(p, do[sl])
            dqh_scr[sl, :] = _mm(dp, kh[sl])
            dkh_scr[sl, :] = _mm_tn(dp, qh[sl])

        ds = ds_scr[...]
        for c in reversed(range(nc)):
            sl = slice(c * CHUNK, (c + 1) * CHUNK)
            st_c = st_ref[c]
            dqi_scr[sl, :] = _mm(do[sl], st_c)
            dko_scr[sl, :] = _mm(vb[sl], ds)
            dv_scr[sl, :] = dv_scr[sl, :] + _mm_nt(ko[sl], ds)
            dd_scr[sl, :] = jnp.broadcast_to(jnp.sum(ds * st_c, axis=0, keepdims=True), (CHUNK, HEAD_DIM))
            ds = dec[c * CHUNK:c * CHUNK + 1, :] * ds + _mm_tn(do[sl], qi[sl])
        ds_scr[...] = ds

        dko_e = dko_scr[...] * e_ko
        dq = dqh_scr[...] * e_qa + dqi_scr[...] * e_b
        dk = dkh_scr[...] * e_ka + dko_e
        kd3 = (k * dko_e).reshape(nc, CHUNK, HEAD_DIM)
        last = jnp.broadcast_to(jnp.sum(kd3, axis=1, keepdims=True), kd3.shape).reshape(tb, HEAD_DIM)
        db = q * dq - k * dk + jnp.where(pos == CHUNK - 1, dec * dd_scr[...] + last, 0.0)
        dlg = _chunk_rev_cumsum(db, pos)
        dfv = dlg / f - dk
        s_low = jnp.sum(dfv * (1.0 - sg), axis=0, keepdims=True)
        dlow_ref[0:1, :] += s_low * lb * (1.0 - lb)
        dlow_ref[1:2, :] += -s_low * lb * lb1
        df_ref[...] = (dfv * (1.0 - lb) * sg * (1.0 - sg)).astype(BF16)
        dq_ref[...] = (dq * Q_SCALE * (sq * (1.0 + qr * (1.0 - sq)))).astype(BF16)
        di_ref[...] = dv_scr[...].astype(BF16)

    rt = lambda t: nb - 1 - t
    col = lambda p: pl.BlockSpec((tb, HEAD_DIM), lambda h, t: (rt(t), p * N_HEADS + h))
    hcol = pl.BlockSpec((tb, HEAD_DIM), lambda h, t: (rt(t), h))
    piece = jax.ShapeDtypeStruct((T, HG_WIDTH), BF16)
    tile = pltpu.VMEM((tb, HEAD_DIM), F32)
    return pl.pallas_call(
        body, name="hg_bwd", grid=(N_HEADS, nb),
        in_specs=[col(0), col(1), col(2), col(3), pl.BlockSpec((2, HEAD_DIM), lambda h, t: (0, h)),
                  pl.BlockSpec((1, HEAD_DIM), lambda h, t: (0, 0)), hcol, hcol,
                  pl.BlockSpec((None, nc, HEAD_DIM, HEAD_DIM), lambda h, t: (h, rt(t), 0, 0))],
        out_specs=[hcol, hcol, hcol, hcol, pl.BlockSpec((2, HEAD_DIM), lambda h, t: (0, h)),
                   pl.BlockSpec((1, HEAD_DIM), lambda h, t: (0, 0))],
        out_shape=[piece, piece, piece, piece, jax.ShapeDtypeStruct((2, HG_WIDTH), F32),
                   jax.ShapeDtypeStruct((1, HEAD_DIM), F32)],
        scratch_shapes=[pltpu.VMEM((HEAD_DIM, HEAD_DIM), F32), tile, tile, tile, tile, tile, tile],
        compiler_params=_params(("arbitrary", "arbitrary")),
    )(hg, hg, hg, hg, low, gn, o, dog, st)


def _in_bwd(dparts, w_in, x, dx1, g):
    T = x.shape[0]
    tm = min(256, T)
    widths = [p.shape[1] for p in dparts]
    offs = [sum(widths[:i]) for i in range(len(widths))]
    n = len(dparts)

    def body(*refs):
        d_refs = refs[:n]
        w_ref, x_ref, dx1_ref, g_ref, dx_ref, dgn_ref = refs[n:]

        @pl.when(pl.program_id(0) == 0)
        def _():
            dgn_ref[...] = jnp.zeros_like(dgn_ref)

        dh = None
        for d_ref, off, wd in zip(d_refs, offs, widths):
            part = _mm_nt(d_ref[...], w_ref[:, off:off + wd])
            dh = part if dh is None else dh + part
        xv = x_ref[...]
        r = lax.rsqrt(jnp.mean(xv * xv, axis=-1, keepdims=True) + EPS)
        xh = xv * r
        dgn_ref[...] += jnp.sum(dh * xh, axis=0, keepdims=True)
        w = dh * g_ref[...]
        dx_ref[...] = dx1_ref[...] + r * (w - xh * jnp.mean(w * xh, axis=-1, keepdims=True))

    row = lambda m: pl.BlockSpec((tm, m), lambda i: (i, 0))
    return pl.pallas_call(
        body, name="in_bwd", grid=(T // tm,),
        in_specs=[row(wd) for wd in widths] + [_resident(w_in.shape), row(D_MODEL), row(D_MODEL), _full((1, D_MODEL))],
        out_specs=[row(D_MODEL), _full((1, D_MODEL))],
        out_shape=[jax.ShapeDtypeStruct((T, D_MODEL), F32), jax.ShapeDtypeStruct((1, D_MODEL), F32)],
        compiler_params=_params(("arbitrary",)),
    )(*dparts, w_in, x, dx1, g)


def _wgrad(name, at, b, tn):
    M, T = at.shape
    N = b.shape[1]
    tk = min(1024, T)
    nk = T // tk

    def body(a_ref, b_ref, o_ref, acc):
        k = pl.program_id(1)
        part = _mm(a_ref[...], b_ref[...])

        @pl.when(k == 0)
        def _():
            acc[...] = part

        @pl.when(k != 0)
        def _():
            acc[...] += part

        @pl.when(k == nk - 1)
        def _():
            o_ref[...] = acc[...].astype(BF16)

    return pl.pallas_call(
        body, name=name, grid=(N // tn, nk),
        in_specs=[pl.BlockSpec((M, tk), lambda j, k: (0, k)), pl.BlockSpec((tk, tn), lambda j, k: (k, j))],
        out_specs=pl.BlockSpec((M, tn), lambda j, k: (0, j)),
        out_shape=jax.ShapeDtypeStruct((M, N), BF16),
        scratch_shapes=[pltpu.VMEM((M, tn), F32)],
        compiler_params=_params(("parallel", "arbitrary")),
    )(at, b)


def _wgrad_in(ht, dparts):
    M, T = ht.shape
    tn = 512
    tk = min(1024, T)
    nk = T // tk
    nblk = [p.shape[1] // tn for p in dparts]
    start = [sum(nblk[:i]) for i in range(len(nblk))]
    n = len(dparts)

    def body(a_ref, *refs):
        d_refs, o_ref, acc = refs[:n], refs[n], refs[n + 1]
        j = pl.program_id(0)
        k = pl.program_id(1)

        @pl.when(k == 0)
        def _():
            acc[...] = jnp.zeros_like(acc)

        for d_ref, s, nb in zip(d_refs, start, nblk):
            @pl.when((j >= s) & (j < s + nb))
            def _():
                acc[...] += _mm(a_ref[...], d_ref[...])

        @pl.when(k == nk - 1)
        def _():
            o_ref[...] = acc[...].astype(BF16)

    def piece_spec(s, nb):
        def index(j, k):
            inside = (j >= s) & (j < s + nb)
            return jnp.where(inside, k, 0), jnp.clip(j - s, 0, nb - 1)
        return pl.BlockSpec((tk, tn), index)

    return pl.pallas_call(
        body, name="wgrad_in", grid=(sum(nblk), nk),
        in_specs=[pl.BlockSpec((M, tk), lambda j, k: (0, k))] + [piece_spec(s, nb) for s, nb in zip(start, nblk)],
        out_specs=pl.BlockSpec((M, tn), lambda j, k: (0, j)),
        out_shape=jax.ShapeDtypeStruct((M, sum(nblk) * tn), BF16),
        scratch_shapes=[pltpu.VMEM((M, tn), F32)],
        compiler_params=_params(("parallel", "arbitrary")),
    )(ht, *dparts)


def _adamw_math(w, g, m, v):
    m = ADAM_B1 * m + (1.0 - ADAM_B1) * g
    v = ADAM_B2 * v + (1.0 - ADAM_B2) * (g * g)
    m_hat = m / (1.0 - ADAM_B1 ** ADAM_STEP)
    v_hat = v / (1.0 - ADAM_B2 ** ADAM_STEP)
    delta = -ADAM_LR * (m_hat / (jnp.sqrt(v_hat) + ADAM_EPS) + ADAM_WD * w)
    return delta, m, v


def _adamw_sum(name, w, parts, m, v):
    R, C = w.shape
    tr = R
    for cand in (256, 128, 64, 32, 16, 8):
        if R % cand == 0:
            tr = cand
            break

    def body(w_ref, p_ref, m_ref, v_ref, g_out, d_out, m_out, v_out):
        g = p_ref[0].astype(F32)
        for k in range(1, 4):
            g = g + p_ref[k].astype(F32)
        g_out[...] = g
        d_out[...], m_out[...], v_out[...] = _adamw_math(w_ref[...], g, m_ref[...], v_ref[...])

    blk = pl.BlockSpec((tr, C), lambda i: (i, 0))
    out = jax.ShapeDtypeStruct((R, C), F32)
    return pl.pallas_call(
        body, name=name, grid=(R // tr,),
        in_specs=[blk, pl.BlockSpec((4, tr, C), lambda i: (0, i, 0)), blk, blk],
        out_specs=[blk, blk, blk, blk], out_shape=[out, out, out, out],
        compiler_params=_params(("parallel",)),
    )(w, parts, m, v)


def _small_sum(gathered):
    R = gathered.shape[1]

    def body(p_ref, o_ref):
        g = p_ref[0]
        for k in range(1, N_DEV):
            g = g + p_ref[k]
        o_ref[...] = g

    return pl.pallas_call(
        body, name="small_sum", in_specs=[_full(gathered.shape)], out_specs=_full((R, 128)), grid=(1,),
        out_shape=jax.ShapeDtypeStruct((R, 128), F32),
    )(gathered)


def _small_adamw(w, g, m, v):
    def body(w_ref, g_ref, m_ref, v_ref, d_out, m_out, v_out):
        d_out[...], m_out[...], v_out[...] = _adamw_math(w_ref[...], g_ref[...], m_ref[...], v_ref[...])

    out = jax.ShapeDtypeStruct(w.shape, F32)
    spec = _full(w.shape)
    return pl.pallas_call(
        body, name="small_adamw", grid=(1,), in_specs=[spec] * 4, out_specs=[spec] * 3, out_shape=[out, out, out],
    )(w, g, m, v)


def _row_tile(rows):
    for cand in (256, 128, 64, 32, 16):
        if rows % cand == 0:
            return cand
    return rows


def _pair_sum(name, by_owner, got, core):
    _, R, C = got.shape
    tr = _row_tile(R)

    def body(core_ref, a_ref, b_ref, o_ref):
        o_ref[...] = (a_ref[...].astype(F32) + b_ref[...].astype(F32)).astype(BF16)

    blk = pl.BlockSpec((None, tr, C), lambda k, i, core_ref: (k, i, 0))
    mine = pl.BlockSpec((None, tr, C), lambda k, i, core_ref: (2 * k + core_ref[0], i, 0))
    return pl.pallas_call(
        body, name=name,
        grid_spec=pltpu.PrefetchScalarGridSpec(num_scalar_prefetch=1, grid=(4, R // tr), in_specs=[mine, blk],
                                               out_specs=blk),
        out_shape=jax.ShapeDtypeStruct(got.shape, BF16),
        compiler_params=_params(("parallel", "parallel")),
    )(core, by_owner, got)


def _cols_from_shards(name, g):
    _, R, c = g.shape
    tr = _row_tile(R)

    def body(g_ref, o_ref):
        for s in range(N_DEV):
            o_ref[:, s * c:(s + 1) * c] = g_ref[s]

    return pl.pallas_call(
        body, name=name, grid=(R // tr,),
        in_specs=[pl.BlockSpec((N_DEV, tr, c), lambda i: (0, i, 0))],
        out_specs=pl.BlockSpec((tr, N_DEV * c), lambda i: (i, 0)),
        out_shape=jax.ShapeDtypeStruct((R, N_DEV * c), g.dtype),
        compiler_params=_params(("parallel",)),
    )(g)


def _shards_from_cols(name, full):
    R, allc = full.shape
    c = allc // N_DEV
    tr = _row_tile(R)

    def body(f_ref, o_ref):
        for s in range(N_DEV):
            o_ref[s] = f_ref[:, s * c:(s + 1) * c]

    return pl.pallas_call(
        body, name=name, grid=(R // tr,),
        in_specs=[pl.BlockSpec((tr, allc), lambda i: (i, 0))],
        out_specs=pl.BlockSpec((N_DEV, tr, c), lambda i: (0, i, 0)),
        out_shape=jax.ShapeDtypeStruct((N_DEV, R, c), full.dtype),
        compiler_params=_params(("parallel",)),
    )(full)


MESH = pl.DeviceIdType.MESH
HBM_SPEC = pl.BlockSpec(memory_space=pl.ANY)


def _handshake(peers):
    barrier = pltpu.get_barrier_semaphore()
    for peer in peers:
        pl.semaphore_signal(barrier, inc=1, device_id=peer, device_id_type=MESH)
    pl.semaphore_wait(barrier, len(peers))


def _comm_call(body, name, operands, out_shape, scratch, collective_id):
    if collective_id is None:
        return pl.pallas_call(body, name=name, in_specs=[HBM_SPEC] * len(operands), out_specs=[HBM_SPEC] * len(out_shape),
                              out_shape=out_shape, scratch_shapes=scratch)(*operands)
    return pl.kernel(body, out_type=out_shape, mesh=plsc.ScalarSubcoreMesh(axis_name="sequencer", num_cores=1),
                     scratch_types=scratch, name=name,
                     compiler_params=pltpu.CompilerParams(collective_id=collective_id))(*operands)


def _all_gather(name, blocks, collective_id=None):
    n = len(blocks)

    def body(*refs):
        x_refs, out_refs = refs[:n], refs[n:2 * n]
        send_sems, recv_sems, local_sems = refs[2 * n:]
        x, y, c = lax.axis_index("x"), lax.axis_index("y"), lax.axis_index("c")
        me, sibling = (x, y, c), (x, y, 1 - c)
        chips = [(1 - x, y), (x, 1 - y), (1 - x, 1 - y)]
        if collective_id is not None:
            _handshake([sibling] + [(*chip, c) for chip in chips])

        def slot(i, px, py, pc):
            return out_refs[i].at[4 * px + 2 * py + pc]

        def copy(i, k, blk, to, src=None):
            return pltpu.make_async_remote_copy(
                src_ref=slot(i, *blk) if src is None else src, dst_ref=slot(i, *blk),
                send_sem=send_sems.at[7 * i + k], recv_sem=recv_sems.at[7 * i + k], device_id=to, device_id_type=MESH)

        mine = [pltpu.make_async_copy(x_refs[i], slot(i, *me), local_sems.at[i]) for i in range(n)]
        for cp in mine:
            cp.start()
        first = []
        for i in range(n):
            first.append(copy(i, 0, me, sibling, src=x_refs[i]))
            first += [copy(i, 1 + j, me, (*chip, c), src=x_refs[i]) for j, chip in enumerate(chips)]
        for cp in first:
            cp.start()
        passed = []
        for i in range(n):
            for j, chip in enumerate(chips):
                copy(i, 1 + j, (*chip, c), me).wait_recv()
                passed.append(copy(i, 4 + j, (*chip, c), sibling))
                passed[-1].start()
        for i in range(n):
            copy(i, 0, sibling, me).wait_recv()
            for j, chip in enumerate(chips):
                copy(i, 4 + j, (*chip, 1 - c), me).wait_recv()
        for cp in first + passed:
            cp.wait_send()
        for cp in mine:
            cp.wait()

    return _comm_call(
        body, name, blocks, [jax.ShapeDtypeStruct((N_DEV,) + b.shape, b.dtype) for b in blocks],
        [pltpu.SemaphoreType.DMA((7 * n,)), pltpu.SemaphoreType.DMA((7 * n,)), pltpu.SemaphoreType.DMA((n,))],
        collective_id)


def _sibling_swap(name, by_owner, collective_id=None):
    n = len(by_owner)

    def body(*refs):
        x_refs, out_refs = refs[:n], refs[n:2 * n]
        send_sems, recv_sems = refs[2 * n:]
        x, y, c = lax.axis_index("x"), lax.axis_index("y"), lax.axis_index("c")
        if collective_id is not None:
            _handshake([(x, y, 1 - c)])
        copies = []
        for i in range(n):
            for k in range(4):
                copies.append(pltpu.make_async_remote_copy(
                    src_ref=x_refs[i].at[2 * k + 1 - c], dst_ref=out_refs[i].at[k],
                    send_sem=send_sems.at[4 * i + k], recv_sem=recv_sems.at[4 * i + k],
                    device_id=(x, y, 1 - c), device_id_type=MESH))
        for cp in copies:
            cp.start()
        for cp in copies:
            cp.wait()

    return _comm_call(
        body, name, by_owner, [jax.ShapeDtypeStruct((4,) + b.shape[1:], b.dtype) for b in by_owner],
        [pltpu.SemaphoreType.DMA((4 * n,)), pltpu.SemaphoreType.DMA((4 * n,))], collective_id)


def _chip_exchange(name, sums, collective_id=None):
    n = len(sums)

    def body(*refs):
        x_refs, out_refs = refs[:n], refs[n:2 * n]
        send_sems, recv_sems, local_sems = refs[2 * n:]
        x, y, c = lax.axis_index("x"), lax.axis_index("y"), lax.axis_index("c")
        chips = [(1 - x, y), (x, 1 - y), (1 - x, 1 - y)]
        my_chip = 2 * x + y
        if collective_id is not None:
            _handshake([(cx, cy, c) for cx, cy in chips])
        mine = [pltpu.make_async_copy(x_refs[i].at[my_chip], out_refs[i].at[my_chip], local_sems.at[i])
                for i in range(n)]
        for cp in mine:
            cp.start()
        sends = []
        for i in range(n):
            for j, (cx, cy) in enumerate(chips):
                sends.append(pltpu.make_async_remote_copy(
                    src_ref=x_refs[i].at[2 * cx + cy], dst_ref=out_refs[i].at[my_chip],
                    send_sem=send_sems.at[3 * i + j], recv_sem=recv_sems.at[3 * i + j],
                    device_id=(cx, cy, c), device_id_type=MESH))
        for cp in sends:
            cp.start()
        for i in range(n):
            for j, (cx, cy) in enumerate(chips):
                pltpu.make_async_remote_copy(
                    src_ref=x_refs[i].at[my_chip], dst_ref=out_refs[i].at[2 * cx + cy],
                    send_sem=send_sems.at[3 * i + j], recv_sem=recv_sems.at[3 * i + j],
                    device_id=(cx, cy, c), device_id_type=MESH).wait_recv()
        for cp in sends:
            cp.wait_send()
        for cp in mine:
            cp.wait()

    return _comm_call(
        body, name, sums, [jax.ShapeDtypeStruct(s.shape, s.dtype) for s in sums],
        [pltpu.SemaphoreType.DMA((3 * n,)), pltpu.SemaphoreType.DMA((3 * n,)), pltpu.SemaphoreType.DMA((n,))],
        collective_id)


def _cast_shards(shards):
    n = len(shards)

    def body(*refs):
        for i in range(n):
            refs[n + i][...] = refs[i][...].astype(BF16)

    vmem = pl.BlockSpec(memory_space=pltpu.VMEM)
    return pl.pallas_call(
        body, name="cast_shards", in_specs=[vmem] * n, out_specs=[vmem] * n,
        out_shape=[jax.ShapeDtypeStruct(s.shape, BF16) for s in shards],
        compiler_params=pltpu.CompilerParams(vmem_limit_bytes=VMEM_LIMIT_V7X),
    )(*shards)


BIG = ("w_in", "w_branch_a", "w_branch_b", "w_out", "w_ffn_gate", "w_ffn_up", "w_ffn_down")


def _local_step(x, target, gains, low, conv_w, wg8, on_grads):
    g_mix, g_hg, g_ffn, g_fin = gains
    w_in = _cols_from_shards("join_w_in", wg8["w_in"])
    wg = _cols_from_shards("join_w_ffn_gate", wg8["w_ffn_gate"])
    wu = _cols_from_shards("join_w_ffn_up", wg8["w_ffn_up"])
    wa, wb = wg8["w_branch_a"], wg8["w_branch_b"]
    wo = wg8["w_out"].reshape(D_MODEL, D_MODEL)
    wd = wg8["w_ffn_down"].reshape(D_FF, D_MODEL)

    ht, hg, cv, gt = _fwd_in(x, g_mix, w_in)
    o, og, ogt, st = _hg_fwd(hg, low, g_hg)
    cvo, cvot = _conv_fwd(cv, conv_w)
    x1, mgt = _merge_fwd(og, cvo, gt, x, wa, wb, wo)
    h2t, gate, up, actt, x2 = _ffn_fwd(x1, g_ffn, wg, wu, wd)
    loss, d_gfin, dx2 = _final_fwd_bwd(x2, target, g_fin)

    dgate, dup, dx1, d_gffn = _ffn_bwd(dx2, x1, gate, up, g_ffn, wg, wu, wd)
    on_grads(dict(
        w_ffn_down=_wgrad("wgrad_ffn_down", actt, dx2, 512).reshape(N_DEV, D_FF // N_DEV, D_MODEL),
        w_ffn_gate=_shards_from_cols("split_w_ffn_gate", _wgrad("wgrad_ffn_gate", h2t, dgate, 1408)),
        w_ffn_up=_shards_from_cols("split_w_ffn_up", _wgrad("wgrad_ffn_up", h2t, dup, 1408))))
    dgt, dya, dyb, dog, dcvo = _merge_bwd(dx1, og, cvo, gt, wa, wb, wo)
    on_grads(dict(
        w_out=_wgrad("wgrad_out", mgt, dx1, 512).reshape(N_DEV, D_MODEL // N_DEV, D_MODEL),
        w_branch_a=_shards_from_cols("split_w_branch_a", _wgrad("wgrad_branch_a", ogt, dya, 512)),
        w_branch_b=_shards_from_cols("split_w_branch_b", _wgrad("wgrad_branch_b", cvot, dyb, 512))))
    dc, db, dxb, d_conv = _conv_bwd(dcvo, cv, conv_w)
    dq, df, di, dg, d_low, d_ghg = _hg_bwd(dog, hg, o, st, low, g_hg)
    dparts = [dq, df, di, dg, dc, db, dxb, dgt]
    on_grads(dict(w_in=_shards_from_cols("split_w_in", _wgrad_in(ht, dparts))))
    grad_x, d_gmix = _in_bwd(dparts, w_in, x, dx1, g_mix)
    small = dict(norm_mix_g=d_gmix, norm_ffn_g=d_gffn, norm_final_g=d_gfin, lower_bounds=d_low, hg_norm_g=d_ghg,
                 conv_w=d_conv, loss=loss)
    return grad_x, small


_SMALL_LAYOUT = (("norm_mix_g", 0, 8), ("norm_ffn_g", 8, 8), ("norm_final_g", 16, 8), ("lower_bounds", 24, 8),
                 ("hg_norm_g", 32, 1))
_LOSS_ROW = 40
_CONV_ROW = 48


def _pad_rows(a, rows):
    return jnp.pad(a, ((0, rows - a.shape[0]), (0, 0)))


def _pack_small(vals, conv_rows):
    parts = [_pad_rows(vals[name].reshape(rows, 128), 8) for name, _, rows in _SMALL_LAYOUT]
    loss = vals["loss"][:, :128] if "loss" in vals else jnp.zeros((1, 128), F32)
    parts.append(_pad_rows(loss, 8))
    parts.append(_pad_rows(conv_rows, SMALL_ROWS - _CONV_ROW))
    return jnp.concatenate(parts, axis=0)


def _conv_shard_rows(a):
    return jnp.pad(a, ((0, 5), (0, 64)))


def kernel(x, norm_mix_g, w_in, lower_bounds, hg_norm_g, conv_w, w_branch_a, w_branch_b, w_out, norm_ffn_g, w_ffn_gate, w_ffn_up, w_ffn_down, norm_final_g, loss_target, m_norm_mix_g, m_w_in, m_lower_bounds, m_hg_norm_g, m_conv_w, m_w_branch_a, m_w_branch_b, m_w_out, m_norm_ffn_g, m_w_ffn_gate, m_w_ffn_up, m_w_ffn_down, m_norm_final_g, v_norm_mix_g, v_w_in, v_lower_bounds, v_hg_norm_g, v_conv_w, v_w_branch_a, v_w_branch_b, v_w_out, v_norm_ffn_g, v_w_ffn_gate, v_w_ffn_up, v_w_ffn_down, v_norm_final_g):
    cx, cy, cc = lax.axis_index("x"), lax.axis_index("y"), lax.axis_index("c")
    my_dev = 4 * cx + 2 * cy + cc

    big = dict(w_in=w_in[0], w_branch_a=w_branch_a[0], w_branch_b=w_branch_b[0], w_out=w_out[0],
               w_ffn_gate=w_ffn_gate[0], w_ffn_up=w_ffn_up[0], w_ffn_down=w_ffn_down[0])
    big_m = dict(w_in=m_w_in[0], w_branch_a=m_w_branch_a[0], w_branch_b=m_w_branch_b[0], w_out=m_w_out[0],
                 w_ffn_gate=m_w_ffn_gate[0], w_ffn_up=m_w_ffn_up[0], w_ffn_down=m_w_ffn_down[0])
    big_v = dict(w_in=v_w_in[0], w_branch_a=v_w_branch_a[0], w_branch_b=v_w_branch_b[0], w_out=v_w_out[0],
                 w_ffn_gate=v_w_ffn_gate[0], w_ffn_up=v_w_ffn_up[0], w_ffn_down=v_w_ffn_down[0])

    shards = dict(zip(BIG, _cast_shards([big[n] for n in BIG])))
    first = _all_gather("gather_w_in", [shards["w_in"], _conv_shard_rows(conv_w[0])])
    later = _all_gather("gather_rest", [shards[n] for n in BIG[1:]], collective_id=1)
    wg8 = dict(zip(BIG, [first[0]] + list(later)))
    conv_full = first[1][:, :3, :64].transpose(1, 0, 2).reshape(3, CONV_WIDTH)

    core = cc.reshape(1).astype(jnp.int32)
    outs = {}
    ids = iter(range(2, 16))

    def reduce_group(grads):
        names = list(grads)
        tag = names[0]
        by_owner = [grads[n] for n in names]
        got = _sibling_swap("sibling_swap_" + tag, by_owner, collective_id=next(ids))
        chip_sums = [_pair_sum("pair_sum_" + n, a, b, core) for n, a, b in zip(names, by_owner, got)]
        parts = _chip_exchange("chip_exchange_" + tag, chip_sums, collective_id=next(ids))
        for n, p in zip(names, parts):
            outs[n] = _adamw_sum("adamw_" + n, big[n], p, big_m[n], big_v[n])

    gains = (norm_mix_g, hg_norm_g, norm_ffn_g, norm_final_g.reshape(1, D_MODEL))
    grad_x, small = _local_step(x[0], loss_target[0], gains, lower_bounds, conv_full, wg8, reduce_group)

    small_all = _all_gather("gather_small", [_pack_small(small, small["conv_w"].reshape(12, 128))])
    ssum = _small_sum(small_all[0])
    conv_g_full = ssum[_CONV_ROW:_CONV_ROW + 12].reshape(3, CONV_WIDTH)
    conv_g = lax.dynamic_slice_in_dim(conv_g_full, my_dev * 64, 64, axis=1)
    loss = ssum[_LOSS_ROW, 0]
    g_rows = jnp.concatenate([ssum[:_CONV_ROW], _pad_rows(_conv_shard_rows(conv_g), SMALL_ROWS - _CONV_ROW)], axis=0)

    def pack_state(a):
        vals = dict(norm_mix_g=a[0], norm_ffn_g=a[1], norm_final_g=a[2], lower_bounds=a[3], hg_norm_g=a[4])
        return _pack_small(vals, _conv_shard_rows(a[5][0]))

    sw = pack_state((norm_mix_g, norm_ffn_g, norm_final_g, lower_bounds, hg_norm_g, conv_w))
    sm = pack_state((m_norm_mix_g, m_norm_ffn_g, m_norm_final_g, m_lower_bounds, m_hg_norm_g, m_conv_w))
    sv = pack_state((v_norm_mix_g, v_norm_ffn_g, v_norm_final_g, v_lower_bounds, v_hg_norm_g, v_conv_w))
    s_delta, s_m, s_v = _small_adamw(sw, g_rows, sm, sv)

    shapes = dict(norm_mix_g=(1, D_MODEL), norm_ffn_g=(1, D_MODEL), norm_final_g=(D_MODEL,),
                  lower_bounds=(2, HG_WIDTH), hg_norm_g=(1, HEAD_DIM))

    def unpack(buf, name):
        if name == "conv_w":
            return buf[_CONV_ROW:_CONV_ROW + 3, :64].reshape(1, 3, 64)
        for nm, off, rows in _SMALL_LAYOUT:
            if nm == name:
                return buf[off:off + rows].reshape(shapes[name])
        raise KeyError(name)

    order = ["norm_mix_g", "w_in", "lower_bounds", "hg_norm_g", "conv_w", "w_branch_a", "w_branch_b", "w_out",
             "norm_ffn_g", "w_ffn_gate", "w_ffn_up", "w_ffn_down", "norm_final_g"]
    result = [loss, grad_x[None]]
    for k, sbuf in enumerate((g_rows, s_delta, s_m, s_v)):
        for n in order:
            if n in outs:
                result.append(outs[n][k][None])
            else:
                result.append(unpack(sbuf, n))
    return tuple(result)
```

```python
import functools

import jax
import jax.numpy as jnp
from jax import lax
from jax.experimental import pallas as pl
from jax.experimental.pallas import tpu as pltpu
from jax.experimental.pallas import tpu_sc as plsc

F32 = jnp.float32
BF16 = jnp.bfloat16

D_MODEL = 1024
HG_WIDTH = 512
HEAD_DIM = 128
N_HEADS = 4
CONV_WIDTH = 512
D_FF = 2816
CHUNK = 32
EPS = 1e-6
Q_SCALE = HEAD_DIM ** -0.5
N_DEV = 8

ADAM_LR = 0.001
ADAM_B1 = 0.9
ADAM_B2 = 0.999
ADAM_EPS = 1e-08
ADAM_WD = 0.01
ADAM_STEP = 10

VMEM_LIMIT_V7X = 56 * 1024 * 1024

PACK_SPLITS = (("w_in", 704), ("w_branch_a", 64), ("w_branch_b", 64), ("w_out", 128),
               ("w_ffn_gate", 352), ("w_ffn_up", 352), ("w_ffn_down", 352))
PACK_ROWS = sum(r for _, r in PACK_SPLITS)
PACK_OFF = {}
_o = 0
for _n, _r in PACK_SPLITS:
    PACK_OFF[_n] = (_o, _r)
    _o += _r

SMALL_ROWS = 64


def _params(sem, vmem=VMEM_LIMIT_V7X):
    return pltpu.CompilerParams(dimension_semantics=sem, vmem_limit_bytes=vmem)


def _mm(a, b):
    return jnp.dot(a.astype(BF16), b.astype(BF16), preferred_element_type=F32)


def _mm_nt(a, b):
    return lax.dot_general(a.astype(BF16), b.astype(BF16), (((1,), (1,)), ((), ())), preferred_element_type=F32)


def _mm_tn(a, b):
    return lax.dot_general(a.astype(BF16), b.astype(BF16), (((0,), (0,)), ((), ())), preferred_element_type=F32)


def _sigmoid(x):
    return 1.0 / (1.0 + jnp.exp(-x))


def _resident(shape):
    nd = len(shape)
    return pl.BlockSpec(shape, lambda *_: (0,) * nd, pipeline_mode=pl.Buffered(1))


def _full(shape):
    nd = len(shape)
    return pl.BlockSpec(shape, lambda *_: (0,) * nd)


def _shard_cols(w_ref):
    return jnp.concatenate([w_ref[s] for s in range(N_DEV)], axis=1)


N_HG = 4 * HG_WIDTH
N_CV = 3 * CONV_WIDTH
N_GT = 2 * D_MODEL
N_IN = N_HG + N_CV + N_GT


def _col(tm, n):
    return pl.BlockSpec((n, tm), lambda i: (0, i))


def _fwd_in(x, g, w_in):
    T = x.shape[0]
    tm = min(256, T)

    def body(x_ref, g_ref, w_ref, ht_ref, hg_ref, cv_ref, gt_ref):
        xv = x_ref[...]
        r = lax.rsqrt(jnp.mean(xv * xv, axis=-1, keepdims=True) + EPS)
        hf = xv * r * g_ref[...]
        h = hf.astype(BF16)
        ht_ref[...] = hf.T.astype(BF16)
        hg_ref[...] = jnp.dot(h, w_ref[:, :N_HG], preferred_element_type=F32)
        cv_ref[...] = jnp.dot(h, w_ref[:, N_HG:N_HG + N_CV], preferred_element_type=F32)
        gt_ref[...] = jnp.dot(h, w_ref[:, N_HG + N_CV:], preferred_element_type=F32)

    row = lambda n: pl.BlockSpec((tm, n), lambda i: (i, 0))
    return pl.pallas_call(
        body, name="fwd_in", grid=(T // tm,),
        in_specs=[row(D_MODEL), _full((1, D_MODEL)), _resident(w_in.shape)],
        out_specs=[_col(tm, D_MODEL), row(N_HG), row(N_CV), row(N_GT)],
        out_shape=[jax.ShapeDtypeStruct((D_MODEL, T), BF16), jax.ShapeDtypeStruct((T, N_HG), F32),
                   jax.ShapeDtypeStruct((T, N_CV), F32), jax.ShapeDtypeStruct((T, N_GT), F32)],
        compiler_params=_params(("parallel",)),
    )(x, g, w_in)


def _chunk_pos(shape):
    return lax.broadcasted_iota(jnp.int32, shape, 0) & (CHUNK - 1)


def _chunk_cumsum(x, pos):
    s = 1
    while s < CHUNK:
        x = x + jnp.where(pos >= s, pltpu.roll(x, s, 0), 0.0)
        s *= 2
    return x


def _chunk_rev_cumsum(x, pos):
    n = x.shape[0]
    s = 1
    while s < CHUNK:
        x = x + jnp.where(pos + s < CHUNK, pltpu.roll(x, n - s, 0), 0.0)
        s *= 2
    return x


def _chunk_bcast(x3, row, tb):
    return jnp.broadcast_to(x3[:, row:row + 1, :], x3.shape).reshape(tb, x3.shape[-1])


def _lower_bound(low_ref):
    l0 = low_ref[0:1, :]
    l1 = low_ref[1:2, :]
    m = jnp.maximum(l0, l1)
    e0 = jnp.exp(l0 - m)
    e1 = jnp.exp(l1 - m)
    return e0 / (e0 + e1), e1 / (e0 + e1)


def _hg_gates(qr, fr, lb, pos, tb):
    sq = _sigmoid(qr)
    q = qr * sq * Q_SCALE
    sg = _sigmoid(fr)
    f = lb + (1.0 - lb) * sg
    k = 1.0 - f
    b = _chunk_cumsum(jnp.log(f), pos)
    b3 = b.reshape(tb // CHUNK, CHUNK, HEAD_DIM)
    anc = _chunk_bcast(b3, CHUNK // 2 - 1, tb)
    blb = _chunk_bcast(b3, CHUNK - 1, tb)
    e_qa = jnp.exp(b - anc)
    e_ka = jnp.exp(anc - b)
    e_b = jnp.exp(b)
    e_ko = jnp.exp(blb - b)
    dec = jnp.exp(blb)
    return sq, q, sg, f, k, e_qa, e_ka, e_b, e_ko, dec


def _intra_mask(sb):
    r = lax.broadcasted_iota(jnp.int32, (sb, sb), 0)
    c = lax.broadcasted_iota(jnp.int32, (sb, sb), 1)
    return ((r // CHUNK) == (c // CHUNK)) & (c <= r)


def _hg_fwd(hg, low, gn):
    T = hg.shape[0]
    tb = min(512, T)
    sb = min(256, tb)
    nb = T // tb
    nc = tb // CHUNK

    def body(q_ref, f_ref, i_ref, g_ref, low_ref, gn_ref, o_ref, og_ref, ogt_ref, st_ref, s_scr):
        t = pl.program_id(1)

        @pl.when(t == 0)
        def _():
            s_scr[...] = jnp.zeros_like(s_scr)

        pos = _chunk_pos((tb, HEAD_DIM))
        lb, _ = _lower_bound(low_ref)
        v = i_ref[...]
        _, q, _, _, k, e_qa, e_ka, e_b, e_ko, dec = _hg_gates(q_ref[...], f_ref[...], lb, pos, tb)
        qh = (q * e_qa).astype(BF16)
        kh = (k * e_ka).astype(BF16)
        qi = (q * e_b).astype(BF16)
        ko = (k * e_ko).astype(BF16)
        vb = v.astype(BF16)
        mask = _intra_mask(sb)
        for s in range(tb // sb):
            sl = slice(s * sb, (s + 1) * sb)
            p = jnp.where(mask, _mm_nt(qh[sl], kh[sl]), 0.0)
            o_ref[sl, :] = _mm(p, vb[sl])
        st = s_scr[...]
        for c in range(nc):
            sl = slice(c * CHUNK, (c + 1) * CHUNK)
            st_ref[c] = st
            o_ref[sl, :] = o_ref[sl, :] + _mm_nt(qi[sl], st)
            st = dec[c * CHUNK:c * CHUNK + 1, :] * st + _mm_tn(vb[sl], ko[sl])
        s_scr[...] = st
        o = o_ref[...]
        r = lax.rsqrt(jnp.mean(o * o, axis=-1, keepdims=True) + EPS)
        gr = g_ref[...]
        og = (o * r * gn_ref[...]) * (gr * _sigmoid(gr))
        og_ref[...] = og.astype(BF16)
        ogt_ref[...] = og.T.astype(BF16)

    col = lambda p: pl.BlockSpec((tb, HEAD_DIM), lambda h, t: (t, p * N_HEADS + h))
    hcol = pl.BlockSpec((tb, HEAD_DIM), lambda h, t: (t, h))
    return pl.pallas_call(
        body, name="hg_fwd", grid=(N_HEADS, nb),
        in_specs=[col(0), col(1), col(2), col(3), pl.BlockSpec((2, HEAD_DIM), lambda h, t: (0, h)),
                  pl.BlockSpec((1, HEAD_DIM), lambda h, t: (0, 0))],
        out_specs=[hcol, hcol, pl.BlockSpec((HEAD_DIM, tb), lambda h, t: (h, t)),
                   pl.BlockSpec((None, nc, HEAD_DIM, HEAD_DIM), lambda h, t: (h, t, 0, 0))],
        out_shape=[jax.ShapeDtypeStruct((T, HG_WIDTH), F32), jax.ShapeDtypeStruct((T, HG_WIDTH), BF16),
                   jax.ShapeDtypeStruct((HG_WIDTH, T), BF16),
                   jax.ShapeDtypeStruct((N_HEADS, T // CHUNK, HEAD_DIM, HEAD_DIM), F32)],
        scratch_shapes=[pltpu.VMEM((HEAD_DIM, HEAD_DIM), F32)],
        compiler_params=_params(("parallel", "arbitrary")),
    )(hg, hg, hg, hg, low, gn)


def _conv_fwd(cv, conv_w):
    T = cv.shape[0]
    nj = CONV_WIDTH // 128

    def body(c_ref, b_ref, x_ref, w_ref, o_ref, ot_ref):
        row = lax.broadcasted_iota(jnp.int32, (T, 128), 0)
        u = c_ref[...] * x_ref[...]
        u1 = jnp.where(row >= 1, pltpu.roll(u, 1, 0), 0.0)
        u2 = jnp.where(row >= 2, pltpu.roll(u, 2, 0), 0.0)
        y = w_ref[0:1, :] * u2 + w_ref[1:2, :] * u1 + w_ref[2:3, :] * u
        out = b_ref[...] * y
        o_ref[...] = out.astype(BF16)
        ot_ref[...] = out.T.astype(BF16)

    col = lambda p: pl.BlockSpec((T, 128), lambda j: (0, p * nj + j))
    return pl.pallas_call(
        body, name="conv_fwd", grid=(nj,),
        in_specs=[col(0), col(1), col(2), pl.BlockSpec((3, 128), lambda j: (0, j))],
        out_specs=[pl.BlockSpec((T, 128), lambda j: (0, j)), pl.BlockSpec((128, T), lambda j: (j, 0))],
        out_shape=[jax.ShapeDtypeStruct((T, CONV_WIDTH), BF16), jax.ShapeDtypeStruct((CONV_WIDTH, T), BF16)],
        compiler_params=_params(("parallel",)),
    )(cv, cv, cv, conv_w)


def _merge_fwd(og, cvo, gt, x, wa, wb, wo):
    T = x.shape[0]
    tm = min(512, T)

    def body(og_ref, cvo_ref, gt_ref, x_ref, wa_ref, wb_ref, wo_ref, x1_ref, mgt_ref):
        ya = jnp.dot(og_ref[...], _shard_cols(wa_ref), preferred_element_type=F32)
        yb = jnp.dot(cvo_ref[...], _shard_cols(wb_ref), preferred_element_type=F32)
        m = _sigmoid(gt_ref[:, :D_MODEL]) * ya + _sigmoid(gt_ref[:, D_MODEL:]) * yb
        mgt_ref[...] = m.T.astype(BF16)
        x1_ref[...] = x_ref[...] + jnp.dot(m.astype(BF16), wo_ref[...], preferred_element_type=F32)

    row = lambda n: pl.BlockSpec((tm, n), lambda i: (i, 0))
    return pl.pallas_call(
        body, name="merge_fwd", grid=(T // tm,),
        in_specs=[row(HG_WIDTH), row(CONV_WIDTH), row(2 * D_MODEL), row(D_MODEL),
                  _resident(wa.shape), _resident(wb.shape), _resident(wo.shape)],
        out_specs=[row(D_MODEL), _col(tm, D_MODEL)],
        out_shape=[jax.ShapeDtypeStruct((T, D_MODEL), F32), jax.ShapeDtypeStruct((D_MODEL, T), BF16)],
        compiler_params=_params(("parallel",)),
    )(og, cvo, gt, x, wa, wb, wo)


def _ffn_fwd(x1, g, wg, wu, wd):
    T = x1.shape[0]
    tm = min(256, T)

    def body(x_ref, g_ref, wg_ref, wu_ref, wd_ref, ht_ref, gate_ref, up_ref, actt_ref, x2_ref):
        xv = x_ref[...]
        r = lax.rsqrt(jnp.mean(xv * xv, axis=-1, keepdims=True) + EPS)
        hf = xv * r * g_ref[...]
        h = hf.astype(BF16)
        ht_ref[...] = hf.T.astype(BF16)
        gate = jnp.dot(h, wg_ref[...], preferred_element_type=F32)
        up = jnp.dot(h, wu_ref[...], preferred_element_type=F32)
        gate_ref[...] = gate
        up_ref[...] = up
        act = gate * _sigmoid(gate) * up
        actt_ref[...] = act.T.astype(BF16)
        x2_ref[...] = xv + jnp.dot(act.astype(BF16), wd_ref[...], preferred_element_type=F32)

    row = lambda n: pl.BlockSpec((tm, n), lambda i: (i, 0))
    return pl.pallas_call(
        body, name="ffn_fwd", grid=(T // tm,),
        in_specs=[row(D_MODEL), _full((1, D_MODEL)), _resident(wg.shape), _resident(wu.shape), _resident(wd.shape)],
        out_specs=[_col(tm, D_MODEL), row(D_FF), row(D_FF), _col(tm, D_FF), row(D_MODEL)],
        out_shape=[jax.ShapeDtypeStruct((D_MODEL, T), BF16), jax.ShapeDtypeStruct((T, D_FF), F32),
                   jax.ShapeDtypeStruct((T, D_FF), F32), jax.ShapeDtypeStruct((D_FF, T), BF16),
                   jax.ShapeDtypeStruct((T, D_MODEL), F32)],
        compiler_params=_params(("parallel",)),
    )(x1, g, wg, wu, wd)


def _final_fwd_bwd(x2, target, g):
    T = x2.shape[0]
    tm = min(512, T)

    def body(x_ref, t_ref, g_ref, loss_ref, dg_ref, dx_ref):
        @pl.when(pl.program_id(0) == 0)
        def _():
            loss_ref[...] = jnp.zeros_like(loss_ref)
            dg_ref[...] = jnp.zeros_like(dg_ref)

        xv = x_ref[...]
        gv = g_ref[...]
        r = lax.rsqrt(jnp.mean(xv * xv, axis=-1, keepdims=True) + EPS)
        xh = xv * r
        err = xh * gv - t_ref[...]
        loss_ref[...] += 0.5 * jnp.sum(jnp.mean(err * err, axis=-1, keepdims=True), axis=0, keepdims=True)
        dy = err * (1.0 / D_MODEL)
        dg_ref[...] += jnp.sum(dy * xh, axis=0, keepdims=True)
        w = dy * gv
        dx_ref[...] = r * (w - xh * jnp.mean(w * xh, axis=-1, keepdims=True))

    row = pl.BlockSpec((tm, D_MODEL), lambda i: (i, 0))
    return pl.pallas_call(
        body, name="final_fwd_bwd", grid=(T // tm,),
        in_specs=[row, row, _full((1, D_MODEL))],
        out_specs=[_full((1, 128)), _full((1, D_MODEL)), row],
        out_shape=[jax.ShapeDtypeStruct((1, 128), F32), jax.ShapeDtypeStruct((1, D_MODEL), F32),
                   jax.ShapeDtypeStruct((T, D_MODEL), F32)],
        compiler_params=_params(("arbitrary",)),
    )(x2, target, g)


def _ffn_bwd(dx2, x1, gate, up, g, wg, wu, wd):
    T = x1.shape[0]
    tm = min(256, T)

    def body(dx2_ref, x_ref, gate_ref, up_ref, g_ref, wg_ref, wu_ref, wd_ref, dgate_ref, dup_ref, dx1_ref, dgn_ref):
        @pl.when(pl.program_id(0) == 0)
        def _():
            dgn_ref[...] = jnp.zeros_like(dgn_ref)

        dx2 = dx2_ref[...]
        dact = _mm_nt(dx2, wd_ref[...])
        gate = gate_ref[...]
        s = _sigmoid(gate)
        dgate = (dact * up_ref[...] * (s * (1.0 + gate * (1.0 - s)))).astype(BF16)
        dup = (dact * (gate * s)).astype(BF16)
        dgate_ref[...] = dgate
        dup_ref[...] = dup
        dh = _mm_nt(dgate, wg_ref[...]) + _mm_nt(dup, wu_ref[...])
        xv = x_ref[...]
        r = lax.rsqrt(jnp.mean(xv * xv, axis=-1, keepdims=True) + EPS)
        xh = xv * r
        dgn_ref[...] += jnp.sum(dh * xh, axis=0, keepdims=True)
        w = dh * g_ref[...]
        dx1_ref[...] = dx2 + r * (w - xh * jnp.mean(w * xh, axis=-1, keepdims=True))

    row = lambda n: pl.BlockSpec((tm, n), lambda i: (i, 0))
    return pl.pallas_call(
        body, name="ffn_bwd", grid=(T // tm,),
        in_specs=[row(D_MODEL), row(D_MODEL), row(D_FF), row(D_FF), _full((1, D_MODEL)),
                  _resident(wg.shape), _resident(wu.shape), _resident(wd.shape)],
        out_specs=[row(D_FF), row(D_FF), row(D_MODEL), _full((1, D_MODEL))],
        out_shape=[jax.ShapeDtypeStruct((T, D_FF), BF16), jax.ShapeDtypeStruct((T, D_FF), BF16),
                   jax.ShapeDtypeStruct((T, D_MODEL), F32), jax.ShapeDtypeStruct((1, D_MODEL), F32)],
        compiler_params=_params(("arbitrary",)),
    )(dx2, x1, gate, up, g, wg, wu, wd)


def _merge_bwd(dx1, og, cvo, gt, wa, wb, wo):
    T = dx1.shape[0]
    tm = min(512, T)

    def body(dx_ref, og_ref, cvo_ref, gt_ref, wa_ref, wb_ref, wo_ref, dgt_ref, dya_ref, dyb_ref, dog_ref, dcvo_ref):
        dm = _mm_nt(dx_ref[...], wo_ref[...])
        wa = _shard_cols(wa_ref)
        wb = _shard_cols(wb_ref)
        ya = jnp.dot(og_ref[...], wa, preferred_element_type=F32)
        yb = jnp.dot(cvo_ref[...], wb, preferred_element_type=F32)
        sa = _sigmoid(gt_ref[:, :D_MODEL])
        sb = _sigmoid(gt_ref[:, D_MODEL:])
        dgt_ref[:, :D_MODEL] = (dm * ya * (sa * (1.0 - sa))).astype(BF16)
        dgt_ref[:, D_MODEL:] = (dm * yb * (sb * (1.0 - sb))).astype(BF16)
        dya = (dm * sa).astype(BF16)
        dyb = (dm * sb).astype(BF16)
        dya_ref[...] = dya
        dyb_ref[...] = dyb
        dog_ref[...] = _mm_nt(dya, wa)
        dcvo_ref[...] = _mm_nt(dyb, wb)

    row = lambda n: pl.BlockSpec((tm, n), lambda i: (i, 0))
    return pl.pallas_call(
        body, name="merge_bwd", grid=(T // tm,),
        in_specs=[row(D_MODEL), row(HG_WIDTH), row(CONV_WIDTH), row(2 * D_MODEL),
                  _resident(wa.shape), _resident(wb.shape), _resident(wo.shape)],
        out_specs=[row(2 * D_MODEL), row(D_MODEL), row(D_MODEL), row(HG_WIDTH), row(CONV_WIDTH)],
        out_shape=[jax.ShapeDtypeStruct((T, 2 * D_MODEL), BF16), jax.ShapeDtypeStruct((T, D_MODEL), BF16),
                   jax.ShapeDtypeStruct((T, D_MODEL), BF16), jax.ShapeDtypeStruct((T, HG_WIDTH), F32),
                   jax.ShapeDtypeStruct((T, CONV_WIDTH), F32)],
        compiler_params=_params(("parallel",)),
    )(dx1, og, cvo, gt, wa, wb, wo)


def _conv_bwd(dcvo, cv, conv_w):
    T = cv.shape[0]
    nj = CONV_WIDTH // 128

    def body(do_ref, c_ref, b_ref, x_ref, w_ref, dc_ref, db_ref, dx_ref, dw_ref):
        row = lax.broadcasted_iota(jnp.int32, (T, 128), 0)
        c = c_ref[...]
        xb = x_ref[...]
        do = do_ref[...]
        u = c * xb
        u1 = jnp.where(row >= 1, pltpu.roll(u, 1, 0), 0.0)
        u2 = jnp.where(row >= 2, pltpu.roll(u, 2, 0), 0.0)
        w0, w1, w2 = w_ref[0:1, :], w_ref[1:2, :], w_ref[2:3, :]
        y = w0 * u2 + w1 * u1 + w2 * u
        db_ref[...] = (do * y).astype(BF16)
        dy = do * b_ref[...]
        dw_ref[0:1, :] = jnp.sum(dy * u2, axis=0, keepdims=True)
        dw_ref[1:2, :] = jnp.sum(dy * u1, axis=0, keepdims=True)
        dw_ref[2:3, :] = jnp.sum(dy * u, axis=0, keepdims=True)
        dy1 = jnp.where(row < T - 1, pltpu.roll(dy, T - 1, 0), 0.0)
        dy2 = jnp.where(row < T - 2, pltpu.roll(dy, T - 2, 0), 0.0)
        du = w2 * dy + w1 * dy1 + w0 * dy2
        dc_ref[...] = (du * xb).astype(BF16)
        dx_ref[...] = (du * c).astype(BF16)

    col = lambda p: pl.BlockSpec((T, 128), lambda j: (0, p * nj + j))
    one = pl.BlockSpec((T, 128), lambda j: (0, j))
    wspec = pl.BlockSpec((3, 128), lambda j: (0, j))
    out = jax.ShapeDtypeStruct((T, CONV_WIDTH), BF16)
    return pl.pallas_call(
        body, name="conv_bwd", grid=(nj,),
        in_specs=[one, col(0), col(1), col(2), wspec],
        out_specs=[one, one, one, wspec],
        out_shape=[out, out, out, jax.ShapeDtypeStruct((3, CONV_WIDTH), F32)],
        compiler_params=_params(("parallel",)),
    )(dcvo, cv, cv, cv, conv_w)


def _drop_operands(body, first, count):
    def wrapped(*refs):
        return body(*refs[:first], *refs[first + count:])
    return wrapped


def _hg_bwd(dog, hg, o, st, low, gn, after=()):
    T = hg.shape[0]
    tb = min(512, T)
    sb = min(256, tb)
    nb = T // tb
    nc = tb // CHUNK

    def body(q_ref, f_ref, i_ref, g_ref, low_ref, gn_ref, o_ref, dog_ref, st_ref,
             dq_ref, df_ref, di_ref, dg_ref, dlow_ref, dgn_ref,
             ds_scr, dqi_scr, dko_scr, dv_scr, dd_scr, dqh_scr, dkh_scr):
        h = pl.program_id(0)
        t = pl.program_id(1)

        @pl.when(t == 0)
        def _():
            ds_scr[...] = jnp.zeros_like(ds_scr)
            dlow_ref[...] = jnp.zeros_like(dlow_ref)

        @pl.when((t == 0) & (h == 0))
        def _():
            dgn_ref[...] = jnp.zeros_like(dgn_ref)

        pos = _chunk_pos((tb, HEAD_DIM))
        lb, lb1 = _lower_bound(low_ref)
        qr = q_ref[...]
        v = i_ref[...]
        sq, q, sg, f, k, e_qa, e_ka, e_b, e_ko, dec = _hg_gates(qr, f_ref[...], lb, pos, tb)

        gr = g_ref[...]
        gnv = gn_ref[...]
        o = o_ref[...]
        dog_v = dog_ref[...]
        sgr = _sigmoid(gr)
        r = lax.rsqrt(jnp.mean(o * o, axis=-1, keepdims=True) + EPS)
        oh = o * r
        dg_ref[...] = (dog_v * (oh * gnv) * (sgr * (1.0 + gr * (1.0 - sgr)))).astype(BF16)
        don = dog_v * (gr * sgr)
        dgn_ref[...] += jnp.sum(don * oh, axis=0, keepdims=True)
        w = don * gnv
        do = (r * (w - oh * jnp.mean(w * oh, axis=-1, keepdims=True))).astype(BF16)

        qh = (q * e_qa).astype(BF16)
        kh = (k * e_ka).astype(BF16)
        qi = (q * e_b).astype(BF16)
        ko = (k * e_ko).astype(BF16)
        vb = v.astype(BF16)

        mask = _intra_mask(sb)
        for s in range(tb // sb):
            sl = slice(s * sb, (s + 1) * sb)
            p = jnp.where(mask, _mm_nt(qh[sl], kh[sl]), 0.0).astype(BF16)
            dp = jnp.where(mask, _mm_nt(do[sl], vb[sl]), 0.0).astype(BF16)
            dv_scr[sl, :] = _mm_tn(p, do[sl])
            dqh_scr[sl, :] = _mm(dp, kh[sl])
            dkh_scr[sl, :] = _mm_tn(dp, qh[sl])

        ds = ds_scr[...]
        for c in reversed(range(nc)):
            sl = slice(c * CHUNK, (c + 1) * CHUNK)
            st_c = st_ref[c]
            dqi_scr[sl, :] = _mm(do[sl], st_c)
            dko_scr[sl, :] = _mm(vb[sl], ds)
            dv_scr[sl, :] = dv_scr[sl, :] + _mm_nt(ko[sl], ds)
            dd_scr[sl, :] = jnp.broadcast_to(jnp.sum(ds * st_c, axis=0, keepdims=True), (CHUNK, HEAD_DIM))
            ds = dec[c * CHUNK:c * CHUNK + 1, :] * ds + _mm_tn(do[sl], qi[sl])
        ds_scr[...] = ds

        dko_e = dko_scr[...] * e_ko
        dq = dqh_scr[...] * e_qa + dqi_scr[...] * e_b
        dk = dkh_scr[...] * e_ka + dko_e
        kd3 = (k * dko_e).reshape(nc, CHUNK, HEAD_DIM)
        last = jnp.broadcast_to(jnp.sum(kd3, axis=1, keepdims=True), kd3.shape).reshape(tb, HEAD_DIM)
        db = q * dq - k * dk + jnp.where(pos == CHUNK - 1, dec * dd_scr[...] + last, 0.0)
        dlg = _chunk_rev_cumsum(db, pos)
        dfv = dlg / f - dk
        s_low = jnp.sum(dfv * (1.0 - sg), axis=0, keepdims=True)
        dlow_ref[0:1, :] += s_low * lb * (1.0 - lb)
        dlow_ref[1:2, :] += -s_low * lb * lb1
        df_ref[...] = (dfv * (1.0 - lb) * sg * (1.0 - sg)).astype(BF16)
        dq_ref[...] = (dq * Q_SCALE * (sq * (1.0 + qr * (1.0 - sq)))).astype(BF16)
        di_ref[...] = dv_scr[...].astype(BF16)

    rt = lambda t: nb - 1 - t
    col = lambda p: pl.BlockSpec((tb, HEAD_DIM), lambda h, t: (rt(t), p * N_HEADS + h))
    hcol = pl.BlockSpec((tb, HEAD_DIM), lambda h, t: (rt(t), h))
    piece = jax.ShapeDtypeStruct((T, HG_WIDTH), BF16)
    tile = pltpu.VMEM((tb, HEAD_DIM), F32)
    return pl.pallas_call(
        _drop_operands(body, 9, len(after)), name="hg_bwd", grid=(N_HEADS, nb),
        in_specs=[col(0), col(1), col(2), col(3), pl.BlockSpec((2, HEAD_DIM), lambda h, t: (0, h)),
                  pl.BlockSpec((1, HEAD_DIM), lambda h, t: (0, 0)), hcol, hcol,
                  pl.BlockSpec((None, nc, HEAD_DIM, HEAD_DIM), lambda h, t: (h, rt(t), 0, 0))]
                 + [HBM_SPEC] * len(after),
        out_specs=[hcol, hcol, hcol, hcol, pl.BlockSpec((2, HEAD_DIM), lambda h, t: (0, h)),
                   pl.BlockSpec((1, HEAD_DIM), lambda h, t: (0, 0))],
        out_shape=[piece, piece, piece, piece, jax.ShapeDtypeStruct((2, HG_WIDTH), F32),
                   jax.ShapeDtypeStruct((1, HEAD_DIM), F32)],
        scratch_shapes=[pltpu.VMEM((HEAD_DIM, HEAD_DIM), F32), tile, tile, tile, tile, tile, tile],
        compiler_params=_params(("arbitrary", "arbitrary")),
    )(hg, hg, hg, hg, low, gn, o, dog, st, *after)


def _in_bwd(dparts, w_in, x, dx1, g):
    T = x.shape[0]
    tm = min(256, T)
    widths = [p.shape[1] for p in dparts]
    offs = [sum(widths[:i]) for i in range(len(widths))]
    n = len(dparts)

    def body(*refs):
        d_refs = refs[:n]
        w_ref, x_ref, dx1_ref, g_ref, dx_ref, dgn_ref = refs[n:]

        @pl.when(pl.program_id(0) == 0)
        def _():
            dgn_ref[...] = jnp.zeros_like(dgn_ref)

        dh = None
        for d_ref, off, wd in zip(d_refs, offs, widths):
            part = _mm_nt(d_ref[...], w_ref[:, off:off + wd])
            dh = part if dh is None else dh + part
        xv = x_ref[...]
        r = lax.rsqrt(jnp.mean(xv * xv, axis=-1, keepdims=True) + EPS)
        xh = xv * r
        dgn_ref[...] += jnp.sum(dh * xh, axis=0, keepdims=True)
        w = dh * g_ref[...]
        dx_ref[...] = dx1_ref[...] + r * (w - xh * jnp.mean(w * xh, axis=-1, keepdims=True))

    row = lambda m: pl.BlockSpec((tm, m), lambda i: (i, 0))
    return pl.pallas_call(
        body, name="in_bwd", grid=(T // tm,),
        in_specs=[row(wd) for wd in widths] + [_resident(w_in.shape), row(D_MODEL), row(D_MODEL), _full((1, D_MODEL))],
        out_specs=[row(D_MODEL), _full((1, D_MODEL))],
        out_shape=[jax.ShapeDtypeStruct((T, D_MODEL), F32), jax.ShapeDtypeStruct((1, D_MODEL), F32)],
        compiler_params=_params(("arbitrary",)),
    )(*dparts, w_in, x, dx1, g)


def _wgrad(name, at, b, tn):
    M, T = at.shape
    N = b.shape[1]
    tk = min(1024, T)
    nk = T // tk

    def body(a_ref, b_ref, o_ref, acc):
        k = pl.program_id(1)
        part = _mm(a_ref[...], b_ref[...])

        @pl.when(k == 0)
        def _():
            acc[...] = part

        @pl.when(k != 0)
        def _():
            acc[...] += part

        @pl.when(k == nk - 1)
        def _():
            o_ref[...] = acc[...].astype(BF16)

    return pl.pallas_call(
        body, name=name, grid=(N // tn, nk),
        in_specs=[pl.BlockSpec((M, tk), lambda j, k: (0, k)), pl.BlockSpec((tk, tn), lambda j, k: (k, j))],
        out_specs=pl.BlockSpec((M, tn), lambda j, k: (0, j)),
        out_shape=jax.ShapeDtypeStruct((M, N), BF16),
        scratch_shapes=[pltpu.VMEM((M, tn), F32)],
        compiler_params=_params(("parallel", "arbitrary")),
    )(at, b)


def _wgrad_in(ht, dparts, after=()):
    M, T = ht.shape
    tn = 512
    tk = min(1024, T)
    nk = T // tk
    nblk = [p.shape[1] // tn for p in dparts]
    start = [sum(nblk[:i]) for i in range(len(nblk))]
    n = len(dparts)

    def body(a_ref, *refs):
        d_refs, o_ref, acc = refs[:n], refs[n], refs[n + 1]
        j = pl.program_id(0)
        k = pl.program_id(1)

        @pl.when(k == 0)
        def _():
            acc[...] = jnp.zeros_like(acc)

        for d_ref, s, nb in zip(d_refs, start, nblk):
            @pl.when((j >= s) & (j < s + nb))
            def _():
                acc[...] += _mm(a_ref[...], d_ref[...])

        @pl.when(k == nk - 1)
        def _():
            o_ref[...] = acc[...].astype(BF16)

    def piece_spec(s, nb):
        def index(j, k):
            inside = (j >= s) & (j < s + nb)
            return jnp.where(inside, k, 0), jnp.clip(j - s, 0, nb - 1)
        return pl.BlockSpec((tk, tn), index)

    return pl.pallas_call(
        _drop_operands(body, 1 + n, len(after)), name="wgrad_in", grid=(sum(nblk), nk),
        in_specs=[pl.BlockSpec((M, tk), lambda j, k: (0, k))] + [piece_spec(s, nb) for s, nb in zip(start, nblk)]
                 + [HBM_SPEC] * len(after),
        out_specs=pl.BlockSpec((M, tn), lambda j, k: (0, j)),
        out_shape=jax.ShapeDtypeStruct((M, sum(nblk) * tn), BF16),
        scratch_shapes=[pltpu.VMEM((M, tn), F32)],
        compiler_params=_params(("parallel", "arbitrary")),
    )(ht, *dparts, *after)


def _adamw_math(w, g, m, v):
    m = ADAM_B1 * m + (1.0 - ADAM_B1) * g
    v = ADAM_B2 * v + (1.0 - ADAM_B2) * (g * g)
    m_hat = m / (1.0 - ADAM_B1 ** ADAM_STEP)
    v_hat = v / (1.0 - ADAM_B2 ** ADAM_STEP)
    delta = -ADAM_LR * (m_hat / (jnp.sqrt(v_hat) + ADAM_EPS) + ADAM_WD * w)
    return delta, m, v


def _adamw_sum(name, w, parts, m, v):
    R, C = w.shape
    tr = _row_tile(R)

    def body(w_ref, p_ref, m_ref, v_ref, g_out, d_out, m_out, v_out):
        g = p_ref[0].astype(F32)
        for k in range(1, 4):
            g = g + p_ref[k].astype(F32)
        g_out[...] = g
        d_out[...], m_out[...], v_out[...] = _adamw_math(w_ref[...], g, m_ref[...], v_ref[...])

    blk = pl.BlockSpec((tr, C), lambda i: (i, 0))
    out = jax.ShapeDtypeStruct((R, C), F32)
    return pl.pallas_call(
        body, name=name, grid=(R // tr,),
        in_specs=[blk, pl.BlockSpec((4, tr, C), lambda i: (0, i, 0)), blk, blk],
        out_specs=[blk, blk, blk, blk], out_shape=[out, out, out, out],
        compiler_params=_params(("parallel",)),
    )(w, parts, m, v)


def _small_sum(gathered):
    R = gathered.shape[1]

    def body(p_ref, o_ref):
        g = p_ref[0]
        for k in range(1, N_DEV):
            g = g + p_ref[k]
        o_ref[...] = g

    return pl.pallas_call(
        body, name="small_sum", in_specs=[_full(gathered.shape)], out_specs=_full((R, 128)), grid=(1,),
        out_shape=jax.ShapeDtypeStruct((R, 128), F32),
    )(gathered)


def _small_adamw(w, g, m, v):
    def body(w_ref, g_ref, m_ref, v_ref, d_out, m_out, v_out):
        d_out[...], m_out[...], v_out[...] = _adamw_math(w_ref[...], g_ref[...], m_ref[...], v_ref[...])

    out = jax.ShapeDtypeStruct(w.shape, F32)
    spec = _full(w.shape)
    return pl.pallas_call(
        body, name="small_adamw", grid=(1,), in_specs=[spec] * 4, out_specs=[spec] * 3, out_shape=[out, out, out],
    )(w, g, m, v)


def _row_tile(rows):
    for cand in (256, 128):
        if rows % cand == 0:
            return cand
    return rows


def _pair_sum(name, by_owner, got, core):
    _, R, C = got.shape
    tr = _row_tile(R)

    def body(core_ref, a_ref, b_ref, o_ref):
        o_ref[...] = (a_ref[...].astype(F32) + b_ref[...].astype(F32)).astype(BF16)

    blk = pl.BlockSpec((None, tr, C), lambda k, i, core_ref: (k, i, 0))
    mine = pl.BlockSpec((None, tr, C), lambda k, i, core_ref: (2 * k + core_ref[0], i, 0))
    return pl.pallas_call(
        body, name=name,
        grid_spec=pltpu.PrefetchScalarGridSpec(num_scalar_prefetch=1, grid=(4, R // tr), in_specs=[mine, blk],
                                               out_specs=blk),
        out_shape=jax.ShapeDtypeStruct(got.shape, BF16),
        compiler_params=_params(("parallel", "parallel")),
    )(core, by_owner, got)


def _cols_from_shards(name, g):
    _, R, c = g.shape
    tr = _row_tile(R)

    def body(g_ref, o_ref):
        for s in range(N_DEV):
            o_ref[:, s * c:(s + 1) * c] = g_ref[s]

    return pl.pallas_call(
        body, name=name, grid=(R // tr,),
        in_specs=[pl.BlockSpec((N_DEV, tr, c), lambda i: (0, i, 0))],
        out_specs=pl.BlockSpec((tr, N_DEV * c), lambda i: (i, 0)),
        out_shape=jax.ShapeDtypeStruct((R, N_DEV * c), g.dtype),
        compiler_params=_params(("parallel",)),
    )(g)


def _shards_from_cols(name, full):
    R, allc = full.shape
    c = allc // N_DEV
    tr = _row_tile(R)

    def body(f_ref, o_ref):
        for s in range(N_DEV):
            o_ref[s] = f_ref[:, s * c:(s + 1) * c]

    return pl.pallas_call(
        body, name=name, grid=(R // tr,),
        in_specs=[pl.BlockSpec((tr, allc), lambda i: (i, 0))],
        out_specs=pl.BlockSpec((N_DEV, tr, c), lambda i: (0, i, 0)),
        out_shape=jax.ShapeDtypeStruct((N_DEV, R, c), full.dtype),
        compiler_params=_params(("parallel",)),
    )(full)


MESH = pl.DeviceIdType.MESH
HBM_SPEC = pl.BlockSpec(memory_space=pl.ANY)


def _handshake(peers):
    barrier = pltpu.get_barrier_semaphore()
    for peer in peers:
        pl.semaphore_signal(barrier, inc=1, device_id=peer, device_id_type=MESH)
    pl.semaphore_wait(barrier, len(peers))


def _comm_call(body, name, operands, out_shape, scratch, collective_id):
    if collective_id is None:
        return pl.pallas_call(body, name=name, in_specs=[HBM_SPEC] * len(operands), out_specs=[HBM_SPEC] * len(out_shape),
                              out_shape=out_shape, scratch_shapes=scratch)(*operands)
    return pl.kernel(body, out_type=out_shape, mesh=plsc.ScalarSubcoreMesh(axis_name="sequencer", num_cores=1),
                     scratch_types=scratch, name=name,
                     compiler_params=pltpu.CompilerParams(collective_id=collective_id))(*operands)


def _all_gather(name, blocks, collective_id=None, after=()):
    n = len(blocks)
    na = len(after)

    def body(*refs):
        x_refs, out_refs = refs[:n], refs[n + na:2 * n + na]
        send_sems, recv_sems, local_sems = refs[2 * n + na:]
        x, y, c = lax.axis_index("x"), lax.axis_index("y"), lax.axis_index("c")
        me, sibling = (x, y, c), (x, y, 1 - c)
        chips = [(1 - x, y), (x, 1 - y), (1 - x, 1 - y)]
        if collective_id is not None:
            _handshake([sibling] + [(*chip, c) for chip in chips])

        def slot(i, px, py, pc):
            return out_refs[i].at[4 * px + 2 * py + pc]

        def copy(i, k, blk, to, src=None):
            return pltpu.make_async_remote_copy(
                src_ref=slot(i, *blk) if src is None else src, dst_ref=slot(i, *blk),
                send_sem=send_sems.at[7 * i + k], recv_sem=recv_sems.at[7 * i + k], device_id=to, device_id_type=MESH)

        mine = [pltpu.make_async_copy(x_refs[i], slot(i, *me), local_sems.at[i]) for i in range(n)]
        for cp in mine:
            cp.start()
        first = []
        for i in range(n):
            first.append(copy(i, 0, me, sibling, src=x_refs[i]))
            first += [copy(i, 1 + j, me, (*chip, c), src=x_refs[i]) for j, chip in enumerate(chips)]
        for cp in first:
            cp.start()
        passed = []
        for i in range(n):
            for j, chip in enumerate(chips):
                copy(i, 1 + j, (*chip, c), me).wait_recv()
                passed.append(copy(i, 4 + j, (*chip, c), sibling))
                passed[-1].start()
        for i in range(n):
            copy(i, 0, sibling, me).wait_recv()
            for j, chip in enumerate(chips):
                copy(i, 4 + j, (*chip, 1 - c), me).wait_recv()
        for cp in first + passed:
            cp.wait_send()
        for cp in mine:
            cp.wait()

    return _comm_call(
        body, name, list(blocks) + list(after), [jax.ShapeDtypeStruct((N_DEV,) + b.shape, b.dtype) for b in blocks],
        [pltpu.SemaphoreType.DMA((7 * n,)), pltpu.SemaphoreType.DMA((7 * n,)), pltpu.SemaphoreType.DMA((n,))],
        collective_id)


def _sibling_swap(name, by_owner, collective_id=None):
    n = len(by_owner)

    def body(*refs):
        x_refs, out_refs = refs[:n], refs[n:2 * n]
        send_sems, recv_sems = refs[2 * n:]
        x, y, c = lax.axis_index("x"), lax.axis_index("y"), lax.axis_index("c")
        if collective_id is not None:
            _handshake([(x, y, 1 - c)])
        copies = []
        for i in range(n):
            for k in range(4):
                copies.append(pltpu.make_async_remote_copy(
                    src_ref=x_refs[i].at[2 * k + 1 - c], dst_ref=out_refs[i].at[k],
                    send_sem=send_sems.at[4 * i + k], recv_sem=recv_sems.at[4 * i + k],
                    device_id=(x, y, 1 - c), device_id_type=MESH))
        for cp in copies:
            cp.start()
        for cp in copies:
            cp.wait()

    return _comm_call(
        body, name, by_owner, [jax.ShapeDtypeStruct((4,) + b.shape[1:], b.dtype) for b in by_owner],
        [pltpu.SemaphoreType.DMA((4 * n,)), pltpu.SemaphoreType.DMA((4 * n,))], collective_id)


def _chip_exchange(name, sums, collective_id=None):
    n = len(sums)

    def body(*refs):
        x_refs, out_refs = refs[:n], refs[n:2 * n]
        send_sems, recv_sems, local_sems = refs[2 * n:]
        x, y, c = lax.axis_index("x"), lax.axis_index("y"), lax.axis_index("c")
        chips = [(1 - x, y), (x, 1 - y), (1 - x, 1 - y)]
        my_chip = 2 * x + y
        if collective_id is not None:
            _handshake([(cx, cy, c) for cx, cy in chips])
        mine = [pltpu.make_async_copy(x_refs[i].at[my_chip], out_refs[i].at[my_chip], local_sems.at[i])
                for i in range(n)]
        for cp in mine:
            cp.start()
        sends = []
        for i in range(n):
            for j, (cx, cy) in enumerate(chips):
                sends.append(pltpu.make_async_remote_copy(
                    src_ref=x_refs[i].at[2 * cx + cy], dst_ref=out_refs[i].at[my_chip],
                    send_sem=send_sems.at[3 * i + j], recv_sem=recv_sems.at[3 * i + j],
                    device_id=(cx, cy, c), device_id_type=MESH))
        for cp in sends:
            cp.start()
        for i in range(n):
            for j, (cx, cy) in enumerate(chips):
                pltpu.make_async_remote_copy(
                    src_ref=x_refs[i].at[my_chip], dst_ref=out_refs[i].at[2 * cx + cy],
                    send_sem=send_sems.at[3 * i + j], recv_sem=recv_sems.at[3 * i + j],
                    device_id=(cx, cy, c), device_id_type=MESH).wait_recv()
        for cp in sends:
            cp.wait_send()
        for cp in mine:
            cp.wait()

    return _comm_call(
        body, name, sums, [jax.ShapeDtypeStruct(s.shape, s.dtype) for s in sums],
        [pltpu.SemaphoreType.DMA((3 * n,)), pltpu.SemaphoreType.DMA((3 * n,)), pltpu.SemaphoreType.DMA((n,))],
        collective_id)


def _cast_shards(shards):
    n = len(shards)

    def body(*refs):
        for i in range(n):
            refs[n + i][...] = refs[i][...].astype(BF16)

    vmem = pl.BlockSpec(memory_space=pltpu.VMEM)
    return pl.pallas_call(
        body, name="cast_shards", in_specs=[vmem] * n, out_specs=[vmem] * n,
        out_shape=[jax.ShapeDtypeStruct(s.shape, BF16) for s in shards],
        compiler_params=pltpu.CompilerParams(vmem_limit_bytes=VMEM_LIMIT_V7X),
    )(*shards)


BIG = ("w_in", "w_branch_a", "w_branch_b", "w_out", "w_ffn_gate", "w_ffn_up", "w_ffn_down")


def _local_step(x, target, gains, low, conv_w, wg8, on_grads):
    g_mix, g_hg, g_ffn, g_fin = gains
    w_in = _cols_from_shards("join_w_in", wg8["w_in"])
    wg = _cols_from_shards("join_w_ffn_gate", wg8["w_ffn_gate"])
    wu = _cols_from_shards("join_w_ffn_up", wg8["w_ffn_up"])
    wa, wb = wg8["w_branch_a"], wg8["w_branch_b"]
    wo = wg8["w_out"].reshape(D_MODEL, D_MODEL)
    wd = wg8["w_ffn_down"].reshape(D_FF, D_MODEL)

    ht, hg, cv, gt = _fwd_in(x, g_mix, w_in)
    o, og, ogt, st = _hg_fwd(hg, low, g_hg)
    cvo, cvot = _conv_fwd(cv, conv_w)
    x1, mgt = _merge_fwd(og, cvo, gt, x, wa, wb, wo)
    h2t, gate, up, actt, x2 = _ffn_fwd(x1, g_ffn, wg, wu, wd)
    loss, d_gfin, dx2 = _final_fwd_bwd(x2, target, g_fin)

    dgate, dup, dx1, d_gffn = _ffn_bwd(dx2, x1, gate, up, g_ffn, wg, wu, wd)
    sums_ffn = on_grads(dict(
        w_ffn_down=_wgrad("wgrad_ffn_down", actt, dx2, 512).reshape(N_DEV, D_FF // N_DEV, D_MODEL),
        w_ffn_gate=_shards_from_cols("split_w_ffn_gate", _wgrad("wgrad_ffn_gate", h2t, dgate, 1408)),
        w_ffn_up=_shards_from_cols("split_w_ffn_up", _wgrad("wgrad_ffn_up", h2t, dup, 1408))))
    dgt, dya, dyb, dog, dcvo = _merge_bwd(dx1, og, cvo, gt, wa, wb, wo)
    dc, db, dxb, d_conv = _conv_bwd(dcvo, cv, conv_w)
    dq, df, di, dg, d_low, d_ghg = _hg_bwd(dog, hg, o, st, low, g_hg, after=sums_ffn)
    sums_out = on_grads(dict(
        w_out=_wgrad("wgrad_out", mgt, dx1, 512).reshape(N_DEV, D_MODEL // N_DEV, D_MODEL),
        w_branch_a=_shards_from_cols("split_w_branch_a", _wgrad("wgrad_branch_a", ogt, dya, 512)),
        w_branch_b=_shards_from_cols("split_w_branch_b", _wgrad("wgrad_branch_b", cvot, dyb, 512))))
    dparts = [dq, df, di, dg, dc, db, dxb, dgt]
    on_grads(dict(w_in=_shards_from_cols("split_w_in", _wgrad_in(ht, dparts, after=sums_out))))
    grad_x, d_gmix = _in_bwd(dparts, w_in, x, dx1, g_mix)
    small = dict(norm_mix_g=d_gmix, norm_ffn_g=d_gffn, norm_final_g=d_gfin, lower_bounds=d_low, hg_norm_g=d_ghg,
                 conv_w=d_conv, loss=loss)
    return grad_x, small


_SMALL_LAYOUT = (("norm_mix_g", 0, 8), ("norm_ffn_g", 8, 8), ("norm_final_g", 16, 8), ("lower_bounds", 24, 8),
                 ("hg_norm_g", 32, 1))
_LOSS_ROW = 40
_CONV_ROW = 48


def _pad_rows(a, rows):
    return jnp.pad(a, ((0, rows - a.shape[0]), (0, 0)))


def _pack_small(vals, conv_rows):
    parts = [_pad_rows(vals[name].reshape(rows, 128), 8) for name, _, rows in _SMALL_LAYOUT]
    loss = vals["loss"][:, :128] if "loss" in vals else jnp.zeros((1, 128), F32)
    parts.append(_pad_rows(loss, 8))
    parts.append(_pad_rows(conv_rows, SMALL_ROWS - _CONV_ROW))
    return jnp.concatenate(parts, axis=0)


def _conv_shard_rows(a):
    return jnp.pad(a, ((0, 5), (0, 64)))


def kernel(x, norm_mix_g, w_in, lower_bounds, hg_norm_g, conv_w, w_branch_a, w_branch_b, w_out, norm_ffn_g, w_ffn_gate, w_ffn_up, w_ffn_down, norm_final_g, loss_target, m_norm_mix_g, m_w_in, m_lower_bounds, m_hg_norm_g, m_conv_w, m_w_branch_a, m_w_branch_b, m_w_out, m_norm_ffn_g, m_w_ffn_gate, m_w_ffn_up, m_w_ffn_down, m_norm_final_g, v_norm_mix_g, v_w_in, v_lower_bounds, v_hg_norm_g, v_conv_w, v_w_branch_a, v_w_branch_b, v_w_out, v_norm_ffn_g, v_w_ffn_gate, v_w_ffn_up, v_w_ffn_down, v_norm_final_g):
    cx, cy, cc = lax.axis_index("x"), lax.axis_index("y"), lax.axis_index("c")
    my_dev = 4 * cx + 2 * cy + cc

    big = dict(w_in=w_in[0], w_branch_a=w_branch_a[0], w_branch_b=w_branch_b[0], w_out=w_out[0],
               w_ffn_gate=w_ffn_gate[0], w_ffn_up=w_ffn_up[0], w_ffn_down=w_ffn_down[0])
    big_m = dict(w_in=m_w_in[0], w_branch_a=m_w_branch_a[0], w_branch_b=m_w_branch_b[0], w_out=m_w_out[0],
                 w_ffn_gate=m_w_ffn_gate[0], w_ffn_up=m_w_ffn_up[0], w_ffn_down=m_w_ffn_down[0])
    big_v = dict(w_in=v_w_in[0], w_branch_a=v_w_branch_a[0], w_branch_b=v_w_branch_b[0], w_out=v_w_out[0],
                 w_ffn_gate=v_w_ffn_gate[0], w_ffn_up=v_w_ffn_up[0], w_ffn_down=v_w_ffn_down[0])

    shards = dict(zip(BIG, _cast_shards([big[n] for n in BIG])))
    first = _all_gather("gather_w_in", [shards["w_in"], _conv_shard_rows(conv_w[0])])
    later = _all_gather("gather_rest", [shards[n] for n in BIG[1:]], collective_id=1, after=first[1:])
    wg8 = dict(zip(BIG, [first[0]] + list(later)))
    conv_full = first[1][:, :3, :64].transpose(1, 0, 2).reshape(3, CONV_WIDTH)

    core = cc.reshape(1).astype(jnp.int32)
    outs = {}
    ids = iter(range(2, 16))

    def reduce_group(grads):
        names = list(grads)
        tag = names[0]
        by_owner = [grads[n] for n in names]
        got = _sibling_swap("sibling_swap_" + tag, by_owner, collective_id=next(ids))
        chip_sums = [_pair_sum("pair_sum_" + n, a, b, core) for n, a, b in zip(names, by_owner, got)]
        parts = _chip_exchange("chip_exchange_" + tag, chip_sums, collective_id=next(ids))
        for n, p in zip(names, parts):
            outs[n] = _adamw_sum("adamw_" + n, big[n], p, big_m[n], big_v[n])
        return chip_sums

    gains = (norm_mix_g, hg_norm_g, norm_ffn_g, norm_final_g.reshape(1, D_MODEL))
    grad_x, small = _local_step(x[0], loss_target[0], gains, lower_bounds, conv_full, wg8, reduce_group)

    small_all = _all_gather("gather_small", [_pack_small(small, small["conv_w"].reshape(12, 128))],
                            collective_id=next(ids))
    ssum = _small_sum(small_all[0])
    conv_g_full = ssum[_CONV_ROW:_CONV_ROW + 12].reshape(3, CONV_WIDTH)
    conv_g = lax.dynamic_slice_in_dim(conv_g_full, my_dev * 64, 64, axis=1)
    loss = ssum[_LOSS_ROW, 0]
    g_rows = jnp.concatenate([ssum[:_CONV_ROW], _pad_rows(_conv_shard_rows(conv_g), SMALL_ROWS - _CONV_ROW)], axis=0)

    def pack_state(a):
        vals = dict(norm_mix_g=a[0], norm_ffn_g=a[1], norm_final_g=a[2], lower_bounds=a[3], hg_norm_g=a[4])
        return _pack_small(vals, _conv_shard_rows(a[5][0]))

    sw = pack_state((norm_mix_g, norm_ffn_g, norm_final_g, lower_bounds, hg_norm_g, conv_w))
    sm = pack_state((m_norm_mix_g, m_norm_ffn_g, m_norm_final_g, m_lower_bounds, m_hg_norm_g, m_conv_w))
    sv = pack_state((v_norm_mix_g, v_norm_ffn_g, v_norm_final_g, v_lower_bounds, v_hg_norm_g, v_conv_w))
    s_delta, s_m, s_v = _small_adamw(sw, g_rows, sm, sv)

    shapes = dict(norm_mix_g=(1, D_MODEL), norm_ffn_g=(1, D_MODEL), norm_final_g=(D_MODEL,),
                  lower_bounds=(2, HG_WIDTH), hg_norm_g=(1, HEAD_DIM))

    def unpack(buf, name):
        if name == "conv_w":
            return buf[_CONV_ROW:_CONV_ROW + 3, :64].reshape(1, 3, 64)
        for nm, off, rows in _SMALL_LAYOUT:
            if nm == name:
                return buf[off:off + rows].reshape(shapes[name])
        raise KeyError(name)

    order = ["norm_mix_g", "w_in", "lower_bounds", "hg_norm_g", "conv_w", "w_branch_a", "w_branch_b", "w_out",
             "norm_ffn_g", "w_ffn_gate", "w_ffn_up", "w_ffn_down", "norm_final_g"]
    result = [loss, grad_x[None]]
    for k, sbuf in enumerate((g_rows, s_delta, s_m, s_v)):
        for n in order:
            if n in outs:
                result.append(outs[n][k][None])
            else:
                result.append(unpack(sbuf, n))
    return tuple(result)
```

```python
import functools

import jax
import jax.numpy as jnp
from jax import lax
from jax.experimental import pallas as pl
from jax.experimental.pallas import tpu as pltpu
from jax.experimental.pallas import tpu_sc as plsc

F32 = jnp.float32
BF16 = jnp.bfloat16

D_MODEL = 1024
HG_WIDTH = 512
HEAD_DIM = 128
N_HEADS = 4
CONV_WIDTH = 512
D_FF = 2816
CHUNK = 32
EPS = 1e-6
Q_SCALE = HEAD_DIM ** -0.5
N_DEV = 8

ADAM_LR = 0.001
ADAM_B1 = 0.9
ADAM_B2 = 0.999
ADAM_EPS = 1e-08
ADAM_WD = 0.01
ADAM_STEP = 10

VMEM_LIMIT_V7X = 56 * 1024 * 1024

PACK_SPLITS = (("w_in", 704), ("w_branch_a", 64), ("w_branch_b", 64), ("w_out", 128),
               ("w_ffn_gate", 352), ("w_ffn_up", 352), ("w_ffn_down", 352))
PACK_ROWS = sum(r for _, r in PACK_SPLITS)
PACK_OFF = {}
_o = 0
for _n, _r in PACK_SPLITS:
    PACK_OFF[_n] = (_o, _r)
    _o += _r

SMALL_ROWS = 64


def _params(sem, vmem=VMEM_LIMIT_V7X):
    return pltpu.CompilerParams(dimension_semantics=sem, vmem_limit_bytes=vmem)


def _mm(a, b):
    return jnp.dot(a.astype(BF16), b.astype(BF16), preferred_element_type=F32)


def _mm_nt(a, b):
    return lax.dot_general(a.astype(BF16), b.astype(BF16), (((1,), (1,)), ((), ())), preferred_element_type=F32)


def _mm_tn(a, b):
    return lax.dot_general(a.astype(BF16), b.astype(BF16), (((0,), (0,)), ((), ())), preferred_element_type=F32)


def _sigmoid(x):
    return 1.0 / (1.0 + jnp.exp(-x))


def _resident(shape):
    nd = len(shape)
    return pl.BlockSpec(shape, lambda *_: (0,) * nd, pipeline_mode=pl.Buffered(1))


def _full(shape):
    nd = len(shape)
    return pl.BlockSpec(shape, lambda *_: (0,) * nd)


def _shard_cols(w_ref):
    return jnp.concatenate([w_ref[s] for s in range(N_DEV)], axis=1)


N_HG = 4 * HG_WIDTH
N_CV = 3 * CONV_WIDTH
N_GT = 2 * D_MODEL
N_IN = N_HG + N_CV + N_GT


def _col(tm, n):
    return pl.BlockSpec((n, tm), lambda i: (0, i))


def _fwd_in(x, g, w_in):
    T = x.shape[0]
    tm = min(256, T)

    def body(x_ref, g_ref, w_ref, ht_ref, hg_ref, cv_ref, gt_ref):
        xv = x_ref[...]
        r = lax.rsqrt(jnp.mean(xv * xv, axis=-1, keepdims=True) + EPS)
        hf = xv * r * g_ref[...]
        h = hf.astype(BF16)
        ht_ref[...] = hf.T.astype(BF16)
        hg_ref[...] = jnp.dot(h, w_ref[:, :N_HG], preferred_element_type=F32)
        cv_ref[...] = jnp.dot(h, w_ref[:, N_HG:N_HG + N_CV], preferred_element_type=F32)
        gt_ref[...] = jnp.dot(h, w_ref[:, N_HG + N_CV:], preferred_element_type=F32)

    row = lambda n: pl.BlockSpec((tm, n), lambda i: (i, 0))
    return pl.pallas_call(
        body, name="fwd_in", grid=(T // tm,),
        in_specs=[row(D_MODEL), _full((1, D_MODEL)), _resident(w_in.shape)],
        out_specs=[_col(tm, D_MODEL), row(N_HG), row(N_CV), row(N_GT)],
        out_shape=[jax.ShapeDtypeStruct((D_MODEL, T), BF16), jax.ShapeDtypeStruct((T, N_HG), F32),
                   jax.ShapeDtypeStruct((T, N_CV), F32), jax.ShapeDtypeStruct((T, N_GT), F32)],
        compiler_params=_params(("parallel",)),
    )(x, g, w_in)


def _chunk_pos(shape):
    return lax.broadcasted_iota(jnp.int32, shape, 0) & (CHUNK - 1)


def _chunk_cumsum(x, pos):
    s = 1
    while s < CHUNK:
        x = x + jnp.where(pos >= s, pltpu.roll(x, s, 0), 0.0)
        s *= 2
    return x


def _chunk_rev_cumsum(x, pos):
    n = x.shape[0]
    s = 1
    while s < CHUNK:
        x = x + jnp.where(pos + s < CHUNK, pltpu.roll(x, n - s, 0), 0.0)
        s *= 2
    return x


def _chunk_bcast(x3, row, tb):
    return jnp.broadcast_to(x3[:, row:row + 1, :], x3.shape).reshape(tb, x3.shape[-1])


def _lower_bound(low_ref):
    l0 = low_ref[0:1, :]
    l1 = low_ref[1:2, :]
    m = jnp.maximum(l0, l1)
    e0 = jnp.exp(l0 - m)
    e1 = jnp.exp(l1 - m)
    return e0 / (e0 + e1), e1 / (e0 + e1)


def _hg_gates(qr, fr, lb, pos, tb):
    sq = _sigmoid(qr)
    q = qr * sq * Q_SCALE
    sg = _sigmoid(fr)
    f = lb + (1.0 - lb) * sg
    k = 1.0 - f
    b = _chunk_cumsum(jnp.log(f), pos)
    b3 = b.reshape(tb // CHUNK, CHUNK, HEAD_DIM)
    anc = _chunk_bcast(b3, CHUNK // 2 - 1, tb)
    blb = _chunk_bcast(b3, CHUNK - 1, tb)
    e_qa = jnp.exp(b - anc)
    e_ka = jnp.exp(anc - b)
    e_b = jnp.exp(b)
    e_ko = jnp.exp(blb - b)
    dec = jnp.exp(blb)
    return sq, q, sg, f, k, e_qa, e_ka, e_b, e_ko, dec


def _intra_mask(sb):
    r = lax.broadcasted_iota(jnp.int32, (sb, sb), 0)
    c = lax.broadcasted_iota(jnp.int32, (sb, sb), 1)
    return ((r // CHUNK) == (c // CHUNK)) & (c <= r)


def _hg_fwd(hg, low, gn):
    T = hg.shape[0]
    tb = min(512, T)
    sb = min(256, tb)
    nb = T // tb
    nc = tb // CHUNK

    def body(q_ref, f_ref, i_ref, g_ref, low_ref, gn_ref, o_ref, og_ref, ogt_ref, st_ref, s_scr):
        t = pl.program_id(1)

        @pl.when(t == 0)
        def _():
            s_scr[...] = jnp.zeros_like(s_scr)

        pos = _chunk_pos((tb, HEAD_DIM))
        lb, _ = _lower_bound(low_ref)
        v = i_ref[...]
        _, q, _, _, k, e_qa, e_ka, e_b, e_ko, dec = _hg_gates(q_ref[...], f_ref[...], lb, pos, tb)
        qh = (q * e_qa).astype(BF16)
        kh = (k * e_ka).astype(BF16)
        qi = (q * e_b).astype(BF16)
        ko = (k * e_ko).astype(BF16)
        vb = v.astype(BF16)
        mask = _intra_mask(sb)
        for s in range(tb // sb):
            sl = slice(s * sb, (s + 1) * sb)
            p = jnp.where(mask, _mm_nt(qh[sl], kh[sl]), 0.0)
            o_ref[sl, :] = _mm(p, vb[sl])
        st = s_scr[...]
        for c in range(nc):
            sl = slice(c * CHUNK, (c + 1) * CHUNK)
            st_ref[c] = st
            o_ref[sl, :] = o_ref[sl, :] + _mm_nt(qi[sl], st)
            st = dec[c * CHUNK:c * CHUNK + 1, :] * st + _mm_tn(vb[sl], ko[sl])
        s_scr[...] = st
        o = o_ref[...]
        r = lax.rsqrt(jnp.mean(o * o, axis=-1, keepdims=True) + EPS)
        gr = g_ref[...]
        og = (o * r * gn_ref[...]) * (gr * _sigmoid(gr))
        og_ref[...] = og.astype(BF16)
        ogt_ref[...] = og.T.astype(BF16)

    col = lambda p: pl.BlockSpec((tb, HEAD_DIM), lambda h, t: (t, p * N_HEADS + h))
    hcol = pl.BlockSpec((tb, HEAD_DIM), lambda h, t: (t, h))
    return pl.pallas_call(
        body, name="hg_fwd", grid=(N_HEADS, nb),
        in_specs=[col(0), col(1), col(2), col(3), pl.BlockSpec((2, HEAD_DIM), lambda h, t: (0, h)),
                  pl.BlockSpec((1, HEAD_DIM), lambda h, t: (0, 0))],
        out_specs=[hcol, hcol, pl.BlockSpec((HEAD_DIM, tb), lambda h, t: (h, t)),
                   pl.BlockSpec((None, nc, HEAD_DIM, HEAD_DIM), lambda h, t: (h, t, 0, 0))],
        out_shape=[jax.ShapeDtypeStruct((T, HG_WIDTH), F32), jax.ShapeDtypeStruct((T, HG_WIDTH), BF16),
                   jax.ShapeDtypeStruct((HG_WIDTH, T), BF16),
                   jax.ShapeDtypeStruct((N_HEADS, T // CHUNK, HEAD_DIM, HEAD_DIM), F32)],
        scratch_shapes=[pltpu.VMEM((HEAD_DIM, HEAD_DIM), F32)],
        compiler_params=_params(("parallel", "arbitrary")),
    )(hg, hg, hg, hg, low, gn)


def _conv_fwd(cv, conv_w):
    T = cv.shape[0]
    nj = CONV_WIDTH // 128

    def body(c_ref, b_ref, x_ref, w_ref, o_ref, ot_ref):
        row = lax.broadcasted_iota(jnp.int32, (T, 128), 0)
        u = c_ref[...] * x_ref[...]
        u1 = jnp.where(row >= 1, pltpu.roll(u, 1, 0), 0.0)
        u2 = jnp.where(row >= 2, pltpu.roll(u, 2, 0), 0.0)
        y = w_ref[0:1, :] * u2 + w_ref[1:2, :] * u1 + w_ref[2:3, :] * u
        out = b_ref[...] * y
        o_ref[...] = out.astype(BF16)
        ot_ref[...] = out.T.astype(BF16)

    col = lambda p: pl.BlockSpec((T, 128), lambda j: (0, p * nj + j))
    return pl.pallas_call(
        body, name="conv_fwd", grid=(nj,),
        in_specs=[col(0), col(1), col(2), pl.BlockSpec((3, 128), lambda j: (0, j))],
        out_specs=[pl.BlockSpec((T, 128), lambda j: (0, j)), pl.BlockSpec((128, T), lambda j: (j, 0))],
        out_shape=[jax.ShapeDtypeStruct((T, CONV_WIDTH), BF16), jax.ShapeDtypeStruct((CONV_WIDTH, T), BF16)],
        compiler_params=_params(("parallel",)),
    )(cv, cv, cv, conv_w)


def _merge_fwd(og, cvo, gt, x, wa, wb, wo):
    T = x.shape[0]
    tm = min(512, T)

    def body(og_ref, cvo_ref, gt_ref, x_ref, wa_ref, wb_ref, wo_ref, x1_ref, mgt_ref):
        ya = jnp.dot(og_ref[...], _shard_cols(wa_ref), preferred_element_type=F32)
        yb = jnp.dot(cvo_ref[...], _shard_cols(wb_ref), preferred_element_type=F32)
        m = _sigmoid(gt_ref[:, :D_MODEL]) * ya + _sigmoid(gt_ref[:, D_MODEL:]) * yb
        mgt_ref[...] = m.T.astype(BF16)
        x1_ref[...] = x_ref[...] + jnp.dot(m.astype(BF16), wo_ref[...], preferred_element_type=F32)

    row = lambda n: pl.BlockSpec((tm, n), lambda i: (i, 0))
    return pl.pallas_call(
        body, name="merge_fwd", grid=(T // tm,),
        in_specs=[row(HG_WIDTH), row(CONV_WIDTH), row(2 * D_MODEL), row(D_MODEL),
                  _resident(wa.shape), _resident(wb.shape), _resident(wo.shape)],
        out_specs=[row(D_MODEL), _col(tm, D_MODEL)],
        out_shape=[jax.ShapeDtypeStruct((T, D_MODEL), F32), jax.ShapeDtypeStruct((D_MODEL, T), BF16)],
        compiler_params=_params(("parallel",)),
    )(og, cvo, gt, x, wa, wb, wo)


def _ffn_fwd(x1, g, wg, wu, wd):
    T = x1.shape[0]
    tm = min(256, T)

    def body(x_ref, g_ref, wg_ref, wu_ref, wd_ref, ht_ref, gate_ref, up_ref, actt_ref, x2_ref):
        xv = x_ref[...]
        r = lax.rsqrt(jnp.mean(xv * xv, axis=-1, keepdims=True) + EPS)
        hf = xv * r * g_ref[...]
        h = hf.astype(BF16)
        ht_ref[...] = hf.T.astype(BF16)
        gate = jnp.dot(h, wg_ref[...], preferred_element_type=F32)
        up = jnp.dot(h, wu_ref[...], preferred_element_type=F32)
        gate_ref[...] = gate
        up_ref[...] = up
        act = gate * _sigmoid(gate) * up
        actt_ref[...] = act.T.astype(BF16)
        x2_ref[...] = xv + jnp.dot(act.astype(BF16), wd_ref[...], preferred_element_type=F32)

    row = lambda n: pl.BlockSpec((tm, n), lambda i: (i, 0))
    return pl.pallas_call(
        body, name="ffn_fwd", grid=(T // tm,),
        in_specs=[row(D_MODEL), _full((1, D_MODEL)), _resident(wg.shape), _resident(wu.shape), _resident(wd.shape)],
        out_specs=[_col(tm, D_MODEL), row(D_FF), row(D_FF), _col(tm, D_FF), row(D_MODEL)],
        out_shape=[jax.ShapeDtypeStruct((D_MODEL, T), BF16), jax.ShapeDtypeStruct((T, D_FF), F32),
                   jax.ShapeDtypeStruct((T, D_FF), F32), jax.ShapeDtypeStruct((D_FF, T), BF16),
                   jax.ShapeDtypeStruct((T, D_MODEL), F32)],
        compiler_params=_params(("parallel",)),
    )(x1, g, wg, wu, wd)


def _final_fwd_bwd(x2, target, g):
    T = x2.shape[0]
    tm = min(512, T)

    def body(x_ref, t_ref, g_ref, loss_ref, dg_ref, dx_ref):
        @pl.when(pl.program_id(0) == 0)
        def _():
            loss_ref[...] = jnp.zeros_like(loss_ref)
            dg_ref[...] = jnp.zeros_like(dg_ref)

        xv = x_ref[...]
        gv = g_ref[...]
        r = lax.rsqrt(jnp.mean(xv * xv, axis=-1, keepdims=True) + EPS)
        xh = xv * r
        err = xh * gv - t_ref[...]
        loss_ref[...] += 0.5 * jnp.sum(jnp.mean(err * err, axis=-1, keepdims=True), axis=0, keepdims=True)
        dy = err * (1.0 / D_MODEL)
        dg_ref[...] += jnp.sum(dy * xh, axis=0, keepdims=True)
        w = dy * gv
        dx_ref[...] = r * (w - xh * jnp.mean(w * xh, axis=-1, keepdims=True))

    row = pl.BlockSpec((tm, D_MODEL), lambda i: (i, 0))
    return pl.pallas_call(
        body, name="final_fwd_bwd", grid=(T // tm,),
        in_specs=[row, row, _full((1, D_MODEL))],
        out_specs=[_full((1, 128)), _full((1, D_MODEL)), row],
        out_shape=[jax.ShapeDtypeStruct((1, 128), F32), jax.ShapeDtypeStruct((1, D_MODEL), F32),
                   jax.ShapeDtypeStruct((T, D_MODEL), F32)],
        compiler_params=_params(("arbitrary",)),
    )(x2, target, g)


def _ffn_bwd(dx2, x1, gate, up, g, wg, wu, wd):
    T = x1.shape[0]
    tm = min(256, T)

    def body(dx2_ref, x_ref, gate_ref, up_ref, g_ref, wg_ref, wu_ref, wd_ref, dgate_ref, dup_ref, dx1_ref, dgn_ref):
        @pl.when(pl.program_id(0) == 0)
        def _():
            dgn_ref[...] = jnp.zeros_like(dgn_ref)

        dx2 = dx2_ref[...]
        dact = _mm_nt(dx2, wd_ref[...])
        gate = gate_ref[...]
        s = _sigmoid(gate)
        dgate = (dact * up_ref[...] * (s * (1.0 + gate * (1.0 - s)))).astype(BF16)
        dup = (dact * (gate * s)).astype(BF16)
        dgate_ref[...] = dgate
        dup_ref[...] = dup
        dh = _mm_nt(dgate, wg_ref[...]) + _mm_nt(dup, wu_ref[...])
        xv = x_ref[...]
        r = lax.rsqrt(jnp.mean(xv * xv, axis=-1, keepdims=True) + EPS)
        xh = xv * r
        dgn_ref[...] += jnp.sum(dh * xh, axis=0, keepdims=True)
        w = dh * g_ref[...]
        dx1_ref[...] = dx2 + r * (w - xh * jnp.mean(w * xh, axis=-1, keepdims=True))

    row = lambda n: pl.BlockSpec((tm, n), lambda i: (i, 0))
    return pl.pallas_call(
        body, name="ffn_bwd", grid=(T // tm,),
        in_specs=[row(D_MODEL), row(D_MODEL), row(D_FF), row(D_FF), _full((1, D_MODEL)),
                  _resident(wg.shape), _resident(wu.shape), _resident(wd.shape)],
        out_specs=[row(D_FF), row(D_FF), row(D_MODEL), _full((1, D_MODEL))],
        out_shape=[jax.ShapeDtypeStruct((T, D_FF), BF16), jax.ShapeDtypeStruct((T, D_FF), BF16),
                   jax.ShapeDtypeStruct((T, D_MODEL), F32), jax.ShapeDtypeStruct((1, D_MODEL), F32)],
        compiler_params=_params(("arbitrary",)),
    )(dx2, x1, gate, up, g, wg, wu, wd)


def _merge_bwd(dx1, og, cvo, gt, wa, wb, wo):
    T = dx1.shape[0]
    tm = min(512, T)

    def body(dx_ref, og_ref, cvo_ref, gt_ref, wa_ref, wb_ref, wo_ref, dgt_ref, dya_ref, dyb_ref, dog_ref, dcvo_ref):
        dm = _mm_nt(dx_ref[...], wo_ref[...])
        wa = _shard_cols(wa_ref)
        wb = _shard_cols(wb_ref)
        ya = jnp.dot(og_ref[...], wa, preferred_element_type=F32)
        yb = jnp.dot(cvo_ref[...], wb, preferred_element_type=F32)
        sa = _sigmoid(gt_ref[:, :D_MODEL])
        sb = _sigmoid(gt_ref[:, D_MODEL:])
        dgt_ref[:, :D_MODEL] = (dm * ya * (sa * (1.0 - sa))).astype(BF16)
        dgt_ref[:, D_MODEL:] = (dm * yb * (sb * (1.0 - sb))).astype(BF16)
        dya = (dm * sa).astype(BF16)
        dyb = (dm * sb).astype(BF16)
        dya_ref[...] = dya
        dyb_ref[...] = dyb
        dog_ref[...] = _mm_nt(dya, wa)
        dcvo_ref[...] = _mm_nt(dyb, wb)

    row = lambda n: pl.BlockSpec((tm, n), lambda i: (i, 0))
    return pl.pallas_call(
        body, name="merge_bwd", grid=(T // tm,),
        in_specs=[row(D_MODEL), row(HG_WIDTH), row(CONV_WIDTH), row(2 * D_MODEL),
                  _resident(wa.shape), _resident(wb.shape), _resident(wo.shape)],
        out_specs=[row(2 * D_MODEL), row(D_MODEL), row(D_MODEL), row(HG_WIDTH), row(CONV_WIDTH)],
        out_shape=[jax.ShapeDtypeStruct((T, 2 * D_MODEL), BF16), jax.ShapeDtypeStruct((T, D_MODEL), BF16),
                   jax.ShapeDtypeStruct((T, D_MODEL), BF16), jax.ShapeDtypeStruct((T, HG_WIDTH), F32),
                   jax.ShapeDtypeStruct((T, CONV_WIDTH), F32)],
        compiler_params=_params(("parallel",)),
    )(dx1, og, cvo, gt, wa, wb, wo)


def _conv_bwd(dcvo, cv, conv_w):
    T = cv.shape[0]
    nj = CONV_WIDTH // 128

    def body(do_ref, c_ref, b_ref, x_ref, w_ref, dc_ref, db_ref, dx_ref, dw_ref):
        row = lax.broadcasted_iota(jnp.int32, (T, 128), 0)
        c = c_ref[...]
        xb = x_ref[...]
        do = do_ref[...]
        u = c * xb
        u1 = jnp.where(row >= 1, pltpu.roll(u, 1, 0), 0.0)
        u2 = jnp.where(row >= 2, pltpu.roll(u, 2, 0), 0.0)
        w0, w1, w2 = w_ref[0:1, :], w_ref[1:2, :], w_ref[2:3, :]
        y = w0 * u2 + w1 * u1 + w2 * u
        db_ref[...] = (do * y).astype(BF16)
        dy = do * b_ref[...]
        dw_ref[0:1, :] = jnp.sum(dy * u2, axis=0, keepdims=True)
        dw_ref[1:2, :] = jnp.sum(dy * u1, axis=0, keepdims=True)
        dw_ref[2:3, :] = jnp.sum(dy * u, axis=0, keepdims=True)
        dy1 = jnp.where(row < T - 1, pltpu.roll(dy, T - 1, 0), 0.0)
        dy2 = jnp.where(row < T - 2, pltpu.roll(dy, T - 2, 0), 0.0)
        du = w2 * dy + w1 * dy1 + w0 * dy2
        dc_ref[...] = (du * xb).astype(BF16)
        dx_ref[...] = (du * c).astype(BF16)

    col = lambda p: pl.BlockSpec((T, 128), lambda j: (0, p * nj + j))
    one = pl.BlockSpec((T, 128), lambda j: (0, j))
    wspec = pl.BlockSpec((3, 128), lambda j: (0, j))
    out = jax.ShapeDtypeStruct((T, CONV_WIDTH), BF16)
    return pl.pallas_call(
        body, name="conv_bwd", grid=(nj,),
        in_specs=[one, col(0), col(1), col(2), wspec],
        out_specs=[one, one, one, wspec],
        out_shape=[out, out, out, jax.ShapeDtypeStruct((3, CONV_WIDTH), F32)],
        compiler_params=_params(("parallel",)),
    )(dcvo, cv, cv, cv, conv_w)


def _drop_operands(body, first, count):
    def wrapped(*refs):
        return body(*refs[:first], *refs[first + count:])
    return wrapped


def _hg_bwd(dog, hg, o, st, low, gn, after=()):
    T = hg.shape[0]
    tb = min(512, T)
    sb = min(256, tb)
    nb = T // tb
    nc = tb // CHUNK

    def body(q_ref, f_ref, i_ref, g_ref, low_ref, gn_ref, o_ref, dog_ref, st_ref,
             dq_ref, df_ref, di_ref, dg_ref, dlow_ref, dgn_ref,
             ds_scr, dqi_scr, dko_scr, dv_scr, dd_scr, dqh_scr, dkh_scr):
        h = pl.program_id(0)
        t = pl.program_id(1)

        @pl.when(t == 0)
        def _():
            ds_scr[...] = jnp.zeros_like(ds_scr)
            dlow_ref[...] = jnp.zeros_like(dlow_ref)

        @pl.when((t == 0) & (h == 0))
        def _():
            dgn_ref[...] = jnp.zeros_like(dgn_ref)

        pos = _chunk_pos((tb, HEAD_DIM))
        lb, lb1 = _lower_bound(low_ref)
        qr = q_ref[...]
        v = i_ref[...]
        sq, q, sg, f, k, e_qa, e_ka, e_b, e_ko, dec = _hg_gates(qr, f_ref[...], lb, pos, tb)

        gr = g_ref[...]
        gnv = gn_ref[...]
        o = o_ref[...]
        dog_v = dog_ref[...]
        sgr = _sigmoid(gr)
        r = lax.rsqrt(jnp.mean(o * o, axis=-1, keepdims=True) + EPS)
        oh = o * r
        dg_ref[...] = (dog_v * (oh * gnv) * (sgr * (1.0 + gr * (1.0 - sgr)))).astype(BF16)
        don = dog_v * (gr * sgr)
        dgn_ref[...] += jnp.sum(don * oh, axis=0, keepdims=True)
        w = don * gnv
        do = (r * (w - oh * jnp.mean(w * oh, axis=-1, keepdims=True))).astype(BF16)

        qh = (q * e_qa).astype(BF16)
        kh = (k * e_ka).astype(BF16)
        qi = (q * e_b).astype(BF16)
        ko = (k * e_ko).astype(BF16)
        vb = v.astype(BF16)

        mask = _intra_mask(sb)
        for s in range(tb // sb):
            sl = slice(s * sb, (s + 1) * sb)
            p = jnp.where(mask, _mm_nt(qh[sl], kh[sl]), 0.0).astype(BF16)
            dp = jnp.where(mask, _mm_nt(do[sl], vb[sl]), 0.0).astype(BF16)
            dv_scr[sl, :] = _mm_tn(p, do[sl])
            dqh_scr[sl, :] = _mm(dp, kh[sl])
            dkh_scr[sl, :] = _mm_tn(dp, qh[sl])

        ds = ds_scr[...]
        for c in reversed(range(nc)):
            sl = slice(c * CHUNK, (c + 1) * CHUNK)
            st_c = st_ref[c]
            dqi_scr[sl, :] = _mm(do[sl], st_c)
            dko_scr[sl, :] = _mm(vb[sl], ds)
            dv_scr[sl, :] = dv_scr[sl, :] + _mm_nt(ko[sl], ds)
            dd_scr[sl, :] = jnp.broadcast_to(jnp.sum(ds * st_c, axis=0, keepdims=True), (CHUNK, HEAD_DIM))
            ds = dec[c * CHUNK:c * CHUNK + 1, :] * ds + _mm_tn(do[sl], qi[sl])
        ds_scr[...] = ds

        dko_e = dko_scr[...] * e_ko
        dq = dqh_scr[...] * e_qa + dqi_scr[...] * e_b
        dk = dkh_scr[...] * e_ka + dko_e
        kd3 = (k * dko_e).reshape(nc, CHUNK, HEAD_DIM)
        last = jnp.broadcast_to(jnp.sum(kd3, axis=1, keepdims=True), kd3.shape).reshape(tb, HEAD_DIM)
        db = q * dq - k * dk + jnp.where(pos == CHUNK - 1, dec * dd_scr[...] + last, 0.0)
        dlg = _chunk_rev_cumsum(db, pos)
        dfv = dlg / f - dk
        s_low = jnp.sum(dfv * (1.0 - sg), axis=0, keepdims=True)
        dlow_ref[0:1, :] += s_low * lb * (1.0 - lb)
        dlow_ref[1:2, :] += -s_low * lb * lb1
        df_ref[...] = (dfv * (1.0 - lb) * sg * (1.0 - sg)).astype(BF16)
        dq_ref[...] = (dq * Q_SCALE * (sq * (1.0 + qr * (1.0 - sq)))).astype(BF16)
        di_ref[...] = dv_scr[...].astype(BF16)

    rt = lambda t: nb - 1 - t
    col = lambda p: pl.BlockSpec((tb, HEAD_DIM), lambda h, t: (rt(t), p * N_HEADS + h))
    hcol = pl.BlockSpec((tb, HEAD_DIM), lambda h, t: (rt(t), h))
    piece = jax.ShapeDtypeStruct((T, HG_WIDTH), BF16)
    tile = pltpu.VMEM((tb, HEAD_DIM), F32)
    return pl.pallas_call(
        _drop_operands(body, 9, len(after)), name="hg_bwd", grid=(N_HEADS, nb),
        in_specs=[col(0), col(1), col(2), col(3), pl.BlockSpec((2, HEAD_DIM), lambda h, t: (0, h)),
                  pl.BlockSpec((1, HEAD_DIM), lambda h, t: (0, 0)), hcol, hcol,
                  pl.BlockSpec((None, nc, HEAD_DIM, HEAD_DIM), lambda h, t: (h, rt(t), 0, 0))]
                 + [HBM_SPEC] * len(after),
        out_specs=[hcol, hcol, hcol, hcol, pl.BlockSpec((2, HEAD_DIM), lambda h, t: (0, h)),
                   pl.BlockSpec((1, HEAD_DIM), lambda h, t: (0, 0))],
        out_shape=[piece, piece, piece, piece, jax.ShapeDtypeStruct((2, HG_WIDTH), F32),
                   jax.ShapeDtypeStruct((1, HEAD_DIM), F32)],
        scratch_shapes=[pltpu.VMEM((HEAD_DIM, HEAD_DIM), F32), tile, tile, tile, tile, tile, tile],
        compiler_params=_params(("arbitrary", "arbitrary")),
    )(hg, hg, hg, hg, low, gn, o, dog, st, *after)


def _in_bwd(dparts, w_in, x, dx1, g, after=()):
    T = x.shape[0]
    tm = min(256, T)
    widths = [p.shape[1] for p in dparts]
    offs = [sum(widths[:i]) for i in range(len(widths))]
    n = len(dparts)

    def body(*refs):
        d_refs = refs[:n]
        w_ref, x_ref, dx1_ref, g_ref, dx_ref, dgn_ref = refs[n:]

        @pl.when(pl.program_id(0) == 0)
        def _():
            dgn_ref[...] = jnp.zeros_like(dgn_ref)

        dh = None
        for d_ref, off, wd in zip(d_refs, offs, widths):
            part = _mm_nt(d_ref[...], w_ref[:, off:off + wd])
            dh = part if dh is None else dh + part
        xv = x_ref[...]
        r = lax.rsqrt(jnp.mean(xv * xv, axis=-1, keepdims=True) + EPS)
        xh = xv * r
        dgn_ref[...] += jnp.sum(dh * xh, axis=0, keepdims=True)
        w = dh * g_ref[...]
        dx_ref[...] = dx1_ref[...] + r * (w - xh * jnp.mean(w * xh, axis=-1, keepdims=True))

    row = lambda m: pl.BlockSpec((tm, m), lambda i: (i, 0))
    return pl.pallas_call(
        _drop_operands(body, n + 4, len(after)), name="in_bwd", grid=(T // tm,),
        in_specs=[row(wd) for wd in widths] + [_resident(w_in.shape), row(D_MODEL), row(D_MODEL), _full((1, D_MODEL))]
                 + [HBM_SPEC] * len(after),
        out_specs=[row(D_MODEL), _full((1, D_MODEL))],
        out_shape=[jax.ShapeDtypeStruct((T, D_MODEL), F32), jax.ShapeDtypeStruct((1, D_MODEL), F32)],
        compiler_params=_params(("arbitrary",)),
    )(*dparts, w_in, x, dx1, g, *after)


def _wgrad(name, at, b, tn):
    M, T = at.shape
    N = b.shape[1]
    tk = min(1024, T)
    nk = T // tk

    def body(a_ref, b_ref, o_ref, acc):
        k = pl.program_id(1)
        part = _mm(a_ref[...], b_ref[...])

        @pl.when(k == 0)
        def _():
            acc[...] = part

        @pl.when(k != 0)
        def _():
            acc[...] += part

        @pl.when(k == nk - 1)
        def _():
            o_ref[...] = acc[...].astype(BF16)

    return pl.pallas_call(
        body, name=name, grid=(N // tn, nk),
        in_specs=[pl.BlockSpec((M, tk), lambda j, k: (0, k)), pl.BlockSpec((tk, tn), lambda j, k: (k, j))],
        out_specs=pl.BlockSpec((M, tn), lambda j, k: (0, j)),
        out_shape=jax.ShapeDtypeStruct((M, N), BF16),
        scratch_shapes=[pltpu.VMEM((M, tn), F32)],
        compiler_params=_params(("parallel", "arbitrary")),
    )(at, b)


def _wgrad_in(ht, dparts, after=()):
    M, T = ht.shape
    tn = 512
    tk = min(1024, T)
    nk = T // tk
    nblk = [p.shape[1] // tn for p in dparts]
    start = [sum(nblk[:i]) for i in range(len(nblk))]
    n = len(dparts)

    def body(a_ref, *refs):
        d_refs, o_ref, acc = refs[:n], refs[n], refs[n + 1]
        j = pl.program_id(0)
        k = pl.program_id(1)

        @pl.when(k == 0)
        def _():
            acc[...] = jnp.zeros_like(acc)

        for d_ref, s, nb in zip(d_refs, start, nblk):
            @pl.when((j >= s) & (j < s + nb))
            def _():
                acc[...] += _mm(a_ref[...], d_ref[...])

        @pl.when(k == nk - 1)
        def _():
            o_ref[...] = acc[...].astype(BF16)

    def piece_spec(s, nb):
        def index(j, k):
            inside = (j >= s) & (j < s + nb)
            return jnp.where(inside, k, 0), jnp.clip(j - s, 0, nb - 1)
        return pl.BlockSpec((tk, tn), index)

    return pl.pallas_call(
        _drop_operands(body, 1 + n, len(after)), name="wgrad_in", grid=(sum(nblk), nk),
        in_specs=[pl.BlockSpec((M, tk), lambda j, k: (0, k))] + [piece_spec(s, nb) for s, nb in zip(start, nblk)]
                 + [HBM_SPEC] * len(after),
        out_specs=pl.BlockSpec((M, tn), lambda j, k: (0, j)),
        out_shape=jax.ShapeDtypeStruct((M, sum(nblk) * tn), BF16),
        scratch_shapes=[pltpu.VMEM((M, tn), F32)],
        compiler_params=_params(("parallel", "arbitrary")),
    )(ht, *dparts, *after)


def _adamw_math(w, g, m, v):
    m = ADAM_B1 * m + (1.0 - ADAM_B1) * g
    v = ADAM_B2 * v + (1.0 - ADAM_B2) * (g * g)
    m_hat = m / (1.0 - ADAM_B1 ** ADAM_STEP)
    v_hat = v / (1.0 - ADAM_B2 ** ADAM_STEP)
    delta = -ADAM_LR * (m_hat / (jnp.sqrt(v_hat) + ADAM_EPS) + ADAM_WD * w)
    return delta, m, v


def _adamw_sum(name, w, parts, m, v):
    R, C = w.shape
    tr = _row_tile(R)

    def body(w_ref, p_ref, m_ref, v_ref, g_out, d_out, m_out, v_out):
        g = p_ref[0].astype(F32)
        for k in range(1, 4):
            g = g + p_ref[k].astype(F32)
        g_out[...] = g
        d_out[...], m_out[...], v_out[...] = _adamw_math(w_ref[...], g, m_ref[...], v_ref[...])

    blk = pl.BlockSpec((tr, C), lambda i: (i, 0))
    out = jax.ShapeDtypeStruct((R, C), F32)
    return pl.pallas_call(
        body, name=name, grid=(R // tr,),
        in_specs=[blk, pl.BlockSpec((4, tr, C), lambda i: (0, i, 0)), blk, blk],
        out_specs=[blk, blk, blk, blk], out_shape=[out, out, out, out],
        compiler_params=_params(("parallel",)),
    )(w, parts, m, v)


def _small_sum(gathered):
    R = gathered.shape[1]

    def body(p_ref, o_ref):
        g = p_ref[0]
        for k in range(1, N_DEV):
            g = g + p_ref[k]
        o_ref[...] = g

    return pl.pallas_call(
        body, name="small_sum", in_specs=[_full(gathered.shape)], out_specs=_full((R, 128)), grid=(1,),
        out_shape=jax.ShapeDtypeStruct((R, 128), F32),
    )(gathered)


def _small_adamw(w, g, m, v):
    def body(w_ref, g_ref, m_ref, v_ref, d_out, m_out, v_out):
        d_out[...], m_out[...], v_out[...] = _adamw_math(w_ref[...], g_ref[...], m_ref[...], v_ref[...])

    out = jax.ShapeDtypeStruct(w.shape, F32)
    spec = _full(w.shape)
    return pl.pallas_call(
        body, name="small_adamw", grid=(1,), in_specs=[spec] * 4, out_specs=[spec] * 3, out_shape=[out, out, out],
    )(w, g, m, v)


def _row_tile(rows):
    for cand in (256, 128):
        if rows % cand == 0:
            return cand
    return rows


def _pair_sum(name, by_owner, got, core):
    _, R, C = got.shape
    tr = _row_tile(R)

    def body(core_ref, a_ref, b_ref, o_ref):
        o_ref[...] = (a_ref[...].astype(F32) + b_ref[...].astype(F32)).astype(BF16)

    blk = pl.BlockSpec((None, tr, C), lambda k, i, core_ref: (k, i, 0))
    mine = pl.BlockSpec((None, tr, C), lambda k, i, core_ref: (2 * k + core_ref[0], i, 0))
    return pl.pallas_call(
        body, name=name,
        grid_spec=pltpu.PrefetchScalarGridSpec(num_scalar_prefetch=1, grid=(4, R // tr), in_specs=[mine, blk],
                                               out_specs=blk),
        out_shape=jax.ShapeDtypeStruct(got.shape, BF16),
        compiler_params=_params(("parallel", "parallel")),
    )(core, by_owner, got)


def _cols_from_shards(name, g):
    _, R, c = g.shape
    tr = _row_tile(R)

    def body(g_ref, o_ref):
        for s in range(N_DEV):
            o_ref[:, s * c:(s + 1) * c] = g_ref[s]

    return pl.pallas_call(
        body, name=name, grid=(R // tr,),
        in_specs=[pl.BlockSpec((N_DEV, tr, c), lambda i: (0, i, 0))],
        out_specs=pl.BlockSpec((tr, N_DEV * c), lambda i: (i, 0)),
        out_shape=jax.ShapeDtypeStruct((R, N_DEV * c), g.dtype),
        compiler_params=_params(("parallel",)),
    )(g)


def _shards_from_cols(name, full):
    R, allc = full.shape
    c = allc // N_DEV
    tr = _row_tile(R)

    def body(f_ref, o_ref):
        for s in range(N_DEV):
            o_ref[s] = f_ref[:, s * c:(s + 1) * c]

    return pl.pallas_call(
        body, name=name, grid=(R // tr,),
        in_specs=[pl.BlockSpec((tr, allc), lambda i: (i, 0))],
        out_specs=pl.BlockSpec((N_DEV, tr, c), lambda i: (0, i, 0)),
        out_shape=jax.ShapeDtypeStruct((N_DEV, R, c), full.dtype),
        compiler_params=_params(("parallel",)),
    )(full)


MESH = pl.DeviceIdType.MESH
HBM_SPEC = pl.BlockSpec(memory_space=pl.ANY)


def _handshake(peers):
    barrier = pltpu.get_barrier_semaphore()
    for peer in peers:
        pl.semaphore_signal(barrier, inc=1, device_id=peer, device_id_type=MESH)
    pl.semaphore_wait(barrier, len(peers))


def _comm_call(body, name, operands, out_shape, scratch, collective_id):
    if collective_id is None:
        return pl.pallas_call(body, name=name, in_specs=[HBM_SPEC] * len(operands), out_specs=[HBM_SPEC] * len(out_shape),
                              out_shape=out_shape, scratch_shapes=scratch)(*operands)
    return pl.kernel(body, out_type=out_shape, mesh=plsc.ScalarSubcoreMesh(axis_name="sequencer", num_cores=1),
                     scratch_types=scratch, name=name,
                     compiler_params=pltpu.CompilerParams(collective_id=collective_id))(*operands)


def _all_gather(name, blocks, collective_id=None, after=()):
    n = len(blocks)
    na = len(after)

    def body(*refs):
        x_refs, out_refs = refs[:n], refs[n + na:2 * n + na]
        send_sems, recv_sems, local_sems = refs[2 * n + na:]
        x, y, c = lax.axis_index("x"), lax.axis_index("y"), lax.axis_index("c")
        me, sibling = (x, y, c), (x, y, 1 - c)
        chips = [(1 - x, y), (x, 1 - y), (1 - x, 1 - y)]
        if collective_id is not None:
            _handshake([sibling] + [(*chip, c) for chip in chips])

        def slot(i, px, py, pc):
            return out_refs[i].at[4 * px + 2 * py + pc]

        def copy(i, k, blk, to, src=None):
            return pltpu.make_async_remote_copy(
                src_ref=slot(i, *blk) if src is None else src, dst_ref=slot(i, *blk),
                send_sem=send_sems.at[7 * i + k], recv_sem=recv_sems.at[7 * i + k], device_id=to, device_id_type=MESH)

        mine = [pltpu.make_async_copy(x_refs[i], slot(i, *me), local_sems.at[i]) for i in range(n)]
        for cp in mine:
            cp.start()
        first = []
        for i in range(n):
            first.append(copy(i, 0, me, sibling, src=x_refs[i]))
            first += [copy(i, 1 + j, me, (*chip, c), src=x_refs[i]) for j, chip in enumerate(chips)]
        for cp in first:
            cp.start()
        passed = []
        for i in range(n):
            for j, chip in enumerate(chips):
                copy(i, 1 + j, (*chip, c), me).wait_recv()
                passed.append(copy(i, 4 + j, (*chip, c), sibling))
                passed[-1].start()
        for i in range(n):
            copy(i, 0, sibling, me).wait_recv()
            for j, chip in enumerate(chips):
                copy(i, 4 + j, (*chip, 1 - c), me).wait_recv()
        for cp in first + passed:
            cp.wait_send()
        for cp in mine:
            cp.wait()

    return _comm_call(
        body, name, list(blocks) + list(after), [jax.ShapeDtypeStruct((N_DEV,) + b.shape, b.dtype) for b in blocks],
        [pltpu.SemaphoreType.DMA((7 * n,)), pltpu.SemaphoreType.DMA((7 * n,)), pltpu.SemaphoreType.DMA((n,))],
        collective_id)


def _sibling_swap(name, by_owner, collective_id=None, after=()):
    n = len(by_owner)
    na = len(after)

    def body(*refs):
        x_refs, out_refs = refs[:n], refs[n + na:2 * n + na]
        send_sems, recv_sems = refs[2 * n + na:]
        x, y, c = lax.axis_index("x"), lax.axis_index("y"), lax.axis_index("c")
        if collective_id is not None:
            _handshake([(x, y, 1 - c)])
        copies = []
        for i in range(n):
            for k in range(4):
                copies.append(pltpu.make_async_remote_copy(
                    src_ref=x_refs[i].at[2 * k + 1 - c], dst_ref=out_refs[i].at[k],
                    send_sem=send_sems.at[4 * i + k], recv_sem=recv_sems.at[4 * i + k],
                    device_id=(x, y, 1 - c), device_id_type=MESH))
        for cp in copies:
            cp.start()
        for cp in copies:
            cp.wait()

    return _comm_call(
        body, name, list(by_owner) + list(after),
        [jax.ShapeDtypeStruct((4,) + b.shape[1:], b.dtype) for b in by_owner],
        [pltpu.SemaphoreType.DMA((4 * n,)), pltpu.SemaphoreType.DMA((4 * n,))], collective_id)


def _chip_exchange(name, sums, collective_id=None, after=()):
    n = len(sums)
    na = len(after)

    def body(*refs):
        x_refs, out_refs = refs[:n], refs[n + na:2 * n + na]
        send_sems, recv_sems, local_sems = refs[2 * n + na:]
        x, y, c = lax.axis_index("x"), lax.axis_index("y"), lax.axis_index("c")
        chips = [(1 - x, y), (x, 1 - y), (1 - x, 1 - y)]
        my_chip = 2 * x + y
        if collective_id is not None:
            _handshake([(cx, cy, c) for cx, cy in chips])
        mine = [pltpu.make_async_copy(x_refs[i].at[my_chip], out_refs[i].at[my_chip], local_sems.at[i])
                for i in range(n)]
        for cp in mine:
            cp.start()
        sends = []
        for i in range(n):
            for j, (cx, cy) in enumerate(chips):
                sends.append(pltpu.make_async_remote_copy(
                    src_ref=x_refs[i].at[2 * cx + cy], dst_ref=out_refs[i].at[my_chip],
                    send_sem=send_sems.at[3 * i + j], recv_sem=recv_sems.at[3 * i + j],
                    device_id=(cx, cy, c), device_id_type=MESH))
        for cp in sends:
            cp.start()
        for i in range(n):
            for j, (cx, cy) in enumerate(chips):
                pltpu.make_async_remote_copy(
                    src_ref=x_refs[i].at[my_chip], dst_ref=out_refs[i].at[2 * cx + cy],
                    send_sem=send_sems.at[3 * i + j], recv_sem=recv_sems.at[3 * i + j],
                    device_id=(cx, cy, c), device_id_type=MESH).wait_recv()
        for cp in sends:
            cp.wait_send()
        for cp in mine:
            cp.wait()

    return _comm_call(
        body, name, list(sums) + list(after), [jax.ShapeDtypeStruct(s.shape, s.dtype) for s in sums],
        [pltpu.SemaphoreType.DMA((3 * n,)), pltpu.SemaphoreType.DMA((3 * n,)), pltpu.SemaphoreType.DMA((n,))],
        collective_id)


def _cast_shards(shards):
    n = len(shards)

    def body(*refs):
        for i in range(n):
            refs[n + i][...] = refs[i][...].astype(BF16)

    vmem = pl.BlockSpec(memory_space=pltpu.VMEM)
    return pl.pallas_call(
        body, name="cast_shards", in_specs=[vmem] * n, out_specs=[vmem] * n,
        out_shape=[jax.ShapeDtypeStruct(s.shape, BF16) for s in shards],
        compiler_params=pltpu.CompilerParams(vmem_limit_bytes=VMEM_LIMIT_V7X),
    )(*shards)


BIG = ("w_in", "w_branch_a", "w_branch_b", "w_out", "w_ffn_gate", "w_ffn_up", "w_ffn_down")


def _local_step(x, target, gains, low, conv_w, wg8, reduce):
    g_mix, g_hg, g_ffn, g_fin = gains
    w_in = _cols_from_shards("join_w_in", wg8["w_in"])
    wg = _cols_from_shards("join_w_ffn_gate", wg8["w_ffn_gate"])
    wu = _cols_from_shards("join_w_ffn_up", wg8["w_ffn_up"])
    wa, wb = wg8["w_branch_a"], wg8["w_branch_b"]
    wo = wg8["w_out"].reshape(D_MODEL, D_MODEL)
    wd = wg8["w_ffn_down"].reshape(D_FF, D_MODEL)

    ht, hg, cv, gt = _fwd_in(x, g_mix, w_in)
    o, og, ogt, st = _hg_fwd(hg, low, g_hg)
    cvo, cvot = _conv_fwd(cv, conv_w)
    x1, mgt = _merge_fwd(og, cvo, gt, x, wa, wb, wo)
    h2t, gate, up, actt, x2 = _ffn_fwd(x1, g_ffn, wg, wu, wd)
    loss, d_gfin, dx2 = _final_fwd_bwd(x2, target, g_fin)

    dgate, dup, dx1, d_gffn = _ffn_bwd(dx2, x1, gate, up, g_ffn, wg, wu, wd)
    ffn = dict(
        w_ffn_down=_wgrad("wgrad_ffn_down", actt, dx2, 512).reshape(N_DEV, D_FF // N_DEV, D_MODEL),
        w_ffn_gate=_shards_from_cols("split_w_ffn_gate", _wgrad("wgrad_ffn_gate", h2t, dgate, 1408)),
        w_ffn_up=_shards_from_cols("split_w_ffn_up", _wgrad("wgrad_ffn_up", h2t, dup, 1408)))
    sums_ffn, got_ffn = reduce.begin(ffn)
    dgt, dya, dyb, dog, dcvo = _merge_bwd(dx1, og, cvo, gt, wa, wb, wo)
    out = dict(
        w_out=_wgrad("wgrad_out", mgt, dx1, 512).reshape(N_DEV, D_MODEL // N_DEV, D_MODEL),
        w_branch_a=_shards_from_cols("split_w_branch_a", _wgrad("wgrad_branch_a", ogt, dya, 512)),
        w_branch_b=_shards_from_cols("split_w_branch_b", _wgrad("wgrad_branch_b", cvot, dyb, 512)))
    sums_out, got_out = reduce.begin(out, after=got_ffn[:1])
    parts_ffn = reduce.finish(ffn, sums_ffn, after=got_out[:1])
    dc, db, dxb, d_conv = _conv_bwd(dcvo, cv, conv_w)
    dq, df, di, dg, d_low, d_ghg = _hg_bwd(dog, hg, o, st, low, g_hg, after=list(sums_ffn) + list(sums_out))
    parts_out = reduce.finish(out, sums_out, after=[parts_ffn[0], dq])
    dparts = [dq, df, di, dg, dc, db, dxb, dgt]
    w_in_grad = dict(w_in=_shards_from_cols("split_w_in", _wgrad_in(ht, dparts, after=parts_ffn[:1])))
    sums_in, _ = reduce.begin(w_in_grad, after=parts_out[:1])
    parts_in = reduce.finish(w_in_grad, sums_in)
    grad_x, d_gmix = _in_bwd(dparts, w_in, x, dx1, g_mix, after=list(parts_out[:1]) + list(sums_in))
    small = dict(norm_mix_g=d_gmix, norm_ffn_g=d_gffn, norm_final_g=d_gfin, lower_bounds=d_low, hg_norm_g=d_ghg,
                 conv_w=d_conv, loss=loss)
    return grad_x, small, parts_in


_SMALL_LAYOUT = (("norm_mix_g", 0, 8), ("norm_ffn_g", 8, 8), ("norm_final_g", 16, 8), ("lower_bounds", 24, 8),
                 ("hg_norm_g", 32, 1))
_LOSS_ROW = 40
_CONV_ROW = 48


def _pad_rows(a, rows):
    return jnp.pad(a, ((0, rows - a.shape[0]), (0, 0)))


def _pack_small(vals, conv_rows):
    parts = [_pad_rows(vals[name].reshape(rows, 128), 8) for name, _, rows in _SMALL_LAYOUT]
    loss = vals["loss"][:, :128] if "loss" in vals else jnp.zeros((1, 128), F32)
    parts.append(_pad_rows(loss, 8))
    parts.append(_pad_rows(conv_rows, SMALL_ROWS - _CONV_ROW))
    return jnp.concatenate(parts, axis=0)


def _conv_shard_rows(a):
    return jnp.pad(a, ((0, 5), (0, 64)))


def kernel(x, norm_mix_g, w_in, lower_bounds, hg_norm_g, conv_w, w_branch_a, w_branch_b, w_out, norm_ffn_g, w_ffn_gate, w_ffn_up, w_ffn_down, norm_final_g, loss_target, m_norm_mix_g, m_w_in, m_lower_bounds, m_hg_norm_g, m_conv_w, m_w_branch_a, m_w_branch_b, m_w_out, m_norm_ffn_g, m_w_ffn_gate, m_w_ffn_up, m_w_ffn_down, m_norm_final_g, v_norm_mix_g, v_w_in, v_lower_bounds, v_hg_norm_g, v_conv_w, v_w_branch_a, v_w_branch_b, v_w_out, v_norm_ffn_g, v_w_ffn_gate, v_w_ffn_up, v_w_ffn_down, v_norm_final_g):
    cx, cy, cc = lax.axis_index("x"), lax.axis_index("y"), lax.axis_index("c")
    my_dev = 4 * cx + 2 * cy + cc

    big = dict(w_in=w_in[0], w_branch_a=w_branch_a[0], w_branch_b=w_branch_b[0], w_out=w_out[0],
               w_ffn_gate=w_ffn_gate[0], w_ffn_up=w_ffn_up[0], w_ffn_down=w_ffn_down[0])
    big_m = dict(w_in=m_w_in[0], w_branch_a=m_w_branch_a[0], w_branch_b=m_w_branch_b[0], w_out=m_w_out[0],
                 w_ffn_gate=m_w_ffn_gate[0], w_ffn_up=m_w_ffn_up[0], w_ffn_down=m_w_ffn_down[0])
    big_v = dict(w_in=v_w_in[0], w_branch_a=v_w_branch_a[0], w_branch_b=v_w_branch_b[0], w_out=v_w_out[0],
                 w_ffn_gate=v_w_ffn_gate[0], w_ffn_up=v_w_ffn_up[0], w_ffn_down=v_w_ffn_down[0])

    shards = dict(zip(BIG, _cast_shards([big[n] for n in BIG])))
    first = _all_gather("gather_w_in", [shards["w_in"], _conv_shard_rows(conv_w[0])])
    later = _all_gather("gather_rest", [shards[n] for n in BIG[1:]], collective_id=1, after=first[1:])
    wg8 = dict(zip(BIG, [first[0]] + list(later)))
    conv_full = first[1][:, :3, :64].transpose(1, 0, 2).reshape(3, CONV_WIDTH)

    core = cc.reshape(1).astype(jnp.int32)
    outs = {}
    ids = iter(range(2, 16))

    class Reduce:
        @staticmethod
        def begin(grads, after=()):
            names = list(grads)
            by_owner = [grads[n] for n in names]
            got = _sibling_swap("sibling_swap_" + names[0], by_owner, collective_id=next(ids), after=after)
            return [_pair_sum("pair_sum_" + n, a, b, core) for n, a, b in zip(names, by_owner, got)], got

        @staticmethod
        def finish(grads, chip_sums, after=()):
            names = list(grads)
            parts = _chip_exchange("chip_exchange_" + names[0], chip_sums, collective_id=next(ids), after=after)
            for n, p in zip(names, parts):
                outs[n] = _adamw_sum("adamw_" + n, big[n], p, big_m[n], big_v[n])
            return parts

    gains = (norm_mix_g, hg_norm_g, norm_ffn_g, norm_final_g.reshape(1, D_MODEL))
    grad_x, small, last = _local_step(x[0], loss_target[0], gains, lower_bounds, conv_full, wg8, Reduce)

    small_all = _all_gather("gather_small", [_pack_small(small, small["conv_w"].reshape(12, 128))],
                            collective_id=next(ids), after=last[:1])
    ssum = _small_sum(small_all[0])
    conv_g_full = ssum[_CONV_ROW:_CONV_ROW + 12].reshape(3, CONV_WIDTH)
    conv_g = lax.dynamic_slice_in_dim(conv_g_full, my_dev * 64, 64, axis=1)
    loss = ssum[_LOSS_ROW, 0]
    g_rows = jnp.concatenate([ssum[:_CONV_ROW], _pad_rows(_conv_shard_rows(conv_g), SMALL_ROWS - _CONV_ROW)], axis=0)

    def pack_state(a):
        vals = dict(norm_mix_g=a[0], norm_ffn_g=a[1], norm_final_g=a[2], lower_bounds=a[3], hg_norm_g=a[4])
        return _pack_small(vals, _conv_shard_rows(a[5][0]))

    sw = pack_state((norm_mix_g, norm_ffn_g, norm_final_g, lower_bounds, hg_norm_g, conv_w))
    sm = pack_state((m_norm_mix_g, m_norm_ffn_g, m_norm_final_g, m_lower_bounds, m_hg_norm_g, m_conv_w))
    sv = pack_state((v_norm_mix_g, v_norm_ffn_g, v_norm_final_g, v_lower_bounds, v_hg_norm_g, v_conv_w))
    s_delta, s_m, s_v = _small_adamw(sw, g_rows, sm, sv)

    shapes = dict(norm_mix_g=(1, D_MODEL), norm_ffn_g=(1, D_MODEL), norm_final_g=(D_MODEL,),
                  lower_bounds=(2, HG_WIDTH), hg_norm_g=(1, HEAD_DIM))

    def unpack(buf, name):
        if name == "conv_w":
            return buf[_CONV_ROW:_CONV_ROW + 3, :64].reshape(1, 3, 64)
        for nm, off, rows in _SMALL_LAYOUT:
            if nm == name:
                return buf[off:off + rows].reshape(shapes[name])
        raise KeyError(name)

    order = ["norm_mix_g", "w_in", "lower_bounds", "hg_norm_g", "conv_w", "w_branch_a", "w_branch_b", "w_out",
             "norm_ffn_g", "w_ffn_gate", "w_ffn_up", "w_ffn_down", "norm_final_g"]
    result = [loss, grad_x[None]]
    for k, sbuf in enumerate((g_rows, s_delta, s_m, s_v)):
        for n in order:
            if n in outs:
                result.append(outs[n][k][None])
            else:
                result.append(unpack(sbuf, n))
    return tuple(result)
```

```python
import functools

import jax
import jax.numpy as jnp
from jax import lax
from jax.experimental import pallas as pl
from jax.experimental.pallas import tpu as pltpu
from jax.experimental.pallas import tpu_sc as plsc

F32 = jnp.float32
BF16 = jnp.bfloat16

D_MODEL = 1024
HG_WIDTH = 512
HEAD_DIM = 128
N_HEADS = 4
CONV_WIDTH = 512
D_FF = 2816
CHUNK = 32
EPS = 1e-6
Q_SCALE = HEAD_DIM ** -0.5
N_DEV = 8

ADAM_LR = 0.001
ADAM_B1 = 0.9
ADAM_B2 = 0.999
ADAM_EPS = 1e-08
ADAM_WD = 0.01
ADAM_STEP = 10

VMEM_LIMIT_V7X = 56 * 1024 * 1024

PACK_SPLITS = (("w_in", 704), ("w_branch_a", 64), ("w_branch_b", 64), ("w_out", 128),
               ("w_ffn_gate", 352), ("w_ffn_up", 352), ("w_ffn_down", 352))
PACK_ROWS = sum(r for _, r in PACK_SPLITS)
PACK_OFF = {}
_o = 0
for _n, _r in PACK_SPLITS:
    PACK_OFF[_n] = (_o, _r)
    _o += _r

SMALL_ROWS = 64


def _params(sem, vmem=VMEM_LIMIT_V7X):
    return pltpu.CompilerParams(dimension_semantics=sem, vmem_limit_bytes=vmem)


def _mm(a, b):
    return jnp.dot(a.astype(BF16), b.astype(BF16), preferred_element_type=F32)


def _mm_nt(a, b):
    return lax.dot_general(a.astype(BF16), b.astype(BF16), (((1,), (1,)), ((), ())), preferred_element_type=F32)


def _mm_tn(a, b):
    return lax.dot_general(a.astype(BF16), b.astype(BF16), (((0,), (0,)), ((), ())), preferred_element_type=F32)


def _sigmoid(x):
    return 1.0 / (1.0 + jnp.exp(-x))


def _resident(shape):
    nd = len(shape)
    return pl.BlockSpec(shape, lambda *_: (0,) * nd, pipeline_mode=pl.Buffered(1))


def _full(shape):
    nd = len(shape)
    return pl.BlockSpec(shape, lambda *_: (0,) * nd)


def _shard_cols(w_ref):
    return jnp.concatenate([w_ref[s] for s in range(N_DEV)], axis=1)


N_HG = 4 * HG_WIDTH
N_CV = 3 * CONV_WIDTH
N_GT = 2 * D_MODEL
N_IN = N_HG + N_CV + N_GT


def _col(tm, n):
    return pl.BlockSpec((n, tm), lambda i: (0, i))


def _fwd_in(x, g, w_in_t):
    T = x.shape[0]
    tm = min(256, T)

    def body(x_ref, g_ref, w_ref, ht_ref, hg_ref, cv_ref, gt_ref):
        xv = x_ref[...]
        r = lax.rsqrt(jnp.mean(xv * xv, axis=-1, keepdims=True) + EPS)
        hf = xv * r * g_ref[...]
        h = hf.astype(BF16)
        ht_ref[...] = hf.T.astype(BF16)
        hg_ref[...] = _mm_nt(h, w_ref[:N_HG, :])
        cv_ref[...] = _mm_nt(h, w_ref[N_HG:N_HG + N_CV, :])
        gt_ref[...] = _mm_nt(h, w_ref[N_HG + N_CV:, :])

    row = lambda n: pl.BlockSpec((tm, n), lambda i: (i, 0))
    return pl.pallas_call(
        body, name="fwd_in", grid=(T // tm,),
        in_specs=[row(D_MODEL), _full((1, D_MODEL)), _resident(w_in_t.shape)],
        out_specs=[_col(tm, D_MODEL), row(N_HG), row(N_CV), row(N_GT)],
        out_shape=[jax.ShapeDtypeStruct((D_MODEL, T), BF16), jax.ShapeDtypeStruct((T, N_HG), F32),
                   jax.ShapeDtypeStruct((T, N_CV), F32), jax.ShapeDtypeStruct((T, N_GT), F32)],
        compiler_params=_params(("parallel",)),
    )(x, g, w_in_t)


def _chunk_pos(shape):
    return lax.broadcasted_iota(jnp.int32, shape, 0) & (CHUNK - 1)


def _chunk_cumsum(x, pos):
    s = 1
    while s < CHUNK:
        x = x + jnp.where(pos >= s, pltpu.roll(x, s, 0), 0.0)
        s *= 2
    return x


def _chunk_rev_cumsum(x, pos):
    n = x.shape[0]
    s = 1
    while s < CHUNK:
        x = x + jnp.where(pos + s < CHUNK, pltpu.roll(x, n - s, 0), 0.0)
        s *= 2
    return x


def _chunk_bcast(x3, row, tb):
    return jnp.broadcast_to(x3[:, row:row + 1, :], x3.shape).reshape(tb, x3.shape[-1])


def _lower_bound(low_ref):
    l0 = low_ref[0:1, :]
    l1 = low_ref[1:2, :]
    m = jnp.maximum(l0, l1)
    e0 = jnp.exp(l0 - m)
    e1 = jnp.exp(l1 - m)
    return e0 / (e0 + e1), e1 / (e0 + e1)


def _hg_gates(qr, fr, lb, pos, tb):
    sq = _sigmoid(qr)
    q = qr * sq * Q_SCALE
    sg = _sigmoid(fr)
    f = lb + (1.0 - lb) * sg
    k = 1.0 - f
    b = _chunk_cumsum(jnp.log(f), pos)
    b3 = b.reshape(tb // CHUNK, CHUNK, HEAD_DIM)
    anc = _chunk_bcast(b3, CHUNK // 2 - 1, tb)
    blb = _chunk_bcast(b3, CHUNK - 1, tb)
    e_qa = jnp.exp(b - anc)
    e_ka = jnp.exp(anc - b)
    e_b = jnp.exp(b)
    e_ko = jnp.exp(blb - b)
    dec = jnp.exp(blb)
    return sq, q, sg, f, k, e_qa, e_ka, e_b, e_ko, dec


def _intra_mask(sb):
    r = lax.broadcasted_iota(jnp.int32, (sb, sb), 0)
    c = lax.broadcasted_iota(jnp.int32, (sb, sb), 1)
    return ((r // CHUNK) == (c // CHUNK)) & (c <= r)


def _hg_fwd(hg, low, gn):
    T = hg.shape[0]
    tb = min(512, T)
    sb = min(256, tb)
    nb = T // tb
    nc = tb // CHUNK

    def body(q_ref, f_ref, i_ref, g_ref, low_ref, gn_ref, o_ref, og_ref, ogt_ref, st_ref, s_scr):
        t = pl.program_id(1)

        @pl.when(t == 0)
        def _():
            s_scr[...] = jnp.zeros_like(s_scr)

        pos = _chunk_pos((tb, HEAD_DIM))
        lb, _ = _lower_bound(low_ref)
        v = i_ref[...]
        _, q, _, _, k, e_qa, e_ka, e_b, e_ko, dec = _hg_gates(q_ref[...], f_ref[...], lb, pos, tb)
        qh = (q * e_qa).astype(BF16)
        kh = (k * e_ka).astype(BF16)
        qi = (q * e_b).astype(BF16)
        ko = (k * e_ko).astype(BF16)
        vb = v.astype(BF16)
        mask = _intra_mask(sb)
        for s in range(tb // sb):
            sl = slice(s * sb, (s + 1) * sb)
            p = jnp.where(mask, _mm_nt(qh[sl], kh[sl]), 0.0)
            o_ref[sl, :] = _mm(p, vb[sl])
        st = s_scr[...]
        for c in range(nc):
            sl = slice(c * CHUNK, (c + 1) * CHUNK)
            st_ref[c] = st
            o_ref[sl, :] = o_ref[sl, :] + _mm_nt(qi[sl], st)
            st = dec[c * CHUNK:c * CHUNK + 1, :] * st + _mm_tn(vb[sl], ko[sl])
        s_scr[...] = st
        o = o_ref[...]
        r = lax.rsqrt(jnp.mean(o * o, axis=-1, keepdims=True) + EPS)
        gr = g_ref[...]
        og = (o * r * gn_ref[...]) * (gr * _sigmoid(gr))
        og_ref[...] = og.astype(BF16)
        ogt_ref[...] = og.T.astype(BF16)

    col = lambda p: pl.BlockSpec((tb, HEAD_DIM), lambda h, t: (t, p * N_HEADS + h))
    hcol = pl.BlockSpec((tb, HEAD_DIM), lambda h, t: (t, h))
    return pl.pallas_call(
        body, name="hg_fwd", grid=(N_HEADS, nb),
        in_specs=[col(0), col(1), col(2), col(3), pl.BlockSpec((2, HEAD_DIM), lambda h, t: (0, h)),
                  pl.BlockSpec((1, HEAD_DIM), lambda h, t: (0, 0))],
        out_specs=[hcol, hcol, pl.BlockSpec((HEAD_DIM, tb), lambda h, t: (h, t)),
                   pl.BlockSpec((None, nc, HEAD_DIM, HEAD_DIM), lambda h, t: (h, t, 0, 0))],
        out_shape=[jax.ShapeDtypeStruct((T, HG_WIDTH), F32), jax.ShapeDtypeStruct((T, HG_WIDTH), BF16),
                   jax.ShapeDtypeStruct((HG_WIDTH, T), BF16),
                   jax.ShapeDtypeStruct((N_HEADS, T // CHUNK, HEAD_DIM, HEAD_DIM), F32)],
        scratch_shapes=[pltpu.VMEM((HEAD_DIM, HEAD_DIM), F32)],
        compiler_params=_params(("parallel", "arbitrary")),
    )(hg, hg, hg, hg, low, gn)


def _conv_fwd(cv, conv_w):
    T = cv.shape[0]
    nj = CONV_WIDTH // 128

    def body(c_ref, b_ref, x_ref, w_ref, o_ref, ot_ref):
        row = lax.broadcasted_iota(jnp.int32, (T, 128), 0)
        u = c_ref[...] * x_ref[...]
        u1 = jnp.where(row >= 1, pltpu.roll(u, 1, 0), 0.0)
        u2 = jnp.where(row >= 2, pltpu.roll(u, 2, 0), 0.0)
        y = w_ref[0:1, :] * u2 + w_ref[1:2, :] * u1 + w_ref[2:3, :] * u
        out = b_ref[...] * y
        o_ref[...] = out.astype(BF16)
        ot_ref[...] = out.T.astype(BF16)

    col = lambda p: pl.BlockSpec((T, 128), lambda j: (0, p * nj + j))
    return pl.pallas_call(
        body, name="conv_fwd", grid=(nj,),
        in_specs=[col(0), col(1), col(2), pl.BlockSpec((3, 128), lambda j: (0, j))],
        out_specs=[pl.BlockSpec((T, 128), lambda j: (0, j)), pl.BlockSpec((128, T), lambda j: (j, 0))],
        out_shape=[jax.ShapeDtypeStruct((T, CONV_WIDTH), BF16), jax.ShapeDtypeStruct((CONV_WIDTH, T), BF16)],
        compiler_params=_params(("parallel",)),
    )(cv, cv, cv, conv_w)


def _merge_fwd(og, cvo, gt, x, wa, wb, wo):
    T = x.shape[0]
    tm = min(512, T)

    def body(og_ref, cvo_ref, gt_ref, x_ref, wa_ref, wb_ref, wo_ref, x1_ref, mgt_ref):
        ya = jnp.dot(og_ref[...], _shard_cols(wa_ref), preferred_element_type=F32)
        yb = jnp.dot(cvo_ref[...], _shard_cols(wb_ref), preferred_element_type=F32)
        m = _sigmoid(gt_ref[:, :D_MODEL]) * ya + _sigmoid(gt_ref[:, D_MODEL:]) * yb
        mgt_ref[...] = m.T.astype(BF16)
        x1_ref[...] = x_ref[...] + jnp.dot(m.astype(BF16), wo_ref[...], preferred_element_type=F32)

    row = lambda n: pl.BlockSpec((tm, n), lambda i: (i, 0))
    return pl.pallas_call(
        body, name="merge_fwd", grid=(T // tm,),
        in_specs=[row(HG_WIDTH), row(CONV_WIDTH), row(2 * D_MODEL), row(D_MODEL),
                  _resident(wa.shape), _resident(wb.shape), _resident(wo.shape)],
        out_specs=[row(D_MODEL), _col(tm, D_MODEL)],
        out_shape=[jax.ShapeDtypeStruct((T, D_MODEL), F32), jax.ShapeDtypeStruct((D_MODEL, T), BF16)],
        compiler_params=_params(("parallel",)),
    )(og, cvo, gt, x, wa, wb, wo)


def _ffn_fwd(x1, g, wg, wu, wd):
    T = x1.shape[0]
    tm = min(256, T)

    def body(x_ref, g_ref, wg_ref, wu_ref, wd_ref, ht_ref, gate_ref, up_ref, actt_ref, x2_ref):
        xv = x_ref[...]
        r = lax.rsqrt(jnp.mean(xv * xv, axis=-1, keepdims=True) + EPS)
        hf = xv * r * g_ref[...]
        h = hf.astype(BF16)
        ht_ref[...] = hf.T.astype(BF16)
        gate = _mm_nt(h, wg_ref[...])
        up = _mm_nt(h, wu_ref[...])
        gate_ref[...] = gate
        up_ref[...] = up
        act = gate * _sigmoid(gate) * up
        actt_ref[...] = act.T.astype(BF16)
        x2_ref[...] = xv + jnp.dot(act.astype(BF16), wd_ref[...], preferred_element_type=F32)

    row = lambda n: pl.BlockSpec((tm, n), lambda i: (i, 0))
    return pl.pallas_call(
        body, name="ffn_fwd", grid=(T // tm,),
        in_specs=[row(D_MODEL), _full((1, D_MODEL)), _resident(wg.shape), _resident(wu.shape), _resident(wd.shape)],
        out_specs=[_col(tm, D_MODEL), row(D_FF), row(D_FF), _col(tm, D_FF), row(D_MODEL)],
        out_shape=[jax.ShapeDtypeStruct((D_MODEL, T), BF16), jax.ShapeDtypeStruct((T, D_FF), F32),
                   jax.ShapeDtypeStruct((T, D_FF), F32), jax.ShapeDtypeStruct((D_FF, T), BF16),
                   jax.ShapeDtypeStruct((T, D_MODEL), F32)],
        compiler_params=_params(("parallel",)),
    )(x1, g, wg, wu, wd)


def _final_fwd_bwd(x2, target, g):
    T = x2.shape[0]
    tm = min(512, T)

    def body(x_ref, t_ref, g_ref, loss_ref, dg_ref, dx_ref):
        @pl.when(pl.program_id(0) == 0)
        def _():
            loss_ref[...] = jnp.zeros_like(loss_ref)
            dg_ref[...] = jnp.zeros_like(dg_ref)

        xv = x_ref[...]
        gv = g_ref[...]
        r = lax.rsqrt(jnp.mean(xv * xv, axis=-1, keepdims=True) + EPS)
        xh = xv * r
        err = xh * gv - t_ref[...]
        loss_ref[...] += 0.5 * jnp.sum(jnp.mean(err * err, axis=-1, keepdims=True), axis=0, keepdims=True)
        dy = err * (1.0 / D_MODEL)
        dg_ref[...] += jnp.sum(dy * xh, axis=0, keepdims=True)
        w = dy * gv
        dx_ref[...] = r * (w - xh * jnp.mean(w * xh, axis=-1, keepdims=True))

    row = pl.BlockSpec((tm, D_MODEL), lambda i: (i, 0))
    return pl.pallas_call(
        body, name="final_fwd_bwd", grid=(T // tm,),
        in_specs=[row, row, _full((1, D_MODEL))],
        out_specs=[_full((1, 128)), _full((1, D_MODEL)), row],
        out_shape=[jax.ShapeDtypeStruct((1, 128), F32), jax.ShapeDtypeStruct((1, D_MODEL), F32),
                   jax.ShapeDtypeStruct((T, D_MODEL), F32)],
        compiler_params=_params(("arbitrary",)),
    )(x2, target, g)


def _ffn_bwd(dx2, x1, gate, up, g, wg, wu, wd):
    T = x1.shape[0]
    tm = min(256, T)

    def body(dx2_ref, x_ref, gate_ref, up_ref, g_ref, wg_ref, wu_ref, wd_ref, dgate_ref, dup_ref, dx1_ref, dgn_ref):
        @pl.when(pl.program_id(0) == 0)
        def _():
            dgn_ref[...] = jnp.zeros_like(dgn_ref)

        dx2 = dx2_ref[...]
        dact = _mm_nt(dx2, wd_ref[...])
        gate = gate_ref[...]
        s = _sigmoid(gate)
        dgate = (dact * up_ref[...] * (s * (1.0 + gate * (1.0 - s)))).astype(BF16)
        dup = (dact * (gate * s)).astype(BF16)
        dgate_ref[...] = dgate
        dup_ref[...] = dup
        dh = _mm(dgate, wg_ref[...]) + _mm(dup, wu_ref[...])
        xv = x_ref[...]
        r = lax.rsqrt(jnp.mean(xv * xv, axis=-1, keepdims=True) + EPS)
        xh = xv * r
        dgn_ref[...] += jnp.sum(dh * xh, axis=0, keepdims=True)
        w = dh * g_ref[...]
        dx1_ref[...] = dx2 + r * (w - xh * jnp.mean(w * xh, axis=-1, keepdims=True))

    row = lambda n: pl.BlockSpec((tm, n), lambda i: (i, 0))
    return pl.pallas_call(
        body, name="ffn_bwd", grid=(T // tm,),
        in_specs=[row(D_MODEL), row(D_MODEL), row(D_FF), row(D_FF), _full((1, D_MODEL)),
                  _resident(wg.shape), _resident(wu.shape), _resident(wd.shape)],
        out_specs=[row(D_FF), row(D_FF), row(D_MODEL), _full((1, D_MODEL))],
        out_shape=[jax.ShapeDtypeStruct((T, D_FF), BF16), jax.ShapeDtypeStruct((T, D_FF), BF16),
                   jax.ShapeDtypeStruct((T, D_MODEL), F32), jax.ShapeDtypeStruct((1, D_MODEL), F32)],
        compiler_params=_params(("arbitrary",)),
    )(dx2, x1, gate, up, g, wg, wu, wd)


def _merge_bwd(dx1, og, cvo, gt, wa, wb, wo):
    T = dx1.shape[0]
    tm = min(512, T)

    def body(dx_ref, og_ref, cvo_ref, gt_ref, wa_ref, wb_ref, wo_ref, dgt_ref, dya_ref, dyb_ref, dog_ref, dcvo_ref):
        dm = _mm_nt(dx_ref[...], wo_ref[...])
        wa = _shard_cols(wa_ref)
        wb = _shard_cols(wb_ref)
        ya = jnp.dot(og_ref[...], wa, preferred_element_type=F32)
        yb = jnp.dot(cvo_ref[...], wb, preferred_element_type=F32)
        sa = _sigmoid(gt_ref[:, :D_MODEL])
        sb = _sigmoid(gt_ref[:, D_MODEL:])
        dgt_ref[:, :D_MODEL] = (dm * ya * (sa * (1.0 - sa))).astype(BF16)
        dgt_ref[:, D_MODEL:] = (dm * yb * (sb * (1.0 - sb))).astype(BF16)
        dya = (dm * sa).astype(BF16)
        dyb = (dm * sb).astype(BF16)
        dya_ref[...] = dya
        dyb_ref[...] = dyb
        dog_ref[...] = _mm_nt(dya, wa)
        dcvo_ref[...] = _mm_nt(dyb, wb)

    row = lambda n: pl.BlockSpec((tm, n), lambda i: (i, 0))
    return pl.pallas_call(
        body, name="merge_bwd", grid=(T // tm,),
        in_specs=[row(D_MODEL), row(HG_WIDTH), row(CONV_WIDTH), row(2 * D_MODEL),
                  _resident(wa.shape), _resident(wb.shape), _resident(wo.shape)],
        out_specs=[row(2 * D_MODEL), row(D_MODEL), row(D_MODEL), row(HG_WIDTH), row(CONV_WIDTH)],
        out_shape=[jax.ShapeDtypeStruct((T, 2 * D_MODEL), BF16), jax.ShapeDtypeStruct((T, D_MODEL), BF16),
                   jax.ShapeDtypeStruct((T, D_MODEL), BF16), jax.ShapeDtypeStruct((T, HG_WIDTH), F32),
                   jax.ShapeDtypeStruct((T, CONV_WIDTH), F32)],
        compiler_params=_params(("parallel",)),
    )(dx1, og, cvo, gt, wa, wb, wo)


def _conv_bwd(dcvo, cv, conv_w):
    T = cv.shape[0]
    nj = CONV_WIDTH // 128

    def body(do_ref, c_ref, b_ref, x_ref, w_ref, dc_ref, db_ref, dx_ref, dw_ref):
        row = lax.broadcasted_iota(jnp.int32, (T, 128), 0)
        c = c_ref[...]
        xb = x_ref[...]
        do = do_ref[...]
        u = c * xb
        u1 = jnp.where(row >= 1, pltpu.roll(u, 1, 0), 0.0)
        u2 = jnp.where(row >= 2, pltpu.roll(u, 2, 0), 0.0)
        w0, w1, w2 = w_ref[0:1, :], w_ref[1:2, :], w_ref[2:3, :]
        y = w0 * u2 + w1 * u1 + w2 * u
        db_ref[...] = (do * y).astype(BF16)
        dy = do * b_ref[...]
        dw_ref[0:1, :] = jnp.sum(dy * u2, axis=0, keepdims=True)
        dw_ref[1:2, :] = jnp.sum(dy * u1, axis=0, keepdims=True)
        dw_ref[2:3, :] = jnp.sum(dy * u, axis=0, keepdims=True)
        dy1 = jnp.where(row < T - 1, pltpu.roll(dy, T - 1, 0), 0.0)
        dy2 = jnp.where(row < T - 2, pltpu.roll(dy, T - 2, 0), 0.0)
        du = w2 * dy + w1 * dy1 + w0 * dy2
        dc_ref[...] = (du * xb).astype(BF16)
        dx_ref[...] = (du * c).astype(BF16)

    col = lambda p: pl.BlockSpec((T, 128), lambda j: (0, p * nj + j))
    one = pl.BlockSpec((T, 128), lambda j: (0, j))
    wspec = pl.BlockSpec((3, 128), lambda j: (0, j))
    out = jax.ShapeDtypeStruct((T, CONV_WIDTH), BF16)
    return pl.pallas_call(
        body, name="conv_bwd", grid=(nj,),
        in_specs=[one, col(0), col(1), col(2), wspec],
        out_specs=[one, one, one, wspec],
        out_shape=[out, out, out, jax.ShapeDtypeStruct((3, CONV_WIDTH), F32)],
        compiler_params=_params(("parallel",)),
    )(dcvo, cv, cv, cv, conv_w)


def _drop_operands(body, first, count):
    def wrapped(*refs):
        return body(*refs[:first], *refs[first + count:])
    return wrapped


def _hg_bwd(dog, hg, o, st, low, gn, after=()):
    T = hg.shape[0]
    tb = min(512, T)
    sb = min(256, tb)
    nb = T // tb
    nc = tb // CHUNK

    def body(q_ref, f_ref, i_ref, g_ref, low_ref, gn_ref, o_ref, dog_ref, st_ref,
             dq_ref, df_ref, di_ref, dg_ref, dlow_ref, dgn_ref,
             ds_scr, dqi_scr, dko_scr, dv_scr, dd_scr, dqh_scr, dkh_scr):
        h = pl.program_id(0)
        t = pl.program_id(1)

        @pl.when(t == 0)
        def _():
            ds_scr[...] = jnp.zeros_like(ds_scr)
            dlow_ref[...] = jnp.zeros_like(dlow_ref)

        @pl.when((t == 0) & (h == 0))
        def _():
            dgn_ref[...] = jnp.zeros_like(dgn_ref)

        pos = _chunk_pos((tb, HEAD_DIM))
        lb, lb1 = _lower_bound(low_ref)
        qr = q_ref[...]
        v = i_ref[...]
        sq, q, sg, f, k, e_qa, e_ka, e_b, e_ko, dec = _hg_gates(qr, f_ref[...], lb, pos, tb)

        gr = g_ref[...]
        gnv = gn_ref[...]
        o = o_ref[...]
        dog_v = dog_ref[...]
        sgr = _sigmoid(gr)
        r = lax.rsqrt(jnp.mean(o * o, axis=-1, keepdims=True) + EPS)
        oh = o * r
        dg_ref[...] = (dog_v * (oh * gnv) * (sgr * (1.0 + gr * (1.0 - sgr)))).astype(BF16)
        don = dog_v * (gr * sgr)
        dgn_ref[...] += jnp.sum(don * oh, axis=0, keepdims=True)
        w = don * gnv
        do = (r * (w - oh * jnp.mean(w * oh, axis=-1, keepdims=True))).astype(BF16)

        qh = (q * e_qa).astype(BF16)
        kh = (k * e_ka).astype(BF16)
        qi = (q * e_b).astype(BF16)
        ko = (k * e_ko).astype(BF16)
        vb = v.astype(BF16)

        mask = _intra_mask(sb)
        for s in range(tb // sb):
            sl = slice(s * sb, (s + 1) * sb)
            p = jnp.where(mask, _mm_nt(qh[sl], kh[sl]), 0.0).astype(BF16)
            dp = jnp.where(mask, _mm_nt(do[sl], vb[sl]), 0.0).astype(BF16)
            dv_scr[sl, :] = _mm_tn(p, do[sl])
            dqh_scr[sl, :] = _mm(dp, kh[sl])
            dkh_scr[sl, :] = _mm_tn(dp, qh[sl])

        ds = ds_scr[...]
        for c in reversed(range(nc)):
            sl = slice(c * CHUNK, (c + 1) * CHUNK)
            st_c = st_ref[c]
            dqi_scr[sl, :] = _mm(do[sl], st_c)
            dko_scr[sl, :] = _mm(vb[sl], ds)
            dv_scr[sl, :] = dv_scr[sl, :] + _mm_nt(ko[sl], ds)
            dd_scr[sl, :] = jnp.broadcast_to(jnp.sum(ds * st_c, axis=0, keepdims=True), (CHUNK, HEAD_DIM))
            ds = dec[c * CHUNK:c * CHUNK + 1, :] * ds + _mm_tn(do[sl], qi[sl])
        ds_scr[...] = ds

        dko_e = dko_scr[...] * e_ko
        dq = dqh_scr[...] * e_qa + dqi_scr[...] * e_b
        dk = dkh_scr[...] * e_ka + dko_e
        kd3 = (k * dko_e).reshape(nc, CHUNK, HEAD_DIM)
        last = jnp.broadcast_to(jnp.sum(kd3, axis=1, keepdims=True), kd3.shape).reshape(tb, HEAD_DIM)
        db = q * dq - k * dk + jnp.where(pos == CHUNK - 1, dec * dd_scr[...] + last, 0.0)
        dlg = _chunk_rev_cumsum(db, pos)
        dfv = dlg / f - dk
        s_low = jnp.sum(dfv * (1.0 - sg), axis=0, keepdims=True)
        dlow_ref[0:1, :] += s_low * lb * (1.0 - lb)
        dlow_ref[1:2, :] += -s_low * lb * lb1
        df_ref[...] = (dfv * (1.0 - lb) * sg * (1.0 - sg)).astype(BF16)
        dq_ref[...] = (dq * Q_SCALE * (sq * (1.0 + qr * (1.0 - sq)))).astype(BF16)
        di_ref[...] = dv_scr[...].astype(BF16)

    rt = lambda t: nb - 1 - t
    col = lambda p: pl.BlockSpec((tb, HEAD_DIM), lambda h, t: (rt(t), p * N_HEADS + h))
    hcol = pl.BlockSpec((tb, HEAD_DIM), lambda h, t: (rt(t), h))
    piece = jax.ShapeDtypeStruct((T, HG_WIDTH), BF16)
    tile = pltpu.VMEM((tb, HEAD_DIM), F32)
    return pl.pallas_call(
        _drop_operands(body, 9, len(after)), name="hg_bwd", grid=(N_HEADS, nb),
        in_specs=[col(0), col(1), col(2), col(3), pl.BlockSpec((2, HEAD_DIM), lambda h, t: (0, h)),
                  pl.BlockSpec((1, HEAD_DIM), lambda h, t: (0, 0)), hcol, hcol,
                  pl.BlockSpec((None, nc, HEAD_DIM, HEAD_DIM), lambda h, t: (h, rt(t), 0, 0))]
                 + [HBM_SPEC] * len(after),
        out_specs=[hcol, hcol, hcol, hcol, pl.BlockSpec((2, HEAD_DIM), lambda h, t: (0, h)),
                   pl.BlockSpec((1, HEAD_DIM), lambda h, t: (0, 0))],
        out_shape=[piece, piece, piece, piece, jax.ShapeDtypeStruct((2, HG_WIDTH), F32),
                   jax.ShapeDtypeStruct((1, HEAD_DIM), F32)],
        scratch_shapes=[pltpu.VMEM((HEAD_DIM, HEAD_DIM), F32), tile, tile, tile, tile, tile, tile],
        compiler_params=_params(("arbitrary", "arbitrary")),
    )(hg, hg, hg, hg, low, gn, o, dog, st, *after)


def _in_bwd(dparts, w_in, x, dx1, g, after=()):
    T = x.shape[0]
    tm = min(256, T)
    widths = [p.shape[1] for p in dparts]
    offs = [sum(widths[:i]) for i in range(len(widths))]
    n = len(dparts)

    def body(*refs):
        d_refs = refs[:n]
        w_ref, x_ref, dx1_ref, g_ref, dx_ref, dgn_ref = refs[n:]

        @pl.when(pl.program_id(0) == 0)
        def _():
            dgn_ref[...] = jnp.zeros_like(dgn_ref)

        dh = None
        for d_ref, off, wd in zip(d_refs, offs, widths):
            part = _mm(d_ref[...], w_ref[off:off + wd, :])
            dh = part if dh is None else dh + part
        xv = x_ref[...]
        r = lax.rsqrt(jnp.mean(xv * xv, axis=-1, keepdims=True) + EPS)
        xh = xv * r
        dgn_ref[...] += jnp.sum(dh * xh, axis=0, keepdims=True)
        w = dh * g_ref[...]
        dx_ref[...] = dx1_ref[...] + r * (w - xh * jnp.mean(w * xh, axis=-1, keepdims=True))

    row = lambda m: pl.BlockSpec((tm, m), lambda i: (i, 0))
    return pl.pallas_call(
        _drop_operands(body, n + 4, len(after)), name="in_bwd", grid=(T // tm,),
        in_specs=[row(wd) for wd in widths] + [_resident(w_in.shape), row(D_MODEL), row(D_MODEL), _full((1, D_MODEL))]
                 + [HBM_SPEC] * len(after),
        out_specs=[row(D_MODEL), _full((1, D_MODEL))],
        out_shape=[jax.ShapeDtypeStruct((T, D_MODEL), F32), jax.ShapeDtypeStruct((1, D_MODEL), F32)],
        compiler_params=_params(("arbitrary",)),
    )(*dparts, w_in, x, dx1, g, *after)


def _wgrad(name, at, b, tn, transposed=False):
    M, T = at.shape
    N = b.shape[1]
    tk = min(1024, T)
    nk = T // tk

    def body(a_ref, b_ref, o_ref, acc):
        k = pl.program_id(1)
        part = _mm(a_ref[...], b_ref[...])

        @pl.when(k == 0)
        def _():
            acc[...] = part

        @pl.when(k != 0)
        def _():
            acc[...] += part

        @pl.when(k == nk - 1)
        def _():
            o_ref[...] = (acc[...].T if transposed else acc[...]).astype(BF16)

    if transposed:
        out_spec, out_shape = pl.BlockSpec((tn, M), lambda j, k: (j, 0)), (N, M)
    else:
        out_spec, out_shape = pl.BlockSpec((M, tn), lambda j, k: (0, j)), (M, N)
    return pl.pallas_call(
        body, name=name, grid=(N // tn, nk),
        in_specs=[pl.BlockSpec((M, tk), lambda j, k: (0, k)), pl.BlockSpec((tk, tn), lambda j, k: (k, j))],
        out_specs=out_spec, out_shape=jax.ShapeDtypeStruct(out_shape, BF16),
        scratch_shapes=[pltpu.VMEM((M, tn), F32)],
        compiler_params=_params(("parallel", "arbitrary")),
    )(at, b)


def _wgrad_in(ht, dparts, after=()):
    M, T = ht.shape
    tn = 512
    tk = min(1024, T)
    nk = T // tk
    nblk = [p.shape[1] // tn for p in dparts]
    start = [sum(nblk[:i]) for i in range(len(nblk))]
    n = len(dparts)

    def body(a_ref, *refs):
        d_refs, o_ref, acc = refs[:n], refs[n], refs[n + 1]
        j = pl.program_id(0)
        k = pl.program_id(1)

        @pl.when(k == 0)
        def _():
            acc[...] = jnp.zeros_like(acc)

        for d_ref, s, nb in zip(d_refs, start, nblk):
            @pl.when((j >= s) & (j < s + nb))
            def _():
                acc[...] += _mm(a_ref[...], d_ref[...])

        @pl.when(k == nk - 1)
        def _():
            o_ref[...] = acc[...].T.astype(BF16)

    def piece_spec(s, nb):
        def index(j, k):
            inside = (j >= s) & (j < s + nb)
            return jnp.where(inside, k, 0), jnp.clip(j - s, 0, nb - 1)
        return pl.BlockSpec((tk, tn), index)

    return pl.pallas_call(
        _drop_operands(body, 1 + n, len(after)), name="wgrad_in", grid=(sum(nblk), nk),
        in_specs=[pl.BlockSpec((M, tk), lambda j, k: (0, k))] + [piece_spec(s, nb) for s, nb in zip(start, nblk)]
                 + [HBM_SPEC] * len(after),
        out_specs=pl.BlockSpec((tn, M), lambda j, k: (j, 0)),
        out_shape=jax.ShapeDtypeStruct((sum(nblk) * tn, M), BF16),
        scratch_shapes=[pltpu.VMEM((M, tn), F32)],
        compiler_params=_params(("parallel", "arbitrary")),
    )(ht, *dparts, *after)


def _adamw_math(w, g, m, v):
    m = ADAM_B1 * m + (1.0 - ADAM_B1) * g
    v = ADAM_B2 * v + (1.0 - ADAM_B2) * (g * g)
    m_hat = m / (1.0 - ADAM_B1 ** ADAM_STEP)
    v_hat = v / (1.0 - ADAM_B2 ** ADAM_STEP)
    delta = -ADAM_LR * (m_hat / (jnp.sqrt(v_hat) + ADAM_EPS) + ADAM_WD * w)
    return delta, m, v


def _adamw_sum(name, w, parts, m, v):
    R, C = w.shape
    tr = _row_tile(R)

    def body(w_ref, p_ref, m_ref, v_ref, g_out, d_out, m_out, v_out):
        g = p_ref[0].astype(F32)
        for k in range(1, 4):
            g = g + p_ref[k].astype(F32)
        g_out[...] = g
        d_out[...], m_out[...], v_out[...] = _adamw_math(w_ref[...], g, m_ref[...], v_ref[...])

    blk = pl.BlockSpec((tr, C), lambda i: (i, 0))
    out = jax.ShapeDtypeStruct((R, C), F32)
    return pl.pallas_call(
        body, name=name, grid=(R // tr,),
        in_specs=[blk, pl.BlockSpec((4, tr, C), lambda i: (0, i, 0)), blk, blk],
        out_specs=[blk, blk, blk, blk], out_shape=[out, out, out, out],
        compiler_params=_params(("parallel",)),
    )(w, parts, m, v)


def _small_sum(gathered):
    R = gathered.shape[1]

    def body(p_ref, o_ref):
        g = p_ref[0]
        for k in range(1, N_DEV):
            g = g + p_ref[k]
        o_ref[...] = g

    return pl.pallas_call(
        body, name="small_sum", in_specs=[_full(gathered.shape)], out_specs=_full((R, 128)), grid=(1,),
        out_shape=jax.ShapeDtypeStruct((R, 128), F32),
    )(gathered)


def _small_adamw(w, g, m, v):
    def body(w_ref, g_ref, m_ref, v_ref, d_out, m_out, v_out):
        d_out[...], m_out[...], v_out[...] = _adamw_math(w_ref[...], g_ref[...], m_ref[...], v_ref[...])

    out = jax.ShapeDtypeStruct(w.shape, F32)
    spec = _full(w.shape)
    return pl.pallas_call(
        body, name="small_adamw", grid=(1,), in_specs=[spec] * 4, out_specs=[spec] * 3, out_shape=[out, out, out],
    )(w, g, m, v)


def _row_tile(rows):
    for cand in (256, 128):
        if rows % cand == 0:
            return cand
    return rows


def _pair_sum(name, by_owner, got, core):
    _, R, C = got.shape
    tr = _row_tile(R)

    def body(core_ref, a_ref, b_ref, o_ref):
        o_ref[...] = (a_ref[...].astype(F32) + b_ref[...].astype(F32)).astype(BF16)

    blk = pl.BlockSpec((None, tr, C), lambda k, i, core_ref: (k, i, 0))
    mine = pl.BlockSpec((None, tr, C), lambda k, i, core_ref: (2 * k + core_ref[0], i, 0))
    return pl.pallas_call(
        body, name=name,
        grid_spec=pltpu.PrefetchScalarGridSpec(num_scalar_prefetch=1, grid=(4, R // tr), in_specs=[mine, blk],
                                               out_specs=blk),
        out_shape=jax.ShapeDtypeStruct(got.shape, BF16),
        compiler_params=_params(("parallel", "parallel")),
    )(core, by_owner, got)


def _cols_from_shards(name, g):
    _, R, c = g.shape
    tr = _row_tile(R)

    def body(g_ref, o_ref):
        for s in range(N_DEV):
            o_ref[:, s * c:(s + 1) * c] = g_ref[s]

    return pl.pallas_call(
        body, name=name, grid=(R // tr,),
        in_specs=[pl.BlockSpec((N_DEV, tr, c), lambda i: (0, i, 0))],
        out_specs=pl.BlockSpec((tr, N_DEV * c), lambda i: (i, 0)),
        out_shape=jax.ShapeDtypeStruct((R, N_DEV * c), g.dtype),
        compiler_params=_params(("parallel",)),
    )(g)


def _shards_from_cols(name, full):
    R, allc = full.shape
    c = allc // N_DEV
    tr = _row_tile(R)

    def body(f_ref, o_ref):
        for s in range(N_DEV):
            o_ref[s] = f_ref[:, s * c:(s + 1) * c]

    return pl.pallas_call(
        body, name=name, grid=(R // tr,),
        in_specs=[pl.BlockSpec((tr, allc), lambda i: (i, 0))],
        out_specs=pl.BlockSpec((N_DEV, tr, c), lambda i: (0, i, 0)),
        out_shape=jax.ShapeDtypeStruct((N_DEV, R, c), full.dtype),
        compiler_params=_params(("parallel",)),
    )(full)


MESH = pl.DeviceIdType.MESH
HBM_SPEC = pl.BlockSpec(memory_space=pl.ANY)


def _handshake(peers):
    barrier = pltpu.get_barrier_semaphore()
    for peer in peers:
        pl.semaphore_signal(barrier, inc=1, device_id=peer, device_id_type=MESH)
    pl.semaphore_wait(barrier, len(peers))


def _comm_call(body, name, operands, out_shape, scratch, collective_id):
    if collective_id is None:
        return pl.pallas_call(body, name=name, in_specs=[HBM_SPEC] * len(operands), out_specs=[HBM_SPEC] * len(out_shape),
                              out_shape=out_shape, scratch_shapes=scratch)(*operands)
    return pl.kernel(body, out_type=out_shape, mesh=plsc.ScalarSubcoreMesh(axis_name="sequencer", num_cores=1),
                     scratch_types=scratch, name=name,
                     compiler_params=pltpu.CompilerParams(collective_id=collective_id))(*operands)


def _all_gather(name, blocks, collective_id=None, after=()):
    n = len(blocks)
    na = len(after)

    def body(*refs):
        x_refs, out_refs = refs[:n], refs[n + na:2 * n + na]
        send_sems, recv_sems, local_sems = refs[2 * n + na:]
        x, y, c = lax.axis_index("x"), lax.axis_index("y"), lax.axis_index("c")
        me, sibling = (x, y, c), (x, y, 1 - c)
        chips = [(1 - x, y), (x, 1 - y), (1 - x, 1 - y)]
        if collective_id is not None:
            _handshake([sibling] + [(*chip, c) for chip in chips])

        def slot(i, px, py, pc):
            return out_refs[i].at[4 * px + 2 * py + pc]

        def copy(i, k, blk, to, src=None):
            return pltpu.make_async_remote_copy(
                src_ref=slot(i, *blk) if src is None else src, dst_ref=slot(i, *blk),
                send_sem=send_sems.at[7 * i + k], recv_sem=recv_sems.at[7 * i + k], device_id=to, device_id_type=MESH)

        mine = [pltpu.make_async_copy(x_refs[i], slot(i, *me), local_sems.at[i]) for i in range(n)]
        for cp in mine:
            cp.start()
        first = []
        for i in range(n):
            first.append(copy(i, 0, me, sibling, src=x_refs[i]))
            first += [copy(i, 1 + j, me, (*chip, c), src=x_refs[i]) for j, chip in enumerate(chips)]
        for cp in first:
            cp.start()
        passed = []
        for i in range(n):
            for j, chip in enumerate(chips):
                copy(i, 1 + j, (*chip, c), me).wait_recv()
                passed.append(copy(i, 4 + j, (*chip, c), sibling))
                passed[-1].start()
        for i in range(n):
            copy(i, 0, sibling, me).wait_recv()
            for j, chip in enumerate(chips):
                copy(i, 4 + j, (*chip, 1 - c), me).wait_recv()
        for cp in first + passed:
            cp.wait_send()
        for cp in mine:
            cp.wait()

    return _comm_call(
        body, name, list(blocks) + list(after), [jax.ShapeDtypeStruct((N_DEV,) + b.shape, b.dtype) for b in blocks],
        [pltpu.SemaphoreType.DMA((7 * n,)), pltpu.SemaphoreType.DMA((7 * n,)), pltpu.SemaphoreType.DMA((n,))],
        collective_id)


def _sibling_swap(name, by_owner, collective_id=None, after=()):
    n = len(by_owner)
    na = len(after)

    def body(*refs):
        x_refs, out_refs = refs[:n], refs[n + na:2 * n + na]
        send_sems, recv_sems = refs[2 * n + na:]
        x, y, c = lax.axis_index("x"), lax.axis_index("y"), lax.axis_index("c")
        if collective_id is not None:
            _handshake([(x, y, 1 - c)])
        copies = []
        for i in range(n):
            for k in range(4):
                copies.append(pltpu.make_async_remote_copy(
                    src_ref=x_refs[i].at[2 * k + 1 - c], dst_ref=out_refs[i].at[k],
                    send_sem=send_sems.at[4 * i + k], recv_sem=recv_sems.at[4 * i + k],
                    device_id=(x, y, 1 - c), device_id_type=MESH))
        for cp in copies:
            cp.start()
        for cp in copies:
            cp.wait()

    return _comm_call(
        body, name, list(by_owner) + list(after),
        [jax.ShapeDtypeStruct((4,) + b.shape[1:], b.dtype) for b in by_owner],
        [pltpu.SemaphoreType.DMA((4 * n,)), pltpu.SemaphoreType.DMA((4 * n,))], collective_id)


def _chip_exchange(name, sums, collective_id=None, after=()):
    n = len(sums)
    na = len(after)

    def body(*refs):
        x_refs, out_refs = refs[:n], refs[n + na:2 * n + na]
        send_sems, recv_sems, local_sems = refs[2 * n + na:]
        x, y, c = lax.axis_index("x"), lax.axis_index("y"), lax.axis_index("c")
        chips = [(1 - x, y), (x, 1 - y), (1 - x, 1 - y)]
        my_chip = 2 * x + y
        if collective_id is not None:
            _handshake([(cx, cy, c) for cx, cy in chips])
        mine = [pltpu.make_async_copy(x_refs[i].at[my_chip], out_refs[i].at[my_chip], local_sems.at[i])
                for i in range(n)]
        for cp in mine:
            cp.start()
        sends = []
        for i in range(n):
            for j, (cx, cy) in enumerate(chips):
                sends.append(pltpu.make_async_remote_copy(
                    src_ref=x_refs[i].at[2 * cx + cy], dst_ref=out_refs[i].at[my_chip],
                    send_sem=send_sems.at[3 * i + j], recv_sem=recv_sems.at[3 * i + j],
                    device_id=(cx, cy, c), device_id_type=MESH))
        for cp in sends:
            cp.start()
        for i in range(n):
            for j, (cx, cy) in enumerate(chips):
                pltpu.make_async_remote_copy(
                    src_ref=x_refs[i].at[my_chip], dst_ref=out_refs[i].at[2 * cx + cy],
                    send_sem=send_sems.at[3 * i + j], recv_sem=recv_sems.at[3 * i + j],
                    device_id=(cx, cy, c), device_id_type=MESH).wait_recv()
        for cp in sends:
            cp.wait_send()
        for cp in mine:
            cp.wait()

    return _comm_call(
        body, name, list(sums) + list(after), [jax.ShapeDtypeStruct(s.shape, s.dtype) for s in sums],
        [pltpu.SemaphoreType.DMA((3 * n,)), pltpu.SemaphoreType.DMA((3 * n,)), pltpu.SemaphoreType.DMA((n,))],
        collective_id)


def _cast_shards(shards):
    n = len(shards)

    def body(*refs):
        for i in range(n):
            refs[n + i][...] = refs[i][...].astype(BF16)

    vmem = pl.BlockSpec(memory_space=pltpu.VMEM)
    return pl.pallas_call(
        body, name="cast_shards", in_specs=[vmem] * n, out_specs=[vmem] * n,
        out_shape=[jax.ShapeDtypeStruct(s.shape, BF16) for s in shards],
        compiler_params=pltpu.CompilerParams(vmem_limit_bytes=VMEM_LIMIT_V7X),
    )(*shards)


BIG = ("w_in", "w_branch_a", "w_branch_b", "w_out", "w_ffn_gate", "w_ffn_up", "w_ffn_down")


def _local_step(x, target, gains, low, conv_w, wg8, reduce):
    g_mix, g_hg, g_ffn, g_fin = gains
    w_in = wg8["w_in"].reshape(N_IN, D_MODEL)
    wg = wg8["w_ffn_gate"].reshape(D_FF, D_MODEL)
    wu = wg8["w_ffn_up"].reshape(D_FF, D_MODEL)
    wa, wb = wg8["w_branch_a"], wg8["w_branch_b"]
    wo = wg8["w_out"].reshape(D_MODEL, D_MODEL)
    wd = wg8["w_ffn_down"].reshape(D_FF, D_MODEL)

    ht, hg, cv, gt = _fwd_in(x, g_mix, w_in)
    o, og, ogt, st = _hg_fwd(hg, low, g_hg)
    cvo, cvot = _conv_fwd(cv, conv_w)
    x1, mgt = _merge_fwd(og, cvo, gt, x, wa, wb, wo)
    h2t, gate, up, actt, x2 = _ffn_fwd(x1, g_ffn, wg, wu, wd)
    loss, d_gfin, dx2 = _final_fwd_bwd(x2, target, g_fin)

    dgate, dup, dx1, d_gffn = _ffn_bwd(dx2, x1, gate, up, g_ffn, wg, wu, wd)
    ffn = dict(
        w_ffn_down=_wgrad("wgrad_ffn_down", actt, dx2, 512).reshape(N_DEV, D_FF // N_DEV, D_MODEL),
        w_ffn_gate=_wgrad("wgrad_ffn_gate", h2t, dgate, 1408, transposed=True).reshape(N_DEV, D_FF // N_DEV, D_MODEL),
        w_ffn_up=_wgrad("wgrad_ffn_up", h2t, dup, 1408, transposed=True).reshape(N_DEV, D_FF // N_DEV, D_MODEL))
    sums_ffn, got_ffn = reduce.begin(ffn)
    dgt, dya, dyb, dog, dcvo = _merge_bwd(dx1, og, cvo, gt, wa, wb, wo)
    out = dict(
        w_out=_wgrad("wgrad_out", mgt, dx1, 512).reshape(N_DEV, D_MODEL // N_DEV, D_MODEL),
        w_branch_a=_shards_from_cols("split_w_branch_a", _wgrad("wgrad_branch_a", ogt, dya, 512)),
        w_branch_b=_shards_from_cols("split_w_branch_b", _wgrad("wgrad_branch_b", cvot, dyb, 512)))
    sums_out, got_out = reduce.begin(out, after=got_ffn[:1])
    parts_ffn = reduce.finish(ffn, sums_ffn, after=got_out[:1])
    dc, db, dxb, d_conv = _conv_bwd(dcvo, cv, conv_w)
    dq, df, di, dg, d_low, d_ghg = _hg_bwd(dog, hg, o, st, low, g_hg, after=list(sums_ffn) + list(sums_out))
    parts_out = reduce.finish(out, sums_out, after=[parts_ffn[0], dq])
    dparts = [dq, df, di, dg, dc, db, dxb, dgt]
    w_in_grad = dict(w_in=_wgrad_in(ht, dparts, after=parts_ffn[:1]).reshape(N_DEV, N_IN // N_DEV, D_MODEL))
    sums_in, _ = reduce.begin(w_in_grad, after=parts_out[:1])
    parts_in = reduce.finish(w_in_grad, sums_in)
    grad_x, d_gmix = _in_bwd(dparts, w_in, x, dx1, g_mix, after=list(parts_out[:1]) + list(sums_in))
    small = dict(norm_mix_g=d_gmix, norm_ffn_g=d_gffn, norm_final_g=d_gfin, lower_bounds=d_low, hg_norm_g=d_ghg,
                 conv_w=d_conv, loss=loss)
    return grad_x, small, parts_in


_SMALL_LAYOUT = (("norm_mix_g", 0, 8), ("norm_ffn_g", 8, 8), ("norm_final_g", 16, 8), ("lower_bounds", 24, 8),
                 ("hg_norm_g", 32, 1))
_LOSS_ROW = 40
_CONV_ROW = 48


def _pad_rows(a, rows):
    return jnp.pad(a, ((0, rows - a.shape[0]), (0, 0)))


def _pack_small(vals, conv_rows):
    parts = [_pad_rows(vals[name].reshape(rows, 128), 8) for name, _, rows in _SMALL_LAYOUT]
    loss = vals["loss"][:, :128] if "loss" in vals else jnp.zeros((1, 128), F32)
    parts.append(_pad_rows(loss, 8))
    parts.append(_pad_rows(conv_rows, SMALL_ROWS - _CONV_ROW))
    return jnp.concatenate(parts, axis=0)


def _conv_shard_rows(a):
    return jnp.pad(a, ((0, 5), (0, 64)))


def kernel(x, norm_mix_g, w_in, lower_bounds, hg_norm_g, conv_w, w_branch_a, w_branch_b, w_out, norm_ffn_g, w_ffn_gate, w_ffn_up, w_ffn_down, norm_final_g, loss_target, m_norm_mix_g, m_w_in, m_lower_bounds, m_hg_norm_g, m_conv_w, m_w_branch_a, m_w_branch_b, m_w_out, m_norm_ffn_g, m_w_ffn_gate, m_w_ffn_up, m_w_ffn_down, m_norm_final_g, v_norm_mix_g, v_w_in, v_lower_bounds, v_hg_norm_g, v_conv_w, v_w_branch_a, v_w_branch_b, v_w_out, v_norm_ffn_g, v_w_ffn_gate, v_w_ffn_up, v_w_ffn_down, v_norm_final_g):
    cx, cy, cc = lax.axis_index("x"), lax.axis_index("y"), lax.axis_index("c")
    my_dev = 4 * cx + 2 * cy + cc

    def tr(a):
        return a[0].T

    big = dict(w_in=tr(w_in), w_branch_a=w_branch_a[0], w_branch_b=w_branch_b[0], w_out=w_out[0],
               w_ffn_gate=tr(w_ffn_gate), w_ffn_up=tr(w_ffn_up), w_ffn_down=w_ffn_down[0])
    big_m = dict(w_in=tr(m_w_in), w_branch_a=m_w_branch_a[0], w_branch_b=m_w_branch_b[0], w_out=m_w_out[0],
                 w_ffn_gate=tr(m_w_ffn_gate), w_ffn_up=tr(m_w_ffn_up), w_ffn_down=m_w_ffn_down[0])
    big_v = dict(w_in=tr(v_w_in), w_branch_a=v_w_branch_a[0], w_branch_b=v_w_branch_b[0], w_out=v_w_out[0],
                 w_ffn_gate=tr(v_w_ffn_gate), w_ffn_up=tr(v_w_ffn_up), w_ffn_down=v_w_ffn_down[0])
    transposed = ("w_in", "w_ffn_gate", "w_ffn_up")

    shards = dict(zip(BIG, _cast_shards([big[n] for n in BIG])))
    first = _all_gather("gather_w_in", [shards["w_in"], _conv_shard_rows(conv_w[0])])
    later = _all_gather("gather_rest", [shards[n] for n in BIG[1:]], collective_id=1, after=first[1:])
    wg8 = dict(zip(BIG, [first[0]] + list(later)))
    conv_full = first[1][:, :3, :64].transpose(1, 0, 2).reshape(3, CONV_WIDTH)

    core = cc.reshape(1).astype(jnp.int32)
    outs = {}
    ids = iter(range(2, 16))

    class Reduce:
        @staticmethod
        def begin(grads, after=()):
            names = list(grads)
            by_owner = [grads[n] for n in names]
            got = _sibling_swap("sibling_swap_" + names[0], by_owner, collective_id=next(ids), after=after)
            return [_pair_sum("pair_sum_" + n, a, b, core) for n, a, b in zip(names, by_owner, got)], got

        @staticmethod
        def finish(grads, chip_sums, after=()):
            names = list(grads)
            parts = _chip_exchange("chip_exchange_" + names[0], chip_sums, collective_id=next(ids), after=after)
            for n, p in zip(names, parts):
                outs[n] = _adamw_sum("adamw_" + n, big[n], p, big_m[n], big_v[n])
            return parts

    gains = (norm_mix_g, hg_norm_g, norm_ffn_g, norm_final_g.reshape(1, D_MODEL))
    grad_x, small, last = _local_step(x[0], loss_target[0], gains, lower_bounds, conv_full, wg8, Reduce)

    small_all = _all_gather("gather_small", [_pack_small(small, small["conv_w"].reshape(12, 128))],
                            collective_id=next(ids), after=last[:1])
    ssum = _small_sum(small_all[0])
    conv_g_full = ssum[_CONV_ROW:_CONV_ROW + 12].reshape(3, CONV_WIDTH)
    conv_g = lax.dynamic_slice_in_dim(conv_g_full, my_dev * 64, 64, axis=1)
    loss = ssum[_LOSS_ROW, 0]
    g_rows = jnp.concatenate([ssum[:_CONV_ROW], _pad_rows(_conv_shard_rows(conv_g), SMALL_ROWS - _CONV_ROW)], axis=0)

    def pack_state(a):
        vals = dict(norm_mix_g=a[0], norm_ffn_g=a[1], norm_final_g=a[2], lower_bounds=a[3], hg_norm_g=a[4])
        return _pack_small(vals, _conv_shard_rows(a[5][0]))

    sw = pack_state((norm_mix_g, norm_ffn_g, norm_final_g, lower_bounds, hg_norm_g, conv_w))
    sm = pack_state((m_norm_mix_g, m_norm_ffn_g, m_norm_final_g, m_lower_bounds, m_hg_norm_g, m_conv_w))
    sv = pack_state((v_norm_mix_g, v_norm_ffn_g, v_norm_final_g, v_lower_bounds, v_hg_norm_g, v_conv_w))
    s_delta, s_m, s_v = _small_adamw(sw, g_rows, sm, sv)

    shapes = dict(norm_mix_g=(1, D_MODEL), norm_ffn_g=(1, D_MODEL), norm_final_g=(D_MODEL,),
                  lower_bounds=(2, HG_WIDTH), hg_norm_g=(1, HEAD_DIM))

    def unpack(buf, name):
        if name == "conv_w":
            return buf[_CONV_ROW:_CONV_ROW + 3, :64].reshape(1, 3, 64)
        for nm, off, rows in _SMALL_LAYOUT:
            if nm == name:
                return buf[off:off + rows].reshape(shapes[name])
        raise KeyError(name)

    order = ["norm_mix_g", "w_in", "lower_bounds", "hg_norm_g", "conv_w", "w_branch_a", "w_branch_b", "w_out",
             "norm_ffn_g", "w_ffn_gate", "w_ffn_up", "w_ffn_down", "norm_final_g"]
    result = [loss, grad_x[None]]
    for k, sbuf in enumerate((g_rows, s_delta, s_m, s_v)):
        for n in order:
            if n in outs:
                result.append((outs[n][k].T if n in transposed else outs[n][k])[None])
            else:
                result.append(unpack(sbuf, n))
    return tuple(result)
```

```python
import jax
import jax.numpy as jnp
from jax import lax
from jax.experimental import pallas as pl
from jax.experimental.pallas import tpu as pltpu
from jax.experimental.pallas import tpu_sc as plsc

F32 = jnp.float32
BF16 = jnp.bfloat16

D_MODEL = 1024
HG_WIDTH = 512
HEAD_DIM = 128
N_HEADS = 4
CONV_WIDTH = 512
D_FF = 2816
CHUNK = 32
EPS = 1e-6
Q_SCALE = HEAD_DIM ** -0.5
N_DEV = 8

ADAM_LR = 0.001
ADAM_B1 = 0.9
ADAM_B2 = 0.999
ADAM_EPS = 1e-08
ADAM_WD = 0.01
ADAM_STEP = 10

VMEM_LIMIT_V7X = 56 * 1024 * 1024

SMALL_ROWS = 64


def _params(sem, vmem=VMEM_LIMIT_V7X):
    return pltpu.CompilerParams(dimension_semantics=sem, vmem_limit_bytes=vmem)


def _mm(a, b):
    return jnp.dot(a.astype(BF16), b.astype(BF16), preferred_element_type=F32)


def _mm_nt(a, b):
    return lax.dot_general(a.astype(BF16), b.astype(BF16), (((1,), (1,)), ((), ())), preferred_element_type=F32)


def _mm_tn(a, b):
    return lax.dot_general(a.astype(BF16), b.astype(BF16), (((0,), (0,)), ((), ())), preferred_element_type=F32)


def _sigmoid(x):
    return 1.0 / (1.0 + jnp.exp(-x))


def _resident(shape):
    nd = len(shape)
    return pl.BlockSpec(shape, lambda *_: (0,) * nd, pipeline_mode=pl.Buffered(1))


def _full(shape):
    nd = len(shape)
    return pl.BlockSpec(shape, lambda *_: (0,) * nd)


def _shard_cols(w_ref):
    return jnp.concatenate([w_ref[s] for s in range(N_DEV)], axis=1)


N_HG = 4 * HG_WIDTH
N_CV = 3 * CONV_WIDTH
N_GT = 2 * D_MODEL
N_IN = N_HG + N_CV + N_GT


def _col(tm, n):
    return pl.BlockSpec((n, tm), lambda i: (0, i))


def _fwd_in(x, g, w_in_t):
    T = x.shape[0]
    tm = min(512, T)

    def body(x_ref, g_ref, w_ref, ht_ref, hg_ref, cv_ref, gt_ref):
        xv = x_ref[...]
        r = lax.rsqrt(jnp.mean(xv * xv, axis=-1, keepdims=True) + EPS)
        hf = xv * r * g_ref[...]
        h = hf.astype(BF16)
        ht_ref[...] = hf.T.astype(BF16)
        hg_ref[...] = _mm_nt(h, w_ref[:N_HG, :])
        cv_ref[...] = _mm_nt(h, w_ref[N_HG:N_HG + N_CV, :])
        gt_ref[...] = _mm_nt(h, w_ref[N_HG + N_CV:, :])

    row = lambda n: pl.BlockSpec((tm, n), lambda i: (i, 0))
    return pl.pallas_call(
        body, name="fwd_in", grid=(T // tm,),
        in_specs=[row(D_MODEL), _full((1, D_MODEL)), _resident(w_in_t.shape)],
        out_specs=[_col(tm, D_MODEL), row(N_HG), row(N_CV), row(N_GT)],
        out_shape=[jax.ShapeDtypeStruct((D_MODEL, T), BF16), jax.ShapeDtypeStruct((T, N_HG), F32),
                   jax.ShapeDtypeStruct((T, N_CV), F32), jax.ShapeDtypeStruct((T, N_GT), F32)],
        compiler_params=_params(("parallel",)),
    )(x, g, w_in_t)


def _chunk_pos(shape):
    return lax.broadcasted_iota(jnp.int32, shape, 0) & (CHUNK - 1)


def _chunk_cumsum(x, pos):
    s = 1
    while s < CHUNK:
        x = x + jnp.where(pos >= s, pltpu.roll(x, s, 0), 0.0)
        s *= 2
    return x


def _chunk_rev_cumsum(x, pos):
    n = x.shape[0]
    s = 1
    while s < CHUNK:
        x = x + jnp.where(pos + s < CHUNK, pltpu.roll(x, n - s, 0), 0.0)
        s *= 2
    return x


def _chunk_bcast(x3, row, tb):
    return jnp.broadcast_to(x3[:, row:row + 1, :], x3.shape).reshape(tb, x3.shape[-1])


def _lower_bound(low_ref):
    l0 = low_ref[0:1, :]
    l1 = low_ref[1:2, :]
    m = jnp.maximum(l0, l1)
    e0 = jnp.exp(l0 - m)
    e1 = jnp.exp(l1 - m)
    return e0 / (e0 + e1), e1 / (e0 + e1)


def _hg_gates(qr, fr, lb, pos, tb):
    sq = _sigmoid(qr)
    q = qr * sq * Q_SCALE
    sg = _sigmoid(fr)
    f = lb + (1.0 - lb) * sg
    k = 1.0 - f
    b = _chunk_cumsum(jnp.log(f), pos)
    b3 = b.reshape(tb // CHUNK, CHUNK, HEAD_DIM)
    anc = _chunk_bcast(b3, CHUNK // 2 - 1, tb)
    blb = _chunk_bcast(b3, CHUNK - 1, tb)
    e_qa = jnp.exp(b - anc)
    e_ka = jnp.exp(anc - b)
    e_b = jnp.exp(b)
    e_ko = jnp.exp(blb - b)
    dec = jnp.exp(blb)
    return sq, q, sg, f, k, e_qa, e_ka, e_b, e_ko, dec


def _intra_mask(sb):
    r = lax.broadcasted_iota(jnp.int32, (sb, sb), 0)
    c = lax.broadcasted_iota(jnp.int32, (sb, sb), 1)
    return ((r // CHUNK) == (c // CHUNK)) & (c <= r)


def _hg_fwd(hg, low, gn):
    T = hg.shape[0]
    tb = min(1024, T)
    sb = min(256, tb)
    nb = T // tb
    nc = tb // CHUNK

    def body(q_ref, f_ref, i_ref, g_ref, low_ref, gn_ref, o_ref, og_ref, ogt_ref, st_ref, s_scr):
        t = pl.program_id(1)

        @pl.when(t == 0)
        def _():
            s_scr[...] = jnp.zeros_like(s_scr)

        pos = _chunk_pos((tb, HEAD_DIM))
        lb, _ = _lower_bound(low_ref)
        v = i_ref[...]
        _, q, _, _, k, e_qa, e_ka, e_b, e_ko, dec = _hg_gates(q_ref[...], f_ref[...], lb, pos, tb)
        qh = (q * e_qa).astype(BF16)
        kh = (k * e_ka).astype(BF16)
        qi = (q * e_b).astype(BF16)
        ko = (k * e_ko).astype(BF16)
        vb = v.astype(BF16)
        mask = _intra_mask(sb)
        for s in range(tb // sb):
            sl = slice(s * sb, (s + 1) * sb)
            p = jnp.where(mask, _mm_nt(qh[sl], kh[sl]), 0.0)
            o_ref[sl, :] = _mm(p, vb[sl])
        st = s_scr[...]
        for c in range(nc):
            sl = slice(c * CHUNK, (c + 1) * CHUNK)
            st_ref[c] = st
            o_ref[sl, :] = o_ref[sl, :] + _mm_nt(qi[sl], st)
            st = dec[c * CHUNK:c * CHUNK + 1, :] * st + _mm_tn(vb[sl], ko[sl])
        s_scr[...] = st
        o = o_ref[...]
        r = lax.rsqrt(jnp.mean(o * o, axis=-1, keepdims=True) + EPS)
        gr = g_ref[...]
        og = (o * r * gn_ref[...]) * (gr * _sigmoid(gr))
        og_ref[...] = og.astype(BF16)
        ogt_ref[...] = og.T.astype(BF16)

    col = lambda p: pl.BlockSpec((tb, HEAD_DIM), lambda h, t: (t, p * N_HEADS + h))
    hcol = pl.BlockSpec((tb, HEAD_DIM), lambda h, t: (t, h))
    return pl.pallas_call(
        body, name="hg_fwd", grid=(N_HEADS, nb),
        in_specs=[col(0), col(1), col(2), col(3), pl.BlockSpec((2, HEAD_DIM), lambda h, t: (0, h)),
                  pl.BlockSpec((1, HEAD_DIM), lambda h, t: (0, 0))],
        out_specs=[hcol, hcol, pl.BlockSpec((HEAD_DIM, tb), lambda h, t: (h, t)),
                   pl.BlockSpec((None, nc, HEAD_DIM, HEAD_DIM), lambda h, t: (h, t, 0, 0))],
        out_shape=[jax.ShapeDtypeStruct((T, HG_WIDTH), F32), jax.ShapeDtypeStruct((T, HG_WIDTH), BF16),
                   jax.ShapeDtypeStruct((HG_WIDTH, T), BF16),
                   jax.ShapeDtypeStruct((N_HEADS, T // CHUNK, HEAD_DIM, HEAD_DIM), F32)],
        scratch_shapes=[pltpu.VMEM((HEAD_DIM, HEAD_DIM), F32)],
        compiler_params=_params(("parallel", "arbitrary")),
    )(hg, hg, hg, hg, low, gn)


def _conv_fwd(cv, conv_w):
    T = cv.shape[0]
    nj = CONV_WIDTH // 128

    def body(c_ref, b_ref, x_ref, w_ref, o_ref, ot_ref):
        row = lax.broadcasted_iota(jnp.int32, (T, 128), 0)
        u = c_ref[...] * x_ref[...]
        u1 = jnp.where(row >= 1, pltpu.roll(u, 1, 0), 0.0)
        u2 = jnp.where(row >= 2, pltpu.roll(u, 2, 0), 0.0)
        y = w_ref[0:1, :] * u2 + w_ref[1:2, :] * u1 + w_ref[2:3, :] * u
        out = b_ref[...] * y
        o_ref[...] = out.astype(BF16)
        ot_ref[...] = out.T.astype(BF16)

    col = lambda p: pl.BlockSpec((T, 128), lambda j: (0, p * nj + j))
    return pl.pallas_call(
        body, name="conv_fwd", grid=(nj,),
        in_specs=[col(0), col(1), col(2), pl.BlockSpec((3, 128), lambda j: (0, j))],
        out_specs=[pl.BlockSpec((T, 128), lambda j: (0, j)), pl.BlockSpec((128, T), lambda j: (j, 0))],
        out_shape=[jax.ShapeDtypeStruct((T, CONV_WIDTH), BF16), jax.ShapeDtypeStruct((CONV_WIDTH, T), BF16)],
        compiler_params=_params(("parallel",)),
    )(cv, cv, cv, conv_w)


def _merge_fwd(og, cvo, gt, x, wa, wb, wo):
    T = x.shape[0]
    tm = min(512, T)

    def body(og_ref, cvo_ref, gt_ref, x_ref, wa_ref, wb_ref, wo_ref, x1_ref, mgt_ref):
        ya = jnp.dot(og_ref[...], _shard_cols(wa_ref), preferred_element_type=F32)
        yb = jnp.dot(cvo_ref[...], _shard_cols(wb_ref), preferred_element_type=F32)
        m = _sigmoid(gt_ref[:, :D_MODEL]) * ya + _sigmoid(gt_ref[:, D_MODEL:]) * yb
        mgt_ref[...] = m.T.astype(BF16)
        x1_ref[...] = x_ref[...] + jnp.dot(m.astype(BF16), wo_ref[...], preferred_element_type=F32)

    row = lambda n: pl.BlockSpec((tm, n), lambda i: (i, 0))
    return pl.pallas_call(
        body, name="merge_fwd", grid=(T // tm,),
        in_specs=[row(HG_WIDTH), row(CONV_WIDTH), row(2 * D_MODEL), row(D_MODEL),
                  _resident(wa.shape), _resident(wb.shape), _resident(wo.shape)],
        out_specs=[row(D_MODEL), _col(tm, D_MODEL)],
        out_shape=[jax.ShapeDtypeStruct((T, D_MODEL), F32), jax.ShapeDtypeStruct((D_MODEL, T), BF16)],
        compiler_params=_params(("parallel",)),
    )(og, cvo, gt, x, wa, wb, wo)


def _ffn_fwd(x1, g, wg, wu, wd):
    T = x1.shape[0]
    tm = min(256, T)

    def body(x_ref, g_ref, wg_ref, wu_ref, wd_ref, ht_ref, gate_ref, up_ref, actt_ref, x2_ref):
        xv = x_ref[...]
        r = lax.rsqrt(jnp.mean(xv * xv, axis=-1, keepdims=True) + EPS)
        hf = xv * r * g_ref[...]
        h = hf.astype(BF16)
        ht_ref[...] = hf.T.astype(BF16)
        gate = _mm_nt(h, wg_ref[...])
        up = _mm_nt(h, wu_ref[...])
        gate_ref[...] = gate
        up_ref[...] = up
        act = gate * _sigmoid(gate) * up
        actt_ref[...] = act.T.astype(BF16)
        x2_ref[...] = xv + jnp.dot(act.astype(BF16), wd_ref[...], preferred_element_type=F32)

    row = lambda n: pl.BlockSpec((tm, n), lambda i: (i, 0))
    return pl.pallas_call(
        body, name="ffn_fwd", grid=(T // tm,),
        in_specs=[row(D_MODEL), _full((1, D_MODEL)), _resident(wg.shape), _resident(wu.shape), _resident(wd.shape)],
        out_specs=[_col(tm, D_MODEL), row(D_FF), row(D_FF), _col(tm, D_FF), row(D_MODEL)],
        out_shape=[jax.ShapeDtypeStruct((D_MODEL, T), BF16), jax.ShapeDtypeStruct((T, D_FF), F32),
                   jax.ShapeDtypeStruct((T, D_FF), F32), jax.ShapeDtypeStruct((D_FF, T), BF16),
                   jax.ShapeDtypeStruct((T, D_MODEL), F32)],
        compiler_params=_params(("parallel",)),
    )(x1, g, wg, wu, wd)


def _final_fwd_bwd(x2, target, g):
    T = x2.shape[0]
    tm = min(512, T)

    def body(x_ref, t_ref, g_ref, loss_ref, dg_ref, dx_ref):
        @pl.when(pl.program_id(0) == 0)
        def _():
            loss_ref[...] = jnp.zeros_like(loss_ref)
            dg_ref[...] = jnp.zeros_like(dg_ref)

        xv = x_ref[...]
        gv = g_ref[...]
        r = lax.rsqrt(jnp.mean(xv * xv, axis=-1, keepdims=True) + EPS)
        xh = xv * r
        err = xh * gv - t_ref[...]
        loss_ref[...] += 0.5 * jnp.sum(jnp.mean(err * err, axis=-1, keepdims=True), axis=0, keepdims=True)
        dy = err * (1.0 / D_MODEL)
        dg_ref[...] += jnp.sum(dy * xh, axis=0, keepdims=True)
        w = dy * gv
        dx_ref[...] = r * (w - xh * jnp.mean(w * xh, axis=-1, keepdims=True))

    row = pl.BlockSpec((tm, D_MODEL), lambda i: (i, 0))
    return pl.pallas_call(
        body, name="final_fwd_bwd", grid=(T // tm,),
        in_specs=[row, row, _full((1, D_MODEL))],
        out_specs=[_full((1, 128)), _full((1, D_MODEL)), row],
        out_shape=[jax.ShapeDtypeStruct((1, 128), F32), jax.ShapeDtypeStruct((1, D_MODEL), F32),
                   jax.ShapeDtypeStruct((T, D_MODEL), F32)],
        compiler_params=_params(("arbitrary",)),
    )(x2, target, g)


def _ffn_bwd(dx2, x1, gate, up, g, wg, wu, wd):
    T = x1.shape[0]
    tm = min(256, T)

    def body(dx2_ref, x_ref, gate_ref, up_ref, g_ref, wg_ref, wu_ref, wd_ref, dgate_ref, dup_ref, dx1_ref, dgn_ref):
        @pl.when(pl.program_id(0) == 0)
        def _():
            dgn_ref[...] = jnp.zeros_like(dgn_ref)

        dx2 = dx2_ref[...]
        dact = _mm_nt(dx2, wd_ref[...])
        gate = gate_ref[...]
        s = _sigmoid(gate)
        dgate = (dact * up_ref[...] * (s * (1.0 + gate * (1.0 - s)))).astype(BF16)
        dup = (dact * (gate * s)).astype(BF16)
        dgate_ref[...] = dgate
        dup_ref[...] = dup
        dh = _mm(dgate, wg_ref[...]) + _mm(dup, wu_ref[...])
        xv = x_ref[...]
        r = lax.rsqrt(jnp.mean(xv * xv, axis=-1, keepdims=True) + EPS)
        xh = xv * r
        dgn_ref[...] += jnp.sum(dh * xh, axis=0, keepdims=True)
        w = dh * g_ref[...]
        dx1_ref[...] = dx2 + r * (w - xh * jnp.mean(w * xh, axis=-1, keepdims=True))

    row = lambda n: pl.BlockSpec((tm, n), lambda i: (i, 0))
    return pl.pallas_call(
        body, name="ffn_bwd", grid=(T // tm,),
        in_specs=[row(D_MODEL), row(D_MODEL), row(D_FF), row(D_FF), _full((1, D_MODEL)),
                  _resident(wg.shape), _resident(wu.shape), _resident(wd.shape)],
        out_specs=[row(D_FF), row(D_FF), row(D_MODEL), _full((1, D_MODEL))],
        out_shape=[jax.ShapeDtypeStruct((T, D_FF), BF16), jax.ShapeDtypeStruct((T, D_FF), BF16),
                   jax.ShapeDtypeStruct((T, D_MODEL), F32), jax.ShapeDtypeStruct((1, D_MODEL), F32)],
        compiler_params=_params(("arbitrary",)),
    )(dx2, x1, gate, up, g, wg, wu, wd)


def _merge_bwd(dx1, og, cvo, gt, wa, wb, wo):
    T = dx1.shape[0]
    tm = min(512, T)

    def body(dx_ref, og_ref, cvo_ref, gt_ref, wa_ref, wb_ref, wo_ref, dgt_ref, dya_ref, dyb_ref, dog_ref, dcvo_ref):
        dm = _mm_nt(dx_ref[...], wo_ref[...])
        wa = _shard_cols(wa_ref)
        wb = _shard_cols(wb_ref)
        ya = jnp.dot(og_ref[...], wa, preferred_element_type=F32)
        yb = jnp.dot(cvo_ref[...], wb, preferred_element_type=F32)
        sa = _sigmoid(gt_ref[:, :D_MODEL])
        sb = _sigmoid(gt_ref[:, D_MODEL:])
        dgt_ref[:, :D_MODEL] = (dm * ya * (sa * (1.0 - sa))).astype(BF16)
        dgt_ref[:, D_MODEL:] = (dm * yb * (sb * (1.0 - sb))).astype(BF16)
        dya = (dm * sa).astype(BF16)
        dyb = (dm * sb).astype(BF16)
        dya_ref[...] = dya
        dyb_ref[...] = dyb
        dog_ref[...] = _mm_nt(dya, wa)
        dcvo_ref[...] = _mm_nt(dyb, wb)

    row = lambda n: pl.BlockSpec((tm, n), lambda i: (i, 0))
    return pl.pallas_call(
        body, name="merge_bwd", grid=(T // tm,),
        in_specs=[row(D_MODEL), row(HG_WIDTH), row(CONV_WIDTH), row(2 * D_MODEL),
                  _resident(wa.shape), _resident(wb.shape), _resident(wo.shape)],
        out_specs=[row(2 * D_MODEL), row(D_MODEL), row(D_MODEL), row(HG_WIDTH), row(CONV_WIDTH)],
        out_shape=[jax.ShapeDtypeStruct((T, 2 * D_MODEL), BF16), jax.ShapeDtypeStruct((T, D_MODEL), BF16),
                   jax.ShapeDtypeStruct((T, D_MODEL), BF16), jax.ShapeDtypeStruct((T, HG_WIDTH), F32),
                   jax.ShapeDtypeStruct((T, CONV_WIDTH), F32)],
        compiler_params=_params(("parallel",)),
    )(dx1, og, cvo, gt, wa, wb, wo)


def _conv_bwd(dcvo, cv, conv_w):
    T = cv.shape[0]
    nj = CONV_WIDTH // 128

    def body(do_ref, c_ref, b_ref, x_ref, w_ref, dc_ref, db_ref, dx_ref, dw_ref):
        row = lax.broadcasted_iota(jnp.int32, (T, 128), 0)
        c = c_ref[...]
        xb = x_ref[...]
        do = do_ref[...]
        u = c * xb
        u1 = jnp.where(row >= 1, pltpu.roll(u, 1, 0), 0.0)
        u2 = jnp.where(row >= 2, pltpu.roll(u, 2, 0), 0.0)
        w0, w1, w2 = w_ref[0:1, :], w_ref[1:2, :], w_ref[2:3, :]
        y = w0 * u2 + w1 * u1 + w2 * u
        db_ref[...] = (do * y).astype(BF16)
        dy = do * b_ref[...]
        dw_ref[0:1, :] = jnp.sum(dy * u2, axis=0, keepdims=True)
        dw_ref[1:2, :] = jnp.sum(dy * u1, axis=0, keepdims=True)
        dw_ref[2:3, :] = jnp.sum(dy * u, axis=0, keepdims=True)
        dy1 = jnp.where(row < T - 1, pltpu.roll(dy, T - 1, 0), 0.0)
        dy2 = jnp.where(row < T - 2, pltpu.roll(dy, T - 2, 0), 0.0)
        du = w2 * dy + w1 * dy1 + w0 * dy2
        dc_ref[...] = (du * xb).astype(BF16)
        dx_ref[...] = (du * c).astype(BF16)

    col = lambda p: pl.BlockSpec((T, 128), lambda j: (0, p * nj + j))
    one = pl.BlockSpec((T, 128), lambda j: (0, j))
    wspec = pl.BlockSpec((3, 128), lambda j: (0, j))
    out = jax.ShapeDtypeStruct((T, CONV_WIDTH), BF16)
    return pl.pallas_call(
        body, name="conv_bwd", grid=(nj,),
        in_specs=[one, col(0), col(1), col(2), wspec],
        out_specs=[one, one, one, wspec],
        out_shape=[out, out, out, jax.ShapeDtypeStruct((3, CONV_WIDTH), F32)],
        compiler_params=_params(("parallel",)),
    )(dcvo, cv, cv, cv, conv_w)


def _drop_operands(body, first, count):
    def wrapped(*refs):
        return body(*refs[:first], *refs[first + count:])
    return wrapped


def _hg_bwd(dog, hg, o, st, low, gn, after=()):
    T = hg.shape[0]
    tb = min(1024, T)
    sb = min(256, tb)
    nb = T // tb
    nc = tb // CHUNK

    def body(q_ref, f_ref, i_ref, g_ref, low_ref, gn_ref, o_ref, dog_ref, st_ref,
             dq_ref, df_ref, di_ref, dg_ref, dlow_ref, dgn_ref,
             ds_scr, dqi_scr, dko_scr, dv_scr, dd_scr, dqh_scr, dkh_scr):
        h = pl.program_id(0)
        t = pl.program_id(1)

        @pl.when(t == 0)
        def _():
            ds_scr[...] = jnp.zeros_like(ds_scr)
            dlow_ref[...] = jnp.zeros_like(dlow_ref)

        @pl.when((t == 0) & (h == 0))
        def _():
            dgn_ref[...] = jnp.zeros_like(dgn_ref)

        pos = _chunk_pos((tb, HEAD_DIM))
        lb, lb1 = _lower_bound(low_ref)
        qr = q_ref[...]
        v = i_ref[...]
        sq, q, sg, f, k, e_qa, e_ka, e_b, e_ko, dec = _hg_gates(qr, f_ref[...], lb, pos, tb)

        gr = g_ref[...]
        gnv = gn_ref[...]
        o = o_ref[...]
        dog_v = dog_ref[...]
        sgr = _sigmoid(gr)
        r = lax.rsqrt(jnp.mean(o * o, axis=-1, keepdims=True) + EPS)
        oh = o * r
        dg_ref[...] = (dog_v * (oh * gnv) * (sgr * (1.0 + gr * (1.0 - sgr)))).astype(BF16)
        don = dog_v * (gr * sgr)
        dgn_ref[...] += jnp.sum(don * oh, axis=0, keepdims=True)
        w = don * gnv
        do = (r * (w - oh * jnp.mean(w * oh, axis=-1, keepdims=True))).astype(BF16)

        qh = (q * e_qa).astype(BF16)
        kh = (k * e_ka).astype(BF16)
        qi = (q * e_b).astype(BF16)
        ko = (k * e_ko).astype(BF16)
        vb = v.astype(BF16)

        mask = _intra_mask(sb)
        for s in range(tb // sb):
            sl = slice(s * sb, (s + 1) * sb)
            p = jnp.where(mask, _mm_nt(qh[sl], kh[sl]), 0.0).astype(BF16)
            dp = jnp.where(mask, _mm_nt(do[sl], vb[sl]), 0.0).astype(BF16)
            dv_scr[sl, :] = _mm_tn(p, do[sl])
            dqh_scr[sl, :] = _mm(dp, kh[sl])
            dkh_scr[sl, :] = _mm_tn(dp, qh[sl])

        ds = ds_scr[...]
        for c in reversed(range(nc)):
            sl = slice(c * CHUNK, (c + 1) * CHUNK)
            st_c = st_ref[c]
            dqi_scr[sl, :] = _mm(do[sl], st_c)
            dko_scr[sl, :] = _mm(vb[sl], ds)
            dv_scr[sl, :] = dv_scr[sl, :] + _mm_nt(ko[sl], ds)
            dd_scr[sl, :] = jnp.broadcast_to(jnp.sum(ds * st_c, axis=0, keepdims=True), (CHUNK, HEAD_DIM))
            ds = dec[c * CHUNK:c * CHUNK + 1, :] * ds + _mm_tn(do[sl], qi[sl])
        ds_scr[...] = ds

        dko_e = dko_scr[...] * e_ko
        dq = dqh_scr[...] * e_qa + dqi_scr[...] * e_b
        dk = dkh_scr[...] * e_ka + dko_e
        kd3 = (k * dko_e).reshape(nc, CHUNK, HEAD_DIM)
        last = jnp.broadcast_to(jnp.sum(kd3, axis=1, keepdims=True), kd3.shape).reshape(tb, HEAD_DIM)
        db = q * dq - k * dk + jnp.where(pos == CHUNK - 1, dec * dd_scr[...] + last, 0.0)
        dlg = _chunk_rev_cumsum(db, pos)
        dfv = dlg / f - dk
        s_low = jnp.sum(dfv * (1.0 - sg), axis=0, keepdims=True)
        dlow_ref[0:1, :] += s_low * lb * (1.0 - lb)
        dlow_ref[1:2, :] += -s_low * lb * lb1
        df_ref[...] = (dfv * (1.0 - lb) * sg * (1.0 - sg)).astype(BF16)
        dq_ref[...] = (dq * Q_SCALE * (sq * (1.0 + qr * (1.0 - sq)))).astype(BF16)
        di_ref[...] = dv_scr[...].astype(BF16)

    rt = lambda t: nb - 1 - t
    col = lambda p: pl.BlockSpec((tb, HEAD_DIM), lambda h, t: (rt(t), p * N_HEADS + h))
    hcol = pl.BlockSpec((tb, HEAD_DIM), lambda h, t: (rt(t), h))
    piece = jax.ShapeDtypeStruct((T, HG_WIDTH), BF16)
    tile = pltpu.VMEM((tb, HEAD_DIM), F32)
    return pl.pallas_call(
        _drop_operands(body, 9, len(after)), name="hg_bwd", grid=(N_HEADS, nb),
        in_specs=[col(0), col(1), col(2), col(3), pl.BlockSpec((2, HEAD_DIM), lambda h, t: (0, h)),
                  pl.BlockSpec((1, HEAD_DIM), lambda h, t: (0, 0)), hcol, hcol,
                  pl.BlockSpec((None, nc, HEAD_DIM, HEAD_DIM), lambda h, t: (h, rt(t), 0, 0))]
                 + [HBM_SPEC] * len(after),
        out_specs=[hcol, hcol, hcol, hcol, pl.BlockSpec((2, HEAD_DIM), lambda h, t: (0, h)),
                   pl.BlockSpec((1, HEAD_DIM), lambda h, t: (0, 0))],
        out_shape=[piece, piece, piece, piece, jax.ShapeDtypeStruct((2, HG_WIDTH), F32),
                   jax.ShapeDtypeStruct((1, HEAD_DIM), F32)],
        scratch_shapes=[pltpu.VMEM((HEAD_DIM, HEAD_DIM), F32), tile, tile, tile, tile, tile, tile],
        compiler_params=_params(("arbitrary", "arbitrary")),
    )(hg, hg, hg, hg, low, gn, o, dog, st, *after)


def _in_bwd(dparts, w_in, x, dx1, g, after=()):
    T = x.shape[0]
    tm = min(512, T)
    widths = [p.shape[1] for p in dparts]
    offs = [sum(widths[:i]) for i in range(len(widths))]
    n = len(dparts)

    def body(*refs):
        d_refs = refs[:n]
        w_ref, x_ref, dx1_ref, g_ref, dx_ref, dgn_ref = refs[n:]

        @pl.when(pl.program_id(0) == 0)
        def _():
            dgn_ref[...] = jnp.zeros_like(dgn_ref)

        dh = None
        for d_ref, off, wd in zip(d_refs, offs, widths):
            part = _mm(d_ref[...], w_ref[off:off + wd, :])
            dh = part if dh is None else dh + part
        xv = x_ref[...]
        r = lax.rsqrt(jnp.mean(xv * xv, axis=-1, keepdims=True) + EPS)
        xh = xv * r
        dgn_ref[...] += jnp.sum(dh * xh, axis=0, keepdims=True)
        w = dh * g_ref[...]
        dx_ref[...] = dx1_ref[...] + r * (w - xh * jnp.mean(w * xh, axis=-1, keepdims=True))

    row = lambda m: pl.BlockSpec((tm, m), lambda i: (i, 0))
    return pl.pallas_call(
        _drop_operands(body, n + 4, len(after)), name="in_bwd", grid=(T // tm,),
        in_specs=[row(wd) for wd in widths] + [_resident(w_in.shape), row(D_MODEL), row(D_MODEL), _full((1, D_MODEL))]
                 + [HBM_SPEC] * len(after),
        out_specs=[row(D_MODEL), _full((1, D_MODEL))],
        out_shape=[jax.ShapeDtypeStruct((T, D_MODEL), F32), jax.ShapeDtypeStruct((1, D_MODEL), F32)],
        compiler_params=_params(("arbitrary",)),
    )(*dparts, w_in, x, dx1, g, *after)


def _wgrad(name, at, b, tn, transposed=False):
    M, T = at.shape
    N = b.shape[1]
    tk = min(2048, T)
    nk = T // tk

    def body(a_ref, b_ref, o_ref, acc):
        k = pl.program_id(1)
        part = _mm(a_ref[...], b_ref[...])

        @pl.when(k == 0)
        def _():
            acc[...] = part

        @pl.when(k != 0)
        def _():
            acc[...] += part

        @pl.when(k == nk - 1)
        def _():
            o_ref[...] = (acc[...].T if transposed else acc[...]).astype(BF16)

    if transposed:
        out_spec, out_shape = pl.BlockSpec((tn, M), lambda j, k: (j, 0)), (N, M)
    else:
        out_spec, out_shape = pl.BlockSpec((M, tn), lambda j, k: (0, j)), (M, N)
    return pl.pallas_call(
        body, name=name, grid=(N // tn, nk),
        in_specs=[pl.BlockSpec((M, tk), lambda j, k: (0, k)), pl.BlockSpec((tk, tn), lambda j, k: (k, j))],
        out_specs=out_spec, out_shape=jax.ShapeDtypeStruct(out_shape, BF16),
        scratch_shapes=[pltpu.VMEM((M, tn), F32)],
        compiler_params=_params(("parallel", "arbitrary")),
    )(at, b)


def _wgrad_in(ht, dparts, after=()):
    M, T = ht.shape
    tn = 512
    tk = min(2048, T)
    nk = T // tk
    nblk = [p.shape[1] // tn for p in dparts]
    start = [sum(nblk[:i]) for i in range(len(nblk))]
    n = len(dparts)

    def body(a_ref, *refs):
        d_refs, o_ref, acc = refs[:n], refs[n], refs[n + 1]
        j = pl.program_id(0)
        k = pl.program_id(1)

        @pl.when(k == 0)
        def _():
            acc[...] = jnp.zeros_like(acc)

        for d_ref, s, nb in zip(d_refs, start, nblk):
            @pl.when((j >= s) & (j < s + nb))
            def _():
                acc[...] += _mm(a_ref[...], d_ref[...])

        @pl.when(k == nk - 1)
        def _():
            o_ref[...] = acc[...].T.astype(BF16)

    def piece_spec(s, nb):
        def index(j, k):
            inside = (j >= s) & (j < s + nb)
            return jnp.where(inside, k, 0), jnp.clip(j - s, 0, nb - 1)
        return pl.BlockSpec((tk, tn), index)

    return pl.pallas_call(
        _drop_operands(body, 1 + n, len(after)), name="wgrad_in", grid=(sum(nblk), nk),
        in_specs=[pl.BlockSpec((M, tk), lambda j, k: (0, k))] + [piece_spec(s, nb) for s, nb in zip(start, nblk)]
                 + [HBM_SPEC] * len(after),
        out_specs=pl.BlockSpec((tn, M), lambda j, k: (j, 0)),
        out_shape=jax.ShapeDtypeStruct((sum(nblk) * tn, M), BF16),
        scratch_shapes=[pltpu.VMEM((M, tn), F32)],
        compiler_params=_params(("parallel", "arbitrary")),
    )(ht, *dparts, *after)


def _adamw_math(w, g, m, v):
    m = ADAM_B1 * m + (1.0 - ADAM_B1) * g
    v = ADAM_B2 * v + (1.0 - ADAM_B2) * (g * g)
    m_hat = m / (1.0 - ADAM_B1 ** ADAM_STEP)
    v_hat = v / (1.0 - ADAM_B2 ** ADAM_STEP)
    delta = -ADAM_LR * (m_hat / (jnp.sqrt(v_hat) + ADAM_EPS) + ADAM_WD * w)
    return delta, m, v


def _adamw_sum(name, w, parts, m, v):
    R, C = w.shape
    tr = _row_tile(R)

    def body(w_ref, p_ref, m_ref, v_ref, g_out, d_out, m_out, v_out):
        g = p_ref[0].astype(F32)
        for k in range(1, 4):
            g = g + p_ref[k].astype(F32)
        g_out[...] = g
        d_out[...], m_out[...], v_out[...] = _adamw_math(w_ref[...], g, m_ref[...], v_ref[...])

    blk = pl.BlockSpec((tr, C), lambda i: (i, 0))
    out = jax.ShapeDtypeStruct((R, C), F32)
    return pl.pallas_call(
        body, name=name, grid=(R // tr,),
        in_specs=[blk, pl.BlockSpec((4, tr, C), lambda i: (0, i, 0)), blk, blk],
        out_specs=[blk, blk, blk, blk], out_shape=[out, out, out, out],
        compiler_params=_params(("parallel",)),
    )(w, parts, m, v)


def _small_sum(gathered):
    R = gathered.shape[1]

    def body(p_ref, o_ref):
        g = p_ref[0]
        for k in range(1, N_DEV):
            g = g + p_ref[k]
        o_ref[...] = g

    return pl.pallas_call(
        body, name="small_sum", in_specs=[_full(gathered.shape)], out_specs=_full((R, 128)), grid=(1,),
        out_shape=jax.ShapeDtypeStruct((R, 128), F32),
    )(gathered)


def _small_adamw(w, g, m, v):
    def body(w_ref, g_ref, m_ref, v_ref, d_out, m_out, v_out):
        d_out[...], m_out[...], v_out[...] = _adamw_math(w_ref[...], g_ref[...], m_ref[...], v_ref[...])

    out = jax.ShapeDtypeStruct(w.shape, F32)
    spec = _full(w.shape)
    return pl.pallas_call(
        body, name="small_adamw", grid=(1,), in_specs=[spec] * 4, out_specs=[spec] * 3, out_shape=[out, out, out],
    )(w, g, m, v)


def _row_tile(rows):
    for cand in (256, 128):
        if rows % cand == 0:
            return cand
    return rows


def _pair_sum(name, by_owner, got, core):
    _, R, C = got.shape
    tr = _row_tile(R)

    def body(core_ref, a_ref, b_ref, o_ref):
        o_ref[...] = (a_ref[...].astype(F32) + b_ref[...].astype(F32)).astype(BF16)

    blk = pl.BlockSpec((None, tr, C), lambda k, i, core_ref: (k, i, 0))
    mine = pl.BlockSpec((None, tr, C), lambda k, i, core_ref: (2 * k + core_ref[0], i, 0))
    return pl.pallas_call(
        body, name=name,
        grid_spec=pltpu.PrefetchScalarGridSpec(num_scalar_prefetch=1, grid=(4, R // tr), in_specs=[mine, blk],
                                               out_specs=blk),
        out_shape=jax.ShapeDtypeStruct(got.shape, BF16),
        compiler_params=_params(("parallel", "parallel")),
    )(core, by_owner, got)


def _shards_from_cols(name, full):
    R, allc = full.shape
    c = allc // N_DEV
    tr = _row_tile(R)

    def body(f_ref, o_ref):
        for s in range(N_DEV):
            o_ref[s] = f_ref[:, s * c:(s + 1) * c]

    return pl.pallas_call(
        body, name=name, grid=(R // tr,),
        in_specs=[pl.BlockSpec((tr, allc), lambda i: (i, 0))],
        out_specs=pl.BlockSpec((N_DEV, tr, c), lambda i: (0, i, 0)),
        out_shape=jax.ShapeDtypeStruct((N_DEV, R, c), full.dtype),
        compiler_params=_params(("parallel",)),
    )(full)


MESH = pl.DeviceIdType.MESH
HBM_SPEC = pl.BlockSpec(memory_space=pl.ANY)


def _handshake(peers):
    barrier = pltpu.get_barrier_semaphore()
    for peer in peers:
        pl.semaphore_signal(barrier, inc=1, device_id=peer, device_id_type=MESH)
    pl.semaphore_wait(barrier, len(peers))


def _comm_call(body, name, operands, out_shape, scratch, collective_id):
    if collective_id is None:
        return pl.pallas_call(body, name=name, in_specs=[HBM_SPEC] * len(operands), out_specs=[HBM_SPEC] * len(out_shape),
                              out_shape=out_shape, scratch_shapes=scratch)(*operands)
    return pl.kernel(body, out_type=out_shape, mesh=plsc.ScalarSubcoreMesh(axis_name="sequencer", num_cores=1),
                     scratch_types=scratch, name=name,
                     compiler_params=pltpu.CompilerParams(collective_id=collective_id))(*operands)


def _all_gather(name, blocks, collective_id=None, after=()):
    n = len(blocks)
    na = len(after)

    def body(*refs):
        x_refs, out_refs = refs[:n], refs[n + na:2 * n + na]
        send_sems, recv_sems, local_sems = refs[2 * n + na:]
        x, y, c = lax.axis_index("x"), lax.axis_index("y"), lax.axis_index("c")
        me, sibling = (x, y, c), (x, y, 1 - c)
        chips = [(1 - x, y), (x, 1 - y), (1 - x, 1 - y)]
        if collective_id is not None:
            _handshake([sibling] + [(*chip, c) for chip in chips])

        def slot(i, px, py, pc):
            return out_refs[i].at[4 * px + 2 * py + pc]

        def copy(i, k, blk, to, src=None):
            return pltpu.make_async_remote_copy(
                src_ref=slot(i, *blk) if src is None else src, dst_ref=slot(i, *blk),
                send_sem=send_sems.at[7 * i + k], recv_sem=recv_sems.at[7 * i + k], device_id=to, device_id_type=MESH)

        mine = [pltpu.make_async_copy(x_refs[i], slot(i, *me), local_sems.at[i]) for i in range(n)]
        for cp in mine:
            cp.start()
        first = []
        for i in range(n):
            first.append(copy(i, 0, me, sibling, src=x_refs[i]))
            first += [copy(i, 1 + j, me, (*chip, c), src=x_refs[i]) for j, chip in enumerate(chips)]
        for cp in first:
            cp.start()
        passed = []
        for i in range(n):
            for j, chip in enumerate(chips):
                copy(i, 1 + j, (*chip, c), me).wait_recv()
                passed.append(copy(i, 4 + j, (*chip, c), sibling))
                passed[-1].start()
        for i in range(n):
            copy(i, 0, sibling, me).wait_recv()
            for j, chip in enumerate(chips):
                copy(i, 4 + j, (*chip, 1 - c), me).wait_recv()
        for cp in first + passed:
            cp.wait_send()
        for cp in mine:
            cp.wait()

    return _comm_call(
        body, name, list(blocks) + list(after), [jax.ShapeDtypeStruct((N_DEV,) + b.shape, b.dtype) for b in blocks],
        [pltpu.SemaphoreType.DMA((7 * n,)), pltpu.SemaphoreType.DMA((7 * n,)), pltpu.SemaphoreType.DMA((n,))],
        collective_id)


def _sibling_swap(name, by_owner, collective_id=None, after=()):
    n = len(by_owner)
    na = len(after)

    def body(*refs):
        x_refs, out_refs = refs[:n], refs[n + na:2 * n + na]
        send_sems, recv_sems = refs[2 * n + na:]
        x, y, c = lax.axis_index("x"), lax.axis_index("y"), lax.axis_index("c")
        if collective_id is not None:
            _handshake([(x, y, 1 - c)])
        copies = []
        for i in range(n):
            for k in range(4):
                copies.append(pltpu.make_async_remote_copy(
                    src_ref=x_refs[i].at[2 * k + 1 - c], dst_ref=out_refs[i].at[k],
                    send_sem=send_sems.at[4 * i + k], recv_sem=recv_sems.at[4 * i + k],
                    device_id=(x, y, 1 - c), device_id_type=MESH))
        for cp in copies:
            cp.start()
        for cp in copies:
            cp.wait()

    return _comm_call(
        body, name, list(by_owner) + list(after),
        [jax.ShapeDtypeStruct((4,) + b.shape[1:], b.dtype) for b in by_owner],
        [pltpu.SemaphoreType.DMA((4 * n,)), pltpu.SemaphoreType.DMA((4 * n,))], collective_id)


def _chip_exchange(name, sums, collective_id=None, after=()):
    n = len(sums)
    na = len(after)

    def body(*refs):
        x_refs, out_refs = refs[:n], refs[n + na:2 * n + na]
        send_sems, recv_sems, local_sems = refs[2 * n + na:]
        x, y, c = lax.axis_index("x"), lax.axis_index("y"), lax.axis_index("c")
        chips = [(1 - x, y), (x, 1 - y), (1 - x, 1 - y)]
        my_chip = 2 * x + y
        if collective_id is not None:
            _handshake([(cx, cy, c) for cx, cy in chips])
        mine = [pltpu.make_async_copy(x_refs[i].at[my_chip], out_refs[i].at[my_chip], local_sems.at[i])
                for i in range(n)]
        for cp in mine:
            cp.start()
        sends = []
        for i in range(n):
            for j, (cx, cy) in enumerate(chips):
                sends.append(pltpu.make_async_remote_copy(
                    src_ref=x_refs[i].at[2 * cx + cy], dst_ref=out_refs[i].at[my_chip],
                    send_sem=send_sems.at[3 * i + j], recv_sem=recv_sems.at[3 * i + j],
                    device_id=(cx, cy, c), device_id_type=MESH))
        for cp in sends:
            cp.start()
        for i in range(n):
            for j, (cx, cy) in enumerate(chips):
                pltpu.make_async_remote_copy(
                    src_ref=x_refs[i].at[my_chip], dst_ref=out_refs[i].at[2 * cx + cy],
                    send_sem=send_sems.at[3 * i + j], recv_sem=recv_sems.at[3 * i + j],
                    device_id=(cx, cy, c), device_id_type=MESH).wait_recv()
        for cp in sends:
            cp.wait_send()
        for cp in mine:
            cp.wait()

    return _comm_call(
        body, name, list(sums) + list(after), [jax.ShapeDtypeStruct(s.shape, s.dtype) for s in sums],
        [pltpu.SemaphoreType.DMA((3 * n,)), pltpu.SemaphoreType.DMA((3 * n,)), pltpu.SemaphoreType.DMA((n,))],
        collective_id)


def _cast_shards(shards):
    n = len(shards)

    def body(*refs):
        for i in range(n):
            refs[n + i][...] = refs[i][...].astype(BF16)

    vmem = pl.BlockSpec(memory_space=pltpu.VMEM)
    return pl.pallas_call(
        body, name="cast_shards", in_specs=[vmem] * n, out_specs=[vmem] * n,
        out_shape=[jax.ShapeDtypeStruct(s.shape, BF16) for s in shards],
        compiler_params=pltpu.CompilerParams(vmem_limit_bytes=VMEM_LIMIT_V7X),
    )(*shards)


BIG = ("w_in", "w_branch_a", "w_branch_b", "w_out", "w_ffn_gate", "w_ffn_up", "w_ffn_down")


def _local_step(x, target, gains, low, conv_w, wg8, reduce):
    g_mix, g_hg, g_ffn, g_fin = gains
    w_in = wg8["w_in"].reshape(N_IN, D_MODEL)
    wg = wg8["w_ffn_gate"].reshape(D_FF, D_MODEL)
    wu = wg8["w_ffn_up"].reshape(D_FF, D_MODEL)
    wa, wb = wg8["w_branch_a"], wg8["w_branch_b"]
    wo = wg8["w_out"].reshape(D_MODEL, D_MODEL)
    wd = wg8["w_ffn_down"].reshape(D_FF, D_MODEL)

    ht, hg, cv, gt = _fwd_in(x, g_mix, w_in)
    o, og, ogt, st = _hg_fwd(hg, low, g_hg)
    cvo, cvot = _conv_fwd(cv, conv_w)
    x1, mgt = _merge_fwd(og, cvo, gt, x, wa, wb, wo)
    h2t, gate, up, actt, x2 = _ffn_fwd(x1, g_ffn, wg, wu, wd)
    loss, d_gfin, dx2 = _final_fwd_bwd(x2, target, g_fin)

    dgate, dup, dx1, d_gffn = _ffn_bwd(dx2, x1, gate, up, g_ffn, wg, wu, wd)
    ffn = dict(
        w_ffn_down=_wgrad("wgrad_ffn_down", actt, dx2, 512).reshape(N_DEV, D_FF // N_DEV, D_MODEL),
        w_ffn_gate=_wgrad("wgrad_ffn_gate", h2t, dgate, 1408, transposed=True).reshape(N_DEV, D_FF // N_DEV, D_MODEL),
        w_ffn_up=_wgrad("wgrad_ffn_up", h2t, dup, 1408, transposed=True).reshape(N_DEV, D_FF // N_DEV, D_MODEL))
    sums_ffn, got_ffn = reduce.begin(ffn)
    dgt, dya, dyb, dog, dcvo = _merge_bwd(dx1, og, cvo, gt, wa, wb, wo)
    out = dict(
        w_out=_wgrad("wgrad_out", mgt, dx1, 512).reshape(N_DEV, D_MODEL // N_DEV, D_MODEL),
        w_branch_a=_shards_from_cols("split_w_branch_a", _wgrad("wgrad_branch_a", ogt, dya, 512)),
        w_branch_b=_shards_from_cols("split_w_branch_b", _wgrad("wgrad_branch_b", cvot, dyb, 512)))
    sums_out, got_out = reduce.begin(out, after=got_ffn[:1])
    parts_ffn = reduce.finish(ffn, sums_ffn, after=got_out[:1])
    dc, db, dxb, d_conv = _conv_bwd(dcvo, cv, conv_w)
    dq, df, di, dg, d_low, d_ghg = _hg_bwd(dog, hg, o, st, low, g_hg, after=list(sums_ffn) + list(sums_out))
    parts_out = reduce.finish(out, sums_out, after=[parts_ffn[0], dq])
    dparts = [dq, df, di, dg, dc, db, dxb, dgt]
    w_in_grad = dict(w_in=_wgrad_in(ht, dparts, after=parts_ffn[:1]).reshape(N_DEV, N_IN // N_DEV, D_MODEL))
    sums_in, _ = reduce.begin(w_in_grad, after=parts_out[:1])
    parts_in = reduce.finish(w_in_grad, sums_in)
    grad_x, d_gmix = _in_bwd(dparts, w_in, x, dx1, g_mix, after=list(parts_out[:1]) + list(sums_in))
    small = dict(norm_mix_g=d_gmix, norm_ffn_g=d_gffn, norm_final_g=d_gfin, lower_bounds=d_low, hg_norm_g=d_ghg,
                 conv_w=d_conv, loss=loss)
    return grad_x, small, parts_in


_SMALL_LAYOUT = (("norm_mix_g", 0, 8), ("norm_ffn_g", 8, 8), ("norm_final_g", 16, 8), ("lower_bounds", 24, 8),
                 ("hg_norm_g", 32, 1))
_LOSS_ROW = 40
_CONV_ROW = 48


def _pad_rows(a, rows):
    return jnp.pad(a, ((0, rows - a.shape[0]), (0, 0)))


def _pack_small(vals, conv_rows):
    parts = [_pad_rows(vals[name].reshape(rows, 128), 8) for name, _, rows in _SMALL_LAYOUT]
    loss = vals["loss"][:, :128] if "loss" in vals else jnp.zeros((1, 128), F32)
    parts.append(_pad_rows(loss, 8))
    parts.append(_pad_rows(conv_rows, SMALL_ROWS - _CONV_ROW))
    return jnp.concatenate(parts, axis=0)


def _conv_shard_rows(a):
    return jnp.pad(a, ((0, 5), (0, 64)))


def kernel(x, norm_mix_g, w_in, lower_bounds, hg_norm_g, conv_w, w_branch_a, w_branch_b, w_out, norm_ffn_g, w_ffn_gate, w_ffn_up, w_ffn_down, norm_final_g, loss_target, m_norm_mix_g, m_w_in, m_lower_bounds, m_hg_norm_g, m_conv_w, m_w_branch_a, m_w_branch_b, m_w_out, m_norm_ffn_g, m_w_ffn_gate, m_w_ffn_up, m_w_ffn_down, m_norm_final_g, v_norm_mix_g, v_w_in, v_lower_bounds, v_hg_norm_g, v_conv_w, v_w_branch_a, v_w_branch_b, v_w_out, v_norm_ffn_g, v_w_ffn_gate, v_w_ffn_up, v_w_ffn_down, v_norm_final_g):
    cx, cy, cc = lax.axis_index("x"), lax.axis_index("y"), lax.axis_index("c")
    my_dev = 4 * cx + 2 * cy + cc

    def tr(a):
        return a[0].T

    big = dict(w_in=tr(w_in), w_branch_a=w_branch_a[0], w_branch_b=w_branch_b[0], w_out=w_out[0],
               w_ffn_gate=tr(w_ffn_gate), w_ffn_up=tr(w_ffn_up), w_ffn_down=w_ffn_down[0])
    big_m = dict(w_in=tr(m_w_in), w_branch_a=m_w_branch_a[0], w_branch_b=m_w_branch_b[0], w_out=m_w_out[0],
                 w_ffn_gate=tr(m_w_ffn_gate), w_ffn_up=tr(m_w_ffn_up), w_ffn_down=m_w_ffn_down[0])
    big_v = dict(w_in=tr(v_w_in), w_branch_a=v_w_branch_a[0], w_branch_b=v_w_branch_b[0], w_out=v_w_out[0],
                 w_ffn_gate=tr(v_w_ffn_gate), w_ffn_up=tr(v_w_ffn_up), w_ffn_down=v_w_ffn_down[0])
    transposed = ("w_in", "w_ffn_gate", "w_ffn_up")

    shards = dict(zip(BIG, _cast_shards([big[n] for n in BIG])))
    first = _all_gather("gather_w_in", [shards["w_in"], _conv_shard_rows(conv_w[0])])
    later = _all_gather("gather_rest", [shards[n] for n in BIG[1:]], collective_id=1, after=first[1:])
    wg8 = dict(zip(BIG, [first[0]] + list(later)))
    conv_full = first[1][:, :3, :64].transpose(1, 0, 2).reshape(3, CONV_WIDTH)

    core = cc.reshape(1).astype(jnp.int32)
    outs = {}
    ids = iter(range(2, 16))

    class Reduce:
        @staticmethod
        def begin(grads, after=()):
            names = list(grads)
            by_owner = [grads[n] for n in names]
            got = _sibling_swap("sibling_swap_" + names[0], by_owner, collective_id=next(ids), after=after)
            return [_pair_sum("pair_sum_" + n, a, b, core) for n, a, b in zip(names, by_owner, got)], got

        @staticmethod
        def finish(grads, chip_sums, after=()):
            names = list(grads)
            parts = _chip_exchange("chip_exchange_" + names[0], chip_sums, collective_id=next(ids), after=after)
            for n, p in zip(names, parts):
                outs[n] = _adamw_sum("adamw_" + n, big[n], p, big_m[n], big_v[n])
            return parts

    gains = (norm_mix_g, hg_norm_g, norm_ffn_g, norm_final_g.reshape(1, D_MODEL))
    grad_x, small, last = _local_step(x[0], loss_target[0], gains, lower_bounds, conv_full, wg8, Reduce)

    small_all = _all_gather("gather_small", [_pack_small(small, small["conv_w"].reshape(12, 128))],
                            collective_id=next(ids), after=last[:1])
    ssum = _small_sum(small_all[0])
    conv_g_full = ssum[_CONV_ROW:_CONV_ROW + 12].reshape(3, CONV_WIDTH)
    conv_g = lax.dynamic_slice_in_dim(conv_g_full, my_dev * 64, 64, axis=1)
    loss = ssum[_LOSS_ROW, 0]
    g_rows = jnp.concatenate([ssum[:_CONV_ROW], _pad_rows(_conv_shard_rows(conv_g), SMALL_ROWS - _CONV_ROW)], axis=0)

    def pack_state(a):
        vals = dict(norm_mix_g=a[0], norm_ffn_g=a[1], norm_final_g=a[2], lower_bounds=a[3], hg_norm_g=a[4])
        return _pack_small(vals, _conv_shard_rows(a[5][0]))

    sw = pack_state((norm_mix_g, norm_ffn_g, norm_final_g, lower_bounds, hg_norm_g, conv_w))
    sm = pack_state((m_norm_mix_g, m_norm_ffn_g, m_norm_final_g, m_lower_bounds, m_hg_norm_g, m_conv_w))
    sv = pack_state((v_norm_mix_g, v_norm_ffn_g, v_norm_final_g, v_lower_bounds, v_hg_norm_g, v_conv_w))
    s_delta, s_m, s_v = _small_adamw(sw, g_rows, sm, sv)

    shapes = dict(norm_mix_g=(1, D_MODEL), norm_ffn_g=(1, D_MODEL), norm_final_g=(D_MODEL,),
                  lower_bounds=(2, HG_WIDTH), hg_norm_g=(1, HEAD_DIM))

    def unpack(buf, name):
        if name == "conv_w":
            return buf[_CONV_ROW:_CONV_ROW + 3, :64].reshape(1, 3, 64)
        for nm, off, rows in _SMALL_LAYOUT:
            if nm == name:
                return buf[off:off + rows].reshape(shapes[name])
        raise KeyError(name)

    order = ["norm_mix_g", "w_in", "lower_bounds", "hg_norm_g", "conv_w", "w_branch_a", "w_branch_b", "w_out",
             "norm_ffn_g", "w_ffn_gate", "w_ffn_up", "w_ffn_down", "norm_final_g"]
    result = [loss, grad_x[None]]
    for k, sbuf in enumerate((g_rows, s_delta, s_m, s_v)):
        for n in order:
            if n in outs:
                result.append((outs[n][k].T if n in transposed else outs[n][k])[None])
            else:
                result.append(unpack(sbuf, n))
    return tuple(result)
```

```python
import jax
import jax.numpy as jnp
from jax import lax
from jax.experimental import pallas as pl
from jax.experimental.pallas import tpu as pltpu
from jax.experimental.pallas import tpu_sc as plsc

F32 = jnp.float32
BF16 = jnp.bfloat16

D_MODEL = 1024
HG_WIDTH = 512
HEAD_DIM = 128
N_HEADS = 4
HEADS_PER_STEP = 2
HEAD_GROUPS = N_HEADS // HEADS_PER_STEP
CONV_WIDTH = 512
D_FF = 2816
CHUNK = 32
EPS = 1e-6
Q_SCALE = HEAD_DIM ** -0.5
N_DEV = 8

ADAM_LR = 0.001
ADAM_B1 = 0.9
ADAM_B2 = 0.999
ADAM_EPS = 1e-08
ADAM_WD = 0.01
ADAM_STEP = 10

VMEM_LIMIT_V7X = 56 * 1024 * 1024

SMALL_ROWS = 64


def _params(sem, vmem=VMEM_LIMIT_V7X):
    return pltpu.CompilerParams(dimension_semantics=sem, vmem_limit_bytes=vmem)


def _mm(a, b):
    return jnp.dot(a.astype(BF16), b.astype(BF16), preferred_element_type=F32)


def _mm_nt(a, b):
    return lax.dot_general(a.astype(BF16), b.astype(BF16), (((1,), (1,)), ((), ())), preferred_element_type=F32)


def _mm_tn(a, b):
    return lax.dot_general(a.astype(BF16), b.astype(BF16), (((0,), (0,)), ((), ())), preferred_element_type=F32)


def _sigmoid(x):
    return 1.0 / (1.0 + jnp.exp(-x))


def _resident(shape):
    nd = len(shape)
    return pl.BlockSpec(shape, lambda *_: (0,) * nd, pipeline_mode=pl.Buffered(1))


def _full(shape):
    nd = len(shape)
    return pl.BlockSpec(shape, lambda *_: (0,) * nd)


def _shard_cols(w_ref):
    return jnp.concatenate([w_ref[s] for s in range(N_DEV)], axis=1)


N_HG = 4 * HG_WIDTH
N_CV = 3 * CONV_WIDTH
N_GT = 2 * D_MODEL
N_IN = N_HG + N_CV + N_GT


def _col(tm, n):
    return pl.BlockSpec((n, tm), lambda i: (0, i))


def _fwd_in(x, g, w_in_t):
    T = x.shape[0]
    tm = min(512, T)

    def body(x_ref, g_ref, w_ref, ht_ref, hg_ref, cv_ref, gt_ref):
        xv = x_ref[...]
        r = lax.rsqrt(jnp.mean(xv * xv, axis=-1, keepdims=True) + EPS)
        hf = xv * r * g_ref[...]
        h = hf.astype(BF16)
        ht_ref[...] = hf.T.astype(BF16)
        hg_ref[...] = _mm_nt(h, w_ref[:N_HG, :])
        cv_ref[...] = _mm_nt(h, w_ref[N_HG:N_HG + N_CV, :])
        gt_ref[...] = _mm_nt(h, w_ref[N_HG + N_CV:, :])

    row = lambda n: pl.BlockSpec((tm, n), lambda i: (i, 0))
    return pl.pallas_call(
        body, name="fwd_in", grid=(T // tm,),
        in_specs=[row(D_MODEL), _full((1, D_MODEL)), _resident(w_in_t.shape)],
        out_specs=[_col(tm, D_MODEL), row(N_HG), row(N_CV), row(N_GT)],
        out_shape=[jax.ShapeDtypeStruct((D_MODEL, T), BF16), jax.ShapeDtypeStruct((T, N_HG), F32),
                   jax.ShapeDtypeStruct((T, N_CV), F32), jax.ShapeDtypeStruct((T, N_GT), F32)],
        compiler_params=_params(("parallel",)),
    )(x, g, w_in_t)


def _chunk_pos(shape):
    return lax.broadcasted_iota(jnp.int32, shape, 0) & (CHUNK - 1)


def _chunk_cumsum(x, pos):
    s = 1
    while s < CHUNK:
        x = x + jnp.where(pos >= s, pltpu.roll(x, s, 0), 0.0)
        s *= 2
    return x


def _chunk_rev_cumsum(x, pos):
    n = x.shape[0]
    s = 1
    while s < CHUNK:
        x = x + jnp.where(pos + s < CHUNK, pltpu.roll(x, n - s, 0), 0.0)
        s *= 2
    return x


def _chunk_bcast(x3, row, tb):
    return jnp.broadcast_to(x3[:, row:row + 1, :], x3.shape).reshape(tb, x3.shape[-1])


def _lower_bound(low_ref):
    l0 = low_ref[0:1, :]
    l1 = low_ref[1:2, :]
    m = jnp.maximum(l0, l1)
    e0 = jnp.exp(l0 - m)
    e1 = jnp.exp(l1 - m)
    return e0 / (e0 + e1), e1 / (e0 + e1)


def _hg_gates(qr, fr, lb, pos, tb):
    sq = _sigmoid(qr)
    q = qr * sq * Q_SCALE
    sg = _sigmoid(fr)
    f = lb + (1.0 - lb) * sg
    k = 1.0 - f
    b = _chunk_cumsum(jnp.log(f), pos)
    b3 = b.reshape(tb // CHUNK, CHUNK, HEAD_DIM)
    anc = _chunk_bcast(b3, CHUNK // 2 - 1, tb)
    blb = _chunk_bcast(b3, CHUNK - 1, tb)
    e_qa = jnp.exp(b - anc)
    e_ka = jnp.exp(anc - b)
    e_b = jnp.exp(b)
    e_ko = jnp.exp(blb - b)
    dec = jnp.exp(blb)
    return sq, q, sg, f, k, e_qa, e_ka, e_b, e_ko, dec


def _intra_mask(sb):
    r = lax.broadcasted_iota(jnp.int32, (sb, sb), 0)
    c = lax.broadcasted_iota(jnp.int32, (sb, sb), 1)
    return ((r // CHUNK) == (c // CHUNK)) & (c <= r)


def _hg_fwd(hg, low, gn):
    T = hg.shape[0]
    tb = min(512, T)
    sb = min(256, tb)
    nb = T // tb
    nc = tb // CHUNK
    wid = HEADS_PER_STEP * HEAD_DIM

    def body(q_ref, f_ref, i_ref, g_ref, low_ref, gn_ref, o_ref, og_ref, ogt_ref, st_ref, s_scr):
        t = pl.program_id(1)

        @pl.when(t == 0)
        def _():
            s_scr[...] = jnp.zeros_like(s_scr)

        pos = _chunk_pos((tb, HEAD_DIM))
        mask = _intra_mask(sb)
        lanes = [slice(hh * HEAD_DIM, (hh + 1) * HEAD_DIM) for hh in range(HEADS_PER_STEP)]
        qi, ko, vb, dec, st = [], [], [], [], []
        for hh, ln in enumerate(lanes):
            lb, _ = _lower_bound(low_ref.at[:, ln])
            _, q, _, _, k, e_qa, e_ka, e_b, e_ko, dec_h = _hg_gates(q_ref[:, ln], f_ref[:, ln], lb, pos, tb)
            qh = (q * e_qa).astype(BF16)
            kh = (k * e_ka).astype(BF16)
            qi.append((q * e_b).astype(BF16))
            ko.append((k * e_ko).astype(BF16))
            vb.append(i_ref[:, ln].astype(BF16))
            dec.append(dec_h)
            st.append(s_scr[hh])
            for s in range(tb // sb):
                sl = slice(s * sb, (s + 1) * sb)
                p = jnp.where(mask, _mm_nt(qh[sl], kh[sl]), 0.0)
                o_ref[sl, ln] = _mm(p, vb[hh][sl])
        for c in range(nc):
            sl = slice(c * CHUNK, (c + 1) * CHUNK)
            for hh, ln in enumerate(lanes):
                st_ref[hh, c] = st[hh]
                o_ref[sl, ln] = o_ref[sl, ln] + _mm_nt(qi[hh][sl], st[hh])
                st[hh] = dec[hh][c * CHUNK:c * CHUNK + 1, :] * st[hh] + _mm_tn(vb[hh][sl], ko[hh][sl])
        for hh, ln in enumerate(lanes):
            s_scr[hh] = st[hh]
            o = o_ref[:, ln]
            r = lax.rsqrt(jnp.mean(o * o, axis=-1, keepdims=True) + EPS)
            gr = g_ref[:, ln]
            og = (o * r * gn_ref[...]) * (gr * _sigmoid(gr))
            og_ref[:, ln] = og.astype(BF16)
            ogt_ref[ln, :] = og.T.astype(BF16)

    col = lambda p: pl.BlockSpec((tb, wid), lambda h, t: (t, p * HEAD_GROUPS + h))
    hcol = pl.BlockSpec((tb, wid), lambda h, t: (t, h))
    return pl.pallas_call(
        body, name="hg_fwd", grid=(HEAD_GROUPS, nb),
        in_specs=[col(0), col(1), col(2), col(3), pl.BlockSpec((2, wid), lambda h, t: (0, h)),
                  pl.BlockSpec((1, HEAD_DIM), lambda h, t: (0, 0))],
        out_specs=[hcol, hcol, pl.BlockSpec((wid, tb), lambda h, t: (h, t)),
                   pl.BlockSpec((HEADS_PER_STEP, nc, HEAD_DIM, HEAD_DIM), lambda h, t: (h, t, 0, 0))],
        out_shape=[jax.ShapeDtypeStruct((T, HG_WIDTH), F32), jax.ShapeDtypeStruct((T, HG_WIDTH), BF16),
                   jax.ShapeDtypeStruct((HG_WIDTH, T), BF16),
                   jax.ShapeDtypeStruct((N_HEADS, T // CHUNK, HEAD_DIM, HEAD_DIM), F32)],
        scratch_shapes=[pltpu.VMEM((HEADS_PER_STEP, HEAD_DIM, HEAD_DIM), F32)],
        compiler_params=_params(("parallel", "arbitrary")),
    )(hg, hg, hg, hg, low, gn)


def _conv_fwd(cv, conv_w):
    T = cv.shape[0]
    nj = CONV_WIDTH // 128

    def body(c_ref, b_ref, x_ref, w_ref, o_ref, ot_ref):
        row = lax.broadcasted_iota(jnp.int32, (T, 128), 0)
        u = c_ref[...] * x_ref[...]
        u1 = jnp.where(row >= 1, pltpu.roll(u, 1, 0), 0.0)
        u2 = jnp.where(row >= 2, pltpu.roll(u, 2, 0), 0.0)
        y = w_ref[0:1, :] * u2 + w_ref[1:2, :] * u1 + w_ref[2:3, :] * u
        out = b_ref[...] * y
        o_ref[...] = out.astype(BF16)
        ot_ref[...] = out.T.astype(BF16)

    col = lambda p: pl.BlockSpec((T, 128), lambda j: (0, p * nj + j))
    return pl.pallas_call(
        body, name="conv_fwd", grid=(nj,),
        in_specs=[col(0), col(1), col(2), pl.BlockSpec((3, 128), lambda j: (0, j))],
        out_specs=[pl.BlockSpec((T, 128), lambda j: (0, j)), pl.BlockSpec((128, T), lambda j: (j, 0))],
        out_shape=[jax.ShapeDtypeStruct((T, CONV_WIDTH), BF16), jax.ShapeDtypeStruct((CONV_WIDTH, T), BF16)],
        compiler_params=_params(("parallel",)),
    )(cv, cv, cv, conv_w)


def _merge_fwd(og, cvo, gt, x, wa, wb, wo):
    T = x.shape[0]
    tm = min(512, T)

    def body(og_ref, cvo_ref, gt_ref, x_ref, wa_ref, wb_ref, wo_ref, x1_ref, mgt_ref):
        ya = jnp.dot(og_ref[...], _shard_cols(wa_ref), preferred_element_type=F32)
        yb = jnp.dot(cvo_ref[...], _shard_cols(wb_ref), preferred_element_type=F32)
        m = _sigmoid(gt_ref[:, :D_MODEL]) * ya + _sigmoid(gt_ref[:, D_MODEL:]) * yb
        mgt_ref[...] = m.T.astype(BF16)
        x1_ref[...] = x_ref[...] + jnp.dot(m.astype(BF16), wo_ref[...], preferred_element_type=F32)

    row = lambda n: pl.BlockSpec((tm, n), lambda i: (i, 0))
    return pl.pallas_call(
        body, name="merge_fwd", grid=(T // tm,),
        in_specs=[row(HG_WIDTH), row(CONV_WIDTH), row(2 * D_MODEL), row(D_MODEL),
                  _resident(wa.shape), _resident(wb.shape), _resident(wo.shape)],
        out_specs=[row(D_MODEL), _col(tm, D_MODEL)],
        out_shape=[jax.ShapeDtypeStruct((T, D_MODEL), F32), jax.ShapeDtypeStruct((D_MODEL, T), BF16)],
        compiler_params=_params(("parallel",)),
    )(og, cvo, gt, x, wa, wb, wo)


def _ffn_fwd(x1, g, wg, wu, wd):
    T = x1.shape[0]
    tm = min(256, T)

    def body(x_ref, g_ref, wg_ref, wu_ref, wd_ref, ht_ref, gate_ref, up_ref, actt_ref, x2_ref):
        xv = x_ref[...]
        r = lax.rsqrt(jnp.mean(xv * xv, axis=-1, keepdims=True) + EPS)
        hf = xv * r * g_ref[...]
        h = hf.astype(BF16)
        ht_ref[...] = hf.T.astype(BF16)
        gate = _mm_nt(h, wg_ref[...])
        up = _mm_nt(h, wu_ref[...])
        gate_ref[...] = gate
        up_ref[...] = up
        act = gate * _sigmoid(gate) * up
        actt_ref[...] = act.T.astype(BF16)
        x2_ref[...] = xv + jnp.dot(act.astype(BF16), wd_ref[...], preferred_element_type=F32)

    row = lambda n: pl.BlockSpec((tm, n), lambda i: (i, 0))
    return pl.pallas_call(
        body, name="ffn_fwd", grid=(T // tm,),
        in_specs=[row(D_MODEL), _full((1, D_MODEL)), _resident(wg.shape), _resident(wu.shape), _resident(wd.shape)],
        out_specs=[_col(tm, D_MODEL), row(D_FF), row(D_FF), _col(tm, D_FF), row(D_MODEL)],
        out_shape=[jax.ShapeDtypeStruct((D_MODEL, T), BF16), jax.ShapeDtypeStruct((T, D_FF), F32),
                   jax.ShapeDtypeStruct((T, D_FF), F32), jax.ShapeDtypeStruct((D_FF, T), BF16),
                   jax.ShapeDtypeStruct((T, D_MODEL), F32)],
        compiler_params=_params(("parallel",)),
    )(x1, g, wg, wu, wd)


def _final_fwd_bwd(x2, target, g):
    T = x2.shape[0]
    tm = min(512, T)

    def body(x_ref, t_ref, g_ref, loss_ref, dg_ref, dx_ref):
        @pl.when(pl.program_id(0) == 0)
        def _():
            loss_ref[...] = jnp.zeros_like(loss_ref)
            dg_ref[...] = jnp.zeros_like(dg_ref)

        xv = x_ref[...]
        gv = g_ref[...]
        r = lax.rsqrt(jnp.mean(xv * xv, axis=-1, keepdims=True) + EPS)
        xh = xv * r
        err = xh * gv - t_ref[...]
        loss_ref[...] += 0.5 * jnp.sum(jnp.mean(err * err, axis=-1, keepdims=True), axis=0, keepdims=True)
        dy = err * (1.0 / D_MODEL)
        dg_ref[...] += jnp.sum(dy * xh, axis=0, keepdims=True)
        w = dy * gv
        dx_ref[...] = r * (w - xh * jnp.mean(w * xh, axis=-1, keepdims=True))

    row = pl.BlockSpec((tm, D_MODEL), lambda i: (i, 0))
    return pl.pallas_call(
        body, name="final_fwd_bwd", grid=(T // tm,),
        in_specs=[row, row, _full((1, D_MODEL))],
        out_specs=[_full((1, 128)), _full((1, D_MODEL)), row],
        out_shape=[jax.ShapeDtypeStruct((1, 128), F32), jax.ShapeDtypeStruct((1, D_MODEL), F32),
                   jax.ShapeDtypeStruct((T, D_MODEL), F32)],
        compiler_params=_params(("arbitrary",)),
    )(x2, target, g)


def _ffn_bwd(dx2, x1, gate, up, g, wg, wu, wd):
    T = x1.shape[0]
    tm = min(256, T)

    def body(dx2_ref, x_ref, gate_ref, up_ref, g_ref, wg_ref, wu_ref, wd_ref, dgate_ref, dup_ref, dx1_ref, dgn_ref):
        @pl.when(pl.program_id(0) == 0)
        def _():
            dgn_ref[...] = jnp.zeros_like(dgn_ref)

        dx2 = dx2_ref[...]
        dact = _mm_nt(dx2, wd_ref[...])
        gate = gate_ref[...]
        s = _sigmoid(gate)
        dgate = (dact * up_ref[...] * (s * (1.0 + gate * (1.0 - s)))).astype(BF16)
        dup = (dact * (gate * s)).astype(BF16)
        dgate_ref[...] = dgate
        dup_ref[...] = dup
        dh = _mm(dgate, wg_ref[...]) + _mm(dup, wu_ref[...])
        xv = x_ref[...]
        r = lax.rsqrt(jnp.mean(xv * xv, axis=-1, keepdims=True) + EPS)
        xh = xv * r
        dgn_ref[...] += jnp.sum(dh * xh, axis=0, keepdims=True)
        w = dh * g_ref[...]
        dx1_ref[...] = dx2 + r * (w - xh * jnp.mean(w * xh, axis=-1, keepdims=True))

    row = lambda n: pl.BlockSpec((tm, n), lambda i: (i, 0))
    return pl.pallas_call(
        body, name="ffn_bwd", grid=(T // tm,),
        in_specs=[row(D_MODEL), row(D_MODEL), row(D_FF), row(D_FF), _full((1, D_MODEL)),
                  _resident(wg.shape), _resident(wu.shape), _resident(wd.shape)],
        out_specs=[row(D_FF), row(D_FF), row(D_MODEL), _full((1, D_MODEL))],
        out_shape=[jax.ShapeDtypeStruct((T, D_FF), BF16), jax.ShapeDtypeStruct((T, D_FF), BF16),
                   jax.ShapeDtypeStruct((T, D_MODEL), F32), jax.ShapeDtypeStruct((1, D_MODEL), F32)],
        compiler_params=_params(("arbitrary",)),
    )(dx2, x1, gate, up, g, wg, wu, wd)


def _merge_bwd(dx1, og, cvo, gt, wa, wb, wo):
    T = dx1.shape[0]
    tm = min(512, T)

    def body(dx_ref, og_ref, cvo_ref, gt_ref, wa_ref, wb_ref, wo_ref, dgt_ref, dya_ref, dyb_ref, dog_ref, dcvo_ref):
        dm = _mm_nt(dx_ref[...], wo_ref[...])
        wa = _shard_cols(wa_ref)
        wb = _shard_cols(wb_ref)
        ya = jnp.dot(og_ref[...], wa, preferred_element_type=F32)
        yb = jnp.dot(cvo_ref[...], wb, preferred_element_type=F32)
        sa = _sigmoid(gt_ref[:, :D_MODEL])
        sb = _sigmoid(gt_ref[:, D_MODEL:])
        dgt_ref[:, :D_MODEL] = (dm * ya * (sa * (1.0 - sa))).astype(BF16)
        dgt_ref[:, D_MODEL:] = (dm * yb * (sb * (1.0 - sb))).astype(BF16)
        dya = (dm * sa).astype(BF16)
        dyb = (dm * sb).astype(BF16)
        dya_ref[...] = dya
        dyb_ref[...] = dyb
        dog_ref[...] = _mm_nt(dya, wa)
        dcvo_ref[...] = _mm_nt(dyb, wb)

    row = lambda n: pl.BlockSpec((tm, n), lambda i: (i, 0))
    return pl.pallas_call(
        body, name="merge_bwd", grid=(T // tm,),
        in_specs=[row(D_MODEL), row(HG_WIDTH), row(CONV_WIDTH), row(2 * D_MODEL),
                  _resident(wa.shape), _resident(wb.shape), _resident(wo.shape)],
        out_specs=[row(2 * D_MODEL), row(D_MODEL), row(D_MODEL), row(HG_WIDTH), row(CONV_WIDTH)],
        out_shape=[jax.ShapeDtypeStruct((T, 2 * D_MODEL), BF16), jax.ShapeDtypeStruct((T, D_MODEL), BF16),
                   jax.ShapeDtypeStruct((T, D_MODEL), BF16), jax.ShapeDtypeStruct((T, HG_WIDTH), F32),
                   jax.ShapeDtypeStruct((T, CONV_WIDTH), F32)],
        compiler_params=_params(("parallel",)),
    )(dx1, og, cvo, gt, wa, wb, wo)


def _conv_bwd(dcvo, cv, conv_w):
    T = cv.shape[0]
    nj = CONV_WIDTH // 128

    def body(do_ref, c_ref, b_ref, x_ref, w_ref, dc_ref, db_ref, dx_ref, dw_ref):
        row = lax.broadcasted_iota(jnp.int32, (T, 128), 0)
        c = c_ref[...]
        xb = x_ref[...]
        do = do_ref[...]
        u = c * xb
        u1 = jnp.where(row >= 1, pltpu.roll(u, 1, 0), 0.0)
        u2 = jnp.where(row >= 2, pltpu.roll(u, 2, 0), 0.0)
        w0, w1, w2 = w_ref[0:1, :], w_ref[1:2, :], w_ref[2:3, :]
        y = w0 * u2 + w1 * u1 + w2 * u
        db_ref[...] = (do * y).astype(BF16)
        dy = do * b_ref[...]
        dw_ref[0:1, :] = jnp.sum(dy * u2, axis=0, keepdims=True)
        dw_ref[1:2, :] = jnp.sum(dy * u1, axis=0, keepdims=True)
        dw_ref[2:3, :] = jnp.sum(dy * u, axis=0, keepdims=True)
        dy1 = jnp.where(row < T - 1, pltpu.roll(dy, T - 1, 0), 0.0)
        dy2 = jnp.where(row < T - 2, pltpu.roll(dy, T - 2, 0), 0.0)
        du = w2 * dy + w1 * dy1 + w0 * dy2
        dc_ref[...] = (du * xb).astype(BF16)
        dx_ref[...] = (du * c).astype(BF16)

    col = lambda p: pl.BlockSpec((T, 128), lambda j: (0, p * nj + j))
    one = pl.BlockSpec((T, 128), lambda j: (0, j))
    wspec = pl.BlockSpec((3, 128), lambda j: (0, j))
    out = jax.ShapeDtypeStruct((T, CONV_WIDTH), BF16)
    return pl.pallas_call(
        body, name="conv_bwd", grid=(nj,),
        in_specs=[one, col(0), col(1), col(2), wspec],
        out_specs=[one, one, one, wspec],
        out_shape=[out, out, out, jax.ShapeDtypeStruct((3, CONV_WIDTH), F32)],
        compiler_params=_params(("parallel",)),
    )(dcvo, cv, cv, cv, conv_w)


def _drop_operands(body, first, count):
    def wrapped(*refs):
        return body(*refs[:first], *refs[first + count:])
    return wrapped


def _hg_bwd(dog, hg, o, st, low, gn, after=()):
    T = hg.shape[0]
    tb = min(512, T)
    sb = min(256, tb)
    nb = T // tb
    nc = tb // CHUNK
    wid = HEADS_PER_STEP * HEAD_DIM

    def body(q_ref, f_ref, i_ref, g_ref, low_ref, gn_ref, o_ref, dog_ref, st_ref,
             dq_ref, df_ref, di_ref, dg_ref, dlow_ref, dgn_ref,
             ds_scr, dqi_scr, dko_scr, dv_scr, dd_scr, dqh_scr, dkh_scr):
        h = pl.program_id(0)
        t = pl.program_id(1)

        @pl.when(t == 0)
        def _():
            ds_scr[...] = jnp.zeros_like(ds_scr)
            dlow_ref[...] = jnp.zeros_like(dlow_ref)

        @pl.when((t == 0) & (h == 0))
        def _():
            dgn_ref[...] = jnp.zeros_like(dgn_ref)

        pos = _chunk_pos((tb, HEAD_DIM))
        mask = _intra_mask(sb)
        gnv = gn_ref[...]
        lanes = [slice(hh * HEAD_DIM, (hh + 1) * HEAD_DIM) for hh in range(HEADS_PER_STEP)]
        heads = []
        for hh, ln in enumerate(lanes):
            lb, lb1 = _lower_bound(low_ref.at[:, ln])
            qr = q_ref[:, ln]
            sq, q, sg, f, k, e_qa, e_ka, e_b, e_ko, dec = _hg_gates(qr, f_ref[:, ln], lb, pos, tb)

            gr = g_ref[:, ln]
            o = o_ref[:, ln]
            dog_v = dog_ref[:, ln]
            sgr = _sigmoid(gr)
            r = lax.rsqrt(jnp.mean(o * o, axis=-1, keepdims=True) + EPS)
            oh = o * r
            dg_ref[:, ln] = (dog_v * (oh * gnv) * (sgr * (1.0 + gr * (1.0 - sgr)))).astype(BF16)
            don = dog_v * (gr * sgr)
            dgn_ref[...] += jnp.sum(don * oh, axis=0, keepdims=True)
            w = don * gnv
            do = (r * (w - oh * jnp.mean(w * oh, axis=-1, keepdims=True))).astype(BF16)

            qh = (q * e_qa).astype(BF16)
            kh = (k * e_ka).astype(BF16)
            qi = (q * e_b).astype(BF16)
            ko = (k * e_ko).astype(BF16)
            vb = i_ref[:, ln].astype(BF16)

            for s in range(tb // sb):
                sl = slice(s * sb, (s + 1) * sb)
                p = jnp.where(mask, _mm_nt(qh[sl], kh[sl]), 0.0).astype(BF16)
                dp = jnp.where(mask, _mm_nt(do[sl], vb[sl]), 0.0).astype(BF16)
                dv_scr[sl, ln] = _mm_tn(p, do[sl])
                dqh_scr[sl, ln] = _mm(dp, kh[sl])
                dkh_scr[sl, ln] = _mm_tn(dp, qh[sl])
            heads.append(dict(lb=lb, lb1=lb1, qr=qr, sq=sq, q=q, sg=sg, f=f, k=k, e_qa=e_qa, e_ka=e_ka, e_b=e_b,
                              e_ko=e_ko, dec=dec, do=do, qi=qi, ko=ko, vb=vb, ds=ds_scr[hh]))

        for c in reversed(range(nc)):
            sl = slice(c * CHUNK, (c + 1) * CHUNK)
            for hh, ln in enumerate(lanes):
                hd = heads[hh]
                ds = hd["ds"]
                st_c = st_ref[hh, c]
                dqi_scr[sl, ln] = _mm(hd["do"][sl], st_c)
                dko_scr[sl, ln] = _mm(hd["vb"][sl], ds)
                dv_scr[sl, ln] = dv_scr[sl, ln] + _mm_nt(hd["ko"][sl], ds)
                dd_scr[sl, ln] = jnp.broadcast_to(jnp.sum(ds * st_c, axis=0, keepdims=True), (CHUNK, HEAD_DIM))
                hd["ds"] = hd["dec"][c * CHUNK:c * CHUNK + 1, :] * ds + _mm_tn(hd["do"][sl], hd["qi"][sl])

        for hh, ln in enumerate(lanes):
            hd = heads[hh]
            ds_scr[hh] = hd["ds"]
            q, k, lb = hd["q"], hd["k"], hd["lb"]
            dko_e = dko_scr[:, ln] * hd["e_ko"]
            dq = dqh_scr[:, ln] * hd["e_qa"] + dqi_scr[:, ln] * hd["e_b"]
            dk = dkh_scr[:, ln] * hd["e_ka"] + dko_e
            kd3 = (k * dko_e).reshape(nc, CHUNK, HEAD_DIM)
            last = jnp.broadcast_to(jnp.sum(kd3, axis=1, keepdims=True), kd3.shape).reshape(tb, HEAD_DIM)
            db = q * dq - k * dk + jnp.where(pos == CHUNK - 1, hd["dec"] * dd_scr[:, ln] + last, 0.0)
            dlg = _chunk_rev_cumsum(db, pos)
            dfv = dlg / hd["f"] - dk
            s_low = jnp.sum(dfv * (1.0 - hd["sg"]), axis=0, keepdims=True)
            dlow_ref[0:1, ln] += s_low * lb * (1.0 - lb)
            dlow_ref[1:2, ln] += -s_low * lb * hd["lb1"]
            df_ref[:, ln] = (dfv * (1.0 - lb) * hd["sg"] * (1.0 - hd["sg"])).astype(BF16)
            dq_ref[:, ln] = (dq * Q_SCALE * (hd["sq"] * (1.0 + hd["qr"] * (1.0 - hd["sq"])))).astype(BF16)
            di_ref[:, ln] = dv_scr[:, ln].astype(BF16)

    rt = lambda t: nb - 1 - t
    col = lambda p: pl.BlockSpec((tb, wid), lambda h, t: (rt(t), p * HEAD_GROUPS + h))
    hcol = pl.BlockSpec((tb, wid), lambda h, t: (rt(t), h))
    piece = jax.ShapeDtypeStruct((T, HG_WIDTH), BF16)
    tile = pltpu.VMEM((tb, wid), F32)
    return pl.pallas_call(
        _drop_operands(body, 9, len(after)), name="hg_bwd", grid=(HEAD_GROUPS, nb),
        in_specs=[col(0), col(1), col(2), col(3), pl.BlockSpec((2, wid), lambda h, t: (0, h)),
                  pl.BlockSpec((1, HEAD_DIM), lambda h, t: (0, 0)), hcol, hcol,
                  pl.BlockSpec((HEADS_PER_STEP, nc, HEAD_DIM, HEAD_DIM), lambda h, t: (h, rt(t), 0, 0))]
                 + [HBM_SPEC] * len(after),
        out_specs=[hcol, hcol, hcol, hcol, pl.BlockSpec((2, wid), lambda h, t: (0, h)),
                   pl.BlockSpec((1, HEAD_DIM), lambda h, t: (0, 0))],
        out_shape=[piece, piece, piece, piece, jax.ShapeDtypeStruct((2, HG_WIDTH), F32),
                   jax.ShapeDtypeStruct((1, HEAD_DIM), F32)],
        scratch_shapes=[pltpu.VMEM((HEADS_PER_STEP, HEAD_DIM, HEAD_DIM), F32), tile, tile, tile, tile, tile, tile],
        compiler_params=_params(("arbitrary", "arbitrary")),
    )(hg, hg, hg, hg, low, gn, o, dog, st, *after)


def _in_bwd(dparts, w_in, x, dx1, g, after=()):
    T = x.shape[0]
    tm = min(512, T)
    widths = [p.shape[1] for p in dparts]
    offs = [sum(widths[:i]) for i in range(len(widths))]
    n = len(dparts)

    def body(*refs):
        d_refs = refs[:n]
        w_ref, x_ref, dx1_ref, g_ref, dx_ref, dgn_ref = refs[n:]

        @pl.when(pl.program_id(0) == 0)
        def _():
            dgn_ref[...] = jnp.zeros_like(dgn_ref)

        dh = None
        for d_ref, off, wd in zip(d_refs, offs, widths):
            part = _mm(d_ref[...], w_ref[off:off + wd, :])
            dh = part if dh is None else dh + part
        xv = x_ref[...]
        r = lax.rsqrt(jnp.mean(xv * xv, axis=-1, keepdims=True) + EPS)
        xh = xv * r
        dgn_ref[...] += jnp.sum(dh * xh, axis=0, keepdims=True)
        w = dh * g_ref[...]
        dx_ref[...] = dx1_ref[...] + r * (w - xh * jnp.mean(w * xh, axis=-1, keepdims=True))

    row = lambda m: pl.BlockSpec((tm, m), lambda i: (i, 0))
    return pl.pallas_call(
        _drop_operands(body, n + 4, len(after)), name="in_bwd", grid=(T // tm,),
        in_specs=[row(wd) for wd in widths] + [_resident(w_in.shape), row(D_MODEL), row(D_MODEL), _full((1, D_MODEL))]
                 + [HBM_SPEC] * len(after),
        out_specs=[row(D_MODEL), _full((1, D_MODEL))],
        out_shape=[jax.ShapeDtypeStruct((T, D_MODEL), F32), jax.ShapeDtypeStruct((1, D_MODEL), F32)],
        compiler_params=_params(("arbitrary",)),
    )(*dparts, w_in, x, dx1, g, *after)


def _wgrad(name, at, b, tn, transposed=False):
    M, T = at.shape
    N = b.shape[1]
    tk = min(2048, T)
    nk = T // tk

    def body(a_ref, b_ref, o_ref, acc):
        k = pl.program_id(1)
        part = _mm(a_ref[...], b_ref[...])

        @pl.when(k == 0)
        def _():
            acc[...] = part

        @pl.when(k != 0)
        def _():
            acc[...] += part

        @pl.when(k == nk - 1)
        def _():
            o_ref[...] = (acc[...].T if transposed else acc[...]).astype(BF16)

    if transposed:
        out_spec, out_shape = pl.BlockSpec((tn, M), lambda j, k: (j, 0)), (N, M)
    else:
        out_spec, out_shape = pl.BlockSpec((M, tn), lambda j, k: (0, j)), (M, N)
    return pl.pallas_call(
        body, name=name, grid=(N // tn, nk),
        in_specs=[pl.BlockSpec((M, tk), lambda j, k: (0, k)), pl.BlockSpec((tk, tn), lambda j, k: (k, j))],
        out_specs=out_spec, out_shape=jax.ShapeDtypeStruct(out_shape, BF16),
        scratch_shapes=[pltpu.VMEM((M, tn), F32)],
        compiler_params=_params(("parallel", "arbitrary")),
    )(at, b)


def _wgrad_in(ht, dparts, after=()):
    M, T = ht.shape
    tn = 512
    tk = min(2048, T)
    nk = T // tk
    nblk = [p.shape[1] // tn for p in dparts]
    start = [sum(nblk[:i]) for i in range(len(nblk))]
    n = len(dparts)

    def body(a_ref, *refs):
        d_refs, o_ref, acc = refs[:n], refs[n], refs[n + 1]
        j = pl.program_id(0)
        k = pl.program_id(1)

        @pl.when(k == 0)
        def _():
            acc[...] = jnp.zeros_like(acc)

        for d_ref, s, nb in zip(d_refs, start, nblk):
            @pl.when((j >= s) & (j < s + nb))
            def _():
                acc[...] += _mm(a_ref[...], d_ref[...])

        @pl.when(k == nk - 1)
        def _():
            o_ref[...] = acc[...].T.astype(BF16)

    def piece_spec(s, nb):
        def index(j, k):
            inside = (j >= s) & (j < s + nb)
            return jnp.where(inside, k, 0), jnp.clip(j - s, 0, nb - 1)
        return pl.BlockSpec((tk, tn), index)

    return pl.pallas_call(
        _drop_operands(body, 1 + n, len(after)), name="wgrad_in", grid=(sum(nblk), nk),
        in_specs=[pl.BlockSpec((M, tk), lambda j, k: (0, k))] + [piece_spec(s, nb) for s, nb in zip(start, nblk)]
                 + [HBM_SPEC] * len(after),
        out_specs=pl.BlockSpec((tn, M), lambda j, k: (j, 0)),
        out_shape=jax.ShapeDtypeStruct((sum(nblk) * tn, M), BF16),
        scratch_shapes=[pltpu.VMEM((M, tn), F32)],
        compiler_params=_params(("parallel", "arbitrary")),
    )(ht, *dparts, *after)


def _adamw_math(w, g, m, v):
    m = ADAM_B1 * m + (1.0 - ADAM_B1) * g
    v = ADAM_B2 * v + (1.0 - ADAM_B2) * (g * g)
    m_hat = m / (1.0 - ADAM_B1 ** ADAM_STEP)
    v_hat = v / (1.0 - ADAM_B2 ** ADAM_STEP)
    delta = -ADAM_LR * (m_hat / (jnp.sqrt(v_hat) + ADAM_EPS) + ADAM_WD * w)
    return delta, m, v


def _adamw_sum(name, w, parts, m, v):
    R, C = w.shape
    tr = _row_tile(R)

    def body(w_ref, p_ref, m_ref, v_ref, g_out, d_out, m_out, v_out):
        g = p_ref[0].astype(F32)
        for k in range(1, 4):
            g = g + p_ref[k].astype(F32)
        g_out[...] = g
        d_out[...], m_out[...], v_out[...] = _adamw_math(w_ref[...], g, m_ref[...], v_ref[...])

    blk = pl.BlockSpec((tr, C), lambda i: (i, 0))
    out = jax.ShapeDtypeStruct((R, C), F32)
    return pl.pallas_call(
        body, name=name, grid=(R // tr,),
        in_specs=[blk, pl.BlockSpec((4, tr, C), lambda i: (0, i, 0)), blk, blk],
        out_specs=[blk, blk, blk, blk], out_shape=[out, out, out, out],
        compiler_params=_params(("parallel",)),
    )(w, parts, m, v)


def _small_sum(gathered):
    R = gathered.shape[1]

    def body(p_ref, o_ref):
        g = p_ref[0]
        for k in range(1, N_DEV):
            g = g + p_ref[k]
        o_ref[...] = g

    return pl.pallas_call(
        body, name="small_sum", in_specs=[_full(gathered.shape)], out_specs=_full((R, 128)), grid=(1,),
        out_shape=jax.ShapeDtypeStruct((R, 128), F32),
    )(gathered)


def _small_adamw(w, g, m, v):
    def body(w_ref, g_ref, m_ref, v_ref, d_out, m_out, v_out):
        d_out[...], m_out[...], v_out[...] = _adamw_math(w_ref[...], g_ref[...], m_ref[...], v_ref[...])

    out = jax.ShapeDtypeStruct(w.shape, F32)
    spec = _full(w.shape)
    return pl.pallas_call(
        body, name="small_adamw", grid=(1,), in_specs=[spec] * 4, out_specs=[spec] * 3, out_shape=[out, out, out],
    )(w, g, m, v)


def _row_tile(rows):
    for cand in (256, 128):
        if rows % cand == 0:
            return cand
    return rows


def _pair_sum(name, by_owner, got, core):
    _, R, C = got.shape
    tr = _row_tile(R)

    def body(core_ref, a_ref, b_ref, o_ref):
        o_ref[...] = (a_ref[...].astype(F32) + b_ref[...].astype(F32)).astype(BF16)

    blk = pl.BlockSpec((None, tr, C), lambda k, i, core_ref: (k, i, 0))
    mine = pl.BlockSpec((None, tr, C), lambda k, i, core_ref: (2 * k + core_ref[0], i, 0))
    return pl.pallas_call(
        body, name=name,
        grid_spec=pltpu.PrefetchScalarGridSpec(num_scalar_prefetch=1, grid=(4, R // tr), in_specs=[mine, blk],
                                               out_specs=blk),
        out_shape=jax.ShapeDtypeStruct(got.shape, BF16),
        compiler_params=_params(("parallel", "parallel")),
    )(core, by_owner, got)


def _shards_from_cols(name, full):
    R, allc = full.shape
    c = allc // N_DEV
    tr = _row_tile(R)

    def body(f_ref, o_ref):
        for s in range(N_DEV):
            o_ref[s] = f_ref[:, s * c:(s + 1) * c]

    return pl.pallas_call(
        body, name=name, grid=(R // tr,),
        in_specs=[pl.BlockSpec((tr, allc), lambda i: (i, 0))],
        out_specs=pl.BlockSpec((N_DEV, tr, c), lambda i: (0, i, 0)),
        out_shape=jax.ShapeDtypeStruct((N_DEV, R, c), full.dtype),
        compiler_params=_params(("parallel",)),
    )(full)


MESH = pl.DeviceIdType.MESH
HBM_SPEC = pl.BlockSpec(memory_space=pl.ANY)


def _handshake(peers):
    barrier = pltpu.get_barrier_semaphore()
    for peer in peers:
        pl.semaphore_signal(barrier, inc=1, device_id=peer, device_id_type=MESH)
    pl.semaphore_wait(barrier, len(peers))


def _comm_call(body, name, operands, out_shape, scratch, collective_id):
    if collective_id is None:
        return pl.pallas_call(body, name=name, in_specs=[HBM_SPEC] * len(operands), out_specs=[HBM_SPEC] * len(out_shape),
                              out_shape=out_shape, scratch_shapes=scratch)(*operands)
    return pl.kernel(body, out_type=out_shape, mesh=plsc.ScalarSubcoreMesh(axis_name="sequencer", num_cores=1),
                     scratch_types=scratch, name=name,
                     compiler_params=pltpu.CompilerParams(collective_id=collective_id))(*operands)


def _all_gather(name, blocks, collective_id=None, after=()):
    n = len(blocks)
    na = len(after)

    def body(*refs):
        x_refs, out_refs = refs[:n], refs[n + na:2 * n + na]
        send_sems, recv_sems, local_sems = refs[2 * n + na:]
        x, y, c = lax.axis_index("x"), lax.axis_index("y"), lax.axis_index("c")
        me, sibling = (x, y, c), (x, y, 1 - c)
        chips = [(1 - x, y), (x, 1 - y), (1 - x, 1 - y)]
        if collective_id is not None:
            _handshake([sibling] + [(*chip, c) for chip in chips])

        def slot(i, px, py, pc):
            return out_refs[i].at[4 * px + 2 * py + pc]

        def copy(i, k, blk, to, src=None):
            return pltpu.make_async_remote_copy(
                src_ref=slot(i, *blk) if src is None else src, dst_ref=slot(i, *blk),
                send_sem=send_sems.at[7 * i + k], recv_sem=recv_sems.at[7 * i + k], device_id=to, device_id_type=MESH)

        mine = [pltpu.make_async_copy(x_refs[i], slot(i, *me), local_sems.at[i]) for i in range(n)]
        for cp in mine:
            cp.start()
        first = []
        for i in range(n):
            first.append(copy(i, 0, me, sibling, src=x_refs[i]))
            first += [copy(i, 1 + j, me, (*chip, c), src=x_refs[i]) for j, chip in enumerate(chips)]
        for cp in first:
            cp.start()
        passed = []
        for i in range(n):
            for j, chip in enumerate(chips):
                copy(i, 1 + j, (*chip, c), me).wait_recv()
                passed.append(copy(i, 4 + j, (*chip, c), sibling))
                passed[-1].start()
        for i in range(n):
            copy(i, 0, sibling, me).wait_recv()
            for j, chip in enumerate(chips):
                copy(i, 4 + j, (*chip, 1 - c), me).wait_recv()
        for cp in first + passed:
            cp.wait_send()
        for cp in mine:
            cp.wait()

    return _comm_call(
        body, name, list(blocks) + list(after), [jax.ShapeDtypeStruct((N_DEV,) + b.shape, b.dtype) for b in blocks],
        [pltpu.SemaphoreType.DMA((7 * n,)), pltpu.SemaphoreType.DMA((7 * n,)), pltpu.SemaphoreType.DMA((n,))],
        collective_id)


def _sibling_swap(name, by_owner, collective_id=None, after=()):
    n = len(by_owner)
    na = len(after)

    def body(*refs):
        x_refs, out_refs = refs[:n], refs[n + na:2 * n + na]
        send_sems, recv_sems = refs[2 * n + na:]
        x, y, c = lax.axis_index("x"), lax.axis_index("y"), lax.axis_index("c")
        if collective_id is not None:
            _handshake([(x, y, 1 - c)])
        copies = []
        for i in range(n):
            for k in range(4):
                copies.append(pltpu.make_async_remote_copy(
                    src_ref=x_refs[i].at[2 * k + 1 - c], dst_ref=out_refs[i].at[k],
                    send_sem=send_sems.at[4 * i + k], recv_sem=recv_sems.at[4 * i + k],
                    device_id=(x, y, 1 - c), device_id_type=MESH))
        for cp in copies:
            cp.start()
        for cp in copies:
            cp.wait()

    return _comm_call(
        body, name, list(by_owner) + list(after),
        [jax.ShapeDtypeStruct((4,) + b.shape[1:], b.dtype) for b in by_owner],
        [pltpu.SemaphoreType.DMA((4 * n,)), pltpu.SemaphoreType.DMA((4 * n,))], collective_id)


def _chip_exchange(name, sums, collective_id=None, after=()):
    n = len(sums)
    na = len(after)

    def body(*refs):
        x_refs, out_refs = refs[:n], refs[n + na:2 * n + na]
        send_sems, recv_sems, local_sems = refs[2 * n + na:]
        x, y, c = lax.axis_index("x"), lax.axis_index("y"), lax.axis_index("c")
        chips = [(1 - x, y), (x, 1 - y), (1 - x, 1 - y)]
        my_chip = 2 * x + y
        if collective_id is not None:
            _handshake([(cx, cy, c) for cx, cy in chips])
        mine = [pltpu.make_async_copy(x_refs[i].at[my_chip], out_refs[i].at[my_chip], local_sems.at[i])
                for i in range(n)]
        for cp in mine:
            cp.start()
        sends = []
        for i in range(n):
            for j, (cx, cy) in enumerate(chips):
                sends.append(pltpu.make_async_remote_copy(
                    src_ref=x_refs[i].at[2 * cx + cy], dst_ref=out_refs[i].at[my_chip],
                    send_sem=send_sems.at[3 * i + j], recv_sem=recv_sems.at[3 * i + j],
                    device_id=(cx, cy, c), device_id_type=MESH))
        for cp in sends:
            cp.start()
        for i in range(n):
            for j, (cx, cy) in enumerate(chips):
                pltpu.make_async_remote_copy(
                    src_ref=x_refs[i].at[my_chip], dst_ref=out_refs[i].at[2 * cx + cy],
                    send_sem=send_sems.at[3 * i + j], recv_sem=recv_sems.at[3 * i + j],
                    device_id=(cx, cy, c), device_id_type=MESH).wait_recv()
        for cp in sends:
            cp.wait_send()
        for cp in mine:
            cp.wait()

    return _comm_call(
        body, name, list(sums) + list(after), [jax.ShapeDtypeStruct(s.shape, s.dtype) for s in sums],
        [pltpu.SemaphoreType.DMA((3 * n,)), pltpu.SemaphoreType.DMA((3 * n,)), pltpu.SemaphoreType.DMA((n,))],
        collective_id)


def _cast_shards(shards):
    n = len(shards)

    def body(*refs):
        for i in range(n):
            refs[n + i][...] = refs[i][...].astype(BF16)

    vmem = pl.BlockSpec(memory_space=pltpu.VMEM)
    return pl.pallas_call(
        body, name="cast_shards", in_specs=[vmem] * n, out_specs=[vmem] * n,
        out_shape=[jax.ShapeDtypeStruct(s.shape, BF16) for s in shards],
        compiler_params=pltpu.CompilerParams(vmem_limit_bytes=VMEM_LIMIT_V7X),
    )(*shards)


BIG = ("w_in", "w_branch_a", "w_branch_b", "w_out", "w_ffn_gate", "w_ffn_up", "w_ffn_down")


def _local_step(x, target, gains, low, conv_w, wg8, reduce):
    g_mix, g_hg, g_ffn, g_fin = gains
    w_in = wg8["w_in"].reshape(N_IN, D_MODEL)
    wg = wg8["w_ffn_gate"].reshape(D_FF, D_MODEL)
    wu = wg8["w_ffn_up"].reshape(D_FF, D_MODEL)
    wa, wb = wg8["w_branch_a"], wg8["w_branch_b"]
    wo = wg8["w_out"].reshape(D_MODEL, D_MODEL)
    wd = wg8["w_ffn_down"].reshape(D_FF, D_MODEL)

    ht, hg, cv, gt = _fwd_in(x, g_mix, w_in)
    o, og, ogt, st = _hg_fwd(hg, low, g_hg)
    cvo, cvot = _conv_fwd(cv, conv_w)
    x1, mgt = _merge_fwd(og, cvo, gt, x, wa, wb, wo)
    h2t, gate, up, actt, x2 = _ffn_fwd(x1, g_ffn, wg, wu, wd)
    loss, d_gfin, dx2 = _final_fwd_bwd(x2, target, g_fin)

    dgate, dup, dx1, d_gffn = _ffn_bwd(dx2, x1, gate, up, g_ffn, wg, wu, wd)
    ffn = dict(
        w_ffn_down=_wgrad("wgrad_ffn_down", actt, dx2, 512).reshape(N_DEV, D_FF // N_DEV, D_MODEL),
        w_ffn_gate=_wgrad("wgrad_ffn_gate", h2t, dgate, 1408, transposed=True).reshape(N_DEV, D_FF // N_DEV, D_MODEL),
        w_ffn_up=_wgrad("wgrad_ffn_up", h2t, dup, 1408, transposed=True).reshape(N_DEV, D_FF // N_DEV, D_MODEL))
    sums_ffn, got_ffn = reduce.begin(ffn)
    dgt, dya, dyb, dog, dcvo = _merge_bwd(dx1, og, cvo, gt, wa, wb, wo)
    out = dict(
        w_out=_wgrad("wgrad_out", mgt, dx1, 512).reshape(N_DEV, D_MODEL // N_DEV, D_MODEL),
        w_branch_a=_shards_from_cols("split_w_branch_a", _wgrad("wgrad_branch_a", ogt, dya, 512)),
        w_branch_b=_shards_from_cols("split_w_branch_b", _wgrad("wgrad_branch_b", cvot, dyb, 512)))
    sums_out, got_out = reduce.begin(out, after=got_ffn[:1])
    parts_ffn = reduce.finish(ffn, sums_ffn, after=got_out[:1])
    dc, db, dxb, d_conv = _conv_bwd(dcvo, cv, conv_w)
    dq, df, di, dg, d_low, d_ghg = _hg_bwd(dog, hg, o, st, low, g_hg, after=list(sums_ffn) + list(sums_out))
    parts_out = reduce.finish(out, sums_out, after=[parts_ffn[0], dq])
    dparts = [dq, df, di, dg, dc, db, dxb, dgt]
    w_in_grad = dict(w_in=_wgrad_in(ht, dparts, after=parts_ffn[:1]).reshape(N_DEV, N_IN // N_DEV, D_MODEL))
    sums_in, _ = reduce.begin(w_in_grad, after=parts_out[:1])
    parts_in = reduce.finish(w_in_grad, sums_in)
    grad_x, d_gmix = _in_bwd(dparts, w_in, x, dx1, g_mix, after=list(parts_out[:1]) + list(sums_in))
    small = dict(norm_mix_g=d_gmix, norm_ffn_g=d_gffn, norm_final_g=d_gfin, lower_bounds=d_low, hg_norm_g=d_ghg,
                 conv_w=d_conv, loss=loss)
    return grad_x, small, parts_in


_SMALL_LAYOUT = (("norm_mix_g", 0, 8), ("norm_ffn_g", 8, 8), ("norm_final_g", 16, 8), ("lower_bounds", 24, 8),
                 ("hg_norm_g", 32, 1))
_LOSS_ROW = 40
_CONV_ROW = 48


def _pad_rows(a, rows):
    return jnp.pad(a, ((0, rows - a.shape[0]), (0, 0)))


def _pack_small(vals, conv_rows):
    parts = [_pad_rows(vals[name].reshape(rows, 128), 8) for name, _, rows in _SMALL_LAYOUT]
    loss = vals["loss"][:, :128] if "loss" in vals else jnp.zeros((1, 128), F32)
    parts.append(_pad_rows(loss, 8))
    parts.append(_pad_rows(conv_rows, SMALL_ROWS - _CONV_ROW))
    return jnp.concatenate(parts, axis=0)


def _conv_shard_rows(a):
    return jnp.pad(a, ((0, 5), (0, 64)))


def kernel(x, norm_mix_g, w_in, lower_bounds, hg_norm_g, conv_w, w_branch_a, w_branch_b, w_out, norm_ffn_g, w_ffn_gate, w_ffn_up, w_ffn_down, norm_final_g, loss_target, m_norm_mix_g, m_w_in, m_lower_bounds, m_hg_norm_g, m_conv_w, m_w_branch_a, m_w_branch_b, m_w_out, m_norm_ffn_g, m_w_ffn_gate, m_w_ffn_up, m_w_ffn_down, m_norm_final_g, v_norm_mix_g, v_w_in, v_lower_bounds, v_hg_norm_g, v_conv_w, v_w_branch_a, v_w_branch_b, v_w_out, v_norm_ffn_g, v_w_ffn_gate, v_w_ffn_up, v_w_ffn_down, v_norm_final_g):
    cx, cy, cc = lax.axis_index("x"), lax.axis_index("y"), lax.axis_index("c")
    my_dev = 4 * cx + 2 * cy + cc

    def tr(a):
        return a[0].T

    big = dict(w_in=tr(w_in), w_branch_a=w_branch_a[0], w_branch_b=w_branch_b[0], w_out=w_out[0],
               w_ffn_gate=tr(w_ffn_gate), w_ffn_up=tr(w_ffn_up), w_ffn_down=w_ffn_down[0])
    big_m = dict(w_in=tr(m_w_in), w_branch_a=m_w_branch_a[0], w_branch_b=m_w_branch_b[0], w_out=m_w_out[0],
                 w_ffn_gate=tr(m_w_ffn_gate), w_ffn_up=tr(m_w_ffn_up), w_ffn_down=m_w_ffn_down[0])
    big_v = dict(w_in=tr(v_w_in), w_branch_a=v_w_branch_a[0], w_branch_b=v_w_branch_b[0], w_out=v_w_out[0],
                 w_ffn_gate=tr(v_w_ffn_gate), w_ffn_up=tr(v_w_ffn_up), w_ffn_down=v_w_ffn_down[0])
    transposed = ("w_in", "w_ffn_gate", "w_ffn_up")

    shards = dict(zip(BIG, _cast_shards([big[n] for n in BIG])))
    first = _all_gather("gather_w_in", [shards["w_in"], _conv_shard_rows(conv_w[0])])
    later = _all_gather("gather_rest", [shards[n] for n in BIG[1:]], collective_id=1, after=first[1:])
    wg8 = dict(zip(BIG, [first[0]] + list(later)))
    conv_full = first[1][:, :3, :64].transpose(1, 0, 2).reshape(3, CONV_WIDTH)

    core = cc.reshape(1).astype(jnp.int32)
    outs = {}
    ids = iter(range(2, 16))

    class Reduce:
        @staticmethod
        def begin(grads, after=()):
            names = list(grads)
            by_owner = [grads[n] for n in names]
            got = _sibling_swap("sibling_swap_" + names[0], by_owner, collective_id=next(ids), after=after)
            return [_pair_sum("pair_sum_" + n, a, b, core) for n, a, b in zip(names, by_owner, got)], got

        @staticmethod
        def finish(grads, chip_sums, after=()):
            names = list(grads)
            parts = _chip_exchange("chip_exchange_" + names[0], chip_sums, collective_id=next(ids), after=after)
            for n, p in zip(names, parts):
                outs[n] = _adamw_sum("adamw_" + n, big[n], p, big_m[n], big_v[n])
            return parts

    gains = (norm_mix_g, hg_norm_g, norm_ffn_g, norm_final_g.reshape(1, D_MODEL))
    grad_x, small, last = _local_step(x[0], loss_target[0], gains, lower_bounds, conv_full, wg8, Reduce)

    small_all = _all_gather("gather_small", [_pack_small(small, small["conv_w"].reshape(12, 128))],
                            collective_id=next(ids), after=last[:1])
    ssum = _small_sum(small_all[0])
    conv_g_full = ssum[_CONV_ROW:_CONV_ROW + 12].reshape(3, CONV_WIDTH)
    conv_g = lax.dynamic_slice_in_dim(conv_g_full, my_dev * 64, 64, axis=1)
    loss = ssum[_LOSS_ROW, 0]
    g_rows = jnp.concatenate([ssum[:_CONV_ROW], _pad_rows(_conv_shard_rows(conv_g), SMALL_ROWS - _CONV_ROW)], axis=0)

    def pack_state(a):
        vals = dict(norm_mix_g=a[0], norm_ffn_g=a[1], norm_final_g=a[2], lower_bounds=a[3], hg_norm_g=a[4])
        return _pack_small(vals, _conv_shard_rows(a[5][0]))

    sw = pack_state((norm_mix_g, norm_ffn_g, norm_final_g, lower_bounds, hg_norm_g, conv_w))
    sm = pack_state((m_norm_mix_g, m_norm_ffn_g, m_norm_final_g, m_lower_bounds, m_hg_norm_g, m_conv_w))
    sv = pack_state((v_norm_mix_g, v_norm_ffn_g, v_norm_final_g, v_lower_bounds, v_hg_norm_g, v_conv_w))
    s_delta, s_m, s_v = _small_adamw(sw, g_rows, sm, sv)

    shapes = dict(norm_mix_g=(1, D_MODEL), norm_ffn_g=(1, D_MODEL), norm_final_g=(D_MODEL,),
                  lower_bounds=(2, HG_WIDTH), hg_norm_g=(1, HEAD_DIM))

    def unpack(buf, name):
        if name == "conv_w":
            return buf[_CONV_ROW:_CONV_ROW + 3, :64].reshape(1, 3, 64)
        for nm, off, rows in _SMALL_LAYOUT:
            if nm == name:
                return buf[off:off + rows].reshape(shapes[name])
        raise KeyError(name)

    order = ["norm_mix_g", "w_in", "lower_bounds", "hg_norm_g", "conv_w", "w_branch_a", "w_branch_b", "w_out",
             "norm_ffn_g", "w_ffn_gate", "w_ffn_up", "w_ffn_down", "norm_final_g"]
    result = [loss, grad_x[None]]
    for k, sbuf in enumerate((g_rows, s_delta, s_m, s_v)):
        for n in order:
            if n in outs:
                result.append((outs[n][k].T if n in transposed else outs[n][k])[None])
            else:
                result.append(unpack(sbuf, n))
    return tuple(result)
```

```python
import jax
import jax.numpy as jnp
from jax import lax
from jax.experimental import pallas as pl
from jax.experimental.pallas import tpu as pltpu
from jax.experimental.pallas import tpu_sc as plsc

F32 = jnp.float32
BF16 = jnp.bfloat16

D_MODEL = 1024
HG_WIDTH = 512
HEAD_DIM = 128
N_HEADS = 4
HEADS_PER_STEP = 4
HEAD_GROUPS = N_HEADS // HEADS_PER_STEP
CONV_WIDTH = 512
D_FF = 2816
CHUNK = 32
EPS = 1e-6
Q_SCALE = HEAD_DIM ** -0.5
N_DEV = 8

ADAM_LR = 0.001
ADAM_B1 = 0.9
ADAM_B2 = 0.999
ADAM_EPS = 1e-08
ADAM_WD = 0.01
ADAM_STEP = 10

VMEM_LIMIT_V7X = 56 * 1024 * 1024

SMALL_ROWS = 64


def _params(sem, vmem=VMEM_LIMIT_V7X):
    return pltpu.CompilerParams(dimension_semantics=sem, vmem_limit_bytes=vmem)


def _mm(a, b):
    return jnp.dot(a.astype(BF16), b.astype(BF16), preferred_element_type=F32)


def _mm_nt(a, b):
    return lax.dot_general(a.astype(BF16), b.astype(BF16), (((1,), (1,)), ((), ())), preferred_element_type=F32)


def _mm_tn(a, b):
    return lax.dot_general(a.astype(BF16), b.astype(BF16), (((0,), (0,)), ((), ())), preferred_element_type=F32)


def _sigmoid(x):
    return 1.0 / (1.0 + jnp.exp(-x))


def _resident(shape):
    nd = len(shape)
    return pl.BlockSpec(shape, lambda *_: (0,) * nd, pipeline_mode=pl.Buffered(1))


def _full(shape):
    nd = len(shape)
    return pl.BlockSpec(shape, lambda *_: (0,) * nd)


def _shard_cols(w_ref):
    return jnp.concatenate([w_ref[s] for s in range(N_DEV)], axis=1)


N_HG = 4 * HG_WIDTH
N_CV = 3 * CONV_WIDTH
N_GT = 2 * D_MODEL
N_IN = N_HG + N_CV + N_GT


def _col(tm, n):
    return pl.BlockSpec((n, tm), lambda i: (0, i))


def _fwd_in(x, g, w_in_t):
    T = x.shape[0]
    tm = min(512, T)

    def body(x_ref, g_ref, w_ref, ht_ref, hg_ref, cv_ref, gt_ref):
        xv = x_ref[...]
        r = lax.rsqrt(jnp.mean(xv * xv, axis=-1, keepdims=True) + EPS)
        hf = xv * r * g_ref[...]
        h = hf.astype(BF16)
        ht_ref[...] = hf.T.astype(BF16)
        hg_ref[...] = _mm_nt(h, w_ref[:N_HG, :])
        cv_ref[...] = _mm_nt(h, w_ref[N_HG:N_HG + N_CV, :])
        gt_ref[...] = _mm_nt(h, w_ref[N_HG + N_CV:, :])

    row = lambda n: pl.BlockSpec((tm, n), lambda i: (i, 0))
    return pl.pallas_call(
        body, name="fwd_in", grid=(T // tm,),
        in_specs=[row(D_MODEL), _full((1, D_MODEL)), _resident(w_in_t.shape)],
        out_specs=[_col(tm, D_MODEL), row(N_HG), row(N_CV), row(N_GT)],
        out_shape=[jax.ShapeDtypeStruct((D_MODEL, T), BF16), jax.ShapeDtypeStruct((T, N_HG), F32),
                   jax.ShapeDtypeStruct((T, N_CV), F32), jax.ShapeDtypeStruct((T, N_GT), F32)],
        compiler_params=_params(("parallel",)),
    )(x, g, w_in_t)


def _chunk_pos(shape):
    return lax.broadcasted_iota(jnp.int32, shape, 0) & (CHUNK - 1)


def _chunk_cumsum(x, pos):
    s = 1
    while s < CHUNK:
        x = x + jnp.where(pos >= s, pltpu.roll(x, s, 0), 0.0)
        s *= 2
    return x


def _chunk_rev_cumsum(x, pos):
    n = x.shape[0]
    s = 1
    while s < CHUNK:
        x = x + jnp.where(pos + s < CHUNK, pltpu.roll(x, n - s, 0), 0.0)
        s *= 2
    return x


def _chunk_bcast(x3, row, tb):
    return jnp.broadcast_to(x3[:, row:row + 1, :], x3.shape).reshape(tb, x3.shape[-1])


def _lower_bound(low_ref):
    l0 = low_ref[0:1, :]
    l1 = low_ref[1:2, :]
    m = jnp.maximum(l0, l1)
    e0 = jnp.exp(l0 - m)
    e1 = jnp.exp(l1 - m)
    return e0 / (e0 + e1), e1 / (e0 + e1)


def _hg_gates(qr, fr, lb, pos, tb):
    sq = _sigmoid(qr)
    q = qr * sq * Q_SCALE
    sg = _sigmoid(fr)
    f = lb + (1.0 - lb) * sg
    k = 1.0 - f
    b = _chunk_cumsum(jnp.log(f), pos)
    b3 = b.reshape(tb // CHUNK, CHUNK, HEAD_DIM)
    anc = _chunk_bcast(b3, CHUNK // 2 - 1, tb)
    blb = _chunk_bcast(b3, CHUNK - 1, tb)
    e_qa = jnp.exp(b - anc)
    e_ka = jnp.exp(anc - b)
    e_b = jnp.exp(b)
    e_ko = jnp.exp(blb - b)
    dec = jnp.exp(blb)
    return sq, q, sg, f, k, e_qa, e_ka, e_b, e_ko, dec


def _intra_mask(sb):
    r = lax.broadcasted_iota(jnp.int32, (sb, sb), 0)
    c = lax.broadcasted_iota(jnp.int32, (sb, sb), 1)
    return ((r // CHUNK) == (c // CHUNK)) & (c <= r)


def _hg_fwd(hg, low, gn):
    T = hg.shape[0]
    tb = min(512, T)
    sb = min(256, tb)
    nb = T // tb
    nc = tb // CHUNK
    wid = HEADS_PER_STEP * HEAD_DIM

    def body(q_ref, f_ref, i_ref, g_ref, low_ref, gn_ref, o_ref, og_ref, ogt_ref, st_ref, s_scr):
        t = pl.program_id(1)

        @pl.when(t == 0)
        def _():
            s_scr[...] = jnp.zeros_like(s_scr)

        pos = _chunk_pos((tb, HEAD_DIM))
        mask = _intra_mask(sb)
        lanes = [slice(hh * HEAD_DIM, (hh + 1) * HEAD_DIM) for hh in range(HEADS_PER_STEP)]
        qi, ko, vb, dec, st = [], [], [], [], []
        for hh, ln in enumerate(lanes):
            lb, _ = _lower_bound(low_ref.at[:, ln])
            _, q, _, _, k, e_qa, e_ka, e_b, e_ko, dec_h = _hg_gates(q_ref[:, ln], f_ref[:, ln], lb, pos, tb)
            qh = (q * e_qa).astype(BF16)
            kh = (k * e_ka).astype(BF16)
            qi.append((q * e_b).astype(BF16))
            ko.append((k * e_ko).astype(BF16))
            vb.append(i_ref[:, ln].astype(BF16))
            dec.append(dec_h)
            st.append(s_scr[hh])
            for s in range(tb // sb):
                sl = slice(s * sb, (s + 1) * sb)
                p = jnp.where(mask, _mm_nt(qh[sl], kh[sl]), 0.0)
                o_ref[sl, ln] = _mm(p, vb[hh][sl])
        for c in range(nc):
            sl = slice(c * CHUNK, (c + 1) * CHUNK)
            for hh, ln in enumerate(lanes):
                st_ref[hh, c] = st[hh]
                o_ref[sl, ln] = o_ref[sl, ln] + _mm_nt(qi[hh][sl], st[hh])
                st[hh] = dec[hh][c * CHUNK:c * CHUNK + 1, :] * st[hh] + _mm_tn(vb[hh][sl], ko[hh][sl])
        for hh, ln in enumerate(lanes):
            s_scr[hh] = st[hh]
            o = o_ref[:, ln]
            r = lax.rsqrt(jnp.mean(o * o, axis=-1, keepdims=True) + EPS)
            gr = g_ref[:, ln]
            og = (o * r * gn_ref[...]) * (gr * _sigmoid(gr))
            og_ref[:, ln] = og.astype(BF16)
            ogt_ref[ln, :] = og.T.astype(BF16)

    col = lambda p: pl.BlockSpec((tb, wid), lambda h, t: (t, p * HEAD_GROUPS + h))
    hcol = pl.BlockSpec((tb, wid), lambda h, t: (t, h))
    return pl.pallas_call(
        body, name="hg_fwd", grid=(HEAD_GROUPS, nb),
        in_specs=[col(0), col(1), col(2), col(3), pl.BlockSpec((2, wid), lambda h, t: (0, h)),
                  pl.BlockSpec((1, HEAD_DIM), lambda h, t: (0, 0))],
        out_specs=[hcol, hcol, pl.BlockSpec((wid, tb), lambda h, t: (h, t)),
                   pl.BlockSpec((HEADS_PER_STEP, nc, HEAD_DIM, HEAD_DIM), lambda h, t: (h, t, 0, 0))],
        out_shape=[jax.ShapeDtypeStruct((T, HG_WIDTH), F32), jax.ShapeDtypeStruct((T, HG_WIDTH), BF16),
                   jax.ShapeDtypeStruct((HG_WIDTH, T), BF16),
                   jax.ShapeDtypeStruct((N_HEADS, T // CHUNK, HEAD_DIM, HEAD_DIM), F32)],
        scratch_shapes=[pltpu.VMEM((HEADS_PER_STEP, HEAD_DIM, HEAD_DIM), F32)],
        compiler_params=_params(("parallel", "arbitrary")),
    )(hg, hg, hg, hg, low, gn)


def _conv_fwd(cv, conv_w):
    T = cv.shape[0]
    nj = CONV_WIDTH // 128

    def body(c_ref, b_ref, x_ref, w_ref, o_ref, ot_ref):
        row = lax.broadcasted_iota(jnp.int32, (T, 128), 0)
        u = c_ref[...] * x_ref[...]
        u1 = jnp.where(row >= 1, pltpu.roll(u, 1, 0), 0.0)
        u2 = jnp.where(row >= 2, pltpu.roll(u, 2, 0), 0.0)
        y = w_ref[0:1, :] * u2 + w_ref[1:2, :] * u1 + w_ref[2:3, :] * u
        out = b_ref[...] * y
        o_ref[...] = out.astype(BF16)
        ot_ref[...] = out.T.astype(BF16)

    col = lambda p: pl.BlockSpec((T, 128), lambda j: (0, p * nj + j))
    return pl.pallas_call(
        body, name="conv_fwd", grid=(nj,),
        in_specs=[col(0), col(1), col(2), pl.BlockSpec((3, 128), lambda j: (0, j))],
        out_specs=[pl.BlockSpec((T, 128), lambda j: (0, j)), pl.BlockSpec((128, T), lambda j: (j, 0))],
        out_shape=[jax.ShapeDtypeStruct((T, CONV_WIDTH), BF16), jax.ShapeDtypeStruct((CONV_WIDTH, T), BF16)],
        compiler_params=_params(("parallel",)),
    )(cv, cv, cv, conv_w)


def _merge_fwd(og, cvo, gt, x, wa, wb, wo):
    T = x.shape[0]
    tm = min(512, T)

    def body(og_ref, cvo_ref, gt_ref, x_ref, wa_ref, wb_ref, wo_ref, x1_ref, mgt_ref):
        ya = jnp.dot(og_ref[...], _shard_cols(wa_ref), preferred_element_type=F32)
        yb = jnp.dot(cvo_ref[...], _shard_cols(wb_ref), preferred_element_type=F32)
        m = _sigmoid(gt_ref[:, :D_MODEL]) * ya + _sigmoid(gt_ref[:, D_MODEL:]) * yb
        mgt_ref[...] = m.T.astype(BF16)
        x1_ref[...] = x_ref[...] + jnp.dot(m.astype(BF16), wo_ref[...], preferred_element_type=F32)

    row = lambda n: pl.BlockSpec((tm, n), lambda i: (i, 0))
    return pl.pallas_call(
        body, name="merge_fwd", grid=(T // tm,),
        in_specs=[row(HG_WIDTH), row(CONV_WIDTH), row(2 * D_MODEL), row(D_MODEL),
                  _resident(wa.shape), _resident(wb.shape), _resident(wo.shape)],
        out_specs=[row(D_MODEL), _col(tm, D_MODEL)],
        out_shape=[jax.ShapeDtypeStruct((T, D_MODEL), F32), jax.ShapeDtypeStruct((D_MODEL, T), BF16)],
        compiler_params=_params(("parallel",)),
    )(og, cvo, gt, x, wa, wb, wo)


def _ffn_fwd(x1, g, wg, wu, wd):
    T = x1.shape[0]
    tm = min(256, T)

    def body(x_ref, g_ref, wg_ref, wu_ref, wd_ref, ht_ref, gate_ref, up_ref, actt_ref, x2_ref):
        xv = x_ref[...]
        r = lax.rsqrt(jnp.mean(xv * xv, axis=-1, keepdims=True) + EPS)
        hf = xv * r * g_ref[...]
        h = hf.astype(BF16)
        ht_ref[...] = hf.T.astype(BF16)
        gate = _mm_nt(h, wg_ref[...])
        up = _mm_nt(h, wu_ref[...])
        gate_ref[...] = gate
        up_ref[...] = up
        act = gate * _sigmoid(gate) * up
        actt_ref[...] = act.T.astype(BF16)
        x2_ref[...] = xv + jnp.dot(act.astype(BF16), wd_ref[...], preferred_element_type=F32)

    row = lambda n: pl.BlockSpec((tm, n), lambda i: (i, 0))
    return pl.pallas_call(
        body, name="ffn_fwd", grid=(T // tm,),
        in_specs=[row(D_MODEL), _full((1, D_MODEL)), _resident(wg.shape), _resident(wu.shape), _resident(wd.shape)],
        out_specs=[_col(tm, D_MODEL), row(D_FF), row(D_FF), _col(tm, D_FF), row(D_MODEL)],
        out_shape=[jax.ShapeDtypeStruct((D_MODEL, T), BF16), jax.ShapeDtypeStruct((T, D_FF), F32),
                   jax.ShapeDtypeStruct((T, D_FF), F32), jax.ShapeDtypeStruct((D_FF, T), BF16),
                   jax.ShapeDtypeStruct((T, D_MODEL), F32)],
        compiler_params=_params(("parallel",)),
    )(x1, g, wg, wu, wd)


def _final_fwd_bwd(x2, target, g):
    T = x2.shape[0]
    tm = min(512, T)

    def body(x_ref, t_ref, g_ref, loss_ref, dg_ref, dx_ref):
        @pl.when(pl.program_id(0) == 0)
        def _():
            loss_ref[...] = jnp.zeros_like(loss_ref)
            dg_ref[...] = jnp.zeros_like(dg_ref)

        xv = x_ref[...]
        gv = g_ref[...]
        r = lax.rsqrt(jnp.mean(xv * xv, axis=-1, keepdims=True) + EPS)
        xh = xv * r
        err = xh * gv - t_ref[...]
        loss_ref[...] += 0.5 * jnp.sum(jnp.mean(err * err, axis=-1, keepdims=True), axis=0, keepdims=True)
        dy = err * (1.0 / D_MODEL)
        dg_ref[...] += jnp.sum(dy * xh, axis=0, keepdims=True)
        w = dy * gv
        dx_ref[...] = r * (w - xh * jnp.mean(w * xh, axis=-1, keepdims=True))

    row = pl.BlockSpec((tm, D_MODEL), lambda i: (i, 0))
    return pl.pallas_call(
        body, name="final_fwd_bwd", grid=(T // tm,),
        in_specs=[row, row, _full((1, D_MODEL))],
        out_specs=[_full((1, 128)), _full((1, D_MODEL)), row],
        out_shape=[jax.ShapeDtypeStruct((1, 128), F32), jax.ShapeDtypeStruct((1, D_MODEL), F32),
                   jax.ShapeDtypeStruct((T, D_MODEL), F32)],
        compiler_params=_params(("arbitrary",)),
    )(x2, target, g)


def _ffn_bwd(dx2, x1, gate, up, g, wg, wu, wd):
    T = x1.shape[0]
    tm = min(256, T)

    def body(dx2_ref, x_ref, gate_ref, up_ref, g_ref, wg_ref, wu_ref, wd_ref, dgate_ref, dup_ref, dx1_ref, dgn_ref):
        @pl.when(pl.program_id(0) == 0)
        def _():
            dgn_ref[...] = jnp.zeros_like(dgn_ref)

        dx2 = dx2_ref[...]
        dact = _mm_nt(dx2, wd_ref[...])
        gate = gate_ref[...]
        s = _sigmoid(gate)
        dgate = (dact * up_ref[...] * (s * (1.0 + gate * (1.0 - s)))).astype(BF16)
        dup = (dact * (gate * s)).astype(BF16)
        dgate_ref[...] = dgate
        dup_ref[...] = dup
        dh = _mm(dgate, wg_ref[...]) + _mm(dup, wu_ref[...])
        xv = x_ref[...]
        r = lax.rsqrt(jnp.mean(xv * xv, axis=-1, keepdims=True) + EPS)
        xh = xv * r
        dgn_ref[...] += jnp.sum(dh * xh, axis=0, keepdims=True)
        w = dh * g_ref[...]
        dx1_ref[...] = dx2 + r * (w - xh * jnp.mean(w * xh, axis=-1, keepdims=True))

    row = lambda n: pl.BlockSpec((tm, n), lambda i: (i, 0))
    return pl.pallas_call(
        body, name="ffn_bwd", grid=(T // tm,),
        in_specs=[row(D_MODEL), row(D_MODEL), row(D_FF), row(D_FF), _full((1, D_MODEL)),
                  _resident(wg.shape), _resident(wu.shape), _resident(wd.shape)],
        out_specs=[row(D_FF), row(D_FF), row(D_MODEL), _full((1, D_MODEL))],
        out_shape=[jax.ShapeDtypeStruct((T, D_FF), BF16), jax.ShapeDtypeStruct((T, D_FF), BF16),
                   jax.ShapeDtypeStruct((T, D_MODEL), F32), jax.ShapeDtypeStruct((1, D_MODEL), F32)],
        compiler_params=_params(("arbitrary",)),
    )(dx2, x1, gate, up, g, wg, wu, wd)


def _merge_bwd(dx1, og, cvo, gt, wa, wb, wo):
    T = dx1.shape[0]
    tm = min(512, T)

    def body(dx_ref, og_ref, cvo_ref, gt_ref, wa_ref, wb_ref, wo_ref, dgt_ref, dya_ref, dyb_ref, dog_ref, dcvo_ref):
        dm = _mm_nt(dx_ref[...], wo_ref[...])
        wa = _shard_cols(wa_ref)
        wb = _shard_cols(wb_ref)
        ya = jnp.dot(og_ref[...], wa, preferred_element_type=F32)
        yb = jnp.dot(cvo_ref[...], wb, preferred_element_type=F32)
        sa = _sigmoid(gt_ref[:, :D_MODEL])
        sb = _sigmoid(gt_ref[:, D_MODEL:])
        dgt_ref[:, :D_MODEL] = (dm * ya * (sa * (1.0 - sa))).astype(BF16)
        dgt_ref[:, D_MODEL:] = (dm * yb * (sb * (1.0 - sb))).astype(BF16)
        dya = (dm * sa).astype(BF16)
        dyb = (dm * sb).astype(BF16)
        dya_ref[...] = dya
        dyb_ref[...] = dyb
        dog_ref[...] = _mm_nt(dya, wa)
        dcvo_ref[...] = _mm_nt(dyb, wb)

    row = lambda n: pl.BlockSpec((tm, n), lambda i: (i, 0))
    return pl.pallas_call(
        body, name="merge_bwd", grid=(T // tm,),
        in_specs=[row(D_MODEL), row(HG_WIDTH), row(CONV_WIDTH), row(2 * D_MODEL),
                  _resident(wa.shape), _resident(wb.shape), _resident(wo.shape)],
        out_specs=[row(2 * D_MODEL), row(D_MODEL), row(D_MODEL), row(HG_WIDTH), row(CONV_WIDTH)],
        out_shape=[jax.ShapeDtypeStruct((T, 2 * D_MODEL), BF16), jax.ShapeDtypeStruct((T, D_MODEL), BF16),
                   jax.ShapeDtypeStruct((T, D_MODEL), BF16), jax.ShapeDtypeStruct((T, HG_WIDTH), F32),
                   jax.ShapeDtypeStruct((T, CONV_WIDTH), F32)],
        compiler_params=_params(("parallel",)),
    )(dx1, og, cvo, gt, wa, wb, wo)


def _conv_bwd(dcvo, cv, conv_w):
    T = cv.shape[0]
    nj = CONV_WIDTH // 128

    def body(do_ref, c_ref, b_ref, x_ref, w_ref, dc_ref, db_ref, dx_ref, dw_ref):
        row = lax.broadcasted_iota(jnp.int32, (T, 128), 0)
        c = c_ref[...]
        xb = x_ref[...]
        do = do_ref[...]
        u = c * xb
        u1 = jnp.where(row >= 1, pltpu.roll(u, 1, 0), 0.0)
        u2 = jnp.where(row >= 2, pltpu.roll(u, 2, 0), 0.0)
        w0, w1, w2 = w_ref[0:1, :], w_ref[1:2, :], w_ref[2:3, :]
        y = w0 * u2 + w1 * u1 + w2 * u
        db_ref[...] = (do * y).astype(BF16)
        dy = do * b_ref[...]
        dw_ref[0:1, :] = jnp.sum(dy * u2, axis=0, keepdims=True)
        dw_ref[1:2, :] = jnp.sum(dy * u1, axis=0, keepdims=True)
        dw_ref[2:3, :] = jnp.sum(dy * u, axis=0, keepdims=True)
        dy1 = jnp.where(row < T - 1, pltpu.roll(dy, T - 1, 0), 0.0)
        dy2 = jnp.where(row < T - 2, pltpu.roll(dy, T - 2, 0), 0.0)
        du = w2 * dy + w1 * dy1 + w0 * dy2
        dc_ref[...] = (du * xb).astype(BF16)
        dx_ref[...] = (du * c).astype(BF16)

    col = lambda p: pl.BlockSpec((T, 128), lambda j: (0, p * nj + j))
    one = pl.BlockSpec((T, 128), lambda j: (0, j))
    wspec = pl.BlockSpec((3, 128), lambda j: (0, j))
    out = jax.ShapeDtypeStruct((T, CONV_WIDTH), BF16)
    return pl.pallas_call(
        body, name="conv_bwd", grid=(nj,),
        in_specs=[one, col(0), col(1), col(2), wspec],
        out_specs=[one, one, one, wspec],
        out_shape=[out, out, out, jax.ShapeDtypeStruct((3, CONV_WIDTH), F32)],
        compiler_params=_params(("parallel",)),
    )(dcvo, cv, cv, cv, conv_w)


def _drop_operands(body, first, count):
    def wrapped(*refs):
        return body(*refs[:first], *refs[first + count:])
    return wrapped


def _hg_bwd(dog, hg, o, st, low, gn, after=()):
    T = hg.shape[0]
    tb = min(512, T)
    sb = min(256, tb)
    nb = T // tb
    nc = tb // CHUNK
    wid = HEADS_PER_STEP * HEAD_DIM

    def body(q_ref, f_ref, i_ref, g_ref, low_ref, gn_ref, o_ref, dog_ref, st_ref,
             dq_ref, df_ref, di_ref, dg_ref, dlow_ref, dgn_ref,
             ds_scr, dqi_scr, dko_scr, dv_scr, dd_scr, dqh_scr, dkh_scr):
        h = pl.program_id(0)
        t = pl.program_id(1)

        @pl.when(t == 0)
        def _():
            ds_scr[...] = jnp.zeros_like(ds_scr)
            dlow_ref[...] = jnp.zeros_like(dlow_ref)

        @pl.when((t == 0) & (h == 0))
        def _():
            dgn_ref[...] = jnp.zeros_like(dgn_ref)

        pos = _chunk_pos((tb, HEAD_DIM))
        mask = _intra_mask(sb)
        gnv = gn_ref[...]
        lanes = [slice(hh * HEAD_DIM, (hh + 1) * HEAD_DIM) for hh in range(HEADS_PER_STEP)]
        heads = []
        for hh, ln in enumerate(lanes):
            lb, lb1 = _lower_bound(low_ref.at[:, ln])
            qr = q_ref[:, ln]
            sq, q, sg, f, k, e_qa, e_ka, e_b, e_ko, dec = _hg_gates(qr, f_ref[:, ln], lb, pos, tb)

            gr = g_ref[:, ln]
            o = o_ref[:, ln]
            dog_v = dog_ref[:, ln]
            sgr = _sigmoid(gr)
            r = lax.rsqrt(jnp.mean(o * o, axis=-1, keepdims=True) + EPS)
            oh = o * r
            dg_ref[:, ln] = (dog_v * (oh * gnv) * (sgr * (1.0 + gr * (1.0 - sgr)))).astype(BF16)
            don = dog_v * (gr * sgr)
            dgn_ref[...] += jnp.sum(don * oh, axis=0, keepdims=True)
            w = don * gnv
            do = (r * (w - oh * jnp.mean(w * oh, axis=-1, keepdims=True))).astype(BF16)

            qh = (q * e_qa).astype(BF16)
            kh = (k * e_ka).astype(BF16)
            qi = (q * e_b).astype(BF16)
            ko = (k * e_ko).astype(BF16)
            vb = i_ref[:, ln].astype(BF16)

            for s in range(tb // sb):
                sl = slice(s * sb, (s + 1) * sb)
                p = jnp.where(mask, _mm_nt(qh[sl], kh[sl]), 0.0).astype(BF16)
                dp = jnp.where(mask, _mm_nt(do[sl], vb[sl]), 0.0).astype(BF16)
                dv_scr[sl, ln] = _mm_tn(p, do[sl])
                dqh_scr[sl, ln] = _mm(dp, kh[sl])
                dkh_scr[sl, ln] = _mm_tn(dp, qh[sl])
            heads.append(dict(lb=lb, lb1=lb1, qr=qr, sq=sq, q=q, sg=sg, f=f, k=k, e_qa=e_qa, e_ka=e_ka, e_b=e_b,
                              e_ko=e_ko, dec=dec, do=do, qi=qi, ko=ko, vb=vb, ds=ds_scr[hh]))

        for c in reversed(range(nc)):
            sl = slice(c * CHUNK, (c + 1) * CHUNK)
            for hh, ln in enumerate(lanes):
                hd = heads[hh]
                ds = hd["ds"]
                st_c = st_ref[hh, c]
                dqi_scr[sl, ln] = _mm(hd["do"][sl], st_c)
                dko_scr[sl, ln] = _mm(hd["vb"][sl], ds)
                dv_scr[sl, ln] = dv_scr[sl, ln] + _mm_nt(hd["ko"][sl], ds)
                dd_scr[sl, ln] = jnp.broadcast_to(jnp.sum(ds * st_c, axis=0, keepdims=True), (CHUNK, HEAD_DIM))
                hd["ds"] = hd["dec"][c * CHUNK:c * CHUNK + 1, :] * ds + _mm_tn(hd["do"][sl], hd["qi"][sl])

        for hh, ln in enumerate(lanes):
            hd = heads[hh]
            ds_scr[hh] = hd["ds"]
            q, k, lb = hd["q"], hd["k"], hd["lb"]
            dko_e = dko_scr[:, ln] * hd["e_ko"]
            dq = dqh_scr[:, ln] * hd["e_qa"] + dqi_scr[:, ln] * hd["e_b"]
            dk = dkh_scr[:, ln] * hd["e_ka"] + dko_e
            kd3 = (k * dko_e).reshape(nc, CHUNK, HEAD_DIM)
            last = jnp.broadcast_to(jnp.sum(kd3, axis=1, keepdims=True), kd3.shape).reshape(tb, HEAD_DIM)
            db = q * dq - k * dk + jnp.where(pos == CHUNK - 1, hd["dec"] * dd_scr[:, ln] + last, 0.0)
            dlg = _chunk_rev_cumsum(db, pos)
            dfv = dlg / hd["f"] - dk
            s_low = jnp.sum(dfv * (1.0 - hd["sg"]), axis=0, keepdims=True)
            dlow_ref[0:1, ln] += s_low * lb * (1.0 - lb)
            dlow_ref[1:2, ln] += -s_low * lb * hd["lb1"]
            df_ref[:, ln] = (dfv * (1.0 - lb) * hd["sg"] * (1.0 - hd["sg"])).astype(BF16)
            dq_ref[:, ln] = (dq * Q_SCALE * (hd["sq"] * (1.0 + hd["qr"] * (1.0 - hd["sq"])))).astype(BF16)
            di_ref[:, ln] = dv_scr[:, ln].astype(BF16)

    rt = lambda t: nb - 1 - t
    col = lambda p: pl.BlockSpec((tb, wid), lambda h, t: (rt(t), p * HEAD_GROUPS + h))
    hcol = pl.BlockSpec((tb, wid), lambda h, t: (rt(t), h))
    piece = jax.ShapeDtypeStruct((T, HG_WIDTH), BF16)
    tile = pltpu.VMEM((tb, wid), F32)
    return pl.pallas_call(
        _drop_operands(body, 9, len(after)), name="hg_bwd", grid=(HEAD_GROUPS, nb),
        in_specs=[col(0), col(1), col(2), col(3), pl.BlockSpec((2, wid), lambda h, t: (0, h)),
                  pl.BlockSpec((1, HEAD_DIM), lambda h, t: (0, 0)), hcol, hcol,
                  pl.BlockSpec((HEADS_PER_STEP, nc, HEAD_DIM, HEAD_DIM), lambda h, t: (h, rt(t), 0, 0))]
                 + [HBM_SPEC] * len(after),
        out_specs=[hcol, hcol, hcol, hcol, pl.BlockSpec((2, wid), lambda h, t: (0, h)),
                   pl.BlockSpec((1, HEAD_DIM), lambda h, t: (0, 0))],
        out_shape=[piece, piece, piece, piece, jax.ShapeDtypeStruct((2, HG_WIDTH), F32),
                   jax.ShapeDtypeStruct((1, HEAD_DIM), F32)],
        scratch_shapes=[pltpu.VMEM((HEADS_PER_STEP, HEAD_DIM, HEAD_DIM), F32), tile, tile, tile, tile, tile, tile],
        compiler_params=_params(("arbitrary", "arbitrary")),
    )(hg, hg, hg, hg, low, gn, o, dog, st, *after)


def _in_bwd(dparts, w_in, x, dx1, g, after=()):
    T = x.shape[0]
    tm = min(512, T)
    widths = [p.shape[1] for p in dparts]
    offs = [sum(widths[:i]) for i in range(len(widths))]
    n = len(dparts)

    def body(*refs):
        d_refs = refs[:n]
        w_ref, x_ref, dx1_ref, g_ref, dx_ref, dgn_ref = refs[n:]

        @pl.when(pl.program_id(0) == 0)
        def _():
            dgn_ref[...] = jnp.zeros_like(dgn_ref)

        dh = None
        for d_ref, off, wd in zip(d_refs, offs, widths):
            part = _mm(d_ref[...], w_ref[off:off + wd, :])
            dh = part if dh is None else dh + part
        xv = x_ref[...]
        r = lax.rsqrt(jnp.mean(xv * xv, axis=-1, keepdims=True) + EPS)
        xh = xv * r
        dgn_ref[...] += jnp.sum(dh * xh, axis=0, keepdims=True)
        w = dh * g_ref[...]
        dx_ref[...] = dx1_ref[...] + r * (w - xh * jnp.mean(w * xh, axis=-1, keepdims=True))

    row = lambda m: pl.BlockSpec((tm, m), lambda i: (i, 0))
    return pl.pallas_call(
        _drop_operands(body, n + 4, len(after)), name="in_bwd", grid=(T // tm,),
        in_specs=[row(wd) for wd in widths] + [_resident(w_in.shape), row(D_MODEL), row(D_MODEL), _full((1, D_MODEL))]
                 + [HBM_SPEC] * len(after),
        out_specs=[row(D_MODEL), _full((1, D_MODEL))],
        out_shape=[jax.ShapeDtypeStruct((T, D_MODEL), F32), jax.ShapeDtypeStruct((1, D_MODEL), F32)],
        compiler_params=_params(("arbitrary",)),
    )(*dparts, w_in, x, dx1, g, *after)


def _wgrad(name, at, b, tn, transposed=False):
    M, T = at.shape
    N = b.shape[1]
    tk = min(2048, T)
    nk = T // tk

    def body(a_ref, b_ref, o_ref, acc):
        k = pl.program_id(1)
        part = _mm(a_ref[...], b_ref[...])

        @pl.when(k == 0)
        def _():
            acc[...] = part

        @pl.when(k != 0)
        def _():
            acc[...] += part

        @pl.when(k == nk - 1)
        def _():
            o_ref[...] = (acc[...].T if transposed else acc[...]).astype(BF16)

    if transposed:
        out_spec, out_shape = pl.BlockSpec((tn, M), lambda j, k: (j, 0)), (N, M)
    else:
        out_spec, out_shape = pl.BlockSpec((M, tn), lambda j, k: (0, j)), (M, N)
    return pl.pallas_call(
        body, name=name, grid=(N // tn, nk),
        in_specs=[pl.BlockSpec((M, tk), lambda j, k: (0, k)), pl.BlockSpec((tk, tn), lambda j, k: (k, j))],
        out_specs=out_spec, out_shape=jax.ShapeDtypeStruct(out_shape, BF16),
        scratch_shapes=[pltpu.VMEM((M, tn), F32)],
        compiler_params=_params(("parallel", "arbitrary")),
    )(at, b)


def _wgrad_in(ht, dparts, after=()):
    M, T = ht.shape
    tn = 512
    tk = min(2048, T)
    nk = T // tk
    nblk = [p.shape[1] // tn for p in dparts]
    start = [sum(nblk[:i]) for i in range(len(nblk))]
    n = len(dparts)

    def body(a_ref, *refs):
        d_refs, o_ref, acc = refs[:n], refs[n], refs[n + 1]
        j = pl.program_id(0)
        k = pl.program_id(1)

        @pl.when(k == 0)
        def _():
            acc[...] = jnp.zeros_like(acc)

        for d_ref, s, nb in zip(d_refs, start, nblk):
            @pl.when((j >= s) & (j < s + nb))
            def _():
                acc[...] += _mm(a_ref[...], d_ref[...])

        @pl.when(k == nk - 1)
        def _():
            o_ref[...] = acc[...].T.astype(BF16)

    def piece_spec(s, nb):
        def index(j, k):
            inside = (j >= s) & (j < s + nb)
            return jnp.where(inside, k, 0), jnp.clip(j - s, 0, nb - 1)
        return pl.BlockSpec((tk, tn), index)

    return pl.pallas_call(
        _drop_operands(body, 1 + n, len(after)), name="wgrad_in", grid=(sum(nblk), nk),
        in_specs=[pl.BlockSpec((M, tk), lambda j, k: (0, k))] + [piece_spec(s, nb) for s, nb in zip(start, nblk)]
                 + [HBM_SPEC] * len(after),
        out_specs=pl.BlockSpec((tn, M), lambda j, k: (j, 0)),
        out_shape=jax.ShapeDtypeStruct((sum(nblk) * tn, M), BF16),
        scratch_shapes=[pltpu.VMEM((M, tn), F32)],
        compiler_params=_params(("parallel", "arbitrary")),
    )(ht, *dparts, *after)


def _adamw_math(w, g, m, v):
    m = ADAM_B1 * m + (1.0 - ADAM_B1) * g
    v = ADAM_B2 * v + (1.0 - ADAM_B2) * (g * g)
    m_hat = m / (1.0 - ADAM_B1 ** ADAM_STEP)
    v_hat = v / (1.0 - ADAM_B2 ** ADAM_STEP)
    delta = -ADAM_LR * (m_hat / (jnp.sqrt(v_hat) + ADAM_EPS) + ADAM_WD * w)
    return delta, m, v


def _adamw_sum(name, w, parts, m, v):
    R, C = w.shape
    tr = _row_tile(R)

    def body(w_ref, p_ref, m_ref, v_ref, g_out, d_out, m_out, v_out):
        g = p_ref[0].astype(F32)
        for k in range(1, 4):
            g = g + p_ref[k].astype(F32)
        g_out[...] = g
        d_out[...], m_out[...], v_out[...] = _adamw_math(w_ref[...], g, m_ref[...], v_ref[...])

    blk = pl.BlockSpec((tr, C), lambda i: (i, 0))
    out = jax.ShapeDtypeStruct((R, C), F32)
    return pl.pallas_call(
        body, name=name, grid=(R // tr,),
        in_specs=[blk, pl.BlockSpec((4, tr, C), lambda i: (0, i, 0)), blk, blk],
        out_specs=[blk, blk, blk, blk], out_shape=[out, out, out, out],
        compiler_params=_params(("parallel",)),
    )(w, parts, m, v)


def _small_sum(gathered):
    R = gathered.shape[1]

    def body(p_ref, o_ref):
        g = p_ref[0]
        for k in range(1, N_DEV):
            g = g + p_ref[k]
        o_ref[...] = g

    return pl.pallas_call(
        body, name="small_sum", in_specs=[_full(gathered.shape)], out_specs=_full((R, 128)), grid=(1,),
        out_shape=jax.ShapeDtypeStruct((R, 128), F32),
    )(gathered)


def _small_adamw(w, g, m, v):
    def body(w_ref, g_ref, m_ref, v_ref, d_out, m_out, v_out):
        d_out[...], m_out[...], v_out[...] = _adamw_math(w_ref[...], g_ref[...], m_ref[...], v_ref[...])

    out = jax.ShapeDtypeStruct(w.shape, F32)
    spec = _full(w.shape)
    return pl.pallas_call(
        body, name="small_adamw", grid=(1,), in_specs=[spec] * 4, out_specs=[spec] * 3, out_shape=[out, out, out],
    )(w, g, m, v)


def _row_tile(rows):
    for cand in (256, 128):
        if rows % cand == 0:
            return cand
    return rows


def _pair_sum(name, by_owner, got, core):
    _, R, C = got.shape
    tr = _row_tile(R)

    def body(core_ref, a_ref, b_ref, o_ref):
        o_ref[...] = (a_ref[...].astype(F32) + b_ref[...].astype(F32)).astype(BF16)

    blk = pl.BlockSpec((None, tr, C), lambda k, i, core_ref: (k, i, 0))
    mine = pl.BlockSpec((None, tr, C), lambda k, i, core_ref: (2 * k + core_ref[0], i, 0))
    return pl.pallas_call(
        body, name=name,
        grid_spec=pltpu.PrefetchScalarGridSpec(num_scalar_prefetch=1, grid=(4, R // tr), in_specs=[mine, blk],
                                               out_specs=blk),
        out_shape=jax.ShapeDtypeStruct(got.shape, BF16),
        compiler_params=_params(("parallel", "parallel")),
    )(core, by_owner, got)


def _shards_from_cols(name, full):
    R, allc = full.shape
    c = allc // N_DEV
    tr = _row_tile(R)

    def body(f_ref, o_ref):
        for s in range(N_DEV):
            o_ref[s] = f_ref[:, s * c:(s + 1) * c]

    return pl.pallas_call(
        body, name=name, grid=(R // tr,),
        in_specs=[pl.BlockSpec((tr, allc), lambda i: (i, 0))],
        out_specs=pl.BlockSpec((N_DEV, tr, c), lambda i: (0, i, 0)),
        out_shape=jax.ShapeDtypeStruct((N_DEV, R, c), full.dtype),
        compiler_params=_params(("parallel",)),
    )(full)


MESH = pl.DeviceIdType.MESH
HBM_SPEC = pl.BlockSpec(memory_space=pl.ANY)


def _handshake(peers):
    barrier = pltpu.get_barrier_semaphore()
    for peer in peers:
        pl.semaphore_signal(barrier, inc=1, device_id=peer, device_id_type=MESH)
    pl.semaphore_wait(barrier, len(peers))


def _comm_call(body, name, operands, out_shape, scratch, collective_id):
    if collective_id is None:
        return pl.pallas_call(body, name=name, in_specs=[HBM_SPEC] * len(operands), out_specs=[HBM_SPEC] * len(out_shape),
                              out_shape=out_shape, scratch_shapes=scratch)(*operands)
    return pl.kernel(body, out_type=out_shape, mesh=plsc.ScalarSubcoreMesh(axis_name="sequencer", num_cores=1),
                     scratch_types=scratch, name=name,
                     compiler_params=pltpu.CompilerParams(collective_id=collective_id))(*operands)


def _all_gather(name, blocks, collective_id=None, after=()):
    n = len(blocks)
    na = len(after)

    def body(*refs):
        x_refs, out_refs = refs[:n], refs[n + na:2 * n + na]
        send_sems, recv_sems, local_sems = refs[2 * n + na:]
        x, y, c = lax.axis_index("x"), lax.axis_index("y"), lax.axis_index("c")
        me, sibling = (x, y, c), (x, y, 1 - c)
        chips = [(1 - x, y), (x, 1 - y), (1 - x, 1 - y)]
        if collective_id is not None:
            _handshake([sibling] + [(*chip, c) for chip in chips])

        def slot(i, px, py, pc):
            return out_refs[i].at[4 * px + 2 * py + pc]

        def copy(i, k, blk, to, src=None):
            return pltpu.make_async_remote_copy(
                src_ref=slot(i, *blk) if src is None else src, dst_ref=slot(i, *blk),
                send_sem=send_sems.at[7 * i + k], recv_sem=recv_sems.at[7 * i + k], device_id=to, device_id_type=MESH)

        mine = [pltpu.make_async_copy(x_refs[i], slot(i, *me), local_sems.at[i]) for i in range(n)]
        for cp in mine:
            cp.start()
        first = []
        for i in range(n):
            first.append(copy(i, 0, me, sibling, src=x_refs[i]))
            first += [copy(i, 1 + j, me, (*chip, c), src=x_refs[i]) for j, chip in enumerate(chips)]
        for cp in first:
            cp.start()
        passed = []
        for i in range(n):
            for j, chip in enumerate(chips):
                copy(i, 1 + j, (*chip, c), me).wait_recv()
                passed.append(copy(i, 4 + j, (*chip, c), sibling))
                passed[-1].start()
        for i in range(n):
            copy(i, 0, sibling, me).wait_recv()
            for j, chip in enumerate(chips):
                copy(i, 4 + j, (*chip, 1 - c), me).wait_recv()
        for cp in first + passed:
            cp.wait_send()
        for cp in mine:
            cp.wait()

    return _comm_call(
        body, name, list(blocks) + list(after), [jax.ShapeDtypeStruct((N_DEV,) + b.shape, b.dtype) for b in blocks],
        [pltpu.SemaphoreType.DMA((7 * n,)), pltpu.SemaphoreType.DMA((7 * n,)), pltpu.SemaphoreType.DMA((n,))],
        collective_id)


def _sibling_swap(name, by_owner, collective_id=None, after=()):
    n = len(by_owner)
    na = len(after)

    def body(*refs):
        x_refs, out_refs = refs[:n], refs[n + na:2 * n + na]
        send_sems, recv_sems = refs[2 * n + na:]
        x, y, c = lax.axis_index("x"), lax.axis_index("y"), lax.axis_index("c")
        if collective_id is not None:
            _handshake([(x, y, 1 - c)])
        copies = []
        for i in range(n):
            for k in range(4):
                copies.append(pltpu.make_async_remote_copy(
                    src_ref=x_refs[i].at[2 * k + 1 - c], dst_ref=out_refs[i].at[k],
                    send_sem=send_sems.at[4 * i + k], recv_sem=recv_sems.at[4 * i + k],
                    device_id=(x, y, 1 - c), device_id_type=MESH))
        for cp in copies:
            cp.start()
        for cp in copies:
            cp.wait()

    return _comm_call(
        body, name, list(by_owner) + list(after),
        [jax.ShapeDtypeStruct((4,) + b.shape[1:], b.dtype) for b in by_owner],
        [pltpu.SemaphoreType.DMA((4 * n,)), pltpu.SemaphoreType.DMA((4 * n,))], collective_id)


def _chip_exchange(name, sums, collective_id=None, after=()):
    n = len(sums)
    na = len(after)

    def body(*refs):
        x_refs, out_refs = refs[:n], refs[n + na:2 * n + na]
        send_sems, recv_sems, local_sems = refs[2 * n + na:]
        x, y, c = lax.axis_index("x"), lax.axis_index("y"), lax.axis_index("c")
        chips = [(1 - x, y), (x, 1 - y), (1 - x, 1 - y)]
        my_chip = 2 * x + y
        if collective_id is not None:
            _handshake([(cx, cy, c) for cx, cy in chips])
        mine = [pltpu.make_async_copy(x_refs[i].at[my_chip], out_refs[i].at[my_chip], local_sems.at[i])
                for i in range(n)]
        for cp in mine:
            cp.start()
        sends = []
        for i in range(n):
            for j, (cx, cy) in enumerate(chips):
                sends.append(pltpu.make_async_remote_copy(
                    src_ref=x_refs[i].at[2 * cx + cy], dst_ref=out_refs[i].at[my_chip],
                    send_sem=send_sems.at[3 * i + j], recv_sem=recv_sems.at[3 * i + j],
                    device_id=(cx, cy, c), device_id_type=MESH))
        for cp in sends:
            cp.start()
        for i in range(n):
            for j, (cx, cy) in enumerate(chips):
                pltpu.make_async_remote_copy(
                    src_ref=x_refs[i].at[my_chip], dst_ref=out_refs[i].at[2 * cx + cy],
                    send_sem=send_sems.at[3 * i + j], recv_sem=recv_sems.at[3 * i + j],
                    device_id=(cx, cy, c), device_id_type=MESH).wait_recv()
        for cp in sends:
            cp.wait_send()
        for cp in mine:
            cp.wait()

    return _comm_call(
        body, name, list(sums) + list(after), [jax.ShapeDtypeStruct(s.shape, s.dtype) for s in sums],
        [pltpu.SemaphoreType.DMA((3 * n,)), pltpu.SemaphoreType.DMA((3 * n,)), pltpu.SemaphoreType.DMA((n,))],
        collective_id)


def _cast_shards(shards):
    n = len(shards)

    def body(*refs):
        for i in range(n):
            refs[n + i][...] = refs[i][...].astype(BF16)

    vmem = pl.BlockSpec(memory_space=pltpu.VMEM)
    return pl.pallas_call(
        body, name="cast_shards", in_specs=[vmem] * n, out_specs=[vmem] * n,
        out_shape=[jax.ShapeDtypeStruct(s.shape, BF16) for s in shards],
        compiler_params=pltpu.CompilerParams(vmem_limit_bytes=VMEM_LIMIT_V7X),
    )(*shards)


BIG = ("w_in", "w_branch_a", "w_branch_b", "w_out", "w_ffn_gate", "w_ffn_up", "w_ffn_down")


def _local_step(x, target, gains, low, conv_w, wg8, reduce):
    g_mix, g_hg, g_ffn, g_fin = gains
    w_in = wg8["w_in"].reshape(N_IN, D_MODEL)
    wg = wg8["w_ffn_gate"].reshape(D_FF, D_MODEL)
    wu = wg8["w_ffn_up"].reshape(D_FF, D_MODEL)
    wa, wb = wg8["w_branch_a"], wg8["w_branch_b"]
    wo = wg8["w_out"].reshape(D_MODEL, D_MODEL)
    wd = wg8["w_ffn_down"].reshape(D_FF, D_MODEL)

    ht, hg, cv, gt = _fwd_in(x, g_mix, w_in)
    o, og, ogt, st = _hg_fwd(hg, low, g_hg)
    cvo, cvot = _conv_fwd(cv, conv_w)
    x1, mgt = _merge_fwd(og, cvo, gt, x, wa, wb, wo)
    h2t, gate, up, actt, x2 = _ffn_fwd(x1, g_ffn, wg, wu, wd)
    loss, d_gfin, dx2 = _final_fwd_bwd(x2, target, g_fin)

    dgate, dup, dx1, d_gffn = _ffn_bwd(dx2, x1, gate, up, g_ffn, wg, wu, wd)
    ffn = dict(
        w_ffn_down=_wgrad("wgrad_ffn_down", actt, dx2, 512).reshape(N_DEV, D_FF // N_DEV, D_MODEL),
        w_ffn_gate=_wgrad("wgrad_ffn_gate", h2t, dgate, 1408, transposed=True).reshape(N_DEV, D_FF // N_DEV, D_MODEL),
        w_ffn_up=_wgrad("wgrad_ffn_up", h2t, dup, 1408, transposed=True).reshape(N_DEV, D_FF // N_DEV, D_MODEL))
    sums_ffn, got_ffn = reduce.begin(ffn)
    dgt, dya, dyb, dog, dcvo = _merge_bwd(dx1, og, cvo, gt, wa, wb, wo)
    out = dict(
        w_out=_wgrad("wgrad_out", mgt, dx1, 512).reshape(N_DEV, D_MODEL // N_DEV, D_MODEL),
        w_branch_a=_shards_from_cols("split_w_branch_a", _wgrad("wgrad_branch_a", ogt, dya, 512)),
        w_branch_b=_shards_from_cols("split_w_branch_b", _wgrad("wgrad_branch_b", cvot, dyb, 512)))
    sums_out, got_out = reduce.begin(out, after=got_ffn[:1])
    parts_ffn = reduce.finish(ffn, sums_ffn, after=got_out[:1])
    dc, db, dxb, d_conv = _conv_bwd(dcvo, cv, conv_w)
    dq, df, di, dg, d_low, d_ghg = _hg_bwd(dog, hg, o, st, low, g_hg, after=list(sums_ffn) + list(sums_out))
    parts_out = reduce.finish(out, sums_out, after=[parts_ffn[0], dq])
    dparts = [dq, df, di, dg, dc, db, dxb, dgt]
    w_in_grad = dict(w_in=_wgrad_in(ht, dparts, after=parts_ffn[:1]).reshape(N_DEV, N_IN // N_DEV, D_MODEL))
    sums_in, _ = reduce.begin(w_in_grad, after=parts_out[:1])
    parts_in = reduce.finish(w_in_grad, sums_in)
    grad_x, d_gmix = _in_bwd(dparts, w_in, x, dx1, g_mix, after=list(parts_out[:1]) + list(sums_in))
    small = dict(norm_mix_g=d_gmix, norm_ffn_g=d_gffn, norm_final_g=d_gfin, lower_bounds=d_low, hg_norm_g=d_ghg,
                 conv_w=d_conv, loss=loss)
    return grad_x, small, parts_in


_SMALL_LAYOUT = (("norm_mix_g", 0, 8), ("norm_ffn_g", 8, 8), ("norm_final_g", 16, 8), ("lower_bounds", 24, 8),
                 ("hg_norm_g", 32, 1))
_LOSS_ROW = 40
_CONV_ROW = 48


def _pad_rows(a, rows):
    return jnp.pad(a, ((0, rows - a.shape[0]), (0, 0)))


def _pack_small(vals, conv_rows):
    parts = [_pad_rows(vals[name].reshape(rows, 128), 8) for name, _, rows in _SMALL_LAYOUT]
    loss = vals["loss"][:, :128] if "loss" in vals else jnp.zeros((1, 128), F32)
    parts.append(_pad_rows(loss, 8))
    parts.append(_pad_rows(conv_rows, SMALL_ROWS - _CONV_ROW))
    return jnp.concatenate(parts, axis=0)


def _conv_shard_rows(a):
    return jnp.pad(a, ((0, 5), (0, 64)))


def kernel(x, norm_mix_g, w_in, lower_bounds, hg_norm_g, conv_w, w_branch_a, w_branch_b, w_out, norm_ffn_g, w_ffn_gate, w_ffn_up, w_ffn_down, norm_final_g, loss_target, m_norm_mix_g, m_w_in, m_lower_bounds, m_hg_norm_g, m_conv_w, m_w_branch_a, m_w_branch_b, m_w_out, m_norm_ffn_g, m_w_ffn_gate, m_w_ffn_up, m_w_ffn_down, m_norm_final_g, v_norm_mix_g, v_w_in, v_lower_bounds, v_hg_norm_g, v_conv_w, v_w_branch_a, v_w_branch_b, v_w_out, v_norm_ffn_g, v_w_ffn_gate, v_w_ffn_up, v_w_ffn_down, v_norm_final_g):
    cx, cy, cc = lax.axis_index("x"), lax.axis_index("y"), lax.axis_index("c")
    my_dev = 4 * cx + 2 * cy + cc

    def tr(a):
        return a[0].T

    big = dict(w_in=tr(w_in), w_branch_a=w_branch_a[0], w_branch_b=w_branch_b[0], w_out=w_out[0],
               w_ffn_gate=tr(w_ffn_gate), w_ffn_up=tr(w_ffn_up), w_ffn_down=w_ffn_down[0])
    big_m = dict(w_in=tr(m_w_in), w_branch_a=m_w_branch_a[0], w_branch_b=m_w_branch_b[0], w_out=m_w_out[0],
                 w_ffn_gate=tr(m_w_ffn_gate), w_ffn_up=tr(m_w_ffn_up), w_ffn_down=m_w_ffn_down[0])
    big_v = dict(w_in=tr(v_w_in), w_branch_a=v_w_branch_a[0], w_branch_b=v_w_branch_b[0], w_out=v_w_out[0],
                 w_ffn_gate=tr(v_w_ffn_gate), w_ffn_up=tr(v_w_ffn_up), w_ffn_down=v_w_ffn_down[0])
    transposed = ("w_in", "w_ffn_gate", "w_ffn_up")

    shards = dict(zip(BIG, _cast_shards([big[n] for n in BIG])))
    first = _all_gather("gather_w_in", [shards["w_in"], _conv_shard_rows(conv_w[0])])
    later = _all_gather("gather_rest", [shards[n] for n in BIG[1:]], collective_id=1, after=first[1:])
    wg8 = dict(zip(BIG, [first[0]] + list(later)))
    conv_full = first[1][:, :3, :64].transpose(1, 0, 2).reshape(3, CONV_WIDTH)

    core = cc.reshape(1).astype(jnp.int32)
    outs = {}
    ids = iter(range(2, 16))

    class Reduce:
        @staticmethod
        def begin(grads, after=()):
            names = list(grads)
            by_owner = [grads[n] for n in names]
            got = _sibling_swap("sibling_swap_" + names[0], by_owner, collective_id=next(ids), after=after)
            return [_pair_sum("pair_sum_" + n, a, b, core) for n, a, b in zip(names, by_owner, got)], got

        @staticmethod
        def finish(grads, chip_sums, after=()):
            names = list(grads)
            parts = _chip_exchange("chip_exchange_" + names[0], chip_sums, collective_id=next(ids), after=after)
            for n, p in zip(names, parts):
                outs[n] = _adamw_sum("adamw_" + n, big[n], p, big_m[n], big_v[n])
            return parts

    gains = (norm_mix_g, hg_norm_g, norm_ffn_g, norm_final_g.reshape(1, D_MODEL))
    grad_x, small, last = _local_step(x[0], loss_target[0], gains, lower_bounds, conv_full, wg8, Reduce)

    small_all = _all_gather("gather_small", [_pack_small(small, small["conv_w"].reshape(12, 128))],
                            collective_id=next(ids), after=last[:1])
    ssum = _small_sum(small_all[0])
    conv_g_full = ssum[_CONV_ROW:_CONV_ROW + 12].reshape(3, CONV_WIDTH)
    conv_g = lax.dynamic_slice_in_dim(conv_g_full, my_dev * 64, 64, axis=1)
    loss = ssum[_LOSS_ROW, 0]
    g_rows = jnp.concatenate([ssum[:_CONV_ROW], _pad_rows(_conv_shard_rows(conv_g), SMALL_ROWS - _CONV_ROW)], axis=0)

    def pack_state(a):
        vals = dict(norm_mix_g=a[0], norm_ffn_g=a[1], norm_final_g=a[2], lower_bounds=a[3], hg_norm_g=a[4])
        return _pack_small(vals, _conv_shard_rows(a[5][0]))

    sw = pack_state((norm_mix_g, norm_ffn_g, norm_final_g, lower_bounds, hg_norm_g, conv_w))
    sm = pack_state((m_norm_mix_g, m_norm_ffn_g, m_norm_final_g, m_lower_bounds, m_hg_norm_g, m_conv_w))
    sv = pack_state((v_norm_mix_g, v_norm_ffn_g, v_norm_final_g, v_lower_bounds, v_hg_norm_g, v_conv_w))
    s_delta, s_m, s_v = _small_adamw(sw, g_rows, sm, sv)

    shapes = dict(norm_mix_g=(1, D_MODEL), norm_ffn_g=(1, D_MODEL), norm_final_g=(D_MODEL,),
                  lower_bounds=(2, HG_WIDTH), hg_norm_g=(1, HEAD_DIM))

    def unpack(buf, name):
        if name == "conv_w":
            return buf[_CONV_ROW:_CONV_ROW + 3, :64].reshape(1, 3, 64)
        for nm, off, rows in _SMALL_LAYOUT:
            if nm == name:
                return buf[off:off + rows].reshape(shapes[name])
        raise KeyError(name)

    order = ["norm_mix_g", "w_in", "lower_bounds", "hg_norm_g", "conv_w", "w_branch_a", "w_branch_b", "w_out",
             "norm_ffn_g", "w_ffn_gate", "w_ffn_up", "w_ffn_down", "norm_final_g"]
    result = [loss, grad_x[None]]
    for k, sbuf in enumerate((g_rows, s_delta, s_m, s_v)):
        for n in order:
            if n in outs:
                result.append((outs[n][k].T if n in transposed else outs[n][k])[None])
            else:
                result.append(unpack(sbuf, n))
    return tuple(result)
```

```python
import jax
import jax.numpy as jnp
from jax import lax
from jax.experimental import pallas as pl
from jax.experimental.pallas import tpu as pltpu
from jax.experimental.pallas import tpu_sc as plsc

F32 = jnp.float32
BF16 = jnp.bfloat16

D_MODEL = 1024
HG_WIDTH = 512
HEAD_DIM = 128
N_HEADS = 4
HEADS_PER_STEP = 4
HEAD_GROUPS = N_HEADS // HEADS_PER_STEP
CONV_WIDTH = 512
D_FF = 2816
CHUNK = 32
EPS = 1e-6
Q_SCALE = HEAD_DIM ** -0.5
N_DEV = 8

ADAM_LR = 0.001
ADAM_B1 = 0.9
ADAM_B2 = 0.999
ADAM_EPS = 1e-08
ADAM_WD = 0.01
ADAM_STEP = 10

VMEM_LIMIT_V7X = 56 * 1024 * 1024

SMALL_ROWS = 64


def _params(sem, vmem=VMEM_LIMIT_V7X):
    return pltpu.CompilerParams(dimension_semantics=sem, vmem_limit_bytes=vmem)


def _mm(a, b):
    return jnp.dot(a.astype(BF16), b.astype(BF16), preferred_element_type=F32)


def _mm_nt(a, b):
    return lax.dot_general(a.astype(BF16), b.astype(BF16), (((1,), (1,)), ((), ())), preferred_element_type=F32)


def _mm_tn(a, b):
    return lax.dot_general(a.astype(BF16), b.astype(BF16), (((0,), (0,)), ((), ())), preferred_element_type=F32)


def _sigmoid(x):
    return 1.0 / (1.0 + jnp.exp(-x))


def _resident(shape):
    nd = len(shape)
    return pl.BlockSpec(shape, lambda *_: (0,) * nd, pipeline_mode=pl.Buffered(1))


def _full(shape):
    nd = len(shape)
    return pl.BlockSpec(shape, lambda *_: (0,) * nd)


def _shard_cols(w_ref):
    return jnp.concatenate([w_ref[s] for s in range(N_DEV)], axis=1)


N_HG = 4 * HG_WIDTH
N_CV = 3 * CONV_WIDTH
N_GT = 2 * D_MODEL
N_IN = N_HG + N_CV + N_GT


def _col(tm, n):
    return pl.BlockSpec((n, tm), lambda i: (0, i))


def _fwd_in(x, g, w_in_t):
    T = x.shape[0]
    tm = min(512, T)

    def body(x_ref, g_ref, w_ref, ht_ref, hg_ref, cv_ref, gt_ref):
        xv = x_ref[...]
        r = lax.rsqrt(jnp.mean(xv * xv, axis=-1, keepdims=True) + EPS)
        hf = xv * r * g_ref[...]
        h = hf.astype(BF16)
        ht_ref[...] = hf.T.astype(BF16)
        hg_ref[...] = _mm_nt(h, w_ref[:N_HG, :])
        cv_ref[...] = _mm_nt(h, w_ref[N_HG:N_HG + N_CV, :])
        gt_ref[...] = _mm_nt(h, w_ref[N_HG + N_CV:, :])

    row = lambda n: pl.BlockSpec((tm, n), lambda i: (i, 0))
    return pl.pallas_call(
        body, name="fwd_in", grid=(T // tm,),
        in_specs=[row(D_MODEL), _full((1, D_MODEL)), _resident(w_in_t.shape)],
        out_specs=[_col(tm, D_MODEL), row(N_HG), row(N_CV), row(N_GT)],
        out_shape=[jax.ShapeDtypeStruct((D_MODEL, T), BF16), jax.ShapeDtypeStruct((T, N_HG), F32),
                   jax.ShapeDtypeStruct((T, N_CV), F32), jax.ShapeDtypeStruct((T, N_GT), F32)],
        compiler_params=_params(("parallel",)),
    )(x, g, w_in_t)


def _chunk_pos(shape):
    return lax.broadcasted_iota(jnp.int32, shape, 0) & (CHUNK - 1)


def _chunk_cumsum(x, pos):
    s = 1
    while s < CHUNK:
        x = x + jnp.where(pos >= s, pltpu.roll(x, s, 0), 0.0)
        s *= 2
    return x


def _chunk_rev_cumsum(x, pos):
    n = x.shape[0]
    s = 1
    while s < CHUNK:
        x = x + jnp.where(pos + s < CHUNK, pltpu.roll(x, n - s, 0), 0.0)
        s *= 2
    return x


def _chunk_bcast(x3, row, tb):
    return jnp.broadcast_to(x3[:, row:row + 1, :], x3.shape).reshape(tb, x3.shape[-1])


def _lower_bound(low_ref):
    l0 = low_ref[0:1, :]
    l1 = low_ref[1:2, :]
    m = jnp.maximum(l0, l1)
    e0 = jnp.exp(l0 - m)
    e1 = jnp.exp(l1 - m)
    return e0 / (e0 + e1), e1 / (e0 + e1)


def _hg_gates(qr, fr, lb, pos, tb):
    sq = _sigmoid(qr)
    q = qr * sq * Q_SCALE
    sg = _sigmoid(fr)
    f = lb + (1.0 - lb) * sg
    k = 1.0 - f
    b = _chunk_cumsum(jnp.log(f), pos)
    b3 = b.reshape(tb // CHUNK, CHUNK, HEAD_DIM)
    anc = _chunk_bcast(b3, CHUNK // 2 - 1, tb)
    blb = _chunk_bcast(b3, CHUNK - 1, tb)
    e_qa = jnp.exp(b - anc)
    e_ka = jnp.exp(anc - b)
    e_b = jnp.exp(b)
    e_ko = jnp.exp(blb - b)
    dec = jnp.exp(blb)
    return sq, q, sg, f, k, e_qa, e_ka, e_b, e_ko, dec


def _intra_mask(sb):
    r = lax.broadcasted_iota(jnp.int32, (sb, sb), 0)
    c = lax.broadcasted_iota(jnp.int32, (sb, sb), 1)
    return ((r // CHUNK) == (c // CHUNK)) & (c <= r)


def _hg_fwd(hg, low, gn):
    T = hg.shape[0]
    tb = min(512, T)
    sb = min(256, tb)
    nb = T // tb
    nc = tb // CHUNK
    wid = HEADS_PER_STEP * HEAD_DIM

    def body(q_ref, f_ref, i_ref, g_ref, low_ref, gn_ref, o_ref, og_ref, ogt_ref, st_ref, s_scr):
        t = pl.program_id(1)

        @pl.when(t == 0)
        def _():
            s_scr[...] = jnp.zeros_like(s_scr)

        pos = _chunk_pos((tb, HEAD_DIM))
        mask = _intra_mask(sb)
        lanes = [slice(hh * HEAD_DIM, (hh + 1) * HEAD_DIM) for hh in range(HEADS_PER_STEP)]
        qi, ko, vb, dec, st = [], [], [], [], []
        for hh, ln in enumerate(lanes):
            lb, _ = _lower_bound(low_ref.at[:, ln])
            _, q, _, _, k, e_qa, e_ka, e_b, e_ko, dec_h = _hg_gates(q_ref[:, ln], f_ref[:, ln], lb, pos, tb)
            qh = (q * e_qa).astype(BF16)
            kh = (k * e_ka).astype(BF16)
            qi.append((q * e_b).astype(BF16))
            ko.append((k * e_ko).astype(BF16))
            vb.append(i_ref[:, ln].astype(BF16))
            dec.append(dec_h)
            st.append(s_scr[hh])
            for s in range(tb // sb):
                sl = slice(s * sb, (s + 1) * sb)
                p = jnp.where(mask, _mm_nt(qh[sl], kh[sl]), 0.0)
                o_ref[sl, ln] = _mm(p, vb[hh][sl])
        for c in range(nc):
            sl = slice(c * CHUNK, (c + 1) * CHUNK)
            for hh, ln in enumerate(lanes):
                st_ref[hh, c] = st[hh]
                o_ref[sl, ln] = o_ref[sl, ln] + _mm_nt(qi[hh][sl], st[hh])
                st[hh] = dec[hh][c * CHUNK:c * CHUNK + 1, :] * st[hh] + _mm_tn(vb[hh][sl], ko[hh][sl])
        for hh, ln in enumerate(lanes):
            s_scr[hh] = st[hh]
            o = o_ref[:, ln]
            r = lax.rsqrt(jnp.mean(o * o, axis=-1, keepdims=True) + EPS)
            gr = g_ref[:, ln]
            og = (o * r * gn_ref[...]) * (gr * _sigmoid(gr))
            og_ref[:, ln] = og.astype(BF16)
            ogt_ref[ln, :] = og.T.astype(BF16)

    col = lambda p: pl.BlockSpec((tb, wid), lambda h, t: (t, p * HEAD_GROUPS + h))
    hcol = pl.BlockSpec((tb, wid), lambda h, t: (t, h))
    return pl.pallas_call(
        body, name="hg_fwd", grid=(HEAD_GROUPS, nb),
        in_specs=[col(0), col(1), col(2), col(3), pl.BlockSpec((2, wid), lambda h, t: (0, h)),
                  pl.BlockSpec((1, HEAD_DIM), lambda h, t: (0, 0))],
        out_specs=[hcol, hcol, pl.BlockSpec((wid, tb), lambda h, t: (h, t)),
                   pl.BlockSpec((HEADS_PER_STEP, nc, HEAD_DIM, HEAD_DIM), lambda h, t: (h, t, 0, 0))],
        out_shape=[jax.ShapeDtypeStruct((T, HG_WIDTH), F32), jax.ShapeDtypeStruct((T, HG_WIDTH), BF16),
                   jax.ShapeDtypeStruct((HG_WIDTH, T), BF16),
                   jax.ShapeDtypeStruct((N_HEADS, T // CHUNK, HEAD_DIM, HEAD_DIM), F32)],
        scratch_shapes=[pltpu.VMEM((HEADS_PER_STEP, HEAD_DIM, HEAD_DIM), F32)],
        compiler_params=_params(("parallel", "arbitrary")),
    )(hg, hg, hg, hg, low, gn)


def _conv_fwd(cv, conv_w):
    T = cv.shape[0]
    nj = CONV_WIDTH // 128

    def body(c_ref, b_ref, x_ref, w_ref, o_ref, ot_ref):
        row = lax.broadcasted_iota(jnp.int32, (T, 128), 0)
        u = c_ref[...] * x_ref[...]
        u1 = jnp.where(row >= 1, pltpu.roll(u, 1, 0), 0.0)
        u2 = jnp.where(row >= 2, pltpu.roll(u, 2, 0), 0.0)
        y = w_ref[0:1, :] * u2 + w_ref[1:2, :] * u1 + w_ref[2:3, :] * u
        out = b_ref[...] * y
        o_ref[...] = out.astype(BF16)
        ot_ref[...] = out.T.astype(BF16)

    col = lambda p: pl.BlockSpec((T, 128), lambda j: (0, p * nj + j))
    return pl.pallas_call(
        body, name="conv_fwd", grid=(nj,),
        in_specs=[col(0), col(1), col(2), pl.BlockSpec((3, 128), lambda j: (0, j))],
        out_specs=[pl.BlockSpec((T, 128), lambda j: (0, j)), pl.BlockSpec((128, T), lambda j: (j, 0))],
        out_shape=[jax.ShapeDtypeStruct((T, CONV_WIDTH), BF16), jax.ShapeDtypeStruct((CONV_WIDTH, T), BF16)],
        compiler_params=_params(("parallel",)),
    )(cv, cv, cv, conv_w)


def _merge_fwd(og, cvo, gt, x, wa, wb, wo):
    T = x.shape[0]
    tm = min(512, T)

    def body(og_ref, cvo_ref, gt_ref, x_ref, wa_ref, wb_ref, wo_ref, x1_ref, mgt_ref):
        ya = jnp.dot(og_ref[...], _shard_cols(wa_ref), preferred_element_type=F32)
        yb = jnp.dot(cvo_ref[...], _shard_cols(wb_ref), preferred_element_type=F32)
        m = _sigmoid(gt_ref[:, :D_MODEL]) * ya + _sigmoid(gt_ref[:, D_MODEL:]) * yb
        mgt_ref[...] = m.T.astype(BF16)
        x1_ref[...] = x_ref[...] + jnp.dot(m.astype(BF16), wo_ref[...], preferred_element_type=F32)

    row = lambda n: pl.BlockSpec((tm, n), lambda i: (i, 0))
    return pl.pallas_call(
        body, name="merge_fwd", grid=(T // tm,),
        in_specs=[row(HG_WIDTH), row(CONV_WIDTH), row(2 * D_MODEL), row(D_MODEL),
                  _resident(wa.shape), _resident(wb.shape), _resident(wo.shape)],
        out_specs=[row(D_MODEL), _col(tm, D_MODEL)],
        out_shape=[jax.ShapeDtypeStruct((T, D_MODEL), F32), jax.ShapeDtypeStruct((D_MODEL, T), BF16)],
        compiler_params=_params(("parallel",)),
    )(og, cvo, gt, x, wa, wb, wo)


def _ffn_fwd(x1, g, wg, wu, wd):
    T = x1.shape[0]
    tm = min(256, T)

    def body(x_ref, g_ref, wg_ref, wu_ref, wd_ref, ht_ref, gate_ref, up_ref, actt_ref, x2_ref):
        xv = x_ref[...]
        r = lax.rsqrt(jnp.mean(xv * xv, axis=-1, keepdims=True) + EPS)
        hf = xv * r * g_ref[...]
        h = hf.astype(BF16)
        ht_ref[...] = hf.T.astype(BF16)
        gate = _mm_nt(h, wg_ref[...])
        up = _mm_nt(h, wu_ref[...])
        gate_ref[...] = gate
        up_ref[...] = up
        act = gate * _sigmoid(gate) * up
        actt_ref[...] = act.T.astype(BF16)
        x2_ref[...] = xv + jnp.dot(act.astype(BF16), wd_ref[...], preferred_element_type=F32)

    row = lambda n: pl.BlockSpec((tm, n), lambda i: (i, 0))
    return pl.pallas_call(
        body, name="ffn_fwd", grid=(T // tm,),
        in_specs=[row(D_MODEL), _full((1, D_MODEL)), _resident(wg.shape), _resident(wu.shape), _resident(wd.shape)],
        out_specs=[_col(tm, D_MODEL), row(D_FF), row(D_FF), _col(tm, D_FF), row(D_MODEL)],
        out_shape=[jax.ShapeDtypeStruct((D_MODEL, T), BF16), jax.ShapeDtypeStruct((T, D_FF), F32),
                   jax.ShapeDtypeStruct((T, D_FF), F32), jax.ShapeDtypeStruct((D_FF, T), BF16),
                   jax.ShapeDtypeStruct((T, D_MODEL), F32)],
        compiler_params=_params(("parallel",)),
    )(x1, g, wg, wu, wd)


def _final_fwd_bwd(x2, target, g):
    T = x2.shape[0]
    tm = min(512, T)

    def body(x_ref, t_ref, g_ref, loss_ref, dg_ref, dx_ref):
        @pl.when(pl.program_id(0) == 0)
        def _():
            loss_ref[...] = jnp.zeros_like(loss_ref)
            dg_ref[...] = jnp.zeros_like(dg_ref)

        xv = x_ref[...]
        gv = g_ref[...]
        r = lax.rsqrt(jnp.mean(xv * xv, axis=-1, keepdims=True) + EPS)
        xh = xv * r
        err = xh * gv - t_ref[...]
        loss_ref[...] += 0.5 * jnp.sum(jnp.mean(err * err, axis=-1, keepdims=True), axis=0, keepdims=True)
        dy = err * (1.0 / D_MODEL)
        dg_ref[...] += jnp.sum(dy * xh, axis=0, keepdims=True)
        w = dy * gv
        dx_ref[...] = r * (w - xh * jnp.mean(w * xh, axis=-1, keepdims=True))

    row = pl.BlockSpec((tm, D_MODEL), lambda i: (i, 0))
    return pl.pallas_call(
        body, name="final_fwd_bwd", grid=(T // tm,),
        in_specs=[row, row, _full((1, D_MODEL))],
        out_specs=[_full((1, 128)), _full((1, D_MODEL)), row],
        out_shape=[jax.ShapeDtypeStruct((1, 128), F32), jax.ShapeDtypeStruct((1, D_MODEL), F32),
                   jax.ShapeDtypeStruct((T, D_MODEL), F32)],
        compiler_params=_params(("arbitrary",)),
    )(x2, target, g)


def _ffn_bwd(dx2, x1, gate, up, g, wg, wu, wd):
    T = x1.shape[0]
    tm = min(256, T)

    def body(dx2_ref, x_ref, gate_ref, up_ref, g_ref, wg_ref, wu_ref, wd_ref, dgate_ref, dup_ref, dx1_ref, dgn_ref):
        @pl.when(pl.program_id(0) == 0)
        def _():
            dgn_ref[...] = jnp.zeros_like(dgn_ref)

        dx2 = dx2_ref[...]
        dact = _mm_nt(dx2, wd_ref[...])
        gate = gate_ref[...]
        s = _sigmoid(gate)
        dgate = (dact * up_ref[...] * (s * (1.0 + gate * (1.0 - s)))).astype(BF16)
        dup = (dact * (gate * s)).astype(BF16)
        dgate_ref[...] = dgate
        dup_ref[...] = dup
        dh = _mm(dgate, wg_ref[...]) + _mm(dup, wu_ref[...])
        xv = x_ref[...]
        r = lax.rsqrt(jnp.mean(xv * xv, axis=-1, keepdims=True) + EPS)
        xh = xv * r
        dgn_ref[...] += jnp.sum(dh * xh, axis=0, keepdims=True)
        w = dh * g_ref[...]
        dx1_ref[...] = dx2 + r * (w - xh * jnp.mean(w * xh, axis=-1, keepdims=True))

    row = lambda n: pl.BlockSpec((tm, n), lambda i: (i, 0))
    return pl.pallas_call(
        body, name="ffn_bwd", grid=(T // tm,),
        in_specs=[row(D_MODEL), row(D_MODEL), row(D_FF), row(D_FF), _full((1, D_MODEL)),
                  _resident(wg.shape), _resident(wu.shape), _resident(wd.shape)],
        out_specs=[row(D_FF), row(D_FF), row(D_MODEL), _full((1, D_MODEL))],
        out_shape=[jax.ShapeDtypeStruct((T, D_FF), BF16), jax.ShapeDtypeStruct((T, D_FF), BF16),
                   jax.ShapeDtypeStruct((T, D_MODEL), F32), jax.ShapeDtypeStruct((1, D_MODEL), F32)],
        compiler_params=_params(("arbitrary",)),
    )(dx2, x1, gate, up, g, wg, wu, wd)


def _merge_bwd(dx1, og, cvo, gt, wa, wb, wo):
    T = dx1.shape[0]
    tm = min(512, T)

    def body(dx_ref, og_ref, cvo_ref, gt_ref, wa_ref, wb_ref, wo_ref, dgt_ref, dya_ref, dyb_ref, dog_ref, dcvo_ref):
        dm = _mm_nt(dx_ref[...], wo_ref[...])
        wa = _shard_cols(wa_ref)
        wb = _shard_cols(wb_ref)
        ya = jnp.dot(og_ref[...], wa, preferred_element_type=F32)
        yb = jnp.dot(cvo_ref[...], wb, preferred_element_type=F32)
        sa = _sigmoid(gt_ref[:, :D_MODEL])
        sb = _sigmoid(gt_ref[:, D_MODEL:])
        dgt_ref[:, :D_MODEL] = (dm * ya * (sa * (1.0 - sa))).astype(BF16)
        dgt_ref[:, D_MODEL:] = (dm * yb * (sb * (1.0 - sb))).astype(BF16)
        dya = (dm * sa).astype(BF16)
        dyb = (dm * sb).astype(BF16)
        dya_ref[...] = dya
        dyb_ref[...] = dyb
        dog_ref[...] = _mm_nt(dya, wa)
        dcvo_ref[...] = _mm_nt(dyb, wb)

    row = lambda n: pl.BlockSpec((tm, n), lambda i: (i, 0))
    return pl.pallas_call(
        body, name="merge_bwd", grid=(T // tm,),
        in_specs=[row(D_MODEL), row(HG_WIDTH), row(CONV_WIDTH), row(2 * D_MODEL),
                  _resident(wa.shape), _resident(wb.shape), _resident(wo.shape)],
        out_specs=[row(2 * D_MODEL), row(D_MODEL), row(D_MODEL), row(HG_WIDTH), row(CONV_WIDTH)],
        out_shape=[jax.ShapeDtypeStruct((T, 2 * D_MODEL), BF16), jax.ShapeDtypeStruct((T, D_MODEL), BF16),
                   jax.ShapeDtypeStruct((T, D_MODEL), BF16), jax.ShapeDtypeStruct((T, HG_WIDTH), F32),
                   jax.ShapeDtypeStruct((T, CONV_WIDTH), F32)],
        compiler_params=_params(("parallel",)),
    )(dx1, og, cvo, gt, wa, wb, wo)


def _conv_bwd(dcvo, cv, conv_w, after=()):
    T = cv.shape[0]
    nj = CONV_WIDTH // 128

    def body(do_ref, c_ref, b_ref, x_ref, w_ref, dc_ref, db_ref, dx_ref, dw_ref):
        row = lax.broadcasted_iota(jnp.int32, (T, 128), 0)
        c = c_ref[...]
        xb = x_ref[...]
        do = do_ref[...]
        u = c * xb
        u1 = jnp.where(row >= 1, pltpu.roll(u, 1, 0), 0.0)
        u2 = jnp.where(row >= 2, pltpu.roll(u, 2, 0), 0.0)
        w0, w1, w2 = w_ref[0:1, :], w_ref[1:2, :], w_ref[2:3, :]
        y = w0 * u2 + w1 * u1 + w2 * u
        db_ref[...] = (do * y).astype(BF16)
        dy = do * b_ref[...]
        dw_ref[0:1, :] = jnp.sum(dy * u2, axis=0, keepdims=True)
        dw_ref[1:2, :] = jnp.sum(dy * u1, axis=0, keepdims=True)
        dw_ref[2:3, :] = jnp.sum(dy * u, axis=0, keepdims=True)
        dy1 = jnp.where(row < T - 1, pltpu.roll(dy, T - 1, 0), 0.0)
        dy2 = jnp.where(row < T - 2, pltpu.roll(dy, T - 2, 0), 0.0)
        du = w2 * dy + w1 * dy1 + w0 * dy2
        dc_ref[...] = (du * xb).astype(BF16)
        dx_ref[...] = (du * c).astype(BF16)

    col = lambda p: pl.BlockSpec((T, 128), lambda j: (0, p * nj + j))
    one = pl.BlockSpec((T, 128), lambda j: (0, j))
    wspec = pl.BlockSpec((3, 128), lambda j: (0, j))
    out = jax.ShapeDtypeStruct((T, CONV_WIDTH), BF16)
    return pl.pallas_call(
        _drop_operands(body, 5, len(after)), name="conv_bwd", grid=(nj,),
        in_specs=[one, col(0), col(1), col(2), wspec] + [HBM_SPEC] * len(after),
        out_specs=[one, one, one, wspec],
        out_shape=[out, out, out, jax.ShapeDtypeStruct((3, CONV_WIDTH), F32)],
        compiler_params=_params(("parallel",)),
    )(dcvo, cv, cv, cv, conv_w, *after)


def _drop_operands(body, first, count):
    def wrapped(*refs):
        return body(*refs[:first], *refs[first + count:])
    return wrapped


def _hg_bwd(dog, hg, o, st, low, gn, after=()):
    T = hg.shape[0]
    tb = min(512, T)
    sb = min(256, tb)
    nb = T // tb
    nc = tb // CHUNK
    wid = HEADS_PER_STEP * HEAD_DIM

    def body(q_ref, f_ref, i_ref, g_ref, low_ref, gn_ref, o_ref, dog_ref, st_ref,
             dq_ref, df_ref, di_ref, dg_ref, dlow_ref, dgn_ref,
             ds_scr, dqi_scr, dko_scr, dv_scr, dd_scr, dqh_scr, dkh_scr):
        h = pl.program_id(0)
        t = pl.program_id(1)

        @pl.when(t == 0)
        def _():
            ds_scr[...] = jnp.zeros_like(ds_scr)
            dlow_ref[...] = jnp.zeros_like(dlow_ref)

        @pl.when((t == 0) & (h == 0))
        def _():
            dgn_ref[...] = jnp.zeros_like(dgn_ref)

        pos = _chunk_pos((tb, HEAD_DIM))
        mask = _intra_mask(sb)
        gnv = gn_ref[...]
        lanes = [slice(hh * HEAD_DIM, (hh + 1) * HEAD_DIM) for hh in range(HEADS_PER_STEP)]
        heads = []
        for hh, ln in enumerate(lanes):
            lb, lb1 = _lower_bound(low_ref.at[:, ln])
            qr = q_ref[:, ln]
            sq, q, sg, f, k, e_qa, e_ka, e_b, e_ko, dec = _hg_gates(qr, f_ref[:, ln], lb, pos, tb)

            gr = g_ref[:, ln]
            o = o_ref[:, ln]
            dog_v = dog_ref[:, ln]
            sgr = _sigmoid(gr)
            r = lax.rsqrt(jnp.mean(o * o, axis=-1, keepdims=True) + EPS)
            oh = o * r
            dg_ref[:, ln] = (dog_v * (oh * gnv) * (sgr * (1.0 + gr * (1.0 - sgr)))).astype(BF16)
            don = dog_v * (gr * sgr)
            dgn_ref[...] += jnp.sum(don * oh, axis=0, keepdims=True)
            w = don * gnv
            do = (r * (w - oh * jnp.mean(w * oh, axis=-1, keepdims=True))).astype(BF16)

            qh = (q * e_qa).astype(BF16)
            kh = (k * e_ka).astype(BF16)
            qi = (q * e_b).astype(BF16)
            ko = (k * e_ko).astype(BF16)
            vb = i_ref[:, ln].astype(BF16)

            for s in range(tb // sb):
                sl = slice(s * sb, (s + 1) * sb)
                p = jnp.where(mask, _mm_nt(qh[sl], kh[sl]), 0.0).astype(BF16)
                dp = jnp.where(mask, _mm_nt(do[sl], vb[sl]), 0.0).astype(BF16)
                dv_scr[sl, ln] = _mm_tn(p, do[sl])
                dqh_scr[sl, ln] = _mm(dp, kh[sl])
                dkh_scr[sl, ln] = _mm_tn(dp, qh[sl])
            heads.append(dict(lb=lb, lb1=lb1, qr=qr, sq=sq, q=q, sg=sg, f=f, k=k, e_qa=e_qa, e_ka=e_ka, e_b=e_b,
                              e_ko=e_ko, dec=dec, do=do, qi=qi, ko=ko, vb=vb, ds=ds_scr[hh]))

        for c in reversed(range(nc)):
            sl = slice(c * CHUNK, (c + 1) * CHUNK)
            for hh, ln in enumerate(lanes):
                hd = heads[hh]
                ds = hd["ds"]
                st_c = st_ref[hh, c]
                dqi_scr[sl, ln] = _mm(hd["do"][sl], st_c)
                dko_scr[sl, ln] = _mm(hd["vb"][sl], ds)
                dv_scr[sl, ln] = dv_scr[sl, ln] + _mm_nt(hd["ko"][sl], ds)
                dd_scr[sl, ln] = jnp.broadcast_to(jnp.sum(ds * st_c, axis=0, keepdims=True), (CHUNK, HEAD_DIM))
                hd["ds"] = hd["dec"][c * CHUNK:c * CHUNK + 1, :] * ds + _mm_tn(hd["do"][sl], hd["qi"][sl])

        for hh, ln in enumerate(lanes):
            hd = heads[hh]
            ds_scr[hh] = hd["ds"]
            q, k, lb = hd["q"], hd["k"], hd["lb"]
            dko_e = dko_scr[:, ln] * hd["e_ko"]
            dq = dqh_scr[:, ln] * hd["e_qa"] + dqi_scr[:, ln] * hd["e_b"]
            dk = dkh_scr[:, ln] * hd["e_ka"] + dko_e
            kd3 = (k * dko_e).reshape(nc, CHUNK, HEAD_DIM)
            last = jnp.broadcast_to(jnp.sum(kd3, axis=1, keepdims=True), kd3.shape).reshape(tb, HEAD_DIM)
            db = q * dq - k * dk + jnp.where(pos == CHUNK - 1, hd["dec"] * dd_scr[:, ln] + last, 0.0)
            dlg = _chunk_rev_cumsum(db, pos)
            dfv = dlg / hd["f"] - dk
            s_low = jnp.sum(dfv * (1.0 - hd["sg"]), axis=0, keepdims=True)
            dlow_ref[0:1, ln] += s_low * lb * (1.0 - lb)
            dlow_ref[1:2, ln] += -s_low * lb * hd["lb1"]
            df_ref[:, ln] = (dfv * (1.0 - lb) * hd["sg"] * (1.0 - hd["sg"])).astype(BF16)
            dq_ref[:, ln] = (dq * Q_SCALE * (hd["sq"] * (1.0 + hd["qr"] * (1.0 - hd["sq"])))).astype(BF16)
            di_ref[:, ln] = dv_scr[:, ln].astype(BF16)

    rt = lambda t: nb - 1 - t
    col = lambda p: pl.BlockSpec((tb, wid), lambda h, t: (rt(t), p * HEAD_GROUPS + h))
    hcol = pl.BlockSpec((tb, wid), lambda h, t: (rt(t), h))
    piece = jax.ShapeDtypeStruct((T, HG_WIDTH), BF16)
    tile = pltpu.VMEM((tb, wid), F32)
    return pl.pallas_call(
        _drop_operands(body, 9, len(after)), name="hg_bwd", grid=(HEAD_GROUPS, nb),
        in_specs=[col(0), col(1), col(2), col(3), pl.BlockSpec((2, wid), lambda h, t: (0, h)),
                  pl.BlockSpec((1, HEAD_DIM), lambda h, t: (0, 0)), hcol, hcol,
                  pl.BlockSpec((HEADS_PER_STEP, nc, HEAD_DIM, HEAD_DIM), lambda h, t: (h, rt(t), 0, 0))]
                 + [HBM_SPEC] * len(after),
        out_specs=[hcol, hcol, hcol, hcol, pl.BlockSpec((2, wid), lambda h, t: (0, h)),
                   pl.BlockSpec((1, HEAD_DIM), lambda h, t: (0, 0))],
        out_shape=[piece, piece, piece, piece, jax.ShapeDtypeStruct((2, HG_WIDTH), F32),
                   jax.ShapeDtypeStruct((1, HEAD_DIM), F32)],
        scratch_shapes=[pltpu.VMEM((HEADS_PER_STEP, HEAD_DIM, HEAD_DIM), F32), tile, tile, tile, tile, tile, tile],
        compiler_params=_params(("arbitrary", "arbitrary")),
    )(hg, hg, hg, hg, low, gn, o, dog, st, *after)


def _in_bwd(dparts, w_in, x, dx1, g, after=()):
    T = x.shape[0]
    tm = min(512, T)
    widths = [p.shape[1] for p in dparts]
    offs = [sum(widths[:i]) for i in range(len(widths))]
    n = len(dparts)

    def body(*refs):
        d_refs = refs[:n]
        w_ref, x_ref, dx1_ref, g_ref, dx_ref, dgn_ref = refs[n:]

        @pl.when(pl.program_id(0) == 0)
        def _():
            dgn_ref[...] = jnp.zeros_like(dgn_ref)

        dh = None
        for d_ref, off, wd in zip(d_refs, offs, widths):
            part = _mm(d_ref[...], w_ref[off:off + wd, :])
            dh = part if dh is None else dh + part
        xv = x_ref[...]
        r = lax.rsqrt(jnp.mean(xv * xv, axis=-1, keepdims=True) + EPS)
        xh = xv * r
        dgn_ref[...] += jnp.sum(dh * xh, axis=0, keepdims=True)
        w = dh * g_ref[...]
        dx_ref[...] = dx1_ref[...] + r * (w - xh * jnp.mean(w * xh, axis=-1, keepdims=True))

    row = lambda m: pl.BlockSpec((tm, m), lambda i: (i, 0))
    return pl.pallas_call(
        _drop_operands(body, n + 4, len(after)), name="in_bwd", grid=(T // tm,),
        in_specs=[row(wd) for wd in widths] + [_resident(w_in.shape), row(D_MODEL), row(D_MODEL), _full((1, D_MODEL))]
                 + [HBM_SPEC] * len(after),
        out_specs=[row(D_MODEL), _full((1, D_MODEL))],
        out_shape=[jax.ShapeDtypeStruct((T, D_MODEL), F32), jax.ShapeDtypeStruct((1, D_MODEL), F32)],
        compiler_params=_params(("arbitrary",)),
    )(*dparts, w_in, x, dx1, g, *after)


def _wgrad(name, at, b, tn, transposed=False):
    M, T = at.shape
    N = b.shape[1]
    tk = min(2048, T)
    nk = T // tk

    def body(a_ref, b_ref, o_ref, acc):
        k = pl.program_id(1)
        part = _mm(a_ref[...], b_ref[...])

        @pl.when(k == 0)
        def _():
            acc[...] = part

        @pl.when(k != 0)
        def _():
            acc[...] += part

        @pl.when(k == nk - 1)
        def _():
            o_ref[...] = (acc[...].T if transposed else acc[...]).astype(BF16)

    if transposed:
        out_spec, out_shape = pl.BlockSpec((tn, M), lambda j, k: (j, 0)), (N, M)
    else:
        out_spec, out_shape = pl.BlockSpec((M, tn), lambda j, k: (0, j)), (M, N)
    return pl.pallas_call(
        body, name=name, grid=(N // tn, nk),
        in_specs=[pl.BlockSpec((M, tk), lambda j, k: (0, k)), pl.BlockSpec((tk, tn), lambda j, k: (k, j))],
        out_specs=out_spec, out_shape=jax.ShapeDtypeStruct(out_shape, BF16),
        scratch_shapes=[pltpu.VMEM((M, tn), F32)],
        compiler_params=_params(("parallel", "arbitrary")),
    )(at, b)


def _wgrad_in(ht, dparts, after=()):
    M, T = ht.shape
    tn = 512
    tk = min(2048, T)
    nk = T // tk
    nblk = [p.shape[1] // tn for p in dparts]
    start = [sum(nblk[:i]) for i in range(len(nblk))]
    n = len(dparts)

    def body(a_ref, *refs):
        d_refs, o_ref, acc = refs[:n], refs[n], refs[n + 1]
        j = pl.program_id(0)
        k = pl.program_id(1)

        @pl.when(k == 0)
        def _():
            acc[...] = jnp.zeros_like(acc)

        for d_ref, s, nb in zip(d_refs, start, nblk):
            @pl.when((j >= s) & (j < s + nb))
            def _():
                acc[...] += _mm(a_ref[...], d_ref[...])

        @pl.when(k == nk - 1)
        def _():
            o_ref[...] = acc[...].T.astype(BF16)

    def piece_spec(s, nb):
        def index(j, k):
            inside = (j >= s) & (j < s + nb)
            return jnp.where(inside, k, 0), jnp.clip(j - s, 0, nb - 1)
        return pl.BlockSpec((tk, tn), index)

    return pl.pallas_call(
        _drop_operands(body, 1 + n, len(after)), name="wgrad_in", grid=(sum(nblk), nk),
        in_specs=[pl.BlockSpec((M, tk), lambda j, k: (0, k))] + [piece_spec(s, nb) for s, nb in zip(start, nblk)]
                 + [HBM_SPEC] * len(after),
        out_specs=pl.BlockSpec((tn, M), lambda j, k: (j, 0)),
        out_shape=jax.ShapeDtypeStruct((sum(nblk) * tn, M), BF16),
        scratch_shapes=[pltpu.VMEM((M, tn), F32)],
        compiler_params=_params(("parallel", "arbitrary")),
    )(ht, *dparts, *after)


def _adamw_math(w, g, m, v):
    m = ADAM_B1 * m + (1.0 - ADAM_B1) * g
    v = ADAM_B2 * v + (1.0 - ADAM_B2) * (g * g)
    m_hat = m / (1.0 - ADAM_B1 ** ADAM_STEP)
    v_hat = v / (1.0 - ADAM_B2 ** ADAM_STEP)
    delta = -ADAM_LR * (m_hat / (jnp.sqrt(v_hat) + ADAM_EPS) + ADAM_WD * w)
    return delta, m, v


def _adamw_sum(name, w, parts, m, v):
    R, C = w.shape
    tr = _row_tile(R)

    def body(w_ref, p_ref, m_ref, v_ref, g_out, d_out, m_out, v_out):
        g = p_ref[0].astype(F32)
        for k in range(1, 4):
            g = g + p_ref[k].astype(F32)
        g_out[...] = g
        d_out[...], m_out[...], v_out[...] = _adamw_math(w_ref[...], g, m_ref[...], v_ref[...])

    blk = pl.BlockSpec((tr, C), lambda i: (i, 0))
    out = jax.ShapeDtypeStruct((R, C), F32)
    return pl.pallas_call(
        body, name=name, grid=(R // tr,),
        in_specs=[blk, pl.BlockSpec((4, tr, C), lambda i: (0, i, 0)), blk, blk],
        out_specs=[blk, blk, blk, blk], out_shape=[out, out, out, out],
        compiler_params=_params(("parallel",)),
    )(w, parts, m, v)


def _small_sum(gathered):
    R = gathered.shape[1]

    def body(p_ref, o_ref):
        g = p_ref[0]
        for k in range(1, N_DEV):
            g = g + p_ref[k]
        o_ref[...] = g

    return pl.pallas_call(
        body, name="small_sum", in_specs=[_full(gathered.shape)], out_specs=_full((R, 128)), grid=(1,),
        out_shape=jax.ShapeDtypeStruct((R, 128), F32),
    )(gathered)


def _small_adamw(w, g, m, v):
    def body(w_ref, g_ref, m_ref, v_ref, d_out, m_out, v_out):
        d_out[...], m_out[...], v_out[...] = _adamw_math(w_ref[...], g_ref[...], m_ref[...], v_ref[...])

    out = jax.ShapeDtypeStruct(w.shape, F32)
    spec = _full(w.shape)
    return pl.pallas_call(
        body, name="small_adamw", grid=(1,), in_specs=[spec] * 4, out_specs=[spec] * 3, out_shape=[out, out, out],
    )(w, g, m, v)


def _row_tile(rows):
    for cand in (256, 128):
        if rows % cand == 0:
            return cand
    return rows


def _pair_sum(name, by_owner, got, core, after=()):
    _, R, C = got.shape
    tr = _row_tile(R)

    def body(core_ref, a_ref, b_ref, o_ref):
        o_ref[...] = (a_ref[...].astype(F32) + b_ref[...].astype(F32)).astype(BF16)

    blk = pl.BlockSpec((None, tr, C), lambda k, i, core_ref: (k, i, 0))
    mine = pl.BlockSpec((None, tr, C), lambda k, i, core_ref: (2 * k + core_ref[0], i, 0))
    return pl.pallas_call(
        _drop_operands(body, 3, len(after)), name=name,
        grid_spec=pltpu.PrefetchScalarGridSpec(num_scalar_prefetch=1, grid=(4, R // tr),
                                               in_specs=[mine, blk] + [HBM_SPEC] * len(after), out_specs=blk),
        out_shape=jax.ShapeDtypeStruct(got.shape, BF16),
        compiler_params=_params(("parallel", "parallel")),
    )(core, by_owner, got, *after)


def _shards_from_cols(name, full):
    R, allc = full.shape
    c = allc // N_DEV
    tr = _row_tile(R)

    def body(f_ref, o_ref):
        for s in range(N_DEV):
            o_ref[s] = f_ref[:, s * c:(s + 1) * c]

    return pl.pallas_call(
        body, name=name, grid=(R // tr,),
        in_specs=[pl.BlockSpec((tr, allc), lambda i: (i, 0))],
        out_specs=pl.BlockSpec((N_DEV, tr, c), lambda i: (0, i, 0)),
        out_shape=jax.ShapeDtypeStruct((N_DEV, R, c), full.dtype),
        compiler_params=_params(("parallel",)),
    )(full)


MESH = pl.DeviceIdType.MESH
HBM_SPEC = pl.BlockSpec(memory_space=pl.ANY)


def _handshake(peers):
    barrier = pltpu.get_barrier_semaphore()
    for peer in peers:
        pl.semaphore_signal(barrier, inc=1, device_id=peer, device_id_type=MESH)
    pl.semaphore_wait(barrier, len(peers))


def _comm_call(body, name, operands, out_shape, scratch, collective_id):
    if collective_id is None:
        return pl.pallas_call(body, name=name, in_specs=[HBM_SPEC] * len(operands), out_specs=[HBM_SPEC] * len(out_shape),
                              out_shape=out_shape, scratch_shapes=scratch)(*operands)
    return pl.kernel(body, out_type=out_shape, mesh=plsc.ScalarSubcoreMesh(axis_name="sequencer", num_cores=1),
                     scratch_types=scratch, name=name,
                     compiler_params=pltpu.CompilerParams(collective_id=collective_id))(*operands)


def _all_gather(name, blocks, collective_id=None, after=()):
    n = len(blocks)
    na = len(after)

    def body(*refs):
        x_refs, out_refs = refs[:n], refs[n + na:2 * n + na]
        send_sems, recv_sems, local_sems = refs[2 * n + na:]
        x, y, c = lax.axis_index("x"), lax.axis_index("y"), lax.axis_index("c")
        me, sibling = (x, y, c), (x, y, 1 - c)
        chips = [(1 - x, y), (x, 1 - y), (1 - x, 1 - y)]
        if collective_id is not None:
            _handshake([sibling] + [(*chip, c) for chip in chips])

        def slot(i, px, py, pc):
            return out_refs[i].at[4 * px + 2 * py + pc]

        def copy(i, k, blk, to, src=None):
            return pltpu.make_async_remote_copy(
                src_ref=slot(i, *blk) if src is None else src, dst_ref=slot(i, *blk),
                send_sem=send_sems.at[7 * i + k], recv_sem=recv_sems.at[7 * i + k], device_id=to, device_id_type=MESH)

        mine = [pltpu.make_async_copy(x_refs[i], slot(i, *me), local_sems.at[i]) for i in range(n)]
        for cp in mine:
            cp.start()
        first = []
        for i in range(n):
            first.append(copy(i, 0, me, sibling, src=x_refs[i]))
            first += [copy(i, 1 + j, me, (*chip, c), src=x_refs[i]) for j, chip in enumerate(chips)]
        for cp in first:
            cp.start()
        passed = []
        for i in range(n):
            for j, chip in enumerate(chips):
                copy(i, 1 + j, (*chip, c), me).wait_recv()
                passed.append(copy(i, 4 + j, (*chip, c), sibling))
                passed[-1].start()
        for i in range(n):
            copy(i, 0, sibling, me).wait_recv()
            for j, chip in enumerate(chips):
                copy(i, 4 + j, (*chip, 1 - c), me).wait_recv()
        for cp in first + passed:
            cp.wait_send()
        for cp in mine:
            cp.wait()

    return _comm_call(
        body, name, list(blocks) + list(after), [jax.ShapeDtypeStruct((N_DEV,) + b.shape, b.dtype) for b in blocks],
        [pltpu.SemaphoreType.DMA((7 * n,)), pltpu.SemaphoreType.DMA((7 * n,)), pltpu.SemaphoreType.DMA((n,))],
        collective_id)


def _sibling_swap(name, by_owner, collective_id=None, after=()):
    n = len(by_owner)
    na = len(after)

    def body(*refs):
        x_refs, out_refs = refs[:n], refs[n + na:2 * n + na]
        send_sems, recv_sems = refs[2 * n + na:]
        x, y, c = lax.axis_index("x"), lax.axis_index("y"), lax.axis_index("c")
        if collective_id is not None:
            _handshake([(x, y, 1 - c)])
        copies = []
        for i in range(n):
            for k in range(4):
                copies.append(pltpu.make_async_remote_copy(
                    src_ref=x_refs[i].at[2 * k + 1 - c], dst_ref=out_refs[i].at[k],
                    send_sem=send_sems.at[4 * i + k], recv_sem=recv_sems.at[4 * i + k],
                    device_id=(x, y, 1 - c), device_id_type=MESH))
        for cp in copies:
            cp.start()
        for cp in copies:
            cp.wait()

    return _comm_call(
        body, name, list(by_owner) + list(after),
        [jax.ShapeDtypeStruct((4,) + b.shape[1:], b.dtype) for b in by_owner],
        [pltpu.SemaphoreType.DMA((4 * n,)), pltpu.SemaphoreType.DMA((4 * n,))], collective_id)


def _chip_exchange(name, sums, collective_id=None, after=()):
    n = len(sums)
    na = len(after)

    def body(*refs):
        x_refs, out_refs = refs[:n], refs[n + na:2 * n + na]
        send_sems, recv_sems, local_sems = refs[2 * n + na:]
        x, y, c = lax.axis_index("x"), lax.axis_index("y"), lax.axis_index("c")
        chips = [(1 - x, y), (x, 1 - y), (1 - x, 1 - y)]
        my_chip = 2 * x + y
        if collective_id is not None:
            _handshake([(cx, cy, c) for cx, cy in chips])
        mine = [pltpu.make_async_copy(x_refs[i].at[my_chip], out_refs[i].at[my_chip], local_sems.at[i])
                for i in range(n)]
        for cp in mine:
            cp.start()
        sends = []
        for i in range(n):
            for j, (cx, cy) in enumerate(chips):
                sends.append(pltpu.make_async_remote_copy(
                    src_ref=x_refs[i].at[2 * cx + cy], dst_ref=out_refs[i].at[my_chip],
                    send_sem=send_sems.at[3 * i + j], recv_sem=recv_sems.at[3 * i + j],
                    device_id=(cx, cy, c), device_id_type=MESH))
        for cp in sends:
            cp.start()
        for i in range(n):
            for j, (cx, cy) in enumerate(chips):
                pltpu.make_async_remote_copy(
                    src_ref=x_refs[i].at[my_chip], dst_ref=out_refs[i].at[2 * cx + cy],
                    send_sem=send_sems.at[3 * i + j], recv_sem=recv_sems.at[3 * i + j],
                    device_id=(cx, cy, c), device_id_type=MESH).wait_recv()
        for cp in sends:
            cp.wait_send()
        for cp in mine:
            cp.wait()

    return _comm_call(
        body, name, list(sums) + list(after), [jax.ShapeDtypeStruct(s.shape, s.dtype) for s in sums],
        [pltpu.SemaphoreType.DMA((3 * n,)), pltpu.SemaphoreType.DMA((3 * n,)), pltpu.SemaphoreType.DMA((n,))],
        collective_id)


def _cast_shards(shards):
    n = len(shards)

    def body(*refs):
        for i in range(n):
            refs[n + i][...] = refs[i][...].astype(BF16)

    vmem = pl.BlockSpec(memory_space=pltpu.VMEM)
    return pl.pallas_call(
        body, name="cast_shards", in_specs=[vmem] * n, out_specs=[vmem] * n,
        out_shape=[jax.ShapeDtypeStruct(s.shape, BF16) for s in shards],
        compiler_params=pltpu.CompilerParams(vmem_limit_bytes=VMEM_LIMIT_V7X),
    )(*shards)


BIG = ("w_in", "w_branch_a", "w_branch_b", "w_out", "w_ffn_gate", "w_ffn_up", "w_ffn_down")


def _local_step(x, target, gains, low, conv_w, wg8, reduce):
    g_mix, g_hg, g_ffn, g_fin = gains
    w_in = wg8["w_in"].reshape(N_IN, D_MODEL)
    wg = wg8["w_ffn_gate"].reshape(D_FF, D_MODEL)
    wu = wg8["w_ffn_up"].reshape(D_FF, D_MODEL)
    wa, wb = wg8["w_branch_a"], wg8["w_branch_b"]
    wo = wg8["w_out"].reshape(D_MODEL, D_MODEL)
    wd = wg8["w_ffn_down"].reshape(D_FF, D_MODEL)

    ht, hg, cv, gt = _fwd_in(x, g_mix, w_in)
    o, og, ogt, st = _hg_fwd(hg, low, g_hg)
    cvo, cvot = _conv_fwd(cv, conv_w)
    x1, mgt = _merge_fwd(og, cvo, gt, x, wa, wb, wo)
    h2t, gate, up, actt, x2 = _ffn_fwd(x1, g_ffn, wg, wu, wd)
    loss, d_gfin, dx2 = _final_fwd_bwd(x2, target, g_fin)

    dgate, dup, dx1, d_gffn = _ffn_bwd(dx2, x1, gate, up, g_ffn, wg, wu, wd)
    ffn = dict(
        w_ffn_down=_wgrad("wgrad_ffn_down", actt, dx2, 512).reshape(N_DEV, D_FF // N_DEV, D_MODEL),
        w_ffn_gate=_wgrad("wgrad_ffn_gate", h2t, dgate, 1408, transposed=True).reshape(N_DEV, D_FF // N_DEV, D_MODEL),
        w_ffn_up=_wgrad("wgrad_ffn_up", h2t, dup, 1408, transposed=True).reshape(N_DEV, D_FF // N_DEV, D_MODEL))
    sums_ffn, got_ffn = reduce.begin(ffn)
    dgt, dya, dyb, dog, dcvo = _merge_bwd(dx1, og, cvo, gt, wa, wb, wo)
    out = dict(
        w_out=_wgrad("wgrad_out", mgt, dx1, 512).reshape(N_DEV, D_MODEL // N_DEV, D_MODEL),
        w_branch_a=_shards_from_cols("split_w_branch_a", _wgrad("wgrad_branch_a", ogt, dya, 512)),
        w_branch_b=_shards_from_cols("split_w_branch_b", _wgrad("wgrad_branch_b", cvot, dyb, 512)))
    sums_out, got_out = reduce.begin(out, after=got_ffn[:1])
    parts_ffn, _ = reduce.finish(ffn, sums_ffn, after=got_out[:1])
    dq, df, di, dg, d_low, d_ghg = _hg_bwd(dog, hg, o, st, low, g_hg, after=list(sums_ffn) + list(sums_out))
    dc, db, dxb, d_conv = _conv_bwd(dcvo, cv, conv_w, after=[dq])
    parts_out, updated_out = reduce.finish(out, sums_out, after=[parts_ffn[0], dc])
    dparts = [dq, df, di, dg, dc, db, dxb, dgt]
    w_in_grad = dict(w_in=_wgrad_in(ht, dparts, after=parts_ffn[:1]).reshape(N_DEV, N_IN // N_DEV, D_MODEL))
    sums_in, _ = reduce.begin(w_in_grad, after=parts_out[:1], sum_after=updated_out)
    parts_in, _ = reduce.finish(w_in_grad, sums_in)
    grad_x, d_gmix = _in_bwd(dparts, w_in, x, dx1, g_mix, after=list(parts_out[:1]) + list(sums_in))
    small = dict(norm_mix_g=d_gmix, norm_ffn_g=d_gffn, norm_final_g=d_gfin, lower_bounds=d_low, hg_norm_g=d_ghg,
                 conv_w=d_conv, loss=loss)
    return grad_x, small, parts_in


_SMALL_LAYOUT = (("norm_mix_g", 0, 8), ("norm_ffn_g", 8, 8), ("norm_final_g", 16, 8), ("lower_bounds", 24, 8),
                 ("hg_norm_g", 32, 1))
_LOSS_ROW = 40
_CONV_ROW = 48


def _pad_rows(a, rows):
    return jnp.pad(a, ((0, rows - a.shape[0]), (0, 0)))


def _pack_small(vals, conv_rows):
    parts = [_pad_rows(vals[name].reshape(rows, 128), 8) for name, _, rows in _SMALL_LAYOUT]
    loss = vals["loss"][:, :128] if "loss" in vals else jnp.zeros((1, 128), F32)
    parts.append(_pad_rows(loss, 8))
    parts.append(_pad_rows(conv_rows, SMALL_ROWS - _CONV_ROW))
    return jnp.concatenate(parts, axis=0)


def _conv_shard_rows(a):
    return jnp.pad(a, ((0, 5), (0, 64)))


def kernel(x, norm_mix_g, w_in, lower_bounds, hg_norm_g, conv_w, w_branch_a, w_branch_b, w_out, norm_ffn_g, w_ffn_gate, w_ffn_up, w_ffn_down, norm_final_g, loss_target, m_norm_mix_g, m_w_in, m_lower_bounds, m_hg_norm_g, m_conv_w, m_w_branch_a, m_w_branch_b, m_w_out, m_norm_ffn_g, m_w_ffn_gate, m_w_ffn_up, m_w_ffn_down, m_norm_final_g, v_norm_mix_g, v_w_in, v_lower_bounds, v_hg_norm_g, v_conv_w, v_w_branch_a, v_w_branch_b, v_w_out, v_norm_ffn_g, v_w_ffn_gate, v_w_ffn_up, v_w_ffn_down, v_norm_final_g):
    cx, cy, cc = lax.axis_index("x"), lax.axis_index("y"), lax.axis_index("c")
    my_dev = 4 * cx + 2 * cy + cc

    def tr(a):
        return a[0].T

    big = dict(w_in=tr(w_in), w_branch_a=w_branch_a[0], w_branch_b=w_branch_b[0], w_out=w_out[0],
               w_ffn_gate=tr(w_ffn_gate), w_ffn_up=tr(w_ffn_up), w_ffn_down=w_ffn_down[0])
    big_m = dict(w_in=tr(m_w_in), w_branch_a=m_w_branch_a[0], w_branch_b=m_w_branch_b[0], w_out=m_w_out[0],
                 w_ffn_gate=tr(m_w_ffn_gate), w_ffn_up=tr(m_w_ffn_up), w_ffn_down=m_w_ffn_down[0])
    big_v = dict(w_in=tr(v_w_in), w_branch_a=v_w_branch_a[0], w_branch_b=v_w_branch_b[0], w_out=v_w_out[0],
                 w_ffn_gate=tr(v_w_ffn_gate), w_ffn_up=tr(v_w_ffn_up), w_ffn_down=v_w_ffn_down[0])
    transposed = ("w_in", "w_ffn_gate", "w_ffn_up")

    shards = dict(zip(BIG, _cast_shards([big[n] for n in BIG])))
    first = _all_gather("gather_w_in", [shards["w_in"], _conv_shard_rows(conv_w[0])])
    later = _all_gather("gather_rest", [shards[n] for n in BIG[1:]], collective_id=1, after=first[1:])
    wg8 = dict(zip(BIG, [first[0]] + list(later)))
    conv_full = first[1][:, :3, :64].transpose(1, 0, 2).reshape(3, CONV_WIDTH)

    core = cc.reshape(1).astype(jnp.int32)
    outs = {}
    ids = iter(range(2, 16))

    class Reduce:
        @staticmethod
        def begin(grads, after=(), sum_after=()):
            names = list(grads)
            by_owner = [grads[n] for n in names]
            got = _sibling_swap("sibling_swap_" + names[0], by_owner, collective_id=next(ids), after=after)
            sums = [_pair_sum("pair_sum_" + n, a, b, core, after=sum_after) for n, a, b in zip(names, by_owner, got)]
            return sums, got

        @staticmethod
        def finish(grads, chip_sums, after=()):
            names = list(grads)
            parts = _chip_exchange("chip_exchange_" + names[0], chip_sums, collective_id=next(ids), after=after)
            for n, p in zip(names, parts):
                outs[n] = _adamw_sum("adamw_" + n, big[n], p, big_m[n], big_v[n])
            return parts, [outs[n][1] for n in names]

    gains = (norm_mix_g, hg_norm_g, norm_ffn_g, norm_final_g.reshape(1, D_MODEL))
    grad_x, small, last = _local_step(x[0], loss_target[0], gains, lower_bounds, conv_full, wg8, Reduce)

    small_all = _all_gather("gather_small", [_pack_small(small, small["conv_w"].reshape(12, 128))],
                            collective_id=next(ids), after=last[:1])
    ssum = _small_sum(small_all[0])
    conv_g_full = ssum[_CONV_ROW:_CONV_ROW + 12].reshape(3, CONV_WIDTH)
    conv_g = lax.dynamic_slice_in_dim(conv_g_full, my_dev * 64, 64, axis=1)
    loss = ssum[_LOSS_ROW, 0]
    g_rows = jnp.concatenate([ssum[:_CONV_ROW], _pad_rows(_conv_shard_rows(conv_g), SMALL_ROWS - _CONV_ROW)], axis=0)

    def pack_state(a):
        vals = dict(norm_mix_g=a[0], norm_ffn_g=a[1], norm_final_g=a[2], lower_bounds=a[3], hg_norm_g=a[4])
        return _pack_small(vals, _conv_shard_rows(a[5][0]))

    sw = pack_state((norm_mix_g, norm_ffn_g, norm_final_g, lower_bounds, hg_norm_g, conv_w))
    sm = pack_state((m_norm_mix_g, m_norm_ffn_g, m_norm_final_g, m_lower_bounds, m_hg_norm_g, m_conv_w))
    sv = pack_state((v_norm_mix_g, v_norm_ffn_g, v_norm_final_g, v_lower_bounds, v_hg_norm_g, v_conv_w))
    s_delta, s_m, s_v = _small_adamw(sw, g_rows, sm, sv)

    shapes = dict(norm_mix_g=(1, D_MODEL), norm_ffn_g=(1, D_MODEL), norm_final_g=(D_MODEL,),
                  lower_bounds=(2, HG_WIDTH), hg_norm_g=(1, HEAD_DIM))

    def unpack(buf, name):
        if name == "conv_w":
            return buf[_CONV_ROW:_CONV_ROW + 3, :64].reshape(1, 3, 64)
        for nm, off, rows in _SMALL_LAYOUT:
            if nm == name:
                return buf[off:off + rows].reshape(shapes[name])
        raise KeyError(name)

    order = ["norm_mix_g", "w_in", "lower_bounds", "hg_norm_g", "conv_w", "w_branch_a", "w_branch_b", "w_out",
             "norm_ffn_g", "w_ffn_gate", "w_ffn_up", "w_ffn_down", "norm_final_g"]
    result = [loss, grad_x[None]]
    for k, sbuf in enumerate((g_rows, s_delta, s_m, s_v)):
        for n in order:
            if n in outs:
                result.append((outs[n][k].T if n in transposed else outs[n][k])[None])
            else:
                result.append(unpack(sbuf, n))
    return tuple(result)
```

```python
import jax
import jax.numpy as jnp
from jax import lax
from jax.experimental import pallas as pl
from jax.experimental.pallas import tpu as pltpu
from jax.experimental.pallas import tpu_sc as plsc

F32 = jnp.float32
BF16 = jnp.bfloat16
STASH = jnp.bfloat16

D_MODEL = 1024
HG_WIDTH = 512
HEAD_DIM = 128
N_HEADS = 4
HEADS_PER_STEP = 4
HEAD_GROUPS = N_HEADS // HEADS_PER_STEP
CONV_WIDTH = 512
D_FF = 2816
CHUNK = 32
EPS = 1e-6
Q_SCALE = HEAD_DIM ** -0.5
N_DEV = 8

ADAM_LR = 0.001
ADAM_B1 = 0.9
ADAM_B2 = 0.999
ADAM_EPS = 1e-08
ADAM_WD = 0.01
ADAM_STEP = 10

VMEM_LIMIT_V7X = 56 * 1024 * 1024

SMALL_ROWS = 64


def _params(sem, vmem=VMEM_LIMIT_V7X):
    return pltpu.CompilerParams(dimension_semantics=sem, vmem_limit_bytes=vmem)


def _mm(a, b):
    return jnp.dot(a.astype(BF16), b.astype(BF16), preferred_element_type=F32)


def _mm_nt(a, b):
    return lax.dot_general(a.astype(BF16), b.astype(BF16), (((1,), (1,)), ((), ())), preferred_element_type=F32)


def _mm_tn(a, b):
    return lax.dot_general(a.astype(BF16), b.astype(BF16), (((0,), (0,)), ((), ())), preferred_element_type=F32)


def _sigmoid(x):
    return 0.5 * jnp.tanh(0.5 * x) + 0.5


def _resident(shape):
    nd = len(shape)
    return pl.BlockSpec(shape, lambda *_: (0,) * nd, pipeline_mode=pl.Buffered(1))


def _full(shape):
    nd = len(shape)
    return pl.BlockSpec(shape, lambda *_: (0,) * nd)


def _shard_cols(w_ref):
    return jnp.concatenate([w_ref[s] for s in range(N_DEV)], axis=1)


N_HG = 4 * HG_WIDTH
N_CV = 3 * CONV_WIDTH
N_GT = 2 * D_MODEL
N_IN = N_HG + N_CV + N_GT


def _col(tm, n):
    return pl.BlockSpec((n, tm), lambda i: (0, i))


def _fwd_in(x, g, w_in_t):
    T = x.shape[0]
    tm = min(512, T)

    def body(x_ref, g_ref, w_ref, ht_ref, hg_ref, cv_ref, gt_ref):
        xv = x_ref[...]
        r = lax.rsqrt(jnp.mean(xv * xv, axis=-1, keepdims=True) + EPS)
        hf = xv * r * g_ref[...]
        h = hf.astype(BF16)
        ht_ref[...] = hf.T.astype(BF16)
        hg_ref[...] = _mm_nt(h, w_ref[:N_HG, :])
        cv_ref[...] = _mm_nt(h, w_ref[N_HG:N_HG + N_CV, :]).astype(STASH)
        gt_ref[...] = _mm_nt(h, w_ref[N_HG + N_CV:, :]).astype(STASH)

    row = lambda n: pl.BlockSpec((tm, n), lambda i: (i, 0))
    return pl.pallas_call(
        body, name="fwd_in", grid=(T // tm,),
        in_specs=[row(D_MODEL), _full((1, D_MODEL)), _resident(w_in_t.shape)],
        out_specs=[_col(tm, D_MODEL), row(N_HG), row(N_CV), row(N_GT)],
        out_shape=[jax.ShapeDtypeStruct((D_MODEL, T), BF16), jax.ShapeDtypeStruct((T, N_HG), F32),
                   jax.ShapeDtypeStruct((T, N_CV), STASH), jax.ShapeDtypeStruct((T, N_GT), STASH)],
        compiler_params=_params(("parallel",)),
    )(x, g, w_in_t)


def _chunk_pos(shape):
    return lax.broadcasted_iota(jnp.int32, shape, 0) & (CHUNK - 1)


def _chunk_cumsum(x, pos):
    s = 1
    while s < CHUNK:
        x = x + jnp.where(pos >= s, pltpu.roll(x, s, 0), 0.0)
        s *= 2
    return x


def _chunk_rev_cumsum(x, pos):
    n = x.shape[0]
    s = 1
    while s < CHUNK:
        x = x + jnp.where(pos + s < CHUNK, pltpu.roll(x, n - s, 0), 0.0)
        s *= 2
    return x


def _chunk_bcast(x3, row, tb):
    return jnp.broadcast_to(x3[:, row:row + 1, :], x3.shape).reshape(tb, x3.shape[-1])


def _lower_bound(low_ref):
    l0 = low_ref[0:1, :]
    l1 = low_ref[1:2, :]
    m = jnp.maximum(l0, l1)
    e0 = jnp.exp(l0 - m)
    e1 = jnp.exp(l1 - m)
    return e0 / (e0 + e1), e1 / (e0 + e1)


def _hg_gates(qr, fr, lb, pos, tb):
    sq = _sigmoid(qr)
    q = qr * sq * Q_SCALE
    sg = _sigmoid(fr)
    f = lb + (1.0 - lb) * sg
    k = 1.0 - f
    b = _chunk_cumsum(jnp.log(f), pos)
    b3 = b.reshape(tb // CHUNK, CHUNK, HEAD_DIM)
    anc = _chunk_bcast(b3, CHUNK // 2 - 1, tb)
    blb = _chunk_bcast(b3, CHUNK - 1, tb)
    e_qa = jnp.exp(b - anc)
    e_ka = jnp.exp(anc - b)
    e_b = jnp.exp(b)
    e_ko = jnp.exp(blb - b)
    dec = jnp.exp(blb)
    return sq, q, sg, f, k, e_qa, e_ka, e_b, e_ko, dec


def _intra_mask(sb):
    r = lax.broadcasted_iota(jnp.int32, (sb, sb), 0)
    c = lax.broadcasted_iota(jnp.int32, (sb, sb), 1)
    return ((r // CHUNK) == (c // CHUNK)) & (c <= r)


def _hg_fwd(hg, low, gn):
    T = hg.shape[0]
    tb = min(512, T)
    sb = min(256, tb)
    nb = T // tb
    nc = tb // CHUNK
    wid = HEADS_PER_STEP * HEAD_DIM

    def body(q_ref, f_ref, i_ref, g_ref, low_ref, gn_ref, o_ref, og_ref, ogt_ref, st_ref, s_scr):
        t = pl.program_id(1)

        @pl.when(t == 0)
        def _():
            s_scr[...] = jnp.zeros_like(s_scr)

        pos = _chunk_pos((tb, HEAD_DIM))
        mask = _intra_mask(sb)
        lanes = [slice(hh * HEAD_DIM, (hh + 1) * HEAD_DIM) for hh in range(HEADS_PER_STEP)]
        qi, ko, vb, dec, st = [], [], [], [], []
        for hh, ln in enumerate(lanes):
            lb, _ = _lower_bound(low_ref.at[:, ln])
            _, q, _, _, k, e_qa, e_ka, e_b, e_ko, dec_h = _hg_gates(q_ref[:, ln], f_ref[:, ln], lb, pos, tb)
            qh = (q * e_qa).astype(BF16)
            kh = (k * e_ka).astype(BF16)
            qi.append((q * e_b).astype(BF16))
            ko.append((k * e_ko).astype(BF16))
            vb.append(i_ref[:, ln].astype(BF16))
            dec.append(dec_h)
            st.append(s_scr[hh])
            for s in range(tb // sb):
                sl = slice(s * sb, (s + 1) * sb)
                p = jnp.where(mask, _mm_nt(qh[sl], kh[sl]), 0.0)
                o_ref[sl, ln] = _mm(p, vb[hh][sl])
        for c in range(nc):
            sl = slice(c * CHUNK, (c + 1) * CHUNK)
            for hh, ln in enumerate(lanes):
                st_ref[hh, c] = st[hh]
                o_ref[sl, ln] = o_ref[sl, ln] + _mm_nt(qi[hh][sl], st[hh])
                st[hh] = dec[hh][c * CHUNK:c * CHUNK + 1, :] * st[hh] + _mm_tn(vb[hh][sl], ko[hh][sl])
        for hh, ln in enumerate(lanes):
            s_scr[hh] = st[hh]
            o = o_ref[:, ln]
            r = lax.rsqrt(jnp.mean(o * o, axis=-1, keepdims=True) + EPS)
            gr = g_ref[:, ln]
            og = (o * r * gn_ref[...]) * (gr * _sigmoid(gr))
            og_ref[:, ln] = og.astype(BF16)
            ogt_ref[ln, :] = og.T.astype(BF16)

    col = lambda p: pl.BlockSpec((tb, wid), lambda h, t: (t, p * HEAD_GROUPS + h))
    hcol = pl.BlockSpec((tb, wid), lambda h, t: (t, h))
    return pl.pallas_call(
        body, name="hg_fwd", grid=(HEAD_GROUPS, nb),
        in_specs=[col(0), col(1), col(2), col(3), pl.BlockSpec((2, wid), lambda h, t: (0, h)),
                  pl.BlockSpec((1, HEAD_DIM), lambda h, t: (0, 0))],
        out_specs=[hcol, hcol, pl.BlockSpec((wid, tb), lambda h, t: (h, t)),
                   pl.BlockSpec((HEADS_PER_STEP, nc, HEAD_DIM, HEAD_DIM), lambda h, t: (h, t, 0, 0))],
        out_shape=[jax.ShapeDtypeStruct((T, HG_WIDTH), F32), jax.ShapeDtypeStruct((T, HG_WIDTH), BF16),
                   jax.ShapeDtypeStruct((HG_WIDTH, T), BF16),
                   jax.ShapeDtypeStruct((N_HEADS, T // CHUNK, HEAD_DIM, HEAD_DIM), F32)],
        scratch_shapes=[pltpu.VMEM((HEADS_PER_STEP, HEAD_DIM, HEAD_DIM), F32)],
        compiler_params=_params(("parallel", "arbitrary")),
    )(hg, hg, hg, hg, low, gn)


def _conv_fwd(cv, conv_w):
    T = cv.shape[0]
    nj = CONV_WIDTH // 128

    def body(c_ref, b_ref, x_ref, w_ref, o_ref, ot_ref):
        row = lax.broadcasted_iota(jnp.int32, (T, 128), 0)
        u = c_ref[...].astype(F32) * x_ref[...].astype(F32)
        u1 = jnp.where(row >= 1, pltpu.roll(u, 1, 0), 0.0)
        u2 = jnp.where(row >= 2, pltpu.roll(u, 2, 0), 0.0)
        y = w_ref[0:1, :] * u2 + w_ref[1:2, :] * u1 + w_ref[2:3, :] * u
        out = b_ref[...].astype(F32) * y
        o_ref[...] = out.astype(BF16)
        ot_ref[...] = out.T.astype(BF16)

    col = lambda p: pl.BlockSpec((T, 128), lambda j: (0, p * nj + j))
    return pl.pallas_call(
        body, name="conv_fwd", grid=(nj,),
        in_specs=[col(0), col(1), col(2), pl.BlockSpec((3, 128), lambda j: (0, j))],
        out_specs=[pl.BlockSpec((T, 128), lambda j: (0, j)), pl.BlockSpec((128, T), lambda j: (j, 0))],
        out_shape=[jax.ShapeDtypeStruct((T, CONV_WIDTH), BF16), jax.ShapeDtypeStruct((CONV_WIDTH, T), BF16)],
        compiler_params=_params(("parallel",)),
    )(cv, cv, cv, conv_w)


def _merge_fwd(og, cvo, gt, x, wa, wb, wo):
    T = x.shape[0]
    tm = min(512, T)

    def body(og_ref, cvo_ref, gt_ref, x_ref, wa_ref, wb_ref, wo_ref, x1_ref, mgt_ref):
        ya = jnp.dot(og_ref[...], _shard_cols(wa_ref), preferred_element_type=F32)
        yb = jnp.dot(cvo_ref[...], _shard_cols(wb_ref), preferred_element_type=F32)
        m = (_sigmoid(gt_ref[:, :D_MODEL].astype(F32)) * ya
             + _sigmoid(gt_ref[:, D_MODEL:].astype(F32)) * yb)
        mgt_ref[...] = m.T.astype(BF16)
        x1_ref[...] = x_ref[...] + jnp.dot(m.astype(BF16), wo_ref[...], preferred_element_type=F32)

    row = lambda n: pl.BlockSpec((tm, n), lambda i: (i, 0))
    return pl.pallas_call(
        body, name="merge_fwd", grid=(T // tm,),
        in_specs=[row(HG_WIDTH), row(CONV_WIDTH), row(2 * D_MODEL), row(D_MODEL),
                  _resident(wa.shape), _resident(wb.shape), _resident(wo.shape)],
        out_specs=[row(D_MODEL), _col(tm, D_MODEL)],
        out_shape=[jax.ShapeDtypeStruct((T, D_MODEL), F32), jax.ShapeDtypeStruct((D_MODEL, T), BF16)],
        compiler_params=_params(("parallel",)),
    )(og, cvo, gt, x, wa, wb, wo)


def _ffn_fwd(x1, g, wg, wu, wd):
    T = x1.shape[0]
    tm = min(256, T)

    def body(x_ref, g_ref, wg_ref, wu_ref, wd_ref, ht_ref, gate_ref, up_ref, actt_ref, x2_ref):
        xv = x_ref[...]
        r = lax.rsqrt(jnp.mean(xv * xv, axis=-1, keepdims=True) + EPS)
        hf = xv * r * g_ref[...]
        h = hf.astype(BF16)
        ht_ref[...] = hf.T.astype(BF16)
        gate = _mm_nt(h, wg_ref[...])
        up = _mm_nt(h, wu_ref[...])
        gate_ref[...] = gate.astype(STASH)
        up_ref[...] = up.astype(STASH)
        act = gate * _sigmoid(gate) * up
        actt_ref[...] = act.T.astype(BF16)
        x2_ref[...] = xv + jnp.dot(act.astype(BF16), wd_ref[...], preferred_element_type=F32)

    row = lambda n: pl.BlockSpec((tm, n), lambda i: (i, 0))
    return pl.pallas_call(
        body, name="ffn_fwd", grid=(T // tm,),
        in_specs=[row(D_MODEL), _full((1, D_MODEL)), _resident(wg.shape), _resident(wu.shape), _resident(wd.shape)],
        out_specs=[_col(tm, D_MODEL), row(D_FF), row(D_FF), _col(tm, D_FF), row(D_MODEL)],
        out_shape=[jax.ShapeDtypeStruct((D_MODEL, T), BF16), jax.ShapeDtypeStruct((T, D_FF), STASH),
                   jax.ShapeDtypeStruct((T, D_FF), STASH), jax.ShapeDtypeStruct((D_FF, T), BF16),
                   jax.ShapeDtypeStruct((T, D_MODEL), F32)],
        compiler_params=_params(("parallel",)),
    )(x1, g, wg, wu, wd)


def _final_fwd_bwd(x2, target, g):
    T = x2.shape[0]
    tm = min(512, T)

    def body(x_ref, t_ref, g_ref, loss_ref, dg_ref, dx_ref):
        @pl.when(pl.program_id(0) == 0)
        def _():
            loss_ref[...] = jnp.zeros_like(loss_ref)
            dg_ref[...] = jnp.zeros_like(dg_ref)

        xv = x_ref[...]
        gv = g_ref[...]
        r = lax.rsqrt(jnp.mean(xv * xv, axis=-1, keepdims=True) + EPS)
        xh = xv * r
        err = xh * gv - t_ref[...]
        loss_ref[...] += 0.5 * jnp.sum(jnp.mean(err * err, axis=-1, keepdims=True), axis=0, keepdims=True)
        dy = err * (1.0 / D_MODEL)
        dg_ref[...] += jnp.sum(dy * xh, axis=0, keepdims=True)
        w = dy * gv
        dx_ref[...] = r * (w - xh * jnp.mean(w * xh, axis=-1, keepdims=True))

    row = pl.BlockSpec((tm, D_MODEL), lambda i: (i, 0))
    return pl.pallas_call(
        body, name="final_fwd_bwd", grid=(T // tm,),
        in_specs=[row, row, _full((1, D_MODEL))],
        out_specs=[_full((1, 128)), _full((1, D_MODEL)), row],
        out_shape=[jax.ShapeDtypeStruct((1, 128), F32), jax.ShapeDtypeStruct((1, D_MODEL), F32),
                   jax.ShapeDtypeStruct((T, D_MODEL), F32)],
        compiler_params=_params(("arbitrary",)),
    )(x2, target, g)


def _ffn_bwd(dx2, x1, gate, up, g, wg, wu, wd):
    T = x1.shape[0]
    tm = min(256, T)

    def body(dx2_ref, x_ref, gate_ref, up_ref, g_ref, wg_ref, wu_ref, wd_ref, dgate_ref, dup_ref, dx1_ref, dgn_ref):
        @pl.when(pl.program_id(0) == 0)
        def _():
            dgn_ref[...] = jnp.zeros_like(dgn_ref)

        dx2 = dx2_ref[...]
        dact = _mm_nt(dx2, wd_ref[...])
        gate = gate_ref[...].astype(F32)
        s = _sigmoid(gate)
        dgate = (dact * up_ref[...].astype(F32) * (s * (1.0 + gate * (1.0 - s)))).astype(BF16)
        dup = (dact * (gate * s)).astype(BF16)
        dgate_ref[...] = dgate
        dup_ref[...] = dup
        dh = _mm(dgate, wg_ref[...]) + _mm(dup, wu_ref[...])
        xv = x_ref[...]
        r = lax.rsqrt(jnp.mean(xv * xv, axis=-1, keepdims=True) + EPS)
        xh = xv * r
        dgn_ref[...] += jnp.sum(dh * xh, axis=0, keepdims=True)
        w = dh * g_ref[...]
        dx1_ref[...] = dx2 + r * (w - xh * jnp.mean(w * xh, axis=-1, keepdims=True))

    row = lambda n: pl.BlockSpec((tm, n), lambda i: (i, 0))
    return pl.pallas_call(
        body, name="ffn_bwd", grid=(T // tm,),
        in_specs=[row(D_MODEL), row(D_MODEL), row(D_FF), row(D_FF), _full((1, D_MODEL)),
                  _resident(wg.shape), _resident(wu.shape), _resident(wd.shape)],
        out_specs=[row(D_FF), row(D_FF), row(D_MODEL), _full((1, D_MODEL))],
        out_shape=[jax.ShapeDtypeStruct((T, D_FF), BF16), jax.ShapeDtypeStruct((T, D_FF), BF16),
                   jax.ShapeDtypeStruct((T, D_MODEL), F32), jax.ShapeDtypeStruct((1, D_MODEL), F32)],
        compiler_params=_params(("arbitrary",)),
    )(dx2, x1, gate, up, g, wg, wu, wd)


def _merge_bwd(dx1, og, cvo, gt, wa, wb, wo):
    T = dx1.shape[0]
    tm = min(512, T)

    def body(dx_ref, og_ref, cvo_ref, gt_ref, wa_ref, wb_ref, wo_ref, dgt_ref, dya_ref, dyb_ref, dog_ref, dcvo_ref):
        dm = _mm_nt(dx_ref[...], wo_ref[...])
        wa = _shard_cols(wa_ref)
        wb = _shard_cols(wb_ref)
        ya = jnp.dot(og_ref[...], wa, preferred_element_type=F32)
        yb = jnp.dot(cvo_ref[...], wb, preferred_element_type=F32)
        sa = _sigmoid(gt_ref[:, :D_MODEL].astype(F32))
        sb = _sigmoid(gt_ref[:, D_MODEL:].astype(F32))
        dgt_ref[:, :D_MODEL] = (dm * ya * (sa * (1.0 - sa))).astype(BF16)
        dgt_ref[:, D_MODEL:] = (dm * yb * (sb * (1.0 - sb))).astype(BF16)
        dya = (dm * sa).astype(BF16)
        dyb = (dm * sb).astype(BF16)
        dya_ref[...] = dya
        dyb_ref[...] = dyb
        dog_ref[...] = _mm_nt(dya, wa)
        dcvo_ref[...] = _mm_nt(dyb, wb)

    row = lambda n: pl.BlockSpec((tm, n), lambda i: (i, 0))
    return pl.pallas_call(
        body, name="merge_bwd", grid=(T // tm,),
        in_specs=[row(D_MODEL), row(HG_WIDTH), row(CONV_WIDTH), row(2 * D_MODEL),
                  _resident(wa.shape), _resident(wb.shape), _resident(wo.shape)],
        out_specs=[row(2 * D_MODEL), row(D_MODEL), row(D_MODEL), row(HG_WIDTH), row(CONV_WIDTH)],
        out_shape=[jax.ShapeDtypeStruct((T, 2 * D_MODEL), BF16), jax.ShapeDtypeStruct((T, D_MODEL), BF16),
                   jax.ShapeDtypeStruct((T, D_MODEL), BF16), jax.ShapeDtypeStruct((T, HG_WIDTH), F32),
                   jax.ShapeDtypeStruct((T, CONV_WIDTH), F32)],
        compiler_params=_params(("parallel",)),
    )(dx1, og, cvo, gt, wa, wb, wo)


def _conv_bwd(dcvo, cv, conv_w, after=()):
    T = cv.shape[0]
    nj = CONV_WIDTH // 128

    def body(do_ref, c_ref, b_ref, x_ref, w_ref, dc_ref, db_ref, dx_ref, dw_ref):
        row = lax.broadcasted_iota(jnp.int32, (T, 128), 0)
        c = c_ref[...].astype(F32)
        xb = x_ref[...].astype(F32)
        do = do_ref[...]
        u = c * xb
        u1 = jnp.where(row >= 1, pltpu.roll(u, 1, 0), 0.0)
        u2 = jnp.where(row >= 2, pltpu.roll(u, 2, 0), 0.0)
        w0, w1, w2 = w_ref[0:1, :], w_ref[1:2, :], w_ref[2:3, :]
        y = w0 * u2 + w1 * u1 + w2 * u
        db_ref[...] = (do * y).astype(BF16)
        dy = do * b_ref[...].astype(F32)
        dw_ref[0:1, :] = jnp.sum(dy * u2, axis=0, keepdims=True)
        dw_ref[1:2, :] = jnp.sum(dy * u1, axis=0, keepdims=True)
        dw_ref[2:3, :] = jnp.sum(dy * u, axis=0, keepdims=True)
        dy1 = jnp.where(row < T - 1, pltpu.roll(dy, T - 1, 0), 0.0)
        dy2 = jnp.where(row < T - 2, pltpu.roll(dy, T - 2, 0), 0.0)
        du = w2 * dy + w1 * dy1 + w0 * dy2
        dc_ref[...] = (du * xb).astype(BF16)
        dx_ref[...] = (du * c).astype(BF16)

    col = lambda p: pl.BlockSpec((T, 128), lambda j: (0, p * nj + j))
    one = pl.BlockSpec((T, 128), lambda j: (0, j))
    wspec = pl.BlockSpec((3, 128), lambda j: (0, j))
    out = jax.ShapeDtypeStruct((T, CONV_WIDTH), BF16)
    return pl.pallas_call(
        _drop_operands(body, 5, len(after)), name="conv_bwd", grid=(nj,),
        in_specs=[one, col(0), col(1), col(2), wspec] + [HBM_SPEC] * len(after),
        out_specs=[one, one, one, wspec],
        out_shape=[out, out, out, jax.ShapeDtypeStruct((3, CONV_WIDTH), F32)],
        compiler_params=_params(("parallel",)),
    )(dcvo, cv, cv, cv, conv_w, *after)


def _drop_operands(body, first, count):
    def wrapped(*refs):
        return body(*refs[:first], *refs[first + count:])
    return wrapped


def _hg_bwd(dog, hg, o, st, low, gn, after=()):
    T = hg.shape[0]
    tb = min(512, T)
    sb = min(256, tb)
    nb = T // tb
    nc = tb // CHUNK
    wid = HEADS_PER_STEP * HEAD_DIM

    def body(q_ref, f_ref, i_ref, g_ref, low_ref, gn_ref, o_ref, dog_ref, st_ref,
             dq_ref, df_ref, di_ref, dg_ref, dlow_ref, dgn_ref,
             ds_scr, dqi_scr, dko_scr, dv_scr, dd_scr, dqh_scr, dkh_scr):
        h = pl.program_id(0)
        t = pl.program_id(1)

        @pl.when(t == 0)
        def _():
            ds_scr[...] = jnp.zeros_like(ds_scr)
            dlow_ref[...] = jnp.zeros_like(dlow_ref)

        @pl.when((t == 0) & (h == 0))
        def _():
            dgn_ref[...] = jnp.zeros_like(dgn_ref)

        pos = _chunk_pos((tb, HEAD_DIM))
        mask = _intra_mask(sb)
        gnv = gn_ref[...]
        lanes = [slice(hh * HEAD_DIM, (hh + 1) * HEAD_DIM) for hh in range(HEADS_PER_STEP)]
        heads = []
        for hh, ln in enumerate(lanes):
            lb, lb1 = _lower_bound(low_ref.at[:, ln])
            qr = q_ref[:, ln]
            sq, q, sg, f, k, e_qa, e_ka, e_b, e_ko, dec = _hg_gates(qr, f_ref[:, ln], lb, pos, tb)

            gr = g_ref[:, ln]
            o = o_ref[:, ln]
            dog_v = dog_ref[:, ln]
            sgr = _sigmoid(gr)
            r = lax.rsqrt(jnp.mean(o * o, axis=-1, keepdims=True) + EPS)
            oh = o * r
            dg_ref[:, ln] = (dog_v * (oh * gnv) * (sgr * (1.0 + gr * (1.0 - sgr)))).astype(BF16)
            don = dog_v * (gr * sgr)
            dgn_ref[...] += jnp.sum(don * oh, axis=0, keepdims=True)
            w = don * gnv
            do = (r * (w - oh * jnp.mean(w * oh, axis=-1, keepdims=True))).astype(BF16)

            qh = (q * e_qa).astype(BF16)
            kh = (k * e_ka).astype(BF16)
            qi = (q * e_b).astype(BF16)
            ko = (k * e_ko).astype(BF16)
            vb = i_ref[:, ln].astype(BF16)

            for s in range(tb // sb):
                sl = slice(s * sb, (s + 1) * sb)
                p = jnp.where(mask, _mm_nt(qh[sl], kh[sl]), 0.0).astype(BF16)
                dp = jnp.where(mask, _mm_nt(do[sl], vb[sl]), 0.0).astype(BF16)
                dv_scr[sl, ln] = _mm_tn(p, do[sl])
                dqh_scr[sl, ln] = _mm(dp, kh[sl])
                dkh_scr[sl, ln] = _mm_tn(dp, qh[sl])
            heads.append(dict(lb=lb, lb1=lb1, qr=qr, sq=sq, q=q, sg=sg, f=f, k=k, e_qa=e_qa, e_ka=e_ka, e_b=e_b,
                              e_ko=e_ko, dec=dec, do=do, qi=qi, ko=ko, vb=vb, ds=ds_scr[hh]))

        for c in reversed(range(nc)):
            sl = slice(c * CHUNK, (c + 1) * CHUNK)
            for hh, ln in enumerate(lanes):
                hd = heads[hh]
                ds = hd["ds"]
                st_c = st_ref[hh, c]
                dqi_scr[sl, ln] = _mm(hd["do"][sl], st_c)
                dko_scr[sl, ln] = _mm(hd["vb"][sl], ds)
                dv_scr[sl, ln] = dv_scr[sl, ln] + _mm_nt(hd["ko"][sl], ds)
                dd_scr[sl, ln] = jnp.broadcast_to(jnp.sum(ds * st_c, axis=0, keepdims=True), (CHUNK, HEAD_DIM))
                hd["ds"] = hd["dec"][c * CHUNK:c * CHUNK + 1, :] * ds + _mm_tn(hd["do"][sl], hd["qi"][sl])

        for hh, ln in enumerate(lanes):
            hd = heads[hh]
            ds_scr[hh] = hd["ds"]
            q, k, lb = hd["q"], hd["k"], hd["lb"]
            dko_e = dko_scr[:, ln] * hd["e_ko"]
            dq = dqh_scr[:, ln] * hd["e_qa"] + dqi_scr[:, ln] * hd["e_b"]
            dk = dkh_scr[:, ln] * hd["e_ka"] + dko_e
            kd3 = (k * dko_e).reshape(nc, CHUNK, HEAD_DIM)
            last = jnp.broadcast_to(jnp.sum(kd3, axis=1, keepdims=True), kd3.shape).reshape(tb, HEAD_DIM)
            db = q * dq - k * dk + jnp.where(pos == CHUNK - 1, hd["dec"] * dd_scr[:, ln] + last, 0.0)
            dlg = _chunk_rev_cumsum(db, pos)
            dfv = dlg / hd["f"] - dk
            s_low = jnp.sum(dfv * (1.0 - hd["sg"]), axis=0, keepdims=True)
            dlow_ref[0:1, ln] += s_low * lb * (1.0 - lb)
            dlow_ref[1:2, ln] += -s_low * lb * hd["lb1"]
            df_ref[:, ln] = (dfv * (1.0 - lb) * hd["sg"] * (1.0 - hd["sg"])).astype(BF16)
            dq_ref[:, ln] = (dq * Q_SCALE * (hd["sq"] * (1.0 + hd["qr"] * (1.0 - hd["sq"])))).astype(BF16)
            di_ref[:, ln] = dv_scr[:, ln].astype(BF16)

    rt = lambda t: nb - 1 - t
    col = lambda p: pl.BlockSpec((tb, wid), lambda h, t: (rt(t), p * HEAD_GROUPS + h))
    hcol = pl.BlockSpec((tb, wid), lambda h, t: (rt(t), h))
    piece = jax.ShapeDtypeStruct((T, HG_WIDTH), BF16)
    tile = pltpu.VMEM((tb, wid), F32)
    return pl.pallas_call(
        _drop_operands(body, 9, len(after)), name="hg_bwd", grid=(HEAD_GROUPS, nb),
        in_specs=[col(0), col(1), col(2), col(3), pl.BlockSpec((2, wid), lambda h, t: (0, h)),
                  pl.BlockSpec((1, HEAD_DIM), lambda h, t: (0, 0)), hcol, hcol,
                  pl.BlockSpec((HEADS_PER_STEP, nc, HEAD_DIM, HEAD_DIM), lambda h, t: (h, rt(t), 0, 0))]
                 + [HBM_SPEC] * len(after),
        out_specs=[hcol, hcol, hcol, hcol, pl.BlockSpec((2, wid), lambda h, t: (0, h)),
                   pl.BlockSpec((1, HEAD_DIM), lambda h, t: (0, 0))],
        out_shape=[piece, piece, piece, piece, jax.ShapeDtypeStruct((2, HG_WIDTH), F32),
                   jax.ShapeDtypeStruct((1, HEAD_DIM), F32)],
        scratch_shapes=[pltpu.VMEM((HEADS_PER_STEP, HEAD_DIM, HEAD_DIM), F32), tile, tile, tile, tile, tile, tile],
        compiler_params=_params(("arbitrary", "arbitrary")),
    )(hg, hg, hg, hg, low, gn, o, dog, st, *after)


def _in_bwd(dparts, w_in, x, dx1, g, after=()):
    T = x.shape[0]
    tm = min(512, T)
    widths = [p.shape[1] for p in dparts]
    offs = [sum(widths[:i]) for i in range(len(widths))]
    n = len(dparts)

    def body(*refs):
        d_refs = refs[:n]
        w_ref, x_ref, dx1_ref, g_ref, dx_ref, dgn_ref = refs[n:]

        @pl.when(pl.program_id(0) == 0)
        def _():
            dgn_ref[...] = jnp.zeros_like(dgn_ref)

        dh = None
        for d_ref, off, wd in zip(d_refs, offs, widths):
            part = _mm(d_ref[...], w_ref[off:off + wd, :])
            dh = part if dh is None else dh + part
        xv = x_ref[...]
        r = lax.rsqrt(jnp.mean(xv * xv, axis=-1, keepdims=True) + EPS)
        xh = xv * r
        dgn_ref[...] += jnp.sum(dh * xh, axis=0, keepdims=True)
        w = dh * g_ref[...]
        dx_ref[...] = dx1_ref[...] + r * (w - xh * jnp.mean(w * xh, axis=-1, keepdims=True))

    row = lambda m: pl.BlockSpec((tm, m), lambda i: (i, 0))
    return pl.pallas_call(
        _drop_operands(body, n + 4, len(after)), name="in_bwd", grid=(T // tm,),
        in_specs=[row(wd) for wd in widths] + [_resident(w_in.shape), row(D_MODEL), row(D_MODEL), _full((1, D_MODEL))]
                 + [HBM_SPEC] * len(after),
        out_specs=[row(D_MODEL), _full((1, D_MODEL))],
        out_shape=[jax.ShapeDtypeStruct((T, D_MODEL), F32), jax.ShapeDtypeStruct((1, D_MODEL), F32)],
        compiler_params=_params(("arbitrary",)),
    )(*dparts, w_in, x, dx1, g, *after)


def _wgrad(name, at, b, tn, transposed=False):
    M, T = at.shape
    N = b.shape[1]
    tk = min(2048, T)
    nk = T // tk

    def body(a_ref, b_ref, o_ref, acc):
        k = pl.program_id(1)
        part = _mm(a_ref[...], b_ref[...])

        @pl.when(k == 0)
        def _():
            acc[...] = part

        @pl.when(k != 0)
        def _():
            acc[...] += part

        @pl.when(k == nk - 1)
        def _():
            o_ref[...] = (acc[...].T if transposed else acc[...]).astype(BF16)

    if transposed:
        out_spec, out_shape = pl.BlockSpec((tn, M), lambda j, k: (j, 0)), (N, M)
    else:
        out_spec, out_shape = pl.BlockSpec((M, tn), lambda j, k: (0, j)), (M, N)
    return pl.pallas_call(
        body, name=name, grid=(N // tn, nk),
        in_specs=[pl.BlockSpec((M, tk), lambda j, k: (0, k)), pl.BlockSpec((tk, tn), lambda j, k: (k, j))],
        out_specs=out_spec, out_shape=jax.ShapeDtypeStruct(out_shape, BF16),
        scratch_shapes=[pltpu.VMEM((M, tn), F32)],
        compiler_params=_params(("parallel", "arbitrary")),
    )(at, b)


def _wgrad_in(ht, dparts, after=()):
    M, T = ht.shape
    tn = 512
    tk = min(2048, T)
    nk = T // tk
    nblk = [p.shape[1] // tn for p in dparts]
    start = [sum(nblk[:i]) for i in range(len(nblk))]
    n = len(dparts)

    def body(a_ref, *refs):
        d_refs, o_ref, acc = refs[:n], refs[n], refs[n + 1]
        j = pl.program_id(0)
        k = pl.program_id(1)

        @pl.when(k == 0)
        def _():
            acc[...] = jnp.zeros_like(acc)

        for d_ref, s, nb in zip(d_refs, start, nblk):
            @pl.when((j >= s) & (j < s + nb))
            def _():
                acc[...] += _mm(a_ref[...], d_ref[...])

        @pl.when(k == nk - 1)
        def _():
            o_ref[...] = acc[...].T.astype(BF16)

    def piece_spec(s, nb):
        def index(j, k):
            inside = (j >= s) & (j < s + nb)
            return jnp.where(inside, k, 0), jnp.clip(j - s, 0, nb - 1)
        return pl.BlockSpec((tk, tn), index)

    return pl.pallas_call(
        _drop_operands(body, 1 + n, len(after)), name="wgrad_in", grid=(sum(nblk), nk),
        in_specs=[pl.BlockSpec((M, tk), lambda j, k: (0, k))] + [piece_spec(s, nb) for s, nb in zip(start, nblk)]
                 + [HBM_SPEC] * len(after),
        out_specs=pl.BlockSpec((tn, M), lambda j, k: (j, 0)),
        out_shape=jax.ShapeDtypeStruct((sum(nblk) * tn, M), BF16),
        scratch_shapes=[pltpu.VMEM((M, tn), F32)],
        compiler_params=_params(("parallel", "arbitrary")),
    )(ht, *dparts, *after)


def _adamw_math(w, g, m, v):
    m = ADAM_B1 * m + (1.0 - ADAM_B1) * g
    v = ADAM_B2 * v + (1.0 - ADAM_B2) * (g * g)
    m_hat = m / (1.0 - ADAM_B1 ** ADAM_STEP)
    v_hat = v / (1.0 - ADAM_B2 ** ADAM_STEP)
    delta = -ADAM_LR * (m_hat / (jnp.sqrt(v_hat) + ADAM_EPS) + ADAM_WD * w)
    return delta, m, v


def _adamw_sum(name, w, parts, m, v):
    R, C = w.shape
    tr = _row_tile(R)

    def body(w_ref, p_ref, m_ref, v_ref, g_out, d_out, m_out, v_out):
        g = p_ref[0].astype(F32)
        for k in range(1, 4):
            g = g + p_ref[k].astype(F32)
        g_out[...] = g
        d_out[...], m_out[...], v_out[...] = _adamw_math(w_ref[...], g, m_ref[...], v_ref[...])

    blk = pl.BlockSpec((tr, C), lambda i: (i, 0))
    out = jax.ShapeDtypeStruct((R, C), F32)
    return pl.pallas_call(
        body, name=name, grid=(R // tr,),
        in_specs=[blk, pl.BlockSpec((4, tr, C), lambda i: (0, i, 0)), blk, blk],
        out_specs=[blk, blk, blk, blk], out_shape=[out, out, out, out],
        compiler_params=_params(("parallel",)),
    )(w, parts, m, v)


def _small_sum(gathered):
    R = gathered.shape[1]

    def body(p_ref, o_ref):
        g = p_ref[0]
        for k in range(1, N_DEV):
            g = g + p_ref[k]
        o_ref[...] = g

    return pl.pallas_call(
        body, name="small_sum", in_specs=[_full(gathered.shape)], out_specs=_full((R, 128)), grid=(1,),
        out_shape=jax.ShapeDtypeStruct((R, 128), F32),
    )(gathered)


def _small_adamw(w, g, m, v):
    def body(w_ref, g_ref, m_ref, v_ref, d_out, m_out, v_out):
        d_out[...], m_out[...], v_out[...] = _adamw_math(w_ref[...], g_ref[...], m_ref[...], v_ref[...])

    out = jax.ShapeDtypeStruct(w.shape, F32)
    spec = _full(w.shape)
    return pl.pallas_call(
        body, name="small_adamw", grid=(1,), in_specs=[spec] * 4, out_specs=[spec] * 3, out_shape=[out, out, out],
    )(w, g, m, v)


def _row_tile(rows):
    for cand in (256, 128):
        if rows % cand == 0:
            return cand
    return rows


def _pair_sum(name, by_owner, got, core, after=()):
    _, R, C = got.shape
    tr = _row_tile(R)

    def body(core_ref, a_ref, b_ref, o_ref):
        o_ref[...] = (a_ref[...].astype(F32) + b_ref[...].astype(F32)).astype(BF16)

    blk = pl.BlockSpec((None, tr, C), lambda k, i, core_ref: (k, i, 0))
    mine = pl.BlockSpec((None, tr, C), lambda k, i, core_ref: (2 * k + core_ref[0], i, 0))
    return pl.pallas_call(
        _drop_operands(body, 3, len(after)), name=name,
        grid_spec=pltpu.PrefetchScalarGridSpec(num_scalar_prefetch=1, grid=(4, R // tr),
                                               in_specs=[mine, blk] + [HBM_SPEC] * len(after), out_specs=blk),
        out_shape=jax.ShapeDtypeStruct(got.shape, BF16),
        compiler_params=_params(("parallel", "parallel")),
    )(core, by_owner, got, *after)


def _shards_from_cols(name, full):
    R, allc = full.shape
    c = allc // N_DEV
    tr = _row_tile(R)

    def body(f_ref, o_ref):
        for s in range(N_DEV):
            o_ref[s] = f_ref[:, s * c:(s + 1) * c]

    return pl.pallas_call(
        body, name=name, grid=(R // tr,),
        in_specs=[pl.BlockSpec((tr, allc), lambda i: (i, 0))],
        out_specs=pl.BlockSpec((N_DEV, tr, c), lambda i: (0, i, 0)),
        out_shape=jax.ShapeDtypeStruct((N_DEV, R, c), full.dtype),
        compiler_params=_params(("parallel",)),
    )(full)


MESH = pl.DeviceIdType.MESH
HBM_SPEC = pl.BlockSpec(memory_space=pl.ANY)


def _handshake(peers):
    barrier = pltpu.get_barrier_semaphore()
    for peer in peers:
        pl.semaphore_signal(barrier, inc=1, device_id=peer, device_id_type=MESH)
    pl.semaphore_wait(barrier, len(peers))


def _comm_call(body, name, operands, out_shape, scratch, collective_id):
    if collective_id is None:
        return pl.pallas_call(body, name=name, in_specs=[HBM_SPEC] * len(operands), out_specs=[HBM_SPEC] * len(out_shape),
                              out_shape=out_shape, scratch_shapes=scratch)(*operands)
    return pl.kernel(body, out_type=out_shape, mesh=plsc.ScalarSubcoreMesh(axis_name="sequencer", num_cores=1),
                     scratch_types=scratch, name=name,
                     compiler_params=pltpu.CompilerParams(collective_id=collective_id))(*operands)


def _all_gather(name, blocks, collective_id=None, after=()):
    n = len(blocks)
    na = len(after)

    def body(*refs):
        x_refs, out_refs = refs[:n], refs[n + na:2 * n + na]
        send_sems, recv_sems, local_sems = refs[2 * n + na:]
        x, y, c = lax.axis_index("x"), lax.axis_index("y"), lax.axis_index("c")
        me, sibling = (x, y, c), (x, y, 1 - c)
        chips = [(1 - x, y), (x, 1 - y), (1 - x, 1 - y)]
        if collective_id is not None:
            _handshake([sibling] + [(*chip, c) for chip in chips])

        def slot(i, px, py, pc):
            return out_refs[i].at[4 * px + 2 * py + pc]

        def copy(i, k, blk, to, src=None):
            return pltpu.make_async_remote_copy(
                src_ref=slot(i, *blk) if src is None else src, dst_ref=slot(i, *blk),
                send_sem=send_sems.at[7 * i + k], recv_sem=recv_sems.at[7 * i + k], device_id=to, device_id_type=MESH)

        mine = [pltpu.make_async_copy(x_refs[i], slot(i, *me), local_sems.at[i]) for i in range(n)]
        for cp in mine:
            cp.start()
        first = []
        for i in range(n):
            first.append(copy(i, 0, me, sibling, src=x_refs[i]))
            first += [copy(i, 1 + j, me, (*chip, c), src=x_refs[i]) for j, chip in enumerate(chips)]
        for cp in first:
            cp.start()
        passed = []
        for i in range(n):
            for j, chip in enumerate(chips):
                copy(i, 1 + j, (*chip, c), me).wait_recv()
                passed.append(copy(i, 4 + j, (*chip, c), sibling))
                passed[-1].start()
        for i in range(n):
            copy(i, 0, sibling, me).wait_recv()
            for j, chip in enumerate(chips):
                copy(i, 4 + j, (*chip, 1 - c), me).wait_recv()
        for cp in first + passed:
            cp.wait_send()
        for cp in mine:
            cp.wait()

    return _comm_call(
        body, name, list(blocks) + list(after), [jax.ShapeDtypeStruct((N_DEV,) + b.shape, b.dtype) for b in blocks],
        [pltpu.SemaphoreType.DMA((7 * n,)), pltpu.SemaphoreType.DMA((7 * n,)), pltpu.SemaphoreType.DMA((n,))],
        collective_id)


def _sibling_swap(name, by_owner, collective_id=None, after=()):
    n = len(by_owner)
    na = len(after)

    def body(*refs):
        x_refs, out_refs = refs[:n], refs[n + na:2 * n + na]
        send_sems, recv_sems = refs[2 * n + na:]
        x, y, c = lax.axis_index("x"), lax.axis_index("y"), lax.axis_index("c")
        if collective_id is not None:
            _handshake([(x, y, 1 - c)])
        copies = []
        for i in range(n):
            for k in range(4):
                copies.append(pltpu.make_async_remote_copy(
                    src_ref=x_refs[i].at[2 * k + 1 - c], dst_ref=out_refs[i].at[k],
                    send_sem=send_sems.at[4 * i + k], recv_sem=recv_sems.at[4 * i + k],
                    device_id=(x, y, 1 - c), device_id_type=MESH))
        for cp in copies:
            cp.start()
        for cp in copies:
            cp.wait()

    return _comm_call(
        body, name, list(by_owner) + list(after),
        [jax.ShapeDtypeStruct((4,) + b.shape[1:], b.dtype) for b in by_owner],
        [pltpu.SemaphoreType.DMA((4 * n,)), pltpu.SemaphoreType.DMA((4 * n,))], collective_id)


def _chip_exchange(name, sums, collective_id=None, after=()):
    n = len(sums)
    na = len(after)

    def body(*refs):
        x_refs, out_refs = refs[:n], refs[n + na:2 * n + na]
        send_sems, recv_sems, local_sems = refs[2 * n + na:]
        x, y, c = lax.axis_index("x"), lax.axis_index("y"), lax.axis_index("c")
        chips = [(1 - x, y), (x, 1 - y), (1 - x, 1 - y)]
        my_chip = 2 * x + y
        if collective_id is not None:
            _handshake([(cx, cy, c) for cx, cy in chips])
        mine = [pltpu.make_async_copy(x_refs[i].at[my_chip], out_refs[i].at[my_chip], local_sems.at[i])
                for i in range(n)]
        for cp in mine:
            cp.start()
        sends = []
        for i in range(n):
            for j, (cx, cy) in enumerate(chips):
                sends.append(pltpu.make_async_remote_copy(
                    src_ref=x_refs[i].at[2 * cx + cy], dst_ref=out_refs[i].at[my_chip],
                    send_sem=send_sems.at[3 * i + j], recv_sem=recv_sems.at[3 * i + j],
                    device_id=(cx, cy, c), device_id_type=MESH))
        for cp in sends:
            cp.start()
        for i in range(n):
            for j, (cx, cy) in enumerate(chips):
                pltpu.make_async_remote_copy(
                    src_ref=x_refs[i].at[my_chip], dst_ref=out_refs[i].at[2 * cx + cy],
                    send_sem=send_sems.at[3 * i + j], recv_sem=recv_sems.at[3 * i + j],
                    device_id=(cx, cy, c), device_id_type=MESH).wait_recv()
        for cp in sends:
            cp.wait_send()
        for cp in mine:
            cp.wait()

    return _comm_call(
        body, name, list(sums) + list(after), [jax.ShapeDtypeStruct(s.shape, s.dtype) for s in sums],
        [pltpu.SemaphoreType.DMA((3 * n,)), pltpu.SemaphoreType.DMA((3 * n,)), pltpu.SemaphoreType.DMA((n,))],
        collective_id)


def _cast_shards(shards):
    n = len(shards)

    def body(*refs):
        for i in range(n):
            refs[n + i][...] = refs[i][...].astype(BF16)

    vmem = pl.BlockSpec(memory_space=pltpu.VMEM)
    return pl.pallas_call(
        body, name="cast_shards", in_specs=[vmem] * n, out_specs=[vmem] * n,
        out_shape=[jax.ShapeDtypeStruct(s.shape, BF16) for s in shards],
        compiler_params=pltpu.CompilerParams(vmem_limit_bytes=VMEM_LIMIT_V7X),
    )(*shards)


BIG = ("w_in", "w_branch_a", "w_branch_b", "w_out", "w_ffn_gate", "w_ffn_up", "w_ffn_down")


def _local_step(x, target, gains, low, conv_w, wg8, reduce):
    g_mix, g_hg, g_ffn, g_fin = gains
    w_in = wg8["w_in"].reshape(N_IN, D_MODEL)
    wg = wg8["w_ffn_gate"].reshape(D_FF, D_MODEL)
    wu = wg8["w_ffn_up"].reshape(D_FF, D_MODEL)
    wa, wb = wg8["w_branch_a"], wg8["w_branch_b"]
    wo = wg8["w_out"].reshape(D_MODEL, D_MODEL)
    wd = wg8["w_ffn_down"].reshape(D_FF, D_MODEL)

    ht, hg, cv, gt = _fwd_in(x, g_mix, w_in)
    o, og, ogt, st = _hg_fwd(hg, low, g_hg)
    cvo, cvot = _conv_fwd(cv, conv_w)
    x1, mgt = _merge_fwd(og, cvo, gt, x, wa, wb, wo)
    h2t, gate, up, actt, x2 = _ffn_fwd(x1, g_ffn, wg, wu, wd)
    loss, d_gfin, dx2 = _final_fwd_bwd(x2, target, g_fin)

    dgate, dup, dx1, d_gffn = _ffn_bwd(dx2, x1, gate, up, g_ffn, wg, wu, wd)
    ffn = dict(
        w_ffn_down=_wgrad("wgrad_ffn_down", actt, dx2, 512).reshape(N_DEV, D_FF // N_DEV, D_MODEL),
        w_ffn_gate=_wgrad("wgrad_ffn_gate", h2t, dgate, 1408, transposed=True).reshape(N_DEV, D_FF // N_DEV, D_MODEL),
        w_ffn_up=_wgrad("wgrad_ffn_up", h2t, dup, 1408, transposed=True).reshape(N_DEV, D_FF // N_DEV, D_MODEL))
    sums_ffn, got_ffn = reduce.begin(ffn)
    dgt, dya, dyb, dog, dcvo = _merge_bwd(dx1, og, cvo, gt, wa, wb, wo)
    out = dict(
        w_out=_wgrad("wgrad_out", mgt, dx1, 512).reshape(N_DEV, D_MODEL // N_DEV, D_MODEL),
        w_branch_a=_shards_from_cols("split_w_branch_a", _wgrad("wgrad_branch_a", ogt, dya, 512)),
        w_branch_b=_shards_from_cols("split_w_branch_b", _wgrad("wgrad_branch_b", cvot, dyb, 512)))
    sums_out, got_out = reduce.begin(out, after=got_ffn[:1])
    parts_ffn, _ = reduce.finish(ffn, sums_ffn, after=got_out[:1])
    dq, df, di, dg, d_low, d_ghg = _hg_bwd(dog, hg, o, st, low, g_hg, after=list(sums_ffn) + list(sums_out))
    dc, db, dxb, d_conv = _conv_bwd(dcvo, cv, conv_w, after=[dq])
    parts_out, updated_out = reduce.finish(out, sums_out, after=[parts_ffn[0], dc])
    dparts = [dq, df, di, dg, dc, db, dxb, dgt]
    w_in_grad = dict(w_in=_wgrad_in(ht, dparts, after=parts_ffn[:1]).reshape(N_DEV, N_IN // N_DEV, D_MODEL))
    sums_in, _ = reduce.begin(w_in_grad, after=parts_out[:1], sum_after=updated_out)
    parts_in, _ = reduce.finish(w_in_grad, sums_in)
    grad_x, d_gmix = _in_bwd(dparts, w_in, x, dx1, g_mix, after=list(parts_out[:1]) + list(sums_in))
    small = dict(norm_mix_g=d_gmix, norm_ffn_g=d_gffn, norm_final_g=d_gfin, lower_bounds=d_low, hg_norm_g=d_ghg,
                 conv_w=d_conv, loss=loss)
    return grad_x, small, parts_in


_SMALL_LAYOUT = (("norm_mix_g", 0, 8), ("norm_ffn_g", 8, 8), ("norm_final_g", 16, 8), ("lower_bounds", 24, 8),
                 ("hg_norm_g", 32, 1))
_LOSS_ROW = 40
_CONV_ROW = 48


def _pad_rows(a, rows):
    return jnp.pad(a, ((0, rows - a.shape[0]), (0, 0)))


def _pack_small(vals, conv_rows):
    parts = [_pad_rows(vals[name].reshape(rows, 128), 8) for name, _, rows in _SMALL_LAYOUT]
    loss = vals["loss"][:, :128] if "loss" in vals else jnp.zeros((1, 128), F32)
    parts.append(_pad_rows(loss, 8))
    parts.append(_pad_rows(conv_rows, SMALL_ROWS - _CONV_ROW))
    return jnp.concatenate(parts, axis=0)


def _conv_shard_rows(a):
    return jnp.pad(a, ((0, 5), (0, 64)))


def kernel(x, norm_mix_g, w_in, lower_bounds, hg_norm_g, conv_w, w_branch_a, w_branch_b, w_out, norm_ffn_g, w_ffn_gate, w_ffn_up, w_ffn_down, norm_final_g, loss_target, m_norm_mix_g, m_w_in, m_lower_bounds, m_hg_norm_g, m_conv_w, m_w_branch_a, m_w_branch_b, m_w_out, m_norm_ffn_g, m_w_ffn_gate, m_w_ffn_up, m_w_ffn_down, m_norm_final_g, v_norm_mix_g, v_w_in, v_lower_bounds, v_hg_norm_g, v_conv_w, v_w_branch_a, v_w_branch_b, v_w_out, v_norm_ffn_g, v_w_ffn_gate, v_w_ffn_up, v_w_ffn_down, v_norm_final_g):
    cx, cy, cc = lax.axis_index("x"), lax.axis_index("y"), lax.axis_index("c")
    my_dev = 4 * cx + 2 * cy + cc

    def tr(a):
        return a[0].T

    big = dict(w_in=tr(w_in), w_branch_a=w_branch_a[0], w_branch_b=w_branch_b[0], w_out=w_out[0],
               w_ffn_gate=tr(w_ffn_gate), w_ffn_up=tr(w_ffn_up), w_ffn_down=w_ffn_down[0])
    big_m = dict(w_in=tr(m_w_in), w_branch_a=m_w_branch_a[0], w_branch_b=m_w_branch_b[0], w_out=m_w_out[0],
                 w_ffn_gate=tr(m_w_ffn_gate), w_ffn_up=tr(m_w_ffn_up), w_ffn_down=m_w_ffn_down[0])
    big_v = dict(w_in=tr(v_w_in), w_branch_a=v_w_branch_a[0], w_branch_b=v_w_branch_b[0], w_out=v_w_out[0],
                 w_ffn_gate=tr(v_w_ffn_gate), w_ffn_up=tr(v_w_ffn_up), w_ffn_down=v_w_ffn_down[0])
    transposed = ("w_in", "w_ffn_gate", "w_ffn_up")

    shards = dict(zip(BIG, _cast_shards([big[n] for n in BIG])))
    first = _all_gather("gather_w_in", [shards["w_in"], _conv_shard_rows(conv_w[0])])
    later = _all_gather("gather_rest", [shards[n] for n in BIG[1:]], collective_id=1, after=first[1:])
    wg8 = dict(zip(BIG, [first[0]] + list(later)))
    conv_full = first[1][:, :3, :64].transpose(1, 0, 2).reshape(3, CONV_WIDTH)

    core = cc.reshape(1).astype(jnp.int32)
    outs = {}
    ids = iter(range(2, 16))

    class Reduce:
        @staticmethod
        def begin(grads, after=(), sum_after=()):
            names = list(grads)
            by_owner = [grads[n] for n in names]
            got = _sibling_swap("sibling_swap_" + names[0], by_owner, collective_id=next(ids), after=after)
            sums = [_pair_sum("pair_sum_" + n, a, b, core, after=sum_after) for n, a, b in zip(names, by_owner, got)]
            return sums, got

        @staticmethod
        def finish(grads, chip_sums, after=()):
            names = list(grads)
            parts = _chip_exchange("chip_exchange_" + names[0], chip_sums, collective_id=next(ids), after=after)
            for n, p in zip(names, parts):
                outs[n] = _adamw_sum("adamw_" + n, big[n], p, big_m[n], big_v[n])
            return parts, [outs[n][1] for n in names]

    gains = (norm_mix_g, hg_norm_g, norm_ffn_g, norm_final_g.reshape(1, D_MODEL))
    grad_x, small, last = _local_step(x[0], loss_target[0], gains, lower_bounds, conv_full, wg8, Reduce)

    small_all = _all_gather("gather_small", [_pack_small(small, small["conv_w"].reshape(12, 128))],
                            collective_id=next(ids), after=last[:1])
    ssum = _small_sum(small_all[0])
    conv_g_full = ssum[_CONV_ROW:_CONV_ROW + 12].reshape(3, CONV_WIDTH)
    conv_g = lax.dynamic_slice_in_dim(conv_g_full, my_dev * 64, 64, axis=1)
    loss = ssum[_LOSS_ROW, 0]
    g_rows = jnp.concatenate([ssum[:_CONV_ROW], _pad_rows(_conv_shard_rows(conv_g), SMALL_ROWS - _CONV_ROW)], axis=0)

    def pack_state(a):
        vals = dict(norm_mix_g=a[0], norm_ffn_g=a[1], norm_final_g=a[2], lower_bounds=a[3], hg_norm_g=a[4])
        return _pack_small(vals, _conv_shard_rows(a[5][0]))

    sw = pack_state((norm_mix_g, norm_ffn_g, norm_final_g, lower_bounds, hg_norm_g, conv_w))
    sm = pack_state((m_norm_mix_g, m_norm_ffn_g, m_norm_final_g, m_lower_bounds, m_hg_norm_g, m_conv_w))
    sv = pack_state((v_norm_mix_g, v_norm_ffn_g, v_norm_final_g, v_lower_bounds, v_hg_norm_g, v_conv_w))
    s_delta, s_m, s_v = _small_adamw(sw, g_rows, sm, sv)

    shapes = dict(norm_mix_g=(1, D_MODEL), norm_ffn_g=(1, D_MODEL), norm_final_g=(D_MODEL,),
                  lower_bounds=(2, HG_WIDTH), hg_norm_g=(1, HEAD_DIM))

    def unpack(buf, name):
        if name == "conv_w":
            return buf[_CONV_ROW:_CONV_ROW + 3, :64].reshape(1, 3, 64)
        for nm, off, rows in _SMALL_LAYOUT:
            if nm == name:
                return buf[off:off + rows].reshape(shapes[name])
        raise KeyError(name)

    order = ["norm_mix_g", "w_in", "lower_bounds", "hg_norm_g", "conv_w", "w_branch_a", "w_branch_b", "w_out",
             "norm_ffn_g", "w_ffn_gate", "w_ffn_up", "w_ffn_down", "norm_final_g"]
    result = [loss, grad_x[None]]
    for k, sbuf in enumerate((g_rows, s_delta, s_m, s_v)):
        for n in order:
            if n in outs:
                result.append((outs[n][k].T if n in transposed else outs[n][k])[None])
            else:
                result.append(unpack(sbuf, n))
    return tuple(result)
```

```python
import jax
import jax.numpy as jnp
from jax import lax
from jax.experimental import pallas as pl
from jax.experimental.pallas import tpu as pltpu
from jax.experimental.pallas import tpu_sc as plsc

F32 = jnp.float32
BF16 = jnp.bfloat16
STASH = jnp.bfloat16

D_MODEL = 1024
HG_WIDTH = 512
HEAD_DIM = 128
N_HEADS = 4
HEADS_PER_STEP = 4
HEAD_GROUPS = N_HEADS // HEADS_PER_STEP
CONV_WIDTH = 512
D_FF = 2816
CHUNK = 32
EPS = 1e-6
Q_SCALE = HEAD_DIM ** -0.5
N_DEV = 8

ADAM_LR = 0.001
ADAM_B1 = 0.9
ADAM_B2 = 0.999
ADAM_EPS = 1e-08
ADAM_WD = 0.01
ADAM_STEP = 10

VMEM_LIMIT_V7X = 56 * 1024 * 1024

SMALL_ROWS = 64


def _params(sem, vmem=VMEM_LIMIT_V7X):
    return pltpu.CompilerParams(dimension_semantics=sem, vmem_limit_bytes=vmem)


def _mm(a, b):
    return jnp.dot(a.astype(BF16), b.astype(BF16), preferred_element_type=F32)


def _mm_nt(a, b):
    return lax.dot_general(a.astype(BF16), b.astype(BF16), (((1,), (1,)), ((), ())), preferred_element_type=F32)


def _mm_tn(a, b):
    return lax.dot_general(a.astype(BF16), b.astype(BF16), (((0,), (0,)), ((), ())), preferred_element_type=F32)


def _sigmoid(x):
    return 0.5 * jnp.tanh(0.5 * x) + 0.5


def _resident(shape):
    nd = len(shape)
    return pl.BlockSpec(shape, lambda *_: (0,) * nd, pipeline_mode=pl.Buffered(1))


def _full(shape):
    nd = len(shape)
    return pl.BlockSpec(shape, lambda *_: (0,) * nd)


def _shard_cols(w_ref):
    return jnp.concatenate([w_ref[s] for s in range(N_DEV)], axis=1)


N_HG = 4 * HG_WIDTH
N_CV = 3 * CONV_WIDTH
N_GT = 2 * D_MODEL
N_IN = N_HG + N_CV + N_GT


def _col(tm, n):
    return pl.BlockSpec((n, tm), lambda i: (0, i))


def _fwd_in(x, g, w_in_t):
    T = x.shape[0]
    tm = min(512, T)

    def body(x_ref, g_ref, w_ref, ht_ref, hg_ref, cv_ref, gt_ref):
        xv = x_ref[...]
        r = lax.rsqrt(jnp.mean(xv * xv, axis=-1, keepdims=True) + EPS)
        hf = xv * r * g_ref[...]
        h = hf.astype(BF16)
        ht_ref[...] = hf.T.astype(BF16)
        hg_ref[...] = _mm_nt(h, w_ref[:N_HG, :])
        cv_ref[...] = _mm_nt(h, w_ref[N_HG:N_HG + N_CV, :]).astype(STASH)
        gt_ref[...] = _mm_nt(h, w_ref[N_HG + N_CV:, :]).astype(STASH)

    row = lambda n: pl.BlockSpec((tm, n), lambda i: (i, 0))
    return pl.pallas_call(
        body, name="fwd_in", grid=(T // tm,),
        in_specs=[row(D_MODEL), _full((1, D_MODEL)), _resident(w_in_t.shape)],
        out_specs=[_col(tm, D_MODEL), row(N_HG), row(N_CV), row(N_GT)],
        out_shape=[jax.ShapeDtypeStruct((D_MODEL, T), BF16), jax.ShapeDtypeStruct((T, N_HG), F32),
                   jax.ShapeDtypeStruct((T, N_CV), STASH), jax.ShapeDtypeStruct((T, N_GT), STASH)],
        compiler_params=_params(("parallel",)),
    )(x, g, w_in_t)


def _chunk_pos(shape):
    return lax.broadcasted_iota(jnp.int32, shape, 0) & (CHUNK - 1)


def _chunk_cumsum(x, pos):
    s = 1
    while s < CHUNK:
        x = x + jnp.where(pos >= s, pltpu.roll(x, s, 0), 0.0)
        s *= 2
    return x


def _chunk_rev_cumsum(x, pos):
    n = x.shape[0]
    s = 1
    while s < CHUNK:
        x = x + jnp.where(pos + s < CHUNK, pltpu.roll(x, n - s, 0), 0.0)
        s *= 2
    return x


def _chunk_bcast(x3, row, tb):
    return jnp.broadcast_to(x3[:, row:row + 1, :], x3.shape).reshape(tb, x3.shape[-1])


def _lower_bound(low_ref):
    l0 = low_ref[0:1, :]
    l1 = low_ref[1:2, :]
    m = jnp.maximum(l0, l1)
    e0 = jnp.exp(l0 - m)
    e1 = jnp.exp(l1 - m)
    return e0 / (e0 + e1), e1 / (e0 + e1)


def _hg_gates(qr, fr, lb, pos, tb):
    sq = _sigmoid(qr)
    q = qr * sq * Q_SCALE
    sg = _sigmoid(fr)
    f = lb + (1.0 - lb) * sg
    k = 1.0 - f
    b = _chunk_cumsum(jnp.log(f), pos)
    b3 = b.reshape(tb // CHUNK, CHUNK, HEAD_DIM)
    anc = _chunk_bcast(b3, CHUNK // 2 - 1, tb)
    blb = _chunk_bcast(b3, CHUNK - 1, tb)
    e_qa = jnp.exp(b - anc)
    e_ka = jnp.exp(anc - b)
    e_b = jnp.exp(b)
    e_ko = jnp.exp(blb - b)
    dec = jnp.exp(blb)
    return sq, q, sg, f, k, e_qa, e_ka, e_b, e_ko, dec


def _intra_mask(sb):
    r = lax.broadcasted_iota(jnp.int32, (sb, sb), 0)
    c = lax.broadcasted_iota(jnp.int32, (sb, sb), 1)
    return ((r // CHUNK) == (c // CHUNK)) & (c <= r)


def _hg_fwd(hg, low, gn):
    T = hg.shape[0]
    tb = min(512, T)
    sb = min(256, tb)
    nb = T // tb
    nc = tb // CHUNK
    wid = HEADS_PER_STEP * HEAD_DIM

    def body(q_ref, f_ref, i_ref, g_ref, low_ref, gn_ref, o_ref, og_ref, ogt_ref, st_ref, s_scr):
        t = pl.program_id(1)

        @pl.when(t == 0)
        def _():
            s_scr[...] = jnp.zeros_like(s_scr)

        pos = _chunk_pos((tb, HEAD_DIM))
        mask = _intra_mask(sb)
        lanes = [slice(hh * HEAD_DIM, (hh + 1) * HEAD_DIM) for hh in range(HEADS_PER_STEP)]
        qi, ko, vb, dec, st = [], [], [], [], []
        for hh, ln in enumerate(lanes):
            lb, _ = _lower_bound(low_ref.at[:, ln])
            _, q, _, _, k, e_qa, e_ka, e_b, e_ko, dec_h = _hg_gates(q_ref[:, ln], f_ref[:, ln], lb, pos, tb)
            qh = (q * e_qa).astype(BF16)
            kh = (k * e_ka).astype(BF16)
            qi.append((q * e_b).astype(BF16))
            ko.append((k * e_ko).astype(BF16))
            vb.append(i_ref[:, ln].astype(BF16))
            dec.append(dec_h)
            st.append(s_scr[hh])
            for s in range(tb // sb):
                sl = slice(s * sb, (s + 1) * sb)
                p = jnp.where(mask, _mm_nt(qh[sl], kh[sl]), 0.0)
                o_ref[sl, ln] = _mm(p, vb[hh][sl])
        for c in range(nc):
            sl = slice(c * CHUNK, (c + 1) * CHUNK)
            for hh, ln in enumerate(lanes):
                st_ref[hh, c] = st[hh]
                o_ref[sl, ln] = o_ref[sl, ln] + _mm_nt(qi[hh][sl], st[hh])
                st[hh] = dec[hh][c * CHUNK:c * CHUNK + 1, :] * st[hh] + _mm_tn(vb[hh][sl], ko[hh][sl])
        for hh, ln in enumerate(lanes):
            s_scr[hh] = st[hh]
            o = o_ref[:, ln]
            r = lax.rsqrt(jnp.mean(o * o, axis=-1, keepdims=True) + EPS)
            gr = g_ref[:, ln]
            og = (o * r * gn_ref[...]) * (gr * _sigmoid(gr))
            og_ref[:, ln] = og.astype(BF16)
            ogt_ref[ln, :] = og.T.astype(BF16)

    col = lambda p: pl.BlockSpec((tb, wid), lambda h, t: (t, p * HEAD_GROUPS + h))
    hcol = pl.BlockSpec((tb, wid), lambda h, t: (t, h))
    return pl.pallas_call(
        body, name="hg_fwd", grid=(HEAD_GROUPS, nb),
        in_specs=[col(0), col(1), col(2), col(3), pl.BlockSpec((2, wid), lambda h, t: (0, h)),
                  pl.BlockSpec((1, HEAD_DIM), lambda h, t: (0, 0))],
        out_specs=[hcol, hcol, pl.BlockSpec((wid, tb), lambda h, t: (h, t)),
                   pl.BlockSpec((HEADS_PER_STEP, nc, HEAD_DIM, HEAD_DIM), lambda h, t: (h, t, 0, 0))],
        out_shape=[jax.ShapeDtypeStruct((T, HG_WIDTH), F32), jax.ShapeDtypeStruct((T, HG_WIDTH), BF16),
                   jax.ShapeDtypeStruct((HG_WIDTH, T), BF16),
                   jax.ShapeDtypeStruct((N_HEADS, T // CHUNK, HEAD_DIM, HEAD_DIM), F32)],
        scratch_shapes=[pltpu.VMEM((HEADS_PER_STEP, HEAD_DIM, HEAD_DIM), F32)],
        compiler_params=_params(("parallel", "arbitrary")),
    )(hg, hg, hg, hg, low, gn)


def _conv_fwd(cv, conv_w):
    T = cv.shape[0]
    nj = CONV_WIDTH // 128

    def body(c_ref, b_ref, x_ref, w_ref, o_ref, ot_ref):
        row = lax.broadcasted_iota(jnp.int32, (T, 128), 0)
        u = c_ref[...].astype(F32) * x_ref[...].astype(F32)
        u1 = jnp.where(row >= 1, pltpu.roll(u, 1, 0), 0.0)
        u2 = jnp.where(row >= 2, pltpu.roll(u, 2, 0), 0.0)
        y = w_ref[0:1, :] * u2 + w_ref[1:2, :] * u1 + w_ref[2:3, :] * u
        out = b_ref[...].astype(F32) * y
        o_ref[...] = out.astype(BF16)
        ot_ref[...] = out.T.astype(BF16)

    col = lambda p: pl.BlockSpec((T, 128), lambda j: (0, p * nj + j))
    return pl.pallas_call(
        body, name="conv_fwd", grid=(nj,),
        in_specs=[col(0), col(1), col(2), pl.BlockSpec((3, 128), lambda j: (0, j))],
        out_specs=[pl.BlockSpec((T, 128), lambda j: (0, j)), pl.BlockSpec((128, T), lambda j: (j, 0))],
        out_shape=[jax.ShapeDtypeStruct((T, CONV_WIDTH), BF16), jax.ShapeDtypeStruct((CONV_WIDTH, T), BF16)],
        compiler_params=_params(("parallel",)),
    )(cv, cv, cv, conv_w)


def _merge_fwd(og, cvo, gt, x, wa, wb, wo):
    T = x.shape[0]
    tm = min(512, T)

    def body(og_ref, cvo_ref, gt_ref, x_ref, wa_ref, wb_ref, wo_ref, x1_ref, mgt_ref):
        ya = jnp.dot(og_ref[...], _shard_cols(wa_ref), preferred_element_type=F32)
        yb = jnp.dot(cvo_ref[...], _shard_cols(wb_ref), preferred_element_type=F32)
        m = (_sigmoid(gt_ref[:, :D_MODEL].astype(F32)) * ya
             + _sigmoid(gt_ref[:, D_MODEL:].astype(F32)) * yb)
        mgt_ref[...] = m.T.astype(BF16)
        x1_ref[...] = x_ref[...] + jnp.dot(m.astype(BF16), wo_ref[...], preferred_element_type=F32)

    row = lambda n: pl.BlockSpec((tm, n), lambda i: (i, 0))
    return pl.pallas_call(
        body, name="merge_fwd", grid=(T // tm,),
        in_specs=[row(HG_WIDTH), row(CONV_WIDTH), row(2 * D_MODEL), row(D_MODEL),
                  _resident(wa.shape), _resident(wb.shape), _resident(wo.shape)],
        out_specs=[row(D_MODEL), _col(tm, D_MODEL)],
        out_shape=[jax.ShapeDtypeStruct((T, D_MODEL), F32), jax.ShapeDtypeStruct((D_MODEL, T), BF16)],
        compiler_params=_params(("parallel",)),
    )(og, cvo, gt, x, wa, wb, wo)


def _ffn_fwd_loss(x1, g, wg, wu, wd, target, g_fin):
    T = x1.shape[0]
    tm = min(256, T)

    def body(x_ref, g_ref, wg_ref, wu_ref, wd_ref, t_ref, gf_ref,
             ht_ref, gate_ref, up_ref, actt_ref, loss_ref, dgf_ref, dx2_ref):
        @pl.when(pl.program_id(0) == 0)
        def _():
            loss_ref[...] = jnp.zeros_like(loss_ref)
            dgf_ref[...] = jnp.zeros_like(dgf_ref)

        xv = x_ref[...]
        r = lax.rsqrt(jnp.mean(xv * xv, axis=-1, keepdims=True) + EPS)
        hf = xv * r * g_ref[...]
        h = hf.astype(BF16)
        ht_ref[...] = hf.T.astype(BF16)
        gate = _mm_nt(h, wg_ref[...])
        up = _mm_nt(h, wu_ref[...])
        gate_ref[...] = gate.astype(STASH)
        up_ref[...] = up.astype(STASH)
        act = gate * _sigmoid(gate) * up
        actt_ref[...] = act.T.astype(BF16)
        x2 = xv + jnp.dot(act.astype(BF16), wd_ref[...], preferred_element_type=F32)

        gv = gf_ref[...]
        r2 = lax.rsqrt(jnp.mean(x2 * x2, axis=-1, keepdims=True) + EPS)
        xh = x2 * r2
        err = xh * gv - t_ref[...]
        loss_ref[...] += 0.5 * jnp.sum(jnp.mean(err * err, axis=-1, keepdims=True), axis=0, keepdims=True)
        dy = err * (1.0 / D_MODEL)
        dgf_ref[...] += jnp.sum(dy * xh, axis=0, keepdims=True)
        w = dy * gv
        dx2_ref[...] = r2 * (w - xh * jnp.mean(w * xh, axis=-1, keepdims=True))

    row = lambda n: pl.BlockSpec((tm, n), lambda i: (i, 0))
    return pl.pallas_call(
        body, name="ffn_fwd_loss", grid=(T // tm,),
        in_specs=[row(D_MODEL), _full((1, D_MODEL)), _resident(wg.shape), _resident(wu.shape), _resident(wd.shape),
                  row(D_MODEL), _full((1, D_MODEL))],
        out_specs=[_col(tm, D_MODEL), row(D_FF), row(D_FF), _col(tm, D_FF), _full((1, 128)), _full((1, D_MODEL)),
                   row(D_MODEL)],
        out_shape=[jax.ShapeDtypeStruct((D_MODEL, T), BF16), jax.ShapeDtypeStruct((T, D_FF), STASH),
                   jax.ShapeDtypeStruct((T, D_FF), STASH), jax.ShapeDtypeStruct((D_FF, T), BF16),
                   jax.ShapeDtypeStruct((1, 128), F32), jax.ShapeDtypeStruct((1, D_MODEL), F32),
                   jax.ShapeDtypeStruct((T, D_MODEL), F32)],
        compiler_params=_params(("arbitrary",)),
    )(x1, g, wg, wu, wd, target, g_fin)


def _ffn_bwd(dx2, x1, gate, up, g, wg, wu, wd):
    T = x1.shape[0]
    tm = min(256, T)

    def body(dx2_ref, x_ref, gate_ref, up_ref, g_ref, wg_ref, wu_ref, wd_ref, dgate_ref, dup_ref, dx1_ref, dgn_ref):
        @pl.when(pl.program_id(0) == 0)
        def _():
            dgn_ref[...] = jnp.zeros_like(dgn_ref)

        dx2 = dx2_ref[...]
        dact = _mm_nt(dx2, wd_ref[...])
        gate = gate_ref[...].astype(F32)
        s = _sigmoid(gate)
        dgate = (dact * up_ref[...].astype(F32) * (s * (1.0 + gate * (1.0 - s)))).astype(BF16)
        dup = (dact * (gate * s)).astype(BF16)
        dgate_ref[...] = dgate
        dup_ref[...] = dup
        dh = _mm(dgate, wg_ref[...]) + _mm(dup, wu_ref[...])
        xv = x_ref[...]
        r = lax.rsqrt(jnp.mean(xv * xv, axis=-1, keepdims=True) + EPS)
        xh = xv * r
        dgn_ref[...] += jnp.sum(dh * xh, axis=0, keepdims=True)
        w = dh * g_ref[...]
        dx1_ref[...] = dx2 + r * (w - xh * jnp.mean(w * xh, axis=-1, keepdims=True))

    row = lambda n: pl.BlockSpec((tm, n), lambda i: (i, 0))
    return pl.pallas_call(
        body, name="ffn_bwd", grid=(T // tm,),
        in_specs=[row(D_MODEL), row(D_MODEL), row(D_FF), row(D_FF), _full((1, D_MODEL)),
                  _resident(wg.shape), _resident(wu.shape), _resident(wd.shape)],
        out_specs=[row(D_FF), row(D_FF), row(D_MODEL), _full((1, D_MODEL))],
        out_shape=[jax.ShapeDtypeStruct((T, D_FF), BF16), jax.ShapeDtypeStruct((T, D_FF), BF16),
                   jax.ShapeDtypeStruct((T, D_MODEL), F32), jax.ShapeDtypeStruct((1, D_MODEL), F32)],
        compiler_params=_params(("arbitrary",)),
    )(dx2, x1, gate, up, g, wg, wu, wd)


def _merge_bwd(dx1, og, cvo, gt, wa, wb, wo):
    T = dx1.shape[0]
    tm = min(512, T)

    def body(dx_ref, og_ref, cvo_ref, gt_ref, wa_ref, wb_ref, wo_ref, dgt_ref, dya_ref, dyb_ref, dog_ref, dcvo_ref):
        dm = _mm_nt(dx_ref[...], wo_ref[...])
        wa = _shard_cols(wa_ref)
        wb = _shard_cols(wb_ref)
        ya = jnp.dot(og_ref[...], wa, preferred_element_type=F32)
        yb = jnp.dot(cvo_ref[...], wb, preferred_element_type=F32)
        sa = _sigmoid(gt_ref[:, :D_MODEL].astype(F32))
        sb = _sigmoid(gt_ref[:, D_MODEL:].astype(F32))
        dgt_ref[:, :D_MODEL] = (dm * ya * (sa * (1.0 - sa))).astype(BF16)
        dgt_ref[:, D_MODEL:] = (dm * yb * (sb * (1.0 - sb))).astype(BF16)
        dya = (dm * sa).astype(BF16)
        dyb = (dm * sb).astype(BF16)
        dya_ref[...] = dya
        dyb_ref[...] = dyb
        dog_ref[...] = _mm_nt(dya, wa)
        dcvo_ref[...] = _mm_nt(dyb, wb)

    row = lambda n: pl.BlockSpec((tm, n), lambda i: (i, 0))
    return pl.pallas_call(
        body, name="merge_bwd", grid=(T // tm,),
        in_specs=[row(D_MODEL), row(HG_WIDTH), row(CONV_WIDTH), row(2 * D_MODEL),
                  _resident(wa.shape), _resident(wb.shape), _resident(wo.shape)],
        out_specs=[row(2 * D_MODEL), row(D_MODEL), row(D_MODEL), row(HG_WIDTH), row(CONV_WIDTH)],
        out_shape=[jax.ShapeDtypeStruct((T, 2 * D_MODEL), BF16), jax.ShapeDtypeStruct((T, D_MODEL), BF16),
                   jax.ShapeDtypeStruct((T, D_MODEL), BF16), jax.ShapeDtypeStruct((T, HG_WIDTH), F32),
                   jax.ShapeDtypeStruct((T, CONV_WIDTH), F32)],
        compiler_params=_params(("parallel",)),
    )(dx1, og, cvo, gt, wa, wb, wo)


def _conv_bwd(dcvo, cv, conv_w, after=()):
    T = cv.shape[0]
    nj = CONV_WIDTH // 128

    def body(do_ref, c_ref, b_ref, x_ref, w_ref, dc_ref, db_ref, dx_ref, dw_ref):
        row = lax.broadcasted_iota(jnp.int32, (T, 128), 0)
        c = c_ref[...].astype(F32)
        xb = x_ref[...].astype(F32)
        do = do_ref[...]
        u = c * xb
        u1 = jnp.where(row >= 1, pltpu.roll(u, 1, 0), 0.0)
        u2 = jnp.where(row >= 2, pltpu.roll(u, 2, 0), 0.0)
        w0, w1, w2 = w_ref[0:1, :], w_ref[1:2, :], w_ref[2:3, :]
        y = w0 * u2 + w1 * u1 + w2 * u
        db_ref[...] = (do * y).astype(BF16)
        dy = do * b_ref[...].astype(F32)
        dw_ref[0:1, :] = jnp.sum(dy * u2, axis=0, keepdims=True)
        dw_ref[1:2, :] = jnp.sum(dy * u1, axis=0, keepdims=True)
        dw_ref[2:3, :] = jnp.sum(dy * u, axis=0, keepdims=True)
        dy1 = jnp.where(row < T - 1, pltpu.roll(dy, T - 1, 0), 0.0)
        dy2 = jnp.where(row < T - 2, pltpu.roll(dy, T - 2, 0), 0.0)
        du = w2 * dy + w1 * dy1 + w0 * dy2
        dc_ref[...] = (du * xb).astype(BF16)
        dx_ref[...] = (du * c).astype(BF16)

    col = lambda p: pl.BlockSpec((T, 128), lambda j: (0, p * nj + j))
    one = pl.BlockSpec((T, 128), lambda j: (0, j))
    wspec = pl.BlockSpec((3, 128), lambda j: (0, j))
    out = jax.ShapeDtypeStruct((T, CONV_WIDTH), BF16)
    return pl.pallas_call(
        _drop_operands(body, 5, len(after)), name="conv_bwd", grid=(nj,),
        in_specs=[one, col(0), col(1), col(2), wspec] + [HBM_SPEC] * len(after),
        out_specs=[one, one, one, wspec],
        out_shape=[out, out, out, jax.ShapeDtypeStruct((3, CONV_WIDTH), F32)],
        compiler_params=_params(("parallel",)),
    )(dcvo, cv, cv, cv, conv_w, *after)


def _drop_operands(body, first, count):
    def wrapped(*refs):
        return body(*refs[:first], *refs[first + count:])
    return wrapped


def _hg_bwd(dog, hg, o, st, low, gn, after=()):
    T = hg.shape[0]
    tb = min(512, T)
    sb = min(256, tb)
    nb = T // tb
    nc = tb // CHUNK
    wid = HEADS_PER_STEP * HEAD_DIM

    def body(q_ref, f_ref, i_ref, g_ref, low_ref, gn_ref, o_ref, dog_ref, st_ref,
             dq_ref, df_ref, di_ref, dg_ref, dlow_ref, dgn_ref,
             ds_scr, dqi_scr, dko_scr, dv_scr, dd_scr, dqh_scr, dkh_scr):
        h = pl.program_id(0)
        t = pl.program_id(1)

        @pl.when(t == 0)
        def _():
            ds_scr[...] = jnp.zeros_like(ds_scr)
            dlow_ref[...] = jnp.zeros_like(dlow_ref)

        @pl.when((t == 0) & (h == 0))
        def _():
            dgn_ref[...] = jnp.zeros_like(dgn_ref)

        pos = _chunk_pos((tb, HEAD_DIM))
        mask = _intra_mask(sb)
        gnv = gn_ref[...]
        lanes = [slice(hh * HEAD_DIM, (hh + 1) * HEAD_DIM) for hh in range(HEADS_PER_STEP)]
        heads = []
        for hh, ln in enumerate(lanes):
            lb, lb1 = _lower_bound(low_ref.at[:, ln])
            qr = q_ref[:, ln]
            sq, q, sg, f, k, e_qa, e_ka, e_b, e_ko, dec = _hg_gates(qr, f_ref[:, ln], lb, pos, tb)

            gr = g_ref[:, ln]
            o = o_ref[:, ln]
            dog_v = dog_ref[:, ln]
            sgr = _sigmoid(gr)
            r = lax.rsqrt(jnp.mean(o * o, axis=-1, keepdims=True) + EPS)
            oh = o * r
            dg_ref[:, ln] = (dog_v * (oh * gnv) * (sgr * (1.0 + gr * (1.0 - sgr)))).astype(BF16)
            don = dog_v * (gr * sgr)
            dgn_ref[...] += jnp.sum(don * oh, axis=0, keepdims=True)
            w = don * gnv
            do = (r * (w - oh * jnp.mean(w * oh, axis=-1, keepdims=True))).astype(BF16)

            qh = (q * e_qa).astype(BF16)
            kh = (k * e_ka).astype(BF16)
            qi = (q * e_b).astype(BF16)
            ko = (k * e_ko).astype(BF16)
            vb = i_ref[:, ln].astype(BF16)

            for s in range(tb // sb):
                sl = slice(s * sb, (s + 1) * sb)
                p = jnp.where(mask, _mm_nt(qh[sl], kh[sl]), 0.0).astype(BF16)
                dp = jnp.where(mask, _mm_nt(do[sl], vb[sl]), 0.0).astype(BF16)
                dv_scr[sl, ln] = _mm_tn(p, do[sl])
                dqh_scr[sl, ln] = _mm(dp, kh[sl])
                dkh_scr[sl, ln] = _mm_tn(dp, qh[sl])
            heads.append(dict(lb=lb, lb1=lb1, qr=qr, sq=sq, q=q, sg=sg, f=f, k=k, e_qa=e_qa, e_ka=e_ka, e_b=e_b,
                              e_ko=e_ko, dec=dec, do=do, qi=qi, ko=ko, vb=vb, ds=ds_scr[hh]))

        for c in reversed(range(nc)):
            sl = slice(c * CHUNK, (c + 1) * CHUNK)
            for hh, ln in enumerate(lanes):
                hd = heads[hh]
                ds = hd["ds"]
                st_c = st_ref[hh, c]
                dqi_scr[sl, ln] = _mm(hd["do"][sl], st_c)
                dko_scr[sl, ln] = _mm(hd["vb"][sl], ds)
                dv_scr[sl, ln] = dv_scr[sl, ln] + _mm_nt(hd["ko"][sl], ds)
                dd_scr[sl, ln] = jnp.broadcast_to(jnp.sum(ds * st_c, axis=0, keepdims=True), (CHUNK, HEAD_DIM))
                hd["ds"] = hd["dec"][c * CHUNK:c * CHUNK + 1, :] * ds + _mm_tn(hd["do"][sl], hd["qi"][sl])

        for hh, ln in enumerate(lanes):
            hd = heads[hh]
            ds_scr[hh] = hd["ds"]
            q, k, lb = hd["q"], hd["k"], hd["lb"]
            dko_e = dko_scr[:, ln] * hd["e_ko"]
            dq = dqh_scr[:, ln] * hd["e_qa"] + dqi_scr[:, ln] * hd["e_b"]
            dk = dkh_scr[:, ln] * hd["e_ka"] + dko_e
            kd3 = (k * dko_e).reshape(nc, CHUNK, HEAD_DIM)
            last = jnp.broadcast_to(jnp.sum(kd3, axis=1, keepdims=True), kd3.shape).reshape(tb, HEAD_DIM)
            db = q * dq - k * dk + jnp.where(pos == CHUNK - 1, hd["dec"] * dd_scr[:, ln] + last, 0.0)
            dlg = _chunk_rev_cumsum(db, pos)
            dfv = dlg / hd["f"] - dk
            s_low = jnp.sum(dfv * (1.0 - hd["sg"]), axis=0, keepdims=True)
            dlow_ref[0:1, ln] += s_low * lb * (1.0 - lb)
            dlow_ref[1:2, ln] += -s_low * lb * hd["lb1"]
            df_ref[:, ln] = (dfv * (1.0 - lb) * hd["sg"] * (1.0 - hd["sg"])).astype(BF16)
            dq_ref[:, ln] = (dq * Q_SCALE * (hd["sq"] * (1.0 + hd["qr"] * (1.0 - hd["sq"])))).astype(BF16)
            di_ref[:, ln] = dv_scr[:, ln].astype(BF16)

    rt = lambda t: nb - 1 - t
    col = lambda p: pl.BlockSpec((tb, wid), lambda h, t: (rt(t), p * HEAD_GROUPS + h))
    hcol = pl.BlockSpec((tb, wid), lambda h, t: (rt(t), h))
    piece = jax.ShapeDtypeStruct((T, HG_WIDTH), BF16)
    tile = pltpu.VMEM((tb, wid), F32)
    return pl.pallas_call(
        _drop_operands(body, 9, len(after)), name="hg_bwd", grid=(HEAD_GROUPS, nb),
        in_specs=[col(0), col(1), col(2), col(3), pl.BlockSpec((2, wid), lambda h, t: (0, h)),
                  pl.BlockSpec((1, HEAD_DIM), lambda h, t: (0, 0)), hcol, hcol,
                  pl.BlockSpec((HEADS_PER_STEP, nc, HEAD_DIM, HEAD_DIM), lambda h, t: (h, rt(t), 0, 0))]
                 + [HBM_SPEC] * len(after),
        out_specs=[hcol, hcol, hcol, hcol, pl.BlockSpec((2, wid), lambda h, t: (0, h)),
                   pl.BlockSpec((1, HEAD_DIM), lambda h, t: (0, 0))],
        out_shape=[piece, piece, piece, piece, jax.ShapeDtypeStruct((2, HG_WIDTH), F32),
                   jax.ShapeDtypeStruct((1, HEAD_DIM), F32)],
        scratch_shapes=[pltpu.VMEM((HEADS_PER_STEP, HEAD_DIM, HEAD_DIM), F32), tile, tile, tile, tile, tile, tile],
        compiler_params=_params(("arbitrary", "arbitrary")),
    )(hg, hg, hg, hg, low, gn, o, dog, st, *after)


def _in_bwd(dparts, w_in, x, dx1, g, after=()):
    T = x.shape[0]
    tm = min(512, T)
    widths = [p.shape[1] for p in dparts]
    offs = [sum(widths[:i]) for i in range(len(widths))]
    n = len(dparts)

    def body(*refs):
        d_refs = refs[:n]
        w_ref, x_ref, dx1_ref, g_ref, dx_ref, dgn_ref = refs[n:]

        @pl.when(pl.program_id(0) == 0)
        def _():
            dgn_ref[...] = jnp.zeros_like(dgn_ref)

        dh = None
        for d_ref, off, wd in zip(d_refs, offs, widths):
            part = _mm(d_ref[...], w_ref[off:off + wd, :])
            dh = part if dh is None else dh + part
        xv = x_ref[...]
        r = lax.rsqrt(jnp.mean(xv * xv, axis=-1, keepdims=True) + EPS)
        xh = xv * r
        dgn_ref[...] += jnp.sum(dh * xh, axis=0, keepdims=True)
        w = dh * g_ref[...]
        dx_ref[...] = dx1_ref[...] + r * (w - xh * jnp.mean(w * xh, axis=-1, keepdims=True))

    row = lambda m: pl.BlockSpec((tm, m), lambda i: (i, 0))
    return pl.pallas_call(
        _drop_operands(body, n + 4, len(after)), name="in_bwd", grid=(T // tm,),
        in_specs=[row(wd) for wd in widths] + [_resident(w_in.shape), row(D_MODEL), row(D_MODEL), _full((1, D_MODEL))]
                 + [HBM_SPEC] * len(after),
        out_specs=[row(D_MODEL), _full((1, D_MODEL))],
        out_shape=[jax.ShapeDtypeStruct((T, D_MODEL), F32), jax.ShapeDtypeStruct((1, D_MODEL), F32)],
        compiler_params=_params(("arbitrary",)),
    )(*dparts, w_in, x, dx1, g, *after)


def _wgrad(name, at, b, tn, transposed=False):
    M, T = at.shape
    N = b.shape[1]
    tk = min(2048, T)
    nk = T // tk

    def body(a_ref, b_ref, o_ref, acc):
        k = pl.program_id(1)
        part = _mm(a_ref[...], b_ref[...])

        @pl.when(k == 0)
        def _():
            acc[...] = part

        @pl.when(k != 0)
        def _():
            acc[...] += part

        @pl.when(k == nk - 1)
        def _():
            o_ref[...] = (acc[...].T if transposed else acc[...]).astype(BF16)

    if transposed:
        out_spec, out_shape = pl.BlockSpec((tn, M), lambda j, k: (j, 0)), (N, M)
    else:
        out_spec, out_shape = pl.BlockSpec((M, tn), lambda j, k: (0, j)), (M, N)
    return pl.pallas_call(
        body, name=name, grid=(N // tn, nk),
        in_specs=[pl.BlockSpec((M, tk), lambda j, k: (0, k)), pl.BlockSpec((tk, tn), lambda j, k: (k, j))],
        out_specs=out_spec, out_shape=jax.ShapeDtypeStruct(out_shape, BF16),
        scratch_shapes=[pltpu.VMEM((M, tn), F32)],
        compiler_params=_params(("parallel", "arbitrary")),
    )(at, b)


def _wgrad_in(ht, dparts, after=()):
    M, T = ht.shape
    tn = 512
    tk = min(2048, T)
    nk = T // tk
    nblk = [p.shape[1] // tn for p in dparts]
    start = [sum(nblk[:i]) for i in range(len(nblk))]
    n = len(dparts)

    def body(a_ref, *refs):
        d_refs, o_ref, acc = refs[:n], refs[n], refs[n + 1]
        j = pl.program_id(0)
        k = pl.program_id(1)

        @pl.when(k == 0)
        def _():
            acc[...] = jnp.zeros_like(acc)

        for d_ref, s, nb in zip(d_refs, start, nblk):
            @pl.when((j >= s) & (j < s + nb))
            def _():
                acc[...] += _mm(a_ref[...], d_ref[...])

        @pl.when(k == nk - 1)
        def _():
            o_ref[...] = acc[...].T.astype(BF16)

    def piece_spec(s, nb):
        def index(j, k):
            inside = (j >= s) & (j < s + nb)
            return jnp.where(inside, k, 0), jnp.clip(j - s, 0, nb - 1)
        return pl.BlockSpec((tk, tn), index)

    return pl.pallas_call(
        _drop_operands(body, 1 + n, len(after)), name="wgrad_in", grid=(sum(nblk), nk),
        in_specs=[pl.BlockSpec((M, tk), lambda j, k: (0, k))] + [piece_spec(s, nb) for s, nb in zip(start, nblk)]
                 + [HBM_SPEC] * len(after),
        out_specs=pl.BlockSpec((tn, M), lambda j, k: (j, 0)),
        out_shape=jax.ShapeDtypeStruct((sum(nblk) * tn, M), BF16),
        scratch_shapes=[pltpu.VMEM((M, tn), F32)],
        compiler_params=_params(("parallel", "arbitrary")),
    )(ht, *dparts, *after)


def _adamw_math(w, g, m, v):
    m = ADAM_B1 * m + (1.0 - ADAM_B1) * g
    v = ADAM_B2 * v + (1.0 - ADAM_B2) * (g * g)
    m_hat = m / (1.0 - ADAM_B1 ** ADAM_STEP)
    v_hat = v / (1.0 - ADAM_B2 ** ADAM_STEP)
    delta = -ADAM_LR * (m_hat / (jnp.sqrt(v_hat) + ADAM_EPS) + ADAM_WD * w)
    return delta, m, v


def _adamw_sum(name, w, parts, m, v):
    R, C = w.shape
    tr = _row_tile(R)

    def body(w_ref, p_ref, m_ref, v_ref, g_out, d_out, m_out, v_out):
        g = p_ref[0].astype(F32)
        for k in range(1, 4):
            g = g + p_ref[k].astype(F32)
        g_out[...] = g
        d_out[...], m_out[...], v_out[...] = _adamw_math(w_ref[...], g, m_ref[...], v_ref[...])

    blk = pl.BlockSpec((tr, C), lambda i: (i, 0))
    out = jax.ShapeDtypeStruct((R, C), F32)
    return pl.pallas_call(
        body, name=name, grid=(R // tr,),
        in_specs=[blk, pl.BlockSpec((4, tr, C), lambda i: (0, i, 0)), blk, blk],
        out_specs=[blk, blk, blk, blk], out_shape=[out, out, out, out],
        compiler_params=_params(("parallel",)),
    )(w, parts, m, v)


def _small_sum(gathered):
    R = gathered.shape[1]

    def body(p_ref, o_ref):
        g = p_ref[0]
        for k in range(1, N_DEV):
            g = g + p_ref[k]
        o_ref[...] = g

    return pl.pallas_call(
        body, name="small_sum", in_specs=[_full(gathered.shape)], out_specs=_full((R, 128)), grid=(1,),
        out_shape=jax.ShapeDtypeStruct((R, 128), F32),
    )(gathered)


def _small_adamw(w, g, m, v):
    def body(w_ref, g_ref, m_ref, v_ref, d_out, m_out, v_out):
        d_out[...], m_out[...], v_out[...] = _adamw_math(w_ref[...], g_ref[...], m_ref[...], v_ref[...])

    out = jax.ShapeDtypeStruct(w.shape, F32)
    spec = _full(w.shape)
    return pl.pallas_call(
        body, name="small_adamw", grid=(1,), in_specs=[spec] * 4, out_specs=[spec] * 3, out_shape=[out, out, out],
    )(w, g, m, v)


def _row_tile(rows):
    for cand in (256, 128):
        if rows % cand == 0:
            return cand
    return rows


def _pair_sum(name, by_owner, got, core, after=()):
    n = len(got)

    def body(core_ref, *refs):
        for a_ref, b_ref, o_ref in zip(refs[:n], refs[n:2 * n], refs[2 * n:]):
            o_ref[...] = (a_ref[...].astype(F32) + b_ref[...].astype(F32)).astype(BF16)

    def blk(g):
        return pl.BlockSpec((None,) + g.shape[1:], lambda k, core_ref: (k, 0, 0))

    def mine(g):
        return pl.BlockSpec((None,) + g.shape[1:], lambda k, core_ref: (2 * k + core_ref[0], 0, 0))

    return pl.pallas_call(
        _drop_operands(body, 1 + 2 * n, len(after)), name=name,
        grid_spec=pltpu.PrefetchScalarGridSpec(
            num_scalar_prefetch=1, grid=(4,),
            in_specs=[mine(g) for g in got] + [blk(g) for g in got] + [HBM_SPEC] * len(after),
            out_specs=[blk(g) for g in got]),
        out_shape=[jax.ShapeDtypeStruct(g.shape, BF16) for g in got],
        compiler_params=_params(("parallel",)),
    )(core, *by_owner, *got, *after)


def _shards_from_cols(name, full):
    R, allc = full.shape
    c = allc // N_DEV
    tr = _row_tile(R)

    def body(f_ref, o_ref):
        for s in range(N_DEV):
            o_ref[s] = f_ref[:, s * c:(s + 1) * c]

    return pl.pallas_call(
        body, name=name, grid=(R // tr,),
        in_specs=[pl.BlockSpec((tr, allc), lambda i: (i, 0))],
        out_specs=pl.BlockSpec((N_DEV, tr, c), lambda i: (0, i, 0)),
        out_shape=jax.ShapeDtypeStruct((N_DEV, R, c), full.dtype),
        compiler_params=_params(("parallel",)),
    )(full)


MESH = pl.DeviceIdType.MESH
HBM_SPEC = pl.BlockSpec(memory_space=pl.ANY)


def _handshake(peers):
    barrier = pltpu.get_barrier_semaphore()
    for peer in peers:
        pl.semaphore_signal(barrier, inc=1, device_id=peer, device_id_type=MESH)
    pl.semaphore_wait(barrier, len(peers))


def _comm_call(body, name, operands, out_shape, scratch, collective_id):
    if collective_id is None:
        return pl.pallas_call(body, name=name, in_specs=[HBM_SPEC] * len(operands), out_specs=[HBM_SPEC] * len(out_shape),
                              out_shape=out_shape, scratch_shapes=scratch)(*operands)
    return pl.kernel(body, out_type=out_shape, mesh=plsc.ScalarSubcoreMesh(axis_name="sequencer", num_cores=1),
                     scratch_types=scratch, name=name,
                     compiler_params=pltpu.CompilerParams(collective_id=collective_id))(*operands)


def _all_gather(name, blocks, collective_id=None, after=()):
    n = len(blocks)
    na = len(after)

    def body(*refs):
        x_refs, out_refs = refs[:n], refs[n + na:2 * n + na]
        send_sems, recv_sems, local_sems = refs[2 * n + na:]
        x, y, c = lax.axis_index("x"), lax.axis_index("y"), lax.axis_index("c")
        me, sibling = (x, y, c), (x, y, 1 - c)
        chips = [(1 - x, y), (x, 1 - y), (1 - x, 1 - y)]
        if collective_id is not None:
            _handshake([sibling] + [(*chip, c) for chip in chips])

        def slot(i, px, py, pc):
            return out_refs[i].at[4 * px + 2 * py + pc]

        def copy(i, k, blk, to, src=None):
            return pltpu.make_async_remote_copy(
                src_ref=slot(i, *blk) if src is None else src, dst_ref=slot(i, *blk),
                send_sem=send_sems.at[7 * i + k], recv_sem=recv_sems.at[7 * i + k], device_id=to, device_id_type=MESH)

        mine = [pltpu.make_async_copy(x_refs[i], slot(i, *me), local_sems.at[i]) for i in range(n)]
        for cp in mine:
            cp.start()
        first = []
        for i in range(n):
            first.append(copy(i, 0, me, sibling, src=x_refs[i]))
            first += [copy(i, 1 + j, me, (*chip, c), src=x_refs[i]) for j, chip in enumerate(chips)]
        for cp in first:
            cp.start()
        passed = []
        for i in range(n):
            for j, chip in enumerate(chips):
                copy(i, 1 + j, (*chip, c), me).wait_recv()
                passed.append(copy(i, 4 + j, (*chip, c), sibling))
                passed[-1].start()
        for i in range(n):
            copy(i, 0, sibling, me).wait_recv()
            for j, chip in enumerate(chips):
                copy(i, 4 + j, (*chip, 1 - c), me).wait_recv()
        for cp in first + passed:
            cp.wait_send()
        for cp in mine:
            cp.wait()

    return _comm_call(
        body, name, list(blocks) + list(after), [jax.ShapeDtypeStruct((N_DEV,) + b.shape, b.dtype) for b in blocks],
        [pltpu.SemaphoreType.DMA((7 * n,)), pltpu.SemaphoreType.DMA((7 * n,)), pltpu.SemaphoreType.DMA((n,))],
        collective_id)


def _sibling_swap(name, by_owner, collective_id=None, after=()):
    n = len(by_owner)
    na = len(after)

    def body(*refs):
        x_refs, out_refs = refs[:n], refs[n + na:2 * n + na]
        send_sems, recv_sems = refs[2 * n + na:]
        x, y, c = lax.axis_index("x"), lax.axis_index("y"), lax.axis_index("c")
        if collective_id is not None:
            _handshake([(x, y, 1 - c)])
        copies = []
        for i in range(n):
            for k in range(4):
                copies.append(pltpu.make_async_remote_copy(
                    src_ref=x_refs[i].at[2 * k + 1 - c], dst_ref=out_refs[i].at[k],
                    send_sem=send_sems.at[4 * i + k], recv_sem=recv_sems.at[4 * i + k],
                    device_id=(x, y, 1 - c), device_id_type=MESH))
        for cp in copies:
            cp.start()
        for cp in copies:
            cp.wait()

    return _comm_call(
        body, name, list(by_owner) + list(after),
        [jax.ShapeDtypeStruct((4,) + b.shape[1:], b.dtype) for b in by_owner],
        [pltpu.SemaphoreType.DMA((4 * n,)), pltpu.SemaphoreType.DMA((4 * n,))], collective_id)


def _chip_exchange(name, sums, collective_id=None, after=()):
    n = len(sums)
    na = len(after)

    def body(*refs):
        x_refs, out_refs = refs[:n], refs[n + na:2 * n + na]
        send_sems, recv_sems, local_sems = refs[2 * n + na:]
        x, y, c = lax.axis_index("x"), lax.axis_index("y"), lax.axis_index("c")
        chips = [(1 - x, y), (x, 1 - y), (1 - x, 1 - y)]
        my_chip = 2 * x + y
        if collective_id is not None:
            _handshake([(cx, cy, c) for cx, cy in chips])
        mine = [pltpu.make_async_copy(x_refs[i].at[my_chip], out_refs[i].at[my_chip], local_sems.at[i])
                for i in range(n)]
        for cp in mine:
            cp.start()
        sends = []
        for i in range(n):
            for j, (cx, cy) in enumerate(chips):
                sends.append(pltpu.make_async_remote_copy(
                    src_ref=x_refs[i].at[2 * cx + cy], dst_ref=out_refs[i].at[my_chip],
                    send_sem=send_sems.at[3 * i + j], recv_sem=recv_sems.at[3 * i + j],
                    device_id=(cx, cy, c), device_id_type=MESH))
        for cp in sends:
            cp.start()
        for i in range(n):
            for j, (cx, cy) in enumerate(chips):
                pltpu.make_async_remote_copy(
                    src_ref=x_refs[i].at[my_chip], dst_ref=out_refs[i].at[2 * cx + cy],
                    send_sem=send_sems.at[3 * i + j], recv_sem=recv_sems.at[3 * i + j],
                    device_id=(cx, cy, c), device_id_type=MESH).wait_recv()
        for cp in sends:
            cp.wait_send()
        for cp in mine:
            cp.wait()

    return _comm_call(
        body, name, list(sums) + list(after), [jax.ShapeDtypeStruct(s.shape, s.dtype) for s in sums],
        [pltpu.SemaphoreType.DMA((3 * n,)), pltpu.SemaphoreType.DMA((3 * n,)), pltpu.SemaphoreType.DMA((n,))],
        collective_id)


def _cast_shards(shards):
    n = len(shards)

    def body(*refs):
        for i in range(n):
            refs[n + i][...] = refs[i][...].astype(BF16)

    vmem = pl.BlockSpec(memory_space=pltpu.VMEM)
    return pl.pallas_call(
        body, name="cast_shards", in_specs=[vmem] * n, out_specs=[vmem] * n,
        out_shape=[jax.ShapeDtypeStruct(s.shape, BF16) for s in shards],
        compiler_params=pltpu.CompilerParams(vmem_limit_bytes=VMEM_LIMIT_V7X),
    )(*shards)


BIG = ("w_in", "w_branch_a", "w_branch_b", "w_out", "w_ffn_gate", "w_ffn_up", "w_ffn_down")


def _local_step(x, target, gains, low, conv_w, wg8, reduce):
    g_mix, g_hg, g_ffn, g_fin = gains
    w_in = wg8["w_in"].reshape(N_IN, D_MODEL)
    wg = wg8["w_ffn_gate"].reshape(D_FF, D_MODEL)
    wu = wg8["w_ffn_up"].reshape(D_FF, D_MODEL)
    wa, wb = wg8["w_branch_a"], wg8["w_branch_b"]
    wo = wg8["w_out"].reshape(D_MODEL, D_MODEL)
    wd = wg8["w_ffn_down"].reshape(D_FF, D_MODEL)

    ht, hg, cv, gt = _fwd_in(x, g_mix, w_in)
    o, og, ogt, st = _hg_fwd(hg, low, g_hg)
    cvo, cvot = _conv_fwd(cv, conv_w)
    x1, mgt = _merge_fwd(og, cvo, gt, x, wa, wb, wo)
    h2t, gate, up, actt, loss, d_gfin, dx2 = _ffn_fwd_loss(x1, g_ffn, wg, wu, wd, target, g_fin)

    dgate, dup, dx1, d_gffn = _ffn_bwd(dx2, x1, gate, up, g_ffn, wg, wu, wd)
    ffn = dict(
        w_ffn_down=_wgrad("wgrad_ffn_down", actt, dx2, 512).reshape(N_DEV, D_FF // N_DEV, D_MODEL),
        w_ffn_gate=_wgrad("wgrad_ffn_gate", h2t, dgate, 1408, transposed=True).reshape(N_DEV, D_FF // N_DEV, D_MODEL),
        w_ffn_up=_wgrad("wgrad_ffn_up", h2t, dup, 1408, transposed=True).reshape(N_DEV, D_FF // N_DEV, D_MODEL))
    sums_ffn, got_ffn = reduce.begin(ffn)
    dgt, dya, dyb, dog, dcvo = _merge_bwd(dx1, og, cvo, gt, wa, wb, wo)
    out = dict(
        w_out=_wgrad("wgrad_out", mgt, dx1, 512).reshape(N_DEV, D_MODEL // N_DEV, D_MODEL),
        w_branch_a=_shards_from_cols("split_w_branch_a", _wgrad("wgrad_branch_a", ogt, dya, 512)),
        w_branch_b=_shards_from_cols("split_w_branch_b", _wgrad("wgrad_branch_b", cvot, dyb, 512)))
    sums_out, got_out = reduce.begin(out, after=got_ffn[:1])
    parts_ffn, _ = reduce.finish(ffn, sums_ffn, after=got_out[:1])
    dq, df, di, dg, d_low, d_ghg = _hg_bwd(dog, hg, o, st, low, g_hg, after=list(sums_ffn) + list(sums_out))
    dc, db, dxb, d_conv = _conv_bwd(dcvo, cv, conv_w, after=[dq])
    parts_out, updated_out = reduce.finish(out, sums_out, after=[parts_ffn[0], dc])
    dparts = [dq, df, di, dg, dc, db, dxb, dgt]
    w_in_grad = dict(w_in=_wgrad_in(ht, dparts, after=parts_ffn[:1]).reshape(N_DEV, N_IN // N_DEV, D_MODEL))
    sums_in, _ = reduce.begin(w_in_grad, after=parts_out[:1], sum_after=updated_out)
    parts_in, _ = reduce.finish(w_in_grad, sums_in)
    grad_x, d_gmix = _in_bwd(dparts, w_in, x, dx1, g_mix, after=list(parts_out[:1]) + list(sums_in))
    small = dict(norm_mix_g=d_gmix, norm_ffn_g=d_gffn, norm_final_g=d_gfin, lower_bounds=d_low, hg_norm_g=d_ghg,
                 conv_w=d_conv, loss=loss)
    return grad_x, small, parts_in


_SMALL_LAYOUT = (("norm_mix_g", 0, 8), ("norm_ffn_g", 8, 8), ("norm_final_g", 16, 8), ("lower_bounds", 24, 8),
                 ("hg_norm_g", 32, 1))
_LOSS_ROW = 40
_CONV_ROW = 48


def _pad_rows(a, rows):
    return jnp.pad(a, ((0, rows - a.shape[0]), (0, 0)))


def _pack_small(vals, conv_rows):
    parts = [_pad_rows(vals[name].reshape(rows, 128), 8) for name, _, rows in _SMALL_LAYOUT]
    loss = vals["loss"][:, :128] if "loss" in vals else jnp.zeros((1, 128), F32)
    parts.append(_pad_rows(loss, 8))
    parts.append(_pad_rows(conv_rows, SMALL_ROWS - _CONV_ROW))
    return jnp.concatenate(parts, axis=0)


def _conv_shard_rows(a):
    return jnp.pad(a, ((0, 5), (0, 64)))


def kernel(x, norm_mix_g, w_in, lower_bounds, hg_norm_g, conv_w, w_branch_a, w_branch_b, w_out, norm_ffn_g, w_ffn_gate, w_ffn_up, w_ffn_down, norm_final_g, loss_target, m_norm_mix_g, m_w_in, m_lower_bounds, m_hg_norm_g, m_conv_w, m_w_branch_a, m_w_branch_b, m_w_out, m_norm_ffn_g, m_w_ffn_gate, m_w_ffn_up, m_w_ffn_down, m_norm_final_g, v_norm_mix_g, v_w_in, v_lower_bounds, v_hg_norm_g, v_conv_w, v_w_branch_a, v_w_branch_b, v_w_out, v_norm_ffn_g, v_w_ffn_gate, v_w_ffn_up, v_w_ffn_down, v_norm_final_g):
    cx, cy, cc = lax.axis_index("x"), lax.axis_index("y"), lax.axis_index("c")
    my_dev = 4 * cx + 2 * cy + cc

    def tr(a):
        return a[0].T

    big = dict(w_in=tr(w_in), w_branch_a=w_branch_a[0], w_branch_b=w_branch_b[0], w_out=w_out[0],
               w_ffn_gate=tr(w_ffn_gate), w_ffn_up=tr(w_ffn_up), w_ffn_down=w_ffn_down[0])
    big_m = dict(w_in=tr(m_w_in), w_branch_a=m_w_branch_a[0], w_branch_b=m_w_branch_b[0], w_out=m_w_out[0],
                 w_ffn_gate=tr(m_w_ffn_gate), w_ffn_up=tr(m_w_ffn_up), w_ffn_down=m_w_ffn_down[0])
    big_v = dict(w_in=tr(v_w_in), w_branch_a=v_w_branch_a[0], w_branch_b=v_w_branch_b[0], w_out=v_w_out[0],
                 w_ffn_gate=tr(v_w_ffn_gate), w_ffn_up=tr(v_w_ffn_up), w_ffn_down=v_w_ffn_down[0])
    transposed = ("w_in", "w_ffn_gate", "w_ffn_up")

    shards = dict(zip(BIG, _cast_shards([big[n] for n in BIG])))
    first = _all_gather("gather_w_in", [shards["w_in"], _conv_shard_rows(conv_w[0])])
    ids = iter(range(1, 16))
    mid = _all_gather("gather_mid", [shards[n] for n in BIG[1:4]], collective_id=next(ids), after=first[1:])
    ffn = _all_gather("gather_ffn", [shards[n] for n in BIG[4:]], collective_id=next(ids), after=mid[:1])
    wg8 = dict(zip(BIG, [first[0]] + list(mid) + list(ffn)))
    conv_full = first[1][:, :3, :64].transpose(1, 0, 2).reshape(3, CONV_WIDTH)

    core = cc.reshape(1).astype(jnp.int32)
    outs = {}

    class Reduce:
        @staticmethod
        def begin(grads, after=(), sum_after=()):
            names = list(grads)
            by_owner = [grads[n] for n in names]
            got = _sibling_swap("sibling_swap_" + names[0], by_owner, collective_id=next(ids), after=after)
            sums = _pair_sum("pair_sum_" + names[0], by_owner, got, core, after=sum_after)
            return sums, got

        @staticmethod
        def finish(grads, chip_sums, after=()):
            names = list(grads)
            parts = _chip_exchange("chip_exchange_" + names[0], chip_sums, collective_id=next(ids), after=after)
            for n, p in zip(names, parts):
                outs[n] = _adamw_sum("adamw_" + n, big[n], p, big_m[n], big_v[n])
            return parts, [outs[n][1] for n in names]

    gains = (norm_mix_g, hg_norm_g, norm_ffn_g, norm_final_g.reshape(1, D_MODEL))
    grad_x, small, last = _local_step(x[0], loss_target[0], gains, lower_bounds, conv_full, wg8, Reduce)

    small_all = _all_gather("gather_small", [_pack_small(small, small["conv_w"].reshape(12, 128))],
                            collective_id=next(ids), after=last[:1])
    ssum = _small_sum(small_all[0])
    conv_g_full = ssum[_CONV_ROW:_CONV_ROW + 12].reshape(3, CONV_WIDTH)
    conv_g = lax.dynamic_slice_in_dim(conv_g_full, my_dev * 64, 64, axis=1)
    loss = ssum[_LOSS_ROW, 0]
    g_rows = jnp.concatenate([ssum[:_CONV_ROW], _pad_rows(_conv_shard_rows(conv_g), SMALL_ROWS - _CONV_ROW)], axis=0)

    def pack_state(a):
        vals = dict(norm_mix_g=a[0], norm_ffn_g=a[1], norm_final_g=a[2], lower_bounds=a[3], hg_norm_g=a[4])
        return _pack_small(vals, _conv_shard_rows(a[5][0]))

    sw = pack_state((norm_mix_g, norm_ffn_g, norm_final_g, lower_bounds, hg_norm_g, conv_w))
    sm = pack_state((m_norm_mix_g, m_norm_ffn_g, m_norm_final_g, m_lower_bounds, m_hg_norm_g, m_conv_w))
    sv = pack_state((v_norm_mix_g, v_norm_ffn_g, v_norm_final_g, v_lower_bounds, v_hg_norm_g, v_conv_w))
    s_delta, s_m, s_v = _small_adamw(sw, g_rows, sm, sv)

    shapes = dict(norm_mix_g=(1, D_MODEL), norm_ffn_g=(1, D_MODEL), norm_final_g=(D_MODEL,),
                  lower_bounds=(2, HG_WIDTH), hg_norm_g=(1, HEAD_DIM))

    def unpack(buf, name):
        if name == "conv_w":
            return buf[_CONV_ROW:_CONV_ROW + 3, :64].reshape(1, 3, 64)
        for nm, off, rows in _SMALL_LAYOUT:
            if nm == name:
                return buf[off:off + rows].reshape(shapes[name])
        raise KeyError(name)

    order = ["norm_mix_g", "w_in", "lower_bounds", "hg_norm_g", "conv_w", "w_branch_a", "w_branch_b", "w_out",
             "norm_ffn_g", "w_ffn_gate", "w_ffn_up", "w_ffn_down", "norm_final_g"]
    result = [loss, grad_x[None]]
    for k, sbuf in enumerate((g_rows, s_delta, s_m, s_v)):
        for n in order:
            if n in outs:
                result.append((outs[n][k].T if n in transposed else outs[n][k])[None])
            else:
                result.append(unpack(sbuf, n))
    return tuple(result)
```

```python
import jax
import jax.numpy as jnp
from jax import lax
from jax.experimental import pallas as pl
from jax.experimental.pallas import tpu as pltpu
from jax.experimental.pallas import tpu_sc as plsc

F32 = jnp.float32
BF16 = jnp.bfloat16
STASH = jnp.bfloat16

D_MODEL = 1024
HG_WIDTH = 512
HEAD_DIM = 128
N_HEADS = 4
HEADS_PER_STEP = 4
HEAD_GROUPS = N_HEADS // HEADS_PER_STEP
CONV_WIDTH = 512
D_FF = 2816
CHUNK = 32
EPS = 1e-6
Q_SCALE = HEAD_DIM ** -0.5
N_DEV = 8

ADAM_LR = 0.001
ADAM_B1 = 0.9
ADAM_B2 = 0.999
ADAM_EPS = 1e-08
ADAM_WD = 0.01
ADAM_STEP = 10

VMEM_LIMIT_V7X = 56 * 1024 * 1024

SMALL_ROWS = 64


def _params(sem, vmem=VMEM_LIMIT_V7X):
    return pltpu.CompilerParams(dimension_semantics=sem, vmem_limit_bytes=vmem)


def _mm(a, b):
    return jnp.dot(a.astype(BF16), b.astype(BF16), preferred_element_type=F32)


def _mm_nt(a, b):
    return lax.dot_general(a.astype(BF16), b.astype(BF16), (((1,), (1,)), ((), ())), preferred_element_type=F32)


def _mm_tn(a, b):
    return lax.dot_general(a.astype(BF16), b.astype(BF16), (((0,), (0,)), ((), ())), preferred_element_type=F32)


def _sigmoid(x):
    return 0.5 * jnp.tanh(0.5 * x) + 0.5


def _resident(shape):
    nd = len(shape)
    return pl.BlockSpec(shape, lambda *_: (0,) * nd, pipeline_mode=pl.Buffered(1))


def _full(shape):
    nd = len(shape)
    return pl.BlockSpec(shape, lambda *_: (0,) * nd)


def _shard_cols(w_ref):
    return jnp.concatenate([w_ref[s] for s in range(N_DEV)], axis=1)


N_HG = 4 * HG_WIDTH
N_CV = 3 * CONV_WIDTH
N_GT = 2 * D_MODEL
N_IN = N_HG + N_CV + N_GT


def _col(tm, n):
    return pl.BlockSpec((n, tm), lambda i: (0, i))


def _fwd_in(x, g, w_in_t):
    T = x.shape[0]
    tm = min(512, T)

    def body(x_ref, g_ref, w_ref, ht_ref, hg_ref, cv_ref, gt_ref):
        xv = x_ref[...]
        r = lax.rsqrt(jnp.mean(xv * xv, axis=-1, keepdims=True) + EPS)
        hf = xv * r * g_ref[...]
        h = hf.astype(BF16)
        ht_ref[...] = hf.T.astype(BF16)
        hg_ref[...] = _mm_nt(h, w_ref[:N_HG, :])
        cv_ref[...] = _mm_nt(h, w_ref[N_HG:N_HG + N_CV, :]).astype(STASH)
        gt_ref[...] = _mm_nt(h, w_ref[N_HG + N_CV:, :]).astype(STASH)

    row = lambda n: pl.BlockSpec((tm, n), lambda i: (i, 0))
    return pl.pallas_call(
        body, name="fwd_in", grid=(T // tm,),
        in_specs=[row(D_MODEL), _full((1, D_MODEL)), _resident(w_in_t.shape)],
        out_specs=[_col(tm, D_MODEL), row(N_HG), row(N_CV), row(N_GT)],
        out_shape=[jax.ShapeDtypeStruct((D_MODEL, T), BF16), jax.ShapeDtypeStruct((T, N_HG), F32),
                   jax.ShapeDtypeStruct((T, N_CV), STASH), jax.ShapeDtypeStruct((T, N_GT), STASH)],
        compiler_params=_params(("parallel",)),
    )(x, g, w_in_t)


def _chunk_pos(shape):
    return lax.broadcasted_iota(jnp.int32, shape, 0) & (CHUNK - 1)


def _chunk_cumsum(x, pos):
    s = 1
    while s < CHUNK:
        x = x + jnp.where(pos >= s, pltpu.roll(x, s, 0), 0.0)
        s *= 2
    return x


def _chunk_rev_cumsum(x, pos):
    n = x.shape[0]
    s = 1
    while s < CHUNK:
        x = x + jnp.where(pos + s < CHUNK, pltpu.roll(x, n - s, 0), 0.0)
        s *= 2
    return x


def _chunk_bcast(x3, row, tb):
    return jnp.broadcast_to(x3[:, row:row + 1, :], x3.shape).reshape(tb, x3.shape[-1])


def _lower_bound(low_ref):
    l0 = low_ref[0:1, :]
    l1 = low_ref[1:2, :]
    m = jnp.maximum(l0, l1)
    e0 = jnp.exp(l0 - m)
    e1 = jnp.exp(l1 - m)
    return e0 / (e0 + e1), e1 / (e0 + e1)


def _hg_gates(qr, fr, lb, pos, tb):
    sq = _sigmoid(qr)
    q = qr * sq * Q_SCALE
    sg = _sigmoid(fr)
    f = lb + (1.0 - lb) * sg
    k = 1.0 - f
    b = _chunk_cumsum(jnp.log(f), pos)
    b3 = b.reshape(tb // CHUNK, CHUNK, HEAD_DIM)
    anc = _chunk_bcast(b3, CHUNK // 2 - 1, tb)
    blb = _chunk_bcast(b3, CHUNK - 1, tb)
    e_qa = jnp.exp(b - anc)
    e_ka = jnp.exp(anc - b)
    e_b = jnp.exp(b)
    e_ko = jnp.exp(blb - b)
    dec = jnp.exp(blb)
    return sq, q, sg, f, k, e_qa, e_ka, e_b, e_ko, dec


def _intra_mask(sb):
    r = lax.broadcasted_iota(jnp.int32, (sb, sb), 0)
    c = lax.broadcasted_iota(jnp.int32, (sb, sb), 1)
    return ((r // CHUNK) == (c // CHUNK)) & (c <= r)


def _hg_fwd(hg, low, gn):
    T = hg.shape[0]
    tb = min(512, T)
    sb = min(256, tb)
    nb = T // tb
    nc = tb // CHUNK
    wid = HEADS_PER_STEP * HEAD_DIM

    def body(q_ref, f_ref, i_ref, g_ref, low_ref, gn_ref, o_ref, og_ref, ogt_ref, st_ref, s_scr):
        t = pl.program_id(1)

        @pl.when(t == 0)
        def _():
            s_scr[...] = jnp.zeros_like(s_scr)

        pos = _chunk_pos((tb, HEAD_DIM))
        mask = _intra_mask(sb)
        lanes = [slice(hh * HEAD_DIM, (hh + 1) * HEAD_DIM) for hh in range(HEADS_PER_STEP)]
        qi, ko, vb, dec, st = [], [], [], [], []
        for hh, ln in enumerate(lanes):
            lb, _ = _lower_bound(low_ref.at[:, ln])
            _, q, _, _, k, e_qa, e_ka, e_b, e_ko, dec_h = _hg_gates(q_ref[:, ln], f_ref[:, ln], lb, pos, tb)
            qh = (q * e_qa).astype(BF16)
            kh = (k * e_ka).astype(BF16)
            qi.append((q * e_b).astype(BF16))
            ko.append((k * e_ko).astype(BF16))
            vb.append(i_ref[:, ln].astype(BF16))
            dec.append(dec_h)
            st.append(s_scr[hh])
            for s in range(tb // sb):
                sl = slice(s * sb, (s + 1) * sb)
                p = jnp.where(mask, _mm_nt(qh[sl], kh[sl]), 0.0)
                o_ref[sl, ln] = _mm(p, vb[hh][sl])
        for c in range(nc):
            sl = slice(c * CHUNK, (c + 1) * CHUNK)
            for hh, ln in enumerate(lanes):
                st_ref[hh, c] = st[hh]
                o_ref[sl, ln] = o_ref[sl, ln] + _mm_nt(qi[hh][sl], st[hh])
                st[hh] = dec[hh][c * CHUNK:c * CHUNK + 1, :] * st[hh] + _mm_tn(vb[hh][sl], ko[hh][sl])
        for hh, ln in enumerate(lanes):
            s_scr[hh] = st[hh]
            o = o_ref[:, ln]
            r = lax.rsqrt(jnp.mean(o * o, axis=-1, keepdims=True) + EPS)
            gr = g_ref[:, ln]
            og = (o * r * gn_ref[...]) * (gr * _sigmoid(gr))
            og_ref[:, ln] = og.astype(BF16)
            ogt_ref[ln, :] = og.T.astype(BF16)

    col = lambda p: pl.BlockSpec((tb, wid), lambda h, t: (t, p * HEAD_GROUPS + h))
    hcol = pl.BlockSpec((tb, wid), lambda h, t: (t, h))
    return pl.pallas_call(
        body, name="hg_fwd", grid=(HEAD_GROUPS, nb),
        in_specs=[col(0), col(1), col(2), col(3), pl.BlockSpec((2, wid), lambda h, t: (0, h)),
                  pl.BlockSpec((1, HEAD_DIM), lambda h, t: (0, 0))],
        out_specs=[hcol, hcol, pl.BlockSpec((wid, tb), lambda h, t: (h, t)),
                   pl.BlockSpec((HEADS_PER_STEP, nc, HEAD_DIM, HEAD_DIM), lambda h, t: (h, t, 0, 0))],
        out_shape=[jax.ShapeDtypeStruct((T, HG_WIDTH), F32), jax.ShapeDtypeStruct((T, HG_WIDTH), BF16),
                   jax.ShapeDtypeStruct((HG_WIDTH, T), BF16),
                   jax.ShapeDtypeStruct((N_HEADS, T // CHUNK, HEAD_DIM, HEAD_DIM), F32)],
        scratch_shapes=[pltpu.VMEM((HEADS_PER_STEP, HEAD_DIM, HEAD_DIM), F32)],
        compiler_params=_params(("parallel", "arbitrary")),
    )(hg, hg, hg, hg, low, gn)


def _conv_fwd(cv, conv_w):
    T = cv.shape[0]
    nj = CONV_WIDTH // 128

    def body(c_ref, b_ref, x_ref, w_ref, o_ref, ot_ref):
        row = lax.broadcasted_iota(jnp.int32, (T, 128), 0)
        u = c_ref[...].astype(F32) * x_ref[...].astype(F32)
        u1 = jnp.where(row >= 1, pltpu.roll(u, 1, 0), 0.0)
        u2 = jnp.where(row >= 2, pltpu.roll(u, 2, 0), 0.0)
        y = w_ref[0:1, :] * u2 + w_ref[1:2, :] * u1 + w_ref[2:3, :] * u
        out = b_ref[...].astype(F32) * y
        o_ref[...] = out.astype(BF16)
        ot_ref[...] = out.T.astype(BF16)

    col = lambda p: pl.BlockSpec((T, 128), lambda j: (0, p * nj + j))
    return pl.pallas_call(
        body, name="conv_fwd", grid=(nj,),
        in_specs=[col(0), col(1), col(2), pl.BlockSpec((3, 128), lambda j: (0, j))],
        out_specs=[pl.BlockSpec((T, 128), lambda j: (0, j)), pl.BlockSpec((128, T), lambda j: (j, 0))],
        out_shape=[jax.ShapeDtypeStruct((T, CONV_WIDTH), BF16), jax.ShapeDtypeStruct((CONV_WIDTH, T), BF16)],
        compiler_params=_params(("parallel",)),
    )(cv, cv, cv, conv_w)


def _merge_fwd(og, cvo, gt, x, wa, wb, wo):
    T = x.shape[0]
    tm = min(512, T)

    def body(og_ref, cvo_ref, gt_ref, x_ref, wa_ref, wb_ref, wo_ref, x1_ref, mgt_ref):
        ya = jnp.dot(og_ref[...], _shard_cols(wa_ref), preferred_element_type=F32)
        yb = jnp.dot(cvo_ref[...], _shard_cols(wb_ref), preferred_element_type=F32)
        m = (_sigmoid(gt_ref[:, :D_MODEL].astype(F32)) * ya
             + _sigmoid(gt_ref[:, D_MODEL:].astype(F32)) * yb)
        mgt_ref[...] = m.T.astype(BF16)
        x1_ref[...] = x_ref[...] + jnp.dot(m.astype(BF16), wo_ref[...], preferred_element_type=F32)

    row = lambda n: pl.BlockSpec((tm, n), lambda i: (i, 0))
    return pl.pallas_call(
        body, name="merge_fwd", grid=(T // tm,),
        in_specs=[row(HG_WIDTH), row(CONV_WIDTH), row(2 * D_MODEL), row(D_MODEL),
                  _resident(wa.shape), _resident(wb.shape), _resident(wo.shape)],
        out_specs=[row(D_MODEL), _col(tm, D_MODEL)],
        out_shape=[jax.ShapeDtypeStruct((T, D_MODEL), F32), jax.ShapeDtypeStruct((D_MODEL, T), BF16)],
        compiler_params=_params(("parallel",)),
    )(og, cvo, gt, x, wa, wb, wo)


def _ffn_fwd_loss(x1, g, wg, wu, wd, target, g_fin):
    T = x1.shape[0]
    tm = min(256, T)

    def body(x_ref, g_ref, wg_ref, wu_ref, wd_ref, t_ref, gf_ref,
             ht_ref, gate_ref, up_ref, actt_ref, loss_ref, dgf_ref, dx2_ref):
        @pl.when(pl.program_id(0) == 0)
        def _():
            loss_ref[...] = jnp.zeros_like(loss_ref)
            dgf_ref[...] = jnp.zeros_like(dgf_ref)

        xv = x_ref[...]
        r = lax.rsqrt(jnp.mean(xv * xv, axis=-1, keepdims=True) + EPS)
        hf = xv * r * g_ref[...]
        h = hf.astype(BF16)
        ht_ref[...] = hf.T.astype(BF16)
        gate = _mm_nt(h, wg_ref[...])
        up = _mm_nt(h, wu_ref[...])
        gate_ref[...] = gate.astype(STASH)
        up_ref[...] = up.astype(STASH)
        act = gate * _sigmoid(gate) * up
        actt_ref[...] = act.T.astype(BF16)
        x2 = xv + jnp.dot(act.astype(BF16), wd_ref[...], preferred_element_type=F32)

        gv = gf_ref[...]
        r2 = lax.rsqrt(jnp.mean(x2 * x2, axis=-1, keepdims=True) + EPS)
        xh = x2 * r2
        err = xh * gv - t_ref[...]
        loss_ref[...] += 0.5 * jnp.sum(jnp.mean(err * err, axis=-1, keepdims=True), axis=0, keepdims=True)
        dy = err * (1.0 / D_MODEL)
        dgf_ref[...] += jnp.sum(dy * xh, axis=0, keepdims=True)
        w = dy * gv
        dx2_ref[...] = r2 * (w - xh * jnp.mean(w * xh, axis=-1, keepdims=True))

    row = lambda n: pl.BlockSpec((tm, n), lambda i: (i, 0))
    return pl.pallas_call(
        body, name="ffn_fwd_loss", grid=(T // tm,),
        in_specs=[row(D_MODEL), _full((1, D_MODEL)), _resident(wg.shape), _resident(wu.shape), _resident(wd.shape),
                  row(D_MODEL), _full((1, D_MODEL))],
        out_specs=[_col(tm, D_MODEL), row(D_FF), row(D_FF), _col(tm, D_FF), _full((1, 128)), _full((1, D_MODEL)),
                   row(D_MODEL)],
        out_shape=[jax.ShapeDtypeStruct((D_MODEL, T), BF16), jax.ShapeDtypeStruct((T, D_FF), STASH),
                   jax.ShapeDtypeStruct((T, D_FF), STASH), jax.ShapeDtypeStruct((D_FF, T), BF16),
                   jax.ShapeDtypeStruct((1, 128), F32), jax.ShapeDtypeStruct((1, D_MODEL), F32),
                   jax.ShapeDtypeStruct((T, D_MODEL), F32)],
        compiler_params=_params(("arbitrary",)),
    )(x1, g, wg, wu, wd, target, g_fin)


def _ffn_bwd(dx2, x1, gate, up, g, wg, wu, wd):
    T = x1.shape[0]
    tm = min(256, T)

    def body(dx2_ref, x_ref, gate_ref, up_ref, g_ref, wg_ref, wu_ref, wd_ref, dgate_ref, dup_ref, dx1_ref, dgn_ref):
        @pl.when(pl.program_id(0) == 0)
        def _():
            dgn_ref[...] = jnp.zeros_like(dgn_ref)

        dx2 = dx2_ref[...]
        dact = _mm_nt(dx2, wd_ref[...])
        gate = gate_ref[...].astype(F32)
        s = _sigmoid(gate)
        dgate = (dact * up_ref[...].astype(F32) * (s * (1.0 + gate * (1.0 - s)))).astype(BF16)
        dup = (dact * (gate * s)).astype(BF16)
        dgate_ref[...] = dgate
        dup_ref[...] = dup
        dh = _mm(dgate, wg_ref[...]) + _mm(dup, wu_ref[...])
        xv = x_ref[...]
        r = lax.rsqrt(jnp.mean(xv * xv, axis=-1, keepdims=True) + EPS)
        xh = xv * r
        dgn_ref[...] += jnp.sum(dh * xh, axis=0, keepdims=True)
        w = dh * g_ref[...]
        dx1_ref[...] = dx2 + r * (w - xh * jnp.mean(w * xh, axis=-1, keepdims=True))

    row = lambda n: pl.BlockSpec((tm, n), lambda i: (i, 0))
    return pl.pallas_call(
        body, name="ffn_bwd", grid=(T // tm,),
        in_specs=[row(D_MODEL), row(D_MODEL), row(D_FF), row(D_FF), _full((1, D_MODEL)),
                  _resident(wg.shape), _resident(wu.shape), _resident(wd.shape)],
        out_specs=[row(D_FF), row(D_FF), row(D_MODEL), _full((1, D_MODEL))],
        out_shape=[jax.ShapeDtypeStruct((T, D_FF), BF16), jax.ShapeDtypeStruct((T, D_FF), BF16),
                   jax.ShapeDtypeStruct((T, D_MODEL), F32), jax.ShapeDtypeStruct((1, D_MODEL), F32)],
        compiler_params=_params(("arbitrary",)),
    )(dx2, x1, gate, up, g, wg, wu, wd)


def _merge_bwd(dx1, og, cvo, gt, wa, wb, wo):
    T = dx1.shape[0]
    tm = min(512, T)

    def body(dx_ref, og_ref, cvo_ref, gt_ref, wa_ref, wb_ref, wo_ref, dgt_ref, dya_ref, dyb_ref, dog_ref, dcvo_ref):
        dm = _mm_nt(dx_ref[...], wo_ref[...])
        wa = _shard_cols(wa_ref)
        wb = _shard_cols(wb_ref)
        ya = jnp.dot(og_ref[...], wa, preferred_element_type=F32)
        yb = jnp.dot(cvo_ref[...], wb, preferred_element_type=F32)
        sa = _sigmoid(gt_ref[:, :D_MODEL].astype(F32))
        sb = _sigmoid(gt_ref[:, D_MODEL:].astype(F32))
        dgt_ref[:, :D_MODEL] = (dm * ya * (sa * (1.0 - sa))).astype(BF16)
        dgt_ref[:, D_MODEL:] = (dm * yb * (sb * (1.0 - sb))).astype(BF16)
        dya = (dm * sa).astype(BF16)
        dyb = (dm * sb).astype(BF16)
        dya_ref[...] = dya
        dyb_ref[...] = dyb
        dog_ref[...] = _mm_nt(dya, wa)
        dcvo_ref[...] = _mm_nt(dyb, wb)

    row = lambda n: pl.BlockSpec((tm, n), lambda i: (i, 0))
    return pl.pallas_call(
        body, name="merge_bwd", grid=(T // tm,),
        in_specs=[row(D_MODEL), row(HG_WIDTH), row(CONV_WIDTH), row(2 * D_MODEL),
                  _resident(wa.shape), _resident(wb.shape), _resident(wo.shape)],
        out_specs=[row(2 * D_MODEL), row(D_MODEL), row(D_MODEL), row(HG_WIDTH), row(CONV_WIDTH)],
        out_shape=[jax.ShapeDtypeStruct((T, 2 * D_MODEL), BF16), jax.ShapeDtypeStruct((T, D_MODEL), BF16),
                   jax.ShapeDtypeStruct((T, D_MODEL), BF16), jax.ShapeDtypeStruct((T, HG_WIDTH), F32),
                   jax.ShapeDtypeStruct((T, CONV_WIDTH), F32)],
        compiler_params=_params(("parallel",)),
    )(dx1, og, cvo, gt, wa, wb, wo)


def _conv_bwd(dcvo, cv, conv_w, after=()):
    T = cv.shape[0]
    nj = CONV_WIDTH // 128

    def body(do_ref, c_ref, b_ref, x_ref, w_ref, dc_ref, db_ref, dx_ref, dw_ref):
        row = lax.broadcasted_iota(jnp.int32, (T, 128), 0)
        c = c_ref[...].astype(F32)
        xb = x_ref[...].astype(F32)
        do = do_ref[...]
        u = c * xb
        u1 = jnp.where(row >= 1, pltpu.roll(u, 1, 0), 0.0)
        u2 = jnp.where(row >= 2, pltpu.roll(u, 2, 0), 0.0)
        w0, w1, w2 = w_ref[0:1, :], w_ref[1:2, :], w_ref[2:3, :]
        y = w0 * u2 + w1 * u1 + w2 * u
        db_ref[...] = (do * y).astype(BF16)
        dy = do * b_ref[...].astype(F32)
        dw_ref[0:1, :] = jnp.sum(dy * u2, axis=0, keepdims=True)
        dw_ref[1:2, :] = jnp.sum(dy * u1, axis=0, keepdims=True)
        dw_ref[2:3, :] = jnp.sum(dy * u, axis=0, keepdims=True)
        dy1 = jnp.where(row < T - 1, pltpu.roll(dy, T - 1, 0), 0.0)
        dy2 = jnp.where(row < T - 2, pltpu.roll(dy, T - 2, 0), 0.0)
        du = w2 * dy + w1 * dy1 + w0 * dy2
        dc_ref[...] = (du * xb).astype(BF16)
        dx_ref[...] = (du * c).astype(BF16)

    col = lambda p: pl.BlockSpec((T, 128), lambda j: (0, p * nj + j))
    one = pl.BlockSpec((T, 128), lambda j: (0, j))
    wspec = pl.BlockSpec((3, 128), lambda j: (0, j))
    out = jax.ShapeDtypeStruct((T, CONV_WIDTH), BF16)
    return pl.pallas_call(
        _drop_operands(body, 5, len(after)), name="conv_bwd", grid=(nj,),
        in_specs=[one, col(0), col(1), col(2), wspec] + [HBM_SPEC] * len(after),
        out_specs=[one, one, one, wspec],
        out_shape=[out, out, out, jax.ShapeDtypeStruct((3, CONV_WIDTH), F32)],
        compiler_params=_params(("parallel",)),
    )(dcvo, cv, cv, cv, conv_w, *after)


def _drop_operands(body, first, count):
    def wrapped(*refs):
        return body(*refs[:first], *refs[first + count:])
    return wrapped


def _hg_bwd(dog, hg, o, st, low, gn, after=()):
    T = hg.shape[0]
    tb = min(512, T)
    sb = min(256, tb)
    nb = T // tb
    nc = tb // CHUNK
    wid = HEADS_PER_STEP * HEAD_DIM

    def body(q_ref, f_ref, i_ref, g_ref, low_ref, gn_ref, o_ref, dog_ref, st_ref,
             dq_ref, df_ref, di_ref, dg_ref, dlow_ref, dgn_ref,
             ds_scr, dqi_scr, dko_scr, dv_scr, dd_scr, dqh_scr, dkh_scr):
        h = pl.program_id(0)
        t = pl.program_id(1)

        @pl.when(t == 0)
        def _():
            ds_scr[...] = jnp.zeros_like(ds_scr)
            dlow_ref[...] = jnp.zeros_like(dlow_ref)

        @pl.when((t == 0) & (h == 0))
        def _():
            dgn_ref[...] = jnp.zeros_like(dgn_ref)

        pos = _chunk_pos((tb, HEAD_DIM))
        mask = _intra_mask(sb)
        gnv = gn_ref[...]
        lanes = [slice(hh * HEAD_DIM, (hh + 1) * HEAD_DIM) for hh in range(HEADS_PER_STEP)]
        heads = []
        for hh, ln in enumerate(lanes):
            lb, lb1 = _lower_bound(low_ref.at[:, ln])
            qr = q_ref[:, ln]
            sq, q, sg, f, k, e_qa, e_ka, e_b, e_ko, dec = _hg_gates(qr, f_ref[:, ln], lb, pos, tb)

            gr = g_ref[:, ln]
            o = o_ref[:, ln]
            dog_v = dog_ref[:, ln]
            sgr = _sigmoid(gr)
            r = lax.rsqrt(jnp.mean(o * o, axis=-1, keepdims=True) + EPS)
            oh = o * r
            dg_ref[:, ln] = (dog_v * (oh * gnv) * (sgr * (1.0 + gr * (1.0 - sgr)))).astype(BF16)
            don = dog_v * (gr * sgr)
            dgn_ref[...] += jnp.sum(don * oh, axis=0, keepdims=True)
            w = don * gnv
            do = (r * (w - oh * jnp.mean(w * oh, axis=-1, keepdims=True))).astype(BF16)

            qh = (q * e_qa).astype(BF16)
            kh = (k * e_ka).astype(BF16)
            qi = (q * e_b).astype(BF16)
            ko = (k * e_ko).astype(BF16)
            vb = i_ref[:, ln].astype(BF16)

            for s in range(tb // sb):
                sl = slice(s * sb, (s + 1) * sb)
                p = jnp.where(mask, _mm_nt(qh[sl], kh[sl]), 0.0).astype(BF16)
                dp = jnp.where(mask, _mm_nt(do[sl], vb[sl]), 0.0).astype(BF16)
                dv_scr[sl, ln] = _mm_tn(p, do[sl])
                dqh_scr[sl, ln] = _mm(dp, kh[sl])
                dkh_scr[sl, ln] = _mm_tn(dp, qh[sl])
            heads.append(dict(lb=lb, lb1=lb1, qr=qr, sq=sq, q=q, sg=sg, f=f, k=k, e_qa=e_qa, e_ka=e_ka, e_b=e_b,
                              e_ko=e_ko, dec=dec, do=do, qi=qi, ko=ko, vb=vb, ds=ds_scr[hh]))

        for c in reversed(range(nc)):
            sl = slice(c * CHUNK, (c + 1) * CHUNK)
            for hh, ln in enumerate(lanes):
                hd = heads[hh]
                ds = hd["ds"]
                st_c = st_ref[hh, c]
                dqi_scr[sl, ln] = _mm(hd["do"][sl], st_c)
                dko_scr[sl, ln] = _mm(hd["vb"][sl], ds)
                dv_scr[sl, ln] = dv_scr[sl, ln] + _mm_nt(hd["ko"][sl], ds)
                dd_scr[sl, ln] = jnp.broadcast_to(jnp.sum(ds * st_c, axis=0, keepdims=True), (CHUNK, HEAD_DIM))
                hd["ds"] = hd["dec"][c * CHUNK:c * CHUNK + 1, :] * ds + _mm_tn(hd["do"][sl], hd["qi"][sl])

        for hh, ln in enumerate(lanes):
            hd = heads[hh]
            ds_scr[hh] = hd["ds"]
            q, k, lb = hd["q"], hd["k"], hd["lb"]
            dko_e = dko_scr[:, ln] * hd["e_ko"]
            dq = dqh_scr[:, ln] * hd["e_qa"] + dqi_scr[:, ln] * hd["e_b"]
            dk = dkh_scr[:, ln] * hd["e_ka"] + dko_e
            kd3 = (k * dko_e).reshape(nc, CHUNK, HEAD_DIM)
            last = jnp.broadcast_to(jnp.sum(kd3, axis=1, keepdims=True), kd3.shape).reshape(tb, HEAD_DIM)
            db = q * dq - k * dk + jnp.where(pos == CHUNK - 1, hd["dec"] * dd_scr[:, ln] + last, 0.0)
            dlg = _chunk_rev_cumsum(db, pos)
            dfv = dlg / hd["f"] - dk
            s_low = jnp.sum(dfv * (1.0 - hd["sg"]), axis=0, keepdims=True)
            dlow_ref[0:1, ln] += s_low * lb * (1.0 - lb)
            dlow_ref[1:2, ln] += -s_low * lb * hd["lb1"]
            df_ref[:, ln] = (dfv * (1.0 - lb) * hd["sg"] * (1.0 - hd["sg"])).astype(BF16)
            dq_ref[:, ln] = (dq * Q_SCALE * (hd["sq"] * (1.0 + hd["qr"] * (1.0 - hd["sq"])))).astype(BF16)
            di_ref[:, ln] = dv_scr[:, ln].astype(BF16)

    rt = lambda t: nb - 1 - t
    col = lambda p: pl.BlockSpec((tb, wid), lambda h, t: (rt(t), p * HEAD_GROUPS + h))
    hcol = pl.BlockSpec((tb, wid), lambda h, t: (rt(t), h))
    piece = jax.ShapeDtypeStruct((T, HG_WIDTH), BF16)
    tile = pltpu.VMEM((tb, wid), F32)
    return pl.pallas_call(
        _drop_operands(body, 9, len(after)), name="hg_bwd", grid=(HEAD_GROUPS, nb),
        in_specs=[col(0), col(1), col(2), col(3), pl.BlockSpec((2, wid), lambda h, t: (0, h)),
                  pl.BlockSpec((1, HEAD_DIM), lambda h, t: (0, 0)), hcol, hcol,
                  pl.BlockSpec((HEADS_PER_STEP, nc, HEAD_DIM, HEAD_DIM), lambda h, t: (h, rt(t), 0, 0))]
                 + [HBM_SPEC] * len(after),
        out_specs=[hcol, hcol, hcol, hcol, pl.BlockSpec((2, wid), lambda h, t: (0, h)),
                   pl.BlockSpec((1, HEAD_DIM), lambda h, t: (0, 0))],
        out_shape=[piece, piece, piece, piece, jax.ShapeDtypeStruct((2, HG_WIDTH), F32),
                   jax.ShapeDtypeStruct((1, HEAD_DIM), F32)],
        scratch_shapes=[pltpu.VMEM((HEADS_PER_STEP, HEAD_DIM, HEAD_DIM), F32), tile, tile, tile, tile, tile, tile],
        compiler_params=_params(("arbitrary", "arbitrary")),
    )(hg, hg, hg, hg, low, gn, o, dog, st, *after)


def _in_bwd(dparts, w_in, x, dx1, g, after=()):
    T = x.shape[0]
    tm = min(512, T)
    widths = [p.shape[1] for p in dparts]
    offs = [sum(widths[:i]) for i in range(len(widths))]
    n = len(dparts)

    def body(*refs):
        d_refs = refs[:n]
        w_ref, x_ref, dx1_ref, g_ref, dx_ref, dgn_ref = refs[n:]

        @pl.when(pl.program_id(0) == 0)
        def _():
            dgn_ref[...] = jnp.zeros_like(dgn_ref)

        dh = None
        for d_ref, off, wd in zip(d_refs, offs, widths):
            part = _mm(d_ref[...], w_ref[off:off + wd, :])
            dh = part if dh is None else dh + part
        xv = x_ref[...]
        r = lax.rsqrt(jnp.mean(xv * xv, axis=-1, keepdims=True) + EPS)
        xh = xv * r
        dgn_ref[...] += jnp.sum(dh * xh, axis=0, keepdims=True)
        w = dh * g_ref[...]
        dx_ref[...] = dx1_ref[...] + r * (w - xh * jnp.mean(w * xh, axis=-1, keepdims=True))

    row = lambda m: pl.BlockSpec((tm, m), lambda i: (i, 0))
    return pl.pallas_call(
        _drop_operands(body, n + 4, len(after)), name="in_bwd", grid=(T // tm,),
        in_specs=[row(wd) for wd in widths] + [_resident(w_in.shape), row(D_MODEL), row(D_MODEL), _full((1, D_MODEL))]
                 + [HBM_SPEC] * len(after),
        out_specs=[row(D_MODEL), _full((1, D_MODEL))],
        out_shape=[jax.ShapeDtypeStruct((T, D_MODEL), F32), jax.ShapeDtypeStruct((1, D_MODEL), F32)],
        compiler_params=_params(("arbitrary",)),
    )(*dparts, w_in, x, dx1, g, *after)


def _wgrad(name, at, b, tn, transposed=False):
    M, T = at.shape
    N = b.shape[1]
    tk = min(2048, T)
    nk = T // tk

    def body(a_ref, b_ref, o_ref, acc):
        k = pl.program_id(1)
        part = _mm(a_ref[...], b_ref[...])

        @pl.when(k == 0)
        def _():
            acc[...] = part

        @pl.when(k != 0)
        def _():
            acc[...] += part

        @pl.when(k == nk - 1)
        def _():
            o_ref[...] = (acc[...].T if transposed else acc[...]).astype(BF16)

    if transposed:
        out_spec, out_shape = pl.BlockSpec((tn, M), lambda j, k: (j, 0)), (N, M)
    else:
        out_spec, out_shape = pl.BlockSpec((M, tn), lambda j, k: (0, j)), (M, N)
    return pl.pallas_call(
        body, name=name, grid=(N // tn, nk),
        in_specs=[pl.BlockSpec((M, tk), lambda j, k: (0, k)), pl.BlockSpec((tk, tn), lambda j, k: (k, j))],
        out_specs=out_spec, out_shape=jax.ShapeDtypeStruct(out_shape, BF16),
        scratch_shapes=[pltpu.VMEM((M, tn), F32)],
        compiler_params=_params(("parallel", "arbitrary")),
    )(at, b)


def _wgrad_in(ht, dparts, after=()):
    M, T = ht.shape
    tn = 512
    tk = min(2048, T)
    nk = T // tk
    nblk = [p.shape[1] // tn for p in dparts]
    start = [sum(nblk[:i]) for i in range(len(nblk))]
    n = len(dparts)

    def body(a_ref, *refs):
        d_refs, o_ref, acc = refs[:n], refs[n], refs[n + 1]
        j = pl.program_id(0)
        k = pl.program_id(1)

        @pl.when(k == 0)
        def _():
            acc[...] = jnp.zeros_like(acc)

        for d_ref, s, nb in zip(d_refs, start, nblk):
            @pl.when((j >= s) & (j < s + nb))
            def _():
                acc[...] += _mm(a_ref[...], d_ref[...])

        @pl.when(k == nk - 1)
        def _():
            o_ref[...] = acc[...].T.astype(BF16)

    def piece_spec(s, nb):
        def index(j, k):
            inside = (j >= s) & (j < s + nb)
            return jnp.where(inside, k, 0), jnp.clip(j - s, 0, nb - 1)
        return pl.BlockSpec((tk, tn), index)

    return pl.pallas_call(
        _drop_operands(body, 1 + n, len(after)), name="wgrad_in", grid=(sum(nblk), nk),
        in_specs=[pl.BlockSpec((M, tk), lambda j, k: (0, k))] + [piece_spec(s, nb) for s, nb in zip(start, nblk)]
                 + [HBM_SPEC] * len(after),
        out_specs=pl.BlockSpec((tn, M), lambda j, k: (j, 0)),
        out_shape=jax.ShapeDtypeStruct((sum(nblk) * tn, M), BF16),
        scratch_shapes=[pltpu.VMEM((M, tn), F32)],
        compiler_params=_params(("parallel", "arbitrary")),
    )(ht, *dparts, *after)


def _adamw_math(w, g, m, v):
    m = ADAM_B1 * m + (1.0 - ADAM_B1) * g
    v = ADAM_B2 * v + (1.0 - ADAM_B2) * (g * g)
    m_hat = m / (1.0 - ADAM_B1 ** ADAM_STEP)
    v_hat = v / (1.0 - ADAM_B2 ** ADAM_STEP)
    delta = -ADAM_LR * (m_hat / (jnp.sqrt(v_hat) + ADAM_EPS) + ADAM_WD * w)
    return delta, m, v


def _adamw_sum(name, w, parts, m, v):
    R, C = w.shape
    tr = _row_tile(R)

    def body(w_ref, p_ref, m_ref, v_ref, g_out, d_out, m_out, v_out):
        g = p_ref[0].astype(F32)
        for k in range(1, 4):
            g = g + p_ref[k].astype(F32)
        g_out[...] = g
        d_out[...], m_out[...], v_out[...] = _adamw_math(w_ref[...], g, m_ref[...], v_ref[...])

    blk = pl.BlockSpec((tr, C), lambda i: (i, 0))
    out = jax.ShapeDtypeStruct((R, C), F32)
    return pl.pallas_call(
        body, name=name, grid=(R // tr,),
        in_specs=[blk, pl.BlockSpec((4, tr, C), lambda i: (0, i, 0)), blk, blk],
        out_specs=[blk, blk, blk, blk], out_shape=[out, out, out, out],
        compiler_params=_params(("parallel",)),
    )(w, parts, m, v)


def _small_sum(gathered):
    R = gathered.shape[1]

    def body(p_ref, o_ref):
        g = p_ref[0]
        for k in range(1, N_DEV):
            g = g + p_ref[k]
        o_ref[...] = g

    return pl.pallas_call(
        body, name="small_sum", in_specs=[_full(gathered.shape)], out_specs=_full((R, 128)), grid=(1,),
        out_shape=jax.ShapeDtypeStruct((R, 128), F32),
    )(gathered)


def _small_adamw(w, g, m, v):
    def body(w_ref, g_ref, m_ref, v_ref, d_out, m_out, v_out):
        d_out[...], m_out[...], v_out[...] = _adamw_math(w_ref[...], g_ref[...], m_ref[...], v_ref[...])

    out = jax.ShapeDtypeStruct(w.shape, F32)
    spec = _full(w.shape)
    return pl.pallas_call(
        body, name="small_adamw", grid=(1,), in_specs=[spec] * 4, out_specs=[spec] * 3, out_shape=[out, out, out],
    )(w, g, m, v)


def _row_tile(rows):
    for cand in (256, 128):
        if rows % cand == 0:
            return cand
    return rows


def _pair_sum(name, by_owner, got, core, after=()):
    n = len(got)

    def body(core_ref, *refs):
        for a_ref, b_ref, o_ref in zip(refs[:n], refs[n:2 * n], refs[2 * n:]):
            o_ref[...] = (a_ref[...].astype(F32) + b_ref[...].astype(F32)).astype(BF16)

    def blk(g):
        return pl.BlockSpec((None,) + g.shape[1:], lambda k, core_ref: (k, 0, 0))

    def mine(g):
        return pl.BlockSpec((None,) + g.shape[1:], lambda k, core_ref: (2 * k + core_ref[0], 0, 0))

    return pl.pallas_call(
        _drop_operands(body, 1 + 2 * n, len(after)), name=name,
        grid_spec=pltpu.PrefetchScalarGridSpec(
            num_scalar_prefetch=1, grid=(4,),
            in_specs=[mine(g) for g in got] + [blk(g) for g in got] + [HBM_SPEC] * len(after),
            out_specs=[blk(g) for g in got]),
        out_shape=[jax.ShapeDtypeStruct(g.shape, BF16) for g in got],
        compiler_params=_params(("parallel",)),
    )(core, *by_owner, *got, *after)


def _shards_from_cols(name, full):
    R, allc = full.shape
    c = allc // N_DEV
    tr = _row_tile(R)

    def body(f_ref, o_ref):
        for s in range(N_DEV):
            o_ref[s] = f_ref[:, s * c:(s + 1) * c]

    return pl.pallas_call(
        body, name=name, grid=(R // tr,),
        in_specs=[pl.BlockSpec((tr, allc), lambda i: (i, 0))],
        out_specs=pl.BlockSpec((N_DEV, tr, c), lambda i: (0, i, 0)),
        out_shape=jax.ShapeDtypeStruct((N_DEV, R, c), full.dtype),
        compiler_params=_params(("parallel",)),
    )(full)


MESH = pl.DeviceIdType.MESH
HBM_SPEC = pl.BlockSpec(memory_space=pl.ANY)


def _handshake(peers):
    barrier = pltpu.get_barrier_semaphore()
    for peer in peers:
        pl.semaphore_signal(barrier, inc=1, device_id=peer, device_id_type=MESH)
    pl.semaphore_wait(barrier, len(peers))


def _comm_call(body, name, operands, out_shape, scratch, collective_id):
    if collective_id is None:
        return pl.pallas_call(body, name=name, in_specs=[HBM_SPEC] * len(operands), out_specs=[HBM_SPEC] * len(out_shape),
                              out_shape=out_shape, scratch_shapes=scratch)(*operands)
    return pl.kernel(body, out_type=out_shape, mesh=plsc.ScalarSubcoreMesh(axis_name="sequencer", num_cores=1),
                     scratch_types=scratch, name=name,
                     compiler_params=pltpu.CompilerParams(collective_id=collective_id))(*operands)


def _all_gather(name, blocks, collective_id=None, after=()):
    n = len(blocks)
    na = len(after)

    def body(*refs):
        x_refs, out_refs = refs[:n], refs[n + na:2 * n + na]
        send_sems, recv_sems, local_sems = refs[2 * n + na:]
        x, y, c = lax.axis_index("x"), lax.axis_index("y"), lax.axis_index("c")
        me, sibling = (x, y, c), (x, y, 1 - c)
        chips = [(1 - x, y), (x, 1 - y), (1 - x, 1 - y)]
        if collective_id is not None:
            _handshake([sibling] + [(*chip, c) for chip in chips])

        def slot(i, px, py, pc):
            return out_refs[i].at[4 * px + 2 * py + pc]

        def copy(i, k, blk, to, src=None):
            return pltpu.make_async_remote_copy(
                src_ref=slot(i, *blk) if src is None else src, dst_ref=slot(i, *blk),
                send_sem=send_sems.at[7 * i + k], recv_sem=recv_sems.at[7 * i + k], device_id=to, device_id_type=MESH)

        mine = [pltpu.make_async_copy(x_refs[i], slot(i, *me), local_sems.at[i]) for i in range(n)]
        for cp in mine:
            cp.start()
        first = []
        for i in range(n):
            first.append(copy(i, 0, me, sibling, src=x_refs[i]))
            first += [copy(i, 1 + j, me, (*chip, c), src=x_refs[i]) for j, chip in enumerate(chips)]
        for cp in first:
            cp.start()
        passed = []
        for i in range(n):
            for j, chip in enumerate(chips):
                copy(i, 1 + j, (*chip, c), me).wait_recv()
                passed.append(copy(i, 4 + j, (*chip, c), sibling))
                passed[-1].start()
        for i in range(n):
            copy(i, 0, sibling, me).wait_recv()
            for j, chip in enumerate(chips):
                copy(i, 4 + j, (*chip, 1 - c), me).wait_recv()
        for cp in first + passed:
            cp.wait_send()
        for cp in mine:
            cp.wait()

    return _comm_call(
        body, name, list(blocks) + list(after), [jax.ShapeDtypeStruct((N_DEV,) + b.shape, b.dtype) for b in blocks],
        [pltpu.SemaphoreType.DMA((7 * n,)), pltpu.SemaphoreType.DMA((7 * n,)), pltpu.SemaphoreType.DMA((n,))],
        collective_id)


def _sibling_swap(name, by_owner, collective_id=None, after=()):
    n = len(by_owner)
    na = len(after)

    def body(*refs):
        x_refs, out_refs = refs[:n], refs[n + na:2 * n + na]
        send_sems, recv_sems = refs[2 * n + na:]
        x, y, c = lax.axis_index("x"), lax.axis_index("y"), lax.axis_index("c")
        if collective_id is not None:
            _handshake([(x, y, 1 - c)])
        copies = []
        for i in range(n):
            for k in range(4):
                copies.append(pltpu.make_async_remote_copy(
                    src_ref=x_refs[i].at[2 * k + 1 - c], dst_ref=out_refs[i].at[k],
                    send_sem=send_sems.at[4 * i + k], recv_sem=recv_sems.at[4 * i + k],
                    device_id=(x, y, 1 - c), device_id_type=MESH))
        for cp in copies:
            cp.start()
        for cp in copies:
            cp.wait()

    return _comm_call(
        body, name, list(by_owner) + list(after),
        [jax.ShapeDtypeStruct((4,) + b.shape[1:], b.dtype) for b in by_owner],
        [pltpu.SemaphoreType.DMA((4 * n,)), pltpu.SemaphoreType.DMA((4 * n,))], collective_id)


def _chip_exchange(name, sums, collective_id=None, after=()):
    n = len(sums)
    na = len(after)

    def body(*refs):
        x_refs, out_refs = refs[:n], refs[n + na:2 * n + na]
        send_sems, recv_sems, local_sems = refs[2 * n + na:]
        x, y, c = lax.axis_index("x"), lax.axis_index("y"), lax.axis_index("c")
        chips = [(1 - x, y), (x, 1 - y), (1 - x, 1 - y)]
        my_chip = 2 * x + y
        if collective_id is not None:
            _handshake([(cx, cy, c) for cx, cy in chips])
        mine = [pltpu.make_async_copy(x_refs[i].at[my_chip], out_refs[i].at[my_chip], local_sems.at[i])
                for i in range(n)]
        for cp in mine:
            cp.start()
        sends = []
        for i in range(n):
            for j, (cx, cy) in enumerate(chips):
                sends.append(pltpu.make_async_remote_copy(
                    src_ref=x_refs[i].at[2 * cx + cy], dst_ref=out_refs[i].at[my_chip],
                    send_sem=send_sems.at[3 * i + j], recv_sem=recv_sems.at[3 * i + j],
                    device_id=(cx, cy, c), device_id_type=MESH))
        for cp in sends:
            cp.start()
        for i in range(n):
            for j, (cx, cy) in enumerate(chips):
                pltpu.make_async_remote_copy(
                    src_ref=x_refs[i].at[my_chip], dst_ref=out_refs[i].at[2 * cx + cy],
                    send_sem=send_sems.at[3 * i + j], recv_sem=recv_sems.at[3 * i + j],
                    device_id=(cx, cy, c), device_id_type=MESH).wait_recv()
        for cp in sends:
            cp.wait_send()
        for cp in mine:
            cp.wait()

    return _comm_call(
        body, name, list(sums) + list(after), [jax.ShapeDtypeStruct(s.shape, s.dtype) for s in sums],
        [pltpu.SemaphoreType.DMA((3 * n,)), pltpu.SemaphoreType.DMA((3 * n,)), pltpu.SemaphoreType.DMA((n,))],
        collective_id)


def _cast_shards(shards):
    n = len(shards)

    def body(*refs):
        for i in range(n):
            refs[n + i][...] = refs[i][...].astype(BF16)

    vmem = pl.BlockSpec(memory_space=pltpu.VMEM)
    return pl.pallas_call(
        body, name="cast_shards", in_specs=[vmem] * n, out_specs=[vmem] * n,
        out_shape=[jax.ShapeDtypeStruct(s.shape, BF16) for s in shards],
        compiler_params=pltpu.CompilerParams(vmem_limit_bytes=VMEM_LIMIT_V7X),
    )(*shards)


BIG = ("w_in", "w_branch_a", "w_branch_b", "w_out", "w_ffn_gate", "w_ffn_up", "w_ffn_down")


def _local_step(x, target, gains, low, conv_w, wg8, reduce):
    g_mix, g_hg, g_ffn, g_fin = gains
    w_in = wg8["w_in"].reshape(N_IN, D_MODEL)
    wg = wg8["w_ffn_gate"].reshape(D_FF, D_MODEL)
    wu = wg8["w_ffn_up"].reshape(D_FF, D_MODEL)
    wa, wb = wg8["w_branch_a"], wg8["w_branch_b"]
    wo = wg8["w_out"].reshape(D_MODEL, D_MODEL)
    wd = wg8["w_ffn_down"].reshape(D_FF, D_MODEL)

    ht, hg, cv, gt = _fwd_in(x, g_mix, w_in)
    o, og, ogt, st = _hg_fwd(hg, low, g_hg)
    cvo, cvot = _conv_fwd(cv, conv_w)
    x1, mgt = _merge_fwd(og, cvo, gt, x, wa, wb, wo)
    h2t, gate, up, actt, loss, d_gfin, dx2 = _ffn_fwd_loss(x1, g_ffn, wg, wu, wd, target, g_fin)

    dgate, dup, dx1, d_gffn = _ffn_bwd(dx2, x1, gate, up, g_ffn, wg, wu, wd)
    ffn = dict(
        w_ffn_down=_wgrad("wgrad_ffn_down", actt, dx2, 512).reshape(N_DEV, D_FF // N_DEV, D_MODEL),
        w_ffn_gate=_wgrad("wgrad_ffn_gate", h2t, dgate, 1408, transposed=True).reshape(N_DEV, D_FF // N_DEV, D_MODEL),
        w_ffn_up=_wgrad("wgrad_ffn_up", h2t, dup, 1408, transposed=True).reshape(N_DEV, D_FF // N_DEV, D_MODEL))
    sums_ffn, got_ffn = reduce.begin(ffn)
    dgt, dya, dyb, dog, dcvo = _merge_bwd(dx1, og, cvo, gt, wa, wb, wo)
    out = dict(
        w_out=_wgrad("wgrad_out", mgt, dx1, 512).reshape(N_DEV, D_MODEL // N_DEV, D_MODEL),
        w_branch_a=_shards_from_cols("split_w_branch_a", _wgrad("wgrad_branch_a", ogt, dya, 512)),
        w_branch_b=_shards_from_cols("split_w_branch_b", _wgrad("wgrad_branch_b", cvot, dyb, 512)))
    sums_out, got_out = reduce.begin(out, after=got_ffn[:1])
    parts_ffn, _ = reduce.finish(ffn, sums_ffn, after=got_out[:1])
    dq, df, di, dg, d_low, d_ghg = _hg_bwd(dog, hg, o, st, low, g_hg, after=list(sums_ffn) + list(sums_out))
    dc, db, dxb, d_conv = _conv_bwd(dcvo, cv, conv_w, after=[dq])
    parts_out, updated_out = reduce.finish(out, sums_out, after=[parts_ffn[0], dc])
    dparts = [dq, df, di, dg, dc, db, dxb, dgt]
    w_in_grad = dict(w_in=_wgrad_in(ht, dparts, after=parts_ffn[:1]).reshape(N_DEV, N_IN // N_DEV, D_MODEL))
    sums_in, _ = reduce.begin(w_in_grad, after=parts_out[:1], sum_after=updated_out)
    parts_in, _ = reduce.finish(w_in_grad, sums_in)
    grad_x, d_gmix = _in_bwd(dparts, w_in, x, dx1, g_mix, after=list(parts_out[:1]) + list(sums_in))
    small = dict(norm_mix_g=d_gmix, norm_ffn_g=d_gffn, norm_final_g=d_gfin, lower_bounds=d_low, hg_norm_g=d_ghg,
                 conv_w=d_conv, loss=loss)
    return grad_x, small, parts_in


_SMALL_LAYOUT = (("norm_mix_g", 0, 8), ("norm_ffn_g", 8, 8), ("norm_final_g", 16, 8), ("lower_bounds", 24, 8),
                 ("hg_norm_g", 32, 1))
_LOSS_ROW = 40
_CONV_ROW = 48


def _pad_rows(a, rows):
    return jnp.pad(a, ((0, rows - a.shape[0]), (0, 0)))


def _pack_small(vals, conv_rows):
    parts = [_pad_rows(vals[name].reshape(rows, 128), 8) for name, _, rows in _SMALL_LAYOUT]
    loss = vals["loss"][:, :128] if "loss" in vals else jnp.zeros((1, 128), F32)
    parts.append(_pad_rows(loss, 8))
    parts.append(_pad_rows(conv_rows, SMALL_ROWS - _CONV_ROW))
    return jnp.concatenate(parts, axis=0)


def _conv_shard_rows(a):
    return jnp.pad(a, ((0, 5), (0, 64)))


def kernel(x, norm_mix_g, w_in, lower_bounds, hg_norm_g, conv_w, w_branch_a, w_branch_b, w_out, norm_ffn_g, w_ffn_gate, w_ffn_up, w_ffn_down, norm_final_g, loss_target, m_norm_mix_g, m_w_in, m_lower_bounds, m_hg_norm_g, m_conv_w, m_w_branch_a, m_w_branch_b, m_w_out, m_norm_ffn_g, m_w_ffn_gate, m_w_ffn_up, m_w_ffn_down, m_norm_final_g, v_norm_mix_g, v_w_in, v_lower_bounds, v_hg_norm_g, v_conv_w, v_w_branch_a, v_w_branch_b, v_w_out, v_norm_ffn_g, v_w_ffn_gate, v_w_ffn_up, v_w_ffn_down, v_norm_final_g):
    cx, cy, cc = lax.axis_index("x"), lax.axis_index("y"), lax.axis_index("c")
    my_dev = 4 * cx + 2 * cy + cc

    def tr(a):
        return a[0].T

    big = dict(w_in=tr(w_in), w_branch_a=w_branch_a[0], w_branch_b=w_branch_b[0], w_out=w_out[0],
               w_ffn_gate=tr(w_ffn_gate), w_ffn_up=tr(w_ffn_up), w_ffn_down=w_ffn_down[0])
    big_m = dict(w_in=tr(m_w_in), w_branch_a=m_w_branch_a[0], w_branch_b=m_w_branch_b[0], w_out=m_w_out[0],
                 w_ffn_gate=tr(m_w_ffn_gate), w_ffn_up=tr(m_w_ffn_up), w_ffn_down=m_w_ffn_down[0])
    big_v = dict(w_in=tr(v_w_in), w_branch_a=v_w_branch_a[0], w_branch_b=v_w_branch_b[0], w_out=v_w_out[0],
                 w_ffn_gate=tr(v_w_ffn_gate), w_ffn_up=tr(v_w_ffn_up), w_ffn_down=v_w_ffn_down[0])
    transposed = ("w_in", "w_ffn_gate", "w_ffn_up")

    shards = dict(zip(BIG, _cast_shards([big[n] for n in BIG])))
    first = _all_gather("gather_w_in", [shards["w_in"], _conv_shard_rows(conv_w[0])])
    ids = iter(range(1, 16))
    mid = _all_gather("gather_mid", [shards[n] for n in BIG[1:4]], collective_id=next(ids), after=first[1:])
    ffn = _all_gather("gather_ffn", [shards[n] for n in BIG[4:]], collective_id=next(ids), after=first[1:])
    wg8 = dict(zip(BIG, [first[0]] + list(mid) + list(ffn)))
    conv_full = first[1][:, :3, :64].transpose(1, 0, 2).reshape(3, CONV_WIDTH)

    core = cc.reshape(1).astype(jnp.int32)
    outs = {}

    class Reduce:
        @staticmethod
        def begin(grads, after=(), sum_after=()):
            names = list(grads)
            by_owner = [grads[n] for n in names]
            got = _sibling_swap("sibling_swap_" + names[0], by_owner, collective_id=next(ids), after=after)
            sums = _pair_sum("pair_sum_" + names[0], by_owner, got, core, after=sum_after)
            return sums, got

        @staticmethod
        def finish(grads, chip_sums, after=()):
            names = list(grads)
            parts = _chip_exchange("chip_exchange_" + names[0], chip_sums, collective_id=next(ids), after=after)
            for n, p in zip(names, parts):
                outs[n] = _adamw_sum("adamw_" + n, big[n], p, big_m[n], big_v[n])
            return parts, [outs[n][1] for n in names]

    gains = (norm_mix_g, hg_norm_g, norm_ffn_g, norm_final_g.reshape(1, D_MODEL))
    grad_x, small, last = _local_step(x[0], loss_target[0], gains, lower_bounds, conv_full, wg8, Reduce)

    small_all = _all_gather("gather_small", [_pack_small(small, small["conv_w"].reshape(12, 128))],
                            collective_id=next(ids), after=last[:1])
    ssum = _small_sum(small_all[0])
    conv_g_full = ssum[_CONV_ROW:_CONV_ROW + 12].reshape(3, CONV_WIDTH)
    conv_g = lax.dynamic_slice_in_dim(conv_g_full, my_dev * 64, 64, axis=1)
    loss = ssum[_LOSS_ROW, 0]
    g_rows = jnp.concatenate([ssum[:_CONV_ROW], _pad_rows(_conv_shard_rows(conv_g), SMALL_ROWS - _CONV_ROW)], axis=0)

    def pack_state(a):
        vals = dict(norm_mix_g=a[0], norm_ffn_g=a[1], norm_final_g=a[2], lower_bounds=a[3], hg_norm_g=a[4])
        return _pack_small(vals, _conv_shard_rows(a[5][0]))

    sw = pack_state((norm_mix_g, norm_ffn_g, norm_final_g, lower_bounds, hg_norm_g, conv_w))
    sm = pack_state((m_norm_mix_g, m_norm_ffn_g, m_norm_final_g, m_lower_bounds, m_hg_norm_g, m_conv_w))
    sv = pack_state((v_norm_mix_g, v_norm_ffn_g, v_norm_final_g, v_lower_bounds, v_hg_norm_g, v_conv_w))
    s_delta, s_m, s_v = _small_adamw(sw, g_rows, sm, sv)

    shapes = dict(norm_mix_g=(1, D_MODEL), norm_ffn_g=(1, D_MODEL), norm_final_g=(D_MODEL,),
                  lower_bounds=(2, HG_WIDTH), hg_norm_g=(1, HEAD_DIM))

    def unpack(buf, name):
        if name == "conv_w":
            return buf[_CONV_ROW:_CONV_ROW + 3, :64].reshape(1, 3, 64)
        for nm, off, rows in _SMALL_LAYOUT:
            if nm == name:
                return buf[off:off + rows].reshape(shapes[name])
        raise KeyError(name)

    order = ["norm_mix_g", "w_in", "lower_bounds", "hg_norm_g", "conv_w", "w_branch_a", "w_branch_b", "w_out",
             "norm_ffn_g", "w_ffn_gate", "w_ffn_up", "w_ffn_down", "norm_final_g"]
    result = [loss, grad_x[None]]
    for k, sbuf in enumerate((g_rows, s_delta, s_m, s_v)):
        for n in order:
            if n in outs:
                result.append((outs[n][k].T if n in transposed else outs[n][k])[None])
            else:
                result.append(unpack(sbuf, n))
    return tuple(result)
```

```python
import jax
import jax.numpy as jnp
from jax import lax
from jax.experimental import pallas as pl
from jax.experimental.pallas import tpu as pltpu
from jax.experimental.pallas import tpu_sc as plsc

F32 = jnp.float32
BF16 = jnp.bfloat16
STASH = jnp.bfloat16

D_MODEL = 1024
HG_WIDTH = 512
HEAD_DIM = 128
N_HEADS = 4
HEADS_PER_STEP = 4
HEAD_GROUPS = N_HEADS // HEADS_PER_STEP
CONV_WIDTH = 512
D_FF = 2816
CHUNK = 32
EPS = 1e-6
Q_SCALE = HEAD_DIM ** -0.5
N_DEV = 8

ADAM_LR = 0.001
ADAM_B1 = 0.9
ADAM_B2 = 0.999
ADAM_EPS = 1e-08
ADAM_WD = 0.01
ADAM_STEP = 10

VMEM_LIMIT_V7X = 56 * 1024 * 1024

SMALL_ROWS = 64


def _params(sem, vmem=VMEM_LIMIT_V7X):
    return pltpu.CompilerParams(dimension_semantics=sem, vmem_limit_bytes=vmem)


def _mm(a, b):
    return jnp.dot(a.astype(BF16), b.astype(BF16), preferred_element_type=F32)


def _mm_nt(a, b):
    return lax.dot_general(a.astype(BF16), b.astype(BF16), (((1,), (1,)), ((), ())), preferred_element_type=F32)


def _mm_tn(a, b):
    return lax.dot_general(a.astype(BF16), b.astype(BF16), (((0,), (0,)), ((), ())), preferred_element_type=F32)


def _sigmoid(x):
    return 0.5 * jnp.tanh(0.5 * x) + 0.5


def _resident(shape):
    nd = len(shape)
    return pl.BlockSpec(shape, lambda *_: (0,) * nd, pipeline_mode=pl.Buffered(1))


def _full(shape):
    nd = len(shape)
    return pl.BlockSpec(shape, lambda *_: (0,) * nd)


def _shard_cols(w_ref):
    return jnp.concatenate([w_ref[s] for s in range(N_DEV)], axis=1)


N_HG = 4 * HG_WIDTH
N_CV = 3 * CONV_WIDTH
N_GT = 2 * D_MODEL
N_IN = N_HG + N_CV + N_GT


def _col(tm, n):
    return pl.BlockSpec((n, tm), lambda i: (0, i))


def _fwd_in(x, g, w_in_t):
    T = x.shape[0]
    tm = min(512, T)

    def body(x_ref, g_ref, w_ref, ht_ref, hg_ref, cv_ref, gt_ref):
        xv = x_ref[...]
        r = lax.rsqrt(jnp.mean(xv * xv, axis=-1, keepdims=True) + EPS)
        hf = xv * r * g_ref[...]
        h = hf.astype(BF16)
        ht_ref[...] = hf.T.astype(BF16)
        hg_ref[...] = _mm_nt(h, w_ref[:N_HG, :])
        cv_ref[...] = _mm_nt(h, w_ref[N_HG:N_HG + N_CV, :]).astype(STASH)
        gt_ref[...] = _mm_nt(h, w_ref[N_HG + N_CV:, :]).astype(STASH)

    row = lambda n: pl.BlockSpec((tm, n), lambda i: (i, 0))
    return pl.pallas_call(
        body, name="fwd_in", grid=(T // tm,),
        in_specs=[row(D_MODEL), _full((1, D_MODEL)), _resident(w_in_t.shape)],
        out_specs=[_col(tm, D_MODEL), row(N_HG), row(N_CV), row(N_GT)],
        out_shape=[jax.ShapeDtypeStruct((D_MODEL, T), BF16), jax.ShapeDtypeStruct((T, N_HG), F32),
                   jax.ShapeDtypeStruct((T, N_CV), STASH), jax.ShapeDtypeStruct((T, N_GT), STASH)],
        compiler_params=_params(("parallel",)),
    )(x, g, w_in_t)


def _chunk_pos(shape):
    return lax.broadcasted_iota(jnp.int32, shape, 0) & (CHUNK - 1)


def _chunk_cumsum(x, pos):
    s = 1
    while s < CHUNK:
        x = x + jnp.where(pos >= s, pltpu.roll(x, s, 0), 0.0)
        s *= 2
    return x


def _chunk_rev_cumsum(x, pos):
    n = x.shape[0]
    s = 1
    while s < CHUNK:
        x = x + jnp.where(pos + s < CHUNK, pltpu.roll(x, n - s, 0), 0.0)
        s *= 2
    return x


def _chunk_bcast(x3, row, tb):
    return jnp.broadcast_to(x3[:, row:row + 1, :], x3.shape).reshape(tb, x3.shape[-1])


def _lower_bound(low_ref):
    l0 = low_ref[0:1, :]
    l1 = low_ref[1:2, :]
    m = jnp.maximum(l0, l1)
    e0 = jnp.exp(l0 - m)
    e1 = jnp.exp(l1 - m)
    return e0 / (e0 + e1), e1 / (e0 + e1)


def _hg_gates(qr, fr, lb, pos, tb):
    sq = _sigmoid(qr)
    q = qr * sq * Q_SCALE
    sg = _sigmoid(fr)
    f = lb + (1.0 - lb) * sg
    k = 1.0 - f
    b = _chunk_cumsum(jnp.log(f), pos)
    b3 = b.reshape(tb // CHUNK, CHUNK, HEAD_DIM)
    anc = _chunk_bcast(b3, CHUNK // 2 - 1, tb)
    blb = _chunk_bcast(b3, CHUNK - 1, tb)
    e_qa = jnp.exp(b - anc)
    e_ka = jnp.exp(anc - b)
    e_b = jnp.exp(b)
    e_ko = jnp.exp(blb - b)
    dec = jnp.exp(blb)
    return sq, q, sg, f, k, e_qa, e_ka, e_b, e_ko, dec


def _intra_mask(sb):
    r = lax.broadcasted_iota(jnp.int32, (sb, sb), 0)
    c = lax.broadcasted_iota(jnp.int32, (sb, sb), 1)
    return ((r // CHUNK) == (c // CHUNK)) & (c <= r)


def _hg_fwd(hg, low, gn):
    T = hg.shape[0]
    tb = min(512, T)
    sb = min(256, tb)
    nb = T // tb
    nc = tb // CHUNK
    wid = HEADS_PER_STEP * HEAD_DIM

    def body(q_ref, f_ref, i_ref, g_ref, low_ref, gn_ref, o_ref, og_ref, ogt_ref, st_ref, s_scr):
        t = pl.program_id(1)

        @pl.when(t == 0)
        def _():
            s_scr[...] = jnp.zeros_like(s_scr)

        pos = _chunk_pos((tb, HEAD_DIM))
        mask = _intra_mask(sb)
        lanes = [slice(hh * HEAD_DIM, (hh + 1) * HEAD_DIM) for hh in range(HEADS_PER_STEP)]
        qi, ko, vb, dec, st = [], [], [], [], []
        for hh, ln in enumerate(lanes):
            lb, _ = _lower_bound(low_ref.at[:, ln])
            _, q, _, _, k, e_qa, e_ka, e_b, e_ko, dec_h = _hg_gates(q_ref[:, ln], f_ref[:, ln], lb, pos, tb)
            qh = (q * e_qa).astype(BF16)
            kh = (k * e_ka).astype(BF16)
            qi.append((q * e_b).astype(BF16))
            ko.append((k * e_ko).astype(BF16))
            vb.append(i_ref[:, ln].astype(BF16))
            dec.append(dec_h)
            st.append(s_scr[hh])
            for s in range(tb // sb):
                sl = slice(s * sb, (s + 1) * sb)
                p = jnp.where(mask, _mm_nt(qh[sl], kh[sl]), 0.0)
                o_ref[sl, ln] = _mm(p, vb[hh][sl])
        for c in range(nc):
            sl = slice(c * CHUNK, (c + 1) * CHUNK)
            for hh, ln in enumerate(lanes):
                st_ref[hh, c] = st[hh]
                o_ref[sl, ln] = o_ref[sl, ln] + _mm_nt(qi[hh][sl], st[hh])
                st[hh] = dec[hh][c * CHUNK:c * CHUNK + 1, :] * st[hh] + _mm_tn(vb[hh][sl], ko[hh][sl])
        for hh, ln in enumerate(lanes):
            s_scr[hh] = st[hh]
            o = o_ref[:, ln]
            r = lax.rsqrt(jnp.mean(o * o, axis=-1, keepdims=True) + EPS)
            gr = g_ref[:, ln]
            og = (o * r * gn_ref[...]) * (gr * _sigmoid(gr))
            og_ref[:, ln] = og.astype(BF16)
            ogt_ref[ln, :] = og.T.astype(BF16)

    col = lambda p: pl.BlockSpec((tb, wid), lambda h, t: (t, p * HEAD_GROUPS + h))
    hcol = pl.BlockSpec((tb, wid), lambda h, t: (t, h))
    return pl.pallas_call(
        body, name="hg_fwd", grid=(HEAD_GROUPS, nb),
        in_specs=[col(0), col(1), col(2), col(3), pl.BlockSpec((2, wid), lambda h, t: (0, h)),
                  pl.BlockSpec((1, HEAD_DIM), lambda h, t: (0, 0))],
        out_specs=[hcol, hcol, pl.BlockSpec((wid, tb), lambda h, t: (h, t)),
                   pl.BlockSpec((HEADS_PER_STEP, nc, HEAD_DIM, HEAD_DIM), lambda h, t: (h, t, 0, 0))],
        out_shape=[jax.ShapeDtypeStruct((T, HG_WIDTH), F32), jax.ShapeDtypeStruct((T, HG_WIDTH), BF16),
                   jax.ShapeDtypeStruct((HG_WIDTH, T), BF16),
                   jax.ShapeDtypeStruct((N_HEADS, T // CHUNK, HEAD_DIM, HEAD_DIM), F32)],
        scratch_shapes=[pltpu.VMEM((HEADS_PER_STEP, HEAD_DIM, HEAD_DIM), F32)],
        compiler_params=_params(("parallel", "arbitrary")),
    )(hg, hg, hg, hg, low, gn)


def _conv_fwd(cv, conv_w):
    T = cv.shape[0]
    nj = CONV_WIDTH // 128

    def body(c_ref, b_ref, x_ref, w_ref, o_ref, ot_ref):
        row = lax.broadcasted_iota(jnp.int32, (T, 128), 0)
        u = c_ref[...].astype(F32) * x_ref[...].astype(F32)
        u1 = jnp.where(row >= 1, pltpu.roll(u, 1, 0), 0.0)
        u2 = jnp.where(row >= 2, pltpu.roll(u, 2, 0), 0.0)
        y = w_ref[0:1, :] * u2 + w_ref[1:2, :] * u1 + w_ref[2:3, :] * u
        out = b_ref[...].astype(F32) * y
        o_ref[...] = out.astype(BF16)
        ot_ref[...] = out.T.astype(BF16)

    col = lambda p: pl.BlockSpec((T, 128), lambda j: (0, p * nj + j))
    return pl.pallas_call(
        body, name="conv_fwd", grid=(nj,),
        in_specs=[col(0), col(1), col(2), pl.BlockSpec((3, 128), lambda j: (0, j))],
        out_specs=[pl.BlockSpec((T, 128), lambda j: (0, j)), pl.BlockSpec((128, T), lambda j: (j, 0))],
        out_shape=[jax.ShapeDtypeStruct((T, CONV_WIDTH), BF16), jax.ShapeDtypeStruct((CONV_WIDTH, T), BF16)],
        compiler_params=_params(("parallel",)),
    )(cv, cv, cv, conv_w)


def _merge_fwd(og, cvo, gt, x, wa, wb, wo):
    T = x.shape[0]
    tm = min(512, T)

    def body(og_ref, cvo_ref, gt_ref, x_ref, wa_ref, wb_ref, wo_ref, x1_ref, mgt_ref):
        ya = jnp.dot(og_ref[...], _shard_cols(wa_ref), preferred_element_type=F32)
        yb = jnp.dot(cvo_ref[...], _shard_cols(wb_ref), preferred_element_type=F32)
        m = (_sigmoid(gt_ref[:, :D_MODEL].astype(F32)) * ya
             + _sigmoid(gt_ref[:, D_MODEL:].astype(F32)) * yb)
        mgt_ref[...] = m.T.astype(BF16)
        x1_ref[...] = x_ref[...] + jnp.dot(m.astype(BF16), wo_ref[...], preferred_element_type=F32)

    row = lambda n: pl.BlockSpec((tm, n), lambda i: (i, 0))
    return pl.pallas_call(
        body, name="merge_fwd", grid=(T // tm,),
        in_specs=[row(HG_WIDTH), row(CONV_WIDTH), row(2 * D_MODEL), row(D_MODEL),
                  _resident(wa.shape), _resident(wb.shape), _resident(wo.shape)],
        out_specs=[row(D_MODEL), _col(tm, D_MODEL)],
        out_shape=[jax.ShapeDtypeStruct((T, D_MODEL), F32), jax.ShapeDtypeStruct((D_MODEL, T), BF16)],
        compiler_params=_params(("parallel",)),
    )(og, cvo, gt, x, wa, wb, wo)


def _ffn_fwd_loss(x1, g, wg, wu, wd, target, g_fin):
    T = x1.shape[0]
    tm = min(256, T)

    def body(x_ref, g_ref, wg_ref, wu_ref, wd_ref, t_ref, gf_ref,
             ht_ref, gate_ref, up_ref, actt_ref, loss_ref, dgf_ref, dx2_ref):
        @pl.when(pl.program_id(0) == 0)
        def _():
            loss_ref[...] = jnp.zeros_like(loss_ref)
            dgf_ref[...] = jnp.zeros_like(dgf_ref)

        xv = x_ref[...]
        r = lax.rsqrt(jnp.mean(xv * xv, axis=-1, keepdims=True) + EPS)
        hf = xv * r * g_ref[...]
        h = hf.astype(BF16)
        ht_ref[...] = hf.T.astype(BF16)
        gate = _mm_nt(h, wg_ref[...])
        up = _mm_nt(h, wu_ref[...])
        gate_ref[...] = gate.astype(STASH)
        up_ref[...] = up.astype(STASH)
        act = gate * _sigmoid(gate) * up
        actt_ref[...] = act.T.astype(BF16)
        x2 = xv + jnp.dot(act.astype(BF16), wd_ref[...], preferred_element_type=F32)

        gv = gf_ref[...]
        r2 = lax.rsqrt(jnp.mean(x2 * x2, axis=-1, keepdims=True) + EPS)
        xh = x2 * r2
        err = xh * gv - t_ref[...]
        loss_ref[...] += 0.5 * jnp.sum(jnp.mean(err * err, axis=-1, keepdims=True), axis=0, keepdims=True)
        dy = err * (1.0 / D_MODEL)
        dgf_ref[...] += jnp.sum(dy * xh, axis=0, keepdims=True)
        w = dy * gv
        dx2_ref[...] = r2 * (w - xh * jnp.mean(w * xh, axis=-1, keepdims=True))

    row = lambda n: pl.BlockSpec((tm, n), lambda i: (i, 0))
    return pl.pallas_call(
        body, name="ffn_fwd_loss", grid=(T // tm,),
        in_specs=[row(D_MODEL), _full((1, D_MODEL)), _resident(wg.shape), _resident(wu.shape), _resident(wd.shape),
                  row(D_MODEL), _full((1, D_MODEL))],
        out_specs=[_col(tm, D_MODEL), row(D_FF), row(D_FF), _col(tm, D_FF), _full((1, 128)), _full((1, D_MODEL)),
                   row(D_MODEL)],
        out_shape=[jax.ShapeDtypeStruct((D_MODEL, T), BF16), jax.ShapeDtypeStruct((T, D_FF), STASH),
                   jax.ShapeDtypeStruct((T, D_FF), STASH), jax.ShapeDtypeStruct((D_FF, T), BF16),
                   jax.ShapeDtypeStruct((1, 128), F32), jax.ShapeDtypeStruct((1, D_MODEL), F32),
                   jax.ShapeDtypeStruct((T, D_MODEL), F32)],
        compiler_params=_params(("arbitrary",)),
    )(x1, g, wg, wu, wd, target, g_fin)


def _ffn_bwd(dx2, x1, gate, up, g, wg, wu, wd):
    T = x1.shape[0]
    tm = min(256, T)

    def body(dx2_ref, x_ref, gate_ref, up_ref, g_ref, wg_ref, wu_ref, wd_ref, dgate_ref, dup_ref, dx1_ref, dgn_ref):
        @pl.when(pl.program_id(0) == 0)
        def _():
            dgn_ref[...] = jnp.zeros_like(dgn_ref)

        dx2 = dx2_ref[...]
        dact = _mm_nt(dx2, wd_ref[...])
        gate = gate_ref[...].astype(F32)
        s = _sigmoid(gate)
        dgate = (dact * up_ref[...].astype(F32) * (s * (1.0 + gate * (1.0 - s)))).astype(BF16)
        dup = (dact * (gate * s)).astype(BF16)
        dgate_ref[...] = dgate
        dup_ref[...] = dup
        dh = _mm(dgate, wg_ref[...]) + _mm(dup, wu_ref[...])
        xv = x_ref[...]
        r = lax.rsqrt(jnp.mean(xv * xv, axis=-1, keepdims=True) + EPS)
        xh = xv * r
        dgn_ref[...] += jnp.sum(dh * xh, axis=0, keepdims=True)
        w = dh * g_ref[...]
        dx1_ref[...] = dx2 + r * (w - xh * jnp.mean(w * xh, axis=-1, keepdims=True))

    row = lambda n: pl.BlockSpec((tm, n), lambda i: (i, 0))
    return pl.pallas_call(
        body, name="ffn_bwd", grid=(T // tm,),
        in_specs=[row(D_MODEL), row(D_MODEL), row(D_FF), row(D_FF), _full((1, D_MODEL)),
                  _resident(wg.shape), _resident(wu.shape), _resident(wd.shape)],
        out_specs=[row(D_FF), row(D_FF), row(D_MODEL), _full((1, D_MODEL))],
        out_shape=[jax.ShapeDtypeStruct((T, D_FF), BF16), jax.ShapeDtypeStruct((T, D_FF), BF16),
                   jax.ShapeDtypeStruct((T, D_MODEL), F32), jax.ShapeDtypeStruct((1, D_MODEL), F32)],
        compiler_params=_params(("arbitrary",)),
    )(dx2, x1, gate, up, g, wg, wu, wd)


def _merge_bwd(dx1, og, cvo, gt, wa, wb, wo):
    T = dx1.shape[0]
    tm = min(512, T)

    def body(dx_ref, og_ref, cvo_ref, gt_ref, wa_ref, wb_ref, wo_ref, dgt_ref, dya_ref, dyb_ref, dog_ref, dcvo_ref):
        dm = _mm_nt(dx_ref[...], wo_ref[...])
        wa = _shard_cols(wa_ref)
        wb = _shard_cols(wb_ref)
        ya = jnp.dot(og_ref[...], wa, preferred_element_type=F32)
        yb = jnp.dot(cvo_ref[...], wb, preferred_element_type=F32)
        sa = _sigmoid(gt_ref[:, :D_MODEL].astype(F32))
        sb = _sigmoid(gt_ref[:, D_MODEL:].astype(F32))
        dgt_ref[:, :D_MODEL] = (dm * ya * (sa * (1.0 - sa))).astype(BF16)
        dgt_ref[:, D_MODEL:] = (dm * yb * (sb * (1.0 - sb))).astype(BF16)
        dya = (dm * sa).astype(BF16)
        dyb = (dm * sb).astype(BF16)
        dya_ref[...] = dya
        dyb_ref[...] = dyb
        dog_ref[...] = _mm_nt(dya, wa)
        dcvo_ref[...] = _mm_nt(dyb, wb)

    row = lambda n: pl.BlockSpec((tm, n), lambda i: (i, 0))
    return pl.pallas_call(
        body, name="merge_bwd", grid=(T // tm,),
        in_specs=[row(D_MODEL), row(HG_WIDTH), row(CONV_WIDTH), row(2 * D_MODEL),
                  _resident(wa.shape), _resident(wb.shape), _resident(wo.shape)],
        out_specs=[row(2 * D_MODEL), row(D_MODEL), row(D_MODEL), row(HG_WIDTH), row(CONV_WIDTH)],
        out_shape=[jax.ShapeDtypeStruct((T, 2 * D_MODEL), BF16), jax.ShapeDtypeStruct((T, D_MODEL), BF16),
                   jax.ShapeDtypeStruct((T, D_MODEL), BF16), jax.ShapeDtypeStruct((T, HG_WIDTH), F32),
                   jax.ShapeDtypeStruct((T, CONV_WIDTH), F32)],
        compiler_params=_params(("parallel",)),
    )(dx1, og, cvo, gt, wa, wb, wo)


def _conv_bwd(dcvo, cv, conv_w, after=()):
    T = cv.shape[0]
    nj = CONV_WIDTH // 128

    def body(do_ref, c_ref, b_ref, x_ref, w_ref, dc_ref, db_ref, dx_ref, dw_ref):
        row = lax.broadcasted_iota(jnp.int32, (T, 128), 0)
        c = c_ref[...].astype(F32)
        xb = x_ref[...].astype(F32)
        do = do_ref[...]
        u = c * xb
        u1 = jnp.where(row >= 1, pltpu.roll(u, 1, 0), 0.0)
        u2 = jnp.where(row >= 2, pltpu.roll(u, 2, 0), 0.0)
        w0, w1, w2 = w_ref[0:1, :], w_ref[1:2, :], w_ref[2:3, :]
        y = w0 * u2 + w1 * u1 + w2 * u
        db_ref[...] = (do * y).astype(BF16)
        dy = do * b_ref[...].astype(F32)
        dw_ref[0:1, :] = jnp.sum(dy * u2, axis=0, keepdims=True)
        dw_ref[1:2, :] = jnp.sum(dy * u1, axis=0, keepdims=True)
        dw_ref[2:3, :] = jnp.sum(dy * u, axis=0, keepdims=True)
        dy1 = jnp.where(row < T - 1, pltpu.roll(dy, T - 1, 0), 0.0)
        dy2 = jnp.where(row < T - 2, pltpu.roll(dy, T - 2, 0), 0.0)
        du = w2 * dy + w1 * dy1 + w0 * dy2
        dc_ref[...] = (du * xb).astype(BF16)
        dx_ref[...] = (du * c).astype(BF16)

    col = lambda p: pl.BlockSpec((T, 128), lambda j: (0, p * nj + j))
    one = pl.BlockSpec((T, 128), lambda j: (0, j))
    wspec = pl.BlockSpec((3, 128), lambda j: (0, j))
    out = jax.ShapeDtypeStruct((T, CONV_WIDTH), BF16)
    return pl.pallas_call(
        _drop_operands(body, 5, len(after)), name="conv_bwd", grid=(nj,),
        in_specs=[one, col(0), col(1), col(2), wspec] + [HBM_SPEC] * len(after),
        out_specs=[one, one, one, wspec],
        out_shape=[out, out, out, jax.ShapeDtypeStruct((3, CONV_WIDTH), F32)],
        compiler_params=_params(("parallel",)),
    )(dcvo, cv, cv, cv, conv_w, *after)


def _drop_operands(body, first, count):
    def wrapped(*refs):
        return body(*refs[:first], *refs[first + count:])
    return wrapped


def _hg_bwd(dog, hg, o, st, low, gn, after=()):
    T = hg.shape[0]
    tb = min(512, T)
    sb = min(256, tb)
    nb = T // tb
    nc = tb // CHUNK
    wid = HEADS_PER_STEP * HEAD_DIM

    def body(q_ref, f_ref, i_ref, g_ref, low_ref, gn_ref, o_ref, dog_ref, st_ref,
             dq_ref, df_ref, di_ref, dg_ref, dlow_ref, dgn_ref,
             ds_scr, dqi_scr, dko_scr, dv_scr, dd_scr, dqh_scr, dkh_scr):
        h = pl.program_id(0)
        t = pl.program_id(1)

        @pl.when(t == 0)
        def _():
            ds_scr[...] = jnp.zeros_like(ds_scr)
            dlow_ref[...] = jnp.zeros_like(dlow_ref)

        @pl.when((t == 0) & (h == 0))
        def _():
            dgn_ref[...] = jnp.zeros_like(dgn_ref)

        pos = _chunk_pos((tb, HEAD_DIM))
        mask = _intra_mask(sb)
        gnv = gn_ref[...]
        lanes = [slice(hh * HEAD_DIM, (hh + 1) * HEAD_DIM) for hh in range(HEADS_PER_STEP)]
        heads = []
        for hh, ln in enumerate(lanes):
            lb, lb1 = _lower_bound(low_ref.at[:, ln])
            qr = q_ref[:, ln]
            sq, q, sg, f, k, e_qa, e_ka, e_b, e_ko, dec = _hg_gates(qr, f_ref[:, ln], lb, pos, tb)

            gr = g_ref[:, ln]
            o = o_ref[:, ln]
            dog_v = dog_ref[:, ln]
            sgr = _sigmoid(gr)
            r = lax.rsqrt(jnp.mean(o * o, axis=-1, keepdims=True) + EPS)
            oh = o * r
            dg_ref[:, ln] = (dog_v * (oh * gnv) * (sgr * (1.0 + gr * (1.0 - sgr)))).astype(BF16)
            don = dog_v * (gr * sgr)
            dgn_ref[...] += jnp.sum(don * oh, axis=0, keepdims=True)
            w = don * gnv
            do = (r * (w - oh * jnp.mean(w * oh, axis=-1, keepdims=True))).astype(BF16)

            qh = (q * e_qa).astype(BF16)
            kh = (k * e_ka).astype(BF16)
            qi = (q * e_b).astype(BF16)
            ko = (k * e_ko).astype(BF16)
            vb = i_ref[:, ln].astype(BF16)

            for s in range(tb // sb):
                sl = slice(s * sb, (s + 1) * sb)
                p = jnp.where(mask, _mm_nt(qh[sl], kh[sl]), 0.0).astype(BF16)
                dp = jnp.where(mask, _mm_nt(do[sl], vb[sl]), 0.0).astype(BF16)
                dv_scr[sl, ln] = _mm_tn(p, do[sl])
                dqh_scr[sl, ln] = _mm(dp, kh[sl])
                dkh_scr[sl, ln] = _mm_tn(dp, qh[sl])
            heads.append(dict(lb=lb, lb1=lb1, qr=qr, sq=sq, q=q, sg=sg, f=f, k=k, e_qa=e_qa, e_ka=e_ka, e_b=e_b,
                              e_ko=e_ko, dec=dec, do=do, qi=qi, ko=ko, vb=vb, ds=ds_scr[hh]))

        for c in reversed(range(nc)):
            sl = slice(c * CHUNK, (c + 1) * CHUNK)
            for hh, ln in enumerate(lanes):
                hd = heads[hh]
                ds = hd["ds"]
                st_c = st_ref[hh, c]
                dqi_scr[sl, ln] = _mm(hd["do"][sl], st_c)
                dko_scr[sl, ln] = _mm(hd["vb"][sl], ds)
                dv_scr[sl, ln] = dv_scr[sl, ln] + _mm_nt(hd["ko"][sl], ds)
                dd_scr[sl, ln] = jnp.broadcast_to(jnp.sum(ds * st_c, axis=0, keepdims=True), (CHUNK, HEAD_DIM))
                hd["ds"] = hd["dec"][c * CHUNK:c * CHUNK + 1, :] * ds + _mm_tn(hd["do"][sl], hd["qi"][sl])

        for hh, ln in enumerate(lanes):
            hd = heads[hh]
            ds_scr[hh] = hd["ds"]
            q, k, lb = hd["q"], hd["k"], hd["lb"]
            dko_e = dko_scr[:, ln] * hd["e_ko"]
            dq = dqh_scr[:, ln] * hd["e_qa"] + dqi_scr[:, ln] * hd["e_b"]
            dk = dkh_scr[:, ln] * hd["e_ka"] + dko_e
            kd3 = (k * dko_e).reshape(nc, CHUNK, HEAD_DIM)
            last = jnp.broadcast_to(jnp.sum(kd3, axis=1, keepdims=True), kd3.shape).reshape(tb, HEAD_DIM)
            db = q * dq - k * dk + jnp.where(pos == CHUNK - 1, hd["dec"] * dd_scr[:, ln] + last, 0.0)
            dlg = _chunk_rev_cumsum(db, pos)
            dfv = dlg / hd["f"] - dk
            s_low = jnp.sum(dfv * (1.0 - hd["sg"]), axis=0, keepdims=True)
            dlow_ref[0:1, ln] += s_low * lb * (1.0 - lb)
            dlow_ref[1:2, ln] += -s_low * lb * hd["lb1"]
            df_ref[:, ln] = (dfv * (1.0 - lb) * hd["sg"] * (1.0 - hd["sg"])).astype(BF16)
            dq_ref[:, ln] = (dq * Q_SCALE * (hd["sq"] * (1.0 + hd["qr"] * (1.0 - hd["sq"])))).astype(BF16)
            di_ref[:, ln] = dv_scr[:, ln].astype(BF16)

    rt = lambda t: nb - 1 - t
    col = lambda p: pl.BlockSpec((tb, wid), lambda h, t: (rt(t), p * HEAD_GROUPS + h))
    hcol = pl.BlockSpec((tb, wid), lambda h, t: (rt(t), h))
    piece = jax.ShapeDtypeStruct((T, HG_WIDTH), BF16)
    tile = pltpu.VMEM((tb, wid), F32)
    return pl.pallas_call(
        _drop_operands(body, 9, len(after)), name="hg_bwd", grid=(HEAD_GROUPS, nb),
        in_specs=[col(0), col(1), col(2), col(3), pl.BlockSpec((2, wid), lambda h, t: (0, h)),
                  pl.BlockSpec((1, HEAD_DIM), lambda h, t: (0, 0)), hcol, hcol,
                  pl.BlockSpec((HEADS_PER_STEP, nc, HEAD_DIM, HEAD_DIM), lambda h, t: (h, rt(t), 0, 0))]
                 + [HBM_SPEC] * len(after),
        out_specs=[hcol, hcol, hcol, hcol, pl.BlockSpec((2, wid), lambda h, t: (0, h)),
                   pl.BlockSpec((1, HEAD_DIM), lambda h, t: (0, 0))],
        out_shape=[piece, piece, piece, piece, jax.ShapeDtypeStruct((2, HG_WIDTH), F32),
                   jax.ShapeDtypeStruct((1, HEAD_DIM), F32)],
        scratch_shapes=[pltpu.VMEM((HEADS_PER_STEP, HEAD_DIM, HEAD_DIM), F32), tile, tile, tile, tile, tile, tile],
        compiler_params=_params(("arbitrary", "arbitrary")),
    )(hg, hg, hg, hg, low, gn, o, dog, st, *after)


def _in_bwd(dparts, w_in, x, dx1, g, after=()):
    T = x.shape[0]
    tm = min(512, T)
    widths = [p.shape[1] for p in dparts]
    offs = [sum(widths[:i]) for i in range(len(widths))]
    n = len(dparts)

    def body(*refs):
        d_refs = refs[:n]
        w_ref, x_ref, dx1_ref, g_ref, dx_ref, dgn_ref = refs[n:]

        @pl.when(pl.program_id(0) == 0)
        def _():
            dgn_ref[...] = jnp.zeros_like(dgn_ref)

        dh = None
        for d_ref, off, wd in zip(d_refs, offs, widths):
            part = _mm(d_ref[...], w_ref[off:off + wd, :])
            dh = part if dh is None else dh + part
        xv = x_ref[...]
        r = lax.rsqrt(jnp.mean(xv * xv, axis=-1, keepdims=True) + EPS)
        xh = xv * r
        dgn_ref[...] += jnp.sum(dh * xh, axis=0, keepdims=True)
        w = dh * g_ref[...]
        dx_ref[...] = dx1_ref[...] + r * (w - xh * jnp.mean(w * xh, axis=-1, keepdims=True))

    row = lambda m: pl.BlockSpec((tm, m), lambda i: (i, 0))
    return pl.pallas_call(
        _drop_operands(body, n + 4, len(after)), name="in_bwd", grid=(T // tm,),
        in_specs=[row(wd) for wd in widths] + [_resident(w_in.shape), row(D_MODEL), row(D_MODEL), _full((1, D_MODEL))]
                 + [HBM_SPEC] * len(after),
        out_specs=[row(D_MODEL), _full((1, D_MODEL))],
        out_shape=[jax.ShapeDtypeStruct((T, D_MODEL), F32), jax.ShapeDtypeStruct((1, D_MODEL), F32)],
        compiler_params=_params(("arbitrary",)),
    )(*dparts, w_in, x, dx1, g, *after)


def _wgrad(name, at, b, tn, transposed=False, tk=2048):
    M, T = at.shape
    N = b.shape[1]
    tk = min(tk, T)
    nk = T // tk

    def body(a_ref, b_ref, o_ref, acc):
        k = pl.program_id(1)
        part = _mm(a_ref[...], b_ref[...])

        @pl.when(k == 0)
        def _():
            acc[...] = part

        @pl.when(k != 0)
        def _():
            acc[...] += part

        @pl.when(k == nk - 1)
        def _():
            o_ref[...] = (acc[...].T if transposed else acc[...]).astype(BF16)

    if transposed:
        out_spec, out_shape = pl.BlockSpec((tn, M), lambda j, k: (j, 0)), (N, M)
    else:
        out_spec, out_shape = pl.BlockSpec((M, tn), lambda j, k: (0, j)), (M, N)
    return pl.pallas_call(
        body, name=name, grid=(N // tn, nk),
        in_specs=[pl.BlockSpec((M, tk), lambda j, k: (0, k)), pl.BlockSpec((tk, tn), lambda j, k: (k, j))],
        out_specs=out_spec, out_shape=jax.ShapeDtypeStruct(out_shape, BF16),
        scratch_shapes=[pltpu.VMEM((M, tn), F32)],
        compiler_params=_params(("parallel", "arbitrary")),
    )(at, b)


def _wgrad_in(ht, dparts, after=()):
    M, T = ht.shape
    tn = 512
    tk = min(2048, T)
    nk = T // tk
    nblk = [p.shape[1] // tn for p in dparts]
    start = [sum(nblk[:i]) for i in range(len(nblk))]
    n = len(dparts)

    def body(a_ref, *refs):
        d_refs, o_ref, acc = refs[:n], refs[n], refs[n + 1]
        j = pl.program_id(0)
        k = pl.program_id(1)

        @pl.when(k == 0)
        def _():
            acc[...] = jnp.zeros_like(acc)

        for d_ref, s, nb in zip(d_refs, start, nblk):
            @pl.when((j >= s) & (j < s + nb))
            def _():
                acc[...] += _mm(a_ref[...], d_ref[...])

        @pl.when(k == nk - 1)
        def _():
            o_ref[...] = acc[...].T.astype(BF16)

    def piece_spec(s, nb):
        def index(j, k):
            inside = (j >= s) & (j < s + nb)
            return jnp.where(inside, k, 0), jnp.clip(j - s, 0, nb - 1)
        return pl.BlockSpec((tk, tn), index)

    return pl.pallas_call(
        _drop_operands(body, 1 + n, len(after)), name="wgrad_in", grid=(sum(nblk), nk),
        in_specs=[pl.BlockSpec((M, tk), lambda j, k: (0, k))] + [piece_spec(s, nb) for s, nb in zip(start, nblk)]
                 + [HBM_SPEC] * len(after),
        out_specs=pl.BlockSpec((tn, M), lambda j, k: (j, 0)),
        out_shape=jax.ShapeDtypeStruct((sum(nblk) * tn, M), BF16),
        scratch_shapes=[pltpu.VMEM((M, tn), F32)],
        compiler_params=_params(("parallel", "arbitrary")),
    )(ht, *dparts, *after)


def _adamw_math(w, g, m, v):
    m = ADAM_B1 * m + (1.0 - ADAM_B1) * g
    v = ADAM_B2 * v + (1.0 - ADAM_B2) * (g * g)
    m_hat = m / (1.0 - ADAM_B1 ** ADAM_STEP)
    v_hat = v / (1.0 - ADAM_B2 ** ADAM_STEP)
    delta = -ADAM_LR * (m_hat / (jnp.sqrt(v_hat) + ADAM_EPS) + ADAM_WD * w)
    return delta, m, v


def _adamw_sum(name, w, parts, m, v):
    R, C = w.shape
    tr = _row_tile(R)

    def body(w_ref, p_ref, m_ref, v_ref, g_out, d_out, m_out, v_out):
        g = p_ref[0].astype(F32)
        for k in range(1, 4):
            g = g + p_ref[k].astype(F32)
        g_out[...] = g
        d_out[...], m_out[...], v_out[...] = _adamw_math(w_ref[...], g, m_ref[...], v_ref[...])

    blk = pl.BlockSpec((tr, C), lambda i: (i, 0))
    out = jax.ShapeDtypeStruct((R, C), F32)
    return pl.pallas_call(
        body, name=name, grid=(R // tr,),
        in_specs=[blk, pl.BlockSpec((4, tr, C), lambda i: (0, i, 0)), blk, blk],
        out_specs=[blk, blk, blk, blk], out_shape=[out, out, out, out],
        compiler_params=_params(("parallel",)),
    )(w, parts, m, v)


def _small_sum(gathered):
    R = gathered.shape[1]

    def body(p_ref, o_ref):
        g = p_ref[0]
        for k in range(1, N_DEV):
            g = g + p_ref[k]
        o_ref[...] = g

    return pl.pallas_call(
        body, name="small_sum", in_specs=[_full(gathered.shape)], out_specs=_full((R, 128)), grid=(1,),
        out_shape=jax.ShapeDtypeStruct((R, 128), F32),
    )(gathered)


def _small_adamw(w, g, m, v):
    def body(w_ref, g_ref, m_ref, v_ref, d_out, m_out, v_out):
        d_out[...], m_out[...], v_out[...] = _adamw_math(w_ref[...], g_ref[...], m_ref[...], v_ref[...])

    out = jax.ShapeDtypeStruct(w.shape, F32)
    spec = _full(w.shape)
    return pl.pallas_call(
        body, name="small_adamw", grid=(1,), in_specs=[spec] * 4, out_specs=[spec] * 3, out_shape=[out, out, out],
    )(w, g, m, v)


def _row_tile(rows):
    for cand in (256, 128):
        if rows % cand == 0:
            return cand
    return rows


def _pair_sum(name, by_owner, got, core, after=()):
    n = len(got)

    def body(core_ref, *refs):
        for a_ref, b_ref, o_ref in zip(refs[:n], refs[n:2 * n], refs[2 * n:]):
            o_ref[...] = (a_ref[...].astype(F32) + b_ref[...].astype(F32)).astype(BF16)

    def blk(g):
        return pl.BlockSpec((None,) + g.shape[1:], lambda k, core_ref: (k, 0, 0))

    def mine(g):
        return pl.BlockSpec((None,) + g.shape[1:], lambda k, core_ref: (2 * k + core_ref[0], 0, 0))

    return pl.pallas_call(
        _drop_operands(body, 1 + 2 * n, len(after)), name=name,
        grid_spec=pltpu.PrefetchScalarGridSpec(
            num_scalar_prefetch=1, grid=(4,),
            in_specs=[mine(g) for g in got] + [blk(g) for g in got] + [HBM_SPEC] * len(after),
            out_specs=[blk(g) for g in got]),
        out_shape=[jax.ShapeDtypeStruct(g.shape, BF16) for g in got],
        compiler_params=_params(("parallel",)),
    )(core, *by_owner, *got, *after)


def _shards_from_cols(name, full):
    R, allc = full.shape
    c = allc // N_DEV
    tr = _row_tile(R)

    def body(f_ref, o_ref):
        for s in range(N_DEV):
            o_ref[s] = f_ref[:, s * c:(s + 1) * c]

    return pl.pallas_call(
        body, name=name, grid=(R // tr,),
        in_specs=[pl.BlockSpec((tr, allc), lambda i: (i, 0))],
        out_specs=pl.BlockSpec((N_DEV, tr, c), lambda i: (0, i, 0)),
        out_shape=jax.ShapeDtypeStruct((N_DEV, R, c), full.dtype),
        compiler_params=_params(("parallel",)),
    )(full)


MESH = pl.DeviceIdType.MESH
HBM_SPEC = pl.BlockSpec(memory_space=pl.ANY)


def _handshake(peers):
    barrier = pltpu.get_barrier_semaphore()
    for peer in peers:
        pl.semaphore_signal(barrier, inc=1, device_id=peer, device_id_type=MESH)
    pl.semaphore_wait(barrier, len(peers))


def _comm_call(body, name, operands, out_shape, scratch, collective_id):
    if collective_id is None:
        return pl.pallas_call(body, name=name, in_specs=[HBM_SPEC] * len(operands), out_specs=[HBM_SPEC] * len(out_shape),
                              out_shape=out_shape, scratch_shapes=scratch)(*operands)
    return pl.kernel(body, out_type=out_shape, mesh=plsc.ScalarSubcoreMesh(axis_name="sequencer", num_cores=1),
                     scratch_types=scratch, name=name,
                     compiler_params=pltpu.CompilerParams(collective_id=collective_id))(*operands)


def _all_gather(name, blocks, collective_id=None, after=()):
    n = len(blocks)
    na = len(after)

    def body(*refs):
        x_refs, out_refs = refs[:n], refs[n + na:2 * n + na]
        send_sems, recv_sems, local_sems = refs[2 * n + na:]
        x, y, c = lax.axis_index("x"), lax.axis_index("y"), lax.axis_index("c")
        me, sibling = (x, y, c), (x, y, 1 - c)
        chips = [(1 - x, y), (x, 1 - y), (1 - x, 1 - y)]
        if collective_id is not None:
            _handshake([sibling] + [(*chip, c) for chip in chips])

        def slot(i, px, py, pc):
            return out_refs[i].at[4 * px + 2 * py + pc]

        def copy(i, k, blk, to, src=None):
            return pltpu.make_async_remote_copy(
                src_ref=slot(i, *blk) if src is None else src, dst_ref=slot(i, *blk),
                send_sem=send_sems.at[7 * i + k], recv_sem=recv_sems.at[7 * i + k], device_id=to, device_id_type=MESH)

        mine = [pltpu.make_async_copy(x_refs[i], slot(i, *me), local_sems.at[i]) for i in range(n)]
        for cp in mine:
            cp.start()
        first = []
        for i in range(n):
            first.append(copy(i, 0, me, sibling, src=x_refs[i]))
            first += [copy(i, 1 + j, me, (*chip, c), src=x_refs[i]) for j, chip in enumerate(chips)]
        for cp in first:
            cp.start()
        passed = []
        for i in range(n):
            for j, chip in enumerate(chips):
                copy(i, 1 + j, (*chip, c), me).wait_recv()
                passed.append(copy(i, 4 + j, (*chip, c), sibling))
                passed[-1].start()
        for i in range(n):
            copy(i, 0, sibling, me).wait_recv()
            for j, chip in enumerate(chips):
                copy(i, 4 + j, (*chip, 1 - c), me).wait_recv()
        for cp in first + passed:
            cp.wait_send()
        for cp in mine:
            cp.wait()

    return _comm_call(
        body, name, list(blocks) + list(after), [jax.ShapeDtypeStruct((N_DEV,) + b.shape, b.dtype) for b in blocks],
        [pltpu.SemaphoreType.DMA((7 * n,)), pltpu.SemaphoreType.DMA((7 * n,)), pltpu.SemaphoreType.DMA((n,))],
        collective_id)


def _sibling_swap(name, by_owner, collective_id=None, after=()):
    n = len(by_owner)
    na = len(after)

    def body(*refs):
        x_refs, out_refs = refs[:n], refs[n + na:2 * n + na]
        send_sems, recv_sems = refs[2 * n + na:]
        x, y, c = lax.axis_index("x"), lax.axis_index("y"), lax.axis_index("c")
        if collective_id is not None:
            _handshake([(x, y, 1 - c)])
        copies = []
        for i in range(n):
            for k in range(4):
                copies.append(pltpu.make_async_remote_copy(
                    src_ref=x_refs[i].at[2 * k + 1 - c], dst_ref=out_refs[i].at[k],
                    send_sem=send_sems.at[4 * i + k], recv_sem=recv_sems.at[4 * i + k],
                    device_id=(x, y, 1 - c), device_id_type=MESH))
        for cp in copies:
            cp.start()
        for cp in copies:
            cp.wait()

    return _comm_call(
        body, name, list(by_owner) + list(after),
        [jax.ShapeDtypeStruct((4,) + b.shape[1:], b.dtype) for b in by_owner],
        [pltpu.SemaphoreType.DMA((4 * n,)), pltpu.SemaphoreType.DMA((4 * n,))], collective_id)


def _chip_exchange(name, sums, collective_id=None, after=()):
    n = len(sums)
    na = len(after)

    def body(*refs):
        x_refs, out_refs = refs[:n], refs[n + na:2 * n + na]
        send_sems, recv_sems, local_sems = refs[2 * n + na:]
        x, y, c = lax.axis_index("x"), lax.axis_index("y"), lax.axis_index("c")
        chips = [(1 - x, y), (x, 1 - y), (1 - x, 1 - y)]
        my_chip = 2 * x + y
        if collective_id is not None:
            _handshake([(cx, cy, c) for cx, cy in chips])
        mine = [pltpu.make_async_copy(x_refs[i].at[my_chip], out_refs[i].at[my_chip], local_sems.at[i])
                for i in range(n)]
        for cp in mine:
            cp.start()
        sends = []
        for i in range(n):
            for j, (cx, cy) in enumerate(chips):
                sends.append(pltpu.make_async_remote_copy(
                    src_ref=x_refs[i].at[2 * cx + cy], dst_ref=out_refs[i].at[my_chip],
                    send_sem=send_sems.at[3 * i + j], recv_sem=recv_sems.at[3 * i + j],
                    device_id=(cx, cy, c), device_id_type=MESH))
        for cp in sends:
            cp.start()
        for i in range(n):
            for j, (cx, cy) in enumerate(chips):
                pltpu.make_async_remote_copy(
                    src_ref=x_refs[i].at[my_chip], dst_ref=out_refs[i].at[2 * cx + cy],
                    send_sem=send_sems.at[3 * i + j], recv_sem=recv_sems.at[3 * i + j],
                    device_id=(cx, cy, c), device_id_type=MESH).wait_recv()
        for cp in sends:
            cp.wait_send()
        for cp in mine:
            cp.wait()

    return _comm_call(
        body, name, list(sums) + list(after), [jax.ShapeDtypeStruct(s.shape, s.dtype) for s in sums],
        [pltpu.SemaphoreType.DMA((3 * n,)), pltpu.SemaphoreType.DMA((3 * n,)), pltpu.SemaphoreType.DMA((n,))],
        collective_id)


def _cast_shards(shards):
    n = len(shards)

    def body(*refs):
        for i in range(n):
            refs[n + i][...] = refs[i][...].astype(BF16)

    vmem = pl.BlockSpec(memory_space=pltpu.VMEM)
    return pl.pallas_call(
        body, name="cast_shards", in_specs=[vmem] * n, out_specs=[vmem] * n,
        out_shape=[jax.ShapeDtypeStruct(s.shape, BF16) for s in shards],
        compiler_params=pltpu.CompilerParams(vmem_limit_bytes=VMEM_LIMIT_V7X),
    )(*shards)


BIG = ("w_in", "w_branch_a", "w_branch_b", "w_out", "w_ffn_gate", "w_ffn_up", "w_ffn_down")


def _local_step(x, target, gains, low, conv_w, wg8, reduce):
    g_mix, g_hg, g_ffn, g_fin = gains
    w_in = wg8["w_in"].reshape(N_IN, D_MODEL)
    wg = wg8["w_ffn_gate"].reshape(D_FF, D_MODEL)
    wu = wg8["w_ffn_up"].reshape(D_FF, D_MODEL)
    wa, wb = wg8["w_branch_a"], wg8["w_branch_b"]
    wo = wg8["w_out"].reshape(D_MODEL, D_MODEL)
    wd = wg8["w_ffn_down"].reshape(D_FF, D_MODEL)

    ht, hg, cv, gt = _fwd_in(x, g_mix, w_in)
    o, og, ogt, st = _hg_fwd(hg, low, g_hg)
    cvo, cvot = _conv_fwd(cv, conv_w)
    x1, mgt = _merge_fwd(og, cvo, gt, x, wa, wb, wo)
    h2t, gate, up, actt, loss, d_gfin, dx2 = _ffn_fwd_loss(x1, g_ffn, wg, wu, wd, target, g_fin)

    dgate, dup, dx1, d_gffn = _ffn_bwd(dx2, x1, gate, up, g_ffn, wg, wu, wd)
    ffn = dict(
        w_ffn_down=_wgrad("wgrad_ffn_down", actt, dx2, 1024, tk=1024).reshape(N_DEV, D_FF // N_DEV, D_MODEL),
        w_ffn_gate=_wgrad("wgrad_ffn_gate", h2t, dgate, D_FF, transposed=True, tk=1024
                          ).reshape(N_DEV, D_FF // N_DEV, D_MODEL),
        w_ffn_up=_wgrad("wgrad_ffn_up", h2t, dup, D_FF, transposed=True, tk=1024
                        ).reshape(N_DEV, D_FF // N_DEV, D_MODEL))
    sums_ffn, got_ffn = reduce.begin(ffn)
    dgt, dya, dyb, dog, dcvo = _merge_bwd(dx1, og, cvo, gt, wa, wb, wo)
    out = dict(
        w_out=_wgrad("wgrad_out", mgt, dx1, 512).reshape(N_DEV, D_MODEL // N_DEV, D_MODEL),
        w_branch_a=_shards_from_cols("split_w_branch_a", _wgrad("wgrad_branch_a", ogt, dya, 512)),
        w_branch_b=_shards_from_cols("split_w_branch_b", _wgrad("wgrad_branch_b", cvot, dyb, 512)))
    sums_out, got_out = reduce.begin(out, after=got_ffn[:1])
    parts_ffn, _ = reduce.finish(ffn, sums_ffn, after=got_out[:1])
    dq, df, di, dg, d_low, d_ghg = _hg_bwd(dog, hg, o, st, low, g_hg, after=list(sums_ffn) + list(sums_out))
    dc, db, dxb, d_conv = _conv_bwd(dcvo, cv, conv_w, after=[dq])
    parts_out, updated_out = reduce.finish(out, sums_out, after=[parts_ffn[0], dc])
    dparts = [dq, df, di, dg, dc, db, dxb, dgt]
    w_in_grad = dict(w_in=_wgrad_in(ht, dparts, after=parts_ffn[:1]).reshape(N_DEV, N_IN // N_DEV, D_MODEL))
    sums_in, _ = reduce.begin(w_in_grad, after=parts_out[:1], sum_after=updated_out)
    parts_in, _ = reduce.finish(w_in_grad, sums_in)
    grad_x, d_gmix = _in_bwd(dparts, w_in, x, dx1, g_mix, after=list(parts_out[:1]) + list(sums_in))
    small = dict(norm_mix_g=d_gmix, norm_ffn_g=d_gffn, norm_final_g=d_gfin, lower_bounds=d_low, hg_norm_g=d_ghg,
                 conv_w=d_conv, loss=loss)
    return grad_x, small, parts_in


_SMALL_LAYOUT = (("norm_mix_g", 0, 8), ("norm_ffn_g", 8, 8), ("norm_final_g", 16, 8), ("lower_bounds", 24, 8),
                 ("hg_norm_g", 32, 1))
_LOSS_ROW = 40
_CONV_ROW = 48


def _pad_rows(a, rows):
    return jnp.pad(a, ((0, rows - a.shape[0]), (0, 0)))


def _pack_small(vals, conv_rows):
    parts = [_pad_rows(vals[name].reshape(rows, 128), 8) for name, _, rows in _SMALL_LAYOUT]
    loss = vals["loss"][:, :128] if "loss" in vals else jnp.zeros((1, 128), F32)
    parts.append(_pad_rows(loss, 8))
    parts.append(_pad_rows(conv_rows, SMALL_ROWS - _CONV_ROW))
    return jnp.concatenate(parts, axis=0)


def _conv_shard_rows(a):
    return jnp.pad(a, ((0, 5), (0, 64)))


def kernel(x, norm_mix_g, w_in, lower_bounds, hg_norm_g, conv_w, w_branch_a, w_branch_b, w_out, norm_ffn_g, w_ffn_gate, w_ffn_up, w_ffn_down, norm_final_g, loss_target, m_norm_mix_g, m_w_in, m_lower_bounds, m_hg_norm_g, m_conv_w, m_w_branch_a, m_w_branch_b, m_w_out, m_norm_ffn_g, m_w_ffn_gate, m_w_ffn_up, m_w_ffn_down, m_norm_final_g, v_norm_mix_g, v_w_in, v_lower_bounds, v_hg_norm_g, v_conv_w, v_w_branch_a, v_w_branch_b, v_w_out, v_norm_ffn_g, v_w_ffn_gate, v_w_ffn_up, v_w_ffn_down, v_norm_final_g):
    cx, cy, cc = lax.axis_index("x"), lax.axis_index("y"), lax.axis_index("c")
    my_dev = 4 * cx + 2 * cy + cc

    def tr(a):
        return a[0].T

    big = dict(w_in=tr(w_in), w_branch_a=w_branch_a[0], w_branch_b=w_branch_b[0], w_out=w_out[0],
               w_ffn_gate=tr(w_ffn_gate), w_ffn_up=tr(w_ffn_up), w_ffn_down=w_ffn_down[0])
    big_m = dict(w_in=tr(m_w_in), w_branch_a=m_w_branch_a[0], w_branch_b=m_w_branch_b[0], w_out=m_w_out[0],
                 w_ffn_gate=tr(m_w_ffn_gate), w_ffn_up=tr(m_w_ffn_up), w_ffn_down=m_w_ffn_down[0])
    big_v = dict(w_in=tr(v_w_in), w_branch_a=v_w_branch_a[0], w_branch_b=v_w_branch_b[0], w_out=v_w_out[0],
                 w_ffn_gate=tr(v_w_ffn_gate), w_ffn_up=tr(v_w_ffn_up), w_ffn_down=v_w_ffn_down[0])
    transposed = ("w_in", "w_ffn_gate", "w_ffn_up")

    shards = dict(zip(BIG, _cast_shards([big[n] for n in BIG])))
    first = _all_gather("gather_w_in", [shards["w_in"], _conv_shard_rows(conv_w[0])])
    ids = iter(range(1, 16))
    mid = _all_gather("gather_mid", [shards[n] for n in BIG[1:4]], collective_id=next(ids), after=first[1:])
    ffn = _all_gather("gather_ffn", [shards[n] for n in BIG[4:]], collective_id=next(ids), after=first[1:])
    wg8 = dict(zip(BIG, [first[0]] + list(mid) + list(ffn)))
    conv_full = first[1][:, :3, :64].transpose(1, 0, 2).reshape(3, CONV_WIDTH)

    core = cc.reshape(1).astype(jnp.int32)
    outs = {}

    class Reduce:
        @staticmethod
        def begin(grads, after=(), sum_after=()):
            names = list(grads)
            by_owner = [grads[n] for n in names]
            got = _sibling_swap("sibling_swap_" + names[0], by_owner, collective_id=next(ids), after=after)
            sums = _pair_sum("pair_sum_" + names[0], by_owner, got, core, after=sum_after)
            return sums, got

        @staticmethod
        def finish(grads, chip_sums, after=()):
            names = list(grads)
            parts = _chip_exchange("chip_exchange_" + names[0], chip_sums, collective_id=next(ids), after=after)
            for n, p in zip(names, parts):
                outs[n] = _adamw_sum("adamw_" + n, big[n], p, big_m[n], big_v[n])
            return parts, [outs[n][1] for n in names]

    gains = (norm_mix_g, hg_norm_g, norm_ffn_g, norm_final_g.reshape(1, D_MODEL))
    grad_x, small, last = _local_step(x[0], loss_target[0], gains, lower_bounds, conv_full, wg8, Reduce)

    small_all = _all_gather("gather_small", [_pack_small(small, small["conv_w"].reshape(12, 128))],
                            collective_id=next(ids), after=last[:1])
    ssum = _small_sum(small_all[0])
    conv_g_full = ssum[_CONV_ROW:_CONV_ROW + 12].reshape(3, CONV_WIDTH)
    conv_g = lax.dynamic_slice_in_dim(conv_g_full, my_dev * 64, 64, axis=1)
    loss = ssum[_LOSS_ROW, 0]
    g_rows = jnp.concatenate([ssum[:_CONV_ROW], _pad_rows(_conv_shard_rows(conv_g), SMALL_ROWS - _CONV_ROW)], axis=0)

    def pack_state(a):
        vals = dict(norm_mix_g=a[0], norm_ffn_g=a[1], norm_final_g=a[2], lower_bounds=a[3], hg_norm_g=a[4])
        return _pack_small(vals, _conv_shard_rows(a[5][0]))

    sw = pack_state((norm_mix_g, norm_ffn_g, norm_final_g, lower_bounds, hg_norm_g, conv_w))
    sm = pack_state((m_norm_mix_g, m_norm_ffn_g, m_norm_final_g, m_lower_bounds, m_hg_norm_g, m_conv_w))
    sv = pack_state((v_norm_mix_g, v_norm_ffn_g, v_norm_final_g, v_lower_bounds, v_hg_norm_g, v_conv_w))
    s_delta, s_m, s_v = _small_adamw(sw, g_rows, sm, sv)

    shapes = dict(norm_mix_g=(1, D_MODEL), norm_ffn_g=(1, D_MODEL), norm_final_g=(D_MODEL,),
                  lower_bounds=(2, HG_WIDTH), hg_norm_g=(1, HEAD_DIM))

    def unpack(buf, name):
        if name == "conv_w":
            return buf[_CONV_ROW:_CONV_ROW + 3, :64].reshape(1, 3, 64)
        for nm, off, rows in _SMALL_LAYOUT:
            if nm == name:
                return buf[off:off + rows].reshape(shapes[name])
        raise KeyError(name)

    order = ["norm_mix_g", "w_in", "lower_bounds", "hg_norm_g", "conv_w", "w_branch_a", "w_branch_b", "w_out",
             "norm_ffn_g", "w_ffn_gate", "w_ffn_up", "w_ffn_down", "norm_final_g"]
    result = [loss, grad_x[None]]
    for k, sbuf in enumerate((g_rows, s_delta, s_m, s_v)):
        for n in order:
            if n in outs:
                result.append((outs[n][k].T if n in transposed else outs[n][k])[None])
            else:
                result.append(unpack(sbuf, n))
    return tuple(result)
```

```python
import jax
import jax.numpy as jnp
from jax import lax
from jax.experimental import pallas as pl
from jax.experimental.pallas import tpu as pltpu
from jax.experimental.pallas import tpu_sc as plsc

F32 = jnp.float32
BF16 = jnp.bfloat16
STASH = jnp.bfloat16

D_MODEL = 1024
HG_WIDTH = 512
HEAD_DIM = 128
N_HEADS = 4
HEADS_PER_STEP = 4
HEAD_GROUPS = N_HEADS // HEADS_PER_STEP
CONV_WIDTH = 512
D_FF = 2816
CHUNK = 32
EPS = 1e-6
Q_SCALE = HEAD_DIM ** -0.5
N_DEV = 8

ADAM_LR = 0.001
ADAM_B1 = 0.9
ADAM_B2 = 0.999
ADAM_EPS = 1e-08
ADAM_WD = 0.01
ADAM_STEP = 10

VMEM_LIMIT_V7X = 56 * 1024 * 1024

SMALL_ROWS = 16


def _params(sem, vmem=VMEM_LIMIT_V7X):
    return pltpu.CompilerParams(dimension_semantics=sem, vmem_limit_bytes=vmem)


def _mm(a, b):
    return jnp.dot(a.astype(BF16), b.astype(BF16), preferred_element_type=F32)


def _mm_nt(a, b):
    return lax.dot_general(a.astype(BF16), b.astype(BF16), (((1,), (1,)), ((), ())), preferred_element_type=F32)


def _mm_tn(a, b):
    return lax.dot_general(a.astype(BF16), b.astype(BF16), (((0,), (0,)), ((), ())), preferred_element_type=F32)


def _sigmoid(x):
    return 0.5 * jnp.tanh(0.5 * x) + 0.5


def _resident(shape):
    nd = len(shape)
    return pl.BlockSpec(shape, lambda *_: (0,) * nd, pipeline_mode=pl.Buffered(1))


def _full(shape):
    nd = len(shape)
    return pl.BlockSpec(shape, lambda *_: (0,) * nd)


def _shard_cols(w_ref):
    return jnp.concatenate([w_ref[s] for s in range(N_DEV)], axis=1)


N_HG = 4 * HG_WIDTH
N_CV = 3 * CONV_WIDTH
N_GT = 2 * D_MODEL
N_IN = N_HG + N_CV + N_GT


def _col(tm, n):
    return pl.BlockSpec((n, tm), lambda i: (0, i))


def _fwd_in(x, g, w_in_t):
    T = x.shape[0]
    tm = min(512, T)

    def body(x_ref, g_ref, w_ref, ht_ref, hg_ref, cv_ref, gt_ref):
        xv = x_ref[...]
        r = lax.rsqrt(jnp.mean(xv * xv, axis=-1, keepdims=True) + EPS)
        hf = xv * r * g_ref[...]
        h = hf.astype(BF16)
        ht_ref[...] = hf.T.astype(BF16)
        hg_ref[...] = _mm_nt(h, w_ref[:N_HG, :])
        cv_ref[...] = _mm_nt(h, w_ref[N_HG:N_HG + N_CV, :]).astype(STASH)
        gt_ref[...] = _mm_nt(h, w_ref[N_HG + N_CV:, :]).astype(STASH)

    row = lambda n: pl.BlockSpec((tm, n), lambda i: (i, 0))
    return pl.pallas_call(
        body, name="fwd_in", grid=(T // tm,),
        in_specs=[row(D_MODEL), _full((1, D_MODEL)), _resident(w_in_t.shape)],
        out_specs=[_col(tm, D_MODEL), row(N_HG), row(N_CV), row(N_GT)],
        out_shape=[jax.ShapeDtypeStruct((D_MODEL, T), BF16), jax.ShapeDtypeStruct((T, N_HG), F32),
                   jax.ShapeDtypeStruct((T, N_CV), STASH), jax.ShapeDtypeStruct((T, N_GT), STASH)],
        compiler_params=_params(("parallel",)),
    )(x, g, w_in_t)


def _chunk_pos(shape):
    return lax.broadcasted_iota(jnp.int32, shape, 0) & (CHUNK - 1)


def _chunk_cumsum(x, pos):
    s = 1
    while s < CHUNK:
        x = x + jnp.where(pos >= s, pltpu.roll(x, s, 0), 0.0)
        s *= 2
    return x


def _chunk_rev_cumsum(x, pos):
    n = x.shape[0]
    s = 1
    while s < CHUNK:
        x = x + jnp.where(pos + s < CHUNK, pltpu.roll(x, n - s, 0), 0.0)
        s *= 2
    return x


def _chunk_bcast(x3, row, tb):
    return jnp.broadcast_to(x3[:, row:row + 1, :], x3.shape).reshape(tb, x3.shape[-1])


def _lower_bound(low_ref):
    l0 = low_ref[0:1, :]
    l1 = low_ref[1:2, :]
    m = jnp.maximum(l0, l1)
    e0 = jnp.exp(l0 - m)
    e1 = jnp.exp(l1 - m)
    return e0 / (e0 + e1), e1 / (e0 + e1)


def _hg_gates(qr, fr, lb, pos, tb):
    sq = _sigmoid(qr)
    q = qr * sq * Q_SCALE
    sg = _sigmoid(fr)
    f = lb + (1.0 - lb) * sg
    k = 1.0 - f
    b = _chunk_cumsum(jnp.log(f), pos)
    b3 = b.reshape(tb // CHUNK, CHUNK, HEAD_DIM)
    anc = _chunk_bcast(b3, CHUNK // 2 - 1, tb)
    blb = _chunk_bcast(b3, CHUNK - 1, tb)
    e_qa = jnp.exp(b - anc)
    e_ka = jnp.exp(anc - b)
    e_b = jnp.exp(b)
    e_ko = jnp.exp(blb - b)
    dec = jnp.exp(blb)
    return sq, q, sg, f, k, e_qa, e_ka, e_b, e_ko, dec


def _intra_mask(sb):
    r = lax.broadcasted_iota(jnp.int32, (sb, sb), 0)
    c = lax.broadcasted_iota(jnp.int32, (sb, sb), 1)
    return ((r // CHUNK) == (c // CHUNK)) & (c <= r)


def _hg_fwd(hg, low, gn):
    T = hg.shape[0]
    tb = min(512, T)
    sb = min(256, tb)
    nb = T // tb
    nc = tb // CHUNK
    wid = HEADS_PER_STEP * HEAD_DIM

    def body(q_ref, f_ref, i_ref, g_ref, low_ref, gn_ref, o_ref, og_ref, ogt_ref, st_ref, s_scr):
        t = pl.program_id(1)

        @pl.when(t == 0)
        def _():
            s_scr[...] = jnp.zeros_like(s_scr)

        pos = _chunk_pos((tb, HEAD_DIM))
        mask = _intra_mask(sb)
        lanes = [slice(hh * HEAD_DIM, (hh + 1) * HEAD_DIM) for hh in range(HEADS_PER_STEP)]
        qi, ko, vb, dec, st = [], [], [], [], []
        for hh, ln in enumerate(lanes):
            lb, _ = _lower_bound(low_ref.at[:, ln])
            _, q, _, _, k, e_qa, e_ka, e_b, e_ko, dec_h = _hg_gates(q_ref[:, ln], f_ref[:, ln], lb, pos, tb)
            qh = (q * e_qa).astype(BF16)
            kh = (k * e_ka).astype(BF16)
            qi.append((q * e_b).astype(BF16))
            ko.append((k * e_ko).astype(BF16))
            vb.append(i_ref[:, ln].astype(BF16))
            dec.append(dec_h)
            st.append(s_scr[hh])
            for s in range(tb // sb):
                sl = slice(s * sb, (s + 1) * sb)
                p = jnp.where(mask, _mm_nt(qh[sl], kh[sl]), 0.0)
                o_ref[sl, ln] = _mm(p, vb[hh][sl])
        for c in range(nc):
            sl = slice(c * CHUNK, (c + 1) * CHUNK)
            for hh, ln in enumerate(lanes):
                st_ref[hh, c] = st[hh]
                o_ref[sl, ln] = o_ref[sl, ln] + _mm_nt(qi[hh][sl], st[hh])
                st[hh] = dec[hh][c * CHUNK:c * CHUNK + 1, :] * st[hh] + _mm_tn(vb[hh][sl], ko[hh][sl])
        for hh, ln in enumerate(lanes):
            s_scr[hh] = st[hh]
            o = o_ref[:, ln]
            r = lax.rsqrt(jnp.mean(o * o, axis=-1, keepdims=True) + EPS)
            gr = g_ref[:, ln]
            og = (o * r * gn_ref[...]) * (gr * _sigmoid(gr))
            og_ref[:, ln] = og.astype(BF16)
            ogt_ref[ln, :] = og.T.astype(BF16)

    col = lambda p: pl.BlockSpec((tb, wid), lambda h, t: (t, p * HEAD_GROUPS + h))
    hcol = pl.BlockSpec((tb, wid), lambda h, t: (t, h))
    return pl.pallas_call(
        body, name="hg_fwd", grid=(HEAD_GROUPS, nb),
        in_specs=[col(0), col(1), col(2), col(3), pl.BlockSpec((2, wid), lambda h, t: (0, h)),
                  pl.BlockSpec((1, HEAD_DIM), lambda h, t: (0, 0))],
        out_specs=[hcol, hcol, pl.BlockSpec((wid, tb), lambda h, t: (h, t)),
                   pl.BlockSpec((HEADS_PER_STEP, nc, HEAD_DIM, HEAD_DIM), lambda h, t: (h, t, 0, 0))],
        out_shape=[jax.ShapeDtypeStruct((T, HG_WIDTH), F32), jax.ShapeDtypeStruct((T, HG_WIDTH), BF16),
                   jax.ShapeDtypeStruct((HG_WIDTH, T), BF16),
                   jax.ShapeDtypeStruct((N_HEADS, T // CHUNK, HEAD_DIM, HEAD_DIM), F32)],
        scratch_shapes=[pltpu.VMEM((HEADS_PER_STEP, HEAD_DIM, HEAD_DIM), F32)],
        compiler_params=_params(("parallel", "arbitrary")),
    )(hg, hg, hg, hg, low, gn)


def _conv_fwd(cv, conv_w):
    T = cv.shape[0]
    nj = CONV_WIDTH // 128

    def body(c_ref, b_ref, x_ref, w_ref, o_ref, ot_ref):
        row = lax.broadcasted_iota(jnp.int32, (T, 128), 0)
        u = c_ref[...].astype(F32) * x_ref[...].astype(F32)
        u1 = jnp.where(row >= 1, pltpu.roll(u, 1, 0), 0.0)
        u2 = jnp.where(row >= 2, pltpu.roll(u, 2, 0), 0.0)
        y = w_ref[0:1, :] * u2 + w_ref[1:2, :] * u1 + w_ref[2:3, :] * u
        out = b_ref[...].astype(F32) * y
        o_ref[...] = out.astype(BF16)
        ot_ref[...] = out.T.astype(BF16)

    col = lambda p: pl.BlockSpec((T, 128), lambda j: (0, p * nj + j))
    return pl.pallas_call(
        body, name="conv_fwd", grid=(nj,),
        in_specs=[col(0), col(1), col(2), pl.BlockSpec((3, 128), lambda j: (0, j))],
        out_specs=[pl.BlockSpec((T, 128), lambda j: (0, j)), pl.BlockSpec((128, T), lambda j: (j, 0))],
        out_shape=[jax.ShapeDtypeStruct((T, CONV_WIDTH), BF16), jax.ShapeDtypeStruct((CONV_WIDTH, T), BF16)],
        compiler_params=_params(("parallel",)),
    )(cv, cv, cv, conv_w)


def _merge_fwd(og, cvo, gt, x, wa, wb, wo):
    T = x.shape[0]
    tm = min(512, T)

    def body(og_ref, cvo_ref, gt_ref, x_ref, wa_ref, wb_ref, wo_ref, x1_ref, mgt_ref):
        ya = jnp.dot(og_ref[...], _shard_cols(wa_ref), preferred_element_type=F32)
        yb = jnp.dot(cvo_ref[...], _shard_cols(wb_ref), preferred_element_type=F32)
        m = (_sigmoid(gt_ref[:, :D_MODEL].astype(F32)) * ya
             + _sigmoid(gt_ref[:, D_MODEL:].astype(F32)) * yb)
        mgt_ref[...] = m.T.astype(BF16)
        x1_ref[...] = x_ref[...] + jnp.dot(m.astype(BF16), wo_ref[...], preferred_element_type=F32)

    row = lambda n: pl.BlockSpec((tm, n), lambda i: (i, 0))
    return pl.pallas_call(
        body, name="merge_fwd", grid=(T // tm,),
        in_specs=[row(HG_WIDTH), row(CONV_WIDTH), row(2 * D_MODEL), row(D_MODEL),
                  _resident(wa.shape), _resident(wb.shape), _resident(wo.shape)],
        out_specs=[row(D_MODEL), _col(tm, D_MODEL)],
        out_shape=[jax.ShapeDtypeStruct((T, D_MODEL), F32), jax.ShapeDtypeStruct((D_MODEL, T), BF16)],
        compiler_params=_params(("parallel",)),
    )(og, cvo, gt, x, wa, wb, wo)


def _ffn_fwd_loss(x1, g, wg, wu, wd, target, g_fin):
    T = x1.shape[0]
    tm = min(256, T)

    def body(x_ref, g_ref, wg_ref, wu_ref, wd_ref, t_ref, gf_ref,
             ht_ref, gate_ref, up_ref, actt_ref, loss_ref, dgf_ref, dx2_ref):
        @pl.when(pl.program_id(0) == 0)
        def _():
            loss_ref[...] = jnp.zeros_like(loss_ref)
            dgf_ref[...] = jnp.zeros_like(dgf_ref)

        xv = x_ref[...]
        r = lax.rsqrt(jnp.mean(xv * xv, axis=-1, keepdims=True) + EPS)
        hf = xv * r * g_ref[...]
        h = hf.astype(BF16)
        ht_ref[...] = hf.T.astype(BF16)
        gate = _mm_nt(h, wg_ref[...])
        up = _mm_nt(h, wu_ref[...])
        gate_ref[...] = gate.astype(STASH)
        up_ref[...] = up.astype(STASH)
        act = gate * _sigmoid(gate) * up
        actt_ref[...] = act.T.astype(BF16)
        x2 = xv + jnp.dot(act.astype(BF16), wd_ref[...], preferred_element_type=F32)

        gv = gf_ref[...]
        r2 = lax.rsqrt(jnp.mean(x2 * x2, axis=-1, keepdims=True) + EPS)
        xh = x2 * r2
        err = xh * gv - t_ref[...]
        loss_ref[...] += 0.5 * jnp.sum(jnp.mean(err * err, axis=-1, keepdims=True), axis=0, keepdims=True)
        dy = err * (1.0 / D_MODEL)
        dgf_ref[...] += jnp.sum(dy * xh, axis=0, keepdims=True)
        w = dy * gv
        dx2_ref[...] = r2 * (w - xh * jnp.mean(w * xh, axis=-1, keepdims=True))

    row = lambda n: pl.BlockSpec((tm, n), lambda i: (i, 0))
    return pl.pallas_call(
        body, name="ffn_fwd_loss", grid=(T // tm,),
        in_specs=[row(D_MODEL), _full((1, D_MODEL)), _resident(wg.shape), _resident(wu.shape), _resident(wd.shape),
                  row(D_MODEL), _full((1, D_MODEL))],
        out_specs=[_col(tm, D_MODEL), row(D_FF), row(D_FF), _col(tm, D_FF), _full((1, 128)), _full((1, D_MODEL)),
                   row(D_MODEL)],
        out_shape=[jax.ShapeDtypeStruct((D_MODEL, T), BF16), jax.ShapeDtypeStruct((T, D_FF), STASH),
                   jax.ShapeDtypeStruct((T, D_FF), STASH), jax.ShapeDtypeStruct((D_FF, T), BF16),
                   jax.ShapeDtypeStruct((1, 128), F32), jax.ShapeDtypeStruct((1, D_MODEL), F32),
                   jax.ShapeDtypeStruct((T, D_MODEL), F32)],
        compiler_params=_params(("arbitrary",)),
    )(x1, g, wg, wu, wd, target, g_fin)


def _ffn_bwd(dx2, x1, gate, up, g, wg, wu, wd):
    T = x1.shape[0]
    tm = min(256, T)

    def body(dx2_ref, x_ref, gate_ref, up_ref, g_ref, wg_ref, wu_ref, wd_ref, dgate_ref, dup_ref, dx1_ref, dgn_ref):
        @pl.when(pl.program_id(0) == 0)
        def _():
            dgn_ref[...] = jnp.zeros_like(dgn_ref)

        dx2 = dx2_ref[...]
        dact = _mm_nt(dx2, wd_ref[...])
        gate = gate_ref[...].astype(F32)
        s = _sigmoid(gate)
        dgate = (dact * up_ref[...].astype(F32) * (s * (1.0 + gate * (1.0 - s)))).astype(BF16)
        dup = (dact * (gate * s)).astype(BF16)
        dgate_ref[...] = dgate
        dup_ref[...] = dup
        dh = _mm(dgate, wg_ref[...]) + _mm(dup, wu_ref[...])
        xv = x_ref[...]
        r = lax.rsqrt(jnp.mean(xv * xv, axis=-1, keepdims=True) + EPS)
        xh = xv * r
        dgn_ref[...] += jnp.sum(dh * xh, axis=0, keepdims=True)
        w = dh * g_ref[...]
        dx1_ref[...] = dx2 + r * (w - xh * jnp.mean(w * xh, axis=-1, keepdims=True))

    row = lambda n: pl.BlockSpec((tm, n), lambda i: (i, 0))
    return pl.pallas_call(
        body, name="ffn_bwd", grid=(T // tm,),
        in_specs=[row(D_MODEL), row(D_MODEL), row(D_FF), row(D_FF), _full((1, D_MODEL)),
                  _resident(wg.shape), _resident(wu.shape), _resident(wd.shape)],
        out_specs=[row(D_FF), row(D_FF), row(D_MODEL), _full((1, D_MODEL))],
        out_shape=[jax.ShapeDtypeStruct((T, D_FF), BF16), jax.ShapeDtypeStruct((T, D_FF), BF16),
                   jax.ShapeDtypeStruct((T, D_MODEL), F32), jax.ShapeDtypeStruct((1, D_MODEL), F32)],
        compiler_params=_params(("arbitrary",)),
    )(dx2, x1, gate, up, g, wg, wu, wd)


def _merge_bwd(dx1, og, cvo, gt, wa, wb, wo):
    T = dx1.shape[0]
    tm = min(512, T)

    def body(dx_ref, og_ref, cvo_ref, gt_ref, wa_ref, wb_ref, wo_ref, dgt_ref, dya_ref, dyb_ref, dog_ref, dcvo_ref):
        dm = _mm_nt(dx_ref[...], wo_ref[...])
        wa = _shard_cols(wa_ref)
        wb = _shard_cols(wb_ref)
        ya = jnp.dot(og_ref[...], wa, preferred_element_type=F32)
        yb = jnp.dot(cvo_ref[...], wb, preferred_element_type=F32)
        sa = _sigmoid(gt_ref[:, :D_MODEL].astype(F32))
        sb = _sigmoid(gt_ref[:, D_MODEL:].astype(F32))
        dgt_ref[:, :D_MODEL] = (dm * ya * (sa * (1.0 - sa))).astype(BF16)
        dgt_ref[:, D_MODEL:] = (dm * yb * (sb * (1.0 - sb))).astype(BF16)
        dya = (dm * sa).astype(BF16)
        dyb = (dm * sb).astype(BF16)
        dya_ref[...] = dya
        dyb_ref[...] = dyb
        dog_ref[...] = _mm_nt(dya, wa)
        dcvo_ref[...] = _mm_nt(dyb, wb)

    row = lambda n: pl.BlockSpec((tm, n), lambda i: (i, 0))
    return pl.pallas_call(
        body, name="merge_bwd", grid=(T // tm,),
        in_specs=[row(D_MODEL), row(HG_WIDTH), row(CONV_WIDTH), row(2 * D_MODEL),
                  _resident(wa.shape), _resident(wb.shape), _resident(wo.shape)],
        out_specs=[row(2 * D_MODEL), row(D_MODEL), row(D_MODEL), row(HG_WIDTH), row(CONV_WIDTH)],
        out_shape=[jax.ShapeDtypeStruct((T, 2 * D_MODEL), BF16), jax.ShapeDtypeStruct((T, D_MODEL), BF16),
                   jax.ShapeDtypeStruct((T, D_MODEL), BF16), jax.ShapeDtypeStruct((T, HG_WIDTH), F32),
                   jax.ShapeDtypeStruct((T, CONV_WIDTH), F32)],
        compiler_params=_params(("parallel",)),
    )(dx1, og, cvo, gt, wa, wb, wo)


def _conv_bwd(dcvo, cv, conv_w, after=()):
    T = cv.shape[0]
    nj = CONV_WIDTH // 128

    def body(do_ref, c_ref, b_ref, x_ref, w_ref, dc_ref, db_ref, dx_ref, dw_ref):
        row = lax.broadcasted_iota(jnp.int32, (T, 128), 0)
        c = c_ref[...].astype(F32)
        xb = x_ref[...].astype(F32)
        do = do_ref[...]
        u = c * xb
        u1 = jnp.where(row >= 1, pltpu.roll(u, 1, 0), 0.0)
        u2 = jnp.where(row >= 2, pltpu.roll(u, 2, 0), 0.0)
        w0, w1, w2 = w_ref[0:1, :], w_ref[1:2, :], w_ref[2:3, :]
        y = w0 * u2 + w1 * u1 + w2 * u
        db_ref[...] = (do * y).astype(BF16)
        dy = do * b_ref[...].astype(F32)
        dw_ref[0:1, :] = jnp.sum(dy * u2, axis=0, keepdims=True)
        dw_ref[1:2, :] = jnp.sum(dy * u1, axis=0, keepdims=True)
        dw_ref[2:3, :] = jnp.sum(dy * u, axis=0, keepdims=True)
        dy1 = jnp.where(row < T - 1, pltpu.roll(dy, T - 1, 0), 0.0)
        dy2 = jnp.where(row < T - 2, pltpu.roll(dy, T - 2, 0), 0.0)
        du = w2 * dy + w1 * dy1 + w0 * dy2
        dc_ref[...] = (du * xb).astype(BF16)
        dx_ref[...] = (du * c).astype(BF16)

    col = lambda p: pl.BlockSpec((T, 128), lambda j: (0, p * nj + j))
    one = pl.BlockSpec((T, 128), lambda j: (0, j))
    wspec = pl.BlockSpec((3, 128), lambda j: (0, j))
    out = jax.ShapeDtypeStruct((T, CONV_WIDTH), BF16)
    return pl.pallas_call(
        _drop_operands(body, 5, len(after)), name="conv_bwd", grid=(nj,),
        in_specs=[one, col(0), col(1), col(2), wspec] + [HBM_SPEC] * len(after),
        out_specs=[one, one, one, wspec],
        out_shape=[out, out, out, jax.ShapeDtypeStruct((3, CONV_WIDTH), F32)],
        compiler_params=_params(("parallel",)),
    )(dcvo, cv, cv, cv, conv_w, *after)


def _drop_operands(body, first, count):
    def wrapped(*refs):
        return body(*refs[:first], *refs[first + count:])
    return wrapped


def _hg_bwd(dog, hg, o, st, low, gn, after=()):
    T = hg.shape[0]
    tb = min(512, T)
    sb = min(256, tb)
    nb = T // tb
    nc = tb // CHUNK
    wid = HEADS_PER_STEP * HEAD_DIM

    def body(q_ref, f_ref, i_ref, g_ref, low_ref, gn_ref, o_ref, dog_ref, st_ref,
             dq_ref, df_ref, di_ref, dg_ref, dlow_ref, dgn_ref,
             ds_scr, dqi_scr, dko_scr, dv_scr, dd_scr, dqh_scr, dkh_scr):
        h = pl.program_id(0)
        t = pl.program_id(1)

        @pl.when(t == 0)
        def _():
            ds_scr[...] = jnp.zeros_like(ds_scr)
            dlow_ref[...] = jnp.zeros_like(dlow_ref)

        @pl.when((t == 0) & (h == 0))
        def _():
            dgn_ref[...] = jnp.zeros_like(dgn_ref)

        pos = _chunk_pos((tb, HEAD_DIM))
        mask = _intra_mask(sb)
        gnv = gn_ref[...]
        lanes = [slice(hh * HEAD_DIM, (hh + 1) * HEAD_DIM) for hh in range(HEADS_PER_STEP)]
        heads = []
        for hh, ln in enumerate(lanes):
            lb, lb1 = _lower_bound(low_ref.at[:, ln])
            qr = q_ref[:, ln]
            sq, q, sg, f, k, e_qa, e_ka, e_b, e_ko, dec = _hg_gates(qr, f_ref[:, ln], lb, pos, tb)

            gr = g_ref[:, ln]
            o = o_ref[:, ln]
            dog_v = dog_ref[:, ln]
            sgr = _sigmoid(gr)
            r = lax.rsqrt(jnp.mean(o * o, axis=-1, keepdims=True) + EPS)
            oh = o * r
            dg_ref[:, ln] = (dog_v * (oh * gnv) * (sgr * (1.0 + gr * (1.0 - sgr)))).astype(BF16)
            don = dog_v * (gr * sgr)
            dgn_ref[...] += jnp.sum(don * oh, axis=0, keepdims=True)
            w = don * gnv
            do = (r * (w - oh * jnp.mean(w * oh, axis=-1, keepdims=True))).astype(BF16)

            qh = (q * e_qa).astype(BF16)
            kh = (k * e_ka).astype(BF16)
            qi = (q * e_b).astype(BF16)
            ko = (k * e_ko).astype(BF16)
            vb = i_ref[:, ln].astype(BF16)

            for s in range(tb // sb):
                sl = slice(s * sb, (s + 1) * sb)
                p = jnp.where(mask, _mm_nt(qh[sl], kh[sl]), 0.0).astype(BF16)
                dp = jnp.where(mask, _mm_nt(do[sl], vb[sl]), 0.0).astype(BF16)
                dv_scr[sl, ln] = _mm_tn(p, do[sl])
                dqh_scr[sl, ln] = _mm(dp, kh[sl])
                dkh_scr[sl, ln] = _mm_tn(dp, qh[sl])
            heads.append(dict(lb=lb, lb1=lb1, qr=qr, sq=sq, q=q, sg=sg, f=f, k=k, e_qa=e_qa, e_ka=e_ka, e_b=e_b,
                              e_ko=e_ko, dec=dec, do=do, qi=qi, ko=ko, vb=vb, ds=ds_scr[hh]))

        for c in reversed(range(nc)):
            sl = slice(c * CHUNK, (c + 1) * CHUNK)
            for hh, ln in enumerate(lanes):
                hd = heads[hh]
                ds = hd["ds"]
                st_c = st_ref[hh, c]
                dqi_scr[sl, ln] = _mm(hd["do"][sl], st_c)
                dko_scr[sl, ln] = _mm(hd["vb"][sl], ds)
                dv_scr[sl, ln] = dv_scr[sl, ln] + _mm_nt(hd["ko"][sl], ds)
                dd_scr[sl, ln] = jnp.broadcast_to(jnp.sum(ds * st_c, axis=0, keepdims=True), (CHUNK, HEAD_DIM))
                hd["ds"] = hd["dec"][c * CHUNK:c * CHUNK + 1, :] * ds + _mm_tn(hd["do"][sl], hd["qi"][sl])

        for hh, ln in enumerate(lanes):
            hd = heads[hh]
            ds_scr[hh] = hd["ds"]
            q, k, lb = hd["q"], hd["k"], hd["lb"]
            dko_e = dko_scr[:, ln] * hd["e_ko"]
            dq = dqh_scr[:, ln] * hd["e_qa"] + dqi_scr[:, ln] * hd["e_b"]
            dk = dkh_scr[:, ln] * hd["e_ka"] + dko_e
            kd3 = (k * dko_e).reshape(nc, CHUNK, HEAD_DIM)
            last = jnp.broadcast_to(jnp.sum(kd3, axis=1, keepdims=True), kd3.shape).reshape(tb, HEAD_DIM)
            db = q * dq - k * dk + jnp.where(pos == CHUNK - 1, hd["dec"] * dd_scr[:, ln] + last, 0.0)
            dlg = _chunk_rev_cumsum(db, pos)
            dfv = dlg / hd["f"] - dk
            s_low = jnp.sum(dfv * (1.0 - hd["sg"]), axis=0, keepdims=True)
            dlow_ref[0:1, ln] += s_low * lb * (1.0 - lb)
            dlow_ref[1:2, ln] += -s_low * lb * hd["lb1"]
            df_ref[:, ln] = (dfv * (1.0 - lb) * hd["sg"] * (1.0 - hd["sg"])).astype(BF16)
            dq_ref[:, ln] = (dq * Q_SCALE * (hd["sq"] * (1.0 + hd["qr"] * (1.0 - hd["sq"])))).astype(BF16)
            di_ref[:, ln] = dv_scr[:, ln].astype(BF16)

    rt = lambda t: nb - 1 - t
    col = lambda p: pl.BlockSpec((tb, wid), lambda h, t: (rt(t), p * HEAD_GROUPS + h))
    hcol = pl.BlockSpec((tb, wid), lambda h, t: (rt(t), h))
    piece = jax.ShapeDtypeStruct((T, HG_WIDTH), BF16)
    tile = pltpu.VMEM((tb, wid), F32)
    return pl.pallas_call(
        _drop_operands(body, 9, len(after)), name="hg_bwd", grid=(HEAD_GROUPS, nb),
        in_specs=[col(0), col(1), col(2), col(3), pl.BlockSpec((2, wid), lambda h, t: (0, h)),
                  pl.BlockSpec((1, HEAD_DIM), lambda h, t: (0, 0)), hcol, hcol,
                  pl.BlockSpec((HEADS_PER_STEP, nc, HEAD_DIM, HEAD_DIM), lambda h, t: (h, rt(t), 0, 0))]
                 + [HBM_SPEC] * len(after),
        out_specs=[hcol, hcol, hcol, hcol, pl.BlockSpec((2, wid), lambda h, t: (0, h)),
                   pl.BlockSpec((1, HEAD_DIM), lambda h, t: (0, 0))],
        out_shape=[piece, piece, piece, piece, jax.ShapeDtypeStruct((2, HG_WIDTH), F32),
                   jax.ShapeDtypeStruct((1, HEAD_DIM), F32)],
        scratch_shapes=[pltpu.VMEM((HEADS_PER_STEP, HEAD_DIM, HEAD_DIM), F32), tile, tile, tile, tile, tile, tile],
        compiler_params=_params(("arbitrary", "arbitrary")),
    )(hg, hg, hg, hg, low, gn, o, dog, st, *after)


def _in_bwd(dparts, w_in, x, dx1, g, after=()):
    T = x.shape[0]
    tm = min(512, T)
    widths = [p.shape[1] for p in dparts]
    offs = [sum(widths[:i]) for i in range(len(widths))]
    n = len(dparts)

    def body(*refs):
        d_refs = refs[:n]
        w_ref, x_ref, dx1_ref, g_ref, dx_ref, dgn_ref = refs[n:]

        @pl.when(pl.program_id(0) == 0)
        def _():
            dgn_ref[...] = jnp.zeros_like(dgn_ref)

        dh = None
        for d_ref, off, wd in zip(d_refs, offs, widths):
            part = _mm(d_ref[...], w_ref[off:off + wd, :])
            dh = part if dh is None else dh + part
        xv = x_ref[...]
        r = lax.rsqrt(jnp.mean(xv * xv, axis=-1, keepdims=True) + EPS)
        xh = xv * r
        dgn_ref[...] += jnp.sum(dh * xh, axis=0, keepdims=True)
        w = dh * g_ref[...]
        dx_ref[...] = dx1_ref[...] + r * (w - xh * jnp.mean(w * xh, axis=-1, keepdims=True))

    row = lambda m: pl.BlockSpec((tm, m), lambda i: (i, 0))
    return pl.pallas_call(
        _drop_operands(body, n + 4, len(after)), name="in_bwd", grid=(T // tm,),
        in_specs=[row(wd) for wd in widths] + [_resident(w_in.shape), row(D_MODEL), row(D_MODEL), _full((1, D_MODEL))]
                 + [HBM_SPEC] * len(after),
        out_specs=[row(D_MODEL), _full((1, D_MODEL))],
        out_shape=[jax.ShapeDtypeStruct((T, D_MODEL), F32), jax.ShapeDtypeStruct((1, D_MODEL), F32)],
        compiler_params=_params(("arbitrary",)),
    )(*dparts, w_in, x, dx1, g, *after)


def _wgrad(name, at, b, tn, transposed=False, tk=2048):
    M, T = at.shape
    N = b.shape[1]
    tk = min(tk, T)
    nk = T // tk

    def body(a_ref, b_ref, o_ref, acc):
        k = pl.program_id(1)
        part = _mm(a_ref[...], b_ref[...])

        @pl.when(k == 0)
        def _():
            acc[...] = part

        @pl.when(k != 0)
        def _():
            acc[...] += part

        @pl.when(k == nk - 1)
        def _():
            o_ref[...] = (acc[...].T if transposed else acc[...]).astype(BF16)

    if transposed:
        out_spec, out_shape = pl.BlockSpec((tn, M), lambda j, k: (j, 0)), (N, M)
    else:
        out_spec, out_shape = pl.BlockSpec((M, tn), lambda j, k: (0, j)), (M, N)
    return pl.pallas_call(
        body, name=name, grid=(N // tn, nk),
        in_specs=[pl.BlockSpec((M, tk), lambda j, k: (0, k)), pl.BlockSpec((tk, tn), lambda j, k: (k, j))],
        out_specs=out_spec, out_shape=jax.ShapeDtypeStruct(out_shape, BF16),
        scratch_shapes=[pltpu.VMEM((M, tn), F32)],
        compiler_params=_params(("parallel", "arbitrary")),
    )(at, b)


def _wgrad_in(ht, dparts, after=()):
    M, T = ht.shape
    tn = 512
    tk = min(2048, T)
    nk = T // tk
    nblk = [p.shape[1] // tn for p in dparts]
    start = [sum(nblk[:i]) for i in range(len(nblk))]
    n = len(dparts)

    def body(a_ref, *refs):
        d_refs, o_ref, acc = refs[:n], refs[n], refs[n + 1]
        j = pl.program_id(0)
        k = pl.program_id(1)

        @pl.when(k == 0)
        def _():
            acc[...] = jnp.zeros_like(acc)

        for d_ref, s, nb in zip(d_refs, start, nblk):
            @pl.when((j >= s) & (j < s + nb))
            def _():
                acc[...] += _mm(a_ref[...], d_ref[...])

        @pl.when(k == nk - 1)
        def _():
            o_ref[...] = acc[...].T.astype(BF16)

    def piece_spec(s, nb):
        def index(j, k):
            inside = (j >= s) & (j < s + nb)
            return jnp.where(inside, k, 0), jnp.clip(j - s, 0, nb - 1)
        return pl.BlockSpec((tk, tn), index)

    return pl.pallas_call(
        _drop_operands(body, 1 + n, len(after)), name="wgrad_in", grid=(sum(nblk), nk),
        in_specs=[pl.BlockSpec((M, tk), lambda j, k: (0, k))] + [piece_spec(s, nb) for s, nb in zip(start, nblk)]
                 + [HBM_SPEC] * len(after),
        out_specs=pl.BlockSpec((tn, M), lambda j, k: (j, 0)),
        out_shape=jax.ShapeDtypeStruct((sum(nblk) * tn, M), BF16),
        scratch_shapes=[pltpu.VMEM((M, tn), F32)],
        compiler_params=_params(("parallel", "arbitrary")),
    )(ht, *dparts, *after)


def _adamw_math(w, g, m, v):
    m = ADAM_B1 * m + (1.0 - ADAM_B1) * g
    v = ADAM_B2 * v + (1.0 - ADAM_B2) * (g * g)
    m_hat = m / (1.0 - ADAM_B1 ** ADAM_STEP)
    v_hat = v / (1.0 - ADAM_B2 ** ADAM_STEP)
    delta = -ADAM_LR * (m_hat / (jnp.sqrt(v_hat) + ADAM_EPS) + ADAM_WD * w)
    return delta, m, v


def _adamw_sum(name, w, parts, m, v):
    R, C = w.shape
    tr = _row_tile(R)

    def body(w_ref, p_ref, m_ref, v_ref, g_out, d_out, m_out, v_out):
        g = p_ref[0].astype(F32)
        for k in range(1, 4):
            g = g + p_ref[k].astype(F32)
        g_out[...] = g
        d_out[...], m_out[...], v_out[...] = _adamw_math(w_ref[...], g, m_ref[...], v_ref[...])

    blk = pl.BlockSpec((tr, C), lambda i: (i, 0))
    out = jax.ShapeDtypeStruct((R, C), F32)
    return pl.pallas_call(
        body, name=name, grid=(R // tr,),
        in_specs=[blk, pl.BlockSpec((4, tr, C), lambda i: (0, i, 0)), blk, blk],
        out_specs=[blk, blk, blk, blk], out_shape=[out, out, out, out],
        compiler_params=_params(("parallel",)),
    )(w, parts, m, v)


_SMALL_SLOTS = (("norm_mix_g", 0, 1, 1024), ("norm_ffn_g", 1, 1, 1024), ("norm_final_g", 2, 1, 1024),
                ("lower_bounds", 3, 2, 512), ("hg_norm_g", 5, 1, 128), ("loss", 6, 1, 128), ("conv_w", 8, 3, 512))
_SMALL_PARAMS = tuple(s for s in _SMALL_SLOTS if s[0] != "loss")
CONV_SHARD = CONV_WIDTH // N_DEV


def _small_pack(small):
    def body(*refs):
        out = refs[-1]
        out[...] = jnp.zeros_like(out)
        for ref, (_, row, rows, lanes) in zip(refs[:-1], _SMALL_SLOTS):
            out[row:row + rows, 0:lanes] = ref[...]

    vmem = pl.BlockSpec(memory_space=pltpu.VMEM)
    return pl.pallas_call(
        body, name="small_pack", in_specs=[vmem] * len(_SMALL_SLOTS), out_specs=vmem,
        out_shape=jax.ShapeDtypeStruct((SMALL_ROWS, 1024), F32),
    )(*[small[name] for name, _, _, _ in _SMALL_SLOTS])


def _small_update(gathered, dev, w, m, v):
    n = len(_SMALL_PARAMS)

    def body(dev_ref, g_ref, *refs):
        w_refs, m_refs, v_refs = refs[:n], refs[n:2 * n], refs[2 * n:3 * n]
        loss_ref, out_refs, sum_scr = refs[3 * n], refs[3 * n + 1:-1], refs[-1]
        total = g_ref[0]
        for k in range(1, N_DEV):
            total = total + g_ref[k]
        sum_scr[...] = total
        loss_ref[...] = sum_scr[6:7, 0:128]
        for p, (name, row, rows, lanes) in enumerate(_SMALL_PARAMS):
            if name == "conv_w":
                g = sum_scr[row:row + rows, 0:CONV_SHARD]
                for s in range(1, N_DEV):
                    g = jnp.where(dev_ref[0] == s, sum_scr[row:row + rows, s * CONV_SHARD:(s + 1) * CONV_SHARD], g)
            else:
                g = sum_scr[row:row + rows, 0:lanes]
            delta, m_new, v_new = _adamw_math(w_refs[p][...], g, m_refs[p][...], v_refs[p][...])
            out_refs[4 * p][...] = g
            out_refs[4 * p + 1][...] = delta
            out_refs[4 * p + 2][...] = m_new
            out_refs[4 * p + 3][...] = v_new

    vmem = pl.BlockSpec(memory_space=pltpu.VMEM)
    outs = [jax.ShapeDtypeStruct((1, 128), F32)]
    for a in w:
        outs += [jax.ShapeDtypeStruct(a.shape, F32)] * 4
    return pl.pallas_call(
        body, name="small_update",
        in_specs=[pl.BlockSpec(memory_space=pltpu.SMEM)] + [vmem] * (1 + 3 * n), out_specs=[vmem] * len(outs),
        out_shape=outs, scratch_shapes=[pltpu.VMEM((SMALL_ROWS, 1024), F32)],
    )(dev, gathered, *w, *m, *v)


def _row_tile(rows):
    for parts in (4, 2):
        if rows % (16 * parts) == 0:
            return rows // parts
    return rows


def _pair_sum(name, by_owner, got, core, after=()):
    n = len(got)

    def body(core_ref, *refs):
        for a_ref, b_ref, o_ref in zip(refs[:n], refs[n:2 * n], refs[2 * n:]):
            o_ref[...] = (a_ref[...].astype(F32) + b_ref[...].astype(F32)).astype(BF16)

    def blk(g):
        return pl.BlockSpec((None,) + g.shape[1:], lambda k, core_ref: (k, 0, 0))

    def mine(g):
        return pl.BlockSpec((None,) + g.shape[1:], lambda k, core_ref: (2 * k + core_ref[0], 0, 0))

    return pl.pallas_call(
        _drop_operands(body, 1 + 2 * n, len(after)), name=name,
        grid_spec=pltpu.PrefetchScalarGridSpec(
            num_scalar_prefetch=1, grid=(4,),
            in_specs=[mine(g) for g in got] + [blk(g) for g in got] + [HBM_SPEC] * len(after),
            out_specs=[blk(g) for g in got]),
        out_shape=[jax.ShapeDtypeStruct(g.shape, BF16) for g in got],
        compiler_params=_params(("parallel",)),
    )(core, *by_owner, *got, *after)


def _shards_from_cols(name, full):
    R, allc = full.shape
    c = allc // N_DEV
    tr = _row_tile(R)

    def body(f_ref, o_ref):
        for s in range(N_DEV):
            o_ref[s] = f_ref[:, s * c:(s + 1) * c]

    return pl.pallas_call(
        body, name=name, grid=(R // tr,),
        in_specs=[pl.BlockSpec((tr, allc), lambda i: (i, 0))],
        out_specs=pl.BlockSpec((N_DEV, tr, c), lambda i: (0, i, 0)),
        out_shape=jax.ShapeDtypeStruct((N_DEV, R, c), full.dtype),
        compiler_params=_params(("parallel",)),
    )(full)


MESH = pl.DeviceIdType.MESH
HBM_SPEC = pl.BlockSpec(memory_space=pl.ANY)


def _handshake(peers):
    barrier = pltpu.get_barrier_semaphore()
    for peer in peers:
        pl.semaphore_signal(barrier, inc=1, device_id=peer, device_id_type=MESH)
    pl.semaphore_wait(barrier, len(peers))


def _comm_call(body, name, operands, out_shape, scratch, collective_id):
    if collective_id is None:
        return pl.pallas_call(body, name=name, in_specs=[HBM_SPEC] * len(operands), out_specs=[HBM_SPEC] * len(out_shape),
                              out_shape=out_shape, scratch_shapes=scratch)(*operands)
    return pl.kernel(body, out_type=out_shape, mesh=plsc.ScalarSubcoreMesh(axis_name="sequencer", num_cores=1),
                     scratch_types=scratch, name=name,
                     compiler_params=pltpu.CompilerParams(collective_id=collective_id))(*operands)


def _all_gather(name, blocks, collective_id=None, after=()):
    n = len(blocks)
    na = len(after)

    def body(*refs):
        x_refs, out_refs = refs[:n], refs[n + na:2 * n + na]
        send_sems, recv_sems, local_sems = refs[2 * n + na:]
        x, y, c = lax.axis_index("x"), lax.axis_index("y"), lax.axis_index("c")
        me, sibling = (x, y, c), (x, y, 1 - c)
        chips = [(1 - x, y), (x, 1 - y), (1 - x, 1 - y)]
        if collective_id is not None:
            _handshake([sibling] + [(*chip, c) for chip in chips])

        def slot(i, px, py, pc):
            return out_refs[i].at[4 * px + 2 * py + pc]

        def copy(i, k, blk, to, src=None):
            return pltpu.make_async_remote_copy(
                src_ref=slot(i, *blk) if src is None else src, dst_ref=slot(i, *blk),
                send_sem=send_sems.at[7 * i + k], recv_sem=recv_sems.at[7 * i + k], device_id=to, device_id_type=MESH)

        mine = [pltpu.make_async_copy(x_refs[i], slot(i, *me), local_sems.at[i]) for i in range(n)]
        for cp in mine:
            cp.start()
        first = []
        for i in range(n):
            first.append(copy(i, 0, me, sibling, src=x_refs[i]))
            first += [copy(i, 1 + j, me, (*chip, c), src=x_refs[i]) for j, chip in enumerate(chips)]
        for cp in first:
            cp.start()
        passed = []
        for i in range(n):
            for j, chip in enumerate(chips):
                copy(i, 1 + j, (*chip, c), me).wait_recv()
                passed.append(copy(i, 4 + j, (*chip, c), sibling))
                passed[-1].start()
        for i in range(n):
            copy(i, 0, sibling, me).wait_recv()
            for j, chip in enumerate(chips):
                copy(i, 4 + j, (*chip, 1 - c), me).wait_recv()
        for cp in first + passed:
            cp.wait_send()
        for cp in mine:
            cp.wait()

    return _comm_call(
        body, name, list(blocks) + list(after), [jax.ShapeDtypeStruct((N_DEV,) + b.shape, b.dtype) for b in blocks],
        [pltpu.SemaphoreType.DMA((7 * n,)), pltpu.SemaphoreType.DMA((7 * n,)), pltpu.SemaphoreType.DMA((n,))],
        collective_id)


def _sibling_swap(name, by_owner, collective_id=None, after=()):
    n = len(by_owner)
    na = len(after)

    def body(*refs):
        x_refs, out_refs = refs[:n], refs[n + na:2 * n + na]
        send_sems, recv_sems = refs[2 * n + na:]
        x, y, c = lax.axis_index("x"), lax.axis_index("y"), lax.axis_index("c")
        if collective_id is not None:
            _handshake([(x, y, 1 - c)])
        copies = []
        for i in range(n):
            for k in range(4):
                copies.append(pltpu.make_async_remote_copy(
                    src_ref=x_refs[i].at[2 * k + 1 - c], dst_ref=out_refs[i].at[k],
                    send_sem=send_sems.at[4 * i + k], recv_sem=recv_sems.at[4 * i + k],
                    device_id=(x, y, 1 - c), device_id_type=MESH))
        for cp in copies:
            cp.start()
        for cp in copies:
            cp.wait()

    return _comm_call(
        body, name, list(by_owner) + list(after),
        [jax.ShapeDtypeStruct((4,) + b.shape[1:], b.dtype) for b in by_owner],
        [pltpu.SemaphoreType.DMA((4 * n,)), pltpu.SemaphoreType.DMA((4 * n,))], collective_id)


def _chip_exchange(name, sums, collective_id=None, after=()):
    n = len(sums)
    na = len(after)

    def body(*refs):
        x_refs, out_refs = refs[:n], refs[n + na:2 * n + na]
        send_sems, recv_sems, local_sems = refs[2 * n + na:]
        x, y, c = lax.axis_index("x"), lax.axis_index("y"), lax.axis_index("c")
        chips = [(1 - x, y), (x, 1 - y), (1 - x, 1 - y)]
        my_chip = 2 * x + y
        if collective_id is not None:
            _handshake([(cx, cy, c) for cx, cy in chips])
        mine = [pltpu.make_async_copy(x_refs[i].at[my_chip], out_refs[i].at[my_chip], local_sems.at[i])
                for i in range(n)]
        for cp in mine:
            cp.start()
        sends = []
        for i in range(n):
            for j, (cx, cy) in enumerate(chips):
                sends.append(pltpu.make_async_remote_copy(
                    src_ref=x_refs[i].at[2 * cx + cy], dst_ref=out_refs[i].at[my_chip],
                    send_sem=send_sems.at[3 * i + j], recv_sem=recv_sems.at[3 * i + j],
                    device_id=(cx, cy, c), device_id_type=MESH))
        for cp in sends:
            cp.start()
        for i in range(n):
            for j, (cx, cy) in enumerate(chips):
                pltpu.make_async_remote_copy(
                    src_ref=x_refs[i].at[my_chip], dst_ref=out_refs[i].at[2 * cx + cy],
                    send_sem=send_sems.at[3 * i + j], recv_sem=recv_sems.at[3 * i + j],
                    device_id=(cx, cy, c), device_id_type=MESH).wait_recv()
        for cp in sends:
            cp.wait_send()
        for cp in mine:
            cp.wait()

    return _comm_call(
        body, name, list(sums) + list(after), [jax.ShapeDtypeStruct(s.shape, s.dtype) for s in sums],
        [pltpu.SemaphoreType.DMA((3 * n,)), pltpu.SemaphoreType.DMA((3 * n,)), pltpu.SemaphoreType.DMA((n,))],
        collective_id)


def _cast_shards(shards):
    n = len(shards)

    def body(*refs):
        for i in range(n):
            refs[n + i][...] = refs[i][...].astype(BF16)

    vmem = pl.BlockSpec(memory_space=pltpu.VMEM)
    return pl.pallas_call(
        body, name="cast_shards", in_specs=[vmem] * n, out_specs=[vmem] * n,
        out_shape=[jax.ShapeDtypeStruct(s.shape, BF16) for s in shards],
        compiler_params=pltpu.CompilerParams(vmem_limit_bytes=VMEM_LIMIT_V7X),
    )(*shards)


BIG = ("w_in", "w_branch_a", "w_branch_b", "w_out", "w_ffn_gate", "w_ffn_up", "w_ffn_down")


def _local_step(x, target, gains, low, conv_w, wg8, reduce):
    g_mix, g_hg, g_ffn, g_fin = gains
    w_in = wg8["w_in"].reshape(N_IN, D_MODEL)
    wg = wg8["w_ffn_gate"].reshape(D_FF, D_MODEL)
    wu = wg8["w_ffn_up"].reshape(D_FF, D_MODEL)
    wa, wb = wg8["w_branch_a"], wg8["w_branch_b"]
    wo = wg8["w_out"].reshape(D_MODEL, D_MODEL)
    wd = wg8["w_ffn_down"].reshape(D_FF, D_MODEL)

    ht, hg, cv, gt = _fwd_in(x, g_mix, w_in)
    o, og, ogt, st = _hg_fwd(hg, low, g_hg)
    cvo, cvot = _conv_fwd(cv, conv_w)
    x1, mgt = _merge_fwd(og, cvo, gt, x, wa, wb, wo)
    h2t, gate, up, actt, loss, d_gfin, dx2 = _ffn_fwd_loss(x1, g_ffn, wg, wu, wd, target, g_fin)

    dgate, dup, dx1, d_gffn = _ffn_bwd(dx2, x1, gate, up, g_ffn, wg, wu, wd)
    ffn = dict(
        w_ffn_down=_wgrad("wgrad_ffn_down", actt, dx2, 512).reshape(N_DEV, D_FF // N_DEV, D_MODEL),
        w_ffn_gate=_wgrad("wgrad_ffn_gate", h2t, dgate, 1408, transposed=True).reshape(N_DEV, D_FF // N_DEV, D_MODEL),
        w_ffn_up=_wgrad("wgrad_ffn_up", h2t, dup, 1408, transposed=True).reshape(N_DEV, D_FF // N_DEV, D_MODEL))
    sums_ffn, got_ffn = reduce.begin(ffn)
    dgt, dya, dyb, dog, dcvo = _merge_bwd(dx1, og, cvo, gt, wa, wb, wo)
    out = dict(
        w_out=_wgrad("wgrad_out", mgt, dx1, 512).reshape(N_DEV, D_MODEL // N_DEV, D_MODEL),
        w_branch_a=_shards_from_cols("split_w_branch_a", _wgrad("wgrad_branch_a", ogt, dya, 512)),
        w_branch_b=_shards_from_cols("split_w_branch_b", _wgrad("wgrad_branch_b", cvot, dyb, 512)))
    sums_out, got_out = reduce.begin(out, after=got_ffn[:1])
    parts_ffn, _ = reduce.finish(ffn, sums_ffn, after=got_out[:1])
    dq, df, di, dg, d_low, d_ghg = _hg_bwd(dog, hg, o, st, low, g_hg, after=list(sums_ffn) + list(sums_out))
    dc, db, dxb, d_conv = _conv_bwd(dcvo, cv, conv_w, after=[dq])
    parts_out, updated_out = reduce.finish(out, sums_out, after=[parts_ffn[0], dc])
    dparts = [dq, df, di, dg, dc, db, dxb, dgt]
    w_in_grad = dict(w_in=_wgrad_in(ht, dparts, after=parts_ffn[:1]).reshape(N_DEV, N_IN // N_DEV, D_MODEL))
    sums_in, _ = reduce.begin(w_in_grad, after=parts_out[:1], sum_after=updated_out)
    parts_in, _ = reduce.finish(w_in_grad, sums_in)
    grad_x, d_gmix = _in_bwd(dparts, w_in, x, dx1, g_mix, after=list(parts_out[:1]) + list(sums_in))
    small = dict(norm_mix_g=d_gmix, norm_ffn_g=d_gffn, norm_final_g=d_gfin, lower_bounds=d_low, hg_norm_g=d_ghg,
                 conv_w=d_conv, loss=loss)
    return grad_x, small, parts_in


def _conv_shard_rows(a):
    return jnp.pad(a, ((0, 5), (0, 64)))


def kernel(x, norm_mix_g, w_in, lower_bounds, hg_norm_g, conv_w, w_branch_a, w_branch_b, w_out, norm_ffn_g, w_ffn_gate, w_ffn_up, w_ffn_down, norm_final_g, loss_target, m_norm_mix_g, m_w_in, m_lower_bounds, m_hg_norm_g, m_conv_w, m_w_branch_a, m_w_branch_b, m_w_out, m_norm_ffn_g, m_w_ffn_gate, m_w_ffn_up, m_w_ffn_down, m_norm_final_g, v_norm_mix_g, v_w_in, v_lower_bounds, v_hg_norm_g, v_conv_w, v_w_branch_a, v_w_branch_b, v_w_out, v_norm_ffn_g, v_w_ffn_gate, v_w_ffn_up, v_w_ffn_down, v_norm_final_g):
    cx, cy, cc = lax.axis_index("x"), lax.axis_index("y"), lax.axis_index("c")
    my_dev = 4 * cx + 2 * cy + cc

    def tr(a):
        return a[0].T

    big = dict(w_in=tr(w_in), w_branch_a=w_branch_a[0], w_branch_b=w_branch_b[0], w_out=w_out[0],
               w_ffn_gate=tr(w_ffn_gate), w_ffn_up=tr(w_ffn_up), w_ffn_down=w_ffn_down[0])
    big_m = dict(w_in=tr(m_w_in), w_branch_a=m_w_branch_a[0], w_branch_b=m_w_branch_b[0], w_out=m_w_out[0],
                 w_ffn_gate=tr(m_w_ffn_gate), w_ffn_up=tr(m_w_ffn_up), w_ffn_down=m_w_ffn_down[0])
    big_v = dict(w_in=tr(v_w_in), w_branch_a=v_w_branch_a[0], w_branch_b=v_w_branch_b[0], w_out=v_w_out[0],
                 w_ffn_gate=tr(v_w_ffn_gate), w_ffn_up=tr(v_w_ffn_up), w_ffn_down=v_w_ffn_down[0])
    transposed = ("w_in", "w_ffn_gate", "w_ffn_up")

    shards = dict(zip(BIG, _cast_shards([big[n] for n in BIG])))
    first = _all_gather("gather_w_in", [shards["w_in"], _conv_shard_rows(conv_w[0])])
    ids = iter(range(1, 16))
    mid = _all_gather("gather_mid", [shards[n] for n in BIG[1:4]], collective_id=next(ids), after=first[1:])
    ffn = _all_gather("gather_ffn", [shards[n] for n in BIG[4:]], collective_id=next(ids), after=first[1:])
    wg8 = dict(zip(BIG, [first[0]] + list(mid) + list(ffn)))
    conv_full = first[1][:, :3, :64].transpose(1, 0, 2).reshape(3, CONV_WIDTH)

    core = cc.reshape(1).astype(jnp.int32)
    outs = {}

    class Reduce:
        @staticmethod
        def begin(grads, after=(), sum_after=()):
            names = list(grads)
            by_owner = [grads[n] for n in names]
            got = _sibling_swap("sibling_swap_" + names[0], by_owner, collective_id=next(ids), after=after)
            sums = _pair_sum("pair_sum_" + names[0], by_owner, got, core, after=sum_after)
            return sums, got

        @staticmethod
        def finish(grads, chip_sums, after=()):
            names = list(grads)
            parts = _chip_exchange("chip_exchange_" + names[0], chip_sums, collective_id=next(ids), after=after)
            for n, p in zip(names, parts):
                outs[n] = _adamw_sum("adamw_" + n, big[n], p, big_m[n], big_v[n])
            return parts, [outs[n][1] for n in names]

    gains = (norm_mix_g, hg_norm_g, norm_ffn_g, norm_final_g.reshape(1, D_MODEL))
    grad_x, small, last = _local_step(x[0], loss_target[0], gains, lower_bounds, conv_full, wg8, Reduce)

    small_all = _all_gather("gather_small", [_small_pack(small)], collective_id=next(ids), after=last[:1])

    def small_state(a):
        return [a[0], a[1], a[2].reshape(1, D_MODEL), a[3], a[4], a[5][0]]

    upd = _small_update(
        small_all[0], my_dev.reshape(1).astype(jnp.int32),
        small_state((norm_mix_g, norm_ffn_g, norm_final_g, lower_bounds, hg_norm_g, conv_w)),
        small_state((m_norm_mix_g, m_norm_ffn_g, m_norm_final_g, m_lower_bounds, m_hg_norm_g, m_conv_w)),
        small_state((v_norm_mix_g, v_norm_ffn_g, v_norm_final_g, v_lower_bounds, v_hg_norm_g, v_conv_w)))
    loss = upd[0][0, 0]
    small_shape = dict(norm_final_g=(D_MODEL,), conv_w=(1, 3, CONV_SHARD))
    for p, (name, _, _, _) in enumerate(_SMALL_PARAMS):
        outs[name] = [a.reshape(small_shape.get(name, a.shape)) for a in upd[1 + 4 * p:5 + 4 * p]]

    order = ["norm_mix_g", "w_in", "lower_bounds", "hg_norm_g", "conv_w", "w_branch_a", "w_branch_b", "w_out",
             "norm_ffn_g", "w_ffn_gate", "w_ffn_up", "w_ffn_down", "norm_final_g"]
    result = [loss, grad_x[None]]
    for k in range(4):
        for n in order:
            if n in BIG:
                result.append((outs[n][k].T if n in transposed else outs[n][k])[None])
            else:
                result.append(outs[n][k])
    return tuple(result)
```

```python
import jax
import jax.numpy as jnp
from jax import lax
from jax.experimental import pallas as pl
from jax.experimental.pallas import tpu as pltpu
from jax.experimental.pallas import tpu_sc as plsc

F32 = jnp.float32
BF16 = jnp.bfloat16
STASH = jnp.bfloat16

D_MODEL = 1024
HG_WIDTH = 512
HEAD_DIM = 128
N_HEADS = 4
HEADS_PER_STEP = 4
HEAD_GROUPS = N_HEADS // HEADS_PER_STEP
CONV_WIDTH = 512
D_FF = 2816
CHUNK = 32
EPS = 1e-6
Q_SCALE = HEAD_DIM ** -0.5
N_DEV = 8

ADAM_LR = 0.001
ADAM_B1 = 0.9
ADAM_B2 = 0.999
ADAM_EPS = 1e-08
ADAM_WD = 0.01
ADAM_STEP = 10

VMEM_LIMIT_V7X = 56 * 1024 * 1024

SMALL_ROWS = 16


def _params(sem, vmem=VMEM_LIMIT_V7X):
    return pltpu.CompilerParams(dimension_semantics=sem, vmem_limit_bytes=vmem)


def _mm(a, b):
    return jnp.dot(a.astype(BF16), b.astype(BF16), preferred_element_type=F32)


def _mm_nt(a, b):
    return lax.dot_general(a.astype(BF16), b.astype(BF16), (((1,), (1,)), ((), ())), preferred_element_type=F32)


def _mm_tn(a, b):
    return lax.dot_general(a.astype(BF16), b.astype(BF16), (((0,), (0,)), ((), ())), preferred_element_type=F32)


def _sigmoid(x):
    return 0.5 * jnp.tanh(0.5 * x) + 0.5


def _resident(shape):
    nd = len(shape)
    return pl.BlockSpec(shape, lambda *_: (0,) * nd, pipeline_mode=pl.Buffered(1))


def _full(shape):
    nd = len(shape)
    return pl.BlockSpec(shape, lambda *_: (0,) * nd)


def _shard_cols(w_ref):
    return jnp.concatenate([w_ref[s] for s in range(N_DEV)], axis=1)


N_HG = 4 * HG_WIDTH
N_CV = 3 * CONV_WIDTH
N_GT = 2 * D_MODEL
N_IN = N_HG + N_CV + N_GT


def _col(tm, n):
    return pl.BlockSpec((n, tm), lambda i: (0, i))


def _fwd_in(x, g, w_in_t):
    T = x.shape[0]
    tm = min(512, T)

    def body(x_ref, g_ref, w_ref, ht_ref, hg_ref, cv_ref, gt_ref):
        xv = x_ref[...]
        r = lax.rsqrt(jnp.mean(xv * xv, axis=-1, keepdims=True) + EPS)
        hf = xv * r * g_ref[...]
        h = hf.astype(BF16)
        ht_ref[...] = hf.T.astype(BF16)
        hg_ref[...] = _mm_nt(h, w_ref[:N_HG, :])
        cv_ref[...] = _mm_nt(h, w_ref[N_HG:N_HG + N_CV, :]).astype(STASH)
        gt_ref[...] = _mm_nt(h, w_ref[N_HG + N_CV:, :]).astype(STASH)

    row = lambda n: pl.BlockSpec((tm, n), lambda i: (i, 0))
    return pl.pallas_call(
        body, name="fwd_in", grid=(T // tm,),
        in_specs=[row(D_MODEL), _full((1, D_MODEL)), _resident(w_in_t.shape)],
        out_specs=[_col(tm, D_MODEL), row(N_HG), row(N_CV), row(N_GT)],
        out_shape=[jax.ShapeDtypeStruct((D_MODEL, T), BF16), jax.ShapeDtypeStruct((T, N_HG), F32),
                   jax.ShapeDtypeStruct((T, N_CV), STASH), jax.ShapeDtypeStruct((T, N_GT), STASH)],
        compiler_params=_params(("parallel",)),
    )(x, g, w_in_t)


def _chunk_pos(shape):
    return lax.broadcasted_iota(jnp.int32, shape, 0) & (CHUNK - 1)


def _chunk_cumsum(x, pos):
    s = 1
    while s < CHUNK:
        x = x + jnp.where(pos >= s, pltpu.roll(x, s, 0), 0.0)
        s *= 2
    return x


def _chunk_rev_cumsum(x, pos):
    n = x.shape[0]
    s = 1
    while s < CHUNK:
        x = x + jnp.where(pos + s < CHUNK, pltpu.roll(x, n - s, 0), 0.0)
        s *= 2
    return x


def _chunk_bcast(x3, row, tb):
    return jnp.broadcast_to(x3[:, row:row + 1, :], x3.shape).reshape(tb, x3.shape[-1])


def _lower_bound(low_ref):
    l0 = low_ref[0:1, :]
    l1 = low_ref[1:2, :]
    m = jnp.maximum(l0, l1)
    e0 = jnp.exp(l0 - m)
    e1 = jnp.exp(l1 - m)
    return e0 / (e0 + e1), e1 / (e0 + e1)


def _hg_gates(qr, fr, lb, pos, tb):
    sq = _sigmoid(qr)
    q = qr * sq * Q_SCALE
    sg = _sigmoid(fr)
    f = lb + (1.0 - lb) * sg
    k = 1.0 - f
    b = _chunk_cumsum(jnp.log(f), pos)
    b3 = b.reshape(tb // CHUNK, CHUNK, HEAD_DIM)
    anc = _chunk_bcast(b3, CHUNK // 2 - 1, tb)
    blb = _chunk_bcast(b3, CHUNK - 1, tb)
    e_qa = jnp.exp(b - anc)
    e_ka = jnp.exp(anc - b)
    e_b = jnp.exp(b)
    e_ko = jnp.exp(blb - b)
    dec = jnp.exp(blb)
    return sq, q, sg, f, k, e_qa, e_ka, e_b, e_ko, dec


def _intra_mask(sb):
    r = lax.broadcasted_iota(jnp.int32, (sb, sb), 0)
    c = lax.broadcasted_iota(jnp.int32, (sb, sb), 1)
    return ((r // CHUNK) == (c // CHUNK)) & (c <= r)


def _hg_fwd(hg, low, gn):
    T = hg.shape[0]
    tb = min(512, T)
    sb = min(256, tb)
    nb = T // tb
    nc = tb // CHUNK
    wid = HEADS_PER_STEP * HEAD_DIM

    def body(q_ref, f_ref, i_ref, g_ref, low_ref, gn_ref, o_ref, og_ref, ogt_ref, st_ref, s_scr):
        t = pl.program_id(1)

        @pl.when(t == 0)
        def _():
            s_scr[...] = jnp.zeros_like(s_scr)

        pos = _chunk_pos((tb, HEAD_DIM))
        mask = _intra_mask(sb)
        lanes = [slice(hh * HEAD_DIM, (hh + 1) * HEAD_DIM) for hh in range(HEADS_PER_STEP)]
        qi, ko, vb, dec, st = [], [], [], [], []
        for hh, ln in enumerate(lanes):
            lb, _ = _lower_bound(low_ref.at[:, ln])
            _, q, _, _, k, e_qa, e_ka, e_b, e_ko, dec_h = _hg_gates(q_ref[:, ln], f_ref[:, ln], lb, pos, tb)
            qh = (q * e_qa).astype(BF16)
            kh = (k * e_ka).astype(BF16)
            qi.append((q * e_b).astype(BF16))
            ko.append((k * e_ko).astype(BF16))
            vb.append(i_ref[:, ln].astype(BF16))
            dec.append(dec_h)
            st.append(s_scr[hh])
            for s in range(tb // sb):
                sl = slice(s * sb, (s + 1) * sb)
                p = jnp.where(mask, _mm_nt(qh[sl], kh[sl]), 0.0)
                o_ref[sl, ln] = _mm(p, vb[hh][sl])
        for c in range(nc):
            sl = slice(c * CHUNK, (c + 1) * CHUNK)
            for hh, ln in enumerate(lanes):
                st_ref[hh, c] = st[hh]
                o_ref[sl, ln] = o_ref[sl, ln] + _mm_nt(qi[hh][sl], st[hh])
                st[hh] = dec[hh][c * CHUNK:c * CHUNK + 1, :] * st[hh] + _mm_tn(vb[hh][sl], ko[hh][sl])
        for hh, ln in enumerate(lanes):
            s_scr[hh] = st[hh]
            o = o_ref[:, ln]
            r = lax.rsqrt(jnp.mean(o * o, axis=-1, keepdims=True) + EPS)
            gr = g_ref[:, ln]
            og = (o * r * gn_ref[...]) * (gr * _sigmoid(gr))
            og_ref[:, ln] = og.astype(BF16)
            ogt_ref[ln, :] = og.T.astype(BF16)

    col = lambda p: pl.BlockSpec((tb, wid), lambda h, t: (t, p * HEAD_GROUPS + h))
    hcol = pl.BlockSpec((tb, wid), lambda h, t: (t, h))
    return pl.pallas_call(
        body, name="hg_fwd", grid=(HEAD_GROUPS, nb),
        in_specs=[col(0), col(1), col(2), col(3), pl.BlockSpec((2, wid), lambda h, t: (0, h)),
                  pl.BlockSpec((1, HEAD_DIM), lambda h, t: (0, 0))],
        out_specs=[hcol, hcol, pl.BlockSpec((wid, tb), lambda h, t: (h, t)),
                   pl.BlockSpec((HEADS_PER_STEP, nc, HEAD_DIM, HEAD_DIM), lambda h, t: (h, t, 0, 0))],
        out_shape=[jax.ShapeDtypeStruct((T, HG_WIDTH), F32), jax.ShapeDtypeStruct((T, HG_WIDTH), BF16),
                   jax.ShapeDtypeStruct((HG_WIDTH, T), BF16),
                   jax.ShapeDtypeStruct((N_HEADS, T // CHUNK, HEAD_DIM, HEAD_DIM), F32)],
        scratch_shapes=[pltpu.VMEM((HEADS_PER_STEP, HEAD_DIM, HEAD_DIM), F32)],
        compiler_params=_params(("parallel", "arbitrary")),
    )(hg, hg, hg, hg, low, gn)


def _conv_fwd(cv, conv_w):
    T = cv.shape[0]
    nj = CONV_WIDTH // 128

    def body(c_ref, b_ref, x_ref, w_ref, o_ref, ot_ref):
        row = lax.broadcasted_iota(jnp.int32, (T, 128), 0)
        u = c_ref[...].astype(F32) * x_ref[...].astype(F32)
        u1 = jnp.where(row >= 1, pltpu.roll(u, 1, 0), 0.0)
        u2 = jnp.where(row >= 2, pltpu.roll(u, 2, 0), 0.0)
        y = w_ref[0:1, :] * u2 + w_ref[1:2, :] * u1 + w_ref[2:3, :] * u
        out = b_ref[...].astype(F32) * y
        o_ref[...] = out.astype(BF16)
        ot_ref[...] = out.T.astype(BF16)

    col = lambda p: pl.BlockSpec((T, 128), lambda j: (0, p * nj + j))
    return pl.pallas_call(
        body, name="conv_fwd", grid=(nj,),
        in_specs=[col(0), col(1), col(2), pl.BlockSpec((3, 128), lambda j: (0, j))],
        out_specs=[pl.BlockSpec((T, 128), lambda j: (0, j)), pl.BlockSpec((128, T), lambda j: (j, 0))],
        out_shape=[jax.ShapeDtypeStruct((T, CONV_WIDTH), BF16), jax.ShapeDtypeStruct((CONV_WIDTH, T), BF16)],
        compiler_params=_params(("parallel",)),
    )(cv, cv, cv, conv_w)


def _merge_fwd(og, cvo, gt, x, wa, wb, wo):
    T = x.shape[0]
    tm = min(512, T)

    def body(og_ref, cvo_ref, gt_ref, x_ref, wa_ref, wb_ref, wo_ref, x1_ref, mgt_ref):
        ya = jnp.dot(og_ref[...], _shard_cols(wa_ref), preferred_element_type=F32)
        yb = jnp.dot(cvo_ref[...], _shard_cols(wb_ref), preferred_element_type=F32)
        m = (_sigmoid(gt_ref[:, :D_MODEL].astype(F32)) * ya
             + _sigmoid(gt_ref[:, D_MODEL:].astype(F32)) * yb)
        mgt_ref[...] = m.T.astype(BF16)
        x1_ref[...] = x_ref[...] + jnp.dot(m.astype(BF16), wo_ref[...], preferred_element_type=F32)

    row = lambda n: pl.BlockSpec((tm, n), lambda i: (i, 0))
    return pl.pallas_call(
        body, name="merge_fwd", grid=(T // tm,),
        in_specs=[row(HG_WIDTH), row(CONV_WIDTH), row(2 * D_MODEL), row(D_MODEL),
                  _resident(wa.shape), _resident(wb.shape), _resident(wo.shape)],
        out_specs=[row(D_MODEL), _col(tm, D_MODEL)],
        out_shape=[jax.ShapeDtypeStruct((T, D_MODEL), F32), jax.ShapeDtypeStruct((D_MODEL, T), BF16)],
        compiler_params=_params(("parallel",)),
    )(og, cvo, gt, x, wa, wb, wo)


def _ffn_fwd_loss(x1, g, wg, wu, wd, target, g_fin):
    T = x1.shape[0]
    tm = min(256, T)

    def body(x_ref, g_ref, wg_ref, wu_ref, wd_ref, t_ref, gf_ref,
             ht_ref, gate_ref, up_ref, actt_ref, loss_ref, dgf_ref, dx2_ref):
        @pl.when(pl.program_id(0) == 0)
        def _():
            loss_ref[...] = jnp.zeros_like(loss_ref)
            dgf_ref[...] = jnp.zeros_like(dgf_ref)

        xv = x_ref[...]
        r = lax.rsqrt(jnp.mean(xv * xv, axis=-1, keepdims=True) + EPS)
        hf = xv * r * g_ref[...]
        h = hf.astype(BF16)
        ht_ref[...] = hf.T.astype(BF16)
        gate = _mm_nt(h, wg_ref[...])
        up = _mm_nt(h, wu_ref[...])
        gate_ref[...] = gate.astype(STASH)
        up_ref[...] = up.astype(STASH)
        act = gate * _sigmoid(gate) * up
        actt_ref[...] = act.T.astype(BF16)
        x2 = xv + jnp.dot(act.astype(BF16), wd_ref[...], preferred_element_type=F32)

        gv = gf_ref[...]
        r2 = lax.rsqrt(jnp.mean(x2 * x2, axis=-1, keepdims=True) + EPS)
        xh = x2 * r2
        err = xh * gv - t_ref[...]
        loss_ref[...] += 0.5 * jnp.sum(jnp.mean(err * err, axis=-1, keepdims=True), axis=0, keepdims=True)
        dy = err * (1.0 / D_MODEL)
        dgf_ref[...] += jnp.sum(dy * xh, axis=0, keepdims=True)
        w = dy * gv
        dx2_ref[...] = r2 * (w - xh * jnp.mean(w * xh, axis=-1, keepdims=True))

    row = lambda n: pl.BlockSpec((tm, n), lambda i: (i, 0))
    return pl.pallas_call(
        body, name="ffn_fwd_loss", grid=(T // tm,),
        in_specs=[row(D_MODEL), _full((1, D_MODEL)), _resident(wg.shape), _resident(wu.shape), _resident(wd.shape),
                  row(D_MODEL), _full((1, D_MODEL))],
        out_specs=[_col(tm, D_MODEL), row(D_FF), row(D_FF), _col(tm, D_FF), _full((1, 128)), _full((1, D_MODEL)),
                   row(D_MODEL)],
        out_shape=[jax.ShapeDtypeStruct((D_MODEL, T), BF16), jax.ShapeDtypeStruct((T, D_FF), STASH),
                   jax.ShapeDtypeStruct((T, D_FF), STASH), jax.ShapeDtypeStruct((D_FF, T), BF16),
                   jax.ShapeDtypeStruct((1, 128), F32), jax.ShapeDtypeStruct((1, D_MODEL), F32),
                   jax.ShapeDtypeStruct((T, D_MODEL), F32)],
        compiler_params=_params(("arbitrary",)),
    )(x1, g, wg, wu, wd, target, g_fin)


def _ffn_bwd(dx2, x1, gate, up, g, wg, wu, wd):
    T = x1.shape[0]
    tm = min(256, T)

    def body(dx2_ref, x_ref, gate_ref, up_ref, g_ref, wg_ref, wu_ref, wd_ref, dgate_ref, dup_ref, dx1_ref, dgn_ref):
        @pl.when(pl.program_id(0) == 0)
        def _():
            dgn_ref[...] = jnp.zeros_like(dgn_ref)

        dx2 = dx2_ref[...]
        dact = _mm_nt(dx2, wd_ref[...])
        gate = gate_ref[...].astype(F32)
        s = _sigmoid(gate)
        dgate = (dact * up_ref[...].astype(F32) * (s * (1.0 + gate * (1.0 - s)))).astype(BF16)
        dup = (dact * (gate * s)).astype(BF16)
        dgate_ref[...] = dgate
        dup_ref[...] = dup
        dh = _mm(dgate, wg_ref[...]) + _mm(dup, wu_ref[...])
        xv = x_ref[...]
        r = lax.rsqrt(jnp.mean(xv * xv, axis=-1, keepdims=True) + EPS)
        xh = xv * r
        dgn_ref[...] += jnp.sum(dh * xh, axis=0, keepdims=True)
        w = dh * g_ref[...]
        dx1_ref[...] = dx2 + r * (w - xh * jnp.mean(w * xh, axis=-1, keepdims=True))

    row = lambda n: pl.BlockSpec((tm, n), lambda i: (i, 0))
    return pl.pallas_call(
        body, name="ffn_bwd", grid=(T // tm,),
        in_specs=[row(D_MODEL), row(D_MODEL), row(D_FF), row(D_FF), _full((1, D_MODEL)),
                  _resident(wg.shape), _resident(wu.shape), _resident(wd.shape)],
        out_specs=[row(D_FF), row(D_FF), row(D_MODEL), _full((1, D_MODEL))],
        out_shape=[jax.ShapeDtypeStruct((T, D_FF), BF16), jax.ShapeDtypeStruct((T, D_FF), BF16),
                   jax.ShapeDtypeStruct((T, D_MODEL), F32), jax.ShapeDtypeStruct((1, D_MODEL), F32)],
        compiler_params=_params(("arbitrary",)),
    )(dx2, x1, gate, up, g, wg, wu, wd)


def _merge_bwd(dx1, og, cvo, gt, wa, wb, wo):
    T = dx1.shape[0]
    tm = min(512, T)

    def body(dx_ref, og_ref, cvo_ref, gt_ref, wa_ref, wb_ref, wo_ref, dgt_ref, dya_ref, dyb_ref, dog_ref, dcvo_ref):
        dm = _mm_nt(dx_ref[...], wo_ref[...])
        wa = _shard_cols(wa_ref)
        wb = _shard_cols(wb_ref)
        ya = jnp.dot(og_ref[...], wa, preferred_element_type=F32)
        yb = jnp.dot(cvo_ref[...], wb, preferred_element_type=F32)
        sa = _sigmoid(gt_ref[:, :D_MODEL].astype(F32))
        sb = _sigmoid(gt_ref[:, D_MODEL:].astype(F32))
        dgt_ref[:, :D_MODEL] = (dm * ya * (sa * (1.0 - sa))).astype(BF16)
        dgt_ref[:, D_MODEL:] = (dm * yb * (sb * (1.0 - sb))).astype(BF16)
        dya = (dm * sa).astype(BF16)
        dyb = (dm * sb).astype(BF16)
        dya_ref[...] = dya
        dyb_ref[...] = dyb
        dog_ref[...] = _mm_nt(dya, wa)
        dcvo_ref[...] = _mm_nt(dyb, wb)

    row = lambda n: pl.BlockSpec((tm, n), lambda i: (i, 0))
    return pl.pallas_call(
        body, name="merge_bwd", grid=(T // tm,),
        in_specs=[row(D_MODEL), row(HG_WIDTH), row(CONV_WIDTH), row(2 * D_MODEL),
                  _resident(wa.shape), _resident(wb.shape), _resident(wo.shape)],
        out_specs=[row(2 * D_MODEL), row(D_MODEL), row(D_MODEL), row(HG_WIDTH), row(CONV_WIDTH)],
        out_shape=[jax.ShapeDtypeStruct((T, 2 * D_MODEL), BF16), jax.ShapeDtypeStruct((T, D_MODEL), BF16),
                   jax.ShapeDtypeStruct((T, D_MODEL), BF16), jax.ShapeDtypeStruct((T, HG_WIDTH), F32),
                   jax.ShapeDtypeStruct((T, CONV_WIDTH), F32)],
        compiler_params=_params(("parallel",)),
    )(dx1, og, cvo, gt, wa, wb, wo)


def _conv_bwd(dcvo, cv, conv_w, after=()):
    T = cv.shape[0]
    nj = CONV_WIDTH // 128

    def body(do_ref, c_ref, b_ref, x_ref, w_ref, dc_ref, db_ref, dx_ref, dw_ref):
        row = lax.broadcasted_iota(jnp.int32, (T, 128), 0)
        c = c_ref[...].astype(F32)
        xb = x_ref[...].astype(F32)
        do = do_ref[...]
        u = c * xb
        u1 = jnp.where(row >= 1, pltpu.roll(u, 1, 0), 0.0)
        u2 = jnp.where(row >= 2, pltpu.roll(u, 2, 0), 0.0)
        w0, w1, w2 = w_ref[0:1, :], w_ref[1:2, :], w_ref[2:3, :]
        y = w0 * u2 + w1 * u1 + w2 * u
        db_ref[...] = (do * y).astype(BF16)
        dy = do * b_ref[...].astype(F32)
        dw_ref[0:1, :] = jnp.sum(dy * u2, axis=0, keepdims=True)
        dw_ref[1:2, :] = jnp.sum(dy * u1, axis=0, keepdims=True)
        dw_ref[2:3, :] = jnp.sum(dy * u, axis=0, keepdims=True)
        dy1 = jnp.where(row < T - 1, pltpu.roll(dy, T - 1, 0), 0.0)
        dy2 = jnp.where(row < T - 2, pltpu.roll(dy, T - 2, 0), 0.0)
        du = w2 * dy + w1 * dy1 + w0 * dy2
        dc_ref[...] = (du * xb).astype(BF16)
        dx_ref[...] = (du * c).astype(BF16)

    col = lambda p: pl.BlockSpec((T, 128), lambda j: (0, p * nj + j))
    one = pl.BlockSpec((T, 128), lambda j: (0, j))
    wspec = pl.BlockSpec((3, 128), lambda j: (0, j))
    out = jax.ShapeDtypeStruct((T, CONV_WIDTH), BF16)
    return pl.pallas_call(
        _drop_operands(body, 5, len(after)), name="conv_bwd", grid=(nj,),
        in_specs=[one, col(0), col(1), col(2), wspec] + [HBM_SPEC] * len(after),
        out_specs=[one, one, one, wspec],
        out_shape=[out, out, out, jax.ShapeDtypeStruct((3, CONV_WIDTH), F32)],
        compiler_params=_params(("parallel",)),
    )(dcvo, cv, cv, cv, conv_w, *after)


def _drop_operands(body, first, count):
    def wrapped(*refs):
        return body(*refs[:first], *refs[first + count:])
    return wrapped


def _hg_bwd(dog, hg, o, st, low, gn, after=()):
    T = hg.shape[0]
    tb = min(512, T)
    sb = min(256, tb)
    nb = T // tb
    nc = tb // CHUNK
    wid = HEADS_PER_STEP * HEAD_DIM

    def body(q_ref, f_ref, i_ref, g_ref, low_ref, gn_ref, o_ref, dog_ref, st_ref,
             dq_ref, df_ref, di_ref, dg_ref, dlow_ref, dgn_ref,
             ds_scr, dqi_scr, dko_scr, dv_scr, dd_scr, dqh_scr, dkh_scr):
        h = pl.program_id(0)
        t = pl.program_id(1)

        @pl.when(t == 0)
        def _():
            ds_scr[...] = jnp.zeros_like(ds_scr)
            dlow_ref[...] = jnp.zeros_like(dlow_ref)

        @pl.when((t == 0) & (h == 0))
        def _():
            dgn_ref[...] = jnp.zeros_like(dgn_ref)

        pos = _chunk_pos((tb, HEAD_DIM))
        mask = _intra_mask(sb)
        gnv = gn_ref[...]
        lanes = [slice(hh * HEAD_DIM, (hh + 1) * HEAD_DIM) for hh in range(HEADS_PER_STEP)]
        heads = []
        for hh, ln in enumerate(lanes):
            lb, lb1 = _lower_bound(low_ref.at[:, ln])
            qr = q_ref[:, ln]
            sq, q, sg, f, k, e_qa, e_ka, e_b, e_ko, dec = _hg_gates(qr, f_ref[:, ln], lb, pos, tb)

            gr = g_ref[:, ln]
            o = o_ref[:, ln]
            dog_v = dog_ref[:, ln]
            sgr = _sigmoid(gr)
            r = lax.rsqrt(jnp.mean(o * o, axis=-1, keepdims=True) + EPS)
            oh = o * r
            dg_ref[:, ln] = (dog_v * (oh * gnv) * (sgr * (1.0 + gr * (1.0 - sgr)))).astype(BF16)
            don = dog_v * (gr * sgr)
            dgn_ref[...] += jnp.sum(don * oh, axis=0, keepdims=True)
            w = don * gnv
            do = (r * (w - oh * jnp.mean(w * oh, axis=-1, keepdims=True))).astype(BF16)

            qh = (q * e_qa).astype(BF16)
            kh = (k * e_ka).astype(BF16)
            qi = (q * e_b).astype(BF16)
            ko = (k * e_ko).astype(BF16)
            vb = i_ref[:, ln].astype(BF16)

            for s in range(tb // sb):
                sl = slice(s * sb, (s + 1) * sb)
                p = jnp.where(mask, _mm_nt(qh[sl], kh[sl]), 0.0).astype(BF16)
                dp = jnp.where(mask, _mm_nt(do[sl], vb[sl]), 0.0).astype(BF16)
                dv_scr[sl, ln] = _mm_tn(p, do[sl])
                dqh_scr[sl, ln] = _mm(dp, kh[sl])
                dkh_scr[sl, ln] = _mm_tn(dp, qh[sl])
            heads.append(dict(lb=lb, lb1=lb1, qr=qr, sq=sq, q=q, sg=sg, f=f, k=k, e_qa=e_qa, e_ka=e_ka, e_b=e_b,
                              e_ko=e_ko, dec=dec, do=do, qi=qi, ko=ko, vb=vb, ds=ds_scr[hh]))

        for c in reversed(range(nc)):
            sl = slice(c * CHUNK, (c + 1) * CHUNK)
            for hh, ln in enumerate(lanes):
                hd = heads[hh]
                ds = hd["ds"]
                st_c = st_ref[hh, c]
                dqi_scr[sl, ln] = _mm(hd["do"][sl], st_c)
                dko_scr[sl, ln] = _mm(hd["vb"][sl], ds)
                dv_scr[sl, ln] = dv_scr[sl, ln] + _mm_nt(hd["ko"][sl], ds)
                dd_scr[sl, ln] = jnp.broadcast_to(jnp.sum(ds * st_c, axis=0, keepdims=True), (CHUNK, HEAD_DIM))
                hd["ds"] = hd["dec"][c * CHUNK:c * CHUNK + 1, :] * ds + _mm_tn(hd["do"][sl], hd["qi"][sl])

        for hh, ln in enumerate(lanes):
            hd = heads[hh]
            ds_scr[hh] = hd["ds"]
            q, k, lb = hd["q"], hd["k"], hd["lb"]
            dko_e = dko_scr[:, ln] * hd["e_ko"]
            dq = dqh_scr[:, ln] * hd["e_qa"] + dqi_scr[:, ln] * hd["e_b"]
            dk = dkh_scr[:, ln] * hd["e_ka"] + dko_e
            kd3 = (k * dko_e).reshape(nc, CHUNK, HEAD_DIM)
            last = jnp.broadcast_to(jnp.sum(kd3, axis=1, keepdims=True), kd3.shape).reshape(tb, HEAD_DIM)
            db = q * dq - k * dk + jnp.where(pos == CHUNK - 1, hd["dec"] * dd_scr[:, ln] + last, 0.0)
            dlg = _chunk_rev_cumsum(db, pos)
            dfv = dlg / hd["f"] - dk
            s_low = jnp.sum(dfv * (1.0 - hd["sg"]), axis=0, keepdims=True)
            dlow_ref[0:1, ln] += s_low * lb * (1.0 - lb)
            dlow_ref[1:2, ln] += -s_low * lb * hd["lb1"]
            df_ref[:, ln] = (dfv * (1.0 - lb) * hd["sg"] * (1.0 - hd["sg"])).astype(BF16)
            dq_ref[:, ln] = (dq * Q_SCALE * (hd["sq"] * (1.0 + hd["qr"] * (1.0 - hd["sq"])))).astype(BF16)
            di_ref[:, ln] = dv_scr[:, ln].astype(BF16)

    rt = lambda t: nb - 1 - t
    col = lambda p: pl.BlockSpec((tb, wid), lambda h, t: (rt(t), p * HEAD_GROUPS + h))
    hcol = pl.BlockSpec((tb, wid), lambda h, t: (rt(t), h))
    piece = jax.ShapeDtypeStruct((T, HG_WIDTH), BF16)
    tile = pltpu.VMEM((tb, wid), F32)
    return pl.pallas_call(
        _drop_operands(body, 9, len(after)), name="hg_bwd", grid=(HEAD_GROUPS, nb),
        in_specs=[col(0), col(1), col(2), col(3), pl.BlockSpec((2, wid), lambda h, t: (0, h)),
                  pl.BlockSpec((1, HEAD_DIM), lambda h, t: (0, 0)), hcol, hcol,
                  pl.BlockSpec((HEADS_PER_STEP, nc, HEAD_DIM, HEAD_DIM), lambda h, t: (h, rt(t), 0, 0))]
                 + [HBM_SPEC] * len(after),
        out_specs=[hcol, hcol, hcol, hcol, pl.BlockSpec((2, wid), lambda h, t: (0, h)),
                   pl.BlockSpec((1, HEAD_DIM), lambda h, t: (0, 0))],
        out_shape=[piece, piece, piece, piece, jax.ShapeDtypeStruct((2, HG_WIDTH), F32),
                   jax.ShapeDtypeStruct((1, HEAD_DIM), F32)],
        scratch_shapes=[pltpu.VMEM((HEADS_PER_STEP, HEAD_DIM, HEAD_DIM), F32), tile, tile, tile, tile, tile, tile],
        compiler_params=_params(("arbitrary", "arbitrary")),
    )(hg, hg, hg, hg, low, gn, o, dog, st, *after)


def _in_bwd(dparts, w_in, x, dx1, g, after=()):
    T = x.shape[0]
    tm = min(512, T)
    widths = [p.shape[1] for p in dparts]
    offs = [sum(widths[:i]) for i in range(len(widths))]
    n = len(dparts)

    def body(*refs):
        d_refs = refs[:n]
        w_ref, x_ref, dx1_ref, g_ref, dx_ref, dgn_ref = refs[n:]

        @pl.when(pl.program_id(0) == 0)
        def _():
            dgn_ref[...] = jnp.zeros_like(dgn_ref)

        dh = None
        for d_ref, off, wd in zip(d_refs, offs, widths):
            part = _mm(d_ref[...], w_ref[off:off + wd, :])
            dh = part if dh is None else dh + part
        xv = x_ref[...]
        r = lax.rsqrt(jnp.mean(xv * xv, axis=-1, keepdims=True) + EPS)
        xh = xv * r
        dgn_ref[...] += jnp.sum(dh * xh, axis=0, keepdims=True)
        w = dh * g_ref[...]
        dx_ref[...] = dx1_ref[...] + r * (w - xh * jnp.mean(w * xh, axis=-1, keepdims=True))

    row = lambda m: pl.BlockSpec((tm, m), lambda i: (i, 0))
    return pl.pallas_call(
        _drop_operands(body, n + 4, len(after)), name="in_bwd", grid=(T // tm,),
        in_specs=[row(wd) for wd in widths] + [_resident(w_in.shape), row(D_MODEL), row(D_MODEL), _full((1, D_MODEL))]
                 + [HBM_SPEC] * len(after),
        out_specs=[row(D_MODEL), _full((1, D_MODEL))],
        out_shape=[jax.ShapeDtypeStruct((T, D_MODEL), F32), jax.ShapeDtypeStruct((1, D_MODEL), F32)],
        compiler_params=_params(("arbitrary",)),
    )(*dparts, w_in, x, dx1, g, *after)


def _wgrad(name, at, b, tn, transposed=False, tk=2048, after=()):
    M, T = at.shape
    N = b.shape[1]
    tk = min(tk, T)
    nk = T // tk

    def body(a_ref, b_ref, o_ref, acc):
        k = pl.program_id(1)
        part = _mm(a_ref[...], b_ref[...])

        @pl.when(k == 0)
        def _():
            acc[...] = part

        @pl.when(k != 0)
        def _():
            acc[...] += part

        @pl.when(k == nk - 1)
        def _():
            o_ref[...] = (acc[...].T if transposed else acc[...]).astype(BF16)

    if transposed:
        out_spec, out_shape = pl.BlockSpec((tn, M), lambda j, k: (j, 0)), (N, M)
    else:
        out_spec, out_shape = pl.BlockSpec((M, tn), lambda j, k: (0, j)), (M, N)
    return pl.pallas_call(
        _drop_operands(body, 2, len(after)), name=name, grid=(N // tn, nk),
        in_specs=[pl.BlockSpec((M, tk), lambda j, k: (0, k)), pl.BlockSpec((tk, tn), lambda j, k: (k, j))]
                 + [HBM_SPEC] * len(after),
        out_specs=out_spec, out_shape=jax.ShapeDtypeStruct(out_shape, BF16),
        scratch_shapes=[pltpu.VMEM((M, tn), F32)],
        compiler_params=_params(("parallel", "arbitrary")),
    )(at, b, *after)


def _wgrad_in(ht, dparts, after=()):
    M, T = ht.shape
    tn = 512
    tk = min(2048, T)
    nk = T // tk
    nblk = [p.shape[1] // tn for p in dparts]
    start = [sum(nblk[:i]) for i in range(len(nblk))]
    n = len(dparts)

    def body(a_ref, *refs):
        d_refs, o_ref, acc = refs[:n], refs[n], refs[n + 1]
        j = pl.program_id(0)
        k = pl.program_id(1)

        @pl.when(k == 0)
        def _():
            acc[...] = jnp.zeros_like(acc)

        for d_ref, s, nb in zip(d_refs, start, nblk):
            @pl.when((j >= s) & (j < s + nb))
            def _():
                acc[...] += _mm(a_ref[...], d_ref[...])

        @pl.when(k == nk - 1)
        def _():
            o_ref[...] = acc[...].T.astype(BF16)

    def piece_spec(s, nb):
        def index(j, k):
            inside = (j >= s) & (j < s + nb)
            return jnp.where(inside, k, 0), jnp.clip(j - s, 0, nb - 1)
        return pl.BlockSpec((tk, tn), index)

    return pl.pallas_call(
        _drop_operands(body, 1 + n, len(after)), name="wgrad_in", grid=(sum(nblk), nk),
        in_specs=[pl.BlockSpec((M, tk), lambda j, k: (0, k))] + [piece_spec(s, nb) for s, nb in zip(start, nblk)]
                 + [HBM_SPEC] * len(after),
        out_specs=pl.BlockSpec((tn, M), lambda j, k: (j, 0)),
        out_shape=jax.ShapeDtypeStruct((sum(nblk) * tn, M), BF16),
        scratch_shapes=[pltpu.VMEM((M, tn), F32)],
        compiler_params=_params(("parallel", "arbitrary")),
    )(ht, *dparts, *after)


def _adamw_math(w, g, m, v):
    m = ADAM_B1 * m + (1.0 - ADAM_B1) * g
    v = ADAM_B2 * v + (1.0 - ADAM_B2) * (g * g)
    m_hat = m / (1.0 - ADAM_B1 ** ADAM_STEP)
    v_hat = v / (1.0 - ADAM_B2 ** ADAM_STEP)
    delta = -ADAM_LR * (m_hat / (jnp.sqrt(v_hat) + ADAM_EPS) + ADAM_WD * w)
    return delta, m, v


def _adamw_sum(name, w, parts, m, v):
    R, C = w.shape
    tr = _row_tile(R)

    def body(w_ref, p_ref, m_ref, v_ref, g_out, d_out, m_out, v_out):
        g = p_ref[0].astype(F32)
        for k in range(1, 4):
            g = g + p_ref[k].astype(F32)
        g_out[...] = g
        d_out[...], m_out[...], v_out[...] = _adamw_math(w_ref[...], g, m_ref[...], v_ref[...])

    blk = pl.BlockSpec((tr, C), lambda i: (i, 0))
    out = jax.ShapeDtypeStruct((R, C), F32)
    return pl.pallas_call(
        body, name=name, grid=(R // tr,),
        in_specs=[blk, pl.BlockSpec((4, tr, C), lambda i: (0, i, 0)), blk, blk],
        out_specs=[blk, blk, blk, blk], out_shape=[out, out, out, out],
        compiler_params=_params(("parallel",)),
    )(w, parts, m, v)


_SMALL_SLOTS = (("norm_mix_g", 0, 1, 1024), ("norm_ffn_g", 1, 1, 1024), ("norm_final_g", 2, 1, 1024),
                ("lower_bounds", 3, 2, 512), ("hg_norm_g", 5, 1, 128), ("loss", 6, 1, 128), ("conv_w", 8, 3, 512))
_SMALL_PARAMS = tuple(s for s in _SMALL_SLOTS if s[0] != "loss")
CONV_SHARD = CONV_WIDTH // N_DEV


def _small_pack(small):
    def body(*refs):
        out = refs[-1]
        out[...] = jnp.zeros_like(out)
        for ref, (_, row, rows, lanes) in zip(refs[:-1], _SMALL_SLOTS):
            out[row:row + rows, 0:lanes] = ref[...]

    vmem = pl.BlockSpec(memory_space=pltpu.VMEM)
    return pl.pallas_call(
        body, name="small_pack", in_specs=[vmem] * len(_SMALL_SLOTS), out_specs=vmem,
        out_shape=jax.ShapeDtypeStruct((SMALL_ROWS, 1024), F32),
    )(*[small[name] for name, _, _, _ in _SMALL_SLOTS])


def _small_update(gathered, dev, w, m, v):
    n = len(_SMALL_PARAMS)

    def body(dev_ref, g_ref, *refs):
        w_refs, m_refs, v_refs = refs[:n], refs[n:2 * n], refs[2 * n:3 * n]
        loss_ref, out_refs, sum_scr = refs[3 * n], refs[3 * n + 1:-1], refs[-1]
        total = g_ref[0]
        for k in range(1, N_DEV):
            total = total + g_ref[k]
        sum_scr[...] = total
        loss_ref[...] = sum_scr[6:7, 0:128]
        for p, (name, row, rows, lanes) in enumerate(_SMALL_PARAMS):
            if name == "conv_w":
                g = sum_scr[row:row + rows, 0:CONV_SHARD]
                for s in range(1, N_DEV):
                    g = jnp.where(dev_ref[0] == s, sum_scr[row:row + rows, s * CONV_SHARD:(s + 1) * CONV_SHARD], g)
            else:
                g = sum_scr[row:row + rows, 0:lanes]
            delta, m_new, v_new = _adamw_math(w_refs[p][...], g, m_refs[p][...], v_refs[p][...])
            out_refs[4 * p][...] = g
            out_refs[4 * p + 1][...] = delta
            out_refs[4 * p + 2][...] = m_new
            out_refs[4 * p + 3][...] = v_new

    vmem = pl.BlockSpec(memory_space=pltpu.VMEM)
    outs = [jax.ShapeDtypeStruct((1, 128), F32)]
    for a in w:
        outs += [jax.ShapeDtypeStruct(a.shape, F32)] * 4
    return pl.pallas_call(
        body, name="small_update",
        in_specs=[pl.BlockSpec(memory_space=pltpu.SMEM)] + [vmem] * (1 + 3 * n), out_specs=[vmem] * len(outs),
        out_shape=outs, scratch_shapes=[pltpu.VMEM((SMALL_ROWS, 1024), F32)],
    )(dev, gathered, *w, *m, *v)


def _row_tile(rows):
    for parts in (4, 2):
        if rows % (16 * parts) == 0:
            return rows // parts
    return rows


def _pair_sum(name, by_owner, got, core, after=()):
    n = len(got)

    def body(core_ref, *refs):
        for a_ref, b_ref, o_ref in zip(refs[:n], refs[n:2 * n], refs[2 * n:]):
            o_ref[...] = (a_ref[...].astype(F32) + b_ref[...].astype(F32)).astype(BF16)

    def blk(g):
        return pl.BlockSpec((None,) + g.shape[1:], lambda k, core_ref: (k, 0, 0))

    def mine(g):
        return pl.BlockSpec((None,) + g.shape[1:], lambda k, core_ref: (2 * k + core_ref[0], 0, 0))

    return pl.pallas_call(
        _drop_operands(body, 1 + 2 * n, len(after)), name=name,
        grid_spec=pltpu.PrefetchScalarGridSpec(
            num_scalar_prefetch=1, grid=(4,),
            in_specs=[mine(g) for g in got] + [blk(g) for g in got] + [HBM_SPEC] * len(after),
            out_specs=[blk(g) for g in got]),
        out_shape=[jax.ShapeDtypeStruct(g.shape, BF16) for g in got],
        compiler_params=_params(("parallel",)),
    )(core, *by_owner, *got, *after)


def _shards_from_cols(name, full):
    R, allc = full.shape
    c = allc // N_DEV
    tr = _row_tile(R)

    def body(f_ref, o_ref):
        for s in range(N_DEV):
            o_ref[s] = f_ref[:, s * c:(s + 1) * c]

    return pl.pallas_call(
        body, name=name, grid=(R // tr,),
        in_specs=[pl.BlockSpec((tr, allc), lambda i: (i, 0))],
        out_specs=pl.BlockSpec((N_DEV, tr, c), lambda i: (0, i, 0)),
        out_shape=jax.ShapeDtypeStruct((N_DEV, R, c), full.dtype),
        compiler_params=_params(("parallel",)),
    )(full)


MESH = pl.DeviceIdType.MESH
HBM_SPEC = pl.BlockSpec(memory_space=pl.ANY)


def _handshake(peers):
    barrier = pltpu.get_barrier_semaphore()
    for peer in peers:
        pl.semaphore_signal(barrier, inc=1, device_id=peer, device_id_type=MESH)
    pl.semaphore_wait(barrier, len(peers))


def _comm_call(body, name, operands, out_shape, scratch, collective_id):
    if collective_id is None:
        return pl.pallas_call(body, name=name, in_specs=[HBM_SPEC] * len(operands), out_specs=[HBM_SPEC] * len(out_shape),
                              out_shape=out_shape, scratch_shapes=scratch)(*operands)
    return pl.kernel(body, out_type=out_shape, mesh=plsc.ScalarSubcoreMesh(axis_name="sequencer", num_cores=1),
                     scratch_types=scratch, name=name,
                     compiler_params=pltpu.CompilerParams(collective_id=collective_id))(*operands)


def _all_gather(name, blocks, collective_id=None, after=()):
    n = len(blocks)
    na = len(after)

    def body(*refs):
        x_refs, out_refs = refs[:n], refs[n + na:2 * n + na]
        send_sems, recv_sems, local_sems = refs[2 * n + na:]
        x, y, c = lax.axis_index("x"), lax.axis_index("y"), lax.axis_index("c")
        me, sibling = (x, y, c), (x, y, 1 - c)
        chips = [(1 - x, y), (x, 1 - y), (1 - x, 1 - y)]
        if collective_id is not None:
            _handshake([sibling] + [(*chip, c) for chip in chips])

        def slot(i, px, py, pc):
            return out_refs[i].at[4 * px + 2 * py + pc]

        def copy(i, k, blk, to, src=None):
            return pltpu.make_async_remote_copy(
                src_ref=slot(i, *blk) if src is None else src, dst_ref=slot(i, *blk),
                send_sem=send_sems.at[7 * i + k], recv_sem=recv_sems.at[7 * i + k], device_id=to, device_id_type=MESH)

        mine = [pltpu.make_async_copy(x_refs[i], slot(i, *me), local_sems.at[i]) for i in range(n)]
        for cp in mine:
            cp.start()
        first = []
        for i in range(n):
            first.append(copy(i, 0, me, sibling, src=x_refs[i]))
            first += [copy(i, 1 + j, me, (*chip, c), src=x_refs[i]) for j, chip in enumerate(chips)]
        for cp in first:
            cp.start()
        passed = []
        for i in range(n):
            for j, chip in enumerate(chips):
                copy(i, 1 + j, (*chip, c), me).wait_recv()
                passed.append(copy(i, 4 + j, (*chip, c), sibling))
                passed[-1].start()
        for i in range(n):
            copy(i, 0, sibling, me).wait_recv()
            for j, chip in enumerate(chips):
                copy(i, 4 + j, (*chip, 1 - c), me).wait_recv()
        for cp in first + passed:
            cp.wait_send()
        for cp in mine:
            cp.wait()

    return _comm_call(
        body, name, list(blocks) + list(after), [jax.ShapeDtypeStruct((N_DEV,) + b.shape, b.dtype) for b in blocks],
        [pltpu.SemaphoreType.DMA((7 * n,)), pltpu.SemaphoreType.DMA((7 * n,)), pltpu.SemaphoreType.DMA((n,))],
        collective_id)


def _sibling_swap(name, by_owner, collective_id=None, after=()):
    n = len(by_owner)
    na = len(after)

    def body(*refs):
        x_refs, out_refs = refs[:n], refs[n + na:2 * n + na]
        send_sems, recv_sems = refs[2 * n + na:]
        x, y, c = lax.axis_index("x"), lax.axis_index("y"), lax.axis_index("c")
        if collective_id is not None:
            _handshake([(x, y, 1 - c)])
        copies = []
        for i in range(n):
            for k in range(4):
                copies.append(pltpu.make_async_remote_copy(
                    src_ref=x_refs[i].at[2 * k + 1 - c], dst_ref=out_refs[i].at[k],
                    send_sem=send_sems.at[4 * i + k], recv_sem=recv_sems.at[4 * i + k],
                    device_id=(x, y, 1 - c), device_id_type=MESH))
        for cp in copies:
            cp.start()
        for cp in copies:
            cp.wait()

    return _comm_call(
        body, name, list(by_owner) + list(after),
        [jax.ShapeDtypeStruct((4,) + b.shape[1:], b.dtype) for b in by_owner],
        [pltpu.SemaphoreType.DMA((4 * n,)), pltpu.SemaphoreType.DMA((4 * n,))], collective_id)


def _chip_exchange(name, sums, collective_id=None, after=()):
    n = len(sums)
    na = len(after)

    def body(*refs):
        x_refs, out_refs = refs[:n], refs[n + na:2 * n + na]
        send_sems, recv_sems, local_sems = refs[2 * n + na:]
        x, y, c = lax.axis_index("x"), lax.axis_index("y"), lax.axis_index("c")
        chips = [(1 - x, y), (x, 1 - y), (1 - x, 1 - y)]
        my_chip = 2 * x + y
        if collective_id is not None:
            _handshake([(cx, cy, c) for cx, cy in chips])
        mine = [pltpu.make_async_copy(x_refs[i].at[my_chip], out_refs[i].at[my_chip], local_sems.at[i])
                for i in range(n)]
        for cp in mine:
            cp.start()
        sends = []
        for i in range(n):
            for j, (cx, cy) in enumerate(chips):
                sends.append(pltpu.make_async_remote_copy(
                    src_ref=x_refs[i].at[2 * cx + cy], dst_ref=out_refs[i].at[my_chip],
                    send_sem=send_sems.at[3 * i + j], recv_sem=recv_sems.at[3 * i + j],
                    device_id=(cx, cy, c), device_id_type=MESH))
        for cp in sends:
            cp.start()
        for i in range(n):
            for j, (cx, cy) in enumerate(chips):
                pltpu.make_async_remote_copy(
                    src_ref=x_refs[i].at[my_chip], dst_ref=out_refs[i].at[2 * cx + cy],
                    send_sem=send_sems.at[3 * i + j], recv_sem=recv_sems.at[3 * i + j],
                    device_id=(cx, cy, c), device_id_type=MESH).wait_recv()
        for cp in sends:
            cp.wait_send()
        for cp in mine:
            cp.wait()

    return _comm_call(
        body, name, list(sums) + list(after), [jax.ShapeDtypeStruct(s.shape, s.dtype) for s in sums],
        [pltpu.SemaphoreType.DMA((3 * n,)), pltpu.SemaphoreType.DMA((3 * n,)), pltpu.SemaphoreType.DMA((n,))],
        collective_id)


def _cast_shards(shards):
    n = len(shards)

    def body(*refs):
        for i in range(n):
            refs[n + i][...] = refs[i][...].astype(BF16)

    vmem = pl.BlockSpec(memory_space=pltpu.VMEM)
    return pl.pallas_call(
        body, name="cast_shards", in_specs=[vmem] * n, out_specs=[vmem] * n,
        out_shape=[jax.ShapeDtypeStruct(s.shape, BF16) for s in shards],
        compiler_params=pltpu.CompilerParams(vmem_limit_bytes=VMEM_LIMIT_V7X),
    )(*shards)


BIG = ("w_in", "w_branch_a", "w_branch_b", "w_out", "w_ffn_gate", "w_ffn_up", "w_ffn_down")


def _local_step(x, target, gains, low, conv_w, wg8, reduce):
    g_mix, g_hg, g_ffn, g_fin = gains
    w_in = wg8["w_in"].reshape(N_IN, D_MODEL)
    wg = wg8["w_ffn_gate"].reshape(D_FF, D_MODEL)
    wu = wg8["w_ffn_up"].reshape(D_FF, D_MODEL)
    wa, wb = wg8["w_branch_a"], wg8["w_branch_b"]
    wo = wg8["w_out"].reshape(D_MODEL, D_MODEL)
    wd = wg8["w_ffn_down"].reshape(D_FF, D_MODEL)

    ht, hg, cv, gt = _fwd_in(x, g_mix, w_in)
    o, og, ogt, st = _hg_fwd(hg, low, g_hg)
    cvo, cvot = _conv_fwd(cv, conv_w)
    x1, mgt = _merge_fwd(og, cvo, gt, x, wa, wb, wo)
    h2t, gate, up, actt, loss, d_gfin, dx2 = _ffn_fwd_loss(x1, g_ffn, wg, wu, wd, target, g_fin)

    dgate, dup, dx1, d_gffn = _ffn_bwd(dx2, x1, gate, up, g_ffn, wg, wu, wd)
    ffn = dict(
        w_ffn_down=_wgrad("wgrad_ffn_down", actt, dx2, 512).reshape(N_DEV, D_FF // N_DEV, D_MODEL),
        w_ffn_gate=_wgrad("wgrad_ffn_gate", h2t, dgate, 1408, transposed=True).reshape(N_DEV, D_FF // N_DEV, D_MODEL),
        w_ffn_up=_wgrad("wgrad_ffn_up", h2t, dup, 1408, transposed=True).reshape(N_DEV, D_FF // N_DEV, D_MODEL))
    dgt, dya, dyb, dog, dcvo = _merge_bwd(dx1, og, cvo, gt, wa, wb, wo)
    sums_ffn, got_ffn = reduce.begin(ffn, sum_after=[dya])
    parts_ffn, _ = reduce.finish(ffn, sums_ffn)
    out = dict(
        w_out=_wgrad("wgrad_out", mgt, dx1, 512, after=sums_ffn[:1]).reshape(N_DEV, D_MODEL // N_DEV, D_MODEL),
        w_branch_a=_shards_from_cols("split_w_branch_a", _wgrad("wgrad_branch_a", ogt, dya, 512)),
        w_branch_b=_shards_from_cols("split_w_branch_b", _wgrad("wgrad_branch_b", cvot, dyb, 512)))
    dq, df, di, dg, d_low, d_ghg = _hg_bwd(dog, hg, o, st, low, g_hg, after=list(sums_ffn) + [out["w_out"]])
    sums_out, got_out = reduce.begin(out, after=[parts_ffn[0], dq])
    dc, db, dxb, d_conv = _conv_bwd(dcvo, cv, conv_w, after=[dq])
    parts_out, updated_out = reduce.finish(out, sums_out, after=[dc])
    dparts = [dq, df, di, dg, dc, db, dxb, dgt]
    w_in_grad = dict(w_in=_wgrad_in(ht, dparts, after=sums_out[:1]).reshape(N_DEV, N_IN // N_DEV, D_MODEL))
    sums_in, _ = reduce.begin(w_in_grad, after=parts_out[:1], sum_after=updated_out)
    parts_in, _ = reduce.finish(w_in_grad, sums_in)
    grad_x, d_gmix = _in_bwd(dparts, w_in, x, dx1, g_mix, after=list(parts_out[:1]) + list(sums_in))
    small = dict(norm_mix_g=d_gmix, norm_ffn_g=d_gffn, norm_final_g=d_gfin, lower_bounds=d_low, hg_norm_g=d_ghg,
                 conv_w=d_conv, loss=loss)
    return grad_x, small, parts_in


def _conv_shard_rows(a):
    return jnp.pad(a, ((0, 5), (0, 64)))


def kernel(x, norm_mix_g, w_in, lower_bounds, hg_norm_g, conv_w, w_branch_a, w_branch_b, w_out, norm_ffn_g, w_ffn_gate, w_ffn_up, w_ffn_down, norm_final_g, loss_target, m_norm_mix_g, m_w_in, m_lower_bounds, m_hg_norm_g, m_conv_w, m_w_branch_a, m_w_branch_b, m_w_out, m_norm_ffn_g, m_w_ffn_gate, m_w_ffn_up, m_w_ffn_down, m_norm_final_g, v_norm_mix_g, v_w_in, v_lower_bounds, v_hg_norm_g, v_conv_w, v_w_branch_a, v_w_branch_b, v_w_out, v_norm_ffn_g, v_w_ffn_gate, v_w_ffn_up, v_w_ffn_down, v_norm_final_g):
    cx, cy, cc = lax.axis_index("x"), lax.axis_index("y"), lax.axis_index("c")
    my_dev = 4 * cx + 2 * cy + cc

    def tr(a):
        return a[0].T

    big = dict(w_in=tr(w_in), w_branch_a=w_branch_a[0], w_branch_b=w_branch_b[0], w_out=w_out[0],
               w_ffn_gate=tr(w_ffn_gate), w_ffn_up=tr(w_ffn_up), w_ffn_down=w_ffn_down[0])
    big_m = dict(w_in=tr(m_w_in), w_branch_a=m_w_branch_a[0], w_branch_b=m_w_branch_b[0], w_out=m_w_out[0],
                 w_ffn_gate=tr(m_w_ffn_gate), w_ffn_up=tr(m_w_ffn_up), w_ffn_down=m_w_ffn_down[0])
    big_v = dict(w_in=tr(v_w_in), w_branch_a=v_w_branch_a[0], w_branch_b=v_w_branch_b[0], w_out=v_w_out[0],
                 w_ffn_gate=tr(v_w_ffn_gate), w_ffn_up=tr(v_w_ffn_up), w_ffn_down=v_w_ffn_down[0])
    transposed = ("w_in", "w_ffn_gate", "w_ffn_up")

    shards = dict(zip(BIG, _cast_shards([big[n] for n in BIG])))
    first = _all_gather("gather_w_in", [shards["w_in"], _conv_shard_rows(conv_w[0])])
    ids = iter(range(1, 16))
    mid = _all_gather("gather_mid", [shards[n] for n in BIG[1:4]], collective_id=next(ids), after=first[1:])
    ffn = _all_gather("gather_ffn", [shards[n] for n in BIG[4:]], collective_id=next(ids), after=first[1:])
    wg8 = dict(zip(BIG, [first[0]] + list(mid) + list(ffn)))
    conv_full = first[1][:, :3, :64].transpose(1, 0, 2).reshape(3, CONV_WIDTH)

    core = cc.reshape(1).astype(jnp.int32)
    outs = {}

    class Reduce:
        @staticmethod
        def begin(grads, after=(), sum_after=()):
            names = list(grads)
            by_owner = [grads[n] for n in names]
            got = _sibling_swap("sibling_swap_" + names[0], by_owner, collective_id=next(ids), after=after)
            sums = _pair_sum("pair_sum_" + names[0], by_owner, got, core, after=sum_after)
            return sums, got

        @staticmethod
        def finish(grads, chip_sums, after=()):
            names = list(grads)
            parts = _chip_exchange("chip_exchange_" + names[0], chip_sums, collective_id=next(ids), after=after)
            for n, p in zip(names, parts):
                outs[n] = _adamw_sum("adamw_" + n, big[n], p, big_m[n], big_v[n])
            return parts, [outs[n][1] for n in names]

    gains = (norm_mix_g, hg_norm_g, norm_ffn_g, norm_final_g.reshape(1, D_MODEL))
    grad_x, small, last = _local_step(x[0], loss_target[0], gains, lower_bounds, conv_full, wg8, Reduce)

    small_all = _all_gather("gather_small", [_small_pack(small)], collective_id=next(ids), after=last[:1])

    def small_state(a):
        return [a[0], a[1], a[2].reshape(1, D_MODEL), a[3], a[4], a[5][0]]

    upd = _small_update(
        small_all[0], my_dev.reshape(1).astype(jnp.int32),
        small_state((norm_mix_g, norm_ffn_g, norm_final_g, lower_bounds, hg_norm_g, conv_w)),
        small_state((m_norm_mix_g, m_norm_ffn_g, m_norm_final_g, m_lower_bounds, m_hg_norm_g, m_conv_w)),
        small_state((v_norm_mix_g, v_norm_ffn_g, v_norm_final_g, v_lower_bounds, v_hg_norm_g, v_conv_w)))
    loss = upd[0][0, 0]
    small_shape = dict(norm_final_g=(D_MODEL,), conv_w=(1, 3, CONV_SHARD))
    for p, (name, _, _, _) in enumerate(_SMALL_PARAMS):
        outs[name] = [a.reshape(small_shape.get(name, a.shape)) for a in upd[1 + 4 * p:5 + 4 * p]]

    order = ["norm_mix_g", "w_in", "lower_bounds", "hg_norm_g", "conv_w", "w_branch_a", "w_branch_b", "w_out",
             "norm_ffn_g", "w_ffn_gate", "w_ffn_up", "w_ffn_down", "norm_final_g"]
    result = [loss, grad_x[None]]
    for k in range(4):
        for n in order:
            if n in BIG:
                result.append((outs[n][k].T if n in transposed else outs[n][k])[None])
            else:
                result.append(outs[n][k])
    return tuple(result)
```

```python
import jax
import jax.numpy as jnp
from jax import lax
from jax.experimental import pallas as pl
from jax.experimental.pallas import tpu as pltpu
from jax.experimental.pallas import tpu_sc as plsc

F32 = jnp.float32
BF16 = jnp.bfloat16
STASH = jnp.bfloat16

D_MODEL = 1024
HG_WIDTH = 512
HEAD_DIM = 128
N_HEADS = 4
HEADS_PER_STEP = 4
HEAD_GROUPS = N_HEADS // HEADS_PER_STEP
CONV_WIDTH = 512
CONV_K = 3
D_FF = 2816
CHUNK = 32
EPS = 1e-6
Q_SCALE = HEAD_DIM ** -0.5
N_DEV = 8

ADAM_LR = 0.001
ADAM_B1 = 0.9
ADAM_B2 = 0.999
ADAM_EPS = 1e-08
ADAM_WD = 0.01
ADAM_STEP = 10

VMEM_LIMIT_V7X = 56 * 1024 * 1024

SMALL_ROWS = 16


def _params(sem, vmem=VMEM_LIMIT_V7X):
    return pltpu.CompilerParams(dimension_semantics=sem, vmem_limit_bytes=vmem)


def _mm(a, b):
    return jnp.dot(a.astype(BF16), b.astype(BF16), preferred_element_type=F32)


def _mm_nt(a, b):
    return lax.dot_general(a.astype(BF16), b.astype(BF16), (((1,), (1,)), ((), ())), preferred_element_type=F32)


def _mm_tn(a, b):
    return lax.dot_general(a.astype(BF16), b.astype(BF16), (((0,), (0,)), ((), ())), preferred_element_type=F32)


def _sigmoid(x):
    return 0.5 * jnp.tanh(0.5 * x) + 0.5


def _resident(shape):
    nd = len(shape)
    return pl.BlockSpec(shape, lambda *_: (0,) * nd, pipeline_mode=pl.Buffered(1))


def _full(shape):
    nd = len(shape)
    return pl.BlockSpec(shape, lambda *_: (0,) * nd)


def _shard_cols(w_ref):
    return jnp.concatenate([w_ref[s] for s in range(N_DEV)], axis=1)


N_HG = 4 * HG_WIDTH
N_CV = 3 * CONV_WIDTH
N_GT = 2 * D_MODEL
N_IN = N_HG + N_CV + N_GT


def _col(tm, n):
    return pl.BlockSpec((n, tm), lambda i: (0, i))


HALO = 8


def _fwd_in(x, g, w_in_t, conv_w):
    T = x.shape[0]
    tm = min(512, T)

    def body(x_ref, g_ref, w_ref, cw_ref, ht_ref, hg_ref, cv_ref, gt_ref, cvo_ref, cvot_ref, tail_scr):
        @pl.when(pl.program_id(0) == 0)
        def _():
            tail_scr[...] = jnp.zeros_like(tail_scr)

        xv = x_ref[...]
        r = lax.rsqrt(jnp.mean(xv * xv, axis=-1, keepdims=True) + EPS)
        hf = xv * r * g_ref[...]
        h = hf.astype(BF16)
        ht_ref[...] = hf.T.astype(BF16)
        hg_ref[...] = _mm_nt(h, w_ref[:N_HG, :])
        cv = _mm_nt(h, w_ref[N_HG:N_HG + N_CV, :])
        cv_ref[...] = cv.astype(STASH)
        gt_ref[...] = _mm_nt(h, w_ref[N_HG + N_CV:, :]).astype(STASH)

        u = cv[:, :CONV_WIDTH] * cv[:, 2 * CONV_WIDTH:]
        row = lax.broadcasted_iota(jnp.int32, u.shape, 0)
        prev1 = tail_scr[HALO - 1:HALO, :]
        prev2 = tail_scr[HALO - 2:HALO - 1, :]
        u1 = jnp.where(row >= 1, pltpu.roll(u, 1, 0), prev1)
        u2 = jnp.where(row >= 2, pltpu.roll(u, 2, 0), jnp.where(row == 1, prev1, prev2))
        y = cw_ref[0:1, :] * u2 + cw_ref[1:2, :] * u1 + cw_ref[2:3, :] * u
        out = cv[:, CONV_WIDTH:2 * CONV_WIDTH] * y
        cvo_ref[...] = out.astype(BF16)
        cvot_ref[...] = out.T.astype(BF16)
        tail_scr[...] = u[tm - HALO:, :]

    row = lambda n: pl.BlockSpec((tm, n), lambda i: (i, 0))
    return pl.pallas_call(
        body, name="fwd_in", grid=(T // tm,),
        in_specs=[row(D_MODEL), _full((1, D_MODEL)), _resident(w_in_t.shape), _full((CONV_K, CONV_WIDTH))],
        out_specs=[_col(tm, D_MODEL), row(N_HG), row(N_CV), row(N_GT), row(CONV_WIDTH), _col(tm, CONV_WIDTH)],
        out_shape=[jax.ShapeDtypeStruct((D_MODEL, T), BF16), jax.ShapeDtypeStruct((T, N_HG), F32),
                   jax.ShapeDtypeStruct((T, N_CV), STASH), jax.ShapeDtypeStruct((T, N_GT), STASH),
                   jax.ShapeDtypeStruct((T, CONV_WIDTH), BF16), jax.ShapeDtypeStruct((CONV_WIDTH, T), BF16)],
        scratch_shapes=[pltpu.VMEM((HALO, CONV_WIDTH), F32)],
        compiler_params=_params(("arbitrary",)),
    )(x, g, w_in_t, conv_w)


def _chunk_pos(shape):
    return lax.broadcasted_iota(jnp.int32, shape, 0) & (CHUNK - 1)


def _chunk_cumsum(x, pos):
    s = 1
    while s < CHUNK:
        x = x + jnp.where(pos >= s, pltpu.roll(x, s, 0), 0.0)
        s *= 2
    return x


def _chunk_rev_cumsum(x, pos):
    n = x.shape[0]
    s = 1
    while s < CHUNK:
        x = x + jnp.where(pos + s < CHUNK, pltpu.roll(x, n - s, 0), 0.0)
        s *= 2
    return x


def _chunk_bcast(x3, row, tb):
    return jnp.broadcast_to(x3[:, row:row + 1, :], x3.shape).reshape(tb, x3.shape[-1])


def _lower_bound(low_ref):
    l0 = low_ref[0:1, :]
    l1 = low_ref[1:2, :]
    m = jnp.maximum(l0, l1)
    e0 = jnp.exp(l0 - m)
    e1 = jnp.exp(l1 - m)
    return e0 / (e0 + e1), e1 / (e0 + e1)


def _hg_gates(qr, fr, lb, pos, tb):
    sq = _sigmoid(qr)
    q = qr * sq * Q_SCALE
    sg = _sigmoid(fr)
    f = lb + (1.0 - lb) * sg
    k = 1.0 - f
    b = _chunk_cumsum(jnp.log(f), pos)
    b3 = b.reshape(tb // CHUNK, CHUNK, HEAD_DIM)
    anc = _chunk_bcast(b3, CHUNK // 2 - 1, tb)
    blb = _chunk_bcast(b3, CHUNK - 1, tb)
    e_qa = jnp.exp(b - anc)
    e_ka = jnp.exp(anc - b)
    e_b = jnp.exp(b)
    e_ko = jnp.exp(blb - b)
    dec = jnp.exp(blb)
    return sq, q, sg, f, k, e_qa, e_ka, e_b, e_ko, dec


def _intra_mask(sb):
    r = lax.broadcasted_iota(jnp.int32, (sb, sb), 0)
    c = lax.broadcasted_iota(jnp.int32, (sb, sb), 1)
    return ((r // CHUNK) == (c // CHUNK)) & (c <= r)


def _hg_fwd(hg, low, gn):
    T = hg.shape[0]
    tb = min(512, T)
    sb = min(256, tb)
    nb = T // tb
    nc = tb // CHUNK
    wid = HEADS_PER_STEP * HEAD_DIM

    def body(q_ref, f_ref, i_ref, g_ref, low_ref, gn_ref, o_ref, og_ref, ogt_ref, st_ref, s_scr):
        t = pl.program_id(1)

        @pl.when(t == 0)
        def _():
            s_scr[...] = jnp.zeros_like(s_scr)

        pos = _chunk_pos((tb, HEAD_DIM))
        mask = _intra_mask(sb)
        lanes = [slice(hh * HEAD_DIM, (hh + 1) * HEAD_DIM) for hh in range(HEADS_PER_STEP)]
        qi, ko, vb, dec, st = [], [], [], [], []
        for hh, ln in enumerate(lanes):
            lb, _ = _lower_bound(low_ref.at[:, ln])
            _, q, _, _, k, e_qa, e_ka, e_b, e_ko, dec_h = _hg_gates(q_ref[:, ln], f_ref[:, ln], lb, pos, tb)
            qh = (q * e_qa).astype(BF16)
            kh = (k * e_ka).astype(BF16)
            qi.append((q * e_b).astype(BF16))
            ko.append((k * e_ko).astype(BF16))
            vb.append(i_ref[:, ln].astype(BF16))
            dec.append(dec_h)
            st.append(s_scr[hh])
            for s in range(tb // sb):
                sl = slice(s * sb, (s + 1) * sb)
                p = jnp.where(mask, _mm_nt(qh[sl], kh[sl]), 0.0)
                o_ref[sl, ln] = _mm(p, vb[hh][sl])
        for c in range(nc):
            sl = slice(c * CHUNK, (c + 1) * CHUNK)
            for hh, ln in enumerate(lanes):
                st_ref[hh, c] = st[hh]
                o_ref[sl, ln] = o_ref[sl, ln] + _mm_nt(qi[hh][sl], st[hh])
                st[hh] = dec[hh][c * CHUNK:c * CHUNK + 1, :] * st[hh] + _mm_tn(vb[hh][sl], ko[hh][sl])
        for hh, ln in enumerate(lanes):
            s_scr[hh] = st[hh]
            o = o_ref[:, ln]
            r = lax.rsqrt(jnp.mean(o * o, axis=-1, keepdims=True) + EPS)
            gr = g_ref[:, ln]
            og = (o * r * gn_ref[...]) * (gr * _sigmoid(gr))
            og_ref[:, ln] = og.astype(BF16)
            ogt_ref[ln, :] = og.T.astype(BF16)

    col = lambda p: pl.BlockSpec((tb, wid), lambda h, t: (t, p * HEAD_GROUPS + h))
    hcol = pl.BlockSpec((tb, wid), lambda h, t: (t, h))
    return pl.pallas_call(
        body, name="hg_fwd", grid=(HEAD_GROUPS, nb),
        in_specs=[col(0), col(1), col(2), col(3), pl.BlockSpec((2, wid), lambda h, t: (0, h)),
                  pl.BlockSpec((1, HEAD_DIM), lambda h, t: (0, 0))],
        out_specs=[hcol, hcol, pl.BlockSpec((wid, tb), lambda h, t: (h, t)),
                   pl.BlockSpec((HEADS_PER_STEP, nc, HEAD_DIM, HEAD_DIM), lambda h, t: (h, t, 0, 0))],
        out_shape=[jax.ShapeDtypeStruct((T, HG_WIDTH), F32), jax.ShapeDtypeStruct((T, HG_WIDTH), BF16),
                   jax.ShapeDtypeStruct((HG_WIDTH, T), BF16),
                   jax.ShapeDtypeStruct((N_HEADS, T // CHUNK, HEAD_DIM, HEAD_DIM), F32)],
        scratch_shapes=[pltpu.VMEM((HEADS_PER_STEP, HEAD_DIM, HEAD_DIM), F32)],
        compiler_params=_params(("parallel", "arbitrary")),
    )(hg, hg, hg, hg, low, gn)


def _merge_fwd(og, cvo, gt, x, wa, wb, wo):
    T = x.shape[0]
    tm = min(512, T)

    def body(og_ref, cvo_ref, gt_ref, x_ref, wa_ref, wb_ref, wo_ref, x1_ref, mgt_ref):
        ya = jnp.dot(og_ref[...], _shard_cols(wa_ref), preferred_element_type=F32)
        yb = jnp.dot(cvo_ref[...], _shard_cols(wb_ref), preferred_element_type=F32)
        m = (_sigmoid(gt_ref[:, :D_MODEL].astype(F32)) * ya
             + _sigmoid(gt_ref[:, D_MODEL:].astype(F32)) * yb)
        mgt_ref[...] = m.T.astype(BF16)
        x1_ref[...] = x_ref[...] + jnp.dot(m.astype(BF16), wo_ref[...], preferred_element_type=F32)

    row = lambda n: pl.BlockSpec((tm, n), lambda i: (i, 0))
    return pl.pallas_call(
        body, name="merge_fwd", grid=(T // tm,),
        in_specs=[row(HG_WIDTH), row(CONV_WIDTH), row(2 * D_MODEL), row(D_MODEL),
                  _resident(wa.shape), _resident(wb.shape), _resident(wo.shape)],
        out_specs=[row(D_MODEL), _col(tm, D_MODEL)],
        out_shape=[jax.ShapeDtypeStruct((T, D_MODEL), F32), jax.ShapeDtypeStruct((D_MODEL, T), BF16)],
        compiler_params=_params(("parallel",)),
    )(og, cvo, gt, x, wa, wb, wo)


def _ffn_fwd_loss(x1, g, wg, wu, wd, target, g_fin):
    T = x1.shape[0]
    tm = min(256, T)

    def body(x_ref, g_ref, wg_ref, wu_ref, wd_ref, t_ref, gf_ref,
             ht_ref, gate_ref, up_ref, actt_ref, loss_ref, dgf_ref, dx2_ref):
        @pl.when(pl.program_id(0) == 0)
        def _():
            loss_ref[...] = jnp.zeros_like(loss_ref)
            dgf_ref[...] = jnp.zeros_like(dgf_ref)

        xv = x_ref[...]
        r = lax.rsqrt(jnp.mean(xv * xv, axis=-1, keepdims=True) + EPS)
        hf = xv * r * g_ref[...]
        h = hf.astype(BF16)
        ht_ref[...] = hf.T.astype(BF16)
        gate = _mm_nt(h, wg_ref[...])
        up = _mm_nt(h, wu_ref[...])
        gate_ref[...] = gate.astype(STASH)
        up_ref[...] = up.astype(STASH)
        act = gate * _sigmoid(gate) * up
        actt_ref[...] = act.T.astype(BF16)
        x2 = xv + jnp.dot(act.astype(BF16), wd_ref[...], preferred_element_type=F32)

        gv = gf_ref[...]
        r2 = lax.rsqrt(jnp.mean(x2 * x2, axis=-1, keepdims=True) + EPS)
        xh = x2 * r2
        err = xh * gv - t_ref[...]
        loss_ref[...] += 0.5 * jnp.sum(jnp.mean(err * err, axis=-1, keepdims=True), axis=0, keepdims=True)
        dy = err * (1.0 / D_MODEL)
        dgf_ref[...] += jnp.sum(dy * xh, axis=0, keepdims=True)
        w = dy * gv
        dx2_ref[...] = r2 * (w - xh * jnp.mean(w * xh, axis=-1, keepdims=True))

    row = lambda n: pl.BlockSpec((tm, n), lambda i: (i, 0))
    return pl.pallas_call(
        body, name="ffn_fwd_loss", grid=(T // tm,),
        in_specs=[row(D_MODEL), _full((1, D_MODEL)), _resident(wg.shape), _resident(wu.shape), _resident(wd.shape),
                  row(D_MODEL), _full((1, D_MODEL))],
        out_specs=[_col(tm, D_MODEL), row(D_FF), row(D_FF), _col(tm, D_FF), _full((1, 128)), _full((1, D_MODEL)),
                   row(D_MODEL)],
        out_shape=[jax.ShapeDtypeStruct((D_MODEL, T), BF16), jax.ShapeDtypeStruct((T, D_FF), STASH),
                   jax.ShapeDtypeStruct((T, D_FF), STASH), jax.ShapeDtypeStruct((D_FF, T), BF16),
                   jax.ShapeDtypeStruct((1, 128), F32), jax.ShapeDtypeStruct((1, D_MODEL), F32),
                   jax.ShapeDtypeStruct((T, D_MODEL), F32)],
        compiler_params=_params(("arbitrary",)),
    )(x1, g, wg, wu, wd, target, g_fin)


def _ffn_bwd(dx2, x1, gate, up, g, wg, wu, wd):
    T = x1.shape[0]
    tm = min(256, T)

    def body(dx2_ref, x_ref, gate_ref, up_ref, g_ref, wg_ref, wu_ref, wd_ref, dgate_ref, dup_ref, dx1_ref, dgn_ref):
        @pl.when(pl.program_id(0) == 0)
        def _():
            dgn_ref[...] = jnp.zeros_like(dgn_ref)

        dx2 = dx2_ref[...]
        dact = _mm_nt(dx2, wd_ref[...])
        gate = gate_ref[...].astype(F32)
        s = _sigmoid(gate)
        dgate = (dact * up_ref[...].astype(F32) * (s * (1.0 + gate * (1.0 - s)))).astype(BF16)
        dup = (dact * (gate * s)).astype(BF16)
        dgate_ref[...] = dgate
        dup_ref[...] = dup
        dh = _mm(dgate, wg_ref[...]) + _mm(dup, wu_ref[...])
        xv = x_ref[...]
        r = lax.rsqrt(jnp.mean(xv * xv, axis=-1, keepdims=True) + EPS)
        xh = xv * r
        dgn_ref[...] += jnp.sum(dh * xh, axis=0, keepdims=True)
        w = dh * g_ref[...]
        dx1_ref[...] = dx2 + r * (w - xh * jnp.mean(w * xh, axis=-1, keepdims=True))

    row = lambda n: pl.BlockSpec((tm, n), lambda i: (i, 0))
    return pl.pallas_call(
        body, name="ffn_bwd", grid=(T // tm,),
        in_specs=[row(D_MODEL), row(D_MODEL), row(D_FF), row(D_FF), _full((1, D_MODEL)),
                  _resident(wg.shape), _resident(wu.shape), _resident(wd.shape)],
        out_specs=[row(D_FF), row(D_FF), row(D_MODEL), _full((1, D_MODEL))],
        out_shape=[jax.ShapeDtypeStruct((T, D_FF), BF16), jax.ShapeDtypeStruct((T, D_FF), BF16),
                   jax.ShapeDtypeStruct((T, D_MODEL), F32), jax.ShapeDtypeStruct((1, D_MODEL), F32)],
        compiler_params=_params(("arbitrary",)),
    )(dx2, x1, gate, up, g, wg, wu, wd)


def _merge_bwd(dx1, og, cvo, gt, wa, wb, wo):
    T = dx1.shape[0]
    tm = min(512, T)

    def body(dx_ref, og_ref, cvo_ref, gt_ref, wa_ref, wb_ref, wo_ref, dgt_ref, dya_ref, dyb_ref, dog_ref, dcvo_ref):
        dm = _mm_nt(dx_ref[...], wo_ref[...])
        wa = _shard_cols(wa_ref)
        wb = _shard_cols(wb_ref)
        ya = jnp.dot(og_ref[...], wa, preferred_element_type=F32)
        yb = jnp.dot(cvo_ref[...], wb, preferred_element_type=F32)
        sa = _sigmoid(gt_ref[:, :D_MODEL].astype(F32))
        sb = _sigmoid(gt_ref[:, D_MODEL:].astype(F32))
        dgt_ref[:, :D_MODEL] = (dm * ya * (sa * (1.0 - sa))).astype(BF16)
        dgt_ref[:, D_MODEL:] = (dm * yb * (sb * (1.0 - sb))).astype(BF16)
        dya = (dm * sa).astype(BF16)
        dyb = (dm * sb).astype(BF16)
        dya_ref[...] = dya
        dyb_ref[...] = dyb
        dog_ref[...] = _mm_nt(dya, wa)
        dcvo_ref[...] = _mm_nt(dyb, wb)

    row = lambda n: pl.BlockSpec((tm, n), lambda i: (i, 0))
    return pl.pallas_call(
        body, name="merge_bwd", grid=(T // tm,),
        in_specs=[row(D_MODEL), row(HG_WIDTH), row(CONV_WIDTH), row(2 * D_MODEL),
                  _resident(wa.shape), _resident(wb.shape), _resident(wo.shape)],
        out_specs=[row(2 * D_MODEL), row(D_MODEL), row(D_MODEL), row(HG_WIDTH), row(CONV_WIDTH)],
        out_shape=[jax.ShapeDtypeStruct((T, 2 * D_MODEL), BF16), jax.ShapeDtypeStruct((T, D_MODEL), BF16),
                   jax.ShapeDtypeStruct((T, D_MODEL), BF16), jax.ShapeDtypeStruct((T, HG_WIDTH), F32),
                   jax.ShapeDtypeStruct((T, CONV_WIDTH), F32)],
        compiler_params=_params(("parallel",)),
    )(dx1, og, cvo, gt, wa, wb, wo)


def _conv_bwd(dcvo, cv, conv_w, after=()):
    T = cv.shape[0]
    nj = CONV_WIDTH // 128

    def body(do_ref, c_ref, b_ref, x_ref, w_ref, dc_ref, db_ref, dx_ref, dw_ref):
        row = lax.broadcasted_iota(jnp.int32, (T, 128), 0)
        c = c_ref[...].astype(F32)
        xb = x_ref[...].astype(F32)
        do = do_ref[...]
        u = c * xb
        u1 = jnp.where(row >= 1, pltpu.roll(u, 1, 0), 0.0)
        u2 = jnp.where(row >= 2, pltpu.roll(u, 2, 0), 0.0)
        w0, w1, w2 = w_ref[0:1, :], w_ref[1:2, :], w_ref[2:3, :]
        y = w0 * u2 + w1 * u1 + w2 * u
        db_ref[...] = (do * y).astype(BF16)
        dy = do * b_ref[...].astype(F32)
        dw_ref[0:1, :] = jnp.sum(dy * u2, axis=0, keepdims=True)
        dw_ref[1:2, :] = jnp.sum(dy * u1, axis=0, keepdims=True)
        dw_ref[2:3, :] = jnp.sum(dy * u, axis=0, keepdims=True)
        dy1 = jnp.where(row < T - 1, pltpu.roll(dy, T - 1, 0), 0.0)
        dy2 = jnp.where(row < T - 2, pltpu.roll(dy, T - 2, 0), 0.0)
        du = w2 * dy + w1 * dy1 + w0 * dy2
        dc_ref[...] = (du * xb).astype(BF16)
        dx_ref[...] = (du * c).astype(BF16)

    col = lambda p: pl.BlockSpec((T, 128), lambda j: (0, p * nj + j))
    one = pl.BlockSpec((T, 128), lambda j: (0, j))
    wspec = pl.BlockSpec((3, 128), lambda j: (0, j))
    out = jax.ShapeDtypeStruct((T, CONV_WIDTH), BF16)
    return pl.pallas_call(
        _drop_operands(body, 5, len(after)), name="conv_bwd", grid=(nj,),
        in_specs=[one, col(0), col(1), col(2), wspec] + [HBM_SPEC] * len(after),
        out_specs=[one, one, one, wspec],
        out_shape=[out, out, out, jax.ShapeDtypeStruct((3, CONV_WIDTH), F32)],
        compiler_params=_params(("parallel",)),
    )(dcvo, cv, cv, cv, conv_w, *after)


def _drop_operands(body, first, count):
    def wrapped(*refs):
        return body(*refs[:first], *refs[first + count:])
    return wrapped


def _hg_bwd(dog, hg, o, st, low, gn, after=()):
    T = hg.shape[0]
    tb = min(512, T)
    sb = min(256, tb)
    nb = T // tb
    nc = tb // CHUNK
    wid = HEADS_PER_STEP * HEAD_DIM

    def body(q_ref, f_ref, i_ref, g_ref, low_ref, gn_ref, o_ref, dog_ref, st_ref,
             dq_ref, df_ref, di_ref, dg_ref, dlow_ref, dgn_ref,
             ds_scr, dqi_scr, dko_scr, dv_scr, dd_scr, dqh_scr, dkh_scr):
        h = pl.program_id(0)
        t = pl.program_id(1)

        @pl.when(t == 0)
        def _():
            ds_scr[...] = jnp.zeros_like(ds_scr)
            dlow_ref[...] = jnp.zeros_like(dlow_ref)

        @pl.when((t == 0) & (h == 0))
        def _():
            dgn_ref[...] = jnp.zeros_like(dgn_ref)

        pos = _chunk_pos((tb, HEAD_DIM))
        mask = _intra_mask(sb)
        gnv = gn_ref[...]
        lanes = [slice(hh * HEAD_DIM, (hh + 1) * HEAD_DIM) for hh in range(HEADS_PER_STEP)]
        heads = []
        for hh, ln in enumerate(lanes):
            lb, lb1 = _lower_bound(low_ref.at[:, ln])
            qr = q_ref[:, ln]
            sq, q, sg, f, k, e_qa, e_ka, e_b, e_ko, dec = _hg_gates(qr, f_ref[:, ln], lb, pos, tb)

            gr = g_ref[:, ln]
            o = o_ref[:, ln]
            dog_v = dog_ref[:, ln]
            sgr = _sigmoid(gr)
            r = lax.rsqrt(jnp.mean(o * o, axis=-1, keepdims=True) + EPS)
            oh = o * r
            dg_ref[:, ln] = (dog_v * (oh * gnv) * (sgr * (1.0 + gr * (1.0 - sgr)))).astype(BF16)
            don = dog_v * (gr * sgr)
            dgn_ref[...] += jnp.sum(don * oh, axis=0, keepdims=True)
            w = don * gnv
            do = (r * (w - oh * jnp.mean(w * oh, axis=-1, keepdims=True))).astype(BF16)

            qh = (q * e_qa).astype(BF16)
            kh = (k * e_ka).astype(BF16)
            qi = (q * e_b).astype(BF16)
            ko = (k * e_ko).astype(BF16)
            vb = i_ref[:, ln].astype(BF16)

            for s in range(tb // sb):
                sl = slice(s * sb, (s + 1) * sb)
                p = jnp.where(mask, _mm_nt(qh[sl], kh[sl]), 0.0).astype(BF16)
                dp = jnp.where(mask, _mm_nt(do[sl], vb[sl]), 0.0).astype(BF16)
                dv_scr[sl, ln] = _mm_tn(p, do[sl])
                dqh_scr[sl, ln] = _mm(dp, kh[sl])
                dkh_scr[sl, ln] = _mm_tn(dp, qh[sl])
            heads.append(dict(lb=lb, lb1=lb1, qr=qr, sq=sq, q=q, sg=sg, f=f, k=k, e_qa=e_qa, e_ka=e_ka, e_b=e_b,
                              e_ko=e_ko, dec=dec, do=do, qi=qi, ko=ko, vb=vb, ds=ds_scr[hh]))

        for c in reversed(range(nc)):
            sl = slice(c * CHUNK, (c + 1) * CHUNK)
            for hh, ln in enumerate(lanes):
                hd = heads[hh]
                ds = hd["ds"]
                st_c = st_ref[hh, c]
                dqi_scr[sl, ln] = _mm(hd["do"][sl], st_c)
                dko_scr[sl, ln] = _mm(hd["vb"][sl], ds)
                dv_scr[sl, ln] = dv_scr[sl, ln] + _mm_nt(hd["ko"][sl], ds)
                dd_scr[sl, ln] = jnp.broadcast_to(jnp.sum(ds * st_c, axis=0, keepdims=True), (CHUNK, HEAD_DIM))
                hd["ds"] = hd["dec"][c * CHUNK:c * CHUNK + 1, :] * ds + _mm_tn(hd["do"][sl], hd["qi"][sl])

        for hh, ln in enumerate(lanes):
            hd = heads[hh]
            ds_scr[hh] = hd["ds"]
            q, k, lb = hd["q"], hd["k"], hd["lb"]
            dko_e = dko_scr[:, ln] * hd["e_ko"]
            dq = dqh_scr[:, ln] * hd["e_qa"] + dqi_scr[:, ln] * hd["e_b"]
            dk = dkh_scr[:, ln] * hd["e_ka"] + dko_e
            kd3 = (k * dko_e).reshape(nc, CHUNK, HEAD_DIM)
            last = jnp.broadcast_to(jnp.sum(kd3, axis=1, keepdims=True), kd3.shape).reshape(tb, HEAD_DIM)
            db = q * dq - k * dk + jnp.where(pos == CHUNK - 1, hd["dec"] * dd_scr[:, ln] + last, 0.0)
            dlg = _chunk_rev_cumsum(db, pos)
            dfv = dlg / hd["f"] - dk
            s_low = jnp.sum(dfv * (1.0 - hd["sg"]), axis=0, keepdims=True)
            dlow_ref[0:1, ln] += s_low * lb * (1.0 - lb)
            dlow_ref[1:2, ln] += -s_low * lb * hd["lb1"]
            df_ref[:, ln] = (dfv * (1.0 - lb) * hd["sg"] * (1.0 - hd["sg"])).astype(BF16)
            dq_ref[:, ln] = (dq * Q_SCALE * (hd["sq"] * (1.0 + hd["qr"] * (1.0 - hd["sq"])))).astype(BF16)
            di_ref[:, ln] = dv_scr[:, ln].astype(BF16)

    rt = lambda t: nb - 1 - t
    col = lambda p: pl.BlockSpec((tb, wid), lambda h, t: (rt(t), p * HEAD_GROUPS + h))
    hcol = pl.BlockSpec((tb, wid), lambda h, t: (rt(t), h))
    piece = jax.ShapeDtypeStruct((T, HG_WIDTH), BF16)
    tile = pltpu.VMEM((tb, wid), F32)
    return pl.pallas_call(
        _drop_operands(body, 9, len(after)), name="hg_bwd", grid=(HEAD_GROUPS, nb),
        in_specs=[col(0), col(1), col(2), col(3), pl.BlockSpec((2, wid), lambda h, t: (0, h)),
                  pl.BlockSpec((1, HEAD_DIM), lambda h, t: (0, 0)), hcol, hcol,
                  pl.BlockSpec((HEADS_PER_STEP, nc, HEAD_DIM, HEAD_DIM), lambda h, t: (h, rt(t), 0, 0))]
                 + [HBM_SPEC] * len(after),
        out_specs=[hcol, hcol, hcol, hcol, pl.BlockSpec((2, wid), lambda h, t: (0, h)),
                   pl.BlockSpec((1, HEAD_DIM), lambda h, t: (0, 0))],
        out_shape=[piece, piece, piece, piece, jax.ShapeDtypeStruct((2, HG_WIDTH), F32),
                   jax.ShapeDtypeStruct((1, HEAD_DIM), F32)],
        scratch_shapes=[pltpu.VMEM((HEADS_PER_STEP, HEAD_DIM, HEAD_DIM), F32), tile, tile, tile, tile, tile, tile],
        compiler_params=_params(("arbitrary", "arbitrary")),
    )(hg, hg, hg, hg, low, gn, o, dog, st, *after)


def _in_bwd(dparts, w_in, x, dx1, g, after=()):
    T = x.shape[0]
    tm = min(512, T)
    widths = [p.shape[1] for p in dparts]
    offs = [sum(widths[:i]) for i in range(len(widths))]
    n = len(dparts)

    def body(*refs):
        d_refs = refs[:n]
        w_ref, x_ref, dx1_ref, g_ref, dx_ref, dgn_ref = refs[n:]

        @pl.when(pl.program_id(0) == 0)
        def _():
            dgn_ref[...] = jnp.zeros_like(dgn_ref)

        dh = None
        for d_ref, off, wd in zip(d_refs, offs, widths):
            part = _mm(d_ref[...], w_ref[off:off + wd, :])
            dh = part if dh is None else dh + part
        xv = x_ref[...]
        r = lax.rsqrt(jnp.mean(xv * xv, axis=-1, keepdims=True) + EPS)
        xh = xv * r
        dgn_ref[...] += jnp.sum(dh * xh, axis=0, keepdims=True)
        w = dh * g_ref[...]
        dx_ref[...] = dx1_ref[...] + r * (w - xh * jnp.mean(w * xh, axis=-1, keepdims=True))

    row = lambda m: pl.BlockSpec((tm, m), lambda i: (i, 0))
    return pl.pallas_call(
        _drop_operands(body, n + 4, len(after)), name="in_bwd", grid=(T // tm,),
        in_specs=[row(wd) for wd in widths] + [_resident(w_in.shape), row(D_MODEL), row(D_MODEL), _full((1, D_MODEL))]
                 + [HBM_SPEC] * len(after),
        out_specs=[row(D_MODEL), _full((1, D_MODEL))],
        out_shape=[jax.ShapeDtypeStruct((T, D_MODEL), F32), jax.ShapeDtypeStruct((1, D_MODEL), F32)],
        compiler_params=_params(("arbitrary",)),
    )(*dparts, w_in, x, dx1, g, *after)


def _wgrad(name, at, b, tn, transposed=False, tk=2048, after=()):
    M, T = at.shape
    N = b.shape[1]
    tk = min(tk, T)
    nk = T // tk

    def body(a_ref, b_ref, o_ref, acc):
        k = pl.program_id(1)
        part = _mm(a_ref[...], b_ref[...])

        @pl.when(k == 0)
        def _():
            acc[...] = part

        @pl.when(k != 0)
        def _():
            acc[...] += part

        @pl.when(k == nk - 1)
        def _():
            o_ref[...] = (acc[...].T if transposed else acc[...]).astype(BF16)

    if transposed:
        out_spec, out_shape = pl.BlockSpec((tn, M), lambda j, k: (j, 0)), (N, M)
    else:
        out_spec, out_shape = pl.BlockSpec((M, tn), lambda j, k: (0, j)), (M, N)
    return pl.pallas_call(
        _drop_operands(body, 2, len(after)), name=name, grid=(N // tn, nk),
        in_specs=[pl.BlockSpec((M, tk), lambda j, k: (0, k)), pl.BlockSpec((tk, tn), lambda j, k: (k, j))]
                 + [HBM_SPEC] * len(after),
        out_specs=out_spec, out_shape=jax.ShapeDtypeStruct(out_shape, BF16),
        scratch_shapes=[pltpu.VMEM((M, tn), F32)],
        compiler_params=_params(("parallel", "arbitrary")),
    )(at, b, *after)


def _wgrad_branches(ogt, dya, cvot, dyb):
    M, T = ogt.shape
    N = dya.shape[1]
    tk = min(1024, T)
    nk = T // tk
    c = N // N_DEV

    def body(at_ref, da_ref, bt_ref, db_ref, oa_ref, ob_ref, acc_a, acc_b):
        k = pl.program_id(0)

        @pl.when(k == 0)
        def _():
            acc_a[...] = jnp.zeros_like(acc_a)
            acc_b[...] = jnp.zeros_like(acc_b)

        acc_a[...] += _mm(at_ref[...], da_ref[...])
        acc_b[...] += _mm(bt_ref[...], db_ref[...])

        @pl.when(k == nk - 1)
        def _():
            for s in range(N_DEV):
                oa_ref[s] = acc_a[:, s * c:(s + 1) * c].astype(BF16)
                ob_ref[s] = acc_b[:, s * c:(s + 1) * c].astype(BF16)

    lhs = pl.BlockSpec((M, tk), lambda k: (0, k))
    rhs = pl.BlockSpec((tk, N), lambda k: (k, 0))
    out = jax.ShapeDtypeStruct((N_DEV, M, c), BF16)
    return pl.pallas_call(
        body, name="wgrad_branches", grid=(nk,),
        in_specs=[lhs, rhs, lhs, rhs], out_specs=[_full((N_DEV, M, c))] * 2, out_shape=[out, out],
        scratch_shapes=[pltpu.VMEM((M, N), F32)] * 2,
        compiler_params=_params(("arbitrary",)),
    )(ogt, dya, cvot, dyb)


def _wgrad_in(ht, dparts, after=()):
    M, T = ht.shape
    tn = 512
    tk = min(2048, T)
    nk = T // tk
    nblk = [p.shape[1] // tn for p in dparts]
    start = [sum(nblk[:i]) for i in range(len(nblk))]
    n = len(dparts)

    def body(a_ref, *refs):
        d_refs, o_ref, acc = refs[:n], refs[n], refs[n + 1]
        j = pl.program_id(0)
        k = pl.program_id(1)

        @pl.when(k == 0)
        def _():
            acc[...] = jnp.zeros_like(acc)

        for d_ref, s, nb in zip(d_refs, start, nblk):
            @pl.when((j >= s) & (j < s + nb))
            def _():
                acc[...] += _mm(a_ref[...], d_ref[...])

        @pl.when(k == nk - 1)
        def _():
            o_ref[...] = acc[...].T.astype(BF16)

    def piece_spec(s, nb):
        def index(j, k):
            inside = (j >= s) & (j < s + nb)
            return jnp.where(inside, k, 0), jnp.clip(j - s, 0, nb - 1)
        return pl.BlockSpec((tk, tn), index)

    return pl.pallas_call(
        _drop_operands(body, 1 + n, len(after)), name="wgrad_in", grid=(sum(nblk), nk),
        in_specs=[pl.BlockSpec((M, tk), lambda j, k: (0, k))] + [piece_spec(s, nb) for s, nb in zip(start, nblk)]
                 + [HBM_SPEC] * len(after),
        out_specs=pl.BlockSpec((tn, M), lambda j, k: (j, 0)),
        out_shape=jax.ShapeDtypeStruct((sum(nblk) * tn, M), BF16),
        scratch_shapes=[pltpu.VMEM((M, tn), F32)],
        compiler_params=_params(("parallel", "arbitrary")),
    )(ht, *dparts, *after)


def _adamw_math(w, g, m, v):
    m = ADAM_B1 * m + (1.0 - ADAM_B1) * g
    v = ADAM_B2 * v + (1.0 - ADAM_B2) * (g * g)
    m_hat = m / (1.0 - ADAM_B1 ** ADAM_STEP)
    v_hat = v / (1.0 - ADAM_B2 ** ADAM_STEP)
    delta = -ADAM_LR * (m_hat / (jnp.sqrt(v_hat) + ADAM_EPS) + ADAM_WD * w)
    return delta, m, v


def _adamw_sum(name, w, parts, m, v):
    R, C = w.shape
    tr = _row_tile(R)

    def body(w_ref, p_ref, m_ref, v_ref, g_out, d_out, m_out, v_out):
        g = p_ref[0].astype(F32)
        for k in range(1, 4):
            g = g + p_ref[k].astype(F32)
        g_out[...] = g
        d_out[...], m_out[...], v_out[...] = _adamw_math(w_ref[...], g, m_ref[...], v_ref[...])

    blk = pl.BlockSpec((tr, C), lambda i: (i, 0))
    out = jax.ShapeDtypeStruct((R, C), F32)
    return pl.pallas_call(
        body, name=name, grid=(R // tr,),
        in_specs=[blk, pl.BlockSpec((4, tr, C), lambda i: (0, i, 0)), blk, blk],
        out_specs=[blk, blk, blk, blk], out_shape=[out, out, out, out],
        compiler_params=_params(("parallel",)),
    )(w, parts, m, v)


_SMALL_SLOTS = (("norm_mix_g", 0, 1, 1024), ("norm_ffn_g", 1, 1, 1024), ("norm_final_g", 2, 1, 1024),
                ("lower_bounds", 3, 2, 512), ("hg_norm_g", 5, 1, 128), ("loss", 6, 1, 128), ("conv_w", 8, 3, 512))
_SMALL_PARAMS = tuple(s for s in _SMALL_SLOTS if s[0] != "loss")
CONV_SHARD = CONV_WIDTH // N_DEV


def _small_pack(small):
    def body(*refs):
        out = refs[-1]
        out[...] = jnp.zeros_like(out)
        for ref, (_, row, rows, lanes) in zip(refs[:-1], _SMALL_SLOTS):
            out[row:row + rows, 0:lanes] = ref[...]

    vmem = pl.BlockSpec(memory_space=pltpu.VMEM)
    return pl.pallas_call(
        body, name="small_pack", in_specs=[vmem] * len(_SMALL_SLOTS), out_specs=vmem,
        out_shape=jax.ShapeDtypeStruct((SMALL_ROWS, 1024), F32),
    )(*[small[name] for name, _, _, _ in _SMALL_SLOTS])


def _small_update(gathered, dev, w, m, v):
    n = len(_SMALL_PARAMS)

    def body(dev_ref, g_ref, *refs):
        w_refs, m_refs, v_refs = refs[:n], refs[n:2 * n], refs[2 * n:3 * n]
        loss_ref, out_refs, sum_scr = refs[3 * n], refs[3 * n + 1:-1], refs[-1]
        total = g_ref[0]
        for k in range(1, N_DEV):
            total = total + g_ref[k]
        sum_scr[...] = total
        loss_ref[...] = sum_scr[6:7, 0:128]
        for p, (name, row, rows, lanes) in enumerate(_SMALL_PARAMS):
            if name == "conv_w":
                g = sum_scr[row:row + rows, 0:CONV_SHARD]
                for s in range(1, N_DEV):
                    g = jnp.where(dev_ref[0] == s, sum_scr[row:row + rows, s * CONV_SHARD:(s + 1) * CONV_SHARD], g)
            else:
                g = sum_scr[row:row + rows, 0:lanes]
            delta, m_new, v_new = _adamw_math(w_refs[p][...], g, m_refs[p][...], v_refs[p][...])
            out_refs[4 * p][...] = g
            out_refs[4 * p + 1][...] = delta
            out_refs[4 * p + 2][...] = m_new
            out_refs[4 * p + 3][...] = v_new

    vmem = pl.BlockSpec(memory_space=pltpu.VMEM)
    outs = [jax.ShapeDtypeStruct((1, 128), F32)]
    for a in w:
        outs += [jax.ShapeDtypeStruct(a.shape, F32)] * 4
    return pl.pallas_call(
        body, name="small_update",
        in_specs=[pl.BlockSpec(memory_space=pltpu.SMEM)] + [vmem] * (1 + 3 * n), out_specs=[vmem] * len(outs),
        out_shape=outs, scratch_shapes=[pltpu.VMEM((SMALL_ROWS, 1024), F32)],
    )(dev, gathered, *w, *m, *v)


def _row_tile(rows):
    for parts in (4, 2):
        if rows % (16 * parts) == 0:
            return rows // parts
    return rows


def _pair_sum(name, by_owner, got, core, after=()):
    n = len(got)

    def body(core_ref, *refs):
        for a_ref, b_ref, o_ref in zip(refs[:n], refs[n:2 * n], refs[2 * n:]):
            o_ref[...] = (a_ref[...].astype(F32) + b_ref[...].astype(F32)).astype(BF16)

    def blk(g):
        return pl.BlockSpec((None,) + g.shape[1:], lambda k, core_ref: (k, 0, 0))

    def mine(g):
        return pl.BlockSpec((None,) + g.shape[1:], lambda k, core_ref: (2 * k + core_ref[0], 0, 0))

    return pl.pallas_call(
        _drop_operands(body, 1 + 2 * n, len(after)), name=name,
        grid_spec=pltpu.PrefetchScalarGridSpec(
            num_scalar_prefetch=1, grid=(4,),
            in_specs=[mine(g) for g in got] + [blk(g) for g in got] + [HBM_SPEC] * len(after),
            out_specs=[blk(g) for g in got]),
        out_shape=[jax.ShapeDtypeStruct(g.shape, BF16) for g in got],
        compiler_params=_params(("parallel",)),
    )(core, *by_owner, *got, *after)


MESH = pl.DeviceIdType.MESH
HBM_SPEC = pl.BlockSpec(memory_space=pl.ANY)


def _handshake(peers):
    barrier = pltpu.get_barrier_semaphore()
    for peer in peers:
        pl.semaphore_signal(barrier, inc=1, device_id=peer, device_id_type=MESH)
    pl.semaphore_wait(barrier, len(peers))


def _comm_call(body, name, operands, out_shape, scratch, collective_id):
    if collective_id is None:
        return pl.pallas_call(body, name=name, in_specs=[HBM_SPEC] * len(operands), out_specs=[HBM_SPEC] * len(out_shape),
                              out_shape=out_shape, scratch_shapes=scratch)(*operands)
    return pl.kernel(body, out_type=out_shape, mesh=plsc.ScalarSubcoreMesh(axis_name="sequencer", num_cores=1),
                     scratch_types=scratch, name=name,
                     compiler_params=pltpu.CompilerParams(collective_id=collective_id))(*operands)


def _all_gather(name, blocks, collective_id=None, after=()):
    n = len(blocks)
    na = len(after)

    def body(*refs):
        x_refs, out_refs = refs[:n], refs[n + na:2 * n + na]
        send_sems, recv_sems, local_sems = refs[2 * n + na:]
        x, y, c = lax.axis_index("x"), lax.axis_index("y"), lax.axis_index("c")
        me, sibling = (x, y, c), (x, y, 1 - c)
        chips = [(1 - x, y), (x, 1 - y), (1 - x, 1 - y)]
        if collective_id is not None:
            _handshake([sibling] + [(*chip, c) for chip in chips])

        def slot(i, px, py, pc):
            return out_refs[i].at[4 * px + 2 * py + pc]

        def copy(i, k, blk, to, src=None):
            return pltpu.make_async_remote_copy(
                src_ref=slot(i, *blk) if src is None else src, dst_ref=slot(i, *blk),
                send_sem=send_sems.at[7 * i + k], recv_sem=recv_sems.at[7 * i + k], device_id=to, device_id_type=MESH)

        mine = [pltpu.make_async_copy(x_refs[i], slot(i, *me), local_sems.at[i]) for i in range(n)]
        for cp in mine:
            cp.start()
        first = []
        for i in range(n):
            first.append(copy(i, 0, me, sibling, src=x_refs[i]))
            first += [copy(i, 1 + j, me, (*chip, c), src=x_refs[i]) for j, chip in enumerate(chips)]
        for cp in first:
            cp.start()
        passed = []
        for i in range(n):
            for j, chip in enumerate(chips):
                copy(i, 1 + j, (*chip, c), me).wait_recv()
                passed.append(copy(i, 4 + j, (*chip, c), sibling))
                passed[-1].start()
        for i in range(n):
            copy(i, 0, sibling, me).wait_recv()
            for j, chip in enumerate(chips):
                copy(i, 4 + j, (*chip, 1 - c), me).wait_recv()
        for cp in first + passed:
            cp.wait_send()
        for cp in mine:
            cp.wait()

    return _comm_call(
        body, name, list(blocks) + list(after), [jax.ShapeDtypeStruct((N_DEV,) + b.shape, b.dtype) for b in blocks],
        [pltpu.SemaphoreType.DMA((7 * n,)), pltpu.SemaphoreType.DMA((7 * n,)), pltpu.SemaphoreType.DMA((n,))],
        collective_id)


def _sibling_swap(name, by_owner, collective_id=None, after=()):
    n = len(by_owner)
    na = len(after)

    def body(*refs):
        x_refs, out_refs = refs[:n], refs[n + na:2 * n + na]
        send_sems, recv_sems = refs[2 * n + na:]
        x, y, c = lax.axis_index("x"), lax.axis_index("y"), lax.axis_index("c")
        if collective_id is not None:
            _handshake([(x, y, 1 - c)])
        copies = []
        for i in range(n):
            for k in range(4):
                copies.append(pltpu.make_async_remote_copy(
                    src_ref=x_refs[i].at[2 * k + 1 - c], dst_ref=out_refs[i].at[k],
                    send_sem=send_sems.at[4 * i + k], recv_sem=recv_sems.at[4 * i + k],
                    device_id=(x, y, 1 - c), device_id_type=MESH))
        for cp in copies:
            cp.start()
        for cp in copies:
            cp.wait()

    return _comm_call(
        body, name, list(by_owner) + list(after),
        [jax.ShapeDtypeStruct((4,) + b.shape[1:], b.dtype) for b in by_owner],
        [pltpu.SemaphoreType.DMA((4 * n,)), pltpu.SemaphoreType.DMA((4 * n,))], collective_id)


def _chip_exchange(name, sums, collective_id=None, after=()):
    n = len(sums)
    na = len(after)

    def body(*refs):
        x_refs, out_refs = refs[:n], refs[n + na:2 * n + na]
        send_sems, recv_sems, local_sems = refs[2 * n + na:]
        x, y, c = lax.axis_index("x"), lax.axis_index("y"), lax.axis_index("c")
        chips = [(1 - x, y), (x, 1 - y), (1 - x, 1 - y)]
        my_chip = 2 * x + y
        if collective_id is not None:
            _handshake([(cx, cy, c) for cx, cy in chips])
        mine = [pltpu.make_async_copy(x_refs[i].at[my_chip], out_refs[i].at[my_chip], local_sems.at[i])
                for i in range(n)]
        for cp in mine:
            cp.start()
        sends = []
        for i in range(n):
            for j, (cx, cy) in enumerate(chips):
                sends.append(pltpu.make_async_remote_copy(
                    src_ref=x_refs[i].at[2 * cx + cy], dst_ref=out_refs[i].at[my_chip],
                    send_sem=send_sems.at[3 * i + j], recv_sem=recv_sems.at[3 * i + j],
                    device_id=(cx, cy, c), device_id_type=MESH))
        for cp in sends:
            cp.start()
        for i in range(n):
            for j, (cx, cy) in enumerate(chips):
                pltpu.make_async_remote_copy(
                    src_ref=x_refs[i].at[my_chip], dst_ref=out_refs[i].at[2 * cx + cy],
                    send_sem=send_sems.at[3 * i + j], recv_sem=recv_sems.at[3 * i + j],
                    device_id=(cx, cy, c), device_id_type=MESH).wait_recv()
        for cp in sends:
            cp.wait_send()
        for cp in mine:
            cp.wait()

    return _comm_call(
        body, name, list(sums) + list(after), [jax.ShapeDtypeStruct(s.shape, s.dtype) for s in sums],
        [pltpu.SemaphoreType.DMA((3 * n,)), pltpu.SemaphoreType.DMA((3 * n,)), pltpu.SemaphoreType.DMA((n,))],
        collective_id)


def _cast_shards(shards):
    n = len(shards)

    def body(*refs):
        for i in range(n):
            refs[n + i][...] = refs[i][...].astype(BF16)

    vmem = pl.BlockSpec(memory_space=pltpu.VMEM)
    return pl.pallas_call(
        body, name="cast_shards", in_specs=[vmem] * n, out_specs=[vmem] * n,
        out_shape=[jax.ShapeDtypeStruct(s.shape, BF16) for s in shards],
        compiler_params=pltpu.CompilerParams(vmem_limit_bytes=VMEM_LIMIT_V7X),
    )(*shards)


BIG = ("w_in", "w_branch_a", "w_branch_b", "w_out", "w_ffn_gate", "w_ffn_up", "w_ffn_down")


def _local_step(x, target, gains, low, conv_w, wg8, reduce):
    g_mix, g_hg, g_ffn, g_fin = gains
    w_in = wg8["w_in"].reshape(N_IN, D_MODEL)
    wg = wg8["w_ffn_gate"].reshape(D_FF, D_MODEL)
    wu = wg8["w_ffn_up"].reshape(D_FF, D_MODEL)
    wa, wb = wg8["w_branch_a"], wg8["w_branch_b"]
    wo = wg8["w_out"].reshape(D_MODEL, D_MODEL)
    wd = wg8["w_ffn_down"].reshape(D_FF, D_MODEL)

    ht, hg, cv, gt, cvo, cvot = _fwd_in(x, g_mix, w_in, conv_w)
    o, og, ogt, st = _hg_fwd(hg, low, g_hg)
    x1, mgt = _merge_fwd(og, cvo, gt, x, wa, wb, wo)
    h2t, gate, up, actt, loss, d_gfin, dx2 = _ffn_fwd_loss(x1, g_ffn, wg, wu, wd, target, g_fin)

    dgate, dup, dx1, d_gffn = _ffn_bwd(dx2, x1, gate, up, g_ffn, wg, wu, wd)
    ffn = dict(
        w_ffn_down=_wgrad("wgrad_ffn_down", actt, dx2, 512).reshape(N_DEV, D_FF // N_DEV, D_MODEL),
        w_ffn_gate=_wgrad("wgrad_ffn_gate", h2t, dgate, 1408, transposed=True).reshape(N_DEV, D_FF // N_DEV, D_MODEL),
        w_ffn_up=_wgrad("wgrad_ffn_up", h2t, dup, 1408, transposed=True).reshape(N_DEV, D_FF // N_DEV, D_MODEL))
    dgt, dya, dyb, dog, dcvo = _merge_bwd(dx1, og, cvo, gt, wa, wb, wo)
    sums_ffn, got_ffn = reduce.begin(ffn, sum_after=[dya])
    parts_ffn, _ = reduce.finish(ffn, sums_ffn)
    grad_a, grad_b = _wgrad_branches(ogt, dya, cvot, dyb)
    out = dict(
        w_out=_wgrad("wgrad_out", mgt, dx1, 512, after=sums_ffn[:1]).reshape(N_DEV, D_MODEL // N_DEV, D_MODEL),
        w_branch_a=grad_a, w_branch_b=grad_b)
    dq, df, di, dg, d_low, d_ghg = _hg_bwd(dog, hg, o, st, low, g_hg, after=list(sums_ffn) + [out["w_out"]])
    sums_out, got_out = reduce.begin(out, after=[parts_ffn[0], dq])
    dc, db, dxb, d_conv = _conv_bwd(dcvo, cv, conv_w, after=[dq])
    parts_out, updated_out = reduce.finish(out, sums_out, after=[dc])
    dparts = [dq, df, di, dg, dc, db, dxb, dgt]
    w_in_grad = dict(w_in=_wgrad_in(ht, dparts, after=sums_out[:1]).reshape(N_DEV, N_IN // N_DEV, D_MODEL))
    sums_in, _ = reduce.begin(w_in_grad, after=parts_out[:1], sum_after=updated_out)
    parts_in, _ = reduce.finish(w_in_grad, sums_in)
    grad_x, d_gmix = _in_bwd(dparts, w_in, x, dx1, g_mix, after=list(parts_out[:1]) + list(sums_in))
    small = dict(norm_mix_g=d_gmix, norm_ffn_g=d_gffn, norm_final_g=d_gfin, lower_bounds=d_low, hg_norm_g=d_ghg,
                 conv_w=d_conv, loss=loss)
    return grad_x, small, parts_in


def _conv_shard_rows(a):
    return jnp.pad(a, ((0, 5), (0, 64)))


def kernel(x, norm_mix_g, w_in, lower_bounds, hg_norm_g, conv_w, w_branch_a, w_branch_b, w_out, norm_ffn_g, w_ffn_gate, w_ffn_up, w_ffn_down, norm_final_g, loss_target, m_norm_mix_g, m_w_in, m_lower_bounds, m_hg_norm_g, m_conv_w, m_w_branch_a, m_w_branch_b, m_w_out, m_norm_ffn_g, m_w_ffn_gate, m_w_ffn_up, m_w_ffn_down, m_norm_final_g, v_norm_mix_g, v_w_in, v_lower_bounds, v_hg_norm_g, v_conv_w, v_w_branch_a, v_w_branch_b, v_w_out, v_norm_ffn_g, v_w_ffn_gate, v_w_ffn_up, v_w_ffn_down, v_norm_final_g):
    cx, cy, cc = lax.axis_index("x"), lax.axis_index("y"), lax.axis_index("c")
    my_dev = 4 * cx + 2 * cy + cc

    def tr(a):
        return a[0].T

    big = dict(w_in=tr(w_in), w_branch_a=w_branch_a[0], w_branch_b=w_branch_b[0], w_out=w_out[0],
               w_ffn_gate=tr(w_ffn_gate), w_ffn_up=tr(w_ffn_up), w_ffn_down=w_ffn_down[0])
    big_m = dict(w_in=tr(m_w_in), w_branch_a=m_w_branch_a[0], w_branch_b=m_w_branch_b[0], w_out=m_w_out[0],
                 w_ffn_gate=tr(m_w_ffn_gate), w_ffn_up=tr(m_w_ffn_up), w_ffn_down=m_w_ffn_down[0])
    big_v = dict(w_in=tr(v_w_in), w_branch_a=v_w_branch_a[0], w_branch_b=v_w_branch_b[0], w_out=v_w_out[0],
                 w_ffn_gate=tr(v_w_ffn_gate), w_ffn_up=tr(v_w_ffn_up), w_ffn_down=v_w_ffn_down[0])
    transposed = ("w_in", "w_ffn_gate", "w_ffn_up")

    shards = dict(zip(BIG, _cast_shards([big[n] for n in BIG])))
    first = _all_gather("gather_w_in", [shards["w_in"], _conv_shard_rows(conv_w[0])])
    ids = iter(range(1, 16))
    mid = _all_gather("gather_mid", [shards[n] for n in BIG[1:4]], collective_id=next(ids), after=first[1:])
    ffn = _all_gather("gather_ffn", [shards[n] for n in BIG[4:]], collective_id=next(ids), after=first[1:])
    wg8 = dict(zip(BIG, [first[0]] + list(mid) + list(ffn)))
    conv_full = first[1][:, :3, :64].transpose(1, 0, 2).reshape(3, CONV_WIDTH)

    core = cc.reshape(1).astype(jnp.int32)
    outs = {}

    class Reduce:
        @staticmethod
        def begin(grads, after=(), sum_after=()):
            names = list(grads)
            by_owner = [grads[n] for n in names]
            got = _sibling_swap("sibling_swap_" + names[0], by_owner, collective_id=next(ids), after=after)
            sums = _pair_sum("pair_sum_" + names[0], by_owner, got, core, after=sum_after)
            return sums, got

        @staticmethod
        def finish(grads, chip_sums, after=()):
            names = list(grads)
            parts = _chip_exchange("chip_exchange_" + names[0], chip_sums, collective_id=next(ids), after=after)
            for n, p in zip(names, parts):
                outs[n] = _adamw_sum("adamw_" + n, big[n], p, big_m[n], big_v[n])
            return parts, [outs[n][1] for n in names]

    gains = (norm_mix_g, hg_norm_g, norm_ffn_g, norm_final_g.reshape(1, D_MODEL))
    grad_x, small, last = _local_step(x[0], loss_target[0], gains, lower_bounds, conv_full, wg8, Reduce)

    small_all = _all_gather("gather_small", [_small_pack(small)], collective_id=next(ids), after=last[:1])

    def small_state(a):
        return [a[0], a[1], a[2].reshape(1, D_MODEL), a[3], a[4], a[5][0]]

    upd = _small_update(
        small_all[0], my_dev.reshape(1).astype(jnp.int32),
        small_state((norm_mix_g, norm_ffn_g, norm_final_g, lower_bounds, hg_norm_g, conv_w)),
        small_state((m_norm_mix_g, m_norm_ffn_g, m_norm_final_g, m_lower_bounds, m_hg_norm_g, m_conv_w)),
        small_state((v_norm_mix_g, v_norm_ffn_g, v_norm_final_g, v_lower_bounds, v_hg_norm_g, v_conv_w)))
    loss = upd[0][0, 0]
    small_shape = dict(norm_final_g=(D_MODEL,), conv_w=(1, 3, CONV_SHARD))
    for p, (name, _, _, _) in enumerate(_SMALL_PARAMS):
        outs[name] = [a.reshape(small_shape.get(name, a.shape)) for a in upd[1 + 4 * p:5 + 4 * p]]

    order = ["norm_mix_g", "w_in", "lower_bounds", "hg_norm_g", "conv_w", "w_branch_a", "w_branch_b", "w_out",
             "norm_ffn_g", "w_ffn_gate", "w_ffn_up", "w_ffn_down", "norm_final_g"]
    result = [loss, grad_x[None]]
    for k in range(4):
        for n in order:
            if n in BIG:
                result.append((outs[n][k].T if n in transposed else outs[n][k])[None])
            else:
                result.append(outs[n][k])
    return tuple(result)
```

```python
import jax
import jax.numpy as jnp
from jax import lax
from jax.experimental import pallas as pl
from jax.experimental.pallas import tpu as pltpu
from jax.experimental.pallas import tpu_sc as plsc

F32 = jnp.float32
BF16 = jnp.bfloat16
STASH = jnp.bfloat16

D_MODEL = 1024
HG_WIDTH = 512
HEAD_DIM = 128
N_HEADS = 4
HEADS_PER_STEP = 4
HEAD_GROUPS = N_HEADS // HEADS_PER_STEP
CONV_WIDTH = 512
CONV_K = 3
D_FF = 2816
CHUNK = 32
EPS = 1e-6
Q_SCALE = HEAD_DIM ** -0.5
N_DEV = 8

ADAM_LR = 0.001
ADAM_B1 = 0.9
ADAM_B2 = 0.999
ADAM_EPS = 1e-08
ADAM_WD = 0.01
ADAM_STEP = 10

VMEM_LIMIT_V7X = 56 * 1024 * 1024

SMALL_ROWS = 16


def _params(sem, vmem=VMEM_LIMIT_V7X):
    return pltpu.CompilerParams(dimension_semantics=sem, vmem_limit_bytes=vmem)


def _mm(a, b):
    return jnp.dot(a.astype(BF16), b.astype(BF16), preferred_element_type=F32)


def _mm_nt(a, b):
    return lax.dot_general(a.astype(BF16), b.astype(BF16), (((1,), (1,)), ((), ())), preferred_element_type=F32)


def _mm_tn(a, b):
    return lax.dot_general(a.astype(BF16), b.astype(BF16), (((0,), (0,)), ((), ())), preferred_element_type=F32)


def _sigmoid(x):
    return 0.5 * jnp.tanh(0.5 * x) + 0.5


def _resident(shape):
    nd = len(shape)
    return pl.BlockSpec(shape, lambda *_: (0,) * nd, pipeline_mode=pl.Buffered(1))


def _full(shape):
    nd = len(shape)
    return pl.BlockSpec(shape, lambda *_: (0,) * nd)


def _shard_cols(w_ref):
    return jnp.concatenate([w_ref[s] for s in range(N_DEV)], axis=1)


N_HG = 4 * HG_WIDTH
N_CV = 3 * CONV_WIDTH
N_GT = 2 * D_MODEL
N_IN = N_HG + N_CV + N_GT


def _col(tm, n):
    return pl.BlockSpec((n, tm), lambda i: (0, i))


HALO = 8


def _fwd_in(x, g, w_in_t, conv_w):
    T = x.shape[0]
    tm = min(512, T)

    def body(x_ref, g_ref, w_ref, cw_ref, ht_ref, hg_ref, cv_ref, gt_ref, cvo_ref, cvot_ref, tail_scr):
        @pl.when(pl.program_id(0) == 0)
        def _():
            tail_scr[...] = jnp.zeros_like(tail_scr)

        xv = x_ref[...]
        r = lax.rsqrt(jnp.mean(xv * xv, axis=-1, keepdims=True) + EPS)
        hf = xv * r * g_ref[...]
        h = hf.astype(BF16)
        ht_ref[...] = hf.T.astype(BF16)
        hg_ref[...] = _mm_nt(h, w_ref[:N_HG, :])
        cv = _mm_nt(h, w_ref[N_HG:N_HG + N_CV, :])
        cv_ref[...] = cv.astype(STASH)
        gt_ref[...] = _mm_nt(h, w_ref[N_HG + N_CV:, :]).astype(STASH)

        u = cv[:, :CONV_WIDTH] * cv[:, 2 * CONV_WIDTH:]
        row = lax.broadcasted_iota(jnp.int32, u.shape, 0)
        prev1 = tail_scr[HALO - 1:HALO, :]
        prev2 = tail_scr[HALO - 2:HALO - 1, :]
        u1 = jnp.where(row >= 1, pltpu.roll(u, 1, 0), prev1)
        u2 = jnp.where(row >= 2, pltpu.roll(u, 2, 0), jnp.where(row == 1, prev1, prev2))
        y = cw_ref[0:1, :] * u2 + cw_ref[1:2, :] * u1 + cw_ref[2:3, :] * u
        out = cv[:, CONV_WIDTH:2 * CONV_WIDTH] * y
        cvo_ref[...] = out.astype(BF16)
        cvot_ref[...] = out.T.astype(BF16)
        tail_scr[...] = u[tm - HALO:, :]

    row = lambda n: pl.BlockSpec((tm, n), lambda i: (i, 0))
    return pl.pallas_call(
        body, name="fwd_in", grid=(T // tm,),
        in_specs=[row(D_MODEL), _full((1, D_MODEL)), _resident(w_in_t.shape), _full((CONV_K, CONV_WIDTH))],
        out_specs=[_col(tm, D_MODEL), row(N_HG), row(N_CV), row(N_GT), row(CONV_WIDTH), _col(tm, CONV_WIDTH)],
        out_shape=[jax.ShapeDtypeStruct((D_MODEL, T), BF16), jax.ShapeDtypeStruct((T, N_HG), F32),
                   jax.ShapeDtypeStruct((T, N_CV), STASH), jax.ShapeDtypeStruct((T, N_GT), STASH),
                   jax.ShapeDtypeStruct((T, CONV_WIDTH), BF16), jax.ShapeDtypeStruct((CONV_WIDTH, T), BF16)],
        scratch_shapes=[pltpu.VMEM((HALO, CONV_WIDTH), F32)],
        compiler_params=_params(("arbitrary",)),
    )(x, g, w_in_t, conv_w)


def _chunk_pos(shape):
    return lax.broadcasted_iota(jnp.int32, shape, 0) & (CHUNK - 1)


def _chunk_cumsum(x, pos):
    s = 1
    while s < CHUNK:
        x = x + jnp.where(pos >= s, pltpu.roll(x, s, 0), 0.0)
        s *= 2
    return x


def _chunk_rev_cumsum(x, pos):
    n = x.shape[0]
    s = 1
    while s < CHUNK:
        x = x + jnp.where(pos + s < CHUNK, pltpu.roll(x, n - s, 0), 0.0)
        s *= 2
    return x


def _chunk_bcast(x3, row, tb):
    return jnp.broadcast_to(x3[:, row:row + 1, :], x3.shape).reshape(tb, x3.shape[-1])


def _lower_bound(low_ref):
    l0 = low_ref[0:1, :]
    l1 = low_ref[1:2, :]
    m = jnp.maximum(l0, l1)
    e0 = jnp.exp(l0 - m)
    e1 = jnp.exp(l1 - m)
    return e0 / (e0 + e1), e1 / (e0 + e1)


def _hg_gates(qr, fr, lb, pos, tb):
    sq = _sigmoid(qr)
    q = qr * sq * Q_SCALE
    sg = _sigmoid(fr)
    f = lb + (1.0 - lb) * sg
    k = 1.0 - f
    b = _chunk_cumsum(jnp.log(f), pos)
    b3 = b.reshape(tb // CHUNK, CHUNK, HEAD_DIM)
    anc = _chunk_bcast(b3, CHUNK // 2 - 1, tb)
    blb = _chunk_bcast(b3, CHUNK - 1, tb)
    e_qa = jnp.exp(b - anc)
    e_ka = jnp.exp(anc - b)
    e_b = jnp.exp(b)
    e_ko = jnp.exp(blb - b)
    dec = jnp.exp(blb)
    return sq, q, sg, f, k, e_qa, e_ka, e_b, e_ko, dec


def _intra_mask(sb):
    r = lax.broadcasted_iota(jnp.int32, (sb, sb), 0)
    c = lax.broadcasted_iota(jnp.int32, (sb, sb), 1)
    return ((r // CHUNK) == (c // CHUNK)) & (c <= r)


def _hg_fwd(hg, low, gn):
    T = hg.shape[0]
    tb = min(512, T)
    sb = min(256, tb)
    nb = T // tb
    nc = tb // CHUNK
    wid = HEADS_PER_STEP * HEAD_DIM

    def body(q_ref, f_ref, i_ref, g_ref, low_ref, gn_ref, o_ref, og_ref, ogt_ref, st_ref, s_scr):
        t = pl.program_id(1)

        @pl.when(t == 0)
        def _():
            s_scr[...] = jnp.zeros_like(s_scr)

        pos = _chunk_pos((tb, HEAD_DIM))
        mask = _intra_mask(sb)
        lanes = [slice(hh * HEAD_DIM, (hh + 1) * HEAD_DIM) for hh in range(HEADS_PER_STEP)]
        qi, ko, vb, dec, st = [], [], [], [], []
        for hh, ln in enumerate(lanes):
            lb, _ = _lower_bound(low_ref.at[:, ln])
            _, q, _, _, k, e_qa, e_ka, e_b, e_ko, dec_h = _hg_gates(q_ref[:, ln], f_ref[:, ln], lb, pos, tb)
            qh = (q * e_qa).astype(BF16)
            kh = (k * e_ka).astype(BF16)
            qi.append((q * e_b).astype(BF16))
            ko.append((k * e_ko).astype(BF16))
            vb.append(i_ref[:, ln].astype(BF16))
            dec.append(dec_h)
            st.append(s_scr[hh])
            for s in range(tb // sb):
                sl = slice(s * sb, (s + 1) * sb)
                p = jnp.where(mask, _mm_nt(qh[sl], kh[sl]), 0.0)
                o_ref[sl, ln] = _mm(p, vb[hh][sl])
        for c in range(nc):
            sl = slice(c * CHUNK, (c + 1) * CHUNK)
            for hh, ln in enumerate(lanes):
                st_ref[hh, c] = st[hh]
                o_ref[sl, ln] = o_ref[sl, ln] + _mm_nt(qi[hh][sl], st[hh])
                st[hh] = dec[hh][c * CHUNK:c * CHUNK + 1, :] * st[hh] + _mm_tn(vb[hh][sl], ko[hh][sl])
        for hh, ln in enumerate(lanes):
            s_scr[hh] = st[hh]
            o = o_ref[:, ln]
            r = lax.rsqrt(jnp.mean(o * o, axis=-1, keepdims=True) + EPS)
            gr = g_ref[:, ln]
            og = (o * r * gn_ref[...]) * (gr * _sigmoid(gr))
            og_ref[:, ln] = og.astype(BF16)
            ogt_ref[ln, :] = og.T.astype(BF16)

    col = lambda p: pl.BlockSpec((tb, wid), lambda h, t: (t, p * HEAD_GROUPS + h))
    hcol = pl.BlockSpec((tb, wid), lambda h, t: (t, h))
    return pl.pallas_call(
        body, name="hg_fwd", grid=(HEAD_GROUPS, nb),
        in_specs=[col(0), col(1), col(2), col(3), pl.BlockSpec((2, wid), lambda h, t: (0, h)),
                  pl.BlockSpec((1, HEAD_DIM), lambda h, t: (0, 0))],
        out_specs=[hcol, hcol, pl.BlockSpec((wid, tb), lambda h, t: (h, t)),
                   pl.BlockSpec((HEADS_PER_STEP, nc, HEAD_DIM, HEAD_DIM), lambda h, t: (h, t, 0, 0))],
        out_shape=[jax.ShapeDtypeStruct((T, HG_WIDTH), F32), jax.ShapeDtypeStruct((T, HG_WIDTH), BF16),
                   jax.ShapeDtypeStruct((HG_WIDTH, T), BF16),
                   jax.ShapeDtypeStruct((N_HEADS, T // CHUNK, HEAD_DIM, HEAD_DIM), F32)],
        scratch_shapes=[pltpu.VMEM((HEADS_PER_STEP, HEAD_DIM, HEAD_DIM), F32)],
        compiler_params=_params(("parallel", "arbitrary")),
    )(hg, hg, hg, hg, low, gn)


def _merge_fwd(og, cvo, gt, x, wa, wb, wo):
    T = x.shape[0]
    tm = min(512, T)

    def body(og_ref, cvo_ref, gt_ref, x_ref, wa_ref, wb_ref, wo_ref, x1_ref, mgt_ref):
        ya = jnp.dot(og_ref[...], _shard_cols(wa_ref), preferred_element_type=F32)
        yb = jnp.dot(cvo_ref[...], _shard_cols(wb_ref), preferred_element_type=F32)
        m = (_sigmoid(gt_ref[:, :D_MODEL].astype(F32)) * ya
             + _sigmoid(gt_ref[:, D_MODEL:].astype(F32)) * yb)
        mgt_ref[...] = m.T.astype(BF16)
        x1_ref[...] = x_ref[...] + jnp.dot(m.astype(BF16), wo_ref[...], preferred_element_type=F32)

    row = lambda n: pl.BlockSpec((tm, n), lambda i: (i, 0))
    return pl.pallas_call(
        body, name="merge_fwd", grid=(T // tm,),
        in_specs=[row(HG_WIDTH), row(CONV_WIDTH), row(2 * D_MODEL), row(D_MODEL),
                  _resident(wa.shape), _resident(wb.shape), _resident(wo.shape)],
        out_specs=[row(D_MODEL), _col(tm, D_MODEL)],
        out_shape=[jax.ShapeDtypeStruct((T, D_MODEL), F32), jax.ShapeDtypeStruct((D_MODEL, T), BF16)],
        compiler_params=_params(("parallel",)),
    )(og, cvo, gt, x, wa, wb, wo)


def _ffn_fwd_loss(x1, g, wg, wu, wd, target, g_fin):
    T = x1.shape[0]
    tm = min(256, T)

    def body(x_ref, g_ref, wg_ref, wu_ref, wd_ref, t_ref, gf_ref,
             ht_ref, gate_ref, up_ref, actt_ref, loss_ref, dgf_ref, dx2_ref):
        @pl.when(pl.program_id(0) == 0)
        def _():
            loss_ref[...] = jnp.zeros_like(loss_ref)
            dgf_ref[...] = jnp.zeros_like(dgf_ref)

        xv = x_ref[...]
        r = lax.rsqrt(jnp.mean(xv * xv, axis=-1, keepdims=True) + EPS)
        hf = xv * r * g_ref[...]
        h = hf.astype(BF16)
        ht_ref[...] = hf.T.astype(BF16)
        gate = _mm_nt(h, wg_ref[...])
        up = _mm_nt(h, wu_ref[...])
        gate_ref[...] = gate.astype(STASH)
        up_ref[...] = up.astype(STASH)
        act = gate * _sigmoid(gate) * up
        actt_ref[...] = act.T.astype(BF16)
        x2 = xv + jnp.dot(act.astype(BF16), wd_ref[...], preferred_element_type=F32)

        gv = gf_ref[...]
        r2 = lax.rsqrt(jnp.mean(x2 * x2, axis=-1, keepdims=True) + EPS)
        xh = x2 * r2
        err = xh * gv - t_ref[...]
        loss_ref[...] += 0.5 * jnp.sum(jnp.mean(err * err, axis=-1, keepdims=True), axis=0, keepdims=True)
        dy = err * (1.0 / D_MODEL)
        dgf_ref[...] += jnp.sum(dy * xh, axis=0, keepdims=True)
        w = dy * gv
        dx2_ref[...] = r2 * (w - xh * jnp.mean(w * xh, axis=-1, keepdims=True))

    row = lambda n: pl.BlockSpec((tm, n), lambda i: (i, 0))
    return pl.pallas_call(
        body, name="ffn_fwd_loss", grid=(T // tm,),
        in_specs=[row(D_MODEL), _full((1, D_MODEL)), _resident(wg.shape), _resident(wu.shape), _resident(wd.shape),
                  row(D_MODEL), _full((1, D_MODEL))],
        out_specs=[_col(tm, D_MODEL), row(D_FF), row(D_FF), _col(tm, D_FF), _full((1, 128)), _full((1, D_MODEL)),
                   row(D_MODEL)],
        out_shape=[jax.ShapeDtypeStruct((D_MODEL, T), BF16), jax.ShapeDtypeStruct((T, D_FF), STASH),
                   jax.ShapeDtypeStruct((T, D_FF), STASH), jax.ShapeDtypeStruct((D_FF, T), BF16),
                   jax.ShapeDtypeStruct((1, 128), F32), jax.ShapeDtypeStruct((1, D_MODEL), F32),
                   jax.ShapeDtypeStruct((T, D_MODEL), F32)],
        compiler_params=_params(("arbitrary",)),
    )(x1, g, wg, wu, wd, target, g_fin)


def _ffn_bwd(dx2, x1, gate, up, g, wg, wu, wd):
    T = x1.shape[0]
    tm = min(256, T)

    def body(dx2_ref, x_ref, gate_ref, up_ref, g_ref, wg_ref, wu_ref, wd_ref, dgate_ref, dup_ref, dx1_ref, dgn_ref):
        @pl.when(pl.program_id(0) == 0)
        def _():
            dgn_ref[...] = jnp.zeros_like(dgn_ref)

        dx2 = dx2_ref[...]
        dact = _mm_nt(dx2, wd_ref[...])
        gate = gate_ref[...].astype(F32)
        s = _sigmoid(gate)
        dgate = (dact * up_ref[...].astype(F32) * (s * (1.0 + gate * (1.0 - s)))).astype(BF16)
        dup = (dact * (gate * s)).astype(BF16)
        dgate_ref[...] = dgate
        dup_ref[...] = dup
        dh = _mm(dgate, wg_ref[...]) + _mm(dup, wu_ref[...])
        xv = x_ref[...]
        r = lax.rsqrt(jnp.mean(xv * xv, axis=-1, keepdims=True) + EPS)
        xh = xv * r
        dgn_ref[...] += jnp.sum(dh * xh, axis=0, keepdims=True)
        w = dh * g_ref[...]
        dx1_ref[...] = dx2 + r * (w - xh * jnp.mean(w * xh, axis=-1, keepdims=True))

    row = lambda n: pl.BlockSpec((tm, n), lambda i: (i, 0))
    return pl.pallas_call(
        body, name="ffn_bwd", grid=(T // tm,),
        in_specs=[row(D_MODEL), row(D_MODEL), row(D_FF), row(D_FF), _full((1, D_MODEL)),
                  _resident(wg.shape), _resident(wu.shape), _resident(wd.shape)],
        out_specs=[row(D_FF), row(D_FF), row(D_MODEL), _full((1, D_MODEL))],
        out_shape=[jax.ShapeDtypeStruct((T, D_FF), BF16), jax.ShapeDtypeStruct((T, D_FF), BF16),
                   jax.ShapeDtypeStruct((T, D_MODEL), F32), jax.ShapeDtypeStruct((1, D_MODEL), F32)],
        compiler_params=_params(("arbitrary",)),
    )(dx2, x1, gate, up, g, wg, wu, wd)


def _merge_bwd(dx1, og, cvo, gt, cv, wa, wb, wo, conv_w):
    T = dx1.shape[0]
    tm = min(512, T)
    nt = T // tm

    def body(dx_ref, og_ref, cvo_ref, gt_ref, cv_ref, halo_ref, wa_ref, wb_ref, wo_ref, cw_ref,
             dgt_ref, dya_ref, dyb_ref, dog_ref, dcv_ref, dcw_ref, prev_u, next_dy):
        step = pl.program_id(0)

        @pl.when(step == 0)
        def _():
            next_dy[...] = jnp.zeros_like(next_dy)
            dcw_ref[...] = jnp.zeros_like(dcw_ref)

        dm = _mm_nt(dx_ref[...], wo_ref[...])
        wa = _shard_cols(wa_ref)
        wb = _shard_cols(wb_ref)
        ya = jnp.dot(og_ref[...], wa, preferred_element_type=F32)
        yb = jnp.dot(cvo_ref[...], wb, preferred_element_type=F32)
        sa = _sigmoid(gt_ref[:, :D_MODEL].astype(F32))
        sb = _sigmoid(gt_ref[:, D_MODEL:].astype(F32))
        dgt_ref[:, :D_MODEL] = (dm * ya * (sa * (1.0 - sa))).astype(BF16)
        dgt_ref[:, D_MODEL:] = (dm * yb * (sb * (1.0 - sb))).astype(BF16)
        dya = (dm * sa).astype(BF16)
        dyb = (dm * sb).astype(BF16)
        dya_ref[...] = dya
        dyb_ref[...] = dyb
        dog_ref[...] = _mm_nt(dya, wa)
        dcvo = _mm_nt(dyb, wb)

        cvt = cv_ref[...].astype(F32)
        c, bg, xb = cvt[:, :CONV_WIDTH], cvt[:, CONV_WIDTH:2 * CONV_WIDTH], cvt[:, 2 * CONV_WIDTH:]
        halo = halo_ref[...].astype(F32)
        first_tile = step == nt - 1
        prev_u[...] = jnp.where(first_tile, 0.0, halo[:, :CONV_WIDTH] * halo[:, 2 * CONV_WIDTH:])
        u = c * xb
        row = lax.broadcasted_iota(jnp.int32, u.shape, 0)
        p1 = prev_u[HALO - 1:HALO, :]
        p2 = prev_u[HALO - 2:HALO - 1, :]
        u1 = jnp.where(row >= 1, pltpu.roll(u, 1, 0), p1)
        u2 = jnp.where(row >= 2, pltpu.roll(u, 2, 0), jnp.where(row == 1, p1, p2))
        w0, w1, w2 = cw_ref[0:1, :], cw_ref[1:2, :], cw_ref[2:3, :]
        y = w0 * u2 + w1 * u1 + w2 * u
        dcv_ref[:, CONV_WIDTH:2 * CONV_WIDTH] = (dcvo * y).astype(BF16)
        dy = dcvo * bg
        dcw_ref[0:1, :] += jnp.sum(dy * u2, axis=0, keepdims=True)
        dcw_ref[1:2, :] += jnp.sum(dy * u1, axis=0, keepdims=True)
        dcw_ref[2:3, :] += jnp.sum(dy * u, axis=0, keepdims=True)
        n1 = next_dy[0:1, :]
        n2 = next_dy[1:2, :]
        dy1 = jnp.where(row < tm - 1, pltpu.roll(dy, tm - 1, 0), n1)
        dy2 = jnp.where(row < tm - 2, pltpu.roll(dy, tm - 2, 0), jnp.where(row == tm - 2, n1, n2))
        du = w2 * dy + w1 * dy1 + w0 * dy2
        dcv_ref[:, :CONV_WIDTH] = (du * xb).astype(BF16)
        dcv_ref[:, 2 * CONV_WIDTH:] = (du * c).astype(BF16)
        next_dy[...] = dy[:HALO, :]

    rt = lambda i: nt - 1 - i
    row = lambda n: pl.BlockSpec((tm, n), lambda i: (rt(i), 0))
    halo = pl.BlockSpec((HALO, N_CV), lambda i: (jnp.maximum(rt(i) * (tm // HALO) - 1, 0), 0))
    return pl.pallas_call(
        body, name="merge_bwd", grid=(nt,),
        in_specs=[row(D_MODEL), row(HG_WIDTH), row(CONV_WIDTH), row(2 * D_MODEL), row(N_CV), halo,
                  _resident(wa.shape), _resident(wb.shape), _resident(wo.shape), _full((CONV_K, CONV_WIDTH))],
        out_specs=[row(2 * D_MODEL), row(D_MODEL), row(D_MODEL), row(HG_WIDTH), row(N_CV),
                   _full((CONV_K, CONV_WIDTH))],
        out_shape=[jax.ShapeDtypeStruct((T, 2 * D_MODEL), BF16), jax.ShapeDtypeStruct((T, D_MODEL), BF16),
                   jax.ShapeDtypeStruct((T, D_MODEL), BF16), jax.ShapeDtypeStruct((T, HG_WIDTH), F32),
                   jax.ShapeDtypeStruct((T, N_CV), BF16), jax.ShapeDtypeStruct((CONV_K, CONV_WIDTH), F32)],
        scratch_shapes=[pltpu.VMEM((HALO, CONV_WIDTH), F32), pltpu.VMEM((HALO, CONV_WIDTH), F32)],
        compiler_params=_params(("arbitrary",)),
    )(dx1, og, cvo, gt, cv, cv, wa, wb, wo, conv_w)


def _drop_operands(body, first, count):
    def wrapped(*refs):
        return body(*refs[:first], *refs[first + count:])
    return wrapped


def _hg_bwd(dog, hg, o, st, low, gn, after=()):
    T = hg.shape[0]
    tb = min(512, T)
    sb = min(256, tb)
    nb = T // tb
    nc = tb // CHUNK
    wid = HEADS_PER_STEP * HEAD_DIM

    def body(q_ref, f_ref, i_ref, g_ref, low_ref, gn_ref, o_ref, dog_ref, st_ref,
             dq_ref, df_ref, di_ref, dg_ref, dlow_ref, dgn_ref,
             ds_scr, dqi_scr, dko_scr, dv_scr, dd_scr, dqh_scr, dkh_scr):
        h = pl.program_id(0)
        t = pl.program_id(1)

        @pl.when(t == 0)
        def _():
            ds_scr[...] = jnp.zeros_like(ds_scr)
            dlow_ref[...] = jnp.zeros_like(dlow_ref)

        @pl.when((t == 0) & (h == 0))
        def _():
            dgn_ref[...] = jnp.zeros_like(dgn_ref)

        pos = _chunk_pos((tb, HEAD_DIM))
        mask = _intra_mask(sb)
        gnv = gn_ref[...]
        lanes = [slice(hh * HEAD_DIM, (hh + 1) * HEAD_DIM) for hh in range(HEADS_PER_STEP)]
        heads = []
        for hh, ln in enumerate(lanes):
            lb, lb1 = _lower_bound(low_ref.at[:, ln])
            qr = q_ref[:, ln]
            sq, q, sg, f, k, e_qa, e_ka, e_b, e_ko, dec = _hg_gates(qr, f_ref[:, ln], lb, pos, tb)

            gr = g_ref[:, ln]
            o = o_ref[:, ln]
            dog_v = dog_ref[:, ln]
            sgr = _sigmoid(gr)
            r = lax.rsqrt(jnp.mean(o * o, axis=-1, keepdims=True) + EPS)
            oh = o * r
            dg_ref[:, ln] = (dog_v * (oh * gnv) * (sgr * (1.0 + gr * (1.0 - sgr)))).astype(BF16)
            don = dog_v * (gr * sgr)
            dgn_ref[...] += jnp.sum(don * oh, axis=0, keepdims=True)
            w = don * gnv
            do = (r * (w - oh * jnp.mean(w * oh, axis=-1, keepdims=True))).astype(BF16)

            qh = (q * e_qa).astype(BF16)
            kh = (k * e_ka).astype(BF16)
            qi = (q * e_b).astype(BF16)
            ko = (k * e_ko).astype(BF16)
            vb = i_ref[:, ln].astype(BF16)

            for s in range(tb // sb):
                sl = slice(s * sb, (s + 1) * sb)
                p = jnp.where(mask, _mm_nt(qh[sl], kh[sl]), 0.0).astype(BF16)
                dp = jnp.where(mask, _mm_nt(do[sl], vb[sl]), 0.0).astype(BF16)
                dv_scr[sl, ln] = _mm_tn(p, do[sl])
                dqh_scr[sl, ln] = _mm(dp, kh[sl])
                dkh_scr[sl, ln] = _mm_tn(dp, qh[sl])
            heads.append(dict(lb=lb, lb1=lb1, qr=qr, sq=sq, q=q, sg=sg, f=f, k=k, e_qa=e_qa, e_ka=e_ka, e_b=e_b,
                              e_ko=e_ko, dec=dec, do=do, qi=qi, ko=ko, vb=vb, ds=ds_scr[hh]))

        for c in reversed(range(nc)):
            sl = slice(c * CHUNK, (c + 1) * CHUNK)
            for hh, ln in enumerate(lanes):
                hd = heads[hh]
                ds = hd["ds"]
                st_c = st_ref[hh, c]
                dqi_scr[sl, ln] = _mm(hd["do"][sl], st_c)
                dko_scr[sl, ln] = _mm(hd["vb"][sl], ds)
                dv_scr[sl, ln] = dv_scr[sl, ln] + _mm_nt(hd["ko"][sl], ds)
                dd_scr[sl, ln] = jnp.broadcast_to(jnp.sum(ds * st_c, axis=0, keepdims=True), (CHUNK, HEAD_DIM))
                hd["ds"] = hd["dec"][c * CHUNK:c * CHUNK + 1, :] * ds + _mm_tn(hd["do"][sl], hd["qi"][sl])

        for hh, ln in enumerate(lanes):
            hd = heads[hh]
            ds_scr[hh] = hd["ds"]
            q, k, lb = hd["q"], hd["k"], hd["lb"]
            dko_e = dko_scr[:, ln] * hd["e_ko"]
            dq = dqh_scr[:, ln] * hd["e_qa"] + dqi_scr[:, ln] * hd["e_b"]
            dk = dkh_scr[:, ln] * hd["e_ka"] + dko_e
            kd3 = (k * dko_e).reshape(nc, CHUNK, HEAD_DIM)
            last = jnp.broadcast_to(jnp.sum(kd3, axis=1, keepdims=True), kd3.shape).reshape(tb, HEAD_DIM)
            db = q * dq - k * dk + jnp.where(pos == CHUNK - 1, hd["dec"] * dd_scr[:, ln] + last, 0.0)
            dlg = _chunk_rev_cumsum(db, pos)
            dfv = dlg / hd["f"] - dk
            s_low = jnp.sum(dfv * (1.0 - hd["sg"]), axis=0, keepdims=True)
            dlow_ref[0:1, ln] += s_low * lb * (1.0 - lb)
            dlow_ref[1:2, ln] += -s_low * lb * hd["lb1"]
            df_ref[:, ln] = (dfv * (1.0 - lb) * hd["sg"] * (1.0 - hd["sg"])).astype(BF16)
            dq_ref[:, ln] = (dq * Q_SCALE * (hd["sq"] * (1.0 + hd["qr"] * (1.0 - hd["sq"])))).astype(BF16)
            di_ref[:, ln] = dv_scr[:, ln].astype(BF16)

    rt = lambda t: nb - 1 - t
    col = lambda p: pl.BlockSpec((tb, wid), lambda h, t: (rt(t), p * HEAD_GROUPS + h))
    hcol = pl.BlockSpec((tb, wid), lambda h, t: (rt(t), h))
    piece = jax.ShapeDtypeStruct((T, HG_WIDTH), BF16)
    tile = pltpu.VMEM((tb, wid), F32)
    return pl.pallas_call(
        _drop_operands(body, 9, len(after)), name="hg_bwd", grid=(HEAD_GROUPS, nb),
        in_specs=[col(0), col(1), col(2), col(3), pl.BlockSpec((2, wid), lambda h, t: (0, h)),
                  pl.BlockSpec((1, HEAD_DIM), lambda h, t: (0, 0)), hcol, hcol,
                  pl.BlockSpec((HEADS_PER_STEP, nc, HEAD_DIM, HEAD_DIM), lambda h, t: (h, rt(t), 0, 0))]
                 + [HBM_SPEC] * len(after),
        out_specs=[hcol, hcol, hcol, hcol, pl.BlockSpec((2, wid), lambda h, t: (0, h)),
                   pl.BlockSpec((1, HEAD_DIM), lambda h, t: (0, 0))],
        out_shape=[piece, piece, piece, piece, jax.ShapeDtypeStruct((2, HG_WIDTH), F32),
                   jax.ShapeDtypeStruct((1, HEAD_DIM), F32)],
        scratch_shapes=[pltpu.VMEM((HEADS_PER_STEP, HEAD_DIM, HEAD_DIM), F32), tile, tile, tile, tile, tile, tile],
        compiler_params=_params(("arbitrary", "arbitrary")),
    )(hg, hg, hg, hg, low, gn, o, dog, st, *after)


def _in_bwd(dparts, w_in, x, dx1, g, after=()):
    T = x.shape[0]
    tm = min(512, T)
    widths = [p.shape[1] for p in dparts]
    offs = [sum(widths[:i]) for i in range(len(widths))]
    n = len(dparts)

    def body(*refs):
        d_refs = refs[:n]
        w_ref, x_ref, dx1_ref, g_ref, dx_ref, dgn_ref = refs[n:]

        @pl.when(pl.program_id(0) == 0)
        def _():
            dgn_ref[...] = jnp.zeros_like(dgn_ref)

        dh = None
        for d_ref, off, wd in zip(d_refs, offs, widths):
            part = _mm(d_ref[...], w_ref[off:off + wd, :])
            dh = part if dh is None else dh + part
        xv = x_ref[...]
        r = lax.rsqrt(jnp.mean(xv * xv, axis=-1, keepdims=True) + EPS)
        xh = xv * r
        dgn_ref[...] += jnp.sum(dh * xh, axis=0, keepdims=True)
        w = dh * g_ref[...]
        dx_ref[...] = dx1_ref[...] + r * (w - xh * jnp.mean(w * xh, axis=-1, keepdims=True))

    row = lambda m: pl.BlockSpec((tm, m), lambda i: (i, 0))
    return pl.pallas_call(
        _drop_operands(body, n + 4, len(after)), name="in_bwd", grid=(T // tm,),
        in_specs=[row(wd) for wd in widths] + [_resident(w_in.shape), row(D_MODEL), row(D_MODEL), _full((1, D_MODEL))]
                 + [HBM_SPEC] * len(after),
        out_specs=[row(D_MODEL), _full((1, D_MODEL))],
        out_shape=[jax.ShapeDtypeStruct((T, D_MODEL), F32), jax.ShapeDtypeStruct((1, D_MODEL), F32)],
        compiler_params=_params(("arbitrary",)),
    )(*dparts, w_in, x, dx1, g, *after)


def _wgrad(name, at, b, tn, transposed=False, tk=2048, after=()):
    M, T = at.shape
    N = b.shape[1]
    tk = min(tk, T)
    nk = T // tk

    def body(a_ref, b_ref, o_ref, acc):
        k = pl.program_id(1)
        part = _mm(a_ref[...], b_ref[...])

        @pl.when(k == 0)
        def _():
            acc[...] = part

        @pl.when(k != 0)
        def _():
            acc[...] += part

        @pl.when(k == nk - 1)
        def _():
            o_ref[...] = (acc[...].T if transposed else acc[...]).astype(BF16)

    if transposed:
        out_spec, out_shape = pl.BlockSpec((tn, M), lambda j, k: (j, 0)), (N, M)
    else:
        out_spec, out_shape = pl.BlockSpec((M, tn), lambda j, k: (0, j)), (M, N)
    return pl.pallas_call(
        _drop_operands(body, 2, len(after)), name=name, grid=(N // tn, nk),
        in_specs=[pl.BlockSpec((M, tk), lambda j, k: (0, k)), pl.BlockSpec((tk, tn), lambda j, k: (k, j))]
                 + [HBM_SPEC] * len(after),
        out_specs=out_spec, out_shape=jax.ShapeDtypeStruct(out_shape, BF16),
        scratch_shapes=[pltpu.VMEM((M, tn), F32)],
        compiler_params=_params(("parallel", "arbitrary")),
    )(at, b, *after)


def _wgrad_branches(ogt, dya, cvot, dyb):
    M, T = ogt.shape
    N = dya.shape[1]
    tk = min(1024, T)
    nk = T // tk
    c = N // N_DEV

    def body(at_ref, da_ref, bt_ref, db_ref, oa_ref, ob_ref, acc_a, acc_b):
        k = pl.program_id(0)

        @pl.when(k == 0)
        def _():
            acc_a[...] = jnp.zeros_like(acc_a)
            acc_b[...] = jnp.zeros_like(acc_b)

        acc_a[...] += _mm(at_ref[...], da_ref[...])
        acc_b[...] += _mm(bt_ref[...], db_ref[...])

        @pl.when(k == nk - 1)
        def _():
            for s in range(N_DEV):
                oa_ref[s] = acc_a[:, s * c:(s + 1) * c].astype(BF16)
                ob_ref[s] = acc_b[:, s * c:(s + 1) * c].astype(BF16)

    lhs = pl.BlockSpec((M, tk), lambda k: (0, k))
    rhs = pl.BlockSpec((tk, N), lambda k: (k, 0))
    out = jax.ShapeDtypeStruct((N_DEV, M, c), BF16)
    return pl.pallas_call(
        body, name="wgrad_branches", grid=(nk,),
        in_specs=[lhs, rhs, lhs, rhs], out_specs=[_full((N_DEV, M, c))] * 2, out_shape=[out, out],
        scratch_shapes=[pltpu.VMEM((M, N), F32)] * 2,
        compiler_params=_params(("arbitrary",)),
    )(ogt, dya, cvot, dyb)


def _wgrad_in(ht, dparts, after=()):
    M, T = ht.shape
    tn = 512
    tk = min(2048, T)
    nk = T // tk
    nblk = [p.shape[1] // tn for p in dparts]
    start = [sum(nblk[:i]) for i in range(len(nblk))]
    n = len(dparts)

    def body(a_ref, *refs):
        d_refs, o_ref, acc = refs[:n], refs[n], refs[n + 1]
        j = pl.program_id(0)
        k = pl.program_id(1)

        @pl.when(k == 0)
        def _():
            acc[...] = jnp.zeros_like(acc)

        for d_ref, s, nb in zip(d_refs, start, nblk):
            @pl.when((j >= s) & (j < s + nb))
            def _():
                acc[...] += _mm(a_ref[...], d_ref[...])

        @pl.when(k == nk - 1)
        def _():
            o_ref[...] = acc[...].T.astype(BF16)

    def piece_spec(s, nb):
        def index(j, k):
            inside = (j >= s) & (j < s + nb)
            return jnp.where(inside, k, 0), jnp.clip(j - s, 0, nb - 1)
        return pl.BlockSpec((tk, tn), index)

    return pl.pallas_call(
        _drop_operands(body, 1 + n, len(after)), name="wgrad_in", grid=(sum(nblk), nk),
        in_specs=[pl.BlockSpec((M, tk), lambda j, k: (0, k))] + [piece_spec(s, nb) for s, nb in zip(start, nblk)]
                 + [HBM_SPEC] * len(after),
        out_specs=pl.BlockSpec((tn, M), lambda j, k: (j, 0)),
        out_shape=jax.ShapeDtypeStruct((sum(nblk) * tn, M), BF16),
        scratch_shapes=[pltpu.VMEM((M, tn), F32)],
        compiler_params=_params(("parallel", "arbitrary")),
    )(ht, *dparts, *after)


def _adamw_math(w, g, m, v):
    m = ADAM_B1 * m + (1.0 - ADAM_B1) * g
    v = ADAM_B2 * v + (1.0 - ADAM_B2) * (g * g)
    m_hat = m / (1.0 - ADAM_B1 ** ADAM_STEP)
    v_hat = v / (1.0 - ADAM_B2 ** ADAM_STEP)
    delta = -ADAM_LR * (m_hat / (jnp.sqrt(v_hat) + ADAM_EPS) + ADAM_WD * w)
    return delta, m, v


def _adamw_sum(name, w, parts, m, v):
    R, C = w.shape
    tr = _row_tile(R)

    def body(w_ref, p_ref, m_ref, v_ref, g_out, d_out, m_out, v_out):
        g = p_ref[0].astype(F32)
        for k in range(1, 4):
            g = g + p_ref[k].astype(F32)
        g_out[...] = g
        d_out[...], m_out[...], v_out[...] = _adamw_math(w_ref[...], g, m_ref[...], v_ref[...])

    blk = pl.BlockSpec((tr, C), lambda i: (i, 0))
    out = jax.ShapeDtypeStruct((R, C), F32)
    return pl.pallas_call(
        body, name=name, grid=(R // tr,),
        in_specs=[blk, pl.BlockSpec((4, tr, C), lambda i: (0, i, 0)), blk, blk],
        out_specs=[blk, blk, blk, blk], out_shape=[out, out, out, out],
        compiler_params=_params(("parallel",)),
    )(w, parts, m, v)


_SMALL_SLOTS = (("norm_mix_g", 0, 1, 1024), ("norm_ffn_g", 1, 1, 1024), ("norm_final_g", 2, 1, 1024),
                ("lower_bounds", 3, 2, 512), ("hg_norm_g", 5, 1, 128), ("loss", 6, 1, 128), ("conv_w", 8, 3, 512))
_SMALL_PARAMS = tuple(s for s in _SMALL_SLOTS if s[0] != "loss")
CONV_SHARD = CONV_WIDTH // N_DEV


def _small_pack(small):
    def body(*refs):
        out = refs[-1]
        out[...] = jnp.zeros_like(out)
        for ref, (_, row, rows, lanes) in zip(refs[:-1], _SMALL_SLOTS):
            out[row:row + rows, 0:lanes] = ref[...]

    vmem = pl.BlockSpec(memory_space=pltpu.VMEM)
    return pl.pallas_call(
        body, name="small_pack", in_specs=[vmem] * len(_SMALL_SLOTS), out_specs=vmem,
        out_shape=jax.ShapeDtypeStruct((SMALL_ROWS, 1024), F32),
    )(*[small[name] for name, _, _, _ in _SMALL_SLOTS])


def _small_update(gathered, dev, w, m, v):
    n = len(_SMALL_PARAMS)

    def body(dev_ref, g_ref, *refs):
        w_refs, m_refs, v_refs = refs[:n], refs[n:2 * n], refs[2 * n:3 * n]
        loss_ref, out_refs, sum_scr = refs[3 * n], refs[3 * n + 1:-1], refs[-1]
        total = g_ref[0]
        for k in range(1, N_DEV):
            total = total + g_ref[k]
        sum_scr[...] = total
        loss_ref[...] = sum_scr[6:7, 0:128]
        for p, (name, row, rows, lanes) in enumerate(_SMALL_PARAMS):
            if name == "conv_w":
                g = sum_scr[row:row + rows, 0:CONV_SHARD]
                for s in range(1, N_DEV):
                    g = jnp.where(dev_ref[0] == s, sum_scr[row:row + rows, s * CONV_SHARD:(s + 1) * CONV_SHARD], g)
            else:
                g = sum_scr[row:row + rows, 0:lanes]
            delta, m_new, v_new = _adamw_math(w_refs[p][...], g, m_refs[p][...], v_refs[p][...])
            out_refs[4 * p][...] = g
            out_refs[4 * p + 1][...] = delta
            out_refs[4 * p + 2][...] = m_new
            out_refs[4 * p + 3][...] = v_new

    vmem = pl.BlockSpec(memory_space=pltpu.VMEM)
    outs = [jax.ShapeDtypeStruct((1, 128), F32)]
    for a in w:
        outs += [jax.ShapeDtypeStruct(a.shape, F32)] * 4
    return pl.pallas_call(
        body, name="small_update",
        in_specs=[pl.BlockSpec(memory_space=pltpu.SMEM)] + [vmem] * (1 + 3 * n), out_specs=[vmem] * len(outs),
        out_shape=outs, scratch_shapes=[pltpu.VMEM((SMALL_ROWS, 1024), F32)],
    )(dev, gathered, *w, *m, *v)


def _row_tile(rows):
    for parts in (4, 2):
        if rows % (16 * parts) == 0:
            return rows // parts
    return rows


def _pair_sum(name, by_owner, got, core, after=()):
    n = len(got)

    def body(core_ref, *refs):
        for a_ref, b_ref, o_ref in zip(refs[:n], refs[n:2 * n], refs[2 * n:]):
            o_ref[...] = (a_ref[...].astype(F32) + b_ref[...].astype(F32)).astype(BF16)

    def blk(g):
        return pl.BlockSpec((None,) + g.shape[1:], lambda k, core_ref: (k, 0, 0))

    def mine(g):
        return pl.BlockSpec((None,) + g.shape[1:], lambda k, core_ref: (2 * k + core_ref[0], 0, 0))

    return pl.pallas_call(
        _drop_operands(body, 1 + 2 * n, len(after)), name=name,
        grid_spec=pltpu.PrefetchScalarGridSpec(
            num_scalar_prefetch=1, grid=(4,),
            in_specs=[mine(g) for g in got] + [blk(g) for g in got] + [HBM_SPEC] * len(after),
            out_specs=[blk(g) for g in got]),
        out_shape=[jax.ShapeDtypeStruct(g.shape, BF16) for g in got],
        compiler_params=_params(("parallel",)),
    )(core, *by_owner, *got, *after)


MESH = pl.DeviceIdType.MESH
HBM_SPEC = pl.BlockSpec(memory_space=pl.ANY)


def _handshake(peers):
    barrier = pltpu.get_barrier_semaphore()
    for peer in peers:
        pl.semaphore_signal(barrier, inc=1, device_id=peer, device_id_type=MESH)
    pl.semaphore_wait(barrier, len(peers))


def _comm_call(body, name, operands, out_shape, scratch, collective_id):
    if collective_id is None:
        return pl.pallas_call(body, name=name, in_specs=[HBM_SPEC] * len(operands), out_specs=[HBM_SPEC] * len(out_shape),
                              out_shape=out_shape, scratch_shapes=scratch)(*operands)
    return pl.kernel(body, out_type=out_shape, mesh=plsc.ScalarSubcoreMesh(axis_name="sequencer", num_cores=1),
                     scratch_types=scratch, name=name,
                     compiler_params=pltpu.CompilerParams(collective_id=collective_id))(*operands)


def _all_gather(name, blocks, collective_id=None, after=()):
    n = len(blocks)
    na = len(after)

    def body(*refs):
        x_refs, out_refs = refs[:n], refs[n + na:2 * n + na]
        send_sems, recv_sems, local_sems = refs[2 * n + na:]
        x, y, c = lax.axis_index("x"), lax.axis_index("y"), lax.axis_index("c")
        me, sibling = (x, y, c), (x, y, 1 - c)
        chips = [(1 - x, y), (x, 1 - y), (1 - x, 1 - y)]
        if collective_id is not None:
            _handshake([sibling] + [(*chip, c) for chip in chips])

        def slot(i, px, py, pc):
            return out_refs[i].at[4 * px + 2 * py + pc]

        def copy(i, k, blk, to, src=None):
            return pltpu.make_async_remote_copy(
                src_ref=slot(i, *blk) if src is None else src, dst_ref=slot(i, *blk),
                send_sem=send_sems.at[7 * i + k], recv_sem=recv_sems.at[7 * i + k], device_id=to, device_id_type=MESH)

        mine = [pltpu.make_async_copy(x_refs[i], slot(i, *me), local_sems.at[i]) for i in range(n)]
        for cp in mine:
            cp.start()
        first = []
        for i in range(n):
            first.append(copy(i, 0, me, sibling, src=x_refs[i]))
            first += [copy(i, 1 + j, me, (*chip, c), src=x_refs[i]) for j, chip in enumerate(chips)]
        for cp in first:
            cp.start()
        passed = []
        for i in range(n):
            for j, chip in enumerate(chips):
                copy(i, 1 + j, (*chip, c), me).wait_recv()
                passed.append(copy(i, 4 + j, (*chip, c), sibling))
                passed[-1].start()
        for i in range(n):
            copy(i, 0, sibling, me).wait_recv()
            for j, chip in enumerate(chips):
                copy(i, 4 + j, (*chip, 1 - c), me).wait_recv()
        for cp in first + passed:
            cp.wait_send()
        for cp in mine:
            cp.wait()

    return _comm_call(
        body, name, list(blocks) + list(after), [jax.ShapeDtypeStruct((N_DEV,) + b.shape, b.dtype) for b in blocks],
        [pltpu.SemaphoreType.DMA((7 * n,)), pltpu.SemaphoreType.DMA((7 * n,)), pltpu.SemaphoreType.DMA((n,))],
        collective_id)


def _sibling_swap(name, by_owner, collective_id=None, after=()):
    n = len(by_owner)
    na = len(after)

    def body(*refs):
        x_refs, out_refs = refs[:n], refs[n + na:2 * n + na]
        send_sems, recv_sems = refs[2 * n + na:]
        x, y, c = lax.axis_index("x"), lax.axis_index("y"), lax.axis_index("c")
        if collective_id is not None:
            _handshake([(x, y, 1 - c)])
        copies = []
        for i in range(n):
            for k in range(4):
                copies.append(pltpu.make_async_remote_copy(
                    src_ref=x_refs[i].at[2 * k + 1 - c], dst_ref=out_refs[i].at[k],
                    send_sem=send_sems.at[4 * i + k], recv_sem=recv_sems.at[4 * i + k],
                    device_id=(x, y, 1 - c), device_id_type=MESH))
        for cp in copies:
            cp.start()
        for cp in copies:
            cp.wait()

    return _comm_call(
        body, name, list(by_owner) + list(after),
        [jax.ShapeDtypeStruct((4,) + b.shape[1:], b.dtype) for b in by_owner],
        [pltpu.SemaphoreType.DMA((4 * n,)), pltpu.SemaphoreType.DMA((4 * n,))], collective_id)


def _chip_exchange(name, sums, collective_id=None, after=()):
    n = len(sums)
    na = len(after)

    def body(*refs):
        x_refs, out_refs = refs[:n], refs[n + na:2 * n + na]
        send_sems, recv_sems, local_sems = refs[2 * n + na:]
        x, y, c = lax.axis_index("x"), lax.axis_index("y"), lax.axis_index("c")
        chips = [(1 - x, y), (x, 1 - y), (1 - x, 1 - y)]
        my_chip = 2 * x + y
        if collective_id is not None:
            _handshake([(cx, cy, c) for cx, cy in chips])
        mine = [pltpu.make_async_copy(x_refs[i].at[my_chip], out_refs[i].at[my_chip], local_sems.at[i])
                for i in range(n)]
        for cp in mine:
            cp.start()
        sends = []
        for i in range(n):
            for j, (cx, cy) in enumerate(chips):
                sends.append(pltpu.make_async_remote_copy(
                    src_ref=x_refs[i].at[2 * cx + cy], dst_ref=out_refs[i].at[my_chip],
                    send_sem=send_sems.at[3 * i + j], recv_sem=recv_sems.at[3 * i + j],
                    device_id=(cx, cy, c), device_id_type=MESH))
        for cp in sends:
            cp.start()
        for i in range(n):
            for j, (cx, cy) in enumerate(chips):
                pltpu.make_async_remote_copy(
                    src_ref=x_refs[i].at[my_chip], dst_ref=out_refs[i].at[2 * cx + cy],
                    send_sem=send_sems.at[3 * i + j], recv_sem=recv_sems.at[3 * i + j],
                    device_id=(cx, cy, c), device_id_type=MESH).wait_recv()
        for cp in sends:
            cp.wait_send()
        for cp in mine:
            cp.wait()

    return _comm_call(
        body, name, list(sums) + list(after), [jax.ShapeDtypeStruct(s.shape, s.dtype) for s in sums],
        [pltpu.SemaphoreType.DMA((3 * n,)), pltpu.SemaphoreType.DMA((3 * n,)), pltpu.SemaphoreType.DMA((n,))],
        collective_id)


def _cast_shards(shards):
    n = len(shards)

    def body(*refs):
        for i in range(n):
            refs[n + i][...] = refs[i][...].astype(BF16)

    vmem = pl.BlockSpec(memory_space=pltpu.VMEM)
    return pl.pallas_call(
        body, name="cast_shards", in_specs=[vmem] * n, out_specs=[vmem] * n,
        out_shape=[jax.ShapeDtypeStruct(s.shape, BF16) for s in shards],
        compiler_params=pltpu.CompilerParams(vmem_limit_bytes=VMEM_LIMIT_V7X),
    )(*shards)


BIG = ("w_in", "w_branch_a", "w_branch_b", "w_out", "w_ffn_gate", "w_ffn_up", "w_ffn_down")


def _local_step(x, target, gains, low, conv_w, wg8, reduce):
    g_mix, g_hg, g_ffn, g_fin = gains
    w_in = wg8["w_in"].reshape(N_IN, D_MODEL)
    wg = wg8["w_ffn_gate"].reshape(D_FF, D_MODEL)
    wu = wg8["w_ffn_up"].reshape(D_FF, D_MODEL)
    wa, wb = wg8["w_branch_a"], wg8["w_branch_b"]
    wo = wg8["w_out"].reshape(D_MODEL, D_MODEL)
    wd = wg8["w_ffn_down"].reshape(D_FF, D_MODEL)

    ht, hg, cv, gt, cvo, cvot = _fwd_in(x, g_mix, w_in, conv_w)
    o, og, ogt, st = _hg_fwd(hg, low, g_hg)
    x1, mgt = _merge_fwd(og, cvo, gt, x, wa, wb, wo)
    h2t, gate, up, actt, loss, d_gfin, dx2 = _ffn_fwd_loss(x1, g_ffn, wg, wu, wd, target, g_fin)

    dgate, dup, dx1, d_gffn = _ffn_bwd(dx2, x1, gate, up, g_ffn, wg, wu, wd)
    ffn = dict(
        w_ffn_down=_wgrad("wgrad_ffn_down", actt, dx2, 512).reshape(N_DEV, D_FF // N_DEV, D_MODEL),
        w_ffn_gate=_wgrad("wgrad_ffn_gate", h2t, dgate, 1408, transposed=True).reshape(N_DEV, D_FF // N_DEV, D_MODEL),
        w_ffn_up=_wgrad("wgrad_ffn_up", h2t, dup, 1408, transposed=True).reshape(N_DEV, D_FF // N_DEV, D_MODEL))
    dgt, dya, dyb, dog, dcv, d_conv = _merge_bwd(dx1, og, cvo, gt, cv, wa, wb, wo, conv_w)
    sums_ffn, got_ffn = reduce.begin(ffn, sum_after=[dya])
    parts_ffn, updated_ffn = reduce.finish(ffn, sums_ffn)
    grad_a, grad_b = _wgrad_branches(ogt, dya, cvot, dyb)
    out = dict(
        w_out=_wgrad("wgrad_out", mgt, dx1, 512, after=sums_ffn[:1]).reshape(N_DEV, D_MODEL // N_DEV, D_MODEL),
        w_branch_a=grad_a, w_branch_b=grad_b)
    dq, df, di, dg, d_low, d_ghg = _hg_bwd(dog, hg, o, st, low, g_hg, after=list(sums_ffn) + [out["w_out"]])
    sums_out, got_out = reduce.begin(out, after=[parts_ffn[0], dq], sum_after=updated_ffn)
    parts_out, updated_out = reduce.finish(out, sums_out)
    dparts = [dq, df, di, dg, dcv, dgt]
    w_in_grad = dict(w_in=_wgrad_in(ht, dparts, after=sums_out[:1]).reshape(N_DEV, N_IN // N_DEV, D_MODEL))
    sums_in, _ = reduce.begin(w_in_grad, after=parts_out[:1], sum_after=updated_out)
    parts_in, _ = reduce.finish(w_in_grad, sums_in)
    grad_x, d_gmix = _in_bwd(dparts, w_in, x, dx1, g_mix, after=list(parts_out[:1]) + list(sums_in))
    small = dict(norm_mix_g=d_gmix, norm_ffn_g=d_gffn, norm_final_g=d_gfin, lower_bounds=d_low, hg_norm_g=d_ghg,
                 conv_w=d_conv, loss=loss)
    return grad_x, small, parts_in


def _conv_shard_rows(a):
    return jnp.pad(a, ((0, 5), (0, 64)))


def kernel(x, norm_mix_g, w_in, lower_bounds, hg_norm_g, conv_w, w_branch_a, w_branch_b, w_out, norm_ffn_g, w_ffn_gate, w_ffn_up, w_ffn_down, norm_final_g, loss_target, m_norm_mix_g, m_w_in, m_lower_bounds, m_hg_norm_g, m_conv_w, m_w_branch_a, m_w_branch_b, m_w_out, m_norm_ffn_g, m_w_ffn_gate, m_w_ffn_up, m_w_ffn_down, m_norm_final_g, v_norm_mix_g, v_w_in, v_lower_bounds, v_hg_norm_g, v_conv_w, v_w_branch_a, v_w_branch_b, v_w_out, v_norm_ffn_g, v_w_ffn_gate, v_w_ffn_up, v_w_ffn_down, v_norm_final_g):
    cx, cy, cc = lax.axis_index("x"), lax.axis_index("y"), lax.axis_index("c")
    my_dev = 4 * cx + 2 * cy + cc

    def tr(a):
        return a[0].T

    big = dict(w_in=tr(w_in), w_branch_a=w_branch_a[0], w_branch_b=w_branch_b[0], w_out=w_out[0],
               w_ffn_gate=tr(w_ffn_gate), w_ffn_up=tr(w_ffn_up), w_ffn_down=w_ffn_down[0])
    big_m = dict(w_in=tr(m_w_in), w_branch_a=m_w_branch_a[0], w_branch_b=m_w_branch_b[0], w_out=m_w_out[0],
                 w_ffn_gate=tr(m_w_ffn_gate), w_ffn_up=tr(m_w_ffn_up), w_ffn_down=m_w_ffn_down[0])
    big_v = dict(w_in=tr(v_w_in), w_branch_a=v_w_branch_a[0], w_branch_b=v_w_branch_b[0], w_out=v_w_out[0],
                 w_ffn_gate=tr(v_w_ffn_gate), w_ffn_up=tr(v_w_ffn_up), w_ffn_down=v_w_ffn_down[0])
    transposed = ("w_in", "w_ffn_gate", "w_ffn_up")

    shards = dict(zip(BIG, _cast_shards([big[n] for n in BIG])))
    first = _all_gather("gather_w_in", [shards["w_in"], _conv_shard_rows(conv_w[0])])
    ids = iter(range(1, 16))
    mid = _all_gather("gather_mid", [shards[n] for n in BIG[1:4]], collective_id=next(ids), after=first[1:])
    ffn = _all_gather("gather_ffn", [shards[n] for n in BIG[4:]], collective_id=next(ids), after=first[1:])
    wg8 = dict(zip(BIG, [first[0]] + list(mid) + list(ffn)))
    conv_full = first[1][:, :3, :64].transpose(1, 0, 2).reshape(3, CONV_WIDTH)

    core = cc.reshape(1).astype(jnp.int32)
    outs = {}

    class Reduce:
        @staticmethod
        def begin(grads, after=(), sum_after=()):
            names = list(grads)
            by_owner = [grads[n] for n in names]
            got = _sibling_swap("sibling_swap_" + names[0], by_owner, collective_id=next(ids), after=after)
            sums = _pair_sum("pair_sum_" + names[0], by_owner, got, core, after=sum_after)
            return sums, got

        @staticmethod
        def finish(grads, chip_sums, after=()):
            names = list(grads)
            parts = _chip_exchange("chip_exchange_" + names[0], chip_sums, collective_id=next(ids), after=after)
            for n, p in zip(names, parts):
                outs[n] = _adamw_sum("adamw_" + n, big[n], p, big_m[n], big_v[n])
            return parts, [outs[n][1] for n in names]

    gains = (norm_mix_g, hg_norm_g, norm_ffn_g, norm_final_g.reshape(1, D_MODEL))
    grad_x, small, last = _local_step(x[0], loss_target[0], gains, lower_bounds, conv_full, wg8, Reduce)

    small_all = _all_gather("gather_small", [_small_pack(small)], collective_id=next(ids), after=last[:1])

    def small_state(a):
        return [a[0], a[1], a[2].reshape(1, D_MODEL), a[3], a[4], a[5][0]]

    upd = _small_update(
        small_all[0], my_dev.reshape(1).astype(jnp.int32),
        small_state((norm_mix_g, norm_ffn_g, norm_final_g, lower_bounds, hg_norm_g, conv_w)),
        small_state((m_norm_mix_g, m_norm_ffn_g, m_norm_final_g, m_lower_bounds, m_hg_norm_g, m_conv_w)),
        small_state((v_norm_mix_g, v_norm_ffn_g, v_norm_final_g, v_lower_bounds, v_hg_norm_g, v_conv_w)))
    loss = upd[0][0, 0]
    small_shape = dict(norm_final_g=(D_MODEL,), conv_w=(1, 3, CONV_SHARD))
    for p, (name, _, _, _) in enumerate(_SMALL_PARAMS):
        outs[name] = [a.reshape(small_shape.get(name, a.shape)) for a in upd[1 + 4 * p:5 + 4 * p]]

    order = ["norm_mix_g", "w_in", "lower_bounds", "hg_norm_g", "conv_w", "w_branch_a", "w_branch_b", "w_out",
             "norm_ffn_g", "w_ffn_gate", "w_ffn_up", "w_ffn_down", "norm_final_g"]
    result = [loss, grad_x[None]]
    for k in range(4):
        for n in order:
            if n in BIG:
                result.append((outs[n][k].T if n in transposed else outs[n][k])[None])
            else:
                result.append(outs[n][k])
    return tuple(result)
```

```python
import jax
import jax.numpy as jnp
from jax import lax
from jax.experimental import pallas as pl
from jax.experimental.pallas import tpu as pltpu
from jax.experimental.pallas import tpu_sc as plsc

F32 = jnp.float32
BF16 = jnp.bfloat16
STASH = jnp.bfloat16

D_MODEL = 1024
HG_WIDTH = 512
HEAD_DIM = 128
N_HEADS = 4
HEADS_PER_STEP = 4
HEAD_GROUPS = N_HEADS // HEADS_PER_STEP
CONV_WIDTH = 512
CONV_K = 3
D_FF = 2816
CHUNK = 32
EPS = 1e-6
Q_SCALE = HEAD_DIM ** -0.5
N_DEV = 8

ADAM_LR = 0.001
ADAM_B1 = 0.9
ADAM_B2 = 0.999
ADAM_EPS = 1e-08
ADAM_WD = 0.01
ADAM_STEP = 10

VMEM_LIMIT_V7X = 56 * 1024 * 1024

SMALL_ROWS = 16


def _params(sem, vmem=VMEM_LIMIT_V7X):
    return pltpu.CompilerParams(dimension_semantics=sem, vmem_limit_bytes=vmem)


def _mm(a, b):
    return jnp.dot(a.astype(BF16), b.astype(BF16), preferred_element_type=F32)


def _mm_nt(a, b):
    return lax.dot_general(a.astype(BF16), b.astype(BF16), (((1,), (1,)), ((), ())), preferred_element_type=F32)


def _mm_tn(a, b):
    return lax.dot_general(a.astype(BF16), b.astype(BF16), (((0,), (0,)), ((), ())), preferred_element_type=F32)


def _sigmoid(x):
    return 0.5 * jnp.tanh(0.5 * x) + 0.5


def _resident(shape):
    nd = len(shape)
    return pl.BlockSpec(shape, lambda *_: (0,) * nd, pipeline_mode=pl.Buffered(1))


def _full(shape):
    nd = len(shape)
    return pl.BlockSpec(shape, lambda *_: (0,) * nd)


def _shard_cols(w_ref):
    return jnp.concatenate([w_ref[s] for s in range(N_DEV)], axis=1)


N_HG = 4 * HG_WIDTH
N_CV = 3 * CONV_WIDTH
N_GT = 2 * D_MODEL
N_IN = N_HG + N_CV + N_GT


def _col(tm, n):
    return pl.BlockSpec((n, tm), lambda i: (0, i))


HALO = 8


def _fwd_in(x, g, w_in_t, conv_w):
    T = x.shape[0]
    tm = min(512, T)

    def body(x_ref, g_ref, w_ref, cw_ref, ht_ref, hg_ref, cv_ref, gt_ref, cvo_ref, cvot_ref, tail_scr):
        @pl.when(pl.program_id(0) == 0)
        def _():
            tail_scr[...] = jnp.zeros_like(tail_scr)

        xv = x_ref[...]
        r = lax.rsqrt(jnp.mean(xv * xv, axis=-1, keepdims=True) + EPS)
        hf = xv * r * g_ref[...]
        h = hf.astype(BF16)
        ht_ref[...] = hf.T.astype(BF16)
        hg_ref[...] = _mm_nt(h, w_ref[:N_HG, :])
        cv = _mm_nt(h, w_ref[N_HG:N_HG + N_CV, :])
        cv_ref[...] = cv.astype(STASH)
        gt_ref[...] = _mm_nt(h, w_ref[N_HG + N_CV:, :]).astype(STASH)

        u = cv[:, :CONV_WIDTH] * cv[:, 2 * CONV_WIDTH:]
        row = lax.broadcasted_iota(jnp.int32, u.shape, 0)
        prev1 = tail_scr[HALO - 1:HALO, :]
        prev2 = tail_scr[HALO - 2:HALO - 1, :]
        u1 = jnp.where(row >= 1, pltpu.roll(u, 1, 0), prev1)
        u2 = jnp.where(row >= 2, pltpu.roll(u, 2, 0), jnp.where(row == 1, prev1, prev2))
        y = cw_ref[0:1, :] * u2 + cw_ref[1:2, :] * u1 + cw_ref[2:3, :] * u
        out = cv[:, CONV_WIDTH:2 * CONV_WIDTH] * y
        cvo_ref[...] = out.astype(BF16)
        cvot_ref[...] = out.T.astype(BF16)
        tail_scr[...] = u[tm - HALO:, :]

    row = lambda n: pl.BlockSpec((tm, n), lambda i: (i, 0))
    return pl.pallas_call(
        body, name="fwd_in", grid=(T // tm,),
        in_specs=[row(D_MODEL), _full((1, D_MODEL)), _resident(w_in_t.shape), _full((CONV_K, CONV_WIDTH))],
        out_specs=[_col(tm, D_MODEL), row(N_HG), row(N_CV), row(N_GT), row(CONV_WIDTH), _col(tm, CONV_WIDTH)],
        out_shape=[jax.ShapeDtypeStruct((D_MODEL, T), BF16), jax.ShapeDtypeStruct((T, N_HG), F32),
                   jax.ShapeDtypeStruct((T, N_CV), STASH), jax.ShapeDtypeStruct((T, N_GT), STASH),
                   jax.ShapeDtypeStruct((T, CONV_WIDTH), BF16), jax.ShapeDtypeStruct((CONV_WIDTH, T), BF16)],
        scratch_shapes=[pltpu.VMEM((HALO, CONV_WIDTH), F32)],
        compiler_params=_params(("arbitrary",)),
    )(x, g, w_in_t, conv_w)


def _chunk_pos(shape):
    return lax.broadcasted_iota(jnp.int32, shape, 0) & (CHUNK - 1)


def _chunk_cumsum(x, pos):
    s = 1
    while s < CHUNK:
        x = x + jnp.where(pos >= s, pltpu.roll(x, s, 0), 0.0)
        s *= 2
    return x


def _chunk_rev_cumsum(x, pos):
    n = x.shape[0]
    s = 1
    while s < CHUNK:
        x = x + jnp.where(pos + s < CHUNK, pltpu.roll(x, n - s, 0), 0.0)
        s *= 2
    return x


def _chunk_bcast(x3, row, tb):
    return jnp.broadcast_to(x3[:, row:row + 1, :], x3.shape).reshape(tb, x3.shape[-1])


def _lower_bound(low_ref):
    l0 = low_ref[0:1, :]
    l1 = low_ref[1:2, :]
    m = jnp.maximum(l0, l1)
    e0 = jnp.exp(l0 - m)
    e1 = jnp.exp(l1 - m)
    return e0 / (e0 + e1), e1 / (e0 + e1)


def _hg_gates(qr, fr, lb, pos, tb):
    sq = _sigmoid(qr)
    q = qr * sq * Q_SCALE
    sg = _sigmoid(fr)
    f = lb + (1.0 - lb) * sg
    k = 1.0 - f
    b = _chunk_cumsum(jnp.log(f), pos)
    b3 = b.reshape(tb // CHUNK, CHUNK, HEAD_DIM)
    anc = _chunk_bcast(b3, CHUNK // 2 - 1, tb)
    blb = _chunk_bcast(b3, CHUNK - 1, tb)
    e_qa = jnp.exp(b - anc)
    e_ka = jnp.exp(anc - b)
    e_b = jnp.exp(b)
    e_ko = jnp.exp(blb - b)
    dec = jnp.exp(blb)
    return sq, q, sg, f, k, e_qa, e_ka, e_b, e_ko, dec


def _intra_mask(sb):
    r = lax.broadcasted_iota(jnp.int32, (sb, sb), 0)
    c = lax.broadcasted_iota(jnp.int32, (sb, sb), 1)
    return ((r // CHUNK) == (c // CHUNK)) & (c <= r)


def _hg_fwd(hg, low, gn):
    T = hg.shape[0]
    tb = min(1024, T)
    sb = min(256, tb)
    nb = T // tb
    nc = tb // CHUNK
    wid = HEADS_PER_STEP * HEAD_DIM

    def body(q_ref, f_ref, i_ref, g_ref, low_ref, gn_ref, o_ref, og_ref, ogt_ref, st_ref, s_scr):
        t = pl.program_id(1)

        @pl.when(t == 0)
        def _():
            s_scr[...] = jnp.zeros_like(s_scr)

        pos = _chunk_pos((tb, HEAD_DIM))
        mask = _intra_mask(sb)
        lanes = [slice(hh * HEAD_DIM, (hh + 1) * HEAD_DIM) for hh in range(HEADS_PER_STEP)]
        qi, ko, vb, dec, st = [], [], [], [], []
        for hh, ln in enumerate(lanes):
            lb, _ = _lower_bound(low_ref.at[:, ln])
            _, q, _, _, k, e_qa, e_ka, e_b, e_ko, dec_h = _hg_gates(q_ref[:, ln], f_ref[:, ln], lb, pos, tb)
            qh = (q * e_qa).astype(BF16)
            kh = (k * e_ka).astype(BF16)
            qi.append((q * e_b).astype(BF16))
            ko.append((k * e_ko).astype(BF16))
            vb.append(i_ref[:, ln].astype(BF16))
            dec.append(dec_h)
            st.append(s_scr[hh])
            for s in range(tb // sb):
                sl = slice(s * sb, (s + 1) * sb)
                p = jnp.where(mask, _mm_nt(qh[sl], kh[sl]), 0.0)
                o_ref[sl, ln] = _mm(p, vb[hh][sl])
        for c in range(nc):
            sl = slice(c * CHUNK, (c + 1) * CHUNK)
            for hh, ln in enumerate(lanes):
                st_ref[hh, c] = st[hh]
                o_ref[sl, ln] = o_ref[sl, ln] + _mm_nt(qi[hh][sl], st[hh])
                st[hh] = dec[hh][c * CHUNK:c * CHUNK + 1, :] * st[hh] + _mm_tn(vb[hh][sl], ko[hh][sl])
        for hh, ln in enumerate(lanes):
            s_scr[hh] = st[hh]
            o = o_ref[:, ln]
            r = lax.rsqrt(jnp.mean(o * o, axis=-1, keepdims=True) + EPS)
            gr = g_ref[:, ln]
            og = (o * r * gn_ref[...]) * (gr * _sigmoid(gr))
            og_ref[:, ln] = og.astype(BF16)
            ogt_ref[ln, :] = og.T.astype(BF16)

    col = lambda p: pl.BlockSpec((tb, wid), lambda h, t: (t, p * HEAD_GROUPS + h))
    hcol = pl.BlockSpec((tb, wid), lambda h, t: (t, h))
    return pl.pallas_call(
        body, name="hg_fwd", grid=(HEAD_GROUPS, nb),
        in_specs=[col(0), col(1), col(2), col(3), pl.BlockSpec((2, wid), lambda h, t: (0, h)),
                  pl.BlockSpec((1, HEAD_DIM), lambda h, t: (0, 0))],
        out_specs=[hcol, hcol, pl.BlockSpec((wid, tb), lambda h, t: (h, t)),
                   pl.BlockSpec((HEADS_PER_STEP, nc, HEAD_DIM, HEAD_DIM), lambda h, t: (h, t, 0, 0))],
        out_shape=[jax.ShapeDtypeStruct((T, HG_WIDTH), F32), jax.ShapeDtypeStruct((T, HG_WIDTH), BF16),
                   jax.ShapeDtypeStruct((HG_WIDTH, T), BF16),
                   jax.ShapeDtypeStruct((N_HEADS, T // CHUNK, HEAD_DIM, HEAD_DIM), F32)],
        scratch_shapes=[pltpu.VMEM((HEADS_PER_STEP, HEAD_DIM, HEAD_DIM), F32)],
        compiler_params=_params(("parallel", "arbitrary")),
    )(hg, hg, hg, hg, low, gn)


def _merge_fwd(og, cvo, gt, x, wa, wb, wo):
    T = x.shape[0]
    tm = min(1024, T)

    def body(og_ref, cvo_ref, gt_ref, x_ref, wa_ref, wb_ref, wo_ref, x1_ref, mgt_ref):
        ya = jnp.dot(og_ref[...], _shard_cols(wa_ref), preferred_element_type=F32)
        yb = jnp.dot(cvo_ref[...], _shard_cols(wb_ref), preferred_element_type=F32)
        m = (_sigmoid(gt_ref[:, :D_MODEL].astype(F32)) * ya
             + _sigmoid(gt_ref[:, D_MODEL:].astype(F32)) * yb)
        mgt_ref[...] = m.T.astype(BF16)
        x1_ref[...] = x_ref[...] + jnp.dot(m.astype(BF16), wo_ref[...], preferred_element_type=F32)

    row = lambda n: pl.BlockSpec((tm, n), lambda i: (i, 0))
    return pl.pallas_call(
        body, name="merge_fwd", grid=(T // tm,),
        in_specs=[row(HG_WIDTH), row(CONV_WIDTH), row(2 * D_MODEL), row(D_MODEL),
                  _resident(wa.shape), _resident(wb.shape), _resident(wo.shape)],
        out_specs=[row(D_MODEL), _col(tm, D_MODEL)],
        out_shape=[jax.ShapeDtypeStruct((T, D_MODEL), F32), jax.ShapeDtypeStruct((D_MODEL, T), BF16)],
        compiler_params=_params(("parallel",)),
    )(og, cvo, gt, x, wa, wb, wo)


def _ffn_fwd_loss(x1, g, wg, wu, wd, target, g_fin):
    T = x1.shape[0]
    tm = min(256, T)

    def body(x_ref, g_ref, wg_ref, wu_ref, wd_ref, t_ref, gf_ref,
             ht_ref, gate_ref, up_ref, actt_ref, loss_ref, dgf_ref, dx2_ref):
        @pl.when(pl.program_id(0) == 0)
        def _():
            loss_ref[...] = jnp.zeros_like(loss_ref)
            dgf_ref[...] = jnp.zeros_like(dgf_ref)

        xv = x_ref[...]
        r = lax.rsqrt(jnp.mean(xv * xv, axis=-1, keepdims=True) + EPS)
        hf = xv * r * g_ref[...]
        h = hf.astype(BF16)
        ht_ref[...] = hf.T.astype(BF16)
        gate = _mm_nt(h, wg_ref[...])
        up = _mm_nt(h, wu_ref[...])
        gate_ref[...] = gate.astype(STASH)
        up_ref[...] = up.astype(STASH)
        act = gate * _sigmoid(gate) * up
        actt_ref[...] = act.T.astype(BF16)
        x2 = xv + jnp.dot(act.astype(BF16), wd_ref[...], preferred_element_type=F32)

        gv = gf_ref[...]
        r2 = lax.rsqrt(jnp.mean(x2 * x2, axis=-1, keepdims=True) + EPS)
        xh = x2 * r2
        err = xh * gv - t_ref[...]
        loss_ref[...] += 0.5 * jnp.sum(jnp.mean(err * err, axis=-1, keepdims=True), axis=0, keepdims=True)
        dy = err * (1.0 / D_MODEL)
        dgf_ref[...] += jnp.sum(dy * xh, axis=0, keepdims=True)
        w = dy * gv
        dx2_ref[...] = r2 * (w - xh * jnp.mean(w * xh, axis=-1, keepdims=True))

    row = lambda n: pl.BlockSpec((tm, n), lambda i: (i, 0))
    return pl.pallas_call(
        body, name="ffn_fwd_loss", grid=(T // tm,),
        in_specs=[row(D_MODEL), _full((1, D_MODEL)), _resident(wg.shape), _resident(wu.shape), _resident(wd.shape),
                  row(D_MODEL), _full((1, D_MODEL))],
        out_specs=[_col(tm, D_MODEL), row(D_FF), row(D_FF), _col(tm, D_FF), _full((1, 128)), _full((1, D_MODEL)),
                   row(D_MODEL)],
        out_shape=[jax.ShapeDtypeStruct((D_MODEL, T), BF16), jax.ShapeDtypeStruct((T, D_FF), STASH),
                   jax.ShapeDtypeStruct((T, D_FF), STASH), jax.ShapeDtypeStruct((D_FF, T), BF16),
                   jax.ShapeDtypeStruct((1, 128), F32), jax.ShapeDtypeStruct((1, D_MODEL), F32),
                   jax.ShapeDtypeStruct((T, D_MODEL), F32)],
        compiler_params=_params(("arbitrary",)),
    )(x1, g, wg, wu, wd, target, g_fin)


def _ffn_bwd(dx2, x1, gate, up, g, wg, wu, wd):
    T = x1.shape[0]
    tm = min(256, T)

    def body(dx2_ref, x_ref, gate_ref, up_ref, g_ref, wg_ref, wu_ref, wd_ref, dgate_ref, dup_ref, dx1_ref, dgn_ref):
        @pl.when(pl.program_id(0) == 0)
        def _():
            dgn_ref[...] = jnp.zeros_like(dgn_ref)

        dx2 = dx2_ref[...]
        dact = _mm_nt(dx2, wd_ref[...])
        gate = gate_ref[...].astype(F32)
        s = _sigmoid(gate)
        dgate = (dact * up_ref[...].astype(F32) * (s * (1.0 + gate * (1.0 - s)))).astype(BF16)
        dup = (dact * (gate * s)).astype(BF16)
        dgate_ref[...] = dgate
        dup_ref[...] = dup
        dh = _mm(dgate, wg_ref[...]) + _mm(dup, wu_ref[...])
        xv = x_ref[...]
        r = lax.rsqrt(jnp.mean(xv * xv, axis=-1, keepdims=True) + EPS)
        xh = xv * r
        dgn_ref[...] += jnp.sum(dh * xh, axis=0, keepdims=True)
        w = dh * g_ref[...]
        dx1_ref[...] = dx2 + r * (w - xh * jnp.mean(w * xh, axis=-1, keepdims=True))

    row = lambda n: pl.BlockSpec((tm, n), lambda i: (i, 0))
    return pl.pallas_call(
        body, name="ffn_bwd", grid=(T // tm,),
        in_specs=[row(D_MODEL), row(D_MODEL), row(D_FF), row(D_FF), _full((1, D_MODEL)),
                  _resident(wg.shape), _resident(wu.shape), _resident(wd.shape)],
        out_specs=[row(D_FF), row(D_FF), row(D_MODEL), _full((1, D_MODEL))],
        out_shape=[jax.ShapeDtypeStruct((T, D_FF), BF16), jax.ShapeDtypeStruct((T, D_FF), BF16),
                   jax.ShapeDtypeStruct((T, D_MODEL), F32), jax.ShapeDtypeStruct((1, D_MODEL), F32)],
        compiler_params=_params(("arbitrary",)),
    )(dx2, x1, gate, up, g, wg, wu, wd)


def _merge_bwd(dx1, og, cvo, gt, cv, wa, wb, wo, conv_w):
    T = dx1.shape[0]
    tm = min(512, T)
    nt = T // tm

    def body(dx_ref, og_ref, cvo_ref, gt_ref, cv_ref, halo_ref, wa_ref, wb_ref, wo_ref, cw_ref,
             dgt_ref, dya_ref, dyb_ref, dog_ref, dcv_ref, dcw_ref, prev_u, next_dy):
        step = pl.program_id(0)

        @pl.when(step == 0)
        def _():
            next_dy[...] = jnp.zeros_like(next_dy)
            dcw_ref[...] = jnp.zeros_like(dcw_ref)

        dm = _mm_nt(dx_ref[...], wo_ref[...])
        wa = _shard_cols(wa_ref)
        wb = _shard_cols(wb_ref)
        ya = jnp.dot(og_ref[...], wa, preferred_element_type=F32)
        yb = jnp.dot(cvo_ref[...], wb, preferred_element_type=F32)
        sa = _sigmoid(gt_ref[:, :D_MODEL].astype(F32))
        sb = _sigmoid(gt_ref[:, D_MODEL:].astype(F32))
        dgt_ref[:, :D_MODEL] = (dm * ya * (sa * (1.0 - sa))).astype(BF16)
        dgt_ref[:, D_MODEL:] = (dm * yb * (sb * (1.0 - sb))).astype(BF16)
        dya = (dm * sa).astype(BF16)
        dyb = (dm * sb).astype(BF16)
        dya_ref[...] = dya
        dyb_ref[...] = dyb
        dog_ref[...] = _mm_nt(dya, wa)
        dcvo = _mm_nt(dyb, wb)

        cvt = cv_ref[...].astype(F32)
        c, bg, xb = cvt[:, :CONV_WIDTH], cvt[:, CONV_WIDTH:2 * CONV_WIDTH], cvt[:, 2 * CONV_WIDTH:]
        halo = halo_ref[...].astype(F32)
        first_tile = step == nt - 1
        prev_u[...] = jnp.where(first_tile, 0.0, halo[:, :CONV_WIDTH] * halo[:, 2 * CONV_WIDTH:])
        u = c * xb
        row = lax.broadcasted_iota(jnp.int32, u.shape, 0)
        p1 = prev_u[HALO - 1:HALO, :]
        p2 = prev_u[HALO - 2:HALO - 1, :]
        u1 = jnp.where(row >= 1, pltpu.roll(u, 1, 0), p1)
        u2 = jnp.where(row >= 2, pltpu.roll(u, 2, 0), jnp.where(row == 1, p1, p2))
        w0, w1, w2 = cw_ref[0:1, :], cw_ref[1:2, :], cw_ref[2:3, :]
        y = w0 * u2 + w1 * u1 + w2 * u
        dcv_ref[:, CONV_WIDTH:2 * CONV_WIDTH] = (dcvo * y).astype(BF16)
        dy = dcvo * bg
        dcw_ref[0:1, :] += jnp.sum(dy * u2, axis=0, keepdims=True)
        dcw_ref[1:2, :] += jnp.sum(dy * u1, axis=0, keepdims=True)
        dcw_ref[2:3, :] += jnp.sum(dy * u, axis=0, keepdims=True)
        n1 = next_dy[0:1, :]
        n2 = next_dy[1:2, :]
        dy1 = jnp.where(row < tm - 1, pltpu.roll(dy, tm - 1, 0), n1)
        dy2 = jnp.where(row < tm - 2, pltpu.roll(dy, tm - 2, 0), jnp.where(row == tm - 2, n1, n2))
        du = w2 * dy + w1 * dy1 + w0 * dy2
        dcv_ref[:, :CONV_WIDTH] = (du * xb).astype(BF16)
        dcv_ref[:, 2 * CONV_WIDTH:] = (du * c).astype(BF16)
        next_dy[...] = dy[:HALO, :]

    rt = lambda i: nt - 1 - i
    row = lambda n: pl.BlockSpec((tm, n), lambda i: (rt(i), 0))
    halo = pl.BlockSpec((HALO, N_CV), lambda i: (jnp.maximum(rt(i) * (tm // HALO) - 1, 0), 0))
    return pl.pallas_call(
        body, name="merge_bwd", grid=(nt,),
        in_specs=[row(D_MODEL), row(HG_WIDTH), row(CONV_WIDTH), row(2 * D_MODEL), row(N_CV), halo,
                  _resident(wa.shape), _resident(wb.shape), _resident(wo.shape), _full((CONV_K, CONV_WIDTH))],
        out_specs=[row(2 * D_MODEL), row(D_MODEL), row(D_MODEL), row(HG_WIDTH), row(N_CV),
                   _full((CONV_K, CONV_WIDTH))],
        out_shape=[jax.ShapeDtypeStruct((T, 2 * D_MODEL), BF16), jax.ShapeDtypeStruct((T, D_MODEL), BF16),
                   jax.ShapeDtypeStruct((T, D_MODEL), BF16), jax.ShapeDtypeStruct((T, HG_WIDTH), F32),
                   jax.ShapeDtypeStruct((T, N_CV), BF16), jax.ShapeDtypeStruct((CONV_K, CONV_WIDTH), F32)],
        scratch_shapes=[pltpu.VMEM((HALO, CONV_WIDTH), F32), pltpu.VMEM((HALO, CONV_WIDTH), F32)],
        compiler_params=_params(("arbitrary",)),
    )(dx1, og, cvo, gt, cv, cv, wa, wb, wo, conv_w)


def _drop_operands(body, first, count):
    def wrapped(*refs):
        return body(*refs[:first], *refs[first + count:])
    return wrapped


def _hg_bwd(dog, hg, o, st, low, gn, after=()):
    T = hg.shape[0]
    tb = min(512, T)
    sb = min(256, tb)
    nb = T // tb
    nc = tb // CHUNK
    wid = HEADS_PER_STEP * HEAD_DIM

    def body(q_ref, f_ref, i_ref, g_ref, low_ref, gn_ref, o_ref, dog_ref, st_ref,
             dq_ref, df_ref, di_ref, dg_ref, dlow_ref, dgn_ref,
             ds_scr, dqi_scr, dko_scr, dv_scr, dd_scr, dqh_scr, dkh_scr):
        h = pl.program_id(0)
        t = pl.program_id(1)

        @pl.when(t == 0)
        def _():
            ds_scr[...] = jnp.zeros_like(ds_scr)
            dlow_ref[...] = jnp.zeros_like(dlow_ref)

        @pl.when((t == 0) & (h == 0))
        def _():
            dgn_ref[...] = jnp.zeros_like(dgn_ref)

        pos = _chunk_pos((tb, HEAD_DIM))
        mask = _intra_mask(sb)
        gnv = gn_ref[...]
        lanes = [slice(hh * HEAD_DIM, (hh + 1) * HEAD_DIM) for hh in range(HEADS_PER_STEP)]
        heads = []
        for hh, ln in enumerate(lanes):
            lb, lb1 = _lower_bound(low_ref.at[:, ln])
            qr = q_ref[:, ln]
            sq, q, sg, f, k, e_qa, e_ka, e_b, e_ko, dec = _hg_gates(qr, f_ref[:, ln], lb, pos, tb)

            gr = g_ref[:, ln]
            o = o_ref[:, ln]
            dog_v = dog_ref[:, ln]
            sgr = _sigmoid(gr)
            r = lax.rsqrt(jnp.mean(o * o, axis=-1, keepdims=True) + EPS)
            oh = o * r
            dg_ref[:, ln] = (dog_v * (oh * gnv) * (sgr * (1.0 + gr * (1.0 - sgr)))).astype(BF16)
            don = dog_v * (gr * sgr)
            dgn_ref[...] += jnp.sum(don * oh, axis=0, keepdims=True)
            w = don * gnv
            do = (r * (w - oh * jnp.mean(w * oh, axis=-1, keepdims=True))).astype(BF16)

            qh = (q * e_qa).astype(BF16)
            kh = (k * e_ka).astype(BF16)
            qi = (q * e_b).astype(BF16)
            ko = (k * e_ko).astype(BF16)
            vb = i_ref[:, ln].astype(BF16)

            for s in range(tb // sb):
                sl = slice(s * sb, (s + 1) * sb)
                p = jnp.where(mask, _mm_nt(qh[sl], kh[sl]), 0.0).astype(BF16)
                dp = jnp.where(mask, _mm_nt(do[sl], vb[sl]), 0.0).astype(BF16)
                dv_scr[sl, ln] = _mm_tn(p, do[sl])
                dqh_scr[sl, ln] = _mm(dp, kh[sl])
                dkh_scr[sl, ln] = _mm_tn(dp, qh[sl])
            heads.append(dict(lb=lb, lb1=lb1, qr=qr, sq=sq, q=q, sg=sg, f=f, k=k, e_qa=e_qa, e_ka=e_ka, e_b=e_b,
                              e_ko=e_ko, dec=dec, do=do, qi=qi, ko=ko, vb=vb, ds=ds_scr[hh]))

        for c in reversed(range(nc)):
            sl = slice(c * CHUNK, (c + 1) * CHUNK)
            for hh, ln in enumerate(lanes):
                hd = heads[hh]
                ds = hd["ds"]
                st_c = st_ref[hh, c]
                dqi_scr[sl, ln] = _mm(hd["do"][sl], st_c)
                dko_scr[sl, ln] = _mm(hd["vb"][sl], ds)
                dv_scr[sl, ln] = dv_scr[sl, ln] + _mm_nt(hd["ko"][sl], ds)
                dd_scr[sl, ln] = jnp.broadcast_to(jnp.sum(ds * st_c, axis=0, keepdims=True), (CHUNK, HEAD_DIM))
                hd["ds"] = hd["dec"][c * CHUNK:c * CHUNK + 1, :] * ds + _mm_tn(hd["do"][sl], hd["qi"][sl])

        for hh, ln in enumerate(lanes):
            hd = heads[hh]
            ds_scr[hh] = hd["ds"]
            q, k, lb = hd["q"], hd["k"], hd["lb"]
            dko_e = dko_scr[:, ln] * hd["e_ko"]
            dq = dqh_scr[:, ln] * hd["e_qa"] + dqi_scr[:, ln] * hd["e_b"]
            dk = dkh_scr[:, ln] * hd["e_ka"] + dko_e
            kd3 = (k * dko_e).reshape(nc, CHUNK, HEAD_DIM)
            last = jnp.broadcast_to(jnp.sum(kd3, axis=1, keepdims=True), kd3.shape).reshape(tb, HEAD_DIM)
            db = q * dq - k * dk + jnp.where(pos == CHUNK - 1, hd["dec"] * dd_scr[:, ln] + last, 0.0)
            dlg = _chunk_rev_cumsum(db, pos)
            dfv = dlg / hd["f"] - dk
            s_low = jnp.sum(dfv * (1.0 - hd["sg"]), axis=0, keepdims=True)
            dlow_ref[0:1, ln] += s_low * lb * (1.0 - lb)
            dlow_ref[1:2, ln] += -s_low * lb * hd["lb1"]
            df_ref[:, ln] = (dfv * (1.0 - lb) * hd["sg"] * (1.0 - hd["sg"])).astype(BF16)
            dq_ref[:, ln] = (dq * Q_SCALE * (hd["sq"] * (1.0 + hd["qr"] * (1.0 - hd["sq"])))).astype(BF16)
            di_ref[:, ln] = dv_scr[:, ln].astype(BF16)

    rt = lambda t: nb - 1 - t
    col = lambda p: pl.BlockSpec((tb, wid), lambda h, t: (rt(t), p * HEAD_GROUPS + h))
    hcol = pl.BlockSpec((tb, wid), lambda h, t: (rt(t), h))
    piece = jax.ShapeDtypeStruct((T, HG_WIDTH), BF16)
    tile = pltpu.VMEM((tb, wid), F32)
    return pl.pallas_call(
        _drop_operands(body, 9, len(after)), name="hg_bwd", grid=(HEAD_GROUPS, nb),
        in_specs=[col(0), col(1), col(2), col(3), pl.BlockSpec((2, wid), lambda h, t: (0, h)),
                  pl.BlockSpec((1, HEAD_DIM), lambda h, t: (0, 0)), hcol, hcol,
                  pl.BlockSpec((HEADS_PER_STEP, nc, HEAD_DIM, HEAD_DIM), lambda h, t: (h, rt(t), 0, 0))]
                 + [HBM_SPEC] * len(after),
        out_specs=[hcol, hcol, hcol, hcol, pl.BlockSpec((2, wid), lambda h, t: (0, h)),
                   pl.BlockSpec((1, HEAD_DIM), lambda h, t: (0, 0))],
        out_shape=[piece, piece, piece, piece, jax.ShapeDtypeStruct((2, HG_WIDTH), F32),
                   jax.ShapeDtypeStruct((1, HEAD_DIM), F32)],
        scratch_shapes=[pltpu.VMEM((HEADS_PER_STEP, HEAD_DIM, HEAD_DIM), F32), tile, tile, tile, tile, tile, tile],
        compiler_params=_params(("arbitrary", "arbitrary")),
    )(hg, hg, hg, hg, low, gn, o, dog, st, *after)


def _in_bwd(dparts, w_in, x, dx1, g, after=()):
    T = x.shape[0]
    tm = min(512, T)
    widths = [p.shape[1] for p in dparts]
    offs = [sum(widths[:i]) for i in range(len(widths))]
    n = len(dparts)

    def body(*refs):
        d_refs = refs[:n]
        w_ref, x_ref, dx1_ref, g_ref, dx_ref, dgn_ref = refs[n:]

        @pl.when(pl.program_id(0) == 0)
        def _():
            dgn_ref[...] = jnp.zeros_like(dgn_ref)

        dh = None
        for d_ref, off, wd in zip(d_refs, offs, widths):
            part = _mm(d_ref[...], w_ref[off:off + wd, :])
            dh = part if dh is None else dh + part
        xv = x_ref[...]
        r = lax.rsqrt(jnp.mean(xv * xv, axis=-1, keepdims=True) + EPS)
        xh = xv * r
        dgn_ref[...] += jnp.sum(dh * xh, axis=0, keepdims=True)
        w = dh * g_ref[...]
        dx_ref[...] = dx1_ref[...] + r * (w - xh * jnp.mean(w * xh, axis=-1, keepdims=True))

    row = lambda m: pl.BlockSpec((tm, m), lambda i: (i, 0))
    return pl.pallas_call(
        _drop_operands(body, n + 4, len(after)), name="in_bwd", grid=(T // tm,),
        in_specs=[row(wd) for wd in widths] + [_resident(w_in.shape), row(D_MODEL), row(D_MODEL), _full((1, D_MODEL))]
                 + [HBM_SPEC] * len(after),
        out_specs=[row(D_MODEL), _full((1, D_MODEL))],
        out_shape=[jax.ShapeDtypeStruct((T, D_MODEL), F32), jax.ShapeDtypeStruct((1, D_MODEL), F32)],
        compiler_params=_params(("arbitrary",)),
    )(*dparts, w_in, x, dx1, g, *after)


def _wgrad(name, at, b, tn, transposed=False, tk=2048, after=()):
    M, T = at.shape
    N = b.shape[1]
    tk = min(tk, T)
    nk = T // tk

    def body(a_ref, b_ref, o_ref, acc):
        k = pl.program_id(1)
        part = _mm(a_ref[...], b_ref[...])

        @pl.when(k == 0)
        def _():
            acc[...] = part

        @pl.when(k != 0)
        def _():
            acc[...] += part

        @pl.when(k == nk - 1)
        def _():
            o_ref[...] = (acc[...].T if transposed else acc[...]).astype(BF16)

    if transposed:
        out_spec, out_shape = pl.BlockSpec((tn, M), lambda j, k: (j, 0)), (N, M)
    else:
        out_spec, out_shape = pl.BlockSpec((M, tn), lambda j, k: (0, j)), (M, N)
    return pl.pallas_call(
        _drop_operands(body, 2, len(after)), name=name, grid=(N // tn, nk),
        in_specs=[pl.BlockSpec((M, tk), lambda j, k: (0, k)), pl.BlockSpec((tk, tn), lambda j, k: (k, j))]
                 + [HBM_SPEC] * len(after),
        out_specs=out_spec, out_shape=jax.ShapeDtypeStruct(out_shape, BF16),
        scratch_shapes=[pltpu.VMEM((M, tn), F32)],
        compiler_params=_params(("parallel", "arbitrary")),
    )(at, b, *after)


def _wgrad_branches(ogt, dya, cvot, dyb):
    M, T = ogt.shape
    N = dya.shape[1]
    tk = min(1024, T)
    nk = T // tk
    c = N // N_DEV

    def body(at_ref, da_ref, bt_ref, db_ref, oa_ref, ob_ref, acc_a, acc_b):
        k = pl.program_id(0)

        @pl.when(k == 0)
        def _():
            acc_a[...] = jnp.zeros_like(acc_a)
            acc_b[...] = jnp.zeros_like(acc_b)

        acc_a[...] += _mm(at_ref[...], da_ref[...])
        acc_b[...] += _mm(bt_ref[...], db_ref[...])

        @pl.when(k == nk - 1)
        def _():
            for s in range(N_DEV):
                oa_ref[s] = acc_a[:, s * c:(s + 1) * c].astype(BF16)
                ob_ref[s] = acc_b[:, s * c:(s + 1) * c].astype(BF16)

    lhs = pl.BlockSpec((M, tk), lambda k: (0, k))
    rhs = pl.BlockSpec((tk, N), lambda k: (k, 0))
    out = jax.ShapeDtypeStruct((N_DEV, M, c), BF16)
    return pl.pallas_call(
        body, name="wgrad_branches", grid=(nk,),
        in_specs=[lhs, rhs, lhs, rhs], out_specs=[_full((N_DEV, M, c))] * 2, out_shape=[out, out],
        scratch_shapes=[pltpu.VMEM((M, N), F32)] * 2,
        compiler_params=_params(("arbitrary",)),
    )(ogt, dya, cvot, dyb)


def _wgrad_in(ht, dparts, after=()):
    M, T = ht.shape
    tn = 512
    tk = min(2048, T)
    nk = T // tk
    nblk = [p.shape[1] // tn for p in dparts]
    start = [sum(nblk[:i]) for i in range(len(nblk))]
    n = len(dparts)

    def body(a_ref, *refs):
        d_refs, o_ref, acc = refs[:n], refs[n], refs[n + 1]
        j = pl.program_id(0)
        k = pl.program_id(1)

        @pl.when(k == 0)
        def _():
            acc[...] = jnp.zeros_like(acc)

        for d_ref, s, nb in zip(d_refs, start, nblk):
            @pl.when((j >= s) & (j < s + nb))
            def _():
                acc[...] += _mm(a_ref[...], d_ref[...])

        @pl.when(k == nk - 1)
        def _():
            o_ref[...] = acc[...].T.astype(BF16)

    def piece_spec(s, nb):
        def index(j, k):
            inside = (j >= s) & (j < s + nb)
            return jnp.where(inside, k, 0), jnp.clip(j - s, 0, nb - 1)
        return pl.BlockSpec((tk, tn), index)

    return pl.pallas_call(
        _drop_operands(body, 1 + n, len(after)), name="wgrad_in", grid=(sum(nblk), nk),
        in_specs=[pl.BlockSpec((M, tk), lambda j, k: (0, k))] + [piece_spec(s, nb) for s, nb in zip(start, nblk)]
                 + [HBM_SPEC] * len(after),
        out_specs=pl.BlockSpec((tn, M), lambda j, k: (j, 0)),
        out_shape=jax.ShapeDtypeStruct((sum(nblk) * tn, M), BF16),
        scratch_shapes=[pltpu.VMEM((M, tn), F32)],
        compiler_params=_params(("parallel", "arbitrary")),
    )(ht, *dparts, *after)


def _adamw_math(w, g, m, v):
    m = ADAM_B1 * m + (1.0 - ADAM_B1) * g
    v = ADAM_B2 * v + (1.0 - ADAM_B2) * (g * g)
    m_hat = m / (1.0 - ADAM_B1 ** ADAM_STEP)
    v_hat = v / (1.0 - ADAM_B2 ** ADAM_STEP)
    delta = -ADAM_LR * (m_hat / (jnp.sqrt(v_hat) + ADAM_EPS) + ADAM_WD * w)
    return delta, m, v


def _adamw_sum(name, w, parts, m, v):
    R, C = w.shape
    tr = _row_tile(R)

    def body(w_ref, p_ref, m_ref, v_ref, g_out, d_out, m_out, v_out):
        g = p_ref[0].astype(F32)
        for k in range(1, 4):
            g = g + p_ref[k].astype(F32)
        g_out[...] = g
        d_out[...], m_out[...], v_out[...] = _adamw_math(w_ref[...], g, m_ref[...], v_ref[...])

    blk = pl.BlockSpec((tr, C), lambda i: (i, 0))
    out = jax.ShapeDtypeStruct((R, C), F32)
    return pl.pallas_call(
        body, name=name, grid=(R // tr,),
        in_specs=[blk, pl.BlockSpec((4, tr, C), lambda i: (0, i, 0)), blk, blk],
        out_specs=[blk, blk, blk, blk], out_shape=[out, out, out, out],
        compiler_params=_params(("parallel",)),
    )(w, parts, m, v)


_SMALL_SLOTS = (("norm_mix_g", 0, 1, 1024), ("norm_ffn_g", 1, 1, 1024), ("norm_final_g", 2, 1, 1024),
                ("lower_bounds", 3, 2, 512), ("hg_norm_g", 5, 1, 128), ("loss", 6, 1, 128), ("conv_w", 8, 3, 512))
_SMALL_PARAMS = tuple(s for s in _SMALL_SLOTS if s[0] != "loss")
CONV_SHARD = CONV_WIDTH // N_DEV


def _small_pack(small):
    def body(*refs):
        out = refs[-1]
        out[...] = jnp.zeros_like(out)
        for ref, (_, row, rows, lanes) in zip(refs[:-1], _SMALL_SLOTS):
            out[row:row + rows, 0:lanes] = ref[...]

    vmem = pl.BlockSpec(memory_space=pltpu.VMEM)
    return pl.pallas_call(
        body, name="small_pack", in_specs=[vmem] * len(_SMALL_SLOTS), out_specs=vmem,
        out_shape=jax.ShapeDtypeStruct((SMALL_ROWS, 1024), F32),
    )(*[small[name] for name, _, _, _ in _SMALL_SLOTS])


def _small_update(gathered, dev, w, m, v):
    n = len(_SMALL_PARAMS)

    def body(dev_ref, g_ref, *refs):
        w_refs, m_refs, v_refs = refs[:n], refs[n:2 * n], refs[2 * n:3 * n]
        loss_ref, out_refs, sum_scr = refs[3 * n], refs[3 * n + 1:-1], refs[-1]
        total = g_ref[0]
        for k in range(1, N_DEV):
            total = total + g_ref[k]
        sum_scr[...] = total
        loss_ref[...] = sum_scr[6:7, 0:128]
        for p, (name, row, rows, lanes) in enumerate(_SMALL_PARAMS):
            if name == "conv_w":
                g = sum_scr[row:row + rows, 0:CONV_SHARD]
                for s in range(1, N_DEV):
                    g = jnp.where(dev_ref[0] == s, sum_scr[row:row + rows, s * CONV_SHARD:(s + 1) * CONV_SHARD], g)
            else:
                g = sum_scr[row:row + rows, 0:lanes]
            delta, m_new, v_new = _adamw_math(w_refs[p][...], g, m_refs[p][...], v_refs[p][...])
            out_refs[4 * p][...] = g
            out_refs[4 * p + 1][...] = delta
            out_refs[4 * p + 2][...] = m_new
            out_refs[4 * p + 3][...] = v_new

    vmem = pl.BlockSpec(memory_space=pltpu.VMEM)
    outs = [jax.ShapeDtypeStruct((1, 128), F32)]
    for a in w:
        outs += [jax.ShapeDtypeStruct(a.shape, F32)] * 4
    return pl.pallas_call(
        body, name="small_update",
        in_specs=[pl.BlockSpec(memory_space=pltpu.SMEM)] + [vmem] * (1 + 3 * n), out_specs=[vmem] * len(outs),
        out_shape=outs, scratch_shapes=[pltpu.VMEM((SMALL_ROWS, 1024), F32)],
    )(dev, gathered, *w, *m, *v)


def _row_tile(rows):
    for parts in (4, 2):
        if rows % (16 * parts) == 0:
            return rows // parts
    return rows


def _pair_sum(name, by_owner, got, core, after=()):
    n = len(got)

    def body(core_ref, *refs):
        for a_ref, b_ref, o_ref in zip(refs[:n], refs[n:2 * n], refs[2 * n:]):
            o_ref[...] = (a_ref[...].astype(F32) + b_ref[...].astype(F32)).astype(BF16)

    def blk(g):
        return pl.BlockSpec((None,) + g.shape[1:], lambda k, core_ref: (k, 0, 0))

    def mine(g):
        return pl.BlockSpec((None,) + g.shape[1:], lambda k, core_ref: (2 * k + core_ref[0], 0, 0))

    return pl.pallas_call(
        _drop_operands(body, 1 + 2 * n, len(after)), name=name,
        grid_spec=pltpu.PrefetchScalarGridSpec(
            num_scalar_prefetch=1, grid=(4,),
            in_specs=[mine(g) for g in got] + [blk(g) for g in got] + [HBM_SPEC] * len(after),
            out_specs=[blk(g) for g in got]),
        out_shape=[jax.ShapeDtypeStruct(g.shape, BF16) for g in got],
        compiler_params=_params(("parallel",)),
    )(core, *by_owner, *got, *after)


MESH = pl.DeviceIdType.MESH
HBM_SPEC = pl.BlockSpec(memory_space=pl.ANY)


def _handshake(peers):
    barrier = pltpu.get_barrier_semaphore()
    for peer in peers:
        pl.semaphore_signal(barrier, inc=1, device_id=peer, device_id_type=MESH)
    pl.semaphore_wait(barrier, len(peers))


def _comm_call(body, name, operands, out_shape, scratch, collective_id):
    if collective_id is None:
        return pl.pallas_call(body, name=name, in_specs=[HBM_SPEC] * len(operands), out_specs=[HBM_SPEC] * len(out_shape),
                              out_shape=out_shape, scratch_shapes=scratch)(*operands)
    return pl.kernel(body, out_type=out_shape, mesh=plsc.ScalarSubcoreMesh(axis_name="sequencer", num_cores=1),
                     scratch_types=scratch, name=name,
                     compiler_params=pltpu.CompilerParams(collective_id=collective_id))(*operands)


def _all_gather(name, blocks, collective_id=None, after=()):
    n = len(blocks)
    na = len(after)

    def body(*refs):
        x_refs, out_refs = refs[:n], refs[n + na:2 * n + na]
        send_sems, recv_sems, local_sems = refs[2 * n + na:]
        x, y, c = lax.axis_index("x"), lax.axis_index("y"), lax.axis_index("c")
        me, sibling = (x, y, c), (x, y, 1 - c)
        chips = [(1 - x, y), (x, 1 - y), (1 - x, 1 - y)]
        if collective_id is not None:
            _handshake([sibling] + [(*chip, c) for chip in chips])

        def slot(i, px, py, pc):
            return out_refs[i].at[4 * px + 2 * py + pc]

        def copy(i, k, blk, to, src=None):
            return pltpu.make_async_remote_copy(
                src_ref=slot(i, *blk) if src is None else src, dst_ref=slot(i, *blk),
                send_sem=send_sems.at[7 * i + k], recv_sem=recv_sems.at[7 * i + k], device_id=to, device_id_type=MESH)

        mine = [pltpu.make_async_copy(x_refs[i], slot(i, *me), local_sems.at[i]) for i in range(n)]
        for cp in mine:
            cp.start()
        first = []
        for i in range(n):
            first.append(copy(i, 0, me, sibling, src=x_refs[i]))
            first += [copy(i, 1 + j, me, (*chip, c), src=x_refs[i]) for j, chip in enumerate(chips)]
        for cp in first:
            cp.start()
        passed = []
        for i in range(n):
            for j, chip in enumerate(chips):
                copy(i, 1 + j, (*chip, c), me).wait_recv()
                passed.append(copy(i, 4 + j, (*chip, c), sibling))
                passed[-1].start()
        for i in range(n):
            copy(i, 0, sibling, me).wait_recv()
            for j, chip in enumerate(chips):
                copy(i, 4 + j, (*chip, 1 - c), me).wait_recv()
        for cp in first + passed:
            cp.wait_send()
        for cp in mine:
            cp.wait()

    return _comm_call(
        body, name, list(blocks) + list(after), [jax.ShapeDtypeStruct((N_DEV,) + b.shape, b.dtype) for b in blocks],
        [pltpu.SemaphoreType.DMA((7 * n,)), pltpu.SemaphoreType.DMA((7 * n,)), pltpu.SemaphoreType.DMA((n,))],
        collective_id)


def _sibling_swap(name, by_owner, collective_id=None, after=()):
    n = len(by_owner)
    na = len(after)

    def body(*refs):
        x_refs, out_refs = refs[:n], refs[n + na:2 * n + na]
        send_sems, recv_sems = refs[2 * n + na:]
        x, y, c = lax.axis_index("x"), lax.axis_index("y"), lax.axis_index("c")
        if collective_id is not None:
            _handshake([(x, y, 1 - c)])
        copies = []
        for i in range(n):
            for k in range(4):
                copies.append(pltpu.make_async_remote_copy(
                    src_ref=x_refs[i].at[2 * k + 1 - c], dst_ref=out_refs[i].at[k],
                    send_sem=send_sems.at[4 * i + k], recv_sem=recv_sems.at[4 * i + k],
                    device_id=(x, y, 1 - c), device_id_type=MESH))
        for cp in copies:
            cp.start()
        for cp in copies:
            cp.wait()

    return _comm_call(
        body, name, list(by_owner) + list(after),
        [jax.ShapeDtypeStruct((4,) + b.shape[1:], b.dtype) for b in by_owner],
        [pltpu.SemaphoreType.DMA((4 * n,)), pltpu.SemaphoreType.DMA((4 * n,))], collective_id)


def _chip_exchange(name, sums, collective_id=None, after=()):
    n = len(sums)
    na = len(after)

    def body(*refs):
        x_refs, out_refs = refs[:n], refs[n + na:2 * n + na]
        send_sems, recv_sems, local_sems = refs[2 * n + na:]
        x, y, c = lax.axis_index("x"), lax.axis_index("y"), lax.axis_index("c")
        chips = [(1 - x, y), (x, 1 - y), (1 - x, 1 - y)]
        my_chip = 2 * x + y
        if collective_id is not None:
            _handshake([(cx, cy, c) for cx, cy in chips])
        mine = [pltpu.make_async_copy(x_refs[i].at[my_chip], out_refs[i].at[my_chip], local_sems.at[i])
                for i in range(n)]
        for cp in mine:
            cp.start()
        sends = []
        for i in range(n):
            for j, (cx, cy) in enumerate(chips):
                sends.append(pltpu.make_async_remote_copy(
                    src_ref=x_refs[i].at[2 * cx + cy], dst_ref=out_refs[i].at[my_chip],
                    send_sem=send_sems.at[3 * i + j], recv_sem=recv_sems.at[3 * i + j],
                    device_id=(cx, cy, c), device_id_type=MESH))
        for cp in sends:
            cp.start()
        for i in range(n):
            for j, (cx, cy) in enumerate(chips):
                pltpu.make_async_remote_copy(
                    src_ref=x_refs[i].at[my_chip], dst_ref=out_refs[i].at[2 * cx + cy],
                    send_sem=send_sems.at[3 * i + j], recv_sem=recv_sems.at[3 * i + j],
                    device_id=(cx, cy, c), device_id_type=MESH).wait_recv()
        for cp in sends:
            cp.wait_send()
        for cp in mine:
            cp.wait()

    return _comm_call(
        body, name, list(sums) + list(after), [jax.ShapeDtypeStruct(s.shape, s.dtype) for s in sums],
        [pltpu.SemaphoreType.DMA((3 * n,)), pltpu.SemaphoreType.DMA((3 * n,)), pltpu.SemaphoreType.DMA((n,))],
        collective_id)


def _cast_shards(shards):
    n = len(shards)

    def body(*refs):
        for i in range(n):
            refs[n + i][...] = refs[i][...].astype(BF16)

    vmem = pl.BlockSpec(memory_space=pltpu.VMEM)
    return pl.pallas_call(
        body, name="cast_shards", in_specs=[vmem] * n, out_specs=[vmem] * n,
        out_shape=[jax.ShapeDtypeStruct(s.shape, BF16) for s in shards],
        compiler_params=pltpu.CompilerParams(vmem_limit_bytes=VMEM_LIMIT_V7X),
    )(*shards)


BIG = ("w_in", "w_branch_a", "w_branch_b", "w_out", "w_ffn_gate", "w_ffn_up", "w_ffn_down")


def _local_step(x, target, gains, low, conv_w, wg8, reduce):
    g_mix, g_hg, g_ffn, g_fin = gains
    w_in = wg8["w_in"].reshape(N_IN, D_MODEL)
    wg = wg8["w_ffn_gate"].reshape(D_FF, D_MODEL)
    wu = wg8["w_ffn_up"].reshape(D_FF, D_MODEL)
    wa, wb = wg8["w_branch_a"], wg8["w_branch_b"]
    wo = wg8["w_out"].reshape(D_MODEL, D_MODEL)
    wd = wg8["w_ffn_down"].reshape(D_FF, D_MODEL)

    ht, hg, cv, gt, cvo, cvot = _fwd_in(x, g_mix, w_in, conv_w)
    o, og, ogt, st = _hg_fwd(hg, low, g_hg)
    x1, mgt = _merge_fwd(og, cvo, gt, x, wa, wb, wo)
    h2t, gate, up, actt, loss, d_gfin, dx2 = _ffn_fwd_loss(x1, g_ffn, wg, wu, wd, target, g_fin)

    dgate, dup, dx1, d_gffn = _ffn_bwd(dx2, x1, gate, up, g_ffn, wg, wu, wd)
    ffn = dict(
        w_ffn_down=_wgrad("wgrad_ffn_down", actt, dx2, 512).reshape(N_DEV, D_FF // N_DEV, D_MODEL),
        w_ffn_gate=_wgrad("wgrad_ffn_gate", h2t, dgate, 1408, transposed=True).reshape(N_DEV, D_FF // N_DEV, D_MODEL),
        w_ffn_up=_wgrad("wgrad_ffn_up", h2t, dup, 1408, transposed=True).reshape(N_DEV, D_FF // N_DEV, D_MODEL))
    dgt, dya, dyb, dog, dcv, d_conv = _merge_bwd(dx1, og, cvo, gt, cv, wa, wb, wo, conv_w)
    sums_ffn, got_ffn = reduce.begin(ffn, sum_after=[dya])
    parts_ffn, updated_ffn = reduce.finish(ffn, sums_ffn)
    grad_a, grad_b = _wgrad_branches(ogt, dya, cvot, dyb)
    out = dict(
        w_out=_wgrad("wgrad_out", mgt, dx1, 512, after=sums_ffn[:1]).reshape(N_DEV, D_MODEL // N_DEV, D_MODEL),
        w_branch_a=grad_a, w_branch_b=grad_b)
    dq, df, di, dg, d_low, d_ghg = _hg_bwd(dog, hg, o, st, low, g_hg, after=list(sums_ffn) + [out["w_out"]])
    sums_out, got_out = reduce.begin(out, after=[parts_ffn[0], dq], sum_after=updated_ffn)
    parts_out, updated_out = reduce.finish(out, sums_out)
    dparts = [dq, df, di, dg, dcv, dgt]
    w_in_grad = dict(w_in=_wgrad_in(ht, dparts, after=sums_out[:1]).reshape(N_DEV, N_IN // N_DEV, D_MODEL))
    sums_in, _ = reduce.begin(w_in_grad, after=parts_out[:1], sum_after=updated_out)
    parts_in, _ = reduce.finish(w_in_grad, sums_in)
    grad_x, d_gmix = _in_bwd(dparts, w_in, x, dx1, g_mix, after=list(parts_out[:1]) + list(sums_in))
    small = dict(norm_mix_g=d_gmix, norm_ffn_g=d_gffn, norm_final_g=d_gfin, lower_bounds=d_low, hg_norm_g=d_ghg,
                 conv_w=d_conv, loss=loss)
    return grad_x, small, parts_in


def _conv_shard_rows(a):
    return jnp.pad(a, ((0, 5), (0, 64)))


def kernel(x, norm_mix_g, w_in, lower_bounds, hg_norm_g, conv_w, w_branch_a, w_branch_b, w_out, norm_ffn_g, w_ffn_gate, w_ffn_up, w_ffn_down, norm_final_g, loss_target, m_norm_mix_g, m_w_in, m_lower_bounds, m_hg_norm_g, m_conv_w, m_w_branch_a, m_w_branch_b, m_w_out, m_norm_ffn_g, m_w_ffn_gate, m_w_ffn_up, m_w_ffn_down, m_norm_final_g, v_norm_mix_g, v_w_in, v_lower_bounds, v_hg_norm_g, v_conv_w, v_w_branch_a, v_w_branch_b, v_w_out, v_norm_ffn_g, v_w_ffn_gate, v_w_ffn_up, v_w_ffn_down, v_norm_final_g):
    cx, cy, cc = lax.axis_index("x"), lax.axis_index("y"), lax.axis_index("c")
    my_dev = 4 * cx + 2 * cy + cc

    def tr(a):
        return a[0].T

    big = dict(w_in=tr(w_in), w_branch_a=w_branch_a[0], w_branch_b=w_branch_b[0], w_out=w_out[0],
               w_ffn_gate=tr(w_ffn_gate), w_ffn_up=tr(w_ffn_up), w_ffn_down=w_ffn_down[0])
    big_m = dict(w_in=tr(m_w_in), w_branch_a=m_w_branch_a[0], w_branch_b=m_w_branch_b[0], w_out=m_w_out[0],
                 w_ffn_gate=tr(m_w_ffn_gate), w_ffn_up=tr(m_w_ffn_up), w_ffn_down=m_w_ffn_down[0])
    big_v = dict(w_in=tr(v_w_in), w_branch_a=v_w_branch_a[0], w_branch_b=v_w_branch_b[0], w_out=v_w_out[0],
                 w_ffn_gate=tr(v_w_ffn_gate), w_ffn_up=tr(v_w_ffn_up), w_ffn_down=v_w_ffn_down[0])
    transposed = ("w_in", "w_ffn_gate", "w_ffn_up")

    shards = dict(zip(BIG, _cast_shards([big[n] for n in BIG])))
    first = _all_gather("gather_w_in", [shards["w_in"], _conv_shard_rows(conv_w[0])])
    ids = iter(range(1, 16))
    mid = _all_gather("gather_mid", [shards[n] for n in BIG[1:4]], collective_id=next(ids), after=first[1:])
    ffn = _all_gather("gather_ffn", [shards[n] for n in BIG[4:]], collective_id=next(ids), after=first[1:])
    wg8 = dict(zip(BIG, [first[0]] + list(mid) + list(ffn)))
    conv_full = first[1][:, :3, :64].transpose(1, 0, 2).reshape(3, CONV_WIDTH)

    core = cc.reshape(1).astype(jnp.int32)
    outs = {}

    class Reduce:
        @staticmethod
        def begin(grads, after=(), sum_after=()):
            names = list(grads)
            by_owner = [grads[n] for n in names]
            got = _sibling_swap("sibling_swap_" + names[0], by_owner, collective_id=next(ids), after=after)
            sums = _pair_sum("pair_sum_" + names[0], by_owner, got, core, after=sum_after)
            return sums, got

        @staticmethod
        def finish(grads, chip_sums, after=()):
            names = list(grads)
            parts = _chip_exchange("chip_exchange_" + names[0], chip_sums, collective_id=next(ids), after=after)
            for n, p in zip(names, parts):
                outs[n] = _adamw_sum("adamw_" + n, big[n], p, big_m[n], big_v[n])
            return parts, [outs[n][1] for n in names]

    gains = (norm_mix_g, hg_norm_g, norm_ffn_g, norm_final_g.reshape(1, D_MODEL))
    grad_x, small, last = _local_step(x[0], loss_target[0], gains, lower_bounds, conv_full, wg8, Reduce)

    small_all = _all_gather("gather_small", [_small_pack(small)], collective_id=next(ids), after=last[:1])

    def small_state(a):
        return [a[0], a[1], a[2].reshape(1, D_MODEL), a[3], a[4], a[5][0]]

    upd = _small_update(
        small_all[0], my_dev.reshape(1).astype(jnp.int32),
        small_state((norm_mix_g, norm_ffn_g, norm_final_g, lower_bounds, hg_norm_g, conv_w)),
        small_state((m_norm_mix_g, m_norm_ffn_g, m_norm_final_g, m_lower_bounds, m_hg_norm_g, m_conv_w)),
        small_state((v_norm_mix_g, v_norm_ffn_g, v_norm_final_g, v_lower_bounds, v_hg_norm_g, v_conv_w)))
    loss = upd[0][0, 0]
    small_shape = dict(norm_final_g=(D_MODEL,), conv_w=(1, 3, CONV_SHARD))
    for p, (name, _, _, _) in enumerate(_SMALL_PARAMS):
        outs[name] = [a.reshape(small_shape.get(name, a.shape)) for a in upd[1 + 4 * p:5 + 4 * p]]

    order = ["norm_mix_g", "w_in", "lower_bounds", "hg_norm_g", "conv_w", "w_branch_a", "w_branch_b", "w_out",
             "norm_ffn_g", "w_ffn_gate", "w_ffn_up", "w_ffn_down", "norm_final_g"]
    result = [loss, grad_x[None]]
    for k in range(4):
        for n in order:
            if n in BIG:
                result.append((outs[n][k].T if n in transposed else outs[n][k])[None])
            else:
                result.append(outs[n][k])
    return tuple(result)
```

```python
import jax
import jax.numpy as jnp
from jax import lax
from jax.experimental import pallas as pl
from jax.experimental.pallas import tpu as pltpu
from jax.experimental.pallas import tpu_sc as plsc

F32 = jnp.float32
BF16 = jnp.bfloat16
STASH = jnp.bfloat16

D_MODEL = 1024
HG_WIDTH = 512
HEAD_DIM = 128
N_HEADS = 4
HEADS_PER_STEP = 4
HEAD_GROUPS = N_HEADS // HEADS_PER_STEP
CONV_WIDTH = 512
CONV_K = 3
D_FF = 2816
CHUNK = 32
EPS = 1e-6
Q_SCALE = HEAD_DIM ** -0.5
N_DEV = 8

ADAM_LR = 0.001
ADAM_B1 = 0.9
ADAM_B2 = 0.999
ADAM_EPS = 1e-08
ADAM_WD = 0.01
ADAM_STEP = 10

VMEM_LIMIT_V7X = 56 * 1024 * 1024

SMALL_ROWS = 16


def _params(sem, vmem=VMEM_LIMIT_V7X):
    return pltpu.CompilerParams(dimension_semantics=sem, vmem_limit_bytes=vmem)


def _mm(a, b):
    return jnp.dot(a.astype(BF16), b.astype(BF16), preferred_element_type=F32)


def _mm_nt(a, b):
    return lax.dot_general(a.astype(BF16), b.astype(BF16), (((1,), (1,)), ((), ())), preferred_element_type=F32)


def _mm_tn(a, b):
    return lax.dot_general(a.astype(BF16), b.astype(BF16), (((0,), (0,)), ((), ())), preferred_element_type=F32)


def _sigmoid(x):
    return 0.5 * jnp.tanh(0.5 * x) + 0.5


def _resident(shape):
    nd = len(shape)
    return pl.BlockSpec(shape, lambda *_: (0,) * nd, pipeline_mode=pl.Buffered(1))


def _full(shape):
    nd = len(shape)
    return pl.BlockSpec(shape, lambda *_: (0,) * nd)


def _shard_cols(w_ref):
    return jnp.concatenate([w_ref[s] for s in range(N_DEV)], axis=1)


N_HG = 4 * HG_WIDTH
N_CV = 3 * CONV_WIDTH
N_GT = 2 * D_MODEL
N_IN = N_HG + N_CV + N_GT


def _col(tm, n):
    return pl.BlockSpec((n, tm), lambda i: (0, i))


HALO = 8


def _fwd_in(x, g, w_in_t, conv_w):
    T = x.shape[0]
    tm = min(512, T)

    def body(x_ref, g_ref, w_ref, cw_ref, ht_ref, hg_ref, cv_ref, gt_ref, cvo_ref, cvot_ref, tail_scr):
        @pl.when(pl.program_id(0) == 0)
        def _():
            tail_scr[...] = jnp.zeros_like(tail_scr)

        xv = x_ref[...]
        r = lax.rsqrt(jnp.mean(xv * xv, axis=-1, keepdims=True) + EPS)
        hf = xv * r * g_ref[...]
        h = hf.astype(BF16)
        ht_ref[...] = hf.T.astype(BF16)
        hg_ref[...] = _mm_nt(h, w_ref[:N_HG, :])
        cv = _mm_nt(h, w_ref[N_HG:N_HG + N_CV, :])
        cv_ref[...] = cv.astype(STASH)
        gt_ref[...] = _mm_nt(h, w_ref[N_HG + N_CV:, :]).astype(STASH)

        u = cv[:, :CONV_WIDTH] * cv[:, 2 * CONV_WIDTH:]
        row = lax.broadcasted_iota(jnp.int32, u.shape, 0)
        prev1 = tail_scr[HALO - 1:HALO, :]
        prev2 = tail_scr[HALO - 2:HALO - 1, :]
        u1 = jnp.where(row >= 1, pltpu.roll(u, 1, 0), prev1)
        u2 = jnp.where(row >= 2, pltpu.roll(u, 2, 0), jnp.where(row == 1, prev1, prev2))
        y = cw_ref[0:1, :] * u2 + cw_ref[1:2, :] * u1 + cw_ref[2:3, :] * u
        out = cv[:, CONV_WIDTH:2 * CONV_WIDTH] * y
        cvo_ref[...] = out.astype(BF16)
        cvot_ref[...] = out.T.astype(BF16)
        tail_scr[...] = u[tm - HALO:, :]

    row = lambda n: pl.BlockSpec((tm, n), lambda i: (i, 0))
    return pl.pallas_call(
        body, name="fwd_in", grid=(T // tm,),
        in_specs=[row(D_MODEL), _full((1, D_MODEL)), _resident(w_in_t.shape), _full((CONV_K, CONV_WIDTH))],
        out_specs=[_col(tm, D_MODEL), row(N_HG), row(N_CV), row(N_GT), row(CONV_WIDTH), _col(tm, CONV_WIDTH)],
        out_shape=[jax.ShapeDtypeStruct((D_MODEL, T), BF16), jax.ShapeDtypeStruct((T, N_HG), F32),
                   jax.ShapeDtypeStruct((T, N_CV), STASH), jax.ShapeDtypeStruct((T, N_GT), STASH),
                   jax.ShapeDtypeStruct((T, CONV_WIDTH), BF16), jax.ShapeDtypeStruct((CONV_WIDTH, T), BF16)],
        scratch_shapes=[pltpu.VMEM((HALO, CONV_WIDTH), F32)],
        compiler_params=_params(("arbitrary",)),
    )(x, g, w_in_t, conv_w)


def _chunk_pos(shape):
    return lax.broadcasted_iota(jnp.int32, shape, 0) & (CHUNK - 1)


def _chunk_cumsum(x, pos):
    s = 1
    while s < CHUNK:
        x = x + jnp.where(pos >= s, pltpu.roll(x, s, 0), 0.0)
        s *= 2
    return x


def _chunk_rev_cumsum(x, pos):
    n = x.shape[0]
    s = 1
    while s < CHUNK:
        x = x + jnp.where(pos + s < CHUNK, pltpu.roll(x, n - s, 0), 0.0)
        s *= 2
    return x


def _chunk_bcast(x3, row, tb):
    return jnp.broadcast_to(x3[:, row:row + 1, :], x3.shape).reshape(tb, x3.shape[-1])


def _lower_bound(low_ref):
    l0 = low_ref[0:1, :]
    l1 = low_ref[1:2, :]
    m = jnp.maximum(l0, l1)
    e0 = jnp.exp(l0 - m)
    e1 = jnp.exp(l1 - m)
    return e0 / (e0 + e1), e1 / (e0 + e1)


def _hg_gates(qr, fr, lb, pos, tb):
    sq = _sigmoid(qr)
    q = qr * sq * Q_SCALE
    sg = _sigmoid(fr)
    f = lb + (1.0 - lb) * sg
    k = 1.0 - f
    b = _chunk_cumsum(jnp.log(f), pos)
    b3 = b.reshape(tb // CHUNK, CHUNK, HEAD_DIM)
    anc = _chunk_bcast(b3, CHUNK // 2 - 1, tb)
    blb = _chunk_bcast(b3, CHUNK - 1, tb)
    e_qa = jnp.exp(b - anc)
    e_ka = jnp.exp(anc - b)
    e_b = jnp.exp(b)
    e_ko = jnp.exp(blb - b)
    dec = jnp.exp(blb)
    return sq, q, sg, f, k, e_qa, e_ka, e_b, e_ko, dec


def _intra_mask(sb):
    r = lax.broadcasted_iota(jnp.int32, (sb, sb), 0)
    c = lax.broadcasted_iota(jnp.int32, (sb, sb), 1)
    return ((r // CHUNK) == (c // CHUNK)) & (c <= r)


def _hg_fwd(hg, low, gn):
    T = hg.shape[0]
    tb = min(1024, T)
    sb = min(256, tb)
    nb = T // tb
    nc = tb // CHUNK
    wid = HEADS_PER_STEP * HEAD_DIM

    def body(q_ref, f_ref, i_ref, g_ref, low_ref, gn_ref, o_ref, og_ref, ogt_ref, st_ref, s_scr):
        t = pl.program_id(1)

        @pl.when(t == 0)
        def _():
            s_scr[...] = jnp.zeros_like(s_scr)

        pos = _chunk_pos((tb, HEAD_DIM))
        mask = _intra_mask(sb)
        lanes = [slice(hh * HEAD_DIM, (hh + 1) * HEAD_DIM) for hh in range(HEADS_PER_STEP)]
        qi, ko, vb, dec, st = [], [], [], [], []
        for hh, ln in enumerate(lanes):
            lb, _ = _lower_bound(low_ref.at[:, ln])
            _, q, _, _, k, e_qa, e_ka, e_b, e_ko, dec_h = _hg_gates(q_ref[:, ln], f_ref[:, ln], lb, pos, tb)
            qh = (q * e_qa).astype(BF16)
            kh = (k * e_ka).astype(BF16)
            qi.append((q * e_b).astype(BF16))
            ko.append((k * e_ko).astype(BF16))
            vb.append(i_ref[:, ln].astype(BF16))
            dec.append(dec_h)
            st.append(s_scr[hh])
            for s in range(tb // sb):
                sl = slice(s * sb, (s + 1) * sb)
                p = jnp.where(mask, _mm_nt(qh[sl], kh[sl]), 0.0)
                o_ref[sl, ln] = _mm(p, vb[hh][sl])
        for c in range(nc):
            sl = slice(c * CHUNK, (c + 1) * CHUNK)
            for hh, ln in enumerate(lanes):
                st_ref[hh, c] = st[hh]
                o_ref[sl, ln] = o_ref[sl, ln] + _mm_nt(qi[hh][sl], st[hh])
                st[hh] = dec[hh][c * CHUNK:c * CHUNK + 1, :] * st[hh] + _mm_tn(vb[hh][sl], ko[hh][sl])
        for hh, ln in enumerate(lanes):
            s_scr[hh] = st[hh]
            o = o_ref[:, ln]
            r = lax.rsqrt(jnp.mean(o * o, axis=-1, keepdims=True) + EPS)
            gr = g_ref[:, ln]
            og = (o * r * gn_ref[...]) * (gr * _sigmoid(gr))
            og_ref[:, ln] = og.astype(BF16)
            ogt_ref[ln, :] = og.T.astype(BF16)

    col = lambda p: pl.BlockSpec((tb, wid), lambda h, t: (t, p * HEAD_GROUPS + h))
    hcol = pl.BlockSpec((tb, wid), lambda h, t: (t, h))
    return pl.pallas_call(
        body, name="hg_fwd", grid=(HEAD_GROUPS, nb),
        in_specs=[col(0), col(1), col(2), col(3), pl.BlockSpec((2, wid), lambda h, t: (0, h)),
                  pl.BlockSpec((1, HEAD_DIM), lambda h, t: (0, 0))],
        out_specs=[hcol, hcol, pl.BlockSpec((wid, tb), lambda h, t: (h, t)),
                   pl.BlockSpec((HEADS_PER_STEP, nc, HEAD_DIM, HEAD_DIM), lambda h, t: (h, t, 0, 0))],
        out_shape=[jax.ShapeDtypeStruct((T, HG_WIDTH), F32), jax.ShapeDtypeStruct((T, HG_WIDTH), BF16),
                   jax.ShapeDtypeStruct((HG_WIDTH, T), BF16),
                   jax.ShapeDtypeStruct((N_HEADS, T // CHUNK, HEAD_DIM, HEAD_DIM), F32)],
        scratch_shapes=[pltpu.VMEM((HEADS_PER_STEP, HEAD_DIM, HEAD_DIM), F32)],
        compiler_params=_params(("parallel", "arbitrary")),
    )(hg, hg, hg, hg, low, gn)


def _merge_fwd(og, cvo, gt, x, wa, wb, wo):
    T = x.shape[0]
    tm = min(1024, T)

    def body(og_ref, cvo_ref, gt_ref, x_ref, wa_ref, wb_ref, wo_ref, x1_ref, mgt_ref, ya_ref, yb_ref):
        ya = jnp.dot(og_ref[...], _shard_cols(wa_ref), preferred_element_type=F32)
        yb = jnp.dot(cvo_ref[...], _shard_cols(wb_ref), preferred_element_type=F32)
        ya_ref[...] = ya.astype(STASH)
        yb_ref[...] = yb.astype(STASH)
        m = (_sigmoid(gt_ref[:, :D_MODEL].astype(F32)) * ya
             + _sigmoid(gt_ref[:, D_MODEL:].astype(F32)) * yb)
        mgt_ref[...] = m.T.astype(BF16)
        x1_ref[...] = x_ref[...] + jnp.dot(m.astype(BF16), wo_ref[...], preferred_element_type=F32)

    row = lambda n: pl.BlockSpec((tm, n), lambda i: (i, 0))
    stash = jax.ShapeDtypeStruct((T, D_MODEL), STASH)
    return pl.pallas_call(
        body, name="merge_fwd", grid=(T // tm,),
        in_specs=[row(HG_WIDTH), row(CONV_WIDTH), row(2 * D_MODEL), row(D_MODEL),
                  _resident(wa.shape), _resident(wb.shape), _resident(wo.shape)],
        out_specs=[row(D_MODEL), _col(tm, D_MODEL), row(D_MODEL), row(D_MODEL)],
        out_shape=[jax.ShapeDtypeStruct((T, D_MODEL), F32), jax.ShapeDtypeStruct((D_MODEL, T), BF16), stash, stash],
        compiler_params=_params(("parallel",)),
    )(og, cvo, gt, x, wa, wb, wo)


def _ffn_fwd_loss(x1, g, wg, wu, wd, target, g_fin):
    T = x1.shape[0]
    tm = min(256, T)

    def body(x_ref, g_ref, wg_ref, wu_ref, wd_ref, t_ref, gf_ref,
             ht_ref, gate_ref, up_ref, actt_ref, loss_ref, dgf_ref, dx2_ref):
        @pl.when(pl.program_id(0) == 0)
        def _():
            loss_ref[...] = jnp.zeros_like(loss_ref)
            dgf_ref[...] = jnp.zeros_like(dgf_ref)

        xv = x_ref[...]
        r = lax.rsqrt(jnp.mean(xv * xv, axis=-1, keepdims=True) + EPS)
        hf = xv * r * g_ref[...]
        h = hf.astype(BF16)
        ht_ref[...] = hf.T.astype(BF16)
        gate = _mm_nt(h, wg_ref[...])
        up = _mm_nt(h, wu_ref[...])
        gate_ref[...] = gate.astype(STASH)
        up_ref[...] = up.astype(STASH)
        act = gate * _sigmoid(gate) * up
        actt_ref[...] = act.T.astype(BF16)
        x2 = xv + jnp.dot(act.astype(BF16), wd_ref[...], preferred_element_type=F32)

        gv = gf_ref[...]
        r2 = lax.rsqrt(jnp.mean(x2 * x2, axis=-1, keepdims=True) + EPS)
        xh = x2 * r2
        err = xh * gv - t_ref[...]
        loss_ref[...] += 0.5 * jnp.sum(jnp.mean(err * err, axis=-1, keepdims=True), axis=0, keepdims=True)
        dy = err * (1.0 / D_MODEL)
        dgf_ref[...] += jnp.sum(dy * xh, axis=0, keepdims=True)
        w = dy * gv
        dx2_ref[...] = r2 * (w - xh * jnp.mean(w * xh, axis=-1, keepdims=True))

    row = lambda n: pl.BlockSpec((tm, n), lambda i: (i, 0))
    return pl.pallas_call(
        body, name="ffn_fwd_loss", grid=(T // tm,),
        in_specs=[row(D_MODEL), _full((1, D_MODEL)), _resident(wg.shape), _resident(wu.shape), _resident(wd.shape),
                  row(D_MODEL), _full((1, D_MODEL))],
        out_specs=[_col(tm, D_MODEL), row(D_FF), row(D_FF), _col(tm, D_FF), _full((1, 128)), _full((1, D_MODEL)),
                   row(D_MODEL)],
        out_shape=[jax.ShapeDtypeStruct((D_MODEL, T), BF16), jax.ShapeDtypeStruct((T, D_FF), STASH),
                   jax.ShapeDtypeStruct((T, D_FF), STASH), jax.ShapeDtypeStruct((D_FF, T), BF16),
                   jax.ShapeDtypeStruct((1, 128), F32), jax.ShapeDtypeStruct((1, D_MODEL), F32),
                   jax.ShapeDtypeStruct((T, D_MODEL), F32)],
        compiler_params=_params(("arbitrary",)),
    )(x1, g, wg, wu, wd, target, g_fin)


def _ffn_bwd(dx2, x1, gate, up, g, wg, wu, wd):
    T = x1.shape[0]
    tm = min(256, T)

    def body(dx2_ref, x_ref, gate_ref, up_ref, g_ref, wg_ref, wu_ref, wd_ref, dgate_ref, dup_ref, dx1_ref, dgn_ref):
        @pl.when(pl.program_id(0) == 0)
        def _():
            dgn_ref[...] = jnp.zeros_like(dgn_ref)

        dx2 = dx2_ref[...]
        dact = _mm_nt(dx2, wd_ref[...])
        gate = gate_ref[...].astype(F32)
        s = _sigmoid(gate)
        dgate = (dact * up_ref[...].astype(F32) * (s * (1.0 + gate * (1.0 - s)))).astype(BF16)
        dup = (dact * (gate * s)).astype(BF16)
        dgate_ref[...] = dgate
        dup_ref[...] = dup
        dh = _mm(dgate, wg_ref[...]) + _mm(dup, wu_ref[...])
        xv = x_ref[...]
        r = lax.rsqrt(jnp.mean(xv * xv, axis=-1, keepdims=True) + EPS)
        xh = xv * r
        dgn_ref[...] += jnp.sum(dh * xh, axis=0, keepdims=True)
        w = dh * g_ref[...]
        dx1_ref[...] = dx2 + r * (w - xh * jnp.mean(w * xh, axis=-1, keepdims=True))

    row = lambda n: pl.BlockSpec((tm, n), lambda i: (i, 0))
    return pl.pallas_call(
        body, name="ffn_bwd", grid=(T // tm,),
        in_specs=[row(D_MODEL), row(D_MODEL), row(D_FF), row(D_FF), _full((1, D_MODEL)),
                  _resident(wg.shape), _resident(wu.shape), _resident(wd.shape)],
        out_specs=[row(D_FF), row(D_FF), row(D_MODEL), _full((1, D_MODEL))],
        out_shape=[jax.ShapeDtypeStruct((T, D_FF), BF16), jax.ShapeDtypeStruct((T, D_FF), BF16),
                   jax.ShapeDtypeStruct((T, D_MODEL), F32), jax.ShapeDtypeStruct((1, D_MODEL), F32)],
        compiler_params=_params(("arbitrary",)),
    )(dx2, x1, gate, up, g, wg, wu, wd)


def _merge_bwd(dx1, ya, yb, gt, cv, wa, wb, wo, conv_w):
    T = dx1.shape[0]
    tm = min(512, T)
    nt = T // tm

    def body(dx_ref, ya_ref, yb_ref, gt_ref, cv_ref, halo_ref, wa_ref, wb_ref, wo_ref, cw_ref,
             dgt_ref, dya_ref, dyb_ref, dog_ref, dcv_ref, dcw_ref, prev_u, next_dy):
        step = pl.program_id(0)

        @pl.when(step == 0)
        def _():
            next_dy[...] = jnp.zeros_like(next_dy)
            dcw_ref[...] = jnp.zeros_like(dcw_ref)

        dm = _mm_nt(dx_ref[...], wo_ref[...])
        wa = _shard_cols(wa_ref)
        wb = _shard_cols(wb_ref)
        ya = ya_ref[...].astype(F32)
        yb = yb_ref[...].astype(F32)
        sa = _sigmoid(gt_ref[:, :D_MODEL].astype(F32))
        sb = _sigmoid(gt_ref[:, D_MODEL:].astype(F32))
        dgt_ref[:, :D_MODEL] = (dm * ya * (sa * (1.0 - sa))).astype(BF16)
        dgt_ref[:, D_MODEL:] = (dm * yb * (sb * (1.0 - sb))).astype(BF16)
        dya = (dm * sa).astype(BF16)
        dyb = (dm * sb).astype(BF16)
        dya_ref[...] = dya
        dyb_ref[...] = dyb
        dog_ref[...] = _mm_nt(dya, wa)
        dcvo = _mm_nt(dyb, wb)

        cvt = cv_ref[...].astype(F32)
        c, bg, xb = cvt[:, :CONV_WIDTH], cvt[:, CONV_WIDTH:2 * CONV_WIDTH], cvt[:, 2 * CONV_WIDTH:]
        halo = halo_ref[...].astype(F32)
        first_tile = step == nt - 1
        prev_u[...] = jnp.where(first_tile, 0.0, halo[:, :CONV_WIDTH] * halo[:, 2 * CONV_WIDTH:])
        u = c * xb
        row = lax.broadcasted_iota(jnp.int32, u.shape, 0)
        p1 = prev_u[HALO - 1:HALO, :]
        p2 = prev_u[HALO - 2:HALO - 1, :]
        u1 = jnp.where(row >= 1, pltpu.roll(u, 1, 0), p1)
        u2 = jnp.where(row >= 2, pltpu.roll(u, 2, 0), jnp.where(row == 1, p1, p2))
        w0, w1, w2 = cw_ref[0:1, :], cw_ref[1:2, :], cw_ref[2:3, :]
        y = w0 * u2 + w1 * u1 + w2 * u
        dcv_ref[:, CONV_WIDTH:2 * CONV_WIDTH] = (dcvo * y).astype(BF16)
        dy = dcvo * bg
        dcw_ref[0:1, :] += jnp.sum(dy * u2, axis=0, keepdims=True)
        dcw_ref[1:2, :] += jnp.sum(dy * u1, axis=0, keepdims=True)
        dcw_ref[2:3, :] += jnp.sum(dy * u, axis=0, keepdims=True)
        n1 = next_dy[0:1, :]
        n2 = next_dy[1:2, :]
        dy1 = jnp.where(row < tm - 1, pltpu.roll(dy, tm - 1, 0), n1)
        dy2 = jnp.where(row < tm - 2, pltpu.roll(dy, tm - 2, 0), jnp.where(row == tm - 2, n1, n2))
        du = w2 * dy + w1 * dy1 + w0 * dy2
        dcv_ref[:, :CONV_WIDTH] = (du * xb).astype(BF16)
        dcv_ref[:, 2 * CONV_WIDTH:] = (du * c).astype(BF16)
        next_dy[...] = dy[:HALO, :]

    rt = lambda i: nt - 1 - i
    row = lambda n: pl.BlockSpec((tm, n), lambda i: (rt(i), 0))
    halo = pl.BlockSpec((HALO, N_CV), lambda i: (jnp.maximum(rt(i) * (tm // HALO) - 1, 0), 0))
    return pl.pallas_call(
        body, name="merge_bwd", grid=(nt,),
        in_specs=[row(D_MODEL), row(D_MODEL), row(D_MODEL), row(2 * D_MODEL), row(N_CV), halo,
                  _resident(wa.shape), _resident(wb.shape), _resident(wo.shape), _full((CONV_K, CONV_WIDTH))],
        out_specs=[row(2 * D_MODEL), row(D_MODEL), row(D_MODEL), row(HG_WIDTH), row(N_CV),
                   _full((CONV_K, CONV_WIDTH))],
        out_shape=[jax.ShapeDtypeStruct((T, 2 * D_MODEL), BF16), jax.ShapeDtypeStruct((T, D_MODEL), BF16),
                   jax.ShapeDtypeStruct((T, D_MODEL), BF16), jax.ShapeDtypeStruct((T, HG_WIDTH), F32),
                   jax.ShapeDtypeStruct((T, N_CV), BF16), jax.ShapeDtypeStruct((CONV_K, CONV_WIDTH), F32)],
        scratch_shapes=[pltpu.VMEM((HALO, CONV_WIDTH), F32), pltpu.VMEM((HALO, CONV_WIDTH), F32)],
        compiler_params=_params(("arbitrary",)),
    )(dx1, ya, yb, gt, cv, cv, wa, wb, wo, conv_w)


def _drop_operands(body, first, count):
    def wrapped(*refs):
        return body(*refs[:first], *refs[first + count:])
    return wrapped


def _hg_bwd(dog, hg, o, st, low, gn, after=()):
    T = hg.shape[0]
    tb = min(512, T)
    sb = min(256, tb)
    nb = T // tb
    nc = tb // CHUNK
    wid = HEADS_PER_STEP * HEAD_DIM

    def body(q_ref, f_ref, i_ref, g_ref, low_ref, gn_ref, o_ref, dog_ref, st_ref,
             dq_ref, df_ref, di_ref, dg_ref, dlow_ref, dgn_ref,
             ds_scr, dqi_scr, dko_scr, dv_scr, dd_scr, dqh_scr, dkh_scr):
        h = pl.program_id(0)
        t = pl.program_id(1)

        @pl.when(t == 0)
        def _():
            ds_scr[...] = jnp.zeros_like(ds_scr)
            dlow_ref[...] = jnp.zeros_like(dlow_ref)

        @pl.when((t == 0) & (h == 0))
        def _():
            dgn_ref[...] = jnp.zeros_like(dgn_ref)

        pos = _chunk_pos((tb, HEAD_DIM))
        mask = _intra_mask(sb)
        gnv = gn_ref[...]
        lanes = [slice(hh * HEAD_DIM, (hh + 1) * HEAD_DIM) for hh in range(HEADS_PER_STEP)]
        heads = []
        for hh, ln in enumerate(lanes):
            lb, lb1 = _lower_bound(low_ref.at[:, ln])
            qr = q_ref[:, ln]
            sq, q, sg, f, k, e_qa, e_ka, e_b, e_ko, dec = _hg_gates(qr, f_ref[:, ln], lb, pos, tb)

            gr = g_ref[:, ln]
            o = o_ref[:, ln]
            dog_v = dog_ref[:, ln]
            sgr = _sigmoid(gr)
            r = lax.rsqrt(jnp.mean(o * o, axis=-1, keepdims=True) + EPS)
            oh = o * r
            dg_ref[:, ln] = (dog_v * (oh * gnv) * (sgr * (1.0 + gr * (1.0 - sgr)))).astype(BF16)
            don = dog_v * (gr * sgr)
            dgn_ref[...] += jnp.sum(don * oh, axis=0, keepdims=True)
            w = don * gnv
            do = (r * (w - oh * jnp.mean(w * oh, axis=-1, keepdims=True))).astype(BF16)

            qh = (q * e_qa).astype(BF16)
            kh = (k * e_ka).astype(BF16)
            qi = (q * e_b).astype(BF16)
            ko = (k * e_ko).astype(BF16)
            vb = i_ref[:, ln].astype(BF16)

            for s in range(tb // sb):
                sl = slice(s * sb, (s + 1) * sb)
                p = jnp.where(mask, _mm_nt(qh[sl], kh[sl]), 0.0).astype(BF16)
                dp = jnp.where(mask, _mm_nt(do[sl], vb[sl]), 0.0).astype(BF16)
                dv_scr[sl, ln] = _mm_tn(p, do[sl])
                dqh_scr[sl, ln] = _mm(dp, kh[sl])
                dkh_scr[sl, ln] = _mm_tn(dp, qh[sl])
            heads.append(dict(lb=lb, lb1=lb1, qr=qr, sq=sq, q=q, sg=sg, f=f, k=k, e_qa=e_qa, e_ka=e_ka, e_b=e_b,
                              e_ko=e_ko, dec=dec, do=do, qi=qi, ko=ko, vb=vb, ds=ds_scr[hh]))

        for c in reversed(range(nc)):
            sl = slice(c * CHUNK, (c + 1) * CHUNK)
            for hh, ln in enumerate(lanes):
                hd = heads[hh]
                ds = hd["ds"]
                st_c = st_ref[hh, c]
                dqi_scr[sl, ln] = _mm(hd["do"][sl], st_c)
                dko_scr[sl, ln] = _mm(hd["vb"][sl], ds)
                dv_scr[sl, ln] = dv_scr[sl, ln] + _mm_nt(hd["ko"][sl], ds)
                dd_scr[sl, ln] = jnp.broadcast_to(jnp.sum(ds * st_c, axis=0, keepdims=True), (CHUNK, HEAD_DIM))
                hd["ds"] = hd["dec"][c * CHUNK:c * CHUNK + 1, :] * ds + _mm_tn(hd["do"][sl], hd["qi"][sl])

        for hh, ln in enumerate(lanes):
            hd = heads[hh]
            ds_scr[hh] = hd["ds"]
            q, k, lb = hd["q"], hd["k"], hd["lb"]
            dko_e = dko_scr[:, ln] * hd["e_ko"]
            dq = dqh_scr[:, ln] * hd["e_qa"] + dqi_scr[:, ln] * hd["e_b"]
            dk = dkh_scr[:, ln] * hd["e_ka"] + dko_e
            kd3 = (k * dko_e).reshape(nc, CHUNK, HEAD_DIM)
            last = jnp.broadcast_to(jnp.sum(kd3, axis=1, keepdims=True), kd3.shape).reshape(tb, HEAD_DIM)
            db = q * dq - k * dk + jnp.where(pos == CHUNK - 1, hd["dec"] * dd_scr[:, ln] + last, 0.0)
            dlg = _chunk_rev_cumsum(db, pos)
            dfv = dlg / hd["f"] - dk
            s_low = jnp.sum(dfv * (1.0 - hd["sg"]), axis=0, keepdims=True)
            dlow_ref[0:1, ln] += s_low * lb * (1.0 - lb)
            dlow_ref[1:2, ln] += -s_low * lb * hd["lb1"]
            df_ref[:, ln] = (dfv * (1.0 - lb) * hd["sg"] * (1.0 - hd["sg"])).astype(BF16)
            dq_ref[:, ln] = (dq * Q_SCALE * (hd["sq"] * (1.0 + hd["qr"] * (1.0 - hd["sq"])))).astype(BF16)
            di_ref[:, ln] = dv_scr[:, ln].astype(BF16)

    rt = lambda t: nb - 1 - t
    col = lambda p: pl.BlockSpec((tb, wid), lambda h, t: (rt(t), p * HEAD_GROUPS + h))
    hcol = pl.BlockSpec((tb, wid), lambda h, t: (rt(t), h))
    piece = jax.ShapeDtypeStruct((T, HG_WIDTH), BF16)
    tile = pltpu.VMEM((tb, wid), F32)
    return pl.pallas_call(
        _drop_operands(body, 9, len(after)), name="hg_bwd", grid=(HEAD_GROUPS, nb),
        in_specs=[col(0), col(1), col(2), col(3), pl.BlockSpec((2, wid), lambda h, t: (0, h)),
                  pl.BlockSpec((1, HEAD_DIM), lambda h, t: (0, 0)), hcol, hcol,
                  pl.BlockSpec((HEADS_PER_STEP, nc, HEAD_DIM, HEAD_DIM), lambda h, t: (h, rt(t), 0, 0))]
                 + [HBM_SPEC] * len(after),
        out_specs=[hcol, hcol, hcol, hcol, pl.BlockSpec((2, wid), lambda h, t: (0, h)),
                   pl.BlockSpec((1, HEAD_DIM), lambda h, t: (0, 0))],
        out_shape=[piece, piece, piece, piece, jax.ShapeDtypeStruct((2, HG_WIDTH), F32),
                   jax.ShapeDtypeStruct((1, HEAD_DIM), F32)],
        scratch_shapes=[pltpu.VMEM((HEADS_PER_STEP, HEAD_DIM, HEAD_DIM), F32), tile, tile, tile, tile, tile, tile],
        compiler_params=_params(("arbitrary", "arbitrary")),
    )(hg, hg, hg, hg, low, gn, o, dog, st, *after)


def _in_bwd(dparts, w_in, x, dx1, g, after=()):
    T = x.shape[0]
    tm = min(512, T)
    widths = [p.shape[1] for p in dparts]
    offs = [sum(widths[:i]) for i in range(len(widths))]
    n = len(dparts)

    def body(*refs):
        d_refs = refs[:n]
        w_ref, x_ref, dx1_ref, g_ref, dx_ref, dgn_ref = refs[n:]

        @pl.when(pl.program_id(0) == 0)
        def _():
            dgn_ref[...] = jnp.zeros_like(dgn_ref)

        dh = None
        for d_ref, off, wd in zip(d_refs, offs, widths):
            part = _mm(d_ref[...], w_ref[off:off + wd, :])
            dh = part if dh is None else dh + part
        xv = x_ref[...]
        r = lax.rsqrt(jnp.mean(xv * xv, axis=-1, keepdims=True) + EPS)
        xh = xv * r
        dgn_ref[...] += jnp.sum(dh * xh, axis=0, keepdims=True)
        w = dh * g_ref[...]
        dx_ref[...] = dx1_ref[...] + r * (w - xh * jnp.mean(w * xh, axis=-1, keepdims=True))

    row = lambda m: pl.BlockSpec((tm, m), lambda i: (i, 0))
    return pl.pallas_call(
        _drop_operands(body, n + 4, len(after)), name="in_bwd", grid=(T // tm,),
        in_specs=[row(wd) for wd in widths] + [_resident(w_in.shape), row(D_MODEL), row(D_MODEL), _full((1, D_MODEL))]
                 + [HBM_SPEC] * len(after),
        out_specs=[row(D_MODEL), _full((1, D_MODEL))],
        out_shape=[jax.ShapeDtypeStruct((T, D_MODEL), F32), jax.ShapeDtypeStruct((1, D_MODEL), F32)],
        compiler_params=_params(("arbitrary",)),
    )(*dparts, w_in, x, dx1, g, *after)


def _wgrad(name, at, b, tn, transposed=False, tk=2048, after=()):
    M, T = at.shape
    N = b.shape[1]
    tk = min(tk, T)
    nk = T // tk

    def body(a_ref, b_ref, o_ref, acc):
        k = pl.program_id(1)
        part = _mm(a_ref[...], b_ref[...])

        @pl.when(k == 0)
        def _():
            acc[...] = part

        @pl.when(k != 0)
        def _():
            acc[...] += part

        @pl.when(k == nk - 1)
        def _():
            o_ref[...] = (acc[...].T if transposed else acc[...]).astype(BF16)

    if transposed:
        out_spec, out_shape = pl.BlockSpec((tn, M), lambda j, k: (j, 0)), (N, M)
    else:
        out_spec, out_shape = pl.BlockSpec((M, tn), lambda j, k: (0, j)), (M, N)
    return pl.pallas_call(
        _drop_operands(body, 2, len(after)), name=name, grid=(N // tn, nk),
        in_specs=[pl.BlockSpec((M, tk), lambda j, k: (0, k)), pl.BlockSpec((tk, tn), lambda j, k: (k, j))]
                 + [HBM_SPEC] * len(after),
        out_specs=out_spec, out_shape=jax.ShapeDtypeStruct(out_shape, BF16),
        scratch_shapes=[pltpu.VMEM((M, tn), F32)],
        compiler_params=_params(("parallel", "arbitrary")),
    )(at, b, *after)


def _wgrad_branches(ogt, dya, cvot, dyb):
    M, T = ogt.shape
    N = dya.shape[1]
    tk = min(1024, T)
    nk = T // tk
    c = N // N_DEV

    def body(at_ref, da_ref, bt_ref, db_ref, oa_ref, ob_ref, acc_a, acc_b):
        k = pl.program_id(0)

        @pl.when(k == 0)
        def _():
            acc_a[...] = jnp.zeros_like(acc_a)
            acc_b[...] = jnp.zeros_like(acc_b)

        acc_a[...] += _mm(at_ref[...], da_ref[...])
        acc_b[...] += _mm(bt_ref[...], db_ref[...])

        @pl.when(k == nk - 1)
        def _():
            for s in range(N_DEV):
                oa_ref[s] = acc_a[:, s * c:(s + 1) * c].astype(BF16)
                ob_ref[s] = acc_b[:, s * c:(s + 1) * c].astype(BF16)

    lhs = pl.BlockSpec((M, tk), lambda k: (0, k))
    rhs = pl.BlockSpec((tk, N), lambda k: (k, 0))
    out = jax.ShapeDtypeStruct((N_DEV, M, c), BF16)
    return pl.pallas_call(
        body, name="wgrad_branches", grid=(nk,),
        in_specs=[lhs, rhs, lhs, rhs], out_specs=[_full((N_DEV, M, c))] * 2, out_shape=[out, out],
        scratch_shapes=[pltpu.VMEM((M, N), F32)] * 2,
        compiler_params=_params(("arbitrary",)),
    )(ogt, dya, cvot, dyb)


def _wgrad_in(ht, dparts, after=()):
    M, T = ht.shape
    tn = 512
    tk = min(2048, T)
    nk = T // tk
    nblk = [p.shape[1] // tn for p in dparts]
    start = [sum(nblk[:i]) for i in range(len(nblk))]
    n = len(dparts)

    def body(a_ref, *refs):
        d_refs, o_ref, acc = refs[:n], refs[n], refs[n + 1]
        j = pl.program_id(0)
        k = pl.program_id(1)

        @pl.when(k == 0)
        def _():
            acc[...] = jnp.zeros_like(acc)

        for d_ref, s, nb in zip(d_refs, start, nblk):
            @pl.when((j >= s) & (j < s + nb))
            def _():
                acc[...] += _mm(a_ref[...], d_ref[...])

        @pl.when(k == nk - 1)
        def _():
            o_ref[...] = acc[...].T.astype(BF16)

    def piece_spec(s, nb):
        def index(j, k):
            inside = (j >= s) & (j < s + nb)
            return jnp.where(inside, k, 0), jnp.clip(j - s, 0, nb - 1)
        return pl.BlockSpec((tk, tn), index)

    return pl.pallas_call(
        _drop_operands(body, 1 + n, len(after)), name="wgrad_in", grid=(sum(nblk), nk),
        in_specs=[pl.BlockSpec((M, tk), lambda j, k: (0, k))] + [piece_spec(s, nb) for s, nb in zip(start, nblk)]
                 + [HBM_SPEC] * len(after),
        out_specs=pl.BlockSpec((tn, M), lambda j, k: (j, 0)),
        out_shape=jax.ShapeDtypeStruct((sum(nblk) * tn, M), BF16),
        scratch_shapes=[pltpu.VMEM((M, tn), F32)],
        compiler_params=_params(("parallel", "arbitrary")),
    )(ht, *dparts, *after)


def _adamw_math(w, g, m, v):
    m = ADAM_B1 * m + (1.0 - ADAM_B1) * g
    v = ADAM_B2 * v + (1.0 - ADAM_B2) * (g * g)
    m_hat = m / (1.0 - ADAM_B1 ** ADAM_STEP)
    v_hat = v / (1.0 - ADAM_B2 ** ADAM_STEP)
    delta = -ADAM_LR * (m_hat / (jnp.sqrt(v_hat) + ADAM_EPS) + ADAM_WD * w)
    return delta, m, v


def _adamw_sum(name, w, parts, m, v):
    R, C = w.shape
    tr = _row_tile(R)

    def body(w_ref, p_ref, m_ref, v_ref, g_out, d_out, m_out, v_out):
        g = p_ref[0].astype(F32)
        for k in range(1, 4):
            g = g + p_ref[k].astype(F32)
        g_out[...] = g
        d_out[...], m_out[...], v_out[...] = _adamw_math(w_ref[...], g, m_ref[...], v_ref[...])

    blk = pl.BlockSpec((tr, C), lambda i: (i, 0))
    out = jax.ShapeDtypeStruct((R, C), F32)
    return pl.pallas_call(
        body, name=name, grid=(R // tr,),
        in_specs=[blk, pl.BlockSpec((4, tr, C), lambda i: (0, i, 0)), blk, blk],
        out_specs=[blk, blk, blk, blk], out_shape=[out, out, out, out],
        compiler_params=_params(("parallel",)),
    )(w, parts, m, v)


_SMALL_SLOTS = (("norm_mix_g", 0, 1, 1024), ("norm_ffn_g", 1, 1, 1024), ("norm_final_g", 2, 1, 1024),
                ("lower_bounds", 3, 2, 512), ("hg_norm_g", 5, 1, 128), ("loss", 6, 1, 128), ("conv_w", 8, 3, 512))
_SMALL_PARAMS = tuple(s for s in _SMALL_SLOTS if s[0] != "loss")
CONV_SHARD = CONV_WIDTH // N_DEV


def _small_pack(small):
    def body(*refs):
        out = refs[-1]
        out[...] = jnp.zeros_like(out)
        for ref, (_, row, rows, lanes) in zip(refs[:-1], _SMALL_SLOTS):
            out[row:row + rows, 0:lanes] = ref[...]

    vmem = pl.BlockSpec(memory_space=pltpu.VMEM)
    return pl.pallas_call(
        body, name="small_pack", in_specs=[vmem] * len(_SMALL_SLOTS), out_specs=vmem,
        out_shape=jax.ShapeDtypeStruct((SMALL_ROWS, 1024), F32),
    )(*[small[name] for name, _, _, _ in _SMALL_SLOTS])


def _small_update(gathered, dev, w, m, v):
    n = len(_SMALL_PARAMS)

    def body(dev_ref, g_ref, *refs):
        w_refs, m_refs, v_refs = refs[:n], refs[n:2 * n], refs[2 * n:3 * n]
        loss_ref, out_refs, sum_scr = refs[3 * n], refs[3 * n + 1:-1], refs[-1]
        total = g_ref[0]
        for k in range(1, N_DEV):
            total = total + g_ref[k]
        sum_scr[...] = total
        loss_ref[...] = sum_scr[6:7, 0:128]
        for p, (name, row, rows, lanes) in enumerate(_SMALL_PARAMS):
            if name == "conv_w":
                g = sum_scr[row:row + rows, 0:CONV_SHARD]
                for s in range(1, N_DEV):
                    g = jnp.where(dev_ref[0] == s, sum_scr[row:row + rows, s * CONV_SHARD:(s + 1) * CONV_SHARD], g)
            else:
                g = sum_scr[row:row + rows, 0:lanes]
            delta, m_new, v_new = _adamw_math(w_refs[p][...], g, m_refs[p][...], v_refs[p][...])
            out_refs[4 * p][...] = g
            out_refs[4 * p + 1][...] = delta
            out_refs[4 * p + 2][...] = m_new
            out_refs[4 * p + 3][...] = v_new

    vmem = pl.BlockSpec(memory_space=pltpu.VMEM)
    outs = [jax.ShapeDtypeStruct((1, 128), F32)]
    for a in w:
        outs += [jax.ShapeDtypeStruct(a.shape, F32)] * 4
    return pl.pallas_call(
        body, name="small_update",
        in_specs=[pl.BlockSpec(memory_space=pltpu.SMEM)] + [vmem] * (1 + 3 * n), out_specs=[vmem] * len(outs),
        out_shape=outs, scratch_shapes=[pltpu.VMEM((SMALL_ROWS, 1024), F32)],
    )(dev, gathered, *w, *m, *v)


def _row_tile(rows):
    for parts in (4, 2):
        if rows % (16 * parts) == 0:
            return rows // parts
    return rows


def _pair_sum(name, by_owner, got, core, after=()):
    n = len(got)

    def body(core_ref, *refs):
        for a_ref, b_ref, o_ref in zip(refs[:n], refs[n:2 * n], refs[2 * n:]):
            o_ref[...] = (a_ref[...].astype(F32) + b_ref[...].astype(F32)).astype(BF16)

    def blk(g):
        return pl.BlockSpec((None,) + g.shape[1:], lambda k, core_ref: (k, 0, 0))

    def mine(g):
        return pl.BlockSpec((None,) + g.shape[1:], lambda k, core_ref: (2 * k + core_ref[0], 0, 0))

    return pl.pallas_call(
        _drop_operands(body, 1 + 2 * n, len(after)), name=name,
        grid_spec=pltpu.PrefetchScalarGridSpec(
            num_scalar_prefetch=1, grid=(4,),
            in_specs=[mine(g) for g in got] + [blk(g) for g in got] + [HBM_SPEC] * len(after),
            out_specs=[blk(g) for g in got]),
        out_shape=[jax.ShapeDtypeStruct(g.shape, BF16) for g in got],
        compiler_params=_params(("parallel",)),
    )(core, *by_owner, *got, *after)


MESH = pl.DeviceIdType.MESH
HBM_SPEC = pl.BlockSpec(memory_space=pl.ANY)


def _handshake(peers):
    barrier = pltpu.get_barrier_semaphore()
    for peer in peers:
        pl.semaphore_signal(barrier, inc=1, device_id=peer, device_id_type=MESH)
    pl.semaphore_wait(barrier, len(peers))


def _comm_call(body, name, operands, out_shape, scratch, collective_id):
    if collective_id is None:
        return pl.pallas_call(body, name=name, in_specs=[HBM_SPEC] * len(operands), out_specs=[HBM_SPEC] * len(out_shape),
                              out_shape=out_shape, scratch_shapes=scratch)(*operands)
    return pl.kernel(body, out_type=out_shape, mesh=plsc.ScalarSubcoreMesh(axis_name="sequencer", num_cores=1),
                     scratch_types=scratch, name=name,
                     compiler_params=pltpu.CompilerParams(collective_id=collective_id))(*operands)


def _all_gather(name, blocks, collective_id=None, after=()):
    n = len(blocks)
    na = len(after)

    def body(*refs):
        x_refs, out_refs = refs[:n], refs[n + na:2 * n + na]
        send_sems, recv_sems, local_sems = refs[2 * n + na:]
        x, y, c = lax.axis_index("x"), lax.axis_index("y"), lax.axis_index("c")
        me, sibling = (x, y, c), (x, y, 1 - c)
        chips = [(1 - x, y), (x, 1 - y), (1 - x, 1 - y)]
        if collective_id is not None:
            _handshake([sibling] + [(*chip, c) for chip in chips])

        def slot(i, px, py, pc):
            return out_refs[i].at[4 * px + 2 * py + pc]

        def copy(i, k, blk, to, src=None):
            return pltpu.make_async_remote_copy(
                src_ref=slot(i, *blk) if src is None else src, dst_ref=slot(i, *blk),
                send_sem=send_sems.at[7 * i + k], recv_sem=recv_sems.at[7 * i + k], device_id=to, device_id_type=MESH)

        mine = [pltpu.make_async_copy(x_refs[i], slot(i, *me), local_sems.at[i]) for i in range(n)]
        for cp in mine:
            cp.start()
        first = []
        for i in range(n):
            first.append(copy(i, 0, me, sibling, src=x_refs[i]))
            first += [copy(i, 1 + j, me, (*chip, c), src=x_refs[i]) for j, chip in enumerate(chips)]
        for cp in first:
            cp.start()
        passed = []
        for i in range(n):
            for j, chip in enumerate(chips):
                copy(i, 1 + j, (*chip, c), me).wait_recv()
                passed.append(copy(i, 4 + j, (*chip, c), sibling))
                passed[-1].start()
        for i in range(n):
            copy(i, 0, sibling, me).wait_recv()
            for j, chip in enumerate(chips):
                copy(i, 4 + j, (*chip, 1 - c), me).wait_recv()
        for cp in first + passed:
            cp.wait_send()
        for cp in mine:
            cp.wait()

    return _comm_call(
        body, name, list(blocks) + list(after), [jax.ShapeDtypeStruct((N_DEV,) + b.shape, b.dtype) for b in blocks],
        [pltpu.SemaphoreType.DMA((7 * n,)), pltpu.SemaphoreType.DMA((7 * n,)), pltpu.SemaphoreType.DMA((n,))],
        collective_id)


def _sibling_swap(name, by_owner, collective_id=None, after=()):
    n = len(by_owner)
    na = len(after)

    def body(*refs):
        x_refs, out_refs = refs[:n], refs[n + na:2 * n + na]
        send_sems, recv_sems = refs[2 * n + na:]
        x, y, c = lax.axis_index("x"), lax.axis_index("y"), lax.axis_index("c")
        if collective_id is not None:
            _handshake([(x, y, 1 - c)])
        copies = []
        for i in range(n):
            for k in range(4):
                copies.append(pltpu.make_async_remote_copy(
                    src_ref=x_refs[i].at[2 * k + 1 - c], dst_ref=out_refs[i].at[k],
                    send_sem=send_sems.at[4 * i + k], recv_sem=recv_sems.at[4 * i + k],
                    device_id=(x, y, 1 - c), device_id_type=MESH))
        for cp in copies:
            cp.start()
        for cp in copies:
            cp.wait()

    return _comm_call(
        body, name, list(by_owner) + list(after),
        [jax.ShapeDtypeStruct((4,) + b.shape[1:], b.dtype) for b in by_owner],
        [pltpu.SemaphoreType.DMA((4 * n,)), pltpu.SemaphoreType.DMA((4 * n,))], collective_id)


def _chip_exchange(name, sums, collective_id=None, after=()):
    n = len(sums)
    na = len(after)

    def body(*refs):
        x_refs, out_refs = refs[:n], refs[n + na:2 * n + na]
        send_sems, recv_sems, local_sems = refs[2 * n + na:]
        x, y, c = lax.axis_index("x"), lax.axis_index("y"), lax.axis_index("c")
        chips = [(1 - x, y), (x, 1 - y), (1 - x, 1 - y)]
        my_chip = 2 * x + y
        if collective_id is not None:
            _handshake([(cx, cy, c) for cx, cy in chips])
        mine = [pltpu.make_async_copy(x_refs[i].at[my_chip], out_refs[i].at[my_chip], local_sems.at[i])
                for i in range(n)]
        for cp in mine:
            cp.start()
        sends = []
        for i in range(n):
            for j, (cx, cy) in enumerate(chips):
                sends.append(pltpu.make_async_remote_copy(
                    src_ref=x_refs[i].at[2 * cx + cy], dst_ref=out_refs[i].at[my_chip],
                    send_sem=send_sems.at[3 * i + j], recv_sem=recv_sems.at[3 * i + j],
                    device_id=(cx, cy, c), device_id_type=MESH))
        for cp in sends:
            cp.start()
        for i in range(n):
            for j, (cx, cy) in enumerate(chips):
                pltpu.make_async_remote_copy(
                    src_ref=x_refs[i].at[my_chip], dst_ref=out_refs[i].at[2 * cx + cy],
                    send_sem=send_sems.at[3 * i + j], recv_sem=recv_sems.at[3 * i + j],
                    device_id=(cx, cy, c), device_id_type=MESH).wait_recv()
        for cp in sends:
            cp.wait_send()
        for cp in mine:
            cp.wait()

    return _comm_call(
        body, name, list(sums) + list(after), [jax.ShapeDtypeStruct(s.shape, s.dtype) for s in sums],
        [pltpu.SemaphoreType.DMA((3 * n,)), pltpu.SemaphoreType.DMA((3 * n,)), pltpu.SemaphoreType.DMA((n,))],
        collective_id)


def _cast_shards(shards):
    n = len(shards)

    def body(*refs):
        for i in range(n):
            refs[n + i][...] = refs[i][...].astype(BF16)

    vmem = pl.BlockSpec(memory_space=pltpu.VMEM)
    return pl.pallas_call(
        body, name="cast_shards", in_specs=[vmem] * n, out_specs=[vmem] * n,
        out_shape=[jax.ShapeDtypeStruct(s.shape, BF16) for s in shards],
        compiler_params=pltpu.CompilerParams(vmem_limit_bytes=VMEM_LIMIT_V7X),
    )(*shards)


BIG = ("w_in", "w_branch_a", "w_branch_b", "w_out", "w_ffn_gate", "w_ffn_up", "w_ffn_down")


def _local_step(x, target, gains, low, conv_w, wg8, reduce):
    g_mix, g_hg, g_ffn, g_fin = gains
    w_in = wg8["w_in"].reshape(N_IN, D_MODEL)
    wg = wg8["w_ffn_gate"].reshape(D_FF, D_MODEL)
    wu = wg8["w_ffn_up"].reshape(D_FF, D_MODEL)
    wa, wb = wg8["w_branch_a"], wg8["w_branch_b"]
    wo = wg8["w_out"].reshape(D_MODEL, D_MODEL)
    wd = wg8["w_ffn_down"].reshape(D_FF, D_MODEL)

    ht, hg, cv, gt, cvo, cvot = _fwd_in(x, g_mix, w_in, conv_w)
    o, og, ogt, st = _hg_fwd(hg, low, g_hg)
    x1, mgt, ya, yb = _merge_fwd(og, cvo, gt, x, wa, wb, wo)
    h2t, gate, up, actt, loss, d_gfin, dx2 = _ffn_fwd_loss(x1, g_ffn, wg, wu, wd, target, g_fin)

    dgate, dup, dx1, d_gffn = _ffn_bwd(dx2, x1, gate, up, g_ffn, wg, wu, wd)
    ffn = dict(
        w_ffn_down=_wgrad("wgrad_ffn_down", actt, dx2, 512).reshape(N_DEV, D_FF // N_DEV, D_MODEL),
        w_ffn_gate=_wgrad("wgrad_ffn_gate", h2t, dgate, 1408, transposed=True).reshape(N_DEV, D_FF // N_DEV, D_MODEL),
        w_ffn_up=_wgrad("wgrad_ffn_up", h2t, dup, 1408, transposed=True).reshape(N_DEV, D_FF // N_DEV, D_MODEL))
    dgt, dya, dyb, dog, dcv, d_conv = _merge_bwd(dx1, ya, yb, gt, cv, wa, wb, wo, conv_w)
    sums_ffn, got_ffn = reduce.begin(ffn, sum_after=[dya])
    parts_ffn, updated_ffn = reduce.finish(ffn, sums_ffn)
    grad_a, grad_b = _wgrad_branches(ogt, dya, cvot, dyb)
    out = dict(
        w_out=_wgrad("wgrad_out", mgt, dx1, 512, after=sums_ffn[:1]).reshape(N_DEV, D_MODEL // N_DEV, D_MODEL),
        w_branch_a=grad_a, w_branch_b=grad_b)
    dq, df, di, dg, d_low, d_ghg = _hg_bwd(dog, hg, o, st, low, g_hg, after=list(sums_ffn) + [out["w_out"]])
    sums_out, got_out = reduce.begin(out, after=[parts_ffn[0], dq], sum_after=updated_ffn)
    parts_out, updated_out = reduce.finish(out, sums_out)
    dparts = [dq, df, di, dg, dcv, dgt]
    w_in_grad = dict(w_in=_wgrad_in(ht, dparts, after=sums_out[:1]).reshape(N_DEV, N_IN // N_DEV, D_MODEL))
    sums_in, _ = reduce.begin(w_in_grad, after=parts_out[:1], sum_after=updated_out)
    parts_in, _ = reduce.finish(w_in_grad, sums_in)
    grad_x, d_gmix = _in_bwd(dparts, w_in, x, dx1, g_mix, after=list(parts_out[:1]) + list(sums_in))
    small = dict(norm_mix_g=d_gmix, norm_ffn_g=d_gffn, norm_final_g=d_gfin, lower_bounds=d_low, hg_norm_g=d_ghg,
                 conv_w=d_conv, loss=loss)
    return grad_x, small, parts_in


def _conv_shard_rows(a):
    return jnp.pad(a, ((0, 5), (0, 64)))


def kernel(x, norm_mix_g, w_in, lower_bounds, hg_norm_g, conv_w, w_branch_a, w_branch_b, w_out, norm_ffn_g, w_ffn_gate, w_ffn_up, w_ffn_down, norm_final_g, loss_target, m_norm_mix_g, m_w_in, m_lower_bounds, m_hg_norm_g, m_conv_w, m_w_branch_a, m_w_branch_b, m_w_out, m_norm_ffn_g, m_w_ffn_gate, m_w_ffn_up, m_w_ffn_down, m_norm_final_g, v_norm_mix_g, v_w_in, v_lower_bounds, v_hg_norm_g, v_conv_w, v_w_branch_a, v_w_branch_b, v_w_out, v_norm_ffn_g, v_w_ffn_gate, v_w_ffn_up, v_w_ffn_down, v_norm_final_g):
    cx, cy, cc = lax.axis_index("x"), lax.axis_index("y"), lax.axis_index("c")
    my_dev = 4 * cx + 2 * cy + cc

    def tr(a):
        return a[0].T

    big = dict(w_in=tr(w_in), w_branch_a=w_branch_a[0], w_branch_b=w_branch_b[0], w_out=w_out[0],
               w_ffn_gate=tr(w_ffn_gate), w_ffn_up=tr(w_ffn_up), w_ffn_down=w_ffn_down[0])
    big_m = dict(w_in=tr(m_w_in), w_branch_a=m_w_branch_a[0], w_branch_b=m_w_branch_b[0], w_out=m_w_out[0],
                 w_ffn_gate=tr(m_w_ffn_gate), w_ffn_up=tr(m_w_ffn_up), w_ffn_down=m_w_ffn_down[0])
    big_v = dict(w_in=tr(v_w_in), w_branch_a=v_w_branch_a[0], w_branch_b=v_w_branch_b[0], w_out=v_w_out[0],
                 w_ffn_gate=tr(v_w_ffn_gate), w_ffn_up=tr(v_w_ffn_up), w_ffn_down=v_w_ffn_down[0])
    transposed = ("w_in", "w_ffn_gate", "w_ffn_up")

    shards = dict(zip(BIG, _cast_shards([big[n] for n in BIG])))
    first = _all_gather("gather_w_in", [shards["w_in"], _conv_shard_rows(conv_w[0])])
    ids = iter(range(1, 16))
    mid = _all_gather("gather_mid", [shards[n] for n in BIG[1:4]], collective_id=next(ids), after=first[1:])
    ffn = _all_gather("gather_ffn", [shards[n] for n in BIG[4:]], collective_id=next(ids), after=first[1:])
    wg8 = dict(zip(BIG, [first[0]] + list(mid) + list(ffn)))
    conv_full = first[1][:, :3, :64].transpose(1, 0, 2).reshape(3, CONV_WIDTH)

    core = cc.reshape(1).astype(jnp.int32)
    outs = {}

    class Reduce:
        @staticmethod
        def begin(grads, after=(), sum_after=()):
            names = list(grads)
            by_owner = [grads[n] for n in names]
            got = _sibling_swap("sibling_swap_" + names[0], by_owner, collective_id=next(ids), after=after)
            sums = _pair_sum("pair_sum_" + names[0], by_owner, got, core, after=sum_after)
            return sums, got

        @staticmethod
        def finish(grads, chip_sums, after=()):
            names = list(grads)
            parts = _chip_exchange("chip_exchange_" + names[0], chip_sums, collective_id=next(ids), after=after)
            for n, p in zip(names, parts):
                outs[n] = _adamw_sum("adamw_" + n, big[n], p, big_m[n], big_v[n])
            return parts, [outs[n][1] for n in names]

    gains = (norm_mix_g, hg_norm_g, norm_ffn_g, norm_final_g.reshape(1, D_MODEL))
    grad_x, small, last = _local_step(x[0], loss_target[0], gains, lower_bounds, conv_full, wg8, Reduce)

    small_all = _all_gather("gather_small", [_small_pack(small)], collective_id=next(ids), after=last[:1])

    def small_state(a):
        return [a[0], a[1], a[2].reshape(1, D_MODEL), a[3], a[4], a[5][0]]

    upd = _small_update(
        small_all[0], my_dev.reshape(1).astype(jnp.int32),
        small_state((norm_mix_g, norm_ffn_g, norm_final_g, lower_bounds, hg_norm_g, conv_w)),
        small_state((m_norm_mix_g, m_norm_ffn_g, m_norm_final_g, m_lower_bounds, m_hg_norm_g, m_conv_w)),
        small_state((v_norm_mix_g, v_norm_ffn_g, v_norm_final_g, v_lower_bounds, v_hg_norm_g, v_conv_w)))
    loss = upd[0][0, 0]
    small_shape = dict(norm_final_g=(D_MODEL,), conv_w=(1, 3, CONV_SHARD))
    for p, (name, _, _, _) in enumerate(_SMALL_PARAMS):
        outs[name] = [a.reshape(small_shape.get(name, a.shape)) for a in upd[1 + 4 * p:5 + 4 * p]]

    order = ["norm_mix_g", "w_in", "lower_bounds", "hg_norm_g", "conv_w", "w_branch_a", "w_branch_b", "w_out",
             "norm_ffn_g", "w_ffn_gate", "w_ffn_up", "w_ffn_down", "norm_final_g"]
    result = [loss, grad_x[None]]
    for k in range(4):
        for n in order:
            if n in BIG:
                result.append((outs[n][k].T if n in transposed else outs[n][k])[None])
            else:
                result.append(outs[n][k])
    return tuple(result)
```

```python
import jax
import jax.numpy as jnp
from jax import lax
from jax.experimental import pallas as pl
from jax.experimental.pallas import tpu as pltpu
from jax.experimental.pallas import tpu_sc as plsc

F32 = jnp.float32
BF16 = jnp.bfloat16
STASH = jnp.bfloat16

D_MODEL = 1024
HG_WIDTH = 512
HEAD_DIM = 128
N_HEADS = 4
HEADS_PER_STEP = 4
HEAD_GROUPS = N_HEADS // HEADS_PER_STEP
CONV_WIDTH = 512
CONV_K = 3
D_FF = 2816
CHUNK = 32
EPS = 1e-6
Q_SCALE = HEAD_DIM ** -0.5
N_DEV = 8

ADAM_LR = 0.001
ADAM_B1 = 0.9
ADAM_B2 = 0.999
ADAM_EPS = 1e-08
ADAM_WD = 0.01
ADAM_STEP = 10

VMEM_LIMIT_V7X = 56 * 1024 * 1024

SMALL_ROWS = 16


def _params(sem, vmem=VMEM_LIMIT_V7X):
    return pltpu.CompilerParams(dimension_semantics=sem, vmem_limit_bytes=vmem)


def _mm(a, b):
    return jnp.dot(a.astype(BF16), b.astype(BF16), preferred_element_type=F32)


def _mm_nt(a, b):
    return lax.dot_general(a.astype(BF16), b.astype(BF16), (((1,), (1,)), ((), ())), preferred_element_type=F32)


def _mm_tn(a, b):
    return lax.dot_general(a.astype(BF16), b.astype(BF16), (((0,), (0,)), ((), ())), preferred_element_type=F32)


def _sigmoid(x):
    return 0.5 * jnp.tanh(0.5 * x) + 0.5


def _resident(shape):
    nd = len(shape)
    return pl.BlockSpec(shape, lambda *_: (0,) * nd, pipeline_mode=pl.Buffered(1))


def _full(shape):
    nd = len(shape)
    return pl.BlockSpec(shape, lambda *_: (0,) * nd)


def _shard_cols(w_ref):
    return jnp.concatenate([w_ref[s] for s in range(N_DEV)], axis=1)


N_HG = 4 * HG_WIDTH
N_CV = 3 * CONV_WIDTH
N_GT = 2 * D_MODEL
N_IN = N_HG + N_CV + N_GT


def _col(tm, n):
    return pl.BlockSpec((n, tm), lambda i: (0, i))


HALO = 8


def _fwd_in(x, g, w_in_t, conv_w):
    T = x.shape[0]
    tm = min(512, T)

    def body(x_ref, g_ref, w_ref, cw_ref, ht_ref, hg_ref, cv_ref, gt_ref, cvo_ref, cvot_ref, tail_scr):
        @pl.when(pl.program_id(0) == 0)
        def _():
            tail_scr[...] = jnp.zeros_like(tail_scr)

        xv = x_ref[...]
        r = lax.rsqrt(jnp.mean(xv * xv, axis=-1, keepdims=True) + EPS)
        hf = xv * r * g_ref[...]
        h = hf.astype(BF16)
        ht_ref[...] = hf.T.astype(BF16)
        hg_ref[...] = _mm_nt(h, w_ref[:N_HG, :])
        cv = _mm_nt(h, w_ref[N_HG:N_HG + N_CV, :])
        cv_ref[...] = cv.astype(STASH)
        gt_ref[...] = _mm_nt(h, w_ref[N_HG + N_CV:, :]).astype(STASH)

        u = cv[:, :CONV_WIDTH] * cv[:, 2 * CONV_WIDTH:]
        row = lax.broadcasted_iota(jnp.int32, u.shape, 0)
        prev1 = tail_scr[HALO - 1:HALO, :]
        prev2 = tail_scr[HALO - 2:HALO - 1, :]
        u1 = jnp.where(row >= 1, pltpu.roll(u, 1, 0), prev1)
        u2 = jnp.where(row >= 2, pltpu.roll(u, 2, 0), jnp.where(row == 1, prev1, prev2))
        y = cw_ref[0:1, :] * u2 + cw_ref[1:2, :] * u1 + cw_ref[2:3, :] * u
        out = cv[:, CONV_WIDTH:2 * CONV_WIDTH] * y
        cvo_ref[...] = out.astype(BF16)
        cvot_ref[...] = out.T.astype(BF16)
        tail_scr[...] = u[tm - HALO:, :]

    row = lambda n: pl.BlockSpec((tm, n), lambda i: (i, 0))
    return pl.pallas_call(
        body, name="fwd_in", grid=(T // tm,),
        in_specs=[row(D_MODEL), _full((1, D_MODEL)), _resident(w_in_t.shape), _full((CONV_K, CONV_WIDTH))],
        out_specs=[_col(tm, D_MODEL), row(N_HG), row(N_CV), row(N_GT), row(CONV_WIDTH), _col(tm, CONV_WIDTH)],
        out_shape=[jax.ShapeDtypeStruct((D_MODEL, T), BF16), jax.ShapeDtypeStruct((T, N_HG), F32),
                   jax.ShapeDtypeStruct((T, N_CV), STASH), jax.ShapeDtypeStruct((T, N_GT), STASH),
                   jax.ShapeDtypeStruct((T, CONV_WIDTH), BF16), jax.ShapeDtypeStruct((CONV_WIDTH, T), BF16)],
        scratch_shapes=[pltpu.VMEM((HALO, CONV_WIDTH), F32)],
        compiler_params=_params(("arbitrary",)),
    )(x, g, w_in_t, conv_w)


def _chunk_pos(shape):
    return lax.broadcasted_iota(jnp.int32, shape, 0) & (CHUNK - 1)


def _chunk_cumsum(x, pos):
    s = 1
    while s < CHUNK:
        x = x + jnp.where(pos >= s, pltpu.roll(x, s, 0), 0.0)
        s *= 2
    return x


def _chunk_rev_cumsum(x, pos):
    n = x.shape[0]
    s = 1
    while s < CHUNK:
        x = x + jnp.where(pos + s < CHUNK, pltpu.roll(x, n - s, 0), 0.0)
        s *= 2
    return x


def _chunk_bcast(x3, row, tb):
    return jnp.broadcast_to(x3[:, row:row + 1, :], x3.shape).reshape(tb, x3.shape[-1])


def _lower_bound(low_ref):
    l0 = low_ref[0:1, :]
    l1 = low_ref[1:2, :]
    m = jnp.maximum(l0, l1)
    e0 = jnp.exp(l0 - m)
    e1 = jnp.exp(l1 - m)
    return e0 / (e0 + e1), e1 / (e0 + e1)


def _hg_gates(qr, fr, lb, pos, tb):
    sq = _sigmoid(qr)
    q = qr * sq * Q_SCALE
    sg = _sigmoid(fr)
    f = lb + (1.0 - lb) * sg
    k = 1.0 - f
    b = _chunk_cumsum(jnp.log(f), pos)
    b3 = b.reshape(tb // CHUNK, CHUNK, HEAD_DIM)
    anc = _chunk_bcast(b3, CHUNK // 2 - 1, tb)
    blb = _chunk_bcast(b3, CHUNK - 1, tb)
    e_qa = jnp.exp(b - anc)
    e_ka = jnp.exp(anc - b)
    e_b = jnp.exp(b)
    e_ko = jnp.exp(blb - b)
    dec = jnp.exp(blb)
    return sq, q, sg, f, k, e_qa, e_ka, e_b, e_ko, dec


def _intra_mask(sb):
    r = lax.broadcasted_iota(jnp.int32, (sb, sb), 0)
    c = lax.broadcasted_iota(jnp.int32, (sb, sb), 1)
    return ((r // CHUNK) == (c // CHUNK)) & (c <= r)


def _hg_fwd(hg, low, gn):
    T = hg.shape[0]
    tb = min(1024, T)
    sb = min(256, tb)
    nb = T // tb
    nc = tb // CHUNK
    wid = HEADS_PER_STEP * HEAD_DIM

    def body(q_ref, f_ref, i_ref, g_ref, low_ref, gn_ref, o_ref, og_ref, ogt_ref, st_ref, s_scr):
        t = pl.program_id(1)

        @pl.when(t == 0)
        def _():
            s_scr[...] = jnp.zeros_like(s_scr)

        pos = _chunk_pos((tb, HEAD_DIM))
        mask = _intra_mask(sb)
        lanes = [slice(hh * HEAD_DIM, (hh + 1) * HEAD_DIM) for hh in range(HEADS_PER_STEP)]
        qi, ko, vb, dec, st = [], [], [], [], []
        for hh, ln in enumerate(lanes):
            lb, _ = _lower_bound(low_ref.at[:, ln])
            _, q, _, _, k, e_qa, e_ka, e_b, e_ko, dec_h = _hg_gates(q_ref[:, ln], f_ref[:, ln], lb, pos, tb)
            qh = (q * e_qa).astype(BF16)
            kh = (k * e_ka).astype(BF16)
            qi.append((q * e_b).astype(BF16))
            ko.append((k * e_ko).astype(BF16))
            vb.append(i_ref[:, ln].astype(BF16))
            dec.append(dec_h)
            st.append(s_scr[hh])
            for s in range(tb // sb):
                sl = slice(s * sb, (s + 1) * sb)
                p = jnp.where(mask, _mm_nt(qh[sl], kh[sl]), 0.0)
                o_ref[sl, ln] = _mm(p, vb[hh][sl])
        for c in range(nc):
            sl = slice(c * CHUNK, (c + 1) * CHUNK)
            for hh, ln in enumerate(lanes):
                st_ref[hh, c] = st[hh]
                o_ref[sl, ln] = o_ref[sl, ln] + _mm_nt(qi[hh][sl], st[hh])
                st[hh] = dec[hh][c * CHUNK:c * CHUNK + 1, :] * st[hh] + _mm_tn(vb[hh][sl], ko[hh][sl])
        for hh, ln in enumerate(lanes):
            s_scr[hh] = st[hh]
            o = o_ref[:, ln]
            r = lax.rsqrt(jnp.mean(o * o, axis=-1, keepdims=True) + EPS)
            gr = g_ref[:, ln]
            og = (o * r * gn_ref[...]) * (gr * _sigmoid(gr))
            og_ref[:, ln] = og.astype(BF16)
            ogt_ref[ln, :] = og.T.astype(BF16)

    col = lambda p: pl.BlockSpec((tb, wid), lambda h, t: (t, p * HEAD_GROUPS + h))
    hcol = pl.BlockSpec((tb, wid), lambda h, t: (t, h))
    return pl.pallas_call(
        body, name="hg_fwd", grid=(HEAD_GROUPS, nb),
        in_specs=[col(0), col(1), col(2), col(3), pl.BlockSpec((2, wid), lambda h, t: (0, h)),
                  pl.BlockSpec((1, HEAD_DIM), lambda h, t: (0, 0))],
        out_specs=[hcol, hcol, pl.BlockSpec((wid, tb), lambda h, t: (h, t)),
                   pl.BlockSpec((HEADS_PER_STEP, nc, HEAD_DIM, HEAD_DIM), lambda h, t: (h, t, 0, 0))],
        out_shape=[jax.ShapeDtypeStruct((T, HG_WIDTH), F32), jax.ShapeDtypeStruct((T, HG_WIDTH), BF16),
                   jax.ShapeDtypeStruct((HG_WIDTH, T), BF16),
                   jax.ShapeDtypeStruct((N_HEADS, T // CHUNK, HEAD_DIM, HEAD_DIM), F32)],
        scratch_shapes=[pltpu.VMEM((HEADS_PER_STEP, HEAD_DIM, HEAD_DIM), F32)],
        compiler_params=_params(("parallel", "arbitrary")),
    )(hg, hg, hg, hg, low, gn)


def _merge_fwd(og, cvo, gt, x, wa, wb, wo):
    T = x.shape[0]
    tm = min(1024, T)

    def body(og_ref, cvo_ref, gt_ref, x_ref, wa_ref, wb_ref, wo_ref, x1_ref, mgt_ref):
        ya = jnp.dot(og_ref[...], _shard_cols(wa_ref), preferred_element_type=F32)
        yb = jnp.dot(cvo_ref[...], _shard_cols(wb_ref), preferred_element_type=F32)
        m = (_sigmoid(gt_ref[:, :D_MODEL].astype(F32)) * ya
             + _sigmoid(gt_ref[:, D_MODEL:].astype(F32)) * yb)
        mgt_ref[...] = m.T.astype(BF16)
        x1_ref[...] = x_ref[...] + jnp.dot(m.astype(BF16), wo_ref[...], preferred_element_type=F32)

    row = lambda n: pl.BlockSpec((tm, n), lambda i: (i, 0))
    return pl.pallas_call(
        body, name="merge_fwd", grid=(T // tm,),
        in_specs=[row(HG_WIDTH), row(CONV_WIDTH), row(2 * D_MODEL), row(D_MODEL),
                  _resident(wa.shape), _resident(wb.shape), _resident(wo.shape)],
        out_specs=[row(D_MODEL), _col(tm, D_MODEL)],
        out_shape=[jax.ShapeDtypeStruct((T, D_MODEL), F32), jax.ShapeDtypeStruct((D_MODEL, T), BF16)],
        compiler_params=_params(("parallel",)),
    )(og, cvo, gt, x, wa, wb, wo)


def _ffn_fwd_loss(x1, g, wg, wu, wd, target, g_fin):
    T = x1.shape[0]
    tm = min(256, T)

    def body(x_ref, g_ref, wg_ref, wu_ref, wd_ref, t_ref, gf_ref,
             ht_ref, gate_ref, up_ref, actt_ref, loss_ref, dgf_ref, dx2_ref):
        @pl.when(pl.program_id(0) == 0)
        def _():
            loss_ref[...] = jnp.zeros_like(loss_ref)
            dgf_ref[...] = jnp.zeros_like(dgf_ref)

        xv = x_ref[...]
        r = lax.rsqrt(jnp.mean(xv * xv, axis=-1, keepdims=True) + EPS)
        hf = xv * r * g_ref[...]
        h = hf.astype(BF16)
        ht_ref[...] = hf.T.astype(BF16)
        gate = _mm_nt(h, wg_ref[...])
        up = _mm_nt(h, wu_ref[...])
        gate_ref[...] = gate.astype(STASH)
        up_ref[...] = up.astype(STASH)
        act = gate * _sigmoid(gate) * up
        actt_ref[...] = act.T.astype(BF16)
        x2 = xv + jnp.dot(act.astype(BF16), wd_ref[...], preferred_element_type=F32)

        gv = gf_ref[...]
        r2 = lax.rsqrt(jnp.mean(x2 * x2, axis=-1, keepdims=True) + EPS)
        xh = x2 * r2
        err = xh * gv - t_ref[...]
        loss_ref[...] += 0.5 * jnp.sum(jnp.mean(err * err, axis=-1, keepdims=True), axis=0, keepdims=True)
        dy = err * (1.0 / D_MODEL)
        dgf_ref[...] += jnp.sum(dy * xh, axis=0, keepdims=True)
        w = dy * gv
        dx2_ref[...] = r2 * (w - xh * jnp.mean(w * xh, axis=-1, keepdims=True))

    row = lambda n: pl.BlockSpec((tm, n), lambda i: (i, 0))
    return pl.pallas_call(
        body, name="ffn_fwd_loss", grid=(T // tm,),
        in_specs=[row(D_MODEL), _full((1, D_MODEL)), _resident(wg.shape), _resident(wu.shape), _resident(wd.shape),
                  row(D_MODEL), _full((1, D_MODEL))],
        out_specs=[_col(tm, D_MODEL), row(D_FF), row(D_FF), _col(tm, D_FF), _full((1, 128)), _full((1, D_MODEL)),
                   row(D_MODEL)],
        out_shape=[jax.ShapeDtypeStruct((D_MODEL, T), BF16), jax.ShapeDtypeStruct((T, D_FF), STASH),
                   jax.ShapeDtypeStruct((T, D_FF), STASH), jax.ShapeDtypeStruct((D_FF, T), BF16),
                   jax.ShapeDtypeStruct((1, 128), F32), jax.ShapeDtypeStruct((1, D_MODEL), F32),
                   jax.ShapeDtypeStruct((T, D_MODEL), F32)],
        compiler_params=_params(("arbitrary",)),
    )(x1, g, wg, wu, wd, target, g_fin)


def _ffn_bwd(dx2, x1, gate, up, g, wg, wu, wd):
    T = x1.shape[0]
    tm = min(256, T)

    def body(dx2_ref, x_ref, gate_ref, up_ref, g_ref, wg_ref, wu_ref, wd_ref, dgate_ref, dup_ref, dx1_ref, dgn_ref):
        @pl.when(pl.program_id(0) == 0)
        def _():
            dgn_ref[...] = jnp.zeros_like(dgn_ref)

        dx2 = dx2_ref[...]
        dact = _mm_nt(dx2, wd_ref[...])
        gate = gate_ref[...].astype(F32)
        s = _sigmoid(gate)
        dgate = (dact * up_ref[...].astype(F32) * (s * (1.0 + gate * (1.0 - s)))).astype(BF16)
        dup = (dact * (gate * s)).astype(BF16)
        dgate_ref[...] = dgate
        dup_ref[...] = dup
        dh = _mm(dgate, wg_ref[...]) + _mm(dup, wu_ref[...])
        xv = x_ref[...]
        r = lax.rsqrt(jnp.mean(xv * xv, axis=-1, keepdims=True) + EPS)
        xh = xv * r
        dgn_ref[...] += jnp.sum(dh * xh, axis=0, keepdims=True)
        w = dh * g_ref[...]
        dx1_ref[...] = dx2 + r * (w - xh * jnp.mean(w * xh, axis=-1, keepdims=True))

    row = lambda n: pl.BlockSpec((tm, n), lambda i: (i, 0))
    return pl.pallas_call(
        body, name="ffn_bwd", grid=(T // tm,),
        in_specs=[row(D_MODEL), row(D_MODEL), row(D_FF), row(D_FF), _full((1, D_MODEL)),
                  _resident(wg.shape), _resident(wu.shape), _resident(wd.shape)],
        out_specs=[row(D_FF), row(D_FF), row(D_MODEL), _full((1, D_MODEL))],
        out_shape=[jax.ShapeDtypeStruct((T, D_FF), BF16), jax.ShapeDtypeStruct((T, D_FF), BF16),
                   jax.ShapeDtypeStruct((T, D_MODEL), F32), jax.ShapeDtypeStruct((1, D_MODEL), F32)],
        compiler_params=_params(("arbitrary",)),
    )(dx2, x1, gate, up, g, wg, wu, wd)


def _merge_bwd(dx1, og, cvo, gt, cv, wa, wb, wo, conv_w):
    T = dx1.shape[0]
    tm = min(512, T)
    nt = T // tm

    def body(dx_ref, og_ref, cvo_ref, gt_ref, cv_ref, halo_ref, wa_ref, wb_ref, wo_ref, cw_ref,
             dgt_ref, dya_ref, dyb_ref, dog_ref, dcv_ref, dcw_ref, prev_u, next_dy):
        step = pl.program_id(0)

        @pl.when(step == 0)
        def _():
            next_dy[...] = jnp.zeros_like(next_dy)
            dcw_ref[...] = jnp.zeros_like(dcw_ref)

        dm = _mm_nt(dx_ref[...], wo_ref[...])
        wa = _shard_cols(wa_ref)
        wb = _shard_cols(wb_ref)
        ya = jnp.dot(og_ref[...], wa, preferred_element_type=F32)
        yb = jnp.dot(cvo_ref[...], wb, preferred_element_type=F32)
        sa = _sigmoid(gt_ref[:, :D_MODEL].astype(F32))
        sb = _sigmoid(gt_ref[:, D_MODEL:].astype(F32))
        dgt_ref[:, :D_MODEL] = (dm * ya * (sa * (1.0 - sa))).astype(BF16)
        dgt_ref[:, D_MODEL:] = (dm * yb * (sb * (1.0 - sb))).astype(BF16)
        dya = (dm * sa).astype(BF16)
        dyb = (dm * sb).astype(BF16)
        dya_ref[...] = dya
        dyb_ref[...] = dyb
        dog_ref[...] = _mm_nt(dya, wa)
        dcvo = _mm_nt(dyb, wb)

        cvt = cv_ref[...].astype(F32)
        c, bg, xb = cvt[:, :CONV_WIDTH], cvt[:, CONV_WIDTH:2 * CONV_WIDTH], cvt[:, 2 * CONV_WIDTH:]
        halo = halo_ref[...].astype(F32)
        first_tile = step == nt - 1
        prev_u[...] = jnp.where(first_tile, 0.0, halo[:, :CONV_WIDTH] * halo[:, 2 * CONV_WIDTH:])
        u = c * xb
        row = lax.broadcasted_iota(jnp.int32, u.shape, 0)
        p1 = prev_u[HALO - 1:HALO, :]
        p2 = prev_u[HALO - 2:HALO - 1, :]
        u1 = jnp.where(row >= 1, pltpu.roll(u, 1, 0), p1)
        u2 = jnp.where(row >= 2, pltpu.roll(u, 2, 0), jnp.where(row == 1, p1, p2))
        w0, w1, w2 = cw_ref[0:1, :], cw_ref[1:2, :], cw_ref[2:3, :]
        y = w0 * u2 + w1 * u1 + w2 * u
        dcv_ref[:, CONV_WIDTH:2 * CONV_WIDTH] = (dcvo * y).astype(BF16)
        dy = dcvo * bg
        dcw_ref[0:1, :] += jnp.sum(dy * u2, axis=0, keepdims=True)
        dcw_ref[1:2, :] += jnp.sum(dy * u1, axis=0, keepdims=True)
        dcw_ref[2:3, :] += jnp.sum(dy * u, axis=0, keepdims=True)
        n1 = next_dy[0:1, :]
        n2 = next_dy[1:2, :]
        dy1 = jnp.where(row < tm - 1, pltpu.roll(dy, tm - 1, 0), n1)
        dy2 = jnp.where(row < tm - 2, pltpu.roll(dy, tm - 2, 0), jnp.where(row == tm - 2, n1, n2))
        du = w2 * dy + w1 * dy1 + w0 * dy2
        dcv_ref[:, :CONV_WIDTH] = (du * xb).astype(BF16)
        dcv_ref[:, 2 * CONV_WIDTH:] = (du * c).astype(BF16)
        next_dy[...] = dy[:HALO, :]

    rt = lambda i: nt - 1 - i
    row = lambda n: pl.BlockSpec((tm, n), lambda i: (rt(i), 0))
    halo = pl.BlockSpec((HALO, N_CV), lambda i: (jnp.maximum(rt(i) * (tm // HALO) - 1, 0), 0))
    return pl.pallas_call(
        body, name="merge_bwd", grid=(nt,),
        in_specs=[row(D_MODEL), row(HG_WIDTH), row(CONV_WIDTH), row(2 * D_MODEL), row(N_CV), halo,
                  _resident(wa.shape), _resident(wb.shape), _resident(wo.shape), _full((CONV_K, CONV_WIDTH))],
        out_specs=[row(2 * D_MODEL), row(D_MODEL), row(D_MODEL), row(HG_WIDTH), row(N_CV),
                   _full((CONV_K, CONV_WIDTH))],
        out_shape=[jax.ShapeDtypeStruct((T, 2 * D_MODEL), BF16), jax.ShapeDtypeStruct((T, D_MODEL), BF16),
                   jax.ShapeDtypeStruct((T, D_MODEL), BF16), jax.ShapeDtypeStruct((T, HG_WIDTH), F32),
                   jax.ShapeDtypeStruct((T, N_CV), BF16), jax.ShapeDtypeStruct((CONV_K, CONV_WIDTH), F32)],
        scratch_shapes=[pltpu.VMEM((HALO, CONV_WIDTH), F32), pltpu.VMEM((HALO, CONV_WIDTH), F32)],
        compiler_params=_params(("arbitrary",)),
    )(dx1, og, cvo, gt, cv, cv, wa, wb, wo, conv_w)


def _drop_operands(body, first, count):
    def wrapped(*refs):
        return body(*refs[:first], *refs[first + count:])
    return wrapped


def _hg_bwd(dog, hg, o, st, low, gn, after=()):
    T = hg.shape[0]
    tb = min(512, T)
    sb = min(256, tb)
    nb = T // tb
    nc = tb // CHUNK
    wid = HEADS_PER_STEP * HEAD_DIM

    def body(q_ref, f_ref, i_ref, g_ref, low_ref, gn_ref, o_ref, dog_ref, st_ref,
             dq_ref, df_ref, di_ref, dg_ref, dlow_ref, dgn_ref,
             ds_scr, dqi_scr, dko_scr, dv_scr, dd_scr, dqh_scr, dkh_scr):
        h = pl.program_id(0)
        t = pl.program_id(1)

        @pl.when(t == 0)
        def _():
            ds_scr[...] = jnp.zeros_like(ds_scr)
            dlow_ref[...] = jnp.zeros_like(dlow_ref)

        @pl.when((t == 0) & (h == 0))
        def _():
            dgn_ref[...] = jnp.zeros_like(dgn_ref)

        pos = _chunk_pos((tb, HEAD_DIM))
        mask = _intra_mask(sb)
        gnv = gn_ref[...]
        lanes = [slice(hh * HEAD_DIM, (hh + 1) * HEAD_DIM) for hh in range(HEADS_PER_STEP)]
        heads = []
        for hh, ln in enumerate(lanes):
            lb, lb1 = _lower_bound(low_ref.at[:, ln])
            qr = q_ref[:, ln]
            sq, q, sg, f, k, e_qa, e_ka, e_b, e_ko, dec = _hg_gates(qr, f_ref[:, ln], lb, pos, tb)

            gr = g_ref[:, ln]
            o = o_ref[:, ln]
            dog_v = dog_ref[:, ln]
            sgr = _sigmoid(gr)
            r = lax.rsqrt(jnp.mean(o * o, axis=-1, keepdims=True) + EPS)
            oh = o * r
            dg_ref[:, ln] = (dog_v * (oh * gnv) * (sgr * (1.0 + gr * (1.0 - sgr)))).astype(BF16)
            don = dog_v * (gr * sgr)
            dgn_ref[...] += jnp.sum(don * oh, axis=0, keepdims=True)
            w = don * gnv
            do = (r * (w - oh * jnp.mean(w * oh, axis=-1, keepdims=True))).astype(BF16)

            qh = (q * e_qa).astype(BF16)
            kh = (k * e_ka).astype(BF16)
            qi = (q * e_b).astype(BF16)
            ko = (k * e_ko).astype(BF16)
            vb = i_ref[:, ln].astype(BF16)

            for s in range(tb // sb):
                sl = slice(s * sb, (s + 1) * sb)
                p = jnp.where(mask, _mm_nt(qh[sl], kh[sl]), 0.0).astype(BF16)
                dp = jnp.where(mask, _mm_nt(do[sl], vb[sl]), 0.0).astype(BF16)
                dv_scr[sl, ln] = _mm_tn(p, do[sl])
                dqh_scr[sl, ln] = _mm(dp, kh[sl])
                dkh_scr[sl, ln] = _mm_tn(dp, qh[sl])
            heads.append(dict(lb=lb, lb1=lb1, qr=qr, sq=sq, q=q, sg=sg, f=f, k=k, e_qa=e_qa, e_ka=e_ka, e_b=e_b,
                              e_ko=e_ko, dec=dec, do=do, qi=qi, ko=ko, vb=vb, ds=ds_scr[hh]))

        for c in reversed(range(nc)):
            sl = slice(c * CHUNK, (c + 1) * CHUNK)
            for hh, ln in enumerate(lanes):
                hd = heads[hh]
                ds = hd["ds"]
                st_c = st_ref[hh, c]
                dqi_scr[sl, ln] = _mm(hd["do"][sl], st_c)
                dko_scr[sl, ln] = _mm(hd["vb"][sl], ds)
                dv_scr[sl, ln] = dv_scr[sl, ln] + _mm_nt(hd["ko"][sl], ds)
                dd_scr[sl, ln] = jnp.broadcast_to(jnp.sum(ds * st_c, axis=0, keepdims=True), (CHUNK, HEAD_DIM))
                hd["ds"] = hd["dec"][c * CHUNK:c * CHUNK + 1, :] * ds + _mm_tn(hd["do"][sl], hd["qi"][sl])

        for hh, ln in enumerate(lanes):
            hd = heads[hh]
            ds_scr[hh] = hd["ds"]
            q, k, lb = hd["q"], hd["k"], hd["lb"]
            dko_e = dko_scr[:, ln] * hd["e_ko"]
            dq = dqh_scr[:, ln] * hd["e_qa"] + dqi_scr[:, ln] * hd["e_b"]
            dk = dkh_scr[:, ln] * hd["e_ka"] + dko_e
            kd3 = (k * dko_e).reshape(nc, CHUNK, HEAD_DIM)
            last = jnp.broadcast_to(jnp.sum(kd3, axis=1, keepdims=True), kd3.shape).reshape(tb, HEAD_DIM)
            db = q * dq - k * dk + jnp.where(pos == CHUNK - 1, hd["dec"] * dd_scr[:, ln] + last, 0.0)
            dlg = _chunk_rev_cumsum(db, pos)
            dfv = dlg / hd["f"] - dk
            s_low = jnp.sum(dfv * (1.0 - hd["sg"]), axis=0, keepdims=True)
            dlow_ref[0:1, ln] += s_low * lb * (1.0 - lb)
            dlow_ref[1:2, ln] += -s_low * lb * hd["lb1"]
            df_ref[:, ln] = (dfv * (1.0 - lb) * hd["sg"] * (1.0 - hd["sg"])).astype(BF16)
            dq_ref[:, ln] = (dq * Q_SCALE * (hd["sq"] * (1.0 + hd["qr"] * (1.0 - hd["sq"])))).astype(BF16)
            di_ref[:, ln] = dv_scr[:, ln].astype(BF16)

    rt = lambda t: nb - 1 - t
    col = lambda p: pl.BlockSpec((tb, wid), lambda h, t: (rt(t), p * HEAD_GROUPS + h))
    hcol = pl.BlockSpec((tb, wid), lambda h, t: (rt(t), h))
    piece = jax.ShapeDtypeStruct((T, HG_WIDTH), BF16)
    tile = pltpu.VMEM((tb, wid), F32)
    return pl.pallas_call(
        _drop_operands(body, 9, len(after)), name="hg_bwd", grid=(HEAD_GROUPS, nb),
        in_specs=[col(0), col(1), col(2), col(3), pl.BlockSpec((2, wid), lambda h, t: (0, h)),
                  pl.BlockSpec((1, HEAD_DIM), lambda h, t: (0, 0)), hcol, hcol,
                  pl.BlockSpec((HEADS_PER_STEP, nc, HEAD_DIM, HEAD_DIM), lambda h, t: (h, rt(t), 0, 0))]
                 + [HBM_SPEC] * len(after),
        out_specs=[hcol, hcol, hcol, hcol, pl.BlockSpec((2, wid), lambda h, t: (0, h)),
                   pl.BlockSpec((1, HEAD_DIM), lambda h, t: (0, 0))],
        out_shape=[piece, piece, piece, piece, jax.ShapeDtypeStruct((2, HG_WIDTH), F32),
                   jax.ShapeDtypeStruct((1, HEAD_DIM), F32)],
        scratch_shapes=[pltpu.VMEM((HEADS_PER_STEP, HEAD_DIM, HEAD_DIM), F32), tile, tile, tile, tile, tile, tile],
        compiler_params=_params(("arbitrary", "arbitrary")),
    )(hg, hg, hg, hg, low, gn, o, dog, st, *after)


def _in_bwd(dparts, w_in, x, dx1, g, after=()):
    T = x.shape[0]
    tm = min(512, T)
    widths = [p.shape[1] for p in dparts]
    offs = [sum(widths[:i]) for i in range(len(widths))]
    n = len(dparts)

    def body(*refs):
        d_refs = refs[:n]
        w_ref, x_ref, dx1_ref, g_ref, dx_ref, dgn_ref = refs[n:]

        @pl.when(pl.program_id(0) == 0)
        def _():
            dgn_ref[...] = jnp.zeros_like(dgn_ref)

        dh = None
        for d_ref, off, wd in zip(d_refs, offs, widths):
            part = _mm(d_ref[...], w_ref[off:off + wd, :])
            dh = part if dh is None else dh + part
        xv = x_ref[...]
        r = lax.rsqrt(jnp.mean(xv * xv, axis=-1, keepdims=True) + EPS)
        xh = xv * r
        dgn_ref[...] += jnp.sum(dh * xh, axis=0, keepdims=True)
        w = dh * g_ref[...]
        dx_ref[...] = dx1_ref[...] + r * (w - xh * jnp.mean(w * xh, axis=-1, keepdims=True))

    row = lambda m: pl.BlockSpec((tm, m), lambda i: (i, 0))
    return pl.pallas_call(
        _drop_operands(body, n + 4, len(after)), name="in_bwd", grid=(T // tm,),
        in_specs=[row(wd) for wd in widths] + [_resident(w_in.shape), row(D_MODEL), row(D_MODEL), _full((1, D_MODEL))]
                 + [HBM_SPEC] * len(after),
        out_specs=[row(D_MODEL), _full((1, D_MODEL))],
        out_shape=[jax.ShapeDtypeStruct((T, D_MODEL), F32), jax.ShapeDtypeStruct((1, D_MODEL), F32)],
        compiler_params=_params(("arbitrary",)),
    )(*dparts, w_in, x, dx1, g, *after)


def _wgrad(name, at, b, tn, transposed=False, tk=2048, after=()):
    M, T = at.shape
    N = b.shape[1]
    tk = min(tk, T)
    nk = T // tk
    if transposed:
        out_spec, out_shape = pl.BlockSpec((tn, M), lambda j, k: (j, 0)), (N, M)
    else:
        out_spec, out_shape = pl.BlockSpec((M, tn), lambda j, k: (0, j)), (M, N)

    if nk == 1:
        def body1(a_ref, b_ref, o_ref):
            part = _mm(a_ref[...], b_ref[...])
            o_ref[...] = (part.T if transposed else part).astype(BF16)

        return pl.pallas_call(
            _drop_operands(body1, 2, len(after)), name=name, grid=(N // tn, 1),
            in_specs=[_resident((M, T)), pl.BlockSpec((T, tn), lambda j, k: (0, j))] + [HBM_SPEC] * len(after),
            out_specs=out_spec, out_shape=jax.ShapeDtypeStruct(out_shape, BF16),
            compiler_params=_params(("parallel", "arbitrary")),
        )(at, b, *after)

    def body(a_ref, b_ref, o_ref, acc):
        k = pl.program_id(1)

        @pl.when(k == 0)
        def _():
            acc[...] = jnp.zeros_like(acc)

        acc[...] += _mm(a_ref[...], b_ref[...])

        @pl.when(k == nk - 1)
        def _():
            o_ref[...] = (acc[...].T if transposed else acc[...]).astype(BF16)

    return pl.pallas_call(
        _drop_operands(body, 2, len(after)), name=name, grid=(N // tn, nk),
        in_specs=[pl.BlockSpec((M, tk), lambda j, k: (0, k)), pl.BlockSpec((tk, tn), lambda j, k: (k, j))]
                 + [HBM_SPEC] * len(after),
        out_specs=out_spec, out_shape=jax.ShapeDtypeStruct(out_shape, BF16),
        scratch_shapes=[pltpu.VMEM((M, tn), F32)],
        compiler_params=_params(("parallel", "arbitrary")),
    )(at, b, *after)


def _wgrad_branches(ogt, dya, cvot, dyb):
    M, T = ogt.shape
    N = dya.shape[1]
    c = N // N_DEV
    per = 2
    tn = per * c

    def body(at_ref, da_ref, bt_ref, db_ref, oa_ref, ob_ref):
        ga = _mm(at_ref[...], da_ref[...])
        gb = _mm(bt_ref[...], db_ref[...])
        for s in range(per):
            oa_ref[s] = ga[:, s * c:(s + 1) * c].astype(BF16)
            ob_ref[s] = gb[:, s * c:(s + 1) * c].astype(BF16)

    rhs = pl.BlockSpec((T, tn), lambda j: (0, j))
    owners = pl.BlockSpec((per, M, c), lambda j: (j, 0, 0))
    out = jax.ShapeDtypeStruct((N_DEV, M, c), BF16)
    return pl.pallas_call(
        body, name="wgrad_branches", grid=(N // tn,),
        in_specs=[_resident((M, T)), rhs, _resident((M, T)), rhs], out_specs=[owners] * 2, out_shape=[out, out],
        compiler_params=_params(("parallel",)),
    )(ogt, dya, cvot, dyb)


def _wgrad_in(ht, dparts, after=()):
    M, T = ht.shape
    tn = 256
    nblk = [p.shape[1] // tn for p in dparts]
    start = [sum(nblk[:i]) for i in range(len(nblk))]
    n = len(dparts)

    def body(a_ref, *refs):
        d_refs, o_ref = refs[:n], refs[n]
        j = pl.program_id(0)
        for d_ref, s, nb in zip(d_refs, start, nblk):
            @pl.when((j >= s) & (j < s + nb))
            def _():
                o_ref[...] = _mm(a_ref[...], d_ref[...]).T.astype(BF16)

    def piece_spec(s, nb):
        return pl.BlockSpec((T, tn), lambda j: (0, jnp.clip(j - s, 0, nb - 1)))

    return pl.pallas_call(
        _drop_operands(body, 1 + n, len(after)), name="wgrad_in", grid=(sum(nblk),),
        in_specs=[_resident((M, T))] + [piece_spec(s, nb) for s, nb in zip(start, nblk)] + [HBM_SPEC] * len(after),
        out_specs=pl.BlockSpec((tn, M), lambda j: (j, 0)),
        out_shape=jax.ShapeDtypeStruct((sum(nblk) * tn, M), BF16),
        compiler_params=_params(("parallel",)),
    )(ht, *dparts, *after)


def _adamw_math(w, g, m, v):
    m = ADAM_B1 * m + (1.0 - ADAM_B1) * g
    v = ADAM_B2 * v + (1.0 - ADAM_B2) * (g * g)
    m_hat = m / (1.0 - ADAM_B1 ** ADAM_STEP)
    v_hat = v / (1.0 - ADAM_B2 ** ADAM_STEP)
    delta = -ADAM_LR * (m_hat / (jnp.sqrt(v_hat) + ADAM_EPS) + ADAM_WD * w)
    return delta, m, v


def _adamw_sum(name, w, parts, m, v):
    R, C = w.shape
    tr = _row_tile(R)

    def body(w_ref, p_ref, m_ref, v_ref, g_out, d_out, m_out, v_out):
        g = p_ref[0].astype(F32)
        for k in range(1, 4):
            g = g + p_ref[k].astype(F32)
        g_out[...] = g
        d_out[...], m_out[...], v_out[...] = _adamw_math(w_ref[...], g, m_ref[...], v_ref[...])

    blk = pl.BlockSpec((tr, C), lambda i: (i, 0))
    out = jax.ShapeDtypeStruct((R, C), F32)
    return pl.pallas_call(
        body, name=name, grid=(R // tr,),
        in_specs=[blk, pl.BlockSpec((4, tr, C), lambda i: (0, i, 0)), blk, blk],
        out_specs=[blk, blk, blk, blk], out_shape=[out, out, out, out],
        compiler_params=_params(("parallel",)),
    )(w, parts, m, v)


_SMALL_SLOTS = (("norm_mix_g", 0, 1, 1024), ("norm_ffn_g", 1, 1, 1024), ("norm_final_g", 2, 1, 1024),
                ("lower_bounds", 3, 2, 512), ("hg_norm_g", 5, 1, 128), ("loss", 6, 1, 128), ("conv_w", 8, 3, 512))
_SMALL_PARAMS = tuple(s for s in _SMALL_SLOTS if s[0] != "loss")
CONV_SHARD = CONV_WIDTH // N_DEV


def _small_pack(small):
    def body(*refs):
        out = refs[-1]
        out[...] = jnp.zeros_like(out)
        for ref, (_, row, rows, lanes) in zip(refs[:-1], _SMALL_SLOTS):
            out[row:row + rows, 0:lanes] = ref[...]

    vmem = pl.BlockSpec(memory_space=pltpu.VMEM)
    return pl.pallas_call(
        body, name="small_pack", in_specs=[vmem] * len(_SMALL_SLOTS), out_specs=vmem,
        out_shape=jax.ShapeDtypeStruct((SMALL_ROWS, 1024), F32),
    )(*[small[name] for name, _, _, _ in _SMALL_SLOTS])


def _small_update(gathered, dev, w, m, v):
    n = len(_SMALL_PARAMS)

    def body(dev_ref, g_ref, *refs):
        w_refs, m_refs, v_refs = refs[:n], refs[n:2 * n], refs[2 * n:3 * n]
        loss_ref, out_refs, sum_scr = refs[3 * n], refs[3 * n + 1:-1], refs[-1]
        total = g_ref[0]
        for k in range(1, N_DEV):
            total = total + g_ref[k]
        sum_scr[...] = total
        loss_ref[...] = sum_scr[6:7, 0:128]
        for p, (name, row, rows, lanes) in enumerate(_SMALL_PARAMS):
            if name == "conv_w":
                g = sum_scr[row:row + rows, 0:CONV_SHARD]
                for s in range(1, N_DEV):
                    g = jnp.where(dev_ref[0] == s, sum_scr[row:row + rows, s * CONV_SHARD:(s + 1) * CONV_SHARD], g)
            else:
                g = sum_scr[row:row + rows, 0:lanes]
            delta, m_new, v_new = _adamw_math(w_refs[p][...], g, m_refs[p][...], v_refs[p][...])
            out_refs[4 * p][...] = g
            out_refs[4 * p + 1][...] = delta
            out_refs[4 * p + 2][...] = m_new
            out_refs[4 * p + 3][...] = v_new

    vmem = pl.BlockSpec(memory_space=pltpu.VMEM)
    outs = [jax.ShapeDtypeStruct((1, 128), F32)]
    for a in w:
        outs += [jax.ShapeDtypeStruct(a.shape, F32)] * 4
    return pl.pallas_call(
        body, name="small_update",
        in_specs=[pl.BlockSpec(memory_space=pltpu.SMEM)] + [vmem] * (1 + 3 * n), out_specs=[vmem] * len(outs),
        out_shape=outs, scratch_shapes=[pltpu.VMEM((SMALL_ROWS, 1024), F32)],
    )(dev, gathered, *w, *m, *v)


def _row_tile(rows):
    for parts in (4, 2):
        if rows % (16 * parts) == 0:
            return rows // parts
    return rows


def _pair_sum(name, by_owner, got, core, after=()):
    n = len(got)

    def body(core_ref, *refs):
        for a_ref, b_ref, o_ref in zip(refs[:n], refs[n:2 * n], refs[2 * n:]):
            o_ref[...] = (a_ref[...].astype(F32) + b_ref[...].astype(F32)).astype(BF16)

    def blk(g):
        return pl.BlockSpec((None,) + g.shape[1:], lambda k, core_ref: (k, 0, 0))

    def mine(g):
        return pl.BlockSpec((None,) + g.shape[1:], lambda k, core_ref: (2 * k + core_ref[0], 0, 0))

    return pl.pallas_call(
        _drop_operands(body, 1 + 2 * n, len(after)), name=name,
        grid_spec=pltpu.PrefetchScalarGridSpec(
            num_scalar_prefetch=1, grid=(4,),
            in_specs=[mine(g) for g in got] + [blk(g) for g in got] + [HBM_SPEC] * len(after),
            out_specs=[blk(g) for g in got]),
        out_shape=[jax.ShapeDtypeStruct(g.shape, BF16) for g in got],
        compiler_params=_params(("parallel",)),
    )(core, *by_owner, *got, *after)


MESH = pl.DeviceIdType.MESH
HBM_SPEC = pl.BlockSpec(memory_space=pl.ANY)


def _handshake(peers):
    barrier = pltpu.get_barrier_semaphore()
    for peer in peers:
        pl.semaphore_signal(barrier, inc=1, device_id=peer, device_id_type=MESH)
    pl.semaphore_wait(barrier, len(peers))


def _comm_call(body, name, operands, out_shape, scratch, collective_id):
    if collective_id is None:
        return pl.pallas_call(body, name=name, in_specs=[HBM_SPEC] * len(operands), out_specs=[HBM_SPEC] * len(out_shape),
                              out_shape=out_shape, scratch_shapes=scratch)(*operands)
    return pl.kernel(body, out_type=out_shape, mesh=plsc.ScalarSubcoreMesh(axis_name="sequencer", num_cores=1),
                     scratch_types=scratch, name=name,
                     compiler_params=pltpu.CompilerParams(collective_id=collective_id))(*operands)


def _all_gather(name, blocks, collective_id=None, after=()):
    n = len(blocks)
    na = len(after)

    def body(*refs):
        x_refs, out_refs = refs[:n], refs[n + na:2 * n + na]
        send_sems, recv_sems, local_sems = refs[2 * n + na:]
        x, y, c = lax.axis_index("x"), lax.axis_index("y"), lax.axis_index("c")
        me, sibling = (x, y, c), (x, y, 1 - c)
        chips = [(1 - x, y), (x, 1 - y), (1 - x, 1 - y)]
        if collective_id is not None:
            _handshake([sibling] + [(*chip, c) for chip in chips])

        def slot(i, px, py, pc):
            return out_refs[i].at[4 * px + 2 * py + pc]

        def copy(i, k, blk, to, src=None):
            return pltpu.make_async_remote_copy(
                src_ref=slot(i, *blk) if src is None else src, dst_ref=slot(i, *blk),
                send_sem=send_sems.at[7 * i + k], recv_sem=recv_sems.at[7 * i + k], device_id=to, device_id_type=MESH)

        mine = [pltpu.make_async_copy(x_refs[i], slot(i, *me), local_sems.at[i]) for i in range(n)]
        for cp in mine:
            cp.start()
        first = []
        for i in range(n):
            first.append(copy(i, 0, me, sibling, src=x_refs[i]))
            first += [copy(i, 1 + j, me, (*chip, c), src=x_refs[i]) for j, chip in enumerate(chips)]
        for cp in first:
            cp.start()
        passed = []
        for i in range(n):
            for j, chip in enumerate(chips):
                copy(i, 1 + j, (*chip, c), me).wait_recv()
                passed.append(copy(i, 4 + j, (*chip, c), sibling))
                passed[-1].start()
        for i in range(n):
            copy(i, 0, sibling, me).wait_recv()
            for j, chip in enumerate(chips):
                copy(i, 4 + j, (*chip, 1 - c), me).wait_recv()
        for cp in first + passed:
            cp.wait_send()
        for cp in mine:
            cp.wait()

    return _comm_call(
        body, name, list(blocks) + list(after), [jax.ShapeDtypeStruct((N_DEV,) + b.shape, b.dtype) for b in blocks],
        [pltpu.SemaphoreType.DMA((7 * n,)), pltpu.SemaphoreType.DMA((7 * n,)), pltpu.SemaphoreType.DMA((n,))],
        collective_id)


def _sibling_swap(name, by_owner, collective_id=None, after=()):
    n = len(by_owner)
    na = len(after)

    def body(*refs):
        x_refs, out_refs = refs[:n], refs[n + na:2 * n + na]
        send_sems, recv_sems = refs[2 * n + na:]
        x, y, c = lax.axis_index("x"), lax.axis_index("y"), lax.axis_index("c")
        if collective_id is not None:
            _handshake([(x, y, 1 - c)])
        copies = []
        for i in range(n):
            for k in range(4):
                copies.append(pltpu.make_async_remote_copy(
                    src_ref=x_refs[i].at[2 * k + 1 - c], dst_ref=out_refs[i].at[k],
                    send_sem=send_sems.at[4 * i + k], recv_sem=recv_sems.at[4 * i + k],
                    device_id=(x, y, 1 - c), device_id_type=MESH))
        for cp in copies:
            cp.start()
        for cp in copies:
            cp.wait()

    return _comm_call(
        body, name, list(by_owner) + list(after),
        [jax.ShapeDtypeStruct((4,) + b.shape[1:], b.dtype) for b in by_owner],
        [pltpu.SemaphoreType.DMA((4 * n,)), pltpu.SemaphoreType.DMA((4 * n,))], collective_id)


def _chip_exchange(name, sums, collective_id=None, after=()):
    n = len(sums)
    na = len(after)

    def body(*refs):
        x_refs, out_refs = refs[:n], refs[n + na:2 * n + na]
        send_sems, recv_sems, local_sems = refs[2 * n + na:]
        x, y, c = lax.axis_index("x"), lax.axis_index("y"), lax.axis_index("c")
        chips = [(1 - x, y), (x, 1 - y), (1 - x, 1 - y)]
        my_chip = 2 * x + y
        if collective_id is not None:
            _handshake([(cx, cy, c) for cx, cy in chips])
        mine = [pltpu.make_async_copy(x_refs[i].at[my_chip], out_refs[i].at[my_chip], local_sems.at[i])
                for i in range(n)]
        for cp in mine:
            cp.start()
        sends = []
        for i in range(n):
            for j, (cx, cy) in enumerate(chips):
                sends.append(pltpu.make_async_remote_copy(
                    src_ref=x_refs[i].at[2 * cx + cy], dst_ref=out_refs[i].at[my_chip],
                    send_sem=send_sems.at[3 * i + j], recv_sem=recv_sems.at[3 * i + j],
                    device_id=(cx, cy, c), device_id_type=MESH))
        for cp in sends:
            cp.start()
        for i in range(n):
            for j, (cx, cy) in enumerate(chips):
                pltpu.make_async_remote_copy(
                    src_ref=x_refs[i].at[my_chip], dst_ref=out_refs[i].at[2 * cx + cy],
                    send_sem=send_sems.at[3 * i + j], recv_sem=recv_sems.at[3 * i + j],
                    device_id=(cx, cy, c), device_id_type=MESH).wait_recv()
        for cp in sends:
            cp.wait_send()
        for cp in mine:
            cp.wait()

    return _comm_call(
        body, name, list(sums) + list(after), [jax.ShapeDtypeStruct(s.shape, s.dtype) for s in sums],
        [pltpu.SemaphoreType.DMA((3 * n,)), pltpu.SemaphoreType.DMA((3 * n,)), pltpu.SemaphoreType.DMA((n,))],
        collective_id)


def _cast_shards(shards):
    n = len(shards)

    def body(*refs):
        for i in range(n):
            refs[n + i][...] = refs[i][...].astype(BF16)

    vmem = pl.BlockSpec(memory_space=pltpu.VMEM)
    return pl.pallas_call(
        body, name="cast_shards", in_specs=[vmem] * n, out_specs=[vmem] * n,
        out_shape=[jax.ShapeDtypeStruct(s.shape, BF16) for s in shards],
        compiler_params=pltpu.CompilerParams(vmem_limit_bytes=VMEM_LIMIT_V7X),
    )(*shards)


BIG = ("w_in", "w_branch_a", "w_branch_b", "w_out", "w_ffn_gate", "w_ffn_up", "w_ffn_down")


def _local_step(x, target, gains, low, conv_w, wg8, reduce):
    g_mix, g_hg, g_ffn, g_fin = gains
    w_in = wg8["w_in"].reshape(N_IN, D_MODEL)
    wg = wg8["w_ffn_gate"].reshape(D_FF, D_MODEL)
    wu = wg8["w_ffn_up"].reshape(D_FF, D_MODEL)
    wa, wb = wg8["w_branch_a"], wg8["w_branch_b"]
    wo = wg8["w_out"].reshape(D_MODEL, D_MODEL)
    wd = wg8["w_ffn_down"].reshape(D_FF, D_MODEL)

    ht, hg, cv, gt, cvo, cvot = _fwd_in(x, g_mix, w_in, conv_w)
    o, og, ogt, st = _hg_fwd(hg, low, g_hg)
    x1, mgt = _merge_fwd(og, cvo, gt, x, wa, wb, wo)
    h2t, gate, up, actt, loss, d_gfin, dx2 = _ffn_fwd_loss(x1, g_ffn, wg, wu, wd, target, g_fin)

    dgate, dup, dx1, d_gffn = _ffn_bwd(dx2, x1, gate, up, g_ffn, wg, wu, wd)
    ffn = dict(
        w_ffn_down=_wgrad("wgrad_ffn_down", actt, dx2, 256, tk=4096).reshape(N_DEV, D_FF // N_DEV, D_MODEL),
        w_ffn_gate=_wgrad("wgrad_ffn_gate", h2t, dgate, 256, transposed=True, tk=4096
                          ).reshape(N_DEV, D_FF // N_DEV, D_MODEL),
        w_ffn_up=_wgrad("wgrad_ffn_up", h2t, dup, 256, transposed=True, tk=4096
                        ).reshape(N_DEV, D_FF // N_DEV, D_MODEL))
    dgt, dya, dyb, dog, dcv, d_conv = _merge_bwd(dx1, og, cvo, gt, cv, wa, wb, wo, conv_w)
    sums_ffn, got_ffn = reduce.begin(ffn, sum_after=[dya])
    parts_ffn, updated_ffn = reduce.finish(ffn, sums_ffn)
    grad_a, grad_b = _wgrad_branches(ogt, dya, cvot, dyb)
    out = dict(
        w_out=_wgrad("wgrad_out", mgt, dx1, 256, tk=4096, after=sums_ffn[:1]
                     ).reshape(N_DEV, D_MODEL // N_DEV, D_MODEL),
        w_branch_a=grad_a, w_branch_b=grad_b)
    dq, df, di, dg, d_low, d_ghg = _hg_bwd(dog, hg, o, st, low, g_hg, after=list(sums_ffn) + [out["w_out"]])
    sums_out, got_out = reduce.begin(out, after=[parts_ffn[0], dq], sum_after=updated_ffn)
    parts_out, updated_out = reduce.finish(out, sums_out)
    dparts = [dq, df, di, dg, dcv, dgt]
    w_in_grad = dict(w_in=_wgrad_in(ht, dparts, after=sums_out[:1]).reshape(N_DEV, N_IN // N_DEV, D_MODEL))
    sums_in, _ = reduce.begin(w_in_grad, after=parts_out[:1], sum_after=updated_out)
    parts_in, _ = reduce.finish(w_in_grad, sums_in)
    grad_x, d_gmix = _in_bwd(dparts, w_in, x, dx1, g_mix, after=list(parts_out[:1]) + list(sums_in))
    small = dict(norm_mix_g=d_gmix, norm_ffn_g=d_gffn, norm_final_g=d_gfin, lower_bounds=d_low, hg_norm_g=d_ghg,
                 conv_w=d_conv, loss=loss)
    return grad_x, small, parts_in


def _conv_shard_rows(a):
    return jnp.pad(a, ((0, 5), (0, 64)))


def kernel(x, norm_mix_g, w_in, lower_bounds, hg_norm_g, conv_w, w_branch_a, w_branch_b, w_out, norm_ffn_g, w_ffn_gate, w_ffn_up, w_ffn_down, norm_final_g, loss_target, m_norm_mix_g, m_w_in, m_lower_bounds, m_hg_norm_g, m_conv_w, m_w_branch_a, m_w_branch_b, m_w_out, m_norm_ffn_g, m_w_ffn_gate, m_w_ffn_up, m_w_ffn_down, m_norm_final_g, v_norm_mix_g, v_w_in, v_lower_bounds, v_hg_norm_g, v_conv_w, v_w_branch_a, v_w_branch_b, v_w_out, v_norm_ffn_g, v_w_ffn_gate, v_w_ffn_up, v_w_ffn_down, v_norm_final_g):
    cx, cy, cc = lax.axis_index("x"), lax.axis_index("y"), lax.axis_index("c")
    my_dev = 4 * cx + 2 * cy + cc

    def tr(a):
        return a[0].T

    big = dict(w_in=tr(w_in), w_branch_a=w_branch_a[0], w_branch_b=w_branch_b[0], w_out=w_out[0],
               w_ffn_gate=tr(w_ffn_gate), w_ffn_up=tr(w_ffn_up), w_ffn_down=w_ffn_down[0])
    big_m = dict(w_in=tr(m_w_in), w_branch_a=m_w_branch_a[0], w_branch_b=m_w_branch_b[0], w_out=m_w_out[0],
                 w_ffn_gate=tr(m_w_ffn_gate), w_ffn_up=tr(m_w_ffn_up), w_ffn_down=m_w_ffn_down[0])
    big_v = dict(w_in=tr(v_w_in), w_branch_a=v_w_branch_a[0], w_branch_b=v_w_branch_b[0], w_out=v_w_out[0],
                 w_ffn_gate=tr(v_w_ffn_gate), w_ffn_up=tr(v_w_ffn_up), w_ffn_down=v_w_ffn_down[0])
    transposed = ("w_in", "w_ffn_gate", "w_ffn_up")

    shards = dict(zip(BIG, _cast_shards([big[n] for n in BIG])))
    first = _all_gather("gather_w_in", [shards["w_in"], _conv_shard_rows(conv_w[0])])
    ids = iter(range(1, 16))
    mid = _all_gather("gather_mid", [shards[n] for n in BIG[1:4]], collective_id=next(ids), after=first[1:])
    ffn = _all_gather("gather_ffn", [shards[n] for n in BIG[4:]], collective_id=next(ids), after=first[1:])
    wg8 = dict(zip(BIG, [first[0]] + list(mid) + list(ffn)))
    conv_full = first[1][:, :3, :64].transpose(1, 0, 2).reshape(3, CONV_WIDTH)

    core = cc.reshape(1).astype(jnp.int32)
    outs = {}

    class Reduce:
        @staticmethod
        def begin(grads, after=(), sum_after=()):
            names = list(grads)
            by_owner = [grads[n] for n in names]
            got = _sibling_swap("sibling_swap_" + names[0], by_owner, collective_id=next(ids), after=after)
            sums = _pair_sum("pair_sum_" + names[0], by_owner, got, core, after=sum_after)
            return sums, got

        @staticmethod
        def finish(grads, chip_sums, after=()):
            names = list(grads)
            parts = _chip_exchange("chip_exchange_" + names[0], chip_sums, collective_id=next(ids), after=after)
            for n, p in zip(names, parts):
                outs[n] = _adamw_sum("adamw_" + n, big[n], p, big_m[n], big_v[n])
            return parts, [outs[n][1] for n in names]

    gains = (norm_mix_g, hg_norm_g, norm_ffn_g, norm_final_g.reshape(1, D_MODEL))
    grad_x, small, last = _local_step(x[0], loss_target[0], gains, lower_bounds, conv_full, wg8, Reduce)

    small_all = _all_gather("gather_small", [_small_pack(small)], collective_id=next(ids), after=last[:1])

    def small_state(a):
        return [a[0], a[1], a[2].reshape(1, D_MODEL), a[3], a[4], a[5][0]]

    upd = _small_update(
        small_all[0], my_dev.reshape(1).astype(jnp.int32),
        small_state((norm_mix_g, norm_ffn_g, norm_final_g, lower_bounds, hg_norm_g, conv_w)),
        small_state((m_norm_mix_g, m_norm_ffn_g, m_norm_final_g, m_lower_bounds, m_hg_norm_g, m_conv_w)),
        small_state((v_norm_mix_g, v_norm_ffn_g, v_norm_final_g, v_lower_bounds, v_hg_norm_g, v_conv_w)))
    loss = upd[0][0, 0]
    small_shape = dict(norm_final_g=(D_MODEL,), conv_w=(1, 3, CONV_SHARD))
    for p, (name, _, _, _) in enumerate(_SMALL_PARAMS):
        outs[name] = [a.reshape(small_shape.get(name, a.shape)) for a in upd[1 + 4 * p:5 + 4 * p]]

    order = ["norm_mix_g", "w_in", "lower_bounds", "hg_norm_g", "conv_w", "w_branch_a", "w_branch_b", "w_out",
             "norm_ffn_g", "w_ffn_gate", "w_ffn_up", "w_ffn_down", "norm_final_g"]
    result = [loss, grad_x[None]]
    for k in range(4):
        for n in order:
            if n in BIG:
                result.append((outs[n][k].T if n in transposed else outs[n][k])[None])
            else:
                result.append(outs[n][k])
    return tuple(result)
```

```python
import jax
import jax.numpy as jnp
from jax import lax
from jax.experimental import pallas as pl
from jax.experimental.pallas import tpu as pltpu
from jax.experimental.pallas import tpu_sc as plsc

F32 = jnp.float32
BF16 = jnp.bfloat16
STASH = jnp.bfloat16

D_MODEL = 1024
HG_WIDTH = 512
HEAD_DIM = 128
N_HEADS = 4
HEADS_PER_STEP = 4
HEAD_GROUPS = N_HEADS // HEADS_PER_STEP
CONV_WIDTH = 512
CONV_K = 3
D_FF = 2816
CHUNK = 32
EPS = 1e-6
Q_SCALE = HEAD_DIM ** -0.5
N_DEV = 8

ADAM_LR = 0.001
ADAM_B1 = 0.9
ADAM_B2 = 0.999
ADAM_EPS = 1e-08
ADAM_WD = 0.01
ADAM_STEP = 10

VMEM_LIMIT_V7X = 56 * 1024 * 1024

SMALL_ROWS = 16


def _params(sem, vmem=VMEM_LIMIT_V7X):
    return pltpu.CompilerParams(dimension_semantics=sem, vmem_limit_bytes=vmem)


def _mm(a, b):
    return jnp.dot(a.astype(BF16), b.astype(BF16), preferred_element_type=F32)


def _mm_nt(a, b):
    return lax.dot_general(a.astype(BF16), b.astype(BF16), (((1,), (1,)), ((), ())), preferred_element_type=F32)


def _mm_tn(a, b):
    return lax.dot_general(a.astype(BF16), b.astype(BF16), (((0,), (0,)), ((), ())), preferred_element_type=F32)


def _sigmoid(x):
    return 0.5 * jnp.tanh(0.5 * x) + 0.5


def _resident(shape):
    nd = len(shape)
    return pl.BlockSpec(shape, lambda *_: (0,) * nd, pipeline_mode=pl.Buffered(1))


def _full(shape):
    nd = len(shape)
    return pl.BlockSpec(shape, lambda *_: (0,) * nd)


def _shard_cols(w_ref):
    return jnp.concatenate([w_ref[s] for s in range(N_DEV)], axis=1)


N_HG = 4 * HG_WIDTH
N_CV = 3 * CONV_WIDTH
N_GT = 2 * D_MODEL
N_IN = N_HG + N_CV + N_GT


def _col(tm, n):
    return pl.BlockSpec((n, tm), lambda i: (0, i))


HALO = 8


def _fwd_in(x, g, w_in_t, conv_w):
    T = x.shape[0]
    tm = min(512, T)

    def body(x_ref, g_ref, w_ref, cw_ref, ht_ref, hg_ref, cv_ref, gt_ref, cvo_ref, cvot_ref, tail_scr):
        @pl.when(pl.program_id(0) == 0)
        def _():
            tail_scr[...] = jnp.zeros_like(tail_scr)

        xv = x_ref[...]
        r = lax.rsqrt(jnp.mean(xv * xv, axis=-1, keepdims=True) + EPS)
        hf = xv * r * g_ref[...]
        h = hf.astype(BF16)
        ht_ref[...] = hf.T.astype(BF16)
        hg_ref[...] = _mm_nt(h, w_ref[:N_HG, :])
        cv = _mm_nt(h, w_ref[N_HG:N_HG + N_CV, :])
        cv_ref[...] = cv.astype(STASH)
        gt_ref[...] = _mm_nt(h, w_ref[N_HG + N_CV:, :]).astype(STASH)

        u = cv[:, :CONV_WIDTH] * cv[:, 2 * CONV_WIDTH:]
        row = lax.broadcasted_iota(jnp.int32, u.shape, 0)
        prev1 = tail_scr[HALO - 1:HALO, :]
        prev2 = tail_scr[HALO - 2:HALO - 1, :]
        u1 = jnp.where(row >= 1, pltpu.roll(u, 1, 0), prev1)
        u2 = jnp.where(row >= 2, pltpu.roll(u, 2, 0), jnp.where(row == 1, prev1, prev2))
        y = cw_ref[0:1, :] * u2 + cw_ref[1:2, :] * u1 + cw_ref[2:3, :] * u
        out = cv[:, CONV_WIDTH:2 * CONV_WIDTH] * y
        cvo_ref[...] = out.astype(BF16)
        cvot_ref[...] = out.T.astype(BF16)
        tail_scr[...] = u[tm - HALO:, :]

    row = lambda n: pl.BlockSpec((tm, n), lambda i: (i, 0))
    return pl.pallas_call(
        body, name="fwd_in", grid=(T // tm,),
        in_specs=[row(D_MODEL), _full((1, D_MODEL)), _resident(w_in_t.shape), _full((CONV_K, CONV_WIDTH))],
        out_specs=[_col(tm, D_MODEL), row(N_HG), row(N_CV), row(N_GT), row(CONV_WIDTH), _col(tm, CONV_WIDTH)],
        out_shape=[jax.ShapeDtypeStruct((D_MODEL, T), BF16), jax.ShapeDtypeStruct((T, N_HG), F32),
                   jax.ShapeDtypeStruct((T, N_CV), STASH), jax.ShapeDtypeStruct((T, N_GT), STASH),
                   jax.ShapeDtypeStruct((T, CONV_WIDTH), BF16), jax.ShapeDtypeStruct((CONV_WIDTH, T), BF16)],
        scratch_shapes=[pltpu.VMEM((HALO, CONV_WIDTH), F32)],
        compiler_params=_params(("arbitrary",)),
    )(x, g, w_in_t, conv_w)


def _chunk_pos(shape):
    return lax.broadcasted_iota(jnp.int32, shape, 0) & (CHUNK - 1)


def _chunk_cumsum(x, pos):
    s = 1
    while s < CHUNK:
        x = x + jnp.where(pos >= s, pltpu.roll(x, s, 0), 0.0)
        s *= 2
    return x


def _chunk_rev_cumsum(x, pos):
    n = x.shape[0]
    s = 1
    while s < CHUNK:
        x = x + jnp.where(pos + s < CHUNK, pltpu.roll(x, n - s, 0), 0.0)
        s *= 2
    return x


def _chunk_bcast(x3, row, tb):
    return jnp.broadcast_to(x3[:, row:row + 1, :], x3.shape).reshape(tb, x3.shape[-1])


def _lower_bound(low_ref):
    l0 = low_ref[0:1, :]
    l1 = low_ref[1:2, :]
    m = jnp.maximum(l0, l1)
    e0 = jnp.exp(l0 - m)
    e1 = jnp.exp(l1 - m)
    return e0 / (e0 + e1), e1 / (e0 + e1)


def _hg_gates(qr, fr, lb, pos, tb):
    sq = _sigmoid(qr)
    q = qr * sq * Q_SCALE
    sg = _sigmoid(fr)
    f = lb + (1.0 - lb) * sg
    k = 1.0 - f
    b = _chunk_cumsum(jnp.log(f), pos)
    b3 = b.reshape(tb // CHUNK, CHUNK, HEAD_DIM)
    anc = _chunk_bcast(b3, CHUNK // 2 - 1, tb)
    blb = _chunk_bcast(b3, CHUNK - 1, tb)
    e_qa = jnp.exp(b - anc)
    e_ka = jnp.exp(anc - b)
    e_b = jnp.exp(b)
    e_ko = jnp.exp(blb - b)
    dec = jnp.exp(blb)
    return sq, q, sg, f, k, e_qa, e_ka, e_b, e_ko, dec


def _intra_mask(sb):
    r = lax.broadcasted_iota(jnp.int32, (sb, sb), 0)
    c = lax.broadcasted_iota(jnp.int32, (sb, sb), 1)
    return ((r // CHUNK) == (c // CHUNK)) & (c <= r)


def _hg_fwd(hg, low, gn):
    T = hg.shape[0]
    tb = min(1024, T)
    sb = min(256, tb)
    nb = T // tb
    nc = tb // CHUNK
    wid = HEADS_PER_STEP * HEAD_DIM

    def body(q_ref, f_ref, i_ref, g_ref, low_ref, gn_ref, o_ref, og_ref, ogt_ref, st_ref, s_scr):
        t = pl.program_id(1)

        @pl.when(t == 0)
        def _():
            s_scr[...] = jnp.zeros_like(s_scr)

        pos = _chunk_pos((tb, HEAD_DIM))
        mask = _intra_mask(sb)
        lanes = [slice(hh * HEAD_DIM, (hh + 1) * HEAD_DIM) for hh in range(HEADS_PER_STEP)]
        qi, ko, vb, dec, st = [], [], [], [], []
        for hh, ln in enumerate(lanes):
            lb, _ = _lower_bound(low_ref.at[:, ln])
            _, q, _, _, k, e_qa, e_ka, e_b, e_ko, dec_h = _hg_gates(q_ref[:, ln], f_ref[:, ln], lb, pos, tb)
            qh = (q * e_qa).astype(BF16)
            kh = (k * e_ka).astype(BF16)
            qi.append((q * e_b).astype(BF16))
            ko.append((k * e_ko).astype(BF16))
            vb.append(i_ref[:, ln].astype(BF16))
            dec.append(dec_h)
            st.append(s_scr[hh])
            for s in range(tb // sb):
                sl = slice(s * sb, (s + 1) * sb)
                p = jnp.where(mask, _mm_nt(qh[sl], kh[sl]), 0.0)
                o_ref[sl, ln] = _mm(p, vb[hh][sl])
        for c in range(nc):
            sl = slice(c * CHUNK, (c + 1) * CHUNK)
            for hh, ln in enumerate(lanes):
                st_ref[hh, c] = st[hh]
                o_ref[sl, ln] = o_ref[sl, ln] + _mm_nt(qi[hh][sl], st[hh])
                st[hh] = dec[hh][c * CHUNK:c * CHUNK + 1, :] * st[hh] + _mm_tn(vb[hh][sl], ko[hh][sl])
        for hh, ln in enumerate(lanes):
            s_scr[hh] = st[hh]
            o = o_ref[:, ln]
            r = lax.rsqrt(jnp.mean(o * o, axis=-1, keepdims=True) + EPS)
            gr = g_ref[:, ln]
            og = (o * r * gn_ref[...]) * (gr * _sigmoid(gr))
            og_ref[:, ln] = og.astype(BF16)
            ogt_ref[ln, :] = og.T.astype(BF16)

    col = lambda p: pl.BlockSpec((tb, wid), lambda h, t: (t, p * HEAD_GROUPS + h))
    hcol = pl.BlockSpec((tb, wid), lambda h, t: (t, h))
    return pl.pallas_call(
        body, name="hg_fwd", grid=(HEAD_GROUPS, nb),
        in_specs=[col(0), col(1), col(2), col(3), pl.BlockSpec((2, wid), lambda h, t: (0, h)),
                  pl.BlockSpec((1, HEAD_DIM), lambda h, t: (0, 0))],
        out_specs=[hcol, hcol, pl.BlockSpec((wid, tb), lambda h, t: (h, t)),
                   pl.BlockSpec((HEADS_PER_STEP, nc, HEAD_DIM, HEAD_DIM), lambda h, t: (h, t, 0, 0))],
        out_shape=[jax.ShapeDtypeStruct((T, HG_WIDTH), F32), jax.ShapeDtypeStruct((T, HG_WIDTH), BF16),
                   jax.ShapeDtypeStruct((HG_WIDTH, T), BF16),
                   jax.ShapeDtypeStruct((N_HEADS, T // CHUNK, HEAD_DIM, HEAD_DIM), F32)],
        scratch_shapes=[pltpu.VMEM((HEADS_PER_STEP, HEAD_DIM, HEAD_DIM), F32)],
        compiler_params=_params(("parallel", "arbitrary")),
    )(hg, hg, hg, hg, low, gn)


def _merge_fwd(og, cvo, gt, x, wa, wb, wo):
    T = x.shape[0]
    tm = min(1024, T)

    def body(og_ref, cvo_ref, gt_ref, x_ref, wa_ref, wb_ref, wo_ref, x1_ref, mgt_ref):
        ya = jnp.dot(og_ref[...], _shard_cols(wa_ref), preferred_element_type=F32)
        yb = jnp.dot(cvo_ref[...], _shard_cols(wb_ref), preferred_element_type=F32)
        m = (_sigmoid(gt_ref[:, :D_MODEL].astype(F32)) * ya
             + _sigmoid(gt_ref[:, D_MODEL:].astype(F32)) * yb)
        mgt_ref[...] = m.T.astype(BF16)
        x1_ref[...] = x_ref[...] + jnp.dot(m.astype(BF16), wo_ref[...], preferred_element_type=F32)

    row = lambda n: pl.BlockSpec((tm, n), lambda i: (i, 0))
    return pl.pallas_call(
        body, name="merge_fwd", grid=(T // tm,),
        in_specs=[row(HG_WIDTH), row(CONV_WIDTH), row(2 * D_MODEL), row(D_MODEL),
                  _resident(wa.shape), _resident(wb.shape), _resident(wo.shape)],
        out_specs=[row(D_MODEL), _col(tm, D_MODEL)],
        out_shape=[jax.ShapeDtypeStruct((T, D_MODEL), F32), jax.ShapeDtypeStruct((D_MODEL, T), BF16)],
        compiler_params=_params(("parallel",)),
    )(og, cvo, gt, x, wa, wb, wo)


def _ffn_fwd_loss(x1, g, wg, wu, wd, target, g_fin):
    T = x1.shape[0]
    tm = min(256, T)

    def body(x_ref, g_ref, wg_ref, wu_ref, wd_ref, t_ref, gf_ref,
             ht_ref, gate_ref, up_ref, act_ref, loss_ref, dgf_ref, dx2_ref, dx2t_ref):
        @pl.when(pl.program_id(0) == 0)
        def _():
            loss_ref[...] = jnp.zeros_like(loss_ref)
            dgf_ref[...] = jnp.zeros_like(dgf_ref)

        xv = x_ref[...]
        r = lax.rsqrt(jnp.mean(xv * xv, axis=-1, keepdims=True) + EPS)
        hf = xv * r * g_ref[...]
        h = hf.astype(BF16)
        ht_ref[...] = hf.T.astype(BF16)
        gate = _mm_nt(h, wg_ref[...])
        up = _mm_nt(h, wu_ref[...])
        gate_ref[...] = gate.astype(STASH)
        up_ref[...] = up.astype(STASH)
        act = (gate * _sigmoid(gate) * up).astype(BF16)
        act_ref[...] = act
        x2 = xv + jnp.dot(act, wd_ref[...], preferred_element_type=F32)

        gv = gf_ref[...]
        r2 = lax.rsqrt(jnp.mean(x2 * x2, axis=-1, keepdims=True) + EPS)
        xh = x2 * r2
        err = xh * gv - t_ref[...]
        loss_ref[...] += 0.5 * jnp.sum(jnp.mean(err * err, axis=-1, keepdims=True), axis=0, keepdims=True)
        dy = err * (1.0 / D_MODEL)
        dgf_ref[...] += jnp.sum(dy * xh, axis=0, keepdims=True)
        w = dy * gv
        dx2 = r2 * (w - xh * jnp.mean(w * xh, axis=-1, keepdims=True))
        dx2_ref[...] = dx2
        dx2t_ref[...] = dx2.T.astype(BF16)

    row = lambda n: pl.BlockSpec((tm, n), lambda i: (i, 0))
    return pl.pallas_call(
        body, name="ffn_fwd_loss", grid=(T // tm,),
        in_specs=[row(D_MODEL), _full((1, D_MODEL)), _resident(wg.shape), _resident(wu.shape), _resident(wd.shape),
                  row(D_MODEL), _full((1, D_MODEL))],
        out_specs=[_col(tm, D_MODEL), row(D_FF), row(D_FF), row(D_FF), _full((1, 128)), _full((1, D_MODEL)),
                   row(D_MODEL), _col(tm, D_MODEL)],
        out_shape=[jax.ShapeDtypeStruct((D_MODEL, T), BF16), jax.ShapeDtypeStruct((T, D_FF), STASH),
                   jax.ShapeDtypeStruct((T, D_FF), STASH), jax.ShapeDtypeStruct((T, D_FF), BF16),
                   jax.ShapeDtypeStruct((1, 128), F32), jax.ShapeDtypeStruct((1, D_MODEL), F32),
                   jax.ShapeDtypeStruct((T, D_MODEL), F32), jax.ShapeDtypeStruct((D_MODEL, T), BF16)],
        compiler_params=_params(("arbitrary",)),
    )(x1, g, wg, wu, wd, target, g_fin)


def _ffn_bwd(dx2, x1, gate, up, g, wg, wu, wd):
    T = x1.shape[0]
    tm = min(256, T)

    def body(dx2_ref, x_ref, gate_ref, up_ref, g_ref, wg_ref, wu_ref, wd_ref, dgate_ref, dup_ref, dx1_ref, dgn_ref):
        @pl.when(pl.program_id(0) == 0)
        def _():
            dgn_ref[...] = jnp.zeros_like(dgn_ref)

        dx2 = dx2_ref[...]
        dact = _mm_nt(dx2, wd_ref[...])
        gate = gate_ref[...].astype(F32)
        s = _sigmoid(gate)
        dgate = (dact * up_ref[...].astype(F32) * (s * (1.0 + gate * (1.0 - s)))).astype(BF16)
        dup = (dact * (gate * s)).astype(BF16)
        dgate_ref[...] = dgate
        dup_ref[...] = dup
        dh = _mm(dgate, wg_ref[...]) + _mm(dup, wu_ref[...])
        xv = x_ref[...]
        r = lax.rsqrt(jnp.mean(xv * xv, axis=-1, keepdims=True) + EPS)
        xh = xv * r
        dgn_ref[...] += jnp.sum(dh * xh, axis=0, keepdims=True)
        w = dh * g_ref[...]
        dx1_ref[...] = dx2 + r * (w - xh * jnp.mean(w * xh, axis=-1, keepdims=True))

    row = lambda n: pl.BlockSpec((tm, n), lambda i: (i, 0))
    return pl.pallas_call(
        body, name="ffn_bwd", grid=(T // tm,),
        in_specs=[row(D_MODEL), row(D_MODEL), row(D_FF), row(D_FF), _full((1, D_MODEL)),
                  _resident(wg.shape), _resident(wu.shape), _resident(wd.shape)],
        out_specs=[row(D_FF), row(D_FF), row(D_MODEL), _full((1, D_MODEL))],
        out_shape=[jax.ShapeDtypeStruct((T, D_FF), BF16), jax.ShapeDtypeStruct((T, D_FF), BF16),
                   jax.ShapeDtypeStruct((T, D_MODEL), F32), jax.ShapeDtypeStruct((1, D_MODEL), F32)],
        compiler_params=_params(("arbitrary",)),
    )(dx2, x1, gate, up, g, wg, wu, wd)


def _merge_bwd(dx1, og, cvo, gt, cv, wa, wb, wo, conv_w):
    T = dx1.shape[0]
    tm = min(512, T)
    nt = T // tm

    def body(dx_ref, og_ref, cvo_ref, gt_ref, cv_ref, halo_ref, wa_ref, wb_ref, wo_ref, cw_ref,
             dgt_ref, dya_ref, dyb_ref, dog_ref, dcv_ref, dcw_ref, prev_u, next_dy):
        step = pl.program_id(0)

        @pl.when(step == 0)
        def _():
            next_dy[...] = jnp.zeros_like(next_dy)
            dcw_ref[...] = jnp.zeros_like(dcw_ref)

        dm = _mm_nt(dx_ref[...], wo_ref[...])
        wa = _shard_cols(wa_ref)
        wb = _shard_cols(wb_ref)
        ya = jnp.dot(og_ref[...], wa, preferred_element_type=F32)
        yb = jnp.dot(cvo_ref[...], wb, preferred_element_type=F32)
        sa = _sigmoid(gt_ref[:, :D_MODEL].astype(F32))
        sb = _sigmoid(gt_ref[:, D_MODEL:].astype(F32))
        dgt_ref[:, :D_MODEL] = (dm * ya * (sa * (1.0 - sa))).astype(BF16)
        dgt_ref[:, D_MODEL:] = (dm * yb * (sb * (1.0 - sb))).astype(BF16)
        dya = (dm * sa).astype(BF16)
        dyb = (dm * sb).astype(BF16)
        dya_ref[...] = dya
        dyb_ref[...] = dyb
        dog_ref[...] = _mm_nt(dya, wa)
        dcvo = _mm_nt(dyb, wb)

        cvt = cv_ref[...].astype(F32)
        c, bg, xb = cvt[:, :CONV_WIDTH], cvt[:, CONV_WIDTH:2 * CONV_WIDTH], cvt[:, 2 * CONV_WIDTH:]
        halo = halo_ref[...].astype(F32)
        first_tile = step == nt - 1
        prev_u[...] = jnp.where(first_tile, 0.0, halo[:, :CONV_WIDTH] * halo[:, 2 * CONV_WIDTH:])
        u = c * xb
        row = lax.broadcasted_iota(jnp.int32, u.shape, 0)
        p1 = prev_u[HALO - 1:HALO, :]
        p2 = prev_u[HALO - 2:HALO - 1, :]
        u1 = jnp.where(row >= 1, pltpu.roll(u, 1, 0), p1)
        u2 = jnp.where(row >= 2, pltpu.roll(u, 2, 0), jnp.where(row == 1, p1, p2))
        w0, w1, w2 = cw_ref[0:1, :], cw_ref[1:2, :], cw_ref[2:3, :]
        y = w0 * u2 + w1 * u1 + w2 * u
        dcv_ref[:, CONV_WIDTH:2 * CONV_WIDTH] = (dcvo * y).astype(BF16)
        dy = dcvo * bg
        dcw_ref[0:1, :] += jnp.sum(dy * u2, axis=0, keepdims=True)
        dcw_ref[1:2, :] += jnp.sum(dy * u1, axis=0, keepdims=True)
        dcw_ref[2:3, :] += jnp.sum(dy * u, axis=0, keepdims=True)
        n1 = next_dy[0:1, :]
        n2 = next_dy[1:2, :]
        dy1 = jnp.where(row < tm - 1, pltpu.roll(dy, tm - 1, 0), n1)
        dy2 = jnp.where(row < tm - 2, pltpu.roll(dy, tm - 2, 0), jnp.where(row == tm - 2, n1, n2))
        du = w2 * dy + w1 * dy1 + w0 * dy2
        dcv_ref[:, :CONV_WIDTH] = (du * xb).astype(BF16)
        dcv_ref[:, 2 * CONV_WIDTH:] = (du * c).astype(BF16)
        next_dy[...] = dy[:HALO, :]

    rt = lambda i: nt - 1 - i
    row = lambda n: pl.BlockSpec((tm, n), lambda i: (rt(i), 0))
    halo = pl.BlockSpec((HALO, N_CV), lambda i: (jnp.maximum(rt(i) * (tm // HALO) - 1, 0), 0))
    return pl.pallas_call(
        body, name="merge_bwd", grid=(nt,),
        in_specs=[row(D_MODEL), row(HG_WIDTH), row(CONV_WIDTH), row(2 * D_MODEL), row(N_CV), halo,
                  _resident(wa.shape), _resident(wb.shape), _resident(wo.shape), _full((CONV_K, CONV_WIDTH))],
        out_specs=[row(2 * D_MODEL), row(D_MODEL), row(D_MODEL), row(HG_WIDTH), row(N_CV),
                   _full((CONV_K, CONV_WIDTH))],
        out_shape=[jax.ShapeDtypeStruct((T, 2 * D_MODEL), BF16), jax.ShapeDtypeStruct((T, D_MODEL), BF16),
                   jax.ShapeDtypeStruct((T, D_MODEL), BF16), jax.ShapeDtypeStruct((T, HG_WIDTH), F32),
                   jax.ShapeDtypeStruct((T, N_CV), BF16), jax.ShapeDtypeStruct((CONV_K, CONV_WIDTH), F32)],
        scratch_shapes=[pltpu.VMEM((HALO, CONV_WIDTH), F32), pltpu.VMEM((HALO, CONV_WIDTH), F32)],
        compiler_params=_params(("arbitrary",)),
    )(dx1, og, cvo, gt, cv, cv, wa, wb, wo, conv_w)


def _drop_operands(body, first, count):
    def wrapped(*refs):
        return body(*refs[:first], *refs[first + count:])
    return wrapped


def _hg_bwd(dog, hg, o, st, low, gn, after=()):
    T = hg.shape[0]
    tb = min(512, T)
    sb = min(256, tb)
    nb = T // tb
    nc = tb // CHUNK
    wid = HEADS_PER_STEP * HEAD_DIM

    def body(q_ref, f_ref, i_ref, g_ref, low_ref, gn_ref, o_ref, dog_ref, st_ref,
             dq_ref, df_ref, di_ref, dg_ref, dlow_ref, dgn_ref,
             ds_scr, dqi_scr, dko_scr, dv_scr, dd_scr, dqh_scr, dkh_scr):
        h = pl.program_id(0)
        t = pl.program_id(1)

        @pl.when(t == 0)
        def _():
            ds_scr[...] = jnp.zeros_like(ds_scr)
            dlow_ref[...] = jnp.zeros_like(dlow_ref)

        @pl.when((t == 0) & (h == 0))
        def _():
            dgn_ref[...] = jnp.zeros_like(dgn_ref)

        pos = _chunk_pos((tb, HEAD_DIM))
        mask = _intra_mask(sb)
        gnv = gn_ref[...]
        lanes = [slice(hh * HEAD_DIM, (hh + 1) * HEAD_DIM) for hh in range(HEADS_PER_STEP)]
        heads = []
        for hh, ln in enumerate(lanes):
            lb, lb1 = _lower_bound(low_ref.at[:, ln])
            qr = q_ref[:, ln]
            sq, q, sg, f, k, e_qa, e_ka, e_b, e_ko, dec = _hg_gates(qr, f_ref[:, ln], lb, pos, tb)

            gr = g_ref[:, ln]
            o = o_ref[:, ln]
            dog_v = dog_ref[:, ln]
            sgr = _sigmoid(gr)
            r = lax.rsqrt(jnp.mean(o * o, axis=-1, keepdims=True) + EPS)
            oh = o * r
            dg_ref[:, ln] = (dog_v * (oh * gnv) * (sgr * (1.0 + gr * (1.0 - sgr)))).astype(BF16)
            don = dog_v * (gr * sgr)
            dgn_ref[...] += jnp.sum(don * oh, axis=0, keepdims=True)
            w = don * gnv
            do = (r * (w - oh * jnp.mean(w * oh, axis=-1, keepdims=True))).astype(BF16)

            qh = (q * e_qa).astype(BF16)
            kh = (k * e_ka).astype(BF16)
            qi = (q * e_b).astype(BF16)
            ko = (k * e_ko).astype(BF16)
            vb = i_ref[:, ln].astype(BF16)

            for s in range(tb // sb):
                sl = slice(s * sb, (s + 1) * sb)
                p = jnp.where(mask, _mm_nt(qh[sl], kh[sl]), 0.0).astype(BF16)
                dp = jnp.where(mask, _mm_nt(do[sl], vb[sl]), 0.0).astype(BF16)
                dv_scr[sl, ln] = _mm_tn(p, do[sl])
                dqh_scr[sl, ln] = _mm(dp, kh[sl])
                dkh_scr[sl, ln] = _mm_tn(dp, qh[sl])
            heads.append(dict(lb=lb, lb1=lb1, qr=qr, sq=sq, q=q, sg=sg, f=f, k=k, e_qa=e_qa, e_ka=e_ka, e_b=e_b,
                              e_ko=e_ko, dec=dec, do=do, qi=qi, ko=ko, vb=vb, ds=ds_scr[hh]))

        for c in reversed(range(nc)):
            sl = slice(c * CHUNK, (c + 1) * CHUNK)
            for hh, ln in enumerate(lanes):
                hd = heads[hh]
                ds = hd["ds"]
                st_c = st_ref[hh, c]
                dqi_scr[sl, ln] = _mm(hd["do"][sl], st_c)
                dko_scr[sl, ln] = _mm(hd["vb"][sl], ds)
                dv_scr[sl, ln] = dv_scr[sl, ln] + _mm_nt(hd["ko"][sl], ds)
                dd_scr[sl, ln] = jnp.broadcast_to(jnp.sum(ds * st_c, axis=0, keepdims=True), (CHUNK, HEAD_DIM))
                hd["ds"] = hd["dec"][c * CHUNK:c * CHUNK + 1, :] * ds + _mm_tn(hd["do"][sl], hd["qi"][sl])

        for hh, ln in enumerate(lanes):
            hd = heads[hh]
            ds_scr[hh] = hd["ds"]
            q, k, lb = hd["q"], hd["k"], hd["lb"]
            dko_e = dko_scr[:, ln] * hd["e_ko"]
            dq = dqh_scr[:, ln] * hd["e_qa"] + dqi_scr[:, ln] * hd["e_b"]
            dk = dkh_scr[:, ln] * hd["e_ka"] + dko_e
            kd3 = (k * dko_e).reshape(nc, CHUNK, HEAD_DIM)
            last = jnp.broadcast_to(jnp.sum(kd3, axis=1, keepdims=True), kd3.shape).reshape(tb, HEAD_DIM)
            db = q * dq - k * dk + jnp.where(pos == CHUNK - 1, hd["dec"] * dd_scr[:, ln] + last, 0.0)
            dlg = _chunk_rev_cumsum(db, pos)
            dfv = dlg / hd["f"] - dk
            s_low = jnp.sum(dfv * (1.0 - hd["sg"]), axis=0, keepdims=True)
            dlow_ref[0:1, ln] += s_low * lb * (1.0 - lb)
            dlow_ref[1:2, ln] += -s_low * lb * hd["lb1"]
            df_ref[:, ln] = (dfv * (1.0 - lb) * hd["sg"] * (1.0 - hd["sg"])).astype(BF16)
            dq_ref[:, ln] = (dq * Q_SCALE * (hd["sq"] * (1.0 + hd["qr"] * (1.0 - hd["sq"])))).astype(BF16)
            di_ref[:, ln] = dv_scr[:, ln].astype(BF16)

    rt = lambda t: nb - 1 - t
    col = lambda p: pl.BlockSpec((tb, wid), lambda h, t: (rt(t), p * HEAD_GROUPS + h))
    hcol = pl.BlockSpec((tb, wid), lambda h, t: (rt(t), h))
    piece = jax.ShapeDtypeStruct((T, HG_WIDTH), BF16)
    tile = pltpu.VMEM((tb, wid), F32)
    return pl.pallas_call(
        _drop_operands(body, 9, len(after)), name="hg_bwd", grid=(HEAD_GROUPS, nb),
        in_specs=[col(0), col(1), col(2), col(3), pl.BlockSpec((2, wid), lambda h, t: (0, h)),
                  pl.BlockSpec((1, HEAD_DIM), lambda h, t: (0, 0)), hcol, hcol,
                  pl.BlockSpec((HEADS_PER_STEP, nc, HEAD_DIM, HEAD_DIM), lambda h, t: (h, rt(t), 0, 0))]
                 + [HBM_SPEC] * len(after),
        out_specs=[hcol, hcol, hcol, hcol, pl.BlockSpec((2, wid), lambda h, t: (0, h)),
                   pl.BlockSpec((1, HEAD_DIM), lambda h, t: (0, 0))],
        out_shape=[piece, piece, piece, piece, jax.ShapeDtypeStruct((2, HG_WIDTH), F32),
                   jax.ShapeDtypeStruct((1, HEAD_DIM), F32)],
        scratch_shapes=[pltpu.VMEM((HEADS_PER_STEP, HEAD_DIM, HEAD_DIM), F32), tile, tile, tile, tile, tile, tile],
        compiler_params=_params(("arbitrary", "arbitrary")),
    )(hg, hg, hg, hg, low, gn, o, dog, st, *after)


def _in_bwd(dparts, w_in, x, dx1, g, after=()):
    T = x.shape[0]
    tm = min(512, T)
    widths = [p.shape[1] for p in dparts]
    offs = [sum(widths[:i]) for i in range(len(widths))]
    n = len(dparts)

    def body(*refs):
        d_refs = refs[:n]
        w_ref, x_ref, dx1_ref, g_ref, dx_ref, dgn_ref = refs[n:]

        @pl.when(pl.program_id(0) == 0)
        def _():
            dgn_ref[...] = jnp.zeros_like(dgn_ref)

        dh = None
        for d_ref, off, wd in zip(d_refs, offs, widths):
            part = _mm(d_ref[...], w_ref[off:off + wd, :])
            dh = part if dh is None else dh + part
        xv = x_ref[...]
        r = lax.rsqrt(jnp.mean(xv * xv, axis=-1, keepdims=True) + EPS)
        xh = xv * r
        dgn_ref[...] += jnp.sum(dh * xh, axis=0, keepdims=True)
        w = dh * g_ref[...]
        dx_ref[...] = dx1_ref[...] + r * (w - xh * jnp.mean(w * xh, axis=-1, keepdims=True))

    row = lambda m: pl.BlockSpec((tm, m), lambda i: (i, 0))
    return pl.pallas_call(
        _drop_operands(body, n + 4, len(after)), name="in_bwd", grid=(T // tm,),
        in_specs=[row(wd) for wd in widths] + [_resident(w_in.shape), row(D_MODEL), row(D_MODEL), _full((1, D_MODEL))]
                 + [HBM_SPEC] * len(after),
        out_specs=[row(D_MODEL), _full((1, D_MODEL))],
        out_shape=[jax.ShapeDtypeStruct((T, D_MODEL), F32), jax.ShapeDtypeStruct((1, D_MODEL), F32)],
        compiler_params=_params(("arbitrary",)),
    )(*dparts, w_in, x, dx1, g, *after)


def _wgrad(name, at, b, tn, transposed=False, tk=2048, after=()):
    M, T = at.shape
    N = b.shape[1]
    tk = min(tk, T)
    nk = T // tk
    if transposed:
        out_spec, out_shape = pl.BlockSpec((tn, M), lambda j, k: (j, 0)), (N, M)
    else:
        out_spec, out_shape = pl.BlockSpec((M, tn), lambda j, k: (0, j)), (M, N)

    if nk == 1:
        def body1(a_ref, b_ref, o_ref):
            part = _mm(a_ref[...], b_ref[...])
            o_ref[...] = (part.T if transposed else part).astype(BF16)

        return pl.pallas_call(
            _drop_operands(body1, 2, len(after)), name=name, grid=(N // tn, 1),
            in_specs=[_resident((M, T)), pl.BlockSpec((T, tn), lambda j, k: (0, j))] + [HBM_SPEC] * len(after),
            out_specs=out_spec, out_shape=jax.ShapeDtypeStruct(out_shape, BF16),
            compiler_params=_params(("parallel", "arbitrary")),
        )(at, b, *after)

    def body(a_ref, b_ref, o_ref, acc):
        k = pl.program_id(1)

        @pl.when(k == 0)
        def _():
            acc[...] = jnp.zeros_like(acc)

        acc[...] += _mm(a_ref[...], b_ref[...])

        @pl.when(k == nk - 1)
        def _():
            o_ref[...] = (acc[...].T if transposed else acc[...]).astype(BF16)

    return pl.pallas_call(
        _drop_operands(body, 2, len(after)), name=name, grid=(N // tn, nk),
        in_specs=[pl.BlockSpec((M, tk), lambda j, k: (0, k)), pl.BlockSpec((tk, tn), lambda j, k: (k, j))]
                 + [HBM_SPEC] * len(after),
        out_specs=out_spec, out_shape=jax.ShapeDtypeStruct(out_shape, BF16),
        scratch_shapes=[pltpu.VMEM((M, tn), F32)],
        compiler_params=_params(("parallel", "arbitrary")),
    )(at, b, *after)


def _wgrad_branches(ogt, dya, cvot, dyb):
    M, T = ogt.shape
    N = dya.shape[1]
    c = N // N_DEV
    per = 2
    tn = per * c

    def body(at_ref, da_ref, bt_ref, db_ref, oa_ref, ob_ref):
        ga = _mm(at_ref[...], da_ref[...])
        gb = _mm(bt_ref[...], db_ref[...])
        for s in range(per):
            oa_ref[s] = ga[:, s * c:(s + 1) * c].astype(BF16)
            ob_ref[s] = gb[:, s * c:(s + 1) * c].astype(BF16)

    rhs = pl.BlockSpec((T, tn), lambda j: (0, j))
    owners = pl.BlockSpec((per, M, c), lambda j: (j, 0, 0))
    out = jax.ShapeDtypeStruct((N_DEV, M, c), BF16)
    return pl.pallas_call(
        body, name="wgrad_branches", grid=(N // tn,),
        in_specs=[_resident((M, T)), rhs, _resident((M, T)), rhs], out_specs=[owners] * 2, out_shape=[out, out],
        compiler_params=_params(("parallel",)),
    )(ogt, dya, cvot, dyb)


def _wgrad_in(ht, dparts, after=()):
    M, T = ht.shape
    tn = 256
    nblk = [p.shape[1] // tn for p in dparts]
    start = [sum(nblk[:i]) for i in range(len(nblk))]
    n = len(dparts)

    def body(a_ref, *refs):
        d_refs, o_ref = refs[:n], refs[n]
        j = pl.program_id(0)
        for d_ref, s, nb in zip(d_refs, start, nblk):
            @pl.when((j >= s) & (j < s + nb))
            def _():
                o_ref[...] = _mm(a_ref[...], d_ref[...]).T.astype(BF16)

    def piece_spec(s, nb):
        return pl.BlockSpec((T, tn), lambda j: (0, jnp.clip(j - s, 0, nb - 1)))

    return pl.pallas_call(
        _drop_operands(body, 1 + n, len(after)), name="wgrad_in", grid=(sum(nblk),),
        in_specs=[_resident((M, T))] + [piece_spec(s, nb) for s, nb in zip(start, nblk)] + [HBM_SPEC] * len(after),
        out_specs=pl.BlockSpec((tn, M), lambda j: (j, 0)),
        out_shape=jax.ShapeDtypeStruct((sum(nblk) * tn, M), BF16),
        compiler_params=_params(("parallel",)),
    )(ht, *dparts, *after)


def _adamw_math(w, g, m, v):
    m = ADAM_B1 * m + (1.0 - ADAM_B1) * g
    v = ADAM_B2 * v + (1.0 - ADAM_B2) * (g * g)
    m_hat = m / (1.0 - ADAM_B1 ** ADAM_STEP)
    v_hat = v / (1.0 - ADAM_B2 ** ADAM_STEP)
    delta = -ADAM_LR * (m_hat / (jnp.sqrt(v_hat) + ADAM_EPS) + ADAM_WD * w)
    return delta, m, v


def _adamw_sum(name, w, parts, m, v):
    R, C = w.shape
    tr = _row_tile(R)

    def body(w_ref, p_ref, m_ref, v_ref, g_out, d_out, m_out, v_out):
        g = p_ref[0].astype(F32)
        for k in range(1, 4):
            g = g + p_ref[k].astype(F32)
        g_out[...] = g
        d_out[...], m_out[...], v_out[...] = _adamw_math(w_ref[...], g, m_ref[...], v_ref[...])

    blk = pl.BlockSpec((tr, C), lambda i: (i, 0))
    out = jax.ShapeDtypeStruct((R, C), F32)
    return pl.pallas_call(
        body, name=name, grid=(R // tr,),
        in_specs=[blk, pl.BlockSpec((4, tr, C), lambda i: (0, i, 0)), blk, blk],
        out_specs=[blk, blk, blk, blk], out_shape=[out, out, out, out],
        compiler_params=_params(("parallel",)),
    )(w, parts, m, v)


_SMALL_SLOTS = (("norm_mix_g", 0, 1, 1024), ("norm_ffn_g", 1, 1, 1024), ("norm_final_g", 2, 1, 1024),
                ("lower_bounds", 3, 2, 512), ("hg_norm_g", 5, 1, 128), ("loss", 6, 1, 128), ("conv_w", 8, 3, 512))
_SMALL_PARAMS = tuple(s for s in _SMALL_SLOTS if s[0] != "loss")
CONV_SHARD = CONV_WIDTH // N_DEV


def _small_pack(small):
    def body(*refs):
        out = refs[-1]
        out[...] = jnp.zeros_like(out)
        for ref, (_, row, rows, lanes) in zip(refs[:-1], _SMALL_SLOTS):
            out[row:row + rows, 0:lanes] = ref[...]

    vmem = pl.BlockSpec(memory_space=pltpu.VMEM)
    return pl.pallas_call(
        body, name="small_pack", in_specs=[vmem] * len(_SMALL_SLOTS), out_specs=vmem,
        out_shape=jax.ShapeDtypeStruct((SMALL_ROWS, 1024), F32),
    )(*[small[name] for name, _, _, _ in _SMALL_SLOTS])


def _small_update(gathered, dev, w, m, v):
    n = len(_SMALL_PARAMS)

    def body(dev_ref, g_ref, *refs):
        w_refs, m_refs, v_refs = refs[:n], refs[n:2 * n], refs[2 * n:3 * n]
        loss_ref, out_refs, sum_scr = refs[3 * n], refs[3 * n + 1:-1], refs[-1]
        total = g_ref[0]
        for k in range(1, N_DEV):
            total = total + g_ref[k]
        sum_scr[...] = total
        loss_ref[...] = sum_scr[6:7, 0:128]
        for p, (name, row, rows, lanes) in enumerate(_SMALL_PARAMS):
            if name == "conv_w":
                g = sum_scr[row:row + rows, 0:CONV_SHARD]
                for s in range(1, N_DEV):
                    g = jnp.where(dev_ref[0] == s, sum_scr[row:row + rows, s * CONV_SHARD:(s + 1) * CONV_SHARD], g)
            else:
                g = sum_scr[row:row + rows, 0:lanes]
            delta, m_new, v_new = _adamw_math(w_refs[p][...], g, m_refs[p][...], v_refs[p][...])
            out_refs[4 * p][...] = g
            out_refs[4 * p + 1][...] = delta
            out_refs[4 * p + 2][...] = m_new
            out_refs[4 * p + 3][...] = v_new

    vmem = pl.BlockSpec(memory_space=pltpu.VMEM)
    outs = [jax.ShapeDtypeStruct((1, 128), F32)]
    for a in w:
        outs += [jax.ShapeDtypeStruct(a.shape, F32)] * 4
    return pl.pallas_call(
        body, name="small_update",
        in_specs=[pl.BlockSpec(memory_space=pltpu.SMEM)] + [vmem] * (1 + 3 * n), out_specs=[vmem] * len(outs),
        out_shape=outs, scratch_shapes=[pltpu.VMEM((SMALL_ROWS, 1024), F32)],
    )(dev, gathered, *w, *m, *v)


def _row_tile(rows):
    for parts in (4, 2):
        if rows % (16 * parts) == 0:
            return rows // parts
    return rows


def _pair_sum(name, by_owner, got, core, after=()):
    n = len(got)

    def body(core_ref, *refs):
        for a_ref, b_ref, o_ref in zip(refs[:n], refs[n:2 * n], refs[2 * n:]):
            o_ref[...] = (a_ref[...].astype(F32) + b_ref[...].astype(F32)).astype(BF16)

    def blk(g):
        return pl.BlockSpec((None,) + g.shape[1:], lambda k, core_ref: (k, 0, 0))

    def mine(g):
        return pl.BlockSpec((None,) + g.shape[1:], lambda k, core_ref: (2 * k + core_ref[0], 0, 0))

    return pl.pallas_call(
        _drop_operands(body, 1 + 2 * n, len(after)), name=name,
        grid_spec=pltpu.PrefetchScalarGridSpec(
            num_scalar_prefetch=1, grid=(4,),
            in_specs=[mine(g) for g in got] + [blk(g) for g in got] + [HBM_SPEC] * len(after),
            out_specs=[blk(g) for g in got]),
        out_shape=[jax.ShapeDtypeStruct(g.shape, BF16) for g in got],
        compiler_params=_params(("parallel",)),
    )(core, *by_owner, *got, *after)


MESH = pl.DeviceIdType.MESH
HBM_SPEC = pl.BlockSpec(memory_space=pl.ANY)


def _handshake(peers):
    barrier = pltpu.get_barrier_semaphore()
    for peer in peers:
        pl.semaphore_signal(barrier, inc=1, device_id=peer, device_id_type=MESH)
    pl.semaphore_wait(barrier, len(peers))


def _comm_call(body, name, operands, out_shape, scratch, collective_id):
    if collective_id is None:
        return pl.pallas_call(body, name=name, in_specs=[HBM_SPEC] * len(operands), out_specs=[HBM_SPEC] * len(out_shape),
                              out_shape=out_shape, scratch_shapes=scratch)(*operands)
    return pl.kernel(body, out_type=out_shape, mesh=plsc.ScalarSubcoreMesh(axis_name="sequencer", num_cores=1),
                     scratch_types=scratch, name=name,
                     compiler_params=pltpu.CompilerParams(collective_id=collective_id))(*operands)


def _all_gather(name, blocks, collective_id=None, after=()):
    n = len(blocks)
    na = len(after)

    def body(*refs):
        x_refs, out_refs = refs[:n], refs[n + na:2 * n + na]
        send_sems, recv_sems, local_sems = refs[2 * n + na:]
        x, y, c = lax.axis_index("x"), lax.axis_index("y"), lax.axis_index("c")
        me, sibling = (x, y, c), (x, y, 1 - c)
        chips = [(1 - x, y), (x, 1 - y), (1 - x, 1 - y)]
        if collective_id is not None:
            _handshake([sibling] + [(*chip, c) for chip in chips])

        def slot(i, px, py, pc):
            return out_refs[i].at[4 * px + 2 * py + pc]

        def copy(i, k, blk, to, src=None):
            return pltpu.make_async_remote_copy(
                src_ref=slot(i, *blk) if src is None else src, dst_ref=slot(i, *blk),
                send_sem=send_sems.at[7 * i + k], recv_sem=recv_sems.at[7 * i + k], device_id=to, device_id_type=MESH)

        mine = [pltpu.make_async_copy(x_refs[i], slot(i, *me), local_sems.at[i]) for i in range(n)]
        for cp in mine:
            cp.start()
        first = []
        for i in range(n):
            first.append(copy(i, 0, me, sibling, src=x_refs[i]))
            first += [copy(i, 1 + j, me, (*chip, c), src=x_refs[i]) for j, chip in enumerate(chips)]
        for cp in first:
            cp.start()
        passed = []
        for i in range(n):
            for j, chip in enumerate(chips):
                copy(i, 1 + j, (*chip, c), me).wait_recv()
                passed.append(copy(i, 4 + j, (*chip, c), sibling))
                passed[-1].start()
        for i in range(n):
            copy(i, 0, sibling, me).wait_recv()
            for j, chip in enumerate(chips):
                copy(i, 4 + j, (*chip, 1 - c), me).wait_recv()
        for cp in first + passed:
            cp.wait_send()
        for cp in mine:
            cp.wait()

    return _comm_call(
        body, name, list(blocks) + list(after), [jax.ShapeDtypeStruct((N_DEV,) + b.shape, b.dtype) for b in blocks],
        [pltpu.SemaphoreType.DMA((7 * n,)), pltpu.SemaphoreType.DMA((7 * n,)), pltpu.SemaphoreType.DMA((n,))],
        collective_id)


def _sibling_swap(name, by_owner, collective_id=None, after=()):
    n = len(by_owner)
    na = len(after)

    def body(*refs):
        x_refs, out_refs = refs[:n], refs[n + na:2 * n + na]
        send_sems, recv_sems = refs[2 * n + na:]
        x, y, c = lax.axis_index("x"), lax.axis_index("y"), lax.axis_index("c")
        if collective_id is not None:
            _handshake([(x, y, 1 - c)])
        copies = []
        for i in range(n):
            for k in range(4):
                copies.append(pltpu.make_async_remote_copy(
                    src_ref=x_refs[i].at[2 * k + 1 - c], dst_ref=out_refs[i].at[k],
                    send_sem=send_sems.at[4 * i + k], recv_sem=recv_sems.at[4 * i + k],
                    device_id=(x, y, 1 - c), device_id_type=MESH))
        for cp in copies:
            cp.start()
        for cp in copies:
            cp.wait()

    return _comm_call(
        body, name, list(by_owner) + list(after),
        [jax.ShapeDtypeStruct((4,) + b.shape[1:], b.dtype) for b in by_owner],
        [pltpu.SemaphoreType.DMA((4 * n,)), pltpu.SemaphoreType.DMA((4 * n,))], collective_id)


def _chip_exchange(name, sums, collective_id=None, after=()):
    n = len(sums)
    na = len(after)

    def body(*refs):
        x_refs, out_refs = refs[:n], refs[n + na:2 * n + na]
        send_sems, recv_sems, local_sems = refs[2 * n + na:]
        x, y, c = lax.axis_index("x"), lax.axis_index("y"), lax.axis_index("c")
        chips = [(1 - x, y), (x, 1 - y), (1 - x, 1 - y)]
        my_chip = 2 * x + y
        if collective_id is not None:
            _handshake([(cx, cy, c) for cx, cy in chips])
        mine = [pltpu.make_async_copy(x_refs[i].at[my_chip], out_refs[i].at[my_chip], local_sems.at[i])
                for i in range(n)]
        for cp in mine:
            cp.start()
        sends = []
        for i in range(n):
            for j, (cx, cy) in enumerate(chips):
                sends.append(pltpu.make_async_remote_copy(
                    src_ref=x_refs[i].at[2 * cx + cy], dst_ref=out_refs[i].at[my_chip],
                    send_sem=send_sems.at[3 * i + j], recv_sem=recv_sems.at[3 * i + j],
                    device_id=(cx, cy, c), device_id_type=MESH))
        for cp in sends:
            cp.start()
        for i in range(n):
            for j, (cx, cy) in enumerate(chips):
                pltpu.make_async_remote_copy(
                    src_ref=x_refs[i].at[my_chip], dst_ref=out_refs[i].at[2 * cx + cy],
                    send_sem=send_sems.at[3 * i + j], recv_sem=recv_sems.at[3 * i + j],
                    device_id=(cx, cy, c), device_id_type=MESH).wait_recv()
        for cp in sends:
            cp.wait_send()
        for cp in mine:
            cp.wait()

    return _comm_call(
        body, name, list(sums) + list(after), [jax.ShapeDtypeStruct(s.shape, s.dtype) for s in sums],
        [pltpu.SemaphoreType.DMA((3 * n,)), pltpu.SemaphoreType.DMA((3 * n,)), pltpu.SemaphoreType.DMA((n,))],
        collective_id)


def _cast_shards(shards):
    n = len(shards)

    def body(*refs):
        for i in range(n):
            refs[n + i][...] = refs[i][...].astype(BF16)

    vmem = pl.BlockSpec(memory_space=pltpu.VMEM)
    return pl.pallas_call(
        body, name="cast_shards", in_specs=[vmem] * n, out_specs=[vmem] * n,
        out_shape=[jax.ShapeDtypeStruct(s.shape, BF16) for s in shards],
        compiler_params=pltpu.CompilerParams(vmem_limit_bytes=VMEM_LIMIT_V7X),
    )(*shards)


BIG = ("w_in", "w_branch_a", "w_branch_b", "w_out", "w_ffn_gate", "w_ffn_up", "w_ffn_down")


def _local_step(x, target, gains, low, conv_w, wg8, reduce):
    g_mix, g_hg, g_ffn, g_fin = gains
    w_in = wg8["w_in"].reshape(N_IN, D_MODEL)
    wg = wg8["w_ffn_gate"].reshape(D_FF, D_MODEL)
    wu = wg8["w_ffn_up"].reshape(D_FF, D_MODEL)
    wa, wb = wg8["w_branch_a"], wg8["w_branch_b"]
    wo = wg8["w_out"].reshape(D_MODEL, D_MODEL)
    wd = wg8["w_ffn_down"].reshape(D_FF, D_MODEL)

    ht, hg, cv, gt, cvo, cvot = _fwd_in(x, g_mix, w_in, conv_w)
    o, og, ogt, st = _hg_fwd(hg, low, g_hg)
    x1, mgt = _merge_fwd(og, cvo, gt, x, wa, wb, wo)
    h2t, gate, up, act, loss, d_gfin, dx2, dx2t = _ffn_fwd_loss(x1, g_ffn, wg, wu, wd, target, g_fin)

    dgate, dup, dx1, d_gffn = _ffn_bwd(dx2, x1, gate, up, g_ffn, wg, wu, wd)
    ffn = dict(
        w_ffn_down=_wgrad("wgrad_ffn_down", dx2t, act, 256, transposed=True, tk=4096
                          ).reshape(N_DEV, D_FF // N_DEV, D_MODEL),
        w_ffn_gate=_wgrad("wgrad_ffn_gate", h2t, dgate, 256, transposed=True, tk=4096
                          ).reshape(N_DEV, D_FF // N_DEV, D_MODEL),
        w_ffn_up=_wgrad("wgrad_ffn_up", h2t, dup, 256, transposed=True, tk=4096
                        ).reshape(N_DEV, D_FF // N_DEV, D_MODEL))
    dgt, dya, dyb, dog, dcv, d_conv = _merge_bwd(dx1, og, cvo, gt, cv, wa, wb, wo, conv_w)
    sums_ffn, got_ffn = reduce.begin(ffn, sum_after=[dya])
    parts_ffn, updated_ffn = reduce.finish(ffn, sums_ffn)
    grad_a, grad_b = _wgrad_branches(ogt, dya, cvot, dyb)
    out = dict(
        w_out=_wgrad("wgrad_out", mgt, dx1, 256, tk=4096, after=sums_ffn[:1]
                     ).reshape(N_DEV, D_MODEL // N_DEV, D_MODEL),
        w_branch_a=grad_a, w_branch_b=grad_b)
    dq, df, di, dg, d_low, d_ghg = _hg_bwd(dog, hg, o, st, low, g_hg, after=list(sums_ffn) + [out["w_out"]])
    sums_out, got_out = reduce.begin(out, after=[parts_ffn[0], dq], sum_after=updated_ffn)
    parts_out, updated_out = reduce.finish(out, sums_out)
    dparts = [dq, df, di, dg, dcv, dgt]
    w_in_grad = dict(w_in=_wgrad_in(ht, dparts, after=sums_out[:1]).reshape(N_DEV, N_IN // N_DEV, D_MODEL))
    sums_in, _ = reduce.begin(w_in_grad, after=parts_out[:1], sum_after=updated_out)
    parts_in, _ = reduce.finish(w_in_grad, sums_in)
    grad_x, d_gmix = _in_bwd(dparts, w_in, x, dx1, g_mix, after=list(parts_out[:1]) + list(sums_in))
    small = dict(norm_mix_g=d_gmix, norm_ffn_g=d_gffn, norm_final_g=d_gfin, lower_bounds=d_low, hg_norm_g=d_ghg,
                 conv_w=d_conv, loss=loss)
    return grad_x, small, parts_in


def _conv_shard_rows(a):
    return jnp.pad(a, ((0, 5), (0, 64)))


def kernel(x, norm_mix_g, w_in, lower_bounds, hg_norm_g, conv_w, w_branch_a, w_branch_b, w_out, norm_ffn_g, w_ffn_gate, w_ffn_up, w_ffn_down, norm_final_g, loss_target, m_norm_mix_g, m_w_in, m_lower_bounds, m_hg_norm_g, m_conv_w, m_w_branch_a, m_w_branch_b, m_w_out, m_norm_ffn_g, m_w_ffn_gate, m_w_ffn_up, m_w_ffn_down, m_norm_final_g, v_norm_mix_g, v_w_in, v_lower_bounds, v_hg_norm_g, v_conv_w, v_w_branch_a, v_w_branch_b, v_w_out, v_norm_ffn_g, v_w_ffn_gate, v_w_ffn_up, v_w_ffn_down, v_norm_final_g):
    cx, cy, cc = lax.axis_index("x"), lax.axis_index("y"), lax.axis_index("c")
    my_dev = 4 * cx + 2 * cy + cc

    def tr(a):
        return a[0].T

    big = dict(w_in=tr(w_in), w_branch_a=w_branch_a[0], w_branch_b=w_branch_b[0], w_out=w_out[0],
               w_ffn_gate=tr(w_ffn_gate), w_ffn_up=tr(w_ffn_up), w_ffn_down=w_ffn_down[0])
    big_m = dict(w_in=tr(m_w_in), w_branch_a=m_w_branch_a[0], w_branch_b=m_w_branch_b[0], w_out=m_w_out[0],
                 w_ffn_gate=tr(m_w_ffn_gate), w_ffn_up=tr(m_w_ffn_up), w_ffn_down=m_w_ffn_down[0])
    big_v = dict(w_in=tr(v_w_in), w_branch_a=v_w_branch_a[0], w_branch_b=v_w_branch_b[0], w_out=v_w_out[0],
                 w_ffn_gate=tr(v_w_ffn_gate), w_ffn_up=tr(v_w_ffn_up), w_ffn_down=v_w_ffn_down[0])
    transposed = ("w_in", "w_ffn_gate", "w_ffn_up")

    shards = dict(zip(BIG, _cast_shards([big[n] for n in BIG])))
    first = _all_gather("gather_w_in", [shards["w_in"], _conv_shard_rows(conv_w[0])])
    ids = iter(range(1, 16))
    mid = _all_gather("gather_mid", [shards[n] for n in BIG[1:4]], collective_id=next(ids), after=first[1:])
    ffn = _all_gather("gather_ffn", [shards[n] for n in BIG[4:]], collective_id=next(ids), after=first[1:])
    wg8 = dict(zip(BIG, [first[0]] + list(mid) + list(ffn)))
    conv_full = first[1][:, :3, :64].transpose(1, 0, 2).reshape(3, CONV_WIDTH)

    core = cc.reshape(1).astype(jnp.int32)
    outs = {}

    class Reduce:
        @staticmethod
        def begin(grads, after=(), sum_after=()):
            names = list(grads)
            by_owner = [grads[n] for n in names]
            got = _sibling_swap("sibling_swap_" + names[0], by_owner, collective_id=next(ids), after=after)
            sums = _pair_sum("pair_sum_" + names[0], by_owner, got, core, after=sum_after)
            return sums, got

        @staticmethod
        def finish(grads, chip_sums, after=()):
            names = list(grads)
            parts = _chip_exchange("chip_exchange_" + names[0], chip_sums, collective_id=next(ids), after=after)
            for n, p in zip(names, parts):
                outs[n] = _adamw_sum("adamw_" + n, big[n], p, big_m[n], big_v[n])
            return parts, [outs[n][1] for n in names]

    gains = (norm_mix_g, hg_norm_g, norm_ffn_g, norm_final_g.reshape(1, D_MODEL))
    grad_x, small, last = _local_step(x[0], loss_target[0], gains, lower_bounds, conv_full, wg8, Reduce)

    small_all = _all_gather("gather_small", [_small_pack(small)], collective_id=next(ids), after=last[:1])

    def small_state(a):
        return [a[0], a[1], a[2].reshape(1, D_MODEL), a[3], a[4], a[5][0]]

    upd = _small_update(
        small_all[0], my_dev.reshape(1).astype(jnp.int32),
        small_state((norm_mix_g, norm_ffn_g, norm_final_g, lower_bounds, hg_norm_g, conv_w)),
        small_state((m_norm_mix_g, m_norm_ffn_g, m_norm_final_g, m_lower_bounds, m_hg_norm_g, m_conv_w)),
        small_state((v_norm_mix_g, v_norm_ffn_g, v_norm_final_g, v_lower_bounds, v_hg_norm_g, v_conv_w)))
    loss = upd[0][0, 0]
    small_shape = dict(norm_final_g=(D_MODEL,), conv_w=(1, 3, CONV_SHARD))
    for p, (name, _, _, _) in enumerate(_SMALL_PARAMS):
        outs[name] = [a.reshape(small_shape.get(name, a.shape)) for a in upd[1 + 4 * p:5 + 4 * p]]

    order = ["norm_mix_g", "w_in", "lower_bounds", "hg_norm_g", "conv_w", "w_branch_a", "w_branch_b", "w_out",
             "norm_ffn_g", "w_ffn_gate", "w_ffn_up", "w_ffn_down", "norm_final_g"]
    result = [loss, grad_x[None]]
    for k in range(4):
        for n in order:
            if n in BIG:
                result.append((outs[n][k].T if n in transposed else outs[n][k])[None])
            else:
                result.append(outs[n][k])
    return tuple(result)
```

```python
import jax
import jax.numpy as jnp
from jax import lax
from jax.experimental import pallas as pl
from jax.experimental.pallas import tpu as pltpu
from jax.experimental.pallas import tpu_sc as plsc

F32 = jnp.float32
BF16 = jnp.bfloat16
STASH = jnp.bfloat16

D_MODEL = 1024
HG_WIDTH = 512
HEAD_DIM = 128
N_HEADS = 4
HEADS_PER_STEP = 4
HEAD_GROUPS = N_HEADS // HEADS_PER_STEP
CONV_WIDTH = 512
CONV_K = 3
D_FF = 2816
CHUNK = 32
EPS = 1e-6
Q_SCALE = HEAD_DIM ** -0.5
N_DEV = 8

ADAM_LR = 0.001
ADAM_B1 = 0.9
ADAM_B2 = 0.999
ADAM_EPS = 1e-08
ADAM_WD = 0.01
ADAM_STEP = 10

VMEM_LIMIT_V7X = 56 * 1024 * 1024

SMALL_ROWS = 16


def _params(sem, vmem=VMEM_LIMIT_V7X):
    return pltpu.CompilerParams(dimension_semantics=sem, vmem_limit_bytes=vmem)


def _mm(a, b):
    return jnp.dot(a.astype(BF16), b.astype(BF16), preferred_element_type=F32)


def _mm_nt(a, b):
    return lax.dot_general(a.astype(BF16), b.astype(BF16), (((1,), (1,)), ((), ())), preferred_element_type=F32)


def _mm_tn(a, b):
    return lax.dot_general(a.astype(BF16), b.astype(BF16), (((0,), (0,)), ((), ())), preferred_element_type=F32)


def _sigmoid(x):
    return 0.5 * jnp.tanh(0.5 * x) + 0.5


def _resident(shape):
    nd = len(shape)
    return pl.BlockSpec(shape, lambda *_: (0,) * nd, pipeline_mode=pl.Buffered(1))


def _full(shape):
    nd = len(shape)
    return pl.BlockSpec(shape, lambda *_: (0,) * nd)


def _shard_cols(w_ref):
    return jnp.concatenate([w_ref[s] for s in range(N_DEV)], axis=1)


N_HG = 4 * HG_WIDTH
N_CV = 3 * CONV_WIDTH
N_GT = 2 * D_MODEL
N_IN = N_HG + N_CV + N_GT


def _col(tm, n):
    return pl.BlockSpec((n, tm), lambda i: (0, i))


HALO = 8


def _fwd_in(x, g, w_in_t, conv_w):
    T = x.shape[0]
    tm = min(512, T)

    def body(x_ref, g_ref, w_ref, cw_ref, ht_ref, hg_ref, cv_ref, gt_ref, cvo_ref, cvot_ref, tail_scr):
        @pl.when(pl.program_id(0) == 0)
        def _():
            tail_scr[...] = jnp.zeros_like(tail_scr)

        xv = x_ref[...]
        r = lax.rsqrt(jnp.mean(xv * xv, axis=-1, keepdims=True) + EPS)
        hf = xv * r * g_ref[...]
        h = hf.astype(BF16)
        ht_ref[...] = hf.T.astype(BF16)
        hg_ref[...] = _mm_nt(h, w_ref[:N_HG, :])
        cv = _mm_nt(h, w_ref[N_HG:N_HG + N_CV, :])
        cv_ref[...] = cv.astype(STASH)
        gt_ref[...] = _mm_nt(h, w_ref[N_HG + N_CV:, :]).astype(STASH)

        u = cv[:, :CONV_WIDTH] * cv[:, 2 * CONV_WIDTH:]
        row = lax.broadcasted_iota(jnp.int32, u.shape, 0)
        prev1 = tail_scr[HALO - 1:HALO, :]
        prev2 = tail_scr[HALO - 2:HALO - 1, :]
        u1 = jnp.where(row >= 1, pltpu.roll(u, 1, 0), prev1)
        u2 = jnp.where(row >= 2, pltpu.roll(u, 2, 0), jnp.where(row == 1, prev1, prev2))
        y = cw_ref[0:1, :] * u2 + cw_ref[1:2, :] * u1 + cw_ref[2:3, :] * u
        out = cv[:, CONV_WIDTH:2 * CONV_WIDTH] * y
        cvo_ref[...] = out.astype(BF16)
        cvot_ref[...] = out.T.astype(BF16)
        tail_scr[...] = u[tm - HALO:, :]

    row = lambda n: pl.BlockSpec((tm, n), lambda i: (i, 0))
    return pl.pallas_call(
        body, name="fwd_in", grid=(T // tm,),
        in_specs=[row(D_MODEL), _full((1, D_MODEL)), _resident(w_in_t.shape), _full((CONV_K, CONV_WIDTH))],
        out_specs=[_col(tm, D_MODEL), row(N_HG), row(N_CV), row(N_GT), row(CONV_WIDTH), _col(tm, CONV_WIDTH)],
        out_shape=[jax.ShapeDtypeStruct((D_MODEL, T), BF16), jax.ShapeDtypeStruct((T, N_HG), F32),
                   jax.ShapeDtypeStruct((T, N_CV), STASH), jax.ShapeDtypeStruct((T, N_GT), STASH),
                   jax.ShapeDtypeStruct((T, CONV_WIDTH), BF16), jax.ShapeDtypeStruct((CONV_WIDTH, T), BF16)],
        scratch_shapes=[pltpu.VMEM((HALO, CONV_WIDTH), F32)],
        compiler_params=_params(("arbitrary",)),
    )(x, g, w_in_t, conv_w)


def _chunk_pos(shape):
    return lax.broadcasted_iota(jnp.int32, shape, 0) & (CHUNK - 1)


def _chunk_cumsum(x, pos):
    s = 1
    while s < CHUNK:
        x = x + jnp.where(pos >= s, pltpu.roll(x, s, 0), 0.0)
        s *= 2
    return x


def _chunk_rev_cumsum(x, pos):
    n = x.shape[0]
    s = 1
    while s < CHUNK:
        x = x + jnp.where(pos + s < CHUNK, pltpu.roll(x, n - s, 0), 0.0)
        s *= 2
    return x


def _chunk_bcast(x3, row, tb):
    return jnp.broadcast_to(x3[:, row:row + 1, :], x3.shape).reshape(tb, x3.shape[-1])


def _lower_bound(low_ref):
    l0 = low_ref[0:1, :]
    l1 = low_ref[1:2, :]
    m = jnp.maximum(l0, l1)
    e0 = jnp.exp(l0 - m)
    e1 = jnp.exp(l1 - m)
    return e0 / (e0 + e1), e1 / (e0 + e1)


def _hg_gates(qr, fr, lb, pos, tb):
    sq = _sigmoid(qr)
    q = qr * sq * Q_SCALE
    sg = _sigmoid(fr)
    f = lb + (1.0 - lb) * sg
    k = 1.0 - f
    b = _chunk_cumsum(jnp.log(f), pos)
    b3 = b.reshape(tb // CHUNK, CHUNK, HEAD_DIM)
    anc = _chunk_bcast(b3, CHUNK // 2 - 1, tb)
    blb = _chunk_bcast(b3, CHUNK - 1, tb)
    e_qa = jnp.exp(b - anc)
    e_ka = jnp.exp(anc - b)
    e_b = jnp.exp(b)
    e_ko = jnp.exp(blb - b)
    dec = jnp.exp(blb)
    return sq, q, sg, f, k, e_qa, e_ka, e_b, e_ko, dec


def _intra_mask(sb):
    r = lax.broadcasted_iota(jnp.int32, (sb, sb), 0)
    c = lax.broadcasted_iota(jnp.int32, (sb, sb), 1)
    return ((r // CHUNK) == (c // CHUNK)) & (c <= r)


def _hg_fwd(hg, low, gn):
    T = hg.shape[0]
    tb = min(1024, T)
    sb = min(256, tb)
    nb = T // tb
    nc = tb // CHUNK
    wid = HEADS_PER_STEP * HEAD_DIM

    def body(q_ref, f_ref, i_ref, g_ref, low_ref, gn_ref, o_ref, og_ref, ogt_ref, st_ref, s_scr):
        t = pl.program_id(1)

        @pl.when(t == 0)
        def _():
            s_scr[...] = jnp.zeros_like(s_scr)

        pos = _chunk_pos((tb, HEAD_DIM))
        mask = _intra_mask(sb)
        lanes = [slice(hh * HEAD_DIM, (hh + 1) * HEAD_DIM) for hh in range(HEADS_PER_STEP)]
        qi, ko, vb, dec, st = [], [], [], [], []
        for hh, ln in enumerate(lanes):
            lb, _ = _lower_bound(low_ref.at[:, ln])
            _, q, _, _, k, e_qa, e_ka, e_b, e_ko, dec_h = _hg_gates(q_ref[:, ln], f_ref[:, ln], lb, pos, tb)
            qh = (q * e_qa).astype(BF16)
            kh = (k * e_ka).astype(BF16)
            qi.append((q * e_b).astype(BF16))
            ko.append((k * e_ko).astype(BF16))
            vb.append(i_ref[:, ln].astype(BF16))
            dec.append(dec_h)
            st.append(s_scr[hh])
            for s in range(tb // sb):
                sl = slice(s * sb, (s + 1) * sb)
                p = jnp.where(mask, _mm_nt(qh[sl], kh[sl]), 0.0)
                o_ref[sl, ln] = _mm(p, vb[hh][sl])
        for c in range(nc):
            sl = slice(c * CHUNK, (c + 1) * CHUNK)
            for hh, ln in enumerate(lanes):
                st_ref[hh, c] = st[hh]
                o_ref[sl, ln] = o_ref[sl, ln] + _mm_nt(qi[hh][sl], st[hh])
                st[hh] = dec[hh][c * CHUNK:c * CHUNK + 1, :] * st[hh] + _mm_tn(vb[hh][sl], ko[hh][sl])
        for hh, ln in enumerate(lanes):
            s_scr[hh] = st[hh]
            o = o_ref[:, ln]
            r = lax.rsqrt(jnp.mean(o * o, axis=-1, keepdims=True) + EPS)
            gr = g_ref[:, ln]
            og = (o * r * gn_ref[...]) * (gr * _sigmoid(gr))
            og_ref[:, ln] = og.astype(BF16)
            ogt_ref[ln, :] = og.T.astype(BF16)

    col = lambda p: pl.BlockSpec((tb, wid), lambda h, t: (t, p * HEAD_GROUPS + h))
    hcol = pl.BlockSpec((tb, wid), lambda h, t: (t, h))
    return pl.pallas_call(
        body, name="hg_fwd", grid=(HEAD_GROUPS, nb),
        in_specs=[col(0), col(1), col(2), col(3), pl.BlockSpec((2, wid), lambda h, t: (0, h)),
                  pl.BlockSpec((1, HEAD_DIM), lambda h, t: (0, 0))],
        out_specs=[hcol, hcol, pl.BlockSpec((wid, tb), lambda h, t: (h, t)),
                   pl.BlockSpec((HEADS_PER_STEP, nc, HEAD_DIM, HEAD_DIM), lambda h, t: (h, t, 0, 0))],
        out_shape=[jax.ShapeDtypeStruct((T, HG_WIDTH), F32), jax.ShapeDtypeStruct((T, HG_WIDTH), BF16),
                   jax.ShapeDtypeStruct((HG_WIDTH, T), BF16),
                   jax.ShapeDtypeStruct((N_HEADS, T // CHUNK, HEAD_DIM, HEAD_DIM), F32)],
        scratch_shapes=[pltpu.VMEM((HEADS_PER_STEP, HEAD_DIM, HEAD_DIM), F32)],
        compiler_params=_params(("parallel", "arbitrary")),
    )(hg, hg, hg, hg, low, gn)


def _merge_fwd(og, cvo, gt, x, wa, wb, wo):
    T = x.shape[0]
    tm = min(1024, T)

    def body(og_ref, cvo_ref, gt_ref, x_ref, wa_ref, wb_ref, wo_ref, x1_ref, mgt_ref):
        ya = jnp.dot(og_ref[...], _shard_cols(wa_ref), preferred_element_type=F32)
        yb = jnp.dot(cvo_ref[...], _shard_cols(wb_ref), preferred_element_type=F32)
        m = (_sigmoid(gt_ref[:, :D_MODEL].astype(F32)) * ya
             + _sigmoid(gt_ref[:, D_MODEL:].astype(F32)) * yb)
        mgt_ref[...] = m.T.astype(BF16)
        x1_ref[...] = x_ref[...] + jnp.dot(m.astype(BF16), wo_ref[...], preferred_element_type=F32)

    row = lambda n: pl.BlockSpec((tm, n), lambda i: (i, 0))
    return pl.pallas_call(
        body, name="merge_fwd", grid=(T // tm,),
        in_specs=[row(HG_WIDTH), row(CONV_WIDTH), row(2 * D_MODEL), row(D_MODEL),
                  _resident(wa.shape), _resident(wb.shape), _resident(wo.shape)],
        out_specs=[row(D_MODEL), _col(tm, D_MODEL)],
        out_shape=[jax.ShapeDtypeStruct((T, D_MODEL), F32), jax.ShapeDtypeStruct((D_MODEL, T), BF16)],
        compiler_params=_params(("parallel",)),
    )(og, cvo, gt, x, wa, wb, wo)


def _ffn_fwd_loss(x1, g, wg, wu, wd, target, g_fin):
    T = x1.shape[0]
    tm = min(256, T)

    def body(x_ref, g_ref, wg_ref, wu_ref, wd_ref, t_ref, gf_ref,
             ht_ref, gate_ref, up_ref, act_ref, loss_ref, dgf_ref, dx2_ref, dx2t_ref):
        @pl.when(pl.program_id(0) == 0)
        def _():
            loss_ref[...] = jnp.zeros_like(loss_ref)
            dgf_ref[...] = jnp.zeros_like(dgf_ref)

        xv = x_ref[...]
        r = lax.rsqrt(jnp.mean(xv * xv, axis=-1, keepdims=True) + EPS)
        hf = xv * r * g_ref[...]
        h = hf.astype(BF16)
        ht_ref[...] = hf.T.astype(BF16)
        gate = _mm_nt(h, wg_ref[...])
        up = _mm_nt(h, wu_ref[...])
        gate_ref[...] = gate.astype(STASH)
        up_ref[...] = up.astype(STASH)
        act = (gate * _sigmoid(gate) * up).astype(BF16)
        act_ref[...] = act
        x2 = xv + jnp.dot(act, wd_ref[...], preferred_element_type=F32)

        gv = gf_ref[...]
        r2 = lax.rsqrt(jnp.mean(x2 * x2, axis=-1, keepdims=True) + EPS)
        xh = x2 * r2
        err = xh * gv - t_ref[...]
        loss_ref[...] += 0.5 * jnp.sum(jnp.mean(err * err, axis=-1, keepdims=True), axis=0, keepdims=True)
        dy = err * (1.0 / D_MODEL)
        dgf_ref[...] += jnp.sum(dy * xh, axis=0, keepdims=True)
        w = dy * gv
        dx2 = r2 * (w - xh * jnp.mean(w * xh, axis=-1, keepdims=True))
        dx2_ref[...] = dx2
        dx2t_ref[...] = dx2.T.astype(BF16)

    row = lambda n: pl.BlockSpec((tm, n), lambda i: (i, 0))
    return pl.pallas_call(
        body, name="ffn_fwd_loss", grid=(T // tm,),
        in_specs=[row(D_MODEL), _full((1, D_MODEL)), _resident(wg.shape), _resident(wu.shape), _resident(wd.shape),
                  row(D_MODEL), _full((1, D_MODEL))],
        out_specs=[_col(tm, D_MODEL), row(D_FF), row(D_FF), row(D_FF), _full((1, 128)), _full((1, D_MODEL)),
                   row(D_MODEL), _col(tm, D_MODEL)],
        out_shape=[jax.ShapeDtypeStruct((D_MODEL, T), BF16), jax.ShapeDtypeStruct((T, D_FF), STASH),
                   jax.ShapeDtypeStruct((T, D_FF), STASH), jax.ShapeDtypeStruct((T, D_FF), BF16),
                   jax.ShapeDtypeStruct((1, 128), F32), jax.ShapeDtypeStruct((1, D_MODEL), F32),
                   jax.ShapeDtypeStruct((T, D_MODEL), F32), jax.ShapeDtypeStruct((D_MODEL, T), BF16)],
        compiler_params=_params(("arbitrary",)),
    )(x1, g, wg, wu, wd, target, g_fin)


def _ffn_bwd(dx2, x1, gate, up, g, wg, wu, wd):
    T = x1.shape[0]
    tm = min(256, T)

    def body(dx2_ref, x_ref, gate_ref, up_ref, g_ref, wg_ref, wu_ref, wd_ref, dgate_ref, dup_ref, dx1_ref, dgn_ref):
        @pl.when(pl.program_id(0) == 0)
        def _():
            dgn_ref[...] = jnp.zeros_like(dgn_ref)

        dx2 = dx2_ref[...]
        dact = _mm_nt(dx2, wd_ref[...])
        gate = gate_ref[...].astype(F32)
        s = _sigmoid(gate)
        dgate = (dact * up_ref[...].astype(F32) * (s * (1.0 + gate * (1.0 - s)))).astype(BF16)
        dup = (dact * (gate * s)).astype(BF16)
        dgate_ref[...] = dgate
        dup_ref[...] = dup
        dh = _mm(dgate, wg_ref[...]) + _mm(dup, wu_ref[...])
        xv = x_ref[...]
        r = lax.rsqrt(jnp.mean(xv * xv, axis=-1, keepdims=True) + EPS)
        xh = xv * r
        dgn_ref[...] += jnp.sum(dh * xh, axis=0, keepdims=True)
        w = dh * g_ref[...]
        dx1_ref[...] = dx2 + r * (w - xh * jnp.mean(w * xh, axis=-1, keepdims=True))

    row = lambda n: pl.BlockSpec((tm, n), lambda i: (i, 0))
    return pl.pallas_call(
        body, name="ffn_bwd", grid=(T // tm,),
        in_specs=[row(D_MODEL), row(D_MODEL), row(D_FF), row(D_FF), _full((1, D_MODEL)),
                  _resident(wg.shape), _resident(wu.shape), _resident(wd.shape)],
        out_specs=[row(D_FF), row(D_FF), row(D_MODEL), _full((1, D_MODEL))],
        out_shape=[jax.ShapeDtypeStruct((T, D_FF), BF16), jax.ShapeDtypeStruct((T, D_FF), BF16),
                   jax.ShapeDtypeStruct((T, D_MODEL), F32), jax.ShapeDtypeStruct((1, D_MODEL), F32)],
        compiler_params=_params(("arbitrary",)),
    )(dx2, x1, gate, up, g, wg, wu, wd)


def _merge_bwd(dx1, og, cvo, gt, cv, wa, wb, wo, conv_w):
    T = dx1.shape[0]
    tm = min(512, T)
    nt = T // tm

    def body(dx_ref, og_ref, cvo_ref, gt_ref, cv_ref, halo_ref, wa_ref, wb_ref, wo_ref, cw_ref,
             dgt_ref, dya_ref, dyb_ref, dog_ref, dcv_ref, dcw_ref, prev_u, next_dy):
        step = pl.program_id(0)

        @pl.when(step == 0)
        def _():
            next_dy[...] = jnp.zeros_like(next_dy)
            dcw_ref[...] = jnp.zeros_like(dcw_ref)

        dm = _mm_nt(dx_ref[...], wo_ref[...])
        wa = _shard_cols(wa_ref)
        wb = _shard_cols(wb_ref)
        ya = jnp.dot(og_ref[...], wa, preferred_element_type=F32)
        yb = jnp.dot(cvo_ref[...], wb, preferred_element_type=F32)
        sa = _sigmoid(gt_ref[:, :D_MODEL].astype(F32))
        sb = _sigmoid(gt_ref[:, D_MODEL:].astype(F32))
        dgt_ref[:, :D_MODEL] = (dm * ya * (sa * (1.0 - sa))).astype(BF16)
        dgt_ref[:, D_MODEL:] = (dm * yb * (sb * (1.0 - sb))).astype(BF16)
        dya = (dm * sa).astype(BF16)
        dyb = (dm * sb).astype(BF16)
        dya_ref[...] = dya
        dyb_ref[...] = dyb
        dog_ref[...] = _mm_nt(dya, wa)
        dcvo = _mm_nt(dyb, wb)

        cvt = cv_ref[...].astype(F32)
        c, bg, xb = cvt[:, :CONV_WIDTH], cvt[:, CONV_WIDTH:2 * CONV_WIDTH], cvt[:, 2 * CONV_WIDTH:]
        halo = halo_ref[...].astype(F32)
        first_tile = step == nt - 1
        prev_u[...] = jnp.where(first_tile, 0.0, halo[:, :CONV_WIDTH] * halo[:, 2 * CONV_WIDTH:])
        u = c * xb
        row = lax.broadcasted_iota(jnp.int32, u.shape, 0)
        p1 = prev_u[HALO - 1:HALO, :]
        p2 = prev_u[HALO - 2:HALO - 1, :]
        u1 = jnp.where(row >= 1, pltpu.roll(u, 1, 0), p1)
        u2 = jnp.where(row >= 2, pltpu.roll(u, 2, 0), jnp.where(row == 1, p1, p2))
        w0, w1, w2 = cw_ref[0:1, :], cw_ref[1:2, :], cw_ref[2:3, :]
        y = w0 * u2 + w1 * u1 + w2 * u
        dcv_ref[:, CONV_WIDTH:2 * CONV_WIDTH] = (dcvo * y).astype(BF16)
        dy = dcvo * bg
        dcw_ref[0:1, :] += jnp.sum(dy * u2, axis=0, keepdims=True)
        dcw_ref[1:2, :] += jnp.sum(dy * u1, axis=0, keepdims=True)
        dcw_ref[2:3, :] += jnp.sum(dy * u, axis=0, keepdims=True)
        n1 = next_dy[0:1, :]
        n2 = next_dy[1:2, :]
        dy1 = jnp.where(row < tm - 1, pltpu.roll(dy, tm - 1, 0), n1)
        dy2 = jnp.where(row < tm - 2, pltpu.roll(dy, tm - 2, 0), jnp.where(row == tm - 2, n1, n2))
        du = w2 * dy + w1 * dy1 + w0 * dy2
        dcv_ref[:, :CONV_WIDTH] = (du * xb).astype(BF16)
        dcv_ref[:, 2 * CONV_WIDTH:] = (du * c).astype(BF16)
        next_dy[...] = dy[:HALO, :]

    rt = lambda i: nt - 1 - i
    row = lambda n: pl.BlockSpec((tm, n), lambda i: (rt(i), 0))
    halo = pl.BlockSpec((HALO, N_CV), lambda i: (jnp.maximum(rt(i) * (tm // HALO) - 1, 0), 0))
    return pl.pallas_call(
        body, name="merge_bwd", grid=(nt,),
        in_specs=[row(D_MODEL), row(HG_WIDTH), row(CONV_WIDTH), row(2 * D_MODEL), row(N_CV), halo,
                  _resident(wa.shape), _resident(wb.shape), _resident(wo.shape), _full((CONV_K, CONV_WIDTH))],
        out_specs=[row(2 * D_MODEL), row(D_MODEL), row(D_MODEL), row(HG_WIDTH), row(N_CV),
                   _full((CONV_K, CONV_WIDTH))],
        out_shape=[jax.ShapeDtypeStruct((T, 2 * D_MODEL), BF16), jax.ShapeDtypeStruct((T, D_MODEL), BF16),
                   jax.ShapeDtypeStruct((T, D_MODEL), BF16), jax.ShapeDtypeStruct((T, HG_WIDTH), F32),
                   jax.ShapeDtypeStruct((T, N_CV), BF16), jax.ShapeDtypeStruct((CONV_K, CONV_WIDTH), F32)],
        scratch_shapes=[pltpu.VMEM((HALO, CONV_WIDTH), F32), pltpu.VMEM((HALO, CONV_WIDTH), F32)],
        compiler_params=_params(("arbitrary",)),
    )(dx1, og, cvo, gt, cv, cv, wa, wb, wo, conv_w)


def _drop_operands(body, first, count):
    def wrapped(*refs):
        return body(*refs[:first], *refs[first + count:])
    return wrapped


def _hg_bwd(dog, hg, o, st, low, gn, after=()):
    T = hg.shape[0]
    tb = min(512, T)
    sb = min(256, tb)
    nb = T // tb
    nc = tb // CHUNK
    wid = HEADS_PER_STEP * HEAD_DIM

    def body(q_ref, f_ref, i_ref, g_ref, low_ref, gn_ref, o_ref, dog_ref, st_ref,
             dhg_ref, dlow_ref, dgn_ref,
             ds_scr, dqi_scr, dko_scr, dv_scr, dd_scr, dqh_scr, dkh_scr):
        h = pl.program_id(0)
        t = pl.program_id(1)
        dq_ref, df_ref, di_ref, dg_ref = (dhg_ref.at[:, p * HG_WIDTH:(p + 1) * HG_WIDTH] for p in range(4))

        @pl.when(t == 0)
        def _():
            ds_scr[...] = jnp.zeros_like(ds_scr)
            dlow_ref[...] = jnp.zeros_like(dlow_ref)

        @pl.when((t == 0) & (h == 0))
        def _():
            dgn_ref[...] = jnp.zeros_like(dgn_ref)

        pos = _chunk_pos((tb, HEAD_DIM))
        mask = _intra_mask(sb)
        gnv = gn_ref[...]
        lanes = [slice(hh * HEAD_DIM, (hh + 1) * HEAD_DIM) for hh in range(HEADS_PER_STEP)]
        heads = []
        for hh, ln in enumerate(lanes):
            lb, lb1 = _lower_bound(low_ref.at[:, ln])
            qr = q_ref[:, ln]
            sq, q, sg, f, k, e_qa, e_ka, e_b, e_ko, dec = _hg_gates(qr, f_ref[:, ln], lb, pos, tb)

            gr = g_ref[:, ln]
            o = o_ref[:, ln]
            dog_v = dog_ref[:, ln]
            sgr = _sigmoid(gr)
            r = lax.rsqrt(jnp.mean(o * o, axis=-1, keepdims=True) + EPS)
            oh = o * r
            dg_ref[:, ln] = (dog_v * (oh * gnv) * (sgr * (1.0 + gr * (1.0 - sgr)))).astype(BF16)
            don = dog_v * (gr * sgr)
            dgn_ref[...] += jnp.sum(don * oh, axis=0, keepdims=True)
            w = don * gnv
            do = (r * (w - oh * jnp.mean(w * oh, axis=-1, keepdims=True))).astype(BF16)

            qh = (q * e_qa).astype(BF16)
            kh = (k * e_ka).astype(BF16)
            qi = (q * e_b).astype(BF16)
            ko = (k * e_ko).astype(BF16)
            vb = i_ref[:, ln].astype(BF16)

            for s in range(tb // sb):
                sl = slice(s * sb, (s + 1) * sb)
                p = jnp.where(mask, _mm_nt(qh[sl], kh[sl]), 0.0).astype(BF16)
                dp = jnp.where(mask, _mm_nt(do[sl], vb[sl]), 0.0).astype(BF16)
                dv_scr[sl, ln] = _mm_tn(p, do[sl])
                dqh_scr[sl, ln] = _mm(dp, kh[sl])
                dkh_scr[sl, ln] = _mm_tn(dp, qh[sl])
            heads.append(dict(lb=lb, lb1=lb1, qr=qr, sq=sq, q=q, sg=sg, f=f, k=k, e_qa=e_qa, e_ka=e_ka, e_b=e_b,
                              e_ko=e_ko, dec=dec, do=do, qi=qi, ko=ko, vb=vb, ds=ds_scr[hh]))

        for c in reversed(range(nc)):
            sl = slice(c * CHUNK, (c + 1) * CHUNK)
            for hh, ln in enumerate(lanes):
                hd = heads[hh]
                ds = hd["ds"]
                st_c = st_ref[hh, c]
                dqi_scr[sl, ln] = _mm(hd["do"][sl], st_c)
                dko_scr[sl, ln] = _mm(hd["vb"][sl], ds)
                dv_scr[sl, ln] = dv_scr[sl, ln] + _mm_nt(hd["ko"][sl], ds)
                dd_scr[sl, ln] = jnp.broadcast_to(jnp.sum(ds * st_c, axis=0, keepdims=True), (CHUNK, HEAD_DIM))
                hd["ds"] = hd["dec"][c * CHUNK:c * CHUNK + 1, :] * ds + _mm_tn(hd["do"][sl], hd["qi"][sl])

        for hh, ln in enumerate(lanes):
            hd = heads[hh]
            ds_scr[hh] = hd["ds"]
            q, k, lb = hd["q"], hd["k"], hd["lb"]
            dko_e = dko_scr[:, ln] * hd["e_ko"]
            dq = dqh_scr[:, ln] * hd["e_qa"] + dqi_scr[:, ln] * hd["e_b"]
            dk = dkh_scr[:, ln] * hd["e_ka"] + dko_e
            kd3 = (k * dko_e).reshape(nc, CHUNK, HEAD_DIM)
            last = jnp.broadcast_to(jnp.sum(kd3, axis=1, keepdims=True), kd3.shape).reshape(tb, HEAD_DIM)
            db = q * dq - k * dk + jnp.where(pos == CHUNK - 1, hd["dec"] * dd_scr[:, ln] + last, 0.0)
            dlg = _chunk_rev_cumsum(db, pos)
            dfv = dlg / hd["f"] - dk
            s_low = jnp.sum(dfv * (1.0 - hd["sg"]), axis=0, keepdims=True)
            dlow_ref[0:1, ln] += s_low * lb * (1.0 - lb)
            dlow_ref[1:2, ln] += -s_low * lb * hd["lb1"]
            df_ref[:, ln] = (dfv * (1.0 - lb) * hd["sg"] * (1.0 - hd["sg"])).astype(BF16)
            dq_ref[:, ln] = (dq * Q_SCALE * (hd["sq"] * (1.0 + hd["qr"] * (1.0 - hd["sq"])))).astype(BF16)
            di_ref[:, ln] = dv_scr[:, ln].astype(BF16)

    rt = lambda t: nb - 1 - t
    col = lambda p: pl.BlockSpec((tb, wid), lambda h, t: (rt(t), p * HEAD_GROUPS + h))
    hcol = pl.BlockSpec((tb, wid), lambda h, t: (rt(t), h))
    assert HEAD_GROUPS == 1
    tile = pltpu.VMEM((tb, wid), F32)
    return pl.pallas_call(
        _drop_operands(body, 9, len(after)), name="hg_bwd", grid=(HEAD_GROUPS, nb),
        in_specs=[col(0), col(1), col(2), col(3), pl.BlockSpec((2, wid), lambda h, t: (0, h)),
                  pl.BlockSpec((1, HEAD_DIM), lambda h, t: (0, 0)), hcol, hcol,
                  pl.BlockSpec((HEADS_PER_STEP, nc, HEAD_DIM, HEAD_DIM), lambda h, t: (h, rt(t), 0, 0))]
                 + [HBM_SPEC] * len(after),
        out_specs=[pl.BlockSpec((tb, N_HG), lambda h, t: (rt(t), 0)), pl.BlockSpec((2, wid), lambda h, t: (0, h)),
                   pl.BlockSpec((1, HEAD_DIM), lambda h, t: (0, 0))],
        out_shape=[jax.ShapeDtypeStruct((T, N_HG), BF16), jax.ShapeDtypeStruct((2, HG_WIDTH), F32),
                   jax.ShapeDtypeStruct((1, HEAD_DIM), F32)],
        scratch_shapes=[pltpu.VMEM((HEADS_PER_STEP, HEAD_DIM, HEAD_DIM), F32), tile, tile, tile, tile, tile, tile],
        compiler_params=_params(("arbitrary", "arbitrary")),
    )(hg, hg, hg, hg, low, gn, o, dog, st, *after)


def _in_bwd(dparts, w_in, x, dx1, g, after=()):
    T = x.shape[0]
    tm = min(512, T)
    widths = [p.shape[1] for p in dparts]
    offs = [sum(widths[:i]) for i in range(len(widths))]
    n = len(dparts)

    def body(*refs):
        d_refs = refs[:n]
        w_ref, x_ref, dx1_ref, g_ref, dx_ref, dgn_ref = refs[n:]

        @pl.when(pl.program_id(0) == 0)
        def _():
            dgn_ref[...] = jnp.zeros_like(dgn_ref)

        dh = None
        for d_ref, off, wd in zip(d_refs, offs, widths):
            part = _mm(d_ref[...], w_ref[off:off + wd, :])
            dh = part if dh is None else dh + part
        xv = x_ref[...]
        r = lax.rsqrt(jnp.mean(xv * xv, axis=-1, keepdims=True) + EPS)
        xh = xv * r
        dgn_ref[...] += jnp.sum(dh * xh, axis=0, keepdims=True)
        w = dh * g_ref[...]
        dx_ref[...] = dx1_ref[...] + r * (w - xh * jnp.mean(w * xh, axis=-1, keepdims=True))

    row = lambda m: pl.BlockSpec((tm, m), lambda i: (i, 0))
    return pl.pallas_call(
        _drop_operands(body, n + 4, len(after)), name="in_bwd", grid=(T // tm,),
        in_specs=[row(wd) for wd in widths] + [_resident(w_in.shape), row(D_MODEL), row(D_MODEL), _full((1, D_MODEL))]
                 + [HBM_SPEC] * len(after),
        out_specs=[row(D_MODEL), _full((1, D_MODEL))],
        out_shape=[jax.ShapeDtypeStruct((T, D_MODEL), F32), jax.ShapeDtypeStruct((1, D_MODEL), F32)],
        compiler_params=_params(("arbitrary",)),
    )(*dparts, w_in, x, dx1, g, *after)


def _wgrad(name, at, b, tn, transposed=False, tk=2048, after=()):
    M, T = at.shape
    N = b.shape[1]
    tk = min(tk, T)
    nk = T // tk
    if transposed:
        out_spec, out_shape = pl.BlockSpec((tn, M), lambda j, k: (j, 0)), (N, M)
    else:
        out_spec, out_shape = pl.BlockSpec((M, tn), lambda j, k: (0, j)), (M, N)

    if nk == 1:
        def body1(a_ref, b_ref, o_ref):
            part = _mm(a_ref[...], b_ref[...])
            o_ref[...] = (part.T if transposed else part).astype(BF16)

        return pl.pallas_call(
            _drop_operands(body1, 2, len(after)), name=name, grid=(N // tn, 1),
            in_specs=[_resident((M, T)), pl.BlockSpec((T, tn), lambda j, k: (0, j))] + [HBM_SPEC] * len(after),
            out_specs=out_spec, out_shape=jax.ShapeDtypeStruct(out_shape, BF16),
            compiler_params=_params(("parallel", "arbitrary")),
        )(at, b, *after)

    def body(a_ref, b_ref, o_ref, acc):
        k = pl.program_id(1)

        @pl.when(k == 0)
        def _():
            acc[...] = jnp.zeros_like(acc)

        acc[...] += _mm(a_ref[...], b_ref[...])

        @pl.when(k == nk - 1)
        def _():
            o_ref[...] = (acc[...].T if transposed else acc[...]).astype(BF16)

    return pl.pallas_call(
        _drop_operands(body, 2, len(after)), name=name, grid=(N // tn, nk),
        in_specs=[pl.BlockSpec((M, tk), lambda j, k: (0, k)), pl.BlockSpec((tk, tn), lambda j, k: (k, j))]
                 + [HBM_SPEC] * len(after),
        out_specs=out_spec, out_shape=jax.ShapeDtypeStruct(out_shape, BF16),
        scratch_shapes=[pltpu.VMEM((M, tn), F32)],
        compiler_params=_params(("parallel", "arbitrary")),
    )(at, b, *after)


def _wgrad_branches(ogt, dya, cvot, dyb):
    M, T = ogt.shape
    N = dya.shape[1]
    c = N // N_DEV
    per = 2
    tn = per * c

    def body(at_ref, da_ref, bt_ref, db_ref, oa_ref, ob_ref):
        ga = _mm(at_ref[...], da_ref[...])
        gb = _mm(bt_ref[...], db_ref[...])
        for s in range(per):
            oa_ref[s] = ga[:, s * c:(s + 1) * c].astype(BF16)
            ob_ref[s] = gb[:, s * c:(s + 1) * c].astype(BF16)

    rhs = pl.BlockSpec((T, tn), lambda j: (0, j))
    owners = pl.BlockSpec((per, M, c), lambda j: (j, 0, 0))
    out = jax.ShapeDtypeStruct((N_DEV, M, c), BF16)
    return pl.pallas_call(
        body, name="wgrad_branches", grid=(N // tn,),
        in_specs=[_resident((M, T)), rhs, _resident((M, T)), rhs], out_specs=[owners] * 2, out_shape=[out, out],
        compiler_params=_params(("parallel",)),
    )(ogt, dya, cvot, dyb)


def _wgrad_in(ht, dparts, after=()):
    M, T = ht.shape
    tn = 512
    nblk = [p.shape[1] // tn for p in dparts]
    start = [sum(nblk[:i]) for i in range(len(nblk))]
    n = len(dparts)

    def body(a_ref, *refs):
        d_refs, o_ref = refs[:n], refs[n]
        j = pl.program_id(0)
        for d_ref, s, nb in zip(d_refs, start, nblk):
            @pl.when((j >= s) & (j < s + nb))
            def _():
                o_ref[...] = _mm(a_ref[...], d_ref[...]).T.astype(BF16)

    def piece_spec(s, nb):
        return pl.BlockSpec((T, tn), lambda j: (0, jnp.clip(j - s, 0, nb - 1)))

    return pl.pallas_call(
        _drop_operands(body, 1 + n, len(after)), name="wgrad_in", grid=(sum(nblk),),
        in_specs=[_resident((M, T))] + [piece_spec(s, nb) for s, nb in zip(start, nblk)] + [HBM_SPEC] * len(after),
        out_specs=pl.BlockSpec((tn, M), lambda j: (j, 0)),
        out_shape=jax.ShapeDtypeStruct((sum(nblk) * tn, M), BF16),
        compiler_params=_params(("parallel",)),
    )(ht, *dparts, *after)


def _adamw_math(w, g, m, v):
    m = ADAM_B1 * m + (1.0 - ADAM_B1) * g
    v = ADAM_B2 * v + (1.0 - ADAM_B2) * (g * g)
    m_hat = m / (1.0 - ADAM_B1 ** ADAM_STEP)
    v_hat = v / (1.0 - ADAM_B2 ** ADAM_STEP)
    delta = -ADAM_LR * (m_hat / (jnp.sqrt(v_hat) + ADAM_EPS) + ADAM_WD * w)
    return delta, m, v


def _adamw_sum(name, w, parts, m, v):
    R, C = w.shape
    tr = _row_tile(R)

    def body(w_ref, p_ref, m_ref, v_ref, g_out, d_out, m_out, v_out):
        g = p_ref[0].astype(F32)
        for k in range(1, 4):
            g = g + p_ref[k].astype(F32)
        g_out[...] = g
        d_out[...], m_out[...], v_out[...] = _adamw_math(w_ref[...], g, m_ref[...], v_ref[...])

    blk = pl.BlockSpec((tr, C), lambda i: (i, 0))
    out = jax.ShapeDtypeStruct((R, C), F32)
    return pl.pallas_call(
        body, name=name, grid=(R // tr,),
        in_specs=[blk, pl.BlockSpec((4, tr, C), lambda i: (0, i, 0)), blk, blk],
        out_specs=[blk, blk, blk, blk], out_shape=[out, out, out, out],
        compiler_params=_params(("parallel",)),
    )(w, parts, m, v)


_SMALL_SLOTS = (("norm_mix_g", 0, 1, 1024), ("norm_ffn_g", 1, 1, 1024), ("norm_final_g", 2, 1, 1024),
                ("lower_bounds", 3, 2, 512), ("hg_norm_g", 5, 1, 128), ("loss", 6, 1, 128), ("conv_w", 8, 3, 512))
_SMALL_PARAMS = tuple(s for s in _SMALL_SLOTS if s[0] != "loss")
CONV_SHARD = CONV_WIDTH // N_DEV


def _small_pack(small):
    def body(*refs):
        out = refs[-1]
        out[...] = jnp.zeros_like(out)
        for ref, (_, row, rows, lanes) in zip(refs[:-1], _SMALL_SLOTS):
            out[row:row + rows, 0:lanes] = ref[...]

    vmem = pl.BlockSpec(memory_space=pltpu.VMEM)
    return pl.pallas_call(
        body, name="small_pack", in_specs=[vmem] * len(_SMALL_SLOTS), out_specs=vmem,
        out_shape=jax.ShapeDtypeStruct((SMALL_ROWS, 1024), F32),
    )(*[small[name] for name, _, _, _ in _SMALL_SLOTS])


def _small_update(gathered, dev, w, m, v):
    n = len(_SMALL_PARAMS)

    def body(dev_ref, g_ref, *refs):
        w_refs, m_refs, v_refs = refs[:n], refs[n:2 * n], refs[2 * n:3 * n]
        loss_ref, out_refs, sum_scr = refs[3 * n], refs[3 * n + 1:-1], refs[-1]
        total = g_ref[0]
        for k in range(1, N_DEV):
            total = total + g_ref[k]
        sum_scr[...] = total
        loss_ref[...] = sum_scr[6:7, 0:128]
        for p, (name, row, rows, lanes) in enumerate(_SMALL_PARAMS):
            if name == "conv_w":
                g = sum_scr[row:row + rows, 0:CONV_SHARD]
                for s in range(1, N_DEV):
                    g = jnp.where(dev_ref[0] == s, sum_scr[row:row + rows, s * CONV_SHARD:(s + 1) * CONV_SHARD], g)
            else:
                g = sum_scr[row:row + rows, 0:lanes]
            delta, m_new, v_new = _adamw_math(w_refs[p][...], g, m_refs[p][...], v_refs[p][...])
            out_refs[4 * p][...] = g
            out_refs[4 * p + 1][...] = delta
            out_refs[4 * p + 2][...] = m_new
            out_refs[4 * p + 3][...] = v_new

    vmem = pl.BlockSpec(memory_space=pltpu.VMEM)
    outs = [jax.ShapeDtypeStruct((1, 128), F32)]
    for a in w:
        outs += [jax.ShapeDtypeStruct(a.shape, F32)] * 4
    return pl.pallas_call(
        body, name="small_update",
        in_specs=[pl.BlockSpec(memory_space=pltpu.SMEM)] + [vmem] * (1 + 3 * n), out_specs=[vmem] * len(outs),
        out_shape=outs, scratch_shapes=[pltpu.VMEM((SMALL_ROWS, 1024), F32)],
    )(dev, gathered, *w, *m, *v)


def _row_tile(rows):
    for parts in (4, 2):
        if rows % (16 * parts) == 0:
            return rows // parts
    return rows


def _pair_sum(name, by_owner, got, core, after=()):
    n = len(got)

    def body(core_ref, *refs):
        for a_ref, b_ref, o_ref in zip(refs[:n], refs[n:2 * n], refs[2 * n:]):
            o_ref[...] = (a_ref[...].astype(F32) + b_ref[...].astype(F32)).astype(BF16)

    def blk(g):
        return pl.BlockSpec((None,) + g.shape[1:], lambda k, core_ref: (k, 0, 0))

    def mine(g):
        return pl.BlockSpec((None,) + g.shape[1:], lambda k, core_ref: (2 * k + core_ref[0], 0, 0))

    return pl.pallas_call(
        _drop_operands(body, 1 + 2 * n, len(after)), name=name,
        grid_spec=pltpu.PrefetchScalarGridSpec(
            num_scalar_prefetch=1, grid=(4,),
            in_specs=[mine(g) for g in got] + [blk(g) for g in got] + [HBM_SPEC] * len(after),
            out_specs=[blk(g) for g in got]),
        out_shape=[jax.ShapeDtypeStruct(g.shape, BF16) for g in got],
        compiler_params=_params(("parallel",)),
    )(core, *by_owner, *got, *after)


MESH = pl.DeviceIdType.MESH
HBM_SPEC = pl.BlockSpec(memory_space=pl.ANY)


def _handshake(peers):
    barrier = pltpu.get_barrier_semaphore()
    for peer in peers:
        pl.semaphore_signal(barrier, inc=1, device_id=peer, device_id_type=MESH)
    pl.semaphore_wait(barrier, len(peers))


def _comm_call(body, name, operands, out_shape, scratch, collective_id):
    if collective_id is None:
        return pl.pallas_call(body, name=name, in_specs=[HBM_SPEC] * len(operands), out_specs=[HBM_SPEC] * len(out_shape),
                              out_shape=out_shape, scratch_shapes=scratch)(*operands)
    return pl.kernel(body, out_type=out_shape, mesh=plsc.ScalarSubcoreMesh(axis_name="sequencer", num_cores=1),
                     scratch_types=scratch, name=name,
                     compiler_params=pltpu.CompilerParams(collective_id=collective_id))(*operands)


def _all_gather(name, blocks, collective_id=None, after=()):
    n = len(blocks)
    na = len(after)

    def body(*refs):
        x_refs, out_refs = refs[:n], refs[n + na:2 * n + na]
        send_sems, recv_sems, local_sems = refs[2 * n + na:]
        x, y, c = lax.axis_index("x"), lax.axis_index("y"), lax.axis_index("c")
        me, sibling = (x, y, c), (x, y, 1 - c)
        chips = [(1 - x, y), (x, 1 - y), (1 - x, 1 - y)]
        if collective_id is not None:
            _handshake([sibling] + [(*chip, c) for chip in chips])

        def slot(i, px, py, pc):
            return out_refs[i].at[4 * px + 2 * py + pc]

        def copy(i, k, blk, to, src=None):
            return pltpu.make_async_remote_copy(
                src_ref=slot(i, *blk) if src is None else src, dst_ref=slot(i, *blk),
                send_sem=send_sems.at[7 * i + k], recv_sem=recv_sems.at[7 * i + k], device_id=to, device_id_type=MESH)

        mine = [pltpu.make_async_copy(x_refs[i], slot(i, *me), local_sems.at[i]) for i in range(n)]
        for cp in mine:
            cp.start()
        first = []
        for i in range(n):
            first.append(copy(i, 0, me, sibling, src=x_refs[i]))
            first += [copy(i, 1 + j, me, (*chip, c), src=x_refs[i]) for j, chip in enumerate(chips)]
        for cp in first:
            cp.start()
        passed = []
        for i in range(n):
            for j, chip in enumerate(chips):
                copy(i, 1 + j, (*chip, c), me).wait_recv()
                passed.append(copy(i, 4 + j, (*chip, c), sibling))
                passed[-1].start()
        for i in range(n):
            copy(i, 0, sibling, me).wait_recv()
            for j, chip in enumerate(chips):
                copy(i, 4 + j, (*chip, 1 - c), me).wait_recv()
        for cp in first + passed:
            cp.wait_send()
        for cp in mine:
            cp.wait()

    return _comm_call(
        body, name, list(blocks) + list(after), [jax.ShapeDtypeStruct((N_DEV,) + b.shape, b.dtype) for b in blocks],
        [pltpu.SemaphoreType.DMA((7 * n,)), pltpu.SemaphoreType.DMA((7 * n,)), pltpu.SemaphoreType.DMA((n,))],
        collective_id)


def _sibling_swap(name, by_owner, collective_id=None, after=()):
    n = len(by_owner)
    na = len(after)

    def body(*refs):
        x_refs, out_refs = refs[:n], refs[n + na:2 * n + na]
        send_sems, recv_sems = refs[2 * n + na:]
        x, y, c = lax.axis_index("x"), lax.axis_index("y"), lax.axis_index("c")
        if collective_id is not None:
            _handshake([(x, y, 1 - c)])
        copies = []
        for i in range(n):
            for k in range(4):
                copies.append(pltpu.make_async_remote_copy(
                    src_ref=x_refs[i].at[2 * k + 1 - c], dst_ref=out_refs[i].at[k],
                    send_sem=send_sems.at[4 * i + k], recv_sem=recv_sems.at[4 * i + k],
                    device_id=(x, y, 1 - c), device_id_type=MESH))
        for cp in copies:
            cp.start()
        for cp in copies:
            cp.wait()

    return _comm_call(
        body, name, list(by_owner) + list(after),
        [jax.ShapeDtypeStruct((4,) + b.shape[1:], b.dtype) for b in by_owner],
        [pltpu.SemaphoreType.DMA((4 * n,)), pltpu.SemaphoreType.DMA((4 * n,))], collective_id)


def _chip_exchange(name, sums, collective_id=None, after=()):
    n = len(sums)
    na = len(after)

    def body(*refs):
        x_refs, out_refs = refs[:n], refs[n + na:2 * n + na]
        send_sems, recv_sems, local_sems = refs[2 * n + na:]
        x, y, c = lax.axis_index("x"), lax.axis_index("y"), lax.axis_index("c")
        chips = [(1 - x, y), (x, 1 - y), (1 - x, 1 - y)]
        my_chip = 2 * x + y
        if collective_id is not None:
            _handshake([(cx, cy, c) for cx, cy in chips])
        mine = [pltpu.make_async_copy(x_refs[i].at[my_chip], out_refs[i].at[my_chip], local_sems.at[i])
                for i in range(n)]
        for cp in mine:
            cp.start()
        sends = []
        for i in range(n):
            for j, (cx, cy) in enumerate(chips):
                sends.append(pltpu.make_async_remote_copy(
                    src_ref=x_refs[i].at[2 * cx + cy], dst_ref=out_refs[i].at[my_chip],
                    send_sem=send_sems.at[3 * i + j], recv_sem=recv_sems.at[3 * i + j],
                    device_id=(cx, cy, c), device_id_type=MESH))
        for cp in sends:
            cp.start()
        for i in range(n):
            for j, (cx, cy) in enumerate(chips):
                pltpu.make_async_remote_copy(
                    src_ref=x_refs[i].at[my_chip], dst_ref=out_refs[i].at[2 * cx + cy],
                    send_sem=send_sems.at[3 * i + j], recv_sem=recv_sems.at[3 * i + j],
                    device_id=(cx, cy, c), device_id_type=MESH).wait_recv()
        for cp in sends:
            cp.wait_send()
        for cp in mine:
            cp.wait()

    return _comm_call(
        body, name, list(sums) + list(after), [jax.ShapeDtypeStruct(s.shape, s.dtype) for s in sums],
        [pltpu.SemaphoreType.DMA((3 * n,)), pltpu.SemaphoreType.DMA((3 * n,)), pltpu.SemaphoreType.DMA((n,))],
        collective_id)


def _cast_shards(shards):
    n = len(shards)

    def body(*refs):
        for i in range(n):
            refs[n + i][...] = refs[i][...].astype(BF16)

    vmem = pl.BlockSpec(memory_space=pltpu.VMEM)
    return pl.pallas_call(
        body, name="cast_shards", in_specs=[vmem] * n, out_specs=[vmem] * n,
        out_shape=[jax.ShapeDtypeStruct(s.shape, BF16) for s in shards],
        compiler_params=pltpu.CompilerParams(vmem_limit_bytes=VMEM_LIMIT_V7X),
    )(*shards)


BIG = ("w_in", "w_branch_a", "w_branch_b", "w_out", "w_ffn_gate", "w_ffn_up", "w_ffn_down")


def _local_step(x, target, gains, low, conv_w, wg8, reduce):
    g_mix, g_hg, g_ffn, g_fin = gains
    w_in = wg8["w_in"].reshape(N_IN, D_MODEL)
    wg = wg8["w_ffn_gate"].reshape(D_FF, D_MODEL)
    wu = wg8["w_ffn_up"].reshape(D_FF, D_MODEL)
    wa, wb = wg8["w_branch_a"], wg8["w_branch_b"]
    wo = wg8["w_out"].reshape(D_MODEL, D_MODEL)
    wd = wg8["w_ffn_down"].reshape(D_FF, D_MODEL)

    ht, hg, cv, gt, cvo, cvot = _fwd_in(x, g_mix, w_in, conv_w)
    o, og, ogt, st = _hg_fwd(hg, low, g_hg)
    x1, mgt = _merge_fwd(og, cvo, gt, x, wa, wb, wo)
    h2t, gate, up, act, loss, d_gfin, dx2, dx2t = _ffn_fwd_loss(x1, g_ffn, wg, wu, wd, target, g_fin)

    dgate, dup, dx1, d_gffn = _ffn_bwd(dx2, x1, gate, up, g_ffn, wg, wu, wd)
    ffn = dict(
        w_ffn_down=_wgrad("wgrad_ffn_down", dx2t, act, 256, transposed=True, tk=4096
                          ).reshape(N_DEV, D_FF // N_DEV, D_MODEL),
        w_ffn_gate=_wgrad("wgrad_ffn_gate", h2t, dgate, 256, transposed=True, tk=4096
                          ).reshape(N_DEV, D_FF // N_DEV, D_MODEL),
        w_ffn_up=_wgrad("wgrad_ffn_up", h2t, dup, 256, transposed=True, tk=4096
                        ).reshape(N_DEV, D_FF // N_DEV, D_MODEL))
    dgt, dya, dyb, dog, dcv, d_conv = _merge_bwd(dx1, og, cvo, gt, cv, wa, wb, wo, conv_w)
    sums_ffn, got_ffn = reduce.begin(ffn, sum_after=[dya])
    parts_ffn, updated_ffn = reduce.finish(ffn, sums_ffn)
    grad_a, grad_b = _wgrad_branches(ogt, dya, cvot, dyb)
    out = dict(
        w_out=_wgrad("wgrad_out", mgt, dx1, 256, tk=4096, after=sums_ffn[:1]
                     ).reshape(N_DEV, D_MODEL // N_DEV, D_MODEL),
        w_branch_a=grad_a, w_branch_b=grad_b)
    dhg, d_low, d_ghg = _hg_bwd(dog, hg, o, st, low, g_hg, after=list(sums_ffn) + [out["w_out"]])
    sums_out, got_out = reduce.begin(out, after=[parts_ffn[0], dhg], sum_after=updated_ffn)
    parts_out, updated_out = reduce.finish(out, sums_out)
    dparts = [dhg, dcv, dgt]
    w_in_grad = dict(w_in=_wgrad_in(ht, dparts, after=sums_out[:1]).reshape(N_DEV, N_IN // N_DEV, D_MODEL))
    sums_in, _ = reduce.begin(w_in_grad, after=parts_out[:1], sum_after=updated_out)
    parts_in, _ = reduce.finish(w_in_grad, sums_in)
    grad_x, d_gmix = _in_bwd(dparts, w_in, x, dx1, g_mix, after=list(parts_out[:1]) + list(sums_in))
    small = dict(norm_mix_g=d_gmix, norm_ffn_g=d_gffn, norm_final_g=d_gfin, lower_bounds=d_low, hg_norm_g=d_ghg,
                 conv_w=d_conv, loss=loss)
    return grad_x, small, parts_in


def _conv_shard_rows(a):
    return jnp.pad(a, ((0, 5), (0, 64)))


def kernel(x, norm_mix_g, w_in, lower_bounds, hg_norm_g, conv_w, w_branch_a, w_branch_b, w_out, norm_ffn_g, w_ffn_gate, w_ffn_up, w_ffn_down, norm_final_g, loss_target, m_norm_mix_g, m_w_in, m_lower_bounds, m_hg_norm_g, m_conv_w, m_w_branch_a, m_w_branch_b, m_w_out, m_norm_ffn_g, m_w_ffn_gate, m_w_ffn_up, m_w_ffn_down, m_norm_final_g, v_norm_mix_g, v_w_in, v_lower_bounds, v_hg_norm_g, v_conv_w, v_w_branch_a, v_w_branch_b, v_w_out, v_norm_ffn_g, v_w_ffn_gate, v_w_ffn_up, v_w_ffn_down, v_norm_final_g):
    cx, cy, cc = lax.axis_index("x"), lax.axis_index("y"), lax.axis_index("c")
    my_dev = 4 * cx + 2 * cy + cc

    def tr(a):
        return a[0].T

    big = dict(w_in=tr(w_in), w_branch_a=w_branch_a[0], w_branch_b=w_branch_b[0], w_out=w_out[0],
               w_ffn_gate=tr(w_ffn_gate), w_ffn_up=tr(w_ffn_up), w_ffn_down=w_ffn_down[0])
    big_m = dict(w_in=tr(m_w_in), w_branch_a=m_w_branch_a[0], w_branch_b=m_w_branch_b[0], w_out=m_w_out[0],
                 w_ffn_gate=tr(m_w_ffn_gate), w_ffn_up=tr(m_w_ffn_up), w_ffn_down=m_w_ffn_down[0])
    big_v = dict(w_in=tr(v_w_in), w_branch_a=v_w_branch_a[0], w_branch_b=v_w_branch_b[0], w_out=v_w_out[0],
                 w_ffn_gate=tr(v_w_ffn_gate), w_ffn_up=tr(v_w_ffn_up), w_ffn_down=v_w_ffn_down[0])
    transposed = ("w_in", "w_ffn_gate", "w_ffn_up")

    shards = dict(zip(BIG, _cast_shards([big[n] for n in BIG])))
    first = _all_gather("gather_w_in", [shards["w_in"], _conv_shard_rows(conv_w[0])])
    ids = iter(range(1, 16))
    mid = _all_gather("gather_mid", [shards[n] for n in BIG[1:4]], collective_id=next(ids), after=first[1:])
    ffn = _all_gather("gather_ffn", [shards[n] for n in BIG[4:]], collective_id=next(ids), after=first[1:])
    wg8 = dict(zip(BIG, [first[0]] + list(mid) + list(ffn)))
    conv_full = first[1][:, :3, :64].transpose(1, 0, 2).reshape(3, CONV_WIDTH)

    core = cc.reshape(1).astype(jnp.int32)
    outs = {}

    class Reduce:
        @staticmethod
        def begin(grads, after=(), sum_after=()):
            names = list(grads)
            by_owner = [grads[n] for n in names]
            got = _sibling_swap("sibling_swap_" + names[0], by_owner, collective_id=next(ids), after=after)
            sums = _pair_sum("pair_sum_" + names[0], by_owner, got, core, after=sum_after)
            return sums, got

        @staticmethod
        def finish(grads, chip_sums, after=()):
            names = list(grads)
            parts = _chip_exchange("chip_exchange_" + names[0], chip_sums, collective_id=next(ids), after=after)
            for n, p in zip(names, parts):
                outs[n] = _adamw_sum("adamw_" + n, big[n], p, big_m[n], big_v[n])
            return parts, [outs[n][1] for n in names]

    gains = (norm_mix_g, hg_norm_g, norm_ffn_g, norm_final_g.reshape(1, D_MODEL))
    grad_x, small, last = _local_step(x[0], loss_target[0], gains, lower_bounds, conv_full, wg8, Reduce)

    small_all = _all_gather("gather_small", [_small_pack(small)], collective_id=next(ids), after=last[:1])

    def small_state(a):
        return [a[0], a[1], a[2].reshape(1, D_MODEL), a[3], a[4], a[5][0]]

    upd = _small_update(
        small_all[0], my_dev.reshape(1).astype(jnp.int32),
        small_state((norm_mix_g, norm_ffn_g, norm_final_g, lower_bounds, hg_norm_g, conv_w)),
        small_state((m_norm_mix_g, m_norm_ffn_g, m_norm_final_g, m_lower_bounds, m_hg_norm_g, m_conv_w)),
        small_state((v_norm_mix_g, v_norm_ffn_g, v_norm_final_g, v_lower_bounds, v_hg_norm_g, v_conv_w)))
    loss = upd[0][0, 0]
    small_shape = dict(norm_final_g=(D_MODEL,), conv_w=(1, 3, CONV_SHARD))
    for p, (name, _, _, _) in enumerate(_SMALL_PARAMS):
        outs[name] = [a.reshape(small_shape.get(name, a.shape)) for a in upd[1 + 4 * p:5 + 4 * p]]

    order = ["norm_mix_g", "w_in", "lower_bounds", "hg_norm_g", "conv_w", "w_branch_a", "w_branch_b", "w_out",
             "norm_ffn_g", "w_ffn_gate", "w_ffn_up", "w_ffn_down", "norm_final_g"]
    result = [loss, grad_x[None]]
    for k in range(4):
        for n in order:
            if n in BIG:
                result.append((outs[n][k].T if n in transposed else outs[n][k])[None])
            else:
                result.append(outs[n][k])
    return tuple(result)
```

```python
import jax
import jax.numpy as jnp
from jax import lax
from jax.experimental import pallas as pl
from jax.experimental.pallas import tpu as pltpu
from jax.experimental.pallas import tpu_sc as plsc

F32 = jnp.float32
BF16 = jnp.bfloat16
STASH = jnp.bfloat16

D_MODEL = 1024
HG_WIDTH = 512
HEAD_DIM = 128
N_HEADS = 4
HEADS_PER_STEP = 4
HEAD_GROUPS = N_HEADS // HEADS_PER_STEP
CONV_WIDTH = 512
CONV_K = 3
D_FF = 2816
CHUNK = 32
EPS = 1e-6
Q_SCALE = HEAD_DIM ** -0.5
N_DEV = 8

ADAM_LR = 0.001
ADAM_B1 = 0.9
ADAM_B2 = 0.999
ADAM_EPS = 1e-08
ADAM_WD = 0.01
ADAM_STEP = 10

VMEM_LIMIT_V7X = 56 * 1024 * 1024

SMALL_ROWS = 16


def _params(sem, vmem=VMEM_LIMIT_V7X):
    return pltpu.CompilerParams(dimension_semantics=sem, vmem_limit_bytes=vmem)


def _mm(a, b):
    return jnp.dot(a.astype(BF16), b.astype(BF16), preferred_element_type=F32)


def _mm_nt(a, b):
    return lax.dot_general(a.astype(BF16), b.astype(BF16), (((1,), (1,)), ((), ())), preferred_element_type=F32)


def _mm_tn(a, b):
    return lax.dot_general(a.astype(BF16), b.astype(BF16), (((0,), (0,)), ((), ())), preferred_element_type=F32)


def _sigmoid(x):
    return 0.5 * jnp.tanh(0.5 * x) + 0.5


def _resident(shape):
    nd = len(shape)
    return pl.BlockSpec(shape, lambda *_: (0,) * nd, pipeline_mode=pl.Buffered(1))


def _full(shape):
    nd = len(shape)
    return pl.BlockSpec(shape, lambda *_: (0,) * nd)


def _shard_cols(w_ref):
    return jnp.concatenate([w_ref[s] for s in range(N_DEV)], axis=1)


N_HG = 4 * HG_WIDTH
N_CV = 3 * CONV_WIDTH
N_GT = 2 * D_MODEL
N_IN = N_HG + N_CV + N_GT


def _col(tm, n):
    return pl.BlockSpec((n, tm), lambda i: (0, i))


HALO = 8


def _fwd_in(x, g, w_in_t, conv_w):
    T = x.shape[0]
    tm = min(512, T)

    def body(x_ref, g_ref, w_ref, cw_ref, ht_ref, hg_ref, cv_ref, gt_ref, cvo_ref, cvot_ref, tail_scr):
        @pl.when(pl.program_id(0) == 0)
        def _():
            tail_scr[...] = jnp.zeros_like(tail_scr)

        xv = x_ref[...]
        r = lax.rsqrt(jnp.mean(xv * xv, axis=-1, keepdims=True) + EPS)
        hf = xv * r * g_ref[...]
        h = hf.astype(BF16)
        ht_ref[...] = hf.T.astype(BF16)
        hg_ref[...] = _mm_nt(h, w_ref[:N_HG, :])
        cv = _mm_nt(h, w_ref[N_HG:N_HG + N_CV, :])
        cv_ref[...] = cv.astype(STASH)
        gt_ref[...] = _mm_nt(h, w_ref[N_HG + N_CV:, :]).astype(STASH)

        u = cv[:, :CONV_WIDTH] * cv[:, 2 * CONV_WIDTH:]
        row = lax.broadcasted_iota(jnp.int32, u.shape, 0)
        prev1 = tail_scr[HALO - 1:HALO, :]
        prev2 = tail_scr[HALO - 2:HALO - 1, :]
        u1 = jnp.where(row >= 1, pltpu.roll(u, 1, 0), prev1)
        u2 = jnp.where(row >= 2, pltpu.roll(u, 2, 0), jnp.where(row == 1, prev1, prev2))
        y = cw_ref[0:1, :] * u2 + cw_ref[1:2, :] * u1 + cw_ref[2:3, :] * u
        out = cv[:, CONV_WIDTH:2 * CONV_WIDTH] * y
        cvo_ref[...] = out.astype(BF16)
        cvot_ref[...] = out.T.astype(BF16)
        tail_scr[...] = u[tm - HALO:, :]

    row = lambda n: pl.BlockSpec((tm, n), lambda i: (i, 0))
    return pl.pallas_call(
        body, name="fwd_in", grid=(T // tm,),
        in_specs=[row(D_MODEL), _full((1, D_MODEL)), _resident(w_in_t.shape), _full((CONV_K, CONV_WIDTH))],
        out_specs=[_col(tm, D_MODEL), row(N_HG), row(N_CV), row(N_GT), row(CONV_WIDTH), _col(tm, CONV_WIDTH)],
        out_shape=[jax.ShapeDtypeStruct((D_MODEL, T), BF16), jax.ShapeDtypeStruct((T, N_HG), F32),
                   jax.ShapeDtypeStruct((T, N_CV), STASH), jax.ShapeDtypeStruct((T, N_GT), STASH),
                   jax.ShapeDtypeStruct((T, CONV_WIDTH), BF16), jax.ShapeDtypeStruct((CONV_WIDTH, T), BF16)],
        scratch_shapes=[pltpu.VMEM((HALO, CONV_WIDTH), F32)],
        compiler_params=_params(("arbitrary",)),
    )(x, g, w_in_t, conv_w)


def _chunk_pos(shape):
    return lax.broadcasted_iota(jnp.int32, shape, 0) & (CHUNK - 1)


def _chunk_cumsum(x, pos):
    s = 1
    while s < CHUNK:
        x = x + jnp.where(pos >= s, pltpu.roll(x, s, 0), 0.0)
        s *= 2
    return x


def _chunk_rev_cumsum(x, pos):
    n = x.shape[0]
    s = 1
    while s < CHUNK:
        x = x + jnp.where(pos + s < CHUNK, pltpu.roll(x, n - s, 0), 0.0)
        s *= 2
    return x


def _lower_bound(low_ref):
    l0 = low_ref[0:1, :]
    l1 = low_ref[1:2, :]
    m = jnp.maximum(l0, l1)
    e0 = jnp.exp(l0 - m)
    e1 = jnp.exp(l1 - m)
    return e0 / (e0 + e1), e1 / (e0 + e1)


def _hg_gates(qr, fr, lb, pos, tb):
    sq = _sigmoid(qr)
    q = qr * sq * Q_SCALE
    sg = _sigmoid(fr)
    f = lb + (1.0 - lb) * sg
    k = 1.0 - f
    b = _chunk_cumsum(jnp.log(f), pos)
    b3 = b.reshape(tb // CHUNK, CHUNK, HEAD_DIM)
    anc = b3[:, CHUNK // 2 - 1:CHUNK // 2, :]
    last = b3[:, CHUNK - 1:CHUNK, :]
    d3 = b3 - anc
    e_qa3 = jnp.exp(d3)
    e_ka3 = jnp.exp(-d3)
    e_b3 = e_qa3 * jnp.exp(anc)
    e_ko3 = e_ka3 * jnp.exp(last - anc)
    dec = jnp.exp(last)
    flat = lambda a: a.reshape(tb, HEAD_DIM)
    return sq, q, sg, f, k, flat(e_qa3), flat(e_ka3), flat(e_b3), flat(e_ko3), dec


def _intra_mask(sb):
    r = lax.broadcasted_iota(jnp.int32, (sb, sb), 0)
    c = lax.broadcasted_iota(jnp.int32, (sb, sb), 1)
    return ((r // CHUNK) == (c // CHUNK)) & (c <= r)


def _hg_fwd(hg, low, gn):
    T = hg.shape[0]
    tb = min(1024, T)
    sb = min(256, tb)
    nb = T // tb
    nc = tb // CHUNK
    wid = HEADS_PER_STEP * HEAD_DIM

    def body(q_ref, f_ref, i_ref, g_ref, low_ref, gn_ref, o_ref, og_ref, ogt_ref, st_ref, s_scr):
        t = pl.program_id(1)

        @pl.when(t == 0)
        def _():
            s_scr[...] = jnp.zeros_like(s_scr)

        pos = _chunk_pos((tb, HEAD_DIM))
        mask = _intra_mask(sb)
        lanes = [slice(hh * HEAD_DIM, (hh + 1) * HEAD_DIM) for hh in range(HEADS_PER_STEP)]
        qi, ko, vb, dec, st = [], [], [], [], []
        for hh, ln in enumerate(lanes):
            lb, _ = _lower_bound(low_ref.at[:, ln])
            _, q, _, _, k, e_qa, e_ka, e_b, e_ko, dec_h = _hg_gates(q_ref[:, ln], f_ref[:, ln], lb, pos, tb)
            qh = (q * e_qa).astype(BF16)
            kh = (k * e_ka).astype(BF16)
            qi.append((q * e_b).astype(BF16))
            ko.append((k * e_ko).astype(BF16))
            vb.append(i_ref[:, ln].astype(BF16))
            dec.append(dec_h)
            st.append(s_scr[hh])
            for s in range(tb // sb):
                sl = slice(s * sb, (s + 1) * sb)
                p = jnp.where(mask, _mm_nt(qh[sl], kh[sl]), 0.0)
                o_ref[sl, ln] = _mm(p, vb[hh][sl])
        for c in range(nc):
            sl = slice(c * CHUNK, (c + 1) * CHUNK)
            for hh, ln in enumerate(lanes):
                st_ref[hh, c] = st[hh]
                o_ref[sl, ln] = o_ref[sl, ln] + _mm_nt(qi[hh][sl], st[hh])
                st[hh] = dec[hh][c] * st[hh] + _mm_tn(vb[hh][sl], ko[hh][sl])
        for hh, ln in enumerate(lanes):
            s_scr[hh] = st[hh]
            o = o_ref[:, ln]
            r = lax.rsqrt(jnp.mean(o * o, axis=-1, keepdims=True) + EPS)
            gr = g_ref[:, ln]
            og = (o * r * gn_ref[...]) * (gr * _sigmoid(gr))
            og_ref[:, ln] = og.astype(BF16)
            ogt_ref[ln, :] = og.T.astype(BF16)

    col = lambda p: pl.BlockSpec((tb, wid), lambda h, t: (t, p * HEAD_GROUPS + h))
    hcol = pl.BlockSpec((tb, wid), lambda h, t: (t, h))
    return pl.pallas_call(
        body, name="hg_fwd", grid=(HEAD_GROUPS, nb),
        in_specs=[col(0), col(1), col(2), col(3), pl.BlockSpec((2, wid), lambda h, t: (0, h)),
                  pl.BlockSpec((1, HEAD_DIM), lambda h, t: (0, 0))],
        out_specs=[hcol, hcol, pl.BlockSpec((wid, tb), lambda h, t: (h, t)),
                   pl.BlockSpec((HEADS_PER_STEP, nc, HEAD_DIM, HEAD_DIM), lambda h, t: (h, t, 0, 0))],
        out_shape=[jax.ShapeDtypeStruct((T, HG_WIDTH), F32), jax.ShapeDtypeStruct((T, HG_WIDTH), BF16),
                   jax.ShapeDtypeStruct((HG_WIDTH, T), BF16),
                   jax.ShapeDtypeStruct((N_HEADS, T // CHUNK, HEAD_DIM, HEAD_DIM), F32)],
        scratch_shapes=[pltpu.VMEM((HEADS_PER_STEP, HEAD_DIM, HEAD_DIM), F32)],
        compiler_params=_params(("parallel", "arbitrary")),
    )(hg, hg, hg, hg, low, gn)


def _merge_fwd(og, cvo, gt, x, wa, wb, wo):
    T = x.shape[0]
    tm = min(1024, T)

    def body(og_ref, cvo_ref, gt_ref, x_ref, wa_ref, wb_ref, wo_ref, x1_ref, mgt_ref):
        ya = jnp.dot(og_ref[...], _shard_cols(wa_ref), preferred_element_type=F32)
        yb = jnp.dot(cvo_ref[...], _shard_cols(wb_ref), preferred_element_type=F32)
        m = (_sigmoid(gt_ref[:, :D_MODEL].astype(F32)) * ya
             + _sigmoid(gt_ref[:, D_MODEL:].astype(F32)) * yb)
        mgt_ref[...] = m.T.astype(BF16)
        x1_ref[...] = x_ref[...] + jnp.dot(m.astype(BF16), wo_ref[...], preferred_element_type=F32)

    row = lambda n: pl.BlockSpec((tm, n), lambda i: (i, 0))
    return pl.pallas_call(
        body, name="merge_fwd", grid=(T // tm,),
        in_specs=[row(HG_WIDTH), row(CONV_WIDTH), row(2 * D_MODEL), row(D_MODEL),
                  _resident(wa.shape), _resident(wb.shape), _resident(wo.shape)],
        out_specs=[row(D_MODEL), _col(tm, D_MODEL)],
        out_shape=[jax.ShapeDtypeStruct((T, D_MODEL), F32), jax.ShapeDtypeStruct((D_MODEL, T), BF16)],
        compiler_params=_params(("parallel",)),
    )(og, cvo, gt, x, wa, wb, wo)


def _ffn_fwd_loss(x1, g, wg, wu, wd, target, g_fin):
    T = x1.shape[0]
    tm = min(256, T)

    def body(x_ref, g_ref, wg_ref, wu_ref, wd_ref, t_ref, gf_ref,
             ht_ref, gate_ref, up_ref, act_ref, loss_ref, dgf_ref, dx2_ref, dx2t_ref):
        @pl.when(pl.program_id(0) == 0)
        def _():
            loss_ref[...] = jnp.zeros_like(loss_ref)
            dgf_ref[...] = jnp.zeros_like(dgf_ref)

        xv = x_ref[...]
        r = lax.rsqrt(jnp.mean(xv * xv, axis=-1, keepdims=True) + EPS)
        hf = xv * r * g_ref[...]
        h = hf.astype(BF16)
        ht_ref[...] = hf.T.astype(BF16)
        gate = _mm_nt(h, wg_ref[...])
        up = _mm_nt(h, wu_ref[...])
        gate_ref[...] = gate.astype(STASH)
        up_ref[...] = up.astype(STASH)
        act = (gate * _sigmoid(gate) * up).astype(BF16)
        act_ref[...] = act
        x2 = xv + jnp.dot(act, wd_ref[...], preferred_element_type=F32)

        gv = gf_ref[...]
        r2 = lax.rsqrt(jnp.mean(x2 * x2, axis=-1, keepdims=True) + EPS)
        xh = x2 * r2
        err = xh * gv - t_ref[...]
        loss_ref[...] += 0.5 * jnp.sum(jnp.mean(err * err, axis=-1, keepdims=True), axis=0, keepdims=True)
        dy = err * (1.0 / D_MODEL)
        dgf_ref[...] += jnp.sum(dy * xh, axis=0, keepdims=True)
        w = dy * gv
        dx2 = r2 * (w - xh * jnp.mean(w * xh, axis=-1, keepdims=True))
        dx2_ref[...] = dx2
        dx2t_ref[...] = dx2.T.astype(BF16)

    row = lambda n: pl.BlockSpec((tm, n), lambda i: (i, 0))
    return pl.pallas_call(
        body, name="ffn_fwd_loss", grid=(T // tm,),
        in_specs=[row(D_MODEL), _full((1, D_MODEL)), _resident(wg.shape), _resident(wu.shape), _resident(wd.shape),
                  row(D_MODEL), _full((1, D_MODEL))],
        out_specs=[_col(tm, D_MODEL), row(D_FF), row(D_FF), row(D_FF), _full((1, 128)), _full((1, D_MODEL)),
                   row(D_MODEL), _col(tm, D_MODEL)],
        out_shape=[jax.ShapeDtypeStruct((D_MODEL, T), BF16), jax.ShapeDtypeStruct((T, D_FF), STASH),
                   jax.ShapeDtypeStruct((T, D_FF), STASH), jax.ShapeDtypeStruct((T, D_FF), BF16),
                   jax.ShapeDtypeStruct((1, 128), F32), jax.ShapeDtypeStruct((1, D_MODEL), F32),
                   jax.ShapeDtypeStruct((T, D_MODEL), F32), jax.ShapeDtypeStruct((D_MODEL, T), BF16)],
        compiler_params=_params(("arbitrary",)),
    )(x1, g, wg, wu, wd, target, g_fin)


def _ffn_bwd(dx2, x1, gate, up, g, wg, wu, wd):
    T = x1.shape[0]
    tm = min(256, T)

    def body(dx2_ref, x_ref, gate_ref, up_ref, g_ref, wg_ref, wu_ref, wd_ref, dgate_ref, dup_ref, dx1_ref, dgn_ref):
        @pl.when(pl.program_id(0) == 0)
        def _():
            dgn_ref[...] = jnp.zeros_like(dgn_ref)

        dx2 = dx2_ref[...]
        dact = _mm_nt(dx2, wd_ref[...])
        gate = gate_ref[...].astype(F32)
        s = _sigmoid(gate)
        dgate = (dact * up_ref[...].astype(F32) * (s * (1.0 + gate * (1.0 - s)))).astype(BF16)
        dup = (dact * (gate * s)).astype(BF16)
        dgate_ref[...] = dgate
        dup_ref[...] = dup
        dh = _mm(dgate, wg_ref[...]) + _mm(dup, wu_ref[...])
        xv = x_ref[...]
        r = lax.rsqrt(jnp.mean(xv * xv, axis=-1, keepdims=True) + EPS)
        xh = xv * r
        dgn_ref[...] += jnp.sum(dh * xh, axis=0, keepdims=True)
        w = dh * g_ref[...]
        dx1_ref[...] = dx2 + r * (w - xh * jnp.mean(w * xh, axis=-1, keepdims=True))

    row = lambda n: pl.BlockSpec((tm, n), lambda i: (i, 0))
    return pl.pallas_call(
        body, name="ffn_bwd", grid=(T // tm,),
        in_specs=[row(D_MODEL), row(D_MODEL), row(D_FF), row(D_FF), _full((1, D_MODEL)),
                  _resident(wg.shape), _resident(wu.shape), _resident(wd.shape)],
        out_specs=[row(D_FF), row(D_FF), row(D_MODEL), _full((1, D_MODEL))],
        out_shape=[jax.ShapeDtypeStruct((T, D_FF), BF16), jax.ShapeDtypeStruct((T, D_FF), BF16),
                   jax.ShapeDtypeStruct((T, D_MODEL), F32), jax.ShapeDtypeStruct((1, D_MODEL), F32)],
        compiler_params=_params(("arbitrary",)),
    )(dx2, x1, gate, up, g, wg, wu, wd)


def _merge_bwd(dx1, og, cvo, gt, cv, wa, wb, wo, conv_w):
    T = dx1.shape[0]
    tm = min(512, T)
    nt = T // tm

    def body(dx_ref, og_ref, cvo_ref, gt_ref, cv_ref, halo_ref, wa_ref, wb_ref, wo_ref, cw_ref,
             dgt_ref, dya_ref, dyb_ref, dog_ref, dcv_ref, dcw_ref, prev_u, next_dy):
        step = pl.program_id(0)

        @pl.when(step == 0)
        def _():
            next_dy[...] = jnp.zeros_like(next_dy)
            dcw_ref[...] = jnp.zeros_like(dcw_ref)

        dm = _mm_nt(dx_ref[...], wo_ref[...])
        wa = _shard_cols(wa_ref)
        wb = _shard_cols(wb_ref)
        ya = jnp.dot(og_ref[...], wa, preferred_element_type=F32)
        yb = jnp.dot(cvo_ref[...], wb, preferred_element_type=F32)
        sa = _sigmoid(gt_ref[:, :D_MODEL].astype(F32))
        sb = _sigmoid(gt_ref[:, D_MODEL:].astype(F32))
        dgt_ref[:, :D_MODEL] = (dm * ya * (sa * (1.0 - sa))).astype(BF16)
        dgt_ref[:, D_MODEL:] = (dm * yb * (sb * (1.0 - sb))).astype(BF16)
        dya = (dm * sa).astype(BF16)
        dyb = (dm * sb).astype(BF16)
        dya_ref[...] = dya
        dyb_ref[...] = dyb
        dog_ref[...] = _mm_nt(dya, wa)
        dcvo = _mm_nt(dyb, wb)

        cvt = cv_ref[...].astype(F32)
        c, bg, xb = cvt[:, :CONV_WIDTH], cvt[:, CONV_WIDTH:2 * CONV_WIDTH], cvt[:, 2 * CONV_WIDTH:]
        halo = halo_ref[...].astype(F32)
        first_tile = step == nt - 1
        prev_u[...] = jnp.where(first_tile, 0.0, halo[:, :CONV_WIDTH] * halo[:, 2 * CONV_WIDTH:])
        u = c * xb
        row = lax.broadcasted_iota(jnp.int32, u.shape, 0)
        p1 = prev_u[HALO - 1:HALO, :]
        p2 = prev_u[HALO - 2:HALO - 1, :]
        u1 = jnp.where(row >= 1, pltpu.roll(u, 1, 0), p1)
        u2 = jnp.where(row >= 2, pltpu.roll(u, 2, 0), jnp.where(row == 1, p1, p2))
        w0, w1, w2 = cw_ref[0:1, :], cw_ref[1:2, :], cw_ref[2:3, :]
        y = w0 * u2 + w1 * u1 + w2 * u
        dcv_ref[:, CONV_WIDTH:2 * CONV_WIDTH] = (dcvo * y).astype(BF16)
        dy = dcvo * bg
        dcw_ref[0:1, :] += jnp.sum(dy * u2, axis=0, keepdims=True)
        dcw_ref[1:2, :] += jnp.sum(dy * u1, axis=0, keepdims=True)
        dcw_ref[2:3, :] += jnp.sum(dy * u, axis=0, keepdims=True)
        n1 = next_dy[0:1, :]
        n2 = next_dy[1:2, :]
        dy1 = jnp.where(row < tm - 1, pltpu.roll(dy, tm - 1, 0), n1)
        dy2 = jnp.where(row < tm - 2, pltpu.roll(dy, tm - 2, 0), jnp.where(row == tm - 2, n1, n2))
        du = w2 * dy + w1 * dy1 + w0 * dy2
        dcv_ref[:, :CONV_WIDTH] = (du * xb).astype(BF16)
        dcv_ref[:, 2 * CONV_WIDTH:] = (du * c).astype(BF16)
        next_dy[...] = dy[:HALO, :]

    rt = lambda i: nt - 1 - i
    row = lambda n: pl.BlockSpec((tm, n), lambda i: (rt(i), 0))
    halo = pl.BlockSpec((HALO, N_CV), lambda i: (jnp.maximum(rt(i) * (tm // HALO) - 1, 0), 0))
    return pl.pallas_call(
        body, name="merge_bwd", grid=(nt,),
        in_specs=[row(D_MODEL), row(HG_WIDTH), row(CONV_WIDTH), row(2 * D_MODEL), row(N_CV), halo,
                  _resident(wa.shape), _resident(wb.shape), _resident(wo.shape), _full((CONV_K, CONV_WIDTH))],
        out_specs=[row(2 * D_MODEL), row(D_MODEL), row(D_MODEL), row(HG_WIDTH), row(N_CV),
                   _full((CONV_K, CONV_WIDTH))],
        out_shape=[jax.ShapeDtypeStruct((T, 2 * D_MODEL), BF16), jax.ShapeDtypeStruct((T, D_MODEL), BF16),
                   jax.ShapeDtypeStruct((T, D_MODEL), BF16), jax.ShapeDtypeStruct((T, HG_WIDTH), F32),
                   jax.ShapeDtypeStruct((T, N_CV), BF16), jax.ShapeDtypeStruct((CONV_K, CONV_WIDTH), F32)],
        scratch_shapes=[pltpu.VMEM((HALO, CONV_WIDTH), F32), pltpu.VMEM((HALO, CONV_WIDTH), F32)],
        compiler_params=_params(("arbitrary",)),
    )(dx1, og, cvo, gt, cv, cv, wa, wb, wo, conv_w)


def _drop_operands(body, first, count):
    def wrapped(*refs):
        return body(*refs[:first], *refs[first + count:])
    return wrapped


def _hg_bwd(dog, hg, o, st, low, gn, after=()):
    T = hg.shape[0]
    tb = min(512, T)
    sb = min(256, tb)
    nb = T // tb
    nc = tb // CHUNK
    wid = HEADS_PER_STEP * HEAD_DIM

    def body(q_ref, f_ref, i_ref, g_ref, low_ref, gn_ref, o_ref, dog_ref, st_ref,
             dhg_ref, dlow_ref, dgn_ref,
             ds_scr, dqi_scr, dko_scr, dv_scr, dd_scr, dqh_scr, dkh_scr):
        h = pl.program_id(0)
        t = pl.program_id(1)
        dq_ref, df_ref, di_ref, dg_ref = (dhg_ref.at[:, p * HG_WIDTH:(p + 1) * HG_WIDTH] for p in range(4))

        @pl.when(t == 0)
        def _():
            ds_scr[...] = jnp.zeros_like(ds_scr)
            dlow_ref[...] = jnp.zeros_like(dlow_ref)

        @pl.when((t == 0) & (h == 0))
        def _():
            dgn_ref[...] = jnp.zeros_like(dgn_ref)

        pos = _chunk_pos((tb, HEAD_DIM))
        mask = _intra_mask(sb)
        gnv = gn_ref[...]
        lanes = [slice(hh * HEAD_DIM, (hh + 1) * HEAD_DIM) for hh in range(HEADS_PER_STEP)]
        heads = []
        for hh, ln in enumerate(lanes):
            lb, lb1 = _lower_bound(low_ref.at[:, ln])
            qr = q_ref[:, ln]
            sq, q, sg, f, k, e_qa, e_ka, e_b, e_ko, dec = _hg_gates(qr, f_ref[:, ln], lb, pos, tb)

            gr = g_ref[:, ln]
            o = o_ref[:, ln]
            dog_v = dog_ref[:, ln]
            sgr = _sigmoid(gr)
            r = lax.rsqrt(jnp.mean(o * o, axis=-1, keepdims=True) + EPS)
            oh = o * r
            dg_ref[:, ln] = (dog_v * (oh * gnv) * (sgr * (1.0 + gr * (1.0 - sgr)))).astype(BF16)
            don = dog_v * (gr * sgr)
            dgn_ref[...] += jnp.sum(don * oh, axis=0, keepdims=True)
            w = don * gnv
            do = (r * (w - oh * jnp.mean(w * oh, axis=-1, keepdims=True))).astype(BF16)

            qh = (q * e_qa).astype(BF16)
            kh = (k * e_ka).astype(BF16)
            qi = (q * e_b).astype(BF16)
            ko = (k * e_ko).astype(BF16)
            vb = i_ref[:, ln].astype(BF16)

            for s in range(tb // sb):
                sl = slice(s * sb, (s + 1) * sb)
                p = jnp.where(mask, _mm_nt(qh[sl], kh[sl]), 0.0).astype(BF16)
                dp = jnp.where(mask, _mm_nt(do[sl], vb[sl]), 0.0).astype(BF16)
                dv_scr[sl, ln] = _mm_tn(p, do[sl])
                dqh_scr[sl, ln] = _mm(dp, kh[sl])
                dkh_scr[sl, ln] = _mm_tn(dp, qh[sl])
            heads.append(dict(lb=lb, lb1=lb1, qr=qr, sq=sq, q=q, sg=sg, f=f, k=k, e_qa=e_qa, e_ka=e_ka, e_b=e_b,
                              e_ko=e_ko, dec=dec, do=do, qi=qi, ko=ko, vb=vb, ds=ds_scr[hh]))

        for c in reversed(range(nc)):
            sl = slice(c * CHUNK, (c + 1) * CHUNK)
            for hh, ln in enumerate(lanes):
                hd = heads[hh]
                ds = hd["ds"]
                st_c = st_ref[hh, c]
                dqi_scr[sl, ln] = _mm(hd["do"][sl], st_c)
                dko_scr[sl, ln] = _mm(hd["vb"][sl], ds)
                dv_scr[sl, ln] = dv_scr[sl, ln] + _mm_nt(hd["ko"][sl], ds)
                dec_c = hd["dec"][c]
                dd_scr[sl, ln] = jnp.broadcast_to(dec_c * jnp.sum(ds * st_c, axis=0, keepdims=True),
                                                  (CHUNK, HEAD_DIM))
                hd["ds"] = dec_c * ds + _mm_tn(hd["do"][sl], hd["qi"][sl])

        for hh, ln in enumerate(lanes):
            hd = heads[hh]
            ds_scr[hh] = hd["ds"]
            q, k, lb = hd["q"], hd["k"], hd["lb"]
            dko_e = dko_scr[:, ln] * hd["e_ko"]
            dq = dqh_scr[:, ln] * hd["e_qa"] + dqi_scr[:, ln] * hd["e_b"]
            dk = dkh_scr[:, ln] * hd["e_ka"] + dko_e
            kd3 = (k * dko_e).reshape(nc, CHUNK, HEAD_DIM)
            last = jnp.broadcast_to(jnp.sum(kd3, axis=1, keepdims=True), kd3.shape).reshape(tb, HEAD_DIM)
            db = q * dq - k * dk + jnp.where(pos == CHUNK - 1, dd_scr[:, ln] + last, 0.0)
            dlg = _chunk_rev_cumsum(db, pos)
            dfv = dlg / hd["f"] - dk
            s_low = jnp.sum(dfv * (1.0 - hd["sg"]), axis=0, keepdims=True)
            dlow_ref[0:1, ln] += s_low * lb * (1.0 - lb)
            dlow_ref[1:2, ln] += -s_low * lb * hd["lb1"]
            df_ref[:, ln] = (dfv * (1.0 - lb) * hd["sg"] * (1.0 - hd["sg"])).astype(BF16)
            dq_ref[:, ln] = (dq * Q_SCALE * (hd["sq"] * (1.0 + hd["qr"] * (1.0 - hd["sq"])))).astype(BF16)
            di_ref[:, ln] = dv_scr[:, ln].astype(BF16)

    rt = lambda t: nb - 1 - t
    col = lambda p: pl.BlockSpec((tb, wid), lambda h, t: (rt(t), p * HEAD_GROUPS + h))
    hcol = pl.BlockSpec((tb, wid), lambda h, t: (rt(t), h))
    assert HEAD_GROUPS == 1
    tile = pltpu.VMEM((tb, wid), F32)
    return pl.pallas_call(
        _drop_operands(body, 9, len(after)), name="hg_bwd", grid=(HEAD_GROUPS, nb),
        in_specs=[col(0), col(1), col(2), col(3), pl.BlockSpec((2, wid), lambda h, t: (0, h)),
                  pl.BlockSpec((1, HEAD_DIM), lambda h, t: (0, 0)), hcol, hcol,
                  pl.BlockSpec((HEADS_PER_STEP, nc, HEAD_DIM, HEAD_DIM), lambda h, t: (h, rt(t), 0, 0))]
                 + [HBM_SPEC] * len(after),
        out_specs=[pl.BlockSpec((tb, N_HG), lambda h, t: (rt(t), 0)), pl.BlockSpec((2, wid), lambda h, t: (0, h)),
                   pl.BlockSpec((1, HEAD_DIM), lambda h, t: (0, 0))],
        out_shape=[jax.ShapeDtypeStruct((T, N_HG), BF16), jax.ShapeDtypeStruct((2, HG_WIDTH), F32),
                   jax.ShapeDtypeStruct((1, HEAD_DIM), F32)],
        scratch_shapes=[pltpu.VMEM((HEADS_PER_STEP, HEAD_DIM, HEAD_DIM), F32), tile, tile, tile, tile, tile, tile],
        compiler_params=_params(("arbitrary", "arbitrary")),
    )(hg, hg, hg, hg, low, gn, o, dog, st, *after)


def _in_bwd(dparts, w_in, x, dx1, g, after=()):
    T = x.shape[0]
    tm = min(512, T)
    widths = [p.shape[1] for p in dparts]
    offs = [sum(widths[:i]) for i in range(len(widths))]
    n = len(dparts)

    def body(*refs):
        d_refs = refs[:n]
        w_ref, x_ref, dx1_ref, g_ref, dx_ref, dgn_ref = refs[n:]

        @pl.when(pl.program_id(0) == 0)
        def _():
            dgn_ref[...] = jnp.zeros_like(dgn_ref)

        dh = None
        for d_ref, off, wd in zip(d_refs, offs, widths):
            part = _mm(d_ref[...], w_ref[off:off + wd, :])
            dh = part if dh is None else dh + part
        xv = x_ref[...]
        r = lax.rsqrt(jnp.mean(xv * xv, axis=-1, keepdims=True) + EPS)
        xh = xv * r
        dgn_ref[...] += jnp.sum(dh * xh, axis=0, keepdims=True)
        w = dh * g_ref[...]
        dx_ref[...] = dx1_ref[...] + r * (w - xh * jnp.mean(w * xh, axis=-1, keepdims=True))

    row = lambda m: pl.BlockSpec((tm, m), lambda i: (i, 0))
    return pl.pallas_call(
        _drop_operands(body, n + 4, len(after)), name="in_bwd", grid=(T // tm,),
        in_specs=[row(wd) for wd in widths] + [_resident(w_in.shape), row(D_MODEL), row(D_MODEL), _full((1, D_MODEL))]
                 + [HBM_SPEC] * len(after),
        out_specs=[row(D_MODEL), _full((1, D_MODEL))],
        out_shape=[jax.ShapeDtypeStruct((T, D_MODEL), F32), jax.ShapeDtypeStruct((1, D_MODEL), F32)],
        compiler_params=_params(("arbitrary",)),
    )(*dparts, w_in, x, dx1, g, *after)


def _wgrad(name, at, b, tn, transposed=False, tk=2048, after=()):
    M, T = at.shape
    N = b.shape[1]
    tk = min(tk, T)
    nk = T // tk
    if transposed:
        out_spec, out_shape = pl.BlockSpec((tn, M), lambda j, k: (j, 0)), (N, M)
    else:
        out_spec, out_shape = pl.BlockSpec((M, tn), lambda j, k: (0, j)), (M, N)

    if nk == 1:
        def body1(a_ref, b_ref, o_ref):
            part = _mm(a_ref[...], b_ref[...])
            o_ref[...] = (part.T if transposed else part).astype(BF16)

        return pl.pallas_call(
            _drop_operands(body1, 2, len(after)), name=name, grid=(N // tn, 1),
            in_specs=[_resident((M, T)), pl.BlockSpec((T, tn), lambda j, k: (0, j))] + [HBM_SPEC] * len(after),
            out_specs=out_spec, out_shape=jax.ShapeDtypeStruct(out_shape, BF16),
            compiler_params=_params(("parallel", "arbitrary")),
        )(at, b, *after)

    def body(a_ref, b_ref, o_ref, acc):
        k = pl.program_id(1)

        @pl.when(k == 0)
        def _():
            acc[...] = jnp.zeros_like(acc)

        acc[...] += _mm(a_ref[...], b_ref[...])

        @pl.when(k == nk - 1)
        def _():
            o_ref[...] = (acc[...].T if transposed else acc[...]).astype(BF16)

    return pl.pallas_call(
        _drop_operands(body, 2, len(after)), name=name, grid=(N // tn, nk),
        in_specs=[pl.BlockSpec((M, tk), lambda j, k: (0, k)), pl.BlockSpec((tk, tn), lambda j, k: (k, j))]
                 + [HBM_SPEC] * len(after),
        out_specs=out_spec, out_shape=jax.ShapeDtypeStruct(out_shape, BF16),
        scratch_shapes=[pltpu.VMEM((M, tn), F32)],
        compiler_params=_params(("parallel", "arbitrary")),
    )(at, b, *after)


def _wgrad_branches(ogt, dya, cvot, dyb):
    M, T = ogt.shape
    N = dya.shape[1]
    c = N // N_DEV
    per = 2
    tn = per * c

    def body(at_ref, da_ref, bt_ref, db_ref, oa_ref, ob_ref):
        ga = _mm(at_ref[...], da_ref[...])
        gb = _mm(bt_ref[...], db_ref[...])
        for s in range(per):
            oa_ref[s] = ga[:, s * c:(s + 1) * c].astype(BF16)
            ob_ref[s] = gb[:, s * c:(s + 1) * c].astype(BF16)

    rhs = pl.BlockSpec((T, tn), lambda j: (0, j))
    owners = pl.BlockSpec((per, M, c), lambda j: (j, 0, 0))
    out = jax.ShapeDtypeStruct((N_DEV, M, c), BF16)
    return pl.pallas_call(
        body, name="wgrad_branches", grid=(N // tn,),
        in_specs=[_resident((M, T)), rhs, _resident((M, T)), rhs], out_specs=[owners] * 2, out_shape=[out, out],
        compiler_params=_params(("parallel",)),
    )(ogt, dya, cvot, dyb)


def _wgrad_in(ht, dparts, after=()):
    M, T = ht.shape
    tn = 512
    nblk = [p.shape[1] // tn for p in dparts]
    start = [sum(nblk[:i]) for i in range(len(nblk))]
    n = len(dparts)

    def body(a_ref, *refs):
        d_refs, o_ref = refs[:n], refs[n]
        j = pl.program_id(0)
        for d_ref, s, nb in zip(d_refs, start, nblk):
            @pl.when((j >= s) & (j < s + nb))
            def _():
                o_ref[...] = _mm(a_ref[...], d_ref[...]).T.astype(BF16)

    def piece_spec(s, nb):
        return pl.BlockSpec((T, tn), lambda j: (0, jnp.clip(j - s, 0, nb - 1)))

    return pl.pallas_call(
        _drop_operands(body, 1 + n, len(after)), name="wgrad_in", grid=(sum(nblk),),
        in_specs=[_resident((M, T))] + [piece_spec(s, nb) for s, nb in zip(start, nblk)] + [HBM_SPEC] * len(after),
        out_specs=pl.BlockSpec((tn, M), lambda j: (j, 0)),
        out_shape=jax.ShapeDtypeStruct((sum(nblk) * tn, M), BF16),
        compiler_params=_params(("parallel",)),
    )(ht, *dparts, *after)


def _adamw_math(w, g, m, v):
    m = ADAM_B1 * m + (1.0 - ADAM_B1) * g
    v = ADAM_B2 * v + (1.0 - ADAM_B2) * (g * g)
    m_hat = m / (1.0 - ADAM_B1 ** ADAM_STEP)
    v_hat = v / (1.0 - ADAM_B2 ** ADAM_STEP)
    delta = -ADAM_LR * (m_hat / (jnp.sqrt(v_hat) + ADAM_EPS) + ADAM_WD * w)
    return delta, m, v


def _adamw_sum(name, w, parts, m, v):
    R, C = w.shape
    tr = _row_tile(R)

    def body(w_ref, p_ref, m_ref, v_ref, g_out, d_out, m_out, v_out):
        g = p_ref[0].astype(F32)
        for k in range(1, 4):
            g = g + p_ref[k].astype(F32)
        g_out[...] = g
        d_out[...], m_out[...], v_out[...] = _adamw_math(w_ref[...], g, m_ref[...], v_ref[...])

    blk = pl.BlockSpec((tr, C), lambda i: (i, 0))
    out = jax.ShapeDtypeStruct((R, C), F32)
    return pl.pallas_call(
        body, name=name, grid=(R // tr,),
        in_specs=[blk, pl.BlockSpec((4, tr, C), lambda i: (0, i, 0)), blk, blk],
        out_specs=[blk, blk, blk, blk], out_shape=[out, out, out, out],
        compiler_params=_params(("parallel",)),
    )(w, parts, m, v)


_SMALL_SLOTS = (("norm_mix_g", 0, 1, 1024), ("norm_ffn_g", 1, 1, 1024), ("norm_final_g", 2, 1, 1024),
                ("lower_bounds", 3, 2, 512), ("hg_norm_g", 5, 1, 128), ("loss", 6, 1, 128), ("conv_w", 8, 3, 512))
_SMALL_PARAMS = tuple(s for s in _SMALL_SLOTS if s[0] != "loss")
CONV_SHARD = CONV_WIDTH // N_DEV


def _small_pack(small):
    def body(*refs):
        out = refs[-1]
        out[...] = jnp.zeros_like(out)
        for ref, (_, row, rows, lanes) in zip(refs[:-1], _SMALL_SLOTS):
            out[row:row + rows, 0:lanes] = ref[...]

    vmem = pl.BlockSpec(memory_space=pltpu.VMEM)
    return pl.pallas_call(
        body, name="small_pack", in_specs=[vmem] * len(_SMALL_SLOTS), out_specs=vmem,
        out_shape=jax.ShapeDtypeStruct((SMALL_ROWS, 1024), F32),
    )(*[small[name] for name, _, _, _ in _SMALL_SLOTS])


def _small_update(gathered, dev, w, m, v):
    n = len(_SMALL_PARAMS)

    def body(dev_ref, g_ref, *refs):
        w_refs, m_refs, v_refs = refs[:n], refs[n:2 * n], refs[2 * n:3 * n]
        loss_ref, out_refs, sum_scr = refs[3 * n], refs[3 * n + 1:-1], refs[-1]
        total = g_ref[0]
        for k in range(1, N_DEV):
            total = total + g_ref[k]
        sum_scr[...] = total
        loss_ref[...] = sum_scr[6:7, 0:128]
        for p, (name, row, rows, lanes) in enumerate(_SMALL_PARAMS):
            if name == "conv_w":
                g = sum_scr[row:row + rows, 0:CONV_SHARD]
                for s in range(1, N_DEV):
                    g = jnp.where(dev_ref[0] == s, sum_scr[row:row + rows, s * CONV_SHARD:(s + 1) * CONV_SHARD], g)
            else:
                g = sum_scr[row:row + rows, 0:lanes]
            delta, m_new, v_new = _adamw_math(w_refs[p][...], g, m_refs[p][...], v_refs[p][...])
            out_refs[4 * p][...] = g
            out_refs[4 * p + 1][...] = delta
            out_refs[4 * p + 2][...] = m_new
            out_refs[4 * p + 3][...] = v_new

    vmem = pl.BlockSpec(memory_space=pltpu.VMEM)
    outs = [jax.ShapeDtypeStruct((1, 128), F32)]
    for a in w:
        outs += [jax.ShapeDtypeStruct(a.shape, F32)] * 4
    return pl.pallas_call(
        body, name="small_update",
        in_specs=[pl.BlockSpec(memory_space=pltpu.SMEM)] + [vmem] * (1 + 3 * n), out_specs=[vmem] * len(outs),
        out_shape=outs, scratch_shapes=[pltpu.VMEM((SMALL_ROWS, 1024), F32)],
    )(dev, gathered, *w, *m, *v)


def _row_tile(rows):
    for parts in (4, 2):
        if rows % (16 * parts) == 0:
            return rows // parts
    return rows


def _pair_sum(name, by_owner, got, core, after=()):
    n = len(got)

    def body(core_ref, *refs):
        for a_ref, b_ref, o_ref in zip(refs[:n], refs[n:2 * n], refs[2 * n:]):
            o_ref[...] = (a_ref[...].astype(F32) + b_ref[...].astype(F32)).astype(BF16)

    def blk(g):
        return pl.BlockSpec((None,) + g.shape[1:], lambda k, core_ref: (k, 0, 0))

    def mine(g):
        return pl.BlockSpec((None,) + g.shape[1:], lambda k, core_ref: (2 * k + core_ref[0], 0, 0))

    return pl.pallas_call(
        _drop_operands(body, 1 + 2 * n, len(after)), name=name,
        grid_spec=pltpu.PrefetchScalarGridSpec(
            num_scalar_prefetch=1, grid=(4,),
            in_specs=[mine(g) for g in got] + [blk(g) for g in got] + [HBM_SPEC] * len(after),
            out_specs=[blk(g) for g in got]),
        out_shape=[jax.ShapeDtypeStruct(g.shape, BF16) for g in got],
        compiler_params=_params(("parallel",)),
    )(core, *by_owner, *got, *after)


MESH = pl.DeviceIdType.MESH
HBM_SPEC = pl.BlockSpec(memory_space=pl.ANY)


def _handshake(peers):
    barrier = pltpu.get_barrier_semaphore()
    for peer in peers:
        pl.semaphore_signal(barrier, inc=1, device_id=peer, device_id_type=MESH)
    pl.semaphore_wait(barrier, len(peers))


def _comm_call(body, name, operands, out_shape, scratch, collective_id):
    if collective_id is None:
        return pl.pallas_call(body, name=name, in_specs=[HBM_SPEC] * len(operands), out_specs=[HBM_SPEC] * len(out_shape),
                              out_shape=out_shape, scratch_shapes=scratch)(*operands)
    return pl.kernel(body, out_type=out_shape, mesh=plsc.ScalarSubcoreMesh(axis_name="sequencer", num_cores=1),
                     scratch_types=scratch, name=name,
                     compiler_params=pltpu.CompilerParams(collective_id=collective_id))(*operands)


def _all_gather(name, blocks, collective_id=None, after=()):
    n = len(blocks)
    na = len(after)

    def body(*refs):
        x_refs, out_refs = refs[:n], refs[n + na:2 * n + na]
        send_sems, recv_sems, local_sems = refs[2 * n + na:]
        x, y, c = lax.axis_index("x"), lax.axis_index("y"), lax.axis_index("c")
        me, sibling = (x, y, c), (x, y, 1 - c)
        chips = [(1 - x, y), (x, 1 - y), (1 - x, 1 - y)]
        if collective_id is not None:
            _handshake([sibling] + [(*chip, c) for chip in chips])

        def slot(i, px, py, pc):
            return out_refs[i].at[4 * px + 2 * py + pc]

        def copy(i, k, blk, to, src=None):
            return pltpu.make_async_remote_copy(
                src_ref=slot(i, *blk) if src is None else src, dst_ref=slot(i, *blk),
                send_sem=send_sems.at[7 * i + k], recv_sem=recv_sems.at[7 * i + k], device_id=to, device_id_type=MESH)

        mine = [pltpu.make_async_copy(x_refs[i], slot(i, *me), local_sems.at[i]) for i in range(n)]
        for cp in mine:
            cp.start()
        first = []
        for i in range(n):
            first.append(copy(i, 0, me, sibling, src=x_refs[i]))
            first += [copy(i, 1 + j, me, (*chip, c), src=x_refs[i]) for j, chip in enumerate(chips)]
        for cp in first:
            cp.start()
        passed = []
        for i in range(n):
            for j, chip in enumerate(chips):
                copy(i, 1 + j, (*chip, c), me).wait_recv()
                passed.append(copy(i, 4 + j, (*chip, c), sibling))
                passed[-1].start()
        for i in range(n):
            copy(i, 0, sibling, me).wait_recv()
            for j, chip in enumerate(chips):
                copy(i, 4 + j, (*chip, 1 - c), me).wait_recv()
        for cp in first + passed:
            cp.wait_send()
        for cp in mine:
            cp.wait()

    return _comm_call(
        body, name, list(blocks) + list(after), [jax.ShapeDtypeStruct((N_DEV,) + b.shape, b.dtype) for b in blocks],
        [pltpu.SemaphoreType.DMA((7 * n,)), pltpu.SemaphoreType.DMA((7 * n,)), pltpu.SemaphoreType.DMA((n,))],
        collective_id)


def _sibling_swap(name, by_owner, collective_id=None, after=()):
    n = len(by_owner)
    na = len(after)

    def body(*refs):
        x_refs, out_refs = refs[:n], refs[n + na:2 * n + na]
        send_sems, recv_sems = refs[2 * n + na:]
        x, y, c = lax.axis_index("x"), lax.axis_index("y"), lax.axis_index("c")
        if collective_id is not None:
            _handshake([(x, y, 1 - c)])
        copies = []
        for i in range(n):
            for k in range(4):
                copies.append(pltpu.make_async_remote_copy(
                    src_ref=x_refs[i].at[2 * k + 1 - c], dst_ref=out_refs[i].at[k],
                    send_sem=send_sems.at[4 * i + k], recv_sem=recv_sems.at[4 * i + k],
                    device_id=(x, y, 1 - c), device_id_type=MESH))
        for cp in copies:
            cp.start()
        for cp in copies:
            cp.wait()

    return _comm_call(
        body, name, list(by_owner) + list(after),
        [jax.ShapeDtypeStruct((4,) + b.shape[1:], b.dtype) for b in by_owner],
        [pltpu.SemaphoreType.DMA((4 * n,)), pltpu.SemaphoreType.DMA((4 * n,))], collective_id)


def _chip_exchange(name, sums, collective_id=None, after=()):
    n = len(sums)
    na = len(after)

    def body(*refs):
        x_refs, out_refs = refs[:n], refs[n + na:2 * n + na]
        send_sems, recv_sems, local_sems = refs[2 * n + na:]
        x, y, c = lax.axis_index("x"), lax.axis_index("y"), lax.axis_index("c")
        chips = [(1 - x, y), (x, 1 - y), (1 - x, 1 - y)]
        my_chip = 2 * x + y
        if collective_id is not None:
            _handshake([(cx, cy, c) for cx, cy in chips])
        mine = [pltpu.make_async_copy(x_refs[i].at[my_chip], out_refs[i].at[my_chip], local_sems.at[i])
                for i in range(n)]
        for cp in mine:
            cp.start()
        sends = []
        for i in range(n):
            for j, (cx, cy) in enumerate(chips):
                sends.append(pltpu.make_async_remote_copy(
                    src_ref=x_refs[i].at[2 * cx + cy], dst_ref=out_refs[i].at[my_chip],
                    send_sem=send_sems.at[3 * i + j], recv_sem=recv_sems.at[3 * i + j],
                    device_id=(cx, cy, c), device_id_type=MESH))
        for cp in sends:
            cp.start()
        for i in range(n):
            for j, (cx, cy) in enumerate(chips):
                pltpu.make_async_remote_copy(
                    src_ref=x_refs[i].at[my_chip], dst_ref=out_refs[i].at[2 * cx + cy],
                    send_sem=send_sems.at[3 * i + j], recv_sem=recv_sems.at[3 * i + j],
                    device_id=(cx, cy, c), device_id_type=MESH).wait_recv()
        for cp in sends:
            cp.wait_send()
        for cp in mine:
            cp.wait()

    return _comm_call(
        body, name, list(sums) + list(after), [jax.ShapeDtypeStruct(s.shape, s.dtype) for s in sums],
        [pltpu.SemaphoreType.DMA((3 * n,)), pltpu.SemaphoreType.DMA((3 * n,)), pltpu.SemaphoreType.DMA((n,))],
        collective_id)


def _cast_shards(shards):
    n = len(shards)

    def body(*refs):
        for i in range(n):
            refs[n + i][...] = refs[i][...].astype(BF16)

    vmem = pl.BlockSpec(memory_space=pltpu.VMEM)
    return pl.pallas_call(
        body, name="cast_shards", in_specs=[vmem] * n, out_specs=[vmem] * n,
        out_shape=[jax.ShapeDtypeStruct(s.shape, BF16) for s in shards],
        compiler_params=pltpu.CompilerParams(vmem_limit_bytes=VMEM_LIMIT_V7X),
    )(*shards)


BIG = ("w_in", "w_branch_a", "w_branch_b", "w_out", "w_ffn_gate", "w_ffn_up", "w_ffn_down")


def _local_step(x, target, gains, low, conv_w, wg8, reduce):
    g_mix, g_hg, g_ffn, g_fin = gains
    w_in = wg8["w_in"].reshape(N_IN, D_MODEL)
    wg = wg8["w_ffn_gate"].reshape(D_FF, D_MODEL)
    wu = wg8["w_ffn_up"].reshape(D_FF, D_MODEL)
    wa, wb = wg8["w_branch_a"], wg8["w_branch_b"]
    wo = wg8["w_out"].reshape(D_MODEL, D_MODEL)
    wd = wg8["w_ffn_down"].reshape(D_FF, D_MODEL)

    ht, hg, cv, gt, cvo, cvot = _fwd_in(x, g_mix, w_in, conv_w)
    o, og, ogt, st = _hg_fwd(hg, low, g_hg)
    x1, mgt = _merge_fwd(og, cvo, gt, x, wa, wb, wo)
    h2t, gate, up, act, loss, d_gfin, dx2, dx2t = _ffn_fwd_loss(x1, g_ffn, wg, wu, wd, target, g_fin)

    dgate, dup, dx1, d_gffn = _ffn_bwd(dx2, x1, gate, up, g_ffn, wg, wu, wd)
    ffn = dict(
        w_ffn_down=_wgrad("wgrad_ffn_down", dx2t, act, 256, transposed=True, tk=4096
                          ).reshape(N_DEV, D_FF // N_DEV, D_MODEL),
        w_ffn_gate=_wgrad("wgrad_ffn_gate", h2t, dgate, 256, transposed=True, tk=4096
                          ).reshape(N_DEV, D_FF // N_DEV, D_MODEL),
        w_ffn_up=_wgrad("wgrad_ffn_up", h2t, dup, 256, transposed=True, tk=4096
                        ).reshape(N_DEV, D_FF // N_DEV, D_MODEL))
    dgt, dya, dyb, dog, dcv, d_conv = _merge_bwd(dx1, og, cvo, gt, cv, wa, wb, wo, conv_w)
    sums_ffn, got_ffn = reduce.begin(ffn, sum_after=[dya])
    parts_ffn, updated_ffn = reduce.finish(ffn, sums_ffn)
    grad_a, grad_b = _wgrad_branches(ogt, dya, cvot, dyb)
    out = dict(
        w_out=_wgrad("wgrad_out", mgt, dx1, 256, tk=4096, after=sums_ffn[:1]
                     ).reshape(N_DEV, D_MODEL // N_DEV, D_MODEL),
        w_branch_a=grad_a, w_branch_b=grad_b)
    dhg, d_low, d_ghg = _hg_bwd(dog, hg, o, st, low, g_hg, after=list(sums_ffn) + [out["w_out"]])
    sums_out, got_out = reduce.begin(out, after=[parts_ffn[0], dhg], sum_after=updated_ffn)
    parts_out, updated_out = reduce.finish(out, sums_out)
    dparts = [dhg, dcv, dgt]
    w_in_grad = dict(w_in=_wgrad_in(ht, dparts, after=sums_out[:1]).reshape(N_DEV, N_IN // N_DEV, D_MODEL))
    sums_in, _ = reduce.begin(w_in_grad, after=parts_out[:1], sum_after=updated_out)
    parts_in, _ = reduce.finish(w_in_grad, sums_in)
    grad_x, d_gmix = _in_bwd(dparts, w_in, x, dx1, g_mix, after=list(parts_out[:1]) + list(sums_in))
    small = dict(norm_mix_g=d_gmix, norm_ffn_g=d_gffn, norm_final_g=d_gfin, lower_bounds=d_low, hg_norm_g=d_ghg,
                 conv_w=d_conv, loss=loss)
    return grad_x, small, parts_in


def _conv_shard_rows(a):
    return jnp.pad(a, ((0, 5), (0, 64)))


def kernel(x, norm_mix_g, w_in, lower_bounds, hg_norm_g, conv_w, w_branch_a, w_branch_b, w_out, norm_ffn_g, w_ffn_gate, w_ffn_up, w_ffn_down, norm_final_g, loss_target, m_norm_mix_g, m_w_in, m_lower_bounds, m_hg_norm_g, m_conv_w, m_w_branch_a, m_w_branch_b, m_w_out, m_norm_ffn_g, m_w_ffn_gate, m_w_ffn_up, m_w_ffn_down, m_norm_final_g, v_norm_mix_g, v_w_in, v_lower_bounds, v_hg_norm_g, v_conv_w, v_w_branch_a, v_w_branch_b, v_w_out, v_norm_ffn_g, v_w_ffn_gate, v_w_ffn_up, v_w_ffn_down, v_norm_final_g):
    cx, cy, cc = lax.axis_index("x"), lax.axis_index("y"), lax.axis_index("c")
    my_dev = 4 * cx + 2 * cy + cc

    def tr(a):
        return a[0].T

    big = dict(w_in=tr(w_in), w_branch_a=w_branch_a[0], w_branch_b=w_branch_b[0], w_out=w_out[0],
               w_ffn_gate=tr(w_ffn_gate), w_ffn_up=tr(w_ffn_up), w_ffn_down=w_ffn_down[0])
    big_m = dict(w_in=tr(m_w_in), w_branch_a=m_w_branch_a[0], w_branch_b=m_w_branch_b[0], w_out=m_w_out[0],
                 w_ffn_gate=tr(m_w_ffn_gate), w_ffn_up=tr(m_w_ffn_up), w_ffn_down=m_w_ffn_down[0])
    big_v = dict(w_in=tr(v_w_in), w_branch_a=v_w_branch_a[0], w_branch_b=v_w_branch_b[0], w_out=v_w_out[0],
                 w_ffn_gate=tr(v_w_ffn_gate), w_ffn_up=tr(v_w_ffn_up), w_ffn_down=v_w_ffn_down[0])
    transposed = ("w_in", "w_ffn_gate", "w_ffn_up")

    shards = dict(zip(BIG, _cast_shards([big[n] for n in BIG])))
    first = _all_gather("gather_w_in", [shards["w_in"], _conv_shard_rows(conv_w[0])])
    ids = iter(range(1, 16))
    mid = _all_gather("gather_mid", [shards[n] for n in BIG[1:4]], collective_id=next(ids), after=first[1:])
    ffn = _all_gather("gather_ffn", [shards[n] for n in BIG[4:]], collective_id=next(ids), after=first[1:])
    wg8 = dict(zip(BIG, [first[0]] + list(mid) + list(ffn)))
    conv_full = first[1][:, :3, :64].transpose(1, 0, 2).reshape(3, CONV_WIDTH)

    core = cc.reshape(1).astype(jnp.int32)
    outs = {}

    class Reduce:
        @staticmethod
        def begin(grads, after=(), sum_after=()):
            names = list(grads)
            by_owner = [grads[n] for n in names]
            got = _sibling_swap("sibling_swap_" + names[0], by_owner, collective_id=next(ids), after=after)
            sums = _pair_sum("pair_sum_" + names[0], by_owner, got, core, after=sum_after)
            return sums, got

        @staticmethod
        def finish(grads, chip_sums, after=()):
            names = list(grads)
            parts = _chip_exchange("chip_exchange_" + names[0], chip_sums, collective_id=next(ids), after=after)
            for n, p in zip(names, parts):
                outs[n] = _adamw_sum("adamw_" + n, big[n], p, big_m[n], big_v[n])
            return parts, [outs[n][1] for n in names]

    gains = (norm_mix_g, hg_norm_g, norm_ffn_g, norm_final_g.reshape(1, D_MODEL))
    grad_x, small, last = _local_step(x[0], loss_target[0], gains, lower_bounds, conv_full, wg8, Reduce)

    small_all = _all_gather("gather_small", [_small_pack(small)], collective_id=next(ids), after=last[:1])

    def small_state(a):
        return [a[0], a[1], a[2].reshape(1, D_MODEL), a[3], a[4], a[5][0]]

    upd = _small_update(
        small_all[0], my_dev.reshape(1).astype(jnp.int32),
        small_state((norm_mix_g, norm_ffn_g, norm_final_g, lower_bounds, hg_norm_g, conv_w)),
        small_state((m_norm_mix_g, m_norm_ffn_g, m_norm_final_g, m_lower_bounds, m_hg_norm_g, m_conv_w)),
        small_state((v_norm_mix_g, v_norm_ffn_g, v_norm_final_g, v_lower_bounds, v_hg_norm_g, v_conv_w)))
    loss = upd[0][0, 0]
    small_shape = dict(norm_final_g=(D_MODEL,), conv_w=(1, 3, CONV_SHARD))
    for p, (name, _, _, _) in enumerate(_SMALL_PARAMS):
        outs[name] = [a.reshape(small_shape.get(name, a.shape)) for a in upd[1 + 4 * p:5 + 4 * p]]

    order = ["norm_mix_g", "w_in", "lower_bounds", "hg_norm_g", "conv_w", "w_branch_a", "w_branch_b", "w_out",
             "norm_ffn_g", "w_ffn_gate", "w_ffn_up", "w_ffn_down", "norm_final_g"]
    result = [loss, grad_x[None]]
    for k in range(4):
        for n in order:
            if n in BIG:
                result.append((outs[n][k].T if n in transposed else outs[n][k])[None])
            else:
                result.append(outs[n][k])
    return tuple(result)
```

```python
import jax
import jax.numpy as jnp
from jax import lax
from jax.experimental import pallas as pl
from jax.experimental.pallas import tpu as pltpu
from jax.experimental.pallas import tpu_sc as plsc

F32 = jnp.float32
BF16 = jnp.bfloat16
STASH = jnp.bfloat16

D_MODEL = 1024
HG_WIDTH = 512
HEAD_DIM = 128
N_HEADS = 4
HEADS_PER_STEP = 4
HEAD_GROUPS = N_HEADS // HEADS_PER_STEP
CONV_WIDTH = 512
CONV_K = 3
D_FF = 2816
CHUNK = 32
EPS = 1e-6
Q_SCALE = HEAD_DIM ** -0.5
N_DEV = 8

ADAM_LR = 0.001
ADAM_B1 = 0.9
ADAM_B2 = 0.999
ADAM_EPS = 1e-08
ADAM_WD = 0.01
ADAM_STEP = 10

VMEM_LIMIT_V7X = 56 * 1024 * 1024

SMALL_ROWS = 16


def _params(sem, vmem=VMEM_LIMIT_V7X):
    return pltpu.CompilerParams(dimension_semantics=sem, vmem_limit_bytes=vmem)


def _mm(a, b):
    return jnp.dot(a.astype(BF16), b.astype(BF16), preferred_element_type=F32)


def _mm_nt(a, b):
    return lax.dot_general(a.astype(BF16), b.astype(BF16), (((1,), (1,)), ((), ())), preferred_element_type=F32)


def _mm_tn(a, b):
    return lax.dot_general(a.astype(BF16), b.astype(BF16), (((0,), (0,)), ((), ())), preferred_element_type=F32)


def _sigmoid(x):
    return 0.5 * jnp.tanh(0.5 * x) + 0.5


def _resident(shape):
    nd = len(shape)
    return pl.BlockSpec(shape, lambda *_: (0,) * nd, pipeline_mode=pl.Buffered(1))


def _full(shape):
    nd = len(shape)
    return pl.BlockSpec(shape, lambda *_: (0,) * nd)


def _shard_cols(w_ref):
    return jnp.concatenate([w_ref[s] for s in range(N_DEV)], axis=1)


N_HG = 4 * HG_WIDTH
N_CV = 3 * CONV_WIDTH
N_GT = 2 * D_MODEL
N_IN = N_HG + N_CV + N_GT


def _col(tm, n):
    return pl.BlockSpec((n, tm), lambda i: (0, i))


HALO = 8


def _fwd_in(x, g, w_in_t, conv_w):
    T = x.shape[0]
    tm = min(512, T)

    def body(x_ref, g_ref, w_ref, cw_ref, ht_ref, hg_ref, cv_ref, gt_ref, cvo_ref, cvot_ref, tail_scr):
        @pl.when(pl.program_id(0) == 0)
        def _():
            tail_scr[...] = jnp.zeros_like(tail_scr)

        xv = x_ref[...]
        r = lax.rsqrt(jnp.mean(xv * xv, axis=-1, keepdims=True) + EPS)
        hf = xv * r * g_ref[...]
        h = hf.astype(BF16)
        ht_ref[...] = hf.T.astype(BF16)
        hg_ref[...] = _mm_nt(h, w_ref[:N_HG, :])
        cv = _mm_nt(h, w_ref[N_HG:N_HG + N_CV, :])
        cv_ref[...] = cv.astype(STASH)
        gt_ref[...] = _mm_nt(h, w_ref[N_HG + N_CV:, :]).astype(STASH)

        u = cv[:, :CONV_WIDTH] * cv[:, 2 * CONV_WIDTH:]
        row = lax.broadcasted_iota(jnp.int32, u.shape, 0)
        prev1 = tail_scr[HALO - 1:HALO, :]
        prev2 = tail_scr[HALO - 2:HALO - 1, :]
        u1 = jnp.where(row >= 1, pltpu.roll(u, 1, 0), prev1)
        u2 = jnp.where(row >= 2, pltpu.roll(u, 2, 0), jnp.where(row == 1, prev1, prev2))
        y = cw_ref[0:1, :] * u2 + cw_ref[1:2, :] * u1 + cw_ref[2:3, :] * u
        out = cv[:, CONV_WIDTH:2 * CONV_WIDTH] * y
        cvo_ref[...] = out.astype(BF16)
        cvot_ref[...] = out.T.astype(BF16)
        tail_scr[...] = u[tm - HALO:, :]

    row = lambda n: pl.BlockSpec((tm, n), lambda i: (i, 0))
    return pl.pallas_call(
        body, name="fwd_in", grid=(T // tm,),
        in_specs=[row(D_MODEL), _full((1, D_MODEL)), _resident(w_in_t.shape), _full((CONV_K, CONV_WIDTH))],
        out_specs=[_col(tm, D_MODEL), row(N_HG), row(N_CV), row(N_GT), row(CONV_WIDTH), _col(tm, CONV_WIDTH)],
        out_shape=[jax.ShapeDtypeStruct((D_MODEL, T), BF16), jax.ShapeDtypeStruct((T, N_HG), F32),
                   jax.ShapeDtypeStruct((T, N_CV), STASH), jax.ShapeDtypeStruct((T, N_GT), STASH),
                   jax.ShapeDtypeStruct((T, CONV_WIDTH), BF16), jax.ShapeDtypeStruct((CONV_WIDTH, T), BF16)],
        scratch_shapes=[pltpu.VMEM((HALO, CONV_WIDTH), F32)],
        compiler_params=_params(("arbitrary",)),
    )(x, g, w_in_t, conv_w)


def _chunk_pos(shape):
    return lax.broadcasted_iota(jnp.int32, shape, 0) & (CHUNK - 1)


def _chunk_cumsum(x, pos):
    s = 1
    while s < CHUNK:
        x = x + jnp.where(pos >= s, pltpu.roll(x, s, 0), 0.0)
        s *= 2
    return x


def _chunk_rev_cumsum(x, pos):
    n = x.shape[0]
    s = 1
    while s < CHUNK:
        x = x + jnp.where(pos + s < CHUNK, pltpu.roll(x, n - s, 0), 0.0)
        s *= 2
    return x


def _lower_bound(low_ref):
    l0 = low_ref[0:1, :]
    l1 = low_ref[1:2, :]
    m = jnp.maximum(l0, l1)
    e0 = jnp.exp(l0 - m)
    e1 = jnp.exp(l1 - m)
    return e0 / (e0 + e1), e1 / (e0 + e1)


def _hg_gates(qr, fr, lb, pos, tb):
    sq = _sigmoid(qr)
    q = qr * sq * Q_SCALE
    sg = _sigmoid(fr)
    f = lb + (1.0 - lb) * sg
    k = 1.0 - f
    b = _chunk_cumsum(jnp.log(f), pos)
    b3 = b.reshape(tb // CHUNK, CHUNK, HEAD_DIM)
    anc = b3[:, CHUNK // 2 - 1:CHUNK // 2, :]
    last = b3[:, CHUNK - 1:CHUNK, :]
    d3 = b3 - anc
    e_qa3 = jnp.exp(d3)
    e_ka3 = jnp.exp(-d3)
    e_b3 = e_qa3 * jnp.exp(anc)
    e_ko3 = e_ka3 * jnp.exp(last - anc)
    dec = jnp.exp(last)
    flat = lambda a: a.reshape(tb, HEAD_DIM)
    return sq, q, sg, f, k, flat(e_qa3), flat(e_ka3), flat(e_b3), flat(e_ko3), dec


def _intra_mask(sb):
    r = lax.broadcasted_iota(jnp.int32, (sb, sb), 0)
    c = lax.broadcasted_iota(jnp.int32, (sb, sb), 1)
    return ((r // CHUNK) == (c // CHUNK)) & (c <= r)


def _hg_fwd(hg, low, gn):
    T = hg.shape[0]
    tb = min(1024, T)
    sb = min(256, tb)
    nb = T // tb
    nc = tb // CHUNK
    wid = HEADS_PER_STEP * HEAD_DIM

    def body(q_ref, f_ref, i_ref, g_ref, low_ref, gn_ref, o_ref, og_ref, ogt_ref, st_ref, s_scr):
        t = pl.program_id(1)

        @pl.when(t == 0)
        def _():
            s_scr[...] = jnp.zeros_like(s_scr)

        pos = _chunk_pos((tb, HEAD_DIM))
        mask = _intra_mask(sb)
        lanes = [slice(hh * HEAD_DIM, (hh + 1) * HEAD_DIM) for hh in range(HEADS_PER_STEP)]
        qi, ko, vb, dec, st = [], [], [], [], []
        for hh, ln in enumerate(lanes):
            lb, _ = _lower_bound(low_ref.at[:, ln])
            _, q, _, _, k, e_qa, e_ka, e_b, e_ko, dec_h = _hg_gates(q_ref[:, ln], f_ref[:, ln], lb, pos, tb)
            qh = (q * e_qa).astype(BF16)
            kh = (k * e_ka).astype(BF16)
            qi.append((q * e_b).astype(BF16))
            ko.append((k * e_ko).astype(BF16))
            vb.append(i_ref[:, ln].astype(BF16))
            dec.append(dec_h)
            st.append(s_scr[hh])
            for s in range(tb // sb):
                sl = slice(s * sb, (s + 1) * sb)
                p = jnp.where(mask, _mm_nt(qh[sl], kh[sl]), 0.0)
                o_ref[sl, ln] = _mm(p, vb[hh][sl])
        for c in range(nc):
            sl = slice(c * CHUNK, (c + 1) * CHUNK)
            for hh, ln in enumerate(lanes):
                st_ref[hh, c] = st[hh]
                o_ref[sl, ln] = o_ref[sl, ln] + _mm_nt(qi[hh][sl], st[hh])
                st[hh] = dec[hh][c] * st[hh] + _mm_tn(vb[hh][sl], ko[hh][sl])
        for hh, ln in enumerate(lanes):
            s_scr[hh] = st[hh]
            o = o_ref[:, ln]
            r = lax.rsqrt(jnp.mean(o * o, axis=-1, keepdims=True) + EPS)
            gr = g_ref[:, ln]
            og = (o * r * gn_ref[...]) * (gr * _sigmoid(gr))
            og_ref[:, ln] = og.astype(BF16)
            ogt_ref[ln, :] = og.T.astype(BF16)

    col = lambda p: pl.BlockSpec((tb, wid), lambda h, t: (t, p * HEAD_GROUPS + h))
    hcol = pl.BlockSpec((tb, wid), lambda h, t: (t, h))
    return pl.pallas_call(
        body, name="hg_fwd", grid=(HEAD_GROUPS, nb),
        in_specs=[col(0), col(1), col(2), col(3), pl.BlockSpec((2, wid), lambda h, t: (0, h)),
                  pl.BlockSpec((1, HEAD_DIM), lambda h, t: (0, 0))],
        out_specs=[hcol, hcol, pl.BlockSpec((wid, tb), lambda h, t: (h, t)),
                   pl.BlockSpec((HEADS_PER_STEP, nc, HEAD_DIM, HEAD_DIM), lambda h, t: (h, t, 0, 0))],
        out_shape=[jax.ShapeDtypeStruct((T, HG_WIDTH), F32), jax.ShapeDtypeStruct((T, HG_WIDTH), BF16),
                   jax.ShapeDtypeStruct((HG_WIDTH, T), BF16),
                   jax.ShapeDtypeStruct((N_HEADS, T // CHUNK, HEAD_DIM, HEAD_DIM), F32)],
        scratch_shapes=[pltpu.VMEM((HEADS_PER_STEP, HEAD_DIM, HEAD_DIM), F32)],
        compiler_params=_params(("parallel", "arbitrary")),
    )(hg, hg, hg, hg, low, gn)


def _merge_fwd(og, cvo, gt, x, wa, wb, wo):
    T = x.shape[0]
    tm = min(1024, T)

    def body(og_ref, cvo_ref, gt_ref, x_ref, wa_ref, wb_ref, wo_ref, x1_ref, mgt_ref):
        ya = jnp.dot(og_ref[...], _shard_cols(wa_ref), preferred_element_type=F32)
        yb = jnp.dot(cvo_ref[...], _shard_cols(wb_ref), preferred_element_type=F32)
        m = (_sigmoid(gt_ref[:, :D_MODEL].astype(F32)) * ya
             + _sigmoid(gt_ref[:, D_MODEL:].astype(F32)) * yb)
        mgt_ref[...] = m.T.astype(BF16)
        x1_ref[...] = x_ref[...] + jnp.dot(m.astype(BF16), wo_ref[...], preferred_element_type=F32)

    row = lambda n: pl.BlockSpec((tm, n), lambda i: (i, 0))
    return pl.pallas_call(
        body, name="merge_fwd", grid=(T // tm,),
        in_specs=[row(HG_WIDTH), row(CONV_WIDTH), row(2 * D_MODEL), row(D_MODEL),
                  _resident(wa.shape), _resident(wb.shape), _resident(wo.shape)],
        out_specs=[row(D_MODEL), _col(tm, D_MODEL)],
        out_shape=[jax.ShapeDtypeStruct((T, D_MODEL), F32), jax.ShapeDtypeStruct((D_MODEL, T), BF16)],
        compiler_params=_params(("parallel",)),
    )(og, cvo, gt, x, wa, wb, wo)


def _ffn_fwd_loss(x1, g, wg, wu, wd, target, g_fin):
    T = x1.shape[0]
    tm = min(256, T)

    def body(x_ref, g_ref, wg_ref, wu_ref, wd_ref, t_ref, gf_ref,
             ht_ref, gate_ref, up_ref, act_ref, loss_ref, dgf_ref, dx2_ref, dx2t_ref):
        @pl.when(pl.program_id(0) == 0)
        def _():
            loss_ref[...] = jnp.zeros_like(loss_ref)
            dgf_ref[...] = jnp.zeros_like(dgf_ref)

        xv = x_ref[...]
        r = lax.rsqrt(jnp.mean(xv * xv, axis=-1, keepdims=True) + EPS)
        hf = xv * r * g_ref[...]
        h = hf.astype(BF16)
        ht_ref[...] = hf.T.astype(BF16)
        gate = _mm_nt(h, wg_ref[...])
        up = _mm_nt(h, wu_ref[...])
        gate_ref[...] = gate.astype(STASH)
        up_ref[...] = up.astype(STASH)
        act = (gate * _sigmoid(gate) * up).astype(BF16)
        act_ref[...] = act
        x2 = xv + jnp.dot(act, wd_ref[...], preferred_element_type=F32)

        gv = gf_ref[...]
        r2 = lax.rsqrt(jnp.mean(x2 * x2, axis=-1, keepdims=True) + EPS)
        xh = x2 * r2
        err = xh * gv - t_ref[...]
        loss_ref[...] += 0.5 * jnp.sum(jnp.mean(err * err, axis=-1, keepdims=True), axis=0, keepdims=True)
        dy = err * (1.0 / D_MODEL)
        dgf_ref[...] += jnp.sum(dy * xh, axis=0, keepdims=True)
        w = dy * gv
        dx2 = r2 * (w - xh * jnp.mean(w * xh, axis=-1, keepdims=True))
        dx2_ref[...] = dx2
        dx2t_ref[...] = dx2.T.astype(BF16)

    row = lambda n: pl.BlockSpec((tm, n), lambda i: (i, 0))
    return pl.pallas_call(
        body, name="ffn_fwd_loss", grid=(T // tm,),
        in_specs=[row(D_MODEL), _full((1, D_MODEL)), _resident(wg.shape), _resident(wu.shape), _resident(wd.shape),
                  row(D_MODEL), _full((1, D_MODEL))],
        out_specs=[_col(tm, D_MODEL), row(D_FF), row(D_FF), row(D_FF), _full((1, 128)), _full((1, D_MODEL)),
                   row(D_MODEL), _col(tm, D_MODEL)],
        out_shape=[jax.ShapeDtypeStruct((D_MODEL, T), BF16), jax.ShapeDtypeStruct((T, D_FF), STASH),
                   jax.ShapeDtypeStruct((T, D_FF), STASH), jax.ShapeDtypeStruct((T, D_FF), BF16),
                   jax.ShapeDtypeStruct((1, 128), F32), jax.ShapeDtypeStruct((1, D_MODEL), F32),
                   jax.ShapeDtypeStruct((T, D_MODEL), F32), jax.ShapeDtypeStruct((D_MODEL, T), BF16)],
        compiler_params=_params(("arbitrary",)),
    )(x1, g, wg, wu, wd, target, g_fin)


def _ffn_bwd(dx2, x1, gate, up, g, wg, wu, wd):
    T = x1.shape[0]
    tm = min(256, T)

    def body(dx2_ref, x_ref, gate_ref, up_ref, g_ref, wg_ref, wu_ref, wd_ref, dgate_ref, dup_ref, dx1_ref, dgn_ref):
        @pl.when(pl.program_id(0) == 0)
        def _():
            dgn_ref[...] = jnp.zeros_like(dgn_ref)

        dx2 = dx2_ref[...]
        dact = _mm_nt(dx2, wd_ref[...])
        gate = gate_ref[...].astype(F32)
        s = _sigmoid(gate)
        dgate = (dact * up_ref[...].astype(F32) * (s * (1.0 + gate * (1.0 - s)))).astype(BF16)
        dup = (dact * (gate * s)).astype(BF16)
        dgate_ref[...] = dgate
        dup_ref[...] = dup
        dh = _mm(dgate, wg_ref[...]) + _mm(dup, wu_ref[...])
        xv = x_ref[...]
        r = lax.rsqrt(jnp.mean(xv * xv, axis=-1, keepdims=True) + EPS)
        xh = xv * r
        dgn_ref[...] += jnp.sum(dh * xh, axis=0, keepdims=True)
        w = dh * g_ref[...]
        dx1_ref[...] = dx2 + r * (w - xh * jnp.mean(w * xh, axis=-1, keepdims=True))

    row = lambda n: pl.BlockSpec((tm, n), lambda i: (i, 0))
    return pl.pallas_call(
        body, name="ffn_bwd", grid=(T // tm,),
        in_specs=[row(D_MODEL), row(D_MODEL), row(D_FF), row(D_FF), _full((1, D_MODEL)),
                  _resident(wg.shape), _resident(wu.shape), _resident(wd.shape)],
        out_specs=[row(D_FF), row(D_FF), row(D_MODEL), _full((1, D_MODEL))],
        out_shape=[jax.ShapeDtypeStruct((T, D_FF), BF16), jax.ShapeDtypeStruct((T, D_FF), BF16),
                   jax.ShapeDtypeStruct((T, D_MODEL), F32), jax.ShapeDtypeStruct((1, D_MODEL), F32)],
        compiler_params=_params(("arbitrary",)),
    )(dx2, x1, gate, up, g, wg, wu, wd)


def _merge_bwd(dx1, og, cvo, gt, cv, wa, wb, wo, conv_w):
    T = dx1.shape[0]
    tm = min(512, T)
    nt = T // tm

    def body(dx_ref, og_ref, cvo_ref, gt_ref, cv_ref, halo_ref, wa_ref, wb_ref, wo_ref, cw_ref,
             dgt_ref, dya_ref, dyb_ref, dog_ref, dcv_ref, dcw_ref, prev_u, next_dy):
        step = pl.program_id(0)

        @pl.when(step == 0)
        def _():
            next_dy[...] = jnp.zeros_like(next_dy)
            dcw_ref[...] = jnp.zeros_like(dcw_ref)

        dm = _mm_nt(dx_ref[...], wo_ref[...])
        wa = _shard_cols(wa_ref)
        wb = _shard_cols(wb_ref)
        ya = jnp.dot(og_ref[...], wa, preferred_element_type=F32)
        yb = jnp.dot(cvo_ref[...], wb, preferred_element_type=F32)
        sa = _sigmoid(gt_ref[:, :D_MODEL].astype(F32))
        sb = _sigmoid(gt_ref[:, D_MODEL:].astype(F32))
        dgt_ref[:, :D_MODEL] = (dm * ya * (sa * (1.0 - sa))).astype(BF16)
        dgt_ref[:, D_MODEL:] = (dm * yb * (sb * (1.0 - sb))).astype(BF16)
        dya = (dm * sa).astype(BF16)
        dyb = (dm * sb).astype(BF16)
        dya_ref[...] = dya
        dyb_ref[...] = dyb
        dog_ref[...] = _mm_nt(dya, wa)
        dcvo = _mm_nt(dyb, wb)

        cvt = cv_ref[...].astype(F32)
        c, bg, xb = cvt[:, :CONV_WIDTH], cvt[:, CONV_WIDTH:2 * CONV_WIDTH], cvt[:, 2 * CONV_WIDTH:]
        halo = halo_ref[...].astype(F32)
        first_tile = step == nt - 1
        prev_u[...] = jnp.where(first_tile, 0.0, halo[:, :CONV_WIDTH] * halo[:, 2 * CONV_WIDTH:])
        u = c * xb
        row = lax.broadcasted_iota(jnp.int32, u.shape, 0)
        p1 = prev_u[HALO - 1:HALO, :]
        p2 = prev_u[HALO - 2:HALO - 1, :]
        u1 = jnp.where(row >= 1, pltpu.roll(u, 1, 0), p1)
        u2 = jnp.where(row >= 2, pltpu.roll(u, 2, 0), jnp.where(row == 1, p1, p2))
        w0, w1, w2 = cw_ref[0:1, :], cw_ref[1:2, :], cw_ref[2:3, :]
        y = w0 * u2 + w1 * u1 + w2 * u
        dcv_ref[:, CONV_WIDTH:2 * CONV_WIDTH] = (dcvo * y).astype(BF16)
        dy = dcvo * bg
        dcw_ref[0:1, :] += jnp.sum(dy * u2, axis=0, keepdims=True)
        dcw_ref[1:2, :] += jnp.sum(dy * u1, axis=0, keepdims=True)
        dcw_ref[2:3, :] += jnp.sum(dy * u, axis=0, keepdims=True)
        n1 = next_dy[0:1, :]
        n2 = next_dy[1:2, :]
        dy1 = jnp.where(row < tm - 1, pltpu.roll(dy, tm - 1, 0), n1)
        dy2 = jnp.where(row < tm - 2, pltpu.roll(dy, tm - 2, 0), jnp.where(row == tm - 2, n1, n2))
        du = w2 * dy + w1 * dy1 + w0 * dy2
        dcv_ref[:, :CONV_WIDTH] = (du * xb).astype(BF16)
        dcv_ref[:, 2 * CONV_WIDTH:] = (du * c).astype(BF16)
        next_dy[...] = dy[:HALO, :]

    rt = lambda i: nt - 1 - i
    row = lambda n: pl.BlockSpec((tm, n), lambda i: (rt(i), 0))
    halo = pl.BlockSpec((HALO, N_CV), lambda i: (jnp.maximum(rt(i) * (tm // HALO) - 1, 0), 0))
    return pl.pallas_call(
        body, name="merge_bwd", grid=(nt,),
        in_specs=[row(D_MODEL), row(HG_WIDTH), row(CONV_WIDTH), row(2 * D_MODEL), row(N_CV), halo,
                  _resident(wa.shape), _resident(wb.shape), _resident(wo.shape), _full((CONV_K, CONV_WIDTH))],
        out_specs=[row(2 * D_MODEL), row(D_MODEL), row(D_MODEL), row(HG_WIDTH), row(N_CV),
                   _full((CONV_K, CONV_WIDTH))],
        out_shape=[jax.ShapeDtypeStruct((T, 2 * D_MODEL), BF16), jax.ShapeDtypeStruct((T, D_MODEL), BF16),
                   jax.ShapeDtypeStruct((T, D_MODEL), BF16), jax.ShapeDtypeStruct((T, HG_WIDTH), F32),
                   jax.ShapeDtypeStruct((T, N_CV), BF16), jax.ShapeDtypeStruct((CONV_K, CONV_WIDTH), F32)],
        scratch_shapes=[pltpu.VMEM((HALO, CONV_WIDTH), F32), pltpu.VMEM((HALO, CONV_WIDTH), F32)],
        compiler_params=_params(("arbitrary",)),
    )(dx1, og, cvo, gt, cv, cv, wa, wb, wo, conv_w)


def _drop_operands(body, first, count):
    def wrapped(*refs):
        return body(*refs[:first], *refs[first + count:])
    return wrapped


def _hg_bwd(dog, hg, o, st, low, gn, after=()):
    T = hg.shape[0]
    tb = min(512, T)
    sb = min(256, tb)
    nb = T // tb
    nc = tb // CHUNK
    wid = HEADS_PER_STEP * HEAD_DIM

    def body(q_ref, f_ref, i_ref, g_ref, low_ref, gn_ref, o_ref, dog_ref, st_ref,
             dhg_ref, dlow_ref, dgn_ref,
             ds_scr, dqi_scr, dko_scr, dv_scr, dd_scr, dqh_scr, dkh_scr):
        h = pl.program_id(0)
        t = pl.program_id(1)
        dq_ref, df_ref, di_ref, dg_ref = (dhg_ref.at[:, p * HG_WIDTH:(p + 1) * HG_WIDTH] for p in range(4))

        @pl.when(t == 0)
        def _():
            ds_scr[...] = jnp.zeros_like(ds_scr)
            dlow_ref[...] = jnp.zeros_like(dlow_ref)

        @pl.when((t == 0) & (h == 0))
        def _():
            dgn_ref[...] = jnp.zeros_like(dgn_ref)

        pos = _chunk_pos((tb, HEAD_DIM))
        mask = _intra_mask(sb)
        gnv = gn_ref[...]
        lanes = [slice(hh * HEAD_DIM, (hh + 1) * HEAD_DIM) for hh in range(HEADS_PER_STEP)]
        heads = []
        for hh, ln in enumerate(lanes):
            lb, lb1 = _lower_bound(low_ref.at[:, ln])
            qr = q_ref[:, ln]
            sq, q, sg, f, k, e_qa, e_ka, e_b, e_ko, dec = _hg_gates(qr, f_ref[:, ln], lb, pos, tb)

            gr = g_ref[:, ln]
            o = o_ref[:, ln]
            dog_v = dog_ref[:, ln]
            sgr = _sigmoid(gr)
            r = lax.rsqrt(jnp.mean(o * o, axis=-1, keepdims=True) + EPS)
            oh = o * r
            dg_ref[:, ln] = (dog_v * (oh * gnv) * (sgr * (1.0 + gr * (1.0 - sgr)))).astype(BF16)
            don = dog_v * (gr * sgr)
            dgn_ref[...] += jnp.sum(don * oh, axis=0, keepdims=True)
            w = don * gnv
            do = (r * (w - oh * jnp.mean(w * oh, axis=-1, keepdims=True))).astype(BF16)

            qh = (q * e_qa).astype(BF16)
            kh = (k * e_ka).astype(BF16)
            qi = (q * e_b).astype(BF16)
            ko = (k * e_ko).astype(BF16)
            vb = i_ref[:, ln].astype(BF16)

            for s in range(tb // sb):
                sl = slice(s * sb, (s + 1) * sb)
                p = jnp.where(mask, _mm_nt(qh[sl], kh[sl]), 0.0).astype(BF16)
                dp = jnp.where(mask, _mm_nt(do[sl], vb[sl]), 0.0).astype(BF16)
                dv_scr[sl, ln] = _mm_tn(p, do[sl])
                dqh_scr[sl, ln] = _mm(dp, kh[sl])
                dkh_scr[sl, ln] = _mm_tn(dp, qh[sl])
            heads.append(dict(lb=lb, lb1=lb1, qr=qr, sq=sq, q=q, sg=sg, f=f, k=k, e_qa=e_qa, e_ka=e_ka, e_b=e_b,
                              e_ko=e_ko, dec=dec, do=do, qi=qi, ko=ko, vb=vb, ds=ds_scr[hh]))

        for c in reversed(range(nc)):
            sl = slice(c * CHUNK, (c + 1) * CHUNK)
            for hh, ln in enumerate(lanes):
                hd = heads[hh]
                ds = hd["ds"]
                st_c = st_ref[hh, c]
                dqi_scr[sl, ln] = _mm(hd["do"][sl], st_c)
                dko_scr[sl, ln] = _mm(hd["vb"][sl], ds)
                dv_scr[sl, ln] = dv_scr[sl, ln] + _mm_nt(hd["ko"][sl], ds)
                dec_c = hd["dec"][c]
                dd_scr[sl, ln] = jnp.broadcast_to(dec_c * jnp.sum(ds * st_c, axis=0, keepdims=True),
                                                  (CHUNK, HEAD_DIM))
                hd["ds"] = dec_c * ds + _mm_tn(hd["do"][sl], hd["qi"][sl])

        for hh, ln in enumerate(lanes):
            hd = heads[hh]
            ds_scr[hh] = hd["ds"]
            q, k, lb = hd["q"], hd["k"], hd["lb"]
            dko_e = dko_scr[:, ln] * hd["e_ko"]
            dq = dqh_scr[:, ln] * hd["e_qa"] + dqi_scr[:, ln] * hd["e_b"]
            dk = dkh_scr[:, ln] * hd["e_ka"] + dko_e
            kd3 = (k * dko_e).reshape(nc, CHUNK, HEAD_DIM)
            last = jnp.broadcast_to(jnp.sum(kd3, axis=1, keepdims=True), kd3.shape).reshape(tb, HEAD_DIM)
            db = q * dq - k * dk + jnp.where(pos == CHUNK - 1, dd_scr[:, ln] + last, 0.0)
            dlg = _chunk_rev_cumsum(db, pos)
            dfv = dlg / hd["f"] - dk
            s_low = jnp.sum(dfv * (1.0 - hd["sg"]), axis=0, keepdims=True)
            dlow_ref[0:1, ln] += s_low * lb * (1.0 - lb)
            dlow_ref[1:2, ln] += -s_low * lb * hd["lb1"]
            df_ref[:, ln] = (dfv * (1.0 - lb) * hd["sg"] * (1.0 - hd["sg"])).astype(BF16)
            dq_ref[:, ln] = (dq * Q_SCALE * (hd["sq"] * (1.0 + hd["qr"] * (1.0 - hd["sq"])))).astype(BF16)
            di_ref[:, ln] = dv_scr[:, ln].astype(BF16)

    rt = lambda t: nb - 1 - t
    col = lambda p: pl.BlockSpec((tb, wid), lambda h, t: (rt(t), p * HEAD_GROUPS + h))
    hcol = pl.BlockSpec((tb, wid), lambda h, t: (rt(t), h))
    assert HEAD_GROUPS == 1
    tile = pltpu.VMEM((tb, wid), F32)
    return pl.pallas_call(
        _drop_operands(body, 9, len(after)), name="hg_bwd", grid=(HEAD_GROUPS, nb),
        in_specs=[col(0), col(1), col(2), col(3), pl.BlockSpec((2, wid), lambda h, t: (0, h)),
                  pl.BlockSpec((1, HEAD_DIM), lambda h, t: (0, 0)), hcol, hcol,
                  pl.BlockSpec((HEADS_PER_STEP, nc, HEAD_DIM, HEAD_DIM), lambda h, t: (h, rt(t), 0, 0))]
                 + [HBM_SPEC] * len(after),
        out_specs=[pl.BlockSpec((tb, N_HG), lambda h, t: (rt(t), 0)), pl.BlockSpec((2, wid), lambda h, t: (0, h)),
                   pl.BlockSpec((1, HEAD_DIM), lambda h, t: (0, 0))],
        out_shape=[jax.ShapeDtypeStruct((T, N_HG), BF16), jax.ShapeDtypeStruct((2, HG_WIDTH), F32),
                   jax.ShapeDtypeStruct((1, HEAD_DIM), F32)],
        scratch_shapes=[pltpu.VMEM((HEADS_PER_STEP, HEAD_DIM, HEAD_DIM), F32), tile, tile, tile, tile, tile, tile],
        compiler_params=_params(("arbitrary", "arbitrary")),
    )(hg, hg, hg, hg, low, gn, o, dog, st, *after)


def _in_bwd(dparts, w_in, x, dx1, g, after=()):
    T = x.shape[0]
    tm = min(512, T)
    widths = [p.shape[1] for p in dparts]
    offs = [sum(widths[:i]) for i in range(len(widths))]
    n = len(dparts)

    def body(*refs):
        d_refs = refs[:n]
        w_ref, x_ref, dx1_ref, g_ref, dx_ref, dgn_ref = refs[n:]

        @pl.when(pl.program_id(0) == 0)
        def _():
            dgn_ref[...] = jnp.zeros_like(dgn_ref)

        dh = None
        for d_ref, off, wd in zip(d_refs, offs, widths):
            part = _mm(d_ref[...], w_ref[off:off + wd, :])
            dh = part if dh is None else dh + part
        xv = x_ref[...]
        r = lax.rsqrt(jnp.mean(xv * xv, axis=-1, keepdims=True) + EPS)
        xh = xv * r
        dgn_ref[...] += jnp.sum(dh * xh, axis=0, keepdims=True)
        w = dh * g_ref[...]
        dx_ref[...] = dx1_ref[...] + r * (w - xh * jnp.mean(w * xh, axis=-1, keepdims=True))

    row = lambda m: pl.BlockSpec((tm, m), lambda i: (i, 0))
    return pl.pallas_call(
        _drop_operands(body, n + 4, len(after)), name="in_bwd", grid=(T // tm,),
        in_specs=[row(wd) for wd in widths] + [_resident(w_in.shape), row(D_MODEL), row(D_MODEL), _full((1, D_MODEL))]
                 + [HBM_SPEC] * len(after),
        out_specs=[row(D_MODEL), _full((1, D_MODEL))],
        out_shape=[jax.ShapeDtypeStruct((T, D_MODEL), F32), jax.ShapeDtypeStruct((1, D_MODEL), F32)],
        compiler_params=_params(("arbitrary",)),
    )(*dparts, w_in, x, dx1, g, *after)


def _wgrad(name, at, b, tn, transposed=False, tk=2048, after=()):
    M, T = at.shape
    N = b.shape[1]
    tk = min(tk, T)
    nk = T // tk
    if transposed:
        out_spec, out_shape = pl.BlockSpec((tn, M), lambda j, k: (j, 0)), (N, M)
    else:
        out_spec, out_shape = pl.BlockSpec((M, tn), lambda j, k: (0, j)), (M, N)

    if nk == 1:
        def body1(a_ref, b_ref, o_ref):
            part = _mm(a_ref[...], b_ref[...])
            o_ref[...] = (part.T if transposed else part).astype(BF16)

        return pl.pallas_call(
            _drop_operands(body1, 2, len(after)), name=name, grid=(N // tn, 1),
            in_specs=[_resident((M, T)), pl.BlockSpec((T, tn), lambda j, k: (0, j))] + [HBM_SPEC] * len(after),
            out_specs=out_spec, out_shape=jax.ShapeDtypeStruct(out_shape, BF16),
            compiler_params=_params(("parallel", "arbitrary")),
        )(at, b, *after)

    def body(a_ref, b_ref, o_ref, acc):
        k = pl.program_id(1)

        @pl.when(k == 0)
        def _():
            acc[...] = jnp.zeros_like(acc)

        acc[...] += _mm(a_ref[...], b_ref[...])

        @pl.when(k == nk - 1)
        def _():
            o_ref[...] = (acc[...].T if transposed else acc[...]).astype(BF16)

    return pl.pallas_call(
        _drop_operands(body, 2, len(after)), name=name, grid=(N // tn, nk),
        in_specs=[pl.BlockSpec((M, tk), lambda j, k: (0, k)), pl.BlockSpec((tk, tn), lambda j, k: (k, j))]
                 + [HBM_SPEC] * len(after),
        out_specs=out_spec, out_shape=jax.ShapeDtypeStruct(out_shape, BF16),
        scratch_shapes=[pltpu.VMEM((M, tn), F32)],
        compiler_params=_params(("parallel", "arbitrary")),
    )(at, b, *after)


def _wgrad_branches(ogt, dya, cvot, dyb):
    M, T = ogt.shape
    N = dya.shape[1]
    c = N // N_DEV
    per = 2
    tn = per * c

    def body(at_ref, da_ref, bt_ref, db_ref, oa_ref, ob_ref):
        ga = _mm(at_ref[...], da_ref[...])
        gb = _mm(bt_ref[...], db_ref[...])
        for s in range(per):
            oa_ref[s] = ga[:, s * c:(s + 1) * c].astype(BF16)
            ob_ref[s] = gb[:, s * c:(s + 1) * c].astype(BF16)

    rhs = pl.BlockSpec((T, tn), lambda j: (0, j))
    owners = pl.BlockSpec((per, M, c), lambda j: (j, 0, 0))
    out = jax.ShapeDtypeStruct((N_DEV, M, c), BF16)
    return pl.pallas_call(
        body, name="wgrad_branches", grid=(N // tn,),
        in_specs=[_resident((M, T)), rhs, _resident((M, T)), rhs], out_specs=[owners] * 2, out_shape=[out, out],
        compiler_params=_params(("parallel",)),
    )(ogt, dya, cvot, dyb)


def _wgrad_in(ht, dparts, after=()):
    M, T = ht.shape
    tn = 512
    nblk = [p.shape[1] // tn for p in dparts]
    start = [sum(nblk[:i]) for i in range(len(nblk))]
    n = len(dparts)

    def body(a_ref, *refs):
        d_refs, o_ref = refs[:n], refs[n]
        j = pl.program_id(0)
        for d_ref, s, nb in zip(d_refs, start, nblk):
            @pl.when((j >= s) & (j < s + nb))
            def _():
                o_ref[...] = _mm(a_ref[...], d_ref[...]).T.astype(BF16)

    def piece_spec(s, nb):
        return pl.BlockSpec((T, tn), lambda j: (0, jnp.clip(j - s, 0, nb - 1)))

    return pl.pallas_call(
        _drop_operands(body, 1 + n, len(after)), name="wgrad_in", grid=(sum(nblk),),
        in_specs=[_resident((M, T))] + [piece_spec(s, nb) for s, nb in zip(start, nblk)] + [HBM_SPEC] * len(after),
        out_specs=pl.BlockSpec((tn, M), lambda j: (j, 0)),
        out_shape=jax.ShapeDtypeStruct((sum(nblk) * tn, M), BF16),
        compiler_params=_params(("parallel",)),
    )(ht, *dparts, *after)


def _adamw_math(w, g, m, v):
    m = ADAM_B1 * m + (1.0 - ADAM_B1) * g
    v = ADAM_B2 * v + (1.0 - ADAM_B2) * (g * g)
    m_hat = m / (1.0 - ADAM_B1 ** ADAM_STEP)
    v_hat = v / (1.0 - ADAM_B2 ** ADAM_STEP)
    delta = -ADAM_LR * (m_hat / (jnp.sqrt(v_hat) + ADAM_EPS) + ADAM_WD * w)
    return delta, m, v


def _adamw_sum(name, w, parts, m, v):
    R, C = w.shape
    tr = _row_tile(R)

    def body(w_ref, p_ref, m_ref, v_ref, g_out, d_out, m_out, v_out):
        g = p_ref[0].astype(F32)
        for k in range(1, 4):
            g = g + p_ref[k].astype(F32)
        g_out[...] = g
        d_out[...], m_out[...], v_out[...] = _adamw_math(w_ref[...], g, m_ref[...], v_ref[...])

    blk = pl.BlockSpec((tr, C), lambda i: (i, 0))
    out = jax.ShapeDtypeStruct((R, C), F32)
    return pl.pallas_call(
        body, name=name, grid=(R // tr,),
        in_specs=[blk, pl.BlockSpec((4, tr, C), lambda i: (0, i, 0)), blk, blk],
        out_specs=[blk, blk, blk, blk], out_shape=[out, out, out, out],
        compiler_params=_params(("parallel",)),
    )(w, parts, m, v)


_SMALL_SLOTS = (("norm_mix_g", 0, 1, 1024), ("norm_ffn_g", 1, 1, 1024), ("norm_final_g", 2, 1, 1024),
                ("lower_bounds", 3, 2, 512), ("hg_norm_g", 5, 1, 128), ("loss", 6, 1, 128), ("conv_w", 8, 3, 512))
_SMALL_PARAMS = tuple(s for s in _SMALL_SLOTS if s[0] != "loss")
CONV_SHARD = CONV_WIDTH // N_DEV


def _small_pack(small):
    def body(*refs):
        out = refs[-1]
        out[...] = jnp.zeros_like(out)
        for ref, (_, row, rows, lanes) in zip(refs[:-1], _SMALL_SLOTS):
            out[row:row + rows, 0:lanes] = ref[...]

    vmem = pl.BlockSpec(memory_space=pltpu.VMEM)
    return pl.pallas_call(
        body, name="small_pack", in_specs=[vmem] * len(_SMALL_SLOTS), out_specs=vmem,
        out_shape=jax.ShapeDtypeStruct((SMALL_ROWS, 1024), F32),
    )(*[small[name] for name, _, _, _ in _SMALL_SLOTS])


def _small_update(gathered, dev, w, m, v):
    n = len(_SMALL_PARAMS)

    def body(dev_ref, g_ref, *refs):
        w_refs, m_refs, v_refs = refs[:n], refs[n:2 * n], refs[2 * n:3 * n]
        loss_ref, out_refs, sum_scr = refs[3 * n], refs[3 * n + 1:-1], refs[-1]
        total = g_ref[0]
        for k in range(1, N_DEV):
            total = total + g_ref[k]
        sum_scr[...] = total
        loss_ref[...] = sum_scr[6:7, 0:128]
        for p, (name, row, rows, lanes) in enumerate(_SMALL_PARAMS):
            if name == "conv_w":
                g = sum_scr[row:row + rows, 0:CONV_SHARD]
                for s in range(1, N_DEV):
                    g = jnp.where(dev_ref[0] == s, sum_scr[row:row + rows, s * CONV_SHARD:(s + 1) * CONV_SHARD], g)
            else:
                g = sum_scr[row:row + rows, 0:lanes]
            lead = (0,) * (len(w_refs[p].shape) - 2) + (slice(None), slice(None))
            delta, m_new, v_new = _adamw_math(w_refs[p][lead], g, m_refs[p][lead], v_refs[p][lead])
            for k, val in enumerate((g, delta, m_new, v_new)):
                out_refs[4 * p + k][lead] = val

    vmem = pl.BlockSpec(memory_space=pltpu.VMEM)
    outs = [jax.ShapeDtypeStruct((1, 128), F32)]
    for a in w:
        outs += [jax.ShapeDtypeStruct(a.shape, F32)] * 4
    return pl.pallas_call(
        body, name="small_update",
        in_specs=[pl.BlockSpec(memory_space=pltpu.SMEM)] + [vmem] * (1 + 3 * n), out_specs=[vmem] * len(outs),
        out_shape=outs, scratch_shapes=[pltpu.VMEM((SMALL_ROWS, 1024), F32)],
    )(dev, gathered, *w, *m, *v)


def _row_tile(rows):
    for parts in (4, 2):
        if rows % (16 * parts) == 0:
            return rows // parts
    return rows


def _pair_sum(name, by_owner, got, core, after=()):
    n = len(got)

    def body(core_ref, *refs):
        for a_ref, b_ref, o_ref in zip(refs[:n], refs[n:2 * n], refs[2 * n:]):
            o_ref[...] = (a_ref[...].astype(F32) + b_ref[...].astype(F32)).astype(BF16)

    def blk(g):
        return pl.BlockSpec((None,) + g.shape[1:], lambda k, core_ref: (k, 0, 0))

    def mine(g):
        return pl.BlockSpec((None,) + g.shape[1:], lambda k, core_ref: (2 * k + core_ref[0], 0, 0))

    return pl.pallas_call(
        _drop_operands(body, 1 + 2 * n, len(after)), name=name,
        grid_spec=pltpu.PrefetchScalarGridSpec(
            num_scalar_prefetch=1, grid=(4,),
            in_specs=[mine(g) for g in got] + [blk(g) for g in got] + [HBM_SPEC] * len(after),
            out_specs=[blk(g) for g in got]),
        out_shape=[jax.ShapeDtypeStruct(g.shape, BF16) for g in got],
        compiler_params=_params(("parallel",)),
    )(core, *by_owner, *got, *after)


MESH = pl.DeviceIdType.MESH
HBM_SPEC = pl.BlockSpec(memory_space=pl.ANY)


def _handshake(peers):
    barrier = pltpu.get_barrier_semaphore()
    for peer in peers:
        pl.semaphore_signal(barrier, inc=1, device_id=peer, device_id_type=MESH)
    pl.semaphore_wait(barrier, len(peers))


def _comm_call(body, name, operands, out_shape, scratch, collective_id):
    if collective_id is None:
        return pl.pallas_call(body, name=name, in_specs=[HBM_SPEC] * len(operands), out_specs=[HBM_SPEC] * len(out_shape),
                              out_shape=out_shape, scratch_shapes=scratch)(*operands)
    return pl.kernel(body, out_type=out_shape, mesh=plsc.ScalarSubcoreMesh(axis_name="sequencer", num_cores=1),
                     scratch_types=scratch, name=name,
                     compiler_params=pltpu.CompilerParams(collective_id=collective_id))(*operands)


def _all_gather(name, blocks, collective_id=None, after=()):
    n = len(blocks)
    na = len(after)

    def body(*refs):
        x_refs, out_refs = refs[:n], refs[n + na:2 * n + na]
        send_sems, recv_sems, local_sems = refs[2 * n + na:]
        x, y, c = lax.axis_index("x"), lax.axis_index("y"), lax.axis_index("c")
        me, sibling = (x, y, c), (x, y, 1 - c)
        chips = [(1 - x, y), (x, 1 - y), (1 - x, 1 - y)]
        if collective_id is not None:
            _handshake([sibling] + [(*chip, c) for chip in chips])

        def slot(i, px, py, pc):
            return out_refs[i].at[4 * px + 2 * py + pc]

        def copy(i, k, blk, to, src=None):
            return pltpu.make_async_remote_copy(
                src_ref=slot(i, *blk) if src is None else src, dst_ref=slot(i, *blk),
                send_sem=send_sems.at[7 * i + k], recv_sem=recv_sems.at[7 * i + k], device_id=to, device_id_type=MESH)

        mine = [pltpu.make_async_copy(x_refs[i], slot(i, *me), local_sems.at[i]) for i in range(n)]
        for cp in mine:
            cp.start()
        first = []
        for i in range(n):
            first.append(copy(i, 0, me, sibling, src=x_refs[i]))
            first += [copy(i, 1 + j, me, (*chip, c), src=x_refs[i]) for j, chip in enumerate(chips)]
        for cp in first:
            cp.start()
        passed = []
        for i in range(n):
            for j, chip in enumerate(chips):
                copy(i, 1 + j, (*chip, c), me).wait_recv()
                passed.append(copy(i, 4 + j, (*chip, c), sibling))
                passed[-1].start()
        for i in range(n):
            copy(i, 0, sibling, me).wait_recv()
            for j, chip in enumerate(chips):
                copy(i, 4 + j, (*chip, 1 - c), me).wait_recv()
        for cp in first + passed:
            cp.wait_send()
        for cp in mine:
            cp.wait()

    return _comm_call(
        body, name, list(blocks) + list(after), [jax.ShapeDtypeStruct((N_DEV,) + b.shape, b.dtype) for b in blocks],
        [pltpu.SemaphoreType.DMA((7 * n,)), pltpu.SemaphoreType.DMA((7 * n,)), pltpu.SemaphoreType.DMA((n,))],
        collective_id)


def _sibling_swap(name, by_owner, collective_id=None, after=()):
    n = len(by_owner)
    na = len(after)

    def body(*refs):
        x_refs, out_refs = refs[:n], refs[n + na:2 * n + na]
        send_sems, recv_sems = refs[2 * n + na:]
        x, y, c = lax.axis_index("x"), lax.axis_index("y"), lax.axis_index("c")
        if collective_id is not None:
            _handshake([(x, y, 1 - c)])
        copies = []
        for i in range(n):
            for k in range(4):
                copies.append(pltpu.make_async_remote_copy(
                    src_ref=x_refs[i].at[2 * k + 1 - c], dst_ref=out_refs[i].at[k],
                    send_sem=send_sems.at[4 * i + k], recv_sem=recv_sems.at[4 * i + k],
                    device_id=(x, y, 1 - c), device_id_type=MESH))
        for cp in copies:
            cp.start()
        for cp in copies:
            cp.wait()

    return _comm_call(
        body, name, list(by_owner) + list(after),
        [jax.ShapeDtypeStruct((4,) + b.shape[1:], b.dtype) for b in by_owner],
        [pltpu.SemaphoreType.DMA((4 * n,)), pltpu.SemaphoreType.DMA((4 * n,))], collective_id)


def _chip_exchange(name, sums, collective_id=None, after=()):
    n = len(sums)
    na = len(after)

    def body(*refs):
        x_refs, out_refs = refs[:n], refs[n + na:2 * n + na]
        send_sems, recv_sems, local_sems = refs[2 * n + na:]
        x, y, c = lax.axis_index("x"), lax.axis_index("y"), lax.axis_index("c")
        chips = [(1 - x, y), (x, 1 - y), (1 - x, 1 - y)]
        my_chip = 2 * x + y
        if collective_id is not None:
            _handshake([(cx, cy, c) for cx, cy in chips])
        mine = [pltpu.make_async_copy(x_refs[i].at[my_chip], out_refs[i].at[my_chip], local_sems.at[i])
                for i in range(n)]
        for cp in mine:
            cp.start()
        sends = []
        for i in range(n):
            for j, (cx, cy) in enumerate(chips):
                sends.append(pltpu.make_async_remote_copy(
                    src_ref=x_refs[i].at[2 * cx + cy], dst_ref=out_refs[i].at[my_chip],
                    send_sem=send_sems.at[3 * i + j], recv_sem=recv_sems.at[3 * i + j],
                    device_id=(cx, cy, c), device_id_type=MESH))
        for cp in sends:
            cp.start()
        for i in range(n):
            for j, (cx, cy) in enumerate(chips):
                pltpu.make_async_remote_copy(
                    src_ref=x_refs[i].at[my_chip], dst_ref=out_refs[i].at[2 * cx + cy],
                    send_sem=send_sems.at[3 * i + j], recv_sem=recv_sems.at[3 * i + j],
                    device_id=(cx, cy, c), device_id_type=MESH).wait_recv()
        for cp in sends:
            cp.wait_send()
        for cp in mine:
            cp.wait()

    return _comm_call(
        body, name, list(sums) + list(after), [jax.ShapeDtypeStruct(s.shape, s.dtype) for s in sums],
        [pltpu.SemaphoreType.DMA((3 * n,)), pltpu.SemaphoreType.DMA((3 * n,)), pltpu.SemaphoreType.DMA((n,))],
        collective_id)


def _cast_shards(shards):
    n = len(shards)

    def body(*refs):
        for i in range(n):
            refs[n + i][...] = refs[i][...].astype(BF16)

    vmem = pl.BlockSpec(memory_space=pltpu.VMEM)
    return pl.pallas_call(
        body, name="cast_shards", in_specs=[vmem] * n, out_specs=[vmem] * n,
        out_shape=[jax.ShapeDtypeStruct(s.shape, BF16) for s in shards],
        compiler_params=pltpu.CompilerParams(vmem_limit_bytes=VMEM_LIMIT_V7X),
    )(*shards)


BIG = ("w_in", "w_branch_a", "w_branch_b", "w_out", "w_ffn_gate", "w_ffn_up", "w_ffn_down")


def _local_step(x, target, gains, low, conv_w, wg8, reduce):
    g_mix, g_hg, g_ffn, g_fin = gains
    w_in = wg8["w_in"].reshape(N_IN, D_MODEL)
    wg = wg8["w_ffn_gate"].reshape(D_FF, D_MODEL)
    wu = wg8["w_ffn_up"].reshape(D_FF, D_MODEL)
    wa, wb = wg8["w_branch_a"], wg8["w_branch_b"]
    wo = wg8["w_out"].reshape(D_MODEL, D_MODEL)
    wd = wg8["w_ffn_down"].reshape(D_FF, D_MODEL)

    ht, hg, cv, gt, cvo, cvot = _fwd_in(x, g_mix, w_in, conv_w)
    o, og, ogt, st = _hg_fwd(hg, low, g_hg)
    x1, mgt = _merge_fwd(og, cvo, gt, x, wa, wb, wo)
    h2t, gate, up, act, loss, d_gfin, dx2, dx2t = _ffn_fwd_loss(x1, g_ffn, wg, wu, wd, target, g_fin)

    dgate, dup, dx1, d_gffn = _ffn_bwd(dx2, x1, gate, up, g_ffn, wg, wu, wd)
    ffn = dict(
        w_ffn_down=_wgrad("wgrad_ffn_down", dx2t, act, 256, transposed=True, tk=4096
                          ).reshape(N_DEV, D_FF // N_DEV, D_MODEL),
        w_ffn_gate=_wgrad("wgrad_ffn_gate", h2t, dgate, 256, transposed=True, tk=4096
                          ).reshape(N_DEV, D_FF // N_DEV, D_MODEL),
        w_ffn_up=_wgrad("wgrad_ffn_up", h2t, dup, 256, transposed=True, tk=4096
                        ).reshape(N_DEV, D_FF // N_DEV, D_MODEL))
    dgt, dya, dyb, dog, dcv, d_conv = _merge_bwd(dx1, og, cvo, gt, cv, wa, wb, wo, conv_w)
    sums_ffn, got_ffn = reduce.begin(ffn, sum_after=[dya])
    parts_ffn, updated_ffn = reduce.finish(ffn, sums_ffn)
    grad_a, grad_b = _wgrad_branches(ogt, dya, cvot, dyb)
    out = dict(
        w_out=_wgrad("wgrad_out", mgt, dx1, 256, tk=4096, after=sums_ffn[:1]
                     ).reshape(N_DEV, D_MODEL // N_DEV, D_MODEL),
        w_branch_a=grad_a, w_branch_b=grad_b)
    dhg, d_low, d_ghg = _hg_bwd(dog, hg, o, st, low, g_hg, after=list(sums_ffn) + [out["w_out"]])
    sums_out, got_out = reduce.begin(out, after=[parts_ffn[0], dhg], sum_after=updated_ffn)
    parts_out, updated_out = reduce.finish(out, sums_out)
    dparts = [dhg, dcv, dgt]
    w_in_grad = dict(w_in=_wgrad_in(ht, dparts, after=sums_out[:1]).reshape(N_DEV, N_IN // N_DEV, D_MODEL))
    sums_in, _ = reduce.begin(w_in_grad, after=parts_out[:1], sum_after=updated_out)
    parts_in, _ = reduce.finish(w_in_grad, sums_in)
    grad_x, d_gmix = _in_bwd(dparts, w_in, x, dx1, g_mix, after=list(parts_out[:1]) + list(sums_in))
    small = dict(norm_mix_g=d_gmix, norm_ffn_g=d_gffn, norm_final_g=d_gfin, lower_bounds=d_low, hg_norm_g=d_ghg,
                 conv_w=d_conv, loss=loss)
    return grad_x, small, parts_in


def _conv_shard_rows(a):
    return jnp.pad(a, ((0, 5), (0, 64)))


def kernel(x, norm_mix_g, w_in, lower_bounds, hg_norm_g, conv_w, w_branch_a, w_branch_b, w_out, norm_ffn_g, w_ffn_gate, w_ffn_up, w_ffn_down, norm_final_g, loss_target, m_norm_mix_g, m_w_in, m_lower_bounds, m_hg_norm_g, m_conv_w, m_w_branch_a, m_w_branch_b, m_w_out, m_norm_ffn_g, m_w_ffn_gate, m_w_ffn_up, m_w_ffn_down, m_norm_final_g, v_norm_mix_g, v_w_in, v_lower_bounds, v_hg_norm_g, v_conv_w, v_w_branch_a, v_w_branch_b, v_w_out, v_norm_ffn_g, v_w_ffn_gate, v_w_ffn_up, v_w_ffn_down, v_norm_final_g):
    cx, cy, cc = lax.axis_index("x"), lax.axis_index("y"), lax.axis_index("c")
    my_dev = 4 * cx + 2 * cy + cc

    def tr(a):
        return a[0].T

    big = dict(w_in=tr(w_in), w_branch_a=w_branch_a[0], w_branch_b=w_branch_b[0], w_out=w_out[0],
               w_ffn_gate=tr(w_ffn_gate), w_ffn_up=tr(w_ffn_up), w_ffn_down=w_ffn_down[0])
    big_m = dict(w_in=tr(m_w_in), w_branch_a=m_w_branch_a[0], w_branch_b=m_w_branch_b[0], w_out=m_w_out[0],
                 w_ffn_gate=tr(m_w_ffn_gate), w_ffn_up=tr(m_w_ffn_up), w_ffn_down=m_w_ffn_down[0])
    big_v = dict(w_in=tr(v_w_in), w_branch_a=v_w_branch_a[0], w_branch_b=v_w_branch_b[0], w_out=v_w_out[0],
                 w_ffn_gate=tr(v_w_ffn_gate), w_ffn_up=tr(v_w_ffn_up), w_ffn_down=v_w_ffn_down[0])
    transposed = ("w_in", "w_ffn_gate", "w_ffn_up")

    shards = dict(zip(BIG, _cast_shards([big[n] for n in BIG])))
    first = _all_gather("gather_w_in", [shards["w_in"], _conv_shard_rows(conv_w[0])])
    ids = iter(range(1, 16))
    mid = _all_gather("gather_mid", [shards[n] for n in BIG[1:4]], collective_id=next(ids), after=first[1:])
    ffn = _all_gather("gather_ffn", [shards[n] for n in BIG[4:]], collective_id=next(ids), after=first[1:])
    wg8 = dict(zip(BIG, [first[0]] + list(mid) + list(ffn)))
    conv_full = first[1][:, :3, :64].transpose(1, 0, 2).reshape(3, CONV_WIDTH)

    core = cc.reshape(1).astype(jnp.int32)
    outs = {}

    class Reduce:
        @staticmethod
        def begin(grads, after=(), sum_after=()):
            names = list(grads)
            by_owner = [grads[n] for n in names]
            got = _sibling_swap("sibling_swap_" + names[0], by_owner, collective_id=next(ids), after=after)
            sums = _pair_sum("pair_sum_" + names[0], by_owner, got, core, after=sum_after)
            return sums, got

        @staticmethod
        def finish(grads, chip_sums, after=()):
            names = list(grads)
            parts = _chip_exchange("chip_exchange_" + names[0], chip_sums, collective_id=next(ids), after=after)
            for n, p in zip(names, parts):
                outs[n] = _adamw_sum("adamw_" + n, big[n], p, big_m[n], big_v[n])
            return parts, [outs[n][1] for n in names]

    gains = (norm_mix_g, hg_norm_g, norm_ffn_g, norm_final_g.reshape(1, D_MODEL))
    grad_x, small, last = _local_step(x[0], loss_target[0], gains, lower_bounds, conv_full, wg8, Reduce)

    small_all = _all_gather("gather_small", [_small_pack(small)], collective_id=next(ids), after=last[:1])

    def small_state(a):
        return [a[0], a[1], a[2].reshape(1, D_MODEL), a[3], a[4], a[5]]

    upd = _small_update(
        small_all[0], my_dev.reshape(1).astype(jnp.int32),
        small_state((norm_mix_g, norm_ffn_g, norm_final_g, lower_bounds, hg_norm_g, conv_w)),
        small_state((m_norm_mix_g, m_norm_ffn_g, m_norm_final_g, m_lower_bounds, m_hg_norm_g, m_conv_w)),
        small_state((v_norm_mix_g, v_norm_ffn_g, v_norm_final_g, v_lower_bounds, v_hg_norm_g, v_conv_w)))
    loss = upd[0][0, 0]
    small_shape = dict(norm_final_g=(D_MODEL,))
    for p, (name, _, _, _) in enumerate(_SMALL_PARAMS):
        outs[name] = [a.reshape(small_shape.get(name, a.shape)) for a in upd[1 + 4 * p:5 + 4 * p]]

    order = ["norm_mix_g", "w_in", "lower_bounds", "hg_norm_g", "conv_w", "w_branch_a", "w_branch_b", "w_out",
             "norm_ffn_g", "w_ffn_gate", "w_ffn_up", "w_ffn_down", "norm_final_g"]
    result = [loss, grad_x[None]]
    for k in range(4):
        for n in order:
            if n in BIG:
                result.append((outs[n][k].T if n in transposed else outs[n][k])[None])
            else:
                result.append(outs[n][k])
    return tuple(result)
```

```python
import jax
import jax.numpy as jnp
from jax import lax
from jax.experimental import pallas as pl
from jax.experimental.pallas import tpu as pltpu
from jax.experimental.pallas import tpu_sc as plsc

F32 = jnp.float32
BF16 = jnp.bfloat16
STASH = jnp.bfloat16

D_MODEL = 1024
HG_WIDTH = 512
HEAD_DIM = 128
N_HEADS = 4
HEADS_PER_STEP = 4
HEAD_GROUPS = N_HEADS // HEADS_PER_STEP
CONV_WIDTH = 512
CONV_K = 3
D_FF = 2816
CHUNK = 32
EPS = 1e-6
Q_SCALE = HEAD_DIM ** -0.5
N_DEV = 8

ADAM_LR = 0.001
ADAM_B1 = 0.9
ADAM_B2 = 0.999
ADAM_EPS = 1e-08
ADAM_WD = 0.01
ADAM_STEP = 10

VMEM_LIMIT_V7X = 56 * 1024 * 1024
VMEM_LIMIT_LARGE_V7X = 62 * 1024 * 1024

SMALL_ROWS = 16


def _params(sem, vmem=VMEM_LIMIT_V7X):
    return pltpu.CompilerParams(dimension_semantics=sem, vmem_limit_bytes=vmem)


def _mm(a, b):
    return jnp.dot(a.astype(BF16), b.astype(BF16), preferred_element_type=F32)


def _mm_nt(a, b):
    return lax.dot_general(a.astype(BF16), b.astype(BF16), (((1,), (1,)), ((), ())), preferred_element_type=F32)


def _mm_tn(a, b):
    return lax.dot_general(a.astype(BF16), b.astype(BF16), (((0,), (0,)), ((), ())), preferred_element_type=F32)


def _sigmoid(x):
    return 0.5 * jnp.tanh(0.5 * x) + 0.5


def _resident(shape):
    nd = len(shape)
    return pl.BlockSpec(shape, lambda *_: (0,) * nd, pipeline_mode=pl.Buffered(1))


def _full(shape):
    nd = len(shape)
    return pl.BlockSpec(shape, lambda *_: (0,) * nd)


def _shard_cols(w_ref):
    return jnp.concatenate([w_ref[s] for s in range(N_DEV)], axis=1)


N_HG = 4 * HG_WIDTH
N_CV = 3 * CONV_WIDTH
N_GT = 2 * D_MODEL
N_IN = N_HG + N_CV + N_GT


def _col(tm, n):
    return pl.BlockSpec((n, tm), lambda i: (0, i))


HALO = 8


def _fwd_in(x, g, w_in_t, conv_w):
    T = x.shape[0]
    tm = min(512, T)

    def body(x_ref, g_ref, w_ref, cw_ref, ht_ref, hg_ref, cv_ref, gt_ref, cvo_ref, cvot_ref, tail_scr):
        @pl.when(pl.program_id(0) == 0)
        def _():
            tail_scr[...] = jnp.zeros_like(tail_scr)

        xv = x_ref[...]
        r = lax.rsqrt(jnp.mean(xv * xv, axis=-1, keepdims=True) + EPS)
        hf = xv * r * g_ref[...]
        h = hf.astype(BF16)
        ht_ref[...] = hf.T.astype(BF16)
        hg_ref[...] = _mm_nt(h, w_ref[:N_HG, :])
        cv = _mm_nt(h, w_ref[N_HG:N_HG + N_CV, :])
        cv_ref[...] = cv.astype(STASH)
        gt_ref[...] = _mm_nt(h, w_ref[N_HG + N_CV:, :]).astype(STASH)

        u = cv[:, :CONV_WIDTH] * cv[:, 2 * CONV_WIDTH:]
        row = lax.broadcasted_iota(jnp.int32, u.shape, 0)
        prev1 = tail_scr[HALO - 1:HALO, :]
        prev2 = tail_scr[HALO - 2:HALO - 1, :]
        u1 = jnp.where(row >= 1, pltpu.roll(u, 1, 0), prev1)
        u2 = jnp.where(row >= 2, pltpu.roll(u, 2, 0), jnp.where(row == 1, prev1, prev2))
        y = cw_ref[0:1, :] * u2 + cw_ref[1:2, :] * u1 + cw_ref[2:3, :] * u
        out = cv[:, CONV_WIDTH:2 * CONV_WIDTH] * y
        cvo_ref[...] = out.astype(BF16)
        cvot_ref[...] = out.T.astype(BF16)
        tail_scr[...] = u[tm - HALO:, :]

    row = lambda n: pl.BlockSpec((tm, n), lambda i: (i, 0))
    return pl.pallas_call(
        body, name="fwd_in", grid=(T // tm,),
        in_specs=[row(D_MODEL), _full((1, D_MODEL)), _resident(w_in_t.shape), _full((CONV_K, CONV_WIDTH))],
        out_specs=[_col(tm, D_MODEL), row(N_HG), row(N_CV), row(N_GT), row(CONV_WIDTH), _col(tm, CONV_WIDTH)],
        out_shape=[jax.ShapeDtypeStruct((D_MODEL, T), BF16), jax.ShapeDtypeStruct((T, N_HG), F32),
                   jax.ShapeDtypeStruct((T, N_CV), STASH), jax.ShapeDtypeStruct((T, N_GT), STASH),
                   jax.ShapeDtypeStruct((T, CONV_WIDTH), BF16), jax.ShapeDtypeStruct((CONV_WIDTH, T), BF16)],
        scratch_shapes=[pltpu.VMEM((HALO, CONV_WIDTH), F32)],
        compiler_params=_params(("arbitrary",)),
    )(x, g, w_in_t, conv_w)


def _chunk_pos(shape):
    return lax.broadcasted_iota(jnp.int32, shape, 0) & (CHUNK - 1)


def _chunk_cumsum(x, pos):
    s = 1
    while s < CHUNK:
        x = x + jnp.where(pos >= s, pltpu.roll(x, s, 0), 0.0)
        s *= 2
    return x


def _chunk_rev_cumsum(x, pos):
    n = x.shape[0]
    s = 1
    while s < CHUNK:
        x = x + jnp.where(pos + s < CHUNK, pltpu.roll(x, n - s, 0), 0.0)
        s *= 2
    return x


def _lower_bound(low_ref):
    l0 = low_ref[0:1, :]
    l1 = low_ref[1:2, :]
    m = jnp.maximum(l0, l1)
    e0 = jnp.exp(l0 - m)
    e1 = jnp.exp(l1 - m)
    return e0 / (e0 + e1), e1 / (e0 + e1)


def _hg_gates(qr, fr, lb, pos, tb):
    sq = _sigmoid(qr)
    q = qr * sq * Q_SCALE
    sg = _sigmoid(fr)
    f = lb + (1.0 - lb) * sg
    k = 1.0 - f
    b = _chunk_cumsum(jnp.log(f), pos)
    b3 = b.reshape(tb // CHUNK, CHUNK, HEAD_DIM)
    anc = b3[:, CHUNK // 2 - 1:CHUNK // 2, :]
    last = b3[:, CHUNK - 1:CHUNK, :]
    d3 = b3 - anc
    e_qa3 = jnp.exp(d3)
    e_ka3 = jnp.exp(-d3)
    e_b3 = e_qa3 * jnp.exp(anc)
    e_ko3 = e_ka3 * jnp.exp(last - anc)
    dec = jnp.exp(last)
    flat = lambda a: a.reshape(tb, HEAD_DIM)
    return sq, q, sg, f, k, flat(e_qa3), flat(e_ka3), flat(e_b3), flat(e_ko3), dec


def _intra_mask(sb):
    r = lax.broadcasted_iota(jnp.int32, (sb, sb), 0)
    c = lax.broadcasted_iota(jnp.int32, (sb, sb), 1)
    return ((r // CHUNK) == (c // CHUNK)) & (c <= r)


def _hg_fwd(hg, low, gn):
    T = hg.shape[0]
    tb = min(1024, T)
    sb = min(256, tb)
    nb = T // tb
    nc = tb // CHUNK
    wid = HEADS_PER_STEP * HEAD_DIM

    def body(q_ref, f_ref, i_ref, g_ref, low_ref, gn_ref, o_ref, og_ref, ogt_ref, st_ref, s_scr):
        t = pl.program_id(1)

        @pl.when(t == 0)
        def _():
            s_scr[...] = jnp.zeros_like(s_scr)

        pos = _chunk_pos((tb, HEAD_DIM))
        mask = _intra_mask(sb)
        lanes = [slice(hh * HEAD_DIM, (hh + 1) * HEAD_DIM) for hh in range(HEADS_PER_STEP)]
        qi, ko, vb, dec, st = [], [], [], [], []
        for hh, ln in enumerate(lanes):
            lb, _ = _lower_bound(low_ref.at[:, ln])
            _, q, _, _, k, e_qa, e_ka, e_b, e_ko, dec_h = _hg_gates(q_ref[:, ln], f_ref[:, ln], lb, pos, tb)
            qh = (q * e_qa).astype(BF16)
            kh = (k * e_ka).astype(BF16)
            qi.append((q * e_b).astype(BF16))
            ko.append((k * e_ko).astype(BF16))
            vb.append(i_ref[:, ln].astype(BF16))
            dec.append(dec_h)
            st.append(s_scr[hh])
            for s in range(tb // sb):
                sl = slice(s * sb, (s + 1) * sb)
                p = jnp.where(mask, _mm_nt(qh[sl], kh[sl]), 0.0)
                o_ref[sl, ln] = _mm(p, vb[hh][sl])
        for c in range(nc):
            sl = slice(c * CHUNK, (c + 1) * CHUNK)
            for hh, ln in enumerate(lanes):
                st_ref[hh, c] = st[hh]
                o_ref[sl, ln] = o_ref[sl, ln] + _mm_nt(qi[hh][sl], st[hh])
                st[hh] = dec[hh][c] * st[hh] + _mm_tn(vb[hh][sl], ko[hh][sl])
        for hh, ln in enumerate(lanes):
            s_scr[hh] = st[hh]
            o = o_ref[:, ln]
            r = lax.rsqrt(jnp.mean(o * o, axis=-1, keepdims=True) + EPS)
            gr = g_ref[:, ln]
            og = (o * r * gn_ref[...]) * (gr * _sigmoid(gr))
            og_ref[:, ln] = og.astype(BF16)
            ogt_ref[ln, :] = og.T.astype(BF16)

    col = lambda p: pl.BlockSpec((tb, wid), lambda h, t: (t, p * HEAD_GROUPS + h))
    hcol = pl.BlockSpec((tb, wid), lambda h, t: (t, h))
    return pl.pallas_call(
        body, name="hg_fwd", grid=(HEAD_GROUPS, nb),
        in_specs=[col(0), col(1), col(2), col(3), pl.BlockSpec((2, wid), lambda h, t: (0, h)),
                  pl.BlockSpec((1, HEAD_DIM), lambda h, t: (0, 0))],
        out_specs=[hcol, hcol, pl.BlockSpec((wid, tb), lambda h, t: (h, t)),
                   pl.BlockSpec((HEADS_PER_STEP, nc, HEAD_DIM, HEAD_DIM), lambda h, t: (h, t, 0, 0))],
        out_shape=[jax.ShapeDtypeStruct((T, HG_WIDTH), F32), jax.ShapeDtypeStruct((T, HG_WIDTH), BF16),
                   jax.ShapeDtypeStruct((HG_WIDTH, T), BF16),
                   jax.ShapeDtypeStruct((N_HEADS, T // CHUNK, HEAD_DIM, HEAD_DIM), F32)],
        scratch_shapes=[pltpu.VMEM((HEADS_PER_STEP, HEAD_DIM, HEAD_DIM), F32)],
        compiler_params=_params(("parallel", "arbitrary")),
    )(hg, hg, hg, hg, low, gn)


def _merge_fwd(og, cvo, gt, x, wa, wb, wo):
    T = x.shape[0]
    tm = min(1024, T)

    def body(og_ref, cvo_ref, gt_ref, x_ref, wa_ref, wb_ref, wo_ref, x1_ref, mgt_ref):
        ya = jnp.dot(og_ref[...], _shard_cols(wa_ref), preferred_element_type=F32)
        yb = jnp.dot(cvo_ref[...], _shard_cols(wb_ref), preferred_element_type=F32)
        m = (_sigmoid(gt_ref[:, :D_MODEL].astype(F32)) * ya
             + _sigmoid(gt_ref[:, D_MODEL:].astype(F32)) * yb)
        mgt_ref[...] = m.T.astype(BF16)
        x1_ref[...] = x_ref[...] + jnp.dot(m.astype(BF16), wo_ref[...], preferred_element_type=F32)

    row = lambda n: pl.BlockSpec((tm, n), lambda i: (i, 0))
    return pl.pallas_call(
        body, name="merge_fwd", grid=(T // tm,),
        in_specs=[row(HG_WIDTH), row(CONV_WIDTH), row(2 * D_MODEL), row(D_MODEL),
                  _resident(wa.shape), _resident(wb.shape), _resident(wo.shape)],
        out_specs=[row(D_MODEL), _col(tm, D_MODEL)],
        out_shape=[jax.ShapeDtypeStruct((T, D_MODEL), F32), jax.ShapeDtypeStruct((D_MODEL, T), BF16)],
        compiler_params=_params(("parallel",)),
    )(og, cvo, gt, x, wa, wb, wo)


def _ffn_fwd_loss(x1, g, wg, wu, wd, target, g_fin):
    T = x1.shape[0]
    tm = min(512, T)

    def body(x_ref, g_ref, wg_ref, wu_ref, wd_ref, t_ref, gf_ref,
             ht_ref, gate_ref, up_ref, act_ref, loss_ref, dgf_ref, dx2_ref, dx2t_ref):
        @pl.when(pl.program_id(0) == 0)
        def _():
            loss_ref[...] = jnp.zeros_like(loss_ref)
            dgf_ref[...] = jnp.zeros_like(dgf_ref)

        xv = x_ref[...]
        r = lax.rsqrt(jnp.mean(xv * xv, axis=-1, keepdims=True) + EPS)
        hf = xv * r * g_ref[...]
        h = hf.astype(BF16)
        ht_ref[...] = hf.T.astype(BF16)
        gate = _mm_nt(h, wg_ref[...])
        up = _mm_nt(h, wu_ref[...])
        gate_ref[...] = gate.astype(STASH)
        up_ref[...] = up.astype(STASH)
        act = (gate * _sigmoid(gate) * up).astype(BF16)
        act_ref[...] = act
        x2 = xv + jnp.dot(act, wd_ref[...], preferred_element_type=F32)

        gv = gf_ref[...]
        r2 = lax.rsqrt(jnp.mean(x2 * x2, axis=-1, keepdims=True) + EPS)
        xh = x2 * r2
        err = xh * gv - t_ref[...]
        loss_ref[...] += 0.5 * jnp.sum(jnp.mean(err * err, axis=-1, keepdims=True), axis=0, keepdims=True)
        dy = err * (1.0 / D_MODEL)
        dgf_ref[...] += jnp.sum(dy * xh, axis=0, keepdims=True)
        w = dy * gv
        dx2 = r2 * (w - xh * jnp.mean(w * xh, axis=-1, keepdims=True))
        dx2_ref[...] = dx2
        dx2t_ref[...] = dx2.T.astype(BF16)

    row = lambda n: pl.BlockSpec((tm, n), lambda i: (i, 0))
    return pl.pallas_call(
        body, name="ffn_fwd_loss", grid=(T // tm,),
        in_specs=[row(D_MODEL), _full((1, D_MODEL)), _resident(wg.shape), _resident(wu.shape), _resident(wd.shape),
                  row(D_MODEL), _full((1, D_MODEL))],
        out_specs=[_col(tm, D_MODEL), row(D_FF), row(D_FF), row(D_FF), _full((1, 128)), _full((1, D_MODEL)),
                   row(D_MODEL), _col(tm, D_MODEL)],
        out_shape=[jax.ShapeDtypeStruct((D_MODEL, T), BF16), jax.ShapeDtypeStruct((T, D_FF), STASH),
                   jax.ShapeDtypeStruct((T, D_FF), STASH), jax.ShapeDtypeStruct((T, D_FF), BF16),
                   jax.ShapeDtypeStruct((1, 128), F32), jax.ShapeDtypeStruct((1, D_MODEL), F32),
                   jax.ShapeDtypeStruct((T, D_MODEL), F32), jax.ShapeDtypeStruct((D_MODEL, T), BF16)],
        compiler_params=_params(("arbitrary",), vmem=VMEM_LIMIT_LARGE_V7X),
    )(x1, g, wg, wu, wd, target, g_fin)


def _ffn_bwd(dx2, x1, gate, up, g, wg, wu, wd):
    T = x1.shape[0]
    tm = min(512, T)

    def body(dx2_ref, x_ref, gate_ref, up_ref, g_ref, wg_ref, wu_ref, wd_ref, dgate_ref, dup_ref, dx1_ref, dgn_ref):
        @pl.when(pl.program_id(0) == 0)
        def _():
            dgn_ref[...] = jnp.zeros_like(dgn_ref)

        dx2 = dx2_ref[...]
        dact = _mm_nt(dx2, wd_ref[...])
        gate = gate_ref[...].astype(F32)
        s = _sigmoid(gate)
        dgate = (dact * up_ref[...].astype(F32) * (s * (1.0 + gate * (1.0 - s)))).astype(BF16)
        dup = (dact * (gate * s)).astype(BF16)
        dgate_ref[...] = dgate
        dup_ref[...] = dup
        dh = _mm(dgate, wg_ref[...]) + _mm(dup, wu_ref[...])
        xv = x_ref[...]
        r = lax.rsqrt(jnp.mean(xv * xv, axis=-1, keepdims=True) + EPS)
        xh = xv * r
        dgn_ref[...] += jnp.sum(dh * xh, axis=0, keepdims=True)
        w = dh * g_ref[...]
        dx1_ref[...] = dx2 + r * (w - xh * jnp.mean(w * xh, axis=-1, keepdims=True))

    row = lambda n: pl.BlockSpec((tm, n), lambda i: (i, 0))
    return pl.pallas_call(
        body, name="ffn_bwd", grid=(T // tm,),
        in_specs=[row(D_MODEL), row(D_MODEL), row(D_FF), row(D_FF), _full((1, D_MODEL)),
                  _resident(wg.shape), _resident(wu.shape), _resident(wd.shape)],
        out_specs=[row(D_FF), row(D_FF), row(D_MODEL), _full((1, D_MODEL))],
        out_shape=[jax.ShapeDtypeStruct((T, D_FF), BF16), jax.ShapeDtypeStruct((T, D_FF), BF16),
                   jax.ShapeDtypeStruct((T, D_MODEL), F32), jax.ShapeDtypeStruct((1, D_MODEL), F32)],
        compiler_params=_params(("arbitrary",), vmem=VMEM_LIMIT_LARGE_V7X),
    )(dx2, x1, gate, up, g, wg, wu, wd)


def _merge_bwd(dx1, og, cvo, gt, cv, wa, wb, wo, conv_w):
    T = dx1.shape[0]
    tm = min(512, T)
    nt = T // tm

    def body(dx_ref, og_ref, cvo_ref, gt_ref, cv_ref, halo_ref, wa_ref, wb_ref, wo_ref, cw_ref,
             dgt_ref, dya_ref, dyb_ref, dog_ref, dcv_ref, dcw_ref, prev_u, next_dy):
        step = pl.program_id(0)

        @pl.when(step == 0)
        def _():
            next_dy[...] = jnp.zeros_like(next_dy)
            dcw_ref[...] = jnp.zeros_like(dcw_ref)

        dm = _mm_nt(dx_ref[...], wo_ref[...])
        wa = _shard_cols(wa_ref)
        wb = _shard_cols(wb_ref)
        ya = jnp.dot(og_ref[...], wa, preferred_element_type=F32)
        yb = jnp.dot(cvo_ref[...], wb, preferred_element_type=F32)
        sa = _sigmoid(gt_ref[:, :D_MODEL].astype(F32))
        sb = _sigmoid(gt_ref[:, D_MODEL:].astype(F32))
        dgt_ref[:, :D_MODEL] = (dm * ya * (sa * (1.0 - sa))).astype(BF16)
        dgt_ref[:, D_MODEL:] = (dm * yb * (sb * (1.0 - sb))).astype(BF16)
        dya = (dm * sa).astype(BF16)
        dyb = (dm * sb).astype(BF16)
        dya_ref[...] = dya
        dyb_ref[...] = dyb
        dog_ref[...] = _mm_nt(dya, wa)
        dcvo = _mm_nt(dyb, wb)

        cvt = cv_ref[...].astype(F32)
        c, bg, xb = cvt[:, :CONV_WIDTH], cvt[:, CONV_WIDTH:2 * CONV_WIDTH], cvt[:, 2 * CONV_WIDTH:]
        halo = halo_ref[...].astype(F32)
        first_tile = step == nt - 1
        prev_u[...] = jnp.where(first_tile, 0.0, halo[:, :CONV_WIDTH] * halo[:, 2 * CONV_WIDTH:])
        u = c * xb
        row = lax.broadcasted_iota(jnp.int32, u.shape, 0)
        p1 = prev_u[HALO - 1:HALO, :]
        p2 = prev_u[HALO - 2:HALO - 1, :]
        u1 = jnp.where(row >= 1, pltpu.roll(u, 1, 0), p1)
        u2 = jnp.where(row >= 2, pltpu.roll(u, 2, 0), jnp.where(row == 1, p1, p2))
        w0, w1, w2 = cw_ref[0:1, :], cw_ref[1:2, :], cw_ref[2:3, :]
        y = w0 * u2 + w1 * u1 + w2 * u
        dcv_ref[:, CONV_WIDTH:2 * CONV_WIDTH] = (dcvo * y).astype(BF16)
        dy = dcvo * bg
        dcw_ref[0:1, :] += jnp.sum(dy * u2, axis=0, keepdims=True)
        dcw_ref[1:2, :] += jnp.sum(dy * u1, axis=0, keepdims=True)
        dcw_ref[2:3, :] += jnp.sum(dy * u, axis=0, keepdims=True)
        n1 = next_dy[0:1, :]
        n2 = next_dy[1:2, :]
        dy1 = jnp.where(row < tm - 1, pltpu.roll(dy, tm - 1, 0), n1)
        dy2 = jnp.where(row < tm - 2, pltpu.roll(dy, tm - 2, 0), jnp.where(row == tm - 2, n1, n2))
        du = w2 * dy + w1 * dy1 + w0 * dy2
        dcv_ref[:, :CONV_WIDTH] = (du * xb).astype(BF16)
        dcv_ref[:, 2 * CONV_WIDTH:] = (du * c).astype(BF16)
        next_dy[...] = dy[:HALO, :]

    rt = lambda i: nt - 1 - i
    row = lambda n: pl.BlockSpec((tm, n), lambda i: (rt(i), 0))
    halo = pl.BlockSpec((HALO, N_CV), lambda i: (jnp.maximum(rt(i) * (tm // HALO) - 1, 0), 0))
    return pl.pallas_call(
        body, name="merge_bwd", grid=(nt,),
        in_specs=[row(D_MODEL), row(HG_WIDTH), row(CONV_WIDTH), row(2 * D_MODEL), row(N_CV), halo,
                  _resident(wa.shape), _resident(wb.shape), _resident(wo.shape), _full((CONV_K, CONV_WIDTH))],
        out_specs=[row(2 * D_MODEL), row(D_MODEL), row(D_MODEL), row(HG_WIDTH), row(N_CV),
                   _full((CONV_K, CONV_WIDTH))],
        out_shape=[jax.ShapeDtypeStruct((T, 2 * D_MODEL), BF16), jax.ShapeDtypeStruct((T, D_MODEL), BF16),
                   jax.ShapeDtypeStruct((T, D_MODEL), BF16), jax.ShapeDtypeStruct((T, HG_WIDTH), F32),
                   jax.ShapeDtypeStruct((T, N_CV), BF16), jax.ShapeDtypeStruct((CONV_K, CONV_WIDTH), F32)],
        scratch_shapes=[pltpu.VMEM((HALO, CONV_WIDTH), F32), pltpu.VMEM((HALO, CONV_WIDTH), F32)],
        compiler_params=_params(("arbitrary",)),
    )(dx1, og, cvo, gt, cv, cv, wa, wb, wo, conv_w)


def _drop_operands(body, first, count):
    def wrapped(*refs):
        return body(*refs[:first], *refs[first + count:])
    return wrapped


def _hg_bwd(dog, hg, o, st, low, gn, after=()):
    T = hg.shape[0]
    tb = min(512, T)
    sb = min(256, tb)
    nb = T // tb
    nc = tb // CHUNK
    wid = HEADS_PER_STEP * HEAD_DIM

    def body(q_ref, f_ref, i_ref, g_ref, low_ref, gn_ref, o_ref, dog_ref, st_ref,
             dhg_ref, dlow_ref, dgn_ref,
             ds_scr, dqi_scr, dko_scr, dv_scr, dd_scr, dqh_scr, dkh_scr):
        h = pl.program_id(0)
        t = pl.program_id(1)
        dq_ref, df_ref, di_ref, dg_ref = (dhg_ref.at[:, p * HG_WIDTH:(p + 1) * HG_WIDTH] for p in range(4))

        @pl.when(t == 0)
        def _():
            ds_scr[...] = jnp.zeros_like(ds_scr)
            dlow_ref[...] = jnp.zeros_like(dlow_ref)

        @pl.when((t == 0) & (h == 0))
        def _():
            dgn_ref[...] = jnp.zeros_like(dgn_ref)

        pos = _chunk_pos((tb, HEAD_DIM))
        mask = _intra_mask(sb)
        gnv = gn_ref[...]
        lanes = [slice(hh * HEAD_DIM, (hh + 1) * HEAD_DIM) for hh in range(HEADS_PER_STEP)]
        heads = []
        for hh, ln in enumerate(lanes):
            lb, lb1 = _lower_bound(low_ref.at[:, ln])
            qr = q_ref[:, ln]
            sq, q, sg, f, k, e_qa, e_ka, e_b, e_ko, dec = _hg_gates(qr, f_ref[:, ln], lb, pos, tb)

            gr = g_ref[:, ln]
            o = o_ref[:, ln]
            dog_v = dog_ref[:, ln]
            sgr = _sigmoid(gr)
            r = lax.rsqrt(jnp.mean(o * o, axis=-1, keepdims=True) + EPS)
            oh = o * r
            dg_ref[:, ln] = (dog_v * (oh * gnv) * (sgr * (1.0 + gr * (1.0 - sgr)))).astype(BF16)
            don = dog_v * (gr * sgr)
            dgn_ref[...] += jnp.sum(don * oh, axis=0, keepdims=True)
            w = don * gnv
            do = (r * (w - oh * jnp.mean(w * oh, axis=-1, keepdims=True))).astype(BF16)

            qh = (q * e_qa).astype(BF16)
            kh = (k * e_ka).astype(BF16)
            qi = (q * e_b).astype(BF16)
            ko = (k * e_ko).astype(BF16)
            vb = i_ref[:, ln].astype(BF16)

            for s in range(tb // sb):
                sl = slice(s * sb, (s + 1) * sb)
                p = jnp.where(mask, _mm_nt(qh[sl], kh[sl]), 0.0).astype(BF16)
                dp = jnp.where(mask, _mm_nt(do[sl], vb[sl]), 0.0).astype(BF16)
                dv_scr[sl, ln] = _mm_tn(p, do[sl])
                dqh_scr[sl, ln] = _mm(dp, kh[sl])
                dkh_scr[sl, ln] = _mm_tn(dp, qh[sl])
            heads.append(dict(lb=lb, lb1=lb1, qr=qr, sq=sq, q=q, sg=sg, f=f, k=k, e_qa=e_qa, e_ka=e_ka, e_b=e_b,
                              e_ko=e_ko, dec=dec, do=do, qi=qi, ko=ko, vb=vb, ds=ds_scr[hh]))

        for c in reversed(range(nc)):
            sl = slice(c * CHUNK, (c + 1) * CHUNK)
            for hh, ln in enumerate(lanes):
                hd = heads[hh]
                ds = hd["ds"]
                st_c = st_ref[hh, c]
                dqi_scr[sl, ln] = _mm(hd["do"][sl], st_c)
                dko_scr[sl, ln] = _mm(hd["vb"][sl], ds)
                dv_scr[sl, ln] = dv_scr[sl, ln] + _mm_nt(hd["ko"][sl], ds)
                dec_c = hd["dec"][c]
                dd_scr[sl, ln] = jnp.broadcast_to(dec_c * jnp.sum(ds * st_c, axis=0, keepdims=True),
                                                  (CHUNK, HEAD_DIM))
                hd["ds"] = dec_c * ds + _mm_tn(hd["do"][sl], hd["qi"][sl])

        for hh, ln in enumerate(lanes):
            hd = heads[hh]
            ds_scr[hh] = hd["ds"]
            q, k, lb = hd["q"], hd["k"], hd["lb"]
            dko_e = dko_scr[:, ln] * hd["e_ko"]
            dq = dqh_scr[:, ln] * hd["e_qa"] + dqi_scr[:, ln] * hd["e_b"]
            dk = dkh_scr[:, ln] * hd["e_ka"] + dko_e
            kd3 = (k * dko_e).reshape(nc, CHUNK, HEAD_DIM)
            last = jnp.broadcast_to(jnp.sum(kd3, axis=1, keepdims=True), kd3.shape).reshape(tb, HEAD_DIM)
            db = q * dq - k * dk + jnp.where(pos == CHUNK - 1, dd_scr[:, ln] + last, 0.0)
            dlg = _chunk_rev_cumsum(db, pos)
            dfv = dlg / hd["f"] - dk
            s_low = jnp.sum(dfv * (1.0 - hd["sg"]), axis=0, keepdims=True)
            dlow_ref[0:1, ln] += s_low * lb * (1.0 - lb)
            dlow_ref[1:2, ln] += -s_low * lb * hd["lb1"]
            df_ref[:, ln] = (dfv * (1.0 - lb) * hd["sg"] * (1.0 - hd["sg"])).astype(BF16)
            dq_ref[:, ln] = (dq * Q_SCALE * (hd["sq"] * (1.0 + hd["qr"] * (1.0 - hd["sq"])))).astype(BF16)
            di_ref[:, ln] = dv_scr[:, ln].astype(BF16)

    rt = lambda t: nb - 1 - t
    col = lambda p: pl.BlockSpec((tb, wid), lambda h, t: (rt(t), p * HEAD_GROUPS + h))
    hcol = pl.BlockSpec((tb, wid), lambda h, t: (rt(t), h))
    assert HEAD_GROUPS == 1
    tile = pltpu.VMEM((tb, wid), F32)
    return pl.pallas_call(
        _drop_operands(body, 9, len(after)), name="hg_bwd", grid=(HEAD_GROUPS, nb),
        in_specs=[col(0), col(1), col(2), col(3), pl.BlockSpec((2, wid), lambda h, t: (0, h)),
                  pl.BlockSpec((1, HEAD_DIM), lambda h, t: (0, 0)), hcol, hcol,
                  pl.BlockSpec((HEADS_PER_STEP, nc, HEAD_DIM, HEAD_DIM), lambda h, t: (h, rt(t), 0, 0))]
                 + [HBM_SPEC] * len(after),
        out_specs=[pl.BlockSpec((tb, N_HG), lambda h, t: (rt(t), 0)), pl.BlockSpec((2, wid), lambda h, t: (0, h)),
                   pl.BlockSpec((1, HEAD_DIM), lambda h, t: (0, 0))],
        out_shape=[jax.ShapeDtypeStruct((T, N_HG), BF16), jax.ShapeDtypeStruct((2, HG_WIDTH), F32),
                   jax.ShapeDtypeStruct((1, HEAD_DIM), F32)],
        scratch_shapes=[pltpu.VMEM((HEADS_PER_STEP, HEAD_DIM, HEAD_DIM), F32), tile, tile, tile, tile, tile, tile],
        compiler_params=_params(("arbitrary", "arbitrary")),
    )(hg, hg, hg, hg, low, gn, o, dog, st, *after)


def _in_bwd(dparts, w_in, x, dx1, g, after=()):
    T = x.shape[0]
    tm = min(512, T)
    widths = [p.shape[1] for p in dparts]
    offs = [sum(widths[:i]) for i in range(len(widths))]
    n = len(dparts)

    def body(*refs):
        d_refs = refs[:n]
        w_ref, x_ref, dx1_ref, g_ref, dx_ref, dgn_ref = refs[n:]

        @pl.when(pl.program_id(0) == 0)
        def _():
            dgn_ref[...] = jnp.zeros_like(dgn_ref)

        dh = None
        for d_ref, off, wd in zip(d_refs, offs, widths):
            part = _mm(d_ref[...], w_ref[off:off + wd, :])
            dh = part if dh is None else dh + part
        xv = x_ref[...]
        r = lax.rsqrt(jnp.mean(xv * xv, axis=-1, keepdims=True) + EPS)
        xh = xv * r
        dgn_ref[...] += jnp.sum(dh * xh, axis=0, keepdims=True)
        w = dh * g_ref[...]
        dx_ref[...] = dx1_ref[...] + r * (w - xh * jnp.mean(w * xh, axis=-1, keepdims=True))

    row = lambda m: pl.BlockSpec((tm, m), lambda i: (i, 0))
    return pl.pallas_call(
        _drop_operands(body, n + 4, len(after)), name="in_bwd", grid=(T // tm,),
        in_specs=[row(wd) for wd in widths] + [_resident(w_in.shape), row(D_MODEL), row(D_MODEL), _full((1, D_MODEL))]
                 + [HBM_SPEC] * len(after),
        out_specs=[row(D_MODEL), _full((1, D_MODEL))],
        out_shape=[jax.ShapeDtypeStruct((T, D_MODEL), F32), jax.ShapeDtypeStruct((1, D_MODEL), F32)],
        compiler_params=_params(("arbitrary",)),
    )(*dparts, w_in, x, dx1, g, *after)


def _wgrad(name, at, b, tn, transposed=False, tk=2048, after=()):
    M, T = at.shape
    N = b.shape[1]
    tk = min(tk, T)
    nk = T // tk
    if transposed:
        out_spec, out_shape = pl.BlockSpec((tn, M), lambda j, k: (j, 0)), (N, M)
    else:
        out_spec, out_shape = pl.BlockSpec((M, tn), lambda j, k: (0, j)), (M, N)

    if nk == 1:
        def body1(a_ref, b_ref, o_ref):
            part = _mm(a_ref[...], b_ref[...])
            o_ref[...] = (part.T if transposed else part).astype(BF16)

        return pl.pallas_call(
            _drop_operands(body1, 2, len(after)), name=name, grid=(N // tn, 1),
            in_specs=[_resident((M, T)), pl.BlockSpec((T, tn), lambda j, k: (0, j))] + [HBM_SPEC] * len(after),
            out_specs=out_spec, out_shape=jax.ShapeDtypeStruct(out_shape, BF16),
            compiler_params=_params(("parallel", "arbitrary")),
        )(at, b, *after)

    def body(a_ref, b_ref, o_ref, acc):
        k = pl.program_id(1)

        @pl.when(k == 0)
        def _():
            acc[...] = jnp.zeros_like(acc)

        acc[...] += _mm(a_ref[...], b_ref[...])

        @pl.when(k == nk - 1)
        def _():
            o_ref[...] = (acc[...].T if transposed else acc[...]).astype(BF16)

    return pl.pallas_call(
        _drop_operands(body, 2, len(after)), name=name, grid=(N // tn, nk),
        in_specs=[pl.BlockSpec((M, tk), lambda j, k: (0, k)), pl.BlockSpec((tk, tn), lambda j, k: (k, j))]
                 + [HBM_SPEC] * len(after),
        out_specs=out_spec, out_shape=jax.ShapeDtypeStruct(out_shape, BF16),
        scratch_shapes=[pltpu.VMEM((M, tn), F32)],
        compiler_params=_params(("parallel", "arbitrary")),
    )(at, b, *after)


def _wgrad_branches(ogt, dya, cvot, dyb):
    M, T = ogt.shape
    N = dya.shape[1]
    c = N // N_DEV
    per = 2
    tn = per * c

    def body(at_ref, da_ref, bt_ref, db_ref, oa_ref, ob_ref):
        ga = _mm(at_ref[...], da_ref[...])
        gb = _mm(bt_ref[...], db_ref[...])
        for s in range(per):
            oa_ref[s] = ga[:, s * c:(s + 1) * c].astype(BF16)
            ob_ref[s] = gb[:, s * c:(s + 1) * c].astype(BF16)

    rhs = pl.BlockSpec((T, tn), lambda j: (0, j))
    owners = pl.BlockSpec((per, M, c), lambda j: (j, 0, 0))
    out = jax.ShapeDtypeStruct((N_DEV, M, c), BF16)
    return pl.pallas_call(
        body, name="wgrad_branches", grid=(N // tn,),
        in_specs=[_resident((M, T)), rhs, _resident((M, T)), rhs], out_specs=[owners] * 2, out_shape=[out, out],
        compiler_params=_params(("parallel",)),
    )(ogt, dya, cvot, dyb)


def _wgrad_in(ht, dparts, after=()):
    M, T = ht.shape
    tn = 512
    nblk = [p.shape[1] // tn for p in dparts]
    start = [sum(nblk[:i]) for i in range(len(nblk))]
    n = len(dparts)

    def body(a_ref, *refs):
        d_refs, o_ref = refs[:n], refs[n]
        j = pl.program_id(0)
        for d_ref, s, nb in zip(d_refs, start, nblk):
            @pl.when((j >= s) & (j < s + nb))
            def _():
                o_ref[...] = _mm(a_ref[...], d_ref[...]).T.astype(BF16)

    def piece_spec(s, nb):
        return pl.BlockSpec((T, tn), lambda j: (0, jnp.clip(j - s, 0, nb - 1)))

    return pl.pallas_call(
        _drop_operands(body, 1 + n, len(after)), name="wgrad_in", grid=(sum(nblk),),
        in_specs=[_resident((M, T))] + [piece_spec(s, nb) for s, nb in zip(start, nblk)] + [HBM_SPEC] * len(after),
        out_specs=pl.BlockSpec((tn, M), lambda j: (j, 0)),
        out_shape=jax.ShapeDtypeStruct((sum(nblk) * tn, M), BF16),
        compiler_params=_params(("parallel",)),
    )(ht, *dparts, *after)


def _adamw_math(w, g, m, v):
    m = ADAM_B1 * m + (1.0 - ADAM_B1) * g
    v = ADAM_B2 * v + (1.0 - ADAM_B2) * (g * g)
    m_hat = m / (1.0 - ADAM_B1 ** ADAM_STEP)
    v_hat = v / (1.0 - ADAM_B2 ** ADAM_STEP)
    delta = -ADAM_LR * (m_hat / (jnp.sqrt(v_hat) + ADAM_EPS) + ADAM_WD * w)
    return delta, m, v


def _adamw_sum(name, w, parts, m, v):
    R, C = w.shape
    tr = _row_tile(R)

    def body(w_ref, p_ref, m_ref, v_ref, g_out, d_out, m_out, v_out):
        g = p_ref[0].astype(F32)
        for k in range(1, 4):
            g = g + p_ref[k].astype(F32)
        g_out[...] = g
        d_out[...], m_out[...], v_out[...] = _adamw_math(w_ref[...], g, m_ref[...], v_ref[...])

    blk = pl.BlockSpec((tr, C), lambda i: (i, 0))
    out = jax.ShapeDtypeStruct((R, C), F32)
    return pl.pallas_call(
        body, name=name, grid=(R // tr,),
        in_specs=[blk, pl.BlockSpec((4, tr, C), lambda i: (0, i, 0)), blk, blk],
        out_specs=[blk, blk, blk, blk], out_shape=[out, out, out, out],
        compiler_params=_params(("parallel",)),
    )(w, parts, m, v)


_SMALL_SLOTS = (("norm_mix_g", 0, 1, 1024), ("norm_ffn_g", 1, 1, 1024), ("norm_final_g", 2, 1, 1024),
                ("lower_bounds", 3, 2, 512), ("hg_norm_g", 5, 1, 128), ("loss", 6, 1, 128), ("conv_w", 8, 3, 512))
_SMALL_PARAMS = tuple(s for s in _SMALL_SLOTS if s[0] != "loss")
CONV_SHARD = CONV_WIDTH // N_DEV


def _small_pack(small):
    def body(*refs):
        out = refs[-1]
        out[...] = jnp.zeros_like(out)
        for ref, (_, row, rows, lanes) in zip(refs[:-1], _SMALL_SLOTS):
            out[row:row + rows, 0:lanes] = ref[...]

    vmem = pl.BlockSpec(memory_space=pltpu.VMEM)
    return pl.pallas_call(
        body, name="small_pack", in_specs=[vmem] * len(_SMALL_SLOTS), out_specs=vmem,
        out_shape=jax.ShapeDtypeStruct((SMALL_ROWS, 1024), F32),
    )(*[small[name] for name, _, _, _ in _SMALL_SLOTS])


def _small_update(gathered, dev, w, m, v):
    n = len(_SMALL_PARAMS)

    def body(dev_ref, g_ref, *refs):
        w_refs, m_refs, v_refs = refs[:n], refs[n:2 * n], refs[2 * n:3 * n]
        loss_ref, out_refs, sum_scr = refs[3 * n], refs[3 * n + 1:-1], refs[-1]
        total = g_ref[0]
        for k in range(1, N_DEV):
            total = total + g_ref[k]
        sum_scr[...] = total
        loss_ref[...] = sum_scr[6:7, 0:128]
        for p, (name, row, rows, lanes) in enumerate(_SMALL_PARAMS):
            if name == "conv_w":
                g = sum_scr[row:row + rows, 0:CONV_SHARD]
                for s in range(1, N_DEV):
                    g = jnp.where(dev_ref[0] == s, sum_scr[row:row + rows, s * CONV_SHARD:(s + 1) * CONV_SHARD], g)
            else:
                g = sum_scr[row:row + rows, 0:lanes]
            delta, m_new, v_new = _adamw_math(w_refs[p][...], g, m_refs[p][...], v_refs[p][...])
            out_refs[4 * p][...] = g
            out_refs[4 * p + 1][...] = delta
            out_refs[4 * p + 2][...] = m_new
            out_refs[4 * p + 3][...] = v_new

    vmem = pl.BlockSpec(memory_space=pltpu.VMEM)
    outs = [jax.ShapeDtypeStruct((1, 128), F32)]
    for a in w:
        outs += [jax.ShapeDtypeStruct(a.shape, F32)] * 4
    return pl.pallas_call(
        body, name="small_update",
        in_specs=[pl.BlockSpec(memory_space=pltpu.SMEM)] + [vmem] * (1 + 3 * n), out_specs=[vmem] * len(outs),
        out_shape=outs, scratch_shapes=[pltpu.VMEM((SMALL_ROWS, 1024), F32)],
    )(dev, gathered, *w, *m, *v)


def _row_tile(rows):
    for parts in (4, 2):
        if rows % (16 * parts) == 0:
            return rows // parts
    return rows


def _pair_sum(name, by_owner, got, core, after=()):
    n = len(got)

    def body(core_ref, *refs):
        for a_ref, b_ref, o_ref in zip(refs[:n], refs[n:2 * n], refs[2 * n:]):
            o_ref[...] = (a_ref[...].astype(F32) + b_ref[...].astype(F32)).astype(BF16)

    def blk(g):
        return pl.BlockSpec((None,) + g.shape[1:], lambda k, core_ref: (k, 0, 0))

    def mine(g):
        return pl.BlockSpec((None,) + g.shape[1:], lambda k, core_ref: (2 * k + core_ref[0], 0, 0))

    return pl.pallas_call(
        _drop_operands(body, 1 + 2 * n, len(after)), name=name,
        grid_spec=pltpu.PrefetchScalarGridSpec(
            num_scalar_prefetch=1, grid=(4,),
            in_specs=[mine(g) for g in got] + [blk(g) for g in got] + [HBM_SPEC] * len(after),
            out_specs=[blk(g) for g in got]),
        out_shape=[jax.ShapeDtypeStruct(g.shape, BF16) for g in got],
        compiler_params=_params(("parallel",)),
    )(core, *by_owner, *got, *after)


MESH = pl.DeviceIdType.MESH
HBM_SPEC = pl.BlockSpec(memory_space=pl.ANY)


def _handshake(peers):
    barrier = pltpu.get_barrier_semaphore()
    for peer in peers:
        pl.semaphore_signal(barrier, inc=1, device_id=peer, device_id_type=MESH)
    pl.semaphore_wait(barrier, len(peers))


def _comm_call(body, name, operands, out_shape, scratch, collective_id):
    if collective_id is None:
        return pl.pallas_call(body, name=name, in_specs=[HBM_SPEC] * len(operands), out_specs=[HBM_SPEC] * len(out_shape),
                              out_shape=out_shape, scratch_shapes=scratch)(*operands)
    return pl.kernel(body, out_type=out_shape, mesh=plsc.ScalarSubcoreMesh(axis_name="sequencer", num_cores=1),
                     scratch_types=scratch, name=name,
                     compiler_params=pltpu.CompilerParams(collective_id=collective_id))(*operands)


def _all_gather(name, blocks, collective_id=None, after=()):
    n = len(blocks)
    na = len(after)

    def body(*refs):
        x_refs, out_refs = refs[:n], refs[n + na:2 * n + na]
        send_sems, recv_sems, local_sems = refs[2 * n + na:]
        x, y, c = lax.axis_index("x"), lax.axis_index("y"), lax.axis_index("c")
        me, sibling = (x, y, c), (x, y, 1 - c)
        chips = [(1 - x, y), (x, 1 - y), (1 - x, 1 - y)]
        if collective_id is not None:
            _handshake([sibling] + [(*chip, c) for chip in chips])

        def slot(i, px, py, pc):
            return out_refs[i].at[4 * px + 2 * py + pc]

        def copy(i, k, blk, to, src=None):
            return pltpu.make_async_remote_copy(
                src_ref=slot(i, *blk) if src is None else src, dst_ref=slot(i, *blk),
                send_sem=send_sems.at[7 * i + k], recv_sem=recv_sems.at[7 * i + k], device_id=to, device_id_type=MESH)

        mine = [pltpu.make_async_copy(x_refs[i], slot(i, *me), local_sems.at[i]) for i in range(n)]
        for cp in mine:
            cp.start()
        first = []
        for i in range(n):
            first.append(copy(i, 0, me, sibling, src=x_refs[i]))
            first += [copy(i, 1 + j, me, (*chip, c), src=x_refs[i]) for j, chip in enumerate(chips)]
        for cp in first:
            cp.start()
        passed = []
        for i in range(n):
            for j, chip in enumerate(chips):
                copy(i, 1 + j, (*chip, c), me).wait_recv()
                passed.append(copy(i, 4 + j, (*chip, c), sibling))
                passed[-1].start()
        for i in range(n):
            copy(i, 0, sibling, me).wait_recv()
            for j, chip in enumerate(chips):
                copy(i, 4 + j, (*chip, 1 - c), me).wait_recv()
        for cp in first + passed:
            cp.wait_send()
        for cp in mine:
            cp.wait()

    return _comm_call(
        body, name, list(blocks) + list(after), [jax.ShapeDtypeStruct((N_DEV,) + b.shape, b.dtype) for b in blocks],
        [pltpu.SemaphoreType.DMA((7 * n,)), pltpu.SemaphoreType.DMA((7 * n,)), pltpu.SemaphoreType.DMA((n,))],
        collective_id)


def _sibling_swap(name, by_owner, collective_id=None, after=()):
    n = len(by_owner)
    na = len(after)

    def body(*refs):
        x_refs, out_refs = refs[:n], refs[n + na:2 * n + na]
        send_sems, recv_sems = refs[2 * n + na:]
        x, y, c = lax.axis_index("x"), lax.axis_index("y"), lax.axis_index("c")
        if collective_id is not None:
            _handshake([(x, y, 1 - c)])
        copies = []
        for i in range(n):
            for k in range(4):
                copies.append(pltpu.make_async_remote_copy(
                    src_ref=x_refs[i].at[2 * k + 1 - c], dst_ref=out_refs[i].at[k],
                    send_sem=send_sems.at[4 * i + k], recv_sem=recv_sems.at[4 * i + k],
                    device_id=(x, y, 1 - c), device_id_type=MESH))
        for cp in copies:
            cp.start()
        for cp in copies:
            cp.wait()

    return _comm_call(
        body, name, list(by_owner) + list(after),
        [jax.ShapeDtypeStruct((4,) + b.shape[1:], b.dtype) for b in by_owner],
        [pltpu.SemaphoreType.DMA((4 * n,)), pltpu.SemaphoreType.DMA((4 * n,))], collective_id)


def _chip_exchange(name, sums, collective_id=None, after=()):
    n = len(sums)
    na = len(after)

    def body(*refs):
        x_refs, out_refs = refs[:n], refs[n + na:2 * n + na]
        send_sems, recv_sems, local_sems = refs[2 * n + na:]
        x, y, c = lax.axis_index("x"), lax.axis_index("y"), lax.axis_index("c")
        chips = [(1 - x, y), (x, 1 - y), (1 - x, 1 - y)]
        my_chip = 2 * x + y
        if collective_id is not None:
            _handshake([(cx, cy, c) for cx, cy in chips])
        mine = [pltpu.make_async_copy(x_refs[i].at[my_chip], out_refs[i].at[my_chip], local_sems.at[i])
                for i in range(n)]
        for cp in mine:
            cp.start()
        sends = []
        for i in range(n):
            for j, (cx, cy) in enumerate(chips):
                sends.append(pltpu.make_async_remote_copy(
                    src_ref=x_refs[i].at[2 * cx + cy], dst_ref=out_refs[i].at[my_chip],
                    send_sem=send_sems.at[3 * i + j], recv_sem=recv_sems.at[3 * i + j],
                    device_id=(cx, cy, c), device_id_type=MESH))
        for cp in sends:
            cp.start()
        for i in range(n):
            for j, (cx, cy) in enumerate(chips):
                pltpu.make_async_remote_copy(
                    src_ref=x_refs[i].at[my_chip], dst_ref=out_refs[i].at[2 * cx + cy],
                    send_sem=send_sems.at[3 * i + j], recv_sem=recv_sems.at[3 * i + j],
                    device_id=(cx, cy, c), device_id_type=MESH).wait_recv()
        for cp in sends:
            cp.wait_send()
        for cp in mine:
            cp.wait()

    return _comm_call(
        body, name, list(sums) + list(after), [jax.ShapeDtypeStruct(s.shape, s.dtype) for s in sums],
        [pltpu.SemaphoreType.DMA((3 * n,)), pltpu.SemaphoreType.DMA((3 * n,)), pltpu.SemaphoreType.DMA((n,))],
        collective_id)


def _cast_shards(shards):
    n = len(shards)

    def body(*refs):
        for i in range(n):
            refs[n + i][...] = refs[i][...].astype(BF16)

    vmem = pl.BlockSpec(memory_space=pltpu.VMEM)
    return pl.pallas_call(
        body, name="cast_shards", in_specs=[vmem] * n, out_specs=[vmem] * n,
        out_shape=[jax.ShapeDtypeStruct(s.shape, BF16) for s in shards],
        compiler_params=pltpu.CompilerParams(vmem_limit_bytes=VMEM_LIMIT_V7X),
    )(*shards)


BIG = ("w_in", "w_branch_a", "w_branch_b", "w_out", "w_ffn_gate", "w_ffn_up", "w_ffn_down")


def _local_step(x, target, gains, low, conv_w, wg8, reduce):
    g_mix, g_hg, g_ffn, g_fin = gains
    w_in = wg8["w_in"].reshape(N_IN, D_MODEL)
    wg = wg8["w_ffn_gate"].reshape(D_FF, D_MODEL)
    wu = wg8["w_ffn_up"].reshape(D_FF, D_MODEL)
    wa, wb = wg8["w_branch_a"], wg8["w_branch_b"]
    wo = wg8["w_out"].reshape(D_MODEL, D_MODEL)
    wd = wg8["w_ffn_down"].reshape(D_FF, D_MODEL)

    ht, hg, cv, gt, cvo, cvot = _fwd_in(x, g_mix, w_in, conv_w)
    o, og, ogt, st = _hg_fwd(hg, low, g_hg)
    x1, mgt = _merge_fwd(og, cvo, gt, x, wa, wb, wo)
    h2t, gate, up, act, loss, d_gfin, dx2, dx2t = _ffn_fwd_loss(x1, g_ffn, wg, wu, wd, target, g_fin)

    dgate, dup, dx1, d_gffn = _ffn_bwd(dx2, x1, gate, up, g_ffn, wg, wu, wd)
    ffn = dict(
        w_ffn_down=_wgrad("wgrad_ffn_down", dx2t, act, 256, transposed=True, tk=4096
                          ).reshape(N_DEV, D_FF // N_DEV, D_MODEL),
        w_ffn_gate=_wgrad("wgrad_ffn_gate", h2t, dgate, 256, transposed=True, tk=4096
                          ).reshape(N_DEV, D_FF // N_DEV, D_MODEL),
        w_ffn_up=_wgrad("wgrad_ffn_up", h2t, dup, 256, transposed=True, tk=4096
                        ).reshape(N_DEV, D_FF // N_DEV, D_MODEL))
    dgt, dya, dyb, dog, dcv, d_conv = _merge_bwd(dx1, og, cvo, gt, cv, wa, wb, wo, conv_w)
    sums_ffn, got_ffn = reduce.begin(ffn, sum_after=[dya])
    parts_ffn, updated_ffn = reduce.finish(ffn, sums_ffn)
    grad_a, grad_b = _wgrad_branches(ogt, dya, cvot, dyb)
    out = dict(
        w_out=_wgrad("wgrad_out", mgt, dx1, 256, tk=4096, after=sums_ffn[:1]
                     ).reshape(N_DEV, D_MODEL // N_DEV, D_MODEL),
        w_branch_a=grad_a, w_branch_b=grad_b)
    dhg, d_low, d_ghg = _hg_bwd(dog, hg, o, st, low, g_hg, after=list(sums_ffn) + [out["w_out"]])
    sums_out, got_out = reduce.begin(out, after=[parts_ffn[0], dhg], sum_after=updated_ffn)
    parts_out, updated_out = reduce.finish(out, sums_out)
    dparts = [dhg, dcv, dgt]
    w_in_grad = dict(w_in=_wgrad_in(ht, dparts, after=sums_out[:1]).reshape(N_DEV, N_IN // N_DEV, D_MODEL))
    sums_in, _ = reduce.begin(w_in_grad, after=parts_out[:1], sum_after=updated_out)
    parts_in, _ = reduce.finish(w_in_grad, sums_in)
    grad_x, d_gmix = _in_bwd(dparts, w_in, x, dx1, g_mix, after=list(parts_out[:1]) + list(sums_in))
    small = dict(norm_mix_g=d_gmix, norm_ffn_g=d_gffn, norm_final_g=d_gfin, lower_bounds=d_low, hg_norm_g=d_ghg,
                 conv_w=d_conv, loss=loss)
    return grad_x, small, parts_in


def _conv_shard_rows(a):
    return jnp.pad(a, ((0, 5), (0, 64)))


def kernel(x, norm_mix_g, w_in, lower_bounds, hg_norm_g, conv_w, w_branch_a, w_branch_b, w_out, norm_ffn_g, w_ffn_gate, w_ffn_up, w_ffn_down, norm_final_g, loss_target, m_norm_mix_g, m_w_in, m_lower_bounds, m_hg_norm_g, m_conv_w, m_w_branch_a, m_w_branch_b, m_w_out, m_norm_ffn_g, m_w_ffn_gate, m_w_ffn_up, m_w_ffn_down, m_norm_final_g, v_norm_mix_g, v_w_in, v_lower_bounds, v_hg_norm_g, v_conv_w, v_w_branch_a, v_w_branch_b, v_w_out, v_norm_ffn_g, v_w_ffn_gate, v_w_ffn_up, v_w_ffn_down, v_norm_final_g):
    cx, cy, cc = lax.axis_index("x"), lax.axis_index("y"), lax.axis_index("c")
    my_dev = 4 * cx + 2 * cy + cc

    def tr(a):
        return a[0].T

    big = dict(w_in=tr(w_in), w_branch_a=w_branch_a[0], w_branch_b=w_branch_b[0], w_out=w_out[0],
               w_ffn_gate=tr(w_ffn_gate), w_ffn_up=tr(w_ffn_up), w_ffn_down=w_ffn_down[0])
    big_m = dict(w_in=tr(m_w_in), w_branch_a=m_w_branch_a[0], w_branch_b=m_w_branch_b[0], w_out=m_w_out[0],
                 w_ffn_gate=tr(m_w_ffn_gate), w_ffn_up=tr(m_w_ffn_up), w_ffn_down=m_w_ffn_down[0])
    big_v = dict(w_in=tr(v_w_in), w_branch_a=v_w_branch_a[0], w_branch_b=v_w_branch_b[0], w_out=v_w_out[0],
                 w_ffn_gate=tr(v_w_ffn_gate), w_ffn_up=tr(v_w_ffn_up), w_ffn_down=v_w_ffn_down[0])
    transposed = ("w_in", "w_ffn_gate", "w_ffn_up")

    shards = dict(zip(BIG, _cast_shards([big[n] for n in BIG])))
    first = _all_gather("gather_w_in", [shards["w_in"], _conv_shard_rows(conv_w[0])])
    ids = iter(range(1, 16))
    mid = _all_gather("gather_mid", [shards[n] for n in BIG[1:4]], collective_id=next(ids), after=first[1:])
    ffn = _all_gather("gather_ffn", [shards[n] for n in BIG[4:]], collective_id=next(ids), after=first[1:])
    wg8 = dict(zip(BIG, [first[0]] + list(mid) + list(ffn)))
    conv_full = first[1][:, :3, :64].transpose(1, 0, 2).reshape(3, CONV_WIDTH)

    core = cc.reshape(1).astype(jnp.int32)
    outs = {}

    class Reduce:
        @staticmethod
        def begin(grads, after=(), sum_after=()):
            names = list(grads)
            by_owner = [grads[n] for n in names]
            got = _sibling_swap("sibling_swap_" + names[0], by_owner, collective_id=next(ids), after=after)
            sums = _pair_sum("pair_sum_" + names[0], by_owner, got, core, after=sum_after)
            return sums, got

        @staticmethod
        def finish(grads, chip_sums, after=()):
            names = list(grads)
            parts = _chip_exchange("chip_exchange_" + names[0], chip_sums, collective_id=next(ids), after=after)
            for n, p in zip(names, parts):
                outs[n] = _adamw_sum("adamw_" + n, big[n], p, big_m[n], big_v[n])
            return parts, [outs[n][1] for n in names]

    gains = (norm_mix_g, hg_norm_g, norm_ffn_g, norm_final_g.reshape(1, D_MODEL))
    grad_x, small, last = _local_step(x[0], loss_target[0], gains, lower_bounds, conv_full, wg8, Reduce)

    small_all = _all_gather("gather_small", [_small_pack(small)], collective_id=next(ids), after=last[:1])

    def small_state(a):
        return [a[0], a[1], a[2].reshape(1, D_MODEL), a[3], a[4], a[5][0]]

    upd = _small_update(
        small_all[0], my_dev.reshape(1).astype(jnp.int32),
        small_state((norm_mix_g, norm_ffn_g, norm_final_g, lower_bounds, hg_norm_g, conv_w)),
        small_state((m_norm_mix_g, m_norm_ffn_g, m_norm_final_g, m_lower_bounds, m_hg_norm_g, m_conv_w)),
        small_state((v_norm_mix_g, v_norm_ffn_g, v_norm_final_g, v_lower_bounds, v_hg_norm_g, v_conv_w)))
    loss = upd[0][0, 0]
    small_shape = dict(norm_final_g=(D_MODEL,), conv_w=(1, 3, CONV_SHARD))
    for p, (name, _, _, _) in enumerate(_SMALL_PARAMS):
        outs[name] = [a.reshape(small_shape.get(name, a.shape)) for a in upd[1 + 4 * p:5 + 4 * p]]

    order = ["norm_mix_g", "w_in", "lower_bounds", "hg_norm_g", "conv_w", "w_branch_a", "w_branch_b", "w_out",
             "norm_ffn_g", "w_ffn_gate", "w_ffn_up", "w_ffn_down", "norm_final_g"]
    result = [loss, grad_x[None]]
    for k in range(4):
        for n in order:
            if n in BIG:
                result.append((outs[n][k].T if n in transposed else outs[n][k])[None])
            else:
                result.append(outs[n][k])
    return tuple(result)
```

```python
import jax
import jax.numpy as jnp
from jax import lax
from jax.experimental import pallas as pl
from jax.experimental.pallas import tpu as pltpu
from jax.experimental.pallas import tpu_sc as plsc

F32 = jnp.float32
BF16 = jnp.bfloat16
STASH = jnp.bfloat16

D_MODEL = 1024
HG_WIDTH = 512
HEAD_DIM = 128
N_HEADS = 4
HEADS_PER_STEP = 4
HEAD_GROUPS = N_HEADS // HEADS_PER_STEP
CONV_WIDTH = 512
CONV_K = 3
D_FF = 2816
CHUNK = 32
EPS = 1e-6
Q_SCALE = HEAD_DIM ** -0.5
N_DEV = 8

ADAM_LR = 0.001
ADAM_B1 = 0.9
ADAM_B2 = 0.999
ADAM_EPS = 1e-08
ADAM_WD = 0.01
ADAM_STEP = 10

VMEM_LIMIT_V7X = 56 * 1024 * 1024
VMEM_LIMIT_LARGE_V7X = 62 * 1024 * 1024

SMALL_ROWS = 16


def _params(sem, vmem=VMEM_LIMIT_V7X):
    return pltpu.CompilerParams(dimension_semantics=sem, vmem_limit_bytes=vmem)


def _mm(a, b):
    return jnp.dot(a.astype(BF16), b.astype(BF16), preferred_element_type=F32)


def _mm_nt(a, b):
    return lax.dot_general(a.astype(BF16), b.astype(BF16), (((1,), (1,)), ((), ())), preferred_element_type=F32)


def _mm_tn(a, b):
    return lax.dot_general(a.astype(BF16), b.astype(BF16), (((0,), (0,)), ((), ())), preferred_element_type=F32)


def _sigmoid(x):
    return 0.5 * jnp.tanh(0.5 * x) + 0.5


def _resident(shape):
    nd = len(shape)
    return pl.BlockSpec(shape, lambda *_: (0,) * nd, pipeline_mode=pl.Buffered(1))


def _full(shape):
    nd = len(shape)
    return pl.BlockSpec(shape, lambda *_: (0,) * nd)


def _shard_cols(w_ref):
    return jnp.concatenate([w_ref[s] for s in range(N_DEV)], axis=1)


N_HG = 4 * HG_WIDTH
N_CV = 3 * CONV_WIDTH
N_GT = 2 * D_MODEL
N_IN = N_HG + N_CV + N_GT


def _col(tm, n):
    return pl.BlockSpec((n, tm), lambda i: (0, i))


HALO = 8


def _fwd_in(x, g, w_in_t, conv_w):
    T = x.shape[0]
    tm = min(512, T)

    def body(x_ref, g_ref, w_ref, cw_ref, ht_ref, hg_ref, cv_ref, gt_ref, cvo_ref, cvot_ref, tail_scr):
        @pl.when(pl.program_id(0) == 0)
        def _():
            tail_scr[...] = jnp.zeros_like(tail_scr)

        xv = x_ref[...]
        r = lax.rsqrt(jnp.mean(xv * xv, axis=-1, keepdims=True) + EPS)
        hf = xv * r * g_ref[...]
        h = hf.astype(BF16)
        ht_ref[...] = hf.T.astype(BF16)
        hg_ref[...] = _mm_nt(h, w_ref[:N_HG, :])
        cv = _mm_nt(h, w_ref[N_HG:N_HG + N_CV, :])
        cv_ref[...] = cv.astype(STASH)
        gt_ref[...] = _mm_nt(h, w_ref[N_HG + N_CV:, :]).astype(STASH)

        u = cv[:, :CONV_WIDTH] * cv[:, 2 * CONV_WIDTH:]
        row = lax.broadcasted_iota(jnp.int32, u.shape, 0)
        prev1 = tail_scr[HALO - 1:HALO, :]
        prev2 = tail_scr[HALO - 2:HALO - 1, :]
        u1 = jnp.where(row >= 1, pltpu.roll(u, 1, 0), prev1)
        u2 = jnp.where(row >= 2, pltpu.roll(u, 2, 0), jnp.where(row == 1, prev1, prev2))
        y = cw_ref[0:1, :] * u2 + cw_ref[1:2, :] * u1 + cw_ref[2:3, :] * u
        out = cv[:, CONV_WIDTH:2 * CONV_WIDTH] * y
        cvo_ref[...] = out.astype(BF16)
        cvot_ref[...] = out.T.astype(BF16)
        tail_scr[...] = u[tm - HALO:, :]

    row = lambda n: pl.BlockSpec((tm, n), lambda i: (i, 0))
    return pl.pallas_call(
        body, name="fwd_in", grid=(T // tm,),
        in_specs=[row(D_MODEL), _full((1, D_MODEL)), _resident(w_in_t.shape), _full((CONV_K, CONV_WIDTH))],
        out_specs=[_col(tm, D_MODEL), row(N_HG), row(N_CV), row(N_GT), row(CONV_WIDTH), _col(tm, CONV_WIDTH)],
        out_shape=[jax.ShapeDtypeStruct((D_MODEL, T), BF16), jax.ShapeDtypeStruct((T, N_HG), F32),
                   jax.ShapeDtypeStruct((T, N_CV), STASH), jax.ShapeDtypeStruct((T, N_GT), STASH),
                   jax.ShapeDtypeStruct((T, CONV_WIDTH), BF16), jax.ShapeDtypeStruct((CONV_WIDTH, T), BF16)],
        scratch_shapes=[pltpu.VMEM((HALO, CONV_WIDTH), F32)],
        compiler_params=_params(("arbitrary",)),
    )(x, g, w_in_t, conv_w)


def _chunk_pos(shape):
    return lax.broadcasted_iota(jnp.int32, shape, 0) & (CHUNK - 1)


def _chunk_cumsum(x, pos):
    s = 1
    while s < CHUNK:
        x = x + jnp.where(pos >= s, pltpu.roll(x, s, 0), 0.0)
        s *= 2
    return x


def _chunk_rev_cumsum(x, pos):
    n = x.shape[0]
    s = 1
    while s < CHUNK:
        x = x + jnp.where(pos + s < CHUNK, pltpu.roll(x, n - s, 0), 0.0)
        s *= 2
    return x


def _lower_bound(low_ref):
    l0 = low_ref[0:1, :]
    l1 = low_ref[1:2, :]
    m = jnp.maximum(l0, l1)
    e0 = jnp.exp(l0 - m)
    e1 = jnp.exp(l1 - m)
    return e0 / (e0 + e1), e1 / (e0 + e1)


def _hg_gates(qr, fr, lb, pos, tb):
    sq = _sigmoid(qr)
    q = qr * sq * Q_SCALE
    sg = _sigmoid(fr)
    f = lb + (1.0 - lb) * sg
    k = 1.0 - f
    b = _chunk_cumsum(jnp.log(f), pos)
    b3 = b.reshape(tb // CHUNK, CHUNK, HEAD_DIM)
    anc = b3[:, CHUNK // 2 - 1:CHUNK // 2, :]
    last = b3[:, CHUNK - 1:CHUNK, :]
    d3 = b3 - anc
    e_qa3 = jnp.exp(d3)
    e_ka3 = jnp.exp(-d3)
    e_b3 = e_qa3 * jnp.exp(anc)
    e_ko3 = e_ka3 * jnp.exp(last - anc)
    dec = jnp.exp(last)
    flat = lambda a: a.reshape(tb, HEAD_DIM)
    return sq, q, sg, f, k, flat(e_qa3), flat(e_ka3), flat(e_b3), flat(e_ko3), dec


def _intra_mask(sb):
    r = lax.broadcasted_iota(jnp.int32, (sb, sb), 0)
    c = lax.broadcasted_iota(jnp.int32, (sb, sb), 1)
    return ((r // CHUNK) == (c // CHUNK)) & (c <= r)


def _hg_fwd(hg, low, gn):
    T = hg.shape[0]
    tb = min(1024, T)
    sb = min(256, tb)
    nb = T // tb
    nc = tb // CHUNK
    wid = HEADS_PER_STEP * HEAD_DIM

    def body(q_ref, f_ref, i_ref, g_ref, low_ref, gn_ref, o_ref, og_ref, ogt_ref, st_ref, s_scr):
        t = pl.program_id(1)

        @pl.when(t == 0)
        def _():
            s_scr[...] = jnp.zeros_like(s_scr)

        pos = _chunk_pos((tb, HEAD_DIM))
        mask = _intra_mask(sb)
        lanes = [slice(hh * HEAD_DIM, (hh + 1) * HEAD_DIM) for hh in range(HEADS_PER_STEP)]
        qi, ko, vb, dec, st = [], [], [], [], []
        for hh, ln in enumerate(lanes):
            lb, _ = _lower_bound(low_ref.at[:, ln])
            _, q, _, _, k, e_qa, e_ka, e_b, e_ko, dec_h = _hg_gates(q_ref[:, ln], f_ref[:, ln], lb, pos, tb)
            qh = (q * e_qa).astype(BF16)
            kh = (k * e_ka).astype(BF16)
            qi.append((q * e_b).astype(BF16))
            ko.append((k * e_ko).astype(BF16))
            vb.append(i_ref[:, ln].astype(BF16))
            dec.append(dec_h)
            st.append(s_scr[hh])
            for s in range(tb // sb):
                sl = slice(s * sb, (s + 1) * sb)
                p = jnp.where(mask, _mm_nt(qh[sl], kh[sl]), 0.0)
                o_ref[sl, ln] = _mm(p, vb[hh][sl])
        for c in range(nc):
            sl = slice(c * CHUNK, (c + 1) * CHUNK)
            for hh, ln in enumerate(lanes):
                st_ref[hh, c] = st[hh]
                o_ref[sl, ln] = o_ref[sl, ln] + _mm_nt(qi[hh][sl], st[hh])
                st[hh] = dec[hh][c] * st[hh] + _mm_tn(vb[hh][sl], ko[hh][sl])
        for hh, ln in enumerate(lanes):
            s_scr[hh] = st[hh]
            o = o_ref[:, ln]
            r = lax.rsqrt(jnp.mean(o * o, axis=-1, keepdims=True) + EPS)
            gr = g_ref[:, ln]
            og = (o * r * gn_ref[...]) * (gr * _sigmoid(gr))
            og_ref[:, ln] = og.astype(BF16)
            ogt_ref[ln, :] = og.T.astype(BF16)

    col = lambda p: pl.BlockSpec((tb, wid), lambda h, t: (t, p * HEAD_GROUPS + h))
    hcol = pl.BlockSpec((tb, wid), lambda h, t: (t, h))
    return pl.pallas_call(
        body, name="hg_fwd", grid=(HEAD_GROUPS, nb),
        in_specs=[col(0), col(1), col(2), col(3), pl.BlockSpec((2, wid), lambda h, t: (0, h)),
                  pl.BlockSpec((1, HEAD_DIM), lambda h, t: (0, 0))],
        out_specs=[hcol, hcol, pl.BlockSpec((wid, tb), lambda h, t: (h, t)),
                   pl.BlockSpec((HEADS_PER_STEP, nc, HEAD_DIM, HEAD_DIM), lambda h, t: (h, t, 0, 0))],
        out_shape=[jax.ShapeDtypeStruct((T, HG_WIDTH), F32), jax.ShapeDtypeStruct((T, HG_WIDTH), BF16),
                   jax.ShapeDtypeStruct((HG_WIDTH, T), BF16),
                   jax.ShapeDtypeStruct((N_HEADS, T // CHUNK, HEAD_DIM, HEAD_DIM), F32)],
        scratch_shapes=[pltpu.VMEM((HEADS_PER_STEP, HEAD_DIM, HEAD_DIM), F32)],
        compiler_params=_params(("parallel", "arbitrary")),
    )(hg, hg, hg, hg, low, gn)


def _merge_fwd(og, cvo, gt, x, wa, wb, wo):
    T = x.shape[0]
    tm = min(1024, T)

    def body(og_ref, cvo_ref, gt_ref, x_ref, wa_ref, wb_ref, wo_ref, x1_ref, mgt_ref):
        ya = jnp.dot(og_ref[...], _shard_cols(wa_ref), preferred_element_type=F32)
        yb = jnp.dot(cvo_ref[...], _shard_cols(wb_ref), preferred_element_type=F32)
        m = (_sigmoid(gt_ref[:, :D_MODEL].astype(F32)) * ya
             + _sigmoid(gt_ref[:, D_MODEL:].astype(F32)) * yb)
        mgt_ref[...] = m.T.astype(BF16)
        x1_ref[...] = x_ref[...] + jnp.dot(m.astype(BF16), wo_ref[...], preferred_element_type=F32)

    row = lambda n: pl.BlockSpec((tm, n), lambda i: (i, 0))
    return pl.pallas_call(
        body, name="merge_fwd", grid=(T // tm,),
        in_specs=[row(HG_WIDTH), row(CONV_WIDTH), row(2 * D_MODEL), row(D_MODEL),
                  _resident(wa.shape), _resident(wb.shape), _resident(wo.shape)],
        out_specs=[row(D_MODEL), _col(tm, D_MODEL)],
        out_shape=[jax.ShapeDtypeStruct((T, D_MODEL), F32), jax.ShapeDtypeStruct((D_MODEL, T), BF16)],
        compiler_params=_params(("parallel",)),
    )(og, cvo, gt, x, wa, wb, wo)


def _ffn_fwd_loss(x1, g, wg, wu, wd, target, g_fin):
    T = x1.shape[0]
    tm = min(512, T)

    def body(x_ref, g_ref, wg_ref, wu_ref, wd_ref, t_ref, gf_ref,
             ht_ref, gate_ref, up_ref, act_ref, loss_ref, dgf_ref, dx2_ref, dx2t_ref):
        @pl.when(pl.program_id(0) == 0)
        def _():
            loss_ref[...] = jnp.zeros_like(loss_ref)
            dgf_ref[...] = jnp.zeros_like(dgf_ref)

        xv = x_ref[...]
        r = lax.rsqrt(jnp.mean(xv * xv, axis=-1, keepdims=True) + EPS)
        hf = xv * r * g_ref[...]
        h = hf.astype(BF16)
        ht_ref[...] = hf.T.astype(BF16)
        gate = _mm_nt(h, wg_ref[...])
        up = _mm_nt(h, wu_ref[...])
        gate_ref[...] = gate.astype(STASH)
        up_ref[...] = up.astype(STASH)
        act = (gate * _sigmoid(gate) * up).astype(BF16)
        act_ref[...] = act
        x2 = xv + jnp.dot(act, wd_ref[...], preferred_element_type=F32)

        gv = gf_ref[...]
        r2 = lax.rsqrt(jnp.mean(x2 * x2, axis=-1, keepdims=True) + EPS)
        xh = x2 * r2
        err = xh * gv - t_ref[...]
        loss_ref[...] += 0.5 * jnp.sum(jnp.mean(err * err, axis=-1, keepdims=True), axis=0, keepdims=True)
        dy = err * (1.0 / D_MODEL)
        dgf_ref[...] += jnp.sum(dy * xh, axis=0, keepdims=True)
        w = dy * gv
        dx2 = r2 * (w - xh * jnp.mean(w * xh, axis=-1, keepdims=True))
        dx2_ref[...] = dx2
        dx2t_ref[...] = dx2.T.astype(BF16)

    row = lambda n: pl.BlockSpec((tm, n), lambda i: (i, 0))
    return pl.pallas_call(
        body, name="ffn_fwd_loss", grid=(T // tm,),
        in_specs=[row(D_MODEL), _full((1, D_MODEL)), _resident(wg.shape), _resident(wu.shape), _resident(wd.shape),
                  row(D_MODEL), _full((1, D_MODEL))],
        out_specs=[_col(tm, D_MODEL), row(D_FF), row(D_FF), row(D_FF), _full((1, 128)), _full((1, D_MODEL)),
                   row(D_MODEL), _col(tm, D_MODEL)],
        out_shape=[jax.ShapeDtypeStruct((D_MODEL, T), BF16), jax.ShapeDtypeStruct((T, D_FF), STASH),
                   jax.ShapeDtypeStruct((T, D_FF), STASH), jax.ShapeDtypeStruct((T, D_FF), BF16),
                   jax.ShapeDtypeStruct((1, 128), F32), jax.ShapeDtypeStruct((1, D_MODEL), F32),
                   jax.ShapeDtypeStruct((T, D_MODEL), F32), jax.ShapeDtypeStruct((D_MODEL, T), BF16)],
        compiler_params=_params(("arbitrary",), vmem=VMEM_LIMIT_LARGE_V7X),
    )(x1, g, wg, wu, wd, target, g_fin)


def _ffn_bwd(dx2, x1, gate, up, g, wg, wu, wd):
    T = x1.shape[0]
    tm = min(512, T)

    def body(dx2_ref, x_ref, gate_ref, up_ref, g_ref, wg_ref, wu_ref, wd_ref, dgate_ref, dup_ref, dx1_ref, dgn_ref):
        @pl.when(pl.program_id(0) == 0)
        def _():
            dgn_ref[...] = jnp.zeros_like(dgn_ref)

        dx2 = dx2_ref[...]
        dact = _mm_nt(dx2, wd_ref[...])
        gate = gate_ref[...].astype(F32)
        s = _sigmoid(gate)
        dgate = (dact * up_ref[...].astype(F32) * (s * (1.0 + gate * (1.0 - s)))).astype(BF16)
        dup = (dact * (gate * s)).astype(BF16)
        dgate_ref[...] = dgate
        dup_ref[...] = dup
        dh = _mm(dgate, wg_ref[...]) + _mm(dup, wu_ref[...])
        xv = x_ref[...]
        r = lax.rsqrt(jnp.mean(xv * xv, axis=-1, keepdims=True) + EPS)
        xh = xv * r
        dgn_ref[...] += jnp.sum(dh * xh, axis=0, keepdims=True)
        w = dh * g_ref[...]
        dx1_ref[...] = dx2 + r * (w - xh * jnp.mean(w * xh, axis=-1, keepdims=True))

    row = lambda n: pl.BlockSpec((tm, n), lambda i: (i, 0))
    return pl.pallas_call(
        body, name="ffn_bwd", grid=(T // tm,),
        in_specs=[row(D_MODEL), row(D_MODEL), row(D_FF), row(D_FF), _full((1, D_MODEL)),
                  _resident(wg.shape), _resident(wu.shape), _resident(wd.shape)],
        out_specs=[row(D_FF), row(D_FF), row(D_MODEL), _full((1, D_MODEL))],
        out_shape=[jax.ShapeDtypeStruct((T, D_FF), BF16), jax.ShapeDtypeStruct((T, D_FF), BF16),
                   jax.ShapeDtypeStruct((T, D_MODEL), F32), jax.ShapeDtypeStruct((1, D_MODEL), F32)],
        compiler_params=_params(("arbitrary",), vmem=VMEM_LIMIT_LARGE_V7X),
    )(dx2, x1, gate, up, g, wg, wu, wd)


def _merge_bwd(dx1, og, cvo, gt, cv, wa, wb, wo, conv_w):
    T = dx1.shape[0]
    tm = min(512, T)
    nt = T // tm

    def body(dx_ref, og_ref, cvo_ref, gt_ref, cv_ref, halo_ref, wa_ref, wb_ref, wo_ref, cw_ref,
             dgt_ref, dya_ref, dyb_ref, dog_ref, dcv_ref, dcw_ref, prev_u, next_dy):
        step = pl.program_id(0)

        @pl.when(step == 0)
        def _():
            next_dy[...] = jnp.zeros_like(next_dy)
            dcw_ref[...] = jnp.zeros_like(dcw_ref)

        dm = _mm_nt(dx_ref[...], wo_ref[...])
        wa = _shard_cols(wa_ref)
        wb = _shard_cols(wb_ref)
        ya = jnp.dot(og_ref[...], wa, preferred_element_type=F32)
        yb = jnp.dot(cvo_ref[...], wb, preferred_element_type=F32)
        sa = _sigmoid(gt_ref[:, :D_MODEL].astype(F32))
        sb = _sigmoid(gt_ref[:, D_MODEL:].astype(F32))
        dgt_ref[:, :D_MODEL] = (dm * ya * (sa * (1.0 - sa))).astype(BF16)
        dgt_ref[:, D_MODEL:] = (dm * yb * (sb * (1.0 - sb))).astype(BF16)
        dya = (dm * sa).astype(BF16)
        dyb = (dm * sb).astype(BF16)
        dya_ref[...] = dya
        dyb_ref[...] = dyb
        dog_ref[...] = _mm_nt(dya, wa)
        dcvo = _mm_nt(dyb, wb)

        cvt = cv_ref[...].astype(F32)
        c, bg, xb = cvt[:, :CONV_WIDTH], cvt[:, CONV_WIDTH:2 * CONV_WIDTH], cvt[:, 2 * CONV_WIDTH:]
        halo = halo_ref[...].astype(F32)
        first_tile = step == nt - 1
        prev_u[...] = jnp.where(first_tile, 0.0, halo[:, :CONV_WIDTH] * halo[:, 2 * CONV_WIDTH:])
        u = c * xb
        row = lax.broadcasted_iota(jnp.int32, u.shape, 0)
        p1 = prev_u[HALO - 1:HALO, :]
        p2 = prev_u[HALO - 2:HALO - 1, :]
        u1 = jnp.where(row >= 1, pltpu.roll(u, 1, 0), p1)
        u2 = jnp.where(row >= 2, pltpu.roll(u, 2, 0), jnp.where(row == 1, p1, p2))
        w0, w1, w2 = cw_ref[0:1, :], cw_ref[1:2, :], cw_ref[2:3, :]
        y = w0 * u2 + w1 * u1 + w2 * u
        dcv_ref[:, CONV_WIDTH:2 * CONV_WIDTH] = (dcvo * y).astype(BF16)
        dy = dcvo * bg
        dcw_ref[0:1, :] += jnp.sum(dy * u2, axis=0, keepdims=True)
        dcw_ref[1:2, :] += jnp.sum(dy * u1, axis=0, keepdims=True)
        dcw_ref[2:3, :] += jnp.sum(dy * u, axis=0, keepdims=True)
        n1 = next_dy[0:1, :]
        n2 = next_dy[1:2, :]
        dy1 = jnp.where(row < tm - 1, pltpu.roll(dy, tm - 1, 0), n1)
        dy2 = jnp.where(row < tm - 2, pltpu.roll(dy, tm - 2, 0), jnp.where(row == tm - 2, n1, n2))
        du = w2 * dy + w1 * dy1 + w0 * dy2
        dcv_ref[:, :CONV_WIDTH] = (du * xb).astype(BF16)
        dcv_ref[:, 2 * CONV_WIDTH:] = (du * c).astype(BF16)
        next_dy[...] = dy[:HALO, :]

    rt = lambda i: nt - 1 - i
    row = lambda n: pl.BlockSpec((tm, n), lambda i: (rt(i), 0))
    halo = pl.BlockSpec((HALO, N_CV), lambda i: (jnp.maximum(rt(i) * (tm // HALO) - 1, 0), 0))
    return pl.pallas_call(
        body, name="merge_bwd", grid=(nt,),
        in_specs=[row(D_MODEL), row(HG_WIDTH), row(CONV_WIDTH), row(2 * D_MODEL), row(N_CV), halo,
                  _resident(wa.shape), _resident(wb.shape), _resident(wo.shape), _full((CONV_K, CONV_WIDTH))],
        out_specs=[row(2 * D_MODEL), row(D_MODEL), row(D_MODEL), row(HG_WIDTH), row(N_CV),
                   _full((CONV_K, CONV_WIDTH))],
        out_shape=[jax.ShapeDtypeStruct((T, 2 * D_MODEL), BF16), jax.ShapeDtypeStruct((T, D_MODEL), BF16),
                   jax.ShapeDtypeStruct((T, D_MODEL), BF16), jax.ShapeDtypeStruct((T, HG_WIDTH), F32),
                   jax.ShapeDtypeStruct((T, N_CV), BF16), jax.ShapeDtypeStruct((CONV_K, CONV_WIDTH), F32)],
        scratch_shapes=[pltpu.VMEM((HALO, CONV_WIDTH), F32), pltpu.VMEM((HALO, CONV_WIDTH), F32)],
        compiler_params=_params(("arbitrary",)),
    )(dx1, og, cvo, gt, cv, cv, wa, wb, wo, conv_w)


def _drop_operands(body, first, count):
    def wrapped(*refs):
        return body(*refs[:first], *refs[first + count:])
    return wrapped


def _hg_bwd(dog, hg, o, st, low, gn, after=()):
    T = hg.shape[0]
    tb = min(512, T)
    sb = min(256, tb)
    nb = T // tb
    nc = tb // CHUNK
    wid = HEADS_PER_STEP * HEAD_DIM

    def body(q_ref, f_ref, i_ref, g_ref, low_ref, gn_ref, o_ref, dog_ref, st_ref,
             dhg_ref, dlow_ref, dgn_ref,
             ds_scr, dqi_scr, dko_scr, dv_scr, dd_scr, dqh_scr, dkh_scr):
        h = pl.program_id(0)
        t = pl.program_id(1)
        dq_ref, df_ref, di_ref, dg_ref = (dhg_ref.at[:, p * HG_WIDTH:(p + 1) * HG_WIDTH] for p in range(4))

        @pl.when(t == 0)
        def _():
            ds_scr[...] = jnp.zeros_like(ds_scr)
            dlow_ref[...] = jnp.zeros_like(dlow_ref)

        @pl.when((t == 0) & (h == 0))
        def _():
            dgn_ref[...] = jnp.zeros_like(dgn_ref)

        pos = _chunk_pos((tb, HEAD_DIM))
        mask = _intra_mask(sb)
        gnv = gn_ref[...]
        lanes = [slice(hh * HEAD_DIM, (hh + 1) * HEAD_DIM) for hh in range(HEADS_PER_STEP)]
        heads = []
        for hh, ln in enumerate(lanes):
            lb, lb1 = _lower_bound(low_ref.at[:, ln])
            qr = q_ref[:, ln]
            sq, q, sg, f, k, e_qa, e_ka, e_b, e_ko, dec = _hg_gates(qr, f_ref[:, ln], lb, pos, tb)

            gr = g_ref[:, ln]
            o = o_ref[:, ln]
            dog_v = dog_ref[:, ln]
            sgr = _sigmoid(gr)
            r = lax.rsqrt(jnp.mean(o * o, axis=-1, keepdims=True) + EPS)
            oh = o * r
            dg_ref[:, ln] = (dog_v * (oh * gnv) * (sgr * (1.0 + gr * (1.0 - sgr)))).astype(BF16)
            don = dog_v * (gr * sgr)
            dgn_ref[...] += jnp.sum(don * oh, axis=0, keepdims=True)
            w = don * gnv
            do = (r * (w - oh * jnp.mean(w * oh, axis=-1, keepdims=True))).astype(BF16)

            qh = (q * e_qa).astype(BF16)
            kh = (k * e_ka).astype(BF16)
            qi = (q * e_b).astype(BF16)
            ko = (k * e_ko).astype(BF16)
            vb = i_ref[:, ln].astype(BF16)

            for s in range(tb // sb):
                sl = slice(s * sb, (s + 1) * sb)
                p = jnp.where(mask, _mm_nt(qh[sl], kh[sl]), 0.0).astype(BF16)
                dp = jnp.where(mask, _mm_nt(do[sl], vb[sl]), 0.0).astype(BF16)
                dv_scr[sl, ln] = _mm_tn(p, do[sl])
                dqh_scr[sl, ln] = _mm(dp, kh[sl])
                dkh_scr[sl, ln] = _mm_tn(dp, qh[sl])
            heads.append(dict(lb=lb, lb1=lb1, qr=qr, sq=sq, q=q, sg=sg, f=f, k=k, e_qa=e_qa, e_ka=e_ka, e_b=e_b,
                              e_ko=e_ko, dec=dec, do=do, qi=qi, ko=ko, vb=vb, ds=ds_scr[hh]))

        for c in reversed(range(nc)):
            sl = slice(c * CHUNK, (c + 1) * CHUNK)
            for hh, ln in enumerate(lanes):
                hd = heads[hh]
                ds = hd["ds"]
                st_c = st_ref[hh, c]
                dqi_scr[sl, ln] = _mm(hd["do"][sl], st_c)
                dko_scr[sl, ln] = _mm(hd["vb"][sl], ds)
                dv_scr[sl, ln] = dv_scr[sl, ln] + _mm_nt(hd["ko"][sl], ds)
                dec_c = hd["dec"][c]
                dd_scr[sl, ln] = jnp.broadcast_to(dec_c * jnp.sum(ds * st_c, axis=0, keepdims=True),
                                                  (CHUNK, HEAD_DIM))
                hd["ds"] = dec_c * ds + _mm_tn(hd["do"][sl], hd["qi"][sl])

        for hh, ln in enumerate(lanes):
            hd = heads[hh]
            ds_scr[hh] = hd["ds"]
            q, k, lb = hd["q"], hd["k"], hd["lb"]
            dko_e = dko_scr[:, ln] * hd["e_ko"]
            dq = dqh_scr[:, ln] * hd["e_qa"] + dqi_scr[:, ln] * hd["e_b"]
            dk = dkh_scr[:, ln] * hd["e_ka"] + dko_e
            kd3 = (k * dko_e).reshape(nc, CHUNK, HEAD_DIM)
            last = jnp.broadcast_to(jnp.sum(kd3, axis=1, keepdims=True), kd3.shape).reshape(tb, HEAD_DIM)
            db = q * dq - k * dk + jnp.where(pos == CHUNK - 1, dd_scr[:, ln] + last, 0.0)
            dlg = _chunk_rev_cumsum(db, pos)
            dfv = dlg / hd["f"] - dk
            s_low = jnp.sum(dfv * (1.0 - hd["sg"]), axis=0, keepdims=True)
            dlow_ref[0:1, ln] += s_low * lb * (1.0 - lb)
            dlow_ref[1:2, ln] += -s_low * lb * hd["lb1"]
            df_ref[:, ln] = (dfv * (1.0 - lb) * hd["sg"] * (1.0 - hd["sg"])).astype(BF16)
            dq_ref[:, ln] = (dq * Q_SCALE * (hd["sq"] * (1.0 + hd["qr"] * (1.0 - hd["sq"])))).astype(BF16)
            di_ref[:, ln] = dv_scr[:, ln].astype(BF16)

    rt = lambda t: nb - 1 - t
    col = lambda p: pl.BlockSpec((tb, wid), lambda h, t: (rt(t), p * HEAD_GROUPS + h))
    hcol = pl.BlockSpec((tb, wid), lambda h, t: (rt(t), h))
    assert HEAD_GROUPS == 1
    tile = pltpu.VMEM((tb, wid), F32)
    return pl.pallas_call(
        _drop_operands(body, 9, len(after)), name="hg_bwd", grid=(HEAD_GROUPS, nb),
        in_specs=[col(0), col(1), col(2), col(3), pl.BlockSpec((2, wid), lambda h, t: (0, h)),
                  pl.BlockSpec((1, HEAD_DIM), lambda h, t: (0, 0)), hcol, hcol,
                  pl.BlockSpec((HEADS_PER_STEP, nc, HEAD_DIM, HEAD_DIM), lambda h, t: (h, rt(t), 0, 0))]
                 + [HBM_SPEC] * len(after),
        out_specs=[pl.BlockSpec((tb, N_HG), lambda h, t: (rt(t), 0)), pl.BlockSpec((2, wid), lambda h, t: (0, h)),
                   pl.BlockSpec((1, HEAD_DIM), lambda h, t: (0, 0))],
        out_shape=[jax.ShapeDtypeStruct((T, N_HG), BF16), jax.ShapeDtypeStruct((2, HG_WIDTH), F32),
                   jax.ShapeDtypeStruct((1, HEAD_DIM), F32)],
        scratch_shapes=[pltpu.VMEM((HEADS_PER_STEP, HEAD_DIM, HEAD_DIM), F32), tile, tile, tile, tile, tile, tile],
        compiler_params=_params(("arbitrary", "arbitrary")),
    )(hg, hg, hg, hg, low, gn, o, dog, st, *after)


def _in_bwd(dparts, w_in, x, dx1, g, after=()):
    T = x.shape[0]
    tm = min(512, T)
    widths = [p.shape[1] for p in dparts]
    offs = [sum(widths[:i]) for i in range(len(widths))]
    n = len(dparts)

    def body(*refs):
        d_refs = refs[:n]
        w_ref, x_ref, dx1_ref, g_ref, dx_ref, dgn_ref = refs[n:]

        @pl.when(pl.program_id(0) == 0)
        def _():
            dgn_ref[...] = jnp.zeros_like(dgn_ref)

        dh = None
        for d_ref, off, wd in zip(d_refs, offs, widths):
            part = _mm(d_ref[...], w_ref[off:off + wd, :])
            dh = part if dh is None else dh + part
        xv = x_ref[...]
        r = lax.rsqrt(jnp.mean(xv * xv, axis=-1, keepdims=True) + EPS)
        xh = xv * r
        dgn_ref[...] += jnp.sum(dh * xh, axis=0, keepdims=True)
        w = dh * g_ref[...]
        dx_ref[...] = dx1_ref[...] + r * (w - xh * jnp.mean(w * xh, axis=-1, keepdims=True))

    row = lambda m: pl.BlockSpec((tm, m), lambda i: (i, 0))
    return pl.pallas_call(
        _drop_operands(body, n + 4, len(after)), name="in_bwd", grid=(T // tm,),
        in_specs=[row(wd) for wd in widths] + [_resident(w_in.shape), row(D_MODEL), row(D_MODEL), _full((1, D_MODEL))]
                 + [HBM_SPEC] * len(after),
        out_specs=[row(D_MODEL), _full((1, D_MODEL))],
        out_shape=[jax.ShapeDtypeStruct((T, D_MODEL), F32), jax.ShapeDtypeStruct((1, D_MODEL), F32)],
        compiler_params=_params(("arbitrary",)),
    )(*dparts, w_in, x, dx1, g, *after)


def _wgrad(name, at, b, tn, transposed=False, tk=2048, after=()):
    M, T = at.shape
    N = b.shape[1]
    tk = min(tk, T)
    nk = T // tk
    if transposed:
        out_spec, out_shape = pl.BlockSpec((tn, M), lambda j, k: (j, 0)), (N, M)
    else:
        out_spec, out_shape = pl.BlockSpec((M, tn), lambda j, k: (0, j)), (M, N)

    if nk == 1:
        def body1(a_ref, b_ref, o_ref):
            part = _mm(a_ref[...], b_ref[...])
            o_ref[...] = (part.T if transposed else part).astype(BF16)

        return pl.pallas_call(
            _drop_operands(body1, 2, len(after)), name=name, grid=(N // tn, 1),
            in_specs=[_resident((M, T)), pl.BlockSpec((T, tn), lambda j, k: (0, j))] + [HBM_SPEC] * len(after),
            out_specs=out_spec, out_shape=jax.ShapeDtypeStruct(out_shape, BF16),
            compiler_params=_params(("parallel", "arbitrary")),
        )(at, b, *after)

    def body(a_ref, b_ref, o_ref, acc):
        k = pl.program_id(1)

        @pl.when(k == 0)
        def _():
            acc[...] = jnp.zeros_like(acc)

        acc[...] += _mm(a_ref[...], b_ref[...])

        @pl.when(k == nk - 1)
        def _():
            o_ref[...] = (acc[...].T if transposed else acc[...]).astype(BF16)

    return pl.pallas_call(
        _drop_operands(body, 2, len(after)), name=name, grid=(N // tn, nk),
        in_specs=[pl.BlockSpec((M, tk), lambda j, k: (0, k)), pl.BlockSpec((tk, tn), lambda j, k: (k, j))]
                 + [HBM_SPEC] * len(after),
        out_specs=out_spec, out_shape=jax.ShapeDtypeStruct(out_shape, BF16),
        scratch_shapes=[pltpu.VMEM((M, tn), F32)],
        compiler_params=_params(("parallel", "arbitrary")),
    )(at, b, *after)


def _wgrad_pair(name, at, b1, b2, tn):
    M, T = at.shape
    N = b1.shape[1]

    def body(a_ref, b1_ref, b2_ref, o1_ref, o2_ref):
        a = a_ref[...]
        o1_ref[...] = _mm(a, b1_ref[...]).T.astype(BF16)
        o2_ref[...] = _mm(a, b2_ref[...]).T.astype(BF16)

    rhs = pl.BlockSpec((T, tn), lambda j: (0, j))
    out_spec = pl.BlockSpec((tn, M), lambda j: (j, 0))
    out = jax.ShapeDtypeStruct((N, M), BF16)
    return pl.pallas_call(
        body, name=name, grid=(N // tn,),
        in_specs=[_resident((M, T)), rhs, rhs], out_specs=[out_spec, out_spec], out_shape=[out, out],
        compiler_params=_params(("parallel",)),
    )(at, b1, b2)


def _wgrad_branches(ogt, dya, cvot, dyb):
    M, T = ogt.shape
    N = dya.shape[1]
    c = N // N_DEV
    per = 2
    tn = per * c

    def body(at_ref, da_ref, bt_ref, db_ref, oa_ref, ob_ref):
        ga = _mm(at_ref[...], da_ref[...])
        gb = _mm(bt_ref[...], db_ref[...])
        for s in range(per):
            oa_ref[s] = ga[:, s * c:(s + 1) * c].astype(BF16)
            ob_ref[s] = gb[:, s * c:(s + 1) * c].astype(BF16)

    rhs = pl.BlockSpec((T, tn), lambda j: (0, j))
    owners = pl.BlockSpec((per, M, c), lambda j: (j, 0, 0))
    out = jax.ShapeDtypeStruct((N_DEV, M, c), BF16)
    return pl.pallas_call(
        body, name="wgrad_branches", grid=(N // tn,),
        in_specs=[_resident((M, T)), rhs, _resident((M, T)), rhs], out_specs=[owners] * 2, out_shape=[out, out],
        compiler_params=_params(("parallel",)),
    )(ogt, dya, cvot, dyb)


def _wgrad_in(ht, dparts, after=()):
    M, T = ht.shape
    tn = 512
    nblk = [p.shape[1] // tn for p in dparts]
    start = [sum(nblk[:i]) for i in range(len(nblk))]
    n = len(dparts)

    def body(a_ref, *refs):
        d_refs, o_ref = refs[:n], refs[n]
        j = pl.program_id(0)
        for d_ref, s, nb in zip(d_refs, start, nblk):
            @pl.when((j >= s) & (j < s + nb))
            def _():
                o_ref[...] = _mm(a_ref[...], d_ref[...]).T.astype(BF16)

    def piece_spec(s, nb):
        return pl.BlockSpec((T, tn), lambda j: (0, jnp.clip(j - s, 0, nb - 1)))

    return pl.pallas_call(
        _drop_operands(body, 1 + n, len(after)), name="wgrad_in", grid=(sum(nblk),),
        in_specs=[_resident((M, T))] + [piece_spec(s, nb) for s, nb in zip(start, nblk)] + [HBM_SPEC] * len(after),
        out_specs=pl.BlockSpec((tn, M), lambda j: (j, 0)),
        out_shape=jax.ShapeDtypeStruct((sum(nblk) * tn, M), BF16),
        compiler_params=_params(("parallel",)),
    )(ht, *dparts, *after)


def _adamw_math(w, g, m, v):
    m = ADAM_B1 * m + (1.0 - ADAM_B1) * g
    v = ADAM_B2 * v + (1.0 - ADAM_B2) * (g * g)
    m_hat = m / (1.0 - ADAM_B1 ** ADAM_STEP)
    v_hat = v / (1.0 - ADAM_B2 ** ADAM_STEP)
    delta = -ADAM_LR * (m_hat / (jnp.sqrt(v_hat) + ADAM_EPS) + ADAM_WD * w)
    return delta, m, v


def _adamw_sum(name, w, parts, m, v):
    R, C = w.shape
    tr = _row_tile(R)

    def body(w_ref, p_ref, m_ref, v_ref, g_out, d_out, m_out, v_out):
        g = p_ref[0].astype(F32)
        for k in range(1, 4):
            g = g + p_ref[k].astype(F32)
        g_out[...] = g
        d_out[...], m_out[...], v_out[...] = _adamw_math(w_ref[...], g, m_ref[...], v_ref[...])

    blk = pl.BlockSpec((tr, C), lambda i: (i, 0))
    out = jax.ShapeDtypeStruct((R, C), F32)
    return pl.pallas_call(
        body, name=name, grid=(R // tr,),
        in_specs=[blk, pl.BlockSpec((4, tr, C), lambda i: (0, i, 0)), blk, blk],
        out_specs=[blk, blk, blk, blk], out_shape=[out, out, out, out],
        compiler_params=_params(("parallel",)),
    )(w, parts, m, v)


_SMALL_SLOTS = (("norm_mix_g", 0, 1, 1024), ("norm_ffn_g", 1, 1, 1024), ("norm_final_g", 2, 1, 1024),
                ("lower_bounds", 3, 2, 512), ("hg_norm_g", 5, 1, 128), ("loss", 6, 1, 128), ("conv_w", 8, 3, 512))
_SMALL_PARAMS = tuple(s for s in _SMALL_SLOTS if s[0] != "loss")
CONV_SHARD = CONV_WIDTH // N_DEV


def _small_pack(small):
    def body(*refs):
        out = refs[-1]
        out[...] = jnp.zeros_like(out)
        for ref, (_, row, rows, lanes) in zip(refs[:-1], _SMALL_SLOTS):
            out[row:row + rows, 0:lanes] = ref[...]

    vmem = pl.BlockSpec(memory_space=pltpu.VMEM)
    return pl.pallas_call(
        body, name="small_pack", in_specs=[vmem] * len(_SMALL_SLOTS), out_specs=vmem,
        out_shape=jax.ShapeDtypeStruct((SMALL_ROWS, 1024), F32),
    )(*[small[name] for name, _, _, _ in _SMALL_SLOTS])


def _small_update(gathered, dev, w, m, v):
    n = len(_SMALL_PARAMS)

    def body(dev_ref, g_ref, *refs):
        w_refs, m_refs, v_refs = refs[:n], refs[n:2 * n], refs[2 * n:3 * n]
        loss_ref, out_refs, sum_scr = refs[3 * n], refs[3 * n + 1:-1], refs[-1]
        total = g_ref[0]
        for k in range(1, N_DEV):
            total = total + g_ref[k]
        sum_scr[...] = total
        loss_ref[...] = sum_scr[6:7, 0:128]
        for p, (name, row, rows, lanes) in enumerate(_SMALL_PARAMS):
            if name == "conv_w":
                g = sum_scr[row:row + rows, 0:CONV_SHARD]
                for s in range(1, N_DEV):
                    g = jnp.where(dev_ref[0] == s, sum_scr[row:row + rows, s * CONV_SHARD:(s + 1) * CONV_SHARD], g)
            else:
                g = sum_scr[row:row + rows, 0:lanes]
            delta, m_new, v_new = _adamw_math(w_refs[p][...], g, m_refs[p][...], v_refs[p][...])
            out_refs[4 * p][...] = g
            out_refs[4 * p + 1][...] = delta
            out_refs[4 * p + 2][...] = m_new
            out_refs[4 * p + 3][...] = v_new

    vmem = pl.BlockSpec(memory_space=pltpu.VMEM)
    outs = [jax.ShapeDtypeStruct((1, 128), F32)]
    for a in w:
        outs += [jax.ShapeDtypeStruct(a.shape, F32)] * 4
    return pl.pallas_call(
        body, name="small_update",
        in_specs=[pl.BlockSpec(memory_space=pltpu.SMEM)] + [vmem] * (1 + 3 * n), out_specs=[vmem] * len(outs),
        out_shape=outs, scratch_shapes=[pltpu.VMEM((SMALL_ROWS, 1024), F32)],
    )(dev, gathered, *w, *m, *v)


def _row_tile(rows):
    for parts in (4, 2):
        if rows % (16 * parts) == 0:
            return rows // parts
    return rows


def _pair_sum(name, by_owner, got, core, after=()):
    n = len(got)

    def body(core_ref, *refs):
        for a_ref, b_ref, o_ref in zip(refs[:n], refs[n:2 * n], refs[2 * n:]):
            o_ref[...] = (a_ref[...].astype(F32) + b_ref[...].astype(F32)).astype(BF16)

    def blk(g):
        return pl.BlockSpec((None,) + g.shape[1:], lambda k, core_ref: (k, 0, 0))

    def mine(g):
        return pl.BlockSpec((None,) + g.shape[1:], lambda k, core_ref: (2 * k + core_ref[0], 0, 0))

    return pl.pallas_call(
        _drop_operands(body, 1 + 2 * n, len(after)), name=name,
        grid_spec=pltpu.PrefetchScalarGridSpec(
            num_scalar_prefetch=1, grid=(4,),
            in_specs=[mine(g) for g in got] + [blk(g) for g in got] + [HBM_SPEC] * len(after),
            out_specs=[blk(g) for g in got]),
        out_shape=[jax.ShapeDtypeStruct(g.shape, BF16) for g in got],
        compiler_params=_params(("parallel",)),
    )(core, *by_owner, *got, *after)


MESH = pl.DeviceIdType.MESH
HBM_SPEC = pl.BlockSpec(memory_space=pl.ANY)


def _handshake(peers):
    barrier = pltpu.get_barrier_semaphore()
    for peer in peers:
        pl.semaphore_signal(barrier, inc=1, device_id=peer, device_id_type=MESH)
    pl.semaphore_wait(barrier, len(peers))


def _comm_call(body, name, operands, out_shape, scratch, collective_id):
    if collective_id is None:
        return pl.pallas_call(body, name=name, in_specs=[HBM_SPEC] * len(operands), out_specs=[HBM_SPEC] * len(out_shape),
                              out_shape=out_shape, scratch_shapes=scratch)(*operands)
    return pl.kernel(body, out_type=out_shape, mesh=plsc.ScalarSubcoreMesh(axis_name="sequencer", num_cores=1),
                     scratch_types=scratch, name=name,
                     compiler_params=pltpu.CompilerParams(collective_id=collective_id))(*operands)


def _all_gather(name, blocks, collective_id=None, after=()):
    n = len(blocks)
    na = len(after)

    def body(*refs):
        x_refs, out_refs = refs[:n], refs[n + na:2 * n + na]
        send_sems, recv_sems, local_sems = refs[2 * n + na:]
        x, y, c = lax.axis_index("x"), lax.axis_index("y"), lax.axis_index("c")
        me, sibling = (x, y, c), (x, y, 1 - c)
        chips = [(1 - x, y), (x, 1 - y), (1 - x, 1 - y)]
        if collective_id is not None:
            _handshake([sibling] + [(*chip, c) for chip in chips])

        def slot(i, px, py, pc):
            return out_refs[i].at[4 * px + 2 * py + pc]

        def copy(i, k, blk, to, src=None):
            return pltpu.make_async_remote_copy(
                src_ref=slot(i, *blk) if src is None else src, dst_ref=slot(i, *blk),
                send_sem=send_sems.at[7 * i + k], recv_sem=recv_sems.at[7 * i + k], device_id=to, device_id_type=MESH)

        mine = [pltpu.make_async_copy(x_refs[i], slot(i, *me), local_sems.at[i]) for i in range(n)]
        for cp in mine:
            cp.start()
        first = []
        for i in range(n):
            first.append(copy(i, 0, me, sibling, src=x_refs[i]))
            first += [copy(i, 1 + j, me, (*chip, c), src=x_refs[i]) for j, chip in enumerate(chips)]
        for cp in first:
            cp.start()
        passed = []
        for i in range(n):
            for j, chip in enumerate(chips):
                copy(i, 1 + j, (*chip, c), me).wait_recv()
                passed.append(copy(i, 4 + j, (*chip, c), sibling))
                passed[-1].start()
        for i in range(n):
            copy(i, 0, sibling, me).wait_recv()
            for j, chip in enumerate(chips):
                copy(i, 4 + j, (*chip, 1 - c), me).wait_recv()
        for cp in first + passed:
            cp.wait_send()
        for cp in mine:
            cp.wait()

    return _comm_call(
        body, name, list(blocks) + list(after), [jax.ShapeDtypeStruct((N_DEV,) + b.shape, b.dtype) for b in blocks],
        [pltpu.SemaphoreType.DMA((7 * n,)), pltpu.SemaphoreType.DMA((7 * n,)), pltpu.SemaphoreType.DMA((n,))],
        collective_id)


def _sibling_swap(name, by_owner, collective_id=None, after=()):
    n = len(by_owner)
    na = len(after)

    def body(*refs):
        x_refs, out_refs = refs[:n], refs[n + na:2 * n + na]
        send_sems, recv_sems = refs[2 * n + na:]
        x, y, c = lax.axis_index("x"), lax.axis_index("y"), lax.axis_index("c")
        if collective_id is not None:
            _handshake([(x, y, 1 - c)])
        copies = []
        for i in range(n):
            for k in range(4):
                copies.append(pltpu.make_async_remote_copy(
                    src_ref=x_refs[i].at[2 * k + 1 - c], dst_ref=out_refs[i].at[k],
                    send_sem=send_sems.at[4 * i + k], recv_sem=recv_sems.at[4 * i + k],
                    device_id=(x, y, 1 - c), device_id_type=MESH))
        for cp in copies:
            cp.start()
        for cp in copies:
            cp.wait()

    return _comm_call(
        body, name, list(by_owner) + list(after),
        [jax.ShapeDtypeStruct((4,) + b.shape[1:], b.dtype) for b in by_owner],
        [pltpu.SemaphoreType.DMA((4 * n,)), pltpu.SemaphoreType.DMA((4 * n,))], collective_id)


def _chip_exchange(name, sums, collective_id=None, after=()):
    n = len(sums)
    na = len(after)

    def body(*refs):
        x_refs, out_refs = refs[:n], refs[n + na:2 * n + na]
        send_sems, recv_sems, local_sems = refs[2 * n + na:]
        x, y, c = lax.axis_index("x"), lax.axis_index("y"), lax.axis_index("c")
        chips = [(1 - x, y), (x, 1 - y), (1 - x, 1 - y)]
        my_chip = 2 * x + y
        if collective_id is not None:
            _handshake([(cx, cy, c) for cx, cy in chips])
        mine = [pltpu.make_async_copy(x_refs[i].at[my_chip], out_refs[i].at[my_chip], local_sems.at[i])
                for i in range(n)]
        for cp in mine:
            cp.start()
        sends = []
        for i in range(n):
            for j, (cx, cy) in enumerate(chips):
                sends.append(pltpu.make_async_remote_copy(
                    src_ref=x_refs[i].at[2 * cx + cy], dst_ref=out_refs[i].at[my_chip],
                    send_sem=send_sems.at[3 * i + j], recv_sem=recv_sems.at[3 * i + j],
                    device_id=(cx, cy, c), device_id_type=MESH))
        for cp in sends:
            cp.start()
        for i in range(n):
            for j, (cx, cy) in enumerate(chips):
                pltpu.make_async_remote_copy(
                    src_ref=x_refs[i].at[my_chip], dst_ref=out_refs[i].at[2 * cx + cy],
                    send_sem=send_sems.at[3 * i + j], recv_sem=recv_sems.at[3 * i + j],
                    device_id=(cx, cy, c), device_id_type=MESH).wait_recv()
        for cp in sends:
            cp.wait_send()
        for cp in mine:
            cp.wait()

    return _comm_call(
        body, name, list(sums) + list(after), [jax.ShapeDtypeStruct(s.shape, s.dtype) for s in sums],
        [pltpu.SemaphoreType.DMA((3 * n,)), pltpu.SemaphoreType.DMA((3 * n,)), pltpu.SemaphoreType.DMA((n,))],
        collective_id)


def _cast_shards(shards):
    n = len(shards)

    def body(*refs):
        for i in range(n):
            refs[n + i][...] = refs[i][...].astype(BF16)

    vmem = pl.BlockSpec(memory_space=pltpu.VMEM)
    return pl.pallas_call(
        body, name="cast_shards", in_specs=[vmem] * n, out_specs=[vmem] * n,
        out_shape=[jax.ShapeDtypeStruct(s.shape, BF16) for s in shards],
        compiler_params=pltpu.CompilerParams(vmem_limit_bytes=VMEM_LIMIT_V7X),
    )(*shards)


BIG = ("w_in", "w_branch_a", "w_branch_b", "w_out", "w_ffn_gate", "w_ffn_up", "w_ffn_down")


def _local_step(x, target, gains, low, conv_w, wg8, reduce):
    g_mix, g_hg, g_ffn, g_fin = gains
    w_in = wg8["w_in"].reshape(N_IN, D_MODEL)
    wg = wg8["w_ffn_gate"].reshape(D_FF, D_MODEL)
    wu = wg8["w_ffn_up"].reshape(D_FF, D_MODEL)
    wa, wb = wg8["w_branch_a"], wg8["w_branch_b"]
    wo = wg8["w_out"].reshape(D_MODEL, D_MODEL)
    wd = wg8["w_ffn_down"].reshape(D_FF, D_MODEL)

    ht, hg, cv, gt, cvo, cvot = _fwd_in(x, g_mix, w_in, conv_w)
    o, og, ogt, st = _hg_fwd(hg, low, g_hg)
    x1, mgt = _merge_fwd(og, cvo, gt, x, wa, wb, wo)
    h2t, gate, up, act, loss, d_gfin, dx2, dx2t = _ffn_fwd_loss(x1, g_ffn, wg, wu, wd, target, g_fin)

    dgate, dup, dx1, d_gffn = _ffn_bwd(dx2, x1, gate, up, g_ffn, wg, wu, wd)
    d_wg, d_wu = _wgrad_pair("wgrad_ffn_gate_up", h2t, dgate, dup, 256)
    by_owner_ffn = lambda a: a.reshape(N_DEV, D_FF // N_DEV, D_MODEL)
    ffn = dict(
        w_ffn_down=by_owner_ffn(_wgrad("wgrad_ffn_down", dx2t, act, 256, transposed=True, tk=4096)),
        w_ffn_gate=by_owner_ffn(d_wg), w_ffn_up=by_owner_ffn(d_wu))
    dgt, dya, dyb, dog, dcv, d_conv = _merge_bwd(dx1, og, cvo, gt, cv, wa, wb, wo, conv_w)
    sums_ffn, got_ffn = reduce.begin(ffn, sum_after=[dya])
    parts_ffn, updated_ffn = reduce.finish(ffn, sums_ffn)
    grad_a, grad_b = _wgrad_branches(ogt, dya, cvot, dyb)
    out = dict(
        w_out=_wgrad("wgrad_out", mgt, dx1, 256, tk=4096, after=sums_ffn[:1]
                     ).reshape(N_DEV, D_MODEL // N_DEV, D_MODEL),
        w_branch_a=grad_a, w_branch_b=grad_b)
    dhg, d_low, d_ghg = _hg_bwd(dog, hg, o, st, low, g_hg, after=list(sums_ffn) + [out["w_out"]])
    sums_out, got_out = reduce.begin(out, after=[parts_ffn[0], dhg], sum_after=updated_ffn)
    parts_out, updated_out = reduce.finish(out, sums_out)
    dparts = [dhg, dcv, dgt]
    w_in_grad = dict(w_in=_wgrad_in(ht, dparts, after=sums_out[:1]).reshape(N_DEV, N_IN // N_DEV, D_MODEL))
    sums_in, _ = reduce.begin(w_in_grad, after=parts_out[:1], sum_after=updated_out)
    parts_in, _ = reduce.finish(w_in_grad, sums_in)
    grad_x, d_gmix = _in_bwd(dparts, w_in, x, dx1, g_mix, after=list(parts_out[:1]) + list(sums_in))
    small = dict(norm_mix_g=d_gmix, norm_ffn_g=d_gffn, norm_final_g=d_gfin, lower_bounds=d_low, hg_norm_g=d_ghg,
                 conv_w=d_conv, loss=loss)
    return grad_x, small, parts_in


def _conv_shard_rows(a):
    return jnp.pad(a, ((0, 5), (0, 64)))


def kernel(x, norm_mix_g, w_in, lower_bounds, hg_norm_g, conv_w, w_branch_a, w_branch_b, w_out, norm_ffn_g, w_ffn_gate, w_ffn_up, w_ffn_down, norm_final_g, loss_target, m_norm_mix_g, m_w_in, m_lower_bounds, m_hg_norm_g, m_conv_w, m_w_branch_a, m_w_branch_b, m_w_out, m_norm_ffn_g, m_w_ffn_gate, m_w_ffn_up, m_w_ffn_down, m_norm_final_g, v_norm_mix_g, v_w_in, v_lower_bounds, v_hg_norm_g, v_conv_w, v_w_branch_a, v_w_branch_b, v_w_out, v_norm_ffn_g, v_w_ffn_gate, v_w_ffn_up, v_w_ffn_down, v_norm_final_g):
    cx, cy, cc = lax.axis_index("x"), lax.axis_index("y"), lax.axis_index("c")
    my_dev = 4 * cx + 2 * cy + cc

    def tr(a):
        return a[0].T

    big = dict(w_in=tr(w_in), w_branch_a=w_branch_a[0], w_branch_b=w_branch_b[0], w_out=w_out[0],
               w_ffn_gate=tr(w_ffn_gate), w_ffn_up=tr(w_ffn_up), w_ffn_down=w_ffn_down[0])
    big_m = dict(w_in=tr(m_w_in), w_branch_a=m_w_branch_a[0], w_branch_b=m_w_branch_b[0], w_out=m_w_out[0],
                 w_ffn_gate=tr(m_w_ffn_gate), w_ffn_up=tr(m_w_ffn_up), w_ffn_down=m_w_ffn_down[0])
    big_v = dict(w_in=tr(v_w_in), w_branch_a=v_w_branch_a[0], w_branch_b=v_w_branch_b[0], w_out=v_w_out[0],
                 w_ffn_gate=tr(v_w_ffn_gate), w_ffn_up=tr(v_w_ffn_up), w_ffn_down=v_w_ffn_down[0])
    transposed = ("w_in", "w_ffn_gate", "w_ffn_up")

    shards = dict(zip(BIG, _cast_shards([big[n] for n in BIG])))
    first = _all_gather("gather_w_in", [shards["w_in"], _conv_shard_rows(conv_w[0])])
    ids = iter(range(1, 16))
    mid = _all_gather("gather_mid", [shards[n] for n in BIG[1:4]], collective_id=next(ids), after=first[1:])
    ffn = _all_gather("gather_ffn", [shards[n] for n in BIG[4:]], collective_id=next(ids), after=first[1:])
    wg8 = dict(zip(BIG, [first[0]] + list(mid) + list(ffn)))
    conv_full = first[1][:, :3, :64].transpose(1, 0, 2).reshape(3, CONV_WIDTH)

    core = cc.reshape(1).astype(jnp.int32)
    outs = {}

    class Reduce:
        @staticmethod
        def begin(grads, after=(), sum_after=()):
            names = list(grads)
            by_owner = [grads[n] for n in names]
            got = _sibling_swap("sibling_swap_" + names[0], by_owner, collective_id=next(ids), after=after)
            sums = _pair_sum("pair_sum_" + names[0], by_owner, got, core, after=sum_after)
            return sums, got

        @staticmethod
        def finish(grads, chip_sums, after=()):
            names = list(grads)
            parts = _chip_exchange("chip_exchange_" + names[0], chip_sums, collective_id=next(ids), after=after)
            for n, p in zip(names, parts):
                outs[n] = _adamw_sum("adamw_" + n, big[n], p, big_m[n], big_v[n])
            return parts, [outs[n][1] for n in names]

    gains = (norm_mix_g, hg_norm_g, norm_ffn_g, norm_final_g.reshape(1, D_MODEL))
    grad_x, small, last = _local_step(x[0], loss_target[0], gains, lower_bounds, conv_full, wg8, Reduce)

    small_all = _all_gather("gather_small", [_small_pack(small)], collective_id=next(ids), after=last[:1])

    def small_state(a):
        return [a[0], a[1], a[2].reshape(1, D_MODEL), a[3], a[4], a[5][0]]

    upd = _small_update(
        small_all[0], my_dev.reshape(1).astype(jnp.int32),
        small_state((norm_mix_g, norm_ffn_g, norm_final_g, lower_bounds, hg_norm_g, conv_w)),
        small_state((m_norm_mix_g, m_norm_ffn_g, m_norm_final_g, m_lower_bounds, m_hg_norm_g, m_conv_w)),
        small_state((v_norm_mix_g, v_norm_ffn_g, v_norm_final_g, v_lower_bounds, v_hg_norm_g, v_conv_w)))
    loss = upd[0][0, 0]
    small_shape = dict(norm_final_g=(D_MODEL,), conv_w=(1, 3, CONV_SHARD))
    for p, (name, _, _, _) in enumerate(_SMALL_PARAMS):
        outs[name] = [a.reshape(small_shape.get(name, a.shape)) for a in upd[1 + 4 * p:5 + 4 * p]]

    order = ["norm_mix_g", "w_in", "lower_bounds", "hg_norm_g", "conv_w", "w_branch_a", "w_branch_b", "w_out",
             "norm_ffn_g", "w_ffn_gate", "w_ffn_up", "w_ffn_down", "norm_final_g"]
    result = [loss, grad_x[None]]
    for k in range(4):
        for n in order:
            if n in BIG:
                result.append((outs[n][k].T if n in transposed else outs[n][k])[None])
            else:
                result.append(outs[n][k])
    return tuple(result)
```

```python
import jax
import jax.numpy as jnp
from jax import lax
from jax.experimental import pallas as pl
from jax.experimental.pallas import tpu as pltpu
from jax.experimental.pallas import tpu_sc as plsc

F32 = jnp.float32
BF16 = jnp.bfloat16
STASH = jnp.bfloat16

D_MODEL = 1024
HG_WIDTH = 512
HEAD_DIM = 128
N_HEADS = 4
HEADS_PER_STEP = 4
HEAD_GROUPS = N_HEADS // HEADS_PER_STEP
CONV_WIDTH = 512
CONV_K = 3
D_FF = 2816
CHUNK = 32
EPS = 1e-6
Q_SCALE = HEAD_DIM ** -0.5
N_DEV = 8

ADAM_LR = 0.001
ADAM_B1 = 0.9
ADAM_B2 = 0.999
ADAM_EPS = 1e-08
ADAM_WD = 0.01
ADAM_STEP = 10

VMEM_LIMIT_V7X = 56 * 1024 * 1024
VMEM_LIMIT_LARGE_V7X = 62 * 1024 * 1024

SMALL_ROWS = 16


def _params(sem, vmem=VMEM_LIMIT_V7X):
    return pltpu.CompilerParams(dimension_semantics=sem, vmem_limit_bytes=vmem)


def _mm(a, b):
    return jnp.dot(a.astype(BF16), b.astype(BF16), preferred_element_type=F32)


def _mm_nt(a, b):
    return lax.dot_general(a.astype(BF16), b.astype(BF16), (((1,), (1,)), ((), ())), preferred_element_type=F32)


def _mm_tn(a, b):
    return lax.dot_general(a.astype(BF16), b.astype(BF16), (((0,), (0,)), ((), ())), preferred_element_type=F32)


def _sigmoid(x):
    return 0.5 * jnp.tanh(0.5 * x) + 0.5


def _resident(shape):
    nd = len(shape)
    return pl.BlockSpec(shape, lambda *_: (0,) * nd, pipeline_mode=pl.Buffered(1))


def _full(shape):
    nd = len(shape)
    return pl.BlockSpec(shape, lambda *_: (0,) * nd)


def _shard_cols(w_ref):
    return jnp.concatenate([w_ref[s] for s in range(N_DEV)], axis=1)


N_HG = 4 * HG_WIDTH
N_CV = 3 * CONV_WIDTH
N_GT = 2 * D_MODEL
N_IN = N_HG + N_CV + N_GT


def _col(tm, n):
    return pl.BlockSpec((n, tm), lambda i: (0, i))


HALO = 8


def _fwd_in(x, g, w_in_t, conv_w):
    T = x.shape[0]
    tm = min(512, T)

    def body(x_ref, g_ref, w_ref, cw_ref, ht_ref, hg_ref, cv_ref, gt_ref, cvo_ref, cvot_ref, tail_scr):
        @pl.when(pl.program_id(0) == 0)
        def _():
            tail_scr[...] = jnp.zeros_like(tail_scr)

        xv = x_ref[...]
        r = lax.rsqrt(jnp.mean(xv * xv, axis=-1, keepdims=True) + EPS)
        hf = xv * r * g_ref[...]
        h = hf.astype(BF16)
        ht_ref[...] = hf.T.astype(BF16)
        hg_ref[...] = _mm_nt(h, w_ref[:N_HG, :])
        cv = _mm_nt(h, w_ref[N_HG:N_HG + N_CV, :])
        cv_ref[...] = cv.astype(STASH)
        gt_ref[...] = _mm_nt(h, w_ref[N_HG + N_CV:, :]).astype(STASH)

        u = cv[:, :CONV_WIDTH] * cv[:, 2 * CONV_WIDTH:]
        row = lax.broadcasted_iota(jnp.int32, u.shape, 0)
        prev1 = tail_scr[HALO - 1:HALO, :]
        prev2 = tail_scr[HALO - 2:HALO - 1, :]
        u1 = jnp.where(row >= 1, pltpu.roll(u, 1, 0), prev1)
        u2 = jnp.where(row >= 2, pltpu.roll(u, 2, 0), jnp.where(row == 1, prev1, prev2))
        y = cw_ref[0:1, :] * u2 + cw_ref[1:2, :] * u1 + cw_ref[2:3, :] * u
        out = cv[:, CONV_WIDTH:2 * CONV_WIDTH] * y
        cvo_ref[...] = out.astype(BF16)
        cvot_ref[...] = out.T.astype(BF16)
        tail_scr[...] = u[tm - HALO:, :]

    row = lambda n: pl.BlockSpec((tm, n), lambda i: (i, 0))
    return pl.pallas_call(
        body, name="fwd_in", grid=(T // tm,),
        in_specs=[row(D_MODEL), _full((1, D_MODEL)), _resident(w_in_t.shape), _full((CONV_K, CONV_WIDTH))],
        out_specs=[_col(tm, D_MODEL), row(N_HG), row(N_CV), row(N_GT), row(CONV_WIDTH), _col(tm, CONV_WIDTH)],
        out_shape=[jax.ShapeDtypeStruct((D_MODEL, T), BF16), jax.ShapeDtypeStruct((T, N_HG), F32),
                   jax.ShapeDtypeStruct((T, N_CV), STASH), jax.ShapeDtypeStruct((T, N_GT), STASH),
                   jax.ShapeDtypeStruct((T, CONV_WIDTH), BF16), jax.ShapeDtypeStruct((CONV_WIDTH, T), BF16)],
        scratch_shapes=[pltpu.VMEM((HALO, CONV_WIDTH), F32)],
        compiler_params=_params(("arbitrary",)),
    )(x, g, w_in_t, conv_w)


def _chunk_pos(shape):
    return lax.broadcasted_iota(jnp.int32, shape, 0) & (CHUNK - 1)


def _chunk_cumsum(x, pos):
    s = 1
    while s < CHUNK:
        x = x + jnp.where(pos >= s, pltpu.roll(x, s, 0), 0.0)
        s *= 2
    return x


def _chunk_rev_cumsum(x, pos):
    n = x.shape[0]
    s = 1
    while s < CHUNK:
        x = x + jnp.where(pos + s < CHUNK, pltpu.roll(x, n - s, 0), 0.0)
        s *= 2
    return x


def _lower_bound(low_ref):
    l0 = low_ref[0:1, :]
    l1 = low_ref[1:2, :]
    m = jnp.maximum(l0, l1)
    e0 = jnp.exp(l0 - m)
    e1 = jnp.exp(l1 - m)
    return e0 / (e0 + e1), e1 / (e0 + e1)


def _hg_gates(qr, fr, lb, pos, tb):
    sq = _sigmoid(qr)
    q = qr * sq * Q_SCALE
    sg = _sigmoid(fr)
    f = lb + (1.0 - lb) * sg
    k = 1.0 - f
    b = _chunk_cumsum(jnp.log(f), pos)
    b3 = b.reshape(tb // CHUNK, CHUNK, HEAD_DIM)
    anc = b3[:, CHUNK // 2 - 1:CHUNK // 2, :]
    last = b3[:, CHUNK - 1:CHUNK, :]
    d3 = b3 - anc
    e_qa3 = jnp.exp(d3)
    e_ka3 = jnp.exp(-d3)
    e_b3 = e_qa3 * jnp.exp(anc)
    e_ko3 = e_ka3 * jnp.exp(last - anc)
    dec = jnp.exp(last)
    flat = lambda a: a.reshape(tb, HEAD_DIM)
    return sq, q, sg, f, k, flat(e_qa3), flat(e_ka3), flat(e_b3), flat(e_ko3), dec


def _intra_mask(sb):
    r = lax.broadcasted_iota(jnp.int32, (sb, sb), 0)
    c = lax.broadcasted_iota(jnp.int32, (sb, sb), 1)
    return ((r // CHUNK) == (c // CHUNK)) & (c <= r)


def _hg_fwd(hg, low, gn):
    T = hg.shape[0]
    tb = min(1024, T)
    sb = min(256, tb)
    nb = T // tb
    nc = tb // CHUNK
    wid = HEADS_PER_STEP * HEAD_DIM

    def body(q_ref, f_ref, i_ref, g_ref, low_ref, gn_ref, o_ref, og_ref, ogt_ref, st_ref, s_scr):
        t = pl.program_id(1)

        @pl.when(t == 0)
        def _():
            s_scr[...] = jnp.zeros_like(s_scr)

        pos = _chunk_pos((tb, HEAD_DIM))
        mask = _intra_mask(sb)
        lanes = [slice(hh * HEAD_DIM, (hh + 1) * HEAD_DIM) for hh in range(HEADS_PER_STEP)]
        qi, ko, vb, dec, st = [], [], [], [], []
        for hh, ln in enumerate(lanes):
            lb, _ = _lower_bound(low_ref.at[:, ln])
            _, q, _, _, k, e_qa, e_ka, e_b, e_ko, dec_h = _hg_gates(q_ref[:, ln], f_ref[:, ln], lb, pos, tb)
            qh = (q * e_qa).astype(BF16)
            kh = (k * e_ka).astype(BF16)
            qi.append((q * e_b).astype(BF16))
            ko.append((k * e_ko).astype(BF16))
            vb.append(i_ref[:, ln].astype(BF16))
            dec.append(dec_h)
            st.append(s_scr[hh])
            for s in range(tb // sb):
                sl = slice(s * sb, (s + 1) * sb)
                p = jnp.where(mask, _mm_nt(qh[sl], kh[sl]), 0.0)
                o_ref[sl, ln] = _mm(p, vb[hh][sl])
        for c in range(nc):
            sl = slice(c * CHUNK, (c + 1) * CHUNK)
            for hh, ln in enumerate(lanes):
                st_ref[hh, c] = st[hh]
                o_ref[sl, ln] = o_ref[sl, ln] + _mm_nt(qi[hh][sl], st[hh])
                st[hh] = dec[hh][c] * st[hh] + _mm_tn(vb[hh][sl], ko[hh][sl])
        for hh, ln in enumerate(lanes):
            s_scr[hh] = st[hh]
            o = o_ref[:, ln]
            r = lax.rsqrt(jnp.mean(o * o, axis=-1, keepdims=True) + EPS)
            gr = g_ref[:, ln]
            og = (o * r * gn_ref[...]) * (gr * _sigmoid(gr))
            og_ref[:, ln] = og.astype(BF16)
            ogt_ref[ln, :] = og.T.astype(BF16)

    col = lambda p: pl.BlockSpec((tb, wid), lambda h, t: (t, p * HEAD_GROUPS + h))
    hcol = pl.BlockSpec((tb, wid), lambda h, t: (t, h))
    return pl.pallas_call(
        body, name="hg_fwd", grid=(HEAD_GROUPS, nb),
        in_specs=[col(0), col(1), col(2), col(3), pl.BlockSpec((2, wid), lambda h, t: (0, h)),
                  pl.BlockSpec((1, HEAD_DIM), lambda h, t: (0, 0))],
        out_specs=[hcol, hcol, pl.BlockSpec((wid, tb), lambda h, t: (h, t)),
                   pl.BlockSpec((HEADS_PER_STEP, nc, HEAD_DIM, HEAD_DIM), lambda h, t: (h, t, 0, 0))],
        out_shape=[jax.ShapeDtypeStruct((T, HG_WIDTH), F32), jax.ShapeDtypeStruct((T, HG_WIDTH), BF16),
                   jax.ShapeDtypeStruct((HG_WIDTH, T), BF16),
                   jax.ShapeDtypeStruct((N_HEADS, T // CHUNK, HEAD_DIM, HEAD_DIM), F32)],
        scratch_shapes=[pltpu.VMEM((HEADS_PER_STEP, HEAD_DIM, HEAD_DIM), F32)],
        compiler_params=_params(("parallel", "arbitrary")),
    )(hg, hg, hg, hg, low, gn)


def _merge_fwd(og, cvo, gt, x, wa, wb, wo):
    T = x.shape[0]
    tm = min(1024, T)

    def body(og_ref, cvo_ref, gt_ref, x_ref, wa_ref, wb_ref, wo_ref, x1_ref, mgt_ref):
        ya = jnp.dot(og_ref[...], _shard_cols(wa_ref), preferred_element_type=F32)
        yb = jnp.dot(cvo_ref[...], _shard_cols(wb_ref), preferred_element_type=F32)
        m = (_sigmoid(gt_ref[:, :D_MODEL].astype(F32)) * ya
             + _sigmoid(gt_ref[:, D_MODEL:].astype(F32)) * yb)
        mgt_ref[...] = m.T.astype(BF16)
        x1_ref[...] = x_ref[...] + jnp.dot(m.astype(BF16), wo_ref[...], preferred_element_type=F32)

    row = lambda n: pl.BlockSpec((tm, n), lambda i: (i, 0))
    return pl.pallas_call(
        body, name="merge_fwd", grid=(T // tm,),
        in_specs=[row(HG_WIDTH), row(CONV_WIDTH), row(2 * D_MODEL), row(D_MODEL),
                  _resident(wa.shape), _resident(wb.shape), _resident(wo.shape)],
        out_specs=[row(D_MODEL), _col(tm, D_MODEL)],
        out_shape=[jax.ShapeDtypeStruct((T, D_MODEL), F32), jax.ShapeDtypeStruct((D_MODEL, T), BF16)],
        compiler_params=_params(("parallel",)),
    )(og, cvo, gt, x, wa, wb, wo)


def _ffn_fwd_loss(x1, g, wg, wu, wd, target, g_fin):
    T = x1.shape[0]
    tm = min(512, T)

    def body(x_ref, g_ref, wg_ref, wu_ref, wd_ref, t_ref, gf_ref,
             ht_ref, gate_ref, up_ref, act_ref, loss_ref, dgf_ref, dx2_ref, dx2t_ref):
        @pl.when(pl.program_id(0) == 0)
        def _():
            loss_ref[...] = jnp.zeros_like(loss_ref)
            dgf_ref[...] = jnp.zeros_like(dgf_ref)

        xv = x_ref[...]
        r = lax.rsqrt(jnp.mean(xv * xv, axis=-1, keepdims=True) + EPS)
        hf = xv * r * g_ref[...]
        h = hf.astype(BF16)
        ht_ref[...] = hf.T.astype(BF16)
        gate = _mm_nt(h, wg_ref[...])
        up = _mm_nt(h, wu_ref[...])
        gate_ref[...] = gate.astype(STASH)
        up_ref[...] = up.astype(STASH)
        act = (gate * _sigmoid(gate) * up).astype(BF16)
        act_ref[...] = act
        x2 = xv + jnp.dot(act, wd_ref[...], preferred_element_type=F32)

        gv = gf_ref[...]
        r2 = lax.rsqrt(jnp.mean(x2 * x2, axis=-1, keepdims=True) + EPS)
        xh = x2 * r2
        err = xh * gv - t_ref[...]
        loss_ref[...] += 0.5 * jnp.sum(jnp.mean(err * err, axis=-1, keepdims=True), axis=0, keepdims=True)
        dy = err * (1.0 / D_MODEL)
        dgf_ref[...] += jnp.sum(dy * xh, axis=0, keepdims=True)
        w = dy * gv
        dx2 = r2 * (w - xh * jnp.mean(w * xh, axis=-1, keepdims=True))
        dx2_ref[...] = dx2
        dx2t_ref[...] = dx2.T.astype(BF16)

    row = lambda n: pl.BlockSpec((tm, n), lambda i: (i, 0))
    return pl.pallas_call(
        body, name="ffn_fwd_loss", grid=(T // tm,),
        in_specs=[row(D_MODEL), _full((1, D_MODEL)), _resident(wg.shape), _resident(wu.shape), _resident(wd.shape),
                  row(D_MODEL), _full((1, D_MODEL))],
        out_specs=[_col(tm, D_MODEL), row(D_FF), row(D_FF), row(D_FF), _full((1, 128)), _full((1, D_MODEL)),
                   row(D_MODEL), _col(tm, D_MODEL)],
        out_shape=[jax.ShapeDtypeStruct((D_MODEL, T), BF16), jax.ShapeDtypeStruct((T, D_FF), STASH),
                   jax.ShapeDtypeStruct((T, D_FF), STASH), jax.ShapeDtypeStruct((T, D_FF), BF16),
                   jax.ShapeDtypeStruct((1, 128), F32), jax.ShapeDtypeStruct((1, D_MODEL), F32),
                   jax.ShapeDtypeStruct((T, D_MODEL), F32), jax.ShapeDtypeStruct((D_MODEL, T), BF16)],
        compiler_params=_params(("arbitrary",), vmem=VMEM_LIMIT_LARGE_V7X),
    )(x1, g, wg, wu, wd, target, g_fin)


def _ffn_bwd(dx2, x1, gate, up, g, wg, wu, wd):
    T = x1.shape[0]
    tm = min(512, T)

    def body(dx2_ref, x_ref, gate_ref, up_ref, g_ref, wg_ref, wu_ref, wd_ref, dgate_ref, dup_ref, dx1_ref, dgn_ref):
        @pl.when(pl.program_id(0) == 0)
        def _():
            dgn_ref[...] = jnp.zeros_like(dgn_ref)

        dx2 = dx2_ref[...]
        dact = _mm_nt(dx2, wd_ref[...])
        gate = gate_ref[...].astype(F32)
        s = _sigmoid(gate)
        dgate = (dact * up_ref[...].astype(F32) * (s * (1.0 + gate * (1.0 - s)))).astype(BF16)
        dup = (dact * (gate * s)).astype(BF16)
        dgate_ref[...] = dgate
        dup_ref[...] = dup
        dh = _mm(dgate, wg_ref[...]) + _mm(dup, wu_ref[...])
        xv = x_ref[...]
        r = lax.rsqrt(jnp.mean(xv * xv, axis=-1, keepdims=True) + EPS)
        xh = xv * r
        dgn_ref[...] += jnp.sum(dh * xh, axis=0, keepdims=True)
        w = dh * g_ref[...]
        dx1_ref[...] = dx2 + r * (w - xh * jnp.mean(w * xh, axis=-1, keepdims=True))

    row = lambda n: pl.BlockSpec((tm, n), lambda i: (i, 0))
    return pl.pallas_call(
        body, name="ffn_bwd", grid=(T // tm,),
        in_specs=[row(D_MODEL), row(D_MODEL), row(D_FF), row(D_FF), _full((1, D_MODEL)),
                  _resident(wg.shape), _resident(wu.shape), _resident(wd.shape)],
        out_specs=[row(D_FF), row(D_FF), row(D_MODEL), _full((1, D_MODEL))],
        out_shape=[jax.ShapeDtypeStruct((T, D_FF), BF16), jax.ShapeDtypeStruct((T, D_FF), BF16),
                   jax.ShapeDtypeStruct((T, D_MODEL), F32), jax.ShapeDtypeStruct((1, D_MODEL), F32)],
        compiler_params=_params(("arbitrary",), vmem=VMEM_LIMIT_LARGE_V7X),
    )(dx2, x1, gate, up, g, wg, wu, wd)


def _merge_bwd(dx1, og, cvo, gt, cv, wa, wb, wo, conv_w):
    T = dx1.shape[0]
    tm = min(512, T)
    nt = T // tm

    def body(dx_ref, og_ref, cvo_ref, gt_ref, cv_ref, halo_ref, wa_ref, wb_ref, wo_ref, cw_ref,
             dgt_ref, dya_ref, dyb_ref, dog_ref, dcv_ref, dcw_ref, prev_u, next_dy):
        step = pl.program_id(0)

        @pl.when(step == 0)
        def _():
            next_dy[...] = jnp.zeros_like(next_dy)
            dcw_ref[...] = jnp.zeros_like(dcw_ref)

        dm = _mm_nt(dx_ref[...], wo_ref[...])
        wa = _shard_cols(wa_ref)
        wb = _shard_cols(wb_ref)
        ya = jnp.dot(og_ref[...], wa, preferred_element_type=F32)
        yb = jnp.dot(cvo_ref[...], wb, preferred_element_type=F32)
        sa = _sigmoid(gt_ref[:, :D_MODEL].astype(F32))
        sb = _sigmoid(gt_ref[:, D_MODEL:].astype(F32))
        dgt_ref[:, :D_MODEL] = (dm * ya * (sa * (1.0 - sa))).astype(BF16)
        dgt_ref[:, D_MODEL:] = (dm * yb * (sb * (1.0 - sb))).astype(BF16)
        dya = (dm * sa).astype(BF16)
        dyb = (dm * sb).astype(BF16)
        dya_ref[...] = dya
        dyb_ref[...] = dyb
        dog_ref[...] = _mm_nt(dya, wa)
        dcvo = _mm_nt(dyb, wb)

        cvt = cv_ref[...].astype(F32)
        c, bg, xb = cvt[:, :CONV_WIDTH], cvt[:, CONV_WIDTH:2 * CONV_WIDTH], cvt[:, 2 * CONV_WIDTH:]
        halo = halo_ref[...].astype(F32)
        first_tile = step == nt - 1
        prev_u[...] = jnp.where(first_tile, 0.0, halo[:, :CONV_WIDTH] * halo[:, 2 * CONV_WIDTH:])
        u = c * xb
        row = lax.broadcasted_iota(jnp.int32, u.shape, 0)
        p1 = prev_u[HALO - 1:HALO, :]
        p2 = prev_u[HALO - 2:HALO - 1, :]
        u1 = jnp.where(row >= 1, pltpu.roll(u, 1, 0), p1)
        u2 = jnp.where(row >= 2, pltpu.roll(u, 2, 0), jnp.where(row == 1, p1, p2))
        w0, w1, w2 = cw_ref[0:1, :], cw_ref[1:2, :], cw_ref[2:3, :]
        y = w0 * u2 + w1 * u1 + w2 * u
        dcv_ref[:, CONV_WIDTH:2 * CONV_WIDTH] = (dcvo * y).astype(BF16)
        dy = dcvo * bg
        dcw_ref[0:1, :] += jnp.sum(dy * u2, axis=0, keepdims=True)
        dcw_ref[1:2, :] += jnp.sum(dy * u1, axis=0, keepdims=True)
        dcw_ref[2:3, :] += jnp.sum(dy * u, axis=0, keepdims=True)
        n1 = next_dy[0:1, :]
        n2 = next_dy[1:2, :]
        dy1 = jnp.where(row < tm - 1, pltpu.roll(dy, tm - 1, 0), n1)
        dy2 = jnp.where(row < tm - 2, pltpu.roll(dy, tm - 2, 0), jnp.where(row == tm - 2, n1, n2))
        du = w2 * dy + w1 * dy1 + w0 * dy2
        dcv_ref[:, :CONV_WIDTH] = (du * xb).astype(BF16)
        dcv_ref[:, 2 * CONV_WIDTH:] = (du * c).astype(BF16)
        next_dy[...] = dy[:HALO, :]

    rt = lambda i: nt - 1 - i
    row = lambda n: pl.BlockSpec((tm, n), lambda i: (rt(i), 0))
    halo = pl.BlockSpec((HALO, N_CV), lambda i: (jnp.maximum(rt(i) * (tm // HALO) - 1, 0), 0))
    return pl.pallas_call(
        body, name="merge_bwd", grid=(nt,),
        in_specs=[row(D_MODEL), row(HG_WIDTH), row(CONV_WIDTH), row(2 * D_MODEL), row(N_CV), halo,
                  _resident(wa.shape), _resident(wb.shape), _resident(wo.shape), _full((CONV_K, CONV_WIDTH))],
        out_specs=[row(2 * D_MODEL), row(D_MODEL), row(D_MODEL), row(HG_WIDTH), row(N_CV),
                   _full((CONV_K, CONV_WIDTH))],
        out_shape=[jax.ShapeDtypeStruct((T, 2 * D_MODEL), BF16), jax.ShapeDtypeStruct((T, D_MODEL), BF16),
                   jax.ShapeDtypeStruct((T, D_MODEL), BF16), jax.ShapeDtypeStruct((T, HG_WIDTH), F32),
                   jax.ShapeDtypeStruct((T, N_CV), BF16), jax.ShapeDtypeStruct((CONV_K, CONV_WIDTH), F32)],
        scratch_shapes=[pltpu.VMEM((HALO, CONV_WIDTH), F32), pltpu.VMEM((HALO, CONV_WIDTH), F32)],
        compiler_params=_params(("arbitrary",)),
    )(dx1, og, cvo, gt, cv, cv, wa, wb, wo, conv_w)


def _drop_operands(body, first, count):
    def wrapped(*refs):
        return body(*refs[:first], *refs[first + count:])
    return wrapped


def _hg_bwd(dog, hg, o, st, low, gn, after=()):
    T = hg.shape[0]
    tb = min(512, T)
    sb = min(256, tb)
    nb = T // tb
    nc = tb // CHUNK
    wid = HEADS_PER_STEP * HEAD_DIM

    def body(q_ref, f_ref, i_ref, g_ref, low_ref, gn_ref, o_ref, dog_ref, st_ref,
             dhg_ref, dlow_ref, dgn_ref,
             ds_scr, dqi_scr, dko_scr, dv_scr, dd_scr, dqh_scr, dkh_scr):
        h = pl.program_id(0)
        t = pl.program_id(1)
        dq_ref, df_ref, di_ref, dg_ref = (dhg_ref.at[:, p * HG_WIDTH:(p + 1) * HG_WIDTH] for p in range(4))

        @pl.when(t == 0)
        def _():
            ds_scr[...] = jnp.zeros_like(ds_scr)
            dlow_ref[...] = jnp.zeros_like(dlow_ref)

        @pl.when((t == 0) & (h == 0))
        def _():
            dgn_ref[...] = jnp.zeros_like(dgn_ref)

        pos = _chunk_pos((tb, HEAD_DIM))
        mask = _intra_mask(sb)
        gnv = gn_ref[...]
        lanes = [slice(hh * HEAD_DIM, (hh + 1) * HEAD_DIM) for hh in range(HEADS_PER_STEP)]
        heads = []
        for hh, ln in enumerate(lanes):
            lb, lb1 = _lower_bound(low_ref.at[:, ln])
            qr = q_ref[:, ln]
            sq, q, sg, f, k, e_qa, e_ka, e_b, e_ko, dec = _hg_gates(qr, f_ref[:, ln], lb, pos, tb)

            gr = g_ref[:, ln]
            o = o_ref[:, ln]
            dog_v = dog_ref[:, ln]
            sgr = _sigmoid(gr)
            r = lax.rsqrt(jnp.mean(o * o, axis=-1, keepdims=True) + EPS)
            oh = o * r
            dg_ref[:, ln] = (dog_v * (oh * gnv) * (sgr * (1.0 + gr * (1.0 - sgr)))).astype(BF16)
            don = dog_v * (gr * sgr)
            dgn_ref[...] += jnp.sum(don * oh, axis=0, keepdims=True)
            w = don * gnv
            do = (r * (w - oh * jnp.mean(w * oh, axis=-1, keepdims=True))).astype(BF16)

            qh = (q * e_qa).astype(BF16)
            kh = (k * e_ka).astype(BF16)
            qi = (q * e_b).astype(BF16)
            ko = (k * e_ko).astype(BF16)
            vb = i_ref[:, ln].astype(BF16)

            for s in range(tb // sb):
                sl = slice(s * sb, (s + 1) * sb)
                p = jnp.where(mask, _mm_nt(qh[sl], kh[sl]), 0.0).astype(BF16)
                dp = jnp.where(mask, _mm_nt(do[sl], vb[sl]), 0.0).astype(BF16)
                dv_scr[sl, ln] = _mm_tn(p, do[sl])
                dqh_scr[sl, ln] = _mm(dp, kh[sl])
                dkh_scr[sl, ln] = _mm_tn(dp, qh[sl])
            heads.append(dict(lb=lb, lb1=lb1, qr=qr, sq=sq, q=q, sg=sg, f=f, k=k, e_qa=e_qa, e_ka=e_ka, e_b=e_b,
                              e_ko=e_ko, dec=dec, do=do, qi=qi, ko=ko, vb=vb, ds=ds_scr[hh]))

        for c in reversed(range(nc)):
            sl = slice(c * CHUNK, (c + 1) * CHUNK)
            for hh, ln in enumerate(lanes):
                hd = heads[hh]
                ds = hd["ds"]
                st_c = st_ref[hh, c]
                dqi_scr[sl, ln] = _mm(hd["do"][sl], st_c)
                dko_scr[sl, ln] = _mm(hd["vb"][sl], ds)
                dv_scr[sl, ln] = dv_scr[sl, ln] + _mm_nt(hd["ko"][sl], ds)
                dec_c = hd["dec"][c]
                dd_scr[sl, ln] = jnp.broadcast_to(dec_c * jnp.sum(ds * st_c, axis=0, keepdims=True),
                                                  (CHUNK, HEAD_DIM))
                hd["ds"] = dec_c * ds + _mm_tn(hd["do"][sl], hd["qi"][sl])

        for hh, ln in enumerate(lanes):
            hd = heads[hh]
            ds_scr[hh] = hd["ds"]
            q, k, lb = hd["q"], hd["k"], hd["lb"]
            dko_e = dko_scr[:, ln] * hd["e_ko"]
            dq = dqh_scr[:, ln] * hd["e_qa"] + dqi_scr[:, ln] * hd["e_b"]
            dk = dkh_scr[:, ln] * hd["e_ka"] + dko_e
            kd3 = (k * dko_e).reshape(nc, CHUNK, HEAD_DIM)
            last = jnp.broadcast_to(jnp.sum(kd3, axis=1, keepdims=True), kd3.shape).reshape(tb, HEAD_DIM)
            db = q * dq - k * dk + jnp.where(pos == CHUNK - 1, dd_scr[:, ln] + last, 0.0)
            dlg = _chunk_rev_cumsum(db, pos)
            dfv = dlg / hd["f"] - dk
            s_low = jnp.sum(dfv * (1.0 - hd["sg"]), axis=0, keepdims=True)
            dlow_ref[0:1, ln] += s_low * lb * (1.0 - lb)
            dlow_ref[1:2, ln] += -s_low * lb * hd["lb1"]
            df_ref[:, ln] = (dfv * (1.0 - lb) * hd["sg"] * (1.0 - hd["sg"])).astype(BF16)
            dq_ref[:, ln] = (dq * Q_SCALE * (hd["sq"] * (1.0 + hd["qr"] * (1.0 - hd["sq"])))).astype(BF16)
            di_ref[:, ln] = dv_scr[:, ln].astype(BF16)

    rt = lambda t: nb - 1 - t
    col = lambda p: pl.BlockSpec((tb, wid), lambda h, t: (rt(t), p * HEAD_GROUPS + h))
    hcol = pl.BlockSpec((tb, wid), lambda h, t: (rt(t), h))
    assert HEAD_GROUPS == 1
    tile = pltpu.VMEM((tb, wid), F32)
    return pl.pallas_call(
        _drop_operands(body, 9, len(after)), name="hg_bwd", grid=(HEAD_GROUPS, nb),
        in_specs=[col(0), col(1), col(2), col(3), pl.BlockSpec((2, wid), lambda h, t: (0, h)),
                  pl.BlockSpec((1, HEAD_DIM), lambda h, t: (0, 0)), hcol, hcol,
                  pl.BlockSpec((HEADS_PER_STEP, nc, HEAD_DIM, HEAD_DIM), lambda h, t: (h, rt(t), 0, 0))]
                 + [HBM_SPEC] * len(after),
        out_specs=[pl.BlockSpec((tb, N_HG), lambda h, t: (rt(t), 0)), pl.BlockSpec((2, wid), lambda h, t: (0, h)),
                   pl.BlockSpec((1, HEAD_DIM), lambda h, t: (0, 0))],
        out_shape=[jax.ShapeDtypeStruct((T, N_HG), BF16), jax.ShapeDtypeStruct((2, HG_WIDTH), F32),
                   jax.ShapeDtypeStruct((1, HEAD_DIM), F32)],
        scratch_shapes=[pltpu.VMEM((HEADS_PER_STEP, HEAD_DIM, HEAD_DIM), F32), tile, tile, tile, tile, tile, tile],
        compiler_params=_params(("arbitrary", "arbitrary")),
    )(hg, hg, hg, hg, low, gn, o, dog, st, *after)


def _in_bwd(dparts, w_in, x, dx1, g, after=()):
    T = x.shape[0]
    tm = min(512, T)
    widths = [p.shape[1] for p in dparts]
    offs = [sum(widths[:i]) for i in range(len(widths))]
    n = len(dparts)

    def body(*refs):
        d_refs = refs[:n]
        w_ref, x_ref, dx1_ref, g_ref, dx_ref, dgn_ref = refs[n:]

        @pl.when(pl.program_id(0) == 0)
        def _():
            dgn_ref[...] = jnp.zeros_like(dgn_ref)

        dh = None
        for d_ref, off, wd in zip(d_refs, offs, widths):
            part = _mm(d_ref[...], w_ref[off:off + wd, :])
            dh = part if dh is None else dh + part
        xv = x_ref[...]
        r = lax.rsqrt(jnp.mean(xv * xv, axis=-1, keepdims=True) + EPS)
        xh = xv * r
        dgn_ref[...] += jnp.sum(dh * xh, axis=0, keepdims=True)
        w = dh * g_ref[...]
        dx_ref[...] = dx1_ref[...] + r * (w - xh * jnp.mean(w * xh, axis=-1, keepdims=True))

    row = lambda m: pl.BlockSpec((tm, m), lambda i: (i, 0))
    return pl.pallas_call(
        _drop_operands(body, n + 4, len(after)), name="in_bwd", grid=(T // tm,),
        in_specs=[row(wd) for wd in widths] + [_resident(w_in.shape), row(D_MODEL), row(D_MODEL), _full((1, D_MODEL))]
                 + [HBM_SPEC] * len(after),
        out_specs=[row(D_MODEL), _full((1, D_MODEL))],
        out_shape=[jax.ShapeDtypeStruct((T, D_MODEL), F32), jax.ShapeDtypeStruct((1, D_MODEL), F32)],
        compiler_params=_params(("arbitrary",)),
    )(*dparts, w_in, x, dx1, g, *after)


def _wgrad_ffn(h2t, dgate, dup, dx2t, act, tn=256):
    M, T = h2t.shape
    N = dgate.shape[1]

    def body(h_ref, x_ref, dg_ref, du_ref, act_ref, og_ref, ou_ref, od_ref):
        h = h_ref[...]
        og_ref[...] = _mm(h, dg_ref[...]).T.astype(BF16)
        ou_ref[...] = _mm(h, du_ref[...]).T.astype(BF16)
        od_ref[...] = _mm(x_ref[...], act_ref[...]).T.astype(BF16)

    rhs = pl.BlockSpec((T, tn), lambda j: (0, j))
    out_spec = pl.BlockSpec((tn, M), lambda j: (j, 0))
    out = jax.ShapeDtypeStruct((N, M), BF16)
    return pl.pallas_call(
        body, name="wgrad_ffn", grid=(N // tn,),
        in_specs=[_resident((M, T)), _resident((M, T)), rhs, rhs, rhs], out_specs=[out_spec] * 3,
        out_shape=[out, out, out],
        compiler_params=_params(("parallel",)),
    )(h2t, dx2t, dgate, dup, act)


def _wgrad_out_branches(ogt, dya, cvot, dyb, mgt, dx1, after=()):
    M, T = ogt.shape
    N = dya.shape[1]
    c = N // N_DEV
    per = 2
    tn = per * c

    def body(at_ref, da_ref, bt_ref, db_ref, mt_ref, dx_ref, oa_ref, ob_ref, oo_ref):
        ga = _mm(at_ref[...], da_ref[...])
        gb = _mm(bt_ref[...], db_ref[...])
        for s in range(per):
            oa_ref[s] = ga[:, s * c:(s + 1) * c].astype(BF16)
            ob_ref[s] = gb[:, s * c:(s + 1) * c].astype(BF16)
        oo_ref[...] = _mm(mt_ref[...], dx_ref[...]).astype(BF16)

    rhs = pl.BlockSpec((T, tn), lambda j: (0, j))
    owners = pl.BlockSpec((per, M, c), lambda j: (j, 0, 0))
    out = jax.ShapeDtypeStruct((N_DEV, M, c), BF16)
    return pl.pallas_call(
        _drop_operands(body, 6, len(after)), name="wgrad_out_branches", grid=(N // tn,),
        in_specs=[_resident((M, T)), rhs, _resident((M, T)), rhs, _resident(mgt.shape), rhs] + [HBM_SPEC] * len(after),
        out_specs=[owners, owners, pl.BlockSpec((mgt.shape[0], tn), lambda j: (0, j))],
        out_shape=[out, out, jax.ShapeDtypeStruct((mgt.shape[0], dx1.shape[1]), BF16)],
        compiler_params=_params(("parallel",)),
    )(ogt, dya, cvot, dyb, mgt, dx1, *after)


def _wgrad_in(ht, dparts, after=()):
    M, T = ht.shape
    tn = 512
    nblk = [p.shape[1] // tn for p in dparts]
    start = [sum(nblk[:i]) for i in range(len(nblk))]
    n = len(dparts)

    def body(a_ref, *refs):
        d_refs, o_ref = refs[:n], refs[n]
        j = pl.program_id(0)
        for d_ref, s, nb in zip(d_refs, start, nblk):
            @pl.when((j >= s) & (j < s + nb))
            def _():
                o_ref[...] = _mm(a_ref[...], d_ref[...]).T.astype(BF16)

    def piece_spec(s, nb):
        return pl.BlockSpec((T, tn), lambda j: (0, jnp.clip(j - s, 0, nb - 1)))

    return pl.pallas_call(
        _drop_operands(body, 1 + n, len(after)), name="wgrad_in", grid=(sum(nblk),),
        in_specs=[_resident((M, T))] + [piece_spec(s, nb) for s, nb in zip(start, nblk)] + [HBM_SPEC] * len(after),
        out_specs=pl.BlockSpec((tn, M), lambda j: (j, 0)),
        out_shape=jax.ShapeDtypeStruct((sum(nblk) * tn, M), BF16),
        compiler_params=_params(("parallel",)),
    )(ht, *dparts, *after)


def _adamw_math(w, g, m, v):
    m = ADAM_B1 * m + (1.0 - ADAM_B1) * g
    v = ADAM_B2 * v + (1.0 - ADAM_B2) * (g * g)
    m_hat = m / (1.0 - ADAM_B1 ** ADAM_STEP)
    v_hat = v / (1.0 - ADAM_B2 ** ADAM_STEP)
    delta = -ADAM_LR * (m_hat / (jnp.sqrt(v_hat) + ADAM_EPS) + ADAM_WD * w)
    return delta, m, v


def _adamw_sum(name, w, parts, m, v):
    R, C = w.shape
    tr = _row_tile(R)

    def body(w_ref, p_ref, m_ref, v_ref, g_out, d_out, m_out, v_out):
        g = p_ref[0].astype(F32)
        for k in range(1, 4):
            g = g + p_ref[k].astype(F32)
        g_out[...] = g
        d_out[...], m_out[...], v_out[...] = _adamw_math(w_ref[...], g, m_ref[...], v_ref[...])

    blk = pl.BlockSpec((tr, C), lambda i: (i, 0))
    out = jax.ShapeDtypeStruct((R, C), F32)
    return pl.pallas_call(
        body, name=name, grid=(R // tr,),
        in_specs=[blk, pl.BlockSpec((4, tr, C), lambda i: (0, i, 0)), blk, blk],
        out_specs=[blk, blk, blk, blk], out_shape=[out, out, out, out],
        compiler_params=_params(("parallel",)),
    )(w, parts, m, v)


_SMALL_SLOTS = (("norm_mix_g", 0, 1, 1024), ("norm_ffn_g", 1, 1, 1024), ("norm_final_g", 2, 1, 1024),
                ("lower_bounds", 3, 2, 512), ("hg_norm_g", 5, 1, 128), ("loss", 6, 1, 128), ("conv_w", 8, 3, 512))
_SMALL_PARAMS = tuple(s for s in _SMALL_SLOTS if s[0] != "loss")
CONV_SHARD = CONV_WIDTH // N_DEV


def _small_pack(small):
    def body(*refs):
        out = refs[-1]
        out[...] = jnp.zeros_like(out)
        for ref, (_, row, rows, lanes) in zip(refs[:-1], _SMALL_SLOTS):
            out[row:row + rows, 0:lanes] = ref[...]

    vmem = pl.BlockSpec(memory_space=pltpu.VMEM)
    return pl.pallas_call(
        body, name="small_pack", in_specs=[vmem] * len(_SMALL_SLOTS), out_specs=vmem,
        out_shape=jax.ShapeDtypeStruct((SMALL_ROWS, 1024), F32),
    )(*[small[name] for name, _, _, _ in _SMALL_SLOTS])


def _small_update(gathered, dev, w, m, v):
    n = len(_SMALL_PARAMS)

    def body(dev_ref, g_ref, *refs):
        w_refs, m_refs, v_refs = refs[:n], refs[n:2 * n], refs[2 * n:3 * n]
        loss_ref, out_refs, sum_scr = refs[3 * n], refs[3 * n + 1:-1], refs[-1]
        total = g_ref[0]
        for k in range(1, N_DEV):
            total = total + g_ref[k]
        sum_scr[...] = total
        loss_ref[...] = sum_scr[6:7, 0:128]
        for p, (name, row, rows, lanes) in enumerate(_SMALL_PARAMS):
            if name == "conv_w":
                g = sum_scr[row:row + rows, 0:CONV_SHARD]
                for s in range(1, N_DEV):
                    g = jnp.where(dev_ref[0] == s, sum_scr[row:row + rows, s * CONV_SHARD:(s + 1) * CONV_SHARD], g)
            else:
                g = sum_scr[row:row + rows, 0:lanes]
            delta, m_new, v_new = _adamw_math(w_refs[p][...], g, m_refs[p][...], v_refs[p][...])
            out_refs[4 * p][...] = g
            out_refs[4 * p + 1][...] = delta
            out_refs[4 * p + 2][...] = m_new
            out_refs[4 * p + 3][...] = v_new

    vmem = pl.BlockSpec(memory_space=pltpu.VMEM)
    outs = [jax.ShapeDtypeStruct((1, 128), F32)]
    for a in w:
        outs += [jax.ShapeDtypeStruct(a.shape, F32)] * 4
    return pl.pallas_call(
        body, name="small_update",
        in_specs=[pl.BlockSpec(memory_space=pltpu.SMEM)] + [vmem] * (1 + 3 * n), out_specs=[vmem] * len(outs),
        out_shape=outs, scratch_shapes=[pltpu.VMEM((SMALL_ROWS, 1024), F32)],
    )(dev, gathered, *w, *m, *v)


def _row_tile(rows):
    for parts in (4, 2):
        if rows % (16 * parts) == 0:
            return rows // parts
    return rows


def _pair_sum(name, by_owner, got, core, after=()):
    n = len(got)

    def body(core_ref, *refs):
        for a_ref, b_ref, o_ref in zip(refs[:n], refs[n:2 * n], refs[2 * n:]):
            o_ref[...] = (a_ref[...].astype(F32) + b_ref[...].astype(F32)).astype(BF16)

    def blk(g):
        return pl.BlockSpec((None,) + g.shape[1:], lambda k, core_ref: (k, 0, 0))

    def mine(g):
        return pl.BlockSpec((None,) + g.shape[1:], lambda k, core_ref: (2 * k + core_ref[0], 0, 0))

    return pl.pallas_call(
        _drop_operands(body, 1 + 2 * n, len(after)), name=name,
        grid_spec=pltpu.PrefetchScalarGridSpec(
            num_scalar_prefetch=1, grid=(4,),
            in_specs=[mine(g) for g in got] + [blk(g) for g in got] + [HBM_SPEC] * len(after),
            out_specs=[blk(g) for g in got]),
        out_shape=[jax.ShapeDtypeStruct(g.shape, BF16) for g in got],
        compiler_params=_params(("parallel",)),
    )(core, *by_owner, *got, *after)


MESH = pl.DeviceIdType.MESH
HBM_SPEC = pl.BlockSpec(memory_space=pl.ANY)


def _handshake(peers):
    barrier = pltpu.get_barrier_semaphore()
    for peer in peers:
        pl.semaphore_signal(barrier, inc=1, device_id=peer, device_id_type=MESH)
    pl.semaphore_wait(barrier, len(peers))


def _comm_call(body, name, operands, out_shape, scratch, collective_id):
    if collective_id is None:
        return pl.pallas_call(body, name=name, in_specs=[HBM_SPEC] * len(operands), out_specs=[HBM_SPEC] * len(out_shape),
                              out_shape=out_shape, scratch_shapes=scratch)(*operands)
    return pl.kernel(body, out_type=out_shape, mesh=plsc.ScalarSubcoreMesh(axis_name="sequencer", num_cores=1),
                     scratch_types=scratch, name=name,
                     compiler_params=pltpu.CompilerParams(collective_id=collective_id))(*operands)


def _all_gather(name, blocks, collective_id=None, after=()):
    n = len(blocks)
    na = len(after)

    def body(*refs):
        x_refs, out_refs = refs[:n], refs[n + na:2 * n + na]
        send_sems, recv_sems, local_sems = refs[2 * n + na:]
        x, y, c = lax.axis_index("x"), lax.axis_index("y"), lax.axis_index("c")
        me, sibling = (x, y, c), (x, y, 1 - c)
        chips = [(1 - x, y), (x, 1 - y), (1 - x, 1 - y)]
        if collective_id is not None:
            _handshake([sibling] + [(*chip, c) for chip in chips])

        def slot(i, px, py, pc):
            return out_refs[i].at[4 * px + 2 * py + pc]

        def copy(i, k, blk, to, src=None):
            return pltpu.make_async_remote_copy(
                src_ref=slot(i, *blk) if src is None else src, dst_ref=slot(i, *blk),
                send_sem=send_sems.at[7 * i + k], recv_sem=recv_sems.at[7 * i + k], device_id=to, device_id_type=MESH)

        mine = [pltpu.make_async_copy(x_refs[i], slot(i, *me), local_sems.at[i]) for i in range(n)]
        for cp in mine:
            cp.start()
        first = []
        for i in range(n):
            first.append(copy(i, 0, me, sibling, src=x_refs[i]))
            first += [copy(i, 1 + j, me, (*chip, c), src=x_refs[i]) for j, chip in enumerate(chips)]
        for cp in first:
            cp.start()
        passed = []
        for i in range(n):
            for j, chip in enumerate(chips):
                copy(i, 1 + j, (*chip, c), me).wait_recv()
                passed.append(copy(i, 4 + j, (*chip, c), sibling))
                passed[-1].start()
        for i in range(n):
            copy(i, 0, sibling, me).wait_recv()
            for j, chip in enumerate(chips):
                copy(i, 4 + j, (*chip, 1 - c), me).wait_recv()
        for cp in first + passed:
            cp.wait_send()
        for cp in mine:
            cp.wait()

    return _comm_call(
        body, name, list(blocks) + list(after), [jax.ShapeDtypeStruct((N_DEV,) + b.shape, b.dtype) for b in blocks],
        [pltpu.SemaphoreType.DMA((7 * n,)), pltpu.SemaphoreType.DMA((7 * n,)), pltpu.SemaphoreType.DMA((n,))],
        collective_id)


def _sibling_swap(name, by_owner, collective_id=None, after=()):
    n = len(by_owner)
    na = len(after)

    def body(*refs):
        x_refs, out_refs = refs[:n], refs[n + na:2 * n + na]
        send_sems, recv_sems = refs[2 * n + na:]
        x, y, c = lax.axis_index("x"), lax.axis_index("y"), lax.axis_index("c")
        if collective_id is not None:
            _handshake([(x, y, 1 - c)])
        copies = []
        for i in range(n):
            for k in range(4):
                copies.append(pltpu.make_async_remote_copy(
                    src_ref=x_refs[i].at[2 * k + 1 - c], dst_ref=out_refs[i].at[k],
                    send_sem=send_sems.at[4 * i + k], recv_sem=recv_sems.at[4 * i + k],
                    device_id=(x, y, 1 - c), device_id_type=MESH))
        for cp in copies:
            cp.start()
        for cp in copies:
            cp.wait()

    return _comm_call(
        body, name, list(by_owner) + list(after),
        [jax.ShapeDtypeStruct((4,) + b.shape[1:], b.dtype) for b in by_owner],
        [pltpu.SemaphoreType.DMA((4 * n,)), pltpu.SemaphoreType.DMA((4 * n,))], collective_id)


def _chip_exchange(name, sums, collective_id=None, after=()):
    n = len(sums)
    na = len(after)

    def body(*refs):
        x_refs, out_refs = refs[:n], refs[n + na:2 * n + na]
        send_sems, recv_sems, local_sems = refs[2 * n + na:]
        x, y, c = lax.axis_index("x"), lax.axis_index("y"), lax.axis_index("c")
        chips = [(1 - x, y), (x, 1 - y), (1 - x, 1 - y)]
        my_chip = 2 * x + y
        if collective_id is not None:
            _handshake([(cx, cy, c) for cx, cy in chips])
        mine = [pltpu.make_async_copy(x_refs[i].at[my_chip], out_refs[i].at[my_chip], local_sems.at[i])
                for i in range(n)]
        for cp in mine:
            cp.start()
        sends = []
        for i in range(n):
            for j, (cx, cy) in enumerate(chips):
                sends.append(pltpu.make_async_remote_copy(
                    src_ref=x_refs[i].at[2 * cx + cy], dst_ref=out_refs[i].at[my_chip],
                    send_sem=send_sems.at[3 * i + j], recv_sem=recv_sems.at[3 * i + j],
                    device_id=(cx, cy, c), device_id_type=MESH))
        for cp in sends:
            cp.start()
        for i in range(n):
            for j, (cx, cy) in enumerate(chips):
                pltpu.make_async_remote_copy(
                    src_ref=x_refs[i].at[my_chip], dst_ref=out_refs[i].at[2 * cx + cy],
                    send_sem=send_sems.at[3 * i + j], recv_sem=recv_sems.at[3 * i + j],
                    device_id=(cx, cy, c), device_id_type=MESH).wait_recv()
        for cp in sends:
            cp.wait_send()
        for cp in mine:
            cp.wait()

    return _comm_call(
        body, name, list(sums) + list(after), [jax.ShapeDtypeStruct(s.shape, s.dtype) for s in sums],
        [pltpu.SemaphoreType.DMA((3 * n,)), pltpu.SemaphoreType.DMA((3 * n,)), pltpu.SemaphoreType.DMA((n,))],
        collective_id)


def _cast_shards(shards):
    n = len(shards)

    def body(*refs):
        for i in range(n):
            refs[n + i][...] = refs[i][...].astype(BF16)

    vmem = pl.BlockSpec(memory_space=pltpu.VMEM)
    return pl.pallas_call(
        body, name="cast_shards", in_specs=[vmem] * n, out_specs=[vmem] * n,
        out_shape=[jax.ShapeDtypeStruct(s.shape, BF16) for s in shards],
        compiler_params=pltpu.CompilerParams(vmem_limit_bytes=VMEM_LIMIT_V7X),
    )(*shards)


BIG = ("w_in", "w_branch_a", "w_branch_b", "w_out", "w_ffn_gate", "w_ffn_up", "w_ffn_down")


def _local_step(x, target, gains, low, conv_w, wg8, reduce):
    g_mix, g_hg, g_ffn, g_fin = gains
    w_in = wg8["w_in"].reshape(N_IN, D_MODEL)
    wg = wg8["w_ffn_gate"].reshape(D_FF, D_MODEL)
    wu = wg8["w_ffn_up"].reshape(D_FF, D_MODEL)
    wa, wb = wg8["w_branch_a"], wg8["w_branch_b"]
    wo = wg8["w_out"].reshape(D_MODEL, D_MODEL)
    wd = wg8["w_ffn_down"].reshape(D_FF, D_MODEL)

    ht, hg, cv, gt, cvo, cvot = _fwd_in(x, g_mix, w_in, conv_w)
    o, og, ogt, st = _hg_fwd(hg, low, g_hg)
    x1, mgt = _merge_fwd(og, cvo, gt, x, wa, wb, wo)
    h2t, gate, up, act, loss, d_gfin, dx2, dx2t = _ffn_fwd_loss(x1, g_ffn, wg, wu, wd, target, g_fin)

    dgate, dup, dx1, d_gffn = _ffn_bwd(dx2, x1, gate, up, g_ffn, wg, wu, wd)
    d_wg, d_wu, d_wd = _wgrad_ffn(h2t, dgate, dup, dx2t, act)
    by_owner_ffn = lambda a: a.reshape(N_DEV, D_FF // N_DEV, D_MODEL)
    ffn = dict(w_ffn_down=by_owner_ffn(d_wd), w_ffn_gate=by_owner_ffn(d_wg), w_ffn_up=by_owner_ffn(d_wu))
    dgt, dya, dyb, dog, dcv, d_conv = _merge_bwd(dx1, og, cvo, gt, cv, wa, wb, wo, conv_w)
    sums_ffn, got_ffn = reduce.begin(ffn, sum_after=[dya])
    parts_ffn, updated_ffn = reduce.finish(ffn, sums_ffn)
    grad_a, grad_b, grad_o = _wgrad_out_branches(ogt, dya, cvot, dyb, mgt, dx1, after=sums_ffn[:1])
    out = dict(w_out=grad_o.reshape(N_DEV, D_MODEL // N_DEV, D_MODEL), w_branch_a=grad_a, w_branch_b=grad_b)
    dhg, d_low, d_ghg = _hg_bwd(dog, hg, o, st, low, g_hg, after=list(sums_ffn) + [out["w_out"]])
    sums_out, got_out = reduce.begin(out, after=[parts_ffn[0], dhg], sum_after=updated_ffn)
    parts_out, updated_out = reduce.finish(out, sums_out)
    dparts = [dhg, dcv, dgt]
    w_in_grad = dict(w_in=_wgrad_in(ht, dparts, after=sums_out[:1]).reshape(N_DEV, N_IN // N_DEV, D_MODEL))
    sums_in, _ = reduce.begin(w_in_grad, after=parts_out[:1], sum_after=updated_out)
    parts_in, _ = reduce.finish(w_in_grad, sums_in)
    grad_x, d_gmix = _in_bwd(dparts, w_in, x, dx1, g_mix, after=list(parts_out[:1]) + list(sums_in))
    small = dict(norm_mix_g=d_gmix, norm_ffn_g=d_gffn, norm_final_g=d_gfin, lower_bounds=d_low, hg_norm_g=d_ghg,
                 conv_w=d_conv, loss=loss)
    return grad_x, small, parts_in


def _conv_shard_rows(a):
    return jnp.pad(a, ((0, 5), (0, 64)))


def kernel(x, norm_mix_g, w_in, lower_bounds, hg_norm_g, conv_w, w_branch_a, w_branch_b, w_out, norm_ffn_g, w_ffn_gate, w_ffn_up, w_ffn_down, norm_final_g, loss_target, m_norm_mix_g, m_w_in, m_lower_bounds, m_hg_norm_g, m_conv_w, m_w_branch_a, m_w_branch_b, m_w_out, m_norm_ffn_g, m_w_ffn_gate, m_w_ffn_up, m_w_ffn_down, m_norm_final_g, v_norm_mix_g, v_w_in, v_lower_bounds, v_hg_norm_g, v_conv_w, v_w_branch_a, v_w_branch_b, v_w_out, v_norm_ffn_g, v_w_ffn_gate, v_w_ffn_up, v_w_ffn_down, v_norm_final_g):
    cx, cy, cc = lax.axis_index("x"), lax.axis_index("y"), lax.axis_index("c")
    my_dev = 4 * cx + 2 * cy + cc

    def tr(a):
        return a[0].T

    big = dict(w_in=tr(w_in), w_branch_a=w_branch_a[0], w_branch_b=w_branch_b[0], w_out=w_out[0],
               w_ffn_gate=tr(w_ffn_gate), w_ffn_up=tr(w_ffn_up), w_ffn_down=w_ffn_down[0])
    big_m = dict(w_in=tr(m_w_in), w_branch_a=m_w_branch_a[0], w_branch_b=m_w_branch_b[0], w_out=m_w_out[0],
                 w_ffn_gate=tr(m_w_ffn_gate), w_ffn_up=tr(m_w_ffn_up), w_ffn_down=m_w_ffn_down[0])
    big_v = dict(w_in=tr(v_w_in), w_branch_a=v_w_branch_a[0], w_branch_b=v_w_branch_b[0], w_out=v_w_out[0],
                 w_ffn_gate=tr(v_w_ffn_gate), w_ffn_up=tr(v_w_ffn_up), w_ffn_down=v_w_ffn_down[0])
    transposed = ("w_in", "w_ffn_gate", "w_ffn_up")

    shards = dict(zip(BIG, _cast_shards([big[n] for n in BIG])))
    first = _all_gather("gather_w_in", [shards["w_in"], _conv_shard_rows(conv_w[0])])
    ids = iter(range(1, 16))
    mid = _all_gather("gather_mid", [shards[n] for n in BIG[1:4]], collective_id=next(ids), after=first[1:])
    ffn = _all_gather("gather_ffn", [shards[n] for n in BIG[4:]], collective_id=next(ids), after=first[1:])
    wg8 = dict(zip(BIG, [first[0]] + list(mid) + list(ffn)))
    conv_full = first[1][:, :3, :64].transpose(1, 0, 2).reshape(3, CONV_WIDTH)

    core = cc.reshape(1).astype(jnp.int32)
    outs = {}

    class Reduce:
        @staticmethod
        def begin(grads, after=(), sum_after=()):
            names = list(grads)
            by_owner = [grads[n] for n in names]
            got = _sibling_swap("sibling_swap_" + names[0], by_owner, collective_id=next(ids), after=after)
            sums = _pair_sum("pair_sum_" + names[0], by_owner, got, core, after=sum_after)
            return sums, got

        @staticmethod
        def finish(grads, chip_sums, after=()):
            names = list(grads)
            parts = _chip_exchange("chip_exchange_" + names[0], chip_sums, collective_id=next(ids), after=after)
            for n, p in zip(names, parts):
                outs[n] = _adamw_sum("adamw_" + n, big[n], p, big_m[n], big_v[n])
            return parts, [outs[n][1] for n in names]

    gains = (norm_mix_g, hg_norm_g, norm_ffn_g, norm_final_g.reshape(1, D_MODEL))
    grad_x, small, last = _local_step(x[0], loss_target[0], gains, lower_bounds, conv_full, wg8, Reduce)

    small_all = _all_gather("gather_small", [_small_pack(small)], collective_id=next(ids), after=last[:1])

    def small_state(a):
        return [a[0], a[1], a[2].reshape(1, D_MODEL), a[3], a[4], a[5][0]]

    upd = _small_update(
        small_all[0], my_dev.reshape(1).astype(jnp.int32),
        small_state((norm_mix_g, norm_ffn_g, norm_final_g, lower_bounds, hg_norm_g, conv_w)),
        small_state((m_norm_mix_g, m_norm_ffn_g, m_norm_final_g, m_lower_bounds, m_hg_norm_g, m_conv_w)),
        small_state((v_norm_mix_g, v_norm_ffn_g, v_norm_final_g, v_lower_bounds, v_hg_norm_g, v_conv_w)))
    loss = upd[0][0, 0]
    small_shape = dict(norm_final_g=(D_MODEL,), conv_w=(1, 3, CONV_SHARD))
    for p, (name, _, _, _) in enumerate(_SMALL_PARAMS):
        outs[name] = [a.reshape(small_shape.get(name, a.shape)) for a in upd[1 + 4 * p:5 + 4 * p]]

    order = ["norm_mix_g", "w_in", "lower_bounds", "hg_norm_g", "conv_w", "w_branch_a", "w_branch_b", "w_out",
             "norm_ffn_g", "w_ffn_gate", "w_ffn_up", "w_ffn_down", "norm_final_g"]
    result = [loss, grad_x[None]]
    for k in range(4):
        for n in order:
            if n in BIG:
                result.append((outs[n][k].T if n in transposed else outs[n][k])[None])
            else:
                result.append(outs[n][k])
    return tuple(result)
```

```python
import jax
import jax.numpy as jnp
from jax import lax
from jax.experimental import pallas as pl
from jax.experimental.pallas import tpu as pltpu
from jax.experimental.pallas import tpu_sc as plsc

F32 = jnp.float32
BF16 = jnp.bfloat16
STASH = jnp.bfloat16

D_MODEL = 1024
HG_WIDTH = 512
HEAD_DIM = 128
N_HEADS = 4
HEADS_PER_STEP = 4
HEAD_GROUPS = N_HEADS // HEADS_PER_STEP
CONV_WIDTH = 512
CONV_K = 3
D_FF = 2816
CHUNK = 32
EPS = 1e-6
Q_SCALE = HEAD_DIM ** -0.5
N_DEV = 8

ADAM_LR = 0.001
ADAM_B1 = 0.9
ADAM_B2 = 0.999
ADAM_EPS = 1e-08
ADAM_WD = 0.01
ADAM_STEP = 10

VMEM_LIMIT_V7X = 56 * 1024 * 1024
VMEM_LIMIT_LARGE_V7X = 62 * 1024 * 1024

SMALL_ROWS = 16


def _params(sem, vmem=VMEM_LIMIT_V7X):
    return pltpu.CompilerParams(dimension_semantics=sem, vmem_limit_bytes=vmem)


def _mm(a, b):
    return jnp.dot(a.astype(BF16), b.astype(BF16), preferred_element_type=F32)


def _mm_nt(a, b):
    return lax.dot_general(a.astype(BF16), b.astype(BF16), (((1,), (1,)), ((), ())), preferred_element_type=F32)


def _mm_tn(a, b):
    return lax.dot_general(a.astype(BF16), b.astype(BF16), (((0,), (0,)), ((), ())), preferred_element_type=F32)


def _sigmoid(x):
    return 0.5 * jnp.tanh(0.5 * x) + 0.5


def _resident(shape):
    nd = len(shape)
    return pl.BlockSpec(shape, lambda *_: (0,) * nd, pipeline_mode=pl.Buffered(1))


def _full(shape):
    nd = len(shape)
    return pl.BlockSpec(shape, lambda *_: (0,) * nd)


def _shard_cols(w_ref):
    return jnp.concatenate([w_ref[s] for s in range(N_DEV)], axis=1)


N_HG = 4 * HG_WIDTH
N_CV = 3 * CONV_WIDTH
N_GT = 2 * D_MODEL
N_IN = N_HG + N_CV + N_GT


def _col(tm, n):
    return pl.BlockSpec((n, tm), lambda i: (0, i))


HALO = 8


def _fwd_in(x, g, w_in_t, conv_w):
    T = x.shape[0]
    tm = min(512, T)

    def body(x_ref, g_ref, w_ref, cw_ref, ht_ref, hg_ref, cv_ref, gt_ref, cvo_ref, cvot_ref, tail_scr):
        @pl.when(pl.program_id(0) == 0)
        def _():
            tail_scr[...] = jnp.zeros_like(tail_scr)

        xv = x_ref[...]
        r = lax.rsqrt(jnp.mean(xv * xv, axis=-1, keepdims=True) + EPS)
        hf = xv * r * g_ref[...]
        h = hf.astype(BF16)
        ht_ref[...] = hf.T.astype(BF16)
        hg_ref[...] = _mm_nt(h, w_ref[:N_HG, :])
        cv = _mm_nt(h, w_ref[N_HG:N_HG + N_CV, :])
        cv_ref[...] = cv.astype(STASH)
        gt_ref[...] = _mm_nt(h, w_ref[N_HG + N_CV:, :]).astype(STASH)

        u = cv[:, :CONV_WIDTH] * cv[:, 2 * CONV_WIDTH:]
        row = lax.broadcasted_iota(jnp.int32, u.shape, 0)
        prev1 = tail_scr[HALO - 1:HALO, :]
        prev2 = tail_scr[HALO - 2:HALO - 1, :]
        u1 = jnp.where(row >= 1, pltpu.roll(u, 1, 0), prev1)
        u2 = jnp.where(row >= 2, pltpu.roll(u, 2, 0), jnp.where(row == 1, prev1, prev2))
        y = cw_ref[0:1, :] * u2 + cw_ref[1:2, :] * u1 + cw_ref[2:3, :] * u
        out = cv[:, CONV_WIDTH:2 * CONV_WIDTH] * y
        cvo_ref[...] = out.astype(BF16)
        cvot_ref[...] = out.T.astype(BF16)
        tail_scr[...] = u[tm - HALO:, :]

    row = lambda n: pl.BlockSpec((tm, n), lambda i: (i, 0))
    return pl.pallas_call(
        body, name="fwd_in", grid=(T // tm,),
        in_specs=[row(D_MODEL), _full((1, D_MODEL)), _resident(w_in_t.shape), _full((CONV_K, CONV_WIDTH))],
        out_specs=[_col(tm, D_MODEL), row(N_HG), row(N_CV), row(N_GT), row(CONV_WIDTH), _col(tm, CONV_WIDTH)],
        out_shape=[jax.ShapeDtypeStruct((D_MODEL, T), BF16), jax.ShapeDtypeStruct((T, N_HG), F32),
                   jax.ShapeDtypeStruct((T, N_CV), STASH), jax.ShapeDtypeStruct((T, N_GT), STASH),
                   jax.ShapeDtypeStruct((T, CONV_WIDTH), BF16), jax.ShapeDtypeStruct((CONV_WIDTH, T), BF16)],
        scratch_shapes=[pltpu.VMEM((HALO, CONV_WIDTH), F32)],
        compiler_params=_params(("arbitrary",)),
    )(x, g, w_in_t, conv_w)


def _chunk_pos(shape):
    return lax.broadcasted_iota(jnp.int32, shape, 0) & (CHUNK - 1)


def _chunk_cumsum(x, pos):
    s = 1
    while s < CHUNK:
        x = x + jnp.where(pos >= s, pltpu.roll(x, s, 0), 0.0)
        s *= 2
    return x


def _chunk_rev_cumsum(x, pos):
    n = x.shape[0]
    s = 1
    while s < CHUNK:
        x = x + jnp.where(pos + s < CHUNK, pltpu.roll(x, n - s, 0), 0.0)
        s *= 2
    return x


def _lower_bound(low_ref):
    l0 = low_ref[0:1, :]
    l1 = low_ref[1:2, :]
    m = jnp.maximum(l0, l1)
    e0 = jnp.exp(l0 - m)
    e1 = jnp.exp(l1 - m)
    return e0 / (e0 + e1), e1 / (e0 + e1)


def _hg_gates(qr, fr, lb, pos, tb):
    sq = _sigmoid(qr)
    q = qr * sq * Q_SCALE
    sg = _sigmoid(fr)
    f = lb + (1.0 - lb) * sg
    k = 1.0 - f
    b = _chunk_cumsum(jnp.log(f), pos)
    b3 = b.reshape(tb // CHUNK, CHUNK, HEAD_DIM)
    anc = b3[:, CHUNK // 2 - 1:CHUNK // 2, :]
    last = b3[:, CHUNK - 1:CHUNK, :]
    d3 = b3 - anc
    e_qa3 = jnp.exp(d3)
    e_ka3 = jnp.exp(-d3)
    e_b3 = e_qa3 * jnp.exp(anc)
    e_ko3 = e_ka3 * jnp.exp(last - anc)
    dec = jnp.exp(last)
    flat = lambda a: a.reshape(tb, HEAD_DIM)
    return sq, q, sg, f, k, flat(e_qa3), flat(e_ka3), flat(e_b3), flat(e_ko3), dec


def _intra_mask(sb):
    r = lax.broadcasted_iota(jnp.int32, (sb, sb), 0)
    c = lax.broadcasted_iota(jnp.int32, (sb, sb), 1)
    return ((r // CHUNK) == (c // CHUNK)) & (c <= r)


def _hg_fwd(hg, low, gn):
    T = hg.shape[0]
    tb = min(1024, T)
    sb = min(256, tb)
    nb = T // tb
    nc = tb // CHUNK
    wid = HEADS_PER_STEP * HEAD_DIM

    def body(q_ref, f_ref, i_ref, g_ref, low_ref, gn_ref, o_ref, og_ref, ogt_ref, st_ref, s_scr):
        t = pl.program_id(1)

        @pl.when(t == 0)
        def _():
            s_scr[...] = jnp.zeros_like(s_scr)

        pos = _chunk_pos((tb, HEAD_DIM))
        mask = _intra_mask(sb)
        lanes = [slice(hh * HEAD_DIM, (hh + 1) * HEAD_DIM) for hh in range(HEADS_PER_STEP)]
        qi, ko, vb, dec, st = [], [], [], [], []
        for hh, ln in enumerate(lanes):
            lb, _ = _lower_bound(low_ref.at[:, ln])
            _, q, _, _, k, e_qa, e_ka, e_b, e_ko, dec_h = _hg_gates(q_ref[:, ln], f_ref[:, ln], lb, pos, tb)
            qh = (q * e_qa).astype(BF16)
            kh = (k * e_ka).astype(BF16)
            qi.append((q * e_b).astype(BF16))
            ko.append((k * e_ko).astype(BF16))
            vb.append(i_ref[:, ln].astype(BF16))
            dec.append(dec_h)
            st.append(s_scr[hh])
            for s in range(tb // sb):
                sl = slice(s * sb, (s + 1) * sb)
                p = jnp.where(mask, _mm_nt(qh[sl], kh[sl]), 0.0)
                o_ref[sl, ln] = _mm(p, vb[hh][sl])
        for c in range(nc):
            sl = slice(c * CHUNK, (c + 1) * CHUNK)
            for hh, ln in enumerate(lanes):
                st_ref[hh, c] = st[hh]
                o_ref[sl, ln] = o_ref[sl, ln] + _mm_nt(qi[hh][sl], st[hh])
                st[hh] = dec[hh][c] * st[hh] + _mm_tn(vb[hh][sl], ko[hh][sl])
        for hh, ln in enumerate(lanes):
            s_scr[hh] = st[hh]
            o = o_ref[:, ln]
            r = lax.rsqrt(jnp.mean(o * o, axis=-1, keepdims=True) + EPS)
            gr = g_ref[:, ln]
            og = (o * r * gn_ref[...]) * (gr * _sigmoid(gr))
            og_ref[:, ln] = og.astype(BF16)
            ogt_ref[ln, :] = og.T.astype(BF16)

    col = lambda p: pl.BlockSpec((tb, wid), lambda h, t: (t, p * HEAD_GROUPS + h))
    hcol = pl.BlockSpec((tb, wid), lambda h, t: (t, h))
    return pl.pallas_call(
        body, name="hg_fwd", grid=(HEAD_GROUPS, nb),
        in_specs=[col(0), col(1), col(2), col(3), pl.BlockSpec((2, wid), lambda h, t: (0, h)),
                  pl.BlockSpec((1, HEAD_DIM), lambda h, t: (0, 0))],
        out_specs=[hcol, hcol, pl.BlockSpec((wid, tb), lambda h, t: (h, t)),
                   pl.BlockSpec((HEADS_PER_STEP, nc, HEAD_DIM, HEAD_DIM), lambda h, t: (h, t, 0, 0))],
        out_shape=[jax.ShapeDtypeStruct((T, HG_WIDTH), F32), jax.ShapeDtypeStruct((T, HG_WIDTH), BF16),
                   jax.ShapeDtypeStruct((HG_WIDTH, T), BF16),
                   jax.ShapeDtypeStruct((N_HEADS, T // CHUNK, HEAD_DIM, HEAD_DIM), F32)],
        scratch_shapes=[pltpu.VMEM((HEADS_PER_STEP, HEAD_DIM, HEAD_DIM), F32)],
        compiler_params=_params(("parallel", "arbitrary")),
    )(hg, hg, hg, hg, low, gn)


def _merge_fwd(og, cvo, gt, x, wa, wb, wo):
    T = x.shape[0]
    tm = min(1024, T)

    def body(og_ref, cvo_ref, gt_ref, x_ref, wa_ref, wb_ref, wo_ref, x1_ref, mgt_ref):
        ya = jnp.dot(og_ref[...], _shard_cols(wa_ref), preferred_element_type=F32)
        yb = jnp.dot(cvo_ref[...], _shard_cols(wb_ref), preferred_element_type=F32)
        m = (_sigmoid(gt_ref[:, :D_MODEL].astype(F32)) * ya
             + _sigmoid(gt_ref[:, D_MODEL:].astype(F32)) * yb)
        mgt_ref[...] = m.T.astype(BF16)
        x1_ref[...] = x_ref[...] + jnp.dot(m.astype(BF16), wo_ref[...], preferred_element_type=F32)

    row = lambda n: pl.BlockSpec((tm, n), lambda i: (i, 0))
    return pl.pallas_call(
        body, name="merge_fwd", grid=(T // tm,),
        in_specs=[row(HG_WIDTH), row(CONV_WIDTH), row(2 * D_MODEL), row(D_MODEL),
                  _resident(wa.shape), _resident(wb.shape), _resident(wo.shape)],
        out_specs=[row(D_MODEL), _col(tm, D_MODEL)],
        out_shape=[jax.ShapeDtypeStruct((T, D_MODEL), F32), jax.ShapeDtypeStruct((D_MODEL, T), BF16)],
        compiler_params=_params(("parallel",)),
    )(og, cvo, gt, x, wa, wb, wo)


def _ffn_fwd_loss(x1, g, wg, wu, wd, target, g_fin):
    T = x1.shape[0]
    tm = min(512, T)

    def body(x_ref, g_ref, wg_ref, wu_ref, wd_ref, t_ref, gf_ref,
             ht_ref, gate_ref, up_ref, act_ref, loss_ref, dgf_ref, dx2_ref, dx2t_ref):
        @pl.when(pl.program_id(0) == 0)
        def _():
            loss_ref[...] = jnp.zeros_like(loss_ref)
            dgf_ref[...] = jnp.zeros_like(dgf_ref)

        xv = x_ref[...]
        r = lax.rsqrt(jnp.mean(xv * xv, axis=-1, keepdims=True) + EPS)
        hf = xv * r * g_ref[...]
        h = hf.astype(BF16)
        ht_ref[...] = hf.T.astype(BF16)
        gate = _mm_nt(h, wg_ref[...])
        up = _mm_nt(h, wu_ref[...])
        gate_ref[...] = gate.astype(STASH)
        up_ref[...] = up.astype(STASH)
        act = (gate * _sigmoid(gate) * up).astype(BF16)
        act_ref[...] = act
        x2 = xv + jnp.dot(act, wd_ref[...], preferred_element_type=F32)

        gv = gf_ref[...]
        r2 = lax.rsqrt(jnp.mean(x2 * x2, axis=-1, keepdims=True) + EPS)
        xh = x2 * r2
        err = xh * gv - t_ref[...]
        loss_ref[...] += 0.5 * jnp.sum(jnp.mean(err * err, axis=-1, keepdims=True), axis=0, keepdims=True)
        dy = err * (1.0 / D_MODEL)
        dgf_ref[...] += jnp.sum(dy * xh, axis=0, keepdims=True)
        w = dy * gv
        dx2 = r2 * (w - xh * jnp.mean(w * xh, axis=-1, keepdims=True))
        dx2_ref[...] = dx2
        dx2t_ref[...] = dx2.T.astype(BF16)

    row = lambda n: pl.BlockSpec((tm, n), lambda i: (i, 0))
    return pl.pallas_call(
        body, name="ffn_fwd_loss", grid=(T // tm,),
        in_specs=[row(D_MODEL), _full((1, D_MODEL)), _resident(wg.shape), _resident(wu.shape), _resident(wd.shape),
                  row(D_MODEL), _full((1, D_MODEL))],
        out_specs=[_col(tm, D_MODEL), row(D_FF), row(D_FF), row(D_FF), _full((1, 128)), _full((1, D_MODEL)),
                   row(D_MODEL), _col(tm, D_MODEL)],
        out_shape=[jax.ShapeDtypeStruct((D_MODEL, T), BF16), jax.ShapeDtypeStruct((T, D_FF), STASH),
                   jax.ShapeDtypeStruct((T, D_FF), STASH), jax.ShapeDtypeStruct((T, D_FF), BF16),
                   jax.ShapeDtypeStruct((1, 128), F32), jax.ShapeDtypeStruct((1, D_MODEL), F32),
                   jax.ShapeDtypeStruct((T, D_MODEL), F32), jax.ShapeDtypeStruct((D_MODEL, T), BF16)],
        compiler_params=_params(("arbitrary",), vmem=VMEM_LIMIT_LARGE_V7X),
    )(x1, g, wg, wu, wd, target, g_fin)


def _ffn_bwd(dx2, x1, gate, up, g, wg, wu, wd):
    T = x1.shape[0]
    tm = min(512, T)

    def body(dx2_ref, x_ref, gate_ref, up_ref, g_ref, wg_ref, wu_ref, wd_ref, dgate_ref, dup_ref, dx1_ref, dgn_ref):
        @pl.when(pl.program_id(0) == 0)
        def _():
            dgn_ref[...] = jnp.zeros_like(dgn_ref)

        dx2 = dx2_ref[...]
        dact = _mm_nt(dx2, wd_ref[...])
        gate = gate_ref[...].astype(F32)
        s = _sigmoid(gate)
        dgate = (dact * up_ref[...].astype(F32) * (s * (1.0 + gate * (1.0 - s)))).astype(BF16)
        dup = (dact * (gate * s)).astype(BF16)
        dgate_ref[...] = dgate
        dup_ref[...] = dup
        dh = _mm(dgate, wg_ref[...]) + _mm(dup, wu_ref[...])
        xv = x_ref[...]
        r = lax.rsqrt(jnp.mean(xv * xv, axis=-1, keepdims=True) + EPS)
        xh = xv * r
        dgn_ref[...] += jnp.sum(dh * xh, axis=0, keepdims=True)
        w = dh * g_ref[...]
        dx1_ref[...] = dx2 + r * (w - xh * jnp.mean(w * xh, axis=-1, keepdims=True))

    row = lambda n: pl.BlockSpec((tm, n), lambda i: (i, 0))
    return pl.pallas_call(
        body, name="ffn_bwd", grid=(T // tm,),
        in_specs=[row(D_MODEL), row(D_MODEL), row(D_FF), row(D_FF), _full((1, D_MODEL)),
                  _resident(wg.shape), _resident(wu.shape), _resident(wd.shape)],
        out_specs=[row(D_FF), row(D_FF), row(D_MODEL), _full((1, D_MODEL))],
        out_shape=[jax.ShapeDtypeStruct((T, D_FF), BF16), jax.ShapeDtypeStruct((T, D_FF), BF16),
                   jax.ShapeDtypeStruct((T, D_MODEL), F32), jax.ShapeDtypeStruct((1, D_MODEL), F32)],
        compiler_params=_params(("arbitrary",), vmem=VMEM_LIMIT_LARGE_V7X),
    )(dx2, x1, gate, up, g, wg, wu, wd)


def _merge_bwd(dx1, og, cvo, gt, cv, wa, wb, wo, conv_w):
    T = dx1.shape[0]
    tm = min(512, T)
    nt = T // tm

    def body(dx_ref, og_ref, cvo_ref, gt_ref, cv_ref, halo_ref, wa_ref, wb_ref, wo_ref, cw_ref,
             dgt_ref, dya_ref, dyb_ref, dog_ref, dcv_ref, dcw_ref, prev_u, next_dy):
        step = pl.program_id(0)

        @pl.when(step == 0)
        def _():
            next_dy[...] = jnp.zeros_like(next_dy)
            dcw_ref[...] = jnp.zeros_like(dcw_ref)

        dm = _mm_nt(dx_ref[...], wo_ref[...])
        wa = _shard_cols(wa_ref)
        wb = _shard_cols(wb_ref)
        ya = jnp.dot(og_ref[...], wa, preferred_element_type=F32)
        yb = jnp.dot(cvo_ref[...], wb, preferred_element_type=F32)
        sa = _sigmoid(gt_ref[:, :D_MODEL].astype(F32))
        sb = _sigmoid(gt_ref[:, D_MODEL:].astype(F32))
        dgt_ref[:, :D_MODEL] = (dm * ya * (sa * (1.0 - sa))).astype(BF16)
        dgt_ref[:, D_MODEL:] = (dm * yb * (sb * (1.0 - sb))).astype(BF16)
        dya = (dm * sa).astype(BF16)
        dyb = (dm * sb).astype(BF16)
        dya_ref[...] = dya
        dyb_ref[...] = dyb
        dog_ref[...] = _mm_nt(dya, wa)
        dcvo = _mm_nt(dyb, wb)

        cvt = cv_ref[...].astype(F32)
        c, bg, xb = cvt[:, :CONV_WIDTH], cvt[:, CONV_WIDTH:2 * CONV_WIDTH], cvt[:, 2 * CONV_WIDTH:]
        halo = halo_ref[...].astype(F32)
        first_tile = step == nt - 1
        prev_u[...] = jnp.where(first_tile, 0.0, halo[:, :CONV_WIDTH] * halo[:, 2 * CONV_WIDTH:])
        u = c * xb
        row = lax.broadcasted_iota(jnp.int32, u.shape, 0)
        p1 = prev_u[HALO - 1:HALO, :]
        p2 = prev_u[HALO - 2:HALO - 1, :]
        u1 = jnp.where(row >= 1, pltpu.roll(u, 1, 0), p1)
        u2 = jnp.where(row >= 2, pltpu.roll(u, 2, 0), jnp.where(row == 1, p1, p2))
        w0, w1, w2 = cw_ref[0:1, :], cw_ref[1:2, :], cw_ref[2:3, :]
        y = w0 * u2 + w1 * u1 + w2 * u
        dcv_ref[:, CONV_WIDTH:2 * CONV_WIDTH] = (dcvo * y).astype(BF16)
        dy = dcvo * bg
        dcw_ref[0:1, :] += jnp.sum(dy * u2, axis=0, keepdims=True)
        dcw_ref[1:2, :] += jnp.sum(dy * u1, axis=0, keepdims=True)
        dcw_ref[2:3, :] += jnp.sum(dy * u, axis=0, keepdims=True)
        n1 = next_dy[0:1, :]
        n2 = next_dy[1:2, :]
        dy1 = jnp.where(row < tm - 1, pltpu.roll(dy, tm - 1, 0), n1)
        dy2 = jnp.where(row < tm - 2, pltpu.roll(dy, tm - 2, 0), jnp.where(row == tm - 2, n1, n2))
        du = w2 * dy + w1 * dy1 + w0 * dy2
        dcv_ref[:, :CONV_WIDTH] = (du * xb).astype(BF16)
        dcv_ref[:, 2 * CONV_WIDTH:] = (du * c).astype(BF16)
        next_dy[...] = dy[:HALO, :]

    rt = lambda i: nt - 1 - i
    row = lambda n: pl.BlockSpec((tm, n), lambda i: (rt(i), 0))
    halo = pl.BlockSpec((HALO, N_CV), lambda i: (jnp.maximum(rt(i) * (tm // HALO) - 1, 0), 0))
    return pl.pallas_call(
        body, name="merge_bwd", grid=(nt,),
        in_specs=[row(D_MODEL), row(HG_WIDTH), row(CONV_WIDTH), row(2 * D_MODEL), row(N_CV), halo,
                  _resident(wa.shape), _resident(wb.shape), _resident(wo.shape), _full((CONV_K, CONV_WIDTH))],
        out_specs=[row(2 * D_MODEL), row(D_MODEL), row(D_MODEL), row(HG_WIDTH), row(N_CV),
                   _full((CONV_K, CONV_WIDTH))],
        out_shape=[jax.ShapeDtypeStruct((T, 2 * D_MODEL), BF16), jax.ShapeDtypeStruct((T, D_MODEL), BF16),
                   jax.ShapeDtypeStruct((T, D_MODEL), BF16), jax.ShapeDtypeStruct((T, HG_WIDTH), F32),
                   jax.ShapeDtypeStruct((T, N_CV), BF16), jax.ShapeDtypeStruct((CONV_K, CONV_WIDTH), F32)],
        scratch_shapes=[pltpu.VMEM((HALO, CONV_WIDTH), F32), pltpu.VMEM((HALO, CONV_WIDTH), F32)],
        compiler_params=_params(("arbitrary",)),
    )(dx1, og, cvo, gt, cv, cv, wa, wb, wo, conv_w)


def _drop_operands(body, first, count):
    def wrapped(*refs):
        return body(*refs[:first], *refs[first + count:])
    return wrapped


def _hg_bwd(dog, hg, o, st, low, gn, after=()):
    T = hg.shape[0]
    tb = min(512, T)
    sb = min(256, tb)
    nb = T // tb
    nc = tb // CHUNK
    wid = HEADS_PER_STEP * HEAD_DIM

    def body(q_ref, f_ref, i_ref, g_ref, low_ref, gn_ref, o_ref, dog_ref, st_ref,
             dhg_ref, dlow_ref, dgn_ref,
             ds_scr, dqi_scr, dko_scr, dv_scr, dd_scr, dqh_scr, dkh_scr):
        h = pl.program_id(0)
        t = pl.program_id(1)
        dq_ref, df_ref, di_ref, dg_ref = (dhg_ref.at[:, p * HG_WIDTH:(p + 1) * HG_WIDTH] for p in range(4))

        @pl.when(t == 0)
        def _():
            ds_scr[...] = jnp.zeros_like(ds_scr)
            dlow_ref[...] = jnp.zeros_like(dlow_ref)

        @pl.when((t == 0) & (h == 0))
        def _():
            dgn_ref[...] = jnp.zeros_like(dgn_ref)

        pos = _chunk_pos((tb, HEAD_DIM))
        mask = _intra_mask(sb)
        gnv = gn_ref[...]
        lanes = [slice(hh * HEAD_DIM, (hh + 1) * HEAD_DIM) for hh in range(HEADS_PER_STEP)]
        heads = []
        for hh, ln in enumerate(lanes):
            lb, lb1 = _lower_bound(low_ref.at[:, ln])
            qr = q_ref[:, ln]
            sq, q, sg, f, k, e_qa, e_ka, e_b, e_ko, dec = _hg_gates(qr, f_ref[:, ln], lb, pos, tb)

            gr = g_ref[:, ln]
            o = o_ref[:, ln]
            dog_v = dog_ref[:, ln]
            sgr = _sigmoid(gr)
            r = lax.rsqrt(jnp.mean(o * o, axis=-1, keepdims=True) + EPS)
            oh = o * r
            dg_ref[:, ln] = (dog_v * (oh * gnv) * (sgr * (1.0 + gr * (1.0 - sgr)))).astype(BF16)
            don = dog_v * (gr * sgr)
            dgn_ref[...] += jnp.sum(don * oh, axis=0, keepdims=True)
            w = don * gnv
            do = (r * (w - oh * jnp.mean(w * oh, axis=-1, keepdims=True))).astype(BF16)

            qh = (q * e_qa).astype(BF16)
            kh = (k * e_ka).astype(BF16)
            qi = (q * e_b).astype(BF16)
            ko = (k * e_ko).astype(BF16)
            vb = i_ref[:, ln].astype(BF16)

            for s in range(tb // sb):
                sl = slice(s * sb, (s + 1) * sb)
                p = jnp.where(mask, _mm_nt(qh[sl], kh[sl]), 0.0).astype(BF16)
                dp = jnp.where(mask, _mm_nt(do[sl], vb[sl]), 0.0).astype(BF16)
                dv_scr[sl, ln] = _mm_tn(p, do[sl])
                dqh_scr[sl, ln] = _mm(dp, kh[sl])
                dkh_scr[sl, ln] = _mm_tn(dp, qh[sl])
            heads.append(dict(lb=lb, lb1=lb1, qr=qr, sq=sq, q=q, sg=sg, f=f, k=k, e_qa=e_qa, e_ka=e_ka, e_b=e_b,
                              e_ko=e_ko, dec=dec, do=do, qi=qi, ko=ko, vb=vb, ds=ds_scr[hh]))

        for c in reversed(range(nc)):
            sl = slice(c * CHUNK, (c + 1) * CHUNK)
            for hh, ln in enumerate(lanes):
                hd = heads[hh]
                ds = hd["ds"]
                st_c = st_ref[hh, c]
                dqi_scr[sl, ln] = _mm(hd["do"][sl], st_c)
                dko_scr[sl, ln] = _mm(hd["vb"][sl], ds)
                dv_scr[sl, ln] = dv_scr[sl, ln] + _mm_nt(hd["ko"][sl], ds)
                dec_c = hd["dec"][c]
                dd_scr[sl, ln] = jnp.broadcast_to(dec_c * jnp.sum(ds * st_c, axis=0, keepdims=True),
                                                  (CHUNK, HEAD_DIM))
                hd["ds"] = dec_c * ds + _mm_tn(hd["do"][sl], hd["qi"][sl])

        for hh, ln in enumerate(lanes):
            hd = heads[hh]
            ds_scr[hh] = hd["ds"]
            q, k, lb = hd["q"], hd["k"], hd["lb"]
            dko_e = dko_scr[:, ln] * hd["e_ko"]
            dq = dqh_scr[:, ln] * hd["e_qa"] + dqi_scr[:, ln] * hd["e_b"]
            dk = dkh_scr[:, ln] * hd["e_ka"] + dko_e
            kd3 = (k * dko_e).reshape(nc, CHUNK, HEAD_DIM)
            last = jnp.broadcast_to(jnp.sum(kd3, axis=1, keepdims=True), kd3.shape).reshape(tb, HEAD_DIM)
            db = q * dq - k * dk + jnp.where(pos == CHUNK - 1, dd_scr[:, ln] + last, 0.0)
            dlg = _chunk_rev_cumsum(db, pos)
            dfv = dlg / hd["f"] - dk
            s_low = jnp.sum(dfv * (1.0 - hd["sg"]), axis=0, keepdims=True)
            dlow_ref[0:1, ln] += s_low * lb * (1.0 - lb)
            dlow_ref[1:2, ln] += -s_low * lb * hd["lb1"]
            df_ref[:, ln] = (dfv * (1.0 - lb) * hd["sg"] * (1.0 - hd["sg"])).astype(BF16)
            dq_ref[:, ln] = (dq * Q_SCALE * (hd["sq"] * (1.0 + hd["qr"] * (1.0 - hd["sq"])))).astype(BF16)
            di_ref[:, ln] = dv_scr[:, ln].astype(BF16)

    rt = lambda t: nb - 1 - t
    col = lambda p: pl.BlockSpec((tb, wid), lambda h, t: (rt(t), p * HEAD_GROUPS + h))
    hcol = pl.BlockSpec((tb, wid), lambda h, t: (rt(t), h))
    assert HEAD_GROUPS == 1
    tile = pltpu.VMEM((tb, wid), F32)
    return pl.pallas_call(
        _drop_operands(body, 9, len(after)), name="hg_bwd", grid=(HEAD_GROUPS, nb),
        in_specs=[col(0), col(1), col(2), col(3), pl.BlockSpec((2, wid), lambda h, t: (0, h)),
                  pl.BlockSpec((1, HEAD_DIM), lambda h, t: (0, 0)), hcol, hcol,
                  pl.BlockSpec((HEADS_PER_STEP, nc, HEAD_DIM, HEAD_DIM), lambda h, t: (h, rt(t), 0, 0))]
                 + [HBM_SPEC] * len(after),
        out_specs=[pl.BlockSpec((tb, N_HG), lambda h, t: (rt(t), 0)), pl.BlockSpec((2, wid), lambda h, t: (0, h)),
                   pl.BlockSpec((1, HEAD_DIM), lambda h, t: (0, 0))],
        out_shape=[jax.ShapeDtypeStruct((T, N_HG), BF16), jax.ShapeDtypeStruct((2, HG_WIDTH), F32),
                   jax.ShapeDtypeStruct((1, HEAD_DIM), F32)],
        scratch_shapes=[pltpu.VMEM((HEADS_PER_STEP, HEAD_DIM, HEAD_DIM), F32), tile, tile, tile, tile, tile, tile],
        compiler_params=_params(("arbitrary", "arbitrary")),
    )(hg, hg, hg, hg, low, gn, o, dog, st, *after)


def _in_bwd(dparts, w_in, x, dx1, g, after=()):
    T = x.shape[0]
    tm = min(512, T)
    widths = [p.shape[1] for p in dparts]
    offs = [sum(widths[:i]) for i in range(len(widths))]
    n = len(dparts)

    def body(*refs):
        d_refs = refs[:n]
        w_ref, x_ref, dx1_ref, g_ref, dx_ref, dgn_ref = refs[n:]

        @pl.when(pl.program_id(0) == 0)
        def _():
            dgn_ref[...] = jnp.zeros_like(dgn_ref)

        dh = None
        for d_ref, off, wd in zip(d_refs, offs, widths):
            part = _mm(d_ref[...], w_ref[off:off + wd, :])
            dh = part if dh is None else dh + part
        xv = x_ref[...]
        r = lax.rsqrt(jnp.mean(xv * xv, axis=-1, keepdims=True) + EPS)
        xh = xv * r
        dgn_ref[...] += jnp.sum(dh * xh, axis=0, keepdims=True)
        w = dh * g_ref[...]
        dx_ref[...] = dx1_ref[...] + r * (w - xh * jnp.mean(w * xh, axis=-1, keepdims=True))

    row = lambda m: pl.BlockSpec((tm, m), lambda i: (i, 0))
    return pl.pallas_call(
        _drop_operands(body, n + 4, len(after)), name="in_bwd", grid=(T // tm,),
        in_specs=[row(wd) for wd in widths] + [_resident(w_in.shape), row(D_MODEL), row(D_MODEL), _full((1, D_MODEL))]
                 + [HBM_SPEC] * len(after),
        out_specs=[row(D_MODEL), _full((1, D_MODEL))],
        out_shape=[jax.ShapeDtypeStruct((T, D_MODEL), F32), jax.ShapeDtypeStruct((1, D_MODEL), F32)],
        compiler_params=_params(("arbitrary",)),
    )(*dparts, w_in, x, dx1, g, *after)


def _wgrad_ffn(h2t, dgate, dup, dx2t, act, tn=256):
    M, T = h2t.shape
    N = dgate.shape[1]

    def body(h_ref, x_ref, dg_ref, du_ref, act_ref, og_ref, ou_ref, od_ref):
        h = h_ref[...]
        og_ref[...] = _mm(h, dg_ref[...]).T.astype(BF16)
        ou_ref[...] = _mm(h, du_ref[...]).T.astype(BF16)
        od_ref[...] = _mm(x_ref[...], act_ref[...]).T.astype(BF16)

    rhs = pl.BlockSpec((T, tn), lambda j: (0, j))
    out_spec = pl.BlockSpec((tn, M), lambda j: (j, 0))
    out = jax.ShapeDtypeStruct((N, M), BF16)
    return pl.pallas_call(
        body, name="wgrad_ffn", grid=(N // tn,),
        in_specs=[_resident((M, T)), _resident((M, T)), rhs, rhs, rhs], out_specs=[out_spec] * 3,
        out_shape=[out, out, out],
        compiler_params=_params(("parallel",)),
    )(h2t, dx2t, dgate, dup, act)


def _wgrad_out_branches(ogt, dya, cvot, dyb, mgt, dx1, after=()):
    M, T = ogt.shape
    N = dya.shape[1]
    c = N // N_DEV
    per = 2
    tn = per * c

    def body(at_ref, da_ref, bt_ref, db_ref, mt_ref, dx_ref, oa_ref, ob_ref, oo_ref):
        ga = _mm(at_ref[...], da_ref[...])
        gb = _mm(bt_ref[...], db_ref[...])
        for s in range(per):
            oa_ref[s] = ga[:, s * c:(s + 1) * c].astype(BF16)
            ob_ref[s] = gb[:, s * c:(s + 1) * c].astype(BF16)
        oo_ref[...] = _mm(mt_ref[...], dx_ref[...]).astype(BF16)

    rhs = pl.BlockSpec((T, tn), lambda j: (0, j))
    owners = pl.BlockSpec((per, M, c), lambda j: (j, 0, 0))
    out = jax.ShapeDtypeStruct((N_DEV, M, c), BF16)
    return pl.pallas_call(
        _drop_operands(body, 6, len(after)), name="wgrad_out_branches", grid=(N // tn,),
        in_specs=[_resident((M, T)), rhs, _resident((M, T)), rhs, _resident(mgt.shape), rhs] + [HBM_SPEC] * len(after),
        out_specs=[owners, owners, pl.BlockSpec((mgt.shape[0], tn), lambda j: (0, j))],
        out_shape=[out, out, jax.ShapeDtypeStruct((mgt.shape[0], dx1.shape[1]), BF16)],
        compiler_params=_params(("parallel",)),
    )(ogt, dya, cvot, dyb, mgt, dx1, *after)


def _wgrad_in(ht, dparts, after=()):
    M, T = ht.shape
    tn = 512
    nblk = [p.shape[1] // tn for p in dparts]
    start = [sum(nblk[:i]) for i in range(len(nblk))]
    n = len(dparts)

    def body(a_ref, *refs):
        d_refs, o_ref = refs[:n], refs[n]
        j = pl.program_id(0)
        for d_ref, s, nb in zip(d_refs, start, nblk):
            @pl.when((j >= s) & (j < s + nb))
            def _():
                o_ref[...] = _mm(a_ref[...], d_ref[...]).T.astype(BF16)

    def piece_spec(s, nb):
        return pl.BlockSpec((T, tn), lambda j: (0, jnp.clip(j - s, 0, nb - 1)))

    return pl.pallas_call(
        _drop_operands(body, 1 + n, len(after)), name="wgrad_in", grid=(sum(nblk),),
        in_specs=[_resident((M, T))] + [piece_spec(s, nb) for s, nb in zip(start, nblk)] + [HBM_SPEC] * len(after),
        out_specs=pl.BlockSpec((tn, M), lambda j: (j, 0)),
        out_shape=jax.ShapeDtypeStruct((sum(nblk) * tn, M), BF16),
        compiler_params=_params(("parallel",)),
    )(ht, *dparts, *after)


def _adamw_math(w, g, m, v):
    m = ADAM_B1 * m + (1.0 - ADAM_B1) * g
    v = ADAM_B2 * v + (1.0 - ADAM_B2) * (g * g)
    m_hat = m / (1.0 - ADAM_B1 ** ADAM_STEP)
    v_hat = v / (1.0 - ADAM_B2 ** ADAM_STEP)
    delta = -ADAM_LR * (m_hat / (jnp.sqrt(v_hat) + ADAM_EPS) + ADAM_WD * w)
    return delta, m, v


def _adamw_sum(name, ws, parts, ms, vs):
    n = len(ws)
    steps = min(_row_steps(w.shape[0]) for w in ws)
    rows = [w.shape[0] // steps for w in ws]

    def body(*refs):
        w_refs, p_refs, m_refs, v_refs = (refs[k * n:(k + 1) * n] for k in range(4))
        out_refs = refs[4 * n:]
        for i in range(n):
            g = p_refs[i][0].astype(F32)
            for k in range(1, 4):
                g = g + p_refs[i][k].astype(F32)
            delta, m_new, v_new = _adamw_math(w_refs[i][...], g, m_refs[i][...], v_refs[i][...])
            for k, val in enumerate((g, delta, m_new, v_new)):
                out_refs[4 * i + k][...] = val

    blk = [pl.BlockSpec((r, w.shape[1]), lambda s: (s, 0)) for r, w in zip(rows, ws)]
    pblk = [pl.BlockSpec((4, r, w.shape[1]), lambda s: (0, s, 0)) for r, w in zip(rows, ws)]
    out_specs, out_shape = [], []
    for b, w in zip(blk, ws):
        out_specs += [b] * 4
        out_shape += [jax.ShapeDtypeStruct(w.shape, F32)] * 4
    flat = pl.pallas_call(
        body, name=name, grid=(steps,),
        in_specs=blk + pblk + blk + blk, out_specs=out_specs, out_shape=out_shape,
        compiler_params=_params(("parallel",)),
    )(*ws, *parts, *ms, *vs)
    return [flat[4 * i:4 * i + 4] for i in range(n)]


_SMALL_SLOTS = (("norm_mix_g", 0, 1, 1024), ("norm_ffn_g", 1, 1, 1024), ("norm_final_g", 2, 1, 1024),
                ("lower_bounds", 3, 2, 512), ("hg_norm_g", 5, 1, 128), ("loss", 6, 1, 128), ("conv_w", 8, 3, 512))
_SMALL_PARAMS = tuple(s for s in _SMALL_SLOTS if s[0] != "loss")
CONV_SHARD = CONV_WIDTH // N_DEV


def _small_pack(small):
    def body(*refs):
        out = refs[-1]
        out[...] = jnp.zeros_like(out)
        for ref, (_, row, rows, lanes) in zip(refs[:-1], _SMALL_SLOTS):
            out[row:row + rows, 0:lanes] = ref[...]

    vmem = pl.BlockSpec(memory_space=pltpu.VMEM)
    return pl.pallas_call(
        body, name="small_pack", in_specs=[vmem] * len(_SMALL_SLOTS), out_specs=vmem,
        out_shape=jax.ShapeDtypeStruct((SMALL_ROWS, 1024), F32),
    )(*[small[name] for name, _, _, _ in _SMALL_SLOTS])


def _small_update(gathered, dev, w, m, v):
    n = len(_SMALL_PARAMS)

    def body(dev_ref, g_ref, *refs):
        w_refs, m_refs, v_refs = refs[:n], refs[n:2 * n], refs[2 * n:3 * n]
        loss_ref, out_refs, sum_scr = refs[3 * n], refs[3 * n + 1:-1], refs[-1]
        total = g_ref[0]
        for k in range(1, N_DEV):
            total = total + g_ref[k]
        sum_scr[...] = total
        loss_ref[...] = sum_scr[6:7, 0:128]
        for p, (name, row, rows, lanes) in enumerate(_SMALL_PARAMS):
            if name == "conv_w":
                g = sum_scr[row:row + rows, 0:CONV_SHARD]
                for s in range(1, N_DEV):
                    g = jnp.where(dev_ref[0] == s, sum_scr[row:row + rows, s * CONV_SHARD:(s + 1) * CONV_SHARD], g)
            else:
                g = sum_scr[row:row + rows, 0:lanes]
            delta, m_new, v_new = _adamw_math(w_refs[p][...], g, m_refs[p][...], v_refs[p][...])
            out_refs[4 * p][...] = g
            out_refs[4 * p + 1][...] = delta
            out_refs[4 * p + 2][...] = m_new
            out_refs[4 * p + 3][...] = v_new

    vmem = pl.BlockSpec(memory_space=pltpu.VMEM)
    outs = [jax.ShapeDtypeStruct((1, 128), F32)]
    for a in w:
        outs += [jax.ShapeDtypeStruct(a.shape, F32)] * 4
    return pl.pallas_call(
        body, name="small_update",
        in_specs=[pl.BlockSpec(memory_space=pltpu.SMEM)] + [vmem] * (1 + 3 * n), out_specs=[vmem] * len(outs),
        out_shape=outs, scratch_shapes=[pltpu.VMEM((SMALL_ROWS, 1024), F32)],
    )(dev, gathered, *w, *m, *v)


def _row_steps(rows):
    for steps in (4, 2):
        if rows % (16 * steps) == 0:
            return steps
    return 1


def _pair_sum(name, by_owner, got, core, after=()):
    n = len(got)

    def body(core_ref, *refs):
        for a_ref, b_ref, o_ref in zip(refs[:n], refs[n:2 * n], refs[2 * n:]):
            o_ref[...] = (a_ref[...].astype(F32) + b_ref[...].astype(F32)).astype(BF16)

    def blk(g):
        return pl.BlockSpec((None,) + g.shape[1:], lambda k, core_ref: (k, 0, 0))

    def mine(g):
        return pl.BlockSpec((None,) + g.shape[1:], lambda k, core_ref: (2 * k + core_ref[0], 0, 0))

    return pl.pallas_call(
        _drop_operands(body, 1 + 2 * n, len(after)), name=name,
        grid_spec=pltpu.PrefetchScalarGridSpec(
            num_scalar_prefetch=1, grid=(4,),
            in_specs=[mine(g) for g in got] + [blk(g) for g in got] + [HBM_SPEC] * len(after),
            out_specs=[blk(g) for g in got]),
        out_shape=[jax.ShapeDtypeStruct(g.shape, BF16) for g in got],
        compiler_params=_params(("parallel",)),
    )(core, *by_owner, *got, *after)


MESH = pl.DeviceIdType.MESH
HBM_SPEC = pl.BlockSpec(memory_space=pl.ANY)


def _handshake(peers):
    barrier = pltpu.get_barrier_semaphore()
    for peer in peers:
        pl.semaphore_signal(barrier, inc=1, device_id=peer, device_id_type=MESH)
    pl.semaphore_wait(barrier, len(peers))


def _comm_call(body, name, operands, out_shape, scratch, collective_id):
    if collective_id is None:
        return pl.pallas_call(body, name=name, in_specs=[HBM_SPEC] * len(operands), out_specs=[HBM_SPEC] * len(out_shape),
                              out_shape=out_shape, scratch_shapes=scratch)(*operands)
    return pl.kernel(body, out_type=out_shape, mesh=plsc.ScalarSubcoreMesh(axis_name="sequencer", num_cores=1),
                     scratch_types=scratch, name=name,
                     compiler_params=pltpu.CompilerParams(collective_id=collective_id))(*operands)


def _all_gather(name, blocks, collective_id=None, after=()):
    n = len(blocks)
    na = len(after)

    def body(*refs):
        x_refs, out_refs = refs[:n], refs[n + na:2 * n + na]
        send_sems, recv_sems, local_sems = refs[2 * n + na:]
        x, y, c = lax.axis_index("x"), lax.axis_index("y"), lax.axis_index("c")
        me, sibling = (x, y, c), (x, y, 1 - c)
        chips = [(1 - x, y), (x, 1 - y), (1 - x, 1 - y)]
        if collective_id is not None:
            _handshake([sibling] + [(*chip, c) for chip in chips])

        def slot(i, px, py, pc):
            return out_refs[i].at[4 * px + 2 * py + pc]

        def copy(i, k, blk, to, src=None):
            return pltpu.make_async_remote_copy(
                src_ref=slot(i, *blk) if src is None else src, dst_ref=slot(i, *blk),
                send_sem=send_sems.at[7 * i + k], recv_sem=recv_sems.at[7 * i + k], device_id=to, device_id_type=MESH)

        mine = [pltpu.make_async_copy(x_refs[i], slot(i, *me), local_sems.at[i]) for i in range(n)]
        for cp in mine:
            cp.start()
        first = []
        for i in range(n):
            first.append(copy(i, 0, me, sibling, src=x_refs[i]))
            first += [copy(i, 1 + j, me, (*chip, c), src=x_refs[i]) for j, chip in enumerate(chips)]
        for cp in first:
            cp.start()
        passed = []
        for i in range(n):
            for j, chip in enumerate(chips):
                copy(i, 1 + j, (*chip, c), me).wait_recv()
                passed.append(copy(i, 4 + j, (*chip, c), sibling))
                passed[-1].start()
        for i in range(n):
            copy(i, 0, sibling, me).wait_recv()
            for j, chip in enumerate(chips):
                copy(i, 4 + j, (*chip, 1 - c), me).wait_recv()
        for cp in first + passed:
            cp.wait_send()
        for cp in mine:
            cp.wait()

    return _comm_call(
        body, name, list(blocks) + list(after), [jax.ShapeDtypeStruct((N_DEV,) + b.shape, b.dtype) for b in blocks],
        [pltpu.SemaphoreType.DMA((7 * n,)), pltpu.SemaphoreType.DMA((7 * n,)), pltpu.SemaphoreType.DMA((n,))],
        collective_id)


def _sibling_swap(name, by_owner, collective_id=None, after=()):
    n = len(by_owner)
    na = len(after)

    def body(*refs):
        x_refs, out_refs = refs[:n], refs[n + na:2 * n + na]
        send_sems, recv_sems = refs[2 * n + na:]
        x, y, c = lax.axis_index("x"), lax.axis_index("y"), lax.axis_index("c")
        if collective_id is not None:
            _handshake([(x, y, 1 - c)])
        copies = []
        for i in range(n):
            for k in range(4):
                copies.append(pltpu.make_async_remote_copy(
                    src_ref=x_refs[i].at[2 * k + 1 - c], dst_ref=out_refs[i].at[k],
                    send_sem=send_sems.at[4 * i + k], recv_sem=recv_sems.at[4 * i + k],
                    device_id=(x, y, 1 - c), device_id_type=MESH))
        for cp in copies:
            cp.start()
        for cp in copies:
            cp.wait()

    return _comm_call(
        body, name, list(by_owner) + list(after),
        [jax.ShapeDtypeStruct((4,) + b.shape[1:], b.dtype) for b in by_owner],
        [pltpu.SemaphoreType.DMA((4 * n,)), pltpu.SemaphoreType.DMA((4 * n,))], collective_id)


def _chip_exchange(name, sums, collective_id=None, after=()):
    n = len(sums)
    na = len(after)

    def body(*refs):
        x_refs, out_refs = refs[:n], refs[n + na:2 * n + na]
        send_sems, recv_sems, local_sems = refs[2 * n + na:]
        x, y, c = lax.axis_index("x"), lax.axis_index("y"), lax.axis_index("c")
        chips = [(1 - x, y), (x, 1 - y), (1 - x, 1 - y)]
        my_chip = 2 * x + y
        if collective_id is not None:
            _handshake([(cx, cy, c) for cx, cy in chips])
        mine = [pltpu.make_async_copy(x_refs[i].at[my_chip], out_refs[i].at[my_chip], local_sems.at[i])
                for i in range(n)]
        for cp in mine:
            cp.start()
        sends = []
        for i in range(n):
            for j, (cx, cy) in enumerate(chips):
                sends.append(pltpu.make_async_remote_copy(
                    src_ref=x_refs[i].at[2 * cx + cy], dst_ref=out_refs[i].at[my_chip],
                    send_sem=send_sems.at[3 * i + j], recv_sem=recv_sems.at[3 * i + j],
                    device_id=(cx, cy, c), device_id_type=MESH))
        for cp in sends:
            cp.start()
        for i in range(n):
            for j, (cx, cy) in enumerate(chips):
                pltpu.make_async_remote_copy(
                    src_ref=x_refs[i].at[my_chip], dst_ref=out_refs[i].at[2 * cx + cy],
                    send_sem=send_sems.at[3 * i + j], recv_sem=recv_sems.at[3 * i + j],
                    device_id=(cx, cy, c), device_id_type=MESH).wait_recv()
        for cp in sends:
            cp.wait_send()
        for cp in mine:
            cp.wait()

    return _comm_call(
        body, name, list(sums) + list(after), [jax.ShapeDtypeStruct(s.shape, s.dtype) for s in sums],
        [pltpu.SemaphoreType.DMA((3 * n,)), pltpu.SemaphoreType.DMA((3 * n,)), pltpu.SemaphoreType.DMA((n,))],
        collective_id)


def _cast_shards(shards):
    n = len(shards)

    def body(*refs):
        for i in range(n):
            refs[n + i][...] = refs[i][...].astype(BF16)

    vmem = pl.BlockSpec(memory_space=pltpu.VMEM)
    return pl.pallas_call(
        body, name="cast_shards", in_specs=[vmem] * n, out_specs=[vmem] * n,
        out_shape=[jax.ShapeDtypeStruct(s.shape, BF16) for s in shards],
        compiler_params=pltpu.CompilerParams(vmem_limit_bytes=VMEM_LIMIT_V7X),
    )(*shards)


BIG = ("w_in", "w_branch_a", "w_branch_b", "w_out", "w_ffn_gate", "w_ffn_up", "w_ffn_down")


def _local_step(x, target, gains, low, conv_w, wg8, reduce):
    g_mix, g_hg, g_ffn, g_fin = gains
    w_in = wg8["w_in"].reshape(N_IN, D_MODEL)
    wg = wg8["w_ffn_gate"].reshape(D_FF, D_MODEL)
    wu = wg8["w_ffn_up"].reshape(D_FF, D_MODEL)
    wa, wb = wg8["w_branch_a"], wg8["w_branch_b"]
    wo = wg8["w_out"].reshape(D_MODEL, D_MODEL)
    wd = wg8["w_ffn_down"].reshape(D_FF, D_MODEL)

    ht, hg, cv, gt, cvo, cvot = _fwd_in(x, g_mix, w_in, conv_w)
    o, og, ogt, st = _hg_fwd(hg, low, g_hg)
    x1, mgt = _merge_fwd(og, cvo, gt, x, wa, wb, wo)
    h2t, gate, up, act, loss, d_gfin, dx2, dx2t = _ffn_fwd_loss(x1, g_ffn, wg, wu, wd, target, g_fin)

    dgate, dup, dx1, d_gffn = _ffn_bwd(dx2, x1, gate, up, g_ffn, wg, wu, wd)
    d_wg, d_wu, d_wd = _wgrad_ffn(h2t, dgate, dup, dx2t, act)
    by_owner_ffn = lambda a: a.reshape(N_DEV, D_FF // N_DEV, D_MODEL)
    ffn = dict(w_ffn_down=by_owner_ffn(d_wd), w_ffn_gate=by_owner_ffn(d_wg), w_ffn_up=by_owner_ffn(d_wu))
    dgt, dya, dyb, dog, dcv, d_conv = _merge_bwd(dx1, og, cvo, gt, cv, wa, wb, wo, conv_w)
    sums_ffn, got_ffn = reduce.begin(ffn, sum_after=[dya])
    parts_ffn, updated_ffn = reduce.finish(ffn, sums_ffn)
    grad_a, grad_b, grad_o = _wgrad_out_branches(ogt, dya, cvot, dyb, mgt, dx1, after=sums_ffn[:1])
    out = dict(w_out=grad_o.reshape(N_DEV, D_MODEL // N_DEV, D_MODEL), w_branch_a=grad_a, w_branch_b=grad_b)
    dhg, d_low, d_ghg = _hg_bwd(dog, hg, o, st, low, g_hg, after=list(sums_ffn) + [out["w_out"]])
    sums_out, got_out = reduce.begin(out, after=[parts_ffn[0], dhg], sum_after=updated_ffn)
    parts_out, updated_out = reduce.finish(out, sums_out)
    dparts = [dhg, dcv, dgt]
    w_in_grad = dict(w_in=_wgrad_in(ht, dparts, after=sums_out[:1]).reshape(N_DEV, N_IN // N_DEV, D_MODEL))
    sums_in, _ = reduce.begin(w_in_grad, after=parts_out[:1], sum_after=updated_out)
    parts_in, _ = reduce.finish(w_in_grad, sums_in)
    grad_x, d_gmix = _in_bwd(dparts, w_in, x, dx1, g_mix, after=list(parts_out[:1]) + list(sums_in))
    small = dict(norm_mix_g=d_gmix, norm_ffn_g=d_gffn, norm_final_g=d_gfin, lower_bounds=d_low, hg_norm_g=d_ghg,
                 conv_w=d_conv, loss=loss)
    return grad_x, small, parts_in


def _conv_shard_rows(a):
    return jnp.pad(a, ((0, 5), (0, 64)))


def kernel(x, norm_mix_g, w_in, lower_bounds, hg_norm_g, conv_w, w_branch_a, w_branch_b, w_out, norm_ffn_g, w_ffn_gate, w_ffn_up, w_ffn_down, norm_final_g, loss_target, m_norm_mix_g, m_w_in, m_lower_bounds, m_hg_norm_g, m_conv_w, m_w_branch_a, m_w_branch_b, m_w_out, m_norm_ffn_g, m_w_ffn_gate, m_w_ffn_up, m_w_ffn_down, m_norm_final_g, v_norm_mix_g, v_w_in, v_lower_bounds, v_hg_norm_g, v_conv_w, v_w_branch_a, v_w_branch_b, v_w_out, v_norm_ffn_g, v_w_ffn_gate, v_w_ffn_up, v_w_ffn_down, v_norm_final_g):
    cx, cy, cc = lax.axis_index("x"), lax.axis_index("y"), lax.axis_index("c")
    my_dev = 4 * cx + 2 * cy + cc

    def tr(a):
        return a[0].T

    big = dict(w_in=tr(w_in), w_branch_a=w_branch_a[0], w_branch_b=w_branch_b[0], w_out=w_out[0],
               w_ffn_gate=tr(w_ffn_gate), w_ffn_up=tr(w_ffn_up), w_ffn_down=w_ffn_down[0])
    big_m = dict(w_in=tr(m_w_in), w_branch_a=m_w_branch_a[0], w_branch_b=m_w_branch_b[0], w_out=m_w_out[0],
                 w_ffn_gate=tr(m_w_ffn_gate), w_ffn_up=tr(m_w_ffn_up), w_ffn_down=m_w_ffn_down[0])
    big_v = dict(w_in=tr(v_w_in), w_branch_a=v_w_branch_a[0], w_branch_b=v_w_branch_b[0], w_out=v_w_out[0],
                 w_ffn_gate=tr(v_w_ffn_gate), w_ffn_up=tr(v_w_ffn_up), w_ffn_down=v_w_ffn_down[0])
    transposed = ("w_in", "w_ffn_gate", "w_ffn_up")

    shards = dict(zip(BIG, _cast_shards([big[n] for n in BIG])))
    first = _all_gather("gather_w_in", [shards["w_in"], _conv_shard_rows(conv_w[0])])
    ids = iter(range(1, 16))
    mid = _all_gather("gather_mid", [shards[n] for n in BIG[1:4]], collective_id=next(ids), after=first[1:])
    ffn = _all_gather("gather_ffn", [shards[n] for n in BIG[4:]], collective_id=next(ids), after=first[1:])
    wg8 = dict(zip(BIG, [first[0]] + list(mid) + list(ffn)))
    conv_full = first[1][:, :3, :64].transpose(1, 0, 2).reshape(3, CONV_WIDTH)

    core = cc.reshape(1).astype(jnp.int32)
    outs = {}

    class Reduce:
        @staticmethod
        def begin(grads, after=(), sum_after=()):
            names = list(grads)
            by_owner = [grads[n] for n in names]
            got = _sibling_swap("sibling_swap_" + names[0], by_owner, collective_id=next(ids), after=after)
            sums = _pair_sum("pair_sum_" + names[0], by_owner, got, core, after=sum_after)
            return sums, got

        @staticmethod
        def finish(grads, chip_sums, after=()):
            names = list(grads)
            parts = _chip_exchange("chip_exchange_" + names[0], chip_sums, collective_id=next(ids), after=after)
            updated = _adamw_sum("adamw_" + names[0], [big[n] for n in names], parts,
                                 [big_m[n] for n in names], [big_v[n] for n in names])
            outs.update(zip(names, updated))
            return parts, [outs[n][1] for n in names]

    gains = (norm_mix_g, hg_norm_g, norm_ffn_g, norm_final_g.reshape(1, D_MODEL))
    grad_x, small, last = _local_step(x[0], loss_target[0], gains, lower_bounds, conv_full, wg8, Reduce)

    small_all = _all_gather("gather_small", [_small_pack(small)], collective_id=next(ids), after=last[:1])

    def small_state(a):
        return [a[0], a[1], a[2].reshape(1, D_MODEL), a[3], a[4], a[5][0]]

    upd = _small_update(
        small_all[0], my_dev.reshape(1).astype(jnp.int32),
        small_state((norm_mix_g, norm_ffn_g, norm_final_g, lower_bounds, hg_norm_g, conv_w)),
        small_state((m_norm_mix_g, m_norm_ffn_g, m_norm_final_g, m_lower_bounds, m_hg_norm_g, m_conv_w)),
        small_state((v_norm_mix_g, v_norm_ffn_g, v_norm_final_g, v_lower_bounds, v_hg_norm_g, v_conv_w)))
    loss = upd[0][0, 0]
    small_shape = dict(norm_final_g=(D_MODEL,), conv_w=(1, 3, CONV_SHARD))
    for p, (name, _, _, _) in enumerate(_SMALL_PARAMS):
        outs[name] = [a.reshape(small_shape.get(name, a.shape)) for a in upd[1 + 4 * p:5 + 4 * p]]

    order = ["norm_mix_g", "w_in", "lower_bounds", "hg_norm_g", "conv_w", "w_branch_a", "w_branch_b", "w_out",
             "norm_ffn_g", "w_ffn_gate", "w_ffn_up", "w_ffn_down", "norm_final_g"]
    result = [loss, grad_x[None]]
    for k in range(4):
        for n in order:
            if n in BIG:
                result.append((outs[n][k].T if n in transposed else outs[n][k])[None])
            else:
                result.append(outs[n][k])
    return tuple(result)
```

```python
import jax
import jax.numpy as jnp
from jax import lax
from jax.experimental import pallas as pl
from jax.experimental.pallas import tpu as pltpu
from jax.experimental.pallas import tpu_sc as plsc

F32 = jnp.float32
BF16 = jnp.bfloat16
STASH = jnp.bfloat16

D_MODEL = 1024
HG_WIDTH = 512
HEAD_DIM = 128
N_HEADS = 4
HEADS_PER_STEP = 4
HEAD_GROUPS = N_HEADS // HEADS_PER_STEP
CONV_WIDTH = 512
CONV_K = 3
D_FF = 2816
CHUNK = 32
EPS = 1e-6
Q_SCALE = HEAD_DIM ** -0.5
N_DEV = 8

ADAM_LR = 0.001
ADAM_B1 = 0.9
ADAM_B2 = 0.999
ADAM_EPS = 1e-08
ADAM_WD = 0.01
ADAM_STEP = 10

VMEM_LIMIT_V7X = 56 * 1024 * 1024
VMEM_LIMIT_LARGE_V7X = 62 * 1024 * 1024

SMALL_ROWS = 16


def _params(sem, vmem=VMEM_LIMIT_V7X):
    return pltpu.CompilerParams(dimension_semantics=sem, vmem_limit_bytes=vmem)


def _mm(a, b):
    return jnp.dot(a.astype(BF16), b.astype(BF16), preferred_element_type=F32)


def _mm_nt(a, b):
    return lax.dot_general(a.astype(BF16), b.astype(BF16), (((1,), (1,)), ((), ())), preferred_element_type=F32)


def _mm_tn(a, b):
    return lax.dot_general(a.astype(BF16), b.astype(BF16), (((0,), (0,)), ((), ())), preferred_element_type=F32)


def _sigmoid(x):
    return 0.5 * jnp.tanh(0.5 * x) + 0.5


def _resident(shape):
    nd = len(shape)
    return pl.BlockSpec(shape, lambda *_: (0,) * nd, pipeline_mode=pl.Buffered(1))


def _full(shape):
    nd = len(shape)
    return pl.BlockSpec(shape, lambda *_: (0,) * nd)


def _shard_cols(w_ref):
    return jnp.concatenate([w_ref[s] for s in range(N_DEV)], axis=1)


N_HG = 4 * HG_WIDTH
N_CV = 3 * CONV_WIDTH
N_GT = 2 * D_MODEL
N_IN = N_HG + N_CV + N_GT


def _col(tm, n):
    return pl.BlockSpec((n, tm), lambda i: (0, i))


HALO = 8


def _fwd_in(x, g, w_in_t, conv_w, low, gn):
    T = x.shape[0]
    tm = min(512, T)
    nc = tm // CHUNK

    def body(x_ref, g_ref, w_ref, cw_ref, low_ref, gn_ref, ht_ref, hg_ref, cv_ref, gt_ref, cvo_ref, cvot_ref,
             o_ref, og_ref, ogt_ref, st_ref, tail_scr, s_scr):
        @pl.when(pl.program_id(0) == 0)
        def _():
            tail_scr[...] = jnp.zeros_like(tail_scr)
            s_scr[...] = jnp.zeros_like(s_scr)

        xv = x_ref[...]
        r = lax.rsqrt(jnp.mean(xv * xv, axis=-1, keepdims=True) + EPS)
        hf = xv * r * g_ref[...]
        h = hf.astype(BF16)
        ht_ref[...] = hf.T.astype(BF16)
        hg_ref[...] = _mm_nt(h, w_ref[:N_HG, :])
        cv = _mm_nt(h, w_ref[N_HG:N_HG + N_CV, :])
        cv_ref[...] = cv.astype(STASH)
        gt_ref[...] = _mm_nt(h, w_ref[N_HG + N_CV:, :]).astype(STASH)

        u = cv[:, :CONV_WIDTH] * cv[:, 2 * CONV_WIDTH:]
        row = lax.broadcasted_iota(jnp.int32, u.shape, 0)
        prev1 = tail_scr[HALO - 1:HALO, :]
        prev2 = tail_scr[HALO - 2:HALO - 1, :]
        u1 = jnp.where(row >= 1, pltpu.roll(u, 1, 0), prev1)
        u2 = jnp.where(row >= 2, pltpu.roll(u, 2, 0), jnp.where(row == 1, prev1, prev2))
        y = cw_ref[0:1, :] * u2 + cw_ref[1:2, :] * u1 + cw_ref[2:3, :] * u
        out = cv[:, CONV_WIDTH:2 * CONV_WIDTH] * y
        cvo_ref[...] = out.astype(BF16)
        cvot_ref[...] = out.T.astype(BF16)
        tail_scr[...] = u[tm - HALO:, :]

        _hg_fwd_tile(hg_ref, low_ref, gn_ref, o_ref, og_ref, ogt_ref, st_ref, s_scr, tm)

    row = lambda n: pl.BlockSpec((tm, n), lambda i: (i, 0))
    return pl.pallas_call(
        body, name="fwd_in", grid=(T // tm,),
        in_specs=[row(D_MODEL), _full((1, D_MODEL)), _resident(w_in_t.shape), _full((CONV_K, CONV_WIDTH)),
                  _full((2, HG_WIDTH)), _full((1, HEAD_DIM))],
        out_specs=[_col(tm, D_MODEL), row(N_HG), row(N_CV), row(N_GT), row(CONV_WIDTH), _col(tm, CONV_WIDTH),
                   row(HG_WIDTH), row(HG_WIDTH), _col(tm, HG_WIDTH),
                   pl.BlockSpec((N_HEADS, nc, HEAD_DIM, HEAD_DIM), lambda i: (0, i, 0, 0))],
        out_shape=[jax.ShapeDtypeStruct((D_MODEL, T), BF16), jax.ShapeDtypeStruct((T, N_HG), F32),
                   jax.ShapeDtypeStruct((T, N_CV), STASH), jax.ShapeDtypeStruct((T, N_GT), STASH),
                   jax.ShapeDtypeStruct((T, CONV_WIDTH), BF16), jax.ShapeDtypeStruct((CONV_WIDTH, T), BF16),
                   jax.ShapeDtypeStruct((T, HG_WIDTH), F32), jax.ShapeDtypeStruct((T, HG_WIDTH), BF16),
                   jax.ShapeDtypeStruct((HG_WIDTH, T), BF16),
                   jax.ShapeDtypeStruct((N_HEADS, T // CHUNK, HEAD_DIM, HEAD_DIM), F32)],
        scratch_shapes=[pltpu.VMEM((HALO, CONV_WIDTH), F32), pltpu.VMEM((N_HEADS, HEAD_DIM, HEAD_DIM), F32)],
        compiler_params=_params(("arbitrary",), vmem=VMEM_LIMIT_LARGE_V7X),
    )(x, g, w_in_t, conv_w, low, gn)


def _chunk_pos(shape):
    return lax.broadcasted_iota(jnp.int32, shape, 0) & (CHUNK - 1)


def _chunk_cumsum(x, pos):
    s = 1
    while s < CHUNK:
        x = x + jnp.where(pos >= s, pltpu.roll(x, s, 0), 0.0)
        s *= 2
    return x


def _chunk_rev_cumsum(x, pos):
    n = x.shape[0]
    s = 1
    while s < CHUNK:
        x = x + jnp.where(pos + s < CHUNK, pltpu.roll(x, n - s, 0), 0.0)
        s *= 2
    return x


def _lower_bound(low_ref):
    l0 = low_ref[0:1, :]
    l1 = low_ref[1:2, :]
    m = jnp.maximum(l0, l1)
    e0 = jnp.exp(l0 - m)
    e1 = jnp.exp(l1 - m)
    return e0 / (e0 + e1), e1 / (e0 + e1)


def _hg_gates(qr, fr, lb, pos, tb):
    sq = _sigmoid(qr)
    q = qr * sq * Q_SCALE
    sg = _sigmoid(fr)
    f = lb + (1.0 - lb) * sg
    k = 1.0 - f
    b = _chunk_cumsum(jnp.log(f), pos)
    b3 = b.reshape(tb // CHUNK, CHUNK, HEAD_DIM)
    anc = b3[:, CHUNK // 2 - 1:CHUNK // 2, :]
    last = b3[:, CHUNK - 1:CHUNK, :]
    d3 = b3 - anc
    e_qa3 = jnp.exp(d3)
    e_ka3 = jnp.exp(-d3)
    e_b3 = e_qa3 * jnp.exp(anc)
    e_ko3 = e_ka3 * jnp.exp(last - anc)
    dec = jnp.exp(last)
    flat = lambda a: a.reshape(tb, HEAD_DIM)
    return sq, q, sg, f, k, flat(e_qa3), flat(e_ka3), flat(e_b3), flat(e_ko3), dec


def _intra_mask(sb):
    r = lax.broadcasted_iota(jnp.int32, (sb, sb), 0)
    c = lax.broadcasted_iota(jnp.int32, (sb, sb), 1)
    return ((r // CHUNK) == (c // CHUNK)) & (c <= r)


def _hg_fwd_tile(hg_ref, low_ref, gn_ref, o_ref, og_ref, ogt_ref, st_ref, s_scr, tb):
    sb = min(256, tb)
    nc = tb // CHUNK
    q_ref, f_ref, i_ref, g_ref = (hg_ref.at[:, p * HG_WIDTH:(p + 1) * HG_WIDTH] for p in range(4))
    pos = _chunk_pos((tb, HEAD_DIM))
    mask = _intra_mask(sb)
    lanes = [slice(hh * HEAD_DIM, (hh + 1) * HEAD_DIM) for hh in range(N_HEADS)]
    qi, ko, vb, dec, st = [], [], [], [], []
    for hh, ln in enumerate(lanes):
        lb, _ = _lower_bound(low_ref.at[:, ln])
        _, q, _, _, k, e_qa, e_ka, e_b, e_ko, dec_h = _hg_gates(q_ref[:, ln], f_ref[:, ln], lb, pos, tb)
        qh = (q * e_qa).astype(BF16)
        kh = (k * e_ka).astype(BF16)
        qi.append((q * e_b).astype(BF16))
        ko.append((k * e_ko).astype(BF16))
        vb.append(i_ref[:, ln].astype(BF16))
        dec.append(dec_h)
        st.append(s_scr[hh])
        for s in range(tb // sb):
            sl = slice(s * sb, (s + 1) * sb)
            p = jnp.where(mask, _mm_nt(qh[sl], kh[sl]), 0.0)
            o_ref[sl, ln] = _mm(p, vb[hh][sl])
    for c in range(nc):
        sl = slice(c * CHUNK, (c + 1) * CHUNK)
        for hh, ln in enumerate(lanes):
            st_ref[hh, c] = st[hh]
            o_ref[sl, ln] = o_ref[sl, ln] + _mm_nt(qi[hh][sl], st[hh])
            st[hh] = dec[hh][c] * st[hh] + _mm_tn(vb[hh][sl], ko[hh][sl])
    for hh, ln in enumerate(lanes):
        s_scr[hh] = st[hh]
        o = o_ref[:, ln]
        r = lax.rsqrt(jnp.mean(o * o, axis=-1, keepdims=True) + EPS)
        gr = g_ref[:, ln]
        og = (o * r * gn_ref[...]) * (gr * _sigmoid(gr))
        og_ref[:, ln] = og.astype(BF16)
        ogt_ref[ln, :] = og.T.astype(BF16)


def _merge_fwd(og, cvo, gt, x, wa, wb, wo):
    T = x.shape[0]
    tm = min(1024, T)

    def body(og_ref, cvo_ref, gt_ref, x_ref, wa_ref, wb_ref, wo_ref, x1_ref, mgt_ref):
        ya = jnp.dot(og_ref[...], _shard_cols(wa_ref), preferred_element_type=F32)
        yb = jnp.dot(cvo_ref[...], _shard_cols(wb_ref), preferred_element_type=F32)
        m = (_sigmoid(gt_ref[:, :D_MODEL].astype(F32)) * ya
             + _sigmoid(gt_ref[:, D_MODEL:].astype(F32)) * yb)
        mgt_ref[...] = m.T.astype(BF16)
        x1_ref[...] = x_ref[...] + jnp.dot(m.astype(BF16), wo_ref[...], preferred_element_type=F32)

    row = lambda n: pl.BlockSpec((tm, n), lambda i: (i, 0))
    return pl.pallas_call(
        body, name="merge_fwd", grid=(T // tm,),
        in_specs=[row(HG_WIDTH), row(CONV_WIDTH), row(2 * D_MODEL), row(D_MODEL),
                  _resident(wa.shape), _resident(wb.shape), _resident(wo.shape)],
        out_specs=[row(D_MODEL), _col(tm, D_MODEL)],
        out_shape=[jax.ShapeDtypeStruct((T, D_MODEL), F32), jax.ShapeDtypeStruct((D_MODEL, T), BF16)],
        compiler_params=_params(("parallel",)),
    )(og, cvo, gt, x, wa, wb, wo)


def _ffn_fwd_loss(x1, g, wg, wu, wd, target, g_fin):
    T = x1.shape[0]
    tm = min(512, T)

    def body(x_ref, g_ref, wg_ref, wu_ref, wd_ref, t_ref, gf_ref,
             ht_ref, gate_ref, up_ref, act_ref, loss_ref, dgf_ref, dx2_ref, dx2t_ref):
        @pl.when(pl.program_id(0) == 0)
        def _():
            loss_ref[...] = jnp.zeros_like(loss_ref)
            dgf_ref[...] = jnp.zeros_like(dgf_ref)

        xv = x_ref[...]
        r = lax.rsqrt(jnp.mean(xv * xv, axis=-1, keepdims=True) + EPS)
        hf = xv * r * g_ref[...]
        h = hf.astype(BF16)
        ht_ref[...] = hf.T.astype(BF16)
        gate = _mm_nt(h, wg_ref[...])
        up = _mm_nt(h, wu_ref[...])
        gate_ref[...] = gate.astype(STASH)
        up_ref[...] = up.astype(STASH)
        act = (gate * _sigmoid(gate) * up).astype(BF16)
        act_ref[...] = act
        x2 = xv + jnp.dot(act, wd_ref[...], preferred_element_type=F32)

        gv = gf_ref[...]
        r2 = lax.rsqrt(jnp.mean(x2 * x2, axis=-1, keepdims=True) + EPS)
        xh = x2 * r2
        err = xh * gv - t_ref[...]
        loss_ref[...] += 0.5 * jnp.sum(jnp.mean(err * err, axis=-1, keepdims=True), axis=0, keepdims=True)
        dy = err * (1.0 / D_MODEL)
        dgf_ref[...] += jnp.sum(dy * xh, axis=0, keepdims=True)
        w = dy * gv
        dx2 = r2 * (w - xh * jnp.mean(w * xh, axis=-1, keepdims=True))
        dx2_ref[...] = dx2
        dx2t_ref[...] = dx2.T.astype(BF16)

    row = lambda n: pl.BlockSpec((tm, n), lambda i: (i, 0))
    return pl.pallas_call(
        body, name="ffn_fwd_loss", grid=(T // tm,),
        in_specs=[row(D_MODEL), _full((1, D_MODEL)), _resident(wg.shape), _resident(wu.shape), _resident(wd.shape),
                  row(D_MODEL), _full((1, D_MODEL))],
        out_specs=[_col(tm, D_MODEL), row(D_FF), row(D_FF), row(D_FF), _full((1, 128)), _full((1, D_MODEL)),
                   row(D_MODEL), _col(tm, D_MODEL)],
        out_shape=[jax.ShapeDtypeStruct((D_MODEL, T), BF16), jax.ShapeDtypeStruct((T, D_FF), STASH),
                   jax.ShapeDtypeStruct((T, D_FF), STASH), jax.ShapeDtypeStruct((T, D_FF), BF16),
                   jax.ShapeDtypeStruct((1, 128), F32), jax.ShapeDtypeStruct((1, D_MODEL), F32),
                   jax.ShapeDtypeStruct((T, D_MODEL), F32), jax.ShapeDtypeStruct((D_MODEL, T), BF16)],
        compiler_params=_params(("arbitrary",), vmem=VMEM_LIMIT_LARGE_V7X),
    )(x1, g, wg, wu, wd, target, g_fin)


def _ffn_bwd(dx2, x1, gate, up, g, wg, wu, wd):
    T = x1.shape[0]
    tm = min(512, T)

    def body(dx2_ref, x_ref, gate_ref, up_ref, g_ref, wg_ref, wu_ref, wd_ref, dgate_ref, dup_ref, dx1_ref, dgn_ref):
        @pl.when(pl.program_id(0) == 0)
        def _():
            dgn_ref[...] = jnp.zeros_like(dgn_ref)

        dx2 = dx2_ref[...]
        dact = _mm_nt(dx2, wd_ref[...])
        gate = gate_ref[...].astype(F32)
        s = _sigmoid(gate)
        dgate = (dact * up_ref[...].astype(F32) * (s * (1.0 + gate * (1.0 - s)))).astype(BF16)
        dup = (dact * (gate * s)).astype(BF16)
        dgate_ref[...] = dgate
        dup_ref[...] = dup
        dh = _mm(dgate, wg_ref[...]) + _mm(dup, wu_ref[...])
        xv = x_ref[...]
        r = lax.rsqrt(jnp.mean(xv * xv, axis=-1, keepdims=True) + EPS)
        xh = xv * r
        dgn_ref[...] += jnp.sum(dh * xh, axis=0, keepdims=True)
        w = dh * g_ref[...]
        dx1_ref[...] = dx2 + r * (w - xh * jnp.mean(w * xh, axis=-1, keepdims=True))

    row = lambda n: pl.BlockSpec((tm, n), lambda i: (i, 0))
    return pl.pallas_call(
        body, name="ffn_bwd", grid=(T // tm,),
        in_specs=[row(D_MODEL), row(D_MODEL), row(D_FF), row(D_FF), _full((1, D_MODEL)),
                  _resident(wg.shape), _resident(wu.shape), _resident(wd.shape)],
        out_specs=[row(D_FF), row(D_FF), row(D_MODEL), _full((1, D_MODEL))],
        out_shape=[jax.ShapeDtypeStruct((T, D_FF), BF16), jax.ShapeDtypeStruct((T, D_FF), BF16),
                   jax.ShapeDtypeStruct((T, D_MODEL), F32), jax.ShapeDtypeStruct((1, D_MODEL), F32)],
        compiler_params=_params(("arbitrary",), vmem=VMEM_LIMIT_LARGE_V7X),
    )(dx2, x1, gate, up, g, wg, wu, wd)


def _merge_bwd(dx1, og, cvo, gt, cv, wa, wb, wo, conv_w):
    T = dx1.shape[0]
    tm = min(512, T)
    nt = T // tm

    def body(dx_ref, og_ref, cvo_ref, gt_ref, cv_ref, halo_ref, wa_ref, wb_ref, wo_ref, cw_ref,
             dgt_ref, dya_ref, dyb_ref, dog_ref, dcv_ref, dcw_ref, prev_u, next_dy):
        step = pl.program_id(0)

        @pl.when(step == 0)
        def _():
            next_dy[...] = jnp.zeros_like(next_dy)
            dcw_ref[...] = jnp.zeros_like(dcw_ref)

        dm = _mm_nt(dx_ref[...], wo_ref[...])
        wa = _shard_cols(wa_ref)
        wb = _shard_cols(wb_ref)
        ya = jnp.dot(og_ref[...], wa, preferred_element_type=F32)
        yb = jnp.dot(cvo_ref[...], wb, preferred_element_type=F32)
        sa = _sigmoid(gt_ref[:, :D_MODEL].astype(F32))
        sb = _sigmoid(gt_ref[:, D_MODEL:].astype(F32))
        dgt_ref[:, :D_MODEL] = (dm * ya * (sa * (1.0 - sa))).astype(BF16)
        dgt_ref[:, D_MODEL:] = (dm * yb * (sb * (1.0 - sb))).astype(BF16)
        dya = (dm * sa).astype(BF16)
        dyb = (dm * sb).astype(BF16)
        dya_ref[...] = dya
        dyb_ref[...] = dyb
        dog_ref[...] = _mm_nt(dya, wa)
        dcvo = _mm_nt(dyb, wb)

        cvt = cv_ref[...].astype(F32)
        c, bg, xb = cvt[:, :CONV_WIDTH], cvt[:, CONV_WIDTH:2 * CONV_WIDTH], cvt[:, 2 * CONV_WIDTH:]
        halo = halo_ref[...].astype(F32)
        first_tile = step == nt - 1
        prev_u[...] = jnp.where(first_tile, 0.0, halo[:, :CONV_WIDTH] * halo[:, 2 * CONV_WIDTH:])
        u = c * xb
        row = lax.broadcasted_iota(jnp.int32, u.shape, 0)
        p1 = prev_u[HALO - 1:HALO, :]
        p2 = prev_u[HALO - 2:HALO - 1, :]
        u1 = jnp.where(row >= 1, pltpu.roll(u, 1, 0), p1)
        u2 = jnp.where(row >= 2, pltpu.roll(u, 2, 0), jnp.where(row == 1, p1, p2))
        w0, w1, w2 = cw_ref[0:1, :], cw_ref[1:2, :], cw_ref[2:3, :]
        y = w0 * u2 + w1 * u1 + w2 * u
        dcv_ref[:, CONV_WIDTH:2 * CONV_WIDTH] = (dcvo * y).astype(BF16)
        dy = dcvo * bg
        dcw_ref[0:1, :] += jnp.sum(dy * u2, axis=0, keepdims=True)
        dcw_ref[1:2, :] += jnp.sum(dy * u1, axis=0, keepdims=True)
        dcw_ref[2:3, :] += jnp.sum(dy * u, axis=0, keepdims=True)
        n1 = next_dy[0:1, :]
        n2 = next_dy[1:2, :]
        dy1 = jnp.where(row < tm - 1, pltpu.roll(dy, tm - 1, 0), n1)
        dy2 = jnp.where(row < tm - 2, pltpu.roll(dy, tm - 2, 0), jnp.where(row == tm - 2, n1, n2))
        du = w2 * dy + w1 * dy1 + w0 * dy2
        dcv_ref[:, :CONV_WIDTH] = (du * xb).astype(BF16)
        dcv_ref[:, 2 * CONV_WIDTH:] = (du * c).astype(BF16)
        next_dy[...] = dy[:HALO, :]

    rt = lambda i: nt - 1 - i
    row = lambda n: pl.BlockSpec((tm, n), lambda i: (rt(i), 0))
    halo = pl.BlockSpec((HALO, N_CV), lambda i: (jnp.maximum(rt(i) * (tm // HALO) - 1, 0), 0))
    return pl.pallas_call(
        body, name="merge_bwd", grid=(nt,),
        in_specs=[row(D_MODEL), row(HG_WIDTH), row(CONV_WIDTH), row(2 * D_MODEL), row(N_CV), halo,
                  _resident(wa.shape), _resident(wb.shape), _resident(wo.shape), _full((CONV_K, CONV_WIDTH))],
        out_specs=[row(2 * D_MODEL), row(D_MODEL), row(D_MODEL), row(HG_WIDTH), row(N_CV),
                   _full((CONV_K, CONV_WIDTH))],
        out_shape=[jax.ShapeDtypeStruct((T, 2 * D_MODEL), BF16), jax.ShapeDtypeStruct((T, D_MODEL), BF16),
                   jax.ShapeDtypeStruct((T, D_MODEL), BF16), jax.ShapeDtypeStruct((T, HG_WIDTH), F32),
                   jax.ShapeDtypeStruct((T, N_CV), BF16), jax.ShapeDtypeStruct((CONV_K, CONV_WIDTH), F32)],
        scratch_shapes=[pltpu.VMEM((HALO, CONV_WIDTH), F32), pltpu.VMEM((HALO, CONV_WIDTH), F32)],
        compiler_params=_params(("arbitrary",)),
    )(dx1, og, cvo, gt, cv, cv, wa, wb, wo, conv_w)


def _drop_operands(body, first, count):
    def wrapped(*refs):
        return body(*refs[:first], *refs[first + count:])
    return wrapped


def _hg_bwd(dog, hg, o, st, low, gn, after=()):
    T = hg.shape[0]
    tb = min(512, T)
    sb = min(256, tb)
    nb = T // tb
    nc = tb // CHUNK
    wid = HEADS_PER_STEP * HEAD_DIM

    def body(q_ref, f_ref, i_ref, g_ref, low_ref, gn_ref, o_ref, dog_ref, st_ref,
             dhg_ref, dlow_ref, dgn_ref,
             ds_scr, dqi_scr, dko_scr, dv_scr, dd_scr, dqh_scr, dkh_scr):
        h = pl.program_id(0)
        t = pl.program_id(1)
        dq_ref, df_ref, di_ref, dg_ref = (dhg_ref.at[:, p * HG_WIDTH:(p + 1) * HG_WIDTH] for p in range(4))

        @pl.when(t == 0)
        def _():
            ds_scr[...] = jnp.zeros_like(ds_scr)
            dlow_ref[...] = jnp.zeros_like(dlow_ref)

        @pl.when((t == 0) & (h == 0))
        def _():
            dgn_ref[...] = jnp.zeros_like(dgn_ref)

        pos = _chunk_pos((tb, HEAD_DIM))
        mask = _intra_mask(sb)
        gnv = gn_ref[...]
        lanes = [slice(hh * HEAD_DIM, (hh + 1) * HEAD_DIM) for hh in range(HEADS_PER_STEP)]
        heads = []
        for hh, ln in enumerate(lanes):
            lb, lb1 = _lower_bound(low_ref.at[:, ln])
            qr = q_ref[:, ln]
            sq, q, sg, f, k, e_qa, e_ka, e_b, e_ko, dec = _hg_gates(qr, f_ref[:, ln], lb, pos, tb)

            gr = g_ref[:, ln]
            o = o_ref[:, ln]
            dog_v = dog_ref[:, ln]
            sgr = _sigmoid(gr)
            r = lax.rsqrt(jnp.mean(o * o, axis=-1, keepdims=True) + EPS)
            oh = o * r
            dg_ref[:, ln] = (dog_v * (oh * gnv) * (sgr * (1.0 + gr * (1.0 - sgr)))).astype(BF16)
            don = dog_v * (gr * sgr)
            dgn_ref[...] += jnp.sum(don * oh, axis=0, keepdims=True)
            w = don * gnv
            do = (r * (w - oh * jnp.mean(w * oh, axis=-1, keepdims=True))).astype(BF16)

            qh = (q * e_qa).astype(BF16)
            kh = (k * e_ka).astype(BF16)
            qi = (q * e_b).astype(BF16)
            ko = (k * e_ko).astype(BF16)
            vb = i_ref[:, ln].astype(BF16)

            for s in range(tb // sb):
                sl = slice(s * sb, (s + 1) * sb)
                p = jnp.where(mask, _mm_nt(qh[sl], kh[sl]), 0.0).astype(BF16)
                dp = jnp.where(mask, _mm_nt(do[sl], vb[sl]), 0.0).astype(BF16)
                dv_scr[sl, ln] = _mm_tn(p, do[sl])
                dqh_scr[sl, ln] = _mm(dp, kh[sl])
                dkh_scr[sl, ln] = _mm_tn(dp, qh[sl])
            heads.append(dict(lb=lb, lb1=lb1, qr=qr, sq=sq, q=q, sg=sg, f=f, k=k, e_qa=e_qa, e_ka=e_ka, e_b=e_b,
                              e_ko=e_ko, dec=dec, do=do, qi=qi, ko=ko, vb=vb, ds=ds_scr[hh]))

        for c in reversed(range(nc)):
            sl = slice(c * CHUNK, (c + 1) * CHUNK)
            for hh, ln in enumerate(lanes):
                hd = heads[hh]
                ds = hd["ds"]
                st_c = st_ref[hh, c]
                dqi_scr[sl, ln] = _mm(hd["do"][sl], st_c)
                dko_scr[sl, ln] = _mm(hd["vb"][sl], ds)
                dv_scr[sl, ln] = dv_scr[sl, ln] + _mm_nt(hd["ko"][sl], ds)
                dec_c = hd["dec"][c]
                dd_scr[sl, ln] = jnp.broadcast_to(dec_c * jnp.sum(ds * st_c, axis=0, keepdims=True),
                                                  (CHUNK, HEAD_DIM))
                hd["ds"] = dec_c * ds + _mm_tn(hd["do"][sl], hd["qi"][sl])

        for hh, ln in enumerate(lanes):
            hd = heads[hh]
            ds_scr[hh] = hd["ds"]
            q, k, lb = hd["q"], hd["k"], hd["lb"]
            dko_e = dko_scr[:, ln] * hd["e_ko"]
            dq = dqh_scr[:, ln] * hd["e_qa"] + dqi_scr[:, ln] * hd["e_b"]
            dk = dkh_scr[:, ln] * hd["e_ka"] + dko_e
            kd3 = (k * dko_e).reshape(nc, CHUNK, HEAD_DIM)
            last = jnp.broadcast_to(jnp.sum(kd3, axis=1, keepdims=True), kd3.shape).reshape(tb, HEAD_DIM)
            db = q * dq - k * dk + jnp.where(pos == CHUNK - 1, dd_scr[:, ln] + last, 0.0)
            dlg = _chunk_rev_cumsum(db, pos)
            dfv = dlg / hd["f"] - dk
            s_low = jnp.sum(dfv * (1.0 - hd["sg"]), axis=0, keepdims=True)
            dlow_ref[0:1, ln] += s_low * lb * (1.0 - lb)
            dlow_ref[1:2, ln] += -s_low * lb * hd["lb1"]
            df_ref[:, ln] = (dfv * (1.0 - lb) * hd["sg"] * (1.0 - hd["sg"])).astype(BF16)
            dq_ref[:, ln] = (dq * Q_SCALE * (hd["sq"] * (1.0 + hd["qr"] * (1.0 - hd["sq"])))).astype(BF16)
            di_ref[:, ln] = dv_scr[:, ln].astype(BF16)

    rt = lambda t: nb - 1 - t
    col = lambda p: pl.BlockSpec((tb, wid), lambda h, t: (rt(t), p * HEAD_GROUPS + h))
    hcol = pl.BlockSpec((tb, wid), lambda h, t: (rt(t), h))
    assert HEAD_GROUPS == 1
    tile = pltpu.VMEM((tb, wid), F32)
    return pl.pallas_call(
        _drop_operands(body, 9, len(after)), name="hg_bwd", grid=(HEAD_GROUPS, nb),
        in_specs=[col(0), col(1), col(2), col(3), pl.BlockSpec((2, wid), lambda h, t: (0, h)),
                  pl.BlockSpec((1, HEAD_DIM), lambda h, t: (0, 0)), hcol, hcol,
                  pl.BlockSpec((HEADS_PER_STEP, nc, HEAD_DIM, HEAD_DIM), lambda h, t: (h, rt(t), 0, 0))]
                 + [HBM_SPEC] * len(after),
        out_specs=[pl.BlockSpec((tb, N_HG), lambda h, t: (rt(t), 0)), pl.BlockSpec((2, wid), lambda h, t: (0, h)),
                   pl.BlockSpec((1, HEAD_DIM), lambda h, t: (0, 0))],
        out_shape=[jax.ShapeDtypeStruct((T, N_HG), BF16), jax.ShapeDtypeStruct((2, HG_WIDTH), F32),
                   jax.ShapeDtypeStruct((1, HEAD_DIM), F32)],
        scratch_shapes=[pltpu.VMEM((HEADS_PER_STEP, HEAD_DIM, HEAD_DIM), F32), tile, tile, tile, tile, tile, tile],
        compiler_params=_params(("arbitrary", "arbitrary")),
    )(hg, hg, hg, hg, low, gn, o, dog, st, *after)


def _in_bwd(dparts, w_in, x, dx1, g, after=()):
    T = x.shape[0]
    tm = min(512, T)
    widths = [p.shape[1] for p in dparts]
    offs = [sum(widths[:i]) for i in range(len(widths))]
    n = len(dparts)

    def body(*refs):
        d_refs = refs[:n]
        w_ref, x_ref, dx1_ref, g_ref, dx_ref, dgn_ref = refs[n:]

        @pl.when(pl.program_id(0) == 0)
        def _():
            dgn_ref[...] = jnp.zeros_like(dgn_ref)

        dh = None
        for d_ref, off, wd in zip(d_refs, offs, widths):
            part = _mm(d_ref[...], w_ref[off:off + wd, :])
            dh = part if dh is None else dh + part
        xv = x_ref[...]
        r = lax.rsqrt(jnp.mean(xv * xv, axis=-1, keepdims=True) + EPS)
        xh = xv * r
        dgn_ref[...] += jnp.sum(dh * xh, axis=0, keepdims=True)
        w = dh * g_ref[...]
        dx_ref[...] = dx1_ref[...] + r * (w - xh * jnp.mean(w * xh, axis=-1, keepdims=True))

    row = lambda m: pl.BlockSpec((tm, m), lambda i: (i, 0))
    return pl.pallas_call(
        _drop_operands(body, n + 4, len(after)), name="in_bwd", grid=(T // tm,),
        in_specs=[row(wd) for wd in widths] + [_resident(w_in.shape), row(D_MODEL), row(D_MODEL), _full((1, D_MODEL))]
                 + [HBM_SPEC] * len(after),
        out_specs=[row(D_MODEL), _full((1, D_MODEL))],
        out_shape=[jax.ShapeDtypeStruct((T, D_MODEL), F32), jax.ShapeDtypeStruct((1, D_MODEL), F32)],
        compiler_params=_params(("arbitrary",)),
    )(*dparts, w_in, x, dx1, g, *after)


def _wgrad_ffn(h2t, dgate, dup, dx2t, act, tn=256):
    M, T = h2t.shape
    N = dgate.shape[1]

    def body(h_ref, x_ref, dg_ref, du_ref, act_ref, og_ref, ou_ref, od_ref):
        h = h_ref[...]
        og_ref[...] = _mm(h, dg_ref[...]).T.astype(BF16)
        ou_ref[...] = _mm(h, du_ref[...]).T.astype(BF16)
        od_ref[...] = _mm(x_ref[...], act_ref[...]).T.astype(BF16)

    rhs = pl.BlockSpec((T, tn), lambda j: (0, j))
    out_spec = pl.BlockSpec((tn, M), lambda j: (j, 0))
    out = jax.ShapeDtypeStruct((N, M), BF16)
    return pl.pallas_call(
        body, name="wgrad_ffn", grid=(N // tn,),
        in_specs=[_resident((M, T)), _resident((M, T)), rhs, rhs, rhs], out_specs=[out_spec] * 3,
        out_shape=[out, out, out],
        compiler_params=_params(("parallel",)),
    )(h2t, dx2t, dgate, dup, act)


def _wgrad_out_branches(ogt, dya, cvot, dyb, mgt, dx1, after=()):
    M, T = ogt.shape
    N = dya.shape[1]
    c = N // N_DEV
    per = 2
    tn = per * c

    def body(at_ref, da_ref, bt_ref, db_ref, mt_ref, dx_ref, oa_ref, ob_ref, oo_ref):
        ga = _mm(at_ref[...], da_ref[...])
        gb = _mm(bt_ref[...], db_ref[...])
        for s in range(per):
            oa_ref[s] = ga[:, s * c:(s + 1) * c].astype(BF16)
            ob_ref[s] = gb[:, s * c:(s + 1) * c].astype(BF16)
        oo_ref[...] = _mm(mt_ref[...], dx_ref[...]).astype(BF16)

    rhs = pl.BlockSpec((T, tn), lambda j: (0, j))
    owners = pl.BlockSpec((per, M, c), lambda j: (j, 0, 0))
    out = jax.ShapeDtypeStruct((N_DEV, M, c), BF16)
    return pl.pallas_call(
        _drop_operands(body, 6, len(after)), name="wgrad_out_branches", grid=(N // tn,),
        in_specs=[_resident((M, T)), rhs, _resident((M, T)), rhs, _resident(mgt.shape), rhs] + [HBM_SPEC] * len(after),
        out_specs=[owners, owners, pl.BlockSpec((mgt.shape[0], tn), lambda j: (0, j))],
        out_shape=[out, out, jax.ShapeDtypeStruct((mgt.shape[0], dx1.shape[1]), BF16)],
        compiler_params=_params(("parallel",)),
    )(ogt, dya, cvot, dyb, mgt, dx1, *after)


def _wgrad_in(ht, dparts, after=()):
    M, T = ht.shape
    tn = 512
    nblk = [p.shape[1] // tn for p in dparts]
    start = [sum(nblk[:i]) for i in range(len(nblk))]
    n = len(dparts)

    def body(a_ref, *refs):
        d_refs, o_ref = refs[:n], refs[n]
        j = pl.program_id(0)
        for d_ref, s, nb in zip(d_refs, start, nblk):
            @pl.when((j >= s) & (j < s + nb))
            def _():
                o_ref[...] = _mm(a_ref[...], d_ref[...]).T.astype(BF16)

    def piece_spec(s, nb):
        return pl.BlockSpec((T, tn), lambda j: (0, jnp.clip(j - s, 0, nb - 1)))

    return pl.pallas_call(
        _drop_operands(body, 1 + n, len(after)), name="wgrad_in", grid=(sum(nblk),),
        in_specs=[_resident((M, T))] + [piece_spec(s, nb) for s, nb in zip(start, nblk)] + [HBM_SPEC] * len(after),
        out_specs=pl.BlockSpec((tn, M), lambda j: (j, 0)),
        out_shape=jax.ShapeDtypeStruct((sum(nblk) * tn, M), BF16),
        compiler_params=_params(("parallel",)),
    )(ht, *dparts, *after)


def _adamw_math(w, g, m, v):
    m = ADAM_B1 * m + (1.0 - ADAM_B1) * g
    v = ADAM_B2 * v + (1.0 - ADAM_B2) * (g * g)
    m_hat = m / (1.0 - ADAM_B1 ** ADAM_STEP)
    v_hat = v / (1.0 - ADAM_B2 ** ADAM_STEP)
    delta = -ADAM_LR * (m_hat / (jnp.sqrt(v_hat) + ADAM_EPS) + ADAM_WD * w)
    return delta, m, v


def _adamw_sum(name, ws, parts, ms, vs):
    n = len(ws)
    steps = min(_row_steps(w.shape[0]) for w in ws)
    rows = [w.shape[0] // steps for w in ws]

    def body(*refs):
        w_refs, p_refs, m_refs, v_refs = (refs[k * n:(k + 1) * n] for k in range(4))
        out_refs = refs[4 * n:]
        for i in range(n):
            g = p_refs[i][0].astype(F32)
            for k in range(1, 4):
                g = g + p_refs[i][k].astype(F32)
            delta, m_new, v_new = _adamw_math(w_refs[i][...], g, m_refs[i][...], v_refs[i][...])
            for k, val in enumerate((g, delta, m_new, v_new)):
                out_refs[4 * i + k][...] = val

    blk = [pl.BlockSpec((r, w.shape[1]), lambda s: (s, 0)) for r, w in zip(rows, ws)]
    pblk = [pl.BlockSpec((4, r, w.shape[1]), lambda s: (0, s, 0)) for r, w in zip(rows, ws)]
    out_specs, out_shape = [], []
    for b, w in zip(blk, ws):
        out_specs += [b] * 4
        out_shape += [jax.ShapeDtypeStruct(w.shape, F32)] * 4
    flat = pl.pallas_call(
        body, name=name, grid=(steps,),
        in_specs=blk + pblk + blk + blk, out_specs=out_specs, out_shape=out_shape,
        compiler_params=_params(("parallel",)),
    )(*ws, *parts, *ms, *vs)
    return [flat[4 * i:4 * i + 4] for i in range(n)]


_SMALL_SLOTS = (("norm_mix_g", 0, 1, 1024), ("norm_ffn_g", 1, 1, 1024), ("norm_final_g", 2, 1, 1024),
                ("lower_bounds", 3, 2, 512), ("hg_norm_g", 5, 1, 128), ("loss", 6, 1, 128), ("conv_w", 8, 3, 512))
_SMALL_PARAMS = tuple(s for s in _SMALL_SLOTS if s[0] != "loss")
CONV_SHARD = CONV_WIDTH // N_DEV


def _small_pack(small):
    def body(*refs):
        out = refs[-1]
        out[...] = jnp.zeros_like(out)
        for ref, (_, row, rows, lanes) in zip(refs[:-1], _SMALL_SLOTS):
            out[row:row + rows, 0:lanes] = ref[...]

    vmem = pl.BlockSpec(memory_space=pltpu.VMEM)
    return pl.pallas_call(
        body, name="small_pack", in_specs=[vmem] * len(_SMALL_SLOTS), out_specs=vmem,
        out_shape=jax.ShapeDtypeStruct((SMALL_ROWS, 1024), F32),
    )(*[small[name] for name, _, _, _ in _SMALL_SLOTS])


def _small_update(gathered, dev, w, m, v):
    n = len(_SMALL_PARAMS)

    def body(dev_ref, g_ref, *refs):
        w_refs, m_refs, v_refs = refs[:n], refs[n:2 * n], refs[2 * n:3 * n]
        loss_ref, out_refs, sum_scr = refs[3 * n], refs[3 * n + 1:-1], refs[-1]
        total = g_ref[0]
        for k in range(1, N_DEV):
            total = total + g_ref[k]
        sum_scr[...] = total
        loss_ref[...] = sum_scr[6:7, 0:128]
        for p, (name, row, rows, lanes) in enumerate(_SMALL_PARAMS):
            if name == "conv_w":
                g = sum_scr[row:row + rows, 0:CONV_SHARD]
                for s in range(1, N_DEV):
                    g = jnp.where(dev_ref[0] == s, sum_scr[row:row + rows, s * CONV_SHARD:(s + 1) * CONV_SHARD], g)
            else:
                g = sum_scr[row:row + rows, 0:lanes]
            delta, m_new, v_new = _adamw_math(w_refs[p][...], g, m_refs[p][...], v_refs[p][...])
            out_refs[4 * p][...] = g
            out_refs[4 * p + 1][...] = delta
            out_refs[4 * p + 2][...] = m_new
            out_refs[4 * p + 3][...] = v_new

    vmem = pl.BlockSpec(memory_space=pltpu.VMEM)
    outs = [jax.ShapeDtypeStruct((1, 128), F32)]
    for a in w:
        outs += [jax.ShapeDtypeStruct(a.shape, F32)] * 4
    return pl.pallas_call(
        body, name="small_update",
        in_specs=[pl.BlockSpec(memory_space=pltpu.SMEM)] + [vmem] * (1 + 3 * n), out_specs=[vmem] * len(outs),
        out_shape=outs, scratch_shapes=[pltpu.VMEM((SMALL_ROWS, 1024), F32)],
    )(dev, gathered, *w, *m, *v)


def _row_steps(rows):
    for steps in (4, 2):
        if rows % (16 * steps) == 0:
            return steps
    return 1


def _pair_sum(name, by_owner, got, core, after=()):
    n = len(got)

    def body(core_ref, *refs):
        for a_ref, b_ref, o_ref in zip(refs[:n], refs[n:2 * n], refs[2 * n:]):
            o_ref[...] = (a_ref[...].astype(F32) + b_ref[...].astype(F32)).astype(BF16)

    def blk(g):
        return pl.BlockSpec((None,) + g.shape[1:], lambda k, core_ref: (k, 0, 0))

    def mine(g):
        return pl.BlockSpec((None,) + g.shape[1:], lambda k, core_ref: (2 * k + core_ref[0], 0, 0))

    return pl.pallas_call(
        _drop_operands(body, 1 + 2 * n, len(after)), name=name,
        grid_spec=pltpu.PrefetchScalarGridSpec(
            num_scalar_prefetch=1, grid=(4,),
            in_specs=[mine(g) for g in got] + [blk(g) for g in got] + [HBM_SPEC] * len(after),
            out_specs=[blk(g) for g in got]),
        out_shape=[jax.ShapeDtypeStruct(g.shape, BF16) for g in got],
        compiler_params=_params(("parallel",)),
    )(core, *by_owner, *got, *after)


MESH = pl.DeviceIdType.MESH
HBM_SPEC = pl.BlockSpec(memory_space=pl.ANY)


def _handshake(peers):
    barrier = pltpu.get_barrier_semaphore()
    for peer in peers:
        pl.semaphore_signal(barrier, inc=1, device_id=peer, device_id_type=MESH)
    pl.semaphore_wait(barrier, len(peers))


def _comm_call(body, name, operands, out_shape, scratch, collective_id):
    if collective_id is None:
        return pl.pallas_call(body, name=name, in_specs=[HBM_SPEC] * len(operands), out_specs=[HBM_SPEC] * len(out_shape),
                              out_shape=out_shape, scratch_shapes=scratch)(*operands)
    return pl.kernel(body, out_type=out_shape, mesh=plsc.ScalarSubcoreMesh(axis_name="sequencer", num_cores=1),
                     scratch_types=scratch, name=name,
                     compiler_params=pltpu.CompilerParams(collective_id=collective_id))(*operands)


def _all_gather(name, blocks, collective_id=None, after=()):
    n = len(blocks)
    na = len(after)

    def body(*refs):
        x_refs, out_refs = refs[:n], refs[n + na:2 * n + na]
        send_sems, recv_sems, local_sems = refs[2 * n + na:]
        x, y, c = lax.axis_index("x"), lax.axis_index("y"), lax.axis_index("c")
        me, sibling = (x, y, c), (x, y, 1 - c)
        chips = [(1 - x, y), (x, 1 - y), (1 - x, 1 - y)]
        if collective_id is not None:
            _handshake([sibling] + [(*chip, c) for chip in chips])

        def slot(i, px, py, pc):
            return out_refs[i].at[4 * px + 2 * py + pc]

        def copy(i, k, blk, to, src=None):
            return pltpu.make_async_remote_copy(
                src_ref=slot(i, *blk) if src is None else src, dst_ref=slot(i, *blk),
                send_sem=send_sems.at[7 * i + k], recv_sem=recv_sems.at[7 * i + k], device_id=to, device_id_type=MESH)

        mine = [pltpu.make_async_copy(x_refs[i], slot(i, *me), local_sems.at[i]) for i in range(n)]
        for cp in mine:
            cp.start()
        first = []
        for i in range(n):
            first.append(copy(i, 0, me, sibling, src=x_refs[i]))
            first += [copy(i, 1 + j, me, (*chip, c), src=x_refs[i]) for j, chip in enumerate(chips)]
        for cp in first:
            cp.start()
        passed = []
        for i in range(n):
            for j, chip in enumerate(chips):
                copy(i, 1 + j, (*chip, c), me).wait_recv()
                passed.append(copy(i, 4 + j, (*chip, c), sibling))
                passed[-1].start()
        for i in range(n):
            copy(i, 0, sibling, me).wait_recv()
            for j, chip in enumerate(chips):
                copy(i, 4 + j, (*chip, 1 - c), me).wait_recv()
        for cp in first + passed:
            cp.wait_send()
        for cp in mine:
            cp.wait()

    return _comm_call(
        body, name, list(blocks) + list(after), [jax.ShapeDtypeStruct((N_DEV,) + b.shape, b.dtype) for b in blocks],
        [pltpu.SemaphoreType.DMA((7 * n,)), pltpu.SemaphoreType.DMA((7 * n,)), pltpu.SemaphoreType.DMA((n,))],
        collective_id)


def _sibling_swap(name, by_owner, collective_id=None, after=()):
    n = len(by_owner)
    na = len(after)

    def body(*refs):
        x_refs, out_refs = refs[:n], refs[n + na:2 * n + na]
        send_sems, recv_sems = refs[2 * n + na:]
        x, y, c = lax.axis_index("x"), lax.axis_index("y"), lax.axis_index("c")
        if collective_id is not None:
            _handshake([(x, y, 1 - c)])
        copies = []
        for i in range(n):
            for k in range(4):
                copies.append(pltpu.make_async_remote_copy(
                    src_ref=x_refs[i].at[2 * k + 1 - c], dst_ref=out_refs[i].at[k],
                    send_sem=send_sems.at[4 * i + k], recv_sem=recv_sems.at[4 * i + k],
                    device_id=(x, y, 1 - c), device_id_type=MESH))
        for cp in copies:
            cp.start()
        for cp in copies:
            cp.wait()

    return _comm_call(
        body, name, list(by_owner) + list(after),
        [jax.ShapeDtypeStruct((4,) + b.shape[1:], b.dtype) for b in by_owner],
        [pltpu.SemaphoreType.DMA((4 * n,)), pltpu.SemaphoreType.DMA((4 * n,))], collective_id)


def _chip_exchange(name, sums, collective_id=None, after=()):
    n = len(sums)
    na = len(after)

    def body(*refs):
        x_refs, out_refs = refs[:n], refs[n + na:2 * n + na]
        send_sems, recv_sems, local_sems = refs[2 * n + na:]
        x, y, c = lax.axis_index("x"), lax.axis_index("y"), lax.axis_index("c")
        chips = [(1 - x, y), (x, 1 - y), (1 - x, 1 - y)]
        my_chip = 2 * x + y
        if collective_id is not None:
            _handshake([(cx, cy, c) for cx, cy in chips])
        mine = [pltpu.make_async_copy(x_refs[i].at[my_chip], out_refs[i].at[my_chip], local_sems.at[i])
                for i in range(n)]
        for cp in mine:
            cp.start()
        sends = []
        for i in range(n):
            for j, (cx, cy) in enumerate(chips):
                sends.append(pltpu.make_async_remote_copy(
                    src_ref=x_refs[i].at[2 * cx + cy], dst_ref=out_refs[i].at[my_chip],
                    send_sem=send_sems.at[3 * i + j], recv_sem=recv_sems.at[3 * i + j],
                    device_id=(cx, cy, c), device_id_type=MESH))
        for cp in sends:
            cp.start()
        for i in range(n):
            for j, (cx, cy) in enumerate(chips):
                pltpu.make_async_remote_copy(
                    src_ref=x_refs[i].at[my_chip], dst_ref=out_refs[i].at[2 * cx + cy],
                    send_sem=send_sems.at[3 * i + j], recv_sem=recv_sems.at[3 * i + j],
                    device_id=(cx, cy, c), device_id_type=MESH).wait_recv()
        for cp in sends:
            cp.wait_send()
        for cp in mine:
            cp.wait()

    return _comm_call(
        body, name, list(sums) + list(after), [jax.ShapeDtypeStruct(s.shape, s.dtype) for s in sums],
        [pltpu.SemaphoreType.DMA((3 * n,)), pltpu.SemaphoreType.DMA((3 * n,)), pltpu.SemaphoreType.DMA((n,))],
        collective_id)


def _cast_shards(shards):
    n = len(shards)

    def body(*refs):
        for i in range(n):
            refs[n + i][...] = refs[i][...].astype(BF16)

    vmem = pl.BlockSpec(memory_space=pltpu.VMEM)
    return pl.pallas_call(
        body, name="cast_shards", in_specs=[vmem] * n, out_specs=[vmem] * n,
        out_shape=[jax.ShapeDtypeStruct(s.shape, BF16) for s in shards],
        compiler_params=pltpu.CompilerParams(vmem_limit_bytes=VMEM_LIMIT_V7X),
    )(*shards)


BIG = ("w_in", "w_branch_a", "w_branch_b", "w_out", "w_ffn_gate", "w_ffn_up", "w_ffn_down")


def _local_step(x, target, gains, low, conv_w, wg8, reduce):
    g_mix, g_hg, g_ffn, g_fin = gains
    w_in = wg8["w_in"].reshape(N_IN, D_MODEL)
    wg = wg8["w_ffn_gate"].reshape(D_FF, D_MODEL)
    wu = wg8["w_ffn_up"].reshape(D_FF, D_MODEL)
    wa, wb = wg8["w_branch_a"], wg8["w_branch_b"]
    wo = wg8["w_out"].reshape(D_MODEL, D_MODEL)
    wd = wg8["w_ffn_down"].reshape(D_FF, D_MODEL)

    ht, hg, cv, gt, cvo, cvot, o, og, ogt, st = _fwd_in(x, g_mix, w_in, conv_w, low, g_hg)
    x1, mgt = _merge_fwd(og, cvo, gt, x, wa, wb, wo)
    h2t, gate, up, act, loss, d_gfin, dx2, dx2t = _ffn_fwd_loss(x1, g_ffn, wg, wu, wd, target, g_fin)

    dgate, dup, dx1, d_gffn = _ffn_bwd(dx2, x1, gate, up, g_ffn, wg, wu, wd)
    d_wg, d_wu, d_wd = _wgrad_ffn(h2t, dgate, dup, dx2t, act)
    by_owner_ffn = lambda a: a.reshape(N_DEV, D_FF // N_DEV, D_MODEL)
    ffn = dict(w_ffn_down=by_owner_ffn(d_wd), w_ffn_gate=by_owner_ffn(d_wg), w_ffn_up=by_owner_ffn(d_wu))
    dgt, dya, dyb, dog, dcv, d_conv = _merge_bwd(dx1, og, cvo, gt, cv, wa, wb, wo, conv_w)
    sums_ffn, got_ffn = reduce.begin(ffn, sum_after=[dya])
    parts_ffn, updated_ffn = reduce.finish(ffn, sums_ffn)
    grad_a, grad_b, grad_o = _wgrad_out_branches(ogt, dya, cvot, dyb, mgt, dx1, after=sums_ffn[:1])
    out = dict(w_out=grad_o.reshape(N_DEV, D_MODEL // N_DEV, D_MODEL), w_branch_a=grad_a, w_branch_b=grad_b)
    dhg, d_low, d_ghg = _hg_bwd(dog, hg, o, st, low, g_hg, after=list(sums_ffn) + [out["w_out"]])
    sums_out, got_out = reduce.begin(out, after=[parts_ffn[0], dhg], sum_after=updated_ffn)
    parts_out, updated_out = reduce.finish(out, sums_out)
    dparts = [dhg, dcv, dgt]
    w_in_grad = dict(w_in=_wgrad_in(ht, dparts, after=sums_out[:1]).reshape(N_DEV, N_IN // N_DEV, D_MODEL))
    sums_in, _ = reduce.begin(w_in_grad, after=parts_out[:1], sum_after=updated_out)
    parts_in, _ = reduce.finish(w_in_grad, sums_in)
    grad_x, d_gmix = _in_bwd(dparts, w_in, x, dx1, g_mix, after=list(parts_out[:1]) + list(sums_in))
    small = dict(norm_mix_g=d_gmix, norm_ffn_g=d_gffn, norm_final_g=d_gfin, lower_bounds=d_low, hg_norm_g=d_ghg,
                 conv_w=d_conv, loss=loss)
    return grad_x, small, parts_in


def _conv_shard_rows(a):
    return jnp.pad(a, ((0, 5), (0, 64)))


def kernel(x, norm_mix_g, w_in, lower_bounds, hg_norm_g, conv_w, w_branch_a, w_branch_b, w_out, norm_ffn_g, w_ffn_gate, w_ffn_up, w_ffn_down, norm_final_g, loss_target, m_norm_mix_g, m_w_in, m_lower_bounds, m_hg_norm_g, m_conv_w, m_w_branch_a, m_w_branch_b, m_w_out, m_norm_ffn_g, m_w_ffn_gate, m_w_ffn_up, m_w_ffn_down, m_norm_final_g, v_norm_mix_g, v_w_in, v_lower_bounds, v_hg_norm_g, v_conv_w, v_w_branch_a, v_w_branch_b, v_w_out, v_norm_ffn_g, v_w_ffn_gate, v_w_ffn_up, v_w_ffn_down, v_norm_final_g):
    cx, cy, cc = lax.axis_index("x"), lax.axis_index("y"), lax.axis_index("c")
    my_dev = 4 * cx + 2 * cy + cc

    def tr(a):
        return a[0].T

    big = dict(w_in=tr(w_in), w_branch_a=w_branch_a[0], w_branch_b=w_branch_b[0], w_out=w_out[0],
               w_ffn_gate=tr(w_ffn_gate), w_ffn_up=tr(w_ffn_up), w_ffn_down=w_ffn_down[0])
    big_m = dict(w_in=tr(m_w_in), w_branch_a=m_w_branch_a[0], w_branch_b=m_w_branch_b[0], w_out=m_w_out[0],
                 w_ffn_gate=tr(m_w_ffn_gate), w_ffn_up=tr(m_w_ffn_up), w_ffn_down=m_w_ffn_down[0])
    big_v = dict(w_in=tr(v_w_in), w_branch_a=v_w_branch_a[0], w_branch_b=v_w_branch_b[0], w_out=v_w_out[0],
                 w_ffn_gate=tr(v_w_ffn_gate), w_ffn_up=tr(v_w_ffn_up), w_ffn_down=v_w_ffn_down[0])
    transposed = ("w_in", "w_ffn_gate", "w_ffn_up")

    shards = dict(zip(BIG, _cast_shards([big[n] for n in BIG])))
    first = _all_gather("gather_w_in", [shards["w_in"], _conv_shard_rows(conv_w[0])])
    ids = iter(range(1, 16))
    mid = _all_gather("gather_mid", [shards[n] for n in BIG[1:4]], collective_id=next(ids), after=first[1:])
    ffn = _all_gather("gather_ffn", [shards[n] for n in BIG[4:]], collective_id=next(ids), after=first[1:])
    wg8 = dict(zip(BIG, [first[0]] + list(mid) + list(ffn)))
    conv_full = first[1][:, :3, :64].transpose(1, 0, 2).reshape(3, CONV_WIDTH)

    core = cc.reshape(1).astype(jnp.int32)
    outs = {}

    class Reduce:
        @staticmethod
        def begin(grads, after=(), sum_after=()):
            names = list(grads)
            by_owner = [grads[n] for n in names]
            got = _sibling_swap("sibling_swap_" + names[0], by_owner, collective_id=next(ids), after=after)
            sums = _pair_sum("pair_sum_" + names[0], by_owner, got, core, after=sum_after)
            return sums, got

        @staticmethod
        def finish(grads, chip_sums, after=()):
            names = list(grads)
            parts = _chip_exchange("chip_exchange_" + names[0], chip_sums, collective_id=next(ids), after=after)
            updated = _adamw_sum("adamw_" + names[0], [big[n] for n in names], parts,
                                 [big_m[n] for n in names], [big_v[n] for n in names])
            outs.update(zip(names, updated))
            return parts, [outs[n][1] for n in names]

    gains = (norm_mix_g, hg_norm_g, norm_ffn_g, norm_final_g.reshape(1, D_MODEL))
    grad_x, small, last = _local_step(x[0], loss_target[0], gains, lower_bounds, conv_full, wg8, Reduce)

    small_all = _all_gather("gather_small", [_small_pack(small)], collective_id=next(ids), after=last[:1])

    def small_state(a):
        return [a[0], a[1], a[2].reshape(1, D_MODEL), a[3], a[4], a[5][0]]

    upd = _small_update(
        small_all[0], my_dev.reshape(1).astype(jnp.int32),
        small_state((norm_mix_g, norm_ffn_g, norm_final_g, lower_bounds, hg_norm_g, conv_w)),
        small_state((m_norm_mix_g, m_norm_ffn_g, m_norm_final_g, m_lower_bounds, m_hg_norm_g, m_conv_w)),
        small_state((v_norm_mix_g, v_norm_ffn_g, v_norm_final_g, v_lower_bounds, v_hg_norm_g, v_conv_w)))
    loss = upd[0][0, 0]
    small_shape = dict(norm_final_g=(D_MODEL,), conv_w=(1, 3, CONV_SHARD))
    for p, (name, _, _, _) in enumerate(_SMALL_PARAMS):
        outs[name] = [a.reshape(small_shape.get(name, a.shape)) for a in upd[1 + 4 * p:5 + 4 * p]]

    order = ["norm_mix_g", "w_in", "lower_bounds", "hg_norm_g", "conv_w", "w_branch_a", "w_branch_b", "w_out",
             "norm_ffn_g", "w_ffn_gate", "w_ffn_up", "w_ffn_down", "norm_final_g"]
    result = [loss, grad_x[None]]
    for k in range(4):
        for n in order:
            if n in BIG:
                result.append((outs[n][k].T if n in transposed else outs[n][k])[None])
            else:
                result.append(outs[n][k])
    return tuple(result)
```

```python
import jax
import jax.numpy as jnp
from jax import lax
from jax.experimental import pallas as pl
from jax.experimental.pallas import tpu as pltpu
from jax.experimental.pallas import tpu_sc as plsc

F32 = jnp.float32
BF16 = jnp.bfloat16
STASH = jnp.bfloat16

D_MODEL = 1024
HG_WIDTH = 512
HEAD_DIM = 128
N_HEADS = 4
HEADS_PER_STEP = 4
HEAD_GROUPS = N_HEADS // HEADS_PER_STEP
CONV_WIDTH = 512
CONV_K = 3
D_FF = 2816
CHUNK = 32
EPS = 1e-6
Q_SCALE = HEAD_DIM ** -0.5
N_DEV = 8

ADAM_LR = 0.001
ADAM_B1 = 0.9
ADAM_B2 = 0.999
ADAM_EPS = 1e-08
ADAM_WD = 0.01
ADAM_STEP = 10

VMEM_LIMIT_V7X = 56 * 1024 * 1024
VMEM_LIMIT_LARGE_V7X = 62 * 1024 * 1024

SMALL_ROWS = 16


def _params(sem, vmem=VMEM_LIMIT_V7X):
    return pltpu.CompilerParams(dimension_semantics=sem, vmem_limit_bytes=vmem)


def _mm(a, b):
    return jnp.dot(a.astype(BF16), b.astype(BF16), preferred_element_type=F32)


def _mm_nt(a, b):
    return lax.dot_general(a.astype(BF16), b.astype(BF16), (((1,), (1,)), ((), ())), preferred_element_type=F32)


def _mm_tn(a, b):
    return lax.dot_general(a.astype(BF16), b.astype(BF16), (((0,), (0,)), ((), ())), preferred_element_type=F32)


def _sigmoid(x):
    return 0.5 * jnp.tanh(0.5 * x) + 0.5


def _resident(shape):
    nd = len(shape)
    return pl.BlockSpec(shape, lambda *_: (0,) * nd, pipeline_mode=pl.Buffered(1))


def _full(shape):
    nd = len(shape)
    return pl.BlockSpec(shape, lambda *_: (0,) * nd)


def _shard_cols(w_ref):
    return jnp.concatenate([w_ref[s] for s in range(N_DEV)], axis=1)


N_HG = 4 * HG_WIDTH
N_CV = 3 * CONV_WIDTH
N_GT = 2 * D_MODEL
N_IN = N_HG + N_CV + N_GT


def _col(tm, n):
    return pl.BlockSpec((n, tm), lambda i: (0, i))


HALO = 8


def _fwd_in(x, g, w_in_t, conv_w, low, gn):
    T = x.shape[0]
    tm = min(512, T)
    nc = tm // CHUNK

    def body(x_ref, g_ref, w_ref, cw_ref, low_ref, gn_ref, ht_ref, hg_ref, cv_ref, gt_ref, cvo_ref, cvot_ref,
             o_ref, og_ref, ogt_ref, st_ref, tail_scr, s_scr):
        @pl.when(pl.program_id(0) == 0)
        def _():
            tail_scr[...] = jnp.zeros_like(tail_scr)
            s_scr[...] = jnp.zeros_like(s_scr)

        xv = x_ref[...]
        r = lax.rsqrt(jnp.mean(xv * xv, axis=-1, keepdims=True) + EPS)
        hf = xv * r * g_ref[...]
        h = hf.astype(BF16)
        ht_ref[...] = hf.T.astype(BF16)
        hg_ref[...] = _mm_nt(h, w_ref[:N_HG, :])
        cv = _mm_nt(h, w_ref[N_HG:N_HG + N_CV, :])
        cv_ref[...] = cv.astype(STASH)
        gt_ref[...] = _mm_nt(h, w_ref[N_HG + N_CV:, :]).astype(STASH)

        u = cv[:, :CONV_WIDTH] * cv[:, 2 * CONV_WIDTH:]
        row = lax.broadcasted_iota(jnp.int32, u.shape, 0)
        prev1 = tail_scr[HALO - 1:HALO, :]
        prev2 = tail_scr[HALO - 2:HALO - 1, :]
        u1 = jnp.where(row >= 1, pltpu.roll(u, 1, 0), prev1)
        u2 = jnp.where(row >= 2, pltpu.roll(u, 2, 0), jnp.where(row == 1, prev1, prev2))
        y = cw_ref[0:1, :] * u2 + cw_ref[1:2, :] * u1 + cw_ref[2:3, :] * u
        out = cv[:, CONV_WIDTH:2 * CONV_WIDTH] * y
        cvo_ref[...] = out.astype(BF16)
        cvot_ref[...] = out.T.astype(BF16)
        tail_scr[...] = u[tm - HALO:, :]

        _hg_fwd_tile(hg_ref, low_ref, gn_ref, o_ref, og_ref, ogt_ref, st_ref, s_scr, tm)

    row = lambda n: pl.BlockSpec((tm, n), lambda i: (i, 0))
    return pl.pallas_call(
        body, name="fwd_in", grid=(T // tm,),
        in_specs=[row(D_MODEL), _full((1, D_MODEL)), _resident(w_in_t.shape), _full((CONV_K, CONV_WIDTH)),
                  _full((2, HG_WIDTH)), _full((1, HEAD_DIM))],
        out_specs=[_col(tm, D_MODEL), row(N_HG), row(N_CV), row(N_GT), row(CONV_WIDTH), _col(tm, CONV_WIDTH),
                   row(HG_WIDTH), row(HG_WIDTH), _col(tm, HG_WIDTH),
                   pl.BlockSpec((N_HEADS, nc, HEAD_DIM, HEAD_DIM), lambda i: (0, i, 0, 0))],
        out_shape=[jax.ShapeDtypeStruct((D_MODEL, T), BF16), jax.ShapeDtypeStruct((T, N_HG), F32),
                   jax.ShapeDtypeStruct((T, N_CV), STASH), jax.ShapeDtypeStruct((T, N_GT), STASH),
                   jax.ShapeDtypeStruct((T, CONV_WIDTH), BF16), jax.ShapeDtypeStruct((CONV_WIDTH, T), BF16),
                   jax.ShapeDtypeStruct((T, HG_WIDTH), F32), jax.ShapeDtypeStruct((T, HG_WIDTH), BF16),
                   jax.ShapeDtypeStruct((HG_WIDTH, T), BF16),
                   jax.ShapeDtypeStruct((N_HEADS, T // CHUNK, HEAD_DIM, HEAD_DIM), F32)],
        scratch_shapes=[pltpu.VMEM((HALO, CONV_WIDTH), F32), pltpu.VMEM((N_HEADS, HEAD_DIM, HEAD_DIM), F32)],
        compiler_params=_params(("arbitrary",), vmem=VMEM_LIMIT_LARGE_V7X),
    )(x, g, w_in_t, conv_w, low, gn)


def _chunk_pos(shape):
    return lax.broadcasted_iota(jnp.int32, shape, 0) & (CHUNK - 1)


def _chunk_cumsum(x, pos):
    s = 1
    while s < CHUNK:
        x = x + jnp.where(pos >= s, pltpu.roll(x, s, 0), 0.0)
        s *= 2
    return x


def _chunk_rev_cumsum(x, pos):
    n = x.shape[0]
    s = 1
    while s < CHUNK:
        x = x + jnp.where(pos + s < CHUNK, pltpu.roll(x, n - s, 0), 0.0)
        s *= 2
    return x


def _lower_bound(low_ref):
    l0 = low_ref[0:1, :]
    l1 = low_ref[1:2, :]
    m = jnp.maximum(l0, l1)
    e0 = jnp.exp(l0 - m)
    e1 = jnp.exp(l1 - m)
    return e0 / (e0 + e1), e1 / (e0 + e1)


def _hg_gates(qr, fr, lb, pos, tb):
    sq = _sigmoid(qr)
    q = qr * sq * Q_SCALE
    sg = _sigmoid(fr)
    f = lb + (1.0 - lb) * sg
    k = 1.0 - f
    b = _chunk_cumsum(jnp.log(f), pos)
    b3 = b.reshape(tb // CHUNK, CHUNK, HEAD_DIM)
    anc = b3[:, CHUNK // 2 - 1:CHUNK // 2, :]
    last = b3[:, CHUNK - 1:CHUNK, :]
    d3 = b3 - anc
    e_qa3 = jnp.exp(d3)
    e_ka3 = jnp.exp(-d3)
    e_b3 = e_qa3 * jnp.exp(anc)
    e_ko3 = e_ka3 * jnp.exp(last - anc)
    dec = jnp.exp(last)
    flat = lambda a: a.reshape(tb, HEAD_DIM)
    return sq, q, sg, f, k, flat(e_qa3), flat(e_ka3), flat(e_b3), flat(e_ko3), dec


def _intra_mask(sb):
    r = lax.broadcasted_iota(jnp.int32, (sb, sb), 0)
    c = lax.broadcasted_iota(jnp.int32, (sb, sb), 1)
    return ((r // CHUNK) == (c // CHUNK)) & (c <= r)


def _hg_fwd_tile(hg_ref, low_ref, gn_ref, o_ref, og_ref, ogt_ref, st_ref, s_scr, tb):
    sb = min(256, tb)
    nc = tb // CHUNK
    q_ref, f_ref, i_ref, g_ref = (hg_ref.at[:, p * HG_WIDTH:(p + 1) * HG_WIDTH] for p in range(4))
    pos = _chunk_pos((tb, HEAD_DIM))
    mask = _intra_mask(sb)
    lanes = [slice(hh * HEAD_DIM, (hh + 1) * HEAD_DIM) for hh in range(N_HEADS)]
    qi, ko, vb, dec, st = [], [], [], [], []
    for hh, ln in enumerate(lanes):
        lb, _ = _lower_bound(low_ref.at[:, ln])
        _, q, _, _, k, e_qa, e_ka, e_b, e_ko, dec_h = _hg_gates(q_ref[:, ln], f_ref[:, ln], lb, pos, tb)
        qh = (q * e_qa).astype(BF16)
        kh = (k * e_ka).astype(BF16)
        qi.append((q * e_b).astype(BF16))
        ko.append((k * e_ko).astype(BF16))
        vb.append(i_ref[:, ln].astype(BF16))
        dec.append(dec_h)
        st.append(s_scr[hh])
        for s in range(tb // sb):
            sl = slice(s * sb, (s + 1) * sb)
            p = jnp.where(mask, _mm_nt(qh[sl], kh[sl]), 0.0)
            o_ref[sl, ln] = _mm(p, vb[hh][sl])
    for c in range(nc):
        sl = slice(c * CHUNK, (c + 1) * CHUNK)
        for hh, ln in enumerate(lanes):
            st_ref[hh, c] = st[hh]
            o_ref[sl, ln] = o_ref[sl, ln] + _mm_nt(qi[hh][sl], st[hh])
            st[hh] = dec[hh][c] * st[hh] + _mm_tn(vb[hh][sl], ko[hh][sl])
    for hh, ln in enumerate(lanes):
        s_scr[hh] = st[hh]
        o = o_ref[:, ln]
        r = lax.rsqrt(jnp.mean(o * o, axis=-1, keepdims=True) + EPS)
        gr = g_ref[:, ln]
        og = (o * r * gn_ref[...]) * (gr * _sigmoid(gr))
        og_ref[:, ln] = og.astype(BF16)
        ogt_ref[ln, :] = og.T.astype(BF16)


def _merge_fwd(og, cvo, gt, x, wa, wb, wo):
    T = x.shape[0]
    tm = min(1024, T)

    def body(og_ref, cvo_ref, gt_ref, x_ref, wa_ref, wb_ref, wo_ref, x1_ref, mgt_ref):
        ya = jnp.dot(og_ref[...], _shard_cols(wa_ref), preferred_element_type=F32)
        yb = jnp.dot(cvo_ref[...], _shard_cols(wb_ref), preferred_element_type=F32)
        m = (_sigmoid(gt_ref[:, :D_MODEL].astype(F32)) * ya
             + _sigmoid(gt_ref[:, D_MODEL:].astype(F32)) * yb)
        mgt_ref[...] = m.T.astype(BF16)
        x1_ref[...] = x_ref[...] + jnp.dot(m.astype(BF16), wo_ref[...], preferred_element_type=F32)

    row = lambda n: pl.BlockSpec((tm, n), lambda i: (i, 0))
    return pl.pallas_call(
        body, name="merge_fwd", grid=(T // tm,),
        in_specs=[row(HG_WIDTH), row(CONV_WIDTH), row(2 * D_MODEL), row(D_MODEL),
                  _resident(wa.shape), _resident(wb.shape), _resident(wo.shape)],
        out_specs=[row(D_MODEL), _col(tm, D_MODEL)],
        out_shape=[jax.ShapeDtypeStruct((T, D_MODEL), F32), jax.ShapeDtypeStruct((D_MODEL, T), BF16)],
        compiler_params=_params(("parallel",)),
    )(og, cvo, gt, x, wa, wb, wo)


def _ffn_fwd_loss(x1, g, wg, wu, wd, target, g_fin):
    T = x1.shape[0]
    tm = min(512, T)

    def body(x_ref, g_ref, wg_ref, wu_ref, wd_ref, t_ref, gf_ref,
             ht_ref, gate_ref, up_ref, act_ref, loss_ref, dgf_ref, dx2_ref, dx2t_ref):
        @pl.when(pl.program_id(0) == 0)
        def _():
            loss_ref[...] = jnp.zeros_like(loss_ref)
            dgf_ref[...] = jnp.zeros_like(dgf_ref)

        xv = x_ref[...]
        r = lax.rsqrt(jnp.mean(xv * xv, axis=-1, keepdims=True) + EPS)
        hf = xv * r * g_ref[...]
        h = hf.astype(BF16)
        ht_ref[...] = hf.T.astype(BF16)
        gate = _mm_nt(h, wg_ref[...])
        up = _mm_nt(h, wu_ref[...])
        gate_ref[...] = gate.astype(STASH)
        up_ref[...] = up.astype(STASH)
        act = (gate * _sigmoid(gate) * up).astype(BF16)
        act_ref[...] = act
        x2 = xv + jnp.dot(act, wd_ref[...], preferred_element_type=F32)

        gv = gf_ref[...]
        r2 = lax.rsqrt(jnp.mean(x2 * x2, axis=-1, keepdims=True) + EPS)
        xh = x2 * r2
        err = xh * gv - t_ref[...]
        loss_ref[...] += 0.5 * jnp.sum(jnp.mean(err * err, axis=-1, keepdims=True), axis=0, keepdims=True)
        dy = err * (1.0 / D_MODEL)
        dgf_ref[...] += jnp.sum(dy * xh, axis=0, keepdims=True)
        w = dy * gv
        dx2 = r2 * (w - xh * jnp.mean(w * xh, axis=-1, keepdims=True))
        dx2_ref[...] = dx2
        dx2t_ref[...] = dx2.T.astype(BF16)

    row = lambda n: pl.BlockSpec((tm, n), lambda i: (i, 0))
    return pl.pallas_call(
        body, name="ffn_fwd_loss", grid=(T // tm,),
        in_specs=[row(D_MODEL), _full((1, D_MODEL)), _resident(wg.shape), _resident(wu.shape), _resident(wd.shape),
                  row(D_MODEL), _full((1, D_MODEL))],
        out_specs=[_col(tm, D_MODEL), row(D_FF), row(D_FF), row(D_FF), _full((1, 128)), _full((1, D_MODEL)),
                   row(D_MODEL), _col(tm, D_MODEL)],
        out_shape=[jax.ShapeDtypeStruct((D_MODEL, T), BF16), jax.ShapeDtypeStruct((T, D_FF), STASH),
                   jax.ShapeDtypeStruct((T, D_FF), STASH), jax.ShapeDtypeStruct((T, D_FF), BF16),
                   jax.ShapeDtypeStruct((1, 128), F32), jax.ShapeDtypeStruct((1, D_MODEL), F32),
                   jax.ShapeDtypeStruct((T, D_MODEL), F32), jax.ShapeDtypeStruct((D_MODEL, T), BF16)],
        compiler_params=_params(("arbitrary",), vmem=VMEM_LIMIT_LARGE_V7X),
    )(x1, g, wg, wu, wd, target, g_fin)


def _ffn_bwd(dx2, x1, gate, up, g, wg, wu, wd):
    T = x1.shape[0]
    tm = min(512, T)

    def body(dx2_ref, x_ref, gate_ref, up_ref, g_ref, wg_ref, wu_ref, wd_ref, dgate_ref, dup_ref, dx1_ref, dgn_ref):
        @pl.when(pl.program_id(0) == 0)
        def _():
            dgn_ref[...] = jnp.zeros_like(dgn_ref)

        dx2 = dx2_ref[...]
        dact = _mm_nt(dx2, wd_ref[...])
        gate = gate_ref[...].astype(F32)
        s = _sigmoid(gate)
        dgate = (dact * up_ref[...].astype(F32) * (s * (1.0 + gate * (1.0 - s)))).astype(BF16)
        dup = (dact * (gate * s)).astype(BF16)
        dgate_ref[...] = dgate
        dup_ref[...] = dup
        dh = _mm(dgate, wg_ref[...]) + _mm(dup, wu_ref[...])
        xv = x_ref[...]
        r = lax.rsqrt(jnp.mean(xv * xv, axis=-1, keepdims=True) + EPS)
        xh = xv * r
        dgn_ref[...] += jnp.sum(dh * xh, axis=0, keepdims=True)
        w = dh * g_ref[...]
        dx1_ref[...] = dx2 + r * (w - xh * jnp.mean(w * xh, axis=-1, keepdims=True))

    row = lambda n: pl.BlockSpec((tm, n), lambda i: (i, 0))
    return pl.pallas_call(
        body, name="ffn_bwd", grid=(T // tm,),
        in_specs=[row(D_MODEL), row(D_MODEL), row(D_FF), row(D_FF), _full((1, D_MODEL)),
                  _resident(wg.shape), _resident(wu.shape), _resident(wd.shape)],
        out_specs=[row(D_FF), row(D_FF), row(D_MODEL), _full((1, D_MODEL))],
        out_shape=[jax.ShapeDtypeStruct((T, D_FF), BF16), jax.ShapeDtypeStruct((T, D_FF), BF16),
                   jax.ShapeDtypeStruct((T, D_MODEL), F32), jax.ShapeDtypeStruct((1, D_MODEL), F32)],
        compiler_params=_params(("arbitrary",), vmem=VMEM_LIMIT_LARGE_V7X),
    )(dx2, x1, gate, up, g, wg, wu, wd)


def _merge_bwd(dx1, og, cvo, gt, cv, wa, wb, wo, conv_w):
    T = dx1.shape[0]
    tm = min(512, T)
    nt = T // tm

    def body(dx_ref, og_ref, cvo_ref, gt_ref, cv_ref, halo_ref, wa_ref, wb_ref, wo_ref, cw_ref,
             dgt_ref, dya_ref, dyb_ref, dog_ref, dcv_ref, dcw_ref, prev_u, next_dy):
        step = pl.program_id(0)

        @pl.when(step == 0)
        def _():
            next_dy[...] = jnp.zeros_like(next_dy)
            dcw_ref[...] = jnp.zeros_like(dcw_ref)

        dm = _mm_nt(dx_ref[...], wo_ref[...])
        wa = _shard_cols(wa_ref)
        wb = _shard_cols(wb_ref)
        ya = jnp.dot(og_ref[...], wa, preferred_element_type=F32)
        yb = jnp.dot(cvo_ref[...], wb, preferred_element_type=F32)
        sa = _sigmoid(gt_ref[:, :D_MODEL].astype(F32))
        sb = _sigmoid(gt_ref[:, D_MODEL:].astype(F32))
        dgt_ref[:, :D_MODEL] = (dm * ya * (sa * (1.0 - sa))).astype(BF16)
        dgt_ref[:, D_MODEL:] = (dm * yb * (sb * (1.0 - sb))).astype(BF16)
        dya = (dm * sa).astype(BF16)
        dyb = (dm * sb).astype(BF16)
        dya_ref[...] = dya
        dyb_ref[...] = dyb
        dog_ref[...] = _mm_nt(dya, wa)
        dcvo = _mm_nt(dyb, wb)

        cvt = cv_ref[...].astype(F32)
        c, bg, xb = cvt[:, :CONV_WIDTH], cvt[:, CONV_WIDTH:2 * CONV_WIDTH], cvt[:, 2 * CONV_WIDTH:]
        halo = halo_ref[...].astype(F32)
        first_tile = step == nt - 1
        prev_u[...] = jnp.where(first_tile, 0.0, halo[:, :CONV_WIDTH] * halo[:, 2 * CONV_WIDTH:])
        u = c * xb
        row = lax.broadcasted_iota(jnp.int32, u.shape, 0)
        p1 = prev_u[HALO - 1:HALO, :]
        p2 = prev_u[HALO - 2:HALO - 1, :]
        u1 = jnp.where(row >= 1, pltpu.roll(u, 1, 0), p1)
        u2 = jnp.where(row >= 2, pltpu.roll(u, 2, 0), jnp.where(row == 1, p1, p2))
        w0, w1, w2 = cw_ref[0:1, :], cw_ref[1:2, :], cw_ref[2:3, :]
        y = w0 * u2 + w1 * u1 + w2 * u
        dcv_ref[:, CONV_WIDTH:2 * CONV_WIDTH] = (dcvo * y).astype(BF16)
        dy = dcvo * bg
        dcw_ref[0:1, :] += jnp.sum(dy * u2, axis=0, keepdims=True)
        dcw_ref[1:2, :] += jnp.sum(dy * u1, axis=0, keepdims=True)
        dcw_ref[2:3, :] += jnp.sum(dy * u, axis=0, keepdims=True)
        n1 = next_dy[0:1, :]
        n2 = next_dy[1:2, :]
        dy1 = jnp.where(row < tm - 1, pltpu.roll(dy, tm - 1, 0), n1)
        dy2 = jnp.where(row < tm - 2, pltpu.roll(dy, tm - 2, 0), jnp.where(row == tm - 2, n1, n2))
        du = w2 * dy + w1 * dy1 + w0 * dy2
        dcv_ref[:, :CONV_WIDTH] = (du * xb).astype(BF16)
        dcv_ref[:, 2 * CONV_WIDTH:] = (du * c).astype(BF16)
        next_dy[...] = dy[:HALO, :]

    rt = lambda i: nt - 1 - i
    row = lambda n: pl.BlockSpec((tm, n), lambda i: (rt(i), 0))
    halo = pl.BlockSpec((HALO, N_CV), lambda i: (jnp.maximum(rt(i) * (tm // HALO) - 1, 0), 0))
    return pl.pallas_call(
        body, name="merge_bwd", grid=(nt,),
        in_specs=[row(D_MODEL), row(HG_WIDTH), row(CONV_WIDTH), row(2 * D_MODEL), row(N_CV), halo,
                  _resident(wa.shape), _resident(wb.shape), _resident(wo.shape), _full((CONV_K, CONV_WIDTH))],
        out_specs=[row(2 * D_MODEL), row(D_MODEL), row(D_MODEL), row(HG_WIDTH), row(N_CV),
                   _full((CONV_K, CONV_WIDTH))],
        out_shape=[jax.ShapeDtypeStruct((T, 2 * D_MODEL), BF16), jax.ShapeDtypeStruct((T, D_MODEL), BF16),
                   jax.ShapeDtypeStruct((T, D_MODEL), BF16), jax.ShapeDtypeStruct((T, HG_WIDTH), F32),
                   jax.ShapeDtypeStruct((T, N_CV), BF16), jax.ShapeDtypeStruct((CONV_K, CONV_WIDTH), F32)],
        scratch_shapes=[pltpu.VMEM((HALO, CONV_WIDTH), F32), pltpu.VMEM((HALO, CONV_WIDTH), F32)],
        compiler_params=_params(("arbitrary",)),
    )(dx1, og, cvo, gt, cv, cv, wa, wb, wo, conv_w)


def _drop_operands(body, first, count):
    def wrapped(*refs):
        return body(*refs[:first], *refs[first + count:])
    return wrapped


def _hg_bwd(dog, hg, o, st, low, gn, after=()):
    T = hg.shape[0]
    tb = min(512, T)
    sb = min(256, tb)
    nb = T // tb
    nc = tb // CHUNK
    wid = HEADS_PER_STEP * HEAD_DIM

    def body(q_ref, f_ref, i_ref, g_ref, low_ref, gn_ref, o_ref, dog_ref, st_ref,
             dhg_ref, dlow_ref, dgn_ref,
             ds_scr, dqi_scr, dko_scr, dv_scr, dd_scr, dqh_scr, dkh_scr):
        h = pl.program_id(0)
        t = pl.program_id(1)
        dq_ref, df_ref, di_ref, dg_ref = (dhg_ref.at[:, p * HG_WIDTH:(p + 1) * HG_WIDTH] for p in range(4))

        @pl.when(t == 0)
        def _():
            ds_scr[...] = jnp.zeros_like(ds_scr)
            dlow_ref[...] = jnp.zeros_like(dlow_ref)

        @pl.when((t == 0) & (h == 0))
        def _():
            dgn_ref[...] = jnp.zeros_like(dgn_ref)

        pos = _chunk_pos((tb, HEAD_DIM))
        mask = _intra_mask(sb)
        gnv = gn_ref[...]
        lanes = [slice(hh * HEAD_DIM, (hh + 1) * HEAD_DIM) for hh in range(HEADS_PER_STEP)]
        heads = []
        for hh, ln in enumerate(lanes):
            lb, lb1 = _lower_bound(low_ref.at[:, ln])
            qr = q_ref[:, ln]
            sq, q, sg, f, k, e_qa, e_ka, e_b, e_ko, dec = _hg_gates(qr, f_ref[:, ln], lb, pos, tb)

            gr = g_ref[:, ln]
            o = o_ref[:, ln]
            dog_v = dog_ref[:, ln]
            sgr = _sigmoid(gr)
            r = lax.rsqrt(jnp.mean(o * o, axis=-1, keepdims=True) + EPS)
            oh = o * r
            dg_ref[:, ln] = (dog_v * (oh * gnv) * (sgr * (1.0 + gr * (1.0 - sgr)))).astype(BF16)
            don = dog_v * (gr * sgr)
            dgn_ref[...] += jnp.sum(don * oh, axis=0, keepdims=True)
            w = don * gnv
            do = (r * (w - oh * jnp.mean(w * oh, axis=-1, keepdims=True))).astype(BF16)

            qh = (q * e_qa).astype(BF16)
            kh = (k * e_ka).astype(BF16)
            qi = (q * e_b).astype(BF16)
            ko = (k * e_ko).astype(BF16)
            vb = i_ref[:, ln].astype(BF16)

            for s in range(tb // sb):
                sl = slice(s * sb, (s + 1) * sb)
                p = jnp.where(mask, _mm_nt(qh[sl], kh[sl]), 0.0).astype(BF16)
                dp = jnp.where(mask, _mm_nt(do[sl], vb[sl]), 0.0).astype(BF16)
                dv_scr[sl, ln] = _mm_tn(p, do[sl])
                dqh_scr[sl, ln] = _mm(dp, kh[sl])
                dkh_scr[sl, ln] = _mm_tn(dp, qh[sl])
            heads.append(dict(lb=lb, lb1=lb1, qr=qr, sq=sq, q=q, sg=sg, f=f, k=k, e_qa=e_qa, e_ka=e_ka, e_b=e_b,
                              e_ko=e_ko, dec=dec, do=do, qi=qi, ko=ko, vb=vb, ds=ds_scr[hh]))

        for c in reversed(range(nc)):
            sl = slice(c * CHUNK, (c + 1) * CHUNK)
            for hh, ln in enumerate(lanes):
                hd = heads[hh]
                ds = hd["ds"]
                st_c = st_ref[hh, c]
                dqi_scr[sl, ln] = _mm(hd["do"][sl], st_c)
                dko_scr[sl, ln] = _mm(hd["vb"][sl], ds)
                dv_scr[sl, ln] = dv_scr[sl, ln] + _mm_nt(hd["ko"][sl], ds)
                dec_c = hd["dec"][c]
                dd_scr[sl, ln] = jnp.broadcast_to(dec_c * jnp.sum(ds * st_c, axis=0, keepdims=True),
                                                  (CHUNK, HEAD_DIM))
                hd["ds"] = dec_c * ds + _mm_tn(hd["do"][sl], hd["qi"][sl])

        for hh, ln in enumerate(lanes):
            hd = heads[hh]
            ds_scr[hh] = hd["ds"]
            q, k, lb = hd["q"], hd["k"], hd["lb"]
            dko_e = dko_scr[:, ln] * hd["e_ko"]
            dq = dqh_scr[:, ln] * hd["e_qa"] + dqi_scr[:, ln] * hd["e_b"]
            dk = dkh_scr[:, ln] * hd["e_ka"] + dko_e
            kd3 = (k * dko_e).reshape(nc, CHUNK, HEAD_DIM)
            last = jnp.broadcast_to(jnp.sum(kd3, axis=1, keepdims=True), kd3.shape).reshape(tb, HEAD_DIM)
            db = q * dq - k * dk + jnp.where(pos == CHUNK - 1, dd_scr[:, ln] + last, 0.0)
            dlg = _chunk_rev_cumsum(db, pos)
            dfv = dlg / hd["f"] - dk
            s_low = jnp.sum(dfv * (1.0 - hd["sg"]), axis=0, keepdims=True)
            dlow_ref[0:1, ln] += s_low * lb * (1.0 - lb)
            dlow_ref[1:2, ln] += -s_low * lb * hd["lb1"]
            df_ref[:, ln] = (dfv * (1.0 - lb) * hd["sg"] * (1.0 - hd["sg"])).astype(BF16)
            dq_ref[:, ln] = (dq * Q_SCALE * (hd["sq"] * (1.0 + hd["qr"] * (1.0 - hd["sq"])))).astype(BF16)
            di_ref[:, ln] = dv_scr[:, ln].astype(BF16)

    rt = lambda t: nb - 1 - t
    col = lambda p: pl.BlockSpec((tb, wid), lambda h, t: (rt(t), p * HEAD_GROUPS + h))
    hcol = pl.BlockSpec((tb, wid), lambda h, t: (rt(t), h))
    assert HEAD_GROUPS == 1
    tile = pltpu.VMEM((tb, wid), F32)
    return pl.pallas_call(
        _drop_operands(body, 9, len(after)), name="hg_bwd", grid=(HEAD_GROUPS, nb),
        in_specs=[col(0), col(1), col(2), col(3), pl.BlockSpec((2, wid), lambda h, t: (0, h)),
                  pl.BlockSpec((1, HEAD_DIM), lambda h, t: (0, 0)), hcol, hcol,
                  pl.BlockSpec((HEADS_PER_STEP, nc, HEAD_DIM, HEAD_DIM), lambda h, t: (h, rt(t), 0, 0))]
                 + [HBM_SPEC] * len(after),
        out_specs=[pl.BlockSpec((tb, N_HG), lambda h, t: (rt(t), 0)), pl.BlockSpec((2, wid), lambda h, t: (0, h)),
                   pl.BlockSpec((1, HEAD_DIM), lambda h, t: (0, 0))],
        out_shape=[jax.ShapeDtypeStruct((T, N_HG), BF16), jax.ShapeDtypeStruct((2, HG_WIDTH), F32),
                   jax.ShapeDtypeStruct((1, HEAD_DIM), F32)],
        scratch_shapes=[pltpu.VMEM((HEADS_PER_STEP, HEAD_DIM, HEAD_DIM), F32), tile, tile, tile, tile, tile, tile],
        compiler_params=_params(("arbitrary", "arbitrary")),
    )(hg, hg, hg, hg, low, gn, o, dog, st, *after)


def _in_bwd(dparts, w_in, x, dx1, g, after=()):
    T = x.shape[0]
    tm = min(512, T)
    widths = [p.shape[1] for p in dparts]
    offs = [sum(widths[:i]) for i in range(len(widths))]
    n = len(dparts)

    def body(*refs):
        d_refs = refs[:n]
        w_ref, x_ref, dx1_ref, g_ref, dx_ref, dgn_ref = refs[n:]

        @pl.when(pl.program_id(0) == 0)
        def _():
            dgn_ref[...] = jnp.zeros_like(dgn_ref)

        dh = None
        for d_ref, off, wd in zip(d_refs, offs, widths):
            part = _mm(d_ref[...], w_ref[off:off + wd, :])
            dh = part if dh is None else dh + part
        xv = x_ref[...]
        r = lax.rsqrt(jnp.mean(xv * xv, axis=-1, keepdims=True) + EPS)
        xh = xv * r
        dgn_ref[...] += jnp.sum(dh * xh, axis=0, keepdims=True)
        w = dh * g_ref[...]
        dx_ref[...] = dx1_ref[...] + r * (w - xh * jnp.mean(w * xh, axis=-1, keepdims=True))

    row = lambda m: pl.BlockSpec((tm, m), lambda i: (i, 0))
    return pl.pallas_call(
        _drop_operands(body, n + 4, len(after)), name="in_bwd", grid=(T // tm,),
        in_specs=[row(wd) for wd in widths] + [_resident(w_in.shape), row(D_MODEL), row(D_MODEL), _full((1, D_MODEL))]
                 + [HBM_SPEC] * len(after),
        out_specs=[row(D_MODEL), _full((1, D_MODEL))],
        out_shape=[jax.ShapeDtypeStruct((T, D_MODEL), F32), jax.ShapeDtypeStruct((1, D_MODEL), F32)],
        compiler_params=_params(("arbitrary",)),
    )(*dparts, w_in, x, dx1, g, *after)


def _wgrad_ffn(h2t, dgate, dup, dx2t, act, tn=256):
    M, T = h2t.shape
    N = dgate.shape[1]

    def body(h_ref, x_ref, dg_ref, du_ref, act_ref, og_ref, ou_ref, od_ref):
        h = h_ref[...]
        og_ref[...] = _mm(h, dg_ref[...]).T.astype(BF16)
        ou_ref[...] = _mm(h, du_ref[...]).T.astype(BF16)
        od_ref[...] = _mm(x_ref[...], act_ref[...]).T.astype(BF16)

    rhs = pl.BlockSpec((T, tn), lambda j: (0, j))
    out_spec = pl.BlockSpec((tn, M), lambda j: (j, 0))
    out = jax.ShapeDtypeStruct((N, M), BF16)
    return pl.pallas_call(
        body, name="wgrad_ffn", grid=(N // tn,),
        in_specs=[_resident((M, T)), _resident((M, T)), rhs, rhs, rhs], out_specs=[out_spec] * 3,
        out_shape=[out, out, out],
        compiler_params=_params(("parallel",)),
    )(h2t, dx2t, dgate, dup, act)


def _wgrad_out_branches(ogt, dya, cvot, dyb, mgt, dx1, after=()):
    M, T = ogt.shape
    N = dya.shape[1]
    c = N // N_DEV
    per = 2
    tn = per * c

    def body(at_ref, da_ref, bt_ref, db_ref, mt_ref, dx_ref, oa_ref, ob_ref, oo_ref):
        ga = _mm(at_ref[...], da_ref[...])
        gb = _mm(bt_ref[...], db_ref[...])
        for s in range(per):
            oa_ref[s] = ga[:, s * c:(s + 1) * c].astype(BF16)
            ob_ref[s] = gb[:, s * c:(s + 1) * c].astype(BF16)
        oo_ref[...] = _mm(mt_ref[...], dx_ref[...]).astype(BF16)

    rhs = pl.BlockSpec((T, tn), lambda j: (0, j))
    owners = pl.BlockSpec((per, M, c), lambda j: (j, 0, 0))
    out = jax.ShapeDtypeStruct((N_DEV, M, c), BF16)
    return pl.pallas_call(
        _drop_operands(body, 6, len(after)), name="wgrad_out_branches", grid=(N // tn,),
        in_specs=[_resident((M, T)), rhs, _resident((M, T)), rhs, _resident(mgt.shape), rhs] + [HBM_SPEC] * len(after),
        out_specs=[owners, owners, pl.BlockSpec((mgt.shape[0], tn), lambda j: (0, j))],
        out_shape=[out, out, jax.ShapeDtypeStruct((mgt.shape[0], dx1.shape[1]), BF16)],
        compiler_params=_params(("parallel",)),
    )(ogt, dya, cvot, dyb, mgt, dx1, *after)


def _wgrad_in(ht, dparts, after=(), riders=()):
    M, T = ht.shape
    tn = 512
    nblk = [p.shape[1] // tn for p in dparts]
    start = [sum(nblk[:i]) for i in range(len(nblk))]
    n = len(dparts)
    steps = sum(nblk)
    nr = len(riders)
    rows = [r[0].shape[0] // steps for r in riders]
    assert all(r[0].shape[0] == rr * steps and rr % 16 == 0 for r, rr in zip(riders, rows))

    def body(a_ref, *refs):
        d_refs = refs[:n]
        rider_in = refs[n:n + 4 * nr]
        o_ref = refs[n + 4 * nr]
        rider_out = refs[n + 4 * nr + 1:]
        j = pl.program_id(0)
        for d_ref, s, nb in zip(d_refs, start, nblk):
            @pl.when((j >= s) & (j < s + nb))
            def _():
                o_ref[...] = _mm(a_ref[...], d_ref[...]).T.astype(BF16)
        for i in range(nr):
            w_ref, p_ref, m_ref, v_ref = rider_in[4 * i:4 * i + 4]
            g = p_ref[0].astype(F32)
            for k in range(1, 4):
                g = g + p_ref[k].astype(F32)
            delta, m_new, v_new = _adamw_math(w_ref[...], g, m_ref[...], v_ref[...])
            for k, val in enumerate((g, delta, m_new, v_new)):
                rider_out[4 * i + k][...] = val

    def piece_spec(s, nb):
        return pl.BlockSpec((T, tn), lambda j: (0, jnp.clip(j - s, 0, nb - 1)))

    rider_specs, rider_out_specs, rider_out_shape, rider_args = [], [], [], []
    for (w, parts, m, v), rr in zip(riders, rows):
        blk = pl.BlockSpec((rr, w.shape[1]), lambda j: (j, 0))
        rider_specs += [blk, pl.BlockSpec((4, rr, w.shape[1]), lambda j: (0, j, 0)), blk, blk]
        rider_out_specs += [blk] * 4
        rider_out_shape += [jax.ShapeDtypeStruct(w.shape, F32)] * 4
        rider_args += [w, parts, m, v]
    outs = pl.pallas_call(
        _drop_operands(body, 1 + n + 4 * nr, len(after)), name="wgrad_in", grid=(steps,),
        in_specs=[_resident((M, T))] + [piece_spec(s, nb) for s, nb in zip(start, nblk)] + rider_specs
                 + [HBM_SPEC] * len(after),
        out_specs=[pl.BlockSpec((tn, M), lambda j: (j, 0))] + rider_out_specs,
        out_shape=[jax.ShapeDtypeStruct((steps * tn, M), BF16)] + rider_out_shape,
        compiler_params=_params(("parallel",)),
    )(ht, *dparts, *rider_args, *after)
    return outs[0], [outs[1 + 4 * i:5 + 4 * i] for i in range(nr)]


def _adamw_math(w, g, m, v):
    m = ADAM_B1 * m + (1.0 - ADAM_B1) * g
    v = ADAM_B2 * v + (1.0 - ADAM_B2) * (g * g)
    m_hat = m / (1.0 - ADAM_B1 ** ADAM_STEP)
    v_hat = v / (1.0 - ADAM_B2 ** ADAM_STEP)
    delta = -ADAM_LR * (m_hat / (jnp.sqrt(v_hat) + ADAM_EPS) + ADAM_WD * w)
    return delta, m, v


def _adamw_sum(name, ws, parts, ms, vs):
    n = len(ws)
    steps = min(_row_steps(w.shape[0]) for w in ws)
    rows = [w.shape[0] // steps for w in ws]

    def body(*refs):
        w_refs, p_refs, m_refs, v_refs = (refs[k * n:(k + 1) * n] for k in range(4))
        out_refs = refs[4 * n:]
        for i in range(n):
            g = p_refs[i][0].astype(F32)
            for k in range(1, 4):
                g = g + p_refs[i][k].astype(F32)
            delta, m_new, v_new = _adamw_math(w_refs[i][...], g, m_refs[i][...], v_refs[i][...])
            for k, val in enumerate((g, delta, m_new, v_new)):
                out_refs[4 * i + k][...] = val

    blk = [pl.BlockSpec((r, w.shape[1]), lambda s: (s, 0)) for r, w in zip(rows, ws)]
    pblk = [pl.BlockSpec((4, r, w.shape[1]), lambda s: (0, s, 0)) for r, w in zip(rows, ws)]
    out_specs, out_shape = [], []
    for b, w in zip(blk, ws):
        out_specs += [b] * 4
        out_shape += [jax.ShapeDtypeStruct(w.shape, F32)] * 4
    flat = pl.pallas_call(
        body, name=name, grid=(steps,),
        in_specs=blk + pblk + blk + blk, out_specs=out_specs, out_shape=out_shape,
        compiler_params=_params(("parallel",)),
    )(*ws, *parts, *ms, *vs)
    return [flat[4 * i:4 * i + 4] for i in range(n)]


_SMALL_SLOTS = (("norm_mix_g", 0, 1, 1024), ("norm_ffn_g", 1, 1, 1024), ("norm_final_g", 2, 1, 1024),
                ("lower_bounds", 3, 2, 512), ("hg_norm_g", 5, 1, 128), ("loss", 6, 1, 128), ("conv_w", 8, 3, 512))
_SMALL_PARAMS = tuple(s for s in _SMALL_SLOTS if s[0] != "loss")
CONV_SHARD = CONV_WIDTH // N_DEV


def _small_pack(small):
    def body(*refs):
        out = refs[-1]
        out[...] = jnp.zeros_like(out)
        for ref, (_, row, rows, lanes) in zip(refs[:-1], _SMALL_SLOTS):
            out[row:row + rows, 0:lanes] = ref[...]

    vmem = pl.BlockSpec(memory_space=pltpu.VMEM)
    return pl.pallas_call(
        body, name="small_pack", in_specs=[vmem] * len(_SMALL_SLOTS), out_specs=vmem,
        out_shape=jax.ShapeDtypeStruct((SMALL_ROWS, 1024), F32),
    )(*[small[name] for name, _, _, _ in _SMALL_SLOTS])


def _small_update(gathered, dev, w, m, v):
    n = len(_SMALL_PARAMS)

    def body(dev_ref, g_ref, *refs):
        w_refs, m_refs, v_refs = refs[:n], refs[n:2 * n], refs[2 * n:3 * n]
        loss_ref, out_refs, sum_scr = refs[3 * n], refs[3 * n + 1:-1], refs[-1]
        total = g_ref[0]
        for k in range(1, N_DEV):
            total = total + g_ref[k]
        sum_scr[...] = total
        loss_ref[...] = sum_scr[6:7, 0:128]
        for p, (name, row, rows, lanes) in enumerate(_SMALL_PARAMS):
            if name == "conv_w":
                g = sum_scr[row:row + rows, 0:CONV_SHARD]
                for s in range(1, N_DEV):
                    g = jnp.where(dev_ref[0] == s, sum_scr[row:row + rows, s * CONV_SHARD:(s + 1) * CONV_SHARD], g)
            else:
                g = sum_scr[row:row + rows, 0:lanes]
            delta, m_new, v_new = _adamw_math(w_refs[p][...], g, m_refs[p][...], v_refs[p][...])
            out_refs[4 * p][...] = g
            out_refs[4 * p + 1][...] = delta
            out_refs[4 * p + 2][...] = m_new
            out_refs[4 * p + 3][...] = v_new

    vmem = pl.BlockSpec(memory_space=pltpu.VMEM)
    outs = [jax.ShapeDtypeStruct((1, 128), F32)]
    for a in w:
        outs += [jax.ShapeDtypeStruct(a.shape, F32)] * 4
    return pl.pallas_call(
        body, name="small_update",
        in_specs=[pl.BlockSpec(memory_space=pltpu.SMEM)] + [vmem] * (1 + 3 * n), out_specs=[vmem] * len(outs),
        out_shape=outs, scratch_shapes=[pltpu.VMEM((SMALL_ROWS, 1024), F32)],
    )(dev, gathered, *w, *m, *v)


def _row_steps(rows):
    for steps in (4, 2):
        if rows % (16 * steps) == 0:
            return steps
    return 1


def _pair_sum(name, by_owner, got, core, after=()):
    n = len(got)

    def body(core_ref, *refs):
        for a_ref, b_ref, o_ref in zip(refs[:n], refs[n:2 * n], refs[2 * n:]):
            o_ref[...] = (a_ref[...].astype(F32) + b_ref[...].astype(F32)).astype(BF16)

    def blk(g):
        return pl.BlockSpec((None,) + g.shape[1:], lambda k, core_ref: (k, 0, 0))

    def mine(g):
        return pl.BlockSpec((None,) + g.shape[1:], lambda k, core_ref: (2 * k + core_ref[0], 0, 0))

    return pl.pallas_call(
        _drop_operands(body, 1 + 2 * n, len(after)), name=name,
        grid_spec=pltpu.PrefetchScalarGridSpec(
            num_scalar_prefetch=1, grid=(4,),
            in_specs=[mine(g) for g in got] + [blk(g) for g in got] + [HBM_SPEC] * len(after),
            out_specs=[blk(g) for g in got]),
        out_shape=[jax.ShapeDtypeStruct(g.shape, BF16) for g in got],
        compiler_params=_params(("parallel",)),
    )(core, *by_owner, *got, *after)


MESH = pl.DeviceIdType.MESH
HBM_SPEC = pl.BlockSpec(memory_space=pl.ANY)


def _handshake(peers):
    barrier = pltpu.get_barrier_semaphore()
    for peer in peers:
        pl.semaphore_signal(barrier, inc=1, device_id=peer, device_id_type=MESH)
    pl.semaphore_wait(barrier, len(peers))


def _comm_call(body, name, operands, out_shape, scratch, collective_id):
    if collective_id is None:
        return pl.pallas_call(body, name=name, in_specs=[HBM_SPEC] * len(operands), out_specs=[HBM_SPEC] * len(out_shape),
                              out_shape=out_shape, scratch_shapes=scratch)(*operands)
    return pl.kernel(body, out_type=out_shape, mesh=plsc.ScalarSubcoreMesh(axis_name="sequencer", num_cores=1),
                     scratch_types=scratch, name=name,
                     compiler_params=pltpu.CompilerParams(collective_id=collective_id))(*operands)


def _all_gather(name, blocks, collective_id=None, after=()):
    n = len(blocks)
    na = len(after)

    def body(*refs):
        x_refs, out_refs = refs[:n], refs[n + na:2 * n + na]
        send_sems, recv_sems, local_sems = refs[2 * n + na:]
        x, y, c = lax.axis_index("x"), lax.axis_index("y"), lax.axis_index("c")
        me, sibling = (x, y, c), (x, y, 1 - c)
        chips = [(1 - x, y), (x, 1 - y), (1 - x, 1 - y)]
        if collective_id is not None:
            _handshake([sibling] + [(*chip, c) for chip in chips])

        def slot(i, px, py, pc):
            return out_refs[i].at[4 * px + 2 * py + pc]

        def copy(i, k, blk, to, src=None):
            return pltpu.make_async_remote_copy(
                src_ref=slot(i, *blk) if src is None else src, dst_ref=slot(i, *blk),
                send_sem=send_sems.at[7 * i + k], recv_sem=recv_sems.at[7 * i + k], device_id=to, device_id_type=MESH)

        mine = [pltpu.make_async_copy(x_refs[i], slot(i, *me), local_sems.at[i]) for i in range(n)]
        for cp in mine:
            cp.start()
        first = []
        for i in range(n):
            first.append(copy(i, 0, me, sibling, src=x_refs[i]))
            first += [copy(i, 1 + j, me, (*chip, c), src=x_refs[i]) for j, chip in enumerate(chips)]
        for cp in first:
            cp.start()
        passed = []
        for i in range(n):
            for j, chip in enumerate(chips):
                copy(i, 1 + j, (*chip, c), me).wait_recv()
                passed.append(copy(i, 4 + j, (*chip, c), sibling))
                passed[-1].start()
        for i in range(n):
            copy(i, 0, sibling, me).wait_recv()
            for j, chip in enumerate(chips):
                copy(i, 4 + j, (*chip, 1 - c), me).wait_recv()
        for cp in first + passed:
            cp.wait_send()
        for cp in mine:
            cp.wait()

    return _comm_call(
        body, name, list(blocks) + list(after), [jax.ShapeDtypeStruct((N_DEV,) + b.shape, b.dtype) for b in blocks],
        [pltpu.SemaphoreType.DMA((7 * n,)), pltpu.SemaphoreType.DMA((7 * n,)), pltpu.SemaphoreType.DMA((n,))],
        collective_id)


def _sibling_swap(name, by_owner, collective_id=None, after=()):
    n = len(by_owner)
    na = len(after)

    def body(*refs):
        x_refs, out_refs = refs[:n], refs[n + na:2 * n + na]
        send_sems, recv_sems = refs[2 * n + na:]
        x, y, c = lax.axis_index("x"), lax.axis_index("y"), lax.axis_index("c")
        if collective_id is not None:
            _handshake([(x, y, 1 - c)])
        copies = []
        for i in range(n):
            for k in range(4):
                copies.append(pltpu.make_async_remote_copy(
                    src_ref=x_refs[i].at[2 * k + 1 - c], dst_ref=out_refs[i].at[k],
                    send_sem=send_sems.at[4 * i + k], recv_sem=recv_sems.at[4 * i + k],
                    device_id=(x, y, 1 - c), device_id_type=MESH))
        for cp in copies:
            cp.start()
        for cp in copies:
            cp.wait()

    return _comm_call(
        body, name, list(by_owner) + list(after),
        [jax.ShapeDtypeStruct((4,) + b.shape[1:], b.dtype) for b in by_owner],
        [pltpu.SemaphoreType.DMA((4 * n,)), pltpu.SemaphoreType.DMA((4 * n,))], collective_id)


def _chip_exchange(name, sums, collective_id=None, after=()):
    n = len(sums)
    na = len(after)

    def body(*refs):
        x_refs, out_refs = refs[:n], refs[n + na:2 * n + na]
        send_sems, recv_sems, local_sems = refs[2 * n + na:]
        x, y, c = lax.axis_index("x"), lax.axis_index("y"), lax.axis_index("c")
        chips = [(1 - x, y), (x, 1 - y), (1 - x, 1 - y)]
        my_chip = 2 * x + y
        if collective_id is not None:
            _handshake([(cx, cy, c) for cx, cy in chips])
        mine = [pltpu.make_async_copy(x_refs[i].at[my_chip], out_refs[i].at[my_chip], local_sems.at[i])
                for i in range(n)]
        for cp in mine:
            cp.start()
        sends = []
        for i in range(n):
            for j, (cx, cy) in enumerate(chips):
                sends.append(pltpu.make_async_remote_copy(
                    src_ref=x_refs[i].at[2 * cx + cy], dst_ref=out_refs[i].at[my_chip],
                    send_sem=send_sems.at[3 * i + j], recv_sem=recv_sems.at[3 * i + j],
                    device_id=(cx, cy, c), device_id_type=MESH))
        for cp in sends:
            cp.start()
        for i in range(n):
            for j, (cx, cy) in enumerate(chips):
                pltpu.make_async_remote_copy(
                    src_ref=x_refs[i].at[my_chip], dst_ref=out_refs[i].at[2 * cx + cy],
                    send_sem=send_sems.at[3 * i + j], recv_sem=recv_sems.at[3 * i + j],
                    device_id=(cx, cy, c), device_id_type=MESH).wait_recv()
        for cp in sends:
            cp.wait_send()
        for cp in mine:
            cp.wait()

    return _comm_call(
        body, name, list(sums) + list(after), [jax.ShapeDtypeStruct(s.shape, s.dtype) for s in sums],
        [pltpu.SemaphoreType.DMA((3 * n,)), pltpu.SemaphoreType.DMA((3 * n,)), pltpu.SemaphoreType.DMA((n,))],
        collective_id)


def _cast_shards(shards):
    n = len(shards)

    def body(*refs):
        for i in range(n):
            refs[n + i][...] = refs[i][...].astype(BF16)

    vmem = pl.BlockSpec(memory_space=pltpu.VMEM)
    return pl.pallas_call(
        body, name="cast_shards", in_specs=[vmem] * n, out_specs=[vmem] * n,
        out_shape=[jax.ShapeDtypeStruct(s.shape, BF16) for s in shards],
        compiler_params=pltpu.CompilerParams(vmem_limit_bytes=VMEM_LIMIT_V7X),
    )(*shards)


BIG = ("w_in", "w_branch_a", "w_branch_b", "w_out", "w_ffn_gate", "w_ffn_up", "w_ffn_down")


def _local_step(x, target, gains, low, conv_w, wg8, reduce):
    g_mix, g_hg, g_ffn, g_fin = gains
    w_in = wg8["w_in"].reshape(N_IN, D_MODEL)
    wg = wg8["w_ffn_gate"].reshape(D_FF, D_MODEL)
    wu = wg8["w_ffn_up"].reshape(D_FF, D_MODEL)
    wa, wb = wg8["w_branch_a"], wg8["w_branch_b"]
    wo = wg8["w_out"].reshape(D_MODEL, D_MODEL)
    wd = wg8["w_ffn_down"].reshape(D_FF, D_MODEL)

    ht, hg, cv, gt, cvo, cvot, o, og, ogt, st = _fwd_in(x, g_mix, w_in, conv_w, low, g_hg)
    x1, mgt = _merge_fwd(og, cvo, gt, x, wa, wb, wo)
    h2t, gate, up, act, loss, d_gfin, dx2, dx2t = _ffn_fwd_loss(x1, g_ffn, wg, wu, wd, target, g_fin)

    dgate, dup, dx1, d_gffn = _ffn_bwd(dx2, x1, gate, up, g_ffn, wg, wu, wd)
    d_wg, d_wu, d_wd = _wgrad_ffn(h2t, dgate, dup, dx2t, act)
    by_owner_ffn = lambda a: a.reshape(N_DEV, D_FF // N_DEV, D_MODEL)
    ffn = dict(w_ffn_down=by_owner_ffn(d_wd), w_ffn_gate=by_owner_ffn(d_wg), w_ffn_up=by_owner_ffn(d_wu))
    dgt, dya, dyb, dog, dcv, d_conv = _merge_bwd(dx1, og, cvo, gt, cv, wa, wb, wo, conv_w)
    sums_ffn, got_ffn = reduce.begin(ffn, sum_after=[dya])
    late = ("w_ffn_gate", "w_ffn_up")
    parts_ffn, updated_ffn = reduce.finish(ffn, sums_ffn, defer=late)
    grad_a, grad_b, grad_o = _wgrad_out_branches(ogt, dya, cvot, dyb, mgt, dx1, after=sums_ffn[:1])
    out = dict(w_out=grad_o.reshape(N_DEV, D_MODEL // N_DEV, D_MODEL), w_branch_a=grad_a, w_branch_b=grad_b)
    dhg, d_low, d_ghg = _hg_bwd(dog, hg, o, st, low, g_hg, after=list(sums_ffn) + [out["w_out"]])
    sums_out, got_out = reduce.begin(out, after=[parts_ffn[0], dhg], sum_after=updated_ffn)
    parts_out, updated_out = reduce.finish(out, sums_out)
    dparts = [dhg, dcv, dgt]
    d_w_in_t, ridden = _wgrad_in(ht, dparts, after=sums_out[:1],
                                 riders=reduce.riders(late, dict(zip(ffn, parts_ffn))))
    reduce.record(late, ridden)
    w_in_grad = dict(w_in=d_w_in_t.reshape(N_DEV, N_IN // N_DEV, D_MODEL))
    sums_in, _ = reduce.begin(w_in_grad, after=parts_out[:1], sum_after=updated_out)
    parts_in, _ = reduce.finish(w_in_grad, sums_in)
    grad_x, d_gmix = _in_bwd(dparts, w_in, x, dx1, g_mix, after=list(parts_out[:1]) + list(sums_in))
    small = dict(norm_mix_g=d_gmix, norm_ffn_g=d_gffn, norm_final_g=d_gfin, lower_bounds=d_low, hg_norm_g=d_ghg,
                 conv_w=d_conv, loss=loss)
    return grad_x, small, parts_in


def _conv_shard_rows(a):
    return jnp.pad(a, ((0, 5), (0, 64)))


def kernel(x, norm_mix_g, w_in, lower_bounds, hg_norm_g, conv_w, w_branch_a, w_branch_b, w_out, norm_ffn_g, w_ffn_gate, w_ffn_up, w_ffn_down, norm_final_g, loss_target, m_norm_mix_g, m_w_in, m_lower_bounds, m_hg_norm_g, m_conv_w, m_w_branch_a, m_w_branch_b, m_w_out, m_norm_ffn_g, m_w_ffn_gate, m_w_ffn_up, m_w_ffn_down, m_norm_final_g, v_norm_mix_g, v_w_in, v_lower_bounds, v_hg_norm_g, v_conv_w, v_w_branch_a, v_w_branch_b, v_w_out, v_norm_ffn_g, v_w_ffn_gate, v_w_ffn_up, v_w_ffn_down, v_norm_final_g):
    cx, cy, cc = lax.axis_index("x"), lax.axis_index("y"), lax.axis_index("c")
    my_dev = 4 * cx + 2 * cy + cc

    def tr(a):
        return a[0].T

    big = dict(w_in=tr(w_in), w_branch_a=w_branch_a[0], w_branch_b=w_branch_b[0], w_out=w_out[0],
               w_ffn_gate=tr(w_ffn_gate), w_ffn_up=tr(w_ffn_up), w_ffn_down=w_ffn_down[0])
    big_m = dict(w_in=tr(m_w_in), w_branch_a=m_w_branch_a[0], w_branch_b=m_w_branch_b[0], w_out=m_w_out[0],
                 w_ffn_gate=tr(m_w_ffn_gate), w_ffn_up=tr(m_w_ffn_up), w_ffn_down=m_w_ffn_down[0])
    big_v = dict(w_in=tr(v_w_in), w_branch_a=v_w_branch_a[0], w_branch_b=v_w_branch_b[0], w_out=v_w_out[0],
                 w_ffn_gate=tr(v_w_ffn_gate), w_ffn_up=tr(v_w_ffn_up), w_ffn_down=v_w_ffn_down[0])
    transposed = ("w_in", "w_ffn_gate", "w_ffn_up")

    shards = dict(zip(BIG, _cast_shards([big[n] for n in BIG])))
    first = _all_gather("gather_w_in", [shards["w_in"], _conv_shard_rows(conv_w[0])])
    ids = iter(range(1, 16))
    mid = _all_gather("gather_mid", [shards[n] for n in BIG[1:4]], collective_id=next(ids), after=first[1:])
    ffn = _all_gather("gather_ffn", [shards[n] for n in BIG[4:]], collective_id=next(ids), after=first[1:])
    wg8 = dict(zip(BIG, [first[0]] + list(mid) + list(ffn)))
    conv_full = first[1][:, :3, :64].transpose(1, 0, 2).reshape(3, CONV_WIDTH)

    core = cc.reshape(1).astype(jnp.int32)
    outs = {}

    class Reduce:
        @staticmethod
        def begin(grads, after=(), sum_after=()):
            names = list(grads)
            by_owner = [grads[n] for n in names]
            got = _sibling_swap("sibling_swap_" + names[0], by_owner, collective_id=next(ids), after=after)
            sums = _pair_sum("pair_sum_" + names[0], by_owner, got, core, after=sum_after)
            return sums, got

        @staticmethod
        def finish(grads, chip_sums, after=(), defer=()):
            names = list(grads)
            parts = _chip_exchange("chip_exchange_" + names[0], chip_sums, collective_id=next(ids), after=after)
            now = [n for n in names if n not in defer]
            updated = _adamw_sum("adamw_" + now[0], [big[n] for n in now],
                                 [p for n, p in zip(names, parts) if n in now],
                                 [big_m[n] for n in now], [big_v[n] for n in now])
            outs.update(zip(now, updated))
            return parts, [outs[n][1] for n in now]

        @staticmethod
        def riders(names, parts):
            return [(big[n], parts[n], big_m[n], big_v[n]) for n in names]

        @staticmethod
        def record(names, updated):
            outs.update(zip(names, updated))

    gains = (norm_mix_g, hg_norm_g, norm_ffn_g, norm_final_g.reshape(1, D_MODEL))
    grad_x, small, last = _local_step(x[0], loss_target[0], gains, lower_bounds, conv_full, wg8, Reduce)

    small_all = _all_gather("gather_small", [_small_pack(small)], collective_id=next(ids), after=last[:1])

    def small_state(a):
        return [a[0], a[1], a[2].reshape(1, D_MODEL), a[3], a[4], a[5][0]]

    upd = _small_update(
        small_all[0], my_dev.reshape(1).astype(jnp.int32),
        small_state((norm_mix_g, norm_ffn_g, norm_final_g, lower_bounds, hg_norm_g, conv_w)),
        small_state((m_norm_mix_g, m_norm_ffn_g, m_norm_final_g, m_lower_bounds, m_hg_norm_g, m_conv_w)),
        small_state((v_norm_mix_g, v_norm_ffn_g, v_norm_final_g, v_lower_bounds, v_hg_norm_g, v_conv_w)))
    loss = upd[0][0, 0]
    small_shape = dict(norm_final_g=(D_MODEL,), conv_w=(1, 3, CONV_SHARD))
    for p, (name, _, _, _) in enumerate(_SMALL_PARAMS):
        outs[name] = [a.reshape(small_shape.get(name, a.shape)) for a in upd[1 + 4 * p:5 + 4 * p]]

    order = ["norm_mix_g", "w_in", "lower_bounds", "hg_norm_g", "conv_w", "w_branch_a", "w_branch_b", "w_out",
             "norm_ffn_g", "w_ffn_gate", "w_ffn_up", "w_ffn_down", "norm_final_g"]
    result = [loss, grad_x[None]]
    for k in range(4):
        for n in order:
            if n in BIG:
                result.append((outs[n][k].T if n in transposed else outs[n][k])[None])
            else:
                result.append(outs[n][k])
    return tuple(result)
```

```python
import jax
import jax.numpy as jnp
from jax import lax
from jax.experimental import pallas as pl
from jax.experimental.pallas import tpu as pltpu
from jax.experimental.pallas import tpu_sc as plsc

F32 = jnp.float32
BF16 = jnp.bfloat16
STASH = jnp.bfloat16

D_MODEL = 1024
HG_WIDTH = 512
HEAD_DIM = 128
N_HEADS = 4
HEADS_PER_STEP = 4
HEAD_GROUPS = N_HEADS // HEADS_PER_STEP
CONV_WIDTH = 512
CONV_K = 3
D_FF = 2816
CHUNK = 32
EPS = 1e-6
Q_SCALE = HEAD_DIM ** -0.5
N_DEV = 8

ADAM_LR = 0.001
ADAM_B1 = 0.9
ADAM_B2 = 0.999
ADAM_EPS = 1e-08
ADAM_WD = 0.01
ADAM_STEP = 10

VMEM_LIMIT_V7X = 56 * 1024 * 1024
VMEM_LIMIT_LARGE_V7X = 62 * 1024 * 1024

SMALL_ROWS = 16


def _params(sem, vmem=VMEM_LIMIT_V7X):
    return pltpu.CompilerParams(dimension_semantics=sem, vmem_limit_bytes=vmem)


def _mm(a, b):
    return jnp.dot(a.astype(BF16), b.astype(BF16), preferred_element_type=F32)


def _mm_nt(a, b):
    return lax.dot_general(a.astype(BF16), b.astype(BF16), (((1,), (1,)), ((), ())), preferred_element_type=F32)


def _mm_tn(a, b):
    return lax.dot_general(a.astype(BF16), b.astype(BF16), (((0,), (0,)), ((), ())), preferred_element_type=F32)


def _sigmoid(x):
    return 0.5 * jnp.tanh(0.5 * x) + 0.5


def _resident(shape):
    nd = len(shape)
    return pl.BlockSpec(shape, lambda *_: (0,) * nd, pipeline_mode=pl.Buffered(1))


def _full(shape):
    nd = len(shape)
    return pl.BlockSpec(shape, lambda *_: (0,) * nd)


def _shard_cols(w_ref):
    return jnp.concatenate([w_ref[s] for s in range(N_DEV)], axis=1)


N_HG = 4 * HG_WIDTH
N_CV = 3 * CONV_WIDTH
N_GT = 2 * D_MODEL
N_IN = N_HG + N_CV + N_GT


def _col(tm, n):
    return pl.BlockSpec((n, tm), lambda i: (0, i))


HALO = 8


def _fwd_in(x, g, w_in_t, conv_w, low, gn):
    T = x.shape[0]
    tm = min(512, T)
    nc = tm // CHUNK

    def body(x_ref, g_ref, w_ref, cw_ref, low_ref, gn_ref, ht_ref, hg_ref, cv_ref, gt_ref, cvo_ref, cvot_ref,
             o_ref, og_ref, ogt_ref, st_ref, tail_scr, s_scr):
        @pl.when(pl.program_id(0) == 0)
        def _():
            tail_scr[...] = jnp.zeros_like(tail_scr)
            s_scr[...] = jnp.zeros_like(s_scr)

        xv = x_ref[...]
        r = lax.rsqrt(jnp.mean(xv * xv, axis=-1, keepdims=True) + EPS)
        hf = xv * r * g_ref[...]
        h = hf.astype(BF16)
        ht_ref[...] = hf.T.astype(BF16)
        hg_ref[...] = _mm_nt(h, w_ref[:N_HG, :])
        cv = _mm_nt(h, w_ref[N_HG:N_HG + N_CV, :])
        cv_ref[...] = cv.astype(STASH)
        gt_ref[...] = _mm_nt(h, w_ref[N_HG + N_CV:, :]).astype(STASH)

        u = cv[:, :CONV_WIDTH] * cv[:, 2 * CONV_WIDTH:]
        row = lax.broadcasted_iota(jnp.int32, u.shape, 0)
        prev1 = tail_scr[HALO - 1:HALO, :]
        prev2 = tail_scr[HALO - 2:HALO - 1, :]
        u1 = jnp.where(row >= 1, pltpu.roll(u, 1, 0), prev1)
        u2 = jnp.where(row >= 2, pltpu.roll(u, 2, 0), jnp.where(row == 1, prev1, prev2))
        y = cw_ref[0:1, :] * u2 + cw_ref[1:2, :] * u1 + cw_ref[2:3, :] * u
        out = cv[:, CONV_WIDTH:2 * CONV_WIDTH] * y
        cvo_ref[...] = out.astype(BF16)
        cvot_ref[...] = out.T.astype(BF16)
        tail_scr[...] = u[tm - HALO:, :]

        _hg_fwd_tile(hg_ref, low_ref, gn_ref, o_ref, og_ref, ogt_ref, st_ref, s_scr, tm)

    row = lambda n: pl.BlockSpec((tm, n), lambda i: (i, 0))
    return pl.pallas_call(
        body, name="fwd_in", grid=(T // tm,),
        in_specs=[row(D_MODEL), _full((1, D_MODEL)), _resident(w_in_t.shape), _full((CONV_K, CONV_WIDTH)),
                  _full((2, HG_WIDTH)), _full((1, HEAD_DIM))],
        out_specs=[_col(tm, D_MODEL), row(N_HG), row(N_CV), row(N_GT), row(CONV_WIDTH), _col(tm, CONV_WIDTH),
                   row(HG_WIDTH), row(HG_WIDTH), _col(tm, HG_WIDTH),
                   pl.BlockSpec((N_HEADS, nc, HEAD_DIM, HEAD_DIM), lambda i: (0, i, 0, 0))],
        out_shape=[jax.ShapeDtypeStruct((D_MODEL, T), BF16), jax.ShapeDtypeStruct((T, N_HG), F32),
                   jax.ShapeDtypeStruct((T, N_CV), STASH), jax.ShapeDtypeStruct((T, N_GT), STASH),
                   jax.ShapeDtypeStruct((T, CONV_WIDTH), BF16), jax.ShapeDtypeStruct((CONV_WIDTH, T), BF16),
                   jax.ShapeDtypeStruct((T, HG_WIDTH), F32), jax.ShapeDtypeStruct((T, HG_WIDTH), BF16),
                   jax.ShapeDtypeStruct((HG_WIDTH, T), BF16),
                   jax.ShapeDtypeStruct((N_HEADS, T // CHUNK, HEAD_DIM, HEAD_DIM), F32)],
        scratch_shapes=[pltpu.VMEM((HALO, CONV_WIDTH), F32), pltpu.VMEM((N_HEADS, HEAD_DIM, HEAD_DIM), F32)],
        compiler_params=_params(("arbitrary",), vmem=VMEM_LIMIT_LARGE_V7X),
    )(x, g, w_in_t, conv_w, low, gn)


def _chunk_pos(shape):
    return lax.broadcasted_iota(jnp.int32, shape, 0) & (CHUNK - 1)


def _chunk_cumsum(x, pos):
    s = 1
    while s < CHUNK:
        x = x + jnp.where(pos >= s, pltpu.roll(x, s, 0), 0.0)
        s *= 2
    return x


def _chunk_rev_cumsum(x, pos):
    n = x.shape[0]
    s = 1
    while s < CHUNK:
        x = x + jnp.where(pos + s < CHUNK, pltpu.roll(x, n - s, 0), 0.0)
        s *= 2
    return x


def _lower_bound(low_ref):
    l0 = low_ref[0:1, :]
    l1 = low_ref[1:2, :]
    m = jnp.maximum(l0, l1)
    e0 = jnp.exp(l0 - m)
    e1 = jnp.exp(l1 - m)
    return e0 / (e0 + e1), e1 / (e0 + e1)


def _hg_gates(qr, fr, lb, pos, tb):
    sq = _sigmoid(qr)
    q = qr * sq * Q_SCALE
    sg = _sigmoid(fr)
    f = lb + (1.0 - lb) * sg
    k = 1.0 - f
    b = _chunk_cumsum(jnp.log(f), pos)
    b3 = b.reshape(tb // CHUNK, CHUNK, HEAD_DIM)
    anc = b3[:, CHUNK // 2 - 1:CHUNK // 2, :]
    last = b3[:, CHUNK - 1:CHUNK, :]
    d3 = b3 - anc
    e_qa3 = jnp.exp(d3)
    e_ka3 = jnp.exp(-d3)
    e_b3 = e_qa3 * jnp.exp(anc)
    e_ko3 = e_ka3 * jnp.exp(last - anc)
    dec = jnp.exp(last)
    flat = lambda a: a.reshape(tb, HEAD_DIM)
    return sq, q, sg, f, k, flat(e_qa3), flat(e_ka3), flat(e_b3), flat(e_ko3), dec


def _intra_mask(sb):
    r = lax.broadcasted_iota(jnp.int32, (sb, sb), 0)
    c = lax.broadcasted_iota(jnp.int32, (sb, sb), 1)
    return ((r // CHUNK) == (c // CHUNK)) & (c <= r)


def _hg_fwd_tile(hg_ref, low_ref, gn_ref, o_ref, og_ref, ogt_ref, st_ref, s_scr, tb):
    sb = min(256, tb)
    nc = tb // CHUNK
    q_ref, f_ref, i_ref, g_ref = (hg_ref.at[:, p * HG_WIDTH:(p + 1) * HG_WIDTH] for p in range(4))
    pos = _chunk_pos((tb, HEAD_DIM))
    mask = _intra_mask(sb)
    lanes = [slice(hh * HEAD_DIM, (hh + 1) * HEAD_DIM) for hh in range(N_HEADS)]
    qi, ko, vb, dec, st = [], [], [], [], []
    for hh, ln in enumerate(lanes):
        lb, _ = _lower_bound(low_ref.at[:, ln])
        _, q, _, _, k, e_qa, e_ka, e_b, e_ko, dec_h = _hg_gates(q_ref[:, ln], f_ref[:, ln], lb, pos, tb)
        qh = (q * e_qa).astype(BF16)
        kh = (k * e_ka).astype(BF16)
        qi.append((q * e_b).astype(BF16))
        ko.append((k * e_ko).astype(BF16))
        vb.append(i_ref[:, ln].astype(BF16))
        dec.append(dec_h)
        st.append(s_scr[hh])
        for s in range(tb // sb):
            sl = slice(s * sb, (s + 1) * sb)
            p = jnp.where(mask, _mm_nt(qh[sl], kh[sl]), 0.0)
            o_ref[sl, ln] = _mm(p, vb[hh][sl])
    for c in range(nc):
        sl = slice(c * CHUNK, (c + 1) * CHUNK)
        for hh, ln in enumerate(lanes):
            st_ref[hh, c] = st[hh]
            o_ref[sl, ln] = o_ref[sl, ln] + _mm_nt(qi[hh][sl], st[hh])
            st[hh] = dec[hh][c] * st[hh] + _mm_tn(vb[hh][sl], ko[hh][sl])
    for hh, ln in enumerate(lanes):
        s_scr[hh] = st[hh]
        o = o_ref[:, ln]
        r = lax.rsqrt(jnp.mean(o * o, axis=-1, keepdims=True) + EPS)
        gr = g_ref[:, ln]
        og = (o * r * gn_ref[...]) * (gr * _sigmoid(gr))
        og_ref[:, ln] = og.astype(BF16)
        ogt_ref[ln, :] = og.T.astype(BF16)


def _merge_fwd(og, cvo, gt, x, wa, wb, wo):
    T = x.shape[0]
    tm = min(1024, T)

    def body(og_ref, cvo_ref, gt_ref, x_ref, wa_ref, wb_ref, wo_ref, x1_ref, mgt_ref):
        ya = jnp.dot(og_ref[...], _shard_cols(wa_ref), preferred_element_type=F32)
        yb = jnp.dot(cvo_ref[...], _shard_cols(wb_ref), preferred_element_type=F32)
        m = (_sigmoid(gt_ref[:, :D_MODEL].astype(F32)) * ya
             + _sigmoid(gt_ref[:, D_MODEL:].astype(F32)) * yb)
        mgt_ref[...] = m.T.astype(BF16)
        x1_ref[...] = x_ref[...] + jnp.dot(m.astype(BF16), wo_ref[...], preferred_element_type=F32)

    row = lambda n: pl.BlockSpec((tm, n), lambda i: (i, 0))
    return pl.pallas_call(
        body, name="merge_fwd", grid=(T // tm,),
        in_specs=[row(HG_WIDTH), row(CONV_WIDTH), row(2 * D_MODEL), row(D_MODEL),
                  _resident(wa.shape), _resident(wb.shape), _resident(wo.shape)],
        out_specs=[row(D_MODEL), _col(tm, D_MODEL)],
        out_shape=[jax.ShapeDtypeStruct((T, D_MODEL), F32), jax.ShapeDtypeStruct((D_MODEL, T), BF16)],
        compiler_params=_params(("parallel",)),
    )(og, cvo, gt, x, wa, wb, wo)


def _ffn_fwd_loss(x1, g, wg, wu, wd, target, g_fin):
    T = x1.shape[0]
    tm = min(512, T)

    def body(x_ref, g_ref, wg_ref, wu_ref, wd_ref, t_ref, gf_ref,
             ht_ref, gate_ref, up_ref, act_ref, loss_ref, dgf_ref, dx2_ref, dx2t_ref):
        @pl.when(pl.program_id(0) == 0)
        def _():
            loss_ref[...] = jnp.zeros_like(loss_ref)
            dgf_ref[...] = jnp.zeros_like(dgf_ref)

        xv = x_ref[...]
        r = lax.rsqrt(jnp.mean(xv * xv, axis=-1, keepdims=True) + EPS)
        hf = xv * r * g_ref[...]
        h = hf.astype(BF16)
        ht_ref[...] = hf.T.astype(BF16)
        gate = _mm_nt(h, wg_ref[...])
        up = _mm_nt(h, wu_ref[...])
        gate_ref[...] = gate.astype(STASH)
        up_ref[...] = up.astype(STASH)
        act = (gate * _sigmoid(gate) * up).astype(BF16)
        act_ref[...] = act
        x2 = xv + jnp.dot(act, wd_ref[...], preferred_element_type=F32)

        gv = gf_ref[...]
        r2 = lax.rsqrt(jnp.mean(x2 * x2, axis=-1, keepdims=True) + EPS)
        xh = x2 * r2
        err = xh * gv - t_ref[...]
        loss_ref[...] += 0.5 * jnp.sum(jnp.mean(err * err, axis=-1, keepdims=True), axis=0, keepdims=True)
        dy = err * (1.0 / D_MODEL)
        dgf_ref[...] += jnp.sum(dy * xh, axis=0, keepdims=True)
        w = dy * gv
        dx2 = r2 * (w - xh * jnp.mean(w * xh, axis=-1, keepdims=True))
        dx2_ref[...] = dx2
        dx2t_ref[...] = dx2.T.astype(BF16)

    row = lambda n: pl.BlockSpec((tm, n), lambda i: (i, 0))
    return pl.pallas_call(
        body, name="ffn_fwd_loss", grid=(T // tm,),
        in_specs=[row(D_MODEL), _full((1, D_MODEL)), _resident(wg.shape), _resident(wu.shape), _resident(wd.shape),
                  row(D_MODEL), _full((1, D_MODEL))],
        out_specs=[_col(tm, D_MODEL), row(D_FF), row(D_FF), row(D_FF), _full((1, 128)), _full((1, D_MODEL)),
                   row(D_MODEL), _col(tm, D_MODEL)],
        out_shape=[jax.ShapeDtypeStruct((D_MODEL, T), BF16), jax.ShapeDtypeStruct((T, D_FF), STASH),
                   jax.ShapeDtypeStruct((T, D_FF), STASH), jax.ShapeDtypeStruct((T, D_FF), BF16),
                   jax.ShapeDtypeStruct((1, 128), F32), jax.ShapeDtypeStruct((1, D_MODEL), F32),
                   jax.ShapeDtypeStruct((T, D_MODEL), F32), jax.ShapeDtypeStruct((D_MODEL, T), BF16)],
        compiler_params=_params(("arbitrary",), vmem=VMEM_LIMIT_LARGE_V7X),
    )(x1, g, wg, wu, wd, target, g_fin)


def _ffn_bwd(dx2, x1, gate, up, g, wg, wu, wd):
    T = x1.shape[0]
    tm = min(512, T)

    def body(dx2_ref, x_ref, gate_ref, up_ref, g_ref, wg_ref, wu_ref, wd_ref, dgate_ref, dup_ref, dx1_ref, dgn_ref):
        @pl.when(pl.program_id(0) == 0)
        def _():
            dgn_ref[...] = jnp.zeros_like(dgn_ref)

        dx2 = dx2_ref[...]
        dact = _mm_nt(dx2, wd_ref[...])
        gate = gate_ref[...].astype(F32)
        s = _sigmoid(gate)
        dgate = (dact * up_ref[...].astype(F32) * (s * (1.0 + gate * (1.0 - s)))).astype(BF16)
        dup = (dact * (gate * s)).astype(BF16)
        dgate_ref[...] = dgate
        dup_ref[...] = dup
        dh = _mm(dgate, wg_ref[...]) + _mm(dup, wu_ref[...])
        xv = x_ref[...]
        r = lax.rsqrt(jnp.mean(xv * xv, axis=-1, keepdims=True) + EPS)
        xh = xv * r
        dgn_ref[...] += jnp.sum(dh * xh, axis=0, keepdims=True)
        w = dh * g_ref[...]
        dx1_ref[...] = dx2 + r * (w - xh * jnp.mean(w * xh, axis=-1, keepdims=True))

    row = lambda n: pl.BlockSpec((tm, n), lambda i: (i, 0))
    return pl.pallas_call(
        body, name="ffn_bwd", grid=(T // tm,),
        in_specs=[row(D_MODEL), row(D_MODEL), row(D_FF), row(D_FF), _full((1, D_MODEL)),
                  _resident(wg.shape), _resident(wu.shape), _resident(wd.shape)],
        out_specs=[row(D_FF), row(D_FF), row(D_MODEL), _full((1, D_MODEL))],
        out_shape=[jax.ShapeDtypeStruct((T, D_FF), BF16), jax.ShapeDtypeStruct((T, D_FF), BF16),
                   jax.ShapeDtypeStruct((T, D_MODEL), F32), jax.ShapeDtypeStruct((1, D_MODEL), F32)],
        compiler_params=_params(("arbitrary",), vmem=VMEM_LIMIT_LARGE_V7X),
    )(dx2, x1, gate, up, g, wg, wu, wd)


def _merge_bwd(dx1, og, cvo, gt, cv, wa, wb, wo, conv_w):
    T = dx1.shape[0]
    tm = min(512, T)
    nt = T // tm

    def body(dx_ref, og_ref, cvo_ref, gt_ref, cv_ref, halo_ref, wa_ref, wb_ref, wo_ref, cw_ref,
             dgt_ref, dya_ref, dyb_ref, dog_ref, dcv_ref, dcw_ref, prev_u, next_dy):
        step = pl.program_id(0)

        @pl.when(step == 0)
        def _():
            next_dy[...] = jnp.zeros_like(next_dy)
            dcw_ref[...] = jnp.zeros_like(dcw_ref)

        dm = _mm_nt(dx_ref[...], wo_ref[...])
        wa = _shard_cols(wa_ref)
        wb = _shard_cols(wb_ref)
        ya = jnp.dot(og_ref[...], wa, preferred_element_type=F32)
        yb = jnp.dot(cvo_ref[...], wb, preferred_element_type=F32)
        sa = _sigmoid(gt_ref[:, :D_MODEL].astype(F32))
        sb = _sigmoid(gt_ref[:, D_MODEL:].astype(F32))
        dgt_ref[:, :D_MODEL] = (dm * ya * (sa * (1.0 - sa))).astype(BF16)
        dgt_ref[:, D_MODEL:] = (dm * yb * (sb * (1.0 - sb))).astype(BF16)
        dya = (dm * sa).astype(BF16)
        dyb = (dm * sb).astype(BF16)
        dya_ref[...] = dya
        dyb_ref[...] = dyb
        dog_ref[...] = _mm_nt(dya, wa)
        dcvo = _mm_nt(dyb, wb)

        cvt = cv_ref[...].astype(F32)
        c, bg, xb = cvt[:, :CONV_WIDTH], cvt[:, CONV_WIDTH:2 * CONV_WIDTH], cvt[:, 2 * CONV_WIDTH:]
        halo = halo_ref[...].astype(F32)
        first_tile = step == nt - 1
        prev_u[...] = jnp.where(first_tile, 0.0, halo[:, :CONV_WIDTH] * halo[:, 2 * CONV_WIDTH:])
        u = c * xb
        row = lax.broadcasted_iota(jnp.int32, u.shape, 0)
        p1 = prev_u[HALO - 1:HALO, :]
        p2 = prev_u[HALO - 2:HALO - 1, :]
        u1 = jnp.where(row >= 1, pltpu.roll(u, 1, 0), p1)
        u2 = jnp.where(row >= 2, pltpu.roll(u, 2, 0), jnp.where(row == 1, p1, p2))
        w0, w1, w2 = cw_ref[0:1, :], cw_ref[1:2, :], cw_ref[2:3, :]
        y = w0 * u2 + w1 * u1 + w2 * u
        dcv_ref[:, CONV_WIDTH:2 * CONV_WIDTH] = (dcvo * y).astype(BF16)
        dy = dcvo * bg
        dcw_ref[0:1, :] += jnp.sum(dy * u2, axis=0, keepdims=True)
        dcw_ref[1:2, :] += jnp.sum(dy * u1, axis=0, keepdims=True)
        dcw_ref[2:3, :] += jnp.sum(dy * u, axis=0, keepdims=True)
        n1 = next_dy[0:1, :]
        n2 = next_dy[1:2, :]
        dy1 = jnp.where(row < tm - 1, pltpu.roll(dy, tm - 1, 0), n1)
        dy2 = jnp.where(row < tm - 2, pltpu.roll(dy, tm - 2, 0), jnp.where(row == tm - 2, n1, n2))
        du = w2 * dy + w1 * dy1 + w0 * dy2
        dcv_ref[:, :CONV_WIDTH] = (du * xb).astype(BF16)
        dcv_ref[:, 2 * CONV_WIDTH:] = (du * c).astype(BF16)
        next_dy[...] = dy[:HALO, :]

    rt = lambda i: nt - 1 - i
    row = lambda n: pl.BlockSpec((tm, n), lambda i: (rt(i), 0))
    halo = pl.BlockSpec((HALO, N_CV), lambda i: (jnp.maximum(rt(i) * (tm // HALO) - 1, 0), 0))
    return pl.pallas_call(
        body, name="merge_bwd", grid=(nt,),
        in_specs=[row(D_MODEL), row(HG_WIDTH), row(CONV_WIDTH), row(2 * D_MODEL), row(N_CV), halo,
                  _resident(wa.shape), _resident(wb.shape), _resident(wo.shape), _full((CONV_K, CONV_WIDTH))],
        out_specs=[row(2 * D_MODEL), row(D_MODEL), row(D_MODEL), row(HG_WIDTH), row(N_CV),
                   _full((CONV_K, CONV_WIDTH))],
        out_shape=[jax.ShapeDtypeStruct((T, 2 * D_MODEL), BF16), jax.ShapeDtypeStruct((T, D_MODEL), BF16),
                   jax.ShapeDtypeStruct((T, D_MODEL), BF16), jax.ShapeDtypeStruct((T, HG_WIDTH), F32),
                   jax.ShapeDtypeStruct((T, N_CV), BF16), jax.ShapeDtypeStruct((CONV_K, CONV_WIDTH), F32)],
        scratch_shapes=[pltpu.VMEM((HALO, CONV_WIDTH), F32), pltpu.VMEM((HALO, CONV_WIDTH), F32)],
        compiler_params=_params(("arbitrary",)),
    )(dx1, og, cvo, gt, cv, cv, wa, wb, wo, conv_w)


def _drop_operands(body, first, count):
    def wrapped(*refs):
        return body(*refs[:first], *refs[first + count:])
    return wrapped


def _hg_bwd(dog, hg, o, st, low, gn, after=()):
    T = hg.shape[0]
    tb = min(512, T)
    sb = min(256, tb)
    nb = T // tb
    nc = tb // CHUNK
    wid = HEADS_PER_STEP * HEAD_DIM

    def body(q_ref, f_ref, i_ref, g_ref, low_ref, gn_ref, o_ref, dog_ref, st_ref,
             dhg_ref, dlow_ref, dgn_ref,
             ds_scr, dqi_scr, dko_scr, dv_scr, dd_scr, dqh_scr, dkh_scr):
        h = pl.program_id(0)
        t = pl.program_id(1)
        dq_ref, df_ref, di_ref, dg_ref = (dhg_ref.at[:, p * HG_WIDTH:(p + 1) * HG_WIDTH] for p in range(4))

        @pl.when(t == 0)
        def _():
            ds_scr[...] = jnp.zeros_like(ds_scr)
            dlow_ref[...] = jnp.zeros_like(dlow_ref)

        @pl.when((t == 0) & (h == 0))
        def _():
            dgn_ref[...] = jnp.zeros_like(dgn_ref)

        pos = _chunk_pos((tb, HEAD_DIM))
        mask = _intra_mask(sb)
        gnv = gn_ref[...]
        lanes = [slice(hh * HEAD_DIM, (hh + 1) * HEAD_DIM) for hh in range(HEADS_PER_STEP)]
        heads = []
        for hh, ln in enumerate(lanes):
            lb, lb1 = _lower_bound(low_ref.at[:, ln])
            qr = q_ref[:, ln]
            sq, q, sg, f, k, e_qa, e_ka, e_b, e_ko, dec = _hg_gates(qr, f_ref[:, ln], lb, pos, tb)

            gr = g_ref[:, ln]
            o = o_ref[:, ln]
            dog_v = dog_ref[:, ln]
            sgr = _sigmoid(gr)
            r = lax.rsqrt(jnp.mean(o * o, axis=-1, keepdims=True) + EPS)
            oh = o * r
            dg_ref[:, ln] = (dog_v * (oh * gnv) * (sgr * (1.0 + gr * (1.0 - sgr)))).astype(BF16)
            don = dog_v * (gr * sgr)
            dgn_ref[...] += jnp.sum(don * oh, axis=0, keepdims=True)
            w = don * gnv
            do = (r * (w - oh * jnp.mean(w * oh, axis=-1, keepdims=True))).astype(BF16)

            qh = (q * e_qa).astype(BF16)
            kh = (k * e_ka).astype(BF16)
            qi = (q * e_b).astype(BF16)
            ko = (k * e_ko).astype(BF16)
            vb = i_ref[:, ln].astype(BF16)

            for s in range(tb // sb):
                sl = slice(s * sb, (s + 1) * sb)
                p = jnp.where(mask, _mm_nt(qh[sl], kh[sl]), 0.0).astype(BF16)
                dp = jnp.where(mask, _mm_nt(do[sl], vb[sl]), 0.0).astype(BF16)
                dv_scr[sl, ln] = _mm_tn(p, do[sl])
                dqh_scr[sl, ln] = _mm(dp, kh[sl])
                dkh_scr[sl, ln] = _mm_tn(dp, qh[sl])
            heads.append(dict(lb=lb, lb1=lb1, qr=qr, sq=sq, q=q, sg=sg, f=f, k=k, e_qa=e_qa, e_ka=e_ka, e_b=e_b,
                              e_ko=e_ko, dec=dec, do=do, qi=qi, ko=ko, vb=vb, ds=ds_scr[hh]))

        for c in reversed(range(nc)):
            sl = slice(c * CHUNK, (c + 1) * CHUNK)
            for hh, ln in enumerate(lanes):
                hd = heads[hh]
                ds = hd["ds"]
                st_c = st_ref[hh, c]
                dqi_scr[sl, ln] = _mm(hd["do"][sl], st_c)
                dko_scr[sl, ln] = _mm(hd["vb"][sl], ds)
                dv_scr[sl, ln] = dv_scr[sl, ln] + _mm_nt(hd["ko"][sl], ds)
                dec_c = hd["dec"][c]
                dd_scr[sl, ln] = jnp.broadcast_to(dec_c * jnp.sum(ds * st_c, axis=0, keepdims=True),
                                                  (CHUNK, HEAD_DIM))
                hd["ds"] = dec_c * ds + _mm_tn(hd["do"][sl], hd["qi"][sl])

        for hh, ln in enumerate(lanes):
            hd = heads[hh]
            ds_scr[hh] = hd["ds"]
            q, k, lb = hd["q"], hd["k"], hd["lb"]
            dko_e = dko_scr[:, ln] * hd["e_ko"]
            dq = dqh_scr[:, ln] * hd["e_qa"] + dqi_scr[:, ln] * hd["e_b"]
            dk = dkh_scr[:, ln] * hd["e_ka"] + dko_e
            kd3 = (k * dko_e).reshape(nc, CHUNK, HEAD_DIM)
            last = jnp.broadcast_to(jnp.sum(kd3, axis=1, keepdims=True), kd3.shape).reshape(tb, HEAD_DIM)
            db = q * dq - k * dk + jnp.where(pos == CHUNK - 1, dd_scr[:, ln] + last, 0.0)
            dlg = _chunk_rev_cumsum(db, pos)
            dfv = dlg / hd["f"] - dk
            s_low = jnp.sum(dfv * (1.0 - hd["sg"]), axis=0, keepdims=True)
            dlow_ref[0:1, ln] += s_low * lb * (1.0 - lb)
            dlow_ref[1:2, ln] += -s_low * lb * hd["lb1"]
            df_ref[:, ln] = (dfv * (1.0 - lb) * hd["sg"] * (1.0 - hd["sg"])).astype(BF16)
            dq_ref[:, ln] = (dq * Q_SCALE * (hd["sq"] * (1.0 + hd["qr"] * (1.0 - hd["sq"])))).astype(BF16)
            di_ref[:, ln] = dv_scr[:, ln].astype(BF16)

    rt = lambda t: nb - 1 - t
    col = lambda p: pl.BlockSpec((tb, wid), lambda h, t: (rt(t), p * HEAD_GROUPS + h))
    hcol = pl.BlockSpec((tb, wid), lambda h, t: (rt(t), h))
    assert HEAD_GROUPS == 1
    tile = pltpu.VMEM((tb, wid), F32)
    return pl.pallas_call(
        _drop_operands(body, 9, len(after)), name="hg_bwd", grid=(HEAD_GROUPS, nb),
        in_specs=[col(0), col(1), col(2), col(3), pl.BlockSpec((2, wid), lambda h, t: (0, h)),
                  pl.BlockSpec((1, HEAD_DIM), lambda h, t: (0, 0)), hcol, hcol,
                  pl.BlockSpec((HEADS_PER_STEP, nc, HEAD_DIM, HEAD_DIM), lambda h, t: (h, rt(t), 0, 0))]
                 + [HBM_SPEC] * len(after),
        out_specs=[pl.BlockSpec((tb, N_HG), lambda h, t: (rt(t), 0)), pl.BlockSpec((2, wid), lambda h, t: (0, h)),
                   pl.BlockSpec((1, HEAD_DIM), lambda h, t: (0, 0))],
        out_shape=[jax.ShapeDtypeStruct((T, N_HG), BF16), jax.ShapeDtypeStruct((2, HG_WIDTH), F32),
                   jax.ShapeDtypeStruct((1, HEAD_DIM), F32)],
        scratch_shapes=[pltpu.VMEM((HEADS_PER_STEP, HEAD_DIM, HEAD_DIM), F32), tile, tile, tile, tile, tile, tile],
        compiler_params=_params(("arbitrary", "arbitrary")),
    )(hg, hg, hg, hg, low, gn, o, dog, st, *after)


def _in_bwd(dparts, w_in, x, dx1, g, after=()):
    T = x.shape[0]
    tm = min(512, T)
    widths = [p.shape[1] for p in dparts]
    offs = [sum(widths[:i]) for i in range(len(widths))]
    n = len(dparts)

    def body(*refs):
        d_refs = refs[:n]
        w_ref, x_ref, dx1_ref, g_ref, dx_ref, dgn_ref = refs[n:]

        @pl.when(pl.program_id(0) == 0)
        def _():
            dgn_ref[...] = jnp.zeros_like(dgn_ref)

        dh = None
        for d_ref, off, wd in zip(d_refs, offs, widths):
            part = _mm(d_ref[...], w_ref[off:off + wd, :])
            dh = part if dh is None else dh + part
        xv = x_ref[...]
        r = lax.rsqrt(jnp.mean(xv * xv, axis=-1, keepdims=True) + EPS)
        xh = xv * r
        dgn_ref[...] += jnp.sum(dh * xh, axis=0, keepdims=True)
        w = dh * g_ref[...]
        dx_ref[...] = dx1_ref[...] + r * (w - xh * jnp.mean(w * xh, axis=-1, keepdims=True))

    row = lambda m: pl.BlockSpec((tm, m), lambda i: (i, 0))
    return pl.pallas_call(
        _drop_operands(body, n + 4, len(after)), name="in_bwd", grid=(T // tm,),
        in_specs=[row(wd) for wd in widths] + [_resident(w_in.shape), row(D_MODEL), row(D_MODEL), _full((1, D_MODEL))]
                 + [HBM_SPEC] * len(after),
        out_specs=[row(D_MODEL), _full((1, D_MODEL))],
        out_shape=[jax.ShapeDtypeStruct((T, D_MODEL), F32), jax.ShapeDtypeStruct((1, D_MODEL), F32)],
        compiler_params=_params(("arbitrary",)),
    )(*dparts, w_in, x, dx1, g, *after)


def _wgrad_ffn(h2t, dgate, dup, dx2t, act, tn=256):
    M, T = h2t.shape
    N = dgate.shape[1]

    def body(h_ref, x_ref, dg_ref, du_ref, act_ref, og_ref, ou_ref, od_ref):
        h = h_ref[...]
        og_ref[...] = _mm(h, dg_ref[...]).T.astype(BF16)
        ou_ref[...] = _mm(h, du_ref[...]).T.astype(BF16)
        od_ref[...] = _mm(x_ref[...], act_ref[...]).T.astype(BF16)

    rhs = pl.BlockSpec((T, tn), lambda j: (0, j))
    out_spec = pl.BlockSpec((tn, M), lambda j: (j, 0))
    out = jax.ShapeDtypeStruct((N, M), BF16)
    return pl.pallas_call(
        body, name="wgrad_ffn", grid=(N // tn,),
        in_specs=[_resident((M, T)), _resident((M, T)), rhs, rhs, rhs], out_specs=[out_spec] * 3,
        out_shape=[out, out, out],
        compiler_params=_params(("parallel",)),
    )(h2t, dx2t, dgate, dup, act)


def _wgrad_out_branches(ogt, dya, cvot, dyb, mgt, dx1, after=()):
    M, T = ogt.shape
    N = dya.shape[1]
    c = N // N_DEV
    per = 2
    tn = per * c

    def body(at_ref, da_ref, bt_ref, db_ref, mt_ref, dx_ref, oa_ref, ob_ref, oo_ref):
        ga = _mm(at_ref[...], da_ref[...])
        gb = _mm(bt_ref[...], db_ref[...])
        for s in range(per):
            oa_ref[s] = ga[:, s * c:(s + 1) * c].astype(BF16)
            ob_ref[s] = gb[:, s * c:(s + 1) * c].astype(BF16)
        oo_ref[...] = _mm(mt_ref[...], dx_ref[...]).astype(BF16)

    rhs = pl.BlockSpec((T, tn), lambda j: (0, j))
    owners = pl.BlockSpec((per, M, c), lambda j: (j, 0, 0))
    out = jax.ShapeDtypeStruct((N_DEV, M, c), BF16)
    return pl.pallas_call(
        _drop_operands(body, 6, len(after)), name="wgrad_out_branches", grid=(N // tn,),
        in_specs=[_resident((M, T)), rhs, _resident((M, T)), rhs, _resident(mgt.shape), rhs] + [HBM_SPEC] * len(after),
        out_specs=[owners, owners, pl.BlockSpec((mgt.shape[0], tn), lambda j: (0, j))],
        out_shape=[out, out, jax.ShapeDtypeStruct((mgt.shape[0], dx1.shape[1]), BF16)],
        compiler_params=_params(("parallel",)),
    )(ogt, dya, cvot, dyb, mgt, dx1, *after)


def _wgrad_in(ht, dparts, after=(), riders=()):
    M, T = ht.shape
    tn = 512
    nblk = [p.shape[1] // tn for p in dparts]
    start = [sum(nblk[:i]) for i in range(len(nblk))]
    n = len(dparts)
    steps = sum(nblk)
    nr = len(riders)
    rows = [r[0].shape[0] // steps for r in riders]
    assert all(r[0].shape[0] == rr * steps and rr % 16 == 0 for r, rr in zip(riders, rows))

    def body(a_ref, *refs):
        d_refs = refs[:n]
        rider_in = refs[n:n + 4 * nr]
        o_ref = refs[n + 4 * nr]
        rider_out = refs[n + 4 * nr + 1:]
        j = pl.program_id(0)
        for d_ref, s, nb in zip(d_refs, start, nblk):
            @pl.when((j >= s) & (j < s + nb))
            def _():
                o_ref[...] = _mm(a_ref[...], d_ref[...]).T.astype(BF16)
        for i in range(nr):
            w_ref, p_ref, m_ref, v_ref = rider_in[4 * i:4 * i + 4]
            g = p_ref[0].astype(F32)
            for k in range(1, 4):
                g = g + p_ref[k].astype(F32)
            delta, m_new, v_new = _adamw_math(w_ref[...], g, m_ref[...], v_ref[...])
            for k, val in enumerate((g, delta, m_new, v_new)):
                rider_out[4 * i + k][...] = val

    def piece_spec(s, nb):
        return pl.BlockSpec((T, tn), lambda j: (0, jnp.clip(j - s, 0, nb - 1)))

    rider_specs, rider_out_specs, rider_out_shape, rider_args = [], [], [], []
    for (w, parts, m, v), rr in zip(riders, rows):
        blk = pl.BlockSpec((rr, w.shape[1]), lambda j: (j, 0))
        rider_specs += [blk, pl.BlockSpec((4, rr, w.shape[1]), lambda j: (0, j, 0)), blk, blk]
        rider_out_specs += [blk] * 4
        rider_out_shape += [jax.ShapeDtypeStruct(w.shape, F32)] * 4
        rider_args += [w, parts, m, v]
    outs = pl.pallas_call(
        _drop_operands(body, 1 + n + 4 * nr, len(after)), name="wgrad_in", grid=(steps,),
        in_specs=[_resident((M, T))] + [piece_spec(s, nb) for s, nb in zip(start, nblk)] + rider_specs
                 + [HBM_SPEC] * len(after),
        out_specs=[pl.BlockSpec((tn, M), lambda j: (j, 0))] + rider_out_specs,
        out_shape=[jax.ShapeDtypeStruct((steps * tn, M), BF16)] + rider_out_shape,
        compiler_params=_params(("parallel",)),
    )(ht, *dparts, *rider_args, *after)
    return outs[0], [outs[1 + 4 * i:5 + 4 * i] for i in range(nr)]


def _adamw_math(w, g, m, v):
    m = ADAM_B1 * m + (1.0 - ADAM_B1) * g
    v = ADAM_B2 * v + (1.0 - ADAM_B2) * (g * g)
    m_hat = m / (1.0 - ADAM_B1 ** ADAM_STEP)
    v_hat = v / (1.0 - ADAM_B2 ** ADAM_STEP)
    delta = -ADAM_LR * (m_hat / (jnp.sqrt(v_hat) + ADAM_EPS) + ADAM_WD * w)
    return delta, m, v


def _adamw_sum(name, ws, parts, ms, vs):
    n = len(ws)
    steps = min(_row_steps(w.shape[0]) for w in ws)
    rows = [w.shape[0] // steps for w in ws]

    def body(*refs):
        w_refs, p_refs, m_refs, v_refs = (refs[k * n:(k + 1) * n] for k in range(4))
        out_refs = refs[4 * n:]
        for i in range(n):
            g = p_refs[i][0].astype(F32)
            for k in range(1, 4):
                g = g + p_refs[i][k].astype(F32)
            delta, m_new, v_new = _adamw_math(w_refs[i][...], g, m_refs[i][...], v_refs[i][...])
            for k, val in enumerate((g, delta, m_new, v_new)):
                out_refs[4 * i + k][...] = val

    blk = [pl.BlockSpec((r, w.shape[1]), lambda s: (s, 0)) for r, w in zip(rows, ws)]
    pblk = [pl.BlockSpec((4, r, w.shape[1]), lambda s: (0, s, 0)) for r, w in zip(rows, ws)]
    out_specs, out_shape = [], []
    for b, w in zip(blk, ws):
        out_specs += [b] * 4
        out_shape += [jax.ShapeDtypeStruct(w.shape, F32)] * 4
    flat = pl.pallas_call(
        body, name=name, grid=(steps,),
        in_specs=blk + pblk + blk + blk, out_specs=out_specs, out_shape=out_shape,
        compiler_params=_params(("parallel",)),
    )(*ws, *parts, *ms, *vs)
    return [flat[4 * i:4 * i + 4] for i in range(n)]


_SMALL_SLOTS = (("norm_mix_g", 0, 1, 1024), ("norm_ffn_g", 1, 1, 1024), ("norm_final_g", 2, 1, 1024),
                ("lower_bounds", 3, 2, 512), ("hg_norm_g", 5, 1, 128), ("loss", 6, 1, 128), ("conv_w", 8, 3, 512))
_SMALL_PARAMS = tuple(s for s in _SMALL_SLOTS if s[0] != "loss")
CONV_SHARD = CONV_WIDTH // N_DEV


def _small_pack(small):
    def body(*refs):
        out = refs[-1]
        out[...] = jnp.zeros_like(out)
        for ref, (_, row, rows, lanes) in zip(refs[:-1], _SMALL_SLOTS):
            out[row:row + rows, 0:lanes] = ref[...]

    vmem = pl.BlockSpec(memory_space=pltpu.VMEM)
    return pl.pallas_call(
        body, name="small_pack", in_specs=[vmem] * len(_SMALL_SLOTS), out_specs=vmem,
        out_shape=jax.ShapeDtypeStruct((SMALL_ROWS, 1024), F32),
    )(*[small[name] for name, _, _, _ in _SMALL_SLOTS])


def _small_update(gathered, dev, w, m, v):
    n = len(_SMALL_PARAMS)

    def body(dev_ref, g_ref, *refs):
        w_refs, m_refs, v_refs = refs[:n], refs[n:2 * n], refs[2 * n:3 * n]
        loss_ref, out_refs, sum_scr = refs[3 * n], refs[3 * n + 1:-1], refs[-1]
        total = g_ref[0]
        for k in range(1, N_DEV):
            total = total + g_ref[k]
        sum_scr[...] = total
        loss_ref[...] = sum_scr[6:7, 0:128]
        for p, (name, row, rows, lanes) in enumerate(_SMALL_PARAMS):
            if name == "conv_w":
                g = sum_scr[row:row + rows, 0:CONV_SHARD]
                for s in range(1, N_DEV):
                    g = jnp.where(dev_ref[0] == s, sum_scr[row:row + rows, s * CONV_SHARD:(s + 1) * CONV_SHARD], g)
            else:
                g = sum_scr[row:row + rows, 0:lanes]
            delta, m_new, v_new = _adamw_math(w_refs[p][...], g, m_refs[p][...], v_refs[p][...])
            out_refs[4 * p][...] = g
            out_refs[4 * p + 1][...] = delta
            out_refs[4 * p + 2][...] = m_new
            out_refs[4 * p + 3][...] = v_new

    vmem = pl.BlockSpec(memory_space=pltpu.VMEM)
    outs = [jax.ShapeDtypeStruct((1, 128), F32)]
    for a in w:
        outs += [jax.ShapeDtypeStruct(a.shape, F32)] * 4
    return pl.pallas_call(
        body, name="small_update",
        in_specs=[pl.BlockSpec(memory_space=pltpu.SMEM)] + [vmem] * (1 + 3 * n), out_specs=[vmem] * len(outs),
        out_shape=outs, scratch_shapes=[pltpu.VMEM((SMALL_ROWS, 1024), F32)],
    )(dev, gathered, *w, *m, *v)


def _row_steps(rows):
    for steps in (4, 2):
        if rows % (16 * steps) == 0:
            return steps
    return 1


def _pair_sum(name, by_owner, got, core, after=()):
    n = len(got)

    def body(core_ref, *refs):
        for a_ref, b_ref, o_ref in zip(refs[:n], refs[n:2 * n], refs[2 * n:]):
            o_ref[...] = (a_ref[...].astype(F32) + b_ref[...].astype(F32)).astype(BF16)

    def blk(g):
        return pl.BlockSpec((None,) + g.shape[1:], lambda k, core_ref: (k, 0, 0))

    def mine(g):
        return pl.BlockSpec((None,) + g.shape[1:], lambda k, core_ref: (2 * k + core_ref[0], 0, 0))

    return pl.pallas_call(
        _drop_operands(body, 1 + 2 * n, len(after)), name=name,
        grid_spec=pltpu.PrefetchScalarGridSpec(
            num_scalar_prefetch=1, grid=(4,),
            in_specs=[mine(g) for g in got] + [blk(g) for g in got] + [HBM_SPEC] * len(after),
            out_specs=[blk(g) for g in got]),
        out_shape=[jax.ShapeDtypeStruct(g.shape, BF16) for g in got],
        compiler_params=_params(("parallel",)),
    )(core, *by_owner, *got, *after)


MESH = pl.DeviceIdType.MESH
HBM_SPEC = pl.BlockSpec(memory_space=pl.ANY)


def _handshake(peers):
    barrier = pltpu.get_barrier_semaphore()
    for peer in peers:
        pl.semaphore_signal(barrier, inc=1, device_id=peer, device_id_type=MESH)
    pl.semaphore_wait(barrier, len(peers))


def _comm_call(body, name, operands, out_shape, scratch, collective_id):
    if collective_id is None:
        return pl.pallas_call(body, name=name, in_specs=[HBM_SPEC] * len(operands), out_specs=[HBM_SPEC] * len(out_shape),
                              out_shape=out_shape, scratch_shapes=scratch)(*operands)
    return pl.kernel(body, out_type=out_shape, mesh=plsc.ScalarSubcoreMesh(axis_name="sequencer", num_cores=1),
                     scratch_types=scratch, name=name,
                     compiler_params=pltpu.CompilerParams(collective_id=collective_id))(*operands)


def _all_gather(name, blocks, collective_id=None, after=()):
    n = len(blocks)
    na = len(after)

    def body(*refs):
        x_refs, out_refs = refs[:n], refs[n + na:2 * n + na]
        send_sems, recv_sems, local_sems = refs[2 * n + na:]
        x, y, c = lax.axis_index("x"), lax.axis_index("y"), lax.axis_index("c")
        me, sibling = (x, y, c), (x, y, 1 - c)
        chips = [(1 - x, y), (x, 1 - y), (1 - x, 1 - y)]
        if collective_id is not None:
            _handshake([sibling] + [(*chip, c) for chip in chips])

        def slot(i, px, py, pc):
            return out_refs[i].at[4 * px + 2 * py + pc]

        def copy(i, k, blk, to, src=None):
            return pltpu.make_async_remote_copy(
                src_ref=slot(i, *blk) if src is None else src, dst_ref=slot(i, *blk),
                send_sem=send_sems.at[7 * i + k], recv_sem=recv_sems.at[7 * i + k], device_id=to, device_id_type=MESH)

        mine = [pltpu.make_async_copy(x_refs[i], slot(i, *me), local_sems.at[i]) for i in range(n)]
        for cp in mine:
            cp.start()
        first = []
        for i in range(n):
            first.append(copy(i, 0, me, sibling, src=x_refs[i]))
            first += [copy(i, 1 + j, me, (*chip, c), src=x_refs[i]) for j, chip in enumerate(chips)]
        for cp in first:
            cp.start()
        passed = []
        for i in range(n):
            for j, chip in enumerate(chips):
                copy(i, 1 + j, (*chip, c), me).wait_recv()
                passed.append(copy(i, 4 + j, (*chip, c), sibling))
                passed[-1].start()
        for i in range(n):
            copy(i, 0, sibling, me).wait_recv()
            for j, chip in enumerate(chips):
                copy(i, 4 + j, (*chip, 1 - c), me).wait_recv()
        for cp in first + passed:
            cp.wait_send()
        for cp in mine:
            cp.wait()

    return _comm_call(
        body, name, list(blocks) + list(after), [jax.ShapeDtypeStruct((N_DEV,) + b.shape, b.dtype) for b in blocks],
        [pltpu.SemaphoreType.DMA((7 * n,)), pltpu.SemaphoreType.DMA((7 * n,)), pltpu.SemaphoreType.DMA((n,))],
        collective_id)


def _sibling_swap(name, by_owner, collective_id=None, after=()):
    n = len(by_owner)
    na = len(after)

    def body(*refs):
        x_refs, out_refs = refs[:n], refs[n + na:2 * n + na]
        send_sems, recv_sems = refs[2 * n + na:]
        x, y, c = lax.axis_index("x"), lax.axis_index("y"), lax.axis_index("c")
        if collective_id is not None:
            _handshake([(x, y, 1 - c)])
        copies = []
        for i in range(n):
            for k in range(4):
                copies.append(pltpu.make_async_remote_copy(
                    src_ref=x_refs[i].at[2 * k + 1 - c], dst_ref=out_refs[i].at[k],
                    send_sem=send_sems.at[4 * i + k], recv_sem=recv_sems.at[4 * i + k],
                    device_id=(x, y, 1 - c), device_id_type=MESH))
        for cp in copies:
            cp.start()
        for cp in copies:
            cp.wait()

    return _comm_call(
        body, name, list(by_owner) + list(after),
        [jax.ShapeDtypeStruct((4,) + b.shape[1:], b.dtype) for b in by_owner],
        [pltpu.SemaphoreType.DMA((4 * n,)), pltpu.SemaphoreType.DMA((4 * n,))], collective_id)


def _chip_exchange(name, sums, collective_id=None, after=()):
    n = len(sums)
    na = len(after)

    def body(*refs):
        x_refs, out_refs = refs[:n], refs[n + na:2 * n + na]
        send_sems, recv_sems, local_sems = refs[2 * n + na:]
        x, y, c = lax.axis_index("x"), lax.axis_index("y"), lax.axis_index("c")
        chips = [(1 - x, y), (x, 1 - y), (1 - x, 1 - y)]
        my_chip = 2 * x + y
        if collective_id is not None:
            _handshake([(cx, cy, c) for cx, cy in chips])
        mine = [pltpu.make_async_copy(x_refs[i].at[my_chip], out_refs[i].at[my_chip], local_sems.at[i])
                for i in range(n)]
        for cp in mine:
            cp.start()
        sends = []
        for i in range(n):
            for j, (cx, cy) in enumerate(chips):
                sends.append(pltpu.make_async_remote_copy(
                    src_ref=x_refs[i].at[2 * cx + cy], dst_ref=out_refs[i].at[my_chip],
                    send_sem=send_sems.at[3 * i + j], recv_sem=recv_sems.at[3 * i + j],
                    device_id=(cx, cy, c), device_id_type=MESH))
        for cp in sends:
            cp.start()
        for i in range(n):
            for j, (cx, cy) in enumerate(chips):
                pltpu.make_async_remote_copy(
                    src_ref=x_refs[i].at[my_chip], dst_ref=out_refs[i].at[2 * cx + cy],
                    send_sem=send_sems.at[3 * i + j], recv_sem=recv_sems.at[3 * i + j],
                    device_id=(cx, cy, c), device_id_type=MESH).wait_recv()
        for cp in sends:
            cp.wait_send()
        for cp in mine:
            cp.wait()

    return _comm_call(
        body, name, list(sums) + list(after), [jax.ShapeDtypeStruct(s.shape, s.dtype) for s in sums],
        [pltpu.SemaphoreType.DMA((3 * n,)), pltpu.SemaphoreType.DMA((3 * n,)), pltpu.SemaphoreType.DMA((n,))],
        collective_id)


def _cast_shards(shards):
    n = len(shards)

    def body(*refs):
        for i in range(n):
            refs[n + i][...] = refs[i][...].astype(BF16)

    vmem = pl.BlockSpec(memory_space=pltpu.VMEM)
    return pl.pallas_call(
        body, name="cast_shards", in_specs=[vmem] * n, out_specs=[vmem] * n,
        out_shape=[jax.ShapeDtypeStruct(s.shape, BF16) for s in shards],
        compiler_params=pltpu.CompilerParams(vmem_limit_bytes=VMEM_LIMIT_V7X),
    )(*shards)


BIG = ("w_in", "w_branch_a", "w_branch_b", "w_out", "w_ffn_gate", "w_ffn_up", "w_ffn_down")


def _local_step(x, target, gains, low, conv_w, wg8, reduce):
    g_mix, g_hg, g_ffn, g_fin = gains
    w_in = wg8["w_in"].reshape(N_IN, D_MODEL)
    wg = wg8["w_ffn_gate"].reshape(D_FF, D_MODEL)
    wu = wg8["w_ffn_up"].reshape(D_FF, D_MODEL)
    wa, wb = wg8["w_branch_a"], wg8["w_branch_b"]
    wo = wg8["w_out"].reshape(D_MODEL, D_MODEL)
    wd = wg8["w_ffn_down"].reshape(D_FF, D_MODEL)

    ht, hg, cv, gt, cvo, cvot, o, og, ogt, st = _fwd_in(x, g_mix, w_in, conv_w, low, g_hg)
    x1, mgt = _merge_fwd(og, cvo, gt, x, wa, wb, wo)
    h2t, gate, up, act, loss, d_gfin, dx2, dx2t = _ffn_fwd_loss(x1, g_ffn, wg, wu, wd, target, g_fin)

    dgate, dup, dx1, d_gffn = _ffn_bwd(dx2, x1, gate, up, g_ffn, wg, wu, wd)
    d_wg, d_wu, d_wd = _wgrad_ffn(h2t, dgate, dup, dx2t, act)
    by_owner_ffn = lambda a: a.reshape(N_DEV, D_FF // N_DEV, D_MODEL)
    ffn = dict(w_ffn_down=by_owner_ffn(d_wd), w_ffn_gate=by_owner_ffn(d_wg), w_ffn_up=by_owner_ffn(d_wu))
    dgt, dya, dyb, dog, dcv, d_conv = _merge_bwd(dx1, og, cvo, gt, cv, wa, wb, wo, conv_w)
    sums_ffn, got_ffn = reduce.begin(ffn, sum_after=[dya])
    late = ("w_ffn_gate", "w_ffn_up")
    parts_ffn, updated_ffn = reduce.finish(ffn, sums_ffn, defer=late)
    grad_a, grad_b, grad_o = _wgrad_out_branches(ogt, dya, cvot, dyb, mgt, dx1, after=sums_ffn[:1])
    out = dict(w_out=grad_o.reshape(N_DEV, D_MODEL // N_DEV, D_MODEL), w_branch_a=grad_a, w_branch_b=grad_b)
    dhg, d_low, d_ghg = _hg_bwd(dog, hg, o, st, low, g_hg, after=list(sums_ffn) + [out["w_out"]])
    sums_out, got_out = reduce.begin(out, after=[parts_ffn[0], dhg], sum_after=updated_ffn)
    parts_out, updated_out = reduce.finish(out, sums_out)
    dparts = [dhg, dcv, dgt]
    d_w_in_t, ridden = _wgrad_in(ht, dparts, after=sums_out[:1],
                                 riders=reduce.riders(late, dict(zip(ffn, parts_ffn))))
    reduce.record(late, ridden)
    w_in_grad = dict(w_in=d_w_in_t.reshape(N_DEV, N_IN // N_DEV, D_MODEL))
    sums_in, _ = reduce.begin(w_in_grad, after=parts_out[:1], sum_after=updated_out)
    parts_in, _ = reduce.finish(w_in_grad, sums_in)
    grad_x, d_gmix = _in_bwd(dparts, w_in, x, dx1, g_mix, after=list(parts_out[:1]) + list(sums_in))
    small = dict(norm_mix_g=d_gmix, norm_ffn_g=d_gffn, norm_final_g=d_gfin, lower_bounds=d_low, hg_norm_g=d_ghg,
                 conv_w=d_conv, loss=loss)
    return grad_x, small, parts_in


def _conv_shard_rows(a):
    return jnp.pad(a, ((0, 5), (0, 64)))


def kernel(x, norm_mix_g, w_in, lower_bounds, hg_norm_g, conv_w, w_branch_a, w_branch_b, w_out, norm_ffn_g, w_ffn_gate, w_ffn_up, w_ffn_down, norm_final_g, loss_target, m_norm_mix_g, m_w_in, m_lower_bounds, m_hg_norm_g, m_conv_w, m_w_branch_a, m_w_branch_b, m_w_out, m_norm_ffn_g, m_w_ffn_gate, m_w_ffn_up, m_w_ffn_down, m_norm_final_g, v_norm_mix_g, v_w_in, v_lower_bounds, v_hg_norm_g, v_conv_w, v_w_branch_a, v_w_branch_b, v_w_out, v_norm_ffn_g, v_w_ffn_gate, v_w_ffn_up, v_w_ffn_down, v_norm_final_g):
    cx, cy, cc = lax.axis_index("x"), lax.axis_index("y"), lax.axis_index("c")
    my_dev = 4 * cx + 2 * cy + cc

    def tr(a):
        return a[0].T

    big = dict(w_in=tr(w_in), w_branch_a=w_branch_a[0], w_branch_b=w_branch_b[0], w_out=w_out[0],
               w_ffn_gate=tr(w_ffn_gate), w_ffn_up=tr(w_ffn_up), w_ffn_down=w_ffn_down[0])
    big_m = dict(w_in=tr(m_w_in), w_branch_a=m_w_branch_a[0], w_branch_b=m_w_branch_b[0], w_out=m_w_out[0],
                 w_ffn_gate=tr(m_w_ffn_gate), w_ffn_up=tr(m_w_ffn_up), w_ffn_down=m_w_ffn_down[0])
    big_v = dict(w_in=tr(v_w_in), w_branch_a=v_w_branch_a[0], w_branch_b=v_w_branch_b[0], w_out=v_w_out[0],
                 w_ffn_gate=tr(v_w_ffn_gate), w_ffn_up=tr(v_w_ffn_up), w_ffn_down=v_w_ffn_down[0])
    transposed = ("w_in", "w_ffn_gate", "w_ffn_up")

    shards = dict(zip(BIG, _cast_shards([big[n] for n in BIG])))
    first = _all_gather("gather_w_in", [shards["w_in"], _conv_shard_rows(conv_w[0])])
    ids = iter(range(1, 16))
    mid = _all_gather("gather_mid", [shards[n] for n in BIG[1:4]], collective_id=next(ids))
    ffn = _all_gather("gather_ffn", [shards[n] for n in BIG[4:]], collective_id=next(ids), after=first[1:])
    wg8 = dict(zip(BIG, [first[0]] + list(mid) + list(ffn)))
    conv_full = first[1][:, :3, :64].transpose(1, 0, 2).reshape(3, CONV_WIDTH)

    core = cc.reshape(1).astype(jnp.int32)
    outs = {}

    class Reduce:
        @staticmethod
        def begin(grads, after=(), sum_after=()):
            names = list(grads)
            by_owner = [grads[n] for n in names]
            got = _sibling_swap("sibling_swap_" + names[0], by_owner, collective_id=next(ids), after=after)
            sums = _pair_sum("pair_sum_" + names[0], by_owner, got, core, after=sum_after)
            return sums, got

        @staticmethod
        def finish(grads, chip_sums, after=(), defer=()):
            names = list(grads)
            parts = _chip_exchange("chip_exchange_" + names[0], chip_sums, collective_id=next(ids), after=after)
            now = [n for n in names if n not in defer]
            updated = _adamw_sum("adamw_" + now[0], [big[n] for n in now],
                                 [p for n, p in zip(names, parts) if n in now],
                                 [big_m[n] for n in now], [big_v[n] for n in now])
            outs.update(zip(now, updated))
            return parts, [outs[n][1] for n in now]

        @staticmethod
        def riders(names, parts):
            return [(big[n], parts[n], big_m[n], big_v[n]) for n in names]

        @staticmethod
        def record(names, updated):
            outs.update(zip(names, updated))

    gains = (norm_mix_g, hg_norm_g, norm_ffn_g, norm_final_g.reshape(1, D_MODEL))
    grad_x, small, last = _local_step(x[0], loss_target[0], gains, lower_bounds, conv_full, wg8, Reduce)

    small_all = _all_gather("gather_small", [_small_pack(small)], collective_id=next(ids), after=last[:1])

    def small_state(a):
        return [a[0], a[1], a[2].reshape(1, D_MODEL), a[3], a[4], a[5][0]]

    upd = _small_update(
        small_all[0], my_dev.reshape(1).astype(jnp.int32),
        small_state((norm_mix_g, norm_ffn_g, norm_final_g, lower_bounds, hg_norm_g, conv_w)),
        small_state((m_norm_mix_g, m_norm_ffn_g, m_norm_final_g, m_lower_bounds, m_hg_norm_g, m_conv_w)),
        small_state((v_norm_mix_g, v_norm_ffn_g, v_norm_final_g, v_lower_bounds, v_hg_norm_g, v_conv_w)))
    loss = upd[0][0, 0]
    small_shape = dict(norm_final_g=(D_MODEL,), conv_w=(1, 3, CONV_SHARD))
    for p, (name, _, _, _) in enumerate(_SMALL_PARAMS):
        outs[name] = [a.reshape(small_shape.get(name, a.shape)) for a in upd[1 + 4 * p:5 + 4 * p]]

    order = ["norm_mix_g", "w_in", "lower_bounds", "hg_norm_g", "conv_w", "w_branch_a", "w_branch_b", "w_out",
             "norm_ffn_g", "w_ffn_gate", "w_ffn_up", "w_ffn_down", "norm_final_g"]
    result = [loss, grad_x[None]]
    for k in range(4):
        for n in order:
            if n in BIG:
                result.append((outs[n][k].T if n in transposed else outs[n][k])[None])
            else:
                result.append(outs[n][k])
    return tuple(result)
```

```python
import jax
import jax.numpy as jnp
from jax import lax
from jax.experimental import pallas as pl
from jax.experimental.pallas import tpu as pltpu
from jax.experimental.pallas import tpu_sc as plsc

F32 = jnp.float32
BF16 = jnp.bfloat16
STASH = jnp.bfloat16

D_MODEL = 1024
HG_WIDTH = 512
HEAD_DIM = 128
N_HEADS = 4
HEADS_PER_STEP = 4
HEAD_GROUPS = N_HEADS // HEADS_PER_STEP
CONV_WIDTH = 512
CONV_K = 3
D_FF = 2816
CHUNK = 32
EPS = 1e-6
Q_SCALE = HEAD_DIM ** -0.5
N_DEV = 8

ADAM_LR = 0.001
ADAM_B1 = 0.9
ADAM_B2 = 0.999
ADAM_EPS = 1e-08
ADAM_WD = 0.01
ADAM_STEP = 10

VMEM_LIMIT_V7X = 56 * 1024 * 1024
VMEM_LIMIT_LARGE_V7X = 62 * 1024 * 1024

SMALL_ROWS = 16


def _params(sem, vmem=VMEM_LIMIT_V7X):
    return pltpu.CompilerParams(dimension_semantics=sem, vmem_limit_bytes=vmem)


def _mm(a, b):
    return jnp.dot(a.astype(BF16), b.astype(BF16), preferred_element_type=F32)


def _mm_nt(a, b):
    return lax.dot_general(a.astype(BF16), b.astype(BF16), (((1,), (1,)), ((), ())), preferred_element_type=F32)


def _mm_tn(a, b):
    return lax.dot_general(a.astype(BF16), b.astype(BF16), (((0,), (0,)), ((), ())), preferred_element_type=F32)


def _sigmoid(x):
    return 0.5 * jnp.tanh(0.5 * x) + 0.5


def _resident(shape):
    nd = len(shape)
    return pl.BlockSpec(shape, lambda *_: (0,) * nd, pipeline_mode=pl.Buffered(1))


def _full(shape):
    nd = len(shape)
    return pl.BlockSpec(shape, lambda *_: (0,) * nd)


def _shard_cols(w_ref):
    return jnp.concatenate([w_ref[s] for s in range(N_DEV)], axis=1)


N_HG = 4 * HG_WIDTH
N_CV = 3 * CONV_WIDTH
N_GT = 2 * D_MODEL
N_IN = N_HG + N_CV + N_GT


def _col(tm, n):
    return pl.BlockSpec((n, tm), lambda i: (0, i))


HALO = 8


def _fwd_in(x, g, w_in_t, conv_w, low, gn):
    T = x.shape[0]
    tm = min(512, T)
    nc = tm // CHUNK

    def body(x_ref, g_ref, w_ref, cw_ref, low_ref, gn_ref, ht_ref, hg_ref, cv_ref, gt_ref, cvo_ref, cvot_ref,
             o_ref, og_ref, ogt_ref, st_ref, tail_scr, s_scr):
        @pl.when(pl.program_id(0) == 0)
        def _():
            tail_scr[...] = jnp.zeros_like(tail_scr)
            s_scr[...] = jnp.zeros_like(s_scr)

        xv = x_ref[...]
        r = lax.rsqrt(jnp.mean(xv * xv, axis=-1, keepdims=True) + EPS)
        hf = xv * r * g_ref[...]
        h = hf.astype(BF16)
        ht_ref[...] = hf.T.astype(BF16)
        hg_ref[...] = _mm_nt(h, w_ref[:N_HG, :])
        cv = _mm_nt(h, w_ref[N_HG:N_HG + N_CV, :])
        cv_ref[...] = cv.astype(STASH)
        gt_ref[...] = _mm_nt(h, w_ref[N_HG + N_CV:, :]).astype(STASH)

        u = cv[:, :CONV_WIDTH] * cv[:, 2 * CONV_WIDTH:]
        row = lax.broadcasted_iota(jnp.int32, u.shape, 0)
        prev1 = tail_scr[HALO - 1:HALO, :]
        prev2 = tail_scr[HALO - 2:HALO - 1, :]
        u1 = jnp.where(row >= 1, pltpu.roll(u, 1, 0), prev1)
        u2 = jnp.where(row >= 2, pltpu.roll(u, 2, 0), jnp.where(row == 1, prev1, prev2))
        y = cw_ref[0:1, :] * u2 + cw_ref[1:2, :] * u1 + cw_ref[2:3, :] * u
        out = cv[:, CONV_WIDTH:2 * CONV_WIDTH] * y
        cvo_ref[...] = out.astype(BF16)
        cvot_ref[...] = out.T.astype(BF16)
        tail_scr[...] = u[tm - HALO:, :]

        _hg_fwd_tile(hg_ref, low_ref, gn_ref, o_ref, og_ref, ogt_ref, st_ref, s_scr, tm)

    row = lambda n: pl.BlockSpec((tm, n), lambda i: (i, 0))
    return pl.pallas_call(
        body, name="fwd_in", grid=(T // tm,),
        in_specs=[row(D_MODEL), _full((1, D_MODEL)), _resident(w_in_t.shape), _full((CONV_K, CONV_WIDTH)),
                  _full((2, HG_WIDTH)), _full((1, HEAD_DIM))],
        out_specs=[_col(tm, D_MODEL), row(N_HG), row(N_CV), row(N_GT), row(CONV_WIDTH), _col(tm, CONV_WIDTH),
                   row(HG_WIDTH), row(HG_WIDTH), _col(tm, HG_WIDTH),
                   pl.BlockSpec((N_HEADS, nc, HEAD_DIM, HEAD_DIM), lambda i: (0, i, 0, 0))],
        out_shape=[jax.ShapeDtypeStruct((D_MODEL, T), BF16), jax.ShapeDtypeStruct((T, N_HG), F32),
                   jax.ShapeDtypeStruct((T, N_CV), STASH), jax.ShapeDtypeStruct((T, N_GT), STASH),
                   jax.ShapeDtypeStruct((T, CONV_WIDTH), BF16), jax.ShapeDtypeStruct((CONV_WIDTH, T), BF16),
                   jax.ShapeDtypeStruct((T, HG_WIDTH), F32), jax.ShapeDtypeStruct((T, HG_WIDTH), BF16),
                   jax.ShapeDtypeStruct((HG_WIDTH, T), BF16),
                   jax.ShapeDtypeStruct((N_HEADS, T // CHUNK, HEAD_DIM, HEAD_DIM), F32)],
        scratch_shapes=[pltpu.VMEM((HALO, CONV_WIDTH), F32), pltpu.VMEM((N_HEADS, HEAD_DIM, HEAD_DIM), F32)],
        compiler_params=_params(("arbitrary",), vmem=VMEM_LIMIT_LARGE_V7X),
    )(x, g, w_in_t, conv_w, low, gn)


def _chunk_pos(shape):
    return lax.broadcasted_iota(jnp.int32, shape, 0) & (CHUNK - 1)


def _chunk_cumsum(x, pos):
    s = 1
    while s < CHUNK:
        x = x + jnp.where(pos >= s, pltpu.roll(x, s, 0), 0.0)
        s *= 2
    return x


def _chunk_rev_cumsum(x, pos):
    n = x.shape[0]
    s = 1
    while s < CHUNK:
        x = x + jnp.where(pos + s < CHUNK, pltpu.roll(x, n - s, 0), 0.0)
        s *= 2
    return x


def _lower_bound(low_ref):
    l0 = low_ref[0:1, :]
    l1 = low_ref[1:2, :]
    m = jnp.maximum(l0, l1)
    e0 = jnp.exp(l0 - m)
    e1 = jnp.exp(l1 - m)
    return e0 / (e0 + e1), e1 / (e0 + e1)


def _hg_gates(qr, fr, lb, pos, tb):
    sq = _sigmoid(qr)
    q = qr * sq * Q_SCALE
    sg = _sigmoid(fr)
    f = lb + (1.0 - lb) * sg
    k = 1.0 - f
    b = _chunk_cumsum(jnp.log(f), pos)
    b3 = b.reshape(tb // CHUNK, CHUNK, HEAD_DIM)
    anc = b3[:, CHUNK // 2 - 1:CHUNK // 2, :]
    last = b3[:, CHUNK - 1:CHUNK, :]
    d3 = b3 - anc
    e_qa3 = jnp.exp(d3)
    e_ka3 = jnp.exp(-d3)
    e_b3 = e_qa3 * jnp.exp(anc)
    e_ko3 = e_ka3 * jnp.exp(last - anc)
    dec = jnp.exp(last)
    flat = lambda a: a.reshape(tb, HEAD_DIM)
    return sq, q, sg, f, k, flat(e_qa3), flat(e_ka3), flat(e_b3), flat(e_ko3), dec


def _intra_mask(sb):
    r = lax.broadcasted_iota(jnp.int32, (sb, sb), 0)
    c = lax.broadcasted_iota(jnp.int32, (sb, sb), 1)
    return ((r // CHUNK) == (c // CHUNK)) & (c <= r)


def _hg_fwd_tile(hg_ref, low_ref, gn_ref, o_ref, og_ref, ogt_ref, st_ref, s_scr, tb):
    sb = min(256, tb)
    nc = tb // CHUNK
    q_ref, f_ref, i_ref, g_ref = (hg_ref.at[:, p * HG_WIDTH:(p + 1) * HG_WIDTH] for p in range(4))
    pos = _chunk_pos((tb, HEAD_DIM))
    mask = _intra_mask(sb)
    lanes = [slice(hh * HEAD_DIM, (hh + 1) * HEAD_DIM) for hh in range(N_HEADS)]
    qi, ko, vb, dec, st = [], [], [], [], []
    for hh, ln in enumerate(lanes):
        lb, _ = _lower_bound(low_ref.at[:, ln])
        _, q, _, _, k, e_qa, e_ka, e_b, e_ko, dec_h = _hg_gates(q_ref[:, ln], f_ref[:, ln], lb, pos, tb)
        qh = (q * e_qa).astype(BF16)
        kh = (k * e_ka).astype(BF16)
        qi.append((q * e_b).astype(BF16))
        ko.append((k * e_ko).astype(BF16))
        vb.append(i_ref[:, ln].astype(BF16))
        dec.append(dec_h)
        st.append(s_scr[hh])
        for s in range(tb // sb):
            sl = slice(s * sb, (s + 1) * sb)
            p = jnp.where(mask, _mm_nt(qh[sl], kh[sl]), 0.0)
            o_ref[sl, ln] = _mm(p, vb[hh][sl])
    for c in range(nc):
        sl = slice(c * CHUNK, (c + 1) * CHUNK)
        for hh, ln in enumerate(lanes):
            st_ref[hh, c] = st[hh]
            o_ref[sl, ln] = o_ref[sl, ln] + _mm_nt(qi[hh][sl], st[hh])
            st[hh] = dec[hh][c] * st[hh] + _mm_tn(vb[hh][sl], ko[hh][sl])
    for hh, ln in enumerate(lanes):
        s_scr[hh] = st[hh]
        o = o_ref[:, ln]
        r = lax.rsqrt(jnp.mean(o * o, axis=-1, keepdims=True) + EPS)
        gr = g_ref[:, ln]
        og = (o * r * gn_ref[...]) * (gr * _sigmoid(gr))
        og_ref[:, ln] = og.astype(BF16)
        ogt_ref[ln, :] = og.T.astype(BF16)


def _merge_fwd(og, cvo, gt, x, wa, wb, wo):
    T = x.shape[0]
    tm = min(1024, T)

    def body(og_ref, cvo_ref, gt_ref, x_ref, wa_ref, wb_ref, wo_ref, x1_ref, mgt_ref):
        ya = jnp.dot(og_ref[...], _shard_cols(wa_ref), preferred_element_type=F32)
        yb = jnp.dot(cvo_ref[...], _shard_cols(wb_ref), preferred_element_type=F32)
        m = (_sigmoid(gt_ref[:, :D_MODEL].astype(F32)) * ya
             + _sigmoid(gt_ref[:, D_MODEL:].astype(F32)) * yb)
        mgt_ref[...] = m.T.astype(BF16)
        x1_ref[...] = x_ref[...] + jnp.dot(m.astype(BF16), wo_ref[...], preferred_element_type=F32)

    row = lambda n: pl.BlockSpec((tm, n), lambda i: (i, 0))
    return pl.pallas_call(
        body, name="merge_fwd", grid=(T // tm,),
        in_specs=[row(HG_WIDTH), row(CONV_WIDTH), row(2 * D_MODEL), row(D_MODEL),
                  _resident(wa.shape), _resident(wb.shape), _resident(wo.shape)],
        out_specs=[row(D_MODEL), _col(tm, D_MODEL)],
        out_shape=[jax.ShapeDtypeStruct((T, D_MODEL), F32), jax.ShapeDtypeStruct((D_MODEL, T), BF16)],
        compiler_params=_params(("parallel",)),
    )(og, cvo, gt, x, wa, wb, wo)


def _ffn_fwd_loss(x1, g, wg, wu, wd, target, g_fin):
    T = x1.shape[0]
    tm = min(512, T)

    def body(x_ref, g_ref, wg_ref, wu_ref, wd_ref, t_ref, gf_ref,
             ht_ref, gate_ref, up_ref, act_ref, loss_ref, dgf_ref, dx2_ref, dx2t_ref):
        @pl.when(pl.program_id(0) == 0)
        def _():
            loss_ref[...] = jnp.zeros_like(loss_ref)
            dgf_ref[...] = jnp.zeros_like(dgf_ref)

        xv = x_ref[...]
        r = lax.rsqrt(jnp.mean(xv * xv, axis=-1, keepdims=True) + EPS)
        hf = xv * r * g_ref[...]
        h = hf.astype(BF16)
        ht_ref[...] = hf.T.astype(BF16)
        gate = _mm_nt(h, wg_ref[...])
        up = _mm_nt(h, wu_ref[...])
        gate_ref[...] = gate.astype(STASH)
        up_ref[...] = up.astype(STASH)
        act = (gate * _sigmoid(gate) * up).astype(BF16)
        act_ref[...] = act
        x2 = xv + jnp.dot(act, wd_ref[...], preferred_element_type=F32)

        gv = gf_ref[...]
        r2 = lax.rsqrt(jnp.mean(x2 * x2, axis=-1, keepdims=True) + EPS)
        xh = x2 * r2
        err = xh * gv - t_ref[...]
        loss_ref[...] += 0.5 * jnp.sum(jnp.mean(err * err, axis=-1, keepdims=True), axis=0, keepdims=True)
        dy = err * (1.0 / D_MODEL)
        dgf_ref[...] += jnp.sum(dy * xh, axis=0, keepdims=True)
        w = dy * gv
        dx2 = r2 * (w - xh * jnp.mean(w * xh, axis=-1, keepdims=True))
        dx2_ref[...] = dx2
        dx2t_ref[...] = dx2.T.astype(BF16)

    row = lambda n: pl.BlockSpec((tm, n), lambda i: (i, 0))
    return pl.pallas_call(
        body, name="ffn_fwd_loss", grid=(T // tm,),
        in_specs=[row(D_MODEL), _full((1, D_MODEL)), _resident(wg.shape), _resident(wu.shape), _resident(wd.shape),
                  row(D_MODEL), _full((1, D_MODEL))],
        out_specs=[_col(tm, D_MODEL), row(D_FF), row(D_FF), row(D_FF), _full((1, 128)), _full((1, D_MODEL)),
                   row(D_MODEL), _col(tm, D_MODEL)],
        out_shape=[jax.ShapeDtypeStruct((D_MODEL, T), BF16), jax.ShapeDtypeStruct((T, D_FF), STASH),
                   jax.ShapeDtypeStruct((T, D_FF), STASH), jax.ShapeDtypeStruct((T, D_FF), BF16),
                   jax.ShapeDtypeStruct((1, 128), F32), jax.ShapeDtypeStruct((1, D_MODEL), F32),
                   jax.ShapeDtypeStruct((T, D_MODEL), F32), jax.ShapeDtypeStruct((D_MODEL, T), BF16)],
        compiler_params=_params(("arbitrary",), vmem=VMEM_LIMIT_LARGE_V7X),
    )(x1, g, wg, wu, wd, target, g_fin)


def _ffn_bwd(dx2, x1, gate, up, g, wg, wu, wd):
    T = x1.shape[0]
    tm = min(512, T)

    def body(dx2_ref, x_ref, gate_ref, up_ref, g_ref, wg_ref, wu_ref, wd_ref, dgate_ref, dup_ref, dx1_ref, dgn_ref):
        @pl.when(pl.program_id(0) == 0)
        def _():
            dgn_ref[...] = jnp.zeros_like(dgn_ref)

        dx2 = dx2_ref[...]
        dact = _mm_nt(dx2, wd_ref[...])
        gate = gate_ref[...].astype(F32)
        s = _sigmoid(gate)
        dgate = (dact * up_ref[...].astype(F32) * (s * (1.0 + gate * (1.0 - s)))).astype(BF16)
        dup = (dact * (gate * s)).astype(BF16)
        dgate_ref[...] = dgate
        dup_ref[...] = dup
        dh = _mm(dgate, wg_ref[...]) + _mm(dup, wu_ref[...])
        xv = x_ref[...]
        r = lax.rsqrt(jnp.mean(xv * xv, axis=-1, keepdims=True) + EPS)
        xh = xv * r
        dgn_ref[...] += jnp.sum(dh * xh, axis=0, keepdims=True)
        w = dh * g_ref[...]
        dx1_ref[...] = dx2 + r * (w - xh * jnp.mean(w * xh, axis=-1, keepdims=True))

    row = lambda n: pl.BlockSpec((tm, n), lambda i: (i, 0))
    return pl.pallas_call(
        body, name="ffn_bwd", grid=(T // tm,),
        in_specs=[row(D_MODEL), row(D_MODEL), row(D_FF), row(D_FF), _full((1, D_MODEL)),
                  _resident(wg.shape), _resident(wu.shape), _resident(wd.shape)],
        out_specs=[row(D_FF), row(D_FF), row(D_MODEL), _full((1, D_MODEL))],
        out_shape=[jax.ShapeDtypeStruct((T, D_FF), BF16), jax.ShapeDtypeStruct((T, D_FF), BF16),
                   jax.ShapeDtypeStruct((T, D_MODEL), F32), jax.ShapeDtypeStruct((1, D_MODEL), F32)],
        compiler_params=_params(("arbitrary",), vmem=VMEM_LIMIT_LARGE_V7X),
    )(dx2, x1, gate, up, g, wg, wu, wd)


def _merge_bwd(dx1, og, cvo, gt, cv, wa, wb, wo, conv_w):
    T = dx1.shape[0]
    tm = min(512, T)
    nt = T // tm

    def body(dx_ref, og_ref, cvo_ref, gt_ref, cv_ref, halo_ref, wa_ref, wb_ref, wo_ref, cw_ref,
             dgt_ref, dya_ref, dyb_ref, dog_ref, dcv_ref, dcw_ref, prev_u, next_dy):
        step = pl.program_id(0)

        @pl.when(step == 0)
        def _():
            next_dy[...] = jnp.zeros_like(next_dy)
            dcw_ref[...] = jnp.zeros_like(dcw_ref)

        dm = _mm_nt(dx_ref[...], wo_ref[...])
        wa = _shard_cols(wa_ref)
        wb = _shard_cols(wb_ref)
        ya = jnp.dot(og_ref[...], wa, preferred_element_type=F32)
        yb = jnp.dot(cvo_ref[...], wb, preferred_element_type=F32)
        sa = _sigmoid(gt_ref[:, :D_MODEL].astype(F32))
        sb = _sigmoid(gt_ref[:, D_MODEL:].astype(F32))
        dgt_ref[:, :D_MODEL] = (dm * ya * (sa * (1.0 - sa))).astype(BF16)
        dgt_ref[:, D_MODEL:] = (dm * yb * (sb * (1.0 - sb))).astype(BF16)
        dya = (dm * sa).astype(BF16)
        dyb = (dm * sb).astype(BF16)
        dya_ref[...] = dya
        dyb_ref[...] = dyb
        dog_ref[...] = _mm_nt(dya, wa)
        dcvo = _mm_nt(dyb, wb)

        cvt = cv_ref[...].astype(F32)
        c, bg, xb = cvt[:, :CONV_WIDTH], cvt[:, CONV_WIDTH:2 * CONV_WIDTH], cvt[:, 2 * CONV_WIDTH:]
        halo = halo_ref[...].astype(F32)
        first_tile = step == nt - 1
        prev_u[...] = jnp.where(first_tile, 0.0, halo[:, :CONV_WIDTH] * halo[:, 2 * CONV_WIDTH:])
        u = c * xb
        row = lax.broadcasted_iota(jnp.int32, u.shape, 0)
        p1 = prev_u[HALO - 1:HALO, :]
        p2 = prev_u[HALO - 2:HALO - 1, :]
        u1 = jnp.where(row >= 1, pltpu.roll(u, 1, 0), p1)
        u2 = jnp.where(row >= 2, pltpu.roll(u, 2, 0), jnp.where(row == 1, p1, p2))
        w0, w1, w2 = cw_ref[0:1, :], cw_ref[1:2, :], cw_ref[2:3, :]
        y = w0 * u2 + w1 * u1 + w2 * u
        dcv_ref[:, CONV_WIDTH:2 * CONV_WIDTH] = (dcvo * y).astype(BF16)
        dy = dcvo * bg
        dcw_ref[0:1, :] += jnp.sum(dy * u2, axis=0, keepdims=True)
        dcw_ref[1:2, :] += jnp.sum(dy * u1, axis=0, keepdims=True)
        dcw_ref[2:3, :] += jnp.sum(dy * u, axis=0, keepdims=True)
        n1 = next_dy[0:1, :]
        n2 = next_dy[1:2, :]
        dy1 = jnp.where(row < tm - 1, pltpu.roll(dy, tm - 1, 0), n1)
        dy2 = jnp.where(row < tm - 2, pltpu.roll(dy, tm - 2, 0), jnp.where(row == tm - 2, n1, n2))
        du = w2 * dy + w1 * dy1 + w0 * dy2
        dcv_ref[:, :CONV_WIDTH] = (du * xb).astype(BF16)
        dcv_ref[:, 2 * CONV_WIDTH:] = (du * c).astype(BF16)
        next_dy[...] = dy[:HALO, :]

    rt = lambda i: nt - 1 - i
    row = lambda n: pl.BlockSpec((tm, n), lambda i: (rt(i), 0))
    halo = pl.BlockSpec((HALO, N_CV), lambda i: (jnp.maximum(rt(i) * (tm // HALO) - 1, 0), 0))
    return pl.pallas_call(
        body, name="merge_bwd", grid=(nt,),
        in_specs=[row(D_MODEL), row(HG_WIDTH), row(CONV_WIDTH), row(2 * D_MODEL), row(N_CV), halo,
                  _resident(wa.shape), _resident(wb.shape), _resident(wo.shape), _full((CONV_K, CONV_WIDTH))],
        out_specs=[row(2 * D_MODEL), row(D_MODEL), row(D_MODEL), row(HG_WIDTH), row(N_CV),
                   _full((CONV_K, CONV_WIDTH))],
        out_shape=[jax.ShapeDtypeStruct((T, 2 * D_MODEL), BF16), jax.ShapeDtypeStruct((T, D_MODEL), BF16),
                   jax.ShapeDtypeStruct((T, D_MODEL), BF16), jax.ShapeDtypeStruct((T, HG_WIDTH), F32),
                   jax.ShapeDtypeStruct((T, N_CV), BF16), jax.ShapeDtypeStruct((CONV_K, CONV_WIDTH), F32)],
        scratch_shapes=[pltpu.VMEM((HALO, CONV_WIDTH), F32), pltpu.VMEM((HALO, CONV_WIDTH), F32)],
        compiler_params=_params(("arbitrary",)),
    )(dx1, og, cvo, gt, cv, cv, wa, wb, wo, conv_w)


def _drop_operands(body, first, count):
    def wrapped(*refs):
        return body(*refs[:first], *refs[first + count:])
    return wrapped


def _hg_bwd(dog, hg, o, st, low, gn, after=()):
    T = hg.shape[0]
    tb = min(512, T)
    sb = min(256, tb)
    nb = T // tb
    nc = tb // CHUNK
    wid = HEADS_PER_STEP * HEAD_DIM

    def body(q_ref, f_ref, i_ref, g_ref, low_ref, gn_ref, o_ref, dog_ref, st_ref,
             dhg_ref, dlow_ref, dgn_ref,
             ds_scr, dqi_scr, dko_scr, dv_scr, dd_scr, dqh_scr, dkh_scr):
        h = pl.program_id(0)
        t = pl.program_id(1)
        dq_ref, df_ref, di_ref, dg_ref = (dhg_ref.at[:, p * HG_WIDTH:(p + 1) * HG_WIDTH] for p in range(4))

        @pl.when(t == 0)
        def _():
            ds_scr[...] = jnp.zeros_like(ds_scr)
            dlow_ref[...] = jnp.zeros_like(dlow_ref)

        @pl.when((t == 0) & (h == 0))
        def _():
            dgn_ref[...] = jnp.zeros_like(dgn_ref)

        pos = _chunk_pos((tb, HEAD_DIM))
        mask = _intra_mask(sb)
        gnv = gn_ref[...]
        lanes = [slice(hh * HEAD_DIM, (hh + 1) * HEAD_DIM) for hh in range(HEADS_PER_STEP)]
        heads = []
        for hh, ln in enumerate(lanes):
            lb, lb1 = _lower_bound(low_ref.at[:, ln])
            qr = q_ref[:, ln]
            sq, q, sg, f, k, e_qa, e_ka, e_b, e_ko, dec = _hg_gates(qr, f_ref[:, ln], lb, pos, tb)

            gr = g_ref[:, ln]
            o = o_ref[:, ln]
            dog_v = dog_ref[:, ln]
            sgr = _sigmoid(gr)
            r = lax.rsqrt(jnp.mean(o * o, axis=-1, keepdims=True) + EPS)
            oh = o * r
            dg_ref[:, ln] = (dog_v * (oh * gnv) * (sgr * (1.0 + gr * (1.0 - sgr)))).astype(BF16)
            don = dog_v * (gr * sgr)
            dgn_ref[...] += jnp.sum(don * oh, axis=0, keepdims=True)
            w = don * gnv
            do = (r * (w - oh * jnp.mean(w * oh, axis=-1, keepdims=True))).astype(BF16)

            qh = (q * e_qa).astype(BF16)
            kh = (k * e_ka).astype(BF16)
            qi = (q * e_b).astype(BF16)
            ko = (k * e_ko).astype(BF16)
            vb = i_ref[:, ln].astype(BF16)

            for s in range(tb // sb):
                sl = slice(s * sb, (s + 1) * sb)
                p = jnp.where(mask, _mm_nt(qh[sl], kh[sl]), 0.0).astype(BF16)
                dp = jnp.where(mask, _mm_nt(do[sl], vb[sl]), 0.0).astype(BF16)
                dv_scr[sl, ln] = _mm_tn(p, do[sl])
                dqh_scr[sl, ln] = _mm(dp, kh[sl])
                dkh_scr[sl, ln] = _mm_tn(dp, qh[sl])
            heads.append(dict(lb=lb, lb1=lb1, qr=qr, sq=sq, q=q, sg=sg, f=f, k=k, e_qa=e_qa, e_ka=e_ka, e_b=e_b,
                              e_ko=e_ko, dec=dec, do=do, qi=qi, ko=ko, vb=vb, ds=ds_scr[hh]))

        for c in reversed(range(nc)):
            sl = slice(c * CHUNK, (c + 1) * CHUNK)
            for hh, ln in enumerate(lanes):
                hd = heads[hh]
                ds = hd["ds"]
                st_c = st_ref[hh, c]
                dqi_scr[sl, ln] = _mm(hd["do"][sl], st_c)
                dko_scr[sl, ln] = _mm(hd["vb"][sl], ds)
                dv_scr[sl, ln] = dv_scr[sl, ln] + _mm_nt(hd["ko"][sl], ds)
                dec_c = hd["dec"][c]
                dd_scr[sl, ln] = jnp.broadcast_to(dec_c * jnp.sum(ds * st_c, axis=0, keepdims=True),
                                                  (CHUNK, HEAD_DIM))
                hd["ds"] = dec_c * ds + _mm_tn(hd["do"][sl], hd["qi"][sl])

        for hh, ln in enumerate(lanes):
            hd = heads[hh]
            ds_scr[hh] = hd["ds"]
            q, k, lb = hd["q"], hd["k"], hd["lb"]
            dko_e = dko_scr[:, ln] * hd["e_ko"]
            dq = dqh_scr[:, ln] * hd["e_qa"] + dqi_scr[:, ln] * hd["e_b"]
            dk = dkh_scr[:, ln] * hd["e_ka"] + dko_e
            kd3 = (k * dko_e).reshape(nc, CHUNK, HEAD_DIM)
            last = jnp.broadcast_to(jnp.sum(kd3, axis=1, keepdims=True), kd3.shape).reshape(tb, HEAD_DIM)
            db = q * dq - k * dk + jnp.where(pos == CHUNK - 1, dd_scr[:, ln] + last, 0.0)
            dlg = _chunk_rev_cumsum(db, pos)
            dfv = dlg / hd["f"] - dk
            s_low = jnp.sum(dfv * (1.0 - hd["sg"]), axis=0, keepdims=True)
            dlow_ref[0:1, ln] += s_low * lb * (1.0 - lb)
            dlow_ref[1:2, ln] += -s_low * lb * hd["lb1"]
            df_ref[:, ln] = (dfv * (1.0 - lb) * hd["sg"] * (1.0 - hd["sg"])).astype(BF16)
            dq_ref[:, ln] = (dq * Q_SCALE * (hd["sq"] * (1.0 + hd["qr"] * (1.0 - hd["sq"])))).astype(BF16)
            di_ref[:, ln] = dv_scr[:, ln].astype(BF16)

    rt = lambda t: nb - 1 - t
    col = lambda p: pl.BlockSpec((tb, wid), lambda h, t: (rt(t), p * HEAD_GROUPS + h))
    hcol = pl.BlockSpec((tb, wid), lambda h, t: (rt(t), h))
    assert HEAD_GROUPS == 1
    tile = pltpu.VMEM((tb, wid), F32)
    return pl.pallas_call(
        _drop_operands(body, 9, len(after)), name="hg_bwd", grid=(HEAD_GROUPS, nb),
        in_specs=[col(0), col(1), col(2), col(3), pl.BlockSpec((2, wid), lambda h, t: (0, h)),
                  pl.BlockSpec((1, HEAD_DIM), lambda h, t: (0, 0)), hcol, hcol,
                  pl.BlockSpec((HEADS_PER_STEP, nc, HEAD_DIM, HEAD_DIM), lambda h, t: (h, rt(t), 0, 0))]
                 + [HBM_SPEC] * len(after),
        out_specs=[pl.BlockSpec((tb, N_HG), lambda h, t: (rt(t), 0)), pl.BlockSpec((2, wid), lambda h, t: (0, h)),
                   pl.BlockSpec((1, HEAD_DIM), lambda h, t: (0, 0))],
        out_shape=[jax.ShapeDtypeStruct((T, N_HG), BF16), jax.ShapeDtypeStruct((2, HG_WIDTH), F32),
                   jax.ShapeDtypeStruct((1, HEAD_DIM), F32)],
        scratch_shapes=[pltpu.VMEM((HEADS_PER_STEP, HEAD_DIM, HEAD_DIM), F32), tile, tile, tile, tile, tile, tile],
        compiler_params=_params(("arbitrary", "arbitrary")),
    )(hg, hg, hg, hg, low, gn, o, dog, st, *after)


def _in_bwd(dparts, w_in, x, dx1, g, after=()):
    T = x.shape[0]
    tm = min(512, T)
    widths = [p.shape[1] for p in dparts]
    offs = [sum(widths[:i]) for i in range(len(widths))]
    n = len(dparts)

    def body(*refs):
        d_refs = refs[:n]
        w_ref, x_ref, dx1_ref, g_ref, dx_ref, dgn_ref = refs[n:]

        @pl.when(pl.program_id(0) == 0)
        def _():
            dgn_ref[...] = jnp.zeros_like(dgn_ref)

        dh = None
        for d_ref, off, wd in zip(d_refs, offs, widths):
            part = _mm(d_ref[...], w_ref[off:off + wd, :])
            dh = part if dh is None else dh + part
        xv = x_ref[...]
        r = lax.rsqrt(jnp.mean(xv * xv, axis=-1, keepdims=True) + EPS)
        xh = xv * r
        dgn_ref[...] += jnp.sum(dh * xh, axis=0, keepdims=True)
        w = dh * g_ref[...]
        dx_ref[...] = dx1_ref[...] + r * (w - xh * jnp.mean(w * xh, axis=-1, keepdims=True))

    row = lambda m: pl.BlockSpec((tm, m), lambda i: (i, 0))
    return pl.pallas_call(
        _drop_operands(body, n + 4, len(after)), name="in_bwd", grid=(T // tm,),
        in_specs=[row(wd) for wd in widths] + [_resident(w_in.shape), row(D_MODEL), row(D_MODEL), _full((1, D_MODEL))]
                 + [HBM_SPEC] * len(after),
        out_specs=[row(D_MODEL), _full((1, D_MODEL))],
        out_shape=[jax.ShapeDtypeStruct((T, D_MODEL), F32), jax.ShapeDtypeStruct((1, D_MODEL), F32)],
        compiler_params=_params(("arbitrary",)),
    )(*dparts, w_in, x, dx1, g, *after)


def _wgrad_ffn(h2t, dgate, dup, dx2t, act, tn=256):
    M, T = h2t.shape
    N = dgate.shape[1]

    def body(h_ref, x_ref, dg_ref, du_ref, act_ref, og_ref, ou_ref, od_ref):
        h = h_ref[...]
        og_ref[...] = _mm(h, dg_ref[...]).T.astype(BF16)
        ou_ref[...] = _mm(h, du_ref[...]).T.astype(BF16)
        od_ref[...] = _mm(x_ref[...], act_ref[...]).T.astype(BF16)

    rhs = pl.BlockSpec((T, tn), lambda j: (0, j))
    out_spec = pl.BlockSpec((tn, M), lambda j: (j, 0))
    out = jax.ShapeDtypeStruct((N, M), BF16)
    return pl.pallas_call(
        body, name="wgrad_ffn", grid=(N // tn,),
        in_specs=[_resident((M, T)), _resident((M, T)), rhs, rhs, rhs], out_specs=[out_spec] * 3,
        out_shape=[out, out, out],
        compiler_params=_params(("parallel",)),
    )(h2t, dx2t, dgate, dup, act)


def _wgrad_out_branches(ogt, dya, cvot, dyb, mgt, dx1, after=()):
    M, T = ogt.shape
    N = dya.shape[1]
    c = N // N_DEV
    per = 2
    tn = per * c

    def body(at_ref, da_ref, bt_ref, db_ref, mt_ref, dx_ref, oa_ref, ob_ref, oo_ref):
        ga = _mm(at_ref[...], da_ref[...])
        gb = _mm(bt_ref[...], db_ref[...])
        for s in range(per):
            oa_ref[s] = ga[:, s * c:(s + 1) * c].astype(BF16)
            ob_ref[s] = gb[:, s * c:(s + 1) * c].astype(BF16)
        oo_ref[...] = _mm(mt_ref[...], dx_ref[...]).astype(BF16)

    rhs = pl.BlockSpec((T, tn), lambda j: (0, j))
    owners = pl.BlockSpec((per, M, c), lambda j: (j, 0, 0))
    out = jax.ShapeDtypeStruct((N_DEV, M, c), BF16)
    return pl.pallas_call(
        _drop_operands(body, 6, len(after)), name="wgrad_out_branches", grid=(N // tn,),
        in_specs=[_resident((M, T)), rhs, _resident((M, T)), rhs, _resident(mgt.shape), rhs] + [HBM_SPEC] * len(after),
        out_specs=[owners, owners, pl.BlockSpec((mgt.shape[0], tn), lambda j: (0, j))],
        out_shape=[out, out, jax.ShapeDtypeStruct((mgt.shape[0], dx1.shape[1]), BF16)],
        compiler_params=_params(("parallel",)),
    )(ogt, dya, cvot, dyb, mgt, dx1, *after)


def _wgrad_in(ht, dparts, after=(), riders=()):
    M, T = ht.shape
    tn = 512
    nblk = [p.shape[1] // tn for p in dparts]
    start = [sum(nblk[:i]) for i in range(len(nblk))]
    n = len(dparts)
    steps = sum(nblk)
    nr = len(riders)
    rows = [r[0].shape[0] // steps for r in riders]
    assert all(r[0].shape[0] == rr * steps and rr % 16 == 0 for r, rr in zip(riders, rows))

    def body(a_ref, *refs):
        d_refs = refs[:n]
        rider_in = refs[n:n + 4 * nr]
        o_ref = refs[n + 4 * nr]
        rider_out = refs[n + 4 * nr + 1:]
        j = pl.program_id(0)
        for d_ref, s, nb in zip(d_refs, start, nblk):
            @pl.when((j >= s) & (j < s + nb))
            def _():
                o_ref[...] = _mm(a_ref[...], d_ref[...]).T.astype(BF16)
        for i in range(nr):
            w_ref, p_ref, m_ref, v_ref = rider_in[4 * i:4 * i + 4]
            g = p_ref[0].astype(F32)
            for k in range(1, 4):
                g = g + p_ref[k].astype(F32)
            delta, m_new, v_new = _adamw_math(w_ref[...], g, m_ref[...], v_ref[...])
            for k, val in enumerate((g, delta, m_new, v_new)):
                rider_out[4 * i + k][...] = val

    def piece_spec(s, nb):
        return pl.BlockSpec((T, tn), lambda j: (0, jnp.clip(j - s, 0, nb - 1)))

    rider_specs, rider_out_specs, rider_out_shape, rider_args = [], [], [], []
    for (w, parts, m, v), rr in zip(riders, rows):
        blk = pl.BlockSpec((rr, w.shape[1]), lambda j: (j, 0))
        rider_specs += [blk, pl.BlockSpec((4, rr, w.shape[1]), lambda j: (0, j, 0)), blk, blk]
        rider_out_specs += [blk] * 4
        rider_out_shape += [jax.ShapeDtypeStruct(w.shape, F32)] * 4
        rider_args += [w, parts, m, v]
    outs = pl.pallas_call(
        _drop_operands(body, 1 + n + 4 * nr, len(after)), name="wgrad_in", grid=(steps,),
        in_specs=[_resident((M, T))] + [piece_spec(s, nb) for s, nb in zip(start, nblk)] + rider_specs
                 + [HBM_SPEC] * len(after),
        out_specs=[pl.BlockSpec((tn, M), lambda j: (j, 0))] + rider_out_specs,
        out_shape=[jax.ShapeDtypeStruct((steps * tn, M), BF16)] + rider_out_shape,
        compiler_params=_params(("parallel",)),
    )(ht, *dparts, *rider_args, *after)
    return outs[0], [outs[1 + 4 * i:5 + 4 * i] for i in range(nr)]


def _adamw_math(w, g, m, v):
    m = ADAM_B1 * m + (1.0 - ADAM_B1) * g
    v = ADAM_B2 * v + (1.0 - ADAM_B2) * (g * g)
    m_hat = m / (1.0 - ADAM_B1 ** ADAM_STEP)
    v_hat = v / (1.0 - ADAM_B2 ** ADAM_STEP)
    delta = -ADAM_LR * (m_hat / (jnp.sqrt(v_hat) + ADAM_EPS) + ADAM_WD * w)
    return delta, m, v


def _adamw_sum(name, ws, parts, ms, vs):
    n = len(ws)
    steps = min(_row_steps(w.shape[0]) for w in ws)
    rows = [w.shape[0] // steps for w in ws]

    def body(*refs):
        w_refs, p_refs, m_refs, v_refs = (refs[k * n:(k + 1) * n] for k in range(4))
        out_refs = refs[4 * n:]
        for i in range(n):
            g = p_refs[i][0].astype(F32)
            for k in range(1, 4):
                g = g + p_refs[i][k].astype(F32)
            delta, m_new, v_new = _adamw_math(w_refs[i][...], g, m_refs[i][...], v_refs[i][...])
            for k, val in enumerate((g, delta, m_new, v_new)):
                out_refs[4 * i + k][...] = val

    blk = [pl.BlockSpec((r, w.shape[1]), lambda s: (s, 0)) for r, w in zip(rows, ws)]
    pblk = [pl.BlockSpec((4, r, w.shape[1]), lambda s: (0, s, 0)) for r, w in zip(rows, ws)]
    out_specs, out_shape = [], []
    for b, w in zip(blk, ws):
        out_specs += [b] * 4
        out_shape += [jax.ShapeDtypeStruct(w.shape, F32)] * 4
    flat = pl.pallas_call(
        body, name=name, grid=(steps,),
        in_specs=blk + pblk + blk + blk, out_specs=out_specs, out_shape=out_shape,
        compiler_params=_params(("parallel",)),
    )(*ws, *parts, *ms, *vs)
    return [flat[4 * i:4 * i + 4] for i in range(n)]


_SMALL_SLOTS = (("norm_mix_g", 0, 1, 1024), ("norm_ffn_g", 1, 1, 1024), ("norm_final_g", 2, 1, 1024),
                ("lower_bounds", 3, 2, 512), ("hg_norm_g", 5, 1, 128), ("loss", 6, 1, 128), ("conv_w", 8, 3, 512))
_SMALL_PARAMS = tuple(s for s in _SMALL_SLOTS if s[0] != "loss")
CONV_SHARD = CONV_WIDTH // N_DEV


def _small_pack(small):
    def body(*refs):
        out = refs[-1]
        out[...] = jnp.zeros_like(out)
        for ref, (_, row, rows, lanes) in zip(refs[:-1], _SMALL_SLOTS):
            out[row:row + rows, 0:lanes] = ref[...]

    vmem = pl.BlockSpec(memory_space=pltpu.VMEM)
    return pl.pallas_call(
        body, name="small_pack", in_specs=[vmem] * len(_SMALL_SLOTS), out_specs=vmem,
        out_shape=jax.ShapeDtypeStruct((SMALL_ROWS, 1024), F32),
    )(*[small[name] for name, _, _, _ in _SMALL_SLOTS])


def _small_update(gathered, dev, w, m, v):
    n = len(_SMALL_PARAMS)

    def body(dev_ref, g_ref, *refs):
        w_refs, m_refs, v_refs = refs[:n], refs[n:2 * n], refs[2 * n:3 * n]
        loss_ref, out_refs, sum_scr = refs[3 * n], refs[3 * n + 1:-1], refs[-1]
        total = g_ref[0]
        for k in range(1, N_DEV):
            total = total + g_ref[k]
        sum_scr[...] = total
        loss_ref[...] = sum_scr[6:7, 0:128]
        for p, (name, row, rows, lanes) in enumerate(_SMALL_PARAMS):
            if name == "conv_w":
                for r in range(rows):
                    g = sum_scr[row + r:row + r + 1, 0:CONV_SHARD]
                    for s in range(1, N_DEV):
                        mine = sum_scr[row + r:row + r + 1, s * CONV_SHARD:(s + 1) * CONV_SHARD]
                        g = jnp.where(dev_ref[0] == s, mine, g)
                    delta, m_new, v_new = _adamw_math(w_refs[p][r], g, m_refs[p][r], v_refs[p][r])
                    out_refs[4 * p][r] = g
                    out_refs[4 * p + 1][r] = delta
                    out_refs[4 * p + 2][r] = m_new
                    out_refs[4 * p + 3][r] = v_new
                continue
            g = sum_scr[row:row + rows, 0:lanes]
            delta, m_new, v_new = _adamw_math(w_refs[p][...], g, m_refs[p][...], v_refs[p][...])
            out_refs[4 * p][...] = g
            out_refs[4 * p + 1][...] = delta
            out_refs[4 * p + 2][...] = m_new
            out_refs[4 * p + 3][...] = v_new

    vmem = pl.BlockSpec(memory_space=pltpu.VMEM)
    outs = [jax.ShapeDtypeStruct((1, 128), F32)]
    for a in w:
        outs += [jax.ShapeDtypeStruct(a.shape, F32)] * 4
    return pl.pallas_call(
        body, name="small_update",
        in_specs=[pl.BlockSpec(memory_space=pltpu.SMEM)] + [vmem] * (1 + 3 * n), out_specs=[vmem] * len(outs),
        out_shape=outs, scratch_shapes=[pltpu.VMEM((SMALL_ROWS, 1024), F32)],
    )(dev, gathered, *w, *m, *v)


def _row_steps(rows):
    for steps in (4, 2):
        if rows % (16 * steps) == 0:
            return steps
    return 1


def _pair_sum(name, by_owner, got, core, after=()):
    n = len(got)

    def body(core_ref, *refs):
        for a_ref, b_ref, o_ref in zip(refs[:n], refs[n:2 * n], refs[2 * n:]):
            o_ref[...] = (a_ref[...].astype(F32) + b_ref[...].astype(F32)).astype(BF16)

    def blk(g):
        return pl.BlockSpec((None,) + g.shape[1:], lambda k, core_ref: (k, 0, 0))

    def mine(g):
        return pl.BlockSpec((None,) + g.shape[1:], lambda k, core_ref: (2 * k + core_ref[0], 0, 0))

    return pl.pallas_call(
        _drop_operands(body, 1 + 2 * n, len(after)), name=name,
        grid_spec=pltpu.PrefetchScalarGridSpec(
            num_scalar_prefetch=1, grid=(4,),
            in_specs=[mine(g) for g in got] + [blk(g) for g in got] + [HBM_SPEC] * len(after),
            out_specs=[blk(g) for g in got]),
        out_shape=[jax.ShapeDtypeStruct(g.shape, BF16) for g in got],
        compiler_params=_params(("parallel",)),
    )(core, *by_owner, *got, *after)


MESH = pl.DeviceIdType.MESH
HBM_SPEC = pl.BlockSpec(memory_space=pl.ANY)


def _handshake(peers):
    barrier = pltpu.get_barrier_semaphore()
    for peer in peers:
        pl.semaphore_signal(barrier, inc=1, device_id=peer, device_id_type=MESH)
    pl.semaphore_wait(barrier, len(peers))


def _comm_call(body, name, operands, out_shape, scratch, collective_id):
    if collective_id is None:
        return pl.pallas_call(body, name=name, in_specs=[HBM_SPEC] * len(operands), out_specs=[HBM_SPEC] * len(out_shape),
                              out_shape=out_shape, scratch_shapes=scratch)(*operands)
    return pl.kernel(body, out_type=out_shape, mesh=plsc.ScalarSubcoreMesh(axis_name="sequencer", num_cores=1),
                     scratch_types=scratch, name=name,
                     compiler_params=pltpu.CompilerParams(collective_id=collective_id))(*operands)


def _all_gather(name, blocks, collective_id=None, after=()):
    n = len(blocks)
    na = len(after)

    def body(*refs):
        x_refs, out_refs = refs[:n], refs[n + na:2 * n + na]
        send_sems, recv_sems, local_sems = refs[2 * n + na:]
        x, y, c = lax.axis_index("x"), lax.axis_index("y"), lax.axis_index("c")
        me, sibling = (x, y, c), (x, y, 1 - c)
        chips = [(1 - x, y), (x, 1 - y), (1 - x, 1 - y)]
        if collective_id is not None:
            _handshake([sibling] + [(*chip, c) for chip in chips])

        def slot(i, px, py, pc):
            return out_refs[i].at[4 * px + 2 * py + pc]

        def copy(i, k, blk, to, src=None):
            return pltpu.make_async_remote_copy(
                src_ref=slot(i, *blk) if src is None else src, dst_ref=slot(i, *blk),
                send_sem=send_sems.at[7 * i + k], recv_sem=recv_sems.at[7 * i + k], device_id=to, device_id_type=MESH)

        mine = [pltpu.make_async_copy(x_refs[i], slot(i, *me), local_sems.at[i]) for i in range(n)]
        for cp in mine:
            cp.start()
        first = []
        for i in range(n):
            first.append(copy(i, 0, me, sibling, src=x_refs[i]))
            first += [copy(i, 1 + j, me, (*chip, c), src=x_refs[i]) for j, chip in enumerate(chips)]
        for cp in first:
            cp.start()
        passed = []
        for i in range(n):
            for j, chip in enumerate(chips):
                copy(i, 1 + j, (*chip, c), me).wait_recv()
                passed.append(copy(i, 4 + j, (*chip, c), sibling))
                passed[-1].start()
        for i in range(n):
            copy(i, 0, sibling, me).wait_recv()
            for j, chip in enumerate(chips):
                copy(i, 4 + j, (*chip, 1 - c), me).wait_recv()
        for cp in first + passed:
            cp.wait_send()
        for cp in mine:
            cp.wait()

    return _comm_call(
        body, name, list(blocks) + list(after), [jax.ShapeDtypeStruct((N_DEV,) + b.shape, b.dtype) for b in blocks],
        [pltpu.SemaphoreType.DMA((7 * n,)), pltpu.SemaphoreType.DMA((7 * n,)), pltpu.SemaphoreType.DMA((n,))],
        collective_id)


def _sibling_swap(name, by_owner, collective_id=None, after=()):
    n = len(by_owner)
    na = len(after)

    def body(*refs):
        x_refs, out_refs = refs[:n], refs[n + na:2 * n + na]
        send_sems, recv_sems = refs[2 * n + na:]
        x, y, c = lax.axis_index("x"), lax.axis_index("y"), lax.axis_index("c")
        if collective_id is not None:
            _handshake([(x, y, 1 - c)])
        copies = []
        for i in range(n):
            for k in range(4):
                copies.append(pltpu.make_async_remote_copy(
                    src_ref=x_refs[i].at[2 * k + 1 - c], dst_ref=out_refs[i].at[k],
                    send_sem=send_sems.at[4 * i + k], recv_sem=recv_sems.at[4 * i + k],
                    device_id=(x, y, 1 - c), device_id_type=MESH))
        for cp in copies:
            cp.start()
        for cp in copies:
            cp.wait()

    return _comm_call(
        body, name, list(by_owner) + list(after),
        [jax.ShapeDtypeStruct((4,) + b.shape[1:], b.dtype) for b in by_owner],
        [pltpu.SemaphoreType.DMA((4 * n,)), pltpu.SemaphoreType.DMA((4 * n,))], collective_id)


def _chip_exchange(name, sums, collective_id=None, after=()):
    n = len(sums)
    na = len(after)

    def body(*refs):
        x_refs, out_refs = refs[:n], refs[n + na:2 * n + na]
        send_sems, recv_sems, local_sems = refs[2 * n + na:]
        x, y, c = lax.axis_index("x"), lax.axis_index("y"), lax.axis_index("c")
        chips = [(1 - x, y), (x, 1 - y), (1 - x, 1 - y)]
        my_chip = 2 * x + y
        if collective_id is not None:
            _handshake([(cx, cy, c) for cx, cy in chips])
        mine = [pltpu.make_async_copy(x_refs[i].at[my_chip], out_refs[i].at[my_chip], local_sems.at[i])
                for i in range(n)]
        for cp in mine:
            cp.start()
        sends = []
        for i in range(n):
            for j, (cx, cy) in enumerate(chips):
                sends.append(pltpu.make_async_remote_copy(
                    src_ref=x_refs[i].at[2 * cx + cy], dst_ref=out_refs[i].at[my_chip],
                    send_sem=send_sems.at[3 * i + j], recv_sem=recv_sems.at[3 * i + j],
                    device_id=(cx, cy, c), device_id_type=MESH))
        for cp in sends:
            cp.start()
        for i in range(n):
            for j, (cx, cy) in enumerate(chips):
                pltpu.make_async_remote_copy(
                    src_ref=x_refs[i].at[my_chip], dst_ref=out_refs[i].at[2 * cx + cy],
                    send_sem=send_sems.at[3 * i + j], recv_sem=recv_sems.at[3 * i + j],
                    device_id=(cx, cy, c), device_id_type=MESH).wait_recv()
        for cp in sends:
            cp.wait_send()
        for cp in mine:
            cp.wait()

    return _comm_call(
        body, name, list(sums) + list(after), [jax.ShapeDtypeStruct(s.shape, s.dtype) for s in sums],
        [pltpu.SemaphoreType.DMA((3 * n,)), pltpu.SemaphoreType.DMA((3 * n,)), pltpu.SemaphoreType.DMA((n,))],
        collective_id)


def _cast_shards(shards):
    n = len(shards)

    def body(*refs):
        for i in range(n):
            refs[n + i][...] = refs[i][...].astype(BF16)

    vmem = pl.BlockSpec(memory_space=pltpu.VMEM)
    return pl.pallas_call(
        body, name="cast_shards", in_specs=[vmem] * n, out_specs=[vmem] * n,
        out_shape=[jax.ShapeDtypeStruct(s.shape, BF16) for s in shards],
        compiler_params=pltpu.CompilerParams(vmem_limit_bytes=VMEM_LIMIT_V7X),
    )(*shards)


BIG = ("w_in", "w_branch_a", "w_branch_b", "w_out", "w_ffn_gate", "w_ffn_up", "w_ffn_down")


def _local_step(x, target, gains, low, conv_w, wg8, reduce):
    g_mix, g_hg, g_ffn, g_fin = gains
    w_in = wg8["w_in"].reshape(N_IN, D_MODEL)
    wg = wg8["w_ffn_gate"].reshape(D_FF, D_MODEL)
    wu = wg8["w_ffn_up"].reshape(D_FF, D_MODEL)
    wa, wb = wg8["w_branch_a"], wg8["w_branch_b"]
    wo = wg8["w_out"].reshape(D_MODEL, D_MODEL)
    wd = wg8["w_ffn_down"].reshape(D_FF, D_MODEL)

    ht, hg, cv, gt, cvo, cvot, o, og, ogt, st = _fwd_in(x, g_mix, w_in, conv_w, low, g_hg)
    x1, mgt = _merge_fwd(og, cvo, gt, x, wa, wb, wo)
    h2t, gate, up, act, loss, d_gfin, dx2, dx2t = _ffn_fwd_loss(x1, g_ffn, wg, wu, wd, target, g_fin)

    dgate, dup, dx1, d_gffn = _ffn_bwd(dx2, x1, gate, up, g_ffn, wg, wu, wd)
    d_wg, d_wu, d_wd = _wgrad_ffn(h2t, dgate, dup, dx2t, act)
    by_owner_ffn = lambda a: a.reshape(N_DEV, D_FF // N_DEV, D_MODEL)
    ffn = dict(w_ffn_down=by_owner_ffn(d_wd), w_ffn_gate=by_owner_ffn(d_wg), w_ffn_up=by_owner_ffn(d_wu))
    dgt, dya, dyb, dog, dcv, d_conv = _merge_bwd(dx1, og, cvo, gt, cv, wa, wb, wo, conv_w)
    sums_ffn, got_ffn = reduce.begin(ffn, sum_after=[dya])
    late = ("w_ffn_gate", "w_ffn_up")
    parts_ffn, updated_ffn = reduce.finish(ffn, sums_ffn, defer=late)
    grad_a, grad_b, grad_o = _wgrad_out_branches(ogt, dya, cvot, dyb, mgt, dx1, after=sums_ffn[:1])
    out = dict(w_out=grad_o.reshape(N_DEV, D_MODEL // N_DEV, D_MODEL), w_branch_a=grad_a, w_branch_b=grad_b)
    dhg, d_low, d_ghg = _hg_bwd(dog, hg, o, st, low, g_hg, after=list(sums_ffn) + [out["w_out"]])
    sums_out, got_out = reduce.begin(out, after=[parts_ffn[0], dhg], sum_after=updated_ffn)
    parts_out, updated_out = reduce.finish(out, sums_out)
    dparts = [dhg, dcv, dgt]
    d_w_in_t, ridden = _wgrad_in(ht, dparts, after=sums_out[:1],
                                 riders=reduce.riders(late, dict(zip(ffn, parts_ffn))))
    reduce.record(late, ridden)
    w_in_grad = dict(w_in=d_w_in_t.reshape(N_DEV, N_IN // N_DEV, D_MODEL))
    sums_in, _ = reduce.begin(w_in_grad, after=parts_out[:1], sum_after=updated_out)
    parts_in, _ = reduce.finish(w_in_grad, sums_in)
    grad_x, d_gmix = _in_bwd(dparts, w_in, x, dx1, g_mix, after=list(parts_out[:1]) + list(sums_in))
    small = dict(norm_mix_g=d_gmix, norm_ffn_g=d_gffn, norm_final_g=d_gfin, lower_bounds=d_low, hg_norm_g=d_ghg,
                 conv_w=d_conv, loss=loss)
    return grad_x, small, parts_in


def _conv_shard_rows(a):
    return jnp.pad(a, ((0, 5), (0, 64)))


def kernel(x, norm_mix_g, w_in, lower_bounds, hg_norm_g, conv_w, w_branch_a, w_branch_b, w_out, norm_ffn_g, w_ffn_gate, w_ffn_up, w_ffn_down, norm_final_g, loss_target, m_norm_mix_g, m_w_in, m_lower_bounds, m_hg_norm_g, m_conv_w, m_w_branch_a, m_w_branch_b, m_w_out, m_norm_ffn_g, m_w_ffn_gate, m_w_ffn_up, m_w_ffn_down, m_norm_final_g, v_norm_mix_g, v_w_in, v_lower_bounds, v_hg_norm_g, v_conv_w, v_w_branch_a, v_w_branch_b, v_w_out, v_norm_ffn_g, v_w_ffn_gate, v_w_ffn_up, v_w_ffn_down, v_norm_final_g):
    cx, cy, cc = lax.axis_index("x"), lax.axis_index("y"), lax.axis_index("c")
    my_dev = 4 * cx + 2 * cy + cc

    def tr(a):
        return a[0].T

    big = dict(w_in=tr(w_in), w_branch_a=w_branch_a[0], w_branch_b=w_branch_b[0], w_out=w_out[0],
               w_ffn_gate=tr(w_ffn_gate), w_ffn_up=tr(w_ffn_up), w_ffn_down=w_ffn_down[0])
    big_m = dict(w_in=tr(m_w_in), w_branch_a=m_w_branch_a[0], w_branch_b=m_w_branch_b[0], w_out=m_w_out[0],
                 w_ffn_gate=tr(m_w_ffn_gate), w_ffn_up=tr(m_w_ffn_up), w_ffn_down=m_w_ffn_down[0])
    big_v = dict(w_in=tr(v_w_in), w_branch_a=v_w_branch_a[0], w_branch_b=v_w_branch_b[0], w_out=v_w_out[0],
                 w_ffn_gate=tr(v_w_ffn_gate), w_ffn_up=tr(v_w_ffn_up), w_ffn_down=v_w_ffn_down[0])
    transposed = ("w_in", "w_ffn_gate", "w_ffn_up")

    shards = dict(zip(BIG, _cast_shards([big[n] for n in BIG])))
    first = _all_gather("gather_w_in", [shards["w_in"], _conv_shard_rows(conv_w[0])])
    ids = iter(range(1, 16))
    mid = _all_gather("gather_mid", [shards[n] for n in BIG[1:4]], collective_id=next(ids), after=first[1:])
    ffn = _all_gather("gather_ffn", [shards[n] for n in BIG[4:]], collective_id=next(ids), after=first[1:])
    wg8 = dict(zip(BIG, [first[0]] + list(mid) + list(ffn)))
    conv_full = first[1][:, :3, :64].transpose(1, 0, 2).reshape(3, CONV_WIDTH)

    core = cc.reshape(1).astype(jnp.int32)
    outs = {}

    class Reduce:
        @staticmethod
        def begin(grads, after=(), sum_after=()):
            names = list(grads)
            by_owner = [grads[n] for n in names]
            got = _sibling_swap("sibling_swap_" + names[0], by_owner, collective_id=next(ids), after=after)
            sums = _pair_sum("pair_sum_" + names[0], by_owner, got, core, after=sum_after)
            return sums, got

        @staticmethod
        def finish(grads, chip_sums, after=(), defer=()):
            names = list(grads)
            parts = _chip_exchange("chip_exchange_" + names[0], chip_sums, collective_id=next(ids), after=after)
            now = [n for n in names if n not in defer]
            updated = _adamw_sum("adamw_" + now[0], [big[n] for n in now],
                                 [p for n, p in zip(names, parts) if n in now],
                                 [big_m[n] for n in now], [big_v[n] for n in now])
            outs.update(zip(now, updated))
            return parts, [outs[n][1] for n in now]

        @staticmethod
        def riders(names, parts):
            return [(big[n], parts[n], big_m[n], big_v[n]) for n in names]

        @staticmethod
        def record(names, updated):
            outs.update(zip(names, updated))

    gains = (norm_mix_g, hg_norm_g, norm_ffn_g, norm_final_g.reshape(1, D_MODEL))
    grad_x, small, last = _local_step(x[0], loss_target[0], gains, lower_bounds, conv_full, wg8, Reduce)

    small_all = _all_gather("gather_small", [_small_pack(small)], collective_id=next(ids), after=last[:1])

    def small_state(a):
        return [a[0], a[1], a[2].reshape(1, D_MODEL), a[3], a[4], a[5].transpose(1, 0, 2)]

    upd = _small_update(
        small_all[0], my_dev.reshape(1).astype(jnp.int32),
        small_state((norm_mix_g, norm_ffn_g, norm_final_g, lower_bounds, hg_norm_g, conv_w)),
        small_state((m_norm_mix_g, m_norm_ffn_g, m_norm_final_g, m_lower_bounds, m_hg_norm_g, m_conv_w)),
        small_state((v_norm_mix_g, v_norm_ffn_g, v_norm_final_g, v_lower_bounds, v_hg_norm_g, v_conv_w)))
    loss = upd[0][0, 0]
    for p, (name, _, _, _) in enumerate(_SMALL_PARAMS):
        outs[name] = upd[1 + 4 * p:5 + 4 * p]
    outs["norm_final_g"] = [a.reshape(D_MODEL) for a in outs["norm_final_g"]]
    outs["conv_w"] = [a.transpose(1, 0, 2) for a in outs["conv_w"]]

    order = ["norm_mix_g", "w_in", "lower_bounds", "hg_norm_g", "conv_w", "w_branch_a", "w_branch_b", "w_out",
             "norm_ffn_g", "w_ffn_gate", "w_ffn_up", "w_ffn_down", "norm_final_g"]
    result = [loss, grad_x[None]]
    for k in range(4):
        for n in order:
            if n in BIG:
                result.append((outs[n][k].T if n in transposed else outs[n][k])[None])
            else:
                result.append(outs[n][k])
    return tuple(result)
```

```python
import jax
import jax.numpy as jnp
from jax import lax
from jax.experimental import pallas as pl
from jax.experimental.pallas import tpu as pltpu
from jax.experimental.pallas import tpu_sc as plsc

F32 = jnp.float32
BF16 = jnp.bfloat16
STASH = jnp.bfloat16

D_MODEL = 1024
HG_WIDTH = 512
HEAD_DIM = 128
N_HEADS = 4
HEADS_PER_STEP = 4
HEAD_GROUPS = N_HEADS // HEADS_PER_STEP
CONV_WIDTH = 512
CONV_K = 3
D_FF = 2816
CHUNK = 32
EPS = 1e-6
Q_SCALE = HEAD_DIM ** -0.5
N_DEV = 8

ADAM_LR = 0.001
ADAM_B1 = 0.9
ADAM_B2 = 0.999
ADAM_EPS = 1e-08
ADAM_WD = 0.01
ADAM_STEP = 10

VMEM_LIMIT_V7X = 56 * 1024 * 1024
VMEM_LIMIT_LARGE_V7X = 62 * 1024 * 1024

SMALL_ROWS = 16


def _params(sem, vmem=VMEM_LIMIT_V7X):
    return pltpu.CompilerParams(dimension_semantics=sem, vmem_limit_bytes=vmem)


def _mm(a, b):
    return jnp.dot(a.astype(BF16), b.astype(BF16), preferred_element_type=F32)


def _mm_nt(a, b):
    return lax.dot_general(a.astype(BF16), b.astype(BF16), (((1,), (1,)), ((), ())), preferred_element_type=F32)


def _mm_tn(a, b):
    return lax.dot_general(a.astype(BF16), b.astype(BF16), (((0,), (0,)), ((), ())), preferred_element_type=F32)


def _sigmoid(x):
    return 0.5 * jnp.tanh(0.5 * x) + 0.5


def _resident(shape):
    nd = len(shape)
    return pl.BlockSpec(shape, lambda *_: (0,) * nd, pipeline_mode=pl.Buffered(1))


def _full(shape):
    nd = len(shape)
    return pl.BlockSpec(shape, lambda *_: (0,) * nd)


def _shard_cols(w_ref):
    return jnp.concatenate([w_ref[s] for s in range(N_DEV)], axis=1)


N_HG = 4 * HG_WIDTH
N_CV = 3 * CONV_WIDTH
N_GT = 2 * D_MODEL
N_IN = N_HG + N_CV + N_GT


def _col(tm, n):
    return pl.BlockSpec((n, tm), lambda i: (0, i))


HALO = 8


def _fwd_in(x, g, w_in_t, conv_w, low, gn):
    T = x.shape[0]
    tm = min(512, T)
    nc = tm // CHUNK

    def body(x_ref, g_ref, w_ref, cw_ref, low_ref, gn_ref, ht_ref, hg_ref, cv_ref, gt_ref, cvo_ref, cvot_ref,
             o_ref, og_ref, ogt_ref, st_ref, tail_scr, s_scr):
        @pl.when(pl.program_id(0) == 0)
        def _():
            tail_scr[...] = jnp.zeros_like(tail_scr)
            s_scr[...] = jnp.zeros_like(s_scr)

        xv = x_ref[...]
        r = lax.rsqrt(jnp.mean(xv * xv, axis=-1, keepdims=True) + EPS)
        hf = xv * r * g_ref[...]
        h = hf.astype(BF16)
        ht_ref[...] = hf.T.astype(BF16)
        hg_ref[...] = _mm_nt(h, w_ref[:N_HG, :])
        cv = _mm_nt(h, w_ref[N_HG:N_HG + N_CV, :])
        cv_ref[...] = cv.astype(STASH)
        gt_ref[...] = _mm_nt(h, w_ref[N_HG + N_CV:, :]).astype(STASH)

        u = cv[:, :CONV_WIDTH] * cv[:, 2 * CONV_WIDTH:]
        row = lax.broadcasted_iota(jnp.int32, u.shape, 0)
        prev1 = tail_scr[HALO - 1:HALO, :]
        prev2 = tail_scr[HALO - 2:HALO - 1, :]
        u1 = jnp.where(row >= 1, pltpu.roll(u, 1, 0), prev1)
        u2 = jnp.where(row >= 2, pltpu.roll(u, 2, 0), jnp.where(row == 1, prev1, prev2))
        y = cw_ref[0:1, :] * u2 + cw_ref[1:2, :] * u1 + cw_ref[2:3, :] * u
        out = cv[:, CONV_WIDTH:2 * CONV_WIDTH] * y
        cvo_ref[...] = out.astype(BF16)
        cvot_ref[...] = out.T.astype(BF16)
        tail_scr[...] = u[tm - HALO:, :]

        _hg_fwd_tile(hg_ref, low_ref, gn_ref, o_ref, og_ref, ogt_ref, st_ref, s_scr, tm)

    row = lambda n: pl.BlockSpec((tm, n), lambda i: (i, 0))
    return pl.pallas_call(
        body, name="fwd_in", grid=(T // tm,),
        in_specs=[row(D_MODEL), _full((1, D_MODEL)), _resident(w_in_t.shape), _full((CONV_K, CONV_WIDTH)),
                  _full((2, HG_WIDTH)), _full((1, HEAD_DIM))],
        out_specs=[_col(tm, D_MODEL), row(N_HG), row(N_CV), row(N_GT), row(CONV_WIDTH), _col(tm, CONV_WIDTH),
                   row(HG_WIDTH), row(HG_WIDTH), _col(tm, HG_WIDTH),
                   pl.BlockSpec((N_HEADS, nc, HEAD_DIM, HEAD_DIM), lambda i: (0, i, 0, 0))],
        out_shape=[jax.ShapeDtypeStruct((D_MODEL, T), BF16), jax.ShapeDtypeStruct((T, N_HG), F32),
                   jax.ShapeDtypeStruct((T, N_CV), STASH), jax.ShapeDtypeStruct((T, N_GT), STASH),
                   jax.ShapeDtypeStruct((T, CONV_WIDTH), BF16), jax.ShapeDtypeStruct((CONV_WIDTH, T), BF16),
                   jax.ShapeDtypeStruct((T, HG_WIDTH), F32), jax.ShapeDtypeStruct((T, HG_WIDTH), BF16),
                   jax.ShapeDtypeStruct((HG_WIDTH, T), BF16),
                   jax.ShapeDtypeStruct((N_HEADS, T // CHUNK, HEAD_DIM, HEAD_DIM), F32)],
        scratch_shapes=[pltpu.VMEM((HALO, CONV_WIDTH), F32), pltpu.VMEM((N_HEADS, HEAD_DIM, HEAD_DIM), F32)],
        compiler_params=_params(("arbitrary",), vmem=VMEM_LIMIT_LARGE_V7X),
    )(x, g, w_in_t, conv_w, low, gn)


def _chunk_pos(shape):
    return lax.broadcasted_iota(jnp.int32, shape, 0) & (CHUNK - 1)


def _chunk_cumsum(x, pos):
    s = 1
    while s < CHUNK:
        x = x + jnp.where(pos >= s, pltpu.roll(x, s, 0), 0.0)
        s *= 2
    return x


def _chunk_rev_cumsum(x, pos):
    n = x.shape[0]
    s = 1
    while s < CHUNK:
        x = x + jnp.where(pos + s < CHUNK, pltpu.roll(x, n - s, 0), 0.0)
        s *= 2
    return x


def _lower_bound(low_ref):
    l0 = low_ref[0:1, :]
    l1 = low_ref[1:2, :]
    m = jnp.maximum(l0, l1)
    e0 = jnp.exp(l0 - m)
    e1 = jnp.exp(l1 - m)
    return e0 / (e0 + e1), e1 / (e0 + e1)


def _hg_gates(qr, fr, lb, pos, tb):
    sq = _sigmoid(qr)
    q = qr * sq * Q_SCALE
    sg = _sigmoid(fr)
    f = lb + (1.0 - lb) * sg
    k = 1.0 - f
    b = _chunk_cumsum(jnp.log(f), pos)
    b3 = b.reshape(tb // CHUNK, CHUNK, HEAD_DIM)
    anc = b3[:, CHUNK // 2 - 1:CHUNK // 2, :]
    last = b3[:, CHUNK - 1:CHUNK, :]
    d3 = b3 - anc
    e_qa3 = jnp.exp(d3)
    e_ka3 = jnp.exp(-d3)
    e_b3 = e_qa3 * jnp.exp(anc)
    e_ko3 = e_ka3 * jnp.exp(last - anc)
    dec = jnp.exp(last)
    flat = lambda a: a.reshape(tb, HEAD_DIM)
    return sq, q, sg, f, k, flat(e_qa3), flat(e_ka3), flat(e_b3), flat(e_ko3), dec


def _intra_mask(sb):
    r = lax.broadcasted_iota(jnp.int32, (sb, sb), 0)
    c = lax.broadcasted_iota(jnp.int32, (sb, sb), 1)
    return ((r // CHUNK) == (c // CHUNK)) & (c <= r)


def _hg_fwd_tile(hg_ref, low_ref, gn_ref, o_ref, og_ref, ogt_ref, st_ref, s_scr, tb):
    sb = min(256, tb)
    nc = tb // CHUNK
    q_ref, f_ref, i_ref, g_ref = (hg_ref.at[:, p * HG_WIDTH:(p + 1) * HG_WIDTH] for p in range(4))
    pos = _chunk_pos((tb, HEAD_DIM))
    mask = _intra_mask(sb)
    lanes = [slice(hh * HEAD_DIM, (hh + 1) * HEAD_DIM) for hh in range(N_HEADS)]
    qi, ko, vb, dec, st = [], [], [], [], []
    for hh, ln in enumerate(lanes):
        lb, _ = _lower_bound(low_ref.at[:, ln])
        _, q, _, _, k, e_qa, e_ka, e_b, e_ko, dec_h = _hg_gates(q_ref[:, ln], f_ref[:, ln], lb, pos, tb)
        qh = (q * e_qa).astype(BF16)
        kh = (k * e_ka).astype(BF16)
        qi.append((q * e_b).astype(BF16))
        ko.append((k * e_ko).astype(BF16))
        vb.append(i_ref[:, ln].astype(BF16))
        dec.append(dec_h)
        st.append(s_scr[hh])
        for s in range(tb // sb):
            sl = slice(s * sb, (s + 1) * sb)
            p = jnp.where(mask, _mm_nt(qh[sl], kh[sl]), 0.0)
            o_ref[sl, ln] = _mm(p, vb[hh][sl])
    for c in range(nc):
        sl = slice(c * CHUNK, (c + 1) * CHUNK)
        for hh, ln in enumerate(lanes):
            st_ref[hh, c] = st[hh]
            o_ref[sl, ln] = o_ref[sl, ln] + _mm_nt(qi[hh][sl], st[hh])
            st[hh] = dec[hh][c] * st[hh] + _mm_tn(vb[hh][sl], ko[hh][sl])
    for hh, ln in enumerate(lanes):
        s_scr[hh] = st[hh]
        o = o_ref[:, ln]
        r = lax.rsqrt(jnp.mean(o * o, axis=-1, keepdims=True) + EPS)
        gr = g_ref[:, ln]
        og = (o * r * gn_ref[...]) * (gr * _sigmoid(gr))
        og_ref[:, ln] = og.astype(BF16)
        ogt_ref[ln, :] = og.T.astype(BF16)


def _merge_fwd(og, cvo, gt, x, wa, wb, wo):
    T = x.shape[0]
    tm = min(1024, T)

    def body(og_ref, cvo_ref, gt_ref, x_ref, wa_ref, wb_ref, wo_ref, x1_ref, mgt_ref):
        ya = jnp.dot(og_ref[...], _shard_cols(wa_ref), preferred_element_type=F32)
        yb = jnp.dot(cvo_ref[...], _shard_cols(wb_ref), preferred_element_type=F32)
        m = (_sigmoid(gt_ref[:, :D_MODEL].astype(F32)) * ya
             + _sigmoid(gt_ref[:, D_MODEL:].astype(F32)) * yb)
        mgt_ref[...] = m.T.astype(BF16)
        x1_ref[...] = x_ref[...] + jnp.dot(m.astype(BF16), wo_ref[...], preferred_element_type=F32)

    row = lambda n: pl.BlockSpec((tm, n), lambda i: (i, 0))
    return pl.pallas_call(
        body, name="merge_fwd", grid=(T // tm,),
        in_specs=[row(HG_WIDTH), row(CONV_WIDTH), row(2 * D_MODEL), row(D_MODEL),
                  _resident(wa.shape), _resident(wb.shape), _resident(wo.shape)],
        out_specs=[row(D_MODEL), _col(tm, D_MODEL)],
        out_shape=[jax.ShapeDtypeStruct((T, D_MODEL), F32), jax.ShapeDtypeStruct((D_MODEL, T), BF16)],
        compiler_params=_params(("parallel",)),
    )(og, cvo, gt, x, wa, wb, wo)


def _ffn_fwd_loss(x1, g, wg, wu, wd, target, g_fin):
    T = x1.shape[0]
    tm = min(512, T)

    def body(x_ref, g_ref, wg_ref, wu_ref, wd_ref, t_ref, gf_ref,
             ht_ref, gate_ref, up_ref, act_ref, loss_ref, dgf_ref, dx2_ref, dx2t_ref):
        @pl.when(pl.program_id(0) == 0)
        def _():
            loss_ref[...] = jnp.zeros_like(loss_ref)
            dgf_ref[...] = jnp.zeros_like(dgf_ref)

        xv = x_ref[...]
        r = lax.rsqrt(jnp.mean(xv * xv, axis=-1, keepdims=True) + EPS)
        hf = xv * r * g_ref[...]
        h = hf.astype(BF16)
        ht_ref[...] = hf.T.astype(BF16)
        gate = _mm_nt(h, wg_ref[...])
        up = _mm_nt(h, wu_ref[...])
        gate_ref[...] = gate.astype(STASH)
        up_ref[...] = up.astype(STASH)
        act = (gate * _sigmoid(gate) * up).astype(BF16)
        act_ref[...] = act
        x2 = xv + jnp.dot(act, wd_ref[...], preferred_element_type=F32)

        gv = gf_ref[...]
        r2 = lax.rsqrt(jnp.mean(x2 * x2, axis=-1, keepdims=True) + EPS)
        xh = x2 * r2
        err = xh * gv - t_ref[...]
        loss_ref[...] += 0.5 * jnp.sum(jnp.mean(err * err, axis=-1, keepdims=True), axis=0, keepdims=True)
        dy = err * (1.0 / D_MODEL)
        dgf_ref[...] += jnp.sum(dy * xh, axis=0, keepdims=True)
        w = dy * gv
        dx2 = r2 * (w - xh * jnp.mean(w * xh, axis=-1, keepdims=True))
        dx2_ref[...] = dx2
        dx2t_ref[...] = dx2.T.astype(BF16)

    row = lambda n: pl.BlockSpec((tm, n), lambda i: (i, 0))
    return pl.pallas_call(
        body, name="ffn_fwd_loss", grid=(T // tm,),
        in_specs=[row(D_MODEL), _full((1, D_MODEL)), _resident(wg.shape), _resident(wu.shape), _resident(wd.shape),
                  row(D_MODEL), _full((1, D_MODEL))],
        out_specs=[_col(tm, D_MODEL), row(D_FF), row(D_FF), row(D_FF), _full((1, 128)), _full((1, D_MODEL)),
                   row(D_MODEL), _col(tm, D_MODEL)],
        out_shape=[jax.ShapeDtypeStruct((D_MODEL, T), BF16), jax.ShapeDtypeStruct((T, D_FF), STASH),
                   jax.ShapeDtypeStruct((T, D_FF), STASH), jax.ShapeDtypeStruct((T, D_FF), BF16),
                   jax.ShapeDtypeStruct((1, 128), F32), jax.ShapeDtypeStruct((1, D_MODEL), F32),
                   jax.ShapeDtypeStruct((T, D_MODEL), F32), jax.ShapeDtypeStruct((D_MODEL, T), BF16)],
        compiler_params=_params(("arbitrary",), vmem=VMEM_LIMIT_LARGE_V7X),
    )(x1, g, wg, wu, wd, target, g_fin)


def _ffn_bwd(dx2, x1, gate, up, g, wg, wu, wd):
    T = x1.shape[0]
    tm = min(512, T)

    def body(dx2_ref, x_ref, gate_ref, up_ref, g_ref, wg_ref, wu_ref, wd_ref, dgate_ref, dup_ref, dx1_ref, dgn_ref):
        @pl.when(pl.program_id(0) == 0)
        def _():
            dgn_ref[...] = jnp.zeros_like(dgn_ref)

        dx2 = dx2_ref[...]
        dact = _mm_nt(dx2, wd_ref[...])
        gate = gate_ref[...].astype(F32)
        s = _sigmoid(gate)
        dgate = (dact * up_ref[...].astype(F32) * (s * (1.0 + gate * (1.0 - s)))).astype(BF16)
        dup = (dact * (gate * s)).astype(BF16)
        dgate_ref[...] = dgate
        dup_ref[...] = dup
        dh = _mm(dgate, wg_ref[...]) + _mm(dup, wu_ref[...])
        xv = x_ref[...]
        r = lax.rsqrt(jnp.mean(xv * xv, axis=-1, keepdims=True) + EPS)
        xh = xv * r
        dgn_ref[...] += jnp.sum(dh * xh, axis=0, keepdims=True)
        w = dh * g_ref[...]
        dx1_ref[...] = dx2 + r * (w - xh * jnp.mean(w * xh, axis=-1, keepdims=True))

    row = lambda n: pl.BlockSpec((tm, n), lambda i: (i, 0))
    return pl.pallas_call(
        body, name="ffn_bwd", grid=(T // tm,),
        in_specs=[row(D_MODEL), row(D_MODEL), row(D_FF), row(D_FF), _full((1, D_MODEL)),
                  _resident(wg.shape), _resident(wu.shape), _resident(wd.shape)],
        out_specs=[row(D_FF), row(D_FF), row(D_MODEL), _full((1, D_MODEL))],
        out_shape=[jax.ShapeDtypeStruct((T, D_FF), BF16), jax.ShapeDtypeStruct((T, D_FF), BF16),
                   jax.ShapeDtypeStruct((T, D_MODEL), F32), jax.ShapeDtypeStruct((1, D_MODEL), F32)],
        compiler_params=_params(("arbitrary",), vmem=VMEM_LIMIT_LARGE_V7X),
    )(dx2, x1, gate, up, g, wg, wu, wd)


def _merge_bwd(dx1, og, cvo, gt, cv, wa, wb, wo, conv_w):
    T = dx1.shape[0]
    tm = min(512, T)
    nt = T // tm

    def body(dx_ref, og_ref, cvo_ref, gt_ref, cv_ref, halo_ref, wa_ref, wb_ref, wo_ref, cw_ref,
             dgt_ref, dya_ref, dyb_ref, dog_ref, dcv_ref, dcw_ref, prev_u, next_dy):
        step = pl.program_id(0)

        @pl.when(step == 0)
        def _():
            next_dy[...] = jnp.zeros_like(next_dy)
            dcw_ref[...] = jnp.zeros_like(dcw_ref)

        dm = _mm_nt(dx_ref[...], wo_ref[...])
        wa = _shard_cols(wa_ref)
        wb = _shard_cols(wb_ref)
        ya = jnp.dot(og_ref[...], wa, preferred_element_type=F32)
        yb = jnp.dot(cvo_ref[...], wb, preferred_element_type=F32)
        sa = _sigmoid(gt_ref[:, :D_MODEL].astype(F32))
        sb = _sigmoid(gt_ref[:, D_MODEL:].astype(F32))
        dgt_ref[:, :D_MODEL] = (dm * ya * (sa * (1.0 - sa))).astype(BF16)
        dgt_ref[:, D_MODEL:] = (dm * yb * (sb * (1.0 - sb))).astype(BF16)
        dya = (dm * sa).astype(BF16)
        dyb = (dm * sb).astype(BF16)
        dya_ref[...] = dya
        dyb_ref[...] = dyb
        dog_ref[...] = _mm_nt(dya, wa)
        dcvo = _mm_nt(dyb, wb)

        cvt = cv_ref[...].astype(F32)
        c, bg, xb = cvt[:, :CONV_WIDTH], cvt[:, CONV_WIDTH:2 * CONV_WIDTH], cvt[:, 2 * CONV_WIDTH:]
        halo = halo_ref[...].astype(F32)
        first_tile = step == nt - 1
        prev_u[...] = jnp.where(first_tile, 0.0, halo[:, :CONV_WIDTH] * halo[:, 2 * CONV_WIDTH:])
        u = c * xb
        row = lax.broadcasted_iota(jnp.int32, u.shape, 0)
        p1 = prev_u[HALO - 1:HALO, :]
        p2 = prev_u[HALO - 2:HALO - 1, :]
        u1 = jnp.where(row >= 1, pltpu.roll(u, 1, 0), p1)
        u2 = jnp.where(row >= 2, pltpu.roll(u, 2, 0), jnp.where(row == 1, p1, p2))
        w0, w1, w2 = cw_ref[0:1, :], cw_ref[1:2, :], cw_ref[2:3, :]
        y = w0 * u2 + w1 * u1 + w2 * u
        dcv_ref[:, CONV_WIDTH:2 * CONV_WIDTH] = (dcvo * y).astype(BF16)
        dy = dcvo * bg
        dcw_ref[0:1, :] += jnp.sum(dy * u2, axis=0, keepdims=True)
        dcw_ref[1:2, :] += jnp.sum(dy * u1, axis=0, keepdims=True)
        dcw_ref[2:3, :] += jnp.sum(dy * u, axis=0, keepdims=True)
        n1 = next_dy[0:1, :]
        n2 = next_dy[1:2, :]
        dy1 = jnp.where(row < tm - 1, pltpu.roll(dy, tm - 1, 0), n1)
        dy2 = jnp.where(row < tm - 2, pltpu.roll(dy, tm - 2, 0), jnp.where(row == tm - 2, n1, n2))
        du = w2 * dy + w1 * dy1 + w0 * dy2
        dcv_ref[:, :CONV_WIDTH] = (du * xb).astype(BF16)
        dcv_ref[:, 2 * CONV_WIDTH:] = (du * c).astype(BF16)
        next_dy[...] = dy[:HALO, :]

    rt = lambda i: nt - 1 - i
    row = lambda n: pl.BlockSpec((tm, n), lambda i: (rt(i), 0))
    halo = pl.BlockSpec((HALO, N_CV), lambda i: (jnp.maximum(rt(i) * (tm // HALO) - 1, 0), 0))
    return pl.pallas_call(
        body, name="merge_bwd", grid=(nt,),
        in_specs=[row(D_MODEL), row(HG_WIDTH), row(CONV_WIDTH), row(2 * D_MODEL), row(N_CV), halo,
                  _resident(wa.shape), _resident(wb.shape), _resident(wo.shape), _full((CONV_K, CONV_WIDTH))],
        out_specs=[row(2 * D_MODEL), row(D_MODEL), row(D_MODEL), row(HG_WIDTH), row(N_CV),
                   _full((CONV_K, CONV_WIDTH))],
        out_shape=[jax.ShapeDtypeStruct((T, 2 * D_MODEL), BF16), jax.ShapeDtypeStruct((T, D_MODEL), BF16),
                   jax.ShapeDtypeStruct((T, D_MODEL), BF16), jax.ShapeDtypeStruct((T, HG_WIDTH), F32),
                   jax.ShapeDtypeStruct((T, N_CV), BF16), jax.ShapeDtypeStruct((CONV_K, CONV_WIDTH), F32)],
        scratch_shapes=[pltpu.VMEM((HALO, CONV_WIDTH), F32), pltpu.VMEM((HALO, CONV_WIDTH), F32)],
        compiler_params=_params(("arbitrary",)),
    )(dx1, og, cvo, gt, cv, cv, wa, wb, wo, conv_w)


def _drop_operands(body, first, count):
    def wrapped(*refs):
        return body(*refs[:first], *refs[first + count:])
    return wrapped


def _hg_bwd(dog, hg, o, st, low, gn, after=()):
    T = hg.shape[0]
    tb = min(512, T)
    sb = min(256, tb)
    nb = T // tb
    nc = tb // CHUNK
    wid = HEADS_PER_STEP * HEAD_DIM

    def body(q_ref, f_ref, i_ref, g_ref, low_ref, gn_ref, o_ref, dog_ref, st_ref,
             dhg_ref, dlow_ref, dgn_ref,
             ds_scr, dqi_scr, dko_scr, dv_scr, dd_scr, dqh_scr, dkh_scr):
        h = pl.program_id(0)
        t = pl.program_id(1)
        dq_ref, df_ref, di_ref, dg_ref = (dhg_ref.at[:, p * HG_WIDTH:(p + 1) * HG_WIDTH] for p in range(4))

        @pl.when(t == 0)
        def _():
            ds_scr[...] = jnp.zeros_like(ds_scr)
            dlow_ref[...] = jnp.zeros_like(dlow_ref)

        @pl.when((t == 0) & (h == 0))
        def _():
            dgn_ref[...] = jnp.zeros_like(dgn_ref)

        pos = _chunk_pos((tb, HEAD_DIM))
        mask = _intra_mask(sb)
        gnv = gn_ref[...]
        lanes = [slice(hh * HEAD_DIM, (hh + 1) * HEAD_DIM) for hh in range(HEADS_PER_STEP)]
        heads = []
        for hh, ln in enumerate(lanes):
            lb, lb1 = _lower_bound(low_ref.at[:, ln])
            qr = q_ref[:, ln]
            sq, q, sg, f, k, e_qa, e_ka, e_b, e_ko, dec = _hg_gates(qr, f_ref[:, ln], lb, pos, tb)

            gr = g_ref[:, ln]
            o = o_ref[:, ln]
            dog_v = dog_ref[:, ln]
            sgr = _sigmoid(gr)
            r = lax.rsqrt(jnp.mean(o * o, axis=-1, keepdims=True) + EPS)
            oh = o * r
            dg_ref[:, ln] = (dog_v * (oh * gnv) * (sgr * (1.0 + gr * (1.0 - sgr)))).astype(BF16)
            don = dog_v * (gr * sgr)
            dgn_ref[...] += jnp.sum(don * oh, axis=0, keepdims=True)
            w = don * gnv
            do = (r * (w - oh * jnp.mean(w * oh, axis=-1, keepdims=True))).astype(BF16)

            qh = (q * e_qa).astype(BF16)
            kh = (k * e_ka).astype(BF16)
            qi = (q * e_b).astype(BF16)
            ko = (k * e_ko).astype(BF16)
            vb = i_ref[:, ln].astype(BF16)

            for s in range(tb // sb):
                sl = slice(s * sb, (s + 1) * sb)
                p = jnp.where(mask, _mm_nt(qh[sl], kh[sl]), 0.0).astype(BF16)
                dp = jnp.where(mask, _mm_nt(do[sl], vb[sl]), 0.0).astype(BF16)
                dv_scr[sl, ln] = _mm_tn(p, do[sl])
                dqh_scr[sl, ln] = _mm(dp, kh[sl])
                dkh_scr[sl, ln] = _mm_tn(dp, qh[sl])
            heads.append(dict(lb=lb, lb1=lb1, qr=qr, sq=sq, q=q, sg=sg, f=f, k=k, e_qa=e_qa, e_ka=e_ka, e_b=e_b,
                              e_ko=e_ko, dec=dec, do=do, qi=qi, ko=ko, vb=vb, ds=ds_scr[hh]))

        for c in reversed(range(nc)):
            sl = slice(c * CHUNK, (c + 1) * CHUNK)
            for hh, ln in enumerate(lanes):
                hd = heads[hh]
                ds = hd["ds"]
                st_c = st_ref[hh, c]
                dqi_scr[sl, ln] = _mm(hd["do"][sl], st_c)
                dko_scr[sl, ln] = _mm(hd["vb"][sl], ds)
                dv_scr[sl, ln] = dv_scr[sl, ln] + _mm_nt(hd["ko"][sl], ds)
                dec_c = hd["dec"][c]
                dd_scr[sl, ln] = jnp.broadcast_to(dec_c * jnp.sum(ds * st_c, axis=0, keepdims=True),
                                                  (CHUNK, HEAD_DIM))
                hd["ds"] = dec_c * ds + _mm_tn(hd["do"][sl], hd["qi"][sl])

        for hh, ln in enumerate(lanes):
            hd = heads[hh]
            ds_scr[hh] = hd["ds"]
            q, k, lb = hd["q"], hd["k"], hd["lb"]
            dko_e = dko_scr[:, ln] * hd["e_ko"]
            dq = dqh_scr[:, ln] * hd["e_qa"] + dqi_scr[:, ln] * hd["e_b"]
            dk = dkh_scr[:, ln] * hd["e_ka"] + dko_e
            kd3 = (k * dko_e).reshape(nc, CHUNK, HEAD_DIM)
            last = jnp.broadcast_to(jnp.sum(kd3, axis=1, keepdims=True), kd3.shape).reshape(tb, HEAD_DIM)
            db = q * dq - k * dk + jnp.where(pos == CHUNK - 1, dd_scr[:, ln] + last, 0.0)
            dlg = _chunk_rev_cumsum(db, pos)
            dfv = dlg / hd["f"] - dk
            s_low = jnp.sum(dfv * (1.0 - hd["sg"]), axis=0, keepdims=True)
            dlow_ref[0:1, ln] += s_low * lb * (1.0 - lb)
            dlow_ref[1:2, ln] += -s_low * lb * hd["lb1"]
            df_ref[:, ln] = (dfv * (1.0 - lb) * hd["sg"] * (1.0 - hd["sg"])).astype(BF16)
            dq_ref[:, ln] = (dq * Q_SCALE * (hd["sq"] * (1.0 + hd["qr"] * (1.0 - hd["sq"])))).astype(BF16)
            di_ref[:, ln] = dv_scr[:, ln].astype(BF16)

    rt = lambda t: nb - 1 - t
    col = lambda p: pl.BlockSpec((tb, wid), lambda h, t: (rt(t), p * HEAD_GROUPS + h))
    hcol = pl.BlockSpec((tb, wid), lambda h, t: (rt(t), h))
    assert HEAD_GROUPS == 1
    tile = pltpu.VMEM((tb, wid), F32)
    return pl.pallas_call(
        _drop_operands(body, 9, len(after)), name="hg_bwd", grid=(HEAD_GROUPS, nb),
        in_specs=[col(0), col(1), col(2), col(3), pl.BlockSpec((2, wid), lambda h, t: (0, h)),
                  pl.BlockSpec((1, HEAD_DIM), lambda h, t: (0, 0)), hcol, hcol,
                  pl.BlockSpec((HEADS_PER_STEP, nc, HEAD_DIM, HEAD_DIM), lambda h, t: (h, rt(t), 0, 0))]
                 + [HBM_SPEC] * len(after),
        out_specs=[pl.BlockSpec((tb, N_HG), lambda h, t: (rt(t), 0)), pl.BlockSpec((2, wid), lambda h, t: (0, h)),
                   pl.BlockSpec((1, HEAD_DIM), lambda h, t: (0, 0))],
        out_shape=[jax.ShapeDtypeStruct((T, N_HG), BF16), jax.ShapeDtypeStruct((2, HG_WIDTH), F32),
                   jax.ShapeDtypeStruct((1, HEAD_DIM), F32)],
        scratch_shapes=[pltpu.VMEM((HEADS_PER_STEP, HEAD_DIM, HEAD_DIM), F32), tile, tile, tile, tile, tile, tile],
        compiler_params=_params(("arbitrary", "arbitrary")),
    )(hg, hg, hg, hg, low, gn, o, dog, st, *after)


def _in_bwd(dparts, w_in, x, dx1, g, after=()):
    T = x.shape[0]
    tm = min(512, T)
    widths = [p.shape[1] for p in dparts]
    offs = [sum(widths[:i]) for i in range(len(widths))]
    n = len(dparts)

    def body(*refs):
        d_refs = refs[:n]
        w_ref, x_ref, dx1_ref, g_ref, dx_ref, dgn_ref = refs[n:]

        @pl.when(pl.program_id(0) == 0)
        def _():
            dgn_ref[...] = jnp.zeros_like(dgn_ref)

        dh = None
        for d_ref, off, wd in zip(d_refs, offs, widths):
            part = _mm(d_ref[...], w_ref[off:off + wd, :])
            dh = part if dh is None else dh + part
        xv = x_ref[...]
        r = lax.rsqrt(jnp.mean(xv * xv, axis=-1, keepdims=True) + EPS)
        xh = xv * r
        dgn_ref[...] += jnp.sum(dh * xh, axis=0, keepdims=True)
        w = dh * g_ref[...]
        dx_ref[...] = dx1_ref[...] + r * (w - xh * jnp.mean(w * xh, axis=-1, keepdims=True))

    row = lambda m: pl.BlockSpec((tm, m), lambda i: (i, 0))
    return pl.pallas_call(
        _drop_operands(body, n + 4, len(after)), name="in_bwd", grid=(T // tm,),
        in_specs=[row(wd) for wd in widths] + [_resident(w_in.shape), row(D_MODEL), row(D_MODEL), _full((1, D_MODEL))]
                 + [HBM_SPEC] * len(after),
        out_specs=[row(D_MODEL), _full((1, D_MODEL))],
        out_shape=[jax.ShapeDtypeStruct((T, D_MODEL), F32), jax.ShapeDtypeStruct((1, D_MODEL), F32)],
        compiler_params=_params(("arbitrary",)),
    )(*dparts, w_in, x, dx1, g, *after)


def _wgrad_ffn(h2t, dgate, dup, dx2t, act, tn=256):
    M, T = h2t.shape
    N = dgate.shape[1]

    def body(h_ref, x_ref, dg_ref, du_ref, act_ref, og_ref, ou_ref, od_ref):
        h = h_ref[...]
        og_ref[...] = _mm(h, dg_ref[...]).T.astype(BF16)
        ou_ref[...] = _mm(h, du_ref[...]).T.astype(BF16)
        od_ref[...] = _mm(x_ref[...], act_ref[...]).T.astype(BF16)

    rhs = pl.BlockSpec((T, tn), lambda j: (0, j))
    out_spec = pl.BlockSpec((tn, M), lambda j: (j, 0))
    out = jax.ShapeDtypeStruct((N, M), BF16)
    return pl.pallas_call(
        body, name="wgrad_ffn", grid=(N // tn,),
        in_specs=[_resident((M, T)), _resident((M, T)), rhs, rhs, rhs], out_specs=[out_spec] * 3,
        out_shape=[out, out, out],
        compiler_params=_params(("parallel",)),
    )(h2t, dx2t, dgate, dup, act)


def _wgrad_out_branches(ogt, dya, cvot, dyb, mgt, dx1, after=()):
    M, T = ogt.shape
    N = dya.shape[1]
    c = N // N_DEV
    per = 2
    tn = per * c

    def body(at_ref, da_ref, bt_ref, db_ref, mt_ref, dx_ref, oa_ref, ob_ref, oo_ref):
        ga = _mm(at_ref[...], da_ref[...])
        gb = _mm(bt_ref[...], db_ref[...])
        for s in range(per):
            oa_ref[s] = ga[:, s * c:(s + 1) * c].astype(BF16)
            ob_ref[s] = gb[:, s * c:(s + 1) * c].astype(BF16)
        oo_ref[...] = _mm(mt_ref[...], dx_ref[...]).astype(BF16)

    rhs = pl.BlockSpec((T, tn), lambda j: (0, j))
    owners = pl.BlockSpec((per, M, c), lambda j: (j, 0, 0))
    out = jax.ShapeDtypeStruct((N_DEV, M, c), BF16)
    return pl.pallas_call(
        _drop_operands(body, 6, len(after)), name="wgrad_out_branches", grid=(N // tn,),
        in_specs=[_resident((M, T)), rhs, _resident((M, T)), rhs, _resident(mgt.shape), rhs] + [HBM_SPEC] * len(after),
        out_specs=[owners, owners, pl.BlockSpec((mgt.shape[0], tn), lambda j: (0, j))],
        out_shape=[out, out, jax.ShapeDtypeStruct((mgt.shape[0], dx1.shape[1]), BF16)],
        compiler_params=_params(("parallel",)),
    )(ogt, dya, cvot, dyb, mgt, dx1, *after)


def _wgrad_in(ht, dparts, after=(), riders=()):
    M, T = ht.shape
    tn = 512
    nblk = [p.shape[1] // tn for p in dparts]
    start = [sum(nblk[:i]) for i in range(len(nblk))]
    n = len(dparts)
    steps = sum(nblk)
    nr = len(riders)
    rows = [r[0].shape[0] // steps for r in riders]
    assert all(r[0].shape[0] == rr * steps and rr % 16 == 0 for r, rr in zip(riders, rows))

    def body(a_ref, *refs):
        d_refs = refs[:n]
        rider_in = refs[n:n + 4 * nr]
        o_ref = refs[n + 4 * nr]
        rider_out = refs[n + 4 * nr + 1:]
        j = pl.program_id(0)
        for d_ref, s, nb in zip(d_refs, start, nblk):
            @pl.when((j >= s) & (j < s + nb))
            def _():
                o_ref[...] = _mm(a_ref[...], d_ref[...]).T.astype(BF16)
        for i in range(nr):
            w_ref, p_ref, m_ref, v_ref = rider_in[4 * i:4 * i + 4]
            g = p_ref[0].astype(F32)
            for k in range(1, 4):
                g = g + p_ref[k].astype(F32)
            delta, m_new, v_new = _adamw_math(w_ref[...], g, m_ref[...], v_ref[...])
            for k, val in enumerate((g, delta, m_new, v_new)):
                rider_out[4 * i + k][...] = val

    def piece_spec(s, nb):
        return pl.BlockSpec((T, tn), lambda j: (0, jnp.clip(j - s, 0, nb - 1)))

    rider_specs, rider_out_specs, rider_out_shape, rider_args = [], [], [], []
    for (w, parts, m, v), rr in zip(riders, rows):
        blk = pl.BlockSpec((rr, w.shape[1]), lambda j: (j, 0))
        rider_specs += [blk, pl.BlockSpec((4, rr, w.shape[1]), lambda j: (0, j, 0)), blk, blk]
        rider_out_specs += [blk] * 4
        rider_out_shape += [jax.ShapeDtypeStruct(w.shape, F32)] * 4
        rider_args += [w, parts, m, v]
    outs = pl.pallas_call(
        _drop_operands(body, 1 + n + 4 * nr, len(after)), name="wgrad_in", grid=(steps,),
        in_specs=[_resident((M, T))] + [piece_spec(s, nb) for s, nb in zip(start, nblk)] + rider_specs
                 + [HBM_SPEC] * len(after),
        out_specs=[pl.BlockSpec((tn, M), lambda j: (j, 0))] + rider_out_specs,
        out_shape=[jax.ShapeDtypeStruct((steps * tn, M), BF16)] + rider_out_shape,
        compiler_params=_params(("parallel",)),
    )(ht, *dparts, *rider_args, *after)
    return outs[0], [outs[1 + 4 * i:5 + 4 * i] for i in range(nr)]


def _adamw_math(w, g, m, v):
    m = ADAM_B1 * m + (1.0 - ADAM_B1) * g
    v = ADAM_B2 * v + (1.0 - ADAM_B2) * (g * g)
    m_hat = m / (1.0 - ADAM_B1 ** ADAM_STEP)
    v_hat = v / (1.0 - ADAM_B2 ** ADAM_STEP)
    delta = -ADAM_LR * (m_hat / (jnp.sqrt(v_hat) + ADAM_EPS) + ADAM_WD * w)
    return delta, m, v


def _adamw_sum(name, ws, parts, ms, vs):
    n = len(ws)
    steps = min(_row_steps(w.shape[0]) for w in ws)
    rows = [w.shape[0] // steps for w in ws]

    def body(*refs):
        w_refs, p_refs, m_refs, v_refs = (refs[k * n:(k + 1) * n] for k in range(4))
        out_refs = refs[4 * n:]
        for i in range(n):
            g = p_refs[i][0].astype(F32)
            for k in range(1, 4):
                g = g + p_refs[i][k].astype(F32)
            delta, m_new, v_new = _adamw_math(w_refs[i][...], g, m_refs[i][...], v_refs[i][...])
            for k, val in enumerate((g, delta, m_new, v_new)):
                out_refs[4 * i + k][...] = val

    blk = [pl.BlockSpec((r, w.shape[1]), lambda s: (s, 0)) for r, w in zip(rows, ws)]
    pblk = [pl.BlockSpec((4, r, w.shape[1]), lambda s: (0, s, 0)) for r, w in zip(rows, ws)]
    out_specs, out_shape = [], []
    for b, w in zip(blk, ws):
        out_specs += [b] * 4
        out_shape += [jax.ShapeDtypeStruct(w.shape, F32)] * 4
    flat = pl.pallas_call(
        body, name=name, grid=(steps,),
        in_specs=blk + pblk + blk + blk, out_specs=out_specs, out_shape=out_shape,
        compiler_params=_params(("parallel",)),
    )(*ws, *parts, *ms, *vs)
    return [flat[4 * i:4 * i + 4] for i in range(n)]


_SMALL_SLOTS = (("norm_mix_g", 0, 1, 1024), ("norm_ffn_g", 1, 1, 1024), ("norm_final_g", 2, 1, 1024),
                ("lower_bounds", 3, 2, 512), ("hg_norm_g", 5, 1, 128), ("loss", 6, 1, 128), ("conv_w", 8, 3, 512))
_SMALL_PARAMS = tuple(s for s in _SMALL_SLOTS if s[0] != "loss")
CONV_SHARD = CONV_WIDTH // N_DEV


def _small_pack(small):
    def body(*refs):
        out = refs[-1]
        out[...] = jnp.zeros_like(out)
        for ref, (_, row, rows, lanes) in zip(refs[:-1], _SMALL_SLOTS):
            out[row:row + rows, 0:lanes] = ref[...]

    vmem = pl.BlockSpec(memory_space=pltpu.VMEM)
    return pl.pallas_call(
        body, name="small_pack", in_specs=[vmem] * len(_SMALL_SLOTS), out_specs=vmem,
        out_shape=jax.ShapeDtypeStruct((SMALL_ROWS, 1024), F32),
    )(*[small[name] for name, _, _, _ in _SMALL_SLOTS])


def _small_update(gathered, dev, w, m, v):
    n = len(_SMALL_PARAMS)

    def body(dev_ref, g_ref, *refs):
        w_refs, m_refs, v_refs = refs[:n], refs[n:2 * n], refs[2 * n:3 * n]
        loss_ref, out_refs, sum_scr = refs[3 * n], refs[3 * n + 1:-1], refs[-1]
        total = g_ref[0]
        for k in range(1, N_DEV):
            total = total + g_ref[k]
        sum_scr[...] = total
        loss_ref[...] = sum_scr[6:7, 0:128]
        for p, (name, row, rows, lanes) in enumerate(_SMALL_PARAMS):
            if name == "conv_w":
                for r in range(rows):
                    g = sum_scr[row + r:row + r + 1, 0:CONV_SHARD]
                    for s in range(1, N_DEV):
                        mine = sum_scr[row + r:row + r + 1, s * CONV_SHARD:(s + 1) * CONV_SHARD]
                        g = jnp.where(dev_ref[0] == s, mine, g)
                    delta, m_new, v_new = _adamw_math(w_refs[p][r], g, m_refs[p][r], v_refs[p][r])
                    out_refs[4 * p][r] = g
                    out_refs[4 * p + 1][r] = delta
                    out_refs[4 * p + 2][r] = m_new
                    out_refs[4 * p + 3][r] = v_new
                continue
            g = sum_scr[row:row + rows, 0:lanes]
            delta, m_new, v_new = _adamw_math(w_refs[p][...], g, m_refs[p][...], v_refs[p][...])
            out_refs[4 * p][...] = g
            out_refs[4 * p + 1][...] = delta
            out_refs[4 * p + 2][...] = m_new
            out_refs[4 * p + 3][...] = v_new

    vmem = pl.BlockSpec(memory_space=pltpu.VMEM)
    outs = [jax.ShapeDtypeStruct((1, 128), F32)]
    for a in w:
        outs += [jax.ShapeDtypeStruct(a.shape, F32)] * 4
    return pl.pallas_call(
        body, name="small_update",
        in_specs=[pl.BlockSpec(memory_space=pltpu.SMEM)] + [vmem] * (1 + 3 * n), out_specs=[vmem] * len(outs),
        out_shape=outs, scratch_shapes=[pltpu.VMEM((SMALL_ROWS, 1024), F32)],
    )(dev, gathered, *w, *m, *v)


def _row_steps(rows):
    for steps in (4, 2):
        if rows % (16 * steps) == 0:
            return steps
    return 1


def _pair_sum(name, by_owner, got, core, after=()):
    n = len(got)

    def body(core_ref, *refs):
        for a_ref, b_ref, o_ref in zip(refs[:n], refs[n:2 * n], refs[2 * n:]):
            o_ref[...] = (a_ref[...].astype(F32) + b_ref[...].astype(F32)).astype(BF16)

    def blk(g):
        return pl.BlockSpec((None,) + g.shape[1:], lambda k, core_ref: (k, 0, 0))

    def mine(g):
        return pl.BlockSpec((None,) + g.shape[1:], lambda k, core_ref: (2 * k + core_ref[0], 0, 0))

    return pl.pallas_call(
        _drop_operands(body, 1 + 2 * n, len(after)), name=name,
        grid_spec=pltpu.PrefetchScalarGridSpec(
            num_scalar_prefetch=1, grid=(4,),
            in_specs=[mine(g) for g in got] + [blk(g) for g in got] + [HBM_SPEC] * len(after),
            out_specs=[blk(g) for g in got]),
        out_shape=[jax.ShapeDtypeStruct(g.shape, BF16) for g in got],
        compiler_params=_params(("parallel",)),
    )(core, *by_owner, *got, *after)


MESH = pl.DeviceIdType.MESH
HBM_SPEC = pl.BlockSpec(memory_space=pl.ANY)


def _handshake(peers):
    barrier = pltpu.get_barrier_semaphore()
    for peer in peers:
        pl.semaphore_signal(barrier, inc=1, device_id=peer, device_id_type=MESH)
    pl.semaphore_wait(barrier, len(peers))


def _comm_call(body, name, operands, out_shape, scratch, collective_id):
    if collective_id is None:
        return pl.pallas_call(body, name=name, in_specs=[HBM_SPEC] * len(operands), out_specs=[HBM_SPEC] * len(out_shape),
                              out_shape=out_shape, scratch_shapes=scratch)(*operands)
    return pl.kernel(body, out_type=out_shape, mesh=plsc.ScalarSubcoreMesh(axis_name="sequencer", num_cores=1),
                     scratch_types=scratch, name=name,
                     compiler_params=pltpu.CompilerParams(collective_id=collective_id))(*operands)


def _all_gather(name, blocks, collective_id=None, after=()):
    n = len(blocks)
    na = len(after)

    def body(*refs):
        x_refs, out_refs = refs[:n], refs[n + na:2 * n + na]
        send_sems, recv_sems, local_sems = refs[2 * n + na:]
        x, y, c = lax.axis_index("x"), lax.axis_index("y"), lax.axis_index("c")
        me, sibling = (x, y, c), (x, y, 1 - c)
        chips = [(1 - x, y), (x, 1 - y), (1 - x, 1 - y)]
        if collective_id is not None:
            _handshake([sibling] + [(*chip, c) for chip in chips])

        def slot(i, px, py, pc):
            return out_refs[i].at[4 * px + 2 * py + pc]

        def copy(i, k, blk, to, src=None):
            return pltpu.make_async_remote_copy(
                src_ref=slot(i, *blk) if src is None else src, dst_ref=slot(i, *blk),
                send_sem=send_sems.at[7 * i + k], recv_sem=recv_sems.at[7 * i + k], device_id=to, device_id_type=MESH)

        mine = [pltpu.make_async_copy(x_refs[i], slot(i, *me), local_sems.at[i]) for i in range(n)]
        for cp in mine:
            cp.start()
        first = []
        for i in range(n):
            first.append(copy(i, 0, me, sibling, src=x_refs[i]))
            first += [copy(i, 1 + j, me, (*chip, c), src=x_refs[i]) for j, chip in enumerate(chips)]
        for cp in first:
            cp.start()
        passed = []
        for i in range(n):
            for j, chip in enumerate(chips):
                copy(i, 1 + j, (*chip, c), me).wait_recv()
                passed.append(copy(i, 4 + j, (*chip, c), sibling))
                passed[-1].start()
        for i in range(n):
            copy(i, 0, sibling, me).wait_recv()
            for j, chip in enumerate(chips):
                copy(i, 4 + j, (*chip, 1 - c), me).wait_recv()
        for cp in first + passed:
            cp.wait_send()
        for cp in mine:
            cp.wait()

    return _comm_call(
        body, name, list(blocks) + list(after), [jax.ShapeDtypeStruct((N_DEV,) + b.shape, b.dtype) for b in blocks],
        [pltpu.SemaphoreType.DMA((7 * n,)), pltpu.SemaphoreType.DMA((7 * n,)), pltpu.SemaphoreType.DMA((n,))],
        collective_id)


def _sibling_swap(name, by_owner, collective_id=None, after=()):
    n = len(by_owner)
    na = len(after)

    def body(*refs):
        x_refs, out_refs = refs[:n], refs[n + na:2 * n + na]
        send_sems, recv_sems = refs[2 * n + na:]
        x, y, c = lax.axis_index("x"), lax.axis_index("y"), lax.axis_index("c")
        if collective_id is not None:
            _handshake([(x, y, 1 - c)])
        copies = []
        for i in range(n):
            for k in range(4):
                copies.append(pltpu.make_async_remote_copy(
                    src_ref=x_refs[i].at[2 * k + 1 - c], dst_ref=out_refs[i].at[k],
                    send_sem=send_sems.at[4 * i + k], recv_sem=recv_sems.at[4 * i + k],
                    device_id=(x, y, 1 - c), device_id_type=MESH))
        for cp in copies:
            cp.start()
        for cp in copies:
            cp.wait()

    return _comm_call(
        body, name, list(by_owner) + list(after),
        [jax.ShapeDtypeStruct((4,) + b.shape[1:], b.dtype) for b in by_owner],
        [pltpu.SemaphoreType.DMA((4 * n,)), pltpu.SemaphoreType.DMA((4 * n,))], collective_id)


def _chip_exchange(name, sums, collective_id=None, after=()):
    n = len(sums)
    na = len(after)

    def body(*refs):
        x_refs, out_refs = refs[:n], refs[n + na:2 * n + na]
        send_sems, recv_sems, local_sems = refs[2 * n + na:]
        x, y, c = lax.axis_index("x"), lax.axis_index("y"), lax.axis_index("c")
        chips = [(1 - x, y), (x, 1 - y), (1 - x, 1 - y)]
        my_chip = 2 * x + y
        if collective_id is not None:
            _handshake([(cx, cy, c) for cx, cy in chips])
        mine = [pltpu.make_async_copy(x_refs[i].at[my_chip], out_refs[i].at[my_chip], local_sems.at[i])
                for i in range(n)]
        for cp in mine:
            cp.start()
        sends = []
        for i in range(n):
            for j, (cx, cy) in enumerate(chips):
                sends.append(pltpu.make_async_remote_copy(
                    src_ref=x_refs[i].at[2 * cx + cy], dst_ref=out_refs[i].at[my_chip],
                    send_sem=send_sems.at[3 * i + j], recv_sem=recv_sems.at[3 * i + j],
                    device_id=(cx, cy, c), device_id_type=MESH))
        for cp in sends:
            cp.start()
        for i in range(n):
            for j, (cx, cy) in enumerate(chips):
                pltpu.make_async_remote_copy(
                    src_ref=x_refs[i].at[my_chip], dst_ref=out_refs[i].at[2 * cx + cy],
                    send_sem=send_sems.at[3 * i + j], recv_sem=recv_sems.at[3 * i + j],
                    device_id=(cx, cy, c), device_id_type=MESH).wait_recv()
        for cp in sends:
            cp.wait_send()
        for cp in mine:
            cp.wait()

    return _comm_call(
        body, name, list(sums) + list(after), [jax.ShapeDtypeStruct(s.shape, s.dtype) for s in sums],
        [pltpu.SemaphoreType.DMA((3 * n,)), pltpu.SemaphoreType.DMA((3 * n,)), pltpu.SemaphoreType.DMA((n,))],
        collective_id)


def _cast_shards(shards):
    n = len(shards)

    def body(*refs):
        for i in range(n):
            refs[n + i][...] = refs[i][...].astype(BF16)

    vmem = pl.BlockSpec(memory_space=pltpu.VMEM)
    return pl.pallas_call(
        body, name="cast_shards", in_specs=[vmem] * n, out_specs=[vmem] * n,
        out_shape=[jax.ShapeDtypeStruct(s.shape, BF16) for s in shards],
        compiler_params=pltpu.CompilerParams(vmem_limit_bytes=VMEM_LIMIT_V7X),
    )(*shards)


BIG = ("w_in", "w_branch_a", "w_branch_b", "w_out", "w_ffn_gate", "w_ffn_up", "w_ffn_down")


def _local_step(x, target, gains, low, conv_w, wg8, reduce):
    g_mix, g_hg, g_ffn, g_fin = gains
    w_in = wg8["w_in"].reshape(N_IN, D_MODEL)
    wg = wg8["w_ffn_gate"].reshape(D_FF, D_MODEL)
    wu = wg8["w_ffn_up"].reshape(D_FF, D_MODEL)
    wa, wb = wg8["w_branch_a"], wg8["w_branch_b"]
    wo = wg8["w_out"].reshape(D_MODEL, D_MODEL)
    wd = wg8["w_ffn_down"].reshape(D_FF, D_MODEL)

    ht, hg, cv, gt, cvo, cvot, o, og, ogt, st = _fwd_in(x, g_mix, w_in, conv_w, low, g_hg)
    x1, mgt = _merge_fwd(og, cvo, gt, x, wa, wb, wo)
    h2t, gate, up, act, loss, d_gfin, dx2, dx2t = _ffn_fwd_loss(x1, g_ffn, wg, wu, wd, target, g_fin)

    dgate, dup, dx1, d_gffn = _ffn_bwd(dx2, x1, gate, up, g_ffn, wg, wu, wd)
    d_wg, d_wu, d_wd = _wgrad_ffn(h2t, dgate, dup, dx2t, act)
    by_owner_ffn = lambda a: a.reshape(N_DEV, D_FF // N_DEV, D_MODEL)
    ffn = dict(w_ffn_down=by_owner_ffn(d_wd), w_ffn_gate=by_owner_ffn(d_wg), w_ffn_up=by_owner_ffn(d_wu))
    dgt, dya, dyb, dog, dcv, d_conv = _merge_bwd(dx1, og, cvo, gt, cv, wa, wb, wo, conv_w)
    sums_ffn, got_ffn = reduce.begin(ffn, sum_after=[dya])
    late = ("w_ffn_gate", "w_ffn_up")
    parts_ffn, updated_ffn = reduce.finish(ffn, sums_ffn, defer=late)
    grad_a, grad_b, grad_o = _wgrad_out_branches(ogt, dya, cvot, dyb, mgt, dx1, after=sums_ffn[:1])
    out = dict(w_out=grad_o.reshape(N_DEV, D_MODEL // N_DEV, D_MODEL), w_branch_a=grad_a, w_branch_b=grad_b)
    dhg, d_low, d_ghg = _hg_bwd(dog, hg, o, st, low, g_hg, after=list(sums_ffn) + [out["w_out"]])
    sums_out, got_out = reduce.begin(out, after=[parts_ffn[0], dhg], sum_after=updated_ffn)
    parts_out, updated_out = reduce.finish(out, sums_out)
    dparts = [dhg, dcv, dgt]
    d_w_in_t, ridden = _wgrad_in(ht, dparts, after=sums_out[:1],
                                 riders=reduce.riders(late, dict(zip(ffn, parts_ffn))))
    reduce.record(late, ridden)
    w_in_grad = dict(w_in=d_w_in_t.reshape(N_DEV, N_IN // N_DEV, D_MODEL))
    sums_in, _ = reduce.begin(w_in_grad, after=parts_out[:1], sum_after=updated_out)
    parts_in, _ = reduce.finish(w_in_grad, sums_in)
    grad_x, d_gmix = _in_bwd(dparts, w_in, x, dx1, g_mix, after=list(parts_out[:1]) + list(sums_in))
    small = dict(norm_mix_g=d_gmix, norm_ffn_g=d_gffn, norm_final_g=d_gfin, lower_bounds=d_low, hg_norm_g=d_ghg,
                 conv_w=d_conv, loss=loss)
    return grad_x, small, parts_in


def kernel(x, norm_mix_g, w_in, lower_bounds, hg_norm_g, conv_w, w_branch_a, w_branch_b, w_out, norm_ffn_g, w_ffn_gate, w_ffn_up, w_ffn_down, norm_final_g, loss_target, m_norm_mix_g, m_w_in, m_lower_bounds, m_hg_norm_g, m_conv_w, m_w_branch_a, m_w_branch_b, m_w_out, m_norm_ffn_g, m_w_ffn_gate, m_w_ffn_up, m_w_ffn_down, m_norm_final_g, v_norm_mix_g, v_w_in, v_lower_bounds, v_hg_norm_g, v_conv_w, v_w_branch_a, v_w_branch_b, v_w_out, v_norm_ffn_g, v_w_ffn_gate, v_w_ffn_up, v_w_ffn_down, v_norm_final_g):
    cx, cy, cc = lax.axis_index("x"), lax.axis_index("y"), lax.axis_index("c")
    my_dev = 4 * cx + 2 * cy + cc

    def tr(a):
        return a[0].T

    big = dict(w_in=tr(w_in), w_branch_a=w_branch_a[0], w_branch_b=w_branch_b[0], w_out=w_out[0],
               w_ffn_gate=tr(w_ffn_gate), w_ffn_up=tr(w_ffn_up), w_ffn_down=w_ffn_down[0])
    big_m = dict(w_in=tr(m_w_in), w_branch_a=m_w_branch_a[0], w_branch_b=m_w_branch_b[0], w_out=m_w_out[0],
                 w_ffn_gate=tr(m_w_ffn_gate), w_ffn_up=tr(m_w_ffn_up), w_ffn_down=m_w_ffn_down[0])
    big_v = dict(w_in=tr(v_w_in), w_branch_a=v_w_branch_a[0], w_branch_b=v_w_branch_b[0], w_out=v_w_out[0],
                 w_ffn_gate=tr(v_w_ffn_gate), w_ffn_up=tr(v_w_ffn_up), w_ffn_down=v_w_ffn_down[0])
    transposed = ("w_in", "w_ffn_gate", "w_ffn_up")

    shards = dict(zip(BIG, _cast_shards([big[n] for n in BIG])))
    first = _all_gather("gather_w_in", [shards["w_in"], conv_w.transpose(1, 0, 2)])
    ids = iter(range(1, 16))
    mid = _all_gather("gather_mid", [shards[n] for n in BIG[1:4]], collective_id=next(ids), after=first[1:])
    ffn = _all_gather("gather_ffn", [shards[n] for n in BIG[4:]], collective_id=next(ids), after=first[1:])
    wg8 = dict(zip(BIG, [first[0]] + list(mid) + list(ffn)))
    conv_full = first[1].transpose(1, 2, 0, 3).reshape(3, CONV_WIDTH)

    core = cc.reshape(1).astype(jnp.int32)
    outs = {}

    class Reduce:
        @staticmethod
        def begin(grads, after=(), sum_after=()):
            names = list(grads)
            by_owner = [grads[n] for n in names]
            got = _sibling_swap("sibling_swap_" + names[0], by_owner, collective_id=next(ids), after=after)
            sums = _pair_sum("pair_sum_" + names[0], by_owner, got, core, after=sum_after)
            return sums, got

        @staticmethod
        def finish(grads, chip_sums, after=(), defer=()):
            names = list(grads)
            parts = _chip_exchange("chip_exchange_" + names[0], chip_sums, collective_id=next(ids), after=after)
            now = [n for n in names if n not in defer]
            updated = _adamw_sum("adamw_" + now[0], [big[n] for n in now],
                                 [p for n, p in zip(names, parts) if n in now],
                                 [big_m[n] for n in now], [big_v[n] for n in now])
            outs.update(zip(now, updated))
            return parts, [outs[n][1] for n in now]

        @staticmethod
        def riders(names, parts):
            return [(big[n], parts[n], big_m[n], big_v[n]) for n in names]

        @staticmethod
        def record(names, updated):
            outs.update(zip(names, updated))

    gains = (norm_mix_g, hg_norm_g, norm_ffn_g, norm_final_g.reshape(1, D_MODEL))
    grad_x, small, last = _local_step(x[0], loss_target[0], gains, lower_bounds, conv_full, wg8, Reduce)

    small_all = _all_gather("gather_small", [_small_pack(small)], collective_id=next(ids), after=last[:1])

    def small_state(a):
        return [a[0], a[1], a[2].reshape(1, D_MODEL), a[3], a[4], a[5].transpose(1, 0, 2)]

    upd = _small_update(
        small_all[0], my_dev.reshape(1).astype(jnp.int32),
        small_state((norm_mix_g, norm_ffn_g, norm_final_g, lower_bounds, hg_norm_g, conv_w)),
        small_state((m_norm_mix_g, m_norm_ffn_g, m_norm_final_g, m_lower_bounds, m_hg_norm_g, m_conv_w)),
        small_state((v_norm_mix_g, v_norm_ffn_g, v_norm_final_g, v_lower_bounds, v_hg_norm_g, v_conv_w)))
    loss = upd[0][0, 0]
    for p, (name, _, _, _) in enumerate(_SMALL_PARAMS):
        outs[name] = upd[1 + 4 * p:5 + 4 * p]
    outs["norm_final_g"] = [a.reshape(D_MODEL) for a in outs["norm_final_g"]]
    outs["conv_w"] = [a.transpose(1, 0, 2) for a in outs["conv_w"]]

    order = ["norm_mix_g", "w_in", "lower_bounds", "hg_norm_g", "conv_w", "w_branch_a", "w_branch_b", "w_out",
             "norm_ffn_g", "w_ffn_gate", "w_ffn_up", "w_ffn_down", "norm_final_g"]
    result = [loss, grad_x[None]]
    for k in range(4):
        for n in order:
            if n in BIG:
                result.append((outs[n][k].T if n in transposed else outs[n][k])[None])
            else:
                result.append(outs[n][k])
    return tuple(result)
```

```python
import jax
import jax.numpy as jnp
from jax import lax
from jax.experimental import pallas as pl
from jax.experimental.pallas import tpu as pltpu
from jax.experimental.pallas import tpu_sc as plsc

F32 = jnp.float32
BF16 = jnp.bfloat16
STASH = jnp.bfloat16

D_MODEL = 1024
HG_WIDTH = 512
HEAD_DIM = 128
N_HEADS = 4
HEADS_PER_STEP = 4
HEAD_GROUPS = N_HEADS // HEADS_PER_STEP
CONV_WIDTH = 512
CONV_K = 3
D_FF = 2816
CHUNK = 32
EPS = 1e-6
Q_SCALE = HEAD_DIM ** -0.5
N_DEV = 8

ADAM_LR = 0.001
ADAM_B1 = 0.9
ADAM_B2 = 0.999
ADAM_EPS = 1e-08
ADAM_WD = 0.01
ADAM_STEP = 10

VMEM_LIMIT_V7X = 56 * 1024 * 1024
VMEM_LIMIT_LARGE_V7X = 62 * 1024 * 1024

SMALL_ROWS = 16


def _params(sem, vmem=VMEM_LIMIT_V7X):
    return pltpu.CompilerParams(dimension_semantics=sem, vmem_limit_bytes=vmem)


def _mm(a, b):
    return jnp.dot(a.astype(BF16), b.astype(BF16), preferred_element_type=F32)


def _mm_nt(a, b):
    return lax.dot_general(a.astype(BF16), b.astype(BF16), (((1,), (1,)), ((), ())), preferred_element_type=F32)


def _mm_tn(a, b):
    return lax.dot_general(a.astype(BF16), b.astype(BF16), (((0,), (0,)), ((), ())), preferred_element_type=F32)


def _sigmoid(x):
    return 0.5 * jnp.tanh(0.5 * x) + 0.5


def _resident(shape):
    nd = len(shape)
    return pl.BlockSpec(shape, lambda *_: (0,) * nd, pipeline_mode=pl.Buffered(1))


def _full(shape):
    nd = len(shape)
    return pl.BlockSpec(shape, lambda *_: (0,) * nd)


def _shard_cols(w_ref):
    return jnp.concatenate([w_ref[s] for s in range(N_DEV)], axis=1)


N_HG = 4 * HG_WIDTH
N_CV = 3 * CONV_WIDTH
N_GT = 2 * D_MODEL
N_IN = N_HG + N_CV + N_GT


def _col(tm, n):
    return pl.BlockSpec((n, tm), lambda i: (0, i))


HALO = 8


def _fwd_in(x, g, w_in_t, conv_w, low, gn):
    T = x.shape[0]
    tm = min(512, T)
    nc = tm // CHUNK

    def body(x_ref, g_ref, w_ref, cw_ref, low_ref, gn_ref, ht_ref, hg_ref, cv_ref, gt_ref, cvo_ref, cvot_ref,
             o_ref, og_ref, ogt_ref, st_ref, tail_scr, s_scr):
        @pl.when(pl.program_id(0) == 0)
        def _():
            tail_scr[...] = jnp.zeros_like(tail_scr)
            s_scr[...] = jnp.zeros_like(s_scr)

        xv = x_ref[...]
        r = lax.rsqrt(jnp.mean(xv * xv, axis=-1, keepdims=True) + EPS)
        hf = xv * r * g_ref[...]
        h = hf.astype(BF16)
        ht_ref[...] = hf.T.astype(BF16)
        hg_ref[...] = _mm_nt(h, w_ref[:N_HG, :])
        cv = _mm_nt(h, w_ref[N_HG:N_HG + N_CV, :])
        cv_ref[...] = cv.astype(STASH)
        gt_ref[...] = _mm_nt(h, w_ref[N_HG + N_CV:, :]).astype(STASH)

        u = cv[:, :CONV_WIDTH] * cv[:, 2 * CONV_WIDTH:]
        row = lax.broadcasted_iota(jnp.int32, u.shape, 0)
        prev1 = tail_scr[HALO - 1:HALO, :]
        prev2 = tail_scr[HALO - 2:HALO - 1, :]
        u1 = jnp.where(row >= 1, pltpu.roll(u, 1, 0), prev1)
        u2 = jnp.where(row >= 2, pltpu.roll(u, 2, 0), jnp.where(row == 1, prev1, prev2))
        y = cw_ref[0:1, :] * u2 + cw_ref[1:2, :] * u1 + cw_ref[2:3, :] * u
        out = cv[:, CONV_WIDTH:2 * CONV_WIDTH] * y
        cvo_ref[...] = out.astype(BF16)
        cvot_ref[...] = out.T.astype(BF16)
        tail_scr[...] = u[tm - HALO:, :]

        _hg_fwd_tile(hg_ref, low_ref, gn_ref, o_ref, og_ref, ogt_ref, st_ref, s_scr, tm)

    row = lambda n: pl.BlockSpec((tm, n), lambda i: (i, 0))
    return pl.pallas_call(
        body, name="fwd_in", grid=(T // tm,),
        in_specs=[row(D_MODEL), _full((1, D_MODEL)), _resident(w_in_t.shape), _full((CONV_K, CONV_WIDTH)),
                  _full((2, HG_WIDTH)), _full((1, HEAD_DIM))],
        out_specs=[_col(tm, D_MODEL), row(N_HG), row(N_CV), row(N_GT), row(CONV_WIDTH), _col(tm, CONV_WIDTH),
                   row(HG_WIDTH), row(HG_WIDTH), _col(tm, HG_WIDTH),
                   pl.BlockSpec((N_HEADS, nc, HEAD_DIM, HEAD_DIM), lambda i: (0, i, 0, 0))],
        out_shape=[jax.ShapeDtypeStruct((D_MODEL, T), BF16), jax.ShapeDtypeStruct((T, N_HG), F32),
                   jax.ShapeDtypeStruct((T, N_CV), STASH), jax.ShapeDtypeStruct((T, N_GT), STASH),
                   jax.ShapeDtypeStruct((T, CONV_WIDTH), BF16), jax.ShapeDtypeStruct((CONV_WIDTH, T), BF16),
                   jax.ShapeDtypeStruct((T, HG_WIDTH), F32), jax.ShapeDtypeStruct((T, HG_WIDTH), BF16),
                   jax.ShapeDtypeStruct((HG_WIDTH, T), BF16),
                   jax.ShapeDtypeStruct((N_HEADS, T // CHUNK, HEAD_DIM, HEAD_DIM), F32)],
        scratch_shapes=[pltpu.VMEM((HALO, CONV_WIDTH), F32), pltpu.VMEM((N_HEADS, HEAD_DIM, HEAD_DIM), F32)],
        compiler_params=_params(("arbitrary",), vmem=VMEM_LIMIT_LARGE_V7X),
    )(x, g, w_in_t, conv_w, low, gn)


def _chunk_pos(shape):
    return lax.broadcasted_iota(jnp.int32, shape, 0) & (CHUNK - 1)


def _chunk_cumsum(x, pos):
    s = 1
    while s < CHUNK:
        x = x + jnp.where(pos >= s, pltpu.roll(x, s, 0), 0.0)
        s *= 2
    return x


def _chunk_rev_cumsum(x, pos):
    n = x.shape[0]
    s = 1
    while s < CHUNK:
        x = x + jnp.where(pos + s < CHUNK, pltpu.roll(x, n - s, 0), 0.0)
        s *= 2
    return x


def _lower_bound(low_ref):
    l0 = low_ref[0:1, :]
    l1 = low_ref[1:2, :]
    m = jnp.maximum(l0, l1)
    e0 = jnp.exp(l0 - m)
    e1 = jnp.exp(l1 - m)
    return e0 / (e0 + e1), e1 / (e0 + e1)


def _hg_gates(qr, fr, lb, pos, tb):
    sq = _sigmoid(qr)
    q = qr * sq * Q_SCALE
    sg = _sigmoid(fr)
    f = lb + (1.0 - lb) * sg
    k = 1.0 - f
    b = _chunk_cumsum(jnp.log(f), pos)
    b3 = b.reshape(tb // CHUNK, CHUNK, HEAD_DIM)
    anc = b3[:, CHUNK // 2 - 1:CHUNK // 2, :]
    last = b3[:, CHUNK - 1:CHUNK, :]
    d3 = b3 - anc
    e_qa3 = jnp.exp(d3)
    e_ka3 = jnp.exp(-d3)
    e_b3 = e_qa3 * jnp.exp(anc)
    e_ko3 = e_ka3 * jnp.exp(last - anc)
    dec = jnp.exp(last)
    flat = lambda a: a.reshape(tb, HEAD_DIM)
    return sq, q, sg, f, k, flat(e_qa3), flat(e_ka3), flat(e_b3), flat(e_ko3), dec


def _intra_mask(sb):
    r = lax.broadcasted_iota(jnp.int32, (sb, sb), 0)
    c = lax.broadcasted_iota(jnp.int32, (sb, sb), 1)
    return ((r // CHUNK) == (c // CHUNK)) & (c <= r)


def _hg_fwd_tile(hg_ref, low_ref, gn_ref, o_ref, og_ref, ogt_ref, st_ref, s_scr, tb):
    sb = min(256, tb)
    nc = tb // CHUNK
    q_ref, f_ref, i_ref, g_ref = (hg_ref.at[:, p * HG_WIDTH:(p + 1) * HG_WIDTH] for p in range(4))
    pos = _chunk_pos((tb, HEAD_DIM))
    mask = _intra_mask(sb)
    lanes = [slice(hh * HEAD_DIM, (hh + 1) * HEAD_DIM) for hh in range(N_HEADS)]
    qi, ko, vb, dec, st = [], [], [], [], []
    for hh, ln in enumerate(lanes):
        lb, _ = _lower_bound(low_ref.at[:, ln])
        _, q, _, _, k, e_qa, e_ka, e_b, e_ko, dec_h = _hg_gates(q_ref[:, ln], f_ref[:, ln], lb, pos, tb)
        qh = (q * e_qa).astype(BF16)
        kh = (k * e_ka).astype(BF16)
        qi.append((q * e_b).astype(BF16))
        ko.append((k * e_ko).astype(BF16))
        vb.append(i_ref[:, ln].astype(BF16))
        dec.append(dec_h)
        st.append(s_scr[hh])
        for s in range(tb // sb):
            sl = slice(s * sb, (s + 1) * sb)
            p = jnp.where(mask, _mm_nt(qh[sl], kh[sl]), 0.0)
            o_ref[sl, ln] = _mm(p, vb[hh][sl])
    for c in range(nc):
        sl = slice(c * CHUNK, (c + 1) * CHUNK)
        for hh, ln in enumerate(lanes):
            st_ref[hh, c] = st[hh]
            o_ref[sl, ln] = o_ref[sl, ln] + _mm_nt(qi[hh][sl], st[hh])
            st[hh] = dec[hh][c] * st[hh] + _mm_tn(vb[hh][sl], ko[hh][sl])
    for hh, ln in enumerate(lanes):
        s_scr[hh] = st[hh]
        o = o_ref[:, ln]
        r = lax.rsqrt(jnp.mean(o * o, axis=-1, keepdims=True) + EPS)
        gr = g_ref[:, ln]
        og = (o * r * gn_ref[...]) * (gr * _sigmoid(gr))
        og_ref[:, ln] = og.astype(BF16)
        ogt_ref[ln, :] = og.T.astype(BF16)


def _merge_fwd(og, cvo, gt, x, wa, wb, wo):
    T = x.shape[0]
    tm = min(1024, T)

    def body(og_ref, cvo_ref, gt_ref, x_ref, wa_ref, wb_ref, wo_ref, x1_ref, mgt_ref):
        ya = jnp.dot(og_ref[...], _shard_cols(wa_ref), preferred_element_type=F32)
        yb = jnp.dot(cvo_ref[...], _shard_cols(wb_ref), preferred_element_type=F32)
        m = (_sigmoid(gt_ref[:, :D_MODEL].astype(F32)) * ya
             + _sigmoid(gt_ref[:, D_MODEL:].astype(F32)) * yb)
        mgt_ref[...] = m.T.astype(BF16)
        x1_ref[...] = x_ref[...] + jnp.dot(m.astype(BF16), wo_ref[...], preferred_element_type=F32)

    row = lambda n: pl.BlockSpec((tm, n), lambda i: (i, 0))
    return pl.pallas_call(
        body, name="merge_fwd", grid=(T // tm,),
        in_specs=[row(HG_WIDTH), row(CONV_WIDTH), row(2 * D_MODEL), row(D_MODEL),
                  _resident(wa.shape), _resident(wb.shape), _resident(wo.shape)],
        out_specs=[row(D_MODEL), _col(tm, D_MODEL)],
        out_shape=[jax.ShapeDtypeStruct((T, D_MODEL), F32), jax.ShapeDtypeStruct((D_MODEL, T), BF16)],
        compiler_params=_params(("parallel",)),
    )(og, cvo, gt, x, wa, wb, wo)


def _ffn_fwd_loss(x1, g, wg, wu, wd, target, g_fin):
    T = x1.shape[0]
    tm = min(512, T)

    def body(x_ref, g_ref, wg_ref, wu_ref, wd_ref, t_ref, gf_ref,
             ht_ref, gate_ref, up_ref, act_ref, loss_ref, dgf_ref, dx2_ref, dx2t_ref):
        @pl.when(pl.program_id(0) == 0)
        def _():
            loss_ref[...] = jnp.zeros_like(loss_ref)
            dgf_ref[...] = jnp.zeros_like(dgf_ref)

        xv = x_ref[...]
        r = lax.rsqrt(jnp.mean(xv * xv, axis=-1, keepdims=True) + EPS)
        hf = xv * r * g_ref[...]
        h = hf.astype(BF16)
        ht_ref[...] = hf.T.astype(BF16)
        gate = _mm_nt(h, wg_ref[...])
        up = _mm_nt(h, wu_ref[...])
        gate_ref[...] = gate.astype(STASH)
        up_ref[...] = up.astype(STASH)
        act = (gate * _sigmoid(gate) * up).astype(BF16)
        act_ref[...] = act
        x2 = xv + jnp.dot(act, wd_ref[...], preferred_element_type=F32)

        gv = gf_ref[...]
        r2 = lax.rsqrt(jnp.mean(x2 * x2, axis=-1, keepdims=True) + EPS)
        xh = x2 * r2
        err = xh * gv - t_ref[...]
        loss_ref[...] += 0.5 * jnp.sum(jnp.mean(err * err, axis=-1, keepdims=True), axis=0, keepdims=True)
        dy = err * (1.0 / D_MODEL)
        dgf_ref[...] += jnp.sum(dy * xh, axis=0, keepdims=True)
        w = dy * gv
        dx2 = r2 * (w - xh * jnp.mean(w * xh, axis=-1, keepdims=True))
        dx2_ref[...] = dx2
        dx2t_ref[...] = dx2.T.astype(BF16)

    row = lambda n: pl.BlockSpec((tm, n), lambda i: (i, 0))
    return pl.pallas_call(
        body, name="ffn_fwd_loss", grid=(T // tm,),
        in_specs=[row(D_MODEL), _full((1, D_MODEL)), _resident(wg.shape), _resident(wu.shape), _resident(wd.shape),
                  row(D_MODEL), _full((1, D_MODEL))],
        out_specs=[_col(tm, D_MODEL), row(D_FF), row(D_FF), row(D_FF), _full((1, 128)), _full((1, D_MODEL)),
                   row(D_MODEL), _col(tm, D_MODEL)],
        out_shape=[jax.ShapeDtypeStruct((D_MODEL, T), BF16), jax.ShapeDtypeStruct((T, D_FF), STASH),
                   jax.ShapeDtypeStruct((T, D_FF), STASH), jax.ShapeDtypeStruct((T, D_FF), BF16),
                   jax.ShapeDtypeStruct((1, 128), F32), jax.ShapeDtypeStruct((1, D_MODEL), F32),
                   jax.ShapeDtypeStruct((T, D_MODEL), F32), jax.ShapeDtypeStruct((D_MODEL, T), BF16)],
        compiler_params=_params(("arbitrary",), vmem=VMEM_LIMIT_LARGE_V7X),
    )(x1, g, wg, wu, wd, target, g_fin)


def _ffn_bwd(dx2, x1, gate, up, g, wg, wu, wd):
    T = x1.shape[0]
    tm = min(512, T)

    def body(dx2_ref, x_ref, gate_ref, up_ref, g_ref, wg_ref, wu_ref, wd_ref, dgate_ref, dup_ref, dx1_ref, dgn_ref):
        @pl.when(pl.program_id(0) == 0)
        def _():
            dgn_ref[...] = jnp.zeros_like(dgn_ref)

        dx2 = dx2_ref[...]
        dact = _mm_nt(dx2, wd_ref[...])
        gate = gate_ref[...].astype(F32)
        s = _sigmoid(gate)
        dgate = (dact * up_ref[...].astype(F32) * (s * (1.0 + gate * (1.0 - s)))).astype(BF16)
        dup = (dact * (gate * s)).astype(BF16)
        dgate_ref[...] = dgate
        dup_ref[...] = dup
        dh = _mm(dgate, wg_ref[...]) + _mm(dup, wu_ref[...])
        xv = x_ref[...]
        r = lax.rsqrt(jnp.mean(xv * xv, axis=-1, keepdims=True) + EPS)
        xh = xv * r
        dgn_ref[...] += jnp.sum(dh * xh, axis=0, keepdims=True)
        w = dh * g_ref[...]
        dx1_ref[...] = dx2 + r * (w - xh * jnp.mean(w * xh, axis=-1, keepdims=True))

    row = lambda n: pl.BlockSpec((tm, n), lambda i: (i, 0))
    return pl.pallas_call(
        body, name="ffn_bwd", grid=(T // tm,),
        in_specs=[row(D_MODEL), row(D_MODEL), row(D_FF), row(D_FF), _full((1, D_MODEL)),
                  _resident(wg.shape), _resident(wu.shape), _resident(wd.shape)],
        out_specs=[row(D_FF), row(D_FF), row(D_MODEL), _full((1, D_MODEL))],
        out_shape=[jax.ShapeDtypeStruct((T, D_FF), BF16), jax.ShapeDtypeStruct((T, D_FF), BF16),
                   jax.ShapeDtypeStruct((T, D_MODEL), F32), jax.ShapeDtypeStruct((1, D_MODEL), F32)],
        compiler_params=_params(("arbitrary",), vmem=VMEM_LIMIT_LARGE_V7X),
    )(dx2, x1, gate, up, g, wg, wu, wd)


def _merge_bwd(dx1, og, cvo, gt, cv, wa, wb, wo, conv_w):
    T = dx1.shape[0]
    tm = min(512, T)
    nt = T // tm

    def body(dx_ref, og_ref, cvo_ref, gt_ref, cv_ref, halo_ref, wa_ref, wb_ref, wo_ref, cw_ref,
             dgt_ref, dya_ref, dyb_ref, dog_ref, dcv_ref, dcw_ref, prev_u, next_dy):
        step = pl.program_id(0)

        @pl.when(step == 0)
        def _():
            next_dy[...] = jnp.zeros_like(next_dy)
            dcw_ref[...] = jnp.zeros_like(dcw_ref)

        dm = _mm_nt(dx_ref[...], wo_ref[...])
        wa = _shard_cols(wa_ref)
        wb = _shard_cols(wb_ref)
        ya = jnp.dot(og_ref[...], wa, preferred_element_type=F32)
        yb = jnp.dot(cvo_ref[...], wb, preferred_element_type=F32)
        sa = _sigmoid(gt_ref[:, :D_MODEL].astype(F32))
        sb = _sigmoid(gt_ref[:, D_MODEL:].astype(F32))
        dgt_ref[:, :D_MODEL] = (dm * ya * (sa * (1.0 - sa))).astype(BF16)
        dgt_ref[:, D_MODEL:] = (dm * yb * (sb * (1.0 - sb))).astype(BF16)
        dya = (dm * sa).astype(BF16)
        dyb = (dm * sb).astype(BF16)
        dya_ref[...] = dya
        dyb_ref[...] = dyb
        dog_ref[...] = _mm_nt(dya, wa)
        dcvo = _mm_nt(dyb, wb)

        cvt = cv_ref[...].astype(F32)
        c, bg, xb = cvt[:, :CONV_WIDTH], cvt[:, CONV_WIDTH:2 * CONV_WIDTH], cvt[:, 2 * CONV_WIDTH:]
        halo = halo_ref[...].astype(F32)
        first_tile = step == nt - 1
        prev_u[...] = jnp.where(first_tile, 0.0, halo[:, :CONV_WIDTH] * halo[:, 2 * CONV_WIDTH:])
        u = c * xb
        row = lax.broadcasted_iota(jnp.int32, u.shape, 0)
        p1 = prev_u[HALO - 1:HALO, :]
        p2 = prev_u[HALO - 2:HALO - 1, :]
        u1 = jnp.where(row >= 1, pltpu.roll(u, 1, 0), p1)
        u2 = jnp.where(row >= 2, pltpu.roll(u, 2, 0), jnp.where(row == 1, p1, p2))
        w0, w1, w2 = cw_ref[0:1, :], cw_ref[1:2, :], cw_ref[2:3, :]
        y = w0 * u2 + w1 * u1 + w2 * u
        dcv_ref[:, CONV_WIDTH:2 * CONV_WIDTH] = (dcvo * y).astype(BF16)
        dy = dcvo * bg
        dcw_ref[0:1, :] += jnp.sum(dy * u2, axis=0, keepdims=True)
        dcw_ref[1:2, :] += jnp.sum(dy * u1, axis=0, keepdims=True)
        dcw_ref[2:3, :] += jnp.sum(dy * u, axis=0, keepdims=True)
        n1 = next_dy[0:1, :]
        n2 = next_dy[1:2, :]
        dy1 = jnp.where(row < tm - 1, pltpu.roll(dy, tm - 1, 0), n1)
        dy2 = jnp.where(row < tm - 2, pltpu.roll(dy, tm - 2, 0), jnp.where(row == tm - 2, n1, n2))
        du = w2 * dy + w1 * dy1 + w0 * dy2
        dcv_ref[:, :CONV_WIDTH] = (du * xb).astype(BF16)
        dcv_ref[:, 2 * CONV_WIDTH:] = (du * c).astype(BF16)
        next_dy[...] = dy[:HALO, :]

    rt = lambda i: nt - 1 - i
    row = lambda n: pl.BlockSpec((tm, n), lambda i: (rt(i), 0))
    halo = pl.BlockSpec((HALO, N_CV), lambda i: (jnp.maximum(rt(i) * (tm // HALO) - 1, 0), 0))
    return pl.pallas_call(
        body, name="merge_bwd", grid=(nt,),
        in_specs=[row(D_MODEL), row(HG_WIDTH), row(CONV_WIDTH), row(2 * D_MODEL), row(N_CV), halo,
                  _resident(wa.shape), _resident(wb.shape), _resident(wo.shape), _full((CONV_K, CONV_WIDTH))],
        out_specs=[row(2 * D_MODEL), row(D_MODEL), row(D_MODEL), row(HG_WIDTH), row(N_CV),
                   _full((CONV_K, CONV_WIDTH))],
        out_shape=[jax.ShapeDtypeStruct((T, 2 * D_MODEL), BF16), jax.ShapeDtypeStruct((T, D_MODEL), BF16),
                   jax.ShapeDtypeStruct((T, D_MODEL), BF16), jax.ShapeDtypeStruct((T, HG_WIDTH), F32),
                   jax.ShapeDtypeStruct((T, N_CV), BF16), jax.ShapeDtypeStruct((CONV_K, CONV_WIDTH), F32)],
        scratch_shapes=[pltpu.VMEM((HALO, CONV_WIDTH), F32), pltpu.VMEM((HALO, CONV_WIDTH), F32)],
        compiler_params=_params(("arbitrary",)),
    )(dx1, og, cvo, gt, cv, cv, wa, wb, wo, conv_w)


def _drop_operands(body, first, count):
    def wrapped(*refs):
        return body(*refs[:first], *refs[first + count:])
    return wrapped


def _hg_bwd(dog, hg, o, st, low, gn, after=()):
    T = hg.shape[0]
    tb = min(512, T)
    sb = min(256, tb)
    nb = T // tb
    nc = tb // CHUNK
    wid = HEADS_PER_STEP * HEAD_DIM

    def body(q_ref, f_ref, i_ref, g_ref, low_ref, gn_ref, o_ref, dog_ref, st_ref,
             dhg_ref, dlow_ref, dgn_ref,
             ds_scr, dqi_scr, dko_scr, dv_scr, dd_scr, dqh_scr, dkh_scr):
        h = pl.program_id(0)
        t = pl.program_id(1)
        dq_ref, df_ref, di_ref, dg_ref = (dhg_ref.at[:, p * HG_WIDTH:(p + 1) * HG_WIDTH] for p in range(4))

        @pl.when(t == 0)
        def _():
            ds_scr[...] = jnp.zeros_like(ds_scr)
            dlow_ref[...] = jnp.zeros_like(dlow_ref)

        @pl.when((t == 0) & (h == 0))
        def _():
            dgn_ref[...] = jnp.zeros_like(dgn_ref)

        pos = _chunk_pos((tb, HEAD_DIM))
        mask = _intra_mask(sb)
        gnv = gn_ref[...]
        lanes = [slice(hh * HEAD_DIM, (hh + 1) * HEAD_DIM) for hh in range(HEADS_PER_STEP)]
        heads = []
        for hh, ln in enumerate(lanes):
            lb, lb1 = _lower_bound(low_ref.at[:, ln])
            qr = q_ref[:, ln]
            sq, q, sg, f, k, e_qa, e_ka, e_b, e_ko, dec = _hg_gates(qr, f_ref[:, ln], lb, pos, tb)

            gr = g_ref[:, ln]
            o = o_ref[:, ln]
            dog_v = dog_ref[:, ln]
            sgr = _sigmoid(gr)
            r = lax.rsqrt(jnp.mean(o * o, axis=-1, keepdims=True) + EPS)
            oh = o * r
            dg_ref[:, ln] = (dog_v * (oh * gnv) * (sgr * (1.0 + gr * (1.0 - sgr)))).astype(BF16)
            don = dog_v * (gr * sgr)
            dgn_ref[...] += jnp.sum(don * oh, axis=0, keepdims=True)
            w = don * gnv
            do = (r * (w - oh * jnp.mean(w * oh, axis=-1, keepdims=True))).astype(BF16)

            qh = (q * e_qa).astype(BF16)
            kh = (k * e_ka).astype(BF16)
            qi = (q * e_b).astype(BF16)
            ko = (k * e_ko).astype(BF16)
            vb = i_ref[:, ln].astype(BF16)

            for s in range(tb // sb):
                sl = slice(s * sb, (s + 1) * sb)
                p = jnp.where(mask, _mm_nt(qh[sl], kh[sl]), 0.0).astype(BF16)
                dp = jnp.where(mask, _mm_nt(do[sl], vb[sl]), 0.0).astype(BF16)
                dv_scr[sl, ln] = _mm_tn(p, do[sl])
                dqh_scr[sl, ln] = _mm(dp, kh[sl])
                dkh_scr[sl, ln] = _mm_tn(dp, qh[sl])
            heads.append(dict(lb=lb, lb1=lb1, qr=qr, sq=sq, q=q, sg=sg, f=f, k=k, e_qa=e_qa, e_ka=e_ka, e_b=e_b,
                              e_ko=e_ko, dec=dec, do=do, qi=qi, ko=ko, vb=vb, ds=ds_scr[hh]))

        for c in reversed(range(nc)):
            sl = slice(c * CHUNK, (c + 1) * CHUNK)
            for hh, ln in enumerate(lanes):
                hd = heads[hh]
                ds = hd["ds"]
                st_c = st_ref[hh, c]
                dqi_scr[sl, ln] = _mm(hd["do"][sl], st_c)
                dko_scr[sl, ln] = _mm(hd["vb"][sl], ds)
                dv_scr[sl, ln] = dv_scr[sl, ln] + _mm_nt(hd["ko"][sl], ds)
                dec_c = hd["dec"][c]
                dd_scr[sl, ln] = jnp.broadcast_to(dec_c * jnp.sum(ds * st_c, axis=0, keepdims=True),
                                                  (CHUNK, HEAD_DIM))
                hd["ds"] = dec_c * ds + _mm_tn(hd["do"][sl], hd["qi"][sl])

        for hh, ln in enumerate(lanes):
            hd = heads[hh]
            ds_scr[hh] = hd["ds"]
            q, k, lb = hd["q"], hd["k"], hd["lb"]
            dko_e = dko_scr[:, ln] * hd["e_ko"]
            dq = dqh_scr[:, ln] * hd["e_qa"] + dqi_scr[:, ln] * hd["e_b"]
            dk = dkh_scr[:, ln] * hd["e_ka"] + dko_e
            kd3 = (k * dko_e).reshape(nc, CHUNK, HEAD_DIM)
            last = jnp.broadcast_to(jnp.sum(kd3, axis=1, keepdims=True), kd3.shape).reshape(tb, HEAD_DIM)
            db = q * dq - k * dk + jnp.where(pos == CHUNK - 1, dd_scr[:, ln] + last, 0.0)
            dlg = _chunk_rev_cumsum(db, pos)
            dfv = dlg / hd["f"] - dk
            s_low = jnp.sum(dfv * (1.0 - hd["sg"]), axis=0, keepdims=True)
            dlow_ref[0:1, ln] += s_low * lb * (1.0 - lb)
            dlow_ref[1:2, ln] += -s_low * lb * hd["lb1"]
            df_ref[:, ln] = (dfv * (1.0 - lb) * hd["sg"] * (1.0 - hd["sg"])).astype(BF16)
            dq_ref[:, ln] = (dq * Q_SCALE * (hd["sq"] * (1.0 + hd["qr"] * (1.0 - hd["sq"])))).astype(BF16)
            di_ref[:, ln] = dv_scr[:, ln].astype(BF16)

    rt = lambda t: nb - 1 - t
    col = lambda p: pl.BlockSpec((tb, wid), lambda h, t: (rt(t), p * HEAD_GROUPS + h))
    hcol = pl.BlockSpec((tb, wid), lambda h, t: (rt(t), h))
    assert HEAD_GROUPS == 1
    tile = pltpu.VMEM((tb, wid), F32)
    return pl.pallas_call(
        _drop_operands(body, 9, len(after)), name="hg_bwd", grid=(HEAD_GROUPS, nb),
        in_specs=[col(0), col(1), col(2), col(3), pl.BlockSpec((2, wid), lambda h, t: (0, h)),
                  pl.BlockSpec((1, HEAD_DIM), lambda h, t: (0, 0)), hcol, hcol,
                  pl.BlockSpec((HEADS_PER_STEP, nc, HEAD_DIM, HEAD_DIM), lambda h, t: (h, rt(t), 0, 0))]
                 + [HBM_SPEC] * len(after),
        out_specs=[pl.BlockSpec((tb, N_HG), lambda h, t: (rt(t), 0)), pl.BlockSpec((2, wid), lambda h, t: (0, h)),
                   pl.BlockSpec((1, HEAD_DIM), lambda h, t: (0, 0))],
        out_shape=[jax.ShapeDtypeStruct((T, N_HG), BF16), jax.ShapeDtypeStruct((2, HG_WIDTH), F32),
                   jax.ShapeDtypeStruct((1, HEAD_DIM), F32)],
        scratch_shapes=[pltpu.VMEM((HEADS_PER_STEP, HEAD_DIM, HEAD_DIM), F32), tile, tile, tile, tile, tile, tile],
        compiler_params=_params(("arbitrary", "arbitrary")),
    )(hg, hg, hg, hg, low, gn, o, dog, st, *after)


def _in_bwd(dparts, w_in, x, dx1, g, after=()):
    T = x.shape[0]
    tm = min(512, T)
    widths = [p.shape[1] for p in dparts]
    offs = [sum(widths[:i]) for i in range(len(widths))]
    n = len(dparts)

    def body(*refs):
        d_refs = refs[:n]
        w_ref, x_ref, dx1_ref, g_ref, dx_ref, dgn_ref = refs[n:]

        @pl.when(pl.program_id(0) == 0)
        def _():
            dgn_ref[...] = jnp.zeros_like(dgn_ref)

        dh = None
        for d_ref, off, wd in zip(d_refs, offs, widths):
            part = _mm(d_ref[...], w_ref[off:off + wd, :])
            dh = part if dh is None else dh + part
        xv = x_ref[...]
        r = lax.rsqrt(jnp.mean(xv * xv, axis=-1, keepdims=True) + EPS)
        xh = xv * r
        dgn_ref[...] += jnp.sum(dh * xh, axis=0, keepdims=True)
        w = dh * g_ref[...]
        dx_ref[...] = dx1_ref[...] + r * (w - xh * jnp.mean(w * xh, axis=-1, keepdims=True))

    row = lambda m: pl.BlockSpec((tm, m), lambda i: (i, 0))
    return pl.pallas_call(
        _drop_operands(body, n + 4, len(after)), name="in_bwd", grid=(T // tm,),
        in_specs=[row(wd) for wd in widths] + [_resident(w_in.shape), row(D_MODEL), row(D_MODEL), _full((1, D_MODEL))]
                 + [HBM_SPEC] * len(after),
        out_specs=[row(D_MODEL), _full((1, D_MODEL))],
        out_shape=[jax.ShapeDtypeStruct((T, D_MODEL), F32), jax.ShapeDtypeStruct((1, D_MODEL), F32)],
        compiler_params=_params(("arbitrary",)),
    )(*dparts, w_in, x, dx1, g, *after)


def _wgrad_ffn(h2t, dgate, dup, dx2t, act, tn=256):
    M, T = h2t.shape
    N = dgate.shape[1]

    def body(h_ref, x_ref, dg_ref, du_ref, act_ref, og_ref, ou_ref, od_ref):
        h = h_ref[...]
        og_ref[...] = _mm(h, dg_ref[...]).T.astype(BF16)
        ou_ref[...] = _mm(h, du_ref[...]).T.astype(BF16)
        od_ref[...] = _mm(x_ref[...], act_ref[...]).T.astype(BF16)

    rhs = pl.BlockSpec((T, tn), lambda j: (0, j))
    out_spec = pl.BlockSpec((tn, M), lambda j: (j, 0))
    out = jax.ShapeDtypeStruct((N, M), BF16)
    return pl.pallas_call(
        body, name="wgrad_ffn", grid=(N // tn,),
        in_specs=[_resident((M, T)), _resident((M, T)), rhs, rhs, rhs], out_specs=[out_spec] * 3,
        out_shape=[out, out, out],
        compiler_params=_params(("parallel",)),
    )(h2t, dx2t, dgate, dup, act)


def _wgrad_out_branches(ogt, dya, cvot, dyb, mgt, dx1, after=()):
    M, T = ogt.shape
    N = dya.shape[1]
    c = N // N_DEV
    per = 2
    tn = per * c

    def body(at_ref, da_ref, bt_ref, db_ref, mt_ref, dx_ref, oa_ref, ob_ref, oo_ref):
        ga = _mm(at_ref[...], da_ref[...])
        gb = _mm(bt_ref[...], db_ref[...])
        for s in range(per):
            oa_ref[s] = ga[:, s * c:(s + 1) * c].astype(BF16)
            ob_ref[s] = gb[:, s * c:(s + 1) * c].astype(BF16)
        oo_ref[...] = _mm(mt_ref[...], dx_ref[...]).astype(BF16)

    rhs = pl.BlockSpec((T, tn), lambda j: (0, j))
    owners = pl.BlockSpec((per, M, c), lambda j: (j, 0, 0))
    out = jax.ShapeDtypeStruct((N_DEV, M, c), BF16)
    return pl.pallas_call(
        _drop_operands(body, 6, len(after)), name="wgrad_out_branches", grid=(N // tn,),
        in_specs=[_resident((M, T)), rhs, _resident((M, T)), rhs, _resident(mgt.shape), rhs] + [HBM_SPEC] * len(after),
        out_specs=[owners, owners, pl.BlockSpec((mgt.shape[0], tn), lambda j: (0, j))],
        out_shape=[out, out, jax.ShapeDtypeStruct((mgt.shape[0], dx1.shape[1]), BF16)],
        compiler_params=_params(("parallel",)),
    )(ogt, dya, cvot, dyb, mgt, dx1, *after)


def _wgrad_in(ht, dparts, after=(), riders=()):
    M, T = ht.shape
    tn = 512
    nblk = [p.shape[1] // tn for p in dparts]
    start = [sum(nblk[:i]) for i in range(len(nblk))]
    n = len(dparts)
    steps = sum(nblk)
    nr = len(riders)
    rows = [r[0].shape[0] // steps for r in riders]
    assert all(r[0].shape[0] == rr * steps and rr % 16 == 0 for r, rr in zip(riders, rows))

    def body(a_ref, *refs):
        d_refs = refs[:n]
        rider_in = refs[n:n + 4 * nr]
        o_ref = refs[n + 4 * nr]
        rider_out = refs[n + 4 * nr + 1:]
        j = pl.program_id(0)
        for d_ref, s, nb in zip(d_refs, start, nblk):
            @pl.when((j >= s) & (j < s + nb))
            def _():
                o_ref[...] = _mm(a_ref[...], d_ref[...]).T.astype(BF16)
        for i in range(nr):
            w_ref, p_ref, m_ref, v_ref = rider_in[4 * i:4 * i + 4]
            g = p_ref[0].astype(F32)
            for k in range(1, 4):
                g = g + p_ref[k].astype(F32)
            delta, m_new, v_new = _adamw_math(w_ref[...], g, m_ref[...], v_ref[...])
            for k, val in enumerate((g, delta, m_new, v_new)):
                rider_out[4 * i + k][...] = val

    def piece_spec(s, nb):
        return pl.BlockSpec((T, tn), lambda j: (0, jnp.clip(j - s, 0, nb - 1)))

    rider_specs, rider_out_specs, rider_out_shape, rider_args = [], [], [], []
    for (w, parts, m, v), rr in zip(riders, rows):
        blk = pl.BlockSpec((rr, w.shape[1]), lambda j: (j, 0))
        rider_specs += [blk, pl.BlockSpec((4, rr, w.shape[1]), lambda j: (0, j, 0)), blk, blk]
        rider_out_specs += [blk] * 4
        rider_out_shape += [jax.ShapeDtypeStruct(w.shape, F32)] * 4
        rider_args += [w, parts, m, v]
    outs = pl.pallas_call(
        _drop_operands(body, 1 + n + 4 * nr, len(after)), name="wgrad_in", grid=(steps,),
        in_specs=[_resident((M, T))] + [piece_spec(s, nb) for s, nb in zip(start, nblk)] + rider_specs
                 + [HBM_SPEC] * len(after),
        out_specs=[pl.BlockSpec((tn, M), lambda j: (j, 0))] + rider_out_specs,
        out_shape=[jax.ShapeDtypeStruct((steps * tn, M), BF16)] + rider_out_shape,
        compiler_params=_params(("parallel",)),
    )(ht, *dparts, *rider_args, *after)
    return outs[0], [outs[1 + 4 * i:5 + 4 * i] for i in range(nr)]


def _adamw_math(w, g, m, v):
    m = ADAM_B1 * m + (1.0 - ADAM_B1) * g
    v = ADAM_B2 * v + (1.0 - ADAM_B2) * (g * g)
    m_hat = m / (1.0 - ADAM_B1 ** ADAM_STEP)
    v_hat = v / (1.0 - ADAM_B2 ** ADAM_STEP)
    delta = -ADAM_LR * (m_hat / (jnp.sqrt(v_hat) + ADAM_EPS) + ADAM_WD * w)
    return delta, m, v


def _adamw_sum(name, ws, parts, ms, vs):
    n = len(ws)
    steps = min(_row_steps(w.shape[0]) for w in ws)
    rows = [w.shape[0] // steps for w in ws]

    def body(*refs):
        w_refs, p_refs, m_refs, v_refs = (refs[k * n:(k + 1) * n] for k in range(4))
        out_refs = refs[4 * n:]
        for i in range(n):
            g = p_refs[i][0].astype(F32)
            for k in range(1, 4):
                g = g + p_refs[i][k].astype(F32)
            delta, m_new, v_new = _adamw_math(w_refs[i][...], g, m_refs[i][...], v_refs[i][...])
            for k, val in enumerate((g, delta, m_new, v_new)):
                out_refs[4 * i + k][...] = val

    blk = [pl.BlockSpec((r, w.shape[1]), lambda s: (s, 0)) for r, w in zip(rows, ws)]
    pblk = [pl.BlockSpec((4, r, w.shape[1]), lambda s: (0, s, 0)) for r, w in zip(rows, ws)]
    out_specs, out_shape = [], []
    for b, w in zip(blk, ws):
        out_specs += [b] * 4
        out_shape += [jax.ShapeDtypeStruct(w.shape, F32)] * 4
    flat = pl.pallas_call(
        body, name=name, grid=(steps,),
        in_specs=blk + pblk + blk + blk, out_specs=out_specs, out_shape=out_shape,
        compiler_params=_params(("parallel",)),
    )(*ws, *parts, *ms, *vs)
    return [flat[4 * i:4 * i + 4] for i in range(n)]


_SMALL_SLOTS = (("norm_mix_g", 0, 1, 1024), ("norm_ffn_g", 1, 1, 1024), ("norm_final_g", 2, 1, 1024),
                ("lower_bounds", 3, 2, 512), ("hg_norm_g", 5, 1, 128), ("loss", 6, 1, 128), ("conv_w", 8, 3, 512))
_SMALL_PARAMS = tuple(s for s in _SMALL_SLOTS if s[0] != "loss")
CONV_SHARD = CONV_WIDTH // N_DEV


def _small_pack(small):
    def body(*refs):
        out = refs[-1]
        out[...] = jnp.zeros_like(out)
        for ref, (_, row, rows, lanes) in zip(refs[:-1], _SMALL_SLOTS):
            out[row:row + rows, 0:lanes] = ref[...]

    vmem = pl.BlockSpec(memory_space=pltpu.VMEM)
    return pl.pallas_call(
        body, name="small_pack", in_specs=[vmem] * len(_SMALL_SLOTS), out_specs=vmem,
        out_shape=jax.ShapeDtypeStruct((SMALL_ROWS, 1024), F32),
    )(*[small[name] for name, _, _, _ in _SMALL_SLOTS])


def _small_update(gathered, dev, w, m, v):
    n = len(_SMALL_PARAMS)

    def body(dev_ref, g_ref, *refs):
        w_refs, m_refs, v_refs = refs[:n], refs[n:2 * n], refs[2 * n:3 * n]
        loss_ref, out_refs, sum_scr = refs[3 * n], refs[3 * n + 1:-1], refs[-1]
        total = g_ref[0]
        for k in range(1, N_DEV):
            total = total + g_ref[k]
        sum_scr[...] = total
        loss_ref[...] = sum_scr[6:7, 0:128]
        for p, (name, row, rows, lanes) in enumerate(_SMALL_PARAMS):
            if name == "conv_w":
                for r in range(rows):
                    g = sum_scr[row + r:row + r + 1, 0:CONV_SHARD]
                    for s in range(1, N_DEV):
                        mine = sum_scr[row + r:row + r + 1, s * CONV_SHARD:(s + 1) * CONV_SHARD]
                        g = jnp.where(dev_ref[0] == s, mine, g)
                    delta, m_new, v_new = _adamw_math(w_refs[p][r], g, m_refs[p][r], v_refs[p][r])
                    out_refs[4 * p][r] = g
                    out_refs[4 * p + 1][r] = delta
                    out_refs[4 * p + 2][r] = m_new
                    out_refs[4 * p + 3][r] = v_new
                continue
            g = sum_scr[row:row + rows, 0:lanes]
            delta, m_new, v_new = _adamw_math(w_refs[p][...], g, m_refs[p][...], v_refs[p][...])
            out_refs[4 * p][...] = g
            out_refs[4 * p + 1][...] = delta
            out_refs[4 * p + 2][...] = m_new
            out_refs[4 * p + 3][...] = v_new

    vmem = pl.BlockSpec(memory_space=pltpu.VMEM)
    outs = [jax.ShapeDtypeStruct((1, 128), F32)]
    for a in w:
        outs += [jax.ShapeDtypeStruct(a.shape, F32)] * 4
    return pl.pallas_call(
        body, name="small_update",
        in_specs=[pl.BlockSpec(memory_space=pltpu.SMEM)] + [vmem] * (1 + 3 * n), out_specs=[vmem] * len(outs),
        out_shape=outs, scratch_shapes=[pltpu.VMEM((SMALL_ROWS, 1024), F32)],
    )(dev, gathered, *w, *m, *v)


def _row_steps(rows):
    for steps in (4, 2):
        if rows % (16 * steps) == 0:
            return steps
    return 1


def _pair_sum(name, by_owner, got, core, after=()):
    n = len(got)

    def body(core_ref, *refs):
        for a_ref, b_ref, o_ref in zip(refs[:n], refs[n:2 * n], refs[2 * n:]):
            o_ref[...] = (a_ref[...].astype(F32) + b_ref[...].astype(F32)).astype(BF16)

    def blk(g):
        return pl.BlockSpec((None,) + g.shape[1:], lambda k, core_ref: (k, 0, 0))

    def mine(g):
        return pl.BlockSpec((None,) + g.shape[1:], lambda k, core_ref: (2 * k + core_ref[0], 0, 0))

    return pl.pallas_call(
        _drop_operands(body, 1 + 2 * n, len(after)), name=name,
        grid_spec=pltpu.PrefetchScalarGridSpec(
            num_scalar_prefetch=1, grid=(4,),
            in_specs=[mine(g) for g in got] + [blk(g) for g in got] + [HBM_SPEC] * len(after),
            out_specs=[blk(g) for g in got]),
        out_shape=[jax.ShapeDtypeStruct(g.shape, BF16) for g in got],
        compiler_params=_params(("parallel",)),
    )(core, *by_owner, *got, *after)


MESH = pl.DeviceIdType.MESH
HBM_SPEC = pl.BlockSpec(memory_space=pl.ANY)


def _handshake(peers):
    barrier = pltpu.get_barrier_semaphore()
    for peer in peers:
        pl.semaphore_signal(barrier, inc=1, device_id=peer, device_id_type=MESH)
    pl.semaphore_wait(barrier, len(peers))


def _comm_call(body, name, operands, out_shape, scratch, collective_id):
    if collective_id is None:
        return pl.pallas_call(body, name=name, in_specs=[HBM_SPEC] * len(operands), out_specs=[HBM_SPEC] * len(out_shape),
                              out_shape=out_shape, scratch_shapes=scratch)(*operands)
    return pl.kernel(body, out_type=out_shape, mesh=plsc.ScalarSubcoreMesh(axis_name="sequencer", num_cores=1),
                     scratch_types=scratch, name=name,
                     compiler_params=pltpu.CompilerParams(collective_id=collective_id))(*operands)


def _all_gather(name, blocks, collective_id=None, after=()):
    n = len(blocks)
    na = len(after)

    def body(*refs):
        x_refs, out_refs = refs[:n], refs[n + na:2 * n + na]
        send_sems, recv_sems, local_sems = refs[2 * n + na:]
        x, y, c = lax.axis_index("x"), lax.axis_index("y"), lax.axis_index("c")
        me, sibling = (x, y, c), (x, y, 1 - c)
        chips = [(1 - x, y), (x, 1 - y), (1 - x, 1 - y)]
        if collective_id is not None:
            _handshake([sibling] + [(*chip, c) for chip in chips])

        def slot(i, px, py, pc):
            return out_refs[i].at[4 * px + 2 * py + pc]

        def copy(i, k, blk, to, src=None):
            return pltpu.make_async_remote_copy(
                src_ref=slot(i, *blk) if src is None else src, dst_ref=slot(i, *blk),
                send_sem=send_sems.at[7 * i + k], recv_sem=recv_sems.at[7 * i + k], device_id=to, device_id_type=MESH)

        mine = [pltpu.make_async_copy(x_refs[i], slot(i, *me), local_sems.at[i]) for i in range(n)]
        for cp in mine:
            cp.start()
        first = []
        for i in range(n):
            first.append(copy(i, 0, me, sibling, src=x_refs[i]))
            first += [copy(i, 1 + j, me, (*chip, c), src=x_refs[i]) for j, chip in enumerate(chips)]
        for cp in first:
            cp.start()
        passed = []
        for i in range(n):
            for j, chip in enumerate(chips):
                copy(i, 1 + j, (*chip, c), me).wait_recv()
                passed.append(copy(i, 4 + j, (*chip, c), sibling))
                passed[-1].start()
        for i in range(n):
            copy(i, 0, sibling, me).wait_recv()
            for j, chip in enumerate(chips):
                copy(i, 4 + j, (*chip, 1 - c), me).wait_recv()
        for cp in first + passed:
            cp.wait_send()
        for cp in mine:
            cp.wait()

    return _comm_call(
        body, name, list(blocks) + list(after), [jax.ShapeDtypeStruct((N_DEV,) + b.shape, b.dtype) for b in blocks],
        [pltpu.SemaphoreType.DMA((7 * n,)), pltpu.SemaphoreType.DMA((7 * n,)), pltpu.SemaphoreType.DMA((n,))],
        collective_id)


def _sibling_swap(name, by_owner, collective_id=None, after=()):
    n = len(by_owner)
    na = len(after)

    def body(*refs):
        x_refs, out_refs = refs[:n], refs[n + na:2 * n + na]
        send_sems, recv_sems = refs[2 * n + na:]
        x, y, c = lax.axis_index("x"), lax.axis_index("y"), lax.axis_index("c")
        if collective_id is not None:
            _handshake([(x, y, 1 - c)])
        copies = []
        for i in range(n):
            for k in range(4):
                copies.append(pltpu.make_async_remote_copy(
                    src_ref=x_refs[i].at[2 * k + 1 - c], dst_ref=out_refs[i].at[k],
                    send_sem=send_sems.at[4 * i + k], recv_sem=recv_sems.at[4 * i + k],
                    device_id=(x, y, 1 - c), device_id_type=MESH))
        for cp in copies:
            cp.start()
        for cp in copies:
            cp.wait()

    return _comm_call(
        body, name, list(by_owner) + list(after),
        [jax.ShapeDtypeStruct((4,) + b.shape[1:], b.dtype) for b in by_owner],
        [pltpu.SemaphoreType.DMA((4 * n,)), pltpu.SemaphoreType.DMA((4 * n,))], collective_id)


def _chip_exchange(name, sums, collective_id=None, after=()):
    n = len(sums)
    na = len(after)

    def body(*refs):
        x_refs, out_refs = refs[:n], refs[n + na:2 * n + na]
        send_sems, recv_sems, local_sems = refs[2 * n + na:]
        x, y, c = lax.axis_index("x"), lax.axis_index("y"), lax.axis_index("c")
        chips = [(1 - x, y), (x, 1 - y), (1 - x, 1 - y)]
        my_chip = 2 * x + y
        if collective_id is not None:
            _handshake([(cx, cy, c) for cx, cy in chips])
        mine = [pltpu.make_async_copy(x_refs[i].at[my_chip], out_refs[i].at[my_chip], local_sems.at[i])
                for i in range(n)]
        for cp in mine:
            cp.start()
        sends = []
        for i in range(n):
            for j, (cx, cy) in enumerate(chips):
                sends.append(pltpu.make_async_remote_copy(
                    src_ref=x_refs[i].at[2 * cx + cy], dst_ref=out_refs[i].at[my_chip],
                    send_sem=send_sems.at[3 * i + j], recv_sem=recv_sems.at[3 * i + j],
                    device_id=(cx, cy, c), device_id_type=MESH))
        for cp in sends:
            cp.start()
        for i in range(n):
            for j, (cx, cy) in enumerate(chips):
                pltpu.make_async_remote_copy(
                    src_ref=x_refs[i].at[my_chip], dst_ref=out_refs[i].at[2 * cx + cy],
                    send_sem=send_sems.at[3 * i + j], recv_sem=recv_sems.at[3 * i + j],
                    device_id=(cx, cy, c), device_id_type=MESH).wait_recv()
        for cp in sends:
            cp.wait_send()
        for cp in mine:
            cp.wait()

    return _comm_call(
        body, name, list(sums) + list(after), [jax.ShapeDtypeStruct(s.shape, s.dtype) for s in sums],
        [pltpu.SemaphoreType.DMA((3 * n,)), pltpu.SemaphoreType.DMA((3 * n,)), pltpu.SemaphoreType.DMA((n,))],
        collective_id)


def _cast_shards(name, shards):
    n = len(shards)

    def body(*refs):
        for i in range(n):
            refs[n + i][...] = refs[i][...].astype(BF16)

    vmem = pl.BlockSpec(memory_space=pltpu.VMEM)
    return pl.pallas_call(
        body, name=name, in_specs=[vmem] * n, out_specs=[vmem] * n,
        out_shape=[jax.ShapeDtypeStruct(s.shape, BF16) for s in shards],
        compiler_params=pltpu.CompilerParams(vmem_limit_bytes=VMEM_LIMIT_V7X),
    )(*shards)


BIG = ("w_in", "w_branch_a", "w_branch_b", "w_out", "w_ffn_gate", "w_ffn_up", "w_ffn_down")


def _local_step(x, target, gains, low, conv_w, wg8, reduce):
    g_mix, g_hg, g_ffn, g_fin = gains
    w_in = wg8["w_in"].reshape(N_IN, D_MODEL)
    wg = wg8["w_ffn_gate"].reshape(D_FF, D_MODEL)
    wu = wg8["w_ffn_up"].reshape(D_FF, D_MODEL)
    wa, wb = wg8["w_branch_a"], wg8["w_branch_b"]
    wo = wg8["w_out"].reshape(D_MODEL, D_MODEL)
    wd = wg8["w_ffn_down"].reshape(D_FF, D_MODEL)

    ht, hg, cv, gt, cvo, cvot, o, og, ogt, st = _fwd_in(x, g_mix, w_in, conv_w, low, g_hg)
    x1, mgt = _merge_fwd(og, cvo, gt, x, wa, wb, wo)
    h2t, gate, up, act, loss, d_gfin, dx2, dx2t = _ffn_fwd_loss(x1, g_ffn, wg, wu, wd, target, g_fin)

    dgate, dup, dx1, d_gffn = _ffn_bwd(dx2, x1, gate, up, g_ffn, wg, wu, wd)
    d_wg, d_wu, d_wd = _wgrad_ffn(h2t, dgate, dup, dx2t, act)
    by_owner_ffn = lambda a: a.reshape(N_DEV, D_FF // N_DEV, D_MODEL)
    ffn = dict(w_ffn_down=by_owner_ffn(d_wd), w_ffn_gate=by_owner_ffn(d_wg), w_ffn_up=by_owner_ffn(d_wu))
    dgt, dya, dyb, dog, dcv, d_conv = _merge_bwd(dx1, og, cvo, gt, cv, wa, wb, wo, conv_w)
    sums_ffn, got_ffn = reduce.begin(ffn, sum_after=[dya])
    late = ("w_ffn_gate", "w_ffn_up")
    parts_ffn, updated_ffn = reduce.finish(ffn, sums_ffn, defer=late)
    grad_a, grad_b, grad_o = _wgrad_out_branches(ogt, dya, cvot, dyb, mgt, dx1, after=sums_ffn[:1])
    out = dict(w_out=grad_o.reshape(N_DEV, D_MODEL // N_DEV, D_MODEL), w_branch_a=grad_a, w_branch_b=grad_b)
    dhg, d_low, d_ghg = _hg_bwd(dog, hg, o, st, low, g_hg, after=list(sums_ffn) + [out["w_out"]])
    sums_out, got_out = reduce.begin(out, after=[parts_ffn[0], dhg], sum_after=updated_ffn)
    parts_out, updated_out = reduce.finish(out, sums_out)
    dparts = [dhg, dcv, dgt]
    d_w_in_t, ridden = _wgrad_in(ht, dparts, after=sums_out[:1],
                                 riders=reduce.riders(late, dict(zip(ffn, parts_ffn))))
    reduce.record(late, ridden)
    w_in_grad = dict(w_in=d_w_in_t.reshape(N_DEV, N_IN // N_DEV, D_MODEL))
    sums_in, _ = reduce.begin(w_in_grad, after=parts_out[:1], sum_after=updated_out)
    parts_in, _ = reduce.finish(w_in_grad, sums_in)
    grad_x, d_gmix = _in_bwd(dparts, w_in, x, dx1, g_mix, after=list(parts_out[:1]) + list(sums_in))
    small = dict(norm_mix_g=d_gmix, norm_ffn_g=d_gffn, norm_final_g=d_gfin, lower_bounds=d_low, hg_norm_g=d_ghg,
                 conv_w=d_conv, loss=loss)
    return grad_x, small, parts_in


def kernel(x, norm_mix_g, w_in, lower_bounds, hg_norm_g, conv_w, w_branch_a, w_branch_b, w_out, norm_ffn_g, w_ffn_gate, w_ffn_up, w_ffn_down, norm_final_g, loss_target, m_norm_mix_g, m_w_in, m_lower_bounds, m_hg_norm_g, m_conv_w, m_w_branch_a, m_w_branch_b, m_w_out, m_norm_ffn_g, m_w_ffn_gate, m_w_ffn_up, m_w_ffn_down, m_norm_final_g, v_norm_mix_g, v_w_in, v_lower_bounds, v_hg_norm_g, v_conv_w, v_w_branch_a, v_w_branch_b, v_w_out, v_norm_ffn_g, v_w_ffn_gate, v_w_ffn_up, v_w_ffn_down, v_norm_final_g):
    cx, cy, cc = lax.axis_index("x"), lax.axis_index("y"), lax.axis_index("c")
    my_dev = 4 * cx + 2 * cy + cc

    def tr(a):
        return a[0].T

    big = dict(w_in=tr(w_in), w_branch_a=w_branch_a[0], w_branch_b=w_branch_b[0], w_out=w_out[0],
               w_ffn_gate=tr(w_ffn_gate), w_ffn_up=tr(w_ffn_up), w_ffn_down=w_ffn_down[0])
    big_m = dict(w_in=tr(m_w_in), w_branch_a=m_w_branch_a[0], w_branch_b=m_w_branch_b[0], w_out=m_w_out[0],
                 w_ffn_gate=tr(m_w_ffn_gate), w_ffn_up=tr(m_w_ffn_up), w_ffn_down=m_w_ffn_down[0])
    big_v = dict(w_in=tr(v_w_in), w_branch_a=v_w_branch_a[0], w_branch_b=v_w_branch_b[0], w_out=v_w_out[0],
                 w_ffn_gate=tr(v_w_ffn_gate), w_ffn_up=tr(v_w_ffn_up), w_ffn_down=v_w_ffn_down[0])
    transposed = ("w_in", "w_ffn_gate", "w_ffn_up")

    ids = iter(range(1, 16))
    shards = dict(zip(BIG[:1], _cast_shards("cast_w_in", [big["w_in"]])))
    first = _all_gather("gather_w_in", [shards["w_in"], conv_w.transpose(1, 0, 2)], collective_id=next(ids))
    shards.update(zip(BIG[1:], _cast_shards("cast_shards", [big[n] for n in BIG[1:]])))
    mid = _all_gather("gather_mid", [shards[n] for n in BIG[1:4]], collective_id=next(ids))
    ffn = _all_gather("gather_ffn", [shards[n] for n in BIG[4:]], collective_id=next(ids))
    wg8 = dict(zip(BIG, [first[0]] + list(mid) + list(ffn)))
    conv_full = first[1].transpose(1, 2, 0, 3).reshape(3, CONV_WIDTH)

    core = cc.reshape(1).astype(jnp.int32)
    outs = {}

    class Reduce:
        @staticmethod
        def begin(grads, after=(), sum_after=()):
            names = list(grads)
            by_owner = [grads[n] for n in names]
            got = _sibling_swap("sibling_swap_" + names[0], by_owner, collective_id=next(ids), after=after)
            sums = _pair_sum("pair_sum_" + names[0], by_owner, got, core, after=sum_after)
            return sums, got

        @staticmethod
        def finish(grads, chip_sums, after=(), defer=()):
            names = list(grads)
            parts = _chip_exchange("chip_exchange_" + names[0], chip_sums, collective_id=next(ids), after=after)
            now = [n for n in names if n not in defer]
            updated = _adamw_sum("adamw_" + now[0], [big[n] for n in now],
                                 [p for n, p in zip(names, parts) if n in now],
                                 [big_m[n] for n in now], [big_v[n] for n in now])
            outs.update(zip(now, updated))
            return parts, [outs[n][1] for n in now]

        @staticmethod
        def riders(names, parts):
            return [(big[n], parts[n], big_m[n], big_v[n]) for n in names]

        @staticmethod
        def record(names, updated):
            outs.update(zip(names, updated))

    gains = (norm_mix_g, hg_norm_g, norm_ffn_g, norm_final_g.reshape(1, D_MODEL))
    grad_x, small, last = _local_step(x[0], loss_target[0], gains, lower_bounds, conv_full, wg8, Reduce)

    small_all = _all_gather("gather_small", [_small_pack(small)], collective_id=next(ids), after=last[:1])

    def small_state(a):
        return [a[0], a[1], a[2].reshape(1, D_MODEL), a[3], a[4], a[5].transpose(1, 0, 2)]

    upd = _small_update(
        small_all[0], my_dev.reshape(1).astype(jnp.int32),
        small_state((norm_mix_g, norm_ffn_g, norm_final_g, lower_bounds, hg_norm_g, conv_w)),
        small_state((m_norm_mix_g, m_norm_ffn_g, m_norm_final_g, m_lower_bounds, m_hg_norm_g, m_conv_w)),
        small_state((v_norm_mix_g, v_norm_ffn_g, v_norm_final_g, v_lower_bounds, v_hg_norm_g, v_conv_w)))
    loss = upd[0][0, 0]
    for p, (name, _, _, _) in enumerate(_SMALL_PARAMS):
        outs[name] = upd[1 + 4 * p:5 + 4 * p]
    outs["norm_final_g"] = [a.reshape(D_MODEL) for a in outs["norm_final_g"]]
    outs["conv_w"] = [a.transpose(1, 0, 2) for a in outs["conv_w"]]

    order = ["norm_mix_g", "w_in", "lower_bounds", "hg_norm_g", "conv_w", "w_branch_a", "w_branch_b", "w_out",
             "norm_ffn_g", "w_ffn_gate", "w_ffn_up", "w_ffn_down", "norm_final_g"]
    result = [loss, grad_x[None]]
    for k in range(4):
        for n in order:
            if n in BIG:
                result.append((outs[n][k].T if n in transposed else outs[n][k])[None])
            else:
                result.append(outs[n][k])
    return tuple(result)
```

```python
import jax
import jax.numpy as jnp
from jax import lax
from jax.experimental import pallas as pl
from jax.experimental.pallas import tpu as pltpu
from jax.experimental.pallas import tpu_sc as plsc

F32 = jnp.float32
BF16 = jnp.bfloat16
STASH = jnp.bfloat16

D_MODEL = 1024
HG_WIDTH = 512
HEAD_DIM = 128
N_HEADS = 4
HEADS_PER_STEP = 4
HEAD_GROUPS = N_HEADS // HEADS_PER_STEP
CONV_WIDTH = 512
CONV_K = 3
D_FF = 2816
CHUNK = 32
EPS = 1e-6
Q_SCALE = HEAD_DIM ** -0.5
N_DEV = 8

ADAM_LR = 0.001
ADAM_B1 = 0.9
ADAM_B2 = 0.999
ADAM_EPS = 1e-08
ADAM_WD = 0.01
ADAM_STEP = 10

VMEM_LIMIT_V7X = 56 * 1024 * 1024
VMEM_LIMIT_LARGE_V7X = 62 * 1024 * 1024

SMALL_ROWS = 16


def _params(sem, vmem=VMEM_LIMIT_V7X):
    return pltpu.CompilerParams(dimension_semantics=sem, vmem_limit_bytes=vmem)


def _mm(a, b):
    return jnp.dot(a.astype(BF16), b.astype(BF16), preferred_element_type=F32)


def _mm_nt(a, b):
    return lax.dot_general(a.astype(BF16), b.astype(BF16), (((1,), (1,)), ((), ())), preferred_element_type=F32)


def _mm_tn(a, b):
    return lax.dot_general(a.astype(BF16), b.astype(BF16), (((0,), (0,)), ((), ())), preferred_element_type=F32)


def _sigmoid(x):
    return 0.5 * jnp.tanh(0.5 * x) + 0.5


def _resident(shape):
    nd = len(shape)
    return pl.BlockSpec(shape, lambda *_: (0,) * nd, pipeline_mode=pl.Buffered(1))


def _full(shape):
    nd = len(shape)
    return pl.BlockSpec(shape, lambda *_: (0,) * nd)


def _shard_cols(w_ref):
    return jnp.concatenate([w_ref[s] for s in range(N_DEV)], axis=1)


N_HG = 4 * HG_WIDTH
N_CV = 3 * CONV_WIDTH
N_GT = 2 * D_MODEL
N_IN = N_HG + N_CV + N_GT


def _col(tm, n):
    return pl.BlockSpec((n, tm), lambda i: (0, i))


HALO = 8


def _fwd_in(x, g, w_in_t, conv_w, low, gn):
    T = x.shape[0]
    tm = min(512, T)
    nc = tm // CHUNK

    def body(x_ref, g_ref, w_ref, cw_ref, low_ref, gn_ref, ht_ref, hg_ref, cv_ref, gt_ref, cvo_ref, cvot_ref,
             o_ref, og_ref, ogt_ref, st_ref, tail_scr, s_scr):
        @pl.when(pl.program_id(0) == 0)
        def _():
            tail_scr[...] = jnp.zeros_like(tail_scr)
            s_scr[...] = jnp.zeros_like(s_scr)

        xv = x_ref[...]
        r = lax.rsqrt(jnp.mean(xv * xv, axis=-1, keepdims=True) + EPS)
        hf = xv * r * g_ref[...]
        h = hf.astype(BF16)
        ht_ref[...] = hf.T.astype(BF16)
        hg_ref[...] = _mm_nt(h, w_ref[:N_HG, :])
        cv = _mm_nt(h, w_ref[N_HG:N_HG + N_CV, :])
        cv_ref[...] = cv.astype(STASH)
        gt_ref[...] = _mm_nt(h, w_ref[N_HG + N_CV:, :]).astype(STASH)

        u = cv[:, :CONV_WIDTH] * cv[:, 2 * CONV_WIDTH:]
        row = lax.broadcasted_iota(jnp.int32, u.shape, 0)
        prev1 = tail_scr[HALO - 1:HALO, :]
        prev2 = tail_scr[HALO - 2:HALO - 1, :]
        u1 = jnp.where(row >= 1, pltpu.roll(u, 1, 0), prev1)
        u2 = jnp.where(row >= 2, pltpu.roll(u, 2, 0), jnp.where(row == 1, prev1, prev2))
        y = cw_ref[0:1, :] * u2 + cw_ref[1:2, :] * u1 + cw_ref[2:3, :] * u
        out = cv[:, CONV_WIDTH:2 * CONV_WIDTH] * y
        cvo_ref[...] = out.astype(BF16)
        cvot_ref[...] = out.T.astype(BF16)
        tail_scr[...] = u[tm - HALO:, :]

        _hg_fwd_tile(hg_ref, low_ref, gn_ref, o_ref, og_ref, ogt_ref, st_ref, s_scr, tm)

    row = lambda n: pl.BlockSpec((tm, n), lambda i: (i, 0))
    return pl.pallas_call(
        body, name="fwd_in", grid=(T // tm,),
        in_specs=[row(D_MODEL), _full((1, D_MODEL)), _resident(w_in_t.shape), _full((CONV_K, CONV_WIDTH)),
                  _full((2, HG_WIDTH)), _full((1, HEAD_DIM))],
        out_specs=[_col(tm, D_MODEL), row(N_HG), row(N_CV), row(N_GT), row(CONV_WIDTH), _col(tm, CONV_WIDTH),
                   row(HG_WIDTH), row(HG_WIDTH), _col(tm, HG_WIDTH),
                   pl.BlockSpec((N_HEADS, nc, HEAD_DIM, HEAD_DIM), lambda i: (0, i, 0, 0))],
        out_shape=[jax.ShapeDtypeStruct((D_MODEL, T), BF16), jax.ShapeDtypeStruct((T, N_HG), F32),
                   jax.ShapeDtypeStruct((T, N_CV), STASH), jax.ShapeDtypeStruct((T, N_GT), STASH),
                   jax.ShapeDtypeStruct((T, CONV_WIDTH), BF16), jax.ShapeDtypeStruct((CONV_WIDTH, T), BF16),
                   jax.ShapeDtypeStruct((T, HG_WIDTH), F32), jax.ShapeDtypeStruct((T, HG_WIDTH), BF16),
                   jax.ShapeDtypeStruct((HG_WIDTH, T), BF16),
                   jax.ShapeDtypeStruct((N_HEADS, T // CHUNK, HEAD_DIM, HEAD_DIM), F32)],
        scratch_shapes=[pltpu.VMEM((HALO, CONV_WIDTH), F32), pltpu.VMEM((N_HEADS, HEAD_DIM, HEAD_DIM), F32)],
        compiler_params=_params(("arbitrary",), vmem=VMEM_LIMIT_LARGE_V7X),
    )(x, g, w_in_t, conv_w, low, gn)


def _chunk_pos(shape):
    return lax.broadcasted_iota(jnp.int32, shape, 0) & (CHUNK - 1)


def _chunk_cumsum(x, pos):
    s = 1
    while s < CHUNK:
        x = x + jnp.where(pos >= s, pltpu.roll(x, s, 0), 0.0)
        s *= 2
    return x


def _chunk_rev_cumsum(x, pos):
    n = x.shape[0]
    s = 1
    while s < CHUNK:
        x = x + jnp.where(pos + s < CHUNK, pltpu.roll(x, n - s, 0), 0.0)
        s *= 2
    return x


def _lower_bound(low_ref):
    l0 = low_ref[0:1, :]
    l1 = low_ref[1:2, :]
    m = jnp.maximum(l0, l1)
    e0 = jnp.exp(l0 - m)
    e1 = jnp.exp(l1 - m)
    return e0 / (e0 + e1), e1 / (e0 + e1)


def _hg_gates(qr, fr, lb, pos, tb):
    sq = _sigmoid(qr)
    q = qr * sq * Q_SCALE
    sg = _sigmoid(fr)
    f = lb + (1.0 - lb) * sg
    k = 1.0 - f
    b = _chunk_cumsum(jnp.log(f), pos)
    b3 = b.reshape(tb // CHUNK, CHUNK, HEAD_DIM)
    anc = b3[:, CHUNK // 2 - 1:CHUNK // 2, :]
    last = b3[:, CHUNK - 1:CHUNK, :]
    d3 = b3 - anc
    e_qa3 = jnp.exp(d3)
    e_ka3 = jnp.exp(-d3)
    e_b3 = e_qa3 * jnp.exp(anc)
    e_ko3 = e_ka3 * jnp.exp(last - anc)
    dec = jnp.exp(last)
    flat = lambda a: a.reshape(tb, HEAD_DIM)
    return sq, q, sg, f, k, flat(e_qa3), flat(e_ka3), flat(e_b3), flat(e_ko3), dec


def _intra_mask(sb):
    r = lax.broadcasted_iota(jnp.int32, (sb, sb), 0)
    c = lax.broadcasted_iota(jnp.int32, (sb, sb), 1)
    return ((r // CHUNK) == (c // CHUNK)) & (c <= r)


def _hg_fwd_tile(hg_ref, low_ref, gn_ref, o_ref, og_ref, ogt_ref, st_ref, s_scr, tb):
    sb = min(256, tb)
    nc = tb // CHUNK
    q_ref, f_ref, i_ref, g_ref = (hg_ref.at[:, p * HG_WIDTH:(p + 1) * HG_WIDTH] for p in range(4))
    pos = _chunk_pos((tb, HEAD_DIM))
    mask = _intra_mask(sb)
    lanes = [slice(hh * HEAD_DIM, (hh + 1) * HEAD_DIM) for hh in range(N_HEADS)]
    qi, ko, vb, dec, st = [], [], [], [], []
    for hh, ln in enumerate(lanes):
        lb, _ = _lower_bound(low_ref.at[:, ln])
        _, q, _, _, k, e_qa, e_ka, e_b, e_ko, dec_h = _hg_gates(q_ref[:, ln], f_ref[:, ln], lb, pos, tb)
        qh = (q * e_qa).astype(BF16)
        kh = (k * e_ka).astype(BF16)
        qi.append((q * e_b).astype(BF16))
        ko.append((k * e_ko).astype(BF16))
        vb.append(i_ref[:, ln].astype(BF16))
        dec.append(dec_h)
        st.append(s_scr[hh])
        for s in range(tb // sb):
            sl = slice(s * sb, (s + 1) * sb)
            p = jnp.where(mask, _mm_nt(qh[sl], kh[sl]), 0.0)
            o_ref[sl, ln] = _mm(p, vb[hh][sl])
    for c in range(nc):
        sl = slice(c * CHUNK, (c + 1) * CHUNK)
        for hh, ln in enumerate(lanes):
            st_ref[hh, c] = st[hh]
            o_ref[sl, ln] = o_ref[sl, ln] + _mm_nt(qi[hh][sl], st[hh])
            st[hh] = dec[hh][c] * st[hh] + _mm_tn(vb[hh][sl], ko[hh][sl])
    for hh, ln in enumerate(lanes):
        s_scr[hh] = st[hh]
        o = o_ref[:, ln]
        r = lax.rsqrt(jnp.mean(o * o, axis=-1, keepdims=True) + EPS)
        gr = g_ref[:, ln]
        og = (o * r * gn_ref[...]) * (gr * _sigmoid(gr))
        og_ref[:, ln] = og.astype(BF16)
        ogt_ref[ln, :] = og.T.astype(BF16)


def _merge_fwd(og, cvo, gt, x, wa, wb, wo):
    T = x.shape[0]
    tm = min(1024, T)

    def body(og_ref, cvo_ref, gt_ref, x_ref, wa_ref, wb_ref, wo_ref, x1_ref, mgt_ref):
        ya = jnp.dot(og_ref[...], _shard_cols(wa_ref), preferred_element_type=F32)
        yb = jnp.dot(cvo_ref[...], _shard_cols(wb_ref), preferred_element_type=F32)
        m = (_sigmoid(gt_ref[:, :D_MODEL].astype(F32)) * ya
             + _sigmoid(gt_ref[:, D_MODEL:].astype(F32)) * yb)
        mgt_ref[...] = m.T.astype(BF16)
        x1_ref[...] = x_ref[...] + jnp.dot(m.astype(BF16), wo_ref[...], preferred_element_type=F32)

    row = lambda n: pl.BlockSpec((tm, n), lambda i: (i, 0))
    return pl.pallas_call(
        body, name="merge_fwd", grid=(T // tm,),
        in_specs=[row(HG_WIDTH), row(CONV_WIDTH), row(2 * D_MODEL), row(D_MODEL),
                  _resident(wa.shape), _resident(wb.shape), _resident(wo.shape)],
        out_specs=[row(D_MODEL), _col(tm, D_MODEL)],
        out_shape=[jax.ShapeDtypeStruct((T, D_MODEL), F32), jax.ShapeDtypeStruct((D_MODEL, T), BF16)],
        compiler_params=_params(("parallel",)),
    )(og, cvo, gt, x, wa, wb, wo)


def _ffn_fwd_loss(x1, g, wg, wu, wd, target, g_fin):
    T = x1.shape[0]
    tm = min(512, T)

    def body(x_ref, g_ref, wg_ref, wu_ref, wd_ref, t_ref, gf_ref,
             ht_ref, gate_ref, up_ref, act_ref, loss_ref, dgf_ref, dx2_ref, dx2t_ref):
        @pl.when(pl.program_id(0) == 0)
        def _():
            loss_ref[...] = jnp.zeros_like(loss_ref)
            dgf_ref[...] = jnp.zeros_like(dgf_ref)

        xv = x_ref[...]
        r = lax.rsqrt(jnp.mean(xv * xv, axis=-1, keepdims=True) + EPS)
        hf = xv * r * g_ref[...]
        h = hf.astype(BF16)
        ht_ref[...] = hf.T.astype(BF16)
        gate = _mm_nt(h, wg_ref[...])
        up = _mm_nt(h, wu_ref[...])
        gate_ref[...] = gate.astype(STASH)
        up_ref[...] = up.astype(STASH)
        act = (gate * _sigmoid(gate) * up).astype(BF16)
        act_ref[...] = act
        x2 = xv + jnp.dot(act, wd_ref[...], preferred_element_type=F32)

        gv = gf_ref[...]
        r2 = lax.rsqrt(jnp.mean(x2 * x2, axis=-1, keepdims=True) + EPS)
        xh = x2 * r2
        err = xh * gv - t_ref[...]
        loss_ref[...] += 0.5 * jnp.sum(jnp.mean(err * err, axis=-1, keepdims=True), axis=0, keepdims=True)
        dy = err * (1.0 / D_MODEL)
        dgf_ref[...] += jnp.sum(dy * xh, axis=0, keepdims=True)
        w = dy * gv
        dx2 = r2 * (w - xh * jnp.mean(w * xh, axis=-1, keepdims=True))
        dx2_ref[...] = dx2
        dx2t_ref[...] = dx2.T.astype(BF16)

    row = lambda n: pl.BlockSpec((tm, n), lambda i: (i, 0))
    return pl.pallas_call(
        body, name="ffn_fwd_loss", grid=(T // tm,),
        in_specs=[row(D_MODEL), _full((1, D_MODEL)), _resident(wg.shape), _resident(wu.shape), _resident(wd.shape),
                  row(D_MODEL), _full((1, D_MODEL))],
        out_specs=[_col(tm, D_MODEL), row(D_FF), row(D_FF), row(D_FF), _full((1, 128)), _full((1, D_MODEL)),
                   row(D_MODEL), _col(tm, D_MODEL)],
        out_shape=[jax.ShapeDtypeStruct((D_MODEL, T), BF16), jax.ShapeDtypeStruct((T, D_FF), STASH),
                   jax.ShapeDtypeStruct((T, D_FF), STASH), jax.ShapeDtypeStruct((T, D_FF), BF16),
                   jax.ShapeDtypeStruct((1, 128), F32), jax.ShapeDtypeStruct((1, D_MODEL), F32),
                   jax.ShapeDtypeStruct((T, D_MODEL), F32), jax.ShapeDtypeStruct((D_MODEL, T), BF16)],
        compiler_params=_params(("arbitrary",), vmem=VMEM_LIMIT_LARGE_V7X),
    )(x1, g, wg, wu, wd, target, g_fin)


def _ffn_bwd(dx2, x1, gate, up, g, wg, wu, wd):
    T = x1.shape[0]
    tm = min(512, T)

    def body(dx2_ref, x_ref, gate_ref, up_ref, g_ref, wg_ref, wu_ref, wd_ref, dgate_ref, dup_ref, dx1_ref, dgn_ref):
        @pl.when(pl.program_id(0) == 0)
        def _():
            dgn_ref[...] = jnp.zeros_like(dgn_ref)

        dx2 = dx2_ref[...]
        dact = _mm_nt(dx2, wd_ref[...])
        gate = gate_ref[...].astype(F32)
        s = _sigmoid(gate)
        dgate = (dact * up_ref[...].astype(F32) * (s * (1.0 + gate * (1.0 - s)))).astype(BF16)
        dup = (dact * (gate * s)).astype(BF16)
        dgate_ref[...] = dgate
        dup_ref[...] = dup
        dh = _mm(dgate, wg_ref[...]) + _mm(dup, wu_ref[...])
        xv = x_ref[...]
        r = lax.rsqrt(jnp.mean(xv * xv, axis=-1, keepdims=True) + EPS)
        xh = xv * r
        dgn_ref[...] += jnp.sum(dh * xh, axis=0, keepdims=True)
        w = dh * g_ref[...]
        dx1_ref[...] = dx2 + r * (w - xh * jnp.mean(w * xh, axis=-1, keepdims=True))

    row = lambda n: pl.BlockSpec((tm, n), lambda i: (i, 0))
    return pl.pallas_call(
        body, name="ffn_bwd", grid=(T // tm,),
        in_specs=[row(D_MODEL), row(D_MODEL), row(D_FF), row(D_FF), _full((1, D_MODEL)),
                  _resident(wg.shape), _resident(wu.shape), _resident(wd.shape)],
        out_specs=[row(D_FF), row(D_FF), row(D_MODEL), _full((1, D_MODEL))],
        out_shape=[jax.ShapeDtypeStruct((T, D_FF), BF16), jax.ShapeDtypeStruct((T, D_FF), BF16),
                   jax.ShapeDtypeStruct((T, D_MODEL), F32), jax.ShapeDtypeStruct((1, D_MODEL), F32)],
        compiler_params=_params(("arbitrary",), vmem=VMEM_LIMIT_LARGE_V7X),
    )(dx2, x1, gate, up, g, wg, wu, wd)


def _merge_bwd(dx1, og, cvo, gt, cv, wa, wb, wo, conv_w):
    T = dx1.shape[0]
    tm = min(512, T)
    nt = T // tm

    def body(dx_ref, og_ref, cvo_ref, gt_ref, cv_ref, halo_ref, wa_ref, wb_ref, wo_ref, cw_ref,
             dgt_ref, dya_ref, dyb_ref, dog_ref, dcv_ref, dcw_ref, prev_u, next_dy):
        step = pl.program_id(0)

        @pl.when(step == 0)
        def _():
            next_dy[...] = jnp.zeros_like(next_dy)
            dcw_ref[...] = jnp.zeros_like(dcw_ref)

        dm = _mm_nt(dx_ref[...], wo_ref[...])
        wa = _shard_cols(wa_ref)
        wb = _shard_cols(wb_ref)
        ya = jnp.dot(og_ref[...], wa, preferred_element_type=F32)
        yb = jnp.dot(cvo_ref[...], wb, preferred_element_type=F32)
        sa = _sigmoid(gt_ref[:, :D_MODEL].astype(F32))
        sb = _sigmoid(gt_ref[:, D_MODEL:].astype(F32))
        dgt_ref[:, :D_MODEL] = (dm * ya * (sa * (1.0 - sa))).astype(BF16)
        dgt_ref[:, D_MODEL:] = (dm * yb * (sb * (1.0 - sb))).astype(BF16)
        dya = (dm * sa).astype(BF16)
        dyb = (dm * sb).astype(BF16)
        dya_ref[...] = dya
        dyb_ref[...] = dyb
        dog_ref[...] = _mm_nt(dya, wa)
        dcvo = _mm_nt(dyb, wb)

        cvt = cv_ref[...].astype(F32)
        c, bg, xb = cvt[:, :CONV_WIDTH], cvt[:, CONV_WIDTH:2 * CONV_WIDTH], cvt[:, 2 * CONV_WIDTH:]
        halo = halo_ref[...].astype(F32)
        first_tile = step == nt - 1
        prev_u[...] = jnp.where(first_tile, 0.0, halo[:, :CONV_WIDTH] * halo[:, 2 * CONV_WIDTH:])
        u = c * xb
        row = lax.broadcasted_iota(jnp.int32, u.shape, 0)
        p1 = prev_u[HALO - 1:HALO, :]
        p2 = prev_u[HALO - 2:HALO - 1, :]
        u1 = jnp.where(row >= 1, pltpu.roll(u, 1, 0), p1)
        u2 = jnp.where(row >= 2, pltpu.roll(u, 2, 0), jnp.where(row == 1, p1, p2))
        w0, w1, w2 = cw_ref[0:1, :], cw_ref[1:2, :], cw_ref[2:3, :]
        y = w0 * u2 + w1 * u1 + w2 * u
        dcv_ref[:, CONV_WIDTH:2 * CONV_WIDTH] = (dcvo * y).astype(BF16)
        dy = dcvo * bg
        dcw_ref[0:1, :] += jnp.sum(dy * u2, axis=0, keepdims=True)
        dcw_ref[1:2, :] += jnp.sum(dy * u1, axis=0, keepdims=True)
        dcw_ref[2:3, :] += jnp.sum(dy * u, axis=0, keepdims=True)
        n1 = next_dy[0:1, :]
        n2 = next_dy[1:2, :]
        dy1 = jnp.where(row < tm - 1, pltpu.roll(dy, tm - 1, 0), n1)
        dy2 = jnp.where(row < tm - 2, pltpu.roll(dy, tm - 2, 0), jnp.where(row == tm - 2, n1, n2))
        du = w2 * dy + w1 * dy1 + w0 * dy2
        dcv_ref[:, :CONV_WIDTH] = (du * xb).astype(BF16)
        dcv_ref[:, 2 * CONV_WIDTH:] = (du * c).astype(BF16)
        next_dy[...] = dy[:HALO, :]

    rt = lambda i: nt - 1 - i
    row = lambda n: pl.BlockSpec((tm, n), lambda i: (rt(i), 0))
    halo = pl.BlockSpec((HALO, N_CV), lambda i: (jnp.maximum(rt(i) * (tm // HALO) - 1, 0), 0))
    return pl.pallas_call(
        body, name="merge_bwd", grid=(nt,),
        in_specs=[row(D_MODEL), row(HG_WIDTH), row(CONV_WIDTH), row(2 * D_MODEL), row(N_CV), halo,
                  _resident(wa.shape), _resident(wb.shape), _resident(wo.shape), _full((CONV_K, CONV_WIDTH))],
        out_specs=[row(2 * D_MODEL), row(D_MODEL), row(D_MODEL), row(HG_WIDTH), row(N_CV),
                   _full((CONV_K, CONV_WIDTH))],
        out_shape=[jax.ShapeDtypeStruct((T, 2 * D_MODEL), BF16), jax.ShapeDtypeStruct((T, D_MODEL), BF16),
                   jax.ShapeDtypeStruct((T, D_MODEL), BF16), jax.ShapeDtypeStruct((T, HG_WIDTH), F32),
                   jax.ShapeDtypeStruct((T, N_CV), BF16), jax.ShapeDtypeStruct((CONV_K, CONV_WIDTH), F32)],
        scratch_shapes=[pltpu.VMEM((HALO, CONV_WIDTH), F32), pltpu.VMEM((HALO, CONV_WIDTH), F32)],
        compiler_params=_params(("arbitrary",)),
    )(dx1, og, cvo, gt, cv, cv, wa, wb, wo, conv_w)


def _drop_operands(body, first, count):
    def wrapped(*refs):
        return body(*refs[:first], *refs[first + count:])
    return wrapped


def _hg_bwd(dog, hg, o, st, low, gn, after=()):
    T = hg.shape[0]
    tb = min(512, T)
    sb = min(256, tb)
    nb = T // tb
    nc = tb // CHUNK
    wid = HEADS_PER_STEP * HEAD_DIM

    def body(q_ref, f_ref, i_ref, g_ref, low_ref, gn_ref, o_ref, dog_ref, st_ref,
             dhg_ref, dlow_ref, dgn_ref,
             ds_scr, dqi_scr, dko_scr, dv_scr, dd_scr, dqh_scr, dkh_scr):
        h = pl.program_id(0)
        t = pl.program_id(1)
        dq_ref, df_ref, di_ref, dg_ref = (dhg_ref.at[:, p * HG_WIDTH:(p + 1) * HG_WIDTH] for p in range(4))

        @pl.when(t == 0)
        def _():
            ds_scr[...] = jnp.zeros_like(ds_scr)
            dlow_ref[...] = jnp.zeros_like(dlow_ref)

        @pl.when((t == 0) & (h == 0))
        def _():
            dgn_ref[...] = jnp.zeros_like(dgn_ref)

        pos = _chunk_pos((tb, HEAD_DIM))
        mask = _intra_mask(sb)
        gnv = gn_ref[...]
        lanes = [slice(hh * HEAD_DIM, (hh + 1) * HEAD_DIM) for hh in range(HEADS_PER_STEP)]
        heads = []
        for hh, ln in enumerate(lanes):
            lb, lb1 = _lower_bound(low_ref.at[:, ln])
            qr = q_ref[:, ln]
            sq, q, sg, f, k, e_qa, e_ka, e_b, e_ko, dec = _hg_gates(qr, f_ref[:, ln], lb, pos, tb)

            gr = g_ref[:, ln]
            o = o_ref[:, ln]
            dog_v = dog_ref[:, ln]
            sgr = _sigmoid(gr)
            r = lax.rsqrt(jnp.mean(o * o, axis=-1, keepdims=True) + EPS)
            oh = o * r
            dg_ref[:, ln] = (dog_v * (oh * gnv) * (sgr * (1.0 + gr * (1.0 - sgr)))).astype(BF16)
            don = dog_v * (gr * sgr)
            dgn_ref[...] += jnp.sum(don * oh, axis=0, keepdims=True)
            w = don * gnv
            do = (r * (w - oh * jnp.mean(w * oh, axis=-1, keepdims=True))).astype(BF16)

            qh = (q * e_qa).astype(BF16)
            kh = (k * e_ka).astype(BF16)
            qi = (q * e_b).astype(BF16)
            ko = (k * e_ko).astype(BF16)
            vb = i_ref[:, ln].astype(BF16)

            for s in range(tb // sb):
                sl = slice(s * sb, (s + 1) * sb)
                p = jnp.where(mask, _mm_nt(qh[sl], kh[sl]), 0.0).astype(BF16)
                dp = jnp.where(mask, _mm_nt(do[sl], vb[sl]), 0.0).astype(BF16)
                dv_scr[sl, ln] = _mm_tn(p, do[sl])
                dqh_scr[sl, ln] = _mm(dp, kh[sl])
                dkh_scr[sl, ln] = _mm_tn(dp, qh[sl])
            heads.append(dict(lb=lb, lb1=lb1, qr=qr, sq=sq, q=q, sg=sg, f=f, k=k, e_qa=e_qa, e_ka=e_ka, e_b=e_b,
                              e_ko=e_ko, dec=dec, do=do, qi=qi, ko=ko, vb=vb, ds=ds_scr[hh]))

        for c in reversed(range(nc)):
            sl = slice(c * CHUNK, (c + 1) * CHUNK)
            for hh, ln in enumerate(lanes):
                hd = heads[hh]
                ds = hd["ds"]
                st_c = st_ref[hh, c]
                dqi_scr[sl, ln] = _mm(hd["do"][sl], st_c)
                dko_scr[sl, ln] = _mm(hd["vb"][sl], ds)
                dv_scr[sl, ln] = dv_scr[sl, ln] + _mm_nt(hd["ko"][sl], ds)
                dec_c = hd["dec"][c]
                dd_scr[sl, ln] = jnp.broadcast_to(dec_c * jnp.sum(ds * st_c, axis=0, keepdims=True),
                                                  (CHUNK, HEAD_DIM))
                hd["ds"] = dec_c * ds + _mm_tn(hd["do"][sl], hd["qi"][sl])

        for hh, ln in enumerate(lanes):
            hd = heads[hh]
            ds_scr[hh] = hd["ds"]
            q, k, lb = hd["q"], hd["k"], hd["lb"]
            dko_e = dko_scr[:, ln] * hd["e_ko"]
            dq = dqh_scr[:, ln] * hd["e_qa"] + dqi_scr[:, ln] * hd["e_b"]
            dk = dkh_scr[:, ln] * hd["e_ka"] + dko_e
            kd3 = (k * dko_e).reshape(nc, CHUNK, HEAD_DIM)
            last = jnp.broadcast_to(jnp.sum(kd3, axis=1, keepdims=True), kd3.shape).reshape(tb, HEAD_DIM)
            db = q * dq - k * dk + jnp.where(pos == CHUNK - 1, dd_scr[:, ln] + last, 0.0)
            dlg = _chunk_rev_cumsum(db, pos)
            dfv = dlg / hd["f"] - dk
            s_low = jnp.sum(dfv * (1.0 - hd["sg"]), axis=0, keepdims=True)
            dlow_ref[0:1, ln] += s_low * lb * (1.0 - lb)
            dlow_ref[1:2, ln] += -s_low * lb * hd["lb1"]
            df_ref[:, ln] = (dfv * (1.0 - lb) * hd["sg"] * (1.0 - hd["sg"])).astype(BF16)
            dq_ref[:, ln] = (dq * Q_SCALE * (hd["sq"] * (1.0 + hd["qr"] * (1.0 - hd["sq"])))).astype(BF16)
            di_ref[:, ln] = dv_scr[:, ln].astype(BF16)

    rt = lambda t: nb - 1 - t
    col = lambda p: pl.BlockSpec((tb, wid), lambda h, t: (rt(t), p * HEAD_GROUPS + h))
    hcol = pl.BlockSpec((tb, wid), lambda h, t: (rt(t), h))
    assert HEAD_GROUPS == 1
    tile = pltpu.VMEM((tb, wid), F32)
    return pl.pallas_call(
        _drop_operands(body, 9, len(after)), name="hg_bwd", grid=(HEAD_GROUPS, nb),
        in_specs=[col(0), col(1), col(2), col(3), pl.BlockSpec((2, wid), lambda h, t: (0, h)),
                  pl.BlockSpec((1, HEAD_DIM), lambda h, t: (0, 0)), hcol, hcol,
                  pl.BlockSpec((HEADS_PER_STEP, nc, HEAD_DIM, HEAD_DIM), lambda h, t: (h, rt(t), 0, 0))]
                 + [HBM_SPEC] * len(after),
        out_specs=[pl.BlockSpec((tb, N_HG), lambda h, t: (rt(t), 0)), pl.BlockSpec((2, wid), lambda h, t: (0, h)),
                   pl.BlockSpec((1, HEAD_DIM), lambda h, t: (0, 0))],
        out_shape=[jax.ShapeDtypeStruct((T, N_HG), BF16), jax.ShapeDtypeStruct((2, HG_WIDTH), F32),
                   jax.ShapeDtypeStruct((1, HEAD_DIM), F32)],
        scratch_shapes=[pltpu.VMEM((HEADS_PER_STEP, HEAD_DIM, HEAD_DIM), F32), tile, tile, tile, tile, tile, tile],
        compiler_params=_params(("arbitrary", "arbitrary")),
    )(hg, hg, hg, hg, low, gn, o, dog, st, *after)


def _in_bwd(dparts, w_in, x, dx1, g, after=()):
    T = x.shape[0]
    tm = min(512, T)
    widths = [p.shape[1] for p in dparts]
    offs = [sum(widths[:i]) for i in range(len(widths))]
    n = len(dparts)

    def body(*refs):
        d_refs = refs[:n]
        w_ref, x_ref, dx1_ref, g_ref, dx_ref, dgn_ref = refs[n:]

        @pl.when(pl.program_id(0) == 0)
        def _():
            dgn_ref[...] = jnp.zeros_like(dgn_ref)

        dh = None
        for d_ref, off, wd in zip(d_refs, offs, widths):
            part = _mm(d_ref[...], w_ref[off:off + wd, :])
            dh = part if dh is None else dh + part
        xv = x_ref[...]
        r = lax.rsqrt(jnp.mean(xv * xv, axis=-1, keepdims=True) + EPS)
        xh = xv * r
        dgn_ref[...] += jnp.sum(dh * xh, axis=0, keepdims=True)
        w = dh * g_ref[...]
        dx_ref[...] = dx1_ref[...] + r * (w - xh * jnp.mean(w * xh, axis=-1, keepdims=True))

    row = lambda m: pl.BlockSpec((tm, m), lambda i: (i, 0))
    return pl.pallas_call(
        _drop_operands(body, n + 4, len(after)), name="in_bwd", grid=(T // tm,),
        in_specs=[row(wd) for wd in widths] + [_resident(w_in.shape), row(D_MODEL), row(D_MODEL), _full((1, D_MODEL))]
                 + [HBM_SPEC] * len(after),
        out_specs=[row(D_MODEL), _full((1, D_MODEL))],
        out_shape=[jax.ShapeDtypeStruct((T, D_MODEL), F32), jax.ShapeDtypeStruct((1, D_MODEL), F32)],
        compiler_params=_params(("arbitrary",)),
    )(*dparts, w_in, x, dx1, g, *after)


def _wgrad_ffn(h2t, dgate, dup, dx2t, act, tn=256):
    M, T = h2t.shape
    N = dgate.shape[1]

    def body(h_ref, x_ref, dg_ref, du_ref, act_ref, og_ref, ou_ref, od_ref):
        h = h_ref[...]
        og_ref[...] = _mm(h, dg_ref[...]).T.astype(BF16)
        ou_ref[...] = _mm(h, du_ref[...]).T.astype(BF16)
        od_ref[...] = _mm(x_ref[...], act_ref[...]).T.astype(BF16)

    rhs = pl.BlockSpec((T, tn), lambda j: (0, j))
    out_spec = pl.BlockSpec((tn, M), lambda j: (j, 0))
    out = jax.ShapeDtypeStruct((N, M), BF16)
    return pl.pallas_call(
        body, name="wgrad_ffn", grid=(N // tn,),
        in_specs=[_resident((M, T)), _resident((M, T)), rhs, rhs, rhs], out_specs=[out_spec] * 3,
        out_shape=[out, out, out],
        compiler_params=_params(("parallel",)),
    )(h2t, dx2t, dgate, dup, act)


def _wgrad_out_branches(ogt, dya, cvot, dyb, mgt, dx1, after=()):
    M, T = ogt.shape
    N = dya.shape[1]
    c = N // N_DEV
    per = 2
    tn = per * c

    def body(at_ref, da_ref, bt_ref, db_ref, mt_ref, dx_ref, oa_ref, ob_ref, oo_ref):
        ga = _mm(at_ref[...], da_ref[...])
        gb = _mm(bt_ref[...], db_ref[...])
        for s in range(per):
            oa_ref[s] = ga[:, s * c:(s + 1) * c].astype(BF16)
            ob_ref[s] = gb[:, s * c:(s + 1) * c].astype(BF16)
        oo_ref[...] = _mm(mt_ref[...], dx_ref[...]).astype(BF16)

    rhs = pl.BlockSpec((T, tn), lambda j: (0, j))
    owners = pl.BlockSpec((per, M, c), lambda j: (j, 0, 0))
    out = jax.ShapeDtypeStruct((N_DEV, M, c), BF16)
    return pl.pallas_call(
        _drop_operands(body, 6, len(after)), name="wgrad_out_branches", grid=(N // tn,),
        in_specs=[_resident((M, T)), rhs, _resident((M, T)), rhs, _resident(mgt.shape), rhs] + [HBM_SPEC] * len(after),
        out_specs=[owners, owners, pl.BlockSpec((mgt.shape[0], tn), lambda j: (0, j))],
        out_shape=[out, out, jax.ShapeDtypeStruct((mgt.shape[0], dx1.shape[1]), BF16)],
        compiler_params=_params(("parallel",)),
    )(ogt, dya, cvot, dyb, mgt, dx1, *after)


def _wgrad_in(ht, dparts, after=(), riders=()):
    M, T = ht.shape
    tn = 512
    nblk = [p.shape[1] // tn for p in dparts]
    start = [sum(nblk[:i]) for i in range(len(nblk))]
    n = len(dparts)
    steps = sum(nblk)
    nr = len(riders)
    rows = [r[0].shape[0] // steps for r in riders]
    assert all(r[0].shape[0] == rr * steps and rr % 16 == 0 for r, rr in zip(riders, rows))

    def body(a_ref, *refs):
        d_refs = refs[:n]
        rider_in = refs[n:n + 4 * nr]
        o_ref = refs[n + 4 * nr]
        rider_out = refs[n + 4 * nr + 1:]
        j = pl.program_id(0)
        for d_ref, s, nb in zip(d_refs, start, nblk):
            @pl.when((j >= s) & (j < s + nb))
            def _():
                o_ref[...] = _mm(a_ref[...], d_ref[...]).T.astype(BF16)
        for i in range(nr):
            w_ref, p_ref, m_ref, v_ref = rider_in[4 * i:4 * i + 4]
            g = p_ref[0].astype(F32)
            for k in range(1, 4):
                g = g + p_ref[k].astype(F32)
            delta, m_new, v_new = _adamw_math(w_ref[...], g, m_ref[...], v_ref[...])
            for k, val in enumerate((g, delta, m_new, v_new)):
                rider_out[4 * i + k][...] = val

    def piece_spec(s, nb):
        return pl.BlockSpec((T, tn), lambda j: (0, jnp.clip(j - s, 0, nb - 1)))

    rider_specs, rider_out_specs, rider_out_shape, rider_args = [], [], [], []
    for (w, parts, m, v), rr in zip(riders, rows):
        blk = pl.BlockSpec((rr, w.shape[1]), lambda j: (j, 0))
        rider_specs += [blk, pl.BlockSpec((4, rr, w.shape[1]), lambda j: (0, j, 0)), blk, blk]
        rider_out_specs += [blk] * 4
        rider_out_shape += [jax.ShapeDtypeStruct(w.shape, F32)] * 4
        rider_args += [w, parts, m, v]
    outs = pl.pallas_call(
        _drop_operands(body, 1 + n + 4 * nr, len(after)), name="wgrad_in", grid=(steps,),
        in_specs=[_resident((M, T))] + [piece_spec(s, nb) for s, nb in zip(start, nblk)] + rider_specs
                 + [HBM_SPEC] * len(after),
        out_specs=[pl.BlockSpec((tn, M), lambda j: (j, 0))] + rider_out_specs,
        out_shape=[jax.ShapeDtypeStruct((steps * tn, M), BF16)] + rider_out_shape,
        compiler_params=_params(("parallel",)),
    )(ht, *dparts, *rider_args, *after)
    return outs[0], [outs[1 + 4 * i:5 + 4 * i] for i in range(nr)]


def _adamw_math(w, g, m, v):
    m = ADAM_B1 * m + (1.0 - ADAM_B1) * g
    v = ADAM_B2 * v + (1.0 - ADAM_B2) * (g * g)
    m_hat = m / (1.0 - ADAM_B1 ** ADAM_STEP)
    v_hat = v / (1.0 - ADAM_B2 ** ADAM_STEP)
    delta = -ADAM_LR * (m_hat / (jnp.sqrt(v_hat) + ADAM_EPS) + ADAM_WD * w)
    return delta, m, v


def _adamw_sum(name, ws, parts, ms, vs):
    n = len(ws)
    steps = min(_row_steps(w.shape[0]) for w in ws)
    rows = [w.shape[0] // steps for w in ws]

    def body(*refs):
        w_refs, p_refs, m_refs, v_refs = (refs[k * n:(k + 1) * n] for k in range(4))
        out_refs = refs[4 * n:]
        for i in range(n):
            g = p_refs[i][0].astype(F32)
            for k in range(1, 4):
                g = g + p_refs[i][k].astype(F32)
            delta, m_new, v_new = _adamw_math(w_refs[i][...], g, m_refs[i][...], v_refs[i][...])
            for k, val in enumerate((g, delta, m_new, v_new)):
                out_refs[4 * i + k][...] = val

    blk = [pl.BlockSpec((r, w.shape[1]), lambda s: (s, 0)) for r, w in zip(rows, ws)]
    pblk = [pl.BlockSpec((4, r, w.shape[1]), lambda s: (0, s, 0)) for r, w in zip(rows, ws)]
    out_specs, out_shape = [], []
    for b, w in zip(blk, ws):
        out_specs += [b] * 4
        out_shape += [jax.ShapeDtypeStruct(w.shape, F32)] * 4
    flat = pl.pallas_call(
        body, name=name, grid=(steps,),
        in_specs=blk + pblk + blk + blk, out_specs=out_specs, out_shape=out_shape,
        compiler_params=_params(("parallel",)),
    )(*ws, *parts, *ms, *vs)
    return [flat[4 * i:4 * i + 4] for i in range(n)]


_SMALL_SLOTS = (("norm_mix_g", 0, 1, 1024), ("norm_ffn_g", 1, 1, 1024), ("norm_final_g", 2, 1, 1024),
                ("lower_bounds", 3, 2, 512), ("hg_norm_g", 5, 1, 128), ("loss", 6, 1, 128), ("conv_w", 8, 3, 512))
_SMALL_PARAMS = tuple(s for s in _SMALL_SLOTS if s[0] != "loss")
CONV_SHARD = CONV_WIDTH // N_DEV


def _small_pack(small):
    def body(*refs):
        out = refs[-1]
        out[...] = jnp.zeros_like(out)
        for ref, (_, row, rows, lanes) in zip(refs[:-1], _SMALL_SLOTS):
            out[row:row + rows, 0:lanes] = ref[...]

    vmem = pl.BlockSpec(memory_space=pltpu.VMEM)
    return pl.pallas_call(
        body, name="small_pack", in_specs=[vmem] * len(_SMALL_SLOTS), out_specs=vmem,
        out_shape=jax.ShapeDtypeStruct((SMALL_ROWS, 1024), F32),
    )(*[small[name] for name, _, _, _ in _SMALL_SLOTS])


def _small_update(gathered, dev, w, m, v):
    n = len(_SMALL_PARAMS)

    def body(dev_ref, g_ref, *refs):
        w_refs, m_refs, v_refs = refs[:n], refs[n:2 * n], refs[2 * n:3 * n]
        loss_ref, out_refs, sum_scr = refs[3 * n], refs[3 * n + 1:-1], refs[-1]
        total = g_ref[0]
        for k in range(1, N_DEV):
            total = total + g_ref[k]
        sum_scr[...] = total
        loss_ref[...] = sum_scr[6:7, 0:128]
        for p, (name, row, rows, lanes) in enumerate(_SMALL_PARAMS):
            if name == "conv_w":
                for r in range(rows):
                    g = sum_scr[row + r:row + r + 1, 0:CONV_SHARD]
                    for s in range(1, N_DEV):
                        mine = sum_scr[row + r:row + r + 1, s * CONV_SHARD:(s + 1) * CONV_SHARD]
                        g = jnp.where(dev_ref[0] == s, mine, g)
                    delta, m_new, v_new = _adamw_math(w_refs[p][r], g, m_refs[p][r], v_refs[p][r])
                    out_refs[4 * p][r] = g
                    out_refs[4 * p + 1][r] = delta
                    out_refs[4 * p + 2][r] = m_new
                    out_refs[4 * p + 3][r] = v_new
                continue
            g = sum_scr[row:row + rows, 0:lanes]
            delta, m_new, v_new = _adamw_math(w_refs[p][...], g, m_refs[p][...], v_refs[p][...])
            out_refs[4 * p][...] = g
            out_refs[4 * p + 1][...] = delta
            out_refs[4 * p + 2][...] = m_new
            out_refs[4 * p + 3][...] = v_new

    vmem = pl.BlockSpec(memory_space=pltpu.VMEM)
    outs = [jax.ShapeDtypeStruct((1, 128), F32)]
    for a in w:
        outs += [jax.ShapeDtypeStruct(a.shape, F32)] * 4
    return pl.pallas_call(
        body, name="small_update",
        in_specs=[pl.BlockSpec(memory_space=pltpu.SMEM)] + [vmem] * (1 + 3 * n), out_specs=[vmem] * len(outs),
        out_shape=outs, scratch_shapes=[pltpu.VMEM((SMALL_ROWS, 1024), F32)],
    )(dev, gathered, *w, *m, *v)


def _row_steps(rows):
    for steps in (4, 2):
        if rows % (16 * steps) == 0:
            return steps
    return 1


def _pair_sum(name, by_owner, got, core, after=()):
    n = len(got)

    def body(core_ref, *refs):
        for a_ref, b_ref, o_ref in zip(refs[:n], refs[n:2 * n], refs[2 * n:]):
            o_ref[...] = (a_ref[...].astype(F32) + b_ref[...].astype(F32)).astype(BF16)

    def blk(g):
        return pl.BlockSpec((None,) + g.shape[1:], lambda k, core_ref: (k, 0, 0))

    def mine(g):
        return pl.BlockSpec((None,) + g.shape[1:], lambda k, core_ref: (2 * k + core_ref[0], 0, 0))

    return pl.pallas_call(
        _drop_operands(body, 1 + 2 * n, len(after)), name=name,
        grid_spec=pltpu.PrefetchScalarGridSpec(
            num_scalar_prefetch=1, grid=(4,),
            in_specs=[mine(g) for g in got] + [blk(g) for g in got] + [HBM_SPEC] * len(after),
            out_specs=[blk(g) for g in got]),
        out_shape=[jax.ShapeDtypeStruct(g.shape, BF16) for g in got],
        compiler_params=_params(("parallel",)),
    )(core, *by_owner, *got, *after)


MESH = pl.DeviceIdType.MESH
HBM_SPEC = pl.BlockSpec(memory_space=pl.ANY)


def _handshake(peers):
    barrier = pltpu.get_barrier_semaphore()
    for peer in peers:
        pl.semaphore_signal(barrier, inc=1, device_id=peer, device_id_type=MESH)
    pl.semaphore_wait(barrier, len(peers))


def _comm_call(body, name, operands, out_shape, scratch, collective_id):
    if collective_id is None:
        return pl.pallas_call(body, name=name, in_specs=[HBM_SPEC] * len(operands), out_specs=[HBM_SPEC] * len(out_shape),
                              out_shape=out_shape, scratch_shapes=scratch)(*operands)
    return pl.kernel(body, out_type=out_shape, mesh=plsc.ScalarSubcoreMesh(axis_name="sequencer", num_cores=1),
                     scratch_types=scratch, name=name,
                     compiler_params=pltpu.CompilerParams(collective_id=collective_id))(*operands)


def _all_gather(name, blocks, collective_id=None, after=(), pieces=None):
    n = len(blocks)
    na = len(after)
    pieces = pieces or [1] * n
    parts = []
    for i, (b, k) in enumerate(zip(blocks, pieces)):
        rows, rem = divmod(b.shape[0], k)
        assert rem == 0 and (k == 1 or rows % 16 == 0), (b.shape, k)
        parts += [(i, None, None)] if k == 1 else [(i, j * rows, rows) for j in range(k)]
    np_ = len(parts)

    def body(*refs):
        x_refs, out_refs = refs[:n], refs[n + na:2 * n + na]
        send_sems, recv_sems, local_sems = refs[2 * n + na:]
        x, y, c = lax.axis_index("x"), lax.axis_index("y"), lax.axis_index("c")
        me, sibling = (x, y, c), (x, y, 1 - c)
        chips = [(1 - x, y), (x, 1 - y), (1 - x, 1 - y)]
        if collective_id is not None:
            _handshake([sibling] + [(*chip, c) for chip in chips])

        def slot(p, px, py, pc):
            i, r0, rows = parts[p]
            whole = out_refs[i].at[4 * px + 2 * py + pc]
            return whole if r0 is None else whole.at[pl.ds(r0, rows)]

        def own(p):
            i, r0, rows = parts[p]
            return x_refs[i] if r0 is None else x_refs[i].at[pl.ds(r0, rows)]

        def copy(p, k, blk, to, src=None):
            return pltpu.make_async_remote_copy(
                src_ref=slot(p, *blk) if src is None else src, dst_ref=slot(p, *blk),
                send_sem=send_sems.at[7 * p + k], recv_sem=recv_sems.at[7 * p + k], device_id=to, device_id_type=MESH)

        mine = [pltpu.make_async_copy(own(p), slot(p, *me), local_sems.at[p]) for p in range(np_)]
        for cp in mine:
            cp.start()
        first = []
        for p in range(np_):
            first.append(copy(p, 0, me, sibling, src=own(p)))
            first += [copy(p, 1 + j, me, (*chip, c), src=own(p)) for j, chip in enumerate(chips)]
        for cp in first:
            cp.start()
        passed = []
        for p in range(np_):
            for j, chip in enumerate(chips):
                copy(p, 1 + j, (*chip, c), me).wait_recv()
                passed.append(copy(p, 4 + j, (*chip, c), sibling))
                passed[-1].start()
        for p in range(np_):
            copy(p, 0, sibling, me).wait_recv()
            for j, chip in enumerate(chips):
                copy(p, 4 + j, (*chip, 1 - c), me).wait_recv()
        for cp in first + passed:
            cp.wait_send()
        for cp in mine:
            cp.wait()

    return _comm_call(
        body, name, list(blocks) + list(after), [jax.ShapeDtypeStruct((N_DEV,) + b.shape, b.dtype) for b in blocks],
        [pltpu.SemaphoreType.DMA((7 * np_,)), pltpu.SemaphoreType.DMA((7 * np_,)), pltpu.SemaphoreType.DMA((np_,))],
        collective_id)


def _sibling_swap(name, by_owner, collective_id=None, after=()):
    n = len(by_owner)
    na = len(after)

    def body(*refs):
        x_refs, out_refs = refs[:n], refs[n + na:2 * n + na]
        send_sems, recv_sems = refs[2 * n + na:]
        x, y, c = lax.axis_index("x"), lax.axis_index("y"), lax.axis_index("c")
        if collective_id is not None:
            _handshake([(x, y, 1 - c)])
        copies = []
        for i in range(n):
            for k in range(4):
                copies.append(pltpu.make_async_remote_copy(
                    src_ref=x_refs[i].at[2 * k + 1 - c], dst_ref=out_refs[i].at[k],
                    send_sem=send_sems.at[4 * i + k], recv_sem=recv_sems.at[4 * i + k],
                    device_id=(x, y, 1 - c), device_id_type=MESH))
        for cp in copies:
            cp.start()
        for cp in copies:
            cp.wait()

    return _comm_call(
        body, name, list(by_owner) + list(after),
        [jax.ShapeDtypeStruct((4,) + b.shape[1:], b.dtype) for b in by_owner],
        [pltpu.SemaphoreType.DMA((4 * n,)), pltpu.SemaphoreType.DMA((4 * n,))], collective_id)


def _chip_exchange(name, sums, collective_id=None, after=()):
    n = len(sums)
    na = len(after)

    def body(*refs):
        x_refs, out_refs = refs[:n], refs[n + na:2 * n + na]
        send_sems, recv_sems, local_sems = refs[2 * n + na:]
        x, y, c = lax.axis_index("x"), lax.axis_index("y"), lax.axis_index("c")
        chips = [(1 - x, y), (x, 1 - y), (1 - x, 1 - y)]
        my_chip = 2 * x + y
        if collective_id is not None:
            _handshake([(cx, cy, c) for cx, cy in chips])
        mine = [pltpu.make_async_copy(x_refs[i].at[my_chip], out_refs[i].at[my_chip], local_sems.at[i])
                for i in range(n)]
        for cp in mine:
            cp.start()
        sends = []
        for i in range(n):
            for j, (cx, cy) in enumerate(chips):
                sends.append(pltpu.make_async_remote_copy(
                    src_ref=x_refs[i].at[2 * cx + cy], dst_ref=out_refs[i].at[my_chip],
                    send_sem=send_sems.at[3 * i + j], recv_sem=recv_sems.at[3 * i + j],
                    device_id=(cx, cy, c), device_id_type=MESH))
        for cp in sends:
            cp.start()
        for i in range(n):
            for j, (cx, cy) in enumerate(chips):
                pltpu.make_async_remote_copy(
                    src_ref=x_refs[i].at[my_chip], dst_ref=out_refs[i].at[2 * cx + cy],
                    send_sem=send_sems.at[3 * i + j], recv_sem=recv_sems.at[3 * i + j],
                    device_id=(cx, cy, c), device_id_type=MESH).wait_recv()
        for cp in sends:
            cp.wait_send()
        for cp in mine:
            cp.wait()

    return _comm_call(
        body, name, list(sums) + list(after), [jax.ShapeDtypeStruct(s.shape, s.dtype) for s in sums],
        [pltpu.SemaphoreType.DMA((3 * n,)), pltpu.SemaphoreType.DMA((3 * n,)), pltpu.SemaphoreType.DMA((n,))],
        collective_id)


def _cast_shards(name, shards):
    n = len(shards)

    def body(*refs):
        for i in range(n):
            refs[n + i][...] = refs[i][...].astype(BF16)

    vmem = pl.BlockSpec(memory_space=pltpu.VMEM)
    return pl.pallas_call(
        body, name=name, in_specs=[vmem] * n, out_specs=[vmem] * n,
        out_shape=[jax.ShapeDtypeStruct(s.shape, BF16) for s in shards],
        compiler_params=pltpu.CompilerParams(vmem_limit_bytes=VMEM_LIMIT_V7X),
    )(*shards)


BIG = ("w_in", "w_branch_a", "w_branch_b", "w_out", "w_ffn_gate", "w_ffn_up", "w_ffn_down")


def _local_step(x, target, gains, low, conv_w, wg8, reduce):
    g_mix, g_hg, g_ffn, g_fin = gains
    w_in = wg8["w_in"].reshape(N_IN, D_MODEL)
    wg = wg8["w_ffn_gate"].reshape(D_FF, D_MODEL)
    wu = wg8["w_ffn_up"].reshape(D_FF, D_MODEL)
    wa, wb = wg8["w_branch_a"], wg8["w_branch_b"]
    wo = wg8["w_out"].reshape(D_MODEL, D_MODEL)
    wd = wg8["w_ffn_down"].reshape(D_FF, D_MODEL)

    ht, hg, cv, gt, cvo, cvot, o, og, ogt, st = _fwd_in(x, g_mix, w_in, conv_w, low, g_hg)
    x1, mgt = _merge_fwd(og, cvo, gt, x, wa, wb, wo)
    h2t, gate, up, act, loss, d_gfin, dx2, dx2t = _ffn_fwd_loss(x1, g_ffn, wg, wu, wd, target, g_fin)

    dgate, dup, dx1, d_gffn = _ffn_bwd(dx2, x1, gate, up, g_ffn, wg, wu, wd)
    d_wg, d_wu, d_wd = _wgrad_ffn(h2t, dgate, dup, dx2t, act)
    by_owner_ffn = lambda a: a.reshape(N_DEV, D_FF // N_DEV, D_MODEL)
    ffn = dict(w_ffn_down=by_owner_ffn(d_wd), w_ffn_gate=by_owner_ffn(d_wg), w_ffn_up=by_owner_ffn(d_wu))
    dgt, dya, dyb, dog, dcv, d_conv = _merge_bwd(dx1, og, cvo, gt, cv, wa, wb, wo, conv_w)
    sums_ffn, got_ffn = reduce.begin(ffn, sum_after=[dya])
    late = ("w_ffn_gate", "w_ffn_up")
    parts_ffn, updated_ffn = reduce.finish(ffn, sums_ffn, defer=late)
    grad_a, grad_b, grad_o = _wgrad_out_branches(ogt, dya, cvot, dyb, mgt, dx1, after=sums_ffn[:1])
    out = dict(w_out=grad_o.reshape(N_DEV, D_MODEL // N_DEV, D_MODEL), w_branch_a=grad_a, w_branch_b=grad_b)
    dhg, d_low, d_ghg = _hg_bwd(dog, hg, o, st, low, g_hg, after=list(sums_ffn) + [out["w_out"]])
    sums_out, got_out = reduce.begin(out, after=[parts_ffn[0], dhg], sum_after=updated_ffn)
    parts_out, updated_out = reduce.finish(out, sums_out)
    dparts = [dhg, dcv, dgt]
    d_w_in_t, ridden = _wgrad_in(ht, dparts, after=sums_out[:1],
                                 riders=reduce.riders(late, dict(zip(ffn, parts_ffn))))
    reduce.record(late, ridden)
    w_in_grad = dict(w_in=d_w_in_t.reshape(N_DEV, N_IN // N_DEV, D_MODEL))
    sums_in, _ = reduce.begin(w_in_grad, after=parts_out[:1], sum_after=updated_out)
    parts_in, _ = reduce.finish(w_in_grad, sums_in)
    grad_x, d_gmix = _in_bwd(dparts, w_in, x, dx1, g_mix, after=list(parts_out[:1]) + list(sums_in))
    small = dict(norm_mix_g=d_gmix, norm_ffn_g=d_gffn, norm_final_g=d_gfin, lower_bounds=d_low, hg_norm_g=d_ghg,
                 conv_w=d_conv, loss=loss)
    return grad_x, small, parts_in


def kernel(x, norm_mix_g, w_in, lower_bounds, hg_norm_g, conv_w, w_branch_a, w_branch_b, w_out, norm_ffn_g, w_ffn_gate, w_ffn_up, w_ffn_down, norm_final_g, loss_target, m_norm_mix_g, m_w_in, m_lower_bounds, m_hg_norm_g, m_conv_w, m_w_branch_a, m_w_branch_b, m_w_out, m_norm_ffn_g, m_w_ffn_gate, m_w_ffn_up, m_w_ffn_down, m_norm_final_g, v_norm_mix_g, v_w_in, v_lower_bounds, v_hg_norm_g, v_conv_w, v_w_branch_a, v_w_branch_b, v_w_out, v_norm_ffn_g, v_w_ffn_gate, v_w_ffn_up, v_w_ffn_down, v_norm_final_g):
    cx, cy, cc = lax.axis_index("x"), lax.axis_index("y"), lax.axis_index("c")
    my_dev = 4 * cx + 2 * cy + cc

    def tr(a):
        return a[0].T

    big = dict(w_in=tr(w_in), w_branch_a=w_branch_a[0], w_branch_b=w_branch_b[0], w_out=w_out[0],
               w_ffn_gate=tr(w_ffn_gate), w_ffn_up=tr(w_ffn_up), w_ffn_down=w_ffn_down[0])
    big_m = dict(w_in=tr(m_w_in), w_branch_a=m_w_branch_a[0], w_branch_b=m_w_branch_b[0], w_out=m_w_out[0],
                 w_ffn_gate=tr(m_w_ffn_gate), w_ffn_up=tr(m_w_ffn_up), w_ffn_down=m_w_ffn_down[0])
    big_v = dict(w_in=tr(v_w_in), w_branch_a=v_w_branch_a[0], w_branch_b=v_w_branch_b[0], w_out=v_w_out[0],
                 w_ffn_gate=tr(v_w_ffn_gate), w_ffn_up=tr(v_w_ffn_up), w_ffn_down=v_w_ffn_down[0])
    transposed = ("w_in", "w_ffn_gate", "w_ffn_up")

    ids = iter(range(1, 16))
    shards = dict(zip(BIG[:1], _cast_shards("cast_w_in", [big["w_in"]])))
    first = _all_gather("gather_w_in", [shards["w_in"], conv_w.transpose(1, 0, 2)], collective_id=next(ids),
                        pieces=[4, 1])
    shards.update(zip(BIG[1:], _cast_shards("cast_shards", [big[n] for n in BIG[1:]])))
    mid = _all_gather("gather_mid", [shards[n] for n in BIG[1:4]], collective_id=next(ids))
    ffn = _all_gather("gather_ffn", [shards[n] for n in BIG[4:]], collective_id=next(ids), pieces=[2, 2, 2])
    wg8 = dict(zip(BIG, [first[0]] + list(mid) + list(ffn)))
    conv_full = first[1].transpose(1, 2, 0, 3).reshape(3, CONV_WIDTH)

    core = cc.reshape(1).astype(jnp.int32)
    outs = {}

    class Reduce:
        @staticmethod
        def begin(grads, after=(), sum_after=()):
            names = list(grads)
            by_owner = [grads[n] for n in names]
            got = _sibling_swap("sibling_swap_" + names[0], by_owner, collective_id=next(ids), after=after)
            sums = _pair_sum("pair_sum_" + names[0], by_owner, got, core, after=sum_after)
            return sums, got

        @staticmethod
        def finish(grads, chip_sums, after=(), defer=()):
            names = list(grads)
            parts = _chip_exchange("chip_exchange_" + names[0], chip_sums, collective_id=next(ids), after=after)
            now = [n for n in names if n not in defer]
            updated = _adamw_sum("adamw_" + now[0], [big[n] for n in now],
                                 [p for n, p in zip(names, parts) if n in now],
                                 [big_m[n] for n in now], [big_v[n] for n in now])
            outs.update(zip(now, updated))
            return parts, [outs[n][1] for n in now]

        @staticmethod
        def riders(names, parts):
            return [(big[n], parts[n], big_m[n], big_v[n]) for n in names]

        @staticmethod
        def record(names, updated):
            outs.update(zip(names, updated))

    gains = (norm_mix_g, hg_norm_g, norm_ffn_g, norm_final_g.reshape(1, D_MODEL))
    grad_x, small, last = _local_step(x[0], loss_target[0], gains, lower_bounds, conv_full, wg8, Reduce)

    small_all = _all_gather("gather_small", [_small_pack(small)], collective_id=next(ids), after=last[:1])

    def small_state(a):
        return [a[0], a[1], a[2].reshape(1, D_MODEL), a[3], a[4], a[5].transpose(1, 0, 2)]

    upd = _small_update(
        small_all[0], my_dev.reshape(1).astype(jnp.int32),
        small_state((norm_mix_g, norm_ffn_g, norm_final_g, lower_bounds, hg_norm_g, conv_w)),
        small_state((m_norm_mix_g, m_norm_ffn_g, m_norm_final_g, m_lower_bounds, m_hg_norm_g, m_conv_w)),
        small_state((v_norm_mix_g, v_norm_ffn_g, v_norm_final_g, v_lower_bounds, v_hg_norm_g, v_conv_w)))
    loss = upd[0][0, 0]
    for p, (name, _, _, _) in enumerate(_SMALL_PARAMS):
        outs[name] = upd[1 + 4 * p:5 + 4 * p]
    outs["norm_final_g"] = [a.reshape(D_MODEL) for a in outs["norm_final_g"]]
    outs["conv_w"] = [a.transpose(1, 0, 2) for a in outs["conv_w"]]

    order = ["norm_mix_g", "w_in", "lower_bounds", "hg_norm_g", "conv_w", "w_branch_a", "w_branch_b", "w_out",
             "norm_ffn_g", "w_ffn_gate", "w_ffn_up", "w_ffn_down", "norm_final_g"]
    result = [loss, grad_x[None]]
    for k in range(4):
        for n in order:
            if n in BIG:
                result.append((outs[n][k].T if n in transposed else outs[n][k])[None])
            else:
                result.append(outs[n][k])
    return tuple(result)
```

```python
import jax
import jax.numpy as jnp
from jax import lax
from jax.experimental import pallas as pl
from jax.experimental.pallas import tpu as pltpu
from jax.experimental.pallas import tpu_sc as plsc

F32 = jnp.float32
BF16 = jnp.bfloat16
STASH = jnp.bfloat16

D_MODEL = 1024
HG_WIDTH = 512
HEAD_DIM = 128
N_HEADS = 4
HEADS_PER_STEP = 4
HEAD_GROUPS = N_HEADS // HEADS_PER_STEP
CONV_WIDTH = 512
CONV_K = 3
D_FF = 2816
CHUNK = 32
EPS = 1e-6
Q_SCALE = HEAD_DIM ** -0.5
N_DEV = 8

ADAM_LR = 0.001
ADAM_B1 = 0.9
ADAM_B2 = 0.999
ADAM_EPS = 1e-08
ADAM_WD = 0.01
ADAM_STEP = 10

VMEM_LIMIT_V7X = 56 * 1024 * 1024
VMEM_LIMIT_LARGE_V7X = 62 * 1024 * 1024

SMALL_ROWS = 16


def _params(sem, vmem=VMEM_LIMIT_V7X):
    return pltpu.CompilerParams(dimension_semantics=sem, vmem_limit_bytes=vmem)


def _mm(a, b):
    return jnp.dot(a.astype(BF16), b.astype(BF16), preferred_element_type=F32)


def _mm_nt(a, b):
    return lax.dot_general(a.astype(BF16), b.astype(BF16), (((1,), (1,)), ((), ())), preferred_element_type=F32)


def _mm_tn(a, b):
    return lax.dot_general(a.astype(BF16), b.astype(BF16), (((0,), (0,)), ((), ())), preferred_element_type=F32)


def _sigmoid(x):
    return 0.5 * jnp.tanh(0.5 * x) + 0.5


def _resident(shape):
    nd = len(shape)
    return pl.BlockSpec(shape, lambda *_: (0,) * nd, pipeline_mode=pl.Buffered(1))


def _full(shape):
    nd = len(shape)
    return pl.BlockSpec(shape, lambda *_: (0,) * nd)


def _shard_cols(w_ref):
    return jnp.concatenate([w_ref[s] for s in range(N_DEV)], axis=1)


N_HG = 4 * HG_WIDTH
N_CV = 3 * CONV_WIDTH
N_GT = 2 * D_MODEL
N_IN = N_HG + N_CV + N_GT


def _col(tm, n):
    return pl.BlockSpec((n, tm), lambda i: (0, i))


HALO = 8


def _fwd_in(x, g, w_in_t, conv_w, low, gn):
    T = x.shape[0]
    tm = min(512, T)
    nc = tm // CHUNK

    def body(x_ref, g_ref, w_ref, cw_ref, low_ref, gn_ref, ht_ref, hg_ref, cv_ref, gt_ref, cvo_ref, cvot_ref,
             o_ref, og_ref, ogt_ref, st_ref, tail_scr, s_scr):
        @pl.when(pl.program_id(0) == 0)
        def _():
            tail_scr[...] = jnp.zeros_like(tail_scr)
            s_scr[...] = jnp.zeros_like(s_scr)

        xv = x_ref[...]
        r = lax.rsqrt(jnp.mean(xv * xv, axis=-1, keepdims=True) + EPS)
        hf = xv * r * g_ref[...]
        h = hf.astype(BF16)
        ht_ref[...] = hf.T.astype(BF16)
        hg_ref[...] = _mm_nt(h, w_ref[:N_HG, :])
        cv = _mm_nt(h, w_ref[N_HG:N_HG + N_CV, :])
        cv_ref[...] = cv.astype(STASH)
        gt_ref[...] = _mm_nt(h, w_ref[N_HG + N_CV:, :]).astype(STASH)

        u = cv[:, :CONV_WIDTH] * cv[:, 2 * CONV_WIDTH:]
        row = lax.broadcasted_iota(jnp.int32, u.shape, 0)
        prev1 = tail_scr[HALO - 1:HALO, :]
        prev2 = tail_scr[HALO - 2:HALO - 1, :]
        u1 = jnp.where(row >= 1, pltpu.roll(u, 1, 0), prev1)
        u2 = jnp.where(row >= 2, pltpu.roll(u, 2, 0), jnp.where(row == 1, prev1, prev2))
        y = cw_ref[0:1, :] * u2 + cw_ref[1:2, :] * u1 + cw_ref[2:3, :] * u
        out = cv[:, CONV_WIDTH:2 * CONV_WIDTH] * y
        cvo_ref[...] = out.astype(BF16)
        cvot_ref[...] = out.T.astype(BF16)
        tail_scr[...] = u[tm - HALO:, :]

        _hg_fwd_tile(hg_ref, low_ref, gn_ref, o_ref, og_ref, ogt_ref, st_ref, s_scr, tm)

    row = lambda n: pl.BlockSpec((tm, n), lambda i: (i, 0))
    return pl.pallas_call(
        body, name="fwd_in", grid=(T // tm,),
        in_specs=[row(D_MODEL), _full((1, D_MODEL)), _resident(w_in_t.shape), _full((CONV_K, CONV_WIDTH)),
                  _full((2, HG_WIDTH)), _full((1, HEAD_DIM))],
        out_specs=[_col(tm, D_MODEL), row(N_HG), row(N_CV), row(N_GT), row(CONV_WIDTH), _col(tm, CONV_WIDTH),
                   row(HG_WIDTH), row(HG_WIDTH), _col(tm, HG_WIDTH),
                   pl.BlockSpec((N_HEADS, nc, HEAD_DIM, HEAD_DIM), lambda i: (0, i, 0, 0))],
        out_shape=[jax.ShapeDtypeStruct((D_MODEL, T), BF16), jax.ShapeDtypeStruct((T, N_HG), F32),
                   jax.ShapeDtypeStruct((T, N_CV), STASH), jax.ShapeDtypeStruct((T, N_GT), STASH),
                   jax.ShapeDtypeStruct((T, CONV_WIDTH), BF16), jax.ShapeDtypeStruct((CONV_WIDTH, T), BF16),
                   jax.ShapeDtypeStruct((T, HG_WIDTH), F32), jax.ShapeDtypeStruct((T, HG_WIDTH), BF16),
                   jax.ShapeDtypeStruct((HG_WIDTH, T), BF16),
                   jax.ShapeDtypeStruct((N_HEADS, T // CHUNK, HEAD_DIM, HEAD_DIM), F32)],
        scratch_shapes=[pltpu.VMEM((HALO, CONV_WIDTH), F32), pltpu.VMEM((N_HEADS, HEAD_DIM, HEAD_DIM), F32)],
        compiler_params=_params(("arbitrary",), vmem=VMEM_LIMIT_LARGE_V7X),
    )(x, g, w_in_t, conv_w, low, gn)


def _chunk_pos(shape):
    return lax.broadcasted_iota(jnp.int32, shape, 0) & (CHUNK - 1)


def _chunk_cumsum(x, pos):
    s = 1
    while s < CHUNK:
        x = x + jnp.where(pos >= s, pltpu.roll(x, s, 0), 0.0)
        s *= 2
    return x


def _chunk_rev_cumsum(x, pos):
    n = x.shape[0]
    s = 1
    while s < CHUNK:
        x = x + jnp.where(pos + s < CHUNK, pltpu.roll(x, n - s, 0), 0.0)
        s *= 2
    return x


def _lower_bound(low_ref):
    l0 = low_ref[0:1, :]
    l1 = low_ref[1:2, :]
    m = jnp.maximum(l0, l1)
    e0 = jnp.exp(l0 - m)
    e1 = jnp.exp(l1 - m)
    return e0 / (e0 + e1), e1 / (e0 + e1)


def _hg_gates(qr, fr, lb, pos, tb):
    sq = _sigmoid(qr)
    q = qr * sq * Q_SCALE
    sg = _sigmoid(fr)
    f = lb + (1.0 - lb) * sg
    k = 1.0 - f
    b = _chunk_cumsum(jnp.log(f), pos)
    b3 = b.reshape(tb // CHUNK, CHUNK, HEAD_DIM)
    anc = b3[:, CHUNK // 2 - 1:CHUNK // 2, :]
    last = b3[:, CHUNK - 1:CHUNK, :]
    d3 = b3 - anc
    e_qa3 = jnp.exp(d3)
    e_ka3 = jnp.exp(-d3)
    e_b3 = e_qa3 * jnp.exp(anc)
    e_ko3 = e_ka3 * jnp.exp(last - anc)
    dec = jnp.exp(last)
    flat = lambda a: a.reshape(tb, HEAD_DIM)
    return sq, q, sg, f, k, flat(e_qa3), flat(e_ka3), flat(e_b3), flat(e_ko3), dec


def _intra_mask(sb):
    r = lax.broadcasted_iota(jnp.int32, (sb, sb), 0)
    c = lax.broadcasted_iota(jnp.int32, (sb, sb), 1)
    return ((r // CHUNK) == (c // CHUNK)) & (c <= r)


def _hg_fwd_tile(hg_ref, low_ref, gn_ref, o_ref, og_ref, ogt_ref, st_ref, s_scr, tb):
    sb = min(256, tb)
    nc = tb // CHUNK
    q_ref, f_ref, i_ref, g_ref = (hg_ref.at[:, p * HG_WIDTH:(p + 1) * HG_WIDTH] for p in range(4))
    pos = _chunk_pos((tb, HEAD_DIM))
    mask = _intra_mask(sb)
    lanes = [slice(hh * HEAD_DIM, (hh + 1) * HEAD_DIM) for hh in range(N_HEADS)]
    qi, ko, vb, dec, st = [], [], [], [], []
    for hh, ln in enumerate(lanes):
        lb, _ = _lower_bound(low_ref.at[:, ln])
        _, q, _, _, k, e_qa, e_ka, e_b, e_ko, dec_h = _hg_gates(q_ref[:, ln], f_ref[:, ln], lb, pos, tb)
        qh = (q * e_qa).astype(BF16)
        kh = (k * e_ka).astype(BF16)
        qi.append((q * e_b).astype(BF16))
        ko.append((k * e_ko).astype(BF16))
        vb.append(i_ref[:, ln].astype(BF16))
        dec.append(dec_h)
        st.append(s_scr[hh])
        for s in range(tb // sb):
            sl = slice(s * sb, (s + 1) * sb)
            p = jnp.where(mask, _mm_nt(qh[sl], kh[sl]), 0.0)
            o_ref[sl, ln] = _mm(p, vb[hh][sl])
    for c in range(nc):
        sl = slice(c * CHUNK, (c + 1) * CHUNK)
        for hh, ln in enumerate(lanes):
            st_ref[hh, c] = st[hh]
            o_ref[sl, ln] = o_ref[sl, ln] + _mm_nt(qi[hh][sl], st[hh])
            st[hh] = dec[hh][c] * st[hh] + _mm_tn(vb[hh][sl], ko[hh][sl])
    for hh, ln in enumerate(lanes):
        s_scr[hh] = st[hh]
        o = o_ref[:, ln]
        r = lax.rsqrt(jnp.mean(o * o, axis=-1, keepdims=True) + EPS)
        gr = g_ref[:, ln]
        og = (o * r * gn_ref[...]) * (gr * _sigmoid(gr))
        og_ref[:, ln] = og.astype(BF16)
        ogt_ref[ln, :] = og.T.astype(BF16)


def _merge_fwd(og, cvo, gt, x, wa, wb, wo):
    T = x.shape[0]
    tm = min(1024, T)

    def body(og_ref, cvo_ref, gt_ref, x_ref, wa_ref, wb_ref, wo_ref, x1_ref, mgt_ref):
        ya = jnp.dot(og_ref[...], _shard_cols(wa_ref), preferred_element_type=F32)
        yb = jnp.dot(cvo_ref[...], _shard_cols(wb_ref), preferred_element_type=F32)
        m = (_sigmoid(gt_ref[:, :D_MODEL].astype(F32)) * ya
             + _sigmoid(gt_ref[:, D_MODEL:].astype(F32)) * yb)
        mgt_ref[...] = m.T.astype(BF16)
        x1_ref[...] = x_ref[...] + jnp.dot(m.astype(BF16), wo_ref[...], preferred_element_type=F32)

    row = lambda n: pl.BlockSpec((tm, n), lambda i: (i, 0))
    return pl.pallas_call(
        body, name="merge_fwd", grid=(T // tm,),
        in_specs=[row(HG_WIDTH), row(CONV_WIDTH), row(2 * D_MODEL), row(D_MODEL),
                  _resident(wa.shape), _resident(wb.shape), _resident(wo.shape)],
        out_specs=[row(D_MODEL), _col(tm, D_MODEL)],
        out_shape=[jax.ShapeDtypeStruct((T, D_MODEL), F32), jax.ShapeDtypeStruct((D_MODEL, T), BF16)],
        compiler_params=_params(("parallel",)),
    )(og, cvo, gt, x, wa, wb, wo)


def _ffn_fwd_loss(x1, g, wg, wu, wd, target, g_fin):
    T = x1.shape[0]
    tm = min(512, T)

    def body(x_ref, g_ref, wg_ref, wu_ref, wd_ref, t_ref, gf_ref,
             ht_ref, gate_ref, up_ref, act_ref, loss_ref, dgf_ref, dx2_ref, dx2t_ref):
        @pl.when(pl.program_id(0) == 0)
        def _():
            loss_ref[...] = jnp.zeros_like(loss_ref)
            dgf_ref[...] = jnp.zeros_like(dgf_ref)

        xv = x_ref[...]
        r = lax.rsqrt(jnp.mean(xv * xv, axis=-1, keepdims=True) + EPS)
        hf = xv * r * g_ref[...]
        h = hf.astype(BF16)
        ht_ref[...] = hf.T.astype(BF16)
        gate = _mm_nt(h, wg_ref[...])
        up = _mm_nt(h, wu_ref[...])
        gate_ref[...] = gate.astype(STASH)
        up_ref[...] = up.astype(STASH)
        act = (gate * _sigmoid(gate) * up).astype(BF16)
        act_ref[...] = act
        x2 = xv + jnp.dot(act, wd_ref[...], preferred_element_type=F32)

        gv = gf_ref[...]
        r2 = lax.rsqrt(jnp.mean(x2 * x2, axis=-1, keepdims=True) + EPS)
        xh = x2 * r2
        err = xh * gv - t_ref[...]
        loss_ref[...] += 0.5 * jnp.sum(jnp.mean(err * err, axis=-1, keepdims=True), axis=0, keepdims=True)
        dy = err * (1.0 / D_MODEL)
        dgf_ref[...] += jnp.sum(dy * xh, axis=0, keepdims=True)
        w = dy * gv
        dx2 = r2 * (w - xh * jnp.mean(w * xh, axis=-1, keepdims=True))
        dx2_ref[...] = dx2
        dx2t_ref[...] = dx2.T.astype(BF16)

    row = lambda n: pl.BlockSpec((tm, n), lambda i: (i, 0))
    return pl.pallas_call(
        body, name="ffn_fwd_loss", grid=(T // tm,),
        in_specs=[row(D_MODEL), _full((1, D_MODEL)), _resident(wg.shape), _resident(wu.shape), _resident(wd.shape),
                  row(D_MODEL), _full((1, D_MODEL))],
        out_specs=[_col(tm, D_MODEL), row(D_FF), row(D_FF), row(D_FF), _full((1, 128)), _full((1, D_MODEL)),
                   row(D_MODEL), _col(tm, D_MODEL)],
        out_shape=[jax.ShapeDtypeStruct((D_MODEL, T), BF16), jax.ShapeDtypeStruct((T, D_FF), STASH),
                   jax.ShapeDtypeStruct((T, D_FF), STASH), jax.ShapeDtypeStruct((T, D_FF), BF16),
                   jax.ShapeDtypeStruct((1, 128), F32), jax.ShapeDtypeStruct((1, D_MODEL), F32),
                   jax.ShapeDtypeStruct((T, D_MODEL), F32), jax.ShapeDtypeStruct((D_MODEL, T), BF16)],
        compiler_params=_params(("arbitrary",), vmem=VMEM_LIMIT_LARGE_V7X),
    )(x1, g, wg, wu, wd, target, g_fin)


def _ffn_bwd(dx2, x1, gate, up, g, wg, wu, wd):
    T = x1.shape[0]
    tm = min(512, T)

    def body(dx2_ref, x_ref, gate_ref, up_ref, g_ref, wg_ref, wu_ref, wd_ref, dgate_ref, dup_ref, dx1_ref, dgn_ref):
        @pl.when(pl.program_id(0) == 0)
        def _():
            dgn_ref[...] = jnp.zeros_like(dgn_ref)

        dx2 = dx2_ref[...]
        dact = _mm_nt(dx2, wd_ref[...])
        gate = gate_ref[...].astype(F32)
        s = _sigmoid(gate)
        dgate = (dact * up_ref[...].astype(F32) * (s * (1.0 + gate * (1.0 - s)))).astype(BF16)
        dup = (dact * (gate * s)).astype(BF16)
        dgate_ref[...] = dgate
        dup_ref[...] = dup
        dh = _mm(dgate, wg_ref[...]) + _mm(dup, wu_ref[...])
        xv = x_ref[...]
        r = lax.rsqrt(jnp.mean(xv * xv, axis=-1, keepdims=True) + EPS)
        xh = xv * r
        dgn_ref[...] += jnp.sum(dh * xh, axis=0, keepdims=True)
        w = dh * g_ref[...]
        dx1_ref[...] = dx2 + r * (w - xh * jnp.mean(w * xh, axis=-1, keepdims=True))

    row = lambda n: pl.BlockSpec((tm, n), lambda i: (i, 0))
    return pl.pallas_call(
        body, name="ffn_bwd", grid=(T // tm,),
        in_specs=[row(D_MODEL), row(D_MODEL), row(D_FF), row(D_FF), _full((1, D_MODEL)),
                  _resident(wg.shape), _resident(wu.shape), _resident(wd.shape)],
        out_specs=[row(D_FF), row(D_FF), row(D_MODEL), _full((1, D_MODEL))],
        out_shape=[jax.ShapeDtypeStruct((T, D_FF), BF16), jax.ShapeDtypeStruct((T, D_FF), BF16),
                   jax.ShapeDtypeStruct((T, D_MODEL), F32), jax.ShapeDtypeStruct((1, D_MODEL), F32)],
        compiler_params=_params(("arbitrary",), vmem=VMEM_LIMIT_LARGE_V7X),
    )(dx2, x1, gate, up, g, wg, wu, wd)


def _merge_bwd(dx1, og, cvo, gt, cv, wa, wb, wo, conv_w):
    T = dx1.shape[0]
    tm = min(512, T)
    nt = T // tm

    def body(dx_ref, og_ref, cvo_ref, gt_ref, cv_ref, halo_ref, wa_ref, wb_ref, wo_ref, cw_ref,
             dgt_ref, dya_ref, dyb_ref, dog_ref, dcv_ref, dcw_ref, prev_u, next_dy):
        step = pl.program_id(0)

        @pl.when(step == 0)
        def _():
            next_dy[...] = jnp.zeros_like(next_dy)
            dcw_ref[...] = jnp.zeros_like(dcw_ref)

        dm = _mm_nt(dx_ref[...], wo_ref[...])
        wa = _shard_cols(wa_ref)
        wb = _shard_cols(wb_ref)
        ya = jnp.dot(og_ref[...], wa, preferred_element_type=F32)
        yb = jnp.dot(cvo_ref[...], wb, preferred_element_type=F32)
        sa = _sigmoid(gt_ref[:, :D_MODEL].astype(F32))
        sb = _sigmoid(gt_ref[:, D_MODEL:].astype(F32))
        dgt_ref[:, :D_MODEL] = (dm * ya * (sa * (1.0 - sa))).astype(BF16)
        dgt_ref[:, D_MODEL:] = (dm * yb * (sb * (1.0 - sb))).astype(BF16)
        dya = (dm * sa).astype(BF16)
        dyb = (dm * sb).astype(BF16)
        dya_ref[...] = dya
        dyb_ref[...] = dyb
        dog_ref[...] = _mm_nt(dya, wa)
        dcvo = _mm_nt(dyb, wb)

        cvt = cv_ref[...].astype(F32)
        c, bg, xb = cvt[:, :CONV_WIDTH], cvt[:, CONV_WIDTH:2 * CONV_WIDTH], cvt[:, 2 * CONV_WIDTH:]
        halo = halo_ref[...].astype(F32)
        first_tile = step == nt - 1
        prev_u[...] = jnp.where(first_tile, 0.0, halo[:, :CONV_WIDTH] * halo[:, 2 * CONV_WIDTH:])
        u = c * xb
        row = lax.broadcasted_iota(jnp.int32, u.shape, 0)
        p1 = prev_u[HALO - 1:HALO, :]
        p2 = prev_u[HALO - 2:HALO - 1, :]
        u1 = jnp.where(row >= 1, pltpu.roll(u, 1, 0), p1)
        u2 = jnp.where(row >= 2, pltpu.roll(u, 2, 0), jnp.where(row == 1, p1, p2))
        w0, w1, w2 = cw_ref[0:1, :], cw_ref[1:2, :], cw_ref[2:3, :]
        y = w0 * u2 + w1 * u1 + w2 * u
        dcv_ref[:, CONV_WIDTH:2 * CONV_WIDTH] = (dcvo * y).astype(BF16)
        dy = dcvo * bg
        dcw_ref[0:1, :] += jnp.sum(dy * u2, axis=0, keepdims=True)
        dcw_ref[1:2, :] += jnp.sum(dy * u1, axis=0, keepdims=True)
        dcw_ref[2:3, :] += jnp.sum(dy * u, axis=0, keepdims=True)
        n1 = next_dy[0:1, :]
        n2 = next_dy[1:2, :]
        dy1 = jnp.where(row < tm - 1, pltpu.roll(dy, tm - 1, 0), n1)
        dy2 = jnp.where(row < tm - 2, pltpu.roll(dy, tm - 2, 0), jnp.where(row == tm - 2, n1, n2))
        du = w2 * dy + w1 * dy1 + w0 * dy2
        dcv_ref[:, :CONV_WIDTH] = (du * xb).astype(BF16)
        dcv_ref[:, 2 * CONV_WIDTH:] = (du * c).astype(BF16)
        next_dy[...] = dy[:HALO, :]

    rt = lambda i: nt - 1 - i
    row = lambda n: pl.BlockSpec((tm, n), lambda i: (rt(i), 0))
    halo = pl.BlockSpec((HALO, N_CV), lambda i: (jnp.maximum(rt(i) * (tm // HALO) - 1, 0), 0))
    return pl.pallas_call(
        body, name="merge_bwd", grid=(nt,),
        in_specs=[row(D_MODEL), row(HG_WIDTH), row(CONV_WIDTH), row(2 * D_MODEL), row(N_CV), halo,
                  _resident(wa.shape), _resident(wb.shape), _resident(wo.shape), _full((CONV_K, CONV_WIDTH))],
        out_specs=[row(2 * D_MODEL), row(D_MODEL), row(D_MODEL), row(HG_WIDTH), row(N_CV),
                   _full((CONV_K, CONV_WIDTH))],
        out_shape=[jax.ShapeDtypeStruct((T, 2 * D_MODEL), BF16), jax.ShapeDtypeStruct((T, D_MODEL), BF16),
                   jax.ShapeDtypeStruct((T, D_MODEL), BF16), jax.ShapeDtypeStruct((T, HG_WIDTH), F32),
                   jax.ShapeDtypeStruct((T, N_CV), BF16), jax.ShapeDtypeStruct((CONV_K, CONV_WIDTH), F32)],
        scratch_shapes=[pltpu.VMEM((HALO, CONV_WIDTH), F32), pltpu.VMEM((HALO, CONV_WIDTH), F32)],
        compiler_params=_params(("arbitrary",)),
    )(dx1, og, cvo, gt, cv, cv, wa, wb, wo, conv_w)


def _drop_operands(body, first, count):
    def wrapped(*refs):
        return body(*refs[:first], *refs[first + count:])
    return wrapped


def _hg_bwd(dog, hg, o, st, low, gn, after=()):
    T = hg.shape[0]
    tb = min(512, T)
    sb = min(256, tb)
    nb = T // tb
    nc = tb // CHUNK
    wid = HEADS_PER_STEP * HEAD_DIM

    def body(q_ref, f_ref, i_ref, g_ref, low_ref, gn_ref, o_ref, dog_ref, st_ref,
             dhg_ref, dlow_ref, dgn_ref,
             ds_scr, dqi_scr, dko_scr, dv_scr, dd_scr, dqh_scr, dkh_scr):
        h = pl.program_id(0)
        t = pl.program_id(1)
        dq_ref, df_ref, di_ref, dg_ref = (dhg_ref.at[:, p * HG_WIDTH:(p + 1) * HG_WIDTH] for p in range(4))

        @pl.when(t == 0)
        def _():
            ds_scr[...] = jnp.zeros_like(ds_scr)
            dlow_ref[...] = jnp.zeros_like(dlow_ref)

        @pl.when((t == 0) & (h == 0))
        def _():
            dgn_ref[...] = jnp.zeros_like(dgn_ref)

        pos = _chunk_pos((tb, HEAD_DIM))
        mask = _intra_mask(sb)
        gnv = gn_ref[...]
        lanes = [slice(hh * HEAD_DIM, (hh + 1) * HEAD_DIM) for hh in range(HEADS_PER_STEP)]
        heads = []
        for hh, ln in enumerate(lanes):
            lb, lb1 = _lower_bound(low_ref.at[:, ln])
            qr = q_ref[:, ln]
            sq, q, sg, f, k, e_qa, e_ka, e_b, e_ko, dec = _hg_gates(qr, f_ref[:, ln], lb, pos, tb)

            gr = g_ref[:, ln]
            o = o_ref[:, ln]
            dog_v = dog_ref[:, ln]
            sgr = _sigmoid(gr)
            r = lax.rsqrt(jnp.mean(o * o, axis=-1, keepdims=True) + EPS)
            oh = o * r
            dg_ref[:, ln] = (dog_v * (oh * gnv) * (sgr * (1.0 + gr * (1.0 - sgr)))).astype(BF16)
            don = dog_v * (gr * sgr)
            dgn_ref[...] += jnp.sum(don * oh, axis=0, keepdims=True)
            w = don * gnv
            do = (r * (w - oh * jnp.mean(w * oh, axis=-1, keepdims=True))).astype(BF16)

            qh = (q * e_qa).astype(BF16)
            kh = (k * e_ka).astype(BF16)
            qi = (q * e_b).astype(BF16)
            ko = (k * e_ko).astype(BF16)
            vb = i_ref[:, ln].astype(BF16)

            for s in range(tb // sb):
                sl = slice(s * sb, (s + 1) * sb)
                p = jnp.where(mask, _mm_nt(qh[sl], kh[sl]), 0.0).astype(BF16)
                dp = jnp.where(mask, _mm_nt(do[sl], vb[sl]), 0.0).astype(BF16)
                dv_scr[sl, ln] = _mm_tn(p, do[sl])
                dqh_scr[sl, ln] = _mm(dp, kh[sl])
                dkh_scr[sl, ln] = _mm_tn(dp, qh[sl])
            heads.append(dict(lb=lb, lb1=lb1, qr=qr, sq=sq, q=q, sg=sg, f=f, k=k, e_qa=e_qa, e_ka=e_ka, e_b=e_b,
                              e_ko=e_ko, dec=dec, do=do, qi=qi, ko=ko, vb=vb, ds=ds_scr[hh]))

        for c in reversed(range(nc)):
            sl = slice(c * CHUNK, (c + 1) * CHUNK)
            for hh, ln in enumerate(lanes):
                hd = heads[hh]
                ds = hd["ds"]
                st_c = st_ref[hh, c]
                dqi_scr[sl, ln] = _mm(hd["do"][sl], st_c)
                dko_scr[sl, ln] = _mm(hd["vb"][sl], ds)
                dv_scr[sl, ln] = dv_scr[sl, ln] + _mm_nt(hd["ko"][sl], ds)
                dec_c = hd["dec"][c]
                dd_scr[sl, ln] = jnp.broadcast_to(dec_c * jnp.sum(ds * st_c, axis=0, keepdims=True),
                                                  (CHUNK, HEAD_DIM))
                hd["ds"] = dec_c * ds + _mm_tn(hd["do"][sl], hd["qi"][sl])

        for hh, ln in enumerate(lanes):
            hd = heads[hh]
            ds_scr[hh] = hd["ds"]
            q, k, lb = hd["q"], hd["k"], hd["lb"]
            dko_e = dko_scr[:, ln] * hd["e_ko"]
            dq = dqh_scr[:, ln] * hd["e_qa"] + dqi_scr[:, ln] * hd["e_b"]
            dk = dkh_scr[:, ln] * hd["e_ka"] + dko_e
            kd3 = (k * dko_e).reshape(nc, CHUNK, HEAD_DIM)
            last = jnp.broadcast_to(jnp.sum(kd3, axis=1, keepdims=True), kd3.shape).reshape(tb, HEAD_DIM)
            db = q * dq - k * dk + jnp.where(pos == CHUNK - 1, dd_scr[:, ln] + last, 0.0)
            dlg = _chunk_rev_cumsum(db, pos)
            dfv = dlg / hd["f"] - dk
            s_low = jnp.sum(dfv * (1.0 - hd["sg"]), axis=0, keepdims=True)
            dlow_ref[0:1, ln] += s_low * lb * (1.0 - lb)
            dlow_ref[1:2, ln] += -s_low * lb * hd["lb1"]
            df_ref[:, ln] = (dfv * (1.0 - lb) * hd["sg"] * (1.0 - hd["sg"])).astype(BF16)
            dq_ref[:, ln] = (dq * Q_SCALE * (hd["sq"] * (1.0 + hd["qr"] * (1.0 - hd["sq"])))).astype(BF16)
            di_ref[:, ln] = dv_scr[:, ln].astype(BF16)

    rt = lambda t: nb - 1 - t
    col = lambda p: pl.BlockSpec((tb, wid), lambda h, t: (rt(t), p * HEAD_GROUPS + h))
    hcol = pl.BlockSpec((tb, wid), lambda h, t: (rt(t), h))
    assert HEAD_GROUPS == 1
    tile = pltpu.VMEM((tb, wid), F32)
    return pl.pallas_call(
        _drop_operands(body, 9, len(after)), name="hg_bwd", grid=(HEAD_GROUPS, nb),
        in_specs=[col(0), col(1), col(2), col(3), pl.BlockSpec((2, wid), lambda h, t: (0, h)),
                  pl.BlockSpec((1, HEAD_DIM), lambda h, t: (0, 0)), hcol, hcol,
                  pl.BlockSpec((HEADS_PER_STEP, nc, HEAD_DIM, HEAD_DIM), lambda h, t: (h, rt(t), 0, 0))]
                 + [HBM_SPEC] * len(after),
        out_specs=[pl.BlockSpec((tb, N_HG), lambda h, t: (rt(t), 0)), pl.BlockSpec((2, wid), lambda h, t: (0, h)),
                   pl.BlockSpec((1, HEAD_DIM), lambda h, t: (0, 0))],
        out_shape=[jax.ShapeDtypeStruct((T, N_HG), BF16), jax.ShapeDtypeStruct((2, HG_WIDTH), F32),
                   jax.ShapeDtypeStruct((1, HEAD_DIM), F32)],
        scratch_shapes=[pltpu.VMEM((HEADS_PER_STEP, HEAD_DIM, HEAD_DIM), F32), tile, tile, tile, tile, tile, tile],
        compiler_params=_params(("arbitrary", "arbitrary")),
    )(hg, hg, hg, hg, low, gn, o, dog, st, *after)


def _in_bwd(dparts, w_in, x, dx1, g, after=()):
    T = x.shape[0]
    tm = min(512, T)
    widths = [p.shape[1] for p in dparts]
    offs = [sum(widths[:i]) for i in range(len(widths))]
    n = len(dparts)

    def body(*refs):
        d_refs = refs[:n]
        w_ref, x_ref, dx1_ref, g_ref, dx_ref, dgn_ref = refs[n:]

        @pl.when(pl.program_id(0) == 0)
        def _():
            dgn_ref[...] = jnp.zeros_like(dgn_ref)

        dh = None
        for d_ref, off, wd in zip(d_refs, offs, widths):
            part = _mm(d_ref[...], w_ref[off:off + wd, :])
            dh = part if dh is None else dh + part
        xv = x_ref[...]
        r = lax.rsqrt(jnp.mean(xv * xv, axis=-1, keepdims=True) + EPS)
        xh = xv * r
        dgn_ref[...] += jnp.sum(dh * xh, axis=0, keepdims=True)
        w = dh * g_ref[...]
        dx_ref[...] = dx1_ref[...] + r * (w - xh * jnp.mean(w * xh, axis=-1, keepdims=True))

    row = lambda m: pl.BlockSpec((tm, m), lambda i: (i, 0))
    return pl.pallas_call(
        _drop_operands(body, n + 4, len(after)), name="in_bwd", grid=(T // tm,),
        in_specs=[row(wd) for wd in widths] + [_resident(w_in.shape), row(D_MODEL), row(D_MODEL), _full((1, D_MODEL))]
                 + [HBM_SPEC] * len(after),
        out_specs=[row(D_MODEL), _full((1, D_MODEL))],
        out_shape=[jax.ShapeDtypeStruct((T, D_MODEL), F32), jax.ShapeDtypeStruct((1, D_MODEL), F32)],
        compiler_params=_params(("arbitrary",)),
    )(*dparts, w_in, x, dx1, g, *after)


def _wgrad_ffn(h2t, dgate, dup, dx2t, act, tn=256):
    M, T = h2t.shape
    N = dgate.shape[1]

    def body(h_ref, x_ref, dg_ref, du_ref, act_ref, og_ref, ou_ref, od_ref):
        h = h_ref[...]
        og_ref[...] = _mm(h, dg_ref[...]).T.astype(BF16)
        ou_ref[...] = _mm(h, du_ref[...]).T.astype(BF16)
        od_ref[...] = _mm(x_ref[...], act_ref[...]).T.astype(BF16)

    rhs = pl.BlockSpec((T, tn), lambda j: (0, j))
    out_spec = pl.BlockSpec((tn, M), lambda j: (j, 0))
    out = jax.ShapeDtypeStruct((N, M), BF16)
    return pl.pallas_call(
        body, name="wgrad_ffn", grid=(N // tn,),
        in_specs=[_resident((M, T)), _resident((M, T)), rhs, rhs, rhs], out_specs=[out_spec] * 3,
        out_shape=[out, out, out],
        compiler_params=_params(("parallel",)),
    )(h2t, dx2t, dgate, dup, act)


def _wgrad_out_branches(ogt, dya, cvot, dyb, mgt, dx1, after=()):
    M, T = ogt.shape
    N = dya.shape[1]
    c = N // N_DEV
    per = 2
    tn = per * c

    def body(at_ref, da_ref, bt_ref, db_ref, mt_ref, dx_ref, oa_ref, ob_ref, oo_ref):
        ga = _mm(at_ref[...], da_ref[...])
        gb = _mm(bt_ref[...], db_ref[...])
        for s in range(per):
            oa_ref[s] = ga[:, s * c:(s + 1) * c].astype(BF16)
            ob_ref[s] = gb[:, s * c:(s + 1) * c].astype(BF16)
        oo_ref[...] = _mm(mt_ref[...], dx_ref[...]).astype(BF16)

    rhs = pl.BlockSpec((T, tn), lambda j: (0, j))
    owners = pl.BlockSpec((per, M, c), lambda j: (j, 0, 0))
    out = jax.ShapeDtypeStruct((N_DEV, M, c), BF16)
    return pl.pallas_call(
        _drop_operands(body, 6, len(after)), name="wgrad_out_branches", grid=(N // tn,),
        in_specs=[_resident((M, T)), rhs, _resident((M, T)), rhs, _resident(mgt.shape), rhs] + [HBM_SPEC] * len(after),
        out_specs=[owners, owners, pl.BlockSpec((mgt.shape[0], tn), lambda j: (0, j))],
        out_shape=[out, out, jax.ShapeDtypeStruct((mgt.shape[0], dx1.shape[1]), BF16)],
        compiler_params=_params(("parallel",)),
    )(ogt, dya, cvot, dyb, mgt, dx1, *after)


def _wgrad_in(ht, dparts, after=(), riders=()):
    M, T = ht.shape
    tn = 512
    nblk = [p.shape[1] // tn for p in dparts]
    start = [sum(nblk[:i]) for i in range(len(nblk))]
    n = len(dparts)
    steps = sum(nblk)
    nr = len(riders)
    rows = [r[0].shape[0] // steps for r in riders]
    assert all(r[0].shape[0] == rr * steps and rr % 16 == 0 for r, rr in zip(riders, rows))

    def body(a_ref, *refs):
        d_refs = refs[:n]
        rider_in = refs[n:n + 4 * nr]
        o_ref = refs[n + 4 * nr]
        rider_out = refs[n + 4 * nr + 1:]
        j = pl.program_id(0)
        for d_ref, s, nb in zip(d_refs, start, nblk):
            @pl.when((j >= s) & (j < s + nb))
            def _():
                o_ref[...] = _mm(a_ref[...], d_ref[...]).T.astype(BF16)
        for i in range(nr):
            w_ref, p_ref, m_ref, v_ref = rider_in[4 * i:4 * i + 4]
            g = p_ref[0].astype(F32)
            for k in range(1, 4):
                g = g + p_ref[k].astype(F32)
            delta, m_new, v_new = _adamw_math(w_ref[...], g, m_ref[...], v_ref[...])
            for k, val in enumerate((g, delta, m_new, v_new)):
                rider_out[4 * i + k][...] = val

    def piece_spec(s, nb):
        return pl.BlockSpec((T, tn), lambda j: (0, jnp.clip(j - s, 0, nb - 1)))

    rider_specs, rider_out_specs, rider_out_shape, rider_args = [], [], [], []
    for (w, parts, m, v), rr in zip(riders, rows):
        blk = pl.BlockSpec((rr, w.shape[1]), lambda j: (j, 0))
        rider_specs += [blk, pl.BlockSpec((4, rr, w.shape[1]), lambda j: (0, j, 0)), blk, blk]
        rider_out_specs += [blk] * 4
        rider_out_shape += [jax.ShapeDtypeStruct(w.shape, F32)] * 4
        rider_args += [w, parts, m, v]
    outs = pl.pallas_call(
        _drop_operands(body, 1 + n + 4 * nr, len(after)), name="wgrad_in", grid=(steps,),
        in_specs=[_resident((M, T))] + [piece_spec(s, nb) for s, nb in zip(start, nblk)] + rider_specs
                 + [HBM_SPEC] * len(after),
        out_specs=[pl.BlockSpec((tn, M), lambda j: (j, 0))] + rider_out_specs,
        out_shape=[jax.ShapeDtypeStruct((steps * tn, M), BF16)] + rider_out_shape,
        compiler_params=_params(("parallel",)),
    )(ht, *dparts, *rider_args, *after)
    return outs[0], [outs[1 + 4 * i:5 + 4 * i] for i in range(nr)]


def _adamw_math(w, g, m, v):
    m = ADAM_B1 * m + (1.0 - ADAM_B1) * g
    v = ADAM_B2 * v + (1.0 - ADAM_B2) * (g * g)
    m_hat = m / (1.0 - ADAM_B1 ** ADAM_STEP)
    v_hat = v / (1.0 - ADAM_B2 ** ADAM_STEP)
    delta = -ADAM_LR * (m_hat / (jnp.sqrt(v_hat) + ADAM_EPS) + ADAM_WD * w)
    return delta, m, v


def _adamw_sum(name, ws, parts, ms, vs):
    n = len(ws)
    steps = min(_row_steps(w.shape[0]) for w in ws)
    rows = [w.shape[0] // steps for w in ws]

    def body(*refs):
        w_refs, p_refs, m_refs, v_refs = (refs[k * n:(k + 1) * n] for k in range(4))
        out_refs = refs[4 * n:]
        for i in range(n):
            g = p_refs[i][0].astype(F32)
            for k in range(1, 4):
                g = g + p_refs[i][k].astype(F32)
            delta, m_new, v_new = _adamw_math(w_refs[i][...], g, m_refs[i][...], v_refs[i][...])
            for k, val in enumerate((g, delta, m_new, v_new)):
                out_refs[4 * i + k][...] = val

    blk = [pl.BlockSpec((r, w.shape[1]), lambda s: (s, 0)) for r, w in zip(rows, ws)]
    pblk = [pl.BlockSpec((4, r, w.shape[1]), lambda s: (0, s, 0)) for r, w in zip(rows, ws)]
    out_specs, out_shape = [], []
    for b, w in zip(blk, ws):
        out_specs += [b] * 4
        out_shape += [jax.ShapeDtypeStruct(w.shape, F32)] * 4
    flat = pl.pallas_call(
        body, name=name, grid=(steps,),
        in_specs=blk + pblk + blk + blk, out_specs=out_specs, out_shape=out_shape,
        compiler_params=_params(("parallel",)),
    )(*ws, *parts, *ms, *vs)
    return [flat[4 * i:4 * i + 4] for i in range(n)]


_SMALL_SLOTS = (("norm_mix_g", 0, 1, 1024), ("norm_ffn_g", 1, 1, 1024), ("norm_final_g", 2, 1, 1024),
                ("lower_bounds", 3, 2, 512), ("hg_norm_g", 5, 1, 128), ("loss", 6, 1, 128), ("conv_w", 8, 3, 512))
_SMALL_PARAMS = tuple(s for s in _SMALL_SLOTS if s[0] != "loss")
CONV_SHARD = CONV_WIDTH // N_DEV


def _small_pack(small):
    def body(*refs):
        out = refs[-1]
        out[...] = jnp.zeros_like(out)
        for ref, (_, row, rows, lanes) in zip(refs[:-1], _SMALL_SLOTS):
            out[row:row + rows, 0:lanes] = ref[...]

    vmem = pl.BlockSpec(memory_space=pltpu.VMEM)
    return pl.pallas_call(
        body, name="small_pack", in_specs=[vmem] * len(_SMALL_SLOTS), out_specs=vmem,
        out_shape=jax.ShapeDtypeStruct((SMALL_ROWS, 1024), F32),
    )(*[small[name] for name, _, _, _ in _SMALL_SLOTS])


def _small_update(gathered, dev, w, m, v):
    n = len(_SMALL_PARAMS)

    def body(dev_ref, g_ref, *refs):
        w_refs, m_refs, v_refs = refs[:n], refs[n:2 * n], refs[2 * n:3 * n]
        loss_ref, out_refs, sum_scr = refs[3 * n], refs[3 * n + 1:-1], refs[-1]
        total = g_ref[0]
        for k in range(1, N_DEV):
            total = total + g_ref[k]
        sum_scr[...] = total
        loss_ref[...] = sum_scr[6:7, 0:128]
        for p, (name, row, rows, lanes) in enumerate(_SMALL_PARAMS):
            if name == "conv_w":
                for r in range(rows):
                    g = sum_scr[row + r:row + r + 1, 0:CONV_SHARD]
                    for s in range(1, N_DEV):
                        mine = sum_scr[row + r:row + r + 1, s * CONV_SHARD:(s + 1) * CONV_SHARD]
                        g = jnp.where(dev_ref[0] == s, mine, g)
                    delta, m_new, v_new = _adamw_math(w_refs[p][r], g, m_refs[p][r], v_refs[p][r])
                    out_refs[4 * p][r] = g
                    out_refs[4 * p + 1][r] = delta
                    out_refs[4 * p + 2][r] = m_new
                    out_refs[4 * p + 3][r] = v_new
                continue
            g = sum_scr[row:row + rows, 0:lanes]
            delta, m_new, v_new = _adamw_math(w_refs[p][...], g, m_refs[p][...], v_refs[p][...])
            out_refs[4 * p][...] = g
            out_refs[4 * p + 1][...] = delta
            out_refs[4 * p + 2][...] = m_new
            out_refs[4 * p + 3][...] = v_new

    vmem = pl.BlockSpec(memory_space=pltpu.VMEM)
    outs = [jax.ShapeDtypeStruct((1, 128), F32)]
    for a in w:
        outs += [jax.ShapeDtypeStruct(a.shape, F32)] * 4
    return pl.pallas_call(
        body, name="small_update",
        in_specs=[pl.BlockSpec(memory_space=pltpu.SMEM)] + [vmem] * (1 + 3 * n), out_specs=[vmem] * len(outs),
        out_shape=outs, scratch_shapes=[pltpu.VMEM((SMALL_ROWS, 1024), F32)],
    )(dev, gathered, *w, *m, *v)


def _row_steps(rows):
    for steps in (4, 2):
        if rows % (16 * steps) == 0:
            return steps
    return 1


def _pair_sum(name, by_owner, got, core, after=()):
    n = len(got)

    def body(core_ref, *refs):
        for a_ref, b_ref, o_ref in zip(refs[:n], refs[n:2 * n], refs[2 * n:]):
            o_ref[...] = (a_ref[...].astype(F32) + b_ref[...].astype(F32)).astype(BF16)

    def blk(g):
        return pl.BlockSpec((None,) + g.shape[1:], lambda k, core_ref: (k, 0, 0))

    def mine(g):
        return pl.BlockSpec((None,) + g.shape[1:], lambda k, core_ref: (2 * k + core_ref[0], 0, 0))

    return pl.pallas_call(
        _drop_operands(body, 1 + 2 * n, len(after)), name=name,
        grid_spec=pltpu.PrefetchScalarGridSpec(
            num_scalar_prefetch=1, grid=(4,),
            in_specs=[mine(g) for g in got] + [blk(g) for g in got] + [HBM_SPEC] * len(after),
            out_specs=[blk(g) for g in got]),
        out_shape=[jax.ShapeDtypeStruct(g.shape, BF16) for g in got],
        compiler_params=_params(("parallel",)),
    )(core, *by_owner, *got, *after)


MESH = pl.DeviceIdType.MESH
HBM_SPEC = pl.BlockSpec(memory_space=pl.ANY)


def _handshake(peers):
    barrier = pltpu.get_barrier_semaphore()
    for peer in peers:
        pl.semaphore_signal(barrier, inc=1, device_id=peer, device_id_type=MESH)
    pl.semaphore_wait(barrier, len(peers))


def _comm_call(body, name, operands, out_shape, scratch, collective_id):
    if collective_id is None:
        return pl.pallas_call(body, name=name, in_specs=[HBM_SPEC] * len(operands), out_specs=[HBM_SPEC] * len(out_shape),
                              out_shape=out_shape, scratch_shapes=scratch)(*operands)
    return pl.kernel(body, out_type=out_shape, mesh=plsc.ScalarSubcoreMesh(axis_name="sequencer", num_cores=1),
                     scratch_types=scratch, name=name,
                     compiler_params=pltpu.CompilerParams(collective_id=collective_id))(*operands)


def _all_gather(name, blocks, collective_id=None, after=(), pieces=None):
    n = len(blocks)
    na = len(after)
    pieces = pieces or [1] * n
    parts = []
    for i, (b, k) in enumerate(zip(blocks, pieces)):
        rows, rem = divmod(b.shape[0], k)
        assert rem == 0 and (k == 1 or rows % 16 == 0), (b.shape, k)
        parts += [(i, None, None)] if k == 1 else [(i, j * rows, rows) for j in range(k)]
    np_ = len(parts)

    def body(*refs):
        x_refs, out_refs = refs[:n], refs[n + na:2 * n + na]
        send_sems, recv_sems, local_sems = refs[2 * n + na:]
        x, y, c = lax.axis_index("x"), lax.axis_index("y"), lax.axis_index("c")
        me, sibling = (x, y, c), (x, y, 1 - c)
        chips = [(1 - x, y), (x, 1 - y), (1 - x, 1 - y)]
        if collective_id is not None:
            _handshake([sibling] + [(*chip, c) for chip in chips])

        def slot(p, px, py, pc):
            i, r0, rows = parts[p]
            whole = out_refs[i].at[4 * px + 2 * py + pc]
            return whole if r0 is None else whole.at[pl.ds(r0, rows)]

        def own(p):
            i, r0, rows = parts[p]
            return x_refs[i] if r0 is None else x_refs[i].at[pl.ds(r0, rows)]

        def copy(p, k, blk, to, src=None):
            return pltpu.make_async_remote_copy(
                src_ref=slot(p, *blk) if src is None else src, dst_ref=slot(p, *blk),
                send_sem=send_sems.at[7 * p + k], recv_sem=recv_sems.at[7 * p + k], device_id=to, device_id_type=MESH)

        mine = [pltpu.make_async_copy(own(p), slot(p, *me), local_sems.at[p]) for p in range(np_)]
        for cp in mine:
            cp.start()
        first = []
        for p in range(np_):
            first.append(copy(p, 0, me, sibling, src=own(p)))
            first += [copy(p, 1 + j, me, (*chip, c), src=own(p)) for j, chip in enumerate(chips)]
        for cp in first:
            cp.start()
        passed = []
        for p in range(np_):
            for j, chip in enumerate(chips):
                copy(p, 1 + j, (*chip, c), me).wait_recv()
                passed.append(copy(p, 4 + j, (*chip, c), sibling))
                passed[-1].start()
        for p in range(np_):
            copy(p, 0, sibling, me).wait_recv()
            for j, chip in enumerate(chips):
                copy(p, 4 + j, (*chip, 1 - c), me).wait_recv()
        for cp in first + passed:
            cp.wait_send()
        for cp in mine:
            cp.wait()

    return _comm_call(
        body, name, list(blocks) + list(after), [jax.ShapeDtypeStruct((N_DEV,) + b.shape, b.dtype) for b in blocks],
        [pltpu.SemaphoreType.DMA((7 * np_,)), pltpu.SemaphoreType.DMA((7 * np_,)), pltpu.SemaphoreType.DMA((np_,))],
        collective_id)


def _sibling_swap(name, by_owner, collective_id=None, after=()):
    n = len(by_owner)
    na = len(after)

    def body(*refs):
        x_refs, out_refs = refs[:n], refs[n + na:2 * n + na]
        send_sems, recv_sems = refs[2 * n + na:]
        x, y, c = lax.axis_index("x"), lax.axis_index("y"), lax.axis_index("c")
        if collective_id is not None:
            _handshake([(x, y, 1 - c)])
        copies = []
        for i in range(n):
            for k in range(4):
                copies.append(pltpu.make_async_remote_copy(
                    src_ref=x_refs[i].at[2 * k + 1 - c], dst_ref=out_refs[i].at[k],
                    send_sem=send_sems.at[4 * i + k], recv_sem=recv_sems.at[4 * i + k],
                    device_id=(x, y, 1 - c), device_id_type=MESH))
        for cp in copies:
            cp.start()
        for cp in copies:
            cp.wait()

    return _comm_call(
        body, name, list(by_owner) + list(after),
        [jax.ShapeDtypeStruct((4,) + b.shape[1:], b.dtype) for b in by_owner],
        [pltpu.SemaphoreType.DMA((4 * n,)), pltpu.SemaphoreType.DMA((4 * n,))], collective_id)


def _chip_exchange(name, sums, collective_id=None, after=()):
    n = len(sums)
    na = len(after)

    def body(*refs):
        x_refs, out_refs = refs[:n], refs[n + na:2 * n + na]
        send_sems, recv_sems, local_sems = refs[2 * n + na:]
        x, y, c = lax.axis_index("x"), lax.axis_index("y"), lax.axis_index("c")
        chips = [(1 - x, y), (x, 1 - y), (1 - x, 1 - y)]
        my_chip = 2 * x + y
        if collective_id is not None:
            _handshake([(cx, cy, c) for cx, cy in chips])
        mine = [pltpu.make_async_copy(x_refs[i].at[my_chip], out_refs[i].at[my_chip], local_sems.at[i])
                for i in range(n)]
        for cp in mine:
            cp.start()
        sends = []
        for i in range(n):
            for j, (cx, cy) in enumerate(chips):
                sends.append(pltpu.make_async_remote_copy(
                    src_ref=x_refs[i].at[2 * cx + cy], dst_ref=out_refs[i].at[my_chip],
                    send_sem=send_sems.at[3 * i + j], recv_sem=recv_sems.at[3 * i + j],
                    device_id=(cx, cy, c), device_id_type=MESH))
        for cp in sends:
            cp.start()
        for i in range(n):
            for j, (cx, cy) in enumerate(chips):
                pltpu.make_async_remote_copy(
                    src_ref=x_refs[i].at[my_chip], dst_ref=out_refs[i].at[2 * cx + cy],
                    send_sem=send_sems.at[3 * i + j], recv_sem=recv_sems.at[3 * i + j],
                    device_id=(cx, cy, c), device_id_type=MESH).wait_recv()
        for cp in sends:
            cp.wait_send()
        for cp in mine:
            cp.wait()

    return _comm_call(
        body, name, list(sums) + list(after), [jax.ShapeDtypeStruct(s.shape, s.dtype) for s in sums],
        [pltpu.SemaphoreType.DMA((3 * n,)), pltpu.SemaphoreType.DMA((3 * n,)), pltpu.SemaphoreType.DMA((n,))],
        collective_id)


def _cast_shards(name, shards):
    n = len(shards)
    steps = min(_row_steps(s.shape[0]) for s in shards)

    def body(*refs):
        for i in range(n):
            refs[n + i][...] = refs[i][...].astype(BF16)

    blocks = [pl.BlockSpec((s.shape[0] // steps, s.shape[1]), lambda i: (i, 0)) for s in shards]
    return pl.pallas_call(
        body, name=name, grid=(steps,), in_specs=blocks, out_specs=blocks,
        out_shape=[jax.ShapeDtypeStruct(s.shape, BF16) for s in shards],
        compiler_params=_params(("parallel",)),
    )(*shards)


BIG = ("w_in", "w_branch_a", "w_branch_b", "w_out", "w_ffn_gate", "w_ffn_up", "w_ffn_down")


def _local_step(x, target, gains, low, conv_w, wg8, reduce):
    g_mix, g_hg, g_ffn, g_fin = gains
    w_in = wg8["w_in"].reshape(N_IN, D_MODEL)
    wg = wg8["w_ffn_gate"].reshape(D_FF, D_MODEL)
    wu = wg8["w_ffn_up"].reshape(D_FF, D_MODEL)
    wa, wb = wg8["w_branch_a"], wg8["w_branch_b"]
    wo = wg8["w_out"].reshape(D_MODEL, D_MODEL)
    wd = wg8["w_ffn_down"].reshape(D_FF, D_MODEL)

    ht, hg, cv, gt, cvo, cvot, o, og, ogt, st = _fwd_in(x, g_mix, w_in, conv_w, low, g_hg)
    x1, mgt = _merge_fwd(og, cvo, gt, x, wa, wb, wo)
    h2t, gate, up, act, loss, d_gfin, dx2, dx2t = _ffn_fwd_loss(x1, g_ffn, wg, wu, wd, target, g_fin)

    dgate, dup, dx1, d_gffn = _ffn_bwd(dx2, x1, gate, up, g_ffn, wg, wu, wd)
    d_wg, d_wu, d_wd = _wgrad_ffn(h2t, dgate, dup, dx2t, act)
    by_owner_ffn = lambda a: a.reshape(N_DEV, D_FF // N_DEV, D_MODEL)
    ffn = dict(w_ffn_down=by_owner_ffn(d_wd), w_ffn_gate=by_owner_ffn(d_wg), w_ffn_up=by_owner_ffn(d_wu))
    dgt, dya, dyb, dog, dcv, d_conv = _merge_bwd(dx1, og, cvo, gt, cv, wa, wb, wo, conv_w)
    sums_ffn, got_ffn = reduce.begin(ffn, sum_after=[dya])
    late = ("w_ffn_gate", "w_ffn_up")
    parts_ffn, updated_ffn = reduce.finish(ffn, sums_ffn, defer=late)
    grad_a, grad_b, grad_o = _wgrad_out_branches(ogt, dya, cvot, dyb, mgt, dx1, after=sums_ffn[:1])
    out = dict(w_out=grad_o.reshape(N_DEV, D_MODEL // N_DEV, D_MODEL), w_branch_a=grad_a, w_branch_b=grad_b)
    dhg, d_low, d_ghg = _hg_bwd(dog, hg, o, st, low, g_hg, after=list(sums_ffn) + [out["w_out"]])
    sums_out, got_out = reduce.begin(out, after=[parts_ffn[0], dhg], sum_after=updated_ffn)
    parts_out, updated_out = reduce.finish(out, sums_out)
    dparts = [dhg, dcv, dgt]
    d_w_in_t, ridden = _wgrad_in(ht, dparts, after=sums_out[:1],
                                 riders=reduce.riders(late, dict(zip(ffn, parts_ffn))))
    reduce.record(late, ridden)
    w_in_grad = dict(w_in=d_w_in_t.reshape(N_DEV, N_IN // N_DEV, D_MODEL))
    sums_in, _ = reduce.begin(w_in_grad, after=parts_out[:1], sum_after=updated_out)
    parts_in, _ = reduce.finish(w_in_grad, sums_in)
    grad_x, d_gmix = _in_bwd(dparts, w_in, x, dx1, g_mix, after=list(parts_out[:1]) + list(sums_in))
    small = dict(norm_mix_g=d_gmix, norm_ffn_g=d_gffn, norm_final_g=d_gfin, lower_bounds=d_low, hg_norm_g=d_ghg,
                 conv_w=d_conv, loss=loss)
    return grad_x, small, parts_in


def kernel(x, norm_mix_g, w_in, lower_bounds, hg_norm_g, conv_w, w_branch_a, w_branch_b, w_out, norm_ffn_g, w_ffn_gate, w_ffn_up, w_ffn_down, norm_final_g, loss_target, m_norm_mix_g, m_w_in, m_lower_bounds, m_hg_norm_g, m_conv_w, m_w_branch_a, m_w_branch_b, m_w_out, m_norm_ffn_g, m_w_ffn_gate, m_w_ffn_up, m_w_ffn_down, m_norm_final_g, v_norm_mix_g, v_w_in, v_lower_bounds, v_hg_norm_g, v_conv_w, v_w_branch_a, v_w_branch_b, v_w_out, v_norm_ffn_g, v_w_ffn_gate, v_w_ffn_up, v_w_ffn_down, v_norm_final_g):
    cx, cy, cc = lax.axis_index("x"), lax.axis_index("y"), lax.axis_index("c")
    my_dev = 4 * cx + 2 * cy + cc

    def tr(a):
        return a[0].T

    big = dict(w_in=tr(w_in), w_branch_a=w_branch_a[0], w_branch_b=w_branch_b[0], w_out=w_out[0],
               w_ffn_gate=tr(w_ffn_gate), w_ffn_up=tr(w_ffn_up), w_ffn_down=w_ffn_down[0])
    big_m = dict(w_in=tr(m_w_in), w_branch_a=m_w_branch_a[0], w_branch_b=m_w_branch_b[0], w_out=m_w_out[0],
                 w_ffn_gate=tr(m_w_ffn_gate), w_ffn_up=tr(m_w_ffn_up), w_ffn_down=m_w_ffn_down[0])
    big_v = dict(w_in=tr(v_w_in), w_branch_a=v_w_branch_a[0], w_branch_b=v_w_branch_b[0], w_out=v_w_out[0],
                 w_ffn_gate=tr(v_w_ffn_gate), w_ffn_up=tr(v_w_ffn_up), w_ffn_down=v_w_ffn_down[0])
    transposed = ("w_in", "w_ffn_gate", "w_ffn_up")

    ids = iter(range(1, 16))
    shards = dict(zip(BIG[:1], _cast_shards("cast_w_in", [big["w_in"]])))
    first = _all_gather("gather_w_in", [shards["w_in"], conv_w.transpose(1, 0, 2)], collective_id=next(ids),
                        pieces=[4, 1])
    shards.update(zip(BIG[1:], _cast_shards("cast_shards", [big[n] for n in BIG[1:]])))
    mid = _all_gather("gather_mid", [shards[n] for n in BIG[1:4]], collective_id=next(ids))
    ffn = _all_gather("gather_ffn", [shards[n] for n in BIG[4:]], collective_id=next(ids), pieces=[2, 2, 2])
    wg8 = dict(zip(BIG, [first[0]] + list(mid) + list(ffn)))
    conv_full = first[1].transpose(1, 2, 0, 3).reshape(3, CONV_WIDTH)

    core = cc.reshape(1).astype(jnp.int32)
    outs = {}

    class Reduce:
        @staticmethod
        def begin(grads, after=(), sum_after=()):
            names = list(grads)
            by_owner = [grads[n] for n in names]
            got = _sibling_swap("sibling_swap_" + names[0], by_owner, collective_id=next(ids), after=after)
            sums = _pair_sum("pair_sum_" + names[0], by_owner, got, core, after=sum_after)
            return sums, got

        @staticmethod
        def finish(grads, chip_sums, after=(), defer=()):
            names = list(grads)
            parts = _chip_exchange("chip_exchange_" + names[0], chip_sums, collective_id=next(ids), after=after)
            now = [n for n in names if n not in defer]
            updated = _adamw_sum("adamw_" + now[0], [big[n] for n in now],
                                 [p for n, p in zip(names, parts) if n in now],
                                 [big_m[n] for n in now], [big_v[n] for n in now])
            outs.update(zip(now, updated))
            return parts, [outs[n][1] for n in now]

        @staticmethod
        def riders(names, parts):
            return [(big[n], parts[n], big_m[n], big_v[n]) for n in names]

        @staticmethod
        def record(names, updated):
            outs.update(zip(names, updated))

    gains = (norm_mix_g, hg_norm_g, norm_ffn_g, norm_final_g.reshape(1, D_MODEL))
    grad_x, small, last = _local_step(x[0], loss_target[0], gains, lower_bounds, conv_full, wg8, Reduce)

    small_all = _all_gather("gather_small", [_small_pack(small)], collective_id=next(ids), after=last[:1])

    def small_state(a):
        return [a[0], a[1], a[2].reshape(1, D_MODEL), a[3], a[4], a[5].transpose(1, 0, 2)]

    upd = _small_update(
        small_all[0], my_dev.reshape(1).astype(jnp.int32),
        small_state((norm_mix_g, norm_ffn_g, norm_final_g, lower_bounds, hg_norm_g, conv_w)),
        small_state((m_norm_mix_g, m_norm_ffn_g, m_norm_final_g, m_lower_bounds, m_hg_norm_g, m_conv_w)),
        small_state((v_norm_mix_g, v_norm_ffn_g, v_norm_final_g, v_lower_bounds, v_hg_norm_g, v_conv_w)))
    loss = upd[0][0, 0]
    for p, (name, _, _, _) in enumerate(_SMALL_PARAMS):
        outs[name] = upd[1 + 4 * p:5 + 4 * p]
    outs["norm_final_g"] = [a.reshape(D_MODEL) for a in outs["norm_final_g"]]
    outs["conv_w"] = [a.transpose(1, 0, 2) for a in outs["conv_w"]]

    order = ["norm_mix_g", "w_in", "lower_bounds", "hg_norm_g", "conv_w", "w_branch_a", "w_branch_b", "w_out",
             "norm_ffn_g", "w_ffn_gate", "w_ffn_up", "w_ffn_down", "norm_final_g"]
    result = [loss, grad_x[None]]
    for k in range(4):
        for n in order:
            if n in BIG:
                result.append((outs[n][k].T if n in transposed else outs[n][k])[None])
            else:
                result.append(outs[n][k])
    return tuple(result)
```

```python
import jax
import jax.numpy as jnp
from jax import lax
from jax.experimental import pallas as pl
from jax.experimental.pallas import tpu as pltpu
from jax.experimental.pallas import tpu_sc as plsc

F32 = jnp.float32
BF16 = jnp.bfloat16
STASH = jnp.bfloat16

D_MODEL = 1024
HG_WIDTH = 512
HEAD_DIM = 128
N_HEADS = 4
HEADS_PER_STEP = 4
HEAD_GROUPS = N_HEADS // HEADS_PER_STEP
CONV_WIDTH = 512
CONV_K = 3
D_FF = 2816
CHUNK = 32
EPS = 1e-6
Q_SCALE = HEAD_DIM ** -0.5
N_DEV = 8

ADAM_LR = 0.001
ADAM_B1 = 0.9
ADAM_B2 = 0.999
ADAM_EPS = 1e-08
ADAM_WD = 0.01
ADAM_STEP = 10

VMEM_LIMIT_V7X = 56 * 1024 * 1024
VMEM_LIMIT_LARGE_V7X = 62 * 1024 * 1024

SMALL_ROWS = 16


def _params(sem, vmem=VMEM_LIMIT_V7X):
    return pltpu.CompilerParams(dimension_semantics=sem, vmem_limit_bytes=vmem)


def _mm(a, b):
    return jnp.dot(a.astype(BF16), b.astype(BF16), preferred_element_type=F32)


def _mm_nt(a, b):
    return lax.dot_general(a.astype(BF16), b.astype(BF16), (((1,), (1,)), ((), ())), preferred_element_type=F32)


def _mm_tn(a, b):
    return lax.dot_general(a.astype(BF16), b.astype(BF16), (((0,), (0,)), ((), ())), preferred_element_type=F32)


def _sigmoid(x):
    return 0.5 * jnp.tanh(0.5 * x) + 0.5


def _resident(shape):
    nd = len(shape)
    return pl.BlockSpec(shape, lambda *_: (0,) * nd, pipeline_mode=pl.Buffered(1))


def _full(shape):
    nd = len(shape)
    return pl.BlockSpec(shape, lambda *_: (0,) * nd)


def _shard_cols(w_ref):
    return jnp.concatenate([w_ref[s] for s in range(N_DEV)], axis=1)


N_HG = 4 * HG_WIDTH
N_CV = 3 * CONV_WIDTH
N_GT = 2 * D_MODEL
N_IN = N_HG + N_CV + N_GT


def _col(tm, n):
    return pl.BlockSpec((n, tm), lambda i: (0, i))


HALO = 8


def _fwd_in(x, g, w_in_t, conv_w, low, gn):
    T = x.shape[0]
    tm = min(512, T)
    nc = tm // CHUNK

    def body(x_ref, g_ref, w_ref, cw_ref, low_ref, gn_ref, ht_ref, hg_ref, cv_ref, gt_ref, cvo_ref, cvot_ref,
             o_ref, og_ref, ogt_ref, st_ref, tail_scr, s_scr):
        @pl.when(pl.program_id(0) == 0)
        def _():
            tail_scr[...] = jnp.zeros_like(tail_scr)
            s_scr[...] = jnp.zeros_like(s_scr)

        xv = x_ref[...]
        r = lax.rsqrt(jnp.mean(xv * xv, axis=-1, keepdims=True) + EPS)
        hf = xv * r * g_ref[...]
        h = hf.astype(BF16)
        ht_ref[...] = hf.T.astype(BF16)
        hg_ref[...] = _mm_nt(h, w_ref[:N_HG, :])
        cv = _mm_nt(h, w_ref[N_HG:N_HG + N_CV, :])
        cv_ref[...] = cv.astype(STASH)
        gt_ref[...] = _mm_nt(h, w_ref[N_HG + N_CV:, :]).astype(STASH)

        u = cv[:, :CONV_WIDTH] * cv[:, 2 * CONV_WIDTH:]
        row = lax.broadcasted_iota(jnp.int32, u.shape, 0)
        prev1 = tail_scr[HALO - 1:HALO, :]
        prev2 = tail_scr[HALO - 2:HALO - 1, :]
        u1 = jnp.where(row >= 1, pltpu.roll(u, 1, 0), prev1)
        u2 = jnp.where(row >= 2, pltpu.roll(u, 2, 0), jnp.where(row == 1, prev1, prev2))
        y = cw_ref[0:1, :] * u2 + cw_ref[1:2, :] * u1 + cw_ref[2:3, :] * u
        out = cv[:, CONV_WIDTH:2 * CONV_WIDTH] * y
        cvo_ref[...] = out.astype(BF16)
        cvot_ref[...] = out.T.astype(BF16)
        tail_scr[...] = u[tm - HALO:, :]

        _hg_fwd_tile(hg_ref, low_ref, gn_ref, o_ref, og_ref, ogt_ref, st_ref, s_scr, tm)

    row = lambda n: pl.BlockSpec((tm, n), lambda i: (i, 0))
    return pl.pallas_call(
        body, name="fwd_in", grid=(T // tm,),
        in_specs=[row(D_MODEL), _full((1, D_MODEL)), _resident(w_in_t.shape), _full((CONV_K, CONV_WIDTH)),
                  _full((2, HG_WIDTH)), _full((1, HEAD_DIM))],
        out_specs=[_col(tm, D_MODEL), row(N_HG), row(N_CV), row(N_GT), row(CONV_WIDTH), _col(tm, CONV_WIDTH),
                   row(HG_WIDTH), row(HG_WIDTH), _col(tm, HG_WIDTH),
                   pl.BlockSpec((N_HEADS, nc, HEAD_DIM, HEAD_DIM), lambda i: (0, i, 0, 0))],
        out_shape=[jax.ShapeDtypeStruct((D_MODEL, T), BF16), jax.ShapeDtypeStruct((T, N_HG), F32),
                   jax.ShapeDtypeStruct((T, N_CV), STASH), jax.ShapeDtypeStruct((T, N_GT), STASH),
                   jax.ShapeDtypeStruct((T, CONV_WIDTH), BF16), jax.ShapeDtypeStruct((CONV_WIDTH, T), BF16),
                   jax.ShapeDtypeStruct((T, HG_WIDTH), F32), jax.ShapeDtypeStruct((T, HG_WIDTH), BF16),
                   jax.ShapeDtypeStruct((HG_WIDTH, T), BF16),
                   jax.ShapeDtypeStruct((N_HEADS, T // CHUNK, HEAD_DIM, HEAD_DIM), F32)],
        scratch_shapes=[pltpu.VMEM((HALO, CONV_WIDTH), F32), pltpu.VMEM((N_HEADS, HEAD_DIM, HEAD_DIM), F32)],
        compiler_params=_params(("arbitrary",), vmem=VMEM_LIMIT_LARGE_V7X),
    )(x, g, w_in_t, conv_w, low, gn)


def _chunk_pos(shape):
    return lax.broadcasted_iota(jnp.int32, shape, 0) & (CHUNK - 1)


def _chunk_cumsum(x, pos):
    s = 1
    while s < CHUNK:
        x = x + jnp.where(pos >= s, pltpu.roll(x, s, 0), 0.0)
        s *= 2
    return x


def _chunk_rev_cumsum(x, pos):
    n = x.shape[0]
    s = 1
    while s < CHUNK:
        x = x + jnp.where(pos + s < CHUNK, pltpu.roll(x, n - s, 0), 0.0)
        s *= 2
    return x


def _lower_bound(low_ref):
    l0 = low_ref[0:1, :]
    l1 = low_ref[1:2, :]
    m = jnp.maximum(l0, l1)
    e0 = jnp.exp(l0 - m)
    e1 = jnp.exp(l1 - m)
    return e0 / (e0 + e1), e1 / (e0 + e1)


def _hg_gates(qr, fr, lb, pos, tb):
    sq = _sigmoid(qr)
    q = qr * sq * Q_SCALE
    sg = _sigmoid(fr)
    f = lb + (1.0 - lb) * sg
    k = 1.0 - f
    b = _chunk_cumsum(jnp.log(f), pos)
    b3 = b.reshape(tb // CHUNK, CHUNK, HEAD_DIM)
    anc = b3[:, CHUNK // 2 - 1:CHUNK // 2, :]
    last = b3[:, CHUNK - 1:CHUNK, :]
    d3 = b3 - anc
    e_qa3 = jnp.exp(d3)
    e_ka3 = jnp.exp(-d3)
    e_b3 = e_qa3 * jnp.exp(anc)
    e_ko3 = e_ka3 * jnp.exp(last - anc)
    dec = jnp.exp(last)
    flat = lambda a: a.reshape(tb, HEAD_DIM)
    return sq, q, sg, f, k, flat(e_qa3), flat(e_ka3), flat(e_b3), flat(e_ko3), dec


def _intra_mask(sb):
    r = lax.broadcasted_iota(jnp.int32, (sb, sb), 0)
    c = lax.broadcasted_iota(jnp.int32, (sb, sb), 1)
    return ((r // CHUNK) == (c // CHUNK)) & (c <= r)


def _hg_fwd_tile(hg_ref, low_ref, gn_ref, o_ref, og_ref, ogt_ref, st_ref, s_scr, tb):
    sb = min(256, tb)
    nc = tb // CHUNK
    q_ref, f_ref, i_ref, g_ref = (hg_ref.at[:, p * HG_WIDTH:(p + 1) * HG_WIDTH] for p in range(4))
    pos = _chunk_pos((tb, HEAD_DIM))
    mask = _intra_mask(sb)
    lanes = [slice(hh * HEAD_DIM, (hh + 1) * HEAD_DIM) for hh in range(N_HEADS)]
    qi, ko, vb, dec, st = [], [], [], [], []
    for hh, ln in enumerate(lanes):
        lb, _ = _lower_bound(low_ref.at[:, ln])
        _, q, _, _, k, e_qa, e_ka, e_b, e_ko, dec_h = _hg_gates(q_ref[:, ln], f_ref[:, ln], lb, pos, tb)
        qh = (q * e_qa).astype(BF16)
        kh = (k * e_ka).astype(BF16)
        qi.append((q * e_b).astype(BF16))
        ko.append((k * e_ko).astype(BF16))
        vb.append(i_ref[:, ln].astype(BF16))
        dec.append(dec_h)
        st.append(s_scr[hh])
        for s in range(tb // sb):
            sl = slice(s * sb, (s + 1) * sb)
            p = jnp.where(mask, _mm_nt(qh[sl], kh[sl]), 0.0)
            o_ref[sl, ln] = _mm(p, vb[hh][sl])
    for c in range(nc):
        sl = slice(c * CHUNK, (c + 1) * CHUNK)
        for hh, ln in enumerate(lanes):
            st_ref[hh, c] = st[hh]
            o_ref[sl, ln] = o_ref[sl, ln] + _mm_nt(qi[hh][sl], st[hh])
            st[hh] = dec[hh][c] * st[hh] + _mm_tn(vb[hh][sl], ko[hh][sl])
    for hh, ln in enumerate(lanes):
        s_scr[hh] = st[hh]
        o = o_ref[:, ln]
        r = lax.rsqrt(jnp.mean(o * o, axis=-1, keepdims=True) + EPS)
        gr = g_ref[:, ln]
        og = (o * r * gn_ref[...]) * (gr * _sigmoid(gr))
        og_ref[:, ln] = og.astype(BF16)
        ogt_ref[ln, :] = og.T.astype(BF16)


def _merge_fwd(og, cvo, gt, x, wa, wb, wo):
    T = x.shape[0]
    tm = min(1024, T)

    def body(og_ref, cvo_ref, gt_ref, x_ref, wa_ref, wb_ref, wo_ref, x1_ref, mgt_ref, ya_ref, yb_ref):
        ya = jnp.dot(og_ref[...], _shard_cols(wa_ref), preferred_element_type=F32)
        yb = jnp.dot(cvo_ref[...], _shard_cols(wb_ref), preferred_element_type=F32)
        ya_ref[...] = ya.astype(STASH)
        yb_ref[...] = yb.astype(STASH)
        m = (_sigmoid(gt_ref[:, :D_MODEL].astype(F32)) * ya
             + _sigmoid(gt_ref[:, D_MODEL:].astype(F32)) * yb)
        mgt_ref[...] = m.T.astype(BF16)
        x1_ref[...] = x_ref[...] + jnp.dot(m.astype(BF16), wo_ref[...], preferred_element_type=F32)

    row = lambda n: pl.BlockSpec((tm, n), lambda i: (i, 0))
    return pl.pallas_call(
        body, name="merge_fwd", grid=(T // tm,),
        in_specs=[row(HG_WIDTH), row(CONV_WIDTH), row(2 * D_MODEL), row(D_MODEL),
                  _resident(wa.shape), _resident(wb.shape), _resident(wo.shape)],
        out_specs=[row(D_MODEL), _col(tm, D_MODEL), row(D_MODEL), row(D_MODEL)],
        out_shape=[jax.ShapeDtypeStruct((T, D_MODEL), F32), jax.ShapeDtypeStruct((D_MODEL, T), BF16),
                   jax.ShapeDtypeStruct((T, D_MODEL), STASH), jax.ShapeDtypeStruct((T, D_MODEL), STASH)],
        compiler_params=_params(("parallel",)),
    )(og, cvo, gt, x, wa, wb, wo)


def _ffn_fwd_loss(x1, g, wg, wu, wd, target, g_fin):
    T = x1.shape[0]
    tm = min(512, T)

    def body(x_ref, g_ref, wg_ref, wu_ref, wd_ref, t_ref, gf_ref,
             ht_ref, gate_ref, up_ref, act_ref, loss_ref, dgf_ref, dx2_ref, dx2t_ref):
        @pl.when(pl.program_id(0) == 0)
        def _():
            loss_ref[...] = jnp.zeros_like(loss_ref)
            dgf_ref[...] = jnp.zeros_like(dgf_ref)

        xv = x_ref[...]
        r = lax.rsqrt(jnp.mean(xv * xv, axis=-1, keepdims=True) + EPS)
        hf = xv * r * g_ref[...]
        h = hf.astype(BF16)
        ht_ref[...] = hf.T.astype(BF16)
        gate = _mm_nt(h, wg_ref[...])
        up = _mm_nt(h, wu_ref[...])
        gate_ref[...] = gate.astype(STASH)
        up_ref[...] = up.astype(STASH)
        act = (gate * _sigmoid(gate) * up).astype(BF16)
        act_ref[...] = act
        x2 = xv + jnp.dot(act, wd_ref[...], preferred_element_type=F32)

        gv = gf_ref[...]
        r2 = lax.rsqrt(jnp.mean(x2 * x2, axis=-1, keepdims=True) + EPS)
        xh = x2 * r2
        err = xh * gv - t_ref[...]
        loss_ref[...] += 0.5 * jnp.sum(jnp.mean(err * err, axis=-1, keepdims=True), axis=0, keepdims=True)
        dy = err * (1.0 / D_MODEL)
        dgf_ref[...] += jnp.sum(dy * xh, axis=0, keepdims=True)
        w = dy * gv
        dx2 = r2 * (w - xh * jnp.mean(w * xh, axis=-1, keepdims=True))
        dx2_ref[...] = dx2
        dx2t_ref[...] = dx2.T.astype(BF16)

    row = lambda n: pl.BlockSpec((tm, n), lambda i: (i, 0))
    return pl.pallas_call(
        body, name="ffn_fwd_loss", grid=(T // tm,),
        in_specs=[row(D_MODEL), _full((1, D_MODEL)), _resident(wg.shape), _resident(wu.shape), _resident(wd.shape),
                  row(D_MODEL), _full((1, D_MODEL))],
        out_specs=[_col(tm, D_MODEL), row(D_FF), row(D_FF), row(D_FF), _full((1, 128)), _full((1, D_MODEL)),
                   row(D_MODEL), _col(tm, D_MODEL)],
        out_shape=[jax.ShapeDtypeStruct((D_MODEL, T), BF16), jax.ShapeDtypeStruct((T, D_FF), STASH),
                   jax.ShapeDtypeStruct((T, D_FF), STASH), jax.ShapeDtypeStruct((T, D_FF), BF16),
                   jax.ShapeDtypeStruct((1, 128), F32), jax.ShapeDtypeStruct((1, D_MODEL), F32),
                   jax.ShapeDtypeStruct((T, D_MODEL), F32), jax.ShapeDtypeStruct((D_MODEL, T), BF16)],
        compiler_params=_params(("arbitrary",), vmem=VMEM_LIMIT_LARGE_V7X),
    )(x1, g, wg, wu, wd, target, g_fin)


def _ffn_bwd(dx2, x1, gate, up, g, wg, wu, wd):
    T = x1.shape[0]
    tm = min(512, T)

    def body(dx2_ref, x_ref, gate_ref, up_ref, g_ref, wg_ref, wu_ref, wd_ref, dgate_ref, dup_ref, dx1_ref, dgn_ref):
        @pl.when(pl.program_id(0) == 0)
        def _():
            dgn_ref[...] = jnp.zeros_like(dgn_ref)

        dx2 = dx2_ref[...]
        dact = _mm_nt(dx2, wd_ref[...])
        gate = gate_ref[...].astype(F32)
        s = _sigmoid(gate)
        dgate = (dact * up_ref[...].astype(F32) * (s * (1.0 + gate * (1.0 - s)))).astype(BF16)
        dup = (dact * (gate * s)).astype(BF16)
        dgate_ref[...] = dgate
        dup_ref[...] = dup
        dh = _mm(dgate, wg_ref[...]) + _mm(dup, wu_ref[...])
        xv = x_ref[...]
        r = lax.rsqrt(jnp.mean(xv * xv, axis=-1, keepdims=True) + EPS)
        xh = xv * r
        dgn_ref[...] += jnp.sum(dh * xh, axis=0, keepdims=True)
        w = dh * g_ref[...]
        dx1_ref[...] = dx2 + r * (w - xh * jnp.mean(w * xh, axis=-1, keepdims=True))

    row = lambda n: pl.BlockSpec((tm, n), lambda i: (i, 0))
    return pl.pallas_call(
        body, name="ffn_bwd", grid=(T // tm,),
        in_specs=[row(D_MODEL), row(D_MODEL), row(D_FF), row(D_FF), _full((1, D_MODEL)),
                  _resident(wg.shape), _resident(wu.shape), _resident(wd.shape)],
        out_specs=[row(D_FF), row(D_FF), row(D_MODEL), _full((1, D_MODEL))],
        out_shape=[jax.ShapeDtypeStruct((T, D_FF), BF16), jax.ShapeDtypeStruct((T, D_FF), BF16),
                   jax.ShapeDtypeStruct((T, D_MODEL), F32), jax.ShapeDtypeStruct((1, D_MODEL), F32)],
        compiler_params=_params(("arbitrary",), vmem=VMEM_LIMIT_LARGE_V7X),
    )(dx2, x1, gate, up, g, wg, wu, wd)


def _merge_bwd(dx1, ya, yb, gt, cv, wa, wb, wo, conv_w):
    T = dx1.shape[0]
    tm = min(512, T)
    nt = T // tm

    def body(dx_ref, ya_ref, yb_ref, gt_ref, cv_ref, halo_ref, wa_ref, wb_ref, wo_ref, cw_ref,
             dgt_ref, dya_ref, dyb_ref, dog_ref, dcv_ref, dcw_ref, prev_u, next_dy):
        step = pl.program_id(0)

        @pl.when(step == 0)
        def _():
            next_dy[...] = jnp.zeros_like(next_dy)
            dcw_ref[...] = jnp.zeros_like(dcw_ref)

        dm = _mm_nt(dx_ref[...], wo_ref[...])
        wa = _shard_cols(wa_ref)
        wb = _shard_cols(wb_ref)
        ya = ya_ref[...].astype(F32)
        yb = yb_ref[...].astype(F32)
        sa = _sigmoid(gt_ref[:, :D_MODEL].astype(F32))
        sb = _sigmoid(gt_ref[:, D_MODEL:].astype(F32))
        dgt_ref[:, :D_MODEL] = (dm * ya * (sa * (1.0 - sa))).astype(BF16)
        dgt_ref[:, D_MODEL:] = (dm * yb * (sb * (1.0 - sb))).astype(BF16)
        dya = (dm * sa).astype(BF16)
        dyb = (dm * sb).astype(BF16)
        dya_ref[...] = dya
        dyb_ref[...] = dyb
        dog_ref[...] = _mm_nt(dya, wa)
        dcvo = _mm_nt(dyb, wb)

        cvt = cv_ref[...].astype(F32)
        c, bg, xb = cvt[:, :CONV_WIDTH], cvt[:, CONV_WIDTH:2 * CONV_WIDTH], cvt[:, 2 * CONV_WIDTH:]
        halo = halo_ref[...].astype(F32)
        first_tile = step == nt - 1
        prev_u[...] = jnp.where(first_tile, 0.0, halo[:, :CONV_WIDTH] * halo[:, 2 * CONV_WIDTH:])
        u = c * xb
        row = lax.broadcasted_iota(jnp.int32, u.shape, 0)
        p1 = prev_u[HALO - 1:HALO, :]
        p2 = prev_u[HALO - 2:HALO - 1, :]
        u1 = jnp.where(row >= 1, pltpu.roll(u, 1, 0), p1)
        u2 = jnp.where(row >= 2, pltpu.roll(u, 2, 0), jnp.where(row == 1, p1, p2))
        w0, w1, w2 = cw_ref[0:1, :], cw_ref[1:2, :], cw_ref[2:3, :]
        y = w0 * u2 + w1 * u1 + w2 * u
        dcv_ref[:, CONV_WIDTH:2 * CONV_WIDTH] = (dcvo * y).astype(BF16)
        dy = dcvo * bg
        dcw_ref[0:1, :] += jnp.sum(dy * u2, axis=0, keepdims=True)
        dcw_ref[1:2, :] += jnp.sum(dy * u1, axis=0, keepdims=True)
        dcw_ref[2:3, :] += jnp.sum(dy * u, axis=0, keepdims=True)
        n1 = next_dy[0:1, :]
        n2 = next_dy[1:2, :]
        dy1 = jnp.where(row < tm - 1, pltpu.roll(dy, tm - 1, 0), n1)
        dy2 = jnp.where(row < tm - 2, pltpu.roll(dy, tm - 2, 0), jnp.where(row == tm - 2, n1, n2))
        du = w2 * dy + w1 * dy1 + w0 * dy2
        dcv_ref[:, :CONV_WIDTH] = (du * xb).astype(BF16)
        dcv_ref[:, 2 * CONV_WIDTH:] = (du * c).astype(BF16)
        next_dy[...] = dy[:HALO, :]

    rt = lambda i: nt - 1 - i
    row = lambda n: pl.BlockSpec((tm, n), lambda i: (rt(i), 0))
    halo = pl.BlockSpec((HALO, N_CV), lambda i: (jnp.maximum(rt(i) * (tm // HALO) - 1, 0), 0))
    return pl.pallas_call(
        body, name="merge_bwd", grid=(nt,),
        in_specs=[row(D_MODEL), row(D_MODEL), row(D_MODEL), row(2 * D_MODEL), row(N_CV), halo,
                  _resident(wa.shape), _resident(wb.shape), _resident(wo.shape), _full((CONV_K, CONV_WIDTH))],
        out_specs=[row(2 * D_MODEL), row(D_MODEL), row(D_MODEL), row(HG_WIDTH), row(N_CV),
                   _full((CONV_K, CONV_WIDTH))],
        out_shape=[jax.ShapeDtypeStruct((T, 2 * D_MODEL), BF16), jax.ShapeDtypeStruct((T, D_MODEL), BF16),
                   jax.ShapeDtypeStruct((T, D_MODEL), BF16), jax.ShapeDtypeStruct((T, HG_WIDTH), F32),
                   jax.ShapeDtypeStruct((T, N_CV), BF16), jax.ShapeDtypeStruct((CONV_K, CONV_WIDTH), F32)],
        scratch_shapes=[pltpu.VMEM((HALO, CONV_WIDTH), F32), pltpu.VMEM((HALO, CONV_WIDTH), F32)],
        compiler_params=_params(("arbitrary",)),
    )(dx1, ya, yb, gt, cv, cv, wa, wb, wo, conv_w)


def _drop_operands(body, first, count):
    def wrapped(*refs):
        return body(*refs[:first], *refs[first + count:])
    return wrapped


def _hg_bwd(dog, hg, o, st, low, gn, after=()):
    T = hg.shape[0]
    tb = min(512, T)
    sb = min(256, tb)
    nb = T // tb
    nc = tb // CHUNK
    wid = HEADS_PER_STEP * HEAD_DIM

    def body(q_ref, f_ref, i_ref, g_ref, low_ref, gn_ref, o_ref, dog_ref, st_ref,
             dhg_ref, dlow_ref, dgn_ref,
             ds_scr, dqi_scr, dko_scr, dv_scr, dd_scr, dqh_scr, dkh_scr):
        h = pl.program_id(0)
        t = pl.program_id(1)
        dq_ref, df_ref, di_ref, dg_ref = (dhg_ref.at[:, p * HG_WIDTH:(p + 1) * HG_WIDTH] for p in range(4))

        @pl.when(t == 0)
        def _():
            ds_scr[...] = jnp.zeros_like(ds_scr)
            dlow_ref[...] = jnp.zeros_like(dlow_ref)

        @pl.when((t == 0) & (h == 0))
        def _():
            dgn_ref[...] = jnp.zeros_like(dgn_ref)

        pos = _chunk_pos((tb, HEAD_DIM))
        mask = _intra_mask(sb)
        gnv = gn_ref[...]
        lanes = [slice(hh * HEAD_DIM, (hh + 1) * HEAD_DIM) for hh in range(HEADS_PER_STEP)]
        heads = []
        for hh, ln in enumerate(lanes):
            lb, lb1 = _lower_bound(low_ref.at[:, ln])
            qr = q_ref[:, ln]
            sq, q, sg, f, k, e_qa, e_ka, e_b, e_ko, dec = _hg_gates(qr, f_ref[:, ln], lb, pos, tb)

            gr = g_ref[:, ln]
            o = o_ref[:, ln]
            dog_v = dog_ref[:, ln]
            sgr = _sigmoid(gr)
            r = lax.rsqrt(jnp.mean(o * o, axis=-1, keepdims=True) + EPS)
            oh = o * r
            dg_ref[:, ln] = (dog_v * (oh * gnv) * (sgr * (1.0 + gr * (1.0 - sgr)))).astype(BF16)
            don = dog_v * (gr * sgr)
            dgn_ref[...] += jnp.sum(don * oh, axis=0, keepdims=True)
            w = don * gnv
            do = (r * (w - oh * jnp.mean(w * oh, axis=-1, keepdims=True))).astype(BF16)

            qh = (q * e_qa).astype(BF16)
            kh = (k * e_ka).astype(BF16)
            qi = (q * e_b).astype(BF16)
            ko = (k * e_ko).astype(BF16)
            vb = i_ref[:, ln].astype(BF16)

            for s in range(tb // sb):
                sl = slice(s * sb, (s + 1) * sb)
                p = jnp.where(mask, _mm_nt(qh[sl], kh[sl]), 0.0).astype(BF16)
                dp = jnp.where(mask, _mm_nt(do[sl], vb[sl]), 0.0).astype(BF16)
                dv_scr[sl, ln] = _mm_tn(p, do[sl])
                dqh_scr[sl, ln] = _mm(dp, kh[sl])
                dkh_scr[sl, ln] = _mm_tn(dp, qh[sl])
            heads.append(dict(lb=lb, lb1=lb1, qr=qr, sq=sq, q=q, sg=sg, f=f, k=k, e_qa=e_qa, e_ka=e_ka, e_b=e_b,
                              e_ko=e_ko, dec=dec, do=do, qi=qi, ko=ko, vb=vb, ds=ds_scr[hh]))

        for c in reversed(range(nc)):
            sl = slice(c * CHUNK, (c + 1) * CHUNK)
            for hh, ln in enumerate(lanes):
                hd = heads[hh]
                ds = hd["ds"]
                st_c = st_ref[hh, c]
                dqi_scr[sl, ln] = _mm(hd["do"][sl], st_c)
                dko_scr[sl, ln] = _mm(hd["vb"][sl], ds)
                dv_scr[sl, ln] = dv_scr[sl, ln] + _mm_nt(hd["ko"][sl], ds)
                dec_c = hd["dec"][c]
                dd_scr[sl, ln] = jnp.broadcast_to(dec_c * jnp.sum(ds * st_c, axis=0, keepdims=True),
                                                  (CHUNK, HEAD_DIM))
                hd["ds"] = dec_c * ds + _mm_tn(hd["do"][sl], hd["qi"][sl])

        for hh, ln in enumerate(lanes):
            hd = heads[hh]
            ds_scr[hh] = hd["ds"]
            q, k, lb = hd["q"], hd["k"], hd["lb"]
            dko_e = dko_scr[:, ln] * hd["e_ko"]
            dq = dqh_scr[:, ln] * hd["e_qa"] + dqi_scr[:, ln] * hd["e_b"]
            dk = dkh_scr[:, ln] * hd["e_ka"] + dko_e
            kd3 = (k * dko_e).reshape(nc, CHUNK, HEAD_DIM)
            last = jnp.broadcast_to(jnp.sum(kd3, axis=1, keepdims=True), kd3.shape).reshape(tb, HEAD_DIM)
            db = q * dq - k * dk + jnp.where(pos == CHUNK - 1, dd_scr[:, ln] + last, 0.0)
            dlg = _chunk_rev_cumsum(db, pos)
            dfv = dlg / hd["f"] - dk
            s_low = jnp.sum(dfv * (1.0 - hd["sg"]), axis=0, keepdims=True)
            dlow_ref[0:1, ln] += s_low * lb * (1.0 - lb)
            dlow_ref[1:2, ln] += -s_low * lb * hd["lb1"]
            df_ref[:, ln] = (dfv * (1.0 - lb) * hd["sg"] * (1.0 - hd["sg"])).astype(BF16)
            dq_ref[:, ln] = (dq * Q_SCALE * (hd["sq"] * (1.0 + hd["qr"] * (1.0 - hd["sq"])))).astype(BF16)
            di_ref[:, ln] = dv_scr[:, ln].astype(BF16)

    rt = lambda t: nb - 1 - t
    col = lambda p: pl.BlockSpec((tb, wid), lambda h, t: (rt(t), p * HEAD_GROUPS + h))
    hcol = pl.BlockSpec((tb, wid), lambda h, t: (rt(t), h))
    assert HEAD_GROUPS == 1
    tile = pltpu.VMEM((tb, wid), F32)
    return pl.pallas_call(
        _drop_operands(body, 9, len(after)), name="hg_bwd", grid=(HEAD_GROUPS, nb),
        in_specs=[col(0), col(1), col(2), col(3), pl.BlockSpec((2, wid), lambda h, t: (0, h)),
                  pl.BlockSpec((1, HEAD_DIM), lambda h, t: (0, 0)), hcol, hcol,
                  pl.BlockSpec((HEADS_PER_STEP, nc, HEAD_DIM, HEAD_DIM), lambda h, t: (h, rt(t), 0, 0))]
                 + [HBM_SPEC] * len(after),
        out_specs=[pl.BlockSpec((tb, N_HG), lambda h, t: (rt(t), 0)), pl.BlockSpec((2, wid), lambda h, t: (0, h)),
                   pl.BlockSpec((1, HEAD_DIM), lambda h, t: (0, 0))],
        out_shape=[jax.ShapeDtypeStruct((T, N_HG), BF16), jax.ShapeDtypeStruct((2, HG_WIDTH), F32),
                   jax.ShapeDtypeStruct((1, HEAD_DIM), F32)],
        scratch_shapes=[pltpu.VMEM((HEADS_PER_STEP, HEAD_DIM, HEAD_DIM), F32), tile, tile, tile, tile, tile, tile],
        compiler_params=_params(("arbitrary", "arbitrary")),
    )(hg, hg, hg, hg, low, gn, o, dog, st, *after)


def _in_bwd(dparts, w_in, x, dx1, g, after=()):
    T = x.shape[0]
    tm = min(512, T)
    widths = [p.shape[1] for p in dparts]
    offs = [sum(widths[:i]) for i in range(len(widths))]
    n = len(dparts)

    def body(*refs):
        d_refs = refs[:n]
        w_ref, x_ref, dx1_ref, g_ref, dx_ref, dgn_ref = refs[n:]

        @pl.when(pl.program_id(0) == 0)
        def _():
            dgn_ref[...] = jnp.zeros_like(dgn_ref)

        dh = None
        for d_ref, off, wd in zip(d_refs, offs, widths):
            part = _mm(d_ref[...], w_ref[off:off + wd, :])
            dh = part if dh is None else dh + part
        xv = x_ref[...]
        r = lax.rsqrt(jnp.mean(xv * xv, axis=-1, keepdims=True) + EPS)
        xh = xv * r
        dgn_ref[...] += jnp.sum(dh * xh, axis=0, keepdims=True)
        w = dh * g_ref[...]
        dx_ref[...] = dx1_ref[...] + r * (w - xh * jnp.mean(w * xh, axis=-1, keepdims=True))

    row = lambda m: pl.BlockSpec((tm, m), lambda i: (i, 0))
    return pl.pallas_call(
        _drop_operands(body, n + 4, len(after)), name="in_bwd", grid=(T // tm,),
        in_specs=[row(wd) for wd in widths] + [_resident(w_in.shape), row(D_MODEL), row(D_MODEL), _full((1, D_MODEL))]
                 + [HBM_SPEC] * len(after),
        out_specs=[row(D_MODEL), _full((1, D_MODEL))],
        out_shape=[jax.ShapeDtypeStruct((T, D_MODEL), F32), jax.ShapeDtypeStruct((1, D_MODEL), F32)],
        compiler_params=_params(("arbitrary",)),
    )(*dparts, w_in, x, dx1, g, *after)


def _wgrad_ffn(h2t, dgate, dup, dx2t, act, tn=256):
    M, T = h2t.shape
    N = dgate.shape[1]

    def body(h_ref, x_ref, dg_ref, du_ref, act_ref, og_ref, ou_ref, od_ref):
        h = h_ref[...]
        og_ref[...] = _mm(h, dg_ref[...]).T.astype(BF16)
        ou_ref[...] = _mm(h, du_ref[...]).T.astype(BF16)
        od_ref[...] = _mm(x_ref[...], act_ref[...]).T.astype(BF16)

    rhs = pl.BlockSpec((T, tn), lambda j: (0, j))
    out_spec = pl.BlockSpec((tn, M), lambda j: (j, 0))
    out = jax.ShapeDtypeStruct((N, M), BF16)
    return pl.pallas_call(
        body, name="wgrad_ffn", grid=(N // tn,),
        in_specs=[_resident((M, T)), _resident((M, T)), rhs, rhs, rhs], out_specs=[out_spec] * 3,
        out_shape=[out, out, out],
        compiler_params=_params(("parallel",)),
    )(h2t, dx2t, dgate, dup, act)


def _wgrad_out_branches(ogt, dya, cvot, dyb, mgt, dx1, after=()):
    M, T = ogt.shape
    N = dya.shape[1]
    c = N // N_DEV
    per = 2
    tn = per * c

    def body(at_ref, da_ref, bt_ref, db_ref, mt_ref, dx_ref, oa_ref, ob_ref, oo_ref):
        ga = _mm(at_ref[...], da_ref[...])
        gb = _mm(bt_ref[...], db_ref[...])
        for s in range(per):
            oa_ref[s] = ga[:, s * c:(s + 1) * c].astype(BF16)
            ob_ref[s] = gb[:, s * c:(s + 1) * c].astype(BF16)
        oo_ref[...] = _mm(mt_ref[...], dx_ref[...]).astype(BF16)

    rhs = pl.BlockSpec((T, tn), lambda j: (0, j))
    owners = pl.BlockSpec((per, M, c), lambda j: (j, 0, 0))
    out = jax.ShapeDtypeStruct((N_DEV, M, c), BF16)
    return pl.pallas_call(
        _drop_operands(body, 6, len(after)), name="wgrad_out_branches", grid=(N // tn,),
        in_specs=[_resident((M, T)), rhs, _resident((M, T)), rhs, _resident(mgt.shape), rhs] + [HBM_SPEC] * len(after),
        out_specs=[owners, owners, pl.BlockSpec((mgt.shape[0], tn), lambda j: (0, j))],
        out_shape=[out, out, jax.ShapeDtypeStruct((mgt.shape[0], dx1.shape[1]), BF16)],
        compiler_params=_params(("parallel",)),
    )(ogt, dya, cvot, dyb, mgt, dx1, *after)


def _wgrad_in(ht, dparts, after=(), riders=()):
    M, T = ht.shape
    tn = 512
    nblk = [p.shape[1] // tn for p in dparts]
    start = [sum(nblk[:i]) for i in range(len(nblk))]
    n = len(dparts)
    steps = sum(nblk)
    nr = len(riders)
    rows = [r[0].shape[0] // steps for r in riders]
    assert all(r[0].shape[0] == rr * steps and rr % 16 == 0 for r, rr in zip(riders, rows))

    def body(a_ref, *refs):
        d_refs = refs[:n]
        rider_in = refs[n:n + 4 * nr]
        o_ref = refs[n + 4 * nr]
        rider_out = refs[n + 4 * nr + 1:]
        j = pl.program_id(0)
        for d_ref, s, nb in zip(d_refs, start, nblk):
            @pl.when((j >= s) & (j < s + nb))
            def _():
                o_ref[...] = _mm(a_ref[...], d_ref[...]).T.astype(BF16)
        for i in range(nr):
            w_ref, p_ref, m_ref, v_ref = rider_in[4 * i:4 * i + 4]
            g = p_ref[0].astype(F32)
            for k in range(1, 4):
                g = g + p_ref[k].astype(F32)
            delta, m_new, v_new = _adamw_math(w_ref[...], g, m_ref[...], v_ref[...])
            for k, val in enumerate((g, delta, m_new, v_new)):
                rider_out[4 * i + k][...] = val

    def piece_spec(s, nb):
        return pl.BlockSpec((T, tn), lambda j: (0, jnp.clip(j - s, 0, nb - 1)))

    rider_specs, rider_out_specs, rider_out_shape, rider_args = [], [], [], []
    for (w, parts, m, v), rr in zip(riders, rows):
        blk = pl.BlockSpec((rr, w.shape[1]), lambda j: (j, 0))
        rider_specs += [blk, pl.BlockSpec((4, rr, w.shape[1]), lambda j: (0, j, 0)), blk, blk]
        rider_out_specs += [blk] * 4
        rider_out_shape += [jax.ShapeDtypeStruct(w.shape, F32)] * 4
        rider_args += [w, parts, m, v]
    outs = pl.pallas_call(
        _drop_operands(body, 1 + n + 4 * nr, len(after)), name="wgrad_in", grid=(steps,),
        in_specs=[_resident((M, T))] + [piece_spec(s, nb) for s, nb in zip(start, nblk)] + rider_specs
                 + [HBM_SPEC] * len(after),
        out_specs=[pl.BlockSpec((tn, M), lambda j: (j, 0))] + rider_out_specs,
        out_shape=[jax.ShapeDtypeStruct((steps * tn, M), BF16)] + rider_out_shape,
        compiler_params=_params(("parallel",)),
    )(ht, *dparts, *rider_args, *after)
    return outs[0], [outs[1 + 4 * i:5 + 4 * i] for i in range(nr)]


def _adamw_math(w, g, m, v):
    m = ADAM_B1 * m + (1.0 - ADAM_B1) * g
    v = ADAM_B2 * v + (1.0 - ADAM_B2) * (g * g)
    m_hat = m / (1.0 - ADAM_B1 ** ADAM_STEP)
    v_hat = v / (1.0 - ADAM_B2 ** ADAM_STEP)
    delta = -ADAM_LR * (m_hat / (jnp.sqrt(v_hat) + ADAM_EPS) + ADAM_WD * w)
    return delta, m, v


def _adamw_sum(name, ws, parts, ms, vs):
    n = len(ws)
    steps = min(_row_steps(w.shape[0]) for w in ws)
    rows = [w.shape[0] // steps for w in ws]

    def body(*refs):
        w_refs, p_refs, m_refs, v_refs = (refs[k * n:(k + 1) * n] for k in range(4))
        out_refs = refs[4 * n:]
        for i in range(n):
            g = p_refs[i][0].astype(F32)
            for k in range(1, 4):
                g = g + p_refs[i][k].astype(F32)
            delta, m_new, v_new = _adamw_math(w_refs[i][...], g, m_refs[i][...], v_refs[i][...])
            for k, val in enumerate((g, delta, m_new, v_new)):
                out_refs[4 * i + k][...] = val

    blk = [pl.BlockSpec((r, w.shape[1]), lambda s: (s, 0)) for r, w in zip(rows, ws)]
    pblk = [pl.BlockSpec((4, r, w.shape[1]), lambda s: (0, s, 0)) for r, w in zip(rows, ws)]
    out_specs, out_shape = [], []
    for b, w in zip(blk, ws):
        out_specs += [b] * 4
        out_shape += [jax.ShapeDtypeStruct(w.shape, F32)] * 4
    flat = pl.pallas_call(
        body, name=name, grid=(steps,),
        in_specs=blk + pblk + blk + blk, out_specs=out_specs, out_shape=out_shape,
        compiler_params=_params(("parallel",)),
    )(*ws, *parts, *ms, *vs)
    return [flat[4 * i:4 * i + 4] for i in range(n)]


_SMALL_SLOTS = (("norm_mix_g", 0, 1, 1024), ("norm_ffn_g", 1, 1, 1024), ("norm_final_g", 2, 1, 1024),
                ("lower_bounds", 3, 2, 512), ("hg_norm_g", 5, 1, 128), ("loss", 6, 1, 128), ("conv_w", 8, 3, 512))
_SMALL_PARAMS = tuple(s for s in _SMALL_SLOTS if s[0] != "loss")
CONV_SHARD = CONV_WIDTH // N_DEV


def _small_pack(small):
    def body(*refs):
        out = refs[-1]
        out[...] = jnp.zeros_like(out)
        for ref, (_, row, rows, lanes) in zip(refs[:-1], _SMALL_SLOTS):
            out[row:row + rows, 0:lanes] = ref[...]

    vmem = pl.BlockSpec(memory_space=pltpu.VMEM)
    return pl.pallas_call(
        body, name="small_pack", in_specs=[vmem] * len(_SMALL_SLOTS), out_specs=vmem,
        out_shape=jax.ShapeDtypeStruct((SMALL_ROWS, 1024), F32),
    )(*[small[name] for name, _, _, _ in _SMALL_SLOTS])


def _small_update(gathered, dev, w, m, v):
    n = len(_SMALL_PARAMS)

    def body(dev_ref, g_ref, *refs):
        w_refs, m_refs, v_refs = refs[:n], refs[n:2 * n], refs[2 * n:3 * n]
        loss_ref, out_refs, sum_scr = refs[3 * n], refs[3 * n + 1:-1], refs[-1]
        total = g_ref[0]
        for k in range(1, N_DEV):
            total = total + g_ref[k]
        sum_scr[...] = total
        loss_ref[...] = sum_scr[6:7, 0:128]
        for p, (name, row, rows, lanes) in enumerate(_SMALL_PARAMS):
            if name == "conv_w":
                for r in range(rows):
                    g = sum_scr[row + r:row + r + 1, 0:CONV_SHARD]
                    for s in range(1, N_DEV):
                        mine = sum_scr[row + r:row + r + 1, s * CONV_SHARD:(s + 1) * CONV_SHARD]
                        g = jnp.where(dev_ref[0] == s, mine, g)
                    delta, m_new, v_new = _adamw_math(w_refs[p][r], g, m_refs[p][r], v_refs[p][r])
                    out_refs[4 * p][r] = g
                    out_refs[4 * p + 1][r] = delta
                    out_refs[4 * p + 2][r] = m_new
                    out_refs[4 * p + 3][r] = v_new
                continue
            g = sum_scr[row:row + rows, 0:lanes]
            delta, m_new, v_new = _adamw_math(w_refs[p][...], g, m_refs[p][...], v_refs[p][...])
            out_refs[4 * p][...] = g
            out_refs[4 * p + 1][...] = delta
            out_refs[4 * p + 2][...] = m_new
            out_refs[4 * p + 3][...] = v_new

    vmem = pl.BlockSpec(memory_space=pltpu.VMEM)
    outs = [jax.ShapeDtypeStruct((1, 128), F32)]
    for a in w:
        outs += [jax.ShapeDtypeStruct(a.shape, F32)] * 4
    return pl.pallas_call(
        body, name="small_update",
        in_specs=[pl.BlockSpec(memory_space=pltpu.SMEM)] + [vmem] * (1 + 3 * n), out_specs=[vmem] * len(outs),
        out_shape=outs, scratch_shapes=[pltpu.VMEM((SMALL_ROWS, 1024), F32)],
    )(dev, gathered, *w, *m, *v)


def _row_steps(rows):
    for steps in (4, 2):
        if rows % (16 * steps) == 0:
            return steps
    return 1


def _pair_sum(name, by_owner, got, core, after=()):
    n = len(got)

    def body(core_ref, *refs):
        for a_ref, b_ref, o_ref in zip(refs[:n], refs[n:2 * n], refs[2 * n:]):
            o_ref[...] = (a_ref[...].astype(F32) + b_ref[...].astype(F32)).astype(BF16)

    def blk(g):
        return pl.BlockSpec((None,) + g.shape[1:], lambda k, core_ref: (k, 0, 0))

    def mine(g):
        return pl.BlockSpec((None,) + g.shape[1:], lambda k, core_ref: (2 * k + core_ref[0], 0, 0))

    return pl.pallas_call(
        _drop_operands(body, 1 + 2 * n, len(after)), name=name,
        grid_spec=pltpu.PrefetchScalarGridSpec(
            num_scalar_prefetch=1, grid=(4,),
            in_specs=[mine(g) for g in got] + [blk(g) for g in got] + [HBM_SPEC] * len(after),
            out_specs=[blk(g) for g in got]),
        out_shape=[jax.ShapeDtypeStruct(g.shape, BF16) for g in got],
        compiler_params=_params(("parallel",)),
    )(core, *by_owner, *got, *after)


MESH = pl.DeviceIdType.MESH
HBM_SPEC = pl.BlockSpec(memory_space=pl.ANY)


def _handshake(peers):
    barrier = pltpu.get_barrier_semaphore()
    for peer in peers:
        pl.semaphore_signal(barrier, inc=1, device_id=peer, device_id_type=MESH)
    pl.semaphore_wait(barrier, len(peers))


def _comm_call(body, name, operands, out_shape, scratch, collective_id):
    if collective_id is None:
        return pl.pallas_call(body, name=name, in_specs=[HBM_SPEC] * len(operands), out_specs=[HBM_SPEC] * len(out_shape),
                              out_shape=out_shape, scratch_shapes=scratch)(*operands)
    return pl.kernel(body, out_type=out_shape, mesh=plsc.ScalarSubcoreMesh(axis_name="sequencer", num_cores=1),
                     scratch_types=scratch, name=name,
                     compiler_params=pltpu.CompilerParams(collective_id=collective_id))(*operands)


def _all_gather(name, blocks, collective_id=None, after=(), pieces=None):
    n = len(blocks)
    na = len(after)
    pieces = pieces or [1] * n
    parts = []
    for i, (b, k) in enumerate(zip(blocks, pieces)):
        rows, rem = divmod(b.shape[0], k)
        assert rem == 0 and (k == 1 or rows % 16 == 0), (b.shape, k)
        parts += [(i, None, None)] if k == 1 else [(i, j * rows, rows) for j in range(k)]
    np_ = len(parts)

    def body(*refs):
        x_refs, out_refs = refs[:n], refs[n + na:2 * n + na]
        send_sems, recv_sems, local_sems = refs[2 * n + na:]
        x, y, c = lax.axis_index("x"), lax.axis_index("y"), lax.axis_index("c")
        me, sibling = (x, y, c), (x, y, 1 - c)
        chips = [(1 - x, y), (x, 1 - y), (1 - x, 1 - y)]
        if collective_id is not None:
            _handshake([sibling] + [(*chip, c) for chip in chips])

        def slot(p, px, py, pc):
            i, r0, rows = parts[p]
            whole = out_refs[i].at[4 * px + 2 * py + pc]
            return whole if r0 is None else whole.at[pl.ds(r0, rows)]

        def own(p):
            i, r0, rows = parts[p]
            return x_refs[i] if r0 is None else x_refs[i].at[pl.ds(r0, rows)]

        def copy(p, k, blk, to, src=None):
            return pltpu.make_async_remote_copy(
                src_ref=slot(p, *blk) if src is None else src, dst_ref=slot(p, *blk),
                send_sem=send_sems.at[7 * p + k], recv_sem=recv_sems.at[7 * p + k], device_id=to, device_id_type=MESH)

        mine = [pltpu.make_async_copy(own(p), slot(p, *me), local_sems.at[p]) for p in range(np_)]
        for cp in mine:
            cp.start()
        first = []
        for p in range(np_):
            first.append(copy(p, 0, me, sibling, src=own(p)))
            first += [copy(p, 1 + j, me, (*chip, c), src=own(p)) for j, chip in enumerate(chips)]
        for cp in first:
            cp.start()
        passed = []
        for p in range(np_):
            for j, chip in enumerate(chips):
                copy(p, 1 + j, (*chip, c), me).wait_recv()
                passed.append(copy(p, 4 + j, (*chip, c), sibling))
                passed[-1].start()
        for p in range(np_):
            copy(p, 0, sibling, me).wait_recv()
            for j, chip in enumerate(chips):
                copy(p, 4 + j, (*chip, 1 - c), me).wait_recv()
        for cp in first + passed:
            cp.wait_send()
        for cp in mine:
            cp.wait()

    return _comm_call(
        body, name, list(blocks) + list(after), [jax.ShapeDtypeStruct((N_DEV,) + b.shape, b.dtype) for b in blocks],
        [pltpu.SemaphoreType.DMA((7 * np_,)), pltpu.SemaphoreType.DMA((7 * np_,)), pltpu.SemaphoreType.DMA((np_,))],
        collective_id)


def _sibling_swap(name, by_owner, collective_id=None, after=()):
    n = len(by_owner)
    na = len(after)

    def body(*refs):
        x_refs, out_refs = refs[:n], refs[n + na:2 * n + na]
        send_sems, recv_sems = refs[2 * n + na:]
        x, y, c = lax.axis_index("x"), lax.axis_index("y"), lax.axis_index("c")
        if collective_id is not None:
            _handshake([(x, y, 1 - c)])
        copies = []
        for i in range(n):
            for k in range(4):
                copies.append(pltpu.make_async_remote_copy(
                    src_ref=x_refs[i].at[2 * k + 1 - c], dst_ref=out_refs[i].at[k],
                    send_sem=send_sems.at[4 * i + k], recv_sem=recv_sems.at[4 * i + k],
                    device_id=(x, y, 1 - c), device_id_type=MESH))
        for cp in copies:
            cp.start()
        for cp in copies:
            cp.wait()

    return _comm_call(
        body, name, list(by_owner) + list(after),
        [jax.ShapeDtypeStruct((4,) + b.shape[1:], b.dtype) for b in by_owner],
        [pltpu.SemaphoreType.DMA((4 * n,)), pltpu.SemaphoreType.DMA((4 * n,))], collective_id)


def _chip_exchange(name, sums, collective_id=None, after=()):
    n = len(sums)
    na = len(after)

    def body(*refs):
        x_refs, out_refs = refs[:n], refs[n + na:2 * n + na]
        send_sems, recv_sems, local_sems = refs[2 * n + na:]
        x, y, c = lax.axis_index("x"), lax.axis_index("y"), lax.axis_index("c")
        chips = [(1 - x, y), (x, 1 - y), (1 - x, 1 - y)]
        my_chip = 2 * x + y
        if collective_id is not None:
            _handshake([(cx, cy, c) for cx, cy in chips])
        mine = [pltpu.make_async_copy(x_refs[i].at[my_chip], out_refs[i].at[my_chip], local_sems.at[i])
                for i in range(n)]
        for cp in mine:
            cp.start()
        sends = []
        for i in range(n):
            for j, (cx, cy) in enumerate(chips):
                sends.append(pltpu.make_async_remote_copy(
                    src_ref=x_refs[i].at[2 * cx + cy], dst_ref=out_refs[i].at[my_chip],
                    send_sem=send_sems.at[3 * i + j], recv_sem=recv_sems.at[3 * i + j],
                    device_id=(cx, cy, c), device_id_type=MESH))
        for cp in sends:
            cp.start()
        for i in range(n):
            for j, (cx, cy) in enumerate(chips):
                pltpu.make_async_remote_copy(
                    src_ref=x_refs[i].at[my_chip], dst_ref=out_refs[i].at[2 * cx + cy],
                    send_sem=send_sems.at[3 * i + j], recv_sem=recv_sems.at[3 * i + j],
                    device_id=(cx, cy, c), device_id_type=MESH).wait_recv()
        for cp in sends:
            cp.wait_send()
        for cp in mine:
            cp.wait()

    return _comm_call(
        body, name, list(sums) + list(after), [jax.ShapeDtypeStruct(s.shape, s.dtype) for s in sums],
        [pltpu.SemaphoreType.DMA((3 * n,)), pltpu.SemaphoreType.DMA((3 * n,)), pltpu.SemaphoreType.DMA((n,))],
        collective_id)


def _cast_shards(name, shards):
    n = len(shards)
    steps = min(_row_steps(s.shape[0]) for s in shards)

    def body(*refs):
        for i in range(n):
            refs[n + i][...] = refs[i][...].astype(BF16)

    blocks = [pl.BlockSpec((s.shape[0] // steps, s.shape[1]), lambda i: (i, 0)) for s in shards]
    return pl.pallas_call(
        body, name=name, grid=(steps,), in_specs=blocks, out_specs=blocks,
        out_shape=[jax.ShapeDtypeStruct(s.shape, BF16) for s in shards],
        compiler_params=_params(("parallel",)),
    )(*shards)


BIG = ("w_in", "w_branch_a", "w_branch_b", "w_out", "w_ffn_gate", "w_ffn_up", "w_ffn_down")


def _local_step(x, target, gains, low, conv_w, wg8, reduce):
    g_mix, g_hg, g_ffn, g_fin = gains
    w_in = wg8["w_in"].reshape(N_IN, D_MODEL)
    wg = wg8["w_ffn_gate"].reshape(D_FF, D_MODEL)
    wu = wg8["w_ffn_up"].reshape(D_FF, D_MODEL)
    wa, wb = wg8["w_branch_a"], wg8["w_branch_b"]
    wo = wg8["w_out"].reshape(D_MODEL, D_MODEL)
    wd = wg8["w_ffn_down"].reshape(D_FF, D_MODEL)

    ht, hg, cv, gt, cvo, cvot, o, og, ogt, st = _fwd_in(x, g_mix, w_in, conv_w, low, g_hg)
    x1, mgt, ya, yb = _merge_fwd(og, cvo, gt, x, wa, wb, wo)
    h2t, gate, up, act, loss, d_gfin, dx2, dx2t = _ffn_fwd_loss(x1, g_ffn, wg, wu, wd, target, g_fin)

    dgate, dup, dx1, d_gffn = _ffn_bwd(dx2, x1, gate, up, g_ffn, wg, wu, wd)
    d_wg, d_wu, d_wd = _wgrad_ffn(h2t, dgate, dup, dx2t, act)
    by_owner_ffn = lambda a: a.reshape(N_DEV, D_FF // N_DEV, D_MODEL)
    ffn = dict(w_ffn_down=by_owner_ffn(d_wd), w_ffn_gate=by_owner_ffn(d_wg), w_ffn_up=by_owner_ffn(d_wu))
    dgt, dya, dyb, dog, dcv, d_conv = _merge_bwd(dx1, ya, yb, gt, cv, wa, wb, wo, conv_w)
    sums_ffn, got_ffn = reduce.begin(ffn, sum_after=[dya])
    late = ("w_ffn_gate", "w_ffn_up")
    parts_ffn, updated_ffn = reduce.finish(ffn, sums_ffn, defer=late)
    grad_a, grad_b, grad_o = _wgrad_out_branches(ogt, dya, cvot, dyb, mgt, dx1, after=sums_ffn[:1])
    out = dict(w_out=grad_o.reshape(N_DEV, D_MODEL // N_DEV, D_MODEL), w_branch_a=grad_a, w_branch_b=grad_b)
    dhg, d_low, d_ghg = _hg_bwd(dog, hg, o, st, low, g_hg, after=list(sums_ffn) + [out["w_out"]])
    sums_out, got_out = reduce.begin(out, after=[parts_ffn[0], dhg], sum_after=updated_ffn)
    parts_out, updated_out = reduce.finish(out, sums_out)
    dparts = [dhg, dcv, dgt]
    d_w_in_t, ridden = _wgrad_in(ht, dparts, after=sums_out[:1],
                                 riders=reduce.riders(late, dict(zip(ffn, parts_ffn))))
    reduce.record(late, ridden)
    w_in_grad = dict(w_in=d_w_in_t.reshape(N_DEV, N_IN // N_DEV, D_MODEL))
    sums_in, _ = reduce.begin(w_in_grad, after=parts_out[:1], sum_after=updated_out)
    parts_in, _ = reduce.finish(w_in_grad, sums_in)
    grad_x, d_gmix = _in_bwd(dparts, w_in, x, dx1, g_mix, after=list(parts_out[:1]) + list(sums_in))
    small = dict(norm_mix_g=d_gmix, norm_ffn_g=d_gffn, norm_final_g=d_gfin, lower_bounds=d_low, hg_norm_g=d_ghg,
                 conv_w=d_conv, loss=loss)
    return grad_x, small, parts_in


def kernel(x, norm_mix_g, w_in, lower_bounds, hg_norm_g, conv_w, w_branch_a, w_branch_b, w_out, norm_ffn_g, w_ffn_gate, w_ffn_up, w_ffn_down, norm_final_g, loss_target, m_norm_mix_g, m_w_in, m_lower_bounds, m_hg_norm_g, m_conv_w, m_w_branch_a, m_w_branch_b, m_w_out, m_norm_ffn_g, m_w_ffn_gate, m_w_ffn_up, m_w_ffn_down, m_norm_final_g, v_norm_mix_g, v_w_in, v_lower_bounds, v_hg_norm_g, v_conv_w, v_w_branch_a, v_w_branch_b, v_w_out, v_norm_ffn_g, v_w_ffn_gate, v_w_ffn_up, v_w_ffn_down, v_norm_final_g):
    cx, cy, cc = lax.axis_index("x"), lax.axis_index("y"), lax.axis_index("c")
    my_dev = 4 * cx + 2 * cy + cc

    def tr(a):
        return a[0].T

    big = dict(w_in=tr(w_in), w_branch_a=w_branch_a[0], w_branch_b=w_branch_b[0], w_out=w_out[0],
               w_ffn_gate=tr(w_ffn_gate), w_ffn_up=tr(w_ffn_up), w_ffn_down=w_ffn_down[0])
    big_m = dict(w_in=tr(m_w_in), w_branch_a=m_w_branch_a[0], w_branch_b=m_w_branch_b[0], w_out=m_w_out[0],
                 w_ffn_gate=tr(m_w_ffn_gate), w_ffn_up=tr(m_w_ffn_up), w_ffn_down=m_w_ffn_down[0])
    big_v = dict(w_in=tr(v_w_in), w_branch_a=v_w_branch_a[0], w_branch_b=v_w_branch_b[0], w_out=v_w_out[0],
                 w_ffn_gate=tr(v_w_ffn_gate), w_ffn_up=tr(v_w_ffn_up), w_ffn_down=v_w_ffn_down[0])
    transposed = ("w_in", "w_ffn_gate", "w_ffn_up")

    ids = iter(range(1, 16))
    shards = dict(zip(BIG[:1], _cast_shards("cast_w_in", [big["w_in"]])))
    first = _all_gather("gather_w_in", [shards["w_in"], conv_w.transpose(1, 0, 2)], collective_id=next(ids),
                        pieces=[4, 1])
    shards.update(zip(BIG[1:], _cast_shards("cast_shards", [big[n] for n in BIG[1:]])))
    mid = _all_gather("gather_mid", [shards[n] for n in BIG[1:4]], collective_id=next(ids))
    ffn = _all_gather("gather_ffn", [shards[n] for n in BIG[4:]], collective_id=next(ids), pieces=[2, 2, 2])
    wg8 = dict(zip(BIG, [first[0]] + list(mid) + list(ffn)))
    conv_full = first[1].transpose(1, 2, 0, 3).reshape(3, CONV_WIDTH)

    core = cc.reshape(1).astype(jnp.int32)
    outs = {}

    class Reduce:
        @staticmethod
        def begin(grads, after=(), sum_after=()):
            names = list(grads)
            by_owner = [grads[n] for n in names]
            got = _sibling_swap("sibling_swap_" + names[0], by_owner, collective_id=next(ids), after=after)
            sums = _pair_sum("pair_sum_" + names[0], by_owner, got, core, after=sum_after)
            return sums, got

        @staticmethod
        def finish(grads, chip_sums, after=(), defer=()):
            names = list(grads)
            parts = _chip_exchange("chip_exchange_" + names[0], chip_sums, collective_id=next(ids), after=after)
            now = [n for n in names if n not in defer]
            updated = _adamw_sum("adamw_" + now[0], [big[n] for n in now],
                                 [p for n, p in zip(names, parts) if n in now],
                                 [big_m[n] for n in now], [big_v[n] for n in now])
            outs.update(zip(now, updated))
            return parts, [outs[n][1] for n in now]

        @staticmethod
        def riders(names, parts):
            return [(big[n], parts[n], big_m[n], big_v[n]) for n in names]

        @staticmethod
        def record(names, updated):
            outs.update(zip(names, updated))

    gains = (norm_mix_g, hg_norm_g, norm_ffn_g, norm_final_g.reshape(1, D_MODEL))
    grad_x, small, last = _local_step(x[0], loss_target[0], gains, lower_bounds, conv_full, wg8, Reduce)

    small_all = _all_gather("gather_small", [_small_pack(small)], collective_id=next(ids), after=last[:1])

    def small_state(a):
        return [a[0], a[1], a[2].reshape(1, D_MODEL), a[3], a[4], a[5].transpose(1, 0, 2)]

    upd = _small_update(
        small_all[0], my_dev.reshape(1).astype(jnp.int32),
        small_state((norm_mix_g, norm_ffn_g, norm_final_g, lower_bounds, hg_norm_g, conv_w)),
        small_state((m_norm_mix_g, m_norm_ffn_g, m_norm_final_g, m_lower_bounds, m_hg_norm_g, m_conv_w)),
        small_state((v_norm_mix_g, v_norm_ffn_g, v_norm_final_g, v_lower_bounds, v_hg_norm_g, v_conv_w)))
    loss = upd[0][0, 0]
    for p, (name, _, _, _) in enumerate(_SMALL_PARAMS):
        outs[name] = upd[1 + 4 * p:5 + 4 * p]
    outs["norm_final_g"] = [a.reshape(D_MODEL) for a in outs["norm_final_g"]]
    outs["conv_w"] = [a.transpose(1, 0, 2) for a in outs["conv_w"]]

    order = ["norm_mix_g", "w_in", "lower_bounds", "hg_norm_g", "conv_w", "w_branch_a", "w_branch_b", "w_out",
             "norm_ffn_g", "w_ffn_gate", "w_ffn_up", "w_ffn_down", "norm_final_g"]
    result = [loss, grad_x[None]]
    for k in range(4):
        for n in order:
            if n in BIG:
                result.append((outs[n][k].T if n in transposed else outs[n][k])[None])
            else:
                result.append(outs[n][k])
    return tuple(result)
```

```python
import jax
import jax.numpy as jnp
from jax import lax
from jax.experimental import pallas as pl
from jax.experimental.pallas import tpu as pltpu
from jax.experimental.pallas import tpu_sc as plsc

F32 = jnp.float32
BF16 = jnp.bfloat16
STASH = jnp.bfloat16

D_MODEL = 1024
HG_WIDTH = 512
HEAD_DIM = 128
N_HEADS = 4
HEADS_PER_STEP = 4
HEAD_GROUPS = N_HEADS // HEADS_PER_STEP
CONV_WIDTH = 512
CONV_K = 3
D_FF = 2816
CHUNK = 32
EPS = 1e-6
Q_SCALE = HEAD_DIM ** -0.5
N_DEV = 8

ADAM_LR = 0.001
ADAM_B1 = 0.9
ADAM_B2 = 0.999
ADAM_EPS = 1e-08
ADAM_WD = 0.01
ADAM_STEP = 10

VMEM_LIMIT_V7X = 56 * 1024 * 1024
VMEM_LIMIT_LARGE_V7X = 62 * 1024 * 1024

SMALL_ROWS = 16


def _params(sem, vmem=VMEM_LIMIT_V7X):
    return pltpu.CompilerParams(dimension_semantics=sem, vmem_limit_bytes=vmem)


def _mm(a, b):
    return jnp.dot(a.astype(BF16), b.astype(BF16), preferred_element_type=F32)


def _mm_nt(a, b):
    return lax.dot_general(a.astype(BF16), b.astype(BF16), (((1,), (1,)), ((), ())), preferred_element_type=F32)


def _mm_tn(a, b):
    return lax.dot_general(a.astype(BF16), b.astype(BF16), (((0,), (0,)), ((), ())), preferred_element_type=F32)


def _sigmoid(x):
    return 0.5 * jnp.tanh(0.5 * x) + 0.5


def _resident(shape):
    nd = len(shape)
    return pl.BlockSpec(shape, lambda *_: (0,) * nd, pipeline_mode=pl.Buffered(1))


def _full(shape):
    nd = len(shape)
    return pl.BlockSpec(shape, lambda *_: (0,) * nd)


def _shard_cols(w_ref):
    return jnp.concatenate([w_ref[s] for s in range(N_DEV)], axis=1)


N_HG = 4 * HG_WIDTH
N_CV = 3 * CONV_WIDTH
N_GT = 2 * D_MODEL
N_IN = N_HG + N_CV + N_GT


def _col(tm, n):
    return pl.BlockSpec((n, tm), lambda i: (0, i))


HALO = 8


def _fwd_in(x, g, w_in_t, conv_w, low, gn):
    T = x.shape[0]
    tm = min(512, T)
    nc = tm // CHUNK

    def body(x_ref, g_ref, w_ref, cw_ref, low_ref, gn_ref, ht_ref, hg_ref, cv_ref, gt_ref, cvo_ref, cvot_ref, cy_ref,
             o_ref, og_ref, ogt_ref, st_ref, tail_scr, s_scr):
        @pl.when(pl.program_id(0) == 0)
        def _():
            tail_scr[...] = jnp.zeros_like(tail_scr)
            s_scr[...] = jnp.zeros_like(s_scr)

        xv = x_ref[...]
        r = lax.rsqrt(jnp.mean(xv * xv, axis=-1, keepdims=True) + EPS)
        hf = xv * r * g_ref[...]
        h = hf.astype(BF16)
        ht_ref[...] = hf.T.astype(BF16)
        hg_ref[...] = _mm_nt(h, w_ref[:N_HG, :])
        cv = _mm_nt(h, w_ref[N_HG:N_HG + N_CV, :])
        cv_ref[...] = cv.astype(STASH)
        gt_ref[...] = _mm_nt(h, w_ref[N_HG + N_CV:, :]).astype(STASH)

        u = cv[:, :CONV_WIDTH] * cv[:, 2 * CONV_WIDTH:]
        row = lax.broadcasted_iota(jnp.int32, u.shape, 0)
        prev1 = tail_scr[HALO - 1:HALO, :]
        prev2 = tail_scr[HALO - 2:HALO - 1, :]
        u1 = jnp.where(row >= 1, pltpu.roll(u, 1, 0), prev1)
        u2 = jnp.where(row >= 2, pltpu.roll(u, 2, 0), jnp.where(row == 1, prev1, prev2))
        y = cw_ref[0:1, :] * u2 + cw_ref[1:2, :] * u1 + cw_ref[2:3, :] * u
        cy_ref[...] = y.astype(STASH)
        out = cv[:, CONV_WIDTH:2 * CONV_WIDTH] * y
        cvo_ref[...] = out.astype(BF16)
        cvot_ref[...] = out.T.astype(BF16)
        tail_scr[...] = u[tm - HALO:, :]

        _hg_fwd_tile(hg_ref, low_ref, gn_ref, o_ref, og_ref, ogt_ref, st_ref, s_scr, tm)

    row = lambda n: pl.BlockSpec((tm, n), lambda i: (i, 0))
    return pl.pallas_call(
        body, name="fwd_in", grid=(T // tm,),
        in_specs=[row(D_MODEL), _full((1, D_MODEL)), _resident(w_in_t.shape), _full((CONV_K, CONV_WIDTH)),
                  _full((2, HG_WIDTH)), _full((1, HEAD_DIM))],
        out_specs=[_col(tm, D_MODEL), row(N_HG), row(N_CV), row(N_GT), row(CONV_WIDTH), _col(tm, CONV_WIDTH),
                   row(CONV_WIDTH),
                   row(HG_WIDTH), row(HG_WIDTH), _col(tm, HG_WIDTH),
                   pl.BlockSpec((N_HEADS, nc, HEAD_DIM, HEAD_DIM), lambda i: (0, i, 0, 0))],
        out_shape=[jax.ShapeDtypeStruct((D_MODEL, T), BF16), jax.ShapeDtypeStruct((T, N_HG), F32),
                   jax.ShapeDtypeStruct((T, N_CV), STASH), jax.ShapeDtypeStruct((T, N_GT), STASH),
                   jax.ShapeDtypeStruct((T, CONV_WIDTH), BF16), jax.ShapeDtypeStruct((CONV_WIDTH, T), BF16),
                   jax.ShapeDtypeStruct((T, CONV_WIDTH), STASH), jax.ShapeDtypeStruct((T, HG_WIDTH), F32), jax.ShapeDtypeStruct((T, HG_WIDTH), BF16),
                   jax.ShapeDtypeStruct((HG_WIDTH, T), BF16),
                   jax.ShapeDtypeStruct((N_HEADS, T // CHUNK, HEAD_DIM, HEAD_DIM), F32)],
        scratch_shapes=[pltpu.VMEM((HALO, CONV_WIDTH), F32), pltpu.VMEM((N_HEADS, HEAD_DIM, HEAD_DIM), F32)],
        compiler_params=_params(("arbitrary",), vmem=VMEM_LIMIT_LARGE_V7X),
    )(x, g, w_in_t, conv_w, low, gn)


def _chunk_pos(shape):
    return lax.broadcasted_iota(jnp.int32, shape, 0) & (CHUNK - 1)


def _chunk_cumsum(x, pos):
    s = 1
    while s < CHUNK:
        x = x + jnp.where(pos >= s, pltpu.roll(x, s, 0), 0.0)
        s *= 2
    return x


def _chunk_rev_cumsum(x, pos):
    n = x.shape[0]
    s = 1
    while s < CHUNK:
        x = x + jnp.where(pos + s < CHUNK, pltpu.roll(x, n - s, 0), 0.0)
        s *= 2
    return x


def _lower_bound(low_ref):
    l0 = low_ref[0:1, :]
    l1 = low_ref[1:2, :]
    m = jnp.maximum(l0, l1)
    e0 = jnp.exp(l0 - m)
    e1 = jnp.exp(l1 - m)
    return e0 / (e0 + e1), e1 / (e0 + e1)


def _hg_gates(qr, fr, lb, pos, tb):
    sq = _sigmoid(qr)
    q = qr * sq * Q_SCALE
    sg = _sigmoid(fr)
    f = lb + (1.0 - lb) * sg
    k = 1.0 - f
    b = _chunk_cumsum(jnp.log(f), pos)
    b3 = b.reshape(tb // CHUNK, CHUNK, HEAD_DIM)
    anc = b3[:, CHUNK // 2 - 1:CHUNK // 2, :]
    last = b3[:, CHUNK - 1:CHUNK, :]
    d3 = b3 - anc
    e_qa3 = jnp.exp(d3)
    e_ka3 = jnp.exp(-d3)
    e_b3 = e_qa3 * jnp.exp(anc)
    e_ko3 = e_ka3 * jnp.exp(last - anc)
    dec = jnp.exp(last)
    flat = lambda a: a.reshape(tb, HEAD_DIM)
    return sq, q, sg, f, k, flat(e_qa3), flat(e_ka3), flat(e_b3), flat(e_ko3), dec


def _intra_mask(sb):
    r = lax.broadcasted_iota(jnp.int32, (sb, sb), 0)
    c = lax.broadcasted_iota(jnp.int32, (sb, sb), 1)
    return ((r // CHUNK) == (c // CHUNK)) & (c <= r)


def _hg_fwd_tile(hg_ref, low_ref, gn_ref, o_ref, og_ref, ogt_ref, st_ref, s_scr, tb):
    sb = min(256, tb)
    nc = tb // CHUNK
    q_ref, f_ref, i_ref, g_ref = (hg_ref.at[:, p * HG_WIDTH:(p + 1) * HG_WIDTH] for p in range(4))
    pos = _chunk_pos((tb, HEAD_DIM))
    mask = _intra_mask(sb)
    lanes = [slice(hh * HEAD_DIM, (hh + 1) * HEAD_DIM) for hh in range(N_HEADS)]
    qi, ko, vb, dec, st = [], [], [], [], []
    for hh, ln in enumerate(lanes):
        lb, _ = _lower_bound(low_ref.at[:, ln])
        _, q, _, _, k, e_qa, e_ka, e_b, e_ko, dec_h = _hg_gates(q_ref[:, ln], f_ref[:, ln], lb, pos, tb)
        qh = (q * e_qa).astype(BF16)
        kh = (k * e_ka).astype(BF16)
        qi.append((q * e_b).astype(BF16))
        ko.append((k * e_ko).astype(BF16))
        vb.append(i_ref[:, ln].astype(BF16))
        dec.append(dec_h)
        st.append(s_scr[hh])
        for s in range(tb // sb):
            sl = slice(s * sb, (s + 1) * sb)
            p = jnp.where(mask, _mm_nt(qh[sl], kh[sl]), 0.0)
            o_ref[sl, ln] = _mm(p, vb[hh][sl])
    for c in range(nc):
        sl = slice(c * CHUNK, (c + 1) * CHUNK)
        for hh, ln in enumerate(lanes):
            st_ref[hh, c] = st[hh]
            o_ref[sl, ln] = o_ref[sl, ln] + _mm_nt(qi[hh][sl], st[hh])
            st[hh] = dec[hh][c] * st[hh] + _mm_tn(vb[hh][sl], ko[hh][sl])
    for hh, ln in enumerate(lanes):
        s_scr[hh] = st[hh]
        o = o_ref[:, ln]
        r = lax.rsqrt(jnp.mean(o * o, axis=-1, keepdims=True) + EPS)
        gr = g_ref[:, ln]
        og = (o * r * gn_ref[...]) * (gr * _sigmoid(gr))
        og_ref[:, ln] = og.astype(BF16)
        ogt_ref[ln, :] = og.T.astype(BF16)


def _merge_fwd(og, cvo, gt, x, wa, wb, wo):
    T = x.shape[0]
    tm = min(1024, T)

    def body(og_ref, cvo_ref, gt_ref, x_ref, wa_ref, wb_ref, wo_ref, x1_ref, mgt_ref, ya_ref, yb_ref):
        ya = jnp.dot(og_ref[...], _shard_cols(wa_ref), preferred_element_type=F32)
        yb = jnp.dot(cvo_ref[...], _shard_cols(wb_ref), preferred_element_type=F32)
        ya_ref[...] = ya.astype(STASH)
        yb_ref[...] = yb.astype(STASH)
        m = (_sigmoid(gt_ref[:, :D_MODEL].astype(F32)) * ya
             + _sigmoid(gt_ref[:, D_MODEL:].astype(F32)) * yb)
        mgt_ref[...] = m.T.astype(BF16)
        x1_ref[...] = x_ref[...] + jnp.dot(m.astype(BF16), wo_ref[...], preferred_element_type=F32)

    row = lambda n: pl.BlockSpec((tm, n), lambda i: (i, 0))
    return pl.pallas_call(
        body, name="merge_fwd", grid=(T // tm,),
        in_specs=[row(HG_WIDTH), row(CONV_WIDTH), row(2 * D_MODEL), row(D_MODEL),
                  _resident(wa.shape), _resident(wb.shape), _resident(wo.shape)],
        out_specs=[row(D_MODEL), _col(tm, D_MODEL), row(D_MODEL), row(D_MODEL)],
        out_shape=[jax.ShapeDtypeStruct((T, D_MODEL), F32), jax.ShapeDtypeStruct((D_MODEL, T), BF16),
                   jax.ShapeDtypeStruct((T, D_MODEL), STASH), jax.ShapeDtypeStruct((T, D_MODEL), STASH)],
        compiler_params=_params(("parallel",)),
    )(og, cvo, gt, x, wa, wb, wo)


def _ffn_fwd_loss(x1, g, wg, wu, wd, target, g_fin):
    T = x1.shape[0]
    tm = min(512, T)

    def body(x_ref, g_ref, wg_ref, wu_ref, wd_ref, t_ref, gf_ref,
             ht_ref, gate_ref, up_ref, act_ref, loss_ref, dgf_ref, dx2_ref, dx2t_ref):
        @pl.when(pl.program_id(0) == 0)
        def _():
            loss_ref[...] = jnp.zeros_like(loss_ref)
            dgf_ref[...] = jnp.zeros_like(dgf_ref)

        xv = x_ref[...]
        r = lax.rsqrt(jnp.mean(xv * xv, axis=-1, keepdims=True) + EPS)
        hf = xv * r * g_ref[...]
        h = hf.astype(BF16)
        ht_ref[...] = hf.T.astype(BF16)
        gate = _mm_nt(h, wg_ref[...])
        up = _mm_nt(h, wu_ref[...])
        gate_ref[...] = gate.astype(STASH)
        up_ref[...] = up.astype(STASH)
        act = (gate * _sigmoid(gate) * up).astype(BF16)
        act_ref[...] = act
        x2 = xv + jnp.dot(act, wd_ref[...], preferred_element_type=F32)

        gv = gf_ref[...]
        r2 = lax.rsqrt(jnp.mean(x2 * x2, axis=-1, keepdims=True) + EPS)
        xh = x2 * r2
        err = xh * gv - t_ref[...]
        loss_ref[...] += 0.5 * jnp.sum(jnp.mean(err * err, axis=-1, keepdims=True), axis=0, keepdims=True)
        dy = err * (1.0 / D_MODEL)
        dgf_ref[...] += jnp.sum(dy * xh, axis=0, keepdims=True)
        w = dy * gv
        dx2 = r2 * (w - xh * jnp.mean(w * xh, axis=-1, keepdims=True))
        dx2_ref[...] = dx2
        dx2t_ref[...] = dx2.T.astype(BF16)

    row = lambda n: pl.BlockSpec((tm, n), lambda i: (i, 0))
    return pl.pallas_call(
        body, name="ffn_fwd_loss", grid=(T // tm,),
        in_specs=[row(D_MODEL), _full((1, D_MODEL)), _resident(wg.shape), _resident(wu.shape), _resident(wd.shape),
                  row(D_MODEL), _full((1, D_MODEL))],
        out_specs=[_col(tm, D_MODEL), row(D_FF), row(D_FF), row(D_FF), _full((1, 128)), _full((1, D_MODEL)),
                   row(D_MODEL), _col(tm, D_MODEL)],
        out_shape=[jax.ShapeDtypeStruct((D_MODEL, T), BF16), jax.ShapeDtypeStruct((T, D_FF), STASH),
                   jax.ShapeDtypeStruct((T, D_FF), STASH), jax.ShapeDtypeStruct((T, D_FF), BF16),
                   jax.ShapeDtypeStruct((1, 128), F32), jax.ShapeDtypeStruct((1, D_MODEL), F32),
                   jax.ShapeDtypeStruct((T, D_MODEL), F32), jax.ShapeDtypeStruct((D_MODEL, T), BF16)],
        compiler_params=_params(("arbitrary",), vmem=VMEM_LIMIT_LARGE_V7X),
    )(x1, g, wg, wu, wd, target, g_fin)


def _ffn_bwd(dx2, x1, gate, up, g, wg, wu, wd):
    T = x1.shape[0]
    tm = min(512, T)

    def body(dx2_ref, x_ref, gate_ref, up_ref, g_ref, wg_ref, wu_ref, wd_ref, dgate_ref, dup_ref, dx1_ref, dgn_ref):
        @pl.when(pl.program_id(0) == 0)
        def _():
            dgn_ref[...] = jnp.zeros_like(dgn_ref)

        dx2 = dx2_ref[...]
        dact = _mm_nt(dx2, wd_ref[...])
        gate = gate_ref[...].astype(F32)
        s = _sigmoid(gate)
        dgate = (dact * up_ref[...].astype(F32) * (s * (1.0 + gate * (1.0 - s)))).astype(BF16)
        dup = (dact * (gate * s)).astype(BF16)
        dgate_ref[...] = dgate
        dup_ref[...] = dup
        dh = _mm(dgate, wg_ref[...]) + _mm(dup, wu_ref[...])
        xv = x_ref[...]
        r = lax.rsqrt(jnp.mean(xv * xv, axis=-1, keepdims=True) + EPS)
        xh = xv * r
        dgn_ref[...] += jnp.sum(dh * xh, axis=0, keepdims=True)
        w = dh * g_ref[...]
        dx1_ref[...] = dx2 + r * (w - xh * jnp.mean(w * xh, axis=-1, keepdims=True))

    row = lambda n: pl.BlockSpec((tm, n), lambda i: (i, 0))
    return pl.pallas_call(
        body, name="ffn_bwd", grid=(T // tm,),
        in_specs=[row(D_MODEL), row(D_MODEL), row(D_FF), row(D_FF), _full((1, D_MODEL)),
                  _resident(wg.shape), _resident(wu.shape), _resident(wd.shape)],
        out_specs=[row(D_FF), row(D_FF), row(D_MODEL), _full((1, D_MODEL))],
        out_shape=[jax.ShapeDtypeStruct((T, D_FF), BF16), jax.ShapeDtypeStruct((T, D_FF), BF16),
                   jax.ShapeDtypeStruct((T, D_MODEL), F32), jax.ShapeDtypeStruct((1, D_MODEL), F32)],
        compiler_params=_params(("arbitrary",), vmem=VMEM_LIMIT_LARGE_V7X),
    )(dx2, x1, gate, up, g, wg, wu, wd)


def _merge_bwd(dx1, ya, yb, gt, cv, cy, wa, wb, wo, conv_w):
    T = dx1.shape[0]
    tm = min(512, T)
    nt = T // tm

    def body(dx_ref, ya_ref, yb_ref, gt_ref, cv_ref, cy_ref, wa_ref, wb_ref, wo_ref, cw_ref,
             dgt_ref, dya_ref, dyb_ref, dog_ref, dcv_ref, dcw_ref, next_dy):
        @pl.when(pl.program_id(0) == 0)
        def _():
            next_dy[...] = jnp.zeros_like(next_dy)
            dcw_ref[...] = jnp.zeros_like(dcw_ref)

        dm = _mm_nt(dx_ref[...], wo_ref[...])
        wa = _shard_cols(wa_ref)
        wb = _shard_cols(wb_ref)
        ya = ya_ref[...].astype(F32)
        yb = yb_ref[...].astype(F32)
        sa = _sigmoid(gt_ref[:, :D_MODEL].astype(F32))
        sb = _sigmoid(gt_ref[:, D_MODEL:].astype(F32))
        da = dm * sa
        db = dm * sb
        dgt_ref[:, :D_MODEL] = (da * ya * (1.0 - sa)).astype(BF16)
        dgt_ref[:, D_MODEL:] = (db * yb * (1.0 - sb)).astype(BF16)
        dya = da.astype(BF16)
        dyb = db.astype(BF16)
        dya_ref[...] = dya
        dyb_ref[...] = dyb
        dog_ref[...] = _mm_nt(dya, wa)
        dcvo = _mm_nt(dyb, wb)

        cvt = cv_ref[...].astype(F32)
        c, bg, xb = cvt[:, :CONV_WIDTH], cvt[:, CONV_WIDTH:2 * CONV_WIDTH], cvt[:, 2 * CONV_WIDTH:]
        u = c * xb
        row = lax.broadcasted_iota(jnp.int32, u.shape, 0)
        w0, w1, w2 = cw_ref[0:1, :], cw_ref[1:2, :], cw_ref[2:3, :]
        dcv_ref[:, CONV_WIDTH:2 * CONV_WIDTH] = (dcvo * cy_ref[...].astype(F32)).astype(BF16)
        dy = dcvo * bg
        n1 = next_dy[0:1, :]
        n2 = next_dy[1:2, :]
        dy1 = jnp.where(row < tm - 1, pltpu.roll(dy, tm - 1, 0), n1)
        dy2 = jnp.where(row < tm - 2, pltpu.roll(dy, tm - 2, 0), jnp.where(row == tm - 2, n1, n2))
        dcw_ref[0:1, :] += jnp.sum(dy2 * u, axis=0, keepdims=True)
        dcw_ref[1:2, :] += jnp.sum(dy1 * u, axis=0, keepdims=True)
        dcw_ref[2:3, :] += jnp.sum(dy * u, axis=0, keepdims=True)
        du = w2 * dy + w1 * dy1 + w0 * dy2
        dcv_ref[:, :CONV_WIDTH] = (du * xb).astype(BF16)
        dcv_ref[:, 2 * CONV_WIDTH:] = (du * c).astype(BF16)
        next_dy[...] = dy[:HALO, :]

    rt = lambda i: nt - 1 - i
    row = lambda n: pl.BlockSpec((tm, n), lambda i: (rt(i), 0))
    return pl.pallas_call(
        body, name="merge_bwd", grid=(nt,),
        in_specs=[row(D_MODEL), row(D_MODEL), row(D_MODEL), row(2 * D_MODEL), row(N_CV), row(CONV_WIDTH),
                  _resident(wa.shape), _resident(wb.shape), _resident(wo.shape), _full((CONV_K, CONV_WIDTH))],
        out_specs=[row(2 * D_MODEL), row(D_MODEL), row(D_MODEL), row(HG_WIDTH), row(N_CV),
                   _full((CONV_K, CONV_WIDTH))],
        out_shape=[jax.ShapeDtypeStruct((T, 2 * D_MODEL), BF16), jax.ShapeDtypeStruct((T, D_MODEL), BF16),
                   jax.ShapeDtypeStruct((T, D_MODEL), BF16), jax.ShapeDtypeStruct((T, HG_WIDTH), F32),
                   jax.ShapeDtypeStruct((T, N_CV), BF16), jax.ShapeDtypeStruct((CONV_K, CONV_WIDTH), F32)],
        scratch_shapes=[pltpu.VMEM((HALO, CONV_WIDTH), F32)],
        compiler_params=_params(("arbitrary",)),
    )(dx1, ya, yb, gt, cv, cy, wa, wb, wo, conv_w)


def _drop_operands(body, first, count):
    def wrapped(*refs):
        return body(*refs[:first], *refs[first + count:])
    return wrapped


def _hg_bwd(dog, hg, o, st, low, gn, after=()):
    T = hg.shape[0]
    tb = min(512, T)
    sb = min(256, tb)
    nb = T // tb
    nc = tb // CHUNK
    wid = HEADS_PER_STEP * HEAD_DIM

    def body(q_ref, f_ref, i_ref, g_ref, low_ref, gn_ref, o_ref, dog_ref, st_ref,
             dhg_ref, dlow_ref, dgn_ref,
             ds_scr, dqi_scr, dko_scr, dv_scr, dd_scr, dqh_scr, dkh_scr):
        h = pl.program_id(0)
        t = pl.program_id(1)
        dq_ref, df_ref, di_ref, dg_ref = (dhg_ref.at[:, p * HG_WIDTH:(p + 1) * HG_WIDTH] for p in range(4))

        @pl.when(t == 0)
        def _():
            ds_scr[...] = jnp.zeros_like(ds_scr)
            dlow_ref[...] = jnp.zeros_like(dlow_ref)

        @pl.when((t == 0) & (h == 0))
        def _():
            dgn_ref[...] = jnp.zeros_like(dgn_ref)

        pos = _chunk_pos((tb, HEAD_DIM))
        mask = _intra_mask(sb)
        gnv = gn_ref[...]
        lanes = [slice(hh * HEAD_DIM, (hh + 1) * HEAD_DIM) for hh in range(HEADS_PER_STEP)]
        heads = []
        for hh, ln in enumerate(lanes):
            lb, lb1 = _lower_bound(low_ref.at[:, ln])
            qr = q_ref[:, ln]
            sq, q, sg, f, k, e_qa, e_ka, e_b, e_ko, dec = _hg_gates(qr, f_ref[:, ln], lb, pos, tb)

            gr = g_ref[:, ln]
            o = o_ref[:, ln]
            dog_v = dog_ref[:, ln]
            sgr = _sigmoid(gr)
            r = lax.rsqrt(jnp.mean(o * o, axis=-1, keepdims=True) + EPS)
            oh = o * r
            dg_ref[:, ln] = (dog_v * (oh * gnv) * (sgr * (1.0 + gr * (1.0 - sgr)))).astype(BF16)
            don = dog_v * (gr * sgr)
            dgn_ref[...] += jnp.sum(don * oh, axis=0, keepdims=True)
            w = don * gnv
            do = (r * (w - oh * jnp.mean(w * oh, axis=-1, keepdims=True))).astype(BF16)

            qh = (q * e_qa).astype(BF16)
            kh = (k * e_ka).astype(BF16)
            qi = (q * e_b).astype(BF16)
            ko = (k * e_ko).astype(BF16)
            vb = i_ref[:, ln].astype(BF16)

            for s in range(tb // sb):
                sl = slice(s * sb, (s + 1) * sb)
                p = jnp.where(mask, _mm_nt(qh[sl], kh[sl]), 0.0).astype(BF16)
                dp = jnp.where(mask, _mm_nt(do[sl], vb[sl]), 0.0).astype(BF16)
                dv_scr[sl, ln] = _mm_tn(p, do[sl])
                dqh_scr[sl, ln] = _mm(dp, kh[sl])
                dkh_scr[sl, ln] = _mm_tn(dp, qh[sl])
            heads.append(dict(lb=lb, lb1=lb1, qr=qr, sq=sq, q=q, sg=sg, f=f, k=k, e_qa=e_qa, e_ka=e_ka, e_b=e_b,
                              e_ko=e_ko, dec=dec, do=do, qi=qi, ko=ko, vb=vb, ds=ds_scr[hh]))

        for c in reversed(range(nc)):
            sl = slice(c * CHUNK, (c + 1) * CHUNK)
            for hh, ln in enumerate(lanes):
                hd = heads[hh]
                ds = hd["ds"]
                st_c = st_ref[hh, c]
                dqi_scr[sl, ln] = _mm(hd["do"][sl], st_c)
                dko_scr[sl, ln] = _mm(hd["vb"][sl], ds)
                dv_scr[sl, ln] = dv_scr[sl, ln] + _mm_nt(hd["ko"][sl], ds)
                dec_c = hd["dec"][c]
                dd_scr[sl, ln] = jnp.broadcast_to(dec_c * jnp.sum(ds * st_c, axis=0, keepdims=True),
                                                  (CHUNK, HEAD_DIM))
                hd["ds"] = dec_c * ds + _mm_tn(hd["do"][sl], hd["qi"][sl])

        for hh, ln in enumerate(lanes):
            hd = heads[hh]
            ds_scr[hh] = hd["ds"]
            q, k, lb = hd["q"], hd["k"], hd["lb"]
            dko_e = dko_scr[:, ln] * hd["e_ko"]
            dq = dqh_scr[:, ln] * hd["e_qa"] + dqi_scr[:, ln] * hd["e_b"]
            dk = dkh_scr[:, ln] * hd["e_ka"] + dko_e
            kd3 = (k * dko_e).reshape(nc, CHUNK, HEAD_DIM)
            last = jnp.broadcast_to(jnp.sum(kd3, axis=1, keepdims=True), kd3.shape).reshape(tb, HEAD_DIM)
            db = q * dq - k * dk + jnp.where(pos == CHUNK - 1, dd_scr[:, ln] + last, 0.0)
            dlg = _chunk_rev_cumsum(db, pos)
            dfv = dlg / hd["f"] - dk
            s_low = jnp.sum(dfv * (1.0 - hd["sg"]), axis=0, keepdims=True)
            dlow_ref[0:1, ln] += s_low * lb * (1.0 - lb)
            dlow_ref[1:2, ln] += -s_low * lb * hd["lb1"]
            df_ref[:, ln] = (dfv * (1.0 - lb) * hd["sg"] * (1.0 - hd["sg"])).astype(BF16)
            dq_ref[:, ln] = (dq * Q_SCALE * (hd["sq"] * (1.0 + hd["qr"] * (1.0 - hd["sq"])))).astype(BF16)
            di_ref[:, ln] = dv_scr[:, ln].astype(BF16)

    rt = lambda t: nb - 1 - t
    col = lambda p: pl.BlockSpec((tb, wid), lambda h, t: (rt(t), p * HEAD_GROUPS + h))
    hcol = pl.BlockSpec((tb, wid), lambda h, t: (rt(t), h))
    assert HEAD_GROUPS == 1
    tile = pltpu.VMEM((tb, wid), F32)
    return pl.pallas_call(
        _drop_operands(body, 9, len(after)), name="hg_bwd", grid=(HEAD_GROUPS, nb),
        in_specs=[col(0), col(1), col(2), col(3), pl.BlockSpec((2, wid), lambda h, t: (0, h)),
                  pl.BlockSpec((1, HEAD_DIM), lambda h, t: (0, 0)), hcol, hcol,
                  pl.BlockSpec((HEADS_PER_STEP, nc, HEAD_DIM, HEAD_DIM), lambda h, t: (h, rt(t), 0, 0))]
                 + [HBM_SPEC] * len(after),
        out_specs=[pl.BlockSpec((tb, N_HG), lambda h, t: (rt(t), 0)), pl.BlockSpec((2, wid), lambda h, t: (0, h)),
                   pl.BlockSpec((1, HEAD_DIM), lambda h, t: (0, 0))],
        out_shape=[jax.ShapeDtypeStruct((T, N_HG), BF16), jax.ShapeDtypeStruct((2, HG_WIDTH), F32),
                   jax.ShapeDtypeStruct((1, HEAD_DIM), F32)],
        scratch_shapes=[pltpu.VMEM((HEADS_PER_STEP, HEAD_DIM, HEAD_DIM), F32), tile, tile, tile, tile, tile, tile],
        compiler_params=_params(("arbitrary", "arbitrary")),
    )(hg, hg, hg, hg, low, gn, o, dog, st, *after)


def _in_bwd(dparts, w_in, x, dx1, g, after=()):
    T = x.shape[0]
    tm = min(512, T)
    widths = [p.shape[1] for p in dparts]
    offs = [sum(widths[:i]) for i in range(len(widths))]
    n = len(dparts)

    def body(*refs):
        d_refs = refs[:n]
        w_ref, x_ref, dx1_ref, g_ref, dx_ref, dgn_ref = refs[n:]

        @pl.when(pl.program_id(0) == 0)
        def _():
            dgn_ref[...] = jnp.zeros_like(dgn_ref)

        dh = None
        for d_ref, off, wd in zip(d_refs, offs, widths):
            part = _mm(d_ref[...], w_ref[off:off + wd, :])
            dh = part if dh is None else dh + part
        xv = x_ref[...]
        r = lax.rsqrt(jnp.mean(xv * xv, axis=-1, keepdims=True) + EPS)
        xh = xv * r
        dgn_ref[...] += jnp.sum(dh * xh, axis=0, keepdims=True)
        w = dh * g_ref[...]
        dx_ref[...] = dx1_ref[...] + r * (w - xh * jnp.mean(w * xh, axis=-1, keepdims=True))

    row = lambda m: pl.BlockSpec((tm, m), lambda i: (i, 0))
    return pl.pallas_call(
        _drop_operands(body, n + 4, len(after)), name="in_bwd", grid=(T // tm,),
        in_specs=[row(wd) for wd in widths] + [_resident(w_in.shape), row(D_MODEL), row(D_MODEL), _full((1, D_MODEL))]
                 + [HBM_SPEC] * len(after),
        out_specs=[row(D_MODEL), _full((1, D_MODEL))],
        out_shape=[jax.ShapeDtypeStruct((T, D_MODEL), F32), jax.ShapeDtypeStruct((1, D_MODEL), F32)],
        compiler_params=_params(("arbitrary",)),
    )(*dparts, w_in, x, dx1, g, *after)


def _wgrad_ffn(h2t, dgate, dup, dx2t, act, tn=256):
    M, T = h2t.shape
    N = dgate.shape[1]

    def body(h_ref, x_ref, dg_ref, du_ref, act_ref, og_ref, ou_ref, od_ref):
        h = h_ref[...]
        og_ref[...] = _mm(h, dg_ref[...]).T.astype(BF16)
        ou_ref[...] = _mm(h, du_ref[...]).T.astype(BF16)
        od_ref[...] = _mm(x_ref[...], act_ref[...]).T.astype(BF16)

    rhs = pl.BlockSpec((T, tn), lambda j: (0, j))
    out_spec = pl.BlockSpec((tn, M), lambda j: (j, 0))
    out = jax.ShapeDtypeStruct((N, M), BF16)
    return pl.pallas_call(
        body, name="wgrad_ffn", grid=(N // tn,),
        in_specs=[_resident((M, T)), _resident((M, T)), rhs, rhs, rhs], out_specs=[out_spec] * 3,
        out_shape=[out, out, out],
        compiler_params=_params(("parallel",)),
    )(h2t, dx2t, dgate, dup, act)


def _wgrad_out_branches(ogt, dya, cvot, dyb, mgt, dx1, after=()):
    M, T = ogt.shape
    N = dya.shape[1]
    c = N // N_DEV
    per = 2
    tn = per * c

    def body(at_ref, da_ref, bt_ref, db_ref, mt_ref, dx_ref, oa_ref, ob_ref, oo_ref):
        ga = _mm(at_ref[...], da_ref[...])
        gb = _mm(bt_ref[...], db_ref[...])
        for s in range(per):
            oa_ref[s] = ga[:, s * c:(s + 1) * c].astype(BF16)
            ob_ref[s] = gb[:, s * c:(s + 1) * c].astype(BF16)
        oo_ref[...] = _mm(mt_ref[...], dx_ref[...]).astype(BF16)

    rhs = pl.BlockSpec((T, tn), lambda j: (0, j))
    owners = pl.BlockSpec((per, M, c), lambda j: (j, 0, 0))
    out = jax.ShapeDtypeStruct((N_DEV, M, c), BF16)
    return pl.pallas_call(
        _drop_operands(body, 6, len(after)), name="wgrad_out_branches", grid=(N // tn,),
        in_specs=[_resident((M, T)), rhs, _resident((M, T)), rhs, _resident(mgt.shape), rhs] + [HBM_SPEC] * len(after),
        out_specs=[owners, owners, pl.BlockSpec((mgt.shape[0], tn), lambda j: (0, j))],
        out_shape=[out, out, jax.ShapeDtypeStruct((mgt.shape[0], dx1.shape[1]), BF16)],
        compiler_params=_params(("parallel",)),
    )(ogt, dya, cvot, dyb, mgt, dx1, *after)


def _wgrad_in(ht, dparts, after=(), riders=()):
    M, T = ht.shape
    tn = 512
    nblk = [p.shape[1] // tn for p in dparts]
    start = [sum(nblk[:i]) for i in range(len(nblk))]
    n = len(dparts)
    steps = sum(nblk)
    nr = len(riders)
    rows = [r[0].shape[0] // steps for r in riders]
    assert all(r[0].shape[0] == rr * steps and rr % 16 == 0 for r, rr in zip(riders, rows))

    def body(a_ref, *refs):
        d_refs = refs[:n]
        rider_in = refs[n:n + 4 * nr]
        o_ref = refs[n + 4 * nr]
        rider_out = refs[n + 4 * nr + 1:]
        j = pl.program_id(0)
        for d_ref, s, nb in zip(d_refs, start, nblk):
            @pl.when((j >= s) & (j < s + nb))
            def _():
                o_ref[...] = _mm(a_ref[...], d_ref[...]).T.astype(BF16)
        for i in range(nr):
            w_ref, p_ref, m_ref, v_ref = rider_in[4 * i:4 * i + 4]
            g = p_ref[0].astype(F32)
            for k in range(1, 4):
                g = g + p_ref[k].astype(F32)
            delta, m_new, v_new = _adamw_math(w_ref[...], g, m_ref[...], v_ref[...])
            for k, val in enumerate((g, delta, m_new, v_new)):
                rider_out[4 * i + k][...] = val

    def piece_spec(s, nb):
        return pl.BlockSpec((T, tn), lambda j: (0, jnp.clip(j - s, 0, nb - 1)))

    rider_specs, rider_out_specs, rider_out_shape, rider_args = [], [], [], []
    for (w, parts, m, v), rr in zip(riders, rows):
        blk = pl.BlockSpec((rr, w.shape[1]), lambda j: (j, 0))
        rider_specs += [blk, pl.BlockSpec((4, rr, w.shape[1]), lambda j: (0, j, 0)), blk, blk]
        rider_out_specs += [blk] * 4
        rider_out_shape += [jax.ShapeDtypeStruct(w.shape, F32)] * 4
        rider_args += [w, parts, m, v]
    outs = pl.pallas_call(
        _drop_operands(body, 1 + n + 4 * nr, len(after)), name="wgrad_in", grid=(steps,),
        in_specs=[_resident((M, T))] + [piece_spec(s, nb) for s, nb in zip(start, nblk)] + rider_specs
                 + [HBM_SPEC] * len(after),
        out_specs=[pl.BlockSpec((tn, M), lambda j: (j, 0))] + rider_out_specs,
        out_shape=[jax.ShapeDtypeStruct((steps * tn, M), BF16)] + rider_out_shape,
        compiler_params=_params(("parallel",)),
    )(ht, *dparts, *rider_args, *after)
    return outs[0], [outs[1 + 4 * i:5 + 4 * i] for i in range(nr)]


def _adamw_math(w, g, m, v):
    m = ADAM_B1 * m + (1.0 - ADAM_B1) * g
    v = ADAM_B2 * v + (1.0 - ADAM_B2) * (g * g)
    m_hat = m / (1.0 - ADAM_B1 ** ADAM_STEP)
    v_hat = v / (1.0 - ADAM_B2 ** ADAM_STEP)
    delta = -ADAM_LR * (m_hat / (jnp.sqrt(v_hat) + ADAM_EPS) + ADAM_WD * w)
    return delta, m, v


def _adamw_sum(name, ws, parts, ms, vs):
    n = len(ws)
    steps = min(_row_steps(w.shape[0]) for w in ws)
    rows = [w.shape[0] // steps for w in ws]

    def body(*refs):
        w_refs, p_refs, m_refs, v_refs = (refs[k * n:(k + 1) * n] for k in range(4))
        out_refs = refs[4 * n:]
        for i in range(n):
            g = p_refs[i][0].astype(F32)
            for k in range(1, 4):
                g = g + p_refs[i][k].astype(F32)
            delta, m_new, v_new = _adamw_math(w_refs[i][...], g, m_refs[i][...], v_refs[i][...])
            for k, val in enumerate((g, delta, m_new, v_new)):
                out_refs[4 * i + k][...] = val

    blk = [pl.BlockSpec((r, w.shape[1]), lambda s: (s, 0)) for r, w in zip(rows, ws)]
    pblk = [pl.BlockSpec((4, r, w.shape[1]), lambda s: (0, s, 0)) for r, w in zip(rows, ws)]
    out_specs, out_shape = [], []
    for b, w in zip(blk, ws):
        out_specs += [b] * 4
        out_shape += [jax.ShapeDtypeStruct(w.shape, F32)] * 4
    flat = pl.pallas_call(
        body, name=name, grid=(steps,),
        in_specs=blk + pblk + blk + blk, out_specs=out_specs, out_shape=out_shape,
        compiler_params=_params(("parallel",)),
    )(*ws, *parts, *ms, *vs)
    return [flat[4 * i:4 * i + 4] for i in range(n)]


_SMALL_SLOTS = (("norm_mix_g", 0, 1, 1024), ("norm_ffn_g", 1, 1, 1024), ("norm_final_g", 2, 1, 1024),
                ("lower_bounds", 3, 2, 512), ("hg_norm_g", 5, 1, 128), ("loss", 6, 1, 128), ("conv_w", 8, 3, 512))
_SMALL_PARAMS = tuple(s for s in _SMALL_SLOTS if s[0] != "loss")
CONV_SHARD = CONV_WIDTH // N_DEV


def _small_pack(small):
    def body(*refs):
        out = refs[-1]
        out[...] = jnp.zeros_like(out)
        for ref, (_, row, rows, lanes) in zip(refs[:-1], _SMALL_SLOTS):
            out[row:row + rows, 0:lanes] = ref[...]

    vmem = pl.BlockSpec(memory_space=pltpu.VMEM)
    return pl.pallas_call(
        body, name="small_pack", in_specs=[vmem] * len(_SMALL_SLOTS), out_specs=vmem,
        out_shape=jax.ShapeDtypeStruct((SMALL_ROWS, 1024), F32),
    )(*[small[name] for name, _, _, _ in _SMALL_SLOTS])


def _small_update(gathered, dev, w, m, v):
    n = len(_SMALL_PARAMS)

    def body(dev_ref, g_ref, *refs):
        w_refs, m_refs, v_refs = refs[:n], refs[n:2 * n], refs[2 * n:3 * n]
        loss_ref, out_refs, sum_scr = refs[3 * n], refs[3 * n + 1:-1], refs[-1]
        total = g_ref[0]
        for k in range(1, N_DEV):
            total = total + g_ref[k]
        sum_scr[...] = total
        loss_ref[...] = sum_scr[6:7, 0:128]
        for p, (name, row, rows, lanes) in enumerate(_SMALL_PARAMS):
            if name == "conv_w":
                for r in range(rows):
                    g = sum_scr[row + r:row + r + 1, 0:CONV_SHARD]
                    for s in range(1, N_DEV):
                        mine = sum_scr[row + r:row + r + 1, s * CONV_SHARD:(s + 1) * CONV_SHARD]
                        g = jnp.where(dev_ref[0] == s, mine, g)
                    delta, m_new, v_new = _adamw_math(w_refs[p][r], g, m_refs[p][r], v_refs[p][r])
                    out_refs[4 * p][r] = g
                    out_refs[4 * p + 1][r] = delta
                    out_refs[4 * p + 2][r] = m_new
                    out_refs[4 * p + 3][r] = v_new
                continue
            g = sum_scr[row:row + rows, 0:lanes]
            delta, m_new, v_new = _adamw_math(w_refs[p][...], g, m_refs[p][...], v_refs[p][...])
            out_refs[4 * p][...] = g
            out_refs[4 * p + 1][...] = delta
            out_refs[4 * p + 2][...] = m_new
            out_refs[4 * p + 3][...] = v_new

    vmem = pl.BlockSpec(memory_space=pltpu.VMEM)
    outs = [jax.ShapeDtypeStruct((1, 128), F32)]
    for a in w:
        outs += [jax.ShapeDtypeStruct(a.shape, F32)] * 4
    return pl.pallas_call(
        body, name="small_update",
        in_specs=[pl.BlockSpec(memory_space=pltpu.SMEM)] + [vmem] * (1 + 3 * n), out_specs=[vmem] * len(outs),
        out_shape=outs, scratch_shapes=[pltpu.VMEM((SMALL_ROWS, 1024), F32)],
    )(dev, gathered, *w, *m, *v)


def _row_steps(rows):
    for steps in (4, 2):
        if rows % (16 * steps) == 0:
            return steps
    return 1


def _pair_sum(name, by_owner, got, core, after=()):
    n = len(got)

    def body(core_ref, *refs):
        for a_ref, b_ref, o_ref in zip(refs[:n], refs[n:2 * n], refs[2 * n:]):
            o_ref[...] = (a_ref[...].astype(F32) + b_ref[...].astype(F32)).astype(BF16)

    def blk(g):
        return pl.BlockSpec((None,) + g.shape[1:], lambda k, core_ref: (k, 0, 0))

    def mine(g):
        return pl.BlockSpec((None,) + g.shape[1:], lambda k, core_ref: (2 * k + core_ref[0], 0, 0))

    return pl.pallas_call(
        _drop_operands(body, 1 + 2 * n, len(after)), name=name,
        grid_spec=pltpu.PrefetchScalarGridSpec(
            num_scalar_prefetch=1, grid=(4,),
            in_specs=[mine(g) for g in got] + [blk(g) for g in got] + [HBM_SPEC] * len(after),
            out_specs=[blk(g) for g in got]),
        out_shape=[jax.ShapeDtypeStruct(g.shape, BF16) for g in got],
        compiler_params=_params(("parallel",)),
    )(core, *by_owner, *got, *after)


MESH = pl.DeviceIdType.MESH
HBM_SPEC = pl.BlockSpec(memory_space=pl.ANY)


def _handshake(peers):
    barrier = pltpu.get_barrier_semaphore()
    for peer in peers:
        pl.semaphore_signal(barrier, inc=1, device_id=peer, device_id_type=MESH)
    pl.semaphore_wait(barrier, len(peers))


def _comm_call(body, name, operands, out_shape, scratch, collective_id):
    if collective_id is None:
        return pl.pallas_call(body, name=name, in_specs=[HBM_SPEC] * len(operands), out_specs=[HBM_SPEC] * len(out_shape),
                              out_shape=out_shape, scratch_shapes=scratch)(*operands)
    return pl.kernel(body, out_type=out_shape, mesh=plsc.ScalarSubcoreMesh(axis_name="sequencer", num_cores=1),
                     scratch_types=scratch, name=name,
                     compiler_params=pltpu.CompilerParams(collective_id=collective_id))(*operands)


def _all_gather(name, blocks, collective_id=None, after=(), pieces=None):
    n = len(blocks)
    na = len(after)
    pieces = pieces or [1] * n
    parts = []
    for i, (b, k) in enumerate(zip(blocks, pieces)):
        rows, rem = divmod(b.shape[0], k)
        assert rem == 0 and (k == 1 or rows % 16 == 0), (b.shape, k)
        parts += [(i, None, None)] if k == 1 else [(i, j * rows, rows) for j in range(k)]
    np_ = len(parts)

    def body(*refs):
        x_refs, out_refs = refs[:n], refs[n + na:2 * n + na]
        send_sems, recv_sems, local_sems = refs[2 * n + na:]
        x, y, c = lax.axis_index("x"), lax.axis_index("y"), lax.axis_index("c")
        me, sibling = (x, y, c), (x, y, 1 - c)
        chips = [(1 - x, y), (x, 1 - y), (1 - x, 1 - y)]
        if collective_id is not None:
            _handshake([sibling] + [(*chip, c) for chip in chips])

        def slot(p, px, py, pc):
            i, r0, rows = parts[p]
            whole = out_refs[i].at[4 * px + 2 * py + pc]
            return whole if r0 is None else whole.at[pl.ds(r0, rows)]

        def own(p):
            i, r0, rows = parts[p]
            return x_refs[i] if r0 is None else x_refs[i].at[pl.ds(r0, rows)]

        def copy(p, k, blk, to, src=None):
            return pltpu.make_async_remote_copy(
                src_ref=slot(p, *blk) if src is None else src, dst_ref=slot(p, *blk),
                send_sem=send_sems.at[7 * p + k], recv_sem=recv_sems.at[7 * p + k], device_id=to, device_id_type=MESH)

        mine = [pltpu.make_async_copy(own(p), slot(p, *me), local_sems.at[p]) for p in range(np_)]
        for cp in mine:
            cp.start()
        first = []
        for p in range(np_):
            first.append(copy(p, 0, me, sibling, src=own(p)))
            first += [copy(p, 1 + j, me, (*chip, c), src=own(p)) for j, chip in enumerate(chips)]
        for cp in first:
            cp.start()
        passed = []
        for p in range(np_):
            for j, chip in enumerate(chips):
                copy(p, 1 + j, (*chip, c), me).wait_recv()
                passed.append(copy(p, 4 + j, (*chip, c), sibling))
                passed[-1].start()
        for p in range(np_):
            copy(p, 0, sibling, me).wait_recv()
            for j, chip in enumerate(chips):
                copy(p, 4 + j, (*chip, 1 - c), me).wait_recv()
        for cp in first + passed:
            cp.wait_send()
        for cp in mine:
            cp.wait()

    return _comm_call(
        body, name, list(blocks) + list(after), [jax.ShapeDtypeStruct((N_DEV,) + b.shape, b.dtype) for b in blocks],
        [pltpu.SemaphoreType.DMA((7 * np_,)), pltpu.SemaphoreType.DMA((7 * np_,)), pltpu.SemaphoreType.DMA((np_,))],
        collective_id)


def _sibling_swap(name, by_owner, collective_id=None, after=()):
    n = len(by_owner)
    na = len(after)

    def body(*refs):
        x_refs, out_refs = refs[:n], refs[n + na:2 * n + na]
        send_sems, recv_sems = refs[2 * n + na:]
        x, y, c = lax.axis_index("x"), lax.axis_index("y"), lax.axis_index("c")
        if collective_id is not None:
            _handshake([(x, y, 1 - c)])
        copies = []
        for i in range(n):
            for k in range(4):
                copies.append(pltpu.make_async_remote_copy(
                    src_ref=x_refs[i].at[2 * k + 1 - c], dst_ref=out_refs[i].at[k],
                    send_sem=send_sems.at[4 * i + k], recv_sem=recv_sems.at[4 * i + k],
                    device_id=(x, y, 1 - c), device_id_type=MESH))
        for cp in copies:
            cp.start()
        for cp in copies:
            cp.wait()

    return _comm_call(
        body, name, list(by_owner) + list(after),
        [jax.ShapeDtypeStruct((4,) + b.shape[1:], b.dtype) for b in by_owner],
        [pltpu.SemaphoreType.DMA((4 * n,)), pltpu.SemaphoreType.DMA((4 * n,))], collective_id)


def _chip_exchange(name, sums, collective_id=None, after=()):
    n = len(sums)
    na = len(after)

    def body(*refs):
        x_refs, out_refs = refs[:n], refs[n + na:2 * n + na]
        send_sems, recv_sems, local_sems = refs[2 * n + na:]
        x, y, c = lax.axis_index("x"), lax.axis_index("y"), lax.axis_index("c")
        chips = [(1 - x, y), (x, 1 - y), (1 - x, 1 - y)]
        my_chip = 2 * x + y
        if collective_id is not None:
            _handshake([(cx, cy, c) for cx, cy in chips])
        mine = [pltpu.make_async_copy(x_refs[i].at[my_chip], out_refs[i].at[my_chip], local_sems.at[i])
                for i in range(n)]
        for cp in mine:
            cp.start()
        sends = []
        for i in range(n):
            for j, (cx, cy) in enumerate(chips):
                sends.append(pltpu.make_async_remote_copy(
                    src_ref=x_refs[i].at[2 * cx + cy], dst_ref=out_refs[i].at[my_chip],
                    send_sem=send_sems.at[3 * i + j], recv_sem=recv_sems.at[3 * i + j],
                    device_id=(cx, cy, c), device_id_type=MESH))
        for cp in sends:
            cp.start()
        for i in range(n):
            for j, (cx, cy) in enumerate(chips):
                pltpu.make_async_remote_copy(
                    src_ref=x_refs[i].at[my_chip], dst_ref=out_refs[i].at[2 * cx + cy],
                    send_sem=send_sems.at[3 * i + j], recv_sem=recv_sems.at[3 * i + j],
                    device_id=(cx, cy, c), device_id_type=MESH).wait_recv()
        for cp in sends:
            cp.wait_send()
        for cp in mine:
            cp.wait()

    return _comm_call(
        body, name, list(sums) + list(after), [jax.ShapeDtypeStruct(s.shape, s.dtype) for s in sums],
        [pltpu.SemaphoreType.DMA((3 * n,)), pltpu.SemaphoreType.DMA((3 * n,)), pltpu.SemaphoreType.DMA((n,))],
        collective_id)


def _cast_shards(name, shards):
    n = len(shards)
    steps = min(_row_steps(s.shape[0]) for s in shards)

    def body(*refs):
        for i in range(n):
            refs[n + i][...] = refs[i][...].astype(BF16)

    blocks = [pl.BlockSpec((s.shape[0] // steps, s.shape[1]), lambda i: (i, 0)) for s in shards]
    return pl.pallas_call(
        body, name=name, grid=(steps,), in_specs=blocks, out_specs=blocks,
        out_shape=[jax.ShapeDtypeStruct(s.shape, BF16) for s in shards],
        compiler_params=_params(("parallel",)),
    )(*shards)


BIG = ("w_in", "w_branch_a", "w_branch_b", "w_out", "w_ffn_gate", "w_ffn_up", "w_ffn_down")


def _local_step(x, target, gains, low, conv_w, wg8, reduce):
    g_mix, g_hg, g_ffn, g_fin = gains
    w_in = wg8["w_in"].reshape(N_IN, D_MODEL)
    wg = wg8["w_ffn_gate"].reshape(D_FF, D_MODEL)
    wu = wg8["w_ffn_up"].reshape(D_FF, D_MODEL)
    wa, wb = wg8["w_branch_a"], wg8["w_branch_b"]
    wo = wg8["w_out"].reshape(D_MODEL, D_MODEL)
    wd = wg8["w_ffn_down"].reshape(D_FF, D_MODEL)

    ht, hg, cv, gt, cvo, cvot, cy, o, og, ogt, st = _fwd_in(x, g_mix, w_in, conv_w, low, g_hg)
    x1, mgt, ya, yb = _merge_fwd(og, cvo, gt, x, wa, wb, wo)
    h2t, gate, up, act, loss, d_gfin, dx2, dx2t = _ffn_fwd_loss(x1, g_ffn, wg, wu, wd, target, g_fin)

    dgate, dup, dx1, d_gffn = _ffn_bwd(dx2, x1, gate, up, g_ffn, wg, wu, wd)
    d_wg, d_wu, d_wd = _wgrad_ffn(h2t, dgate, dup, dx2t, act)
    by_owner_ffn = lambda a: a.reshape(N_DEV, D_FF // N_DEV, D_MODEL)
    ffn = dict(w_ffn_down=by_owner_ffn(d_wd), w_ffn_gate=by_owner_ffn(d_wg), w_ffn_up=by_owner_ffn(d_wu))
    dgt, dya, dyb, dog, dcv, d_conv = _merge_bwd(dx1, ya, yb, gt, cv, cy, wa, wb, wo, conv_w)
    sums_ffn, got_ffn = reduce.begin(ffn, sum_after=[dya])
    late = ("w_ffn_gate", "w_ffn_up")
    parts_ffn, updated_ffn = reduce.finish(ffn, sums_ffn, defer=late)
    grad_a, grad_b, grad_o = _wgrad_out_branches(ogt, dya, cvot, dyb, mgt, dx1, after=sums_ffn[:1])
    out = dict(w_out=grad_o.reshape(N_DEV, D_MODEL // N_DEV, D_MODEL), w_branch_a=grad_a, w_branch_b=grad_b)
    dhg, d_low, d_ghg = _hg_bwd(dog, hg, o, st, low, g_hg, after=list(sums_ffn) + [out["w_out"]])
    sums_out, got_out = reduce.begin(out, after=[parts_ffn[0], dhg], sum_after=updated_ffn)
    parts_out, updated_out = reduce.finish(out, sums_out)
    dparts = [dhg, dcv, dgt]
    d_w_in_t, ridden = _wgrad_in(ht, dparts, after=sums_out[:1],
                                 riders=reduce.riders(late, dict(zip(ffn, parts_ffn))))
    reduce.record(late, ridden)
    w_in_grad = dict(w_in=d_w_in_t.reshape(N_DEV, N_IN // N_DEV, D_MODEL))
    sums_in, _ = reduce.begin(w_in_grad, after=parts_out[:1], sum_after=updated_out)
    parts_in, _ = reduce.finish(w_in_grad, sums_in)
    grad_x, d_gmix = _in_bwd(dparts, w_in, x, dx1, g_mix, after=list(parts_out[:1]) + list(sums_in))
    small = dict(norm_mix_g=d_gmix, norm_ffn_g=d_gffn, norm_final_g=d_gfin, lower_bounds=d_low, hg_norm_g=d_ghg,
                 conv_w=d_conv, loss=loss)
    return grad_x, small, parts_in


def kernel(x, norm_mix_g, w_in, lower_bounds, hg_norm_g, conv_w, w_branch_a, w_branch_b, w_out, norm_ffn_g, w_ffn_gate, w_ffn_up, w_ffn_down, norm_final_g, loss_target, m_norm_mix_g, m_w_in, m_lower_bounds, m_hg_norm_g, m_conv_w, m_w_branch_a, m_w_branch_b, m_w_out, m_norm_ffn_g, m_w_ffn_gate, m_w_ffn_up, m_w_ffn_down, m_norm_final_g, v_norm_mix_g, v_w_in, v_lower_bounds, v_hg_norm_g, v_conv_w, v_w_branch_a, v_w_branch_b, v_w_out, v_norm_ffn_g, v_w_ffn_gate, v_w_ffn_up, v_w_ffn_down, v_norm_final_g):
    cx, cy, cc = lax.axis_index("x"), lax.axis_index("y"), lax.axis_index("c")
    my_dev = 4 * cx + 2 * cy + cc

    def tr(a):
        return a[0].T

    big = dict(w_in=tr(w_in), w_branch_a=w_branch_a[0], w_branch_b=w_branch_b[0], w_out=w_out[0],
               w_ffn_gate=tr(w_ffn_gate), w_ffn_up=tr(w_ffn_up), w_ffn_down=w_ffn_down[0])
    big_m = dict(w_in=tr(m_w_in), w_branch_a=m_w_branch_a[0], w_branch_b=m_w_branch_b[0], w_out=m_w_out[0],
                 w_ffn_gate=tr(m_w_ffn_gate), w_ffn_up=tr(m_w_ffn_up), w_ffn_down=m_w_ffn_down[0])
    big_v = dict(w_in=tr(v_w_in), w_branch_a=v_w_branch_a[0], w_branch_b=v_w_branch_b[0], w_out=v_w_out[0],
                 w_ffn_gate=tr(v_w_ffn_gate), w_ffn_up=tr(v_w_ffn_up), w_ffn_down=v_w_ffn_down[0])
    transposed = ("w_in", "w_ffn_gate", "w_ffn_up")

    ids = iter(range(1, 16))
    shards = dict(zip(BIG[:1], _cast_shards("cast_w_in", [big["w_in"]])))
    first = _all_gather("gather_w_in", [shards["w_in"], conv_w.transpose(1, 0, 2)], collective_id=next(ids),
                        pieces=[4, 1])
    shards.update(zip(BIG[1:], _cast_shards("cast_shards", [big[n] for n in BIG[1:]])))
    mid = _all_gather("gather_mid", [shards[n] for n in BIG[1:4]], collective_id=next(ids))
    ffn = _all_gather("gather_ffn", [shards[n] for n in BIG[4:]], collective_id=next(ids), pieces=[2, 2, 2])
    wg8 = dict(zip(BIG, [first[0]] + list(mid) + list(ffn)))
    conv_full = first[1].transpose(1, 2, 0, 3).reshape(3, CONV_WIDTH)

    core = cc.reshape(1).astype(jnp.int32)
    outs = {}

    class Reduce:
        @staticmethod
        def begin(grads, after=(), sum_after=()):
            names = list(grads)
            by_owner = [grads[n] for n in names]
            got = _sibling_swap("sibling_swap_" + names[0], by_owner, collective_id=next(ids), after=after)
            sums = _pair_sum("pair_sum_" + names[0], by_owner, got, core, after=sum_after)
            return sums, got

        @staticmethod
        def finish(grads, chip_sums, after=(), defer=()):
            names = list(grads)
            parts = _chip_exchange("chip_exchange_" + names[0], chip_sums, collective_id=next(ids), after=after)
            now = [n for n in names if n not in defer]
            updated = _adamw_sum("adamw_" + now[0], [big[n] for n in now],
                                 [p for n, p in zip(names, parts) if n in now],
                                 [big_m[n] for n in now], [big_v[n] for n in now])
            outs.update(zip(now, updated))
            return parts, [outs[n][1] for n in now]

        @staticmethod
        def riders(names, parts):
            return [(big[n], parts[n], big_m[n], big_v[n]) for n in names]

        @staticmethod
        def record(names, updated):
            outs.update(zip(names, updated))

    gains = (norm_mix_g, hg_norm_g, norm_ffn_g, norm_final_g.reshape(1, D_MODEL))
    grad_x, small, last = _local_step(x[0], loss_target[0], gains, lower_bounds, conv_full, wg8, Reduce)

    small_all = _all_gather("gather_small", [_small_pack(small)], collective_id=next(ids), after=last[:1])

    def small_state(a):
        return [a[0], a[1], a[2].reshape(1, D_MODEL), a[3], a[4], a[5].transpose(1, 0, 2)]

    upd = _small_update(
        small_all[0], my_dev.reshape(1).astype(jnp.int32),
        small_state((norm_mix_g, norm_ffn_g, norm_final_g, lower_bounds, hg_norm_g, conv_w)),
        small_state((m_norm_mix_g, m_norm_ffn_g, m_norm_final_g, m_lower_bounds, m_hg_norm_g, m_conv_w)),
        small_state((v_norm_mix_g, v_norm_ffn_g, v_norm_final_g, v_lower_bounds, v_hg_norm_g, v_conv_w)))
    loss = upd[0][0, 0]
    for p, (name, _, _, _) in enumerate(_SMALL_PARAMS):
        outs[name] = upd[1 + 4 * p:5 + 4 * p]
    outs["norm_final_g"] = [a.reshape(D_MODEL) for a in outs["norm_final_g"]]
    outs["conv_w"] = [a.transpose(1, 0, 2) for a in outs["conv_w"]]

    order = ["norm_mix_g", "w_in", "lower_bounds", "hg_norm_g", "conv_w", "w_branch_a", "w_branch_b", "w_out",
             "norm_ffn_g", "w_ffn_gate", "w_ffn_up", "w_ffn_down", "norm_final_g"]
    result = [loss, grad_x[None]]
    for k in range(4):
        for n in order:
            if n in BIG:
                result.append((outs[n][k].T if n in transposed else outs[n][k])[None])
            else:
                result.append(outs[n][k])
    return tuple(result)
```

```python
import jax
import jax.numpy as jnp
from jax import lax
from jax.experimental import pallas as pl
from jax.experimental.pallas import tpu as pltpu
from jax.experimental.pallas import tpu_sc as plsc

F32 = jnp.float32
BF16 = jnp.bfloat16
STASH = jnp.bfloat16

D_MODEL = 1024
HG_WIDTH = 512
HEAD_DIM = 128
N_HEADS = 4
HEADS_PER_STEP = 4
HEAD_GROUPS = N_HEADS // HEADS_PER_STEP
CONV_WIDTH = 512
CONV_K = 3
D_FF = 2816
CHUNK = 32
EPS = 1e-6
Q_SCALE = HEAD_DIM ** -0.5
N_DEV = 8

ADAM_LR = 0.001
ADAM_B1 = 0.9
ADAM_B2 = 0.999
ADAM_EPS = 1e-08
ADAM_WD = 0.01
ADAM_STEP = 10

VMEM_LIMIT_V7X = 56 * 1024 * 1024
VMEM_LIMIT_LARGE_V7X = 62 * 1024 * 1024

SMALL_ROWS = 16


def _params(sem, vmem=VMEM_LIMIT_V7X):
    return pltpu.CompilerParams(dimension_semantics=sem, vmem_limit_bytes=vmem)


def _mm(a, b):
    return jnp.dot(a.astype(BF16), b.astype(BF16), preferred_element_type=F32)


def _mm_nt(a, b):
    return lax.dot_general(a.astype(BF16), b.astype(BF16), (((1,), (1,)), ((), ())), preferred_element_type=F32)


def _mm_tn(a, b):
    return lax.dot_general(a.astype(BF16), b.astype(BF16), (((0,), (0,)), ((), ())), preferred_element_type=F32)


def _sigmoid(x):
    return 0.5 * jnp.tanh(0.5 * x) + 0.5


def _resident(shape):
    nd = len(shape)
    return pl.BlockSpec(shape, lambda *_: (0,) * nd, pipeline_mode=pl.Buffered(1))


def _full(shape):
    nd = len(shape)
    return pl.BlockSpec(shape, lambda *_: (0,) * nd)


def _shard_cols(w_ref):
    return jnp.concatenate([w_ref[s] for s in range(N_DEV)], axis=1)


N_HG = 4 * HG_WIDTH
N_CV = 3 * CONV_WIDTH
N_GT = 2 * D_MODEL
N_IN = N_HG + N_CV + N_GT


def _col(tm, n):
    return pl.BlockSpec((n, tm), lambda i: (0, i))


HALO = 8


def _fwd_in(x, g, w_in_t, conv_w, low, gn):
    T = x.shape[0]
    tm = min(512, T)
    nc = tm // CHUNK

    def body(x_ref, g_ref, w_ref, cw_ref, low_ref, gn_ref, ht_ref, hg_ref, cv_ref, gt_ref, cvo_ref, cvot_ref, cy_ref,
             o_ref, og_ref, ogt_ref, st_ref, tail_scr, s_scr):
        @pl.when(pl.program_id(0) == 0)
        def _():
            tail_scr[...] = jnp.zeros_like(tail_scr)
            s_scr[...] = jnp.zeros_like(s_scr)

        xv = x_ref[...]
        r = lax.rsqrt(jnp.mean(xv * xv, axis=-1, keepdims=True) + EPS)
        hf = xv * r * g_ref[...]
        h = hf.astype(BF16)
        ht_ref[...] = hf.T.astype(BF16)
        hg_ref[...] = _mm_nt(h, w_ref[:N_HG, :])
        cv = _mm_nt(h, w_ref[N_HG:N_HG + N_CV, :])
        cv_ref[...] = cv.astype(STASH)
        gt_ref[...] = _mm_nt(h, w_ref[N_HG + N_CV:, :]).astype(STASH)

        u = cv[:, :CONV_WIDTH] * cv[:, 2 * CONV_WIDTH:]
        row = lax.broadcasted_iota(jnp.int32, u.shape, 0)
        prev1 = tail_scr[HALO - 1:HALO, :]
        prev2 = tail_scr[HALO - 2:HALO - 1, :]
        u1 = jnp.where(row >= 1, pltpu.roll(u, 1, 0), prev1)
        u2 = jnp.where(row >= 2, pltpu.roll(u, 2, 0), jnp.where(row == 1, prev1, prev2))
        y = cw_ref[0:1, :] * u2 + cw_ref[1:2, :] * u1 + cw_ref[2:3, :] * u
        cy_ref[...] = y.astype(STASH)
        out = cv[:, CONV_WIDTH:2 * CONV_WIDTH] * y
        cvo_ref[...] = out.astype(BF16)
        cvot_ref[...] = out.T.astype(BF16)
        tail_scr[...] = u[tm - HALO:, :]

        _hg_fwd_tile(hg_ref, low_ref, gn_ref, o_ref, og_ref, ogt_ref, st_ref, s_scr, tm)

    row = lambda n: pl.BlockSpec((tm, n), lambda i: (i, 0))
    return pl.pallas_call(
        body, name="fwd_in", grid=(T // tm,),
        in_specs=[row(D_MODEL), _full((1, D_MODEL)), _resident(w_in_t.shape), _full((CONV_K, CONV_WIDTH)),
                  _full((2, HG_WIDTH)), _full((1, HEAD_DIM))],
        out_specs=[_col(tm, D_MODEL), row(N_HG), row(N_CV), row(N_GT), row(CONV_WIDTH), _col(tm, CONV_WIDTH),
                   row(CONV_WIDTH),
                   row(HG_WIDTH), row(HG_WIDTH), _col(tm, HG_WIDTH),
                   pl.BlockSpec((N_HEADS, nc, HEAD_DIM, HEAD_DIM), lambda i: (0, i, 0, 0))],
        out_shape=[jax.ShapeDtypeStruct((D_MODEL, T), BF16), jax.ShapeDtypeStruct((T, N_HG), F32),
                   jax.ShapeDtypeStruct((T, N_CV), STASH), jax.ShapeDtypeStruct((T, N_GT), STASH),
                   jax.ShapeDtypeStruct((T, CONV_WIDTH), BF16), jax.ShapeDtypeStruct((CONV_WIDTH, T), BF16),
                   jax.ShapeDtypeStruct((T, CONV_WIDTH), STASH), jax.ShapeDtypeStruct((T, HG_WIDTH), F32), jax.ShapeDtypeStruct((T, HG_WIDTH), BF16),
                   jax.ShapeDtypeStruct((HG_WIDTH, T), BF16),
                   jax.ShapeDtypeStruct((N_HEADS, T // CHUNK, HEAD_DIM, HEAD_DIM), F32)],
        scratch_shapes=[pltpu.VMEM((HALO, CONV_WIDTH), F32), pltpu.VMEM((N_HEADS, HEAD_DIM, HEAD_DIM), F32)],
        compiler_params=_params(("arbitrary",), vmem=VMEM_LIMIT_LARGE_V7X),
    )(x, g, w_in_t, conv_w, low, gn)


def _chunk_pos(shape):
    return lax.broadcasted_iota(jnp.int32, shape, 0) & (CHUNK - 1)


def _chunk_cumsum(x, pos):
    s = 1
    while s < CHUNK:
        x = x + jnp.where(pos >= s, pltpu.roll(x, s, 0), 0.0)
        s *= 2
    return x


def _chunk_rev_cumsum(x, pos):
    n = x.shape[0]
    s = 1
    while s < CHUNK:
        x = x + jnp.where(pos + s < CHUNK, pltpu.roll(x, n - s, 0), 0.0)
        s *= 2
    return x


def _lower_bound(low_ref):
    l0 = low_ref[0:1, :]
    l1 = low_ref[1:2, :]
    m = jnp.maximum(l0, l1)
    e0 = jnp.exp(l0 - m)
    e1 = jnp.exp(l1 - m)
    return e0 / (e0 + e1), e1 / (e0 + e1)


def _hg_gates(qr, fr, lb, pos, tb):
    sq = _sigmoid(qr)
    q = qr * sq * Q_SCALE
    sg = _sigmoid(fr)
    f = lb + (1.0 - lb) * sg
    k = 1.0 - f
    b = _chunk_cumsum(jnp.log(f), pos)
    b3 = b.reshape(tb // CHUNK, CHUNK, HEAD_DIM)
    anc = b3[:, CHUNK // 2 - 1:CHUNK // 2, :]
    last = b3[:, CHUNK - 1:CHUNK, :]
    d3 = b3 - anc
    e_qa3 = jnp.exp(d3)
    e_ka3 = jnp.exp(-d3)
    e_b3 = e_qa3 * jnp.exp(anc)
    e_ko3 = e_ka3 * jnp.exp(last - anc)
    dec = jnp.exp(last)
    flat = lambda a: a.reshape(tb, HEAD_DIM)
    return sq, q, sg, f, k, flat(e_qa3), flat(e_ka3), flat(e_b3), flat(e_ko3), dec


def _intra_mask(sb):
    r = lax.broadcasted_iota(jnp.int32, (sb, sb), 0)
    c = lax.broadcasted_iota(jnp.int32, (sb, sb), 1)
    return ((r // CHUNK) == (c // CHUNK)) & (c <= r)


def _hg_fwd_tile(hg_ref, low_ref, gn_ref, o_ref, og_ref, ogt_ref, st_ref, s_scr, tb):
    sb = min(256, tb)
    nc = tb // CHUNK
    q_ref, f_ref, i_ref, g_ref = (hg_ref.at[:, p * HG_WIDTH:(p + 1) * HG_WIDTH] for p in range(4))
    pos = _chunk_pos((tb, HEAD_DIM))
    mask = _intra_mask(sb)
    lanes = [slice(hh * HEAD_DIM, (hh + 1) * HEAD_DIM) for hh in range(N_HEADS)]
    qi, ko, vb, dec, st = [], [], [], [], []
    for hh, ln in enumerate(lanes):
        lb, _ = _lower_bound(low_ref.at[:, ln])
        _, q, _, _, k, e_qa, e_ka, e_b, e_ko, dec_h = _hg_gates(q_ref[:, ln], f_ref[:, ln], lb, pos, tb)
        qh = (q * e_qa).astype(BF16)
        kh = (k * e_ka).astype(BF16)
        qi.append((q * e_b).astype(BF16))
        ko.append((k * e_ko).astype(BF16))
        vb.append(i_ref[:, ln].astype(BF16))
        dec.append(dec_h)
        st.append(s_scr[hh])
        for s in range(tb // sb):
            sl = slice(s * sb, (s + 1) * sb)
            p = jnp.where(mask, _mm_nt(qh[sl], kh[sl]), 0.0)
            o_ref[sl, ln] = _mm(p, vb[hh][sl])
    for c in range(nc):
        sl = slice(c * CHUNK, (c + 1) * CHUNK)
        for hh, ln in enumerate(lanes):
            st_ref[hh, c] = st[hh]
            o_ref[sl, ln] = o_ref[sl, ln] + _mm_nt(qi[hh][sl], st[hh])
            st[hh] = dec[hh][c] * st[hh] + _mm_tn(vb[hh][sl], ko[hh][sl])
    for hh, ln in enumerate(lanes):
        s_scr[hh] = st[hh]
        o = o_ref[:, ln]
        r = lax.rsqrt(jnp.mean(o * o, axis=-1, keepdims=True) + EPS)
        gr = g_ref[:, ln]
        og = (o * r * gn_ref[...]) * (gr * _sigmoid(gr))
        og_ref[:, ln] = og.astype(BF16)
        ogt_ref[ln, :] = og.T.astype(BF16)


def _merge_fwd(og, cvo, gt, x, wa, wb, wo):
    T = x.shape[0]
    tm = min(1024, T)

    def body(og_ref, cvo_ref, gt_ref, x_ref, wa_ref, wb_ref, wo_ref, x1_ref, mgt_ref, ya_ref, yb_ref):
        ya = jnp.dot(og_ref[...], _shard_cols(wa_ref), preferred_element_type=F32)
        yb = jnp.dot(cvo_ref[...], _shard_cols(wb_ref), preferred_element_type=F32)
        ya_ref[...] = ya.astype(STASH)
        yb_ref[...] = yb.astype(STASH)
        m = (_sigmoid(gt_ref[:, :D_MODEL].astype(F32)) * ya
             + _sigmoid(gt_ref[:, D_MODEL:].astype(F32)) * yb)
        mgt_ref[...] = m.T.astype(BF16)
        x1_ref[...] = x_ref[...] + jnp.dot(m.astype(BF16), wo_ref[...], preferred_element_type=F32)

    row = lambda n: pl.BlockSpec((tm, n), lambda i: (i, 0))
    return pl.pallas_call(
        body, name="merge_fwd", grid=(T // tm,),
        in_specs=[row(HG_WIDTH), row(CONV_WIDTH), row(2 * D_MODEL), row(D_MODEL),
                  _resident(wa.shape), _resident(wb.shape), _resident(wo.shape)],
        out_specs=[row(D_MODEL), _col(tm, D_MODEL), row(D_MODEL), row(D_MODEL)],
        out_shape=[jax.ShapeDtypeStruct((T, D_MODEL), F32), jax.ShapeDtypeStruct((D_MODEL, T), BF16),
                   jax.ShapeDtypeStruct((T, D_MODEL), STASH), jax.ShapeDtypeStruct((T, D_MODEL), STASH)],
        compiler_params=_params(("parallel",)),
    )(og, cvo, gt, x, wa, wb, wo)


def _ffn_fwd_loss(x1, g, wg, wu, wd, target, g_fin):
    T = x1.shape[0]
    tm = min(512, T)

    def body(x_ref, g_ref, wg_ref, wu_ref, wd_ref, t_ref, gf_ref,
             ht_ref, gate_ref, up_ref, act_ref, loss_ref, dgf_ref, dx2_ref, dx2t_ref):
        @pl.when(pl.program_id(0) == 0)
        def _():
            loss_ref[...] = jnp.zeros_like(loss_ref)
            dgf_ref[...] = jnp.zeros_like(dgf_ref)

        xv = x_ref[...]
        r = lax.rsqrt(jnp.mean(xv * xv, axis=-1, keepdims=True) + EPS)
        hf = xv * r * g_ref[...]
        h = hf.astype(BF16)
        ht_ref[...] = hf.T.astype(BF16)
        gate = _mm_nt(h, wg_ref[...])
        up = _mm_nt(h, wu_ref[...])
        gate_ref[...] = gate.astype(STASH)
        up_ref[...] = up.astype(STASH)
        act = (gate * _sigmoid(gate) * up).astype(BF16)
        act_ref[...] = act
        x2 = xv + jnp.dot(act, wd_ref[...], preferred_element_type=F32)

        gv = gf_ref[...]
        r2 = lax.rsqrt(jnp.mean(x2 * x2, axis=-1, keepdims=True) + EPS)
        xh = x2 * r2
        err = xh * gv - t_ref[...]
        loss_ref[...] += 0.5 * jnp.sum(jnp.mean(err * err, axis=-1, keepdims=True), axis=0, keepdims=True)
        dy = err * (1.0 / D_MODEL)
        dgf_ref[...] += jnp.sum(dy * xh, axis=0, keepdims=True)
        w = dy * gv
        dx2 = r2 * (w - xh * jnp.mean(w * xh, axis=-1, keepdims=True))
        dx2_ref[...] = dx2
        dx2t_ref[...] = dx2.T.astype(BF16)

    row = lambda n: pl.BlockSpec((tm, n), lambda i: (i, 0))
    return pl.pallas_call(
        body, name="ffn_fwd_loss", grid=(T // tm,),
        in_specs=[row(D_MODEL), _full((1, D_MODEL)), _resident(wg.shape), _resident(wu.shape), _resident(wd.shape),
                  row(D_MODEL), _full((1, D_MODEL))],
        out_specs=[_col(tm, D_MODEL), row(D_FF), row(D_FF), row(D_FF), _full((1, 128)), _full((1, D_MODEL)),
                   row(D_MODEL), _col(tm, D_MODEL)],
        out_shape=[jax.ShapeDtypeStruct((D_MODEL, T), BF16), jax.ShapeDtypeStruct((T, D_FF), STASH),
                   jax.ShapeDtypeStruct((T, D_FF), STASH), jax.ShapeDtypeStruct((T, D_FF), BF16),
                   jax.ShapeDtypeStruct((1, 128), F32), jax.ShapeDtypeStruct((1, D_MODEL), F32),
                   jax.ShapeDtypeStruct((T, D_MODEL), F32), jax.ShapeDtypeStruct((D_MODEL, T), BF16)],
        compiler_params=_params(("arbitrary",), vmem=VMEM_LIMIT_LARGE_V7X),
    )(x1, g, wg, wu, wd, target, g_fin)


def _ffn_bwd(dx2, x1, gate, up, g, wg, wu, wd):
    T = x1.shape[0]
    tm = min(512, T)

    def body(dx2_ref, x_ref, gate_ref, up_ref, g_ref, wg_ref, wu_ref, wd_ref, dgate_ref, dup_ref, dx1_ref, dgn_ref):
        @pl.when(pl.program_id(0) == 0)
        def _():
            dgn_ref[...] = jnp.zeros_like(dgn_ref)

        dx2 = dx2_ref[...]
        dact = _mm_nt(dx2, wd_ref[...])
        gate = gate_ref[...].astype(F32)
        s = _sigmoid(gate)
        dgate = (dact * up_ref[...].astype(F32) * (s * (1.0 + gate * (1.0 - s)))).astype(BF16)
        dup = (dact * (gate * s)).astype(BF16)
        dgate_ref[...] = dgate
        dup_ref[...] = dup
        dh = _mm(dgate, wg_ref[...]) + _mm(dup, wu_ref[...])
        xv = x_ref[...]
        r = lax.rsqrt(jnp.mean(xv * xv, axis=-1, keepdims=True) + EPS)
        xh = xv * r
        dgn_ref[...] += jnp.sum(dh * xh, axis=0, keepdims=True)
        w = dh * g_ref[...]
        dx1_ref[...] = dx2 + r * (w - xh * jnp.mean(w * xh, axis=-1, keepdims=True))

    row = lambda n: pl.BlockSpec((tm, n), lambda i: (i, 0))
    return pl.pallas_call(
        body, name="ffn_bwd", grid=(T // tm,),
        in_specs=[row(D_MODEL), row(D_MODEL), row(D_FF), row(D_FF), _full((1, D_MODEL)),
                  _resident(wg.shape), _resident(wu.shape), _resident(wd.shape)],
        out_specs=[row(D_FF), row(D_FF), row(D_MODEL), _full((1, D_MODEL))],
        out_shape=[jax.ShapeDtypeStruct((T, D_FF), BF16), jax.ShapeDtypeStruct((T, D_FF), BF16),
                   jax.ShapeDtypeStruct((T, D_MODEL), F32), jax.ShapeDtypeStruct((1, D_MODEL), F32)],
        compiler_params=_params(("arbitrary",), vmem=VMEM_LIMIT_LARGE_V7X),
    )(dx2, x1, gate, up, g, wg, wu, wd)


def _merge_bwd(dx1, ya, yb, gt, cv, cy, wa, wb, wo, conv_w):
    T = dx1.shape[0]
    tm = min(512, T)
    nt = T // tm

    def body(dx_ref, ya_ref, yb_ref, gt_ref, cv_ref, cy_ref, wa_ref, wb_ref, wo_ref, cw_ref,
             dgt_ref, dya_ref, dyb_ref, dog_ref, dcv_ref, dcw_ref, next_dy):
        @pl.when(pl.program_id(0) == 0)
        def _():
            next_dy[...] = jnp.zeros_like(next_dy)
            dcw_ref[...] = jnp.zeros_like(dcw_ref)

        dm = _mm_nt(dx_ref[...], wo_ref[...])
        wa = _shard_cols(wa_ref)
        wb = _shard_cols(wb_ref)
        ya = ya_ref[...].astype(F32)
        yb = yb_ref[...].astype(F32)
        sa = _sigmoid(gt_ref[:, :D_MODEL].astype(F32))
        sb = _sigmoid(gt_ref[:, D_MODEL:].astype(F32))
        da = dm * sa
        db = dm * sb
        dgt_ref[:, :D_MODEL] = (da * ya * (1.0 - sa)).astype(BF16)
        dgt_ref[:, D_MODEL:] = (db * yb * (1.0 - sb)).astype(BF16)
        dya = da.astype(BF16)
        dyb = db.astype(BF16)
        dya_ref[...] = dya
        dyb_ref[...] = dyb
        dog_ref[...] = _mm_nt(dya, wa)
        dcvo = _mm_nt(dyb, wb)

        cvt = cv_ref[...].astype(F32)
        c, bg, xb = cvt[:, :CONV_WIDTH], cvt[:, CONV_WIDTH:2 * CONV_WIDTH], cvt[:, 2 * CONV_WIDTH:]
        u = c * xb
        row = lax.broadcasted_iota(jnp.int32, u.shape, 0)
        w0, w1, w2 = cw_ref[0:1, :], cw_ref[1:2, :], cw_ref[2:3, :]
        dcv_ref[:, CONV_WIDTH:2 * CONV_WIDTH] = (dcvo * cy_ref[...].astype(F32)).astype(BF16)
        dy = dcvo * bg
        n1 = next_dy[0:1, :]
        n2 = next_dy[1:2, :]
        dy1 = jnp.where(row < tm - 1, pltpu.roll(dy, tm - 1, 0), n1)
        dy2 = jnp.where(row < tm - 2, pltpu.roll(dy, tm - 2, 0), jnp.where(row == tm - 2, n1, n2))
        dcw_ref[0:1, :] += jnp.sum(dy2 * u, axis=0, keepdims=True)
        dcw_ref[1:2, :] += jnp.sum(dy1 * u, axis=0, keepdims=True)
        dcw_ref[2:3, :] += jnp.sum(dy * u, axis=0, keepdims=True)
        du = w2 * dy + w1 * dy1 + w0 * dy2
        dcv_ref[:, :CONV_WIDTH] = (du * xb).astype(BF16)
        dcv_ref[:, 2 * CONV_WIDTH:] = (du * c).astype(BF16)
        next_dy[...] = dy[:HALO, :]

    rt = lambda i: nt - 1 - i
    row = lambda n: pl.BlockSpec((tm, n), lambda i: (rt(i), 0))
    return pl.pallas_call(
        body, name="merge_bwd", grid=(nt,),
        in_specs=[row(D_MODEL), row(D_MODEL), row(D_MODEL), row(2 * D_MODEL), row(N_CV), row(CONV_WIDTH),
                  _resident(wa.shape), _resident(wb.shape), _resident(wo.shape), _full((CONV_K, CONV_WIDTH))],
        out_specs=[row(2 * D_MODEL), row(D_MODEL), row(D_MODEL), row(HG_WIDTH), row(N_CV),
                   _full((CONV_K, CONV_WIDTH))],
        out_shape=[jax.ShapeDtypeStruct((T, 2 * D_MODEL), BF16), jax.ShapeDtypeStruct((T, D_MODEL), BF16),
                   jax.ShapeDtypeStruct((T, D_MODEL), BF16), jax.ShapeDtypeStruct((T, HG_WIDTH), F32),
                   jax.ShapeDtypeStruct((T, N_CV), BF16), jax.ShapeDtypeStruct((CONV_K, CONV_WIDTH), F32)],
        scratch_shapes=[pltpu.VMEM((HALO, CONV_WIDTH), F32)],
        compiler_params=_params(("arbitrary",)),
    )(dx1, ya, yb, gt, cv, cy, wa, wb, wo, conv_w)


def _drop_operands(body, first, count):
    def wrapped(*refs):
        return body(*refs[:first], *refs[first + count:])
    return wrapped


def _hg_bwd(dog, hg, o, st, low, gn, after=()):
    T = hg.shape[0]
    tb = min(512, T)
    sb = min(256, tb)
    nb = T // tb
    nc = tb // CHUNK
    wid = HEADS_PER_STEP * HEAD_DIM

    def body(q_ref, f_ref, i_ref, g_ref, low_ref, gn_ref, o_ref, dog_ref, st_ref,
             dhg_ref, dlow_ref, dgn_ref,
             ds_scr, dqi_scr, dko_scr, dv_scr, dd_scr, dqh_scr, dkh_scr):
        h = pl.program_id(0)
        t = pl.program_id(1)
        dq_ref, df_ref, di_ref, dg_ref = (dhg_ref.at[:, p * HG_WIDTH:(p + 1) * HG_WIDTH] for p in range(4))

        @pl.when(t == 0)
        def _():
            ds_scr[...] = jnp.zeros_like(ds_scr)
            dlow_ref[...] = jnp.zeros_like(dlow_ref)

        @pl.when((t == 0) & (h == 0))
        def _():
            dgn_ref[...] = jnp.zeros_like(dgn_ref)

        pos = _chunk_pos((tb, HEAD_DIM))
        mask = _intra_mask(sb)
        gnv = gn_ref[...]
        lanes = [slice(hh * HEAD_DIM, (hh + 1) * HEAD_DIM) for hh in range(HEADS_PER_STEP)]
        heads = []
        for hh, ln in enumerate(lanes):
            lb, lb1 = _lower_bound(low_ref.at[:, ln])
            qr = q_ref[:, ln]
            sq, q, sg, f, k, e_qa, e_ka, e_b, e_ko, dec = _hg_gates(qr, f_ref[:, ln], lb, pos, tb)

            gr = g_ref[:, ln]
            o = o_ref[:, ln]
            dog_v = dog_ref[:, ln]
            sgr = _sigmoid(gr)
            r = lax.rsqrt(jnp.mean(o * o, axis=-1, keepdims=True) + EPS)
            oh = o * r
            dg_ref[:, ln] = (dog_v * (oh * gnv) * (sgr * (1.0 + gr * (1.0 - sgr)))).astype(BF16)
            don = dog_v * (gr * sgr)
            dgn_ref[...] += jnp.sum(don * oh, axis=0, keepdims=True)
            w = don * gnv
            do = (r * (w - oh * jnp.mean(w * oh, axis=-1, keepdims=True))).astype(BF16)

            qh = (q * e_qa).astype(BF16)
            kh = (k * e_ka).astype(BF16)
            qi = (q * e_b).astype(BF16)
            ko = (k * e_ko).astype(BF16)
            vb = i_ref[:, ln].astype(BF16)

            for s in range(tb // sb):
                sl = slice(s * sb, (s + 1) * sb)
                p = jnp.where(mask, _mm_nt(qh[sl], kh[sl]), 0.0).astype(BF16)
                dp = jnp.where(mask, _mm_nt(do[sl], vb[sl]), 0.0).astype(BF16)
                dv_scr[sl, ln] = _mm_tn(p, do[sl])
                dqh_scr[sl, ln] = _mm(dp, kh[sl])
                dkh_scr[sl, ln] = _mm_tn(dp, qh[sl])
            heads.append(dict(lb=lb, lb1=lb1, qr=qr, sq=sq, q=q, sg=sg, f=f, k=k, e_qa=e_qa, e_ka=e_ka, e_b=e_b,
                              e_ko=e_ko, dec=dec, do=do, qi=qi, ko=ko, vb=vb, ds=ds_scr[hh]))

        for c in reversed(range(nc)):
            sl = slice(c * CHUNK, (c + 1) * CHUNK)
            for hh, ln in enumerate(lanes):
                hd = heads[hh]
                ds = hd["ds"]
                st_c = st_ref[hh, c]
                dqi_scr[sl, ln] = _mm(hd["do"][sl], st_c)
                dko_scr[sl, ln] = _mm(hd["vb"][sl], ds)
                dv_scr[sl, ln] = dv_scr[sl, ln] + _mm_nt(hd["ko"][sl], ds)
                dec_c = hd["dec"][c]
                dd_scr[sl, ln] = jnp.broadcast_to(dec_c * jnp.sum(ds * st_c, axis=0, keepdims=True),
                                                  (CHUNK, HEAD_DIM))
                hd["ds"] = dec_c * ds + _mm_tn(hd["do"][sl], hd["qi"][sl])

        for hh, ln in enumerate(lanes):
            hd = heads[hh]
            ds_scr[hh] = hd["ds"]
            q, k, lb = hd["q"], hd["k"], hd["lb"]
            dko_e = dko_scr[:, ln] * hd["e_ko"]
            dq = dqh_scr[:, ln] * hd["e_qa"] + dqi_scr[:, ln] * hd["e_b"]
            dk = dkh_scr[:, ln] * hd["e_ka"] + dko_e
            kd3 = (k * dko_e).reshape(nc, CHUNK, HEAD_DIM)
            last = jnp.broadcast_to(jnp.sum(kd3, axis=1, keepdims=True), kd3.shape).reshape(tb, HEAD_DIM)
            db = q * dq - k * dk + jnp.where(pos == CHUNK - 1, dd_scr[:, ln] + last, 0.0)
            dlg = _chunk_rev_cumsum(db, pos)
            dfv = dlg / hd["f"] - dk
            s_low = jnp.sum(dfv * (1.0 - hd["sg"]), axis=0, keepdims=True)
            dlow_ref[0:1, ln] += s_low * lb * (1.0 - lb)
            dlow_ref[1:2, ln] += -s_low * lb * hd["lb1"]
            df_ref[:, ln] = (dfv * (1.0 - lb) * hd["sg"] * (1.0 - hd["sg"])).astype(BF16)
            dq_ref[:, ln] = (dq * Q_SCALE * (hd["sq"] * (1.0 + hd["qr"] * (1.0 - hd["sq"])))).astype(BF16)
            di_ref[:, ln] = dv_scr[:, ln].astype(BF16)

    rt = lambda t: nb - 1 - t
    col = lambda p: pl.BlockSpec((tb, wid), lambda h, t: (rt(t), p * HEAD_GROUPS + h))
    hcol = pl.BlockSpec((tb, wid), lambda h, t: (rt(t), h))
    assert HEAD_GROUPS == 1
    tile = pltpu.VMEM((tb, wid), F32)
    return pl.pallas_call(
        _drop_operands(body, 9, len(after)), name="hg_bwd", grid=(HEAD_GROUPS, nb),
        in_specs=[col(0), col(1), col(2), col(3), pl.BlockSpec((2, wid), lambda h, t: (0, h)),
                  pl.BlockSpec((1, HEAD_DIM), lambda h, t: (0, 0)), hcol, hcol,
                  pl.BlockSpec((HEADS_PER_STEP, nc, HEAD_DIM, HEAD_DIM), lambda h, t: (h, rt(t), 0, 0))]
                 + [HBM_SPEC] * len(after),
        out_specs=[pl.BlockSpec((tb, N_HG), lambda h, t: (rt(t), 0)), pl.BlockSpec((2, wid), lambda h, t: (0, h)),
                   pl.BlockSpec((1, HEAD_DIM), lambda h, t: (0, 0))],
        out_shape=[jax.ShapeDtypeStruct((T, N_HG), BF16), jax.ShapeDtypeStruct((2, HG_WIDTH), F32),
                   jax.ShapeDtypeStruct((1, HEAD_DIM), F32)],
        scratch_shapes=[pltpu.VMEM((HEADS_PER_STEP, HEAD_DIM, HEAD_DIM), F32), tile, tile, tile, tile, tile, tile],
        compiler_params=_params(("arbitrary", "arbitrary")),
    )(hg, hg, hg, hg, low, gn, o, dog, st, *after)


def _in_bwd(dparts, w_in, x, dx1, g, after=()):
    T = x.shape[0]
    tm = min(512, T)
    widths = [p.shape[1] for p in dparts]
    offs = [sum(widths[:i]) for i in range(len(widths))]
    n = len(dparts)

    def body(*refs):
        d_refs = refs[:n]
        w_ref, x_ref, dx1_ref, g_ref, dx_ref, dgn_ref = refs[n:]

        @pl.when(pl.program_id(0) == 0)
        def _():
            dgn_ref[...] = jnp.zeros_like(dgn_ref)

        dh = None
        for d_ref, off, wd in zip(d_refs, offs, widths):
            part = _mm(d_ref[...], w_ref[off:off + wd, :])
            dh = part if dh is None else dh + part
        xv = x_ref[...]
        r = lax.rsqrt(jnp.mean(xv * xv, axis=-1, keepdims=True) + EPS)
        xh = xv * r
        dgn_ref[...] += jnp.sum(dh * xh, axis=0, keepdims=True)
        w = dh * g_ref[...]
        dx_ref[...] = dx1_ref[...] + r * (w - xh * jnp.mean(w * xh, axis=-1, keepdims=True))

    row = lambda m: pl.BlockSpec((tm, m), lambda i: (i, 0))
    return pl.pallas_call(
        _drop_operands(body, n + 4, len(after)), name="in_bwd", grid=(T // tm,),
        in_specs=[row(wd) for wd in widths] + [_resident(w_in.shape), row(D_MODEL), row(D_MODEL), _full((1, D_MODEL))]
                 + [HBM_SPEC] * len(after),
        out_specs=[row(D_MODEL), _full((1, D_MODEL))],
        out_shape=[jax.ShapeDtypeStruct((T, D_MODEL), F32), jax.ShapeDtypeStruct((1, D_MODEL), F32)],
        compiler_params=_params(("arbitrary",)),
    )(*dparts, w_in, x, dx1, g, *after)


def _wgrad_ffn(h2t, dgate, dup, dx2t, act, tn=256):
    M, T = h2t.shape
    N = dgate.shape[1]

    def body(h_ref, x_hbm, dg_ref, du_ref, act_ref, og_ref, ou_ref, od_ref, x_scr, sem):
        first = pl.program_id(0) == 0
        fetch = pltpu.make_async_copy(x_hbm, x_scr, sem)

        @pl.when(first)
        def _():
            fetch.start()

        h = h_ref[...]
        og_ref[...] = _mm(h, dg_ref[...]).T.astype(BF16)
        ou_ref[...] = _mm(h, du_ref[...]).T.astype(BF16)

        @pl.when(first)
        def _():
            fetch.wait()

        od_ref[...] = _mm(x_scr[...], act_ref[...]).T.astype(BF16)

    rhs = pl.BlockSpec((T, tn), lambda j: (0, j))
    out_spec = pl.BlockSpec((tn, M), lambda j: (j, 0))
    out = jax.ShapeDtypeStruct((N, M), BF16)
    return pl.pallas_call(
        body, name="wgrad_ffn", grid=(N // tn,),
        in_specs=[_resident((M, T)), HBM_SPEC, rhs, rhs, rhs], out_specs=[out_spec] * 3,
        out_shape=[out, out, out],
        scratch_shapes=[pltpu.VMEM((M, T), BF16), pltpu.SemaphoreType.DMA(())],
        compiler_params=_params(("arbitrary",)),
    )(h2t, dx2t, dgate, dup, act)


def _wgrad_out_branches(ogt, dya, cvot, dyb, mgt, dx1, after=()):
    M, T = ogt.shape
    N = dya.shape[1]
    c = N // N_DEV
    per = 2
    tn = per * c

    def body(at_ref, da_ref, bt_hbm, db_ref, mt_hbm, dx_ref, oa_ref, ob_ref, oo_ref, bt_scr, mt_scr, sems):
        first = pl.program_id(0) == 0
        fetch_b = pltpu.make_async_copy(bt_hbm, bt_scr, sems.at[0])
        fetch_m = pltpu.make_async_copy(mt_hbm, mt_scr, sems.at[1])

        @pl.when(first)
        def _():
            fetch_b.start()
            fetch_m.start()

        ga = _mm(at_ref[...], da_ref[...])
        for s in range(per):
            oa_ref[s] = ga[:, s * c:(s + 1) * c].astype(BF16)

        @pl.when(first)
        def _():
            fetch_b.wait()

        gb = _mm(bt_scr[...], db_ref[...])
        for s in range(per):
            ob_ref[s] = gb[:, s * c:(s + 1) * c].astype(BF16)

        @pl.when(first)
        def _():
            fetch_m.wait()

        oo_ref[...] = _mm(mt_scr[...], dx_ref[...]).astype(BF16)

    rhs = pl.BlockSpec((T, tn), lambda j: (0, j))
    owners = pl.BlockSpec((per, M, c), lambda j: (j, 0, 0))
    out = jax.ShapeDtypeStruct((N_DEV, M, c), BF16)
    return pl.pallas_call(
        _drop_operands(body, 6, len(after)), name="wgrad_out_branches", grid=(N // tn,),
        in_specs=[_resident((M, T)), rhs, HBM_SPEC, rhs, HBM_SPEC, rhs] + [HBM_SPEC] * len(after),
        out_specs=[owners, owners, pl.BlockSpec((mgt.shape[0], tn), lambda j: (0, j))],
        out_shape=[out, out, jax.ShapeDtypeStruct((mgt.shape[0], dx1.shape[1]), BF16)],
        scratch_shapes=[pltpu.VMEM((M, T), BF16), pltpu.VMEM(mgt.shape, BF16), pltpu.SemaphoreType.DMA((2,))],
        compiler_params=_params(("arbitrary",)),
    )(ogt, dya, cvot, dyb, mgt, dx1, *after)


def _wgrad_in(ht, dparts, after=(), riders=()):
    M, T = ht.shape
    tn = 512
    nblk = [p.shape[1] // tn for p in dparts]
    start = [sum(nblk[:i]) for i in range(len(nblk))]
    n = len(dparts)
    steps = sum(nblk)
    nr = len(riders)
    rows = [r[0].shape[0] // steps for r in riders]
    assert all(r[0].shape[0] == rr * steps and rr % 16 == 0 for r, rr in zip(riders, rows))

    def body(a_ref, *refs):
        d_refs = refs[:n]
        rider_in = refs[n:n + 4 * nr]
        o_ref = refs[n + 4 * nr]
        rider_out = refs[n + 4 * nr + 1:]
        j = pl.program_id(0)
        for d_ref, s, nb in zip(d_refs, start, nblk):
            @pl.when((j >= s) & (j < s + nb))
            def _():
                o_ref[...] = _mm(a_ref[...], d_ref[...]).T.astype(BF16)
        for i in range(nr):
            w_ref, p_ref, m_ref, v_ref = rider_in[4 * i:4 * i + 4]
            g = p_ref[0].astype(F32)
            for k in range(1, 4):
                g = g + p_ref[k].astype(F32)
            delta, m_new, v_new = _adamw_math(w_ref[...], g, m_ref[...], v_ref[...])
            for k, val in enumerate((g, delta, m_new, v_new)):
                rider_out[4 * i + k][...] = val

    def piece_spec(s, nb):
        return pl.BlockSpec((T, tn), lambda j: (0, jnp.clip(j - s, 0, nb - 1)))

    rider_specs, rider_out_specs, rider_out_shape, rider_args = [], [], [], []
    for (w, parts, m, v), rr in zip(riders, rows):
        blk = pl.BlockSpec((rr, w.shape[1]), lambda j: (j, 0))
        rider_specs += [blk, pl.BlockSpec((4, rr, w.shape[1]), lambda j: (0, j, 0)), blk, blk]
        rider_out_specs += [blk] * 4
        rider_out_shape += [jax.ShapeDtypeStruct(w.shape, F32)] * 4
        rider_args += [w, parts, m, v]
    outs = pl.pallas_call(
        _drop_operands(body, 1 + n + 4 * nr, len(after)), name="wgrad_in", grid=(steps,),
        in_specs=[_resident((M, T))] + [piece_spec(s, nb) for s, nb in zip(start, nblk)] + rider_specs
                 + [HBM_SPEC] * len(after),
        out_specs=[pl.BlockSpec((tn, M), lambda j: (j, 0))] + rider_out_specs,
        out_shape=[jax.ShapeDtypeStruct((steps * tn, M), BF16)] + rider_out_shape,
        compiler_params=_params(("parallel",)),
    )(ht, *dparts, *rider_args, *after)
    return outs[0], [outs[1 + 4 * i:5 + 4 * i] for i in range(nr)]


def _adamw_math(w, g, m, v):
    m = ADAM_B1 * m + (1.0 - ADAM_B1) * g
    v = ADAM_B2 * v + (1.0 - ADAM_B2) * (g * g)
    m_hat = m / (1.0 - ADAM_B1 ** ADAM_STEP)
    v_hat = v / (1.0 - ADAM_B2 ** ADAM_STEP)
    delta = -ADAM_LR * (m_hat / (jnp.sqrt(v_hat) + ADAM_EPS) + ADAM_WD * w)
    return delta, m, v


def _adamw_sum(name, ws, parts, ms, vs):
    n = len(ws)
    steps = min(_row_steps(w.shape[0]) for w in ws)
    rows = [w.shape[0] // steps for w in ws]

    def body(*refs):
        w_refs, p_refs, m_refs, v_refs = (refs[k * n:(k + 1) * n] for k in range(4))
        out_refs = refs[4 * n:]
        for i in range(n):
            g = p_refs[i][0].astype(F32)
            for k in range(1, 4):
                g = g + p_refs[i][k].astype(F32)
            delta, m_new, v_new = _adamw_math(w_refs[i][...], g, m_refs[i][...], v_refs[i][...])
            for k, val in enumerate((g, delta, m_new, v_new)):
                out_refs[4 * i + k][...] = val

    blk = [pl.BlockSpec((r, w.shape[1]), lambda s: (s, 0)) for r, w in zip(rows, ws)]
    pblk = [pl.BlockSpec((4, r, w.shape[1]), lambda s: (0, s, 0)) for r, w in zip(rows, ws)]
    out_specs, out_shape = [], []
    for b, w in zip(blk, ws):
        out_specs += [b] * 4
        out_shape += [jax.ShapeDtypeStruct(w.shape, F32)] * 4
    flat = pl.pallas_call(
        body, name=name, grid=(steps,),
        in_specs=blk + pblk + blk + blk, out_specs=out_specs, out_shape=out_shape,
        compiler_params=_params(("parallel",)),
    )(*ws, *parts, *ms, *vs)
    return [flat[4 * i:4 * i + 4] for i in range(n)]


_SMALL_SLOTS = (("norm_mix_g", 0, 1, 1024), ("norm_ffn_g", 1, 1, 1024), ("norm_final_g", 2, 1, 1024),
                ("lower_bounds", 3, 2, 512), ("hg_norm_g", 5, 1, 128), ("loss", 6, 1, 128), ("conv_w", 8, 3, 512))
_SMALL_PARAMS = tuple(s for s in _SMALL_SLOTS if s[0] != "loss")
CONV_SHARD = CONV_WIDTH // N_DEV


def _small_pack(small):
    def body(*refs):
        out = refs[-1]
        out[...] = jnp.zeros_like(out)
        for ref, (_, row, rows, lanes) in zip(refs[:-1], _SMALL_SLOTS):
            out[row:row + rows, 0:lanes] = ref[...]

    vmem = pl.BlockSpec(memory_space=pltpu.VMEM)
    return pl.pallas_call(
        body, name="small_pack", in_specs=[vmem] * len(_SMALL_SLOTS), out_specs=vmem,
        out_shape=jax.ShapeDtypeStruct((SMALL_ROWS, 1024), F32),
    )(*[small[name] for name, _, _, _ in _SMALL_SLOTS])


def _small_update(gathered, dev, w, m, v):
    n = len(_SMALL_PARAMS)

    def body(dev_ref, g_ref, *refs):
        w_refs, m_refs, v_refs = refs[:n], refs[n:2 * n], refs[2 * n:3 * n]
        loss_ref, out_refs, sum_scr = refs[3 * n], refs[3 * n + 1:-1], refs[-1]
        total = g_ref[0]
        for k in range(1, N_DEV):
            total = total + g_ref[k]
        sum_scr[...] = total
        loss_ref[...] = sum_scr[6:7, 0:128]
        for p, (name, row, rows, lanes) in enumerate(_SMALL_PARAMS):
            if name == "conv_w":
                for r in range(rows):
                    g = sum_scr[row + r:row + r + 1, 0:CONV_SHARD]
                    for s in range(1, N_DEV):
                        mine = sum_scr[row + r:row + r + 1, s * CONV_SHARD:(s + 1) * CONV_SHARD]
                        g = jnp.where(dev_ref[0] == s, mine, g)
                    delta, m_new, v_new = _adamw_math(w_refs[p][r], g, m_refs[p][r], v_refs[p][r])
                    out_refs[4 * p][r] = g
                    out_refs[4 * p + 1][r] = delta
                    out_refs[4 * p + 2][r] = m_new
                    out_refs[4 * p + 3][r] = v_new
                continue
            g = sum_scr[row:row + rows, 0:lanes]
            delta, m_new, v_new = _adamw_math(w_refs[p][...], g, m_refs[p][...], v_refs[p][...])
            out_refs[4 * p][...] = g
            out_refs[4 * p + 1][...] = delta
            out_refs[4 * p + 2][...] = m_new
            out_refs[4 * p + 3][...] = v_new

    vmem = pl.BlockSpec(memory_space=pltpu.VMEM)
    outs = [jax.ShapeDtypeStruct((1, 128), F32)]
    for a in w:
        outs += [jax.ShapeDtypeStruct(a.shape, F32)] * 4
    return pl.pallas_call(
        body, name="small_update",
        in_specs=[pl.BlockSpec(memory_space=pltpu.SMEM)] + [vmem] * (1 + 3 * n), out_specs=[vmem] * len(outs),
        out_shape=outs, scratch_shapes=[pltpu.VMEM((SMALL_ROWS, 1024), F32)],
    )(dev, gathered, *w, *m, *v)


def _row_steps(rows):
    for steps in (4, 2):
        if rows % (16 * steps) == 0:
            return steps
    return 1


def _pair_sum(name, by_owner, got, core, after=()):
    n = len(got)

    def body(core_ref, *refs):
        for a_ref, b_ref, o_ref in zip(refs[:n], refs[n:2 * n], refs[2 * n:]):
            o_ref[...] = (a_ref[...].astype(F32) + b_ref[...].astype(F32)).astype(BF16)

    def blk(g):
        return pl.BlockSpec((None,) + g.shape[1:], lambda k, core_ref: (k, 0, 0))

    def mine(g):
        return pl.BlockSpec((None,) + g.shape[1:], lambda k, core_ref: (2 * k + core_ref[0], 0, 0))

    return pl.pallas_call(
        _drop_operands(body, 1 + 2 * n, len(after)), name=name,
        grid_spec=pltpu.PrefetchScalarGridSpec(
            num_scalar_prefetch=1, grid=(4,),
            in_specs=[mine(g) for g in got] + [blk(g) for g in got] + [HBM_SPEC] * len(after),
            out_specs=[blk(g) for g in got]),
        out_shape=[jax.ShapeDtypeStruct(g.shape, BF16) for g in got],
        compiler_params=_params(("parallel",)),
    )(core, *by_owner, *got, *after)


MESH = pl.DeviceIdType.MESH
HBM_SPEC = pl.BlockSpec(memory_space=pl.ANY)


def _handshake(peers):
    barrier = pltpu.get_barrier_semaphore()
    for peer in peers:
        pl.semaphore_signal(barrier, inc=1, device_id=peer, device_id_type=MESH)
    pl.semaphore_wait(barrier, len(peers))


def _comm_call(body, name, operands, out_shape, scratch, collective_id):
    if collective_id is None:
        return pl.pallas_call(body, name=name, in_specs=[HBM_SPEC] * len(operands), out_specs=[HBM_SPEC] * len(out_shape),
                              out_shape=out_shape, scratch_shapes=scratch)(*operands)
    return pl.kernel(body, out_type=out_shape, mesh=plsc.ScalarSubcoreMesh(axis_name="sequencer", num_cores=1),
                     scratch_types=scratch, name=name,
                     compiler_params=pltpu.CompilerParams(collective_id=collective_id))(*operands)


def _all_gather(name, blocks, collective_id=None, after=(), pieces=None):
    n = len(blocks)
    na = len(after)
    pieces = pieces or [1] * n
    parts = []
    for i, (b, k) in enumerate(zip(blocks, pieces)):
        rows, rem = divmod(b.shape[0], k)
        assert rem == 0 and (k == 1 or rows % 16 == 0), (b.shape, k)
        parts += [(i, None, None)] if k == 1 else [(i, j * rows, rows) for j in range(k)]
    np_ = len(parts)

    def body(*refs):
        x_refs, out_refs = refs[:n], refs[n + na:2 * n + na]
        send_sems, recv_sems, local_sems = refs[2 * n + na:]
        x, y, c = lax.axis_index("x"), lax.axis_index("y"), lax.axis_index("c")
        me, sibling = (x, y, c), (x, y, 1 - c)
        chips = [(1 - x, y), (x, 1 - y), (1 - x, 1 - y)]
        if collective_id is not None:
            _handshake([sibling] + [(*chip, c) for chip in chips])

        def slot(p, px, py, pc):
            i, r0, rows = parts[p]
            whole = out_refs[i].at[4 * px + 2 * py + pc]
            return whole if r0 is None else whole.at[pl.ds(r0, rows)]

        def own(p):
            i, r0, rows = parts[p]
            return x_refs[i] if r0 is None else x_refs[i].at[pl.ds(r0, rows)]

        def copy(p, k, blk, to, src=None):
            return pltpu.make_async_remote_copy(
                src_ref=slot(p, *blk) if src is None else src, dst_ref=slot(p, *blk),
                send_sem=send_sems.at[7 * p + k], recv_sem=recv_sems.at[7 * p + k], device_id=to, device_id_type=MESH)

        mine = [pltpu.make_async_copy(own(p), slot(p, *me), local_sems.at[p]) for p in range(np_)]
        for cp in mine:
            cp.start()
        first = []
        for p in range(np_):
            first.append(copy(p, 0, me, sibling, src=own(p)))
            first += [copy(p, 1 + j, me, (*chip, c), src=own(p)) for j, chip in enumerate(chips)]
        for cp in first:
            cp.start()
        passed = []
        for p in range(np_):
            for j, chip in enumerate(chips):
                copy(p, 1 + j, (*chip, c), me).wait_recv()
                passed.append(copy(p, 4 + j, (*chip, c), sibling))
                passed[-1].start()
        for p in range(np_):
            copy(p, 0, sibling, me).wait_recv()
            for j, chip in enumerate(chips):
                copy(p, 4 + j, (*chip, 1 - c), me).wait_recv()
        for cp in first + passed:
            cp.wait_send()
        for cp in mine:
            cp.wait()

    return _comm_call(
        body, name, list(blocks) + list(after), [jax.ShapeDtypeStruct((N_DEV,) + b.shape, b.dtype) for b in blocks],
        [pltpu.SemaphoreType.DMA((7 * np_,)), pltpu.SemaphoreType.DMA((7 * np_,)), pltpu.SemaphoreType.DMA((np_,))],
        collective_id)


def _sibling_swap(name, by_owner, collective_id=None, after=()):
    n = len(by_owner)
    na = len(after)

    def body(*refs):
        x_refs, out_refs = refs[:n], refs[n + na:2 * n + na]
        send_sems, recv_sems = refs[2 * n + na:]
        x, y, c = lax.axis_index("x"), lax.axis_index("y"), lax.axis_index("c")
        if collective_id is not None:
            _handshake([(x, y, 1 - c)])
        copies = []
        for i in range(n):
            for k in range(4):
                copies.append(pltpu.make_async_remote_copy(
                    src_ref=x_refs[i].at[2 * k + 1 - c], dst_ref=out_refs[i].at[k],
                    send_sem=send_sems.at[4 * i + k], recv_sem=recv_sems.at[4 * i + k],
                    device_id=(x, y, 1 - c), device_id_type=MESH))
        for cp in copies:
            cp.start()
        for cp in copies:
            cp.wait()

    return _comm_call(
        body, name, list(by_owner) + list(after),
        [jax.ShapeDtypeStruct((4,) + b.shape[1:], b.dtype) for b in by_owner],
        [pltpu.SemaphoreType.DMA((4 * n,)), pltpu.SemaphoreType.DMA((4 * n,))], collective_id)


def _chip_exchange(name, sums, collective_id=None, after=()):
    n = len(sums)
    na = len(after)

    def body(*refs):
        x_refs, out_refs = refs[:n], refs[n + na:2 * n + na]
        send_sems, recv_sems, local_sems = refs[2 * n + na:]
        x, y, c = lax.axis_index("x"), lax.axis_index("y"), lax.axis_index("c")
        chips = [(1 - x, y), (x, 1 - y), (1 - x, 1 - y)]
        my_chip = 2 * x + y
        if collective_id is not None:
            _handshake([(cx, cy, c) for cx, cy in chips])
        mine = [pltpu.make_async_copy(x_refs[i].at[my_chip], out_refs[i].at[my_chip], local_sems.at[i])
                for i in range(n)]
        for cp in mine:
            cp.start()
        sends = []
        for i in range(n):
            for j, (cx, cy) in enumerate(chips):
                sends.append(pltpu.make_async_remote_copy(
                    src_ref=x_refs[i].at[2 * cx + cy], dst_ref=out_refs[i].at[my_chip],
                    send_sem=send_sems.at[3 * i + j], recv_sem=recv_sems.at[3 * i + j],
                    device_id=(cx, cy, c), device_id_type=MESH))
        for cp in sends:
            cp.start()
        for i in range(n):
            for j, (cx, cy) in enumerate(chips):
                pltpu.make_async_remote_copy(
                    src_ref=x_refs[i].at[my_chip], dst_ref=out_refs[i].at[2 * cx + cy],
                    send_sem=send_sems.at[3 * i + j], recv_sem=recv_sems.at[3 * i + j],
                    device_id=(cx, cy, c), device_id_type=MESH).wait_recv()
        for cp in sends:
            cp.wait_send()
        for cp in mine:
            cp.wait()

    return _comm_call(
        body, name, list(sums) + list(after), [jax.ShapeDtypeStruct(s.shape, s.dtype) for s in sums],
        [pltpu.SemaphoreType.DMA((3 * n,)), pltpu.SemaphoreType.DMA((3 * n,)), pltpu.SemaphoreType.DMA((n,))],
        collective_id)


def _cast_shards(name, shards):
    n = len(shards)
    steps = min(_row_steps(s.shape[0]) for s in shards)

    def body(*refs):
        for i in range(n):
            refs[n + i][...] = refs[i][...].astype(BF16)

    blocks = [pl.BlockSpec((s.shape[0] // steps, s.shape[1]), lambda i: (i, 0)) for s in shards]
    return pl.pallas_call(
        body, name=name, grid=(steps,), in_specs=blocks, out_specs=blocks,
        out_shape=[jax.ShapeDtypeStruct(s.shape, BF16) for s in shards],
        compiler_params=_params(("parallel",)),
    )(*shards)


BIG = ("w_in", "w_branch_a", "w_branch_b", "w_out", "w_ffn_gate", "w_ffn_up", "w_ffn_down")


def _local_step(x, target, gains, low, conv_w, wg8, reduce):
    g_mix, g_hg, g_ffn, g_fin = gains
    w_in = wg8["w_in"].reshape(N_IN, D_MODEL)
    wg = wg8["w_ffn_gate"].reshape(D_FF, D_MODEL)
    wu = wg8["w_ffn_up"].reshape(D_FF, D_MODEL)
    wa, wb = wg8["w_branch_a"], wg8["w_branch_b"]
    wo = wg8["w_out"].reshape(D_MODEL, D_MODEL)
    wd = wg8["w_ffn_down"].reshape(D_FF, D_MODEL)

    ht, hg, cv, gt, cvo, cvot, cy, o, og, ogt, st = _fwd_in(x, g_mix, w_in, conv_w, low, g_hg)
    x1, mgt, ya, yb = _merge_fwd(og, cvo, gt, x, wa, wb, wo)
    h2t, gate, up, act, loss, d_gfin, dx2, dx2t = _ffn_fwd_loss(x1, g_ffn, wg, wu, wd, target, g_fin)

    dgate, dup, dx1, d_gffn = _ffn_bwd(dx2, x1, gate, up, g_ffn, wg, wu, wd)
    d_wg, d_wu, d_wd = _wgrad_ffn(h2t, dgate, dup, dx2t, act)
    by_owner_ffn = lambda a: a.reshape(N_DEV, D_FF // N_DEV, D_MODEL)
    ffn = dict(w_ffn_down=by_owner_ffn(d_wd), w_ffn_gate=by_owner_ffn(d_wg), w_ffn_up=by_owner_ffn(d_wu))
    dgt, dya, dyb, dog, dcv, d_conv = _merge_bwd(dx1, ya, yb, gt, cv, cy, wa, wb, wo, conv_w)
    sums_ffn, got_ffn = reduce.begin(ffn, sum_after=[dya])
    late = ("w_ffn_gate", "w_ffn_up")
    parts_ffn, updated_ffn = reduce.finish(ffn, sums_ffn, defer=late)
    grad_a, grad_b, grad_o = _wgrad_out_branches(ogt, dya, cvot, dyb, mgt, dx1, after=sums_ffn[:1])
    out = dict(w_out=grad_o.reshape(N_DEV, D_MODEL // N_DEV, D_MODEL), w_branch_a=grad_a, w_branch_b=grad_b)
    dhg, d_low, d_ghg = _hg_bwd(dog, hg, o, st, low, g_hg, after=list(sums_ffn) + [out["w_out"]])
    sums_out, got_out = reduce.begin(out, after=[parts_ffn[0], dhg], sum_after=updated_ffn)
    parts_out, updated_out = reduce.finish(out, sums_out)
    dparts = [dhg, dcv, dgt]
    d_w_in_t, ridden = _wgrad_in(ht, dparts, after=sums_out[:1],
                                 riders=reduce.riders(late, dict(zip(ffn, parts_ffn))))
    reduce.record(late, ridden)
    w_in_grad = dict(w_in=d_w_in_t.reshape(N_DEV, N_IN // N_DEV, D_MODEL))
    sums_in, _ = reduce.begin(w_in_grad, after=parts_out[:1], sum_after=updated_out)
    parts_in, _ = reduce.finish(w_in_grad, sums_in)
    grad_x, d_gmix = _in_bwd(dparts, w_in, x, dx1, g_mix, after=list(parts_out[:1]) + list(sums_in))
    small = dict(norm_mix_g=d_gmix, norm_ffn_g=d_gffn, norm_final_g=d_gfin, lower_bounds=d_low, hg_norm_g=d_ghg,
                 conv_w=d_conv, loss=loss)
    return grad_x, small, parts_in


def kernel(x, norm_mix_g, w_in, lower_bounds, hg_norm_g, conv_w, w_branch_a, w_branch_b, w_out, norm_ffn_g, w_ffn_gate, w_ffn_up, w_ffn_down, norm_final_g, loss_target, m_norm_mix_g, m_w_in, m_lower_bounds, m_hg_norm_g, m_conv_w, m_w_branch_a, m_w_branch_b, m_w_out, m_norm_ffn_g, m_w_ffn_gate, m_w_ffn_up, m_w_ffn_down, m_norm_final_g, v_norm_mix_g, v_w_in, v_lower_bounds, v_hg_norm_g, v_conv_w, v_w_branch_a, v_w_branch_b, v_w_out, v_norm_ffn_g, v_w_ffn_gate, v_w_ffn_up, v_w_ffn_down, v_norm_final_g):
    cx, cy, cc = lax.axis_index("x"), lax.axis_index("y"), lax.axis_index("c")
    my_dev = 4 * cx + 2 * cy + cc

    def tr(a):
        return a[0].T

    big = dict(w_in=tr(w_in), w_branch_a=w_branch_a[0], w_branch_b=w_branch_b[0], w_out=w_out[0],
               w_ffn_gate=tr(w_ffn_gate), w_ffn_up=tr(w_ffn_up), w_ffn_down=w_ffn_down[0])
    big_m = dict(w_in=tr(m_w_in), w_branch_a=m_w_branch_a[0], w_branch_b=m_w_branch_b[0], w_out=m_w_out[0],
                 w_ffn_gate=tr(m_w_ffn_gate), w_ffn_up=tr(m_w_ffn_up), w_ffn_down=m_w_ffn_down[0])
    big_v = dict(w_in=tr(v_w_in), w_branch_a=v_w_branch_a[0], w_branch_b=v_w_branch_b[0], w_out=v_w_out[0],
                 w_ffn_gate=tr(v_w_ffn_gate), w_ffn_up=tr(v_w_ffn_up), w_ffn_down=v_w_ffn_down[0])
    transposed = ("w_in", "w_ffn_gate", "w_ffn_up")

    ids = iter(range(1, 16))
    shards = dict(zip(BIG[:1], _cast_shards("cast_w_in", [big["w_in"]])))
    first = _all_gather("gather_w_in", [shards["w_in"], conv_w.transpose(1, 0, 2)], collective_id=next(ids),
                        pieces=[4, 1])
    shards.update(zip(BIG[1:], _cast_shards("cast_shards", [big[n] for n in BIG[1:]])))
    mid = _all_gather("gather_mid", [shards[n] for n in BIG[1:4]], collective_id=next(ids))
    ffn = _all_gather("gather_ffn", [shards[n] for n in BIG[4:]], collective_id=next(ids), pieces=[2, 2, 2])
    wg8 = dict(zip(BIG, [first[0]] + list(mid) + list(ffn)))
    conv_full = first[1].transpose(1, 2, 0, 3).reshape(3, CONV_WIDTH)

    core = cc.reshape(1).astype(jnp.int32)
    outs = {}

    class Reduce:
        @staticmethod
        def begin(grads, after=(), sum_after=()):
            names = list(grads)
            by_owner = [grads[n] for n in names]
            got = _sibling_swap("sibling_swap_" + names[0], by_owner, collective_id=next(ids), after=after)
            sums = _pair_sum("pair_sum_" + names[0], by_owner, got, core, after=sum_after)
            return sums, got

        @staticmethod
        def finish(grads, chip_sums, after=(), defer=()):
            names = list(grads)
            parts = _chip_exchange("chip_exchange_" + names[0], chip_sums, collective_id=next(ids), after=after)
            now = [n for n in names if n not in defer]
            updated = _adamw_sum("adamw_" + now[0], [big[n] for n in now],
                                 [p for n, p in zip(names, parts) if n in now],
                                 [big_m[n] for n in now], [big_v[n] for n in now])
            outs.update(zip(now, updated))
            return parts, [outs[n][1] for n in now]

        @staticmethod
        def riders(names, parts):
            return [(big[n], parts[n], big_m[n], big_v[n]) for n in names]

        @staticmethod
        def record(names, updated):
            outs.update(zip(names, updated))

    gains = (norm_mix_g, hg_norm_g, norm_ffn_g, norm_final_g.reshape(1, D_MODEL))
    grad_x, small, last = _local_step(x[0], loss_target[0], gains, lower_bounds, conv_full, wg8, Reduce)

    small_all = _all_gather("gather_small", [_small_pack(small)], collective_id=next(ids), after=last[:1])

    def small_state(a):
        return [a[0], a[1], a[2].reshape(1, D_MODEL), a[3], a[4], a[5].transpose(1, 0, 2)]

    upd = _small_update(
        small_all[0], my_dev.reshape(1).astype(jnp.int32),
        small_state((norm_mix_g, norm_ffn_g, norm_final_g, lower_bounds, hg_norm_g, conv_w)),
        small_state((m_norm_mix_g, m_norm_ffn_g, m_norm_final_g, m_lower_bounds, m_hg_norm_g, m_conv_w)),
        small_state((v_norm_mix_g, v_norm_ffn_g, v_norm_final_g, v_lower_bounds, v_hg_norm_g, v_conv_w)))
    loss = upd[0][0, 0]
    for p, (name, _, _, _) in enumerate(_SMALL_PARAMS):
        outs[name] = upd[1 + 4 * p:5 + 4 * p]
    outs["norm_final_g"] = [a.reshape(D_MODEL) for a in outs["norm_final_g"]]
    outs["conv_w"] = [a.transpose(1, 0, 2) for a in outs["conv_w"]]

    order = ["norm_mix_g", "w_in", "lower_bounds", "hg_norm_g", "conv_w", "w_branch_a", "w_branch_b", "w_out",
             "norm_ffn_g", "w_ffn_gate", "w_ffn_up", "w_ffn_down", "norm_final_g"]
    result = [loss, grad_x[None]]
    for k in range(4):
        for n in order:
            if n in BIG:
                result.append((outs[n][k].T if n in transposed else outs[n][k])[None])
            else:
                result.append(outs[n][k])
    return tuple(result)
```

```python
import jax
import jax.numpy as jnp
from jax import lax
from jax.experimental import pallas as pl
from jax.experimental.pallas import tpu as pltpu
from jax.experimental.pallas import tpu_sc as plsc

F32 = jnp.float32
BF16 = jnp.bfloat16
STASH = jnp.bfloat16

D_MODEL = 1024
HG_WIDTH = 512
HEAD_DIM = 128
N_HEADS = 4
HEADS_PER_STEP = 4
HEAD_GROUPS = N_HEADS // HEADS_PER_STEP
CONV_WIDTH = 512
CONV_K = 3
D_FF = 2816
CHUNK = 32
EPS = 1e-6
Q_SCALE = HEAD_DIM ** -0.5
N_DEV = 8

ADAM_LR = 0.001
ADAM_B1 = 0.9
ADAM_B2 = 0.999
ADAM_EPS = 1e-08
ADAM_WD = 0.01
ADAM_STEP = 10

VMEM_LIMIT_V7X = 56 * 1024 * 1024
VMEM_LIMIT_LARGE_V7X = 62 * 1024 * 1024

SMALL_ROWS = 16


def _params(sem, vmem=VMEM_LIMIT_V7X):
    return pltpu.CompilerParams(dimension_semantics=sem, vmem_limit_bytes=vmem)


def _mm(a, b):
    return jnp.dot(a.astype(BF16), b.astype(BF16), preferred_element_type=F32)


def _mm_nt(a, b):
    return lax.dot_general(a.astype(BF16), b.astype(BF16), (((1,), (1,)), ((), ())), preferred_element_type=F32)


def _mm_tn(a, b):
    return lax.dot_general(a.astype(BF16), b.astype(BF16), (((0,), (0,)), ((), ())), preferred_element_type=F32)


def _sigmoid(x):
    return 0.5 * jnp.tanh(0.5 * x) + 0.5


def _resident(shape):
    nd = len(shape)
    return pl.BlockSpec(shape, lambda *_: (0,) * nd, pipeline_mode=pl.Buffered(1))


def _full(shape):
    nd = len(shape)
    return pl.BlockSpec(shape, lambda *_: (0,) * nd)


def _shard_cols(w_ref):
    return jnp.concatenate([w_ref[s] for s in range(N_DEV)], axis=1)


N_HG = 4 * HG_WIDTH
N_CV = 3 * CONV_WIDTH
N_GT = 2 * D_MODEL
N_IN = N_HG + N_CV + N_GT


def _col(tm, n):
    return pl.BlockSpec((n, tm), lambda i: (0, i))


HALO = 8


def _fwd_in(x, g, w_in_t, conv_w, low, gn):
    T = x.shape[0]
    tm = min(512, T)
    nc = tm // CHUNK

    def body(x_ref, g_ref, w_ref, cw_ref, low_ref, gn_ref, ht_ref, hg_ref, cv_ref, gt_ref, cvo_ref, cvot_ref, cy_ref,
             o_ref, og_ref, ogt_ref, st_ref, tail_scr, s_scr):
        @pl.when(pl.program_id(0) == 0)
        def _():
            tail_scr[...] = jnp.zeros_like(tail_scr)
            s_scr[...] = jnp.zeros_like(s_scr)

        xv = x_ref[...]
        r = lax.rsqrt(jnp.mean(xv * xv, axis=-1, keepdims=True) + EPS)
        hf = xv * r * g_ref[...]
        h = hf.astype(BF16)
        ht_ref[...] = hf.T.astype(BF16)
        hg_ref[...] = _mm_nt(h, w_ref[:N_HG, :])
        cv = _mm_nt(h, w_ref[N_HG:N_HG + N_CV, :])
        cv_ref[...] = cv.astype(STASH)
        gt_ref[...] = _mm_nt(h, w_ref[N_HG + N_CV:, :]).astype(STASH)

        u = cv[:, :CONV_WIDTH] * cv[:, 2 * CONV_WIDTH:]
        row = lax.broadcasted_iota(jnp.int32, u.shape, 0)
        prev1 = tail_scr[HALO - 1:HALO, :]
        prev2 = tail_scr[HALO - 2:HALO - 1, :]
        u1 = jnp.where(row >= 1, pltpu.roll(u, 1, 0), prev1)
        u2 = jnp.where(row >= 2, pltpu.roll(u, 2, 0), jnp.where(row == 1, prev1, prev2))
        y = cw_ref[0:1, :] * u2 + cw_ref[1:2, :] * u1 + cw_ref[2:3, :] * u
        cy_ref[...] = y.astype(STASH)
        out = cv[:, CONV_WIDTH:2 * CONV_WIDTH] * y
        cvo_ref[...] = out.astype(BF16)
        cvot_ref[...] = out.T.astype(BF16)
        tail_scr[...] = u[tm - HALO:, :]

        _hg_fwd_tile(hg_ref, low_ref, gn_ref, o_ref, og_ref, ogt_ref, st_ref, s_scr, tm)

    row = lambda n: pl.BlockSpec((tm, n), lambda i: (i, 0))
    return pl.pallas_call(
        body, name="fwd_in", grid=(T // tm,),
        in_specs=[row(D_MODEL), _full((1, D_MODEL)), _resident(w_in_t.shape), _full((CONV_K, CONV_WIDTH)),
                  _full((2, HG_WIDTH)), _full((1, HEAD_DIM))],
        out_specs=[_col(tm, D_MODEL), row(N_HG), row(N_CV), row(N_GT), row(CONV_WIDTH), _col(tm, CONV_WIDTH),
                   row(CONV_WIDTH),
                   row(HG_WIDTH), row(HG_WIDTH), _col(tm, HG_WIDTH),
                   pl.BlockSpec((N_HEADS, nc, HEAD_DIM, HEAD_DIM), lambda i: (0, i, 0, 0))],
        out_shape=[jax.ShapeDtypeStruct((D_MODEL, T), BF16), jax.ShapeDtypeStruct((T, N_HG), F32),
                   jax.ShapeDtypeStruct((T, N_CV), STASH), jax.ShapeDtypeStruct((T, N_GT), STASH),
                   jax.ShapeDtypeStruct((T, CONV_WIDTH), BF16), jax.ShapeDtypeStruct((CONV_WIDTH, T), BF16),
                   jax.ShapeDtypeStruct((T, CONV_WIDTH), STASH), jax.ShapeDtypeStruct((T, HG_WIDTH), F32), jax.ShapeDtypeStruct((T, HG_WIDTH), BF16),
                   jax.ShapeDtypeStruct((HG_WIDTH, T), BF16),
                   jax.ShapeDtypeStruct((N_HEADS, T // CHUNK, HEAD_DIM, HEAD_DIM), F32)],
        scratch_shapes=[pltpu.VMEM((HALO, CONV_WIDTH), F32), pltpu.VMEM((N_HEADS, HEAD_DIM, HEAD_DIM), F32)],
        compiler_params=_params(("arbitrary",), vmem=VMEM_LIMIT_LARGE_V7X),
    )(x, g, w_in_t, conv_w, low, gn)


def _chunk_pos(shape):
    return lax.broadcasted_iota(jnp.int32, shape, 0) & (CHUNK - 1)


def _chunk_cumsum(x, pos):
    s = 1
    while s < CHUNK:
        x = x + jnp.where(pos >= s, pltpu.roll(x, s, 0), 0.0)
        s *= 2
    return x


def _chunk_rev_cumsum(x, pos):
    n = x.shape[0]
    s = 1
    while s < CHUNK:
        x = x + jnp.where(pos + s < CHUNK, pltpu.roll(x, n - s, 0), 0.0)
        s *= 2
    return x


def _lower_bound(low_ref):
    l0 = low_ref[0:1, :]
    l1 = low_ref[1:2, :]
    m = jnp.maximum(l0, l1)
    e0 = jnp.exp(l0 - m)
    e1 = jnp.exp(l1 - m)
    return e0 / (e0 + e1), e1 / (e0 + e1)


def _hg_gates(qr, fr, lb, pos, tb):
    sq = _sigmoid(qr)
    q = qr * sq * Q_SCALE
    sg = _sigmoid(fr)
    f = lb + (1.0 - lb) * sg
    k = 1.0 - f
    b = _chunk_cumsum(jnp.log(f), pos)
    b3 = b.reshape(tb // CHUNK, CHUNK, HEAD_DIM)
    anc = b3[:, CHUNK // 2 - 1:CHUNK // 2, :]
    last = b3[:, CHUNK - 1:CHUNK, :]
    d3 = b3 - anc
    e_qa3 = jnp.exp(d3)
    e_ka3 = jnp.exp(-d3)
    e_b3 = e_qa3 * jnp.exp(anc)
    e_ko3 = e_ka3 * jnp.exp(last - anc)
    dec = jnp.exp(last)
    flat = lambda a: a.reshape(tb, HEAD_DIM)
    return sq, q, sg, f, k, flat(e_qa3), flat(e_ka3), flat(e_b3), flat(e_ko3), dec


def _intra_mask(sb):
    r = lax.broadcasted_iota(jnp.int32, (sb, sb), 0)
    c = lax.broadcasted_iota(jnp.int32, (sb, sb), 1)
    return ((r // CHUNK) == (c // CHUNK)) & (c <= r)


def _hg_fwd_tile(hg_ref, low_ref, gn_ref, o_ref, og_ref, ogt_ref, st_ref, s_scr, tb):
    sb = min(256, tb)
    nc = tb // CHUNK
    q_ref, f_ref, i_ref, g_ref = (hg_ref.at[:, p * HG_WIDTH:(p + 1) * HG_WIDTH] for p in range(4))
    pos = _chunk_pos((tb, HEAD_DIM))
    mask = _intra_mask(sb)
    lanes = [slice(hh * HEAD_DIM, (hh + 1) * HEAD_DIM) for hh in range(N_HEADS)]
    qi, ko, vb, dec, st = [], [], [], [], []
    for hh, ln in enumerate(lanes):
        lb, _ = _lower_bound(low_ref.at[:, ln])
        _, q, _, _, k, e_qa, e_ka, e_b, e_ko, dec_h = _hg_gates(q_ref[:, ln], f_ref[:, ln], lb, pos, tb)
        qh = (q * e_qa).astype(BF16)
        kh = (k * e_ka).astype(BF16)
        qi.append((q * e_b).astype(BF16))
        ko.append((k * e_ko).astype(BF16))
        vb.append(i_ref[:, ln].astype(BF16))
        dec.append(dec_h)
        st.append(s_scr[hh])
        for s in range(tb // sb):
            sl = slice(s * sb, (s + 1) * sb)
            p = jnp.where(mask, _mm_nt(qh[sl], kh[sl]), 0.0)
            o_ref[sl, ln] = _mm(p, vb[hh][sl])
    for c in range(nc):
        sl = slice(c * CHUNK, (c + 1) * CHUNK)
        for hh, ln in enumerate(lanes):
            st_ref[hh, c] = st[hh]
            o_ref[sl, ln] = o_ref[sl, ln] + _mm_nt(qi[hh][sl], st[hh])
            st[hh] = dec[hh][c] * st[hh] + _mm_tn(vb[hh][sl], ko[hh][sl])
    for hh, ln in enumerate(lanes):
        s_scr[hh] = st[hh]
        o = o_ref[:, ln]
        r = lax.rsqrt(jnp.mean(o * o, axis=-1, keepdims=True) + EPS)
        gr = g_ref[:, ln]
        og = (o * r * gn_ref[...]) * (gr * _sigmoid(gr))
        og_ref[:, ln] = og.astype(BF16)
        ogt_ref[ln, :] = og.T.astype(BF16)


def _merge_fwd(og, cvo, gt, x, wa, wb, wo):
    T = x.shape[0]
    tm = min(1024, T)

    def body(og_ref, cvo_ref, gt_ref, x_ref, wa_ref, wb_ref, wo_ref, x1_ref, mgt_ref, ya_ref, yb_ref):
        ya = jnp.dot(og_ref[...], _shard_cols(wa_ref), preferred_element_type=F32)
        yb = jnp.dot(cvo_ref[...], _shard_cols(wb_ref), preferred_element_type=F32)
        ya_ref[...] = ya.astype(STASH)
        yb_ref[...] = yb.astype(STASH)
        m = (_sigmoid(gt_ref[:, :D_MODEL].astype(F32)) * ya
             + _sigmoid(gt_ref[:, D_MODEL:].astype(F32)) * yb)
        mgt_ref[...] = m.T.astype(BF16)
        x1_ref[...] = x_ref[...] + jnp.dot(m.astype(BF16), wo_ref[...], preferred_element_type=F32)

    row = lambda n: pl.BlockSpec((tm, n), lambda i: (i, 0))
    return pl.pallas_call(
        body, name="merge_fwd", grid=(T // tm,),
        in_specs=[row(HG_WIDTH), row(CONV_WIDTH), row(2 * D_MODEL), row(D_MODEL),
                  _resident(wa.shape), _resident(wb.shape), _resident(wo.shape)],
        out_specs=[row(D_MODEL), _col(tm, D_MODEL), row(D_MODEL), row(D_MODEL)],
        out_shape=[jax.ShapeDtypeStruct((T, D_MODEL), F32), jax.ShapeDtypeStruct((D_MODEL, T), BF16),
                   jax.ShapeDtypeStruct((T, D_MODEL), STASH), jax.ShapeDtypeStruct((T, D_MODEL), STASH)],
        compiler_params=_params(("parallel",)),
    )(og, cvo, gt, x, wa, wb, wo)


def _ffn_fwd_loss(x1, g, wg, wu, wd, target, g_fin):
    T = x1.shape[0]
    tm = min(512, T)

    def body(x_ref, g_ref, wg_ref, wu_ref, wd_ref, t_ref, gf_ref,
             ht_ref, gate_ref, up_ref, act_ref, loss_ref, dgf_ref, dx2_ref, dx2t_ref):
        @pl.when(pl.program_id(0) == 0)
        def _():
            loss_ref[...] = jnp.zeros_like(loss_ref)
            dgf_ref[...] = jnp.zeros_like(dgf_ref)

        xv = x_ref[...]
        r = lax.rsqrt(jnp.mean(xv * xv, axis=-1, keepdims=True) + EPS)
        hf = xv * r * g_ref[...]
        h = hf.astype(BF16)
        ht_ref[...] = hf.T.astype(BF16)
        gate = _mm_nt(h, wg_ref[...])
        up = _mm_nt(h, wu_ref[...])
        gate_ref[...] = gate.astype(STASH)
        up_ref[...] = up.astype(STASH)
        act = (gate * _sigmoid(gate) * up).astype(BF16)
        act_ref[...] = act
        x2 = xv + jnp.dot(act, wd_ref[...], preferred_element_type=F32)

        gv = gf_ref[...]
        r2 = lax.rsqrt(jnp.mean(x2 * x2, axis=-1, keepdims=True) + EPS)
        xh = x2 * r2
        err = xh * gv - t_ref[...]
        loss_ref[...] += 0.5 * jnp.sum(jnp.mean(err * err, axis=-1, keepdims=True), axis=0, keepdims=True)
        dy = err * (1.0 / D_MODEL)
        dgf_ref[...] += jnp.sum(dy * xh, axis=0, keepdims=True)
        w = dy * gv
        dx2 = r2 * (w - xh * jnp.mean(w * xh, axis=-1, keepdims=True))
        dx2_ref[...] = dx2
        dx2t_ref[...] = dx2.T.astype(BF16)

    row = lambda n: pl.BlockSpec((tm, n), lambda i: (i, 0))
    return pl.pallas_call(
        body, name="ffn_fwd_loss", grid=(T // tm,),
        in_specs=[row(D_MODEL), _full((1, D_MODEL)), _resident(wg.shape), _resident(wu.shape), _resident(wd.shape),
                  row(D_MODEL), _full((1, D_MODEL))],
        out_specs=[_col(tm, D_MODEL), row(D_FF), row(D_FF), row(D_FF), _full((1, 128)), _full((1, D_MODEL)),
                   row(D_MODEL), _col(tm, D_MODEL)],
        out_shape=[jax.ShapeDtypeStruct((D_MODEL, T), BF16), jax.ShapeDtypeStruct((T, D_FF), STASH),
                   jax.ShapeDtypeStruct((T, D_FF), STASH), jax.ShapeDtypeStruct((T, D_FF), BF16),
                   jax.ShapeDtypeStruct((1, 128), F32), jax.ShapeDtypeStruct((1, D_MODEL), F32),
                   jax.ShapeDtypeStruct((T, D_MODEL), F32), jax.ShapeDtypeStruct((D_MODEL, T), BF16)],
        compiler_params=_params(("arbitrary",), vmem=VMEM_LIMIT_LARGE_V7X),
    )(x1, g, wg, wu, wd, target, g_fin)


def _ffn_bwd(dx2, x1, gate, up, g, wg, wu, wd):
    T = x1.shape[0]
    tm = min(512, T)

    def body(dx2_ref, x_ref, gate_ref, up_ref, g_ref, wg_ref, wu_ref, wd_ref, dgate_ref, dup_ref, dx1_ref, dgn_ref):
        @pl.when(pl.program_id(0) == 0)
        def _():
            dgn_ref[...] = jnp.zeros_like(dgn_ref)

        dx2 = dx2_ref[...]
        dact = _mm_nt(dx2, wd_ref[...])
        gate = gate_ref[...].astype(F32)
        s = _sigmoid(gate)
        dgate = (dact * up_ref[...].astype(F32) * (s * (1.0 + gate * (1.0 - s)))).astype(BF16)
        dup = (dact * (gate * s)).astype(BF16)
        dgate_ref[...] = dgate
        dup_ref[...] = dup
        dh = _mm(dgate, wg_ref[...]) + _mm(dup, wu_ref[...])
        xv = x_ref[...]
        r = lax.rsqrt(jnp.mean(xv * xv, axis=-1, keepdims=True) + EPS)
        xh = xv * r
        dgn_ref[...] += jnp.sum(dh * xh, axis=0, keepdims=True)
        w = dh * g_ref[...]
        dx1_ref[...] = dx2 + r * (w - xh * jnp.mean(w * xh, axis=-1, keepdims=True))

    row = lambda n: pl.BlockSpec((tm, n), lambda i: (i, 0))
    return pl.pallas_call(
        body, name="ffn_bwd", grid=(T // tm,),
        in_specs=[row(D_MODEL), row(D_MODEL), row(D_FF), row(D_FF), _full((1, D_MODEL)),
                  _resident(wg.shape), _resident(wu.shape), _resident(wd.shape)],
        out_specs=[row(D_FF), row(D_FF), row(D_MODEL), _full((1, D_MODEL))],
        out_shape=[jax.ShapeDtypeStruct((T, D_FF), BF16), jax.ShapeDtypeStruct((T, D_FF), BF16),
                   jax.ShapeDtypeStruct((T, D_MODEL), F32), jax.ShapeDtypeStruct((1, D_MODEL), F32)],
        compiler_params=_params(("arbitrary",), vmem=VMEM_LIMIT_LARGE_V7X),
    )(dx2, x1, gate, up, g, wg, wu, wd)


def _merge_bwd(dx1, ya, yb, gt, cv, cy, wa, wb, wo, conv_w):
    T = dx1.shape[0]
    tm = min(512, T)
    nt = T // tm

    def body(dx_ref, ya_ref, yb_ref, gt_ref, cv_ref, cy_ref, wa_ref, wb_ref, wo_ref, cw_ref,
             dgt_ref, dya_ref, dyb_ref, dog_ref, dcv_ref, dcw_ref, next_dy):
        @pl.when(pl.program_id(0) == 0)
        def _():
            next_dy[...] = jnp.zeros_like(next_dy)
            dcw_ref[...] = jnp.zeros_like(dcw_ref)

        dm = _mm_nt(dx_ref[...], wo_ref[...])
        wa = _shard_cols(wa_ref)
        wb = _shard_cols(wb_ref)
        ya = ya_ref[...].astype(F32)
        yb = yb_ref[...].astype(F32)
        sa = _sigmoid(gt_ref[:, :D_MODEL].astype(F32))
        sb = _sigmoid(gt_ref[:, D_MODEL:].astype(F32))
        da = dm * sa
        db = dm * sb
        dgt_ref[:, :D_MODEL] = (da * ya * (1.0 - sa)).astype(BF16)
        dgt_ref[:, D_MODEL:] = (db * yb * (1.0 - sb)).astype(BF16)
        dya = da.astype(BF16)
        dyb = db.astype(BF16)
        dya_ref[...] = dya
        dyb_ref[...] = dyb
        dog_ref[...] = _mm_nt(dya, wa)
        dcvo = _mm_nt(dyb, wb)

        cvt = cv_ref[...].astype(F32)
        c, bg, xb = cvt[:, :CONV_WIDTH], cvt[:, CONV_WIDTH:2 * CONV_WIDTH], cvt[:, 2 * CONV_WIDTH:]
        u = c * xb
        row = lax.broadcasted_iota(jnp.int32, u.shape, 0)
        w0, w1, w2 = cw_ref[0:1, :], cw_ref[1:2, :], cw_ref[2:3, :]
        dcv_ref[:, CONV_WIDTH:2 * CONV_WIDTH] = (dcvo * cy_ref[...].astype(F32)).astype(BF16)
        dy = dcvo * bg
        n1 = next_dy[0:1, :]
        n2 = next_dy[1:2, :]
        dy1 = jnp.where(row < tm - 1, pltpu.roll(dy, tm - 1, 0), n1)
        dy2 = jnp.where(row < tm - 2, pltpu.roll(dy, tm - 2, 0), jnp.where(row == tm - 2, n1, n2))
        dcw_ref[0:1, :] += jnp.sum(dy2 * u, axis=0, keepdims=True)
        dcw_ref[1:2, :] += jnp.sum(dy1 * u, axis=0, keepdims=True)
        dcw_ref[2:3, :] += jnp.sum(dy * u, axis=0, keepdims=True)
        du = w2 * dy + w1 * dy1 + w0 * dy2
        dcv_ref[:, :CONV_WIDTH] = (du * xb).astype(BF16)
        dcv_ref[:, 2 * CONV_WIDTH:] = (du * c).astype(BF16)
        next_dy[...] = dy[:HALO, :]

    rt = lambda i: nt - 1 - i
    row = lambda n: pl.BlockSpec((tm, n), lambda i: (rt(i), 0))
    return pl.pallas_call(
        body, name="merge_bwd", grid=(nt,),
        in_specs=[row(D_MODEL), row(D_MODEL), row(D_MODEL), row(2 * D_MODEL), row(N_CV), row(CONV_WIDTH),
                  _resident(wa.shape), _resident(wb.shape), _resident(wo.shape), _full((CONV_K, CONV_WIDTH))],
        out_specs=[row(2 * D_MODEL), row(D_MODEL), row(D_MODEL), row(HG_WIDTH), row(N_CV),
                   _full((CONV_K, CONV_WIDTH))],
        out_shape=[jax.ShapeDtypeStruct((T, 2 * D_MODEL), BF16), jax.ShapeDtypeStruct((T, D_MODEL), BF16),
                   jax.ShapeDtypeStruct((T, D_MODEL), BF16), jax.ShapeDtypeStruct((T, HG_WIDTH), F32),
                   jax.ShapeDtypeStruct((T, N_CV), BF16), jax.ShapeDtypeStruct((CONV_K, CONV_WIDTH), F32)],
        scratch_shapes=[pltpu.VMEM((HALO, CONV_WIDTH), F32)],
        compiler_params=_params(("arbitrary",)),
    )(dx1, ya, yb, gt, cv, cy, wa, wb, wo, conv_w)


def _drop_operands(body, first, count):
    def wrapped(*refs):
        return body(*refs[:first], *refs[first + count:])
    return wrapped


def _hg_bwd(dog, hg, o, st, low, gn, after=()):
    T = hg.shape[0]
    tb = min(512, T)
    sb = min(256, tb)
    nb = T // tb
    nc = tb // CHUNK
    wid = HEADS_PER_STEP * HEAD_DIM

    def body(q_ref, f_ref, i_ref, g_ref, low_ref, gn_ref, o_ref, dog_ref, st_ref,
             dhg_ref, dlow_ref, dgn_ref,
             ds_scr, dqi_scr, dko_scr, dv_scr, dd_scr, dqh_scr, dkh_scr):
        h = pl.program_id(0)
        t = pl.program_id(1)
        dq_ref, df_ref, di_ref, dg_ref = (dhg_ref.at[:, p * HG_WIDTH:(p + 1) * HG_WIDTH] for p in range(4))

        @pl.when(t == 0)
        def _():
            ds_scr[...] = jnp.zeros_like(ds_scr)
            dlow_ref[...] = jnp.zeros_like(dlow_ref)

        @pl.when((t == 0) & (h == 0))
        def _():
            dgn_ref[...] = jnp.zeros_like(dgn_ref)

        pos = _chunk_pos((tb, HEAD_DIM))
        mask = _intra_mask(sb)
        gnv = gn_ref[...]
        lanes = [slice(hh * HEAD_DIM, (hh + 1) * HEAD_DIM) for hh in range(HEADS_PER_STEP)]
        heads = []
        for hh, ln in enumerate(lanes):
            lb, lb1 = _lower_bound(low_ref.at[:, ln])
            qr = q_ref[:, ln]
            sq, q, sg, f, k, e_qa, e_ka, e_b, e_ko, dec = _hg_gates(qr, f_ref[:, ln], lb, pos, tb)

            gr = g_ref[:, ln]
            o = o_ref[:, ln]
            dog_v = dog_ref[:, ln]
            sgr = _sigmoid(gr)
            r = lax.rsqrt(jnp.mean(o * o, axis=-1, keepdims=True) + EPS)
            oh = o * r
            dg_ref[:, ln] = (dog_v * (oh * gnv) * (sgr * (1.0 + gr * (1.0 - sgr)))).astype(BF16)
            don = dog_v * (gr * sgr)
            dgn_ref[...] += jnp.sum(don * oh, axis=0, keepdims=True)
            w = don * gnv
            do = (r * (w - oh * jnp.mean(w * oh, axis=-1, keepdims=True))).astype(BF16)

            qh = (q * e_qa).astype(BF16)
            kh = (k * e_ka).astype(BF16)
            qi = (q * e_b).astype(BF16)
            ko = (k * e_ko).astype(BF16)
            vb = i_ref[:, ln].astype(BF16)

            for s in range(tb // sb):
                sl = slice(s * sb, (s + 1) * sb)
                p = jnp.where(mask, _mm_nt(qh[sl], kh[sl]), 0.0).astype(BF16)
                dp = jnp.where(mask, _mm_nt(do[sl], vb[sl]), 0.0).astype(BF16)
                dv_scr[sl, ln] = _mm_tn(p, do[sl])
                dqh_scr[sl, ln] = _mm(dp, kh[sl])
                dkh_scr[sl, ln] = _mm_tn(dp, qh[sl])
            heads.append(dict(lb=lb, lb1=lb1, qr=qr, sq=sq, q=q, sg=sg, f=f, k=k, e_qa=e_qa, e_ka=e_ka, e_b=e_b,
                              e_ko=e_ko, dec=dec, do=do, qi=qi, ko=ko, vb=vb, ds=ds_scr[hh]))

        for c in reversed(range(nc)):
            sl = slice(c * CHUNK, (c + 1) * CHUNK)
            for hh, ln in enumerate(lanes):
                hd = heads[hh]
                ds = hd["ds"]
                st_c = st_ref[hh, c]
                dqi_scr[sl, ln] = _mm(hd["do"][sl], st_c)
                dko_scr[sl, ln] = _mm(hd["vb"][sl], ds)
                dv_scr[sl, ln] = dv_scr[sl, ln] + _mm_nt(hd["ko"][sl], ds)
                dec_c = hd["dec"][c]
                dd_scr[sl, ln] = jnp.broadcast_to(dec_c * jnp.sum(ds * st_c, axis=0, keepdims=True),
                                                  (CHUNK, HEAD_DIM))
                hd["ds"] = dec_c * ds + _mm_tn(hd["do"][sl], hd["qi"][sl])

        for hh, ln in enumerate(lanes):
            hd = heads[hh]
            ds_scr[hh] = hd["ds"]
            q, k, lb = hd["q"], hd["k"], hd["lb"]
            dko_e = dko_scr[:, ln] * hd["e_ko"]
            dq = dqh_scr[:, ln] * hd["e_qa"] + dqi_scr[:, ln] * hd["e_b"]
            dk = dkh_scr[:, ln] * hd["e_ka"] + dko_e
            kd3 = (k * dko_e).reshape(nc, CHUNK, HEAD_DIM)
            last = jnp.broadcast_to(jnp.sum(kd3, axis=1, keepdims=True), kd3.shape).reshape(tb, HEAD_DIM)
            db = q * dq - k * dk + jnp.where(pos == CHUNK - 1, dd_scr[:, ln] + last, 0.0)
            dlg = _chunk_rev_cumsum(db, pos)
            dfv = dlg / hd["f"] - dk
            s_low = jnp.sum(dfv * (1.0 - hd["sg"]), axis=0, keepdims=True)
            dlow_ref[0:1, ln] += s_low * lb * (1.0 - lb)
            dlow_ref[1:2, ln] += -s_low * lb * hd["lb1"]
            df_ref[:, ln] = (dfv * (1.0 - lb) * hd["sg"] * (1.0 - hd["sg"])).astype(BF16)
            dq_ref[:, ln] = (dq * Q_SCALE * (hd["sq"] * (1.0 + hd["qr"] * (1.0 - hd["sq"])))).astype(BF16)
            di_ref[:, ln] = dv_scr[:, ln].astype(BF16)

    rt = lambda t: nb - 1 - t
    col = lambda p: pl.BlockSpec((tb, wid), lambda h, t: (rt(t), p * HEAD_GROUPS + h))
    hcol = pl.BlockSpec((tb, wid), lambda h, t: (rt(t), h))
    assert HEAD_GROUPS == 1
    tile = pltpu.VMEM((tb, wid), F32)
    return pl.pallas_call(
        _drop_operands(body, 9, len(after)), name="hg_bwd", grid=(HEAD_GROUPS, nb),
        in_specs=[col(0), col(1), col(2), col(3), pl.BlockSpec((2, wid), lambda h, t: (0, h)),
                  pl.BlockSpec((1, HEAD_DIM), lambda h, t: (0, 0)), hcol, hcol,
                  pl.BlockSpec((HEADS_PER_STEP, nc, HEAD_DIM, HEAD_DIM), lambda h, t: (h, rt(t), 0, 0))]
                 + [HBM_SPEC] * len(after),
        out_specs=[pl.BlockSpec((tb, N_HG), lambda h, t: (rt(t), 0)), pl.BlockSpec((2, wid), lambda h, t: (0, h)),
                   pl.BlockSpec((1, HEAD_DIM), lambda h, t: (0, 0))],
        out_shape=[jax.ShapeDtypeStruct((T, N_HG), BF16), jax.ShapeDtypeStruct((2, HG_WIDTH), F32),
                   jax.ShapeDtypeStruct((1, HEAD_DIM), F32)],
        scratch_shapes=[pltpu.VMEM((HEADS_PER_STEP, HEAD_DIM, HEAD_DIM), F32), tile, tile, tile, tile, tile, tile],
        compiler_params=_params(("arbitrary", "arbitrary")),
    )(hg, hg, hg, hg, low, gn, o, dog, st, *after)


def _in_bwd(dparts, w_in, x, dx1, g, after=()):
    T = x.shape[0]
    tm = min(512, T)
    widths = [p.shape[1] for p in dparts]
    offs = [sum(widths[:i]) for i in range(len(widths))]
    n = len(dparts)

    def body(*refs):
        d_refs = refs[:n]
        w_ref, x_ref, dx1_ref, g_ref, dx_ref, dgn_ref = refs[n:]

        @pl.when(pl.program_id(0) == 0)
        def _():
            dgn_ref[...] = jnp.zeros_like(dgn_ref)

        dh = None
        for d_ref, off, wd in zip(d_refs, offs, widths):
            part = _mm(d_ref[...], w_ref[off:off + wd, :])
            dh = part if dh is None else dh + part
        xv = x_ref[...]
        r = lax.rsqrt(jnp.mean(xv * xv, axis=-1, keepdims=True) + EPS)
        xh = xv * r
        dgn_ref[...] += jnp.sum(dh * xh, axis=0, keepdims=True)
        w = dh * g_ref[...]
        dx_ref[...] = dx1_ref[...] + r * (w - xh * jnp.mean(w * xh, axis=-1, keepdims=True))

    row = lambda m: pl.BlockSpec((tm, m), lambda i: (i, 0))
    return pl.pallas_call(
        _drop_operands(body, n + 4, len(after)), name="in_bwd", grid=(T // tm,),
        in_specs=[row(wd) for wd in widths] + [_resident(w_in.shape), row(D_MODEL), row(D_MODEL), _full((1, D_MODEL))]
                 + [HBM_SPEC] * len(after),
        out_specs=[row(D_MODEL), _full((1, D_MODEL))],
        out_shape=[jax.ShapeDtypeStruct((T, D_MODEL), F32), jax.ShapeDtypeStruct((1, D_MODEL), F32)],
        compiler_params=_params(("arbitrary",)),
    )(*dparts, w_in, x, dx1, g, *after)


def _wgrad_ffn(h2t, dgate, dup, dx2t, act, tn=256):
    M, T = h2t.shape
    N = dgate.shape[1]

    def body(h_ref, x_ref, dg_ref, du_ref, act_ref, og_ref, ou_ref, od_ref):
        h = h_ref[...]
        og_ref[...] = _mm(h, dg_ref[...]).T.astype(BF16)
        ou_ref[...] = _mm(h, du_ref[...]).T.astype(BF16)
        od_ref[...] = _mm(x_ref[...], act_ref[...]).T.astype(BF16)

    rhs = pl.BlockSpec((T, tn), lambda j: (0, j))
    out_spec = pl.BlockSpec((tn, M), lambda j: (j, 0))
    out = jax.ShapeDtypeStruct((N, M), BF16)
    return pl.pallas_call(
        body, name="wgrad_ffn", grid=(N // tn,),
        in_specs=[_resident((M, T)), _resident((M, T)), rhs, rhs, rhs], out_specs=[out_spec] * 3,
        out_shape=[out, out, out],
        compiler_params=_params(("parallel",)),
    )(h2t, dx2t, dgate, dup, act)


def _wgrad_out_branches(ogt, dya, cvot, dyb, mgt, dx1, after=()):
    M, T = ogt.shape
    N = dya.shape[1]
    c = N // N_DEV
    per = 2
    tn = per * c

    def body(at_ref, da_ref, bt_ref, db_ref, mt_ref, dx_ref, oa_ref, ob_ref, oo_ref):
        ga = _mm(at_ref[...], da_ref[...])
        gb = _mm(bt_ref[...], db_ref[...])
        for s in range(per):
            oa_ref[s] = ga[:, s * c:(s + 1) * c].astype(BF16)
            ob_ref[s] = gb[:, s * c:(s + 1) * c].astype(BF16)
        oo_ref[...] = _mm(mt_ref[...], dx_ref[...]).astype(BF16)

    rhs = pl.BlockSpec((T, tn), lambda j: (0, j))
    owners = pl.BlockSpec((per, M, c), lambda j: (j, 0, 0))
    out = jax.ShapeDtypeStruct((N_DEV, M, c), BF16)
    return pl.pallas_call(
        _drop_operands(body, 6, len(after)), name="wgrad_out_branches", grid=(N // tn,),
        in_specs=[_resident((M, T)), rhs, _resident((M, T)), rhs, _resident(mgt.shape), rhs] + [HBM_SPEC] * len(after),
        out_specs=[owners, owners, pl.BlockSpec((mgt.shape[0], tn), lambda j: (0, j))],
        out_shape=[out, out, jax.ShapeDtypeStruct((mgt.shape[0], dx1.shape[1]), BF16)],
        compiler_params=_params(("parallel",)),
    )(ogt, dya, cvot, dyb, mgt, dx1, *after)


def _wgrad_in(ht, dparts, after=(), riders=()):
    M, T = ht.shape
    tn = 512
    nblk = [p.shape[1] // tn for p in dparts]
    start = [sum(nblk[:i]) for i in range(len(nblk))]
    n = len(dparts)
    steps = sum(nblk)
    nr = len(riders)
    rows = [r[0].shape[0] // steps for r in riders]
    assert all(r[0].shape[0] == rr * steps and rr % 16 == 0 for r, rr in zip(riders, rows))

    def body(a_ref, *refs):
        d_refs = refs[:n]
        rider_in = refs[n:n + 4 * nr]
        o_ref = refs[n + 4 * nr]
        rider_out = refs[n + 4 * nr + 1:]
        j = pl.program_id(0)
        for d_ref, s, nb in zip(d_refs, start, nblk):
            @pl.when((j >= s) & (j < s + nb))
            def _():
                o_ref[...] = _mm(a_ref[...], d_ref[...]).T.astype(BF16)
        for i in range(nr):
            w_ref, p_ref, m_ref, v_ref = rider_in[4 * i:4 * i + 4]
            g = p_ref[0].astype(F32)
            for k in range(1, 4):
                g = g + p_ref[k].astype(F32)
            delta, m_new, v_new = _adamw_math(w_ref[...], g, m_ref[...], v_ref[...])
            for k, val in enumerate((g, delta, m_new, v_new)):
                rider_out[4 * i + k][...] = val

    def piece_spec(s, nb):
        return pl.BlockSpec((T, tn), lambda j: (0, jnp.clip(j - s, 0, nb - 1)))

    rider_specs, rider_out_specs, rider_out_shape, rider_args = [], [], [], []
    for (w, parts, m, v), rr in zip(riders, rows):
        blk = pl.BlockSpec((rr, w.shape[1]), lambda j: (j, 0))
        rider_specs += [blk, pl.BlockSpec((4, rr, w.shape[1]), lambda j: (0, j, 0)), blk, blk]
        rider_out_specs += [blk] * 4
        rider_out_shape += [jax.ShapeDtypeStruct(w.shape, F32)] * 4
        rider_args += [w, parts, m, v]
    outs = pl.pallas_call(
        _drop_operands(body, 1 + n + 4 * nr, len(after)), name="wgrad_in", grid=(steps,),
        in_specs=[_resident((M, T))] + [piece_spec(s, nb) for s, nb in zip(start, nblk)] + rider_specs
                 + [HBM_SPEC] * len(after),
        out_specs=[pl.BlockSpec((tn, M), lambda j: (j, 0))] + rider_out_specs,
        out_shape=[jax.ShapeDtypeStruct((steps * tn, M), BF16)] + rider_out_shape,
        compiler_params=_params(("parallel",)),
    )(ht, *dparts, *rider_args, *after)
    return outs[0], [outs[1 + 4 * i:5 + 4 * i] for i in range(nr)]


def _adamw_math(w, g, m, v):
    m = ADAM_B1 * m + (1.0 - ADAM_B1) * g
    v = ADAM_B2 * v + (1.0 - ADAM_B2) * (g * g)
    m_hat = m / (1.0 - ADAM_B1 ** ADAM_STEP)
    v_hat = v / (1.0 - ADAM_B2 ** ADAM_STEP)
    delta = -ADAM_LR * (m_hat / (jnp.sqrt(v_hat) + ADAM_EPS) + ADAM_WD * w)
    return delta, m, v


def _adamw_sum(name, ws, parts, ms, vs):
    n = len(ws)
    steps = min(_row_steps(w.shape[0]) for w in ws)
    rows = [w.shape[0] // steps for w in ws]

    def body(*refs):
        w_refs, p_refs, m_refs, v_refs = (refs[k * n:(k + 1) * n] for k in range(4))
        out_refs = refs[4 * n:]
        for i in range(n):
            g = p_refs[i][0].astype(F32)
            for k in range(1, 4):
                g = g + p_refs[i][k].astype(F32)
            delta, m_new, v_new = _adamw_math(w_refs[i][...], g, m_refs[i][...], v_refs[i][...])
            for k, val in enumerate((g, delta, m_new, v_new)):
                out_refs[4 * i + k][...] = val

    blk = [pl.BlockSpec((r, w.shape[1]), lambda s: (s, 0)) for r, w in zip(rows, ws)]
    pblk = [pl.BlockSpec((4, r, w.shape[1]), lambda s: (0, s, 0)) for r, w in zip(rows, ws)]
    out_specs, out_shape = [], []
    for b, w in zip(blk, ws):
        out_specs += [b] * 4
        out_shape += [jax.ShapeDtypeStruct(w.shape, F32)] * 4
    flat = pl.pallas_call(
        body, name=name, grid=(steps,),
        in_specs=blk + pblk + blk + blk, out_specs=out_specs, out_shape=out_shape,
        compiler_params=_params(("parallel",)),
    )(*ws, *parts, *ms, *vs)
    return [flat[4 * i:4 * i + 4] for i in range(n)]


_SMALL_SLOTS = (("norm_mix_g", 0, 1, 1024), ("norm_ffn_g", 1, 1, 1024), ("norm_final_g", 2, 1, 1024),
                ("lower_bounds", 3, 2, 512), ("hg_norm_g", 5, 1, 128), ("loss", 6, 1, 128), ("conv_w", 8, 3, 512))
_SMALL_PARAMS = tuple(s for s in _SMALL_SLOTS if s[0] != "loss")
CONV_SHARD = CONV_WIDTH // N_DEV


def _small_pack(small):
    def body(*refs):
        out = refs[-1]
        out[...] = jnp.zeros_like(out)
        for ref, (_, row, rows, lanes) in zip(refs[:-1], _SMALL_SLOTS):
            out[row:row + rows, 0:lanes] = ref[...]

    vmem = pl.BlockSpec(memory_space=pltpu.VMEM)
    return pl.pallas_call(
        body, name="small_pack", in_specs=[vmem] * len(_SMALL_SLOTS), out_specs=vmem,
        out_shape=jax.ShapeDtypeStruct((SMALL_ROWS, 1024), F32),
    )(*[small[name] for name, _, _, _ in _SMALL_SLOTS])


def _small_update(gathered, dev, w, m, v):
    n = len(_SMALL_PARAMS)

    def body(dev_ref, g_ref, *refs):
        w_refs, m_refs, v_refs = refs[:n], refs[n:2 * n], refs[2 * n:3 * n]
        loss_ref, out_refs, sum_scr = refs[3 * n], refs[3 * n + 1:-1], refs[-1]
        total = g_ref[0]
        for k in range(1, N_DEV):
            total = total + g_ref[k]
        sum_scr[...] = total
        loss_ref[...] = sum_scr[6:7, 0:128]
        for p, (name, row, rows, lanes) in enumerate(_SMALL_PARAMS):
            if name == "conv_w":
                for r in range(rows):
                    g = sum_scr[row + r:row + r + 1, 0:CONV_SHARD]
                    for s in range(1, N_DEV):
                        mine = sum_scr[row + r:row + r + 1, s * CONV_SHARD:(s + 1) * CONV_SHARD]
                        g = jnp.where(dev_ref[0] == s, mine, g)
                    delta, m_new, v_new = _adamw_math(w_refs[p][r], g, m_refs[p][r], v_refs[p][r])
                    out_refs[4 * p][r] = g
                    out_refs[4 * p + 1][r] = delta
                    out_refs[4 * p + 2][r] = m_new
                    out_refs[4 * p + 3][r] = v_new
                continue
            g = sum_scr[row:row + rows, 0:lanes]
            delta, m_new, v_new = _adamw_math(w_refs[p][...], g, m_refs[p][...], v_refs[p][...])
            out_refs[4 * p][...] = g
            out_refs[4 * p + 1][...] = delta
            out_refs[4 * p + 2][...] = m_new
            out_refs[4 * p + 3][...] = v_new

    vmem = pl.BlockSpec(memory_space=pltpu.VMEM)
    outs = [jax.ShapeDtypeStruct((1, 128), F32)]
    for a in w:
        outs += [jax.ShapeDtypeStruct(a.shape, F32)] * 4
    return pl.pallas_call(
        body, name="small_update",
        in_specs=[pl.BlockSpec(memory_space=pltpu.SMEM)] + [vmem] * (1 + 3 * n), out_specs=[vmem] * len(outs),
        out_shape=outs, scratch_shapes=[pltpu.VMEM((SMALL_ROWS, 1024), F32)],
    )(dev, gathered, *w, *m, *v)


def _row_steps(rows):
    for steps in (4, 2):
        if rows % (16 * steps) == 0:
            return steps
    return 1


def _pair_sum(name, by_owner, got, core, after=()):
    n = len(got)

    def body(core_ref, *refs):
        for a_ref, b_ref, o_ref in zip(refs[:n], refs[n:2 * n], refs[2 * n:]):
            o_ref[...] = (a_ref[...].astype(F32) + b_ref[...].astype(F32)).astype(BF16)

    def blk(g):
        return pl.BlockSpec((None,) + g.shape[1:], lambda k, core_ref: (k, 0, 0))

    def mine(g):
        return pl.BlockSpec((None,) + g.shape[1:], lambda k, core_ref: (2 * k + core_ref[0], 0, 0))

    return pl.pallas_call(
        _drop_operands(body, 1 + 2 * n, len(after)), name=name,
        grid_spec=pltpu.PrefetchScalarGridSpec(
            num_scalar_prefetch=1, grid=(4,),
            in_specs=[mine(g) for g in got] + [blk(g) for g in got] + [HBM_SPEC] * len(after),
            out_specs=[blk(g) for g in got]),
        out_shape=[jax.ShapeDtypeStruct(g.shape, BF16) for g in got],
        compiler_params=_params(("parallel",)),
    )(core, *by_owner, *got, *after)


MESH = pl.DeviceIdType.MESH
HBM_SPEC = pl.BlockSpec(memory_space=pl.ANY)


def _handshake(peers):
    barrier = pltpu.get_barrier_semaphore()
    for peer in peers:
        pl.semaphore_signal(barrier, inc=1, device_id=peer, device_id_type=MESH)
    pl.semaphore_wait(barrier, len(peers))


def _comm_call(body, name, operands, out_shape, scratch, collective_id):
    if collective_id is None:
        return pl.pallas_call(body, name=name, in_specs=[HBM_SPEC] * len(operands), out_specs=[HBM_SPEC] * len(out_shape),
                              out_shape=out_shape, scratch_shapes=scratch)(*operands)
    return pl.kernel(body, out_type=out_shape, mesh=plsc.ScalarSubcoreMesh(axis_name="sequencer", num_cores=1),
                     scratch_types=scratch, name=name,
                     compiler_params=pltpu.CompilerParams(collective_id=collective_id))(*operands)


def _all_gather(name, blocks, collective_id=None, after=(), pieces=None):
    n = len(blocks)
    na = len(after)
    pieces = pieces or [1] * n
    parts = []
    for i, (b, k) in enumerate(zip(blocks, pieces)):
        rows, rem = divmod(b.shape[0], k)
        assert rem == 0 and (k == 1 or rows % 16 == 0), (b.shape, k)
        parts += [(i, None, None)] if k == 1 else [(i, j * rows, rows) for j in range(k)]
    np_ = len(parts)

    def body(*refs):
        x_refs, out_refs = refs[:n], refs[n + na:2 * n + na]
        send_sems, recv_sems, local_sems = refs[2 * n + na:]
        x, y, c = lax.axis_index("x"), lax.axis_index("y"), lax.axis_index("c")
        me, sibling = (x, y, c), (x, y, 1 - c)
        chips = [(1 - x, y), (x, 1 - y), (1 - x, 1 - y)]
        if collective_id is not None:
            _handshake([sibling] + [(*chip, c) for chip in chips])

        def slot(p, px, py, pc):
            i, r0, rows = parts[p]
            whole = out_refs[i].at[4 * px + 2 * py + pc]
            return whole if r0 is None else whole.at[pl.ds(r0, rows)]

        def own(p):
            i, r0, rows = parts[p]
            return x_refs[i] if r0 is None else x_refs[i].at[pl.ds(r0, rows)]

        def copy(p, k, blk, to, src=None):
            return pltpu.make_async_remote_copy(
                src_ref=slot(p, *blk) if src is None else src, dst_ref=slot(p, *blk),
                send_sem=send_sems.at[7 * p + k], recv_sem=recv_sems.at[7 * p + k], device_id=to, device_id_type=MESH)

        mine = [pltpu.make_async_copy(own(p), slot(p, *me), local_sems.at[p]) for p in range(np_)]
        for cp in mine:
            cp.start()
        first = []
        for p in range(np_):
            first.append(copy(p, 0, me, sibling, src=own(p)))
            first += [copy(p, 1 + j, me, (*chip, c), src=own(p)) for j, chip in enumerate(chips)]
        for cp in first:
            cp.start()
        passed = []
        for p in range(np_):
            for j, chip in enumerate(chips):
                copy(p, 1 + j, (*chip, c), me).wait_recv()
                passed.append(copy(p, 4 + j, (*chip, c), sibling))
                passed[-1].start()
        for p in range(np_):
            copy(p, 0, sibling, me).wait_recv()
            for j, chip in enumerate(chips):
                copy(p, 4 + j, (*chip, 1 - c), me).wait_recv()
        for cp in first + passed:
            cp.wait_send()
        for cp in mine:
            cp.wait()

    return _comm_call(
        body, name, list(blocks) + list(after), [jax.ShapeDtypeStruct((N_DEV,) + b.shape, b.dtype) for b in blocks],
        [pltpu.SemaphoreType.DMA((7 * np_,)), pltpu.SemaphoreType.DMA((7 * np_,)), pltpu.SemaphoreType.DMA((np_,))],
        collective_id)


def _sibling_swap(name, by_owner, collective_id=None, after=()):
    n = len(by_owner)
    na = len(after)

    def body(*refs):
        x_refs, out_refs = refs[:n], refs[n + na:2 * n + na]
        send_sems, recv_sems = refs[2 * n + na:]
        x, y, c = lax.axis_index("x"), lax.axis_index("y"), lax.axis_index("c")
        if collective_id is not None:
            _handshake([(x, y, 1 - c)])
        copies = []
        for i in range(n):
            for k in range(4):
                copies.append(pltpu.make_async_remote_copy(
                    src_ref=x_refs[i].at[2 * k + 1 - c], dst_ref=out_refs[i].at[k],
                    send_sem=send_sems.at[4 * i + k], recv_sem=recv_sems.at[4 * i + k],
                    device_id=(x, y, 1 - c), device_id_type=MESH))
        for cp in copies:
            cp.start()
        for cp in copies:
            cp.wait()

    return _comm_call(
        body, name, list(by_owner) + list(after),
        [jax.ShapeDtypeStruct((4,) + b.shape[1:], b.dtype) for b in by_owner],
        [pltpu.SemaphoreType.DMA((4 * n,)), pltpu.SemaphoreType.DMA((4 * n,))], collective_id)


def _chip_exchange(name, sums, collective_id=None, after=()):
    n = len(sums)
    na = len(after)

    def body(*refs):
        x_refs, out_refs = refs[:n], refs[n + na:2 * n + na]
        send_sems, recv_sems, local_sems = refs[2 * n + na:]
        x, y, c = lax.axis_index("x"), lax.axis_index("y"), lax.axis_index("c")
        chips = [(1 - x, y), (x, 1 - y), (1 - x, 1 - y)]
        my_chip = 2 * x + y
        if collective_id is not None:
            _handshake([(cx, cy, c) for cx, cy in chips])
        mine = [pltpu.make_async_copy(x_refs[i].at[my_chip], out_refs[i].at[my_chip], local_sems.at[i])
                for i in range(n)]
        for cp in mine:
            cp.start()
        sends = []
        for i in range(n):
            for j, (cx, cy) in enumerate(chips):
                sends.append(pltpu.make_async_remote_copy(
                    src_ref=x_refs[i].at[2 * cx + cy], dst_ref=out_refs[i].at[my_chip],
                    send_sem=send_sems.at[3 * i + j], recv_sem=recv_sems.at[3 * i + j],
                    device_id=(cx, cy, c), device_id_type=MESH))
        for cp in sends:
            cp.start()
        for i in range(n):
            for j, (cx, cy) in enumerate(chips):
                pltpu.make_async_remote_copy(
                    src_ref=x_refs[i].at[my_chip], dst_ref=out_refs[i].at[2 * cx + cy],
                    send_sem=send_sems.at[3 * i + j], recv_sem=recv_sems.at[3 * i + j],
                    device_id=(cx, cy, c), device_id_type=MESH).wait_recv()
        for cp in sends:
            cp.wait_send()
        for cp in mine:
            cp.wait()

    return _comm_call(
        body, name, list(sums) + list(after), [jax.ShapeDtypeStruct(s.shape, s.dtype) for s in sums],
        [pltpu.SemaphoreType.DMA((3 * n,)), pltpu.SemaphoreType.DMA((3 * n,)), pltpu.SemaphoreType.DMA((n,))],
        collective_id)


def _cast_shards(name, shards):
    n = len(shards)
    steps = min(_row_steps(s.shape[0]) for s in shards)

    def body(*refs):
        for i in range(n):
            refs[n + i][...] = refs[i][...].astype(BF16)

    blocks = [pl.BlockSpec((s.shape[0] // steps, s.shape[1]), lambda i: (i, 0)) for s in shards]
    return pl.pallas_call(
        body, name=name, grid=(steps,), in_specs=blocks, out_specs=blocks,
        out_shape=[jax.ShapeDtypeStruct(s.shape, BF16) for s in shards],
        compiler_params=_params(("parallel",)),
    )(*shards)


BIG = ("w_in", "w_branch_a", "w_branch_b", "w_out", "w_ffn_gate", "w_ffn_up", "w_ffn_down")


def _local_step(x, target, gains, low, conv_w, wg8, reduce):
    g_mix, g_hg, g_ffn, g_fin = gains
    w_in = wg8["w_in"].reshape(N_IN, D_MODEL)
    wg = wg8["w_ffn_gate"].reshape(D_FF, D_MODEL)
    wu = wg8["w_ffn_up"].reshape(D_FF, D_MODEL)
    wa, wb = wg8["w_branch_a"], wg8["w_branch_b"]
    wo = wg8["w_out"].reshape(D_MODEL, D_MODEL)
    wd = wg8["w_ffn_down"].reshape(D_FF, D_MODEL)

    ht, hg, cv, gt, cvo, cvot, cy, o, og, ogt, st = _fwd_in(x, g_mix, w_in, conv_w, low, g_hg)
    x1, mgt, ya, yb = _merge_fwd(og, cvo, gt, x, wa, wb, wo)
    h2t, gate, up, act, loss, d_gfin, dx2, dx2t = _ffn_fwd_loss(x1, g_ffn, wg, wu, wd, target, g_fin)

    dgate, dup, dx1, d_gffn = _ffn_bwd(dx2, x1, gate, up, g_ffn, wg, wu, wd)
    d_wg, d_wu, d_wd = _wgrad_ffn(h2t, dgate, dup, dx2t, act)
    by_owner_ffn = lambda a: a.reshape(N_DEV, D_FF // N_DEV, D_MODEL)
    late = ("w_ffn_gate", "w_ffn_up")
    ffn = dict(w_ffn_gate=by_owner_ffn(d_wg), w_ffn_up=by_owner_ffn(d_wu), w_ffn_down=by_owner_ffn(d_wd))
    dgt, dya, dyb, dog, dcv, d_conv = _merge_bwd(dx1, ya, yb, gt, cv, cy, wa, wb, wo, conv_w)
    grad_a, grad_b, grad_o = _wgrad_out_branches(ogt, dya, cvot, dyb, mgt, dx1)
    out = dict(w_out=grad_o.reshape(N_DEV, D_MODEL // N_DEV, D_MODEL), w_branch_a=grad_a, w_branch_b=grad_b)
    sums_ffn, got_ffn = reduce.begin(ffn, sum_after=[grad_o])
    parts_ffn, _ = reduce.finish({n: ffn[n] for n in late}, sums_ffn[:2], defer=late)
    dhg, d_low, d_ghg = _hg_bwd(dog, hg, o, st, low, g_hg, after=list(sums_ffn))
    sums_out, got_out = reduce.begin(out, sum_after=[dhg])
    parts_out, updated_out = reduce.finish(dict(out, w_ffn_down=ffn["w_ffn_down"]), list(sums_out) + [sums_ffn[2]])
    dparts = [dhg, dcv, dgt]
    d_w_in_t, ridden = _wgrad_in(ht, dparts, after=sums_out[:1],
                                 riders=reduce.riders(late, dict(zip(late, parts_ffn))))
    reduce.record(late, ridden)
    w_in_grad = dict(w_in=d_w_in_t.reshape(N_DEV, N_IN // N_DEV, D_MODEL))
    sums_in, _ = reduce.begin(w_in_grad, after=parts_out[:1], sum_after=updated_out)
    parts_in, _ = reduce.finish(w_in_grad, sums_in)
    grad_x, d_gmix = _in_bwd(dparts, w_in, x, dx1, g_mix, after=list(parts_out[:1]) + list(sums_in))
    small = dict(norm_mix_g=d_gmix, norm_ffn_g=d_gffn, norm_final_g=d_gfin, lower_bounds=d_low, hg_norm_g=d_ghg,
                 conv_w=d_conv, loss=loss)
    return grad_x, small, parts_in


def kernel(x, norm_mix_g, w_in, lower_bounds, hg_norm_g, conv_w, w_branch_a, w_branch_b, w_out, norm_ffn_g, w_ffn_gate, w_ffn_up, w_ffn_down, norm_final_g, loss_target, m_norm_mix_g, m_w_in, m_lower_bounds, m_hg_norm_g, m_conv_w, m_w_branch_a, m_w_branch_b, m_w_out, m_norm_ffn_g, m_w_ffn_gate, m_w_ffn_up, m_w_ffn_down, m_norm_final_g, v_norm_mix_g, v_w_in, v_lower_bounds, v_hg_norm_g, v_conv_w, v_w_branch_a, v_w_branch_b, v_w_out, v_norm_ffn_g, v_w_ffn_gate, v_w_ffn_up, v_w_ffn_down, v_norm_final_g):
    cx, cy, cc = lax.axis_index("x"), lax.axis_index("y"), lax.axis_index("c")
    my_dev = 4 * cx + 2 * cy + cc

    def tr(a):
        return a[0].T

    big = dict(w_in=tr(w_in), w_branch_a=w_branch_a[0], w_branch_b=w_branch_b[0], w_out=w_out[0],
               w_ffn_gate=tr(w_ffn_gate), w_ffn_up=tr(w_ffn_up), w_ffn_down=w_ffn_down[0])
    big_m = dict(w_in=tr(m_w_in), w_branch_a=m_w_branch_a[0], w_branch_b=m_w_branch_b[0], w_out=m_w_out[0],
                 w_ffn_gate=tr(m_w_ffn_gate), w_ffn_up=tr(m_w_ffn_up), w_ffn_down=m_w_ffn_down[0])
    big_v = dict(w_in=tr(v_w_in), w_branch_a=v_w_branch_a[0], w_branch_b=v_w_branch_b[0], w_out=v_w_out[0],
                 w_ffn_gate=tr(v_w_ffn_gate), w_ffn_up=tr(v_w_ffn_up), w_ffn_down=v_w_ffn_down[0])
    transposed = ("w_in", "w_ffn_gate", "w_ffn_up")

    ids = iter(range(1, 16))
    shards = dict(zip(BIG[:1], _cast_shards("cast_w_in", [big["w_in"]])))
    first = _all_gather("gather_w_in", [shards["w_in"], conv_w.transpose(1, 0, 2)], collective_id=next(ids),
                        pieces=[4, 1])
    shards.update(zip(BIG[1:], _cast_shards("cast_shards", [big[n] for n in BIG[1:]])))
    mid = _all_gather("gather_mid", [shards[n] for n in BIG[1:4]], collective_id=next(ids))
    ffn = _all_gather("gather_ffn", [shards[n] for n in BIG[4:]], collective_id=next(ids), pieces=[2, 2, 2])
    wg8 = dict(zip(BIG, [first[0]] + list(mid) + list(ffn)))
    conv_full = first[1].transpose(1, 2, 0, 3).reshape(3, CONV_WIDTH)

    core = cc.reshape(1).astype(jnp.int32)
    outs = {}

    class Reduce:
        @staticmethod
        def begin(grads, after=(), sum_after=()):
            names = list(grads)
            by_owner = [grads[n] for n in names]
            got = _sibling_swap("sibling_swap_" + names[0], by_owner, collective_id=next(ids), after=after)
            sums = _pair_sum("pair_sum_" + names[0], by_owner, got, core, after=sum_after)
            return sums, got

        @staticmethod
        def finish(grads, chip_sums, after=(), defer=()):
            names = list(grads)
            parts = _chip_exchange("chip_exchange_" + names[0], chip_sums, collective_id=next(ids), after=after)
            now = [n for n in names if n not in defer]
            if now:
                updated = _adamw_sum("adamw_" + now[0], [big[n] for n in now],
                                     [p for n, p in zip(names, parts) if n in now],
                                     [big_m[n] for n in now], [big_v[n] for n in now])
                outs.update(zip(now, updated))
            return parts, [outs[n][1] for n in now]

        @staticmethod
        def riders(names, parts):
            return [(big[n], parts[n], big_m[n], big_v[n]) for n in names]

        @staticmethod
        def record(names, updated):
            outs.update(zip(names, updated))

    gains = (norm_mix_g, hg_norm_g, norm_ffn_g, norm_final_g.reshape(1, D_MODEL))
    grad_x, small, last = _local_step(x[0], loss_target[0], gains, lower_bounds, conv_full, wg8, Reduce)

    small_all = _all_gather("gather_small", [_small_pack(small)], collective_id=next(ids), after=last[:1])

    def small_state(a):
        return [a[0], a[1], a[2].reshape(1, D_MODEL), a[3], a[4], a[5].transpose(1, 0, 2)]

    upd = _small_update(
        small_all[0], my_dev.reshape(1).astype(jnp.int32),
        small_state((norm_mix_g, norm_ffn_g, norm_final_g, lower_bounds, hg_norm_g, conv_w)),
        small_state((m_norm_mix_g, m_norm_ffn_g, m_norm_final_g, m_lower_bounds, m_hg_norm_g, m_conv_w)),
        small_state((v_norm_mix_g, v_norm_ffn_g, v_norm_final_g, v_lower_bounds, v_hg_norm_g, v_conv_w)))
    loss = upd[0][0, 0]
    for p, (name, _, _, _) in enumerate(_SMALL_PARAMS):
        outs[name] = upd[1 + 4 * p:5 + 4 * p]
    outs["norm_final_g"] = [a.reshape(D_MODEL) for a in outs["norm_final_g"]]
    outs["conv_w"] = [a.transpose(1, 0, 2) for a in outs["conv_w"]]

    order = ["norm_mix_g", "w_in", "lower_bounds", "hg_norm_g", "conv_w", "w_branch_a", "w_branch_b", "w_out",
             "norm_ffn_g", "w_ffn_gate", "w_ffn_up", "w_ffn_down", "norm_final_g"]
    result = [loss, grad_x[None]]
    for k in range(4):
        for n in order:
            if n in BIG:
                result.append((outs[n][k].T if n in transposed else outs[n][k])[None])
            else:
                result.append(outs[n][k])
    return tuple(result)
```

```python
import jax
import jax.numpy as jnp
from jax import lax
from jax.experimental import pallas as pl
from jax.experimental.pallas import tpu as pltpu
from jax.experimental.pallas import tpu_sc as plsc

F32 = jnp.float32
BF16 = jnp.bfloat16
STASH = jnp.bfloat16

D_MODEL = 1024
HG_WIDTH = 512
HEAD_DIM = 128
N_HEADS = 4
HEADS_PER_STEP = 4
HEAD_GROUPS = N_HEADS // HEADS_PER_STEP
CONV_WIDTH = 512
CONV_K = 3
D_FF = 2816
CHUNK = 32
EPS = 1e-6
Q_SCALE = HEAD_DIM ** -0.5
N_DEV = 8

ADAM_LR = 0.001
ADAM_B1 = 0.9
ADAM_B2 = 0.999
ADAM_EPS = 1e-08
ADAM_WD = 0.01
ADAM_STEP = 10

VMEM_LIMIT_V7X = 56 * 1024 * 1024
VMEM_LIMIT_LARGE_V7X = 62 * 1024 * 1024

SMALL_ROWS = 16


def _params(sem, vmem=VMEM_LIMIT_V7X):
    return pltpu.CompilerParams(dimension_semantics=sem, vmem_limit_bytes=vmem)


def _mm(a, b):
    return jnp.dot(a.astype(BF16), b.astype(BF16), preferred_element_type=F32)


def _mm_nt(a, b):
    return lax.dot_general(a.astype(BF16), b.astype(BF16), (((1,), (1,)), ((), ())), preferred_element_type=F32)


def _mm_tn(a, b):
    return lax.dot_general(a.astype(BF16), b.astype(BF16), (((0,), (0,)), ((), ())), preferred_element_type=F32)


def _sigmoid(x):
    return 0.5 * jnp.tanh(0.5 * x) + 0.5


def _resident(shape):
    nd = len(shape)
    return pl.BlockSpec(shape, lambda *_: (0,) * nd, pipeline_mode=pl.Buffered(1))


def _full(shape):
    nd = len(shape)
    return pl.BlockSpec(shape, lambda *_: (0,) * nd)


def _shard_cols(w_ref):
    return jnp.concatenate([w_ref[s] for s in range(N_DEV)], axis=1)


N_HG = 4 * HG_WIDTH
N_CV = 3 * CONV_WIDTH
N_GT = 2 * D_MODEL
N_IN = N_HG + N_CV + N_GT


def _col(tm, n):
    return pl.BlockSpec((n, tm), lambda i: (0, i))


HALO = 8


def _fwd_in(x, g, w_in_t, conv_w, low, gn):
    T = x.shape[0]
    tm = min(512, T)
    nc = tm // CHUNK

    def body(x_ref, g_ref, w_ref, cw_ref, low_ref, gn_ref, ht_ref, hg_ref, cv_ref, gt_ref, cvo_ref, cvot_ref, cy_ref,
             o_ref, og_ref, ogt_ref, st_ref, tail_scr, s_scr):
        @pl.when(pl.program_id(0) == 0)
        def _():
            tail_scr[...] = jnp.zeros_like(tail_scr)
            s_scr[...] = jnp.zeros_like(s_scr)

        xv = x_ref[...]
        r = lax.rsqrt(jnp.mean(xv * xv, axis=-1, keepdims=True) + EPS)
        hf = xv * r * g_ref[...]
        h = hf.astype(BF16)
        ht_ref[...] = hf.T.astype(BF16)
        hg_ref[...] = _mm_nt(h, w_ref[:N_HG, :])
        cv = _mm_nt(h, w_ref[N_HG:N_HG + N_CV, :])
        cv_ref[...] = cv.astype(STASH)
        gt_ref[...] = _mm_nt(h, w_ref[N_HG + N_CV:, :]).astype(STASH)

        u = cv[:, :CONV_WIDTH] * cv[:, 2 * CONV_WIDTH:]
        row = lax.broadcasted_iota(jnp.int32, u.shape, 0)
        prev1 = tail_scr[HALO - 1:HALO, :]
        prev2 = tail_scr[HALO - 2:HALO - 1, :]
        u1 = jnp.where(row >= 1, pltpu.roll(u, 1, 0), prev1)
        u2 = jnp.where(row >= 2, pltpu.roll(u, 2, 0), jnp.where(row == 1, prev1, prev2))
        y = cw_ref[0:1, :] * u2 + cw_ref[1:2, :] * u1 + cw_ref[2:3, :] * u
        cy_ref[...] = y.astype(STASH)
        out = cv[:, CONV_WIDTH:2 * CONV_WIDTH] * y
        cvo_ref[...] = out.astype(BF16)
        cvot_ref[...] = out.T.astype(BF16)
        tail_scr[...] = u[tm - HALO:, :]

        _hg_fwd_tile(hg_ref, low_ref, gn_ref, o_ref, og_ref, ogt_ref, st_ref, s_scr, tm)

    row = lambda n: pl.BlockSpec((tm, n), lambda i: (i, 0))
    return pl.pallas_call(
        body, name="fwd_in", grid=(T // tm,),
        in_specs=[row(D_MODEL), _full((1, D_MODEL)), _resident(w_in_t.shape), _full((CONV_K, CONV_WIDTH)),
                  _full((2, HG_WIDTH)), _full((1, HEAD_DIM))],
        out_specs=[_col(tm, D_MODEL), row(N_HG), row(N_CV), row(N_GT), row(CONV_WIDTH), _col(tm, CONV_WIDTH),
                   row(CONV_WIDTH),
                   row(HG_WIDTH), row(HG_WIDTH), _col(tm, HG_WIDTH),
                   pl.BlockSpec((N_HEADS, nc, HEAD_DIM, HEAD_DIM), lambda i: (0, i, 0, 0))],
        out_shape=[jax.ShapeDtypeStruct((D_MODEL, T), BF16), jax.ShapeDtypeStruct((T, N_HG), F32),
                   jax.ShapeDtypeStruct((T, N_CV), STASH), jax.ShapeDtypeStruct((T, N_GT), STASH),
                   jax.ShapeDtypeStruct((T, CONV_WIDTH), BF16), jax.ShapeDtypeStruct((CONV_WIDTH, T), BF16),
                   jax.ShapeDtypeStruct((T, CONV_WIDTH), STASH), jax.ShapeDtypeStruct((T, HG_WIDTH), F32), jax.ShapeDtypeStruct((T, HG_WIDTH), BF16),
                   jax.ShapeDtypeStruct((HG_WIDTH, T), BF16),
                   jax.ShapeDtypeStruct((N_HEADS, T // CHUNK, HEAD_DIM, HEAD_DIM), F32)],
        scratch_shapes=[pltpu.VMEM((HALO, CONV_WIDTH), F32), pltpu.VMEM((N_HEADS, HEAD_DIM, HEAD_DIM), F32)],
        compiler_params=_params(("arbitrary",), vmem=VMEM_LIMIT_LARGE_V7X),
    )(x, g, w_in_t, conv_w, low, gn)


def _chunk_pos(shape):
    return lax.broadcasted_iota(jnp.int32, shape, 0) & (CHUNK - 1)


def _chunk_cumsum(x, pos):
    s = 1
    while s < CHUNK:
        x = x + jnp.where(pos >= s, pltpu.roll(x, s, 0), 0.0)
        s *= 2
    return x


def _chunk_rev_cumsum(x, pos):
    n = x.shape[0]
    s = 1
    while s < CHUNK:
        x = x + jnp.where(pos + s < CHUNK, pltpu.roll(x, n - s, 0), 0.0)
        s *= 2
    return x


def _lower_bound(low_ref):
    l0 = low_ref[0:1, :]
    l1 = low_ref[1:2, :]
    m = jnp.maximum(l0, l1)
    e0 = jnp.exp(l0 - m)
    e1 = jnp.exp(l1 - m)
    return e0 / (e0 + e1), e1 / (e0 + e1)


def _hg_gates(qr, fr, lb, pos, tb):
    sq = _sigmoid(qr)
    q = qr * sq * Q_SCALE
    sg = _sigmoid(fr)
    f = lb + (1.0 - lb) * sg
    k = 1.0 - f
    b = _chunk_cumsum(jnp.log(f), pos)
    b3 = b.reshape(tb // CHUNK, CHUNK, HEAD_DIM)
    anc = b3[:, CHUNK // 2 - 1:CHUNK // 2, :]
    last = b3[:, CHUNK - 1:CHUNK, :]
    d3 = b3 - anc
    e_qa3 = jnp.exp(d3)
    e_ka3 = jnp.exp(-d3)
    e_b3 = e_qa3 * jnp.exp(anc)
    e_ko3 = e_ka3 * jnp.exp(last - anc)
    dec = jnp.exp(last)
    flat = lambda a: a.reshape(tb, HEAD_DIM)
    return sq, q, sg, f, k, flat(e_qa3), flat(e_ka3), flat(e_b3), flat(e_ko3), dec


def _intra_mask(sb):
    r = lax.broadcasted_iota(jnp.int32, (sb, sb), 0)
    c = lax.broadcasted_iota(jnp.int32, (sb, sb), 1)
    return ((r // CHUNK) == (c // CHUNK)) & (c <= r)


def _hg_fwd_tile(hg_ref, low_ref, gn_ref, o_ref, og_ref, ogt_ref, st_ref, s_scr, tb):
    sb = min(256, tb)
    nc = tb // CHUNK
    q_ref, f_ref, i_ref, g_ref = (hg_ref.at[:, p * HG_WIDTH:(p + 1) * HG_WIDTH] for p in range(4))
    pos = _chunk_pos((tb, HEAD_DIM))
    mask = _intra_mask(sb)
    lanes = [slice(hh * HEAD_DIM, (hh + 1) * HEAD_DIM) for hh in range(N_HEADS)]
    qi, ko, vb, dec, st = [], [], [], [], []
    for hh, ln in enumerate(lanes):
        lb, _ = _lower_bound(low_ref.at[:, ln])
        _, q, _, _, k, e_qa, e_ka, e_b, e_ko, dec_h = _hg_gates(q_ref[:, ln], f_ref[:, ln], lb, pos, tb)
        qh = (q * e_qa).astype(BF16)
        kh = (k * e_ka).astype(BF16)
        qi.append((q * e_b).astype(BF16))
        ko.append((k * e_ko).astype(BF16))
        vb.append(i_ref[:, ln].astype(BF16))
        dec.append(dec_h)
        st.append(s_scr[hh])
        for s in range(tb // sb):
            sl = slice(s * sb, (s + 1) * sb)
            p = jnp.where(mask, _mm_nt(qh[sl], kh[sl]), 0.0)
            o_ref[sl, ln] = _mm(p, vb[hh][sl])
    for c in range(nc):
        sl = slice(c * CHUNK, (c + 1) * CHUNK)
        for hh, ln in enumerate(lanes):
            st_ref[hh, c] = st[hh]
            o_ref[sl, ln] = o_ref[sl, ln] + _mm_nt(qi[hh][sl], st[hh])
            st[hh] = dec[hh][c] * st[hh] + _mm_tn(vb[hh][sl], ko[hh][sl])
    for hh, ln in enumerate(lanes):
        s_scr[hh] = st[hh]
        o = o_ref[:, ln]
        r = lax.rsqrt(jnp.mean(o * o, axis=-1, keepdims=True) + EPS)
        gr = g_ref[:, ln]
        og = (o * r * gn_ref[...]) * (gr * _sigmoid(gr))
        og_ref[:, ln] = og.astype(BF16)
        ogt_ref[ln, :] = og.T.astype(BF16)


def _merge_fwd(og, cvo, gt, x, wa, wb, wo):
    T = x.shape[0]
    tm = min(1024, T)

    def body(og_ref, cvo_ref, gt_ref, x_ref, wa_ref, wb_ref, wo_ref, x1_ref, mgt_ref, ya_ref, yb_ref):
        ya = jnp.dot(og_ref[...], _shard_cols(wa_ref), preferred_element_type=F32)
        yb = jnp.dot(cvo_ref[...], _shard_cols(wb_ref), preferred_element_type=F32)
        ya_ref[...] = ya.astype(STASH)
        yb_ref[...] = yb.astype(STASH)
        m = (_sigmoid(gt_ref[:, :D_MODEL].astype(F32)) * ya
             + _sigmoid(gt_ref[:, D_MODEL:].astype(F32)) * yb)
        mgt_ref[...] = m.T.astype(BF16)
        x1_ref[...] = x_ref[...] + jnp.dot(m.astype(BF16), wo_ref[...], preferred_element_type=F32)

    row = lambda n: pl.BlockSpec((tm, n), lambda i: (i, 0))
    return pl.pallas_call(
        body, name="merge_fwd", grid=(T // tm,),
        in_specs=[row(HG_WIDTH), row(CONV_WIDTH), row(2 * D_MODEL), row(D_MODEL),
                  _resident(wa.shape), _resident(wb.shape), _resident(wo.shape)],
        out_specs=[row(D_MODEL), _col(tm, D_MODEL), row(D_MODEL), row(D_MODEL)],
        out_shape=[jax.ShapeDtypeStruct((T, D_MODEL), F32), jax.ShapeDtypeStruct((D_MODEL, T), BF16),
                   jax.ShapeDtypeStruct((T, D_MODEL), STASH), jax.ShapeDtypeStruct((T, D_MODEL), STASH)],
        compiler_params=_params(("parallel",)),
    )(og, cvo, gt, x, wa, wb, wo)


def _ffn_fwd_loss(x1, g, wg, wu, wd, target, g_fin):
    T = x1.shape[0]
    tm = min(512, T)

    def body(x_ref, g_ref, wg_ref, wu_ref, wd_ref, t_ref, gf_ref,
             ht_ref, gate_ref, up_ref, act_ref, loss_ref, dgf_ref, dx2_ref, dx2t_ref):
        @pl.when(pl.program_id(0) == 0)
        def _():
            loss_ref[...] = jnp.zeros_like(loss_ref)
            dgf_ref[...] = jnp.zeros_like(dgf_ref)

        xv = x_ref[...]
        r = lax.rsqrt(jnp.mean(xv * xv, axis=-1, keepdims=True) + EPS)
        hf = xv * r * g_ref[...]
        h = hf.astype(BF16)
        ht_ref[...] = hf.T.astype(BF16)
        gate = _mm_nt(h, wg_ref[...])
        up = _mm_nt(h, wu_ref[...])
        gate_ref[...] = gate.astype(STASH)
        up_ref[...] = up.astype(STASH)
        act = (gate * _sigmoid(gate) * up).astype(BF16)
        act_ref[...] = act
        x2 = xv + jnp.dot(act, wd_ref[...], preferred_element_type=F32)

        gv = gf_ref[...]
        r2 = lax.rsqrt(jnp.mean(x2 * x2, axis=-1, keepdims=True) + EPS)
        xh = x2 * r2
        err = xh * gv - t_ref[...]
        loss_ref[...] += 0.5 * jnp.sum(jnp.mean(err * err, axis=-1, keepdims=True), axis=0, keepdims=True)
        dy = err * (1.0 / D_MODEL)
        dgf_ref[...] += jnp.sum(dy * xh, axis=0, keepdims=True)
        w = dy * gv
        dx2 = r2 * (w - xh * jnp.mean(w * xh, axis=-1, keepdims=True))
        dx2_ref[...] = dx2
        dx2t_ref[...] = dx2.T.astype(BF16)

    row = lambda n: pl.BlockSpec((tm, n), lambda i: (i, 0))
    return pl.pallas_call(
        body, name="ffn_fwd_loss", grid=(T // tm,),
        in_specs=[row(D_MODEL), _full((1, D_MODEL)), _resident(wg.shape), _resident(wu.shape), _resident(wd.shape),
                  row(D_MODEL), _full((1, D_MODEL))],
        out_specs=[_col(tm, D_MODEL), row(D_FF), row(D_FF), row(D_FF), _full((1, 128)), _full((1, D_MODEL)),
                   row(D_MODEL), _col(tm, D_MODEL)],
        out_shape=[jax.ShapeDtypeStruct((D_MODEL, T), BF16), jax.ShapeDtypeStruct((T, D_FF), STASH),
                   jax.ShapeDtypeStruct((T, D_FF), STASH), jax.ShapeDtypeStruct((T, D_FF), BF16),
                   jax.ShapeDtypeStruct((1, 128), F32), jax.ShapeDtypeStruct((1, D_MODEL), F32),
                   jax.ShapeDtypeStruct((T, D_MODEL), F32), jax.ShapeDtypeStruct((D_MODEL, T), BF16)],
        compiler_params=_params(("arbitrary",), vmem=VMEM_LIMIT_LARGE_V7X),
    )(x1, g, wg, wu, wd, target, g_fin)


def _ffn_bwd(dx2, x1, gate, up, g, wg, wu, wd):
    T = x1.shape[0]
    tm = min(512, T)

    def body(dx2_ref, x_ref, gate_ref, up_ref, g_ref, wg_ref, wu_ref, wd_ref, dgate_ref, dup_ref, dx1_ref, dgn_ref):
        @pl.when(pl.program_id(0) == 0)
        def _():
            dgn_ref[...] = jnp.zeros_like(dgn_ref)

        dx2 = dx2_ref[...]
        dact = _mm_nt(dx2, wd_ref[...])
        gate = gate_ref[...].astype(F32)
        s = _sigmoid(gate)
        dgate = (dact * up_ref[...].astype(F32) * (s * (1.0 + gate * (1.0 - s)))).astype(BF16)
        dup = (dact * (gate * s)).astype(BF16)
        dgate_ref[...] = dgate
        dup_ref[...] = dup
        dh = _mm(dgate, wg_ref[...]) + _mm(dup, wu_ref[...])
        xv = x_ref[...]
        r = lax.rsqrt(jnp.mean(xv * xv, axis=-1, keepdims=True) + EPS)
        xh = xv * r
        dgn_ref[...] += jnp.sum(dh * xh, axis=0, keepdims=True)
        w = dh * g_ref[...]
        dx1_ref[...] = dx2 + r * (w - xh * jnp.mean(w * xh, axis=-1, keepdims=True))

    row = lambda n: pl.BlockSpec((tm, n), lambda i: (i, 0))
    return pl.pallas_call(
        body, name="ffn_bwd", grid=(T // tm,),
        in_specs=[row(D_MODEL), row(D_MODEL), row(D_FF), row(D_FF), _full((1, D_MODEL)),
                  _resident(wg.shape), _resident(wu.shape), _resident(wd.shape)],
        out_specs=[row(D_FF), row(D_FF), row(D_MODEL), _full((1, D_MODEL))],
        out_shape=[jax.ShapeDtypeStruct((T, D_FF), BF16), jax.ShapeDtypeStruct((T, D_FF), BF16),
                   jax.ShapeDtypeStruct((T, D_MODEL), F32), jax.ShapeDtypeStruct((1, D_MODEL), F32)],
        compiler_params=_params(("arbitrary",), vmem=VMEM_LIMIT_LARGE_V7X),
    )(dx2, x1, gate, up, g, wg, wu, wd)


def _merge_bwd(dx1, ya, yb, gt, cv, cy, wa, wb, wo, conv_w):
    T = dx1.shape[0]
    tm = min(512, T)
    nt = T // tm

    def body(dx_ref, ya_ref, yb_ref, gt_ref, cv_ref, cy_ref, wa_ref, wb_ref, wo_ref, cw_ref,
             dgt_ref, dya_ref, dyb_ref, dog_ref, dcv_ref, dcw_ref, next_dy):
        @pl.when(pl.program_id(0) == 0)
        def _():
            next_dy[...] = jnp.zeros_like(next_dy)
            dcw_ref[...] = jnp.zeros_like(dcw_ref)

        dm = _mm_nt(dx_ref[...], wo_ref[...])
        wa = _shard_cols(wa_ref)
        wb = _shard_cols(wb_ref)
        ya = ya_ref[...].astype(F32)
        yb = yb_ref[...].astype(F32)
        sa = _sigmoid(gt_ref[:, :D_MODEL].astype(F32))
        sb = _sigmoid(gt_ref[:, D_MODEL:].astype(F32))
        da = dm * sa
        db = dm * sb
        dgt_ref[:, :D_MODEL] = (da * ya * (1.0 - sa)).astype(BF16)
        dgt_ref[:, D_MODEL:] = (db * yb * (1.0 - sb)).astype(BF16)
        dya = da.astype(BF16)
        dyb = db.astype(BF16)
        dya_ref[...] = dya
        dyb_ref[...] = dyb
        dog_ref[...] = _mm_nt(dya, wa)
        dcvo = _mm_nt(dyb, wb)

        cvt = cv_ref[...].astype(F32)
        c, bg, xb = cvt[:, :CONV_WIDTH], cvt[:, CONV_WIDTH:2 * CONV_WIDTH], cvt[:, 2 * CONV_WIDTH:]
        u = c * xb
        row = lax.broadcasted_iota(jnp.int32, u.shape, 0)
        w0, w1, w2 = cw_ref[0:1, :], cw_ref[1:2, :], cw_ref[2:3, :]
        dcv_ref[:, CONV_WIDTH:2 * CONV_WIDTH] = (dcvo * cy_ref[...].astype(F32)).astype(BF16)
        dy = dcvo * bg
        n1 = next_dy[0:1, :]
        n2 = next_dy[1:2, :]
        dy1 = jnp.where(row < tm - 1, pltpu.roll(dy, tm - 1, 0), n1)
        dy2 = jnp.where(row < tm - 2, pltpu.roll(dy, tm - 2, 0), jnp.where(row == tm - 2, n1, n2))
        dcw_ref[0:1, :] += jnp.sum(dy2 * u, axis=0, keepdims=True)
        dcw_ref[1:2, :] += jnp.sum(dy1 * u, axis=0, keepdims=True)
        dcw_ref[2:3, :] += jnp.sum(dy * u, axis=0, keepdims=True)
        du = w2 * dy + w1 * dy1 + w0 * dy2
        dcv_ref[:, :CONV_WIDTH] = (du * xb).astype(BF16)
        dcv_ref[:, 2 * CONV_WIDTH:] = (du * c).astype(BF16)
        next_dy[...] = dy[:HALO, :]

    rt = lambda i: nt - 1 - i
    row = lambda n: pl.BlockSpec((tm, n), lambda i: (rt(i), 0))
    return pl.pallas_call(
        body, name="merge_bwd", grid=(nt,),
        in_specs=[row(D_MODEL), row(D_MODEL), row(D_MODEL), row(2 * D_MODEL), row(N_CV), row(CONV_WIDTH),
                  _resident(wa.shape), _resident(wb.shape), _resident(wo.shape), _full((CONV_K, CONV_WIDTH))],
        out_specs=[row(2 * D_MODEL), row(D_MODEL), row(D_MODEL), row(HG_WIDTH), row(N_CV),
                   _full((CONV_K, CONV_WIDTH))],
        out_shape=[jax.ShapeDtypeStruct((T, 2 * D_MODEL), BF16), jax.ShapeDtypeStruct((T, D_MODEL), BF16),
                   jax.ShapeDtypeStruct((T, D_MODEL), BF16), jax.ShapeDtypeStruct((T, HG_WIDTH), F32),
                   jax.ShapeDtypeStruct((T, N_CV), BF16), jax.ShapeDtypeStruct((CONV_K, CONV_WIDTH), F32)],
        scratch_shapes=[pltpu.VMEM((HALO, CONV_WIDTH), F32)],
        compiler_params=_params(("arbitrary",)),
    )(dx1, ya, yb, gt, cv, cy, wa, wb, wo, conv_w)


def _drop_operands(body, first, count):
    def wrapped(*refs):
        return body(*refs[:first], *refs[first + count:])
    return wrapped


def _hg_bwd(dog, hg, o, st, low, gn, after=()):
    T = hg.shape[0]
    tb = min(512, T)
    sb = min(256, tb)
    nb = T // tb
    nc = tb // CHUNK
    wid = HEADS_PER_STEP * HEAD_DIM

    def body(q_ref, f_ref, i_ref, g_ref, low_ref, gn_ref, o_ref, dog_ref, st_ref,
             dhg_ref, dlow_ref, dgn_ref,
             ds_scr, dqi_scr, dko_scr, dv_scr, dd_scr, dqh_scr, dkh_scr):
        h = pl.program_id(0)
        t = pl.program_id(1)
        dq_ref, df_ref, di_ref, dg_ref = (dhg_ref.at[:, p * HG_WIDTH:(p + 1) * HG_WIDTH] for p in range(4))

        @pl.when(t == 0)
        def _():
            ds_scr[...] = jnp.zeros_like(ds_scr)
            dlow_ref[...] = jnp.zeros_like(dlow_ref)

        @pl.when((t == 0) & (h == 0))
        def _():
            dgn_ref[...] = jnp.zeros_like(dgn_ref)

        pos = _chunk_pos((tb, HEAD_DIM))
        mask = _intra_mask(sb)
        gnv = gn_ref[...]
        lanes = [slice(hh * HEAD_DIM, (hh + 1) * HEAD_DIM) for hh in range(HEADS_PER_STEP)]
        heads = []
        for hh, ln in enumerate(lanes):
            lb, lb1 = _lower_bound(low_ref.at[:, ln])
            qr = q_ref[:, ln]
            sq, q, sg, f, k, e_qa, e_ka, e_b, e_ko, dec = _hg_gates(qr, f_ref[:, ln], lb, pos, tb)

            gr = g_ref[:, ln]
            o = o_ref[:, ln]
            dog_v = dog_ref[:, ln]
            sgr = _sigmoid(gr)
            r = lax.rsqrt(jnp.mean(o * o, axis=-1, keepdims=True) + EPS)
            oh = o * r
            dg_ref[:, ln] = (dog_v * (oh * gnv) * (sgr * (1.0 + gr * (1.0 - sgr)))).astype(BF16)
            don = dog_v * (gr * sgr)
            dgn_ref[...] += jnp.sum(don * oh, axis=0, keepdims=True)
            w = don * gnv
            do = (r * (w - oh * jnp.mean(w * oh, axis=-1, keepdims=True))).astype(BF16)

            qh = (q * e_qa).astype(BF16)
            kh = (k * e_ka).astype(BF16)
            qi = (q * e_b).astype(BF16)
            ko = (k * e_ko).astype(BF16)
            vb = i_ref[:, ln].astype(BF16)

            for s in range(tb // sb):
                sl = slice(s * sb, (s + 1) * sb)
                p = jnp.where(mask, _mm_nt(qh[sl], kh[sl]), 0.0).astype(BF16)
                dp = jnp.where(mask, _mm_nt(do[sl], vb[sl]), 0.0).astype(BF16)
                dv_scr[sl, ln] = _mm_tn(p, do[sl])
                dqh_scr[sl, ln] = _mm(dp, kh[sl])
                dkh_scr[sl, ln] = _mm_tn(dp, qh[sl])
            heads.append(dict(lb=lb, lb1=lb1, qr=qr, sq=sq, q=q, sg=sg, f=f, k=k, e_qa=e_qa, e_ka=e_ka, e_b=e_b,
                              e_ko=e_ko, dec=dec, do=do, qi=qi, ko=ko, vb=vb, ds=ds_scr[hh]))

        for c in reversed(range(nc)):
            sl = slice(c * CHUNK, (c + 1) * CHUNK)
            for hh, ln in enumerate(lanes):
                hd = heads[hh]
                ds = hd["ds"]
                st_c = st_ref[hh, c]
                dqi_scr[sl, ln] = _mm(hd["do"][sl], st_c)
                dko_scr[sl, ln] = _mm(hd["vb"][sl], ds)
                dv_scr[sl, ln] = dv_scr[sl, ln] + _mm_nt(hd["ko"][sl], ds)
                dec_c = hd["dec"][c]
                dd_scr[sl, ln] = jnp.broadcast_to(dec_c * jnp.sum(ds * st_c, axis=0, keepdims=True),
                                                  (CHUNK, HEAD_DIM))
                hd["ds"] = dec_c * ds + _mm_tn(hd["do"][sl], hd["qi"][sl])

        for hh, ln in enumerate(lanes):
            hd = heads[hh]
            ds_scr[hh] = hd["ds"]
            q, k, lb = hd["q"], hd["k"], hd["lb"]
            dko_e = dko_scr[:, ln] * hd["e_ko"]
            dq = dqh_scr[:, ln] * hd["e_qa"] + dqi_scr[:, ln] * hd["e_b"]
            dk = dkh_scr[:, ln] * hd["e_ka"] + dko_e
            kd3 = (k * dko_e).reshape(nc, CHUNK, HEAD_DIM)
            last = jnp.broadcast_to(jnp.sum(kd3, axis=1, keepdims=True), kd3.shape).reshape(tb, HEAD_DIM)
            db = q * dq - k * dk + jnp.where(pos == CHUNK - 1, dd_scr[:, ln] + last, 0.0)
            dlg = _chunk_rev_cumsum(db, pos)
            dfv = dlg / hd["f"] - dk
            s_low = jnp.sum(dfv * (1.0 - hd["sg"]), axis=0, keepdims=True)
            dlow_ref[0:1, ln] += s_low * lb * (1.0 - lb)
            dlow_ref[1:2, ln] += -s_low * lb * hd["lb1"]
            df_ref[:, ln] = (dfv * (1.0 - lb) * hd["sg"] * (1.0 - hd["sg"])).astype(BF16)
            dq_ref[:, ln] = (dq * Q_SCALE * (hd["sq"] * (1.0 + hd["qr"] * (1.0 - hd["sq"])))).astype(BF16)
            di_ref[:, ln] = dv_scr[:, ln].astype(BF16)

    rt = lambda t: nb - 1 - t
    col = lambda p: pl.BlockSpec((tb, wid), lambda h, t: (rt(t), p * HEAD_GROUPS + h))
    hcol = pl.BlockSpec((tb, wid), lambda h, t: (rt(t), h))
    assert HEAD_GROUPS == 1
    tile = pltpu.VMEM((tb, wid), F32)
    return pl.pallas_call(
        _drop_operands(body, 9, len(after)), name="hg_bwd", grid=(HEAD_GROUPS, nb),
        in_specs=[col(0), col(1), col(2), col(3), pl.BlockSpec((2, wid), lambda h, t: (0, h)),
                  pl.BlockSpec((1, HEAD_DIM), lambda h, t: (0, 0)), hcol, hcol,
                  pl.BlockSpec((HEADS_PER_STEP, nc, HEAD_DIM, HEAD_DIM), lambda h, t: (h, rt(t), 0, 0))]
                 + [HBM_SPEC] * len(after),
        out_specs=[pl.BlockSpec((tb, N_HG), lambda h, t: (rt(t), 0)), pl.BlockSpec((2, wid), lambda h, t: (0, h)),
                   pl.BlockSpec((1, HEAD_DIM), lambda h, t: (0, 0))],
        out_shape=[jax.ShapeDtypeStruct((T, N_HG), BF16), jax.ShapeDtypeStruct((2, HG_WIDTH), F32),
                   jax.ShapeDtypeStruct((1, HEAD_DIM), F32)],
        scratch_shapes=[pltpu.VMEM((HEADS_PER_STEP, HEAD_DIM, HEAD_DIM), F32), tile, tile, tile, tile, tile, tile],
        compiler_params=_params(("arbitrary", "arbitrary")),
    )(hg, hg, hg, hg, low, gn, o, dog, st, *after)


def _in_bwd(dparts, w_in, x, dx1, g, after=()):
    T = x.shape[0]
    tm = min(512, T)
    widths = [p.shape[1] for p in dparts]
    offs = [sum(widths[:i]) for i in range(len(widths))]
    n = len(dparts)

    def body(*refs):
        d_refs = refs[:n]
        w_ref, x_ref, dx1_ref, g_ref, dx_ref, dgn_ref = refs[n:]

        @pl.when(pl.program_id(0) == 0)
        def _():
            dgn_ref[...] = jnp.zeros_like(dgn_ref)

        dh = None
        for d_ref, off, wd in zip(d_refs, offs, widths):
            part = _mm(d_ref[...], w_ref[off:off + wd, :])
            dh = part if dh is None else dh + part
        xv = x_ref[...]
        r = lax.rsqrt(jnp.mean(xv * xv, axis=-1, keepdims=True) + EPS)
        xh = xv * r
        dgn_ref[...] += jnp.sum(dh * xh, axis=0, keepdims=True)
        w = dh * g_ref[...]
        dx_ref[...] = dx1_ref[...] + r * (w - xh * jnp.mean(w * xh, axis=-1, keepdims=True))

    row = lambda m: pl.BlockSpec((tm, m), lambda i: (i, 0))
    return pl.pallas_call(
        _drop_operands(body, n + 4, len(after)), name="in_bwd", grid=(T // tm,),
        in_specs=[row(wd) for wd in widths] + [_resident(w_in.shape), row(D_MODEL), row(D_MODEL), _full((1, D_MODEL))]
                 + [HBM_SPEC] * len(after),
        out_specs=[row(D_MODEL), _full((1, D_MODEL))],
        out_shape=[jax.ShapeDtypeStruct((T, D_MODEL), F32), jax.ShapeDtypeStruct((1, D_MODEL), F32)],
        compiler_params=_params(("arbitrary",)),
    )(*dparts, w_in, x, dx1, g, *after)


def _wgrad_ffn(h2t, dgate, dup, dx2t, act, tn=256):
    M, T = h2t.shape
    N = dgate.shape[1]

    def body(h_ref, x_ref, dg_ref, du_ref, act_ref, og_ref, ou_ref, od_ref):
        h = h_ref[...]
        og_ref[...] = _mm(h, dg_ref[...]).T.astype(BF16)
        ou_ref[...] = _mm(h, du_ref[...]).T.astype(BF16)
        od_ref[...] = _mm(x_ref[...], act_ref[...]).T.astype(BF16)

    rhs = pl.BlockSpec((T, tn), lambda j: (0, j))
    out_spec = pl.BlockSpec((tn, M), lambda j: (j, 0))
    out = jax.ShapeDtypeStruct((N, M), BF16)
    return pl.pallas_call(
        body, name="wgrad_ffn", grid=(N // tn,),
        in_specs=[_resident((M, T)), _resident((M, T)), rhs, rhs, rhs], out_specs=[out_spec] * 3,
        out_shape=[out, out, out],
        compiler_params=_params(("parallel",)),
    )(h2t, dx2t, dgate, dup, act)


def _wgrad_out_branches(ogt, dya, cvot, dyb, mgt, dx1, by_owner, got, core):
    M, T = ogt.shape
    N = dya.shape[1]
    c = N // N_DEV
    per = 2
    tn = per * c
    n = len(got)
    assert N // tn == got[0].shape[0]

    def body(core_ref, at_ref, da_ref, bt_ref, db_ref, mt_ref, dx_ref, *refs):
        mine, theirs = refs[:n], refs[n:2 * n]
        oa_ref, ob_ref, oo_ref = refs[2 * n:2 * n + 3]
        ga = _mm(at_ref[...], da_ref[...])
        gb = _mm(bt_ref[...], db_ref[...])
        for s in range(per):
            oa_ref[s] = ga[:, s * c:(s + 1) * c].astype(BF16)
            ob_ref[s] = gb[:, s * c:(s + 1) * c].astype(BF16)
        oo_ref[...] = _mm(mt_ref[...], dx_ref[...]).astype(BF16)
        for a_ref, b_ref, o_ref in zip(mine, theirs, refs[2 * n + 3:]):
            o_ref[...] = (a_ref[...].astype(F32) + b_ref[...].astype(F32)).astype(BF16)

    def blk(g):
        return pl.BlockSpec((None,) + g.shape[1:], lambda j, core_ref: (j, 0, 0))

    def own(g):
        return pl.BlockSpec((None,) + g.shape[1:], lambda j, core_ref: (2 * j + core_ref[0], 0, 0))

    rhs = pl.BlockSpec((T, tn), lambda j, core_ref: (0, j))
    owners = pl.BlockSpec((per, M, c), lambda j, core_ref: (j, 0, 0))
    out = jax.ShapeDtypeStruct((N_DEV, M, c), BF16)
    return pl.pallas_call(
        body, name="wgrad_out_branches",
        grid_spec=pltpu.PrefetchScalarGridSpec(
            num_scalar_prefetch=1, grid=(N // tn,),
            in_specs=[_resident((M, T)), rhs, _resident((M, T)), rhs, _resident(mgt.shape), rhs]
            + [own(g) for g in by_owner] + [blk(g) for g in got],
            out_specs=[owners, owners, pl.BlockSpec((mgt.shape[0], tn), lambda j, core_ref: (0, j))]
            + [blk(g) for g in got]),
        out_shape=[out, out, jax.ShapeDtypeStruct((mgt.shape[0], dx1.shape[1]), BF16)]
        + [jax.ShapeDtypeStruct(g.shape, BF16) for g in got],
        compiler_params=_params(("parallel",)),
    )(core, ogt, dya, cvot, dyb, mgt, dx1, *by_owner, *got)


def _wgrad_in(ht, dparts, after=(), riders=()):
    M, T = ht.shape
    tn = 512
    nblk = [p.shape[1] // tn for p in dparts]
    start = [sum(nblk[:i]) for i in range(len(nblk))]
    n = len(dparts)
    steps = sum(nblk)
    nr = len(riders)
    rows = [r[0].shape[0] // steps for r in riders]
    assert all(r[0].shape[0] == rr * steps and rr % 16 == 0 for r, rr in zip(riders, rows))

    def body(a_ref, *refs):
        d_refs = refs[:n]
        rider_in = refs[n:n + 4 * nr]
        o_ref = refs[n + 4 * nr]
        rider_out = refs[n + 4 * nr + 1:]
        j = pl.program_id(0)
        for d_ref, s, nb in zip(d_refs, start, nblk):
            @pl.when((j >= s) & (j < s + nb))
            def _():
                o_ref[...] = _mm(a_ref[...], d_ref[...]).T.astype(BF16)
        for i in range(nr):
            w_ref, p_ref, m_ref, v_ref = rider_in[4 * i:4 * i + 4]
            g = p_ref[0].astype(F32)
            for k in range(1, 4):
                g = g + p_ref[k].astype(F32)
            delta, m_new, v_new = _adamw_math(w_ref[...], g, m_ref[...], v_ref[...])
            for k, val in enumerate((g, delta, m_new, v_new)):
                rider_out[4 * i + k][...] = val

    def piece_spec(s, nb):
        return pl.BlockSpec((T, tn), lambda j: (0, jnp.clip(j - s, 0, nb - 1)))

    rider_specs, rider_out_specs, rider_out_shape, rider_args = [], [], [], []
    for (w, parts, m, v), rr in zip(riders, rows):
        blk = pl.BlockSpec((rr, w.shape[1]), lambda j: (j, 0))
        rider_specs += [blk, pl.BlockSpec((4, rr, w.shape[1]), lambda j: (0, j, 0)), blk, blk]
        rider_out_specs += [blk] * 4
        rider_out_shape += [jax.ShapeDtypeStruct(w.shape, F32)] * 4
        rider_args += [w, parts, m, v]
    outs = pl.pallas_call(
        _drop_operands(body, 1 + n + 4 * nr, len(after)), name="wgrad_in", grid=(steps,),
        in_specs=[_resident((M, T))] + [piece_spec(s, nb) for s, nb in zip(start, nblk)] + rider_specs
                 + [HBM_SPEC] * len(after),
        out_specs=[pl.BlockSpec((tn, M), lambda j: (j, 0))] + rider_out_specs,
        out_shape=[jax.ShapeDtypeStruct((steps * tn, M), BF16)] + rider_out_shape,
        compiler_params=_params(("parallel",)),
    )(ht, *dparts, *rider_args, *after)
    return outs[0], [outs[1 + 4 * i:5 + 4 * i] for i in range(nr)]


def _adamw_math(w, g, m, v):
    m = ADAM_B1 * m + (1.0 - ADAM_B1) * g
    v = ADAM_B2 * v + (1.0 - ADAM_B2) * (g * g)
    m_hat = m / (1.0 - ADAM_B1 ** ADAM_STEP)
    v_hat = v / (1.0 - ADAM_B2 ** ADAM_STEP)
    delta = -ADAM_LR * (m_hat / (jnp.sqrt(v_hat) + ADAM_EPS) + ADAM_WD * w)
    return delta, m, v


def _adamw_sum(name, ws, parts, ms, vs):
    n = len(ws)
    steps = min(_row_steps(w.shape[0]) for w in ws)
    rows = [w.shape[0] // steps for w in ws]

    def body(*refs):
        w_refs, p_refs, m_refs, v_refs = (refs[k * n:(k + 1) * n] for k in range(4))
        out_refs = refs[4 * n:]
        for i in range(n):
            g = p_refs[i][0].astype(F32)
            for k in range(1, 4):
                g = g + p_refs[i][k].astype(F32)
            delta, m_new, v_new = _adamw_math(w_refs[i][...], g, m_refs[i][...], v_refs[i][...])
            for k, val in enumerate((g, delta, m_new, v_new)):
                out_refs[4 * i + k][...] = val

    blk = [pl.BlockSpec((r, w.shape[1]), lambda s: (s, 0)) for r, w in zip(rows, ws)]
    pblk = [pl.BlockSpec((4, r, w.shape[1]), lambda s: (0, s, 0)) for r, w in zip(rows, ws)]
    out_specs, out_shape = [], []
    for b, w in zip(blk, ws):
        out_specs += [b] * 4
        out_shape += [jax.ShapeDtypeStruct(w.shape, F32)] * 4
    flat = pl.pallas_call(
        body, name=name, grid=(steps,),
        in_specs=blk + pblk + blk + blk, out_specs=out_specs, out_shape=out_shape,
        compiler_params=_params(("parallel",)),
    )(*ws, *parts, *ms, *vs)
    return [flat[4 * i:4 * i + 4] for i in range(n)]


_SMALL_SLOTS = (("norm_mix_g", 0, 1, 1024), ("norm_ffn_g", 1, 1, 1024), ("norm_final_g", 2, 1, 1024),
                ("lower_bounds", 3, 2, 512), ("hg_norm_g", 5, 1, 128), ("loss", 6, 1, 128), ("conv_w", 8, 3, 512))
_SMALL_PARAMS = tuple(s for s in _SMALL_SLOTS if s[0] != "loss")
CONV_SHARD = CONV_WIDTH // N_DEV


def _small_pack(small):
    def body(*refs):
        out = refs[-1]
        out[...] = jnp.zeros_like(out)
        for ref, (_, row, rows, lanes) in zip(refs[:-1], _SMALL_SLOTS):
            out[row:row + rows, 0:lanes] = ref[...]

    vmem = pl.BlockSpec(memory_space=pltpu.VMEM)
    return pl.pallas_call(
        body, name="small_pack", in_specs=[vmem] * len(_SMALL_SLOTS), out_specs=vmem,
        out_shape=jax.ShapeDtypeStruct((SMALL_ROWS, 1024), F32),
    )(*[small[name] for name, _, _, _ in _SMALL_SLOTS])


def _small_update(gathered, dev, w, m, v):
    n = len(_SMALL_PARAMS)

    def body(dev_ref, g_ref, *refs):
        w_refs, m_refs, v_refs = refs[:n], refs[n:2 * n], refs[2 * n:3 * n]
        loss_ref, out_refs, sum_scr = refs[3 * n], refs[3 * n + 1:-1], refs[-1]
        total = g_ref[0]
        for k in range(1, N_DEV):
            total = total + g_ref[k]
        sum_scr[...] = total
        loss_ref[...] = sum_scr[6:7, 0:128]
        for p, (name, row, rows, lanes) in enumerate(_SMALL_PARAMS):
            if name == "conv_w":
                for r in range(rows):
                    g = sum_scr[row + r:row + r + 1, 0:CONV_SHARD]
                    for s in range(1, N_DEV):
                        mine = sum_scr[row + r:row + r + 1, s * CONV_SHARD:(s + 1) * CONV_SHARD]
                        g = jnp.where(dev_ref[0] == s, mine, g)
                    delta, m_new, v_new = _adamw_math(w_refs[p][r], g, m_refs[p][r], v_refs[p][r])
                    out_refs[4 * p][r] = g
                    out_refs[4 * p + 1][r] = delta
                    out_refs[4 * p + 2][r] = m_new
                    out_refs[4 * p + 3][r] = v_new
                continue
            g = sum_scr[row:row + rows, 0:lanes]
            delta, m_new, v_new = _adamw_math(w_refs[p][...], g, m_refs[p][...], v_refs[p][...])
            out_refs[4 * p][...] = g
            out_refs[4 * p + 1][...] = delta
            out_refs[4 * p + 2][...] = m_new
            out_refs[4 * p + 3][...] = v_new

    vmem = pl.BlockSpec(memory_space=pltpu.VMEM)
    outs = [jax.ShapeDtypeStruct((1, 128), F32)]
    for a in w:
        outs += [jax.ShapeDtypeStruct(a.shape, F32)] * 4
    return pl.pallas_call(
        body, name="small_update",
        in_specs=[pl.BlockSpec(memory_space=pltpu.SMEM)] + [vmem] * (1 + 3 * n), out_specs=[vmem] * len(outs),
        out_shape=outs, scratch_shapes=[pltpu.VMEM((SMALL_ROWS, 1024), F32)],
    )(dev, gathered, *w, *m, *v)


def _row_steps(rows):
    for steps in (4, 2):
        if rows % (16 * steps) == 0:
            return steps
    return 1


def _pair_sum(name, by_owner, got, core, after=()):
    n = len(got)

    def body(core_ref, *refs):
        for a_ref, b_ref, o_ref in zip(refs[:n], refs[n:2 * n], refs[2 * n:]):
            o_ref[...] = (a_ref[...].astype(F32) + b_ref[...].astype(F32)).astype(BF16)

    def blk(g):
        return pl.BlockSpec((None,) + g.shape[1:], lambda k, core_ref: (k, 0, 0))

    def mine(g):
        return pl.BlockSpec((None,) + g.shape[1:], lambda k, core_ref: (2 * k + core_ref[0], 0, 0))

    return pl.pallas_call(
        _drop_operands(body, 1 + 2 * n, len(after)), name=name,
        grid_spec=pltpu.PrefetchScalarGridSpec(
            num_scalar_prefetch=1, grid=(4,),
            in_specs=[mine(g) for g in got] + [blk(g) for g in got] + [HBM_SPEC] * len(after),
            out_specs=[blk(g) for g in got]),
        out_shape=[jax.ShapeDtypeStruct(g.shape, BF16) for g in got],
        compiler_params=_params(("parallel",)),
    )(core, *by_owner, *got, *after)


MESH = pl.DeviceIdType.MESH
HBM_SPEC = pl.BlockSpec(memory_space=pl.ANY)


def _handshake(peers):
    barrier = pltpu.get_barrier_semaphore()
    for peer in peers:
        pl.semaphore_signal(barrier, inc=1, device_id=peer, device_id_type=MESH)
    pl.semaphore_wait(barrier, len(peers))


def _comm_call(body, name, operands, out_shape, scratch, collective_id):
    if collective_id is None:
        return pl.pallas_call(body, name=name, in_specs=[HBM_SPEC] * len(operands), out_specs=[HBM_SPEC] * len(out_shape),
                              out_shape=out_shape, scratch_shapes=scratch)(*operands)
    return pl.kernel(body, out_type=out_shape, mesh=plsc.ScalarSubcoreMesh(axis_name="sequencer", num_cores=1),
                     scratch_types=scratch, name=name,
                     compiler_params=pltpu.CompilerParams(collective_id=collective_id))(*operands)


def _all_gather(name, blocks, collective_id=None, after=(), pieces=None):
    n = len(blocks)
    na = len(after)
    pieces = pieces or [1] * n
    parts = []
    for i, (b, k) in enumerate(zip(blocks, pieces)):
        rows, rem = divmod(b.shape[0], k)
        assert rem == 0 and (k == 1 or rows % 16 == 0), (b.shape, k)
        parts += [(i, None, None)] if k == 1 else [(i, j * rows, rows) for j in range(k)]
    np_ = len(parts)

    def body(*refs):
        x_refs, out_refs = refs[:n], refs[n + na:2 * n + na]
        send_sems, recv_sems, local_sems = refs[2 * n + na:]
        x, y, c = lax.axis_index("x"), lax.axis_index("y"), lax.axis_index("c")
        me, sibling = (x, y, c), (x, y, 1 - c)
        chips = [(1 - x, y), (x, 1 - y), (1 - x, 1 - y)]
        if collective_id is not None:
            _handshake([sibling] + [(*chip, c) for chip in chips])

        def slot(p, px, py, pc):
            i, r0, rows = parts[p]
            whole = out_refs[i].at[4 * px + 2 * py + pc]
            return whole if r0 is None else whole.at[pl.ds(r0, rows)]

        def own(p):
            i, r0, rows = parts[p]
            return x_refs[i] if r0 is None else x_refs[i].at[pl.ds(r0, rows)]

        def copy(p, k, blk, to, src=None):
            return pltpu.make_async_remote_copy(
                src_ref=slot(p, *blk) if src is None else src, dst_ref=slot(p, *blk),
                send_sem=send_sems.at[7 * p + k], recv_sem=recv_sems.at[7 * p + k], device_id=to, device_id_type=MESH)

        mine = [pltpu.make_async_copy(own(p), slot(p, *me), local_sems.at[p]) for p in range(np_)]
        for cp in mine:
            cp.start()
        first = []
        for p in range(np_):
            first.append(copy(p, 0, me, sibling, src=own(p)))
            first += [copy(p, 1 + j, me, (*chip, c), src=own(p)) for j, chip in enumerate(chips)]
        for cp in first:
            cp.start()
        passed = []
        for p in range(np_):
            for j, chip in enumerate(chips):
                copy(p, 1 + j, (*chip, c), me).wait_recv()
                passed.append(copy(p, 4 + j, (*chip, c), sibling))
                passed[-1].start()
        for p in range(np_):
            copy(p, 0, sibling, me).wait_recv()
            for j, chip in enumerate(chips):
                copy(p, 4 + j, (*chip, 1 - c), me).wait_recv()
        for cp in first + passed:
            cp.wait_send()
        for cp in mine:
            cp.wait()

    return _comm_call(
        body, name, list(blocks) + list(after), [jax.ShapeDtypeStruct((N_DEV,) + b.shape, b.dtype) for b in blocks],
        [pltpu.SemaphoreType.DMA((7 * np_,)), pltpu.SemaphoreType.DMA((7 * np_,)), pltpu.SemaphoreType.DMA((np_,))],
        collective_id)


def _sibling_swap(name, by_owner, collective_id=None, after=()):
    n = len(by_owner)
    na = len(after)

    def body(*refs):
        x_refs, out_refs = refs[:n], refs[n + na:2 * n + na]
        send_sems, recv_sems = refs[2 * n + na:]
        x, y, c = lax.axis_index("x"), lax.axis_index("y"), lax.axis_index("c")
        if collective_id is not None:
            _handshake([(x, y, 1 - c)])
        copies = []
        for i in range(n):
            for k in range(4):
                copies.append(pltpu.make_async_remote_copy(
                    src_ref=x_refs[i].at[2 * k + 1 - c], dst_ref=out_refs[i].at[k],
                    send_sem=send_sems.at[4 * i + k], recv_sem=recv_sems.at[4 * i + k],
                    device_id=(x, y, 1 - c), device_id_type=MESH))
        for cp in copies:
            cp.start()
        for cp in copies:
            cp.wait()

    return _comm_call(
        body, name, list(by_owner) + list(after),
        [jax.ShapeDtypeStruct((4,) + b.shape[1:], b.dtype) for b in by_owner],
        [pltpu.SemaphoreType.DMA((4 * n,)), pltpu.SemaphoreType.DMA((4 * n,))], collective_id)


def _chip_exchange(name, sums, collective_id=None, after=()):
    n = len(sums)
    na = len(after)

    def body(*refs):
        x_refs, out_refs = refs[:n], refs[n + na:2 * n + na]
        send_sems, recv_sems, local_sems = refs[2 * n + na:]
        x, y, c = lax.axis_index("x"), lax.axis_index("y"), lax.axis_index("c")
        chips = [(1 - x, y), (x, 1 - y), (1 - x, 1 - y)]
        my_chip = 2 * x + y
        if collective_id is not None:
            _handshake([(cx, cy, c) for cx, cy in chips])
        mine = [pltpu.make_async_copy(x_refs[i].at[my_chip], out_refs[i].at[my_chip], local_sems.at[i])
                for i in range(n)]
        for cp in mine:
            cp.start()
        sends = []
        for i in range(n):
            for j, (cx, cy) in enumerate(chips):
                sends.append(pltpu.make_async_remote_copy(
                    src_ref=x_refs[i].at[2 * cx + cy], dst_ref=out_refs[i].at[my_chip],
                    send_sem=send_sems.at[3 * i + j], recv_sem=recv_sems.at[3 * i + j],
                    device_id=(cx, cy, c), device_id_type=MESH))
        for cp in sends:
            cp.start()
        for i in range(n):
            for j, (cx, cy) in enumerate(chips):
                pltpu.make_async_remote_copy(
                    src_ref=x_refs[i].at[my_chip], dst_ref=out_refs[i].at[2 * cx + cy],
                    send_sem=send_sems.at[3 * i + j], recv_sem=recv_sems.at[3 * i + j],
                    device_id=(cx, cy, c), device_id_type=MESH).wait_recv()
        for cp in sends:
            cp.wait_send()
        for cp in mine:
            cp.wait()

    return _comm_call(
        body, name, list(sums) + list(after), [jax.ShapeDtypeStruct(s.shape, s.dtype) for s in sums],
        [pltpu.SemaphoreType.DMA((3 * n,)), pltpu.SemaphoreType.DMA((3 * n,)), pltpu.SemaphoreType.DMA((n,))],
        collective_id)


def _cast_shards(name, shards):
    n = len(shards)
    steps = min(_row_steps(s.shape[0]) for s in shards)

    def body(*refs):
        for i in range(n):
            refs[n + i][...] = refs[i][...].astype(BF16)

    blocks = [pl.BlockSpec((s.shape[0] // steps, s.shape[1]), lambda i: (i, 0)) for s in shards]
    return pl.pallas_call(
        body, name=name, grid=(steps,), in_specs=blocks, out_specs=blocks,
        out_shape=[jax.ShapeDtypeStruct(s.shape, BF16) for s in shards],
        compiler_params=_params(("parallel",)),
    )(*shards)


BIG = ("w_in", "w_branch_a", "w_branch_b", "w_out", "w_ffn_gate", "w_ffn_up", "w_ffn_down")


def _local_step(x, target, gains, low, conv_w, wg8, reduce):
    g_mix, g_hg, g_ffn, g_fin = gains
    w_in = wg8["w_in"].reshape(N_IN, D_MODEL)
    wg = wg8["w_ffn_gate"].reshape(D_FF, D_MODEL)
    wu = wg8["w_ffn_up"].reshape(D_FF, D_MODEL)
    wa, wb = wg8["w_branch_a"], wg8["w_branch_b"]
    wo = wg8["w_out"].reshape(D_MODEL, D_MODEL)
    wd = wg8["w_ffn_down"].reshape(D_FF, D_MODEL)

    ht, hg, cv, gt, cvo, cvot, cy, o, og, ogt, st = _fwd_in(x, g_mix, w_in, conv_w, low, g_hg)
    x1, mgt, ya, yb = _merge_fwd(og, cvo, gt, x, wa, wb, wo)
    h2t, gate, up, act, loss, d_gfin, dx2, dx2t = _ffn_fwd_loss(x1, g_ffn, wg, wu, wd, target, g_fin)

    dgate, dup, dx1, d_gffn = _ffn_bwd(dx2, x1, gate, up, g_ffn, wg, wu, wd)
    d_wg, d_wu, d_wd = _wgrad_ffn(h2t, dgate, dup, dx2t, act)
    by_owner_ffn = lambda a: a.reshape(N_DEV, D_FF // N_DEV, D_MODEL)
    late = ("w_ffn_gate", "w_ffn_up")
    ffn = dict(w_ffn_gate=by_owner_ffn(d_wg), w_ffn_up=by_owner_ffn(d_wu), w_ffn_down=by_owner_ffn(d_wd))
    got_ffn = reduce.swap(ffn)
    dgt, dya, dyb, dog, dcv, d_conv = _merge_bwd(dx1, ya, yb, gt, cv, cy, wa, wb, wo, conv_w)
    grad_a, grad_b, grad_o, *sums_ffn = _wgrad_out_branches(ogt, dya, cvot, dyb, mgt, dx1, list(ffn.values()), got_ffn,
                                                           reduce.core)
    out = dict(w_out=grad_o.reshape(N_DEV, D_MODEL // N_DEV, D_MODEL), w_branch_a=grad_a, w_branch_b=grad_b)
    parts_ffn, _ = reduce.finish({n: ffn[n] for n in late}, sums_ffn[:2], defer=late)
    dhg, d_low, d_ghg = _hg_bwd(dog, hg, o, st, low, g_hg, after=list(sums_ffn))
    sums_out, got_out = reduce.begin(out, sum_after=[dhg])
    parts_out, updated_out = reduce.finish(dict(out, w_ffn_down=ffn["w_ffn_down"]), list(sums_out) + [sums_ffn[2]])
    dparts = [dhg, dcv, dgt]
    d_w_in_t, ridden = _wgrad_in(ht, dparts, after=sums_out[:1],
                                 riders=reduce.riders(late, dict(zip(late, parts_ffn))))
    reduce.record(late, ridden)
    w_in_grad = dict(w_in=d_w_in_t.reshape(N_DEV, N_IN // N_DEV, D_MODEL))
    sums_in, _ = reduce.begin(w_in_grad, after=parts_out[:1], sum_after=updated_out)
    parts_in, _ = reduce.finish(w_in_grad, sums_in)
    grad_x, d_gmix = _in_bwd(dparts, w_in, x, dx1, g_mix, after=list(parts_out[:1]) + list(sums_in))
    small = dict(norm_mix_g=d_gmix, norm_ffn_g=d_gffn, norm_final_g=d_gfin, lower_bounds=d_low, hg_norm_g=d_ghg,
                 conv_w=d_conv, loss=loss)
    return grad_x, small, parts_in


def kernel(x, norm_mix_g, w_in, lower_bounds, hg_norm_g, conv_w, w_branch_a, w_branch_b, w_out, norm_ffn_g, w_ffn_gate, w_ffn_up, w_ffn_down, norm_final_g, loss_target, m_norm_mix_g, m_w_in, m_lower_bounds, m_hg_norm_g, m_conv_w, m_w_branch_a, m_w_branch_b, m_w_out, m_norm_ffn_g, m_w_ffn_gate, m_w_ffn_up, m_w_ffn_down, m_norm_final_g, v_norm_mix_g, v_w_in, v_lower_bounds, v_hg_norm_g, v_conv_w, v_w_branch_a, v_w_branch_b, v_w_out, v_norm_ffn_g, v_w_ffn_gate, v_w_ffn_up, v_w_ffn_down, v_norm_final_g):
    cx, cy, cc = lax.axis_index("x"), lax.axis_index("y"), lax.axis_index("c")
    my_dev = 4 * cx + 2 * cy + cc

    def tr(a):
        return a[0].T

    big = dict(w_in=tr(w_in), w_branch_a=w_branch_a[0], w_branch_b=w_branch_b[0], w_out=w_out[0],
               w_ffn_gate=tr(w_ffn_gate), w_ffn_up=tr(w_ffn_up), w_ffn_down=w_ffn_down[0])
    big_m = dict(w_in=tr(m_w_in), w_branch_a=m_w_branch_a[0], w_branch_b=m_w_branch_b[0], w_out=m_w_out[0],
                 w_ffn_gate=tr(m_w_ffn_gate), w_ffn_up=tr(m_w_ffn_up), w_ffn_down=m_w_ffn_down[0])
    big_v = dict(w_in=tr(v_w_in), w_branch_a=v_w_branch_a[0], w_branch_b=v_w_branch_b[0], w_out=v_w_out[0],
                 w_ffn_gate=tr(v_w_ffn_gate), w_ffn_up=tr(v_w_ffn_up), w_ffn_down=v_w_ffn_down[0])
    transposed = ("w_in", "w_ffn_gate", "w_ffn_up")

    ids = iter(range(1, 16))
    shards = dict(zip(BIG[:1], _cast_shards("cast_w_in", [big["w_in"]])))
    first = _all_gather("gather_w_in", [shards["w_in"], conv_w.transpose(1, 0, 2)], collective_id=next(ids),
                        pieces=[4, 1])
    shards.update(zip(BIG[1:], _cast_shards("cast_shards", [big[n] for n in BIG[1:]])))
    mid = _all_gather("gather_mid", [shards[n] for n in BIG[1:4]], collective_id=next(ids))
    ffn = _all_gather("gather_ffn", [shards[n] for n in BIG[4:]], collective_id=next(ids), pieces=[2, 2, 2])
    wg8 = dict(zip(BIG, [first[0]] + list(mid) + list(ffn)))
    conv_full = first[1].transpose(1, 2, 0, 3).reshape(3, CONV_WIDTH)

    core = cc.reshape(1).astype(jnp.int32)
    outs = {}

    class Reduce:
        core = cc.reshape(1).astype(jnp.int32)

        @staticmethod
        def swap(grads, after=()):
            names = list(grads)
            return _sibling_swap("sibling_swap_" + names[0], [grads[n] for n in names], collective_id=next(ids),
                                 after=after)

        @staticmethod
        def begin(grads, after=(), sum_after=()):
            got = Reduce.swap(grads, after)
            sums = _pair_sum("pair_sum_" + list(grads)[0], list(grads.values()), got, core, after=sum_after)
            return sums, got

        @staticmethod
        def finish(grads, chip_sums, after=(), defer=()):
            names = list(grads)
            parts = _chip_exchange("chip_exchange_" + names[0], chip_sums, collective_id=next(ids), after=after)
            now = [n for n in names if n not in defer]
            if now:
                updated = _adamw_sum("adamw_" + now[0], [big[n] for n in now],
                                     [p for n, p in zip(names, parts) if n in now],
                                     [big_m[n] for n in now], [big_v[n] for n in now])
                outs.update(zip(now, updated))
            return parts, [outs[n][1] for n in now]

        @staticmethod
        def riders(names, parts):
            return [(big[n], parts[n], big_m[n], big_v[n]) for n in names]

        @staticmethod
        def record(names, updated):
            outs.update(zip(names, updated))

    gains = (norm_mix_g, hg_norm_g, norm_ffn_g, norm_final_g.reshape(1, D_MODEL))
    grad_x, small, last = _local_step(x[0], loss_target[0], gains, lower_bounds, conv_full, wg8, Reduce)

    small_all = _all_gather("gather_small", [_small_pack(small)], collective_id=next(ids), after=last[:1])

    def small_state(a):
        return [a[0], a[1], a[2].reshape(1, D_MODEL), a[3], a[4], a[5].transpose(1, 0, 2)]

    upd = _small_update(
        small_all[0], my_dev.reshape(1).astype(jnp.int32),
        small_state((norm_mix_g, norm_ffn_g, norm_final_g, lower_bounds, hg_norm_g, conv_w)),
        small_state((m_norm_mix_g, m_norm_ffn_g, m_norm_final_g, m_lower_bounds, m_hg_norm_g, m_conv_w)),
        small_state((v_norm_mix_g, v_norm_ffn_g, v_norm_final_g, v_lower_bounds, v_hg_norm_g, v_conv_w)))
    loss = upd[0][0, 0]
    for p, (name, _, _, _) in enumerate(_SMALL_PARAMS):
        outs[name] = upd[1 + 4 * p:5 + 4 * p]
    outs["norm_final_g"] = [a.reshape(D_MODEL) for a in outs["norm_final_g"]]
    outs["conv_w"] = [a.transpose(1, 0, 2) for a in outs["conv_w"]]

    order = ["norm_mix_g", "w_in", "lower_bounds", "hg_norm_g", "conv_w", "w_branch_a", "w_branch_b", "w_out",
             "norm_ffn_g", "w_ffn_gate", "w_ffn_up", "w_ffn_down", "norm_final_g"]
    result = [loss, grad_x[None]]
    for k in range(4):
        for n in order:
            if n in BIG:
                result.append((outs[n][k].T if n in transposed else outs[n][k])[None])
            else:
                result.append(outs[n][k])
    return tuple(result)
```

```python
import jax
import jax.numpy as jnp
from jax import lax
from jax.experimental import pallas as pl
from jax.experimental.pallas import tpu as pltpu
from jax.experimental.pallas import tpu_sc as plsc

F32 = jnp.float32
BF16 = jnp.bfloat16
STASH = jnp.bfloat16

D_MODEL = 1024
HG_WIDTH = 512
HEAD_DIM = 128
N_HEADS = 4
HEADS_PER_STEP = 4
HEAD_GROUPS = N_HEADS // HEADS_PER_STEP
CONV_WIDTH = 512
CONV_K = 3
D_FF = 2816
CHUNK = 32
EPS = 1e-6
Q_SCALE = HEAD_DIM ** -0.5
N_DEV = 8

ADAM_LR = 0.001
ADAM_B1 = 0.9
ADAM_B2 = 0.999
ADAM_EPS = 1e-08
ADAM_WD = 0.01
ADAM_STEP = 10

VMEM_LIMIT_V7X = 56 * 1024 * 1024
VMEM_LIMIT_LARGE_V7X = 62 * 1024 * 1024

SMALL_ROWS = 16


def _params(sem, vmem=VMEM_LIMIT_V7X):
    return pltpu.CompilerParams(dimension_semantics=sem, vmem_limit_bytes=vmem)


def _mm(a, b):
    return jnp.dot(a.astype(BF16), b.astype(BF16), preferred_element_type=F32)


def _mm_nt(a, b):
    return lax.dot_general(a.astype(BF16), b.astype(BF16), (((1,), (1,)), ((), ())), preferred_element_type=F32)


def _mm_tn(a, b):
    return lax.dot_general(a.astype(BF16), b.astype(BF16), (((0,), (0,)), ((), ())), preferred_element_type=F32)


def _sigmoid(x):
    return 0.5 * jnp.tanh(0.5 * x) + 0.5


def _resident(shape):
    nd = len(shape)
    return pl.BlockSpec(shape, lambda *_: (0,) * nd, pipeline_mode=pl.Buffered(1))


def _full(shape):
    nd = len(shape)
    return pl.BlockSpec(shape, lambda *_: (0,) * nd)


def _shard_cols(w_ref):
    return jnp.concatenate([w_ref[s] for s in range(N_DEV)], axis=1)


N_HG = 4 * HG_WIDTH
N_CV = 3 * CONV_WIDTH
N_GT = 2 * D_MODEL
N_IN = N_HG + N_CV + N_GT


def _col(tm, n):
    return pl.BlockSpec((n, tm), lambda i: (0, i))


HALO = 8


def _fwd_in(x, g, w_in_t, conv_w, low, gn):
    T = x.shape[0]
    tm = min(512, T)
    nc = tm // CHUNK

    def body(x_ref, g_ref, w_ref, cw_ref, low_ref, gn_ref, ht_ref, hg_ref, cv_ref, gt_ref, cvo_ref, cvot_ref, cy_ref,
             o_ref, og_ref, ogt_ref, st_ref, tail_scr, s_scr):
        @pl.when(pl.program_id(0) == 0)
        def _():
            tail_scr[...] = jnp.zeros_like(tail_scr)
            s_scr[...] = jnp.zeros_like(s_scr)

        xv = x_ref[...]
        r = lax.rsqrt(jnp.mean(xv * xv, axis=-1, keepdims=True) + EPS)
        hf = xv * r * g_ref[...]
        h = hf.astype(BF16)
        ht_ref[...] = hf.T.astype(BF16)
        hg_ref[...] = _mm_nt(h, w_ref[:N_HG, :])
        cv = _mm_nt(h, w_ref[N_HG:N_HG + N_CV, :])
        cv_ref[...] = cv.astype(STASH)
        gt_ref[...] = _mm_nt(h, w_ref[N_HG + N_CV:, :]).astype(STASH)

        u = cv[:, :CONV_WIDTH] * cv[:, 2 * CONV_WIDTH:]
        row = lax.broadcasted_iota(jnp.int32, u.shape, 0)
        prev1 = tail_scr[HALO - 1:HALO, :]
        prev2 = tail_scr[HALO - 2:HALO - 1, :]
        u1 = jnp.where(row >= 1, pltpu.roll(u, 1, 0), prev1)
        u2 = jnp.where(row >= 2, pltpu.roll(u, 2, 0), jnp.where(row == 1, prev1, prev2))
        y = cw_ref[0:1, :] * u2 + cw_ref[1:2, :] * u1 + cw_ref[2:3, :] * u
        cy_ref[...] = y.astype(STASH)
        out = cv[:, CONV_WIDTH:2 * CONV_WIDTH] * y
        cvo_ref[...] = out.astype(BF16)
        cvot_ref[...] = out.T.astype(BF16)
        tail_scr[...] = u[tm - HALO:, :]

        _hg_fwd_tile(hg_ref, low_ref, gn_ref, o_ref, og_ref, ogt_ref, st_ref, s_scr, tm)

    row = lambda n: pl.BlockSpec((tm, n), lambda i: (i, 0))
    return pl.pallas_call(
        body, name="fwd_in", grid=(T // tm,),
        in_specs=[row(D_MODEL), _full((1, D_MODEL)), _resident(w_in_t.shape), _full((CONV_K, CONV_WIDTH)),
                  _full((2, HG_WIDTH)), _full((1, HEAD_DIM))],
        out_specs=[_col(tm, D_MODEL), row(N_HG), row(N_CV), row(N_GT), row(CONV_WIDTH), _col(tm, CONV_WIDTH),
                   row(CONV_WIDTH),
                   row(HG_WIDTH), row(HG_WIDTH), _col(tm, HG_WIDTH),
                   pl.BlockSpec((N_HEADS, nc, HEAD_DIM, HEAD_DIM), lambda i: (0, i, 0, 0))],
        out_shape=[jax.ShapeDtypeStruct((D_MODEL, T), BF16), jax.ShapeDtypeStruct((T, N_HG), F32),
                   jax.ShapeDtypeStruct((T, N_CV), STASH), jax.ShapeDtypeStruct((T, N_GT), STASH),
                   jax.ShapeDtypeStruct((T, CONV_WIDTH), BF16), jax.ShapeDtypeStruct((CONV_WIDTH, T), BF16),
                   jax.ShapeDtypeStruct((T, CONV_WIDTH), STASH), jax.ShapeDtypeStruct((T, HG_WIDTH), F32), jax.ShapeDtypeStruct((T, HG_WIDTH), BF16),
                   jax.ShapeDtypeStruct((HG_WIDTH, T), BF16),
                   jax.ShapeDtypeStruct((N_HEADS, T // CHUNK, HEAD_DIM, HEAD_DIM), F32)],
        scratch_shapes=[pltpu.VMEM((HALO, CONV_WIDTH), F32), pltpu.VMEM((N_HEADS, HEAD_DIM, HEAD_DIM), F32)],
        compiler_params=_params(("arbitrary",), vmem=VMEM_LIMIT_LARGE_V7X),
    )(x, g, w_in_t, conv_w, low, gn)


def _chunk_pos(shape):
    return lax.broadcasted_iota(jnp.int32, shape, 0) & (CHUNK - 1)


def _chunk_cumsum(x, pos):
    s = 1
    while s < CHUNK:
        x = x + jnp.where(pos >= s, pltpu.roll(x, s, 0), 0.0)
        s *= 2
    return x


def _chunk_rev_cumsum(x, pos):
    n = x.shape[0]
    s = 1
    while s < CHUNK:
        x = x + jnp.where(pos + s < CHUNK, pltpu.roll(x, n - s, 0), 0.0)
        s *= 2
    return x


def _lower_bound(low_ref):
    l0 = low_ref[0:1, :]
    l1 = low_ref[1:2, :]
    m = jnp.maximum(l0, l1)
    e0 = jnp.exp(l0 - m)
    e1 = jnp.exp(l1 - m)
    return e0 / (e0 + e1), e1 / (e0 + e1)


def _hg_gates(qr, fr, lb, pos, tb):
    sq = _sigmoid(qr)
    q = qr * sq * Q_SCALE
    sg = _sigmoid(fr)
    f = lb + (1.0 - lb) * sg
    k = 1.0 - f
    b = _chunk_cumsum(jnp.log(f), pos)
    b3 = b.reshape(tb // CHUNK, CHUNK, HEAD_DIM)
    anc = b3[:, CHUNK // 2 - 1:CHUNK // 2, :]
    last = b3[:, CHUNK - 1:CHUNK, :]
    d3 = b3 - anc
    e_qa3 = jnp.exp(d3)
    e_ka3 = jnp.exp(-d3)
    e_b3 = e_qa3 * jnp.exp(anc)
    e_ko3 = e_ka3 * jnp.exp(last - anc)
    dec = jnp.exp(last)
    flat = lambda a: a.reshape(tb, HEAD_DIM)
    return sq, q, sg, f, k, flat(e_qa3), flat(e_ka3), flat(e_b3), flat(e_ko3), dec


def _intra_mask(sb):
    r = lax.broadcasted_iota(jnp.int32, (sb, sb), 0)
    c = lax.broadcasted_iota(jnp.int32, (sb, sb), 1)
    return ((r // CHUNK) == (c // CHUNK)) & (c <= r)


def _hg_fwd_tile(hg_ref, low_ref, gn_ref, o_ref, og_ref, ogt_ref, st_ref, s_scr, tb):
    sb = min(256, tb)
    nc = tb // CHUNK
    q_ref, f_ref, i_ref, g_ref = (hg_ref.at[:, p * HG_WIDTH:(p + 1) * HG_WIDTH] for p in range(4))
    pos = _chunk_pos((tb, HEAD_DIM))
    mask = _intra_mask(sb)
    lanes = [slice(hh * HEAD_DIM, (hh + 1) * HEAD_DIM) for hh in range(N_HEADS)]
    qi, ko, vb, dec, st = [], [], [], [], []
    for hh, ln in enumerate(lanes):
        lb, _ = _lower_bound(low_ref.at[:, ln])
        _, q, _, _, k, e_qa, e_ka, e_b, e_ko, dec_h = _hg_gates(q_ref[:, ln], f_ref[:, ln], lb, pos, tb)
        qh = (q * e_qa).astype(BF16)
        kh = (k * e_ka).astype(BF16)
        qi.append((q * e_b).astype(BF16))
        ko.append((k * e_ko).astype(BF16))
        vb.append(i_ref[:, ln].astype(BF16))
        dec.append(dec_h)
        st.append(s_scr[hh])
        for s in range(tb // sb):
            sl = slice(s * sb, (s + 1) * sb)
            p = jnp.where(mask, _mm_nt(qh[sl], kh[sl]), 0.0)
            o_ref[sl, ln] = _mm(p, vb[hh][sl])
    for c in range(nc):
        sl = slice(c * CHUNK, (c + 1) * CHUNK)
        for hh, ln in enumerate(lanes):
            st_ref[hh, c] = st[hh]
            o_ref[sl, ln] = o_ref[sl, ln] + _mm_nt(qi[hh][sl], st[hh])
            st[hh] = dec[hh][c] * st[hh] + _mm_tn(vb[hh][sl], ko[hh][sl])
    for hh, ln in enumerate(lanes):
        s_scr[hh] = st[hh]
        o = o_ref[:, ln]
        r = lax.rsqrt(jnp.mean(o * o, axis=-1, keepdims=True) + EPS)
        gr = g_ref[:, ln]
        og = (o * r * gn_ref[...]) * (gr * _sigmoid(gr))
        og_ref[:, ln] = og.astype(BF16)
        ogt_ref[ln, :] = og.T.astype(BF16)


def _merge_fwd(og, cvo, gt, x, wa, wb, wo):
    T = x.shape[0]
    tm = min(1024, T)

    def body(og_ref, cvo_ref, gt_ref, x_ref, wa_ref, wb_ref, wo_ref, x1_ref, mgt_ref, ya_ref, yb_ref):
        ya = jnp.dot(og_ref[...], _shard_cols(wa_ref), preferred_element_type=F32)
        yb = jnp.dot(cvo_ref[...], _shard_cols(wb_ref), preferred_element_type=F32)
        ya_ref[...] = ya.astype(STASH)
        yb_ref[...] = yb.astype(STASH)
        m = (_sigmoid(gt_ref[:, :D_MODEL].astype(F32)) * ya
             + _sigmoid(gt_ref[:, D_MODEL:].astype(F32)) * yb)
        mgt_ref[...] = m.T.astype(BF16)
        x1_ref[...] = x_ref[...] + jnp.dot(m.astype(BF16), wo_ref[...], preferred_element_type=F32)

    row = lambda n: pl.BlockSpec((tm, n), lambda i: (i, 0))
    return pl.pallas_call(
        body, name="merge_fwd", grid=(T // tm,),
        in_specs=[row(HG_WIDTH), row(CONV_WIDTH), row(2 * D_MODEL), row(D_MODEL),
                  _resident(wa.shape), _resident(wb.shape), _resident(wo.shape)],
        out_specs=[row(D_MODEL), _col(tm, D_MODEL), row(D_MODEL), row(D_MODEL)],
        out_shape=[jax.ShapeDtypeStruct((T, D_MODEL), F32), jax.ShapeDtypeStruct((D_MODEL, T), BF16),
                   jax.ShapeDtypeStruct((T, D_MODEL), STASH), jax.ShapeDtypeStruct((T, D_MODEL), STASH)],
        compiler_params=_params(("parallel",)),
    )(og, cvo, gt, x, wa, wb, wo)


def _ffn_fwd_loss(x1, g, wg, wu, wd, target, g_fin):
    T = x1.shape[0]
    tm = min(512, T)

    def body(x_ref, g_ref, wg_ref, wu_ref, wd_ref, t_ref, gf_ref,
             ht_ref, gate_ref, up_ref, act_ref, loss_ref, dgf_ref, dx2_ref, dx2t_ref):
        @pl.when(pl.program_id(0) == 0)
        def _():
            loss_ref[...] = jnp.zeros_like(loss_ref)
            dgf_ref[...] = jnp.zeros_like(dgf_ref)

        xv = x_ref[...]
        r = lax.rsqrt(jnp.mean(xv * xv, axis=-1, keepdims=True) + EPS)
        hf = xv * r * g_ref[...]
        h = hf.astype(BF16)
        ht_ref[...] = hf.T.astype(BF16)
        gate = _mm_nt(h, wg_ref[...])
        up = _mm_nt(h, wu_ref[...])
        gate_ref[...] = gate.astype(STASH)
        up_ref[...] = up.astype(STASH)
        act = (gate * _sigmoid(gate) * up).astype(BF16)
        act_ref[...] = act
        x2 = xv + jnp.dot(act, wd_ref[...], preferred_element_type=F32)

        gv = gf_ref[...]
        r2 = lax.rsqrt(jnp.mean(x2 * x2, axis=-1, keepdims=True) + EPS)
        xh = x2 * r2
        err = xh * gv - t_ref[...]
        loss_ref[...] += 0.5 * jnp.sum(jnp.mean(err * err, axis=-1, keepdims=True), axis=0, keepdims=True)
        dy = err * (1.0 / D_MODEL)
        dgf_ref[...] += jnp.sum(dy * xh, axis=0, keepdims=True)
        w = dy * gv
        dx2 = r2 * (w - xh * jnp.mean(w * xh, axis=-1, keepdims=True))
        dx2_ref[...] = dx2
        dx2t_ref[...] = dx2.T.astype(BF16)

    row = lambda n: pl.BlockSpec((tm, n), lambda i: (i, 0))
    return pl.pallas_call(
        body, name="ffn_fwd_loss", grid=(T // tm,),
        in_specs=[row(D_MODEL), _full((1, D_MODEL)), _resident(wg.shape), _resident(wu.shape), _resident(wd.shape),
                  row(D_MODEL), _full((1, D_MODEL))],
        out_specs=[_col(tm, D_MODEL), row(D_FF), row(D_FF), row(D_FF), _full((1, 128)), _full((1, D_MODEL)),
                   row(D_MODEL), _col(tm, D_MODEL)],
        out_shape=[jax.ShapeDtypeStruct((D_MODEL, T), BF16), jax.ShapeDtypeStruct((T, D_FF), STASH),
                   jax.ShapeDtypeStruct((T, D_FF), STASH), jax.ShapeDtypeStruct((T, D_FF), BF16),
                   jax.ShapeDtypeStruct((1, 128), F32), jax.ShapeDtypeStruct((1, D_MODEL), F32),
                   jax.ShapeDtypeStruct((T, D_MODEL), F32), jax.ShapeDtypeStruct((D_MODEL, T), BF16)],
        compiler_params=_params(("arbitrary",), vmem=VMEM_LIMIT_LARGE_V7X),
    )(x1, g, wg, wu, wd, target, g_fin)


def _ffn_bwd(dx2, x1, gate, up, g, wg, wu, wd):
    T = x1.shape[0]
    tm = min(512, T)

    def body(dx2_ref, x_ref, gate_ref, up_ref, g_ref, wg_ref, wu_ref, wd_ref, dgate_ref, dup_ref, dx1_ref, dgn_ref):
        @pl.when(pl.program_id(0) == 0)
        def _():
            dgn_ref[...] = jnp.zeros_like(dgn_ref)

        dx2 = dx2_ref[...]
        dact = _mm_nt(dx2, wd_ref[...])
        gate = gate_ref[...].astype(F32)
        s = _sigmoid(gate)
        dgate = (dact * up_ref[...].astype(F32) * (s * (1.0 + gate * (1.0 - s)))).astype(BF16)
        dup = (dact * (gate * s)).astype(BF16)
        dgate_ref[...] = dgate
        dup_ref[...] = dup
        dh = _mm(dgate, wg_ref[...]) + _mm(dup, wu_ref[...])
        xv = x_ref[...]
        r = lax.rsqrt(jnp.mean(xv * xv, axis=-1, keepdims=True) + EPS)
        xh = xv * r
        dgn_ref[...] += jnp.sum(dh * xh, axis=0, keepdims=True)
        w = dh * g_ref[...]
        dx1_ref[...] = dx2 + r * (w - xh * jnp.mean(w * xh, axis=-1, keepdims=True))

    row = lambda n: pl.BlockSpec((tm, n), lambda i: (i, 0))
    return pl.pallas_call(
        body, name="ffn_bwd", grid=(T // tm,),
        in_specs=[row(D_MODEL), row(D_MODEL), row(D_FF), row(D_FF), _full((1, D_MODEL)),
                  _resident(wg.shape), _resident(wu.shape), _resident(wd.shape)],
        out_specs=[row(D_FF), row(D_FF), row(D_MODEL), _full((1, D_MODEL))],
        out_shape=[jax.ShapeDtypeStruct((T, D_FF), BF16), jax.ShapeDtypeStruct((T, D_FF), BF16),
                   jax.ShapeDtypeStruct((T, D_MODEL), F32), jax.ShapeDtypeStruct((1, D_MODEL), F32)],
        compiler_params=_params(("arbitrary",), vmem=VMEM_LIMIT_LARGE_V7X),
    )(dx2, x1, gate, up, g, wg, wu, wd)


def _merge_bwd(dx1, ya, yb, gt, cv, cy, wa, wb, wo, conv_w):
    T = dx1.shape[0]
    tm = min(512, T)
    nt = T // tm

    def body(dx_ref, ya_ref, yb_ref, gt_ref, cv_ref, cy_ref, wa_ref, wb_ref, wo_ref, cw_ref,
             dgt_ref, dya_ref, dyb_ref, dog_ref, dcv_ref, dcw_ref, next_dy):
        @pl.when(pl.program_id(0) == 0)
        def _():
            next_dy[...] = jnp.zeros_like(next_dy)
            dcw_ref[...] = jnp.zeros_like(dcw_ref)

        dm = _mm_nt(dx_ref[...], wo_ref[...])
        wa = _shard_cols(wa_ref)
        wb = _shard_cols(wb_ref)
        ya = ya_ref[...].astype(F32)
        yb = yb_ref[...].astype(F32)
        sa = _sigmoid(gt_ref[:, :D_MODEL].astype(F32))
        sb = _sigmoid(gt_ref[:, D_MODEL:].astype(F32))
        da = dm * sa
        db = dm * sb
        dgt_ref[:, :D_MODEL] = (da * ya * (1.0 - sa)).astype(BF16)
        dgt_ref[:, D_MODEL:] = (db * yb * (1.0 - sb)).astype(BF16)
        dya = da.astype(BF16)
        dyb = db.astype(BF16)
        dya_ref[...] = dya
        dyb_ref[...] = dyb
        dog_ref[...] = _mm_nt(dya, wa)
        dcvo = _mm_nt(dyb, wb)

        cvt = cv_ref[...].astype(F32)
        c, bg, xb = cvt[:, :CONV_WIDTH], cvt[:, CONV_WIDTH:2 * CONV_WIDTH], cvt[:, 2 * CONV_WIDTH:]
        u = c * xb
        row = lax.broadcasted_iota(jnp.int32, u.shape, 0)
        w0, w1, w2 = cw_ref[0:1, :], cw_ref[1:2, :], cw_ref[2:3, :]
        dcv_ref[:, CONV_WIDTH:2 * CONV_WIDTH] = (dcvo * cy_ref[...].astype(F32)).astype(BF16)
        dy = dcvo * bg
        n1 = next_dy[0:1, :]
        n2 = next_dy[1:2, :]
        dy1 = jnp.where(row < tm - 1, pltpu.roll(dy, tm - 1, 0), n1)
        dy2 = jnp.where(row < tm - 2, pltpu.roll(dy, tm - 2, 0), jnp.where(row == tm - 2, n1, n2))
        dcw_ref[0:1, :] += jnp.sum(dy2 * u, axis=0, keepdims=True)
        dcw_ref[1:2, :] += jnp.sum(dy1 * u, axis=0, keepdims=True)
        dcw_ref[2:3, :] += jnp.sum(dy * u, axis=0, keepdims=True)
        du = w2 * dy + w1 * dy1 + w0 * dy2
        dcv_ref[:, :CONV_WIDTH] = (du * xb).astype(BF16)
        dcv_ref[:, 2 * CONV_WIDTH:] = (du * c).astype(BF16)
        next_dy[...] = dy[:HALO, :]

    rt = lambda i: nt - 1 - i
    row = lambda n: pl.BlockSpec((tm, n), lambda i: (rt(i), 0))
    return pl.pallas_call(
        body, name="merge_bwd", grid=(nt,),
        in_specs=[row(D_MODEL), row(D_MODEL), row(D_MODEL), row(2 * D_MODEL), row(N_CV), row(CONV_WIDTH),
                  _resident(wa.shape), _resident(wb.shape), _resident(wo.shape), _full((CONV_K, CONV_WIDTH))],
        out_specs=[row(2 * D_MODEL), row(D_MODEL), row(D_MODEL), row(HG_WIDTH), row(N_CV),
                   _full((CONV_K, CONV_WIDTH))],
        out_shape=[jax.ShapeDtypeStruct((T, 2 * D_MODEL), BF16), jax.ShapeDtypeStruct((T, D_MODEL), BF16),
                   jax.ShapeDtypeStruct((T, D_MODEL), BF16), jax.ShapeDtypeStruct((T, HG_WIDTH), F32),
                   jax.ShapeDtypeStruct((T, N_CV), BF16), jax.ShapeDtypeStruct((CONV_K, CONV_WIDTH), F32)],
        scratch_shapes=[pltpu.VMEM((HALO, CONV_WIDTH), F32)],
        compiler_params=_params(("arbitrary",)),
    )(dx1, ya, yb, gt, cv, cy, wa, wb, wo, conv_w)


def _drop_operands(body, first, count):
    def wrapped(*refs):
        return body(*refs[:first], *refs[first + count:])
    return wrapped


def _hg_bwd(dog, hg, o, st, low, gn, after=()):
    T = hg.shape[0]
    tb = min(512, T)
    sb = min(256, tb)
    nb = T // tb
    nc = tb // CHUNK
    wid = HEADS_PER_STEP * HEAD_DIM

    def body(q_ref, f_ref, i_ref, g_ref, low_ref, gn_ref, o_ref, dog_ref, st_ref,
             dhg_ref, dlow_ref, dgn_ref,
             ds_scr, dqi_scr, dko_scr, dv_scr, dd_scr, dqh_scr, dkh_scr):
        h = pl.program_id(0)
        t = pl.program_id(1)
        dq_ref, df_ref, di_ref, dg_ref = (dhg_ref.at[:, p * HG_WIDTH:(p + 1) * HG_WIDTH] for p in range(4))

        @pl.when(t == 0)
        def _():
            ds_scr[...] = jnp.zeros_like(ds_scr)
            dlow_ref[...] = jnp.zeros_like(dlow_ref)

        @pl.when((t == 0) & (h == 0))
        def _():
            dgn_ref[...] = jnp.zeros_like(dgn_ref)

        pos = _chunk_pos((tb, HEAD_DIM))
        mask = _intra_mask(sb)
        gnv = gn_ref[...]
        lanes = [slice(hh * HEAD_DIM, (hh + 1) * HEAD_DIM) for hh in range(HEADS_PER_STEP)]
        heads = []
        for hh, ln in enumerate(lanes):
            lb, lb1 = _lower_bound(low_ref.at[:, ln])
            qr = q_ref[:, ln]
            sq, q, sg, f, k, e_qa, e_ka, e_b, e_ko, dec = _hg_gates(qr, f_ref[:, ln], lb, pos, tb)

            gr = g_ref[:, ln]
            o = o_ref[:, ln]
            dog_v = dog_ref[:, ln]
            sgr = _sigmoid(gr)
            r = lax.rsqrt(jnp.mean(o * o, axis=-1, keepdims=True) + EPS)
            oh = o * r
            dg_ref[:, ln] = (dog_v * (oh * gnv) * (sgr * (1.0 + gr * (1.0 - sgr)))).astype(BF16)
            don = dog_v * (gr * sgr)
            dgn_ref[...] += jnp.sum(don * oh, axis=0, keepdims=True)
            w = don * gnv
            do = (r * (w - oh * jnp.mean(w * oh, axis=-1, keepdims=True))).astype(BF16)

            qh = (q * e_qa).astype(BF16)
            kh = (k * e_ka).astype(BF16)
            qi = (q * e_b).astype(BF16)
            ko = (k * e_ko).astype(BF16)
            vb = i_ref[:, ln].astype(BF16)

            for s in range(tb // sb):
                sl = slice(s * sb, (s + 1) * sb)
                p = jnp.where(mask, _mm_nt(qh[sl], kh[sl]), 0.0).astype(BF16)
                dp = jnp.where(mask, _mm_nt(do[sl], vb[sl]), 0.0).astype(BF16)
                dv_scr[sl, ln] = _mm_tn(p, do[sl])
                dqh_scr[sl, ln] = _mm(dp, kh[sl])
                dkh_scr[sl, ln] = _mm_tn(dp, qh[sl])
            heads.append(dict(lb=lb, lb1=lb1, qr=qr, sq=sq, q=q, sg=sg, f=f, k=k, e_qa=e_qa, e_ka=e_ka, e_b=e_b,
                              e_ko=e_ko, dec=dec, do=do, qi=qi, ko=ko, vb=vb, ds=ds_scr[hh]))

        for c in reversed(range(nc)):
            sl = slice(c * CHUNK, (c + 1) * CHUNK)
            for hh, ln in enumerate(lanes):
                hd = heads[hh]
                ds = hd["ds"]
                st_c = st_ref[hh, c]
                dqi_scr[sl, ln] = _mm(hd["do"][sl], st_c)
                dko_scr[sl, ln] = _mm(hd["vb"][sl], ds)
                dv_scr[sl, ln] = dv_scr[sl, ln] + _mm_nt(hd["ko"][sl], ds)
                dec_c = hd["dec"][c]
                dd_scr[sl, ln] = jnp.broadcast_to(dec_c * jnp.sum(ds * st_c, axis=0, keepdims=True),
                                                  (CHUNK, HEAD_DIM))
                hd["ds"] = dec_c * ds + _mm_tn(hd["do"][sl], hd["qi"][sl])

        for hh, ln in enumerate(lanes):
            hd = heads[hh]
            ds_scr[hh] = hd["ds"]
            q, k, lb = hd["q"], hd["k"], hd["lb"]
            dko_e = dko_scr[:, ln] * hd["e_ko"]
            dq = dqh_scr[:, ln] * hd["e_qa"] + dqi_scr[:, ln] * hd["e_b"]
            dk = dkh_scr[:, ln] * hd["e_ka"] + dko_e
            kd3 = (k * dko_e).reshape(nc, CHUNK, HEAD_DIM)
            last = jnp.broadcast_to(jnp.sum(kd3, axis=1, keepdims=True), kd3.shape).reshape(tb, HEAD_DIM)
            db = q * dq - k * dk + jnp.where(pos == CHUNK - 1, dd_scr[:, ln] + last, 0.0)
            dlg = _chunk_rev_cumsum(db, pos)
            dfv = dlg / hd["f"] - dk
            s_low = jnp.sum(dfv * (1.0 - hd["sg"]), axis=0, keepdims=True)
            dlow_ref[0:1, ln] += s_low * lb * (1.0 - lb)
            dlow_ref[1:2, ln] += -s_low * lb * hd["lb1"]
            df_ref[:, ln] = (dfv * (1.0 - lb) * hd["sg"] * (1.0 - hd["sg"])).astype(BF16)
            dq_ref[:, ln] = (dq * Q_SCALE * (hd["sq"] * (1.0 + hd["qr"] * (1.0 - hd["sq"])))).astype(BF16)
            di_ref[:, ln] = dv_scr[:, ln].astype(BF16)

    rt = lambda t: nb - 1 - t
    col = lambda p: pl.BlockSpec((tb, wid), lambda h, t: (rt(t), p * HEAD_GROUPS + h))
    hcol = pl.BlockSpec((tb, wid), lambda h, t: (rt(t), h))
    assert HEAD_GROUPS == 1
    tile = pltpu.VMEM((tb, wid), F32)
    return pl.pallas_call(
        _drop_operands(body, 9, len(after)), name="hg_bwd", grid=(HEAD_GROUPS, nb),
        in_specs=[col(0), col(1), col(2), col(3), pl.BlockSpec((2, wid), lambda h, t: (0, h)),
                  pl.BlockSpec((1, HEAD_DIM), lambda h, t: (0, 0)), hcol, hcol,
                  pl.BlockSpec((HEADS_PER_STEP, nc, HEAD_DIM, HEAD_DIM), lambda h, t: (h, rt(t), 0, 0))]
                 + [HBM_SPEC] * len(after),
        out_specs=[pl.BlockSpec((tb, N_HG), lambda h, t: (rt(t), 0)), pl.BlockSpec((2, wid), lambda h, t: (0, h)),
                   pl.BlockSpec((1, HEAD_DIM), lambda h, t: (0, 0))],
        out_shape=[jax.ShapeDtypeStruct((T, N_HG), BF16), jax.ShapeDtypeStruct((2, HG_WIDTH), F32),
                   jax.ShapeDtypeStruct((1, HEAD_DIM), F32)],
        scratch_shapes=[pltpu.VMEM((HEADS_PER_STEP, HEAD_DIM, HEAD_DIM), F32), tile, tile, tile, tile, tile, tile],
        compiler_params=_params(("arbitrary", "arbitrary")),
    )(hg, hg, hg, hg, low, gn, o, dog, st, *after)


def _in_bwd(dparts, w_in, x, dx1, g, after=()):
    T = x.shape[0]
    tm = min(512, T)
    widths = [p.shape[1] for p in dparts]
    offs = [sum(widths[:i]) for i in range(len(widths))]
    n = len(dparts)

    def body(*refs):
        d_refs = refs[:n]
        w_ref, x_ref, dx1_ref, g_ref, dx_ref, dgn_ref = refs[n:]

        @pl.when(pl.program_id(0) == 0)
        def _():
            dgn_ref[...] = jnp.zeros_like(dgn_ref)

        dh = None
        for d_ref, off, wd in zip(d_refs, offs, widths):
            part = _mm(d_ref[...], w_ref[off:off + wd, :])
            dh = part if dh is None else dh + part
        xv = x_ref[...]
        r = lax.rsqrt(jnp.mean(xv * xv, axis=-1, keepdims=True) + EPS)
        xh = xv * r
        dgn_ref[...] += jnp.sum(dh * xh, axis=0, keepdims=True)
        w = dh * g_ref[...]
        dx_ref[...] = dx1_ref[...] + r * (w - xh * jnp.mean(w * xh, axis=-1, keepdims=True))

    row = lambda m: pl.BlockSpec((tm, m), lambda i: (i, 0))
    return pl.pallas_call(
        _drop_operands(body, n + 4, len(after)), name="in_bwd", grid=(T // tm,),
        in_specs=[row(wd) for wd in widths] + [_resident(w_in.shape), row(D_MODEL), row(D_MODEL), _full((1, D_MODEL))]
                 + [HBM_SPEC] * len(after),
        out_specs=[row(D_MODEL), _full((1, D_MODEL))],
        out_shape=[jax.ShapeDtypeStruct((T, D_MODEL), F32), jax.ShapeDtypeStruct((1, D_MODEL), F32)],
        compiler_params=_params(("arbitrary",)),
    )(*dparts, w_in, x, dx1, g, *after)


def _wgrad_ffn(h2t, dgate, dup, dx2t, act, tn=256):
    M, T = h2t.shape
    N = dgate.shape[1]

    def body(h_ref, x_ref, dg_ref, du_ref, act_ref, og_ref, ou_ref, od_ref):
        h = h_ref[...]
        og_ref[...] = _mm(h, dg_ref[...]).T.astype(BF16)
        ou_ref[...] = _mm(h, du_ref[...]).T.astype(BF16)
        od_ref[...] = _mm(x_ref[...], act_ref[...]).T.astype(BF16)

    rhs = pl.BlockSpec((T, tn), lambda j: (0, j))
    out_spec = pl.BlockSpec((tn, M), lambda j: (j, 0))
    out = jax.ShapeDtypeStruct((N, M), BF16)
    return pl.pallas_call(
        body, name="wgrad_ffn", grid=(N // tn,),
        in_specs=[_resident((M, T)), _resident((M, T)), rhs, rhs, rhs], out_specs=[out_spec] * 3,
        out_shape=[out, out, out],
        compiler_params=_params(("parallel",)),
    )(h2t, dx2t, dgate, dup, act)


def _wgrad_out_branches(ogt, dya, cvot, dyb, mgt, dx1, by_owner, got, core):
    M, T = ogt.shape
    N = dya.shape[1]
    c = N // N_DEV
    per = 2
    tn = per * c
    n = len(got)
    assert N // tn == got[0].shape[0]

    def body(core_ref, at_ref, da_ref, bt_ref, db_ref, mt_ref, dx_ref, *refs):
        mine, theirs = refs[:n], refs[n:2 * n]
        oa_ref, ob_ref, oo_ref = refs[2 * n:2 * n + 3]
        ga = _mm(at_ref[...], da_ref[...])
        gb = _mm(bt_ref[...], db_ref[...])
        for s in range(per):
            oa_ref[s] = ga[:, s * c:(s + 1) * c].astype(BF16)
            ob_ref[s] = gb[:, s * c:(s + 1) * c].astype(BF16)
        oo_ref[...] = _mm(mt_ref[...], dx_ref[...]).astype(BF16)
        for a_ref, b_ref, o_ref in zip(mine, theirs, refs[2 * n + 3:]):
            o_ref[...] = (a_ref[...].astype(F32) + b_ref[...].astype(F32)).astype(BF16)

    def blk(g):
        return pl.BlockSpec((None,) + g.shape[1:], lambda j, core_ref: (j, 0, 0))

    def own(g):
        return pl.BlockSpec((None,) + g.shape[1:], lambda j, core_ref: (2 * j + core_ref[0], 0, 0))

    rhs = pl.BlockSpec((T, tn), lambda j, core_ref: (0, j))
    owners = pl.BlockSpec((per, M, c), lambda j, core_ref: (j, 0, 0))
    out = jax.ShapeDtypeStruct((N_DEV, M, c), BF16)
    return pl.pallas_call(
        body, name="wgrad_out_branches",
        grid_spec=pltpu.PrefetchScalarGridSpec(
            num_scalar_prefetch=1, grid=(N // tn,),
            in_specs=[_resident((M, T)), rhs, _resident((M, T)), rhs, _resident(mgt.shape), rhs]
            + [own(g) for g in by_owner] + [blk(g) for g in got],
            out_specs=[owners, owners, pl.BlockSpec((mgt.shape[0], tn), lambda j, core_ref: (0, j))]
            + [blk(g) for g in got]),
        out_shape=[out, out, jax.ShapeDtypeStruct((mgt.shape[0], dx1.shape[1]), BF16)]
        + [jax.ShapeDtypeStruct(g.shape, BF16) for g in got],
        compiler_params=_params(("parallel",)),
    )(core, ogt, dya, cvot, dyb, mgt, dx1, *by_owner, *got)


def _wgrad_in(ht, dparts, after=(), riders=()):
    M, T = ht.shape
    tn = 512
    nblk = [p.shape[1] // tn for p in dparts]
    start = [sum(nblk[:i]) for i in range(len(nblk))]
    n = len(dparts)
    steps = sum(nblk)
    nr = len(riders)
    rows = [r[0].shape[0] // steps for r in riders]
    assert all(r[0].shape[0] == rr * steps and rr % 16 == 0 for r, rr in zip(riders, rows))

    def body(a_ref, *refs):
        d_refs = refs[:n]
        rider_in = refs[n:n + 4 * nr]
        o_ref = refs[n + 4 * nr]
        rider_out = refs[n + 4 * nr + 1:]
        j = pl.program_id(0)
        for d_ref, s, nb in zip(d_refs, start, nblk):
            @pl.when((j >= s) & (j < s + nb))
            def _():
                o_ref[...] = _mm(a_ref[...], d_ref[...]).T.astype(BF16)
        for i in range(nr):
            w_ref, p_ref, m_ref, v_ref = rider_in[4 * i:4 * i + 4]
            g = p_ref[0].astype(F32)
            for k in range(1, 4):
                g = g + p_ref[k].astype(F32)
            delta, m_new, v_new = _adamw_math(w_ref[...], g, m_ref[...], v_ref[...])
            for k, val in enumerate((g, delta, m_new, v_new)):
                rider_out[4 * i + k][...] = val

    def piece_spec(s, nb):
        return pl.BlockSpec((T, tn), lambda j: (0, jnp.clip(j - s, 0, nb - 1)))

    rider_specs, rider_out_specs, rider_out_shape, rider_args = [], [], [], []
    for (w, parts, m, v), rr in zip(riders, rows):
        blk = pl.BlockSpec((rr, w.shape[1]), lambda j: (j, 0))
        rider_specs += [blk, pl.BlockSpec((4, rr, w.shape[1]), lambda j: (0, j, 0)), blk, blk]
        rider_out_specs += [blk] * 4
        rider_out_shape += [jax.ShapeDtypeStruct(w.shape, F32)] * 4
        rider_args += [w, parts, m, v]
    outs = pl.pallas_call(
        _drop_operands(body, 1 + n + 4 * nr, len(after)), name="wgrad_in", grid=(steps,),
        in_specs=[_resident((M, T))] + [piece_spec(s, nb) for s, nb in zip(start, nblk)] + rider_specs
                 + [HBM_SPEC] * len(after),
        out_specs=[pl.BlockSpec((tn, M), lambda j: (j, 0))] + rider_out_specs,
        out_shape=[jax.ShapeDtypeStruct((steps * tn, M), BF16)] + rider_out_shape,
        compiler_params=_params(("parallel",)),
    )(ht, *dparts, *rider_args, *after)
    return outs[0], [outs[1 + 4 * i:5 + 4 * i] for i in range(nr)]


def _adamw_math(w, g, m, v):
    m = ADAM_B1 * m + (1.0 - ADAM_B1) * g
    v = ADAM_B2 * v + (1.0 - ADAM_B2) * (g * g)
    m_hat = m / (1.0 - ADAM_B1 ** ADAM_STEP)
    v_hat = v / (1.0 - ADAM_B2 ** ADAM_STEP)
    delta = -ADAM_LR * (m_hat / (jnp.sqrt(v_hat) + ADAM_EPS) + ADAM_WD * w)
    return delta, m, v


def _adamw_sum(name, ws, parts, ms, vs):
    n = len(ws)
    steps = min(_row_steps(w.shape[0]) for w in ws)
    rows = [w.shape[0] // steps for w in ws]

    def body(*refs):
        w_refs, p_refs, m_refs, v_refs = (refs[k * n:(k + 1) * n] for k in range(4))
        out_refs = refs[4 * n:]
        for i in range(n):
            g = p_refs[i][0].astype(F32)
            for k in range(1, 4):
                g = g + p_refs[i][k].astype(F32)
            delta, m_new, v_new = _adamw_math(w_refs[i][...], g, m_refs[i][...], v_refs[i][...])
            for k, val in enumerate((g, delta, m_new, v_new)):
                out_refs[4 * i + k][...] = val

    blk = [pl.BlockSpec((r, w.shape[1]), lambda s: (s, 0)) for r, w in zip(rows, ws)]
    pblk = [pl.BlockSpec((4, r, w.shape[1]), lambda s: (0, s, 0)) for r, w in zip(rows, ws)]
    out_specs, out_shape = [], []
    for b, w in zip(blk, ws):
        out_specs += [b] * 4
        out_shape += [jax.ShapeDtypeStruct(w.shape, F32)] * 4
    flat = pl.pallas_call(
        body, name=name, grid=(steps,),
        in_specs=blk + pblk + blk + blk, out_specs=out_specs, out_shape=out_shape,
        compiler_params=_params(("parallel",)),
    )(*ws, *parts, *ms, *vs)
    return [flat[4 * i:4 * i + 4] for i in range(n)]


_SMALL_SLOTS = (("norm_mix_g", 0, 1, 1024), ("norm_ffn_g", 1, 1, 1024), ("norm_final_g", 2, 1, 1024),
                ("lower_bounds", 3, 2, 512), ("hg_norm_g", 5, 1, 128), ("loss", 6, 1, 128), ("conv_w", 8, 3, 512))
_SMALL_PARAMS = tuple(s for s in _SMALL_SLOTS if s[0] != "loss")
CONV_SHARD = CONV_WIDTH // N_DEV


def _small_pack(small):
    def body(*refs):
        out = refs[-1]
        out[...] = jnp.zeros_like(out)
        for ref, (_, row, rows, lanes) in zip(refs[:-1], _SMALL_SLOTS):
            out[row:row + rows, 0:lanes] = ref[...]

    vmem = pl.BlockSpec(memory_space=pltpu.VMEM)
    return pl.pallas_call(
        body, name="small_pack", in_specs=[vmem] * len(_SMALL_SLOTS), out_specs=vmem,
        out_shape=jax.ShapeDtypeStruct((SMALL_ROWS, 1024), F32),
    )(*[small[name] for name, _, _, _ in _SMALL_SLOTS])


def _small_update(gathered, dev, w, m, v):
    n = len(_SMALL_PARAMS)

    def body(dev_ref, g_ref, *refs):
        w_refs, m_refs, v_refs = refs[:n], refs[n:2 * n], refs[2 * n:3 * n]
        loss_ref, out_refs, sum_scr = refs[3 * n], refs[3 * n + 1:-1], refs[-1]
        total = g_ref[0]
        for k in range(1, N_DEV):
            total = total + g_ref[k]
        sum_scr[...] = total
        loss_ref[...] = sum_scr[6:7, 0:128]
        for p, (name, row, rows, lanes) in enumerate(_SMALL_PARAMS):
            if name == "conv_w":
                for r in range(rows):
                    g = sum_scr[row + r:row + r + 1, 0:CONV_SHARD]
                    for s in range(1, N_DEV):
                        mine = sum_scr[row + r:row + r + 1, s * CONV_SHARD:(s + 1) * CONV_SHARD]
                        g = jnp.where(dev_ref[0] == s, mine, g)
                    delta, m_new, v_new = _adamw_math(w_refs[p][r], g, m_refs[p][r], v_refs[p][r])
                    out_refs[4 * p][r] = g
                    out_refs[4 * p + 1][r] = delta
                    out_refs[4 * p + 2][r] = m_new
                    out_refs[4 * p + 3][r] = v_new
                continue
            g = sum_scr[row:row + rows, 0:lanes]
            delta, m_new, v_new = _adamw_math(w_refs[p][...], g, m_refs[p][...], v_refs[p][...])
            out_refs[4 * p][...] = g
            out_refs[4 * p + 1][...] = delta
            out_refs[4 * p + 2][...] = m_new
            out_refs[4 * p + 3][...] = v_new

    vmem = pl.BlockSpec(memory_space=pltpu.VMEM)
    outs = [jax.ShapeDtypeStruct((1, 128), F32)]
    for a in w:
        outs += [jax.ShapeDtypeStruct(a.shape, F32)] * 4
    return pl.pallas_call(
        body, name="small_update",
        in_specs=[pl.BlockSpec(memory_space=pltpu.SMEM)] + [vmem] * (1 + 3 * n), out_specs=[vmem] * len(outs),
        out_shape=outs, scratch_shapes=[pltpu.VMEM((SMALL_ROWS, 1024), F32)],
    )(dev, gathered, *w, *m, *v)


def _row_steps(rows):
    for steps in (4, 2):
        if rows % (16 * steps) == 0:
            return steps
    return 1


def _pair_sum(name, by_owner, got, core, after=()):
    n = len(got)

    def body(core_ref, *refs):
        for a_ref, b_ref, o_ref in zip(refs[:n], refs[n:2 * n], refs[2 * n:]):
            o_ref[...] = (a_ref[...].astype(F32) + b_ref[...].astype(F32)).astype(BF16)

    def blk(g):
        return pl.BlockSpec((None,) + g.shape[1:], lambda k, core_ref: (k, 0, 0))

    def mine(g):
        return pl.BlockSpec((None,) + g.shape[1:], lambda k, core_ref: (2 * k + core_ref[0], 0, 0))

    return pl.pallas_call(
        _drop_operands(body, 1 + 2 * n, len(after)), name=name,
        grid_spec=pltpu.PrefetchScalarGridSpec(
            num_scalar_prefetch=1, grid=(4,),
            in_specs=[mine(g) for g in got] + [blk(g) for g in got] + [HBM_SPEC] * len(after),
            out_specs=[blk(g) for g in got]),
        out_shape=[jax.ShapeDtypeStruct(g.shape, BF16) for g in got],
        compiler_params=_params(("parallel",)),
    )(core, *by_owner, *got, *after)


MESH = pl.DeviceIdType.MESH
HBM_SPEC = pl.BlockSpec(memory_space=pl.ANY)


def _handshake(peers):
    barrier = pltpu.get_barrier_semaphore()
    for peer in peers:
        pl.semaphore_signal(barrier, inc=1, device_id=peer, device_id_type=MESH)
    pl.semaphore_wait(barrier, len(peers))


def _comm_call(body, name, operands, out_shape, scratch, collective_id):
    if collective_id is None:
        return pl.pallas_call(body, name=name, in_specs=[HBM_SPEC] * len(operands), out_specs=[HBM_SPEC] * len(out_shape),
                              out_shape=out_shape, scratch_shapes=scratch)(*operands)
    return pl.kernel(body, out_type=out_shape, mesh=plsc.ScalarSubcoreMesh(axis_name="sequencer", num_cores=1),
                     scratch_types=scratch, name=name,
                     compiler_params=pltpu.CompilerParams(collective_id=collective_id))(*operands)


def _all_gather(name, blocks, collective_id=None, after=(), pieces=None):
    n = len(blocks)
    na = len(after)
    pieces = pieces or [1] * n
    parts = []
    for i, (b, k) in enumerate(zip(blocks, pieces)):
        rows, rem = divmod(b.shape[0], k)
        assert rem == 0 and (k == 1 or rows % 16 == 0), (b.shape, k)
        parts += [(i, None, None)] if k == 1 else [(i, j * rows, rows) for j in range(k)]
    np_ = len(parts)

    def body(*refs):
        x_refs, out_refs = refs[:n], refs[n + na:2 * n + na]
        send_sems, recv_sems, local_sems = refs[2 * n + na:]
        x, y, c = lax.axis_index("x"), lax.axis_index("y"), lax.axis_index("c")
        me, sibling = (x, y, c), (x, y, 1 - c)
        chips = [(1 - x, y), (x, 1 - y), (1 - x, 1 - y)]
        if collective_id is not None:
            _handshake([sibling] + [(*chip, c) for chip in chips])

        def slot(p, px, py, pc):
            i, r0, rows = parts[p]
            whole = out_refs[i].at[4 * px + 2 * py + pc]
            return whole if r0 is None else whole.at[pl.ds(r0, rows)]

        def own(p):
            i, r0, rows = parts[p]
            return x_refs[i] if r0 is None else x_refs[i].at[pl.ds(r0, rows)]

        def copy(p, k, blk, to, src=None):
            return pltpu.make_async_remote_copy(
                src_ref=slot(p, *blk) if src is None else src, dst_ref=slot(p, *blk),
                send_sem=send_sems.at[7 * p + k], recv_sem=recv_sems.at[7 * p + k], device_id=to, device_id_type=MESH)

        mine = [pltpu.make_async_copy(own(p), slot(p, *me), local_sems.at[p]) for p in range(np_)]
        for cp in mine:
            cp.start()
        first = []
        for p in range(np_):
            first.append(copy(p, 0, me, sibling, src=own(p)))
            first += [copy(p, 1 + j, me, (*chip, c), src=own(p)) for j, chip in enumerate(chips)]
        for cp in first:
            cp.start()
        passed = []
        for p in range(np_):
            for j, chip in enumerate(chips):
                copy(p, 1 + j, (*chip, c), me).wait_recv()
                passed.append(copy(p, 4 + j, (*chip, c), sibling))
                passed[-1].start()
        for p in range(np_):
            copy(p, 0, sibling, me).wait_recv()
            for j, chip in enumerate(chips):
                copy(p, 4 + j, (*chip, 1 - c), me).wait_recv()
        for cp in first + passed:
            cp.wait_send()
        for cp in mine:
            cp.wait()

    return _comm_call(
        body, name, list(blocks) + list(after), [jax.ShapeDtypeStruct((N_DEV,) + b.shape, b.dtype) for b in blocks],
        [pltpu.SemaphoreType.DMA((7 * np_,)), pltpu.SemaphoreType.DMA((7 * np_,)), pltpu.SemaphoreType.DMA((np_,))],
        collective_id)


def _sibling_swap(name, by_owner, collective_id=None, after=()):
    n = len(by_owner)
    na = len(after)

    def body(*refs):
        x_refs, out_refs = refs[:n], refs[n + na:2 * n + na]
        send_sems, recv_sems = refs[2 * n + na:]
        x, y, c = lax.axis_index("x"), lax.axis_index("y"), lax.axis_index("c")
        if collective_id is not None:
            _handshake([(x, y, 1 - c)])
        copies = []
        for i in range(n):
            for k in range(4):
                copies.append(pltpu.make_async_remote_copy(
                    src_ref=x_refs[i].at[2 * k + 1 - c], dst_ref=out_refs[i].at[k],
                    send_sem=send_sems.at[4 * i + k], recv_sem=recv_sems.at[4 * i + k],
                    device_id=(x, y, 1 - c), device_id_type=MESH))
        for cp in copies:
            cp.start()
        for cp in copies:
            cp.wait()

    return _comm_call(
        body, name, list(by_owner) + list(after),
        [jax.ShapeDtypeStruct((4,) + b.shape[1:], b.dtype) for b in by_owner],
        [pltpu.SemaphoreType.DMA((4 * n,)), pltpu.SemaphoreType.DMA((4 * n,))], collective_id)


def _chip_exchange(name, sums, collective_id=None, after=()):
    n = len(sums)
    na = len(after)

    def body(*refs):
        x_refs, out_refs = refs[:n], refs[n + na:2 * n + na]
        send_sems, recv_sems, local_sems = refs[2 * n + na:]
        x, y, c = lax.axis_index("x"), lax.axis_index("y"), lax.axis_index("c")
        chips = [(1 - x, y), (x, 1 - y), (1 - x, 1 - y)]
        my_chip = 2 * x + y
        if collective_id is not None:
            _handshake([(cx, cy, c) for cx, cy in chips])
        mine = [pltpu.make_async_copy(x_refs[i].at[my_chip], out_refs[i].at[my_chip], local_sems.at[i])
                for i in range(n)]
        for cp in mine:
            cp.start()
        sends = []
        for i in range(n):
            for j, (cx, cy) in enumerate(chips):
                sends.append(pltpu.make_async_remote_copy(
                    src_ref=x_refs[i].at[2 * cx + cy], dst_ref=out_refs[i].at[my_chip],
                    send_sem=send_sems.at[3 * i + j], recv_sem=recv_sems.at[3 * i + j],
                    device_id=(cx, cy, c), device_id_type=MESH))
        for cp in sends:
            cp.start()
        for i in range(n):
            for j, (cx, cy) in enumerate(chips):
                pltpu.make_async_remote_copy(
                    src_ref=x_refs[i].at[my_chip], dst_ref=out_refs[i].at[2 * cx + cy],
                    send_sem=send_sems.at[3 * i + j], recv_sem=recv_sems.at[3 * i + j],
                    device_id=(cx, cy, c), device_id_type=MESH).wait_recv()
        for cp in sends:
            cp.wait_send()
        for cp in mine:
            cp.wait()

    return _comm_call(
        body, name, list(sums) + list(after), [jax.ShapeDtypeStruct(s.shape, s.dtype) for s in sums],
        [pltpu.SemaphoreType.DMA((3 * n,)), pltpu.SemaphoreType.DMA((3 * n,)), pltpu.SemaphoreType.DMA((n,))],
        collective_id)


def _cast_shards(name, shards):
    n = len(shards)
    steps = min(_row_steps(s.shape[0]) for s in shards)

    def body(*refs):
        for i in range(n):
            refs[n + i][...] = refs[i][...].astype(BF16)

    blocks = [pl.BlockSpec((s.shape[0] // steps, s.shape[1]), lambda i: (i, 0)) for s in shards]
    return pl.pallas_call(
        body, name=name, grid=(steps,), in_specs=blocks, out_specs=blocks,
        out_shape=[jax.ShapeDtypeStruct(s.shape, BF16) for s in shards],
        compiler_params=_params(("parallel",)),
    )(*shards)


BIG = ("w_in", "w_branch_a", "w_branch_b", "w_out", "w_ffn_gate", "w_ffn_up", "w_ffn_down")


def _local_step(x, target, gains, low, conv_w, wg8, reduce):
    g_mix, g_hg, g_ffn, g_fin = gains
    w_in = wg8["w_in"].reshape(N_IN, D_MODEL)
    wg = wg8["w_ffn_gate"].reshape(D_FF, D_MODEL)
    wu = wg8["w_ffn_up"].reshape(D_FF, D_MODEL)
    wa, wb = wg8["w_branch_a"], wg8["w_branch_b"]
    wo = wg8["w_out"].reshape(D_MODEL, D_MODEL)
    wd = wg8["w_ffn_down"].reshape(D_FF, D_MODEL)

    ht, hg, cv, gt, cvo, cvot, cy, o, og, ogt, st = _fwd_in(x, g_mix, w_in, conv_w, low, g_hg)
    x1, mgt, ya, yb = _merge_fwd(og, cvo, gt, x, wa, wb, wo)
    h2t, gate, up, act, loss, d_gfin, dx2, dx2t = _ffn_fwd_loss(x1, g_ffn, wg, wu, wd, target, g_fin)

    dgate, dup, dx1, d_gffn = _ffn_bwd(dx2, x1, gate, up, g_ffn, wg, wu, wd)
    d_wg, d_wu, d_wd = _wgrad_ffn(h2t, dgate, dup, dx2t, act)
    by_owner_ffn = lambda a: a.reshape(N_DEV, D_FF // N_DEV, D_MODEL)
    late = ("w_ffn_gate", "w_ffn_up")
    ffn = dict(w_ffn_gate=by_owner_ffn(d_wg), w_ffn_up=by_owner_ffn(d_wu), w_ffn_down=by_owner_ffn(d_wd))
    got_ffn = reduce.swap(ffn)
    dgt, dya, dyb, dog, dcv, d_conv = _merge_bwd(dx1, ya, yb, gt, cv, cy, wa, wb, wo, conv_w)
    grad_a, grad_b, grad_o, *sums_ffn = _wgrad_out_branches(ogt, dya, cvot, dyb, mgt, dx1, [ffn[n] for n in late],
                                                           got_ffn[:2], reduce.core)
    sums_ffn += _pair_sum("pair_sum_w_ffn_down", [ffn["w_ffn_down"]], got_ffn[2:], reduce.core, after=[grad_o])
    out = dict(w_out=grad_o.reshape(N_DEV, D_MODEL // N_DEV, D_MODEL), w_branch_a=grad_a, w_branch_b=grad_b)
    parts_ffn, _ = reduce.finish({n: ffn[n] for n in late}, sums_ffn[:2], defer=late)
    dhg, d_low, d_ghg = _hg_bwd(dog, hg, o, st, low, g_hg, after=list(sums_ffn))
    sums_out, got_out = reduce.begin(out, sum_after=[dhg])
    parts_out, updated_out = reduce.finish(dict(out, w_ffn_down=ffn["w_ffn_down"]), list(sums_out) + [sums_ffn[2]])
    dparts = [dhg, dcv, dgt]
    d_w_in_t, ridden = _wgrad_in(ht, dparts, after=sums_out[:1],
                                 riders=reduce.riders(late, dict(zip(late, parts_ffn))))
    reduce.record(late, ridden)
    w_in_grad = dict(w_in=d_w_in_t.reshape(N_DEV, N_IN // N_DEV, D_MODEL))
    sums_in, _ = reduce.begin(w_in_grad, after=parts_out[:1], sum_after=updated_out)
    parts_in, _ = reduce.finish(w_in_grad, sums_in)
    grad_x, d_gmix = _in_bwd(dparts, w_in, x, dx1, g_mix, after=list(parts_out[:1]) + list(sums_in))
    small = dict(norm_mix_g=d_gmix, norm_ffn_g=d_gffn, norm_final_g=d_gfin, lower_bounds=d_low, hg_norm_g=d_ghg,
                 conv_w=d_conv, loss=loss)
    return grad_x, small, parts_in


def kernel(x, norm_mix_g, w_in, lower_bounds, hg_norm_g, conv_w, w_branch_a, w_branch_b, w_out, norm_ffn_g, w_ffn_gate, w_ffn_up, w_ffn_down, norm_final_g, loss_target, m_norm_mix_g, m_w_in, m_lower_bounds, m_hg_norm_g, m_conv_w, m_w_branch_a, m_w_branch_b, m_w_out, m_norm_ffn_g, m_w_ffn_gate, m_w_ffn_up, m_w_ffn_down, m_norm_final_g, v_norm_mix_g, v_w_in, v_lower_bounds, v_hg_norm_g, v_conv_w, v_w_branch_a, v_w_branch_b, v_w_out, v_norm_ffn_g, v_w_ffn_gate, v_w_ffn_up, v_w_ffn_down, v_norm_final_g):
    cx, cy, cc = lax.axis_index("x"), lax.axis_index("y"), lax.axis_index("c")
    my_dev = 4 * cx + 2 * cy + cc

    def tr(a):
        return a[0].T

    big = dict(w_in=tr(w_in), w_branch_a=w_branch_a[0], w_branch_b=w_branch_b[0], w_out=w_out[0],
               w_ffn_gate=tr(w_ffn_gate), w_ffn_up=tr(w_ffn_up), w_ffn_down=w_ffn_down[0])
    big_m = dict(w_in=tr(m_w_in), w_branch_a=m_w_branch_a[0], w_branch_b=m_w_branch_b[0], w_out=m_w_out[0],
                 w_ffn_gate=tr(m_w_ffn_gate), w_ffn_up=tr(m_w_ffn_up), w_ffn_down=m_w_ffn_down[0])
    big_v = dict(w_in=tr(v_w_in), w_branch_a=v_w_branch_a[0], w_branch_b=v_w_branch_b[0], w_out=v_w_out[0],
                 w_ffn_gate=tr(v_w_ffn_gate), w_ffn_up=tr(v_w_ffn_up), w_ffn_down=v_w_ffn_down[0])
    transposed = ("w_in", "w_ffn_gate", "w_ffn_up")

    ids = iter(range(1, 16))
    shards = dict(zip(BIG[:1], _cast_shards("cast_w_in", [big["w_in"]])))
    first = _all_gather("gather_w_in", [shards["w_in"], conv_w.transpose(1, 0, 2)], collective_id=next(ids),
                        pieces=[4, 1])
    shards.update(zip(BIG[1:], _cast_shards("cast_shards", [big[n] for n in BIG[1:]])))
    mid = _all_gather("gather_mid", [shards[n] for n in BIG[1:4]], collective_id=next(ids))
    ffn = _all_gather("gather_ffn", [shards[n] for n in BIG[4:]], collective_id=next(ids), pieces=[2, 2, 2])
    wg8 = dict(zip(BIG, [first[0]] + list(mid) + list(ffn)))
    conv_full = first[1].transpose(1, 2, 0, 3).reshape(3, CONV_WIDTH)

    core = cc.reshape(1).astype(jnp.int32)
    outs = {}

    class Reduce:
        core = cc.reshape(1).astype(jnp.int32)

        @staticmethod
        def swap(grads, after=()):
            names = list(grads)
            return _sibling_swap("sibling_swap_" + names[0], [grads[n] for n in names], collective_id=next(ids),
                                 after=after)

        @staticmethod
        def begin(grads, after=(), sum_after=()):
            got = Reduce.swap(grads, after)
            sums = _pair_sum("pair_sum_" + list(grads)[0], list(grads.values()), got, core, after=sum_after)
            return sums, got

        @staticmethod
        def finish(grads, chip_sums, after=(), defer=()):
            names = list(grads)
            parts = _chip_exchange("chip_exchange_" + names[0], chip_sums, collective_id=next(ids), after=after)
            now = [n for n in names if n not in defer]
            if now:
                updated = _adamw_sum("adamw_" + now[0], [big[n] for n in now],
                                     [p for n, p in zip(names, parts) if n in now],
                                     [big_m[n] for n in now], [big_v[n] for n in now])
                outs.update(zip(now, updated))
            return parts, [outs[n][1] for n in now]

        @staticmethod
        def riders(names, parts):
            return [(big[n], parts[n], big_m[n], big_v[n]) for n in names]

        @staticmethod
        def record(names, updated):
            outs.update(zip(names, updated))

    gains = (norm_mix_g, hg_norm_g, norm_ffn_g, norm_final_g.reshape(1, D_MODEL))
    grad_x, small, last = _local_step(x[0], loss_target[0], gains, lower_bounds, conv_full, wg8, Reduce)

    small_all = _all_gather("gather_small", [_small_pack(small)], collective_id=next(ids), after=last[:1])

    def small_state(a):
        return [a[0], a[1], a[2].reshape(1, D_MODEL), a[3], a[4], a[5].transpose(1, 0, 2)]

    upd = _small_update(
        small_all[0], my_dev.reshape(1).astype(jnp.int32),
        small_state((norm_mix_g, norm_ffn_g, norm_final_g, lower_bounds, hg_norm_g, conv_w)),
        small_state((m_norm_mix_g, m_norm_ffn_g, m_norm_final_g, m_lower_bounds, m_hg_norm_g, m_conv_w)),
        small_state((v_norm_mix_g, v_norm_ffn_g, v_norm_final_g, v_lower_bounds, v_hg_norm_g, v_conv_w)))
    loss = upd[0][0, 0]
    for p, (name, _, _, _) in enumerate(_SMALL_PARAMS):
        outs[name] = upd[1 + 4 * p:5 + 4 * p]
    outs["norm_final_g"] = [a.reshape(D_MODEL) for a in outs["norm_final_g"]]
    outs["conv_w"] = [a.transpose(1, 0, 2) for a in outs["conv_w"]]

    order = ["norm_mix_g", "w_in", "lower_bounds", "hg_norm_g", "conv_w", "w_branch_a", "w_branch_b", "w_out",
             "norm_ffn_g", "w_ffn_gate", "w_ffn_up", "w_ffn_down", "norm_final_g"]
    result = [loss, grad_x[None]]
    for k in range(4):
        for n in order:
            if n in BIG:
                result.append((outs[n][k].T if n in transposed else outs[n][k])[None])
            else:
                result.append(outs[n][k])
    return tuple(result)
```

```python
import jax
import jax.numpy as jnp
from jax import lax
from jax.experimental import pallas as pl
from jax.experimental.pallas import tpu as pltpu
from jax.experimental.pallas import tpu_sc as plsc

F32 = jnp.float32
BF16 = jnp.bfloat16
STASH = jnp.bfloat16

D_MODEL = 1024
HG_WIDTH = 512
HEAD_DIM = 128
N_HEADS = 4
HEADS_PER_STEP = 4
HEAD_GROUPS = N_HEADS // HEADS_PER_STEP
CONV_WIDTH = 512
CONV_K = 3
D_FF = 2816
CHUNK = 32
EPS = 1e-6
Q_SCALE = HEAD_DIM ** -0.5
N_DEV = 8

ADAM_LR = 0.001
ADAM_B1 = 0.9
ADAM_B2 = 0.999
ADAM_EPS = 1e-08
ADAM_WD = 0.01
ADAM_STEP = 10

VMEM_LIMIT_V7X = 56 * 1024 * 1024
VMEM_LIMIT_LARGE_V7X = 62 * 1024 * 1024

SMALL_ROWS = 16


def _params(sem, vmem=VMEM_LIMIT_V7X):
    return pltpu.CompilerParams(dimension_semantics=sem, vmem_limit_bytes=vmem)


def _mm(a, b):
    return jnp.dot(a.astype(BF16), b.astype(BF16), preferred_element_type=F32)


def _mm_nt(a, b):
    return lax.dot_general(a.astype(BF16), b.astype(BF16), (((1,), (1,)), ((), ())), preferred_element_type=F32)


def _mm_tn(a, b):
    return lax.dot_general(a.astype(BF16), b.astype(BF16), (((0,), (0,)), ((), ())), preferred_element_type=F32)


def _sigmoid(x):
    return 0.5 * jnp.tanh(0.5 * x) + 0.5


def _resident(shape):
    nd = len(shape)
    return pl.BlockSpec(shape, lambda *_: (0,) * nd, pipeline_mode=pl.Buffered(1))


def _full(shape):
    nd = len(shape)
    return pl.BlockSpec(shape, lambda *_: (0,) * nd)


def _shard_cols(w_ref):
    return jnp.concatenate([w_ref[s] for s in range(N_DEV)], axis=1)


N_HG = 4 * HG_WIDTH
N_CV = 3 * CONV_WIDTH
N_GT = 2 * D_MODEL
N_IN = N_HG + N_CV + N_GT


def _col(tm, n):
    return pl.BlockSpec((n, tm), lambda i: (0, i))


HALO = 8


def _fwd_in(x, g, w_in_t, conv_w, low, gn):
    T = x.shape[0]
    tm = min(512, T)
    nc = tm // CHUNK

    def body(x_ref, g_ref, w_ref, cw_ref, low_ref, gn_ref, ht_ref, hg_ref, cv_ref, gt_ref, cvo_ref, cvot_ref, cy_ref,
             o_ref, og_ref, ogt_ref, st_ref, tail_scr, s_scr):
        @pl.when(pl.program_id(0) == 0)
        def _():
            tail_scr[...] = jnp.zeros_like(tail_scr)
            s_scr[...] = jnp.zeros_like(s_scr)

        xv = x_ref[...]
        r = lax.rsqrt(jnp.mean(xv * xv, axis=-1, keepdims=True) + EPS)
        hf = xv * r * g_ref[...]
        h = hf.astype(BF16)
        ht_ref[...] = hf.T.astype(BF16)
        hg_ref[...] = _mm_nt(h, w_ref[:N_HG, :])
        cv = _mm_nt(h, w_ref[N_HG:N_HG + N_CV, :])
        cv_ref[...] = cv.astype(STASH)
        gt_ref[...] = _mm_nt(h, w_ref[N_HG + N_CV:, :]).astype(STASH)

        u = cv[:, :CONV_WIDTH] * cv[:, 2 * CONV_WIDTH:]
        row = lax.broadcasted_iota(jnp.int32, u.shape, 0)
        prev1 = tail_scr[HALO - 1:HALO, :]
        prev2 = tail_scr[HALO - 2:HALO - 1, :]
        u1 = jnp.where(row >= 1, pltpu.roll(u, 1, 0), prev1)
        u2 = jnp.where(row >= 2, pltpu.roll(u, 2, 0), jnp.where(row == 1, prev1, prev2))
        y = cw_ref[0:1, :] * u2 + cw_ref[1:2, :] * u1 + cw_ref[2:3, :] * u
        cy_ref[...] = y.astype(STASH)
        out = cv[:, CONV_WIDTH:2 * CONV_WIDTH] * y
        cvo_ref[...] = out.astype(BF16)
        cvot_ref[...] = out.T.astype(BF16)
        tail_scr[...] = u[tm - HALO:, :]

        _hg_fwd_tile(hg_ref, low_ref, gn_ref, o_ref, og_ref, ogt_ref, st_ref, s_scr, tm)

    row = lambda n: pl.BlockSpec((tm, n), lambda i: (i, 0))
    return pl.pallas_call(
        body, name="fwd_in", grid=(T // tm,),
        in_specs=[row(D_MODEL), _full((1, D_MODEL)), _resident(w_in_t.shape), _full((CONV_K, CONV_WIDTH)),
                  _full((2, HG_WIDTH)), _full((1, HEAD_DIM))],
        out_specs=[_col(tm, D_MODEL), row(N_HG), row(N_CV), row(N_GT), row(CONV_WIDTH), _col(tm, CONV_WIDTH),
                   row(CONV_WIDTH),
                   row(HG_WIDTH), row(HG_WIDTH), _col(tm, HG_WIDTH),
                   pl.BlockSpec((N_HEADS, nc, HEAD_DIM, HEAD_DIM), lambda i: (0, i, 0, 0))],
        out_shape=[jax.ShapeDtypeStruct((D_MODEL, T), BF16), jax.ShapeDtypeStruct((T, N_HG), F32),
                   jax.ShapeDtypeStruct((T, N_CV), STASH), jax.ShapeDtypeStruct((T, N_GT), STASH),
                   jax.ShapeDtypeStruct((T, CONV_WIDTH), BF16), jax.ShapeDtypeStruct((CONV_WIDTH, T), BF16),
                   jax.ShapeDtypeStruct((T, CONV_WIDTH), STASH), jax.ShapeDtypeStruct((T, HG_WIDTH), F32), jax.ShapeDtypeStruct((T, HG_WIDTH), BF16),
                   jax.ShapeDtypeStruct((HG_WIDTH, T), BF16),
                   jax.ShapeDtypeStruct((N_HEADS, T // CHUNK, HEAD_DIM, HEAD_DIM), F32)],
        scratch_shapes=[pltpu.VMEM((HALO, CONV_WIDTH), F32), pltpu.VMEM((N_HEADS, HEAD_DIM, HEAD_DIM), F32)],
        compiler_params=_params(("arbitrary",), vmem=VMEM_LIMIT_LARGE_V7X),
    )(x, g, w_in_t, conv_w, low, gn)


def _chunk_pos(shape):
    return lax.broadcasted_iota(jnp.int32, shape, 0) & (CHUNK - 1)


def _chunk_cumsum(x, pos):
    s = 1
    while s < CHUNK:
        x = x + jnp.where(pos >= s, pltpu.roll(x, s, 0), 0.0)
        s *= 2
    return x


def _chunk_rev_cumsum(x, pos):
    n = x.shape[0]
    s = 1
    while s < CHUNK:
        x = x + jnp.where(pos + s < CHUNK, pltpu.roll(x, n - s, 0), 0.0)
        s *= 2
    return x


def _lower_bound(low_ref):
    l0 = low_ref[0:1, :]
    l1 = low_ref[1:2, :]
    m = jnp.maximum(l0, l1)
    e0 = jnp.exp(l0 - m)
    e1 = jnp.exp(l1 - m)
    return e0 / (e0 + e1), e1 / (e0 + e1)


def _hg_gates(qr, fr, lb, pos, tb):
    sq = _sigmoid(qr)
    q = qr * sq * Q_SCALE
    sg = _sigmoid(fr)
    f = lb + (1.0 - lb) * sg
    k = 1.0 - f
    b = _chunk_cumsum(jnp.log(f), pos)
    b3 = b.reshape(tb // CHUNK, CHUNK, HEAD_DIM)
    anc = b3[:, CHUNK // 2 - 1:CHUNK // 2, :]
    last = b3[:, CHUNK - 1:CHUNK, :]
    d3 = b3 - anc
    e_qa3 = jnp.exp(d3)
    e_ka3 = jnp.exp(-d3)
    e_b3 = e_qa3 * jnp.exp(anc)
    e_ko3 = e_ka3 * jnp.exp(last - anc)
    dec = jnp.exp(last)
    flat = lambda a: a.reshape(tb, HEAD_DIM)
    return sq, q, sg, f, k, flat(e_qa3), flat(e_ka3), flat(e_b3), flat(e_ko3), dec


def _intra_mask(sb):
    r = lax.broadcasted_iota(jnp.int32, (sb, sb), 0)
    c = lax.broadcasted_iota(jnp.int32, (sb, sb), 1)
    return ((r // CHUNK) == (c // CHUNK)) & (c <= r)


def _hg_fwd_tile(hg_ref, low_ref, gn_ref, o_ref, og_ref, ogt_ref, st_ref, s_scr, tb):
    sb = min(256, tb)
    nc = tb // CHUNK
    q_ref, f_ref, i_ref, g_ref = (hg_ref.at[:, p * HG_WIDTH:(p + 1) * HG_WIDTH] for p in range(4))
    pos = _chunk_pos((tb, HEAD_DIM))
    mask = _intra_mask(sb)
    lanes = [slice(hh * HEAD_DIM, (hh + 1) * HEAD_DIM) for hh in range(N_HEADS)]
    qi, ko, vb, dec, st = [], [], [], [], []
    for hh, ln in enumerate(lanes):
        lb, _ = _lower_bound(low_ref.at[:, ln])
        _, q, _, _, k, e_qa, e_ka, e_b, e_ko, dec_h = _hg_gates(q_ref[:, ln], f_ref[:, ln], lb, pos, tb)
        qh = (q * e_qa).astype(BF16)
        kh = (k * e_ka).astype(BF16)
        qi.append((q * e_b).astype(BF16))
        ko.append((k * e_ko).astype(BF16))
        vb.append(i_ref[:, ln].astype(BF16))
        dec.append(dec_h)
        st.append(s_scr[hh])
        for s in range(tb // sb):
            sl = slice(s * sb, (s + 1) * sb)
            p = jnp.where(mask, _mm_nt(qh[sl], kh[sl]), 0.0)
            o_ref[sl, ln] = _mm(p, vb[hh][sl])
    for c in range(nc):
        sl = slice(c * CHUNK, (c + 1) * CHUNK)
        for hh, ln in enumerate(lanes):
            st_ref[hh, c] = st[hh]
            o_ref[sl, ln] = o_ref[sl, ln] + _mm_nt(qi[hh][sl], st[hh])
            st[hh] = dec[hh][c] * st[hh] + _mm_tn(vb[hh][sl], ko[hh][sl])
    for hh, ln in enumerate(lanes):
        s_scr[hh] = st[hh]
        o = o_ref[:, ln]
        r = lax.rsqrt(jnp.mean(o * o, axis=-1, keepdims=True) + EPS)
        gr = g_ref[:, ln]
        og = (o * r * gn_ref[...]) * (gr * _sigmoid(gr))
        og_ref[:, ln] = og.astype(BF16)
        ogt_ref[ln, :] = og.T.astype(BF16)


def _merge_fwd(og, cvo, gt, x, wa, wb, wo):
    T = x.shape[0]
    tm = min(1024, T)

    def body(og_ref, cvo_ref, gt_ref, x_ref, wa_ref, wb_ref, wo_ref, x1_ref, mgt_ref, ya_ref, yb_ref):
        ya = jnp.dot(og_ref[...], _shard_cols(wa_ref), preferred_element_type=F32)
        yb = jnp.dot(cvo_ref[...], _shard_cols(wb_ref), preferred_element_type=F32)
        ya_ref[...] = ya.astype(STASH)
        yb_ref[...] = yb.astype(STASH)
        m = (_sigmoid(gt_ref[:, :D_MODEL].astype(F32)) * ya
             + _sigmoid(gt_ref[:, D_MODEL:].astype(F32)) * yb)
        mgt_ref[...] = m.T.astype(BF16)
        x1_ref[...] = x_ref[...] + jnp.dot(m.astype(BF16), wo_ref[...], preferred_element_type=F32)

    row = lambda n: pl.BlockSpec((tm, n), lambda i: (i, 0))
    return pl.pallas_call(
        body, name="merge_fwd", grid=(T // tm,),
        in_specs=[row(HG_WIDTH), row(CONV_WIDTH), row(2 * D_MODEL), row(D_MODEL),
                  _resident(wa.shape), _resident(wb.shape), _resident(wo.shape)],
        out_specs=[row(D_MODEL), _col(tm, D_MODEL), row(D_MODEL), row(D_MODEL)],
        out_shape=[jax.ShapeDtypeStruct((T, D_MODEL), F32), jax.ShapeDtypeStruct((D_MODEL, T), BF16),
                   jax.ShapeDtypeStruct((T, D_MODEL), STASH), jax.ShapeDtypeStruct((T, D_MODEL), STASH)],
        compiler_params=_params(("parallel",)),
    )(og, cvo, gt, x, wa, wb, wo)


def _ffn_fwd_loss(x1, g, wg, wu, wd, target, g_fin):
    T = x1.shape[0]
    tm = min(512, T)

    def body(x_ref, g_ref, wg_ref, wu_ref, wd_ref, t_ref, gf_ref,
             ht_ref, gate_ref, up_ref, act_ref, loss_ref, dgf_ref, dx2_ref, dx2t_ref):
        @pl.when(pl.program_id(0) == 0)
        def _():
            loss_ref[...] = jnp.zeros_like(loss_ref)
            dgf_ref[...] = jnp.zeros_like(dgf_ref)

        xv = x_ref[...]
        r = lax.rsqrt(jnp.mean(xv * xv, axis=-1, keepdims=True) + EPS)
        hf = xv * r * g_ref[...]
        h = hf.astype(BF16)
        ht_ref[...] = hf.T.astype(BF16)
        gate = _mm_nt(h, wg_ref[...])
        up = _mm_nt(h, wu_ref[...])
        gate_ref[...] = gate.astype(STASH)
        up_ref[...] = up.astype(STASH)
        act = (gate * _sigmoid(gate) * up).astype(BF16)
        act_ref[...] = act
        x2 = xv + jnp.dot(act, wd_ref[...], preferred_element_type=F32)

        gv = gf_ref[...]
        r2 = lax.rsqrt(jnp.mean(x2 * x2, axis=-1, keepdims=True) + EPS)
        xh = x2 * r2
        err = xh * gv - t_ref[...]
        loss_ref[...] += 0.5 * jnp.sum(jnp.mean(err * err, axis=-1, keepdims=True), axis=0, keepdims=True)
        dy = err * (1.0 / D_MODEL)
        dgf_ref[...] += jnp.sum(dy * xh, axis=0, keepdims=True)
        w = dy * gv
        dx2 = r2 * (w - xh * jnp.mean(w * xh, axis=-1, keepdims=True))
        dx2_ref[...] = dx2
        dx2t_ref[...] = dx2.T.astype(BF16)

    row = lambda n: pl.BlockSpec((tm, n), lambda i: (i, 0))
    return pl.pallas_call(
        body, name="ffn_fwd_loss", grid=(T // tm,),
        in_specs=[row(D_MODEL), _full((1, D_MODEL)), _resident(wg.shape), _resident(wu.shape), _resident(wd.shape),
                  row(D_MODEL), _full((1, D_MODEL))],
        out_specs=[_col(tm, D_MODEL), row(D_FF), row(D_FF), row(D_FF), _full((1, 128)), _full((1, D_MODEL)),
                   row(D_MODEL), _col(tm, D_MODEL)],
        out_shape=[jax.ShapeDtypeStruct((D_MODEL, T), BF16), jax.ShapeDtypeStruct((T, D_FF), STASH),
                   jax.ShapeDtypeStruct((T, D_FF), STASH), jax.ShapeDtypeStruct((T, D_FF), BF16),
                   jax.ShapeDtypeStruct((1, 128), F32), jax.ShapeDtypeStruct((1, D_MODEL), F32),
                   jax.ShapeDtypeStruct((T, D_MODEL), F32), jax.ShapeDtypeStruct((D_MODEL, T), BF16)],
        compiler_params=_params(("arbitrary",), vmem=VMEM_LIMIT_LARGE_V7X),
    )(x1, g, wg, wu, wd, target, g_fin)


def _ffn_bwd(dx2, x1, gate, up, g, wg, wu, wd):
    T = x1.shape[0]
    tm = min(512, T)

    def body(dx2_ref, x_ref, gate_ref, up_ref, g_ref, wg_ref, wu_ref, wd_ref, dgate_ref, dup_ref, dx1_ref, dgn_ref):
        @pl.when(pl.program_id(0) == 0)
        def _():
            dgn_ref[...] = jnp.zeros_like(dgn_ref)

        dx2 = dx2_ref[...]
        dact = _mm_nt(dx2, wd_ref[...])
        gate = gate_ref[...].astype(F32)
        s = _sigmoid(gate)
        dgate = (dact * up_ref[...].astype(F32) * (s * (1.0 + gate * (1.0 - s)))).astype(BF16)
        dup = (dact * (gate * s)).astype(BF16)
        dgate_ref[...] = dgate
        dup_ref[...] = dup
        dh = _mm(dgate, wg_ref[...]) + _mm(dup, wu_ref[...])
        xv = x_ref[...]
        r = lax.rsqrt(jnp.mean(xv * xv, axis=-1, keepdims=True) + EPS)
        xh = xv * r
        dgn_ref[...] += jnp.sum(dh * xh, axis=0, keepdims=True)
        w = dh * g_ref[...]
        dx1_ref[...] = dx2 + r * (w - xh * jnp.mean(w * xh, axis=-1, keepdims=True))

    row = lambda n: pl.BlockSpec((tm, n), lambda i: (i, 0))
    return pl.pallas_call(
        body, name="ffn_bwd", grid=(T // tm,),
        in_specs=[row(D_MODEL), row(D_MODEL), row(D_FF), row(D_FF), _full((1, D_MODEL)),
                  _resident(wg.shape), _resident(wu.shape), _resident(wd.shape)],
        out_specs=[row(D_FF), row(D_FF), row(D_MODEL), _full((1, D_MODEL))],
        out_shape=[jax.ShapeDtypeStruct((T, D_FF), BF16), jax.ShapeDtypeStruct((T, D_FF), BF16),
                   jax.ShapeDtypeStruct((T, D_MODEL), F32), jax.ShapeDtypeStruct((1, D_MODEL), F32)],
        compiler_params=_params(("arbitrary",), vmem=VMEM_LIMIT_LARGE_V7X),
    )(dx2, x1, gate, up, g, wg, wu, wd)


def _merge_bwd(dx1, ya, yb, gt, cv, cy, wa, wb, wo, conv_w):
    T = dx1.shape[0]
    tm = min(512, T)
    nt = T // tm

    def body(dx_hbm, ya_ref, yb_ref, gt_hbm, cv_ref, cy_ref, wa_ref, wb_ref, wo_ref, cw_ref,
             dgt_ref, dya_ref, dyb_ref, dog_ref, dcv_ref, dcw_ref, next_dy, dx_ring, gt_ring, sems):
        step = pl.program_id(0)

        def fetch(s):
            first = (nt - 1 - s) * tm
            rows = pl.ds(first if isinstance(first, int) else pl.multiple_of(first, tm), tm)
            slot = s % 3
            return (pltpu.make_async_copy(dx_hbm.at[rows], dx_ring.at[slot], sems.at[0, slot]),
                    pltpu.make_async_copy(gt_hbm.at[rows], gt_ring.at[slot], sems.at[1, slot]))

        @pl.when(step == 0)
        def _():
            next_dy[...] = jnp.zeros_like(next_dy)
            dcw_ref[...] = jnp.zeros_like(dcw_ref)
            for s in range(min(2, nt)):
                for cp in fetch(s):
                    cp.start()

        @pl.when(step + 2 < nt)
        def _():
            for cp in fetch(step + 2):
                cp.start()

        for cp in fetch(step):
            cp.wait()
        slot = step % 3
        dm = _mm_nt(dx_ring[slot], wo_ref[...])
        wa = _shard_cols(wa_ref)
        wb = _shard_cols(wb_ref)
        ya = ya_ref[...].astype(F32)
        yb = yb_ref[...].astype(F32)
        gates = gt_ring.at[slot]
        sa = _sigmoid(gates[:, :D_MODEL].astype(F32))
        sb = _sigmoid(gates[:, D_MODEL:].astype(F32))
        da = dm * sa
        db = dm * sb
        dgt_ref[:, :D_MODEL] = (da * ya * (1.0 - sa)).astype(BF16)
        dgt_ref[:, D_MODEL:] = (db * yb * (1.0 - sb)).astype(BF16)
        dya = da.astype(BF16)
        dyb = db.astype(BF16)
        dya_ref[...] = dya
        dyb_ref[...] = dyb
        dog_ref[...] = _mm_nt(dya, wa)
        dcvo = _mm_nt(dyb, wb)

        cvt = cv_ref[...].astype(F32)
        c, bg, xb = cvt[:, :CONV_WIDTH], cvt[:, CONV_WIDTH:2 * CONV_WIDTH], cvt[:, 2 * CONV_WIDTH:]
        u = c * xb
        row = lax.broadcasted_iota(jnp.int32, u.shape, 0)
        w0, w1, w2 = cw_ref[0:1, :], cw_ref[1:2, :], cw_ref[2:3, :]
        dcv_ref[:, CONV_WIDTH:2 * CONV_WIDTH] = (dcvo * cy_ref[...].astype(F32)).astype(BF16)
        dy = dcvo * bg
        n1 = next_dy[0:1, :]
        n2 = next_dy[1:2, :]
        dy1 = jnp.where(row < tm - 1, pltpu.roll(dy, tm - 1, 0), n1)
        dy2 = jnp.where(row < tm - 2, pltpu.roll(dy, tm - 2, 0), jnp.where(row == tm - 2, n1, n2))
        dcw_ref[0:1, :] += jnp.sum(dy2 * u, axis=0, keepdims=True)
        dcw_ref[1:2, :] += jnp.sum(dy1 * u, axis=0, keepdims=True)
        dcw_ref[2:3, :] += jnp.sum(dy * u, axis=0, keepdims=True)
        du = w2 * dy + w1 * dy1 + w0 * dy2
        dcv_ref[:, :CONV_WIDTH] = (du * xb).astype(BF16)
        dcv_ref[:, 2 * CONV_WIDTH:] = (du * c).astype(BF16)
        next_dy[...] = dy[:HALO, :]

    rt = lambda i: nt - 1 - i
    row = lambda n: pl.BlockSpec((tm, n), lambda i: (rt(i), 0))
    return pl.pallas_call(
        body, name="merge_bwd", grid=(nt,),
        in_specs=[HBM_SPEC, row(D_MODEL), row(D_MODEL), HBM_SPEC, row(N_CV), row(CONV_WIDTH),
                  _resident(wa.shape), _resident(wb.shape), _resident(wo.shape), _full((CONV_K, CONV_WIDTH))],
        out_specs=[row(2 * D_MODEL), row(D_MODEL), row(D_MODEL), row(HG_WIDTH), row(N_CV),
                   _full((CONV_K, CONV_WIDTH))],
        out_shape=[jax.ShapeDtypeStruct((T, 2 * D_MODEL), BF16), jax.ShapeDtypeStruct((T, D_MODEL), BF16),
                   jax.ShapeDtypeStruct((T, D_MODEL), BF16), jax.ShapeDtypeStruct((T, HG_WIDTH), F32),
                   jax.ShapeDtypeStruct((T, N_CV), BF16), jax.ShapeDtypeStruct((CONV_K, CONV_WIDTH), F32)],
        scratch_shapes=[pltpu.VMEM((HALO, CONV_WIDTH), F32), pltpu.VMEM((3, tm, D_MODEL), dx1.dtype),
                        pltpu.VMEM((3, tm, 2 * D_MODEL), gt.dtype), pltpu.SemaphoreType.DMA((2, 3))],
        compiler_params=_params(("arbitrary",)),
    )(dx1, ya, yb, gt, cv, cy, wa, wb, wo, conv_w)


def _drop_operands(body, first, count):
    def wrapped(*refs):
        return body(*refs[:first], *refs[first + count:])
    return wrapped


def _hg_bwd(dog, hg, o, st, low, gn, after=()):
    T = hg.shape[0]
    tb = min(512, T)
    sb = min(256, tb)
    nb = T // tb
    nc = tb // CHUNK
    wid = HEADS_PER_STEP * HEAD_DIM

    def body(q_ref, f_ref, i_ref, g_ref, low_ref, gn_ref, o_ref, dog_ref, st_ref,
             dhg_ref, dlow_ref, dgn_ref,
             ds_scr, dqi_scr, dko_scr, dv_scr, dd_scr, dqh_scr, dkh_scr):
        h = pl.program_id(0)
        t = pl.program_id(1)
        dq_ref, df_ref, di_ref, dg_ref = (dhg_ref.at[:, p * HG_WIDTH:(p + 1) * HG_WIDTH] for p in range(4))

        @pl.when(t == 0)
        def _():
            ds_scr[...] = jnp.zeros_like(ds_scr)
            dlow_ref[...] = jnp.zeros_like(dlow_ref)

        @pl.when((t == 0) & (h == 0))
        def _():
            dgn_ref[...] = jnp.zeros_like(dgn_ref)

        pos = _chunk_pos((tb, HEAD_DIM))
        mask = _intra_mask(sb)
        gnv = gn_ref[...]
        lanes = [slice(hh * HEAD_DIM, (hh + 1) * HEAD_DIM) for hh in range(HEADS_PER_STEP)]
        heads = []
        for hh, ln in enumerate(lanes):
            lb, lb1 = _lower_bound(low_ref.at[:, ln])
            qr = q_ref[:, ln]
            sq, q, sg, f, k, e_qa, e_ka, e_b, e_ko, dec = _hg_gates(qr, f_ref[:, ln], lb, pos, tb)

            gr = g_ref[:, ln]
            o = o_ref[:, ln]
            dog_v = dog_ref[:, ln]
            sgr = _sigmoid(gr)
            r = lax.rsqrt(jnp.mean(o * o, axis=-1, keepdims=True) + EPS)
            oh = o * r
            dg_ref[:, ln] = (dog_v * (oh * gnv) * (sgr * (1.0 + gr * (1.0 - sgr)))).astype(BF16)
            don = dog_v * (gr * sgr)
            dgn_ref[...] += jnp.sum(don * oh, axis=0, keepdims=True)
            w = don * gnv
            do = (r * (w - oh * jnp.mean(w * oh, axis=-1, keepdims=True))).astype(BF16)

            qh = (q * e_qa).astype(BF16)
            kh = (k * e_ka).astype(BF16)
            qi = (q * e_b).astype(BF16)
            ko = (k * e_ko).astype(BF16)
            vb = i_ref[:, ln].astype(BF16)

            for s in range(tb // sb):
                sl = slice(s * sb, (s + 1) * sb)
                p = jnp.where(mask, _mm_nt(qh[sl], kh[sl]), 0.0).astype(BF16)
                dp = jnp.where(mask, _mm_nt(do[sl], vb[sl]), 0.0).astype(BF16)
                dv_scr[sl, ln] = _mm_tn(p, do[sl])
                dqh_scr[sl, ln] = _mm(dp, kh[sl])
                dkh_scr[sl, ln] = _mm_tn(dp, qh[sl])
            heads.append(dict(lb=lb, lb1=lb1, qr=qr, sq=sq, q=q, sg=sg, f=f, k=k, e_qa=e_qa, e_ka=e_ka, e_b=e_b,
                              e_ko=e_ko, dec=dec, do=do, qi=qi, ko=ko, vb=vb, ds=ds_scr[hh]))

        for c in reversed(range(nc)):
            sl = slice(c * CHUNK, (c + 1) * CHUNK)
            for hh, ln in enumerate(lanes):
                hd = heads[hh]
                ds = hd["ds"]
                st_c = st_ref[hh, c]
                dqi_scr[sl, ln] = _mm(hd["do"][sl], st_c)
                dko_scr[sl, ln] = _mm(hd["vb"][sl], ds)
                dv_scr[sl, ln] = dv_scr[sl, ln] + _mm_nt(hd["ko"][sl], ds)
                dec_c = hd["dec"][c]
                dd_scr[sl, ln] = jnp.broadcast_to(dec_c * jnp.sum(ds * st_c, axis=0, keepdims=True),
                                                  (CHUNK, HEAD_DIM))
                hd["ds"] = dec_c * ds + _mm_tn(hd["do"][sl], hd["qi"][sl])

        for hh, ln in enumerate(lanes):
            hd = heads[hh]
            ds_scr[hh] = hd["ds"]
            q, k, lb = hd["q"], hd["k"], hd["lb"]
            dko_e = dko_scr[:, ln] * hd["e_ko"]
            dq = dqh_scr[:, ln] * hd["e_qa"] + dqi_scr[:, ln] * hd["e_b"]
            dk = dkh_scr[:, ln] * hd["e_ka"] + dko_e
            kd3 = (k * dko_e).reshape(nc, CHUNK, HEAD_DIM)
            last = jnp.broadcast_to(jnp.sum(kd3, axis=1, keepdims=True), kd3.shape).reshape(tb, HEAD_DIM)
            db = q * dq - k * dk + jnp.where(pos == CHUNK - 1, dd_scr[:, ln] + last, 0.0)
            dlg = _chunk_rev_cumsum(db, pos)
            dfv = dlg / hd["f"] - dk
            s_low = jnp.sum(dfv * (1.0 - hd["sg"]), axis=0, keepdims=True)
            dlow_ref[0:1, ln] += s_low * lb * (1.0 - lb)
            dlow_ref[1:2, ln] += -s_low * lb * hd["lb1"]
            df_ref[:, ln] = (dfv * (1.0 - lb) * hd["sg"] * (1.0 - hd["sg"])).astype(BF16)
            dq_ref[:, ln] = (dq * Q_SCALE * (hd["sq"] * (1.0 + hd["qr"] * (1.0 - hd["sq"])))).astype(BF16)
            di_ref[:, ln] = dv_scr[:, ln].astype(BF16)

    rt = lambda t: nb - 1 - t
    col = lambda p: pl.BlockSpec((tb, wid), lambda h, t: (rt(t), p * HEAD_GROUPS + h))
    hcol = pl.BlockSpec((tb, wid), lambda h, t: (rt(t), h))
    assert HEAD_GROUPS == 1
    tile = pltpu.VMEM((tb, wid), F32)
    return pl.pallas_call(
        _drop_operands(body, 9, len(after)), name="hg_bwd", grid=(HEAD_GROUPS, nb),
        in_specs=[col(0), col(1), col(2), col(3), pl.BlockSpec((2, wid), lambda h, t: (0, h)),
                  pl.BlockSpec((1, HEAD_DIM), lambda h, t: (0, 0)), hcol, hcol,
                  pl.BlockSpec((HEADS_PER_STEP, nc, HEAD_DIM, HEAD_DIM), lambda h, t: (h, rt(t), 0, 0))]
                 + [HBM_SPEC] * len(after),
        out_specs=[pl.BlockSpec((tb, N_HG), lambda h, t: (rt(t), 0)), pl.BlockSpec((2, wid), lambda h, t: (0, h)),
                   pl.BlockSpec((1, HEAD_DIM), lambda h, t: (0, 0))],
        out_shape=[jax.ShapeDtypeStruct((T, N_HG), BF16), jax.ShapeDtypeStruct((2, HG_WIDTH), F32),
                   jax.ShapeDtypeStruct((1, HEAD_DIM), F32)],
        scratch_shapes=[pltpu.VMEM((HEADS_PER_STEP, HEAD_DIM, HEAD_DIM), F32), tile, tile, tile, tile, tile, tile],
        compiler_params=_params(("arbitrary", "arbitrary")),
    )(hg, hg, hg, hg, low, gn, o, dog, st, *after)


def _in_bwd(dparts, w_in, x, dx1, g, after=()):
    T = x.shape[0]
    tm = min(512, T)
    widths = [p.shape[1] for p in dparts]
    offs = [sum(widths[:i]) for i in range(len(widths))]
    n = len(dparts)

    def body(*refs):
        d_refs = refs[:n]
        w_ref, x_ref, dx1_ref, g_ref, dx_ref, dgn_ref = refs[n:]

        @pl.when(pl.program_id(0) == 0)
        def _():
            dgn_ref[...] = jnp.zeros_like(dgn_ref)

        dh = None
        for d_ref, off, wd in zip(d_refs, offs, widths):
            part = _mm(d_ref[...], w_ref[off:off + wd, :])
            dh = part if dh is None else dh + part
        xv = x_ref[...]
        r = lax.rsqrt(jnp.mean(xv * xv, axis=-1, keepdims=True) + EPS)
        xh = xv * r
        dgn_ref[...] += jnp.sum(dh * xh, axis=0, keepdims=True)
        w = dh * g_ref[...]
        dx_ref[...] = dx1_ref[...] + r * (w - xh * jnp.mean(w * xh, axis=-1, keepdims=True))

    row = lambda m: pl.BlockSpec((tm, m), lambda i: (i, 0))
    return pl.pallas_call(
        _drop_operands(body, n + 4, len(after)), name="in_bwd", grid=(T // tm,),
        in_specs=[row(wd) for wd in widths] + [_resident(w_in.shape), row(D_MODEL), row(D_MODEL), _full((1, D_MODEL))]
                 + [HBM_SPEC] * len(after),
        out_specs=[row(D_MODEL), _full((1, D_MODEL))],
        out_shape=[jax.ShapeDtypeStruct((T, D_MODEL), F32), jax.ShapeDtypeStruct((1, D_MODEL), F32)],
        compiler_params=_params(("arbitrary",)),
    )(*dparts, w_in, x, dx1, g, *after)


def _wgrad_ffn(h2t, dgate, dup, dx2t, act, tn=256):
    M, T = h2t.shape
    N = dgate.shape[1]

    def body(h_ref, x_ref, dg_ref, du_ref, act_ref, og_ref, ou_ref, od_ref):
        h = h_ref[...]
        og_ref[...] = _mm(h, dg_ref[...]).T.astype(BF16)
        ou_ref[...] = _mm(h, du_ref[...]).T.astype(BF16)
        od_ref[...] = _mm(x_ref[...], act_ref[...]).T.astype(BF16)

    rhs = pl.BlockSpec((T, tn), lambda j: (0, j))
    out_spec = pl.BlockSpec((tn, M), lambda j: (j, 0))
    out = jax.ShapeDtypeStruct((N, M), BF16)
    return pl.pallas_call(
        body, name="wgrad_ffn", grid=(N // tn,),
        in_specs=[_resident((M, T)), _resident((M, T)), rhs, rhs, rhs], out_specs=[out_spec] * 3,
        out_shape=[out, out, out],
        compiler_params=_params(("parallel",)),
    )(h2t, dx2t, dgate, dup, act)


def _wgrad_out_branches(ogt, dya, cvot, dyb, mgt, dx1, by_owner, got, core):
    M, T = ogt.shape
    N = dya.shape[1]
    c = N // N_DEV
    per = 2
    tn = per * c
    n = len(got)
    assert N // tn == got[0].shape[0]

    def body(core_ref, at_ref, da_ref, bt_ref, db_ref, mt_ref, dx_ref, *refs):
        mine, theirs = refs[:n], refs[n:2 * n]
        oa_ref, ob_ref, oo_ref = refs[2 * n:2 * n + 3]
        ga = _mm(at_ref[...], da_ref[...])
        gb = _mm(bt_ref[...], db_ref[...])
        for s in range(per):
            oa_ref[s] = ga[:, s * c:(s + 1) * c].astype(BF16)
            ob_ref[s] = gb[:, s * c:(s + 1) * c].astype(BF16)
        oo_ref[...] = _mm(mt_ref[...], dx_ref[...]).astype(BF16)
        for a_ref, b_ref, o_ref in zip(mine, theirs, refs[2 * n + 3:]):
            o_ref[...] = (a_ref[...].astype(F32) + b_ref[...].astype(F32)).astype(BF16)

    def blk(g):
        return pl.BlockSpec((None,) + g.shape[1:], lambda j, core_ref: (j, 0, 0))

    def own(g):
        return pl.BlockSpec((None,) + g.shape[1:], lambda j, core_ref: (2 * j + core_ref[0], 0, 0))

    rhs = pl.BlockSpec((T, tn), lambda j, core_ref: (0, j))
    owners = pl.BlockSpec((per, M, c), lambda j, core_ref: (j, 0, 0))
    out = jax.ShapeDtypeStruct((N_DEV, M, c), BF16)
    return pl.pallas_call(
        body, name="wgrad_out_branches",
        grid_spec=pltpu.PrefetchScalarGridSpec(
            num_scalar_prefetch=1, grid=(N // tn,),
            in_specs=[_resident((M, T)), rhs, _resident((M, T)), rhs, _resident(mgt.shape), rhs]
            + [own(g) for g in by_owner] + [blk(g) for g in got],
            out_specs=[owners, owners, pl.BlockSpec((mgt.shape[0], tn), lambda j, core_ref: (0, j))]
            + [blk(g) for g in got]),
        out_shape=[out, out, jax.ShapeDtypeStruct((mgt.shape[0], dx1.shape[1]), BF16)]
        + [jax.ShapeDtypeStruct(g.shape, BF16) for g in got],
        compiler_params=_params(("parallel",)),
    )(core, ogt, dya, cvot, dyb, mgt, dx1, *by_owner, *got)


def _wgrad_in(ht, dparts, after=(), riders=()):
    M, T = ht.shape
    tn = 512
    nblk = [p.shape[1] // tn for p in dparts]
    start = [sum(nblk[:i]) for i in range(len(nblk))]
    n = len(dparts)
    steps = sum(nblk)
    nr = len(riders)
    rows = [r[0].shape[0] // steps for r in riders]
    assert all(r[0].shape[0] == rr * steps and rr % 16 == 0 for r, rr in zip(riders, rows))

    def body(a_ref, *refs):
        d_refs = refs[:n]
        rider_in = refs[n:n + 4 * nr]
        o_ref = refs[n + 4 * nr]
        rider_out = refs[n + 4 * nr + 1:]
        j = pl.program_id(0)
        for d_ref, s, nb in zip(d_refs, start, nblk):
            @pl.when((j >= s) & (j < s + nb))
            def _():
                o_ref[...] = _mm(a_ref[...], d_ref[...]).T.astype(BF16)
        for i in range(nr):
            w_ref, p_ref, m_ref, v_ref = rider_in[4 * i:4 * i + 4]
            g = p_ref[0].astype(F32)
            for k in range(1, 4):
                g = g + p_ref[k].astype(F32)
            delta, m_new, v_new = _adamw_math(w_ref[...], g, m_ref[...], v_ref[...])
            for k, val in enumerate((g, delta, m_new, v_new)):
                rider_out[4 * i + k][...] = val

    def piece_spec(s, nb):
        return pl.BlockSpec((T, tn), lambda j: (0, jnp.clip(j - s, 0, nb - 1)))

    rider_specs, rider_out_specs, rider_out_shape, rider_args = [], [], [], []
    for (w, parts, m, v), rr in zip(riders, rows):
        blk = pl.BlockSpec((rr, w.shape[1]), lambda j: (j, 0))
        rider_specs += [blk, pl.BlockSpec((4, rr, w.shape[1]), lambda j: (0, j, 0)), blk, blk]
        rider_out_specs += [blk] * 4
        rider_out_shape += [jax.ShapeDtypeStruct(w.shape, F32)] * 4
        rider_args += [w, parts, m, v]
    outs = pl.pallas_call(
        _drop_operands(body, 1 + n + 4 * nr, len(after)), name="wgrad_in", grid=(steps,),
        in_specs=[_resident((M, T))] + [piece_spec(s, nb) for s, nb in zip(start, nblk)] + rider_specs
                 + [HBM_SPEC] * len(after),
        out_specs=[pl.BlockSpec((tn, M), lambda j: (j, 0))] + rider_out_specs,
        out_shape=[jax.ShapeDtypeStruct((steps * tn, M), BF16)] + rider_out_shape,
        compiler_params=_params(("parallel",)),
    )(ht, *dparts, *rider_args, *after)
    return outs[0], [outs[1 + 4 * i:5 + 4 * i] for i in range(nr)]


def _adamw_math(w, g, m, v):
    m = ADAM_B1 * m + (1.0 - ADAM_B1) * g
    v = ADAM_B2 * v + (1.0 - ADAM_B2) * (g * g)
    m_hat = m / (1.0 - ADAM_B1 ** ADAM_STEP)
    v_hat = v / (1.0 - ADAM_B2 ** ADAM_STEP)
    delta = -ADAM_LR * (m_hat / (jnp.sqrt(v_hat) + ADAM_EPS) + ADAM_WD * w)
    return delta, m, v


def _adamw_sum(name, ws, parts, ms, vs):
    n = len(ws)
    steps = min(_row_steps(w.shape[0]) for w in ws)
    rows = [w.shape[0] // steps for w in ws]

    def body(*refs):
        w_refs, p_refs, m_refs, v_refs = (refs[k * n:(k + 1) * n] for k in range(4))
        out_refs = refs[4 * n:]
        for i in range(n):
            g = p_refs[i][0].astype(F32)
            for k in range(1, 4):
                g = g + p_refs[i][k].astype(F32)
            delta, m_new, v_new = _adamw_math(w_refs[i][...], g, m_refs[i][...], v_refs[i][...])
            for k, val in enumerate((g, delta, m_new, v_new)):
                out_refs[4 * i + k][...] = val

    blk = [pl.BlockSpec((r, w.shape[1]), lambda s: (s, 0)) for r, w in zip(rows, ws)]
    pblk = [pl.BlockSpec((4, r, w.shape[1]), lambda s: (0, s, 0)) for r, w in zip(rows, ws)]
    out_specs, out_shape = [], []
    for b, w in zip(blk, ws):
        out_specs += [b] * 4
        out_shape += [jax.ShapeDtypeStruct(w.shape, F32)] * 4
    flat = pl.pallas_call(
        body, name=name, grid=(steps,),
        in_specs=blk + pblk + blk + blk, out_specs=out_specs, out_shape=out_shape,
        compiler_params=_params(("parallel",)),
    )(*ws, *parts, *ms, *vs)
    return [flat[4 * i:4 * i + 4] for i in range(n)]


_SMALL_SLOTS = (("norm_mix_g", 0, 1, 1024), ("norm_ffn_g", 1, 1, 1024), ("norm_final_g", 2, 1, 1024),
                ("lower_bounds", 3, 2, 512), ("hg_norm_g", 5, 1, 128), ("loss", 6, 1, 128), ("conv_w", 8, 3, 512))
_SMALL_PARAMS = tuple(s for s in _SMALL_SLOTS if s[0] != "loss")
CONV_SHARD = CONV_WIDTH // N_DEV


def _small_pack(small):
    def body(*refs):
        out = refs[-1]
        out[...] = jnp.zeros_like(out)
        for ref, (_, row, rows, lanes) in zip(refs[:-1], _SMALL_SLOTS):
            out[row:row + rows, 0:lanes] = ref[...]

    vmem = pl.BlockSpec(memory_space=pltpu.VMEM)
    return pl.pallas_call(
        body, name="small_pack", in_specs=[vmem] * len(_SMALL_SLOTS), out_specs=vmem,
        out_shape=jax.ShapeDtypeStruct((SMALL_ROWS, 1024), F32),
    )(*[small[name] for name, _, _, _ in _SMALL_SLOTS])


def _small_update(gathered, dev, w, m, v):
    n = len(_SMALL_PARAMS)

    def body(dev_ref, g_ref, *refs):
        w_refs, m_refs, v_refs = refs[:n], refs[n:2 * n], refs[2 * n:3 * n]
        loss_ref, out_refs, sum_scr = refs[3 * n], refs[3 * n + 1:-1], refs[-1]
        total = g_ref[0]
        for k in range(1, N_DEV):
            total = total + g_ref[k]
        sum_scr[...] = total
        loss_ref[...] = sum_scr[6:7, 0:128]
        for p, (name, row, rows, lanes) in enumerate(_SMALL_PARAMS):
            if name == "conv_w":
                for r in range(rows):
                    g = sum_scr[row + r:row + r + 1, 0:CONV_SHARD]
                    for s in range(1, N_DEV):
                        mine = sum_scr[row + r:row + r + 1, s * CONV_SHARD:(s + 1) * CONV_SHARD]
                        g = jnp.where(dev_ref[0] == s, mine, g)
                    delta, m_new, v_new = _adamw_math(w_refs[p][r], g, m_refs[p][r], v_refs[p][r])
                    out_refs[4 * p][r] = g
                    out_refs[4 * p + 1][r] = delta
                    out_refs[4 * p + 2][r] = m_new
                    out_refs[4 * p + 3][r] = v_new
                continue
            g = sum_scr[row:row + rows, 0:lanes]
            delta, m_new, v_new = _adamw_math(w_refs[p][...], g, m_refs[p][...], v_refs[p][...])
            out_refs[4 * p][...] = g
            out_refs[4 * p + 1][...] = delta
            out_refs[4 * p + 2][...] = m_new
            out_refs[4 * p + 3][...] = v_new

    vmem = pl.BlockSpec(memory_space=pltpu.VMEM)
    outs = [jax.ShapeDtypeStruct((1, 128), F32)]
    for a in w:
        outs += [jax.ShapeDtypeStruct(a.shape, F32)] * 4
    return pl.pallas_call(
        body, name="small_update",
        in_specs=[pl.BlockSpec(memory_space=pltpu.SMEM)] + [vmem] * (1 + 3 * n), out_specs=[vmem] * len(outs),
        out_shape=outs, scratch_shapes=[pltpu.VMEM((SMALL_ROWS, 1024), F32)],
    )(dev, gathered, *w, *m, *v)


def _row_steps(rows):
    for steps in (4, 2):
        if rows % (16 * steps) == 0:
            return steps
    return 1


def _pair_sum(name, by_owner, got, core, after=()):
    n = len(got)

    def body(core_ref, *refs):
        for a_ref, b_ref, o_ref in zip(refs[:n], refs[n:2 * n], refs[2 * n:]):
            o_ref[...] = (a_ref[...].astype(F32) + b_ref[...].astype(F32)).astype(BF16)

    def blk(g):
        return pl.BlockSpec((None,) + g.shape[1:], lambda k, core_ref: (k, 0, 0))

    def mine(g):
        return pl.BlockSpec((None,) + g.shape[1:], lambda k, core_ref: (2 * k + core_ref[0], 0, 0))

    return pl.pallas_call(
        _drop_operands(body, 1 + 2 * n, len(after)), name=name,
        grid_spec=pltpu.PrefetchScalarGridSpec(
            num_scalar_prefetch=1, grid=(4,),
            in_specs=[mine(g) for g in got] + [blk(g) for g in got] + [HBM_SPEC] * len(after),
            out_specs=[blk(g) for g in got]),
        out_shape=[jax.ShapeDtypeStruct(g.shape, BF16) for g in got],
        compiler_params=_params(("parallel",)),
    )(core, *by_owner, *got, *after)


MESH = pl.DeviceIdType.MESH
HBM_SPEC = pl.BlockSpec(memory_space=pl.ANY)


def _handshake(peers):
    barrier = pltpu.get_barrier_semaphore()
    for peer in peers:
        pl.semaphore_signal(barrier, inc=1, device_id=peer, device_id_type=MESH)
    pl.semaphore_wait(barrier, len(peers))


def _comm_call(body, name, operands, out_shape, scratch, collective_id):
    if collective_id is None:
        return pl.pallas_call(body, name=name, in_specs=[HBM_SPEC] * len(operands), out_specs=[HBM_SPEC] * len(out_shape),
                              out_shape=out_shape, scratch_shapes=scratch)(*operands)
    return pl.kernel(body, out_type=out_shape, mesh=plsc.ScalarSubcoreMesh(axis_name="sequencer", num_cores=1),
                     scratch_types=scratch, name=name,
                     compiler_params=pltpu.CompilerParams(collective_id=collective_id))(*operands)


def _all_gather(name, blocks, collective_id=None, after=(), pieces=None):
    n = len(blocks)
    na = len(after)
    pieces = pieces or [1] * n
    parts = []
    for i, (b, k) in enumerate(zip(blocks, pieces)):
        rows, rem = divmod(b.shape[0], k)
        assert rem == 0 and (k == 1 or rows % 16 == 0), (b.shape, k)
        parts += [(i, None, None)] if k == 1 else [(i, j * rows, rows) for j in range(k)]
    np_ = len(parts)

    def body(*refs):
        x_refs, out_refs = refs[:n], refs[n + na:2 * n + na]
        send_sems, recv_sems, local_sems = refs[2 * n + na:]
        x, y, c = lax.axis_index("x"), lax.axis_index("y"), lax.axis_index("c")
        me, sibling = (x, y, c), (x, y, 1 - c)
        chips = [(1 - x, y), (x, 1 - y), (1 - x, 1 - y)]
        if collective_id is not None:
            _handshake([sibling] + [(*chip, c) for chip in chips])

        def slot(p, px, py, pc):
            i, r0, rows = parts[p]
            whole = out_refs[i].at[4 * px + 2 * py + pc]
            return whole if r0 is None else whole.at[pl.ds(r0, rows)]

        def own(p):
            i, r0, rows = parts[p]
            return x_refs[i] if r0 is None else x_refs[i].at[pl.ds(r0, rows)]

        def copy(p, k, blk, to, src=None):
            return pltpu.make_async_remote_copy(
                src_ref=slot(p, *blk) if src is None else src, dst_ref=slot(p, *blk),
                send_sem=send_sems.at[7 * p + k], recv_sem=recv_sems.at[7 * p + k], device_id=to, device_id_type=MESH)

        mine = [pltpu.make_async_copy(own(p), slot(p, *me), local_sems.at[p]) for p in range(np_)]
        for cp in mine:
            cp.start()
        first = []
        for p in range(np_):
            first.append(copy(p, 0, me, sibling, src=own(p)))
            first += [copy(p, 1 + j, me, (*chip, c), src=own(p)) for j, chip in enumerate(chips)]
        for cp in first:
            cp.start()
        passed = []
        for p in range(np_):
            for j, chip in enumerate(chips):
                copy(p, 1 + j, (*chip, c), me).wait_recv()
                passed.append(copy(p, 4 + j, (*chip, c), sibling))
                passed[-1].start()
        for p in range(np_):
            copy(p, 0, sibling, me).wait_recv()
            for j, chip in enumerate(chips):
                copy(p, 4 + j, (*chip, 1 - c), me).wait_recv()
        for cp in first + passed:
            cp.wait_send()
        for cp in mine:
            cp.wait()

    return _comm_call(
        body, name, list(blocks) + list(after), [jax.ShapeDtypeStruct((N_DEV,) + b.shape, b.dtype) for b in blocks],
        [pltpu.SemaphoreType.DMA((7 * np_,)), pltpu.SemaphoreType.DMA((7 * np_,)), pltpu.SemaphoreType.DMA((np_,))],
        collective_id)


def _sibling_swap(name, by_owner, collective_id=None, after=()):
    n = len(by_owner)
    na = len(after)

    def body(*refs):
        x_refs, out_refs = refs[:n], refs[n + na:2 * n + na]
        send_sems, recv_sems = refs[2 * n + na:]
        x, y, c = lax.axis_index("x"), lax.axis_index("y"), lax.axis_index("c")
        if collective_id is not None:
            _handshake([(x, y, 1 - c)])
        copies = []
        for i in range(n):
            for k in range(4):
                copies.append(pltpu.make_async_remote_copy(
                    src_ref=x_refs[i].at[2 * k + 1 - c], dst_ref=out_refs[i].at[k],
                    send_sem=send_sems.at[4 * i + k], recv_sem=recv_sems.at[4 * i + k],
                    device_id=(x, y, 1 - c), device_id_type=MESH))
        for cp in copies:
            cp.start()
        for cp in copies:
            cp.wait()

    return _comm_call(
        body, name, list(by_owner) + list(after),
        [jax.ShapeDtypeStruct((4,) + b.shape[1:], b.dtype) for b in by_owner],
        [pltpu.SemaphoreType.DMA((4 * n,)), pltpu.SemaphoreType.DMA((4 * n,))], collective_id)


def _chip_exchange(name, sums, collective_id=None, after=()):
    n = len(sums)
    na = len(after)

    def body(*refs):
        x_refs, out_refs = refs[:n], refs[n + na:2 * n + na]
        send_sems, recv_sems, local_sems = refs[2 * n + na:]
        x, y, c = lax.axis_index("x"), lax.axis_index("y"), lax.axis_index("c")
        chips = [(1 - x, y), (x, 1 - y), (1 - x, 1 - y)]
        my_chip = 2 * x + y
        if collective_id is not None:
            _handshake([(cx, cy, c) for cx, cy in chips])
        mine = [pltpu.make_async_copy(x_refs[i].at[my_chip], out_refs[i].at[my_chip], local_sems.at[i])
                for i in range(n)]
        for cp in mine:
            cp.start()
        sends = []
        for i in range(n):
            for j, (cx, cy) in enumerate(chips):
                sends.append(pltpu.make_async_remote_copy(
                    src_ref=x_refs[i].at[2 * cx + cy], dst_ref=out_refs[i].at[my_chip],
                    send_sem=send_sems.at[3 * i + j], recv_sem=recv_sems.at[3 * i + j],
                    device_id=(cx, cy, c), device_id_type=MESH))
        for cp in sends:
            cp.start()
        for i in range(n):
            for j, (cx, cy) in enumerate(chips):
                pltpu.make_async_remote_copy(
                    src_ref=x_refs[i].at[my_chip], dst_ref=out_refs[i].at[2 * cx + cy],
                    send_sem=send_sems.at[3 * i + j], recv_sem=recv_sems.at[3 * i + j],
                    device_id=(cx, cy, c), device_id_type=MESH).wait_recv()
        for cp in sends:
            cp.wait_send()
        for cp in mine:
            cp.wait()

    return _comm_call(
        body, name, list(sums) + list(after), [jax.ShapeDtypeStruct(s.shape, s.dtype) for s in sums],
        [pltpu.SemaphoreType.DMA((3 * n,)), pltpu.SemaphoreType.DMA((3 * n,)), pltpu.SemaphoreType.DMA((n,))],
        collective_id)


def _cast_shards(name, shards):
    n = len(shards)
    steps = min(_row_steps(s.shape[0]) for s in shards)

    def body(*refs):
        for i in range(n):
            refs[n + i][...] = refs[i][...].astype(BF16)

    blocks = [pl.BlockSpec((s.shape[0] // steps, s.shape[1]), lambda i: (i, 0)) for s in shards]
    return pl.pallas_call(
        body, name=name, grid=(steps,), in_specs=blocks, out_specs=blocks,
        out_shape=[jax.ShapeDtypeStruct(s.shape, BF16) for s in shards],
        compiler_params=_params(("parallel",)),
    )(*shards)


BIG = ("w_in", "w_branch_a", "w_branch_b", "w_out", "w_ffn_gate", "w_ffn_up", "w_ffn_down")


def _local_step(x, target, gains, low, conv_w, wg8, reduce):
    g_mix, g_hg, g_ffn, g_fin = gains
    w_in = wg8["w_in"].reshape(N_IN, D_MODEL)
    wg = wg8["w_ffn_gate"].reshape(D_FF, D_MODEL)
    wu = wg8["w_ffn_up"].reshape(D_FF, D_MODEL)
    wa, wb = wg8["w_branch_a"], wg8["w_branch_b"]
    wo = wg8["w_out"].reshape(D_MODEL, D_MODEL)
    wd = wg8["w_ffn_down"].reshape(D_FF, D_MODEL)

    ht, hg, cv, gt, cvo, cvot, cy, o, og, ogt, st = _fwd_in(x, g_mix, w_in, conv_w, low, g_hg)
    x1, mgt, ya, yb = _merge_fwd(og, cvo, gt, x, wa, wb, wo)
    h2t, gate, up, act, loss, d_gfin, dx2, dx2t = _ffn_fwd_loss(x1, g_ffn, wg, wu, wd, target, g_fin)

    dgate, dup, dx1, d_gffn = _ffn_bwd(dx2, x1, gate, up, g_ffn, wg, wu, wd)
    d_wg, d_wu, d_wd = _wgrad_ffn(h2t, dgate, dup, dx2t, act)
    by_owner_ffn = lambda a: a.reshape(N_DEV, D_FF // N_DEV, D_MODEL)
    late = ("w_ffn_gate", "w_ffn_up")
    ffn = dict(w_ffn_gate=by_owner_ffn(d_wg), w_ffn_up=by_owner_ffn(d_wu), w_ffn_down=by_owner_ffn(d_wd))
    got_ffn = reduce.swap(ffn)
    dgt, dya, dyb, dog, dcv, d_conv = _merge_bwd(dx1, ya, yb, gt, cv, cy, wa, wb, wo, conv_w)
    grad_a, grad_b, grad_o, *sums_ffn = _wgrad_out_branches(ogt, dya, cvot, dyb, mgt, dx1, list(ffn.values()), got_ffn,
                                                           reduce.core)
    out = dict(w_out=grad_o.reshape(N_DEV, D_MODEL // N_DEV, D_MODEL), w_branch_a=grad_a, w_branch_b=grad_b)
    parts_ffn, _ = reduce.finish({n: ffn[n] for n in late}, sums_ffn[:2], defer=late)
    dhg, d_low, d_ghg = _hg_bwd(dog, hg, o, st, low, g_hg, after=list(sums_ffn))
    sums_out, got_out = reduce.begin(out, sum_after=[dhg])
    parts_out, updated_out = reduce.finish(dict(out, w_ffn_down=ffn["w_ffn_down"]), list(sums_out) + [sums_ffn[2]])
    dparts = [dhg, dcv, dgt]
    d_w_in_t, ridden = _wgrad_in(ht, dparts, after=sums_out[:1],
                                 riders=reduce.riders(late, dict(zip(late, parts_ffn))))
    reduce.record(late, ridden)
    w_in_grad = dict(w_in=d_w_in_t.reshape(N_DEV, N_IN // N_DEV, D_MODEL))
    sums_in, _ = reduce.begin(w_in_grad, after=parts_out[:1], sum_after=updated_out)
    parts_in, _ = reduce.finish(w_in_grad, sums_in)
    grad_x, d_gmix = _in_bwd(dparts, w_in, x, dx1, g_mix, after=list(parts_out[:1]) + list(sums_in))
    small = dict(norm_mix_g=d_gmix, norm_ffn_g=d_gffn, norm_final_g=d_gfin, lower_bounds=d_low, hg_norm_g=d_ghg,
                 conv_w=d_conv, loss=loss)
    return grad_x, small, parts_in


def kernel(x, norm_mix_g, w_in, lower_bounds, hg_norm_g, conv_w, w_branch_a, w_branch_b, w_out, norm_ffn_g, w_ffn_gate, w_ffn_up, w_ffn_down, norm_final_g, loss_target, m_norm_mix_g, m_w_in, m_lower_bounds, m_hg_norm_g, m_conv_w, m_w_branch_a, m_w_branch_b, m_w_out, m_norm_ffn_g, m_w_ffn_gate, m_w_ffn_up, m_w_ffn_down, m_norm_final_g, v_norm_mix_g, v_w_in, v_lower_bounds, v_hg_norm_g, v_conv_w, v_w_branch_a, v_w_branch_b, v_w_out, v_norm_ffn_g, v_w_ffn_gate, v_w_ffn_up, v_w_ffn_down, v_norm_final_g):
    cx, cy, cc = lax.axis_index("x"), lax.axis_index("y"), lax.axis_index("c")
    my_dev = 4 * cx + 2 * cy + cc

    def tr(a):
        return a[0].T

    big = dict(w_in=tr(w_in), w_branch_a=w_branch_a[0], w_branch_b=w_branch_b[0], w_out=w_out[0],
               w_ffn_gate=tr(w_ffn_gate), w_ffn_up=tr(w_ffn_up), w_ffn_down=w_ffn_down[0])
    big_m = dict(w_in=tr(m_w_in), w_branch_a=m_w_branch_a[0], w_branch_b=m_w_branch_b[0], w_out=m_w_out[0],
                 w_ffn_gate=tr(m_w_ffn_gate), w_ffn_up=tr(m_w_ffn_up), w_ffn_down=m_w_ffn_down[0])
    big_v = dict(w_in=tr(v_w_in), w_branch_a=v_w_branch_a[0], w_branch_b=v_w_branch_b[0], w_out=v_w_out[0],
                 w_ffn_gate=tr(v_w_ffn_gate), w_ffn_up=tr(v_w_ffn_up), w_ffn_down=v_w_ffn_down[0])
    transposed = ("w_in", "w_ffn_gate", "w_ffn_up")

    ids = iter(range(1, 16))
    shards = dict(zip(BIG[:1], _cast_shards("cast_w_in", [big["w_in"]])))
    first = _all_gather("gather_w_in", [shards["w_in"], conv_w.transpose(1, 0, 2)], collective_id=next(ids),
                        pieces=[4, 1])
    shards.update(zip(BIG[1:], _cast_shards("cast_shards", [big[n] for n in BIG[1:]])))
    mid = _all_gather("gather_mid", [shards[n] for n in BIG[1:4]], collective_id=next(ids))
    ffn = _all_gather("gather_ffn", [shards[n] for n in BIG[4:]], collective_id=next(ids), pieces=[2, 2, 2])
    wg8 = dict(zip(BIG, [first[0]] + list(mid) + list(ffn)))
    conv_full = first[1].transpose(1, 2, 0, 3).reshape(3, CONV_WIDTH)

    core = cc.reshape(1).astype(jnp.int32)
    outs = {}

    class Reduce:
        core = cc.reshape(1).astype(jnp.int32)

        @staticmethod
        def swap(grads, after=()):
            names = list(grads)
            return _sibling_swap("sibling_swap_" + names[0], [grads[n] for n in names], collective_id=next(ids),
                                 after=after)

        @staticmethod
        def begin(grads, after=(), sum_after=()):
            got = Reduce.swap(grads, after)
            sums = _pair_sum("pair_sum_" + list(grads)[0], list(grads.values()), got, core, after=sum_after)
            return sums, got

        @staticmethod
        def finish(grads, chip_sums, after=(), defer=()):
            names = list(grads)
            parts = _chip_exchange("chip_exchange_" + names[0], chip_sums, collective_id=next(ids), after=after)
            now = [n for n in names if n not in defer]
            if now:
                updated = _adamw_sum("adamw_" + now[0], [big[n] for n in now],
                                     [p for n, p in zip(names, parts) if n in now],
                                     [big_m[n] for n in now], [big_v[n] for n in now])
                outs.update(zip(now, updated))
            return parts, [outs[n][1] for n in now]

        @staticmethod
        def riders(names, parts):
            return [(big[n], parts[n], big_m[n], big_v[n]) for n in names]

        @staticmethod
        def record(names, updated):
            outs.update(zip(names, updated))

    gains = (norm_mix_g, hg_norm_g, norm_ffn_g, norm_final_g.reshape(1, D_MODEL))
    grad_x, small, last = _local_step(x[0], loss_target[0], gains, lower_bounds, conv_full, wg8, Reduce)

    small_all = _all_gather("gather_small", [_small_pack(small)], collective_id=next(ids), after=last[:1])

    def small_state(a):
        return [a[0], a[1], a[2].reshape(1, D_MODEL), a[3], a[4], a[5].transpose(1, 0, 2)]

    upd = _small_update(
        small_all[0], my_dev.reshape(1).astype(jnp.int32),
        small_state((norm_mix_g, norm_ffn_g, norm_final_g, lower_bounds, hg_norm_g, conv_w)),
        small_state((m_norm_mix_g, m_norm_ffn_g, m_norm_final_g, m_lower_bounds, m_hg_norm_g, m_conv_w)),
        small_state((v_norm_mix_g, v_norm_ffn_g, v_norm_final_g, v_lower_bounds, v_hg_norm_g, v_conv_w)))
    loss = upd[0][0, 0]
    for p, (name, _, _, _) in enumerate(_SMALL_PARAMS):
        outs[name] = upd[1 + 4 * p:5 + 4 * p]
    outs["norm_final_g"] = [a.reshape(D_MODEL) for a in outs["norm_final_g"]]
    outs["conv_w"] = [a.transpose(1, 0, 2) for a in outs["conv_w"]]

    order = ["norm_mix_g", "w_in", "lower_bounds", "hg_norm_g", "conv_w", "w_branch_a", "w_branch_b", "w_out",
             "norm_ffn_g", "w_ffn_gate", "w_ffn_up", "w_ffn_down", "norm_final_g"]
    result = [loss, grad_x[None]]
    for k in range(4):
        for n in order:
            if n in BIG:
                result.append((outs[n][k].T if n in transposed else outs[n][k])[None])
            else:
                result.append(outs[n][k])
    return tuple(result)
```

```python
import jax
import jax.numpy as jnp
from jax import lax
from jax.experimental import pallas as pl
from jax.experimental.pallas import tpu as pltpu
from jax.experimental.pallas import tpu_sc as plsc

F32 = jnp.float32
BF16 = jnp.bfloat16
STASH = jnp.bfloat16

D_MODEL = 1024
HG_WIDTH = 512
HEAD_DIM = 128
N_HEADS = 4
HEADS_PER_STEP = 4
HEAD_GROUPS = N_HEADS // HEADS_PER_STEP
CONV_WIDTH = 512
CONV_K = 3
D_FF = 2816
CHUNK = 32
EPS = 1e-6
Q_SCALE = HEAD_DIM ** -0.5
N_DEV = 8

ADAM_LR = 0.001
ADAM_B1 = 0.9
ADAM_B2 = 0.999
ADAM_EPS = 1e-08
ADAM_WD = 0.01
ADAM_STEP = 10

VMEM_LIMIT_V7X = 56 * 1024 * 1024
VMEM_LIMIT_LARGE_V7X = 62 * 1024 * 1024

SMALL_ROWS = 16


def _params(sem, vmem=VMEM_LIMIT_V7X):
    return pltpu.CompilerParams(dimension_semantics=sem, vmem_limit_bytes=vmem)


def _mm(a, b):
    return jnp.dot(a.astype(BF16), b.astype(BF16), preferred_element_type=F32)


def _mm_nt(a, b):
    return lax.dot_general(a.astype(BF16), b.astype(BF16), (((1,), (1,)), ((), ())), preferred_element_type=F32)


def _mm_tn(a, b):
    return lax.dot_general(a.astype(BF16), b.astype(BF16), (((0,), (0,)), ((), ())), preferred_element_type=F32)


def _sigmoid(x):
    return 0.5 * jnp.tanh(0.5 * x) + 0.5


def _resident(shape):
    nd = len(shape)
    return pl.BlockSpec(shape, lambda *_: (0,) * nd, pipeline_mode=pl.Buffered(1))


def _full(shape):
    nd = len(shape)
    return pl.BlockSpec(shape, lambda *_: (0,) * nd)


def _shard_cols(w_ref):
    return jnp.concatenate([w_ref[s] for s in range(N_DEV)], axis=1)


N_HG = 4 * HG_WIDTH
N_CV = 3 * CONV_WIDTH
N_GT = 2 * D_MODEL
N_IN = N_HG + N_CV + N_GT


def _col(tm, n):
    return pl.BlockSpec((n, tm), lambda i: (0, i))


HALO = 8


def _fwd_in(x, g, w_in_t, conv_w, low, gn):
    T = x.shape[0]
    tm = min(512, T)
    nc = tm // CHUNK

    def body(x_ref, g_ref, w_ref, cw_ref, low_ref, gn_ref, ht_ref, hg_ref, cv_ref, gt_ref, cvo_ref, cvot_ref, cy_ref,
             o_ref, og_ref, ogt_ref, st_ref, tail_scr, s_scr):
        @pl.when(pl.program_id(0) == 0)
        def _():
            tail_scr[...] = jnp.zeros_like(tail_scr)
            s_scr[...] = jnp.zeros_like(s_scr)

        xv = x_ref[...]
        r = lax.rsqrt(jnp.mean(xv * xv, axis=-1, keepdims=True) + EPS)
        hf = xv * r * g_ref[...]
        h = hf.astype(BF16)
        ht_ref[...] = hf.T.astype(BF16)
        hg_ref[...] = _mm_nt(h, w_ref[:N_HG, :])
        cv = _mm_nt(h, w_ref[N_HG:N_HG + N_CV, :])
        cv_ref[...] = cv.astype(STASH)
        gt_ref[...] = _mm_nt(h, w_ref[N_HG + N_CV:, :]).astype(STASH)

        u = cv[:, :CONV_WIDTH] * cv[:, 2 * CONV_WIDTH:]
        row = lax.broadcasted_iota(jnp.int32, u.shape, 0)
        prev1 = tail_scr[HALO - 1:HALO, :]
        prev2 = tail_scr[HALO - 2:HALO - 1, :]
        u1 = jnp.where(row >= 1, pltpu.roll(u, 1, 0), prev1)
        u2 = jnp.where(row >= 2, pltpu.roll(u, 2, 0), jnp.where(row == 1, prev1, prev2))
        y = cw_ref[0:1, :] * u2 + cw_ref[1:2, :] * u1 + cw_ref[2:3, :] * u
        cy_ref[...] = y.astype(STASH)
        out = cv[:, CONV_WIDTH:2 * CONV_WIDTH] * y
        cvo_ref[...] = out.astype(BF16)
        cvot_ref[...] = out.T.astype(BF16)
        tail_scr[...] = u[tm - HALO:, :]

        _hg_fwd_tile(hg_ref, low_ref, gn_ref, o_ref, og_ref, ogt_ref, st_ref, s_scr, tm)

    row = lambda n: pl.BlockSpec((tm, n), lambda i: (i, 0))
    return pl.pallas_call(
        body, name="fwd_in", grid=(T // tm,),
        in_specs=[row(D_MODEL), _full((1, D_MODEL)), _resident(w_in_t.shape), _full((CONV_K, CONV_WIDTH)),
                  _full((2, HG_WIDTH)), _full((1, HEAD_DIM))],
        out_specs=[_col(tm, D_MODEL), row(N_HG), row(N_CV), row(N_GT), row(CONV_WIDTH), _col(tm, CONV_WIDTH),
                   row(CONV_WIDTH),
                   row(HG_WIDTH), row(HG_WIDTH), _col(tm, HG_WIDTH),
                   pl.BlockSpec((N_HEADS, nc, HEAD_DIM, HEAD_DIM), lambda i: (0, i, 0, 0))],
        out_shape=[jax.ShapeDtypeStruct((D_MODEL, T), BF16), jax.ShapeDtypeStruct((T, N_HG), F32),
                   jax.ShapeDtypeStruct((T, N_CV), STASH), jax.ShapeDtypeStruct((T, N_GT), STASH),
                   jax.ShapeDtypeStruct((T, CONV_WIDTH), BF16), jax.ShapeDtypeStruct((CONV_WIDTH, T), BF16),
                   jax.ShapeDtypeStruct((T, CONV_WIDTH), STASH), jax.ShapeDtypeStruct((T, HG_WIDTH), F32), jax.ShapeDtypeStruct((T, HG_WIDTH), BF16),
                   jax.ShapeDtypeStruct((HG_WIDTH, T), BF16),
                   jax.ShapeDtypeStruct((N_HEADS, T // CHUNK, HEAD_DIM, HEAD_DIM), F32)],
        scratch_shapes=[pltpu.VMEM((HALO, CONV_WIDTH), F32), pltpu.VMEM((N_HEADS, HEAD_DIM, HEAD_DIM), F32)],
        compiler_params=_params(("arbitrary",), vmem=VMEM_LIMIT_LARGE_V7X),
    )(x, g, w_in_t, conv_w, low, gn)


def _chunk_pos(shape):
    return lax.broadcasted_iota(jnp.int32, shape, 0) & (CHUNK - 1)


def _chunk_cumsum(x, pos):
    s = 1
    while s < CHUNK:
        x = x + jnp.where(pos >= s, pltpu.roll(x, s, 0), 0.0)
        s *= 2
    return x


def _chunk_rev_cumsum(x, pos):
    n = x.shape[0]
    s = 1
    while s < CHUNK:
        x = x + jnp.where(pos + s < CHUNK, pltpu.roll(x, n - s, 0), 0.0)
        s *= 2
    return x


def _lower_bound(low_ref):
    l0 = low_ref[0:1, :]
    l1 = low_ref[1:2, :]
    m = jnp.maximum(l0, l1)
    e0 = jnp.exp(l0 - m)
    e1 = jnp.exp(l1 - m)
    return e0 / (e0 + e1), e1 / (e0 + e1)


def _hg_gates(qr, fr, lb, pos, tb):
    sq = _sigmoid(qr)
    q = qr * sq * Q_SCALE
    sg = _sigmoid(fr)
    f = lb + (1.0 - lb) * sg
    k = 1.0 - f
    b = _chunk_cumsum(jnp.log(f), pos)
    b3 = b.reshape(tb // CHUNK, CHUNK, HEAD_DIM)
    anc = b3[:, CHUNK // 2 - 1:CHUNK // 2, :]
    last = b3[:, CHUNK - 1:CHUNK, :]
    d3 = b3 - anc
    e_qa3 = jnp.exp(d3)
    e_ka3 = jnp.exp(-d3)
    e_b3 = e_qa3 * jnp.exp(anc)
    e_ko3 = e_ka3 * jnp.exp(last - anc)
    dec = jnp.exp(last)
    flat = lambda a: a.reshape(tb, HEAD_DIM)
    return sq, q, sg, f, k, flat(e_qa3), flat(e_ka3), flat(e_b3), flat(e_ko3), dec


def _intra_mask(sb):
    r = lax.broadcasted_iota(jnp.int32, (sb, sb), 0)
    c = lax.broadcasted_iota(jnp.int32, (sb, sb), 1)
    return ((r // CHUNK) == (c // CHUNK)) & (c <= r)


def _hg_fwd_tile(hg_ref, low_ref, gn_ref, o_ref, og_ref, ogt_ref, st_ref, s_scr, tb):
    sb = min(256, tb)
    nc = tb // CHUNK
    q_ref, f_ref, i_ref, g_ref = (hg_ref.at[:, p * HG_WIDTH:(p + 1) * HG_WIDTH] for p in range(4))
    pos = _chunk_pos((tb, HEAD_DIM))
    mask = _intra_mask(sb)
    lanes = [slice(hh * HEAD_DIM, (hh + 1) * HEAD_DIM) for hh in range(N_HEADS)]
    qi, ko, vb, dec, st = [], [], [], [], []
    for hh, ln in enumerate(lanes):
        lb, _ = _lower_bound(low_ref.at[:, ln])
        _, q, _, _, k, e_qa, e_ka, e_b, e_ko, dec_h = _hg_gates(q_ref[:, ln], f_ref[:, ln], lb, pos, tb)
        qh = (q * e_qa).astype(BF16)
        kh = (k * e_ka).astype(BF16)
        qi.append((q * e_b).astype(BF16))
        ko.append((k * e_ko).astype(BF16))
        vb.append(i_ref[:, ln].astype(BF16))
        dec.append(dec_h)
        st.append(s_scr[hh])
        for s in range(tb // sb):
            sl = slice(s * sb, (s + 1) * sb)
            p = jnp.where(mask, _mm_nt(qh[sl], kh[sl]), 0.0)
            o_ref[sl, ln] = _mm(p, vb[hh][sl])
    for c in range(nc):
        sl = slice(c * CHUNK, (c + 1) * CHUNK)
        for hh, ln in enumerate(lanes):
            st_ref[hh, c] = st[hh]
            o_ref[sl, ln] = o_ref[sl, ln] + _mm_nt(qi[hh][sl], st[hh])
            st[hh] = dec[hh][c] * st[hh] + _mm_tn(vb[hh][sl], ko[hh][sl])
    for hh, ln in enumerate(lanes):
        s_scr[hh] = st[hh]
        o = o_ref[:, ln]
        r = lax.rsqrt(jnp.mean(o * o, axis=-1, keepdims=True) + EPS)
        gr = g_ref[:, ln]
        og = (o * r * gn_ref[...]) * (gr * _sigmoid(gr))
        og_ref[:, ln] = og.astype(BF16)
        ogt_ref[ln, :] = og.T.astype(BF16)


def _merge_fwd(og, cvo, gt, x, wa, wb, wo):
    T = x.shape[0]
    tm = min(1024, T)

    def body(og_ref, cvo_ref, gt_ref, x_ref, wa_ref, wb_ref, wo_ref, x1_ref, mgt_ref, ya_ref, yb_ref):
        ya = jnp.dot(og_ref[...], _shard_cols(wa_ref), preferred_element_type=F32)
        yb = jnp.dot(cvo_ref[...], _shard_cols(wb_ref), preferred_element_type=F32)
        ya_ref[...] = ya.astype(STASH)
        yb_ref[...] = yb.astype(STASH)
        m = (_sigmoid(gt_ref[:, :D_MODEL].astype(F32)) * ya
             + _sigmoid(gt_ref[:, D_MODEL:].astype(F32)) * yb)
        mgt_ref[...] = m.T.astype(BF16)
        x1_ref[...] = x_ref[...] + jnp.dot(m.astype(BF16), wo_ref[...], preferred_element_type=F32)

    row = lambda n: pl.BlockSpec((tm, n), lambda i: (i, 0))
    return pl.pallas_call(
        body, name="merge_fwd", grid=(T // tm,),
        in_specs=[row(HG_WIDTH), row(CONV_WIDTH), row(2 * D_MODEL), row(D_MODEL),
                  _resident(wa.shape), _resident(wb.shape), _resident(wo.shape)],
        out_specs=[row(D_MODEL), _col(tm, D_MODEL), row(D_MODEL), row(D_MODEL)],
        out_shape=[jax.ShapeDtypeStruct((T, D_MODEL), F32), jax.ShapeDtypeStruct((D_MODEL, T), BF16),
                   jax.ShapeDtypeStruct((T, D_MODEL), STASH), jax.ShapeDtypeStruct((T, D_MODEL), STASH)],
        compiler_params=_params(("parallel",)),
    )(og, cvo, gt, x, wa, wb, wo)


def _ffn_fwd_loss(x1, g, wg, wu, wd, target, g_fin):
    T = x1.shape[0]
    tm = min(512, T)

    def body(x_ref, g_ref, wg_ref, wu_ref, wd_ref, t_ref, gf_ref,
             ht_ref, gate_ref, up_ref, act_ref, loss_ref, dgf_ref, dx2_ref, dx2t_ref):
        @pl.when(pl.program_id(0) == 0)
        def _():
            loss_ref[...] = jnp.zeros_like(loss_ref)
            dgf_ref[...] = jnp.zeros_like(dgf_ref)

        xv = x_ref[...]
        r = lax.rsqrt(jnp.mean(xv * xv, axis=-1, keepdims=True) + EPS)
        hf = xv * r * g_ref[...]
        h = hf.astype(BF16)
        ht_ref[...] = hf.T.astype(BF16)
        gate = _mm_nt(h, wg_ref[...])
        up = _mm_nt(h, wu_ref[...])
        gate_ref[...] = gate.astype(STASH)
        up_ref[...] = up.astype(STASH)
        act = (gate * _sigmoid(gate) * up).astype(BF16)
        act_ref[...] = act
        x2 = xv + jnp.dot(act, wd_ref[...], preferred_element_type=F32)

        gv = gf_ref[...]
        r2 = lax.rsqrt(jnp.mean(x2 * x2, axis=-1, keepdims=True) + EPS)
        xh = x2 * r2
        err = xh * gv - t_ref[...]
        loss_ref[...] += 0.5 * jnp.sum(jnp.mean(err * err, axis=-1, keepdims=True), axis=0, keepdims=True)
        dy = err * (1.0 / D_MODEL)
        dgf_ref[...] += jnp.sum(dy * xh, axis=0, keepdims=True)
        w = dy * gv
        dx2 = r2 * (w - xh * jnp.mean(w * xh, axis=-1, keepdims=True))
        dx2_ref[...] = dx2
        dx2t_ref[...] = dx2.T.astype(BF16)

    row = lambda n: pl.BlockSpec((tm, n), lambda i: (i, 0))
    return pl.pallas_call(
        body, name="ffn_fwd_loss", grid=(T // tm,),
        in_specs=[row(D_MODEL), _full((1, D_MODEL)), _resident(wg.shape), _resident(wu.shape), _resident(wd.shape),
                  row(D_MODEL), _full((1, D_MODEL))],
        out_specs=[_col(tm, D_MODEL), row(D_FF), row(D_FF), row(D_FF), _full((1, 128)), _full((1, D_MODEL)),
                   row(D_MODEL), _col(tm, D_MODEL)],
        out_shape=[jax.ShapeDtypeStruct((D_MODEL, T), BF16), jax.ShapeDtypeStruct((T, D_FF), STASH),
                   jax.ShapeDtypeStruct((T, D_FF), STASH), jax.ShapeDtypeStruct((T, D_FF), BF16),
                   jax.ShapeDtypeStruct((1, 128), F32), jax.ShapeDtypeStruct((1, D_MODEL), F32),
                   jax.ShapeDtypeStruct((T, D_MODEL), F32), jax.ShapeDtypeStruct((D_MODEL, T), BF16)],
        compiler_params=_params(("arbitrary",), vmem=VMEM_LIMIT_LARGE_V7X),
    )(x1, g, wg, wu, wd, target, g_fin)


def _ffn_bwd(dx2, x1, gate, up, g, wg, wu, wd):
    T = x1.shape[0]
    tm = min(512, T)

    def body(dx2_ref, x_ref, gate_ref, up_ref, g_ref, wg_ref, wu_ref, wd_ref, dgate_ref, dup_ref, dx1_ref, dgn_ref):
        @pl.when(pl.program_id(0) == 0)
        def _():
            dgn_ref[...] = jnp.zeros_like(dgn_ref)

        dx2 = dx2_ref[...]
        dact = _mm_nt(dx2, wd_ref[...])
        gate = gate_ref[...].astype(F32)
        s = _sigmoid(gate)
        dgate = (dact * up_ref[...].astype(F32) * (s * (1.0 + gate * (1.0 - s)))).astype(BF16)
        dup = (dact * (gate * s)).astype(BF16)
        dgate_ref[...] = dgate
        dup_ref[...] = dup
        dh = _mm(dgate, wg_ref[...]) + _mm(dup, wu_ref[...])
        xv = x_ref[...]
        r = lax.rsqrt(jnp.mean(xv * xv, axis=-1, keepdims=True) + EPS)
        xh = xv * r
        dgn_ref[...] += jnp.sum(dh * xh, axis=0, keepdims=True)
        w = dh * g_ref[...]
        dx1_ref[...] = dx2 + r * (w - xh * jnp.mean(w * xh, axis=-1, keepdims=True))

    row = lambda n: pl.BlockSpec((tm, n), lambda i: (i, 0))
    return pl.pallas_call(
        body, name="ffn_bwd", grid=(T // tm,),
        in_specs=[row(D_MODEL), row(D_MODEL), row(D_FF), row(D_FF), _full((1, D_MODEL)),
                  _resident(wg.shape), _resident(wu.shape), _resident(wd.shape)],
        out_specs=[row(D_FF), row(D_FF), row(D_MODEL), _full((1, D_MODEL))],
        out_shape=[jax.ShapeDtypeStruct((T, D_FF), BF16), jax.ShapeDtypeStruct((T, D_FF), BF16),
                   jax.ShapeDtypeStruct((T, D_MODEL), F32), jax.ShapeDtypeStruct((1, D_MODEL), F32)],
        compiler_params=_params(("arbitrary",), vmem=VMEM_LIMIT_LARGE_V7X),
    )(dx2, x1, gate, up, g, wg, wu, wd)


def _merge_bwd(dx1, ya, yb, gt, cv, cy, wa, wb, wo, conv_w):
    T = dx1.shape[0]
    tm = min(512, T)
    nt = T // tm
    tiles = (dx1, ya, yb, gt, cv, cy)

    def body(dx_hbm, ya_hbm, yb_hbm, gt_hbm, cv_hbm, cy_hbm, wa_ref, wb_ref, wo_ref, cw_ref,
             dgt_ref, dya_ref, dyb_ref, dog_ref, dcv_ref, dcw_ref, next_dy, *rings_sems):
        step = pl.program_id(0)
        streams = (dx_hbm, ya_hbm, yb_hbm, gt_hbm, cv_hbm, cy_hbm)
        rings, sems = rings_sems[:-1], rings_sems[-1]

        def fetch(s):
            first = (nt - 1 - s) * tm
            rows = pl.ds(first if isinstance(first, int) else pl.multiple_of(first, tm), tm)
            slot = s % 3
            return [pltpu.make_async_copy(hbm.at[rows], ring.at[slot], sems.at[k, slot])
                    for k, (hbm, ring) in enumerate(zip(streams, rings))]

        @pl.when(step == 0)
        def _():
            next_dy[...] = jnp.zeros_like(next_dy)
            dcw_ref[...] = jnp.zeros_like(dcw_ref)
            for s in range(min(2, nt)):
                for cp in fetch(s):
                    cp.start()

        @pl.when(step + 2 < nt)
        def _():
            for cp in fetch(step + 2):
                cp.start()

        for cp in fetch(step):
            cp.wait()
        dx_ref, ya_ref, yb_ref, gates, cv_ref, cy_ref = [ring.at[step % 3] for ring in rings]
        dm = _mm_nt(dx_ref[...], wo_ref[...])
        wa = _shard_cols(wa_ref)
        wb = _shard_cols(wb_ref)
        ya = ya_ref[...].astype(F32)
        yb = yb_ref[...].astype(F32)
        sa = _sigmoid(gates[:, :D_MODEL].astype(F32))
        sb = _sigmoid(gates[:, D_MODEL:].astype(F32))
        da = dm * sa
        db = dm * sb
        dgt_ref[:, :D_MODEL] = (da * ya * (1.0 - sa)).astype(BF16)
        dgt_ref[:, D_MODEL:] = (db * yb * (1.0 - sb)).astype(BF16)
        dya = da.astype(BF16)
        dyb = db.astype(BF16)
        dya_ref[...] = dya
        dyb_ref[...] = dyb
        dog_ref[...] = _mm_nt(dya, wa)
        dcvo = _mm_nt(dyb, wb)

        cvt = cv_ref[...].astype(F32)
        c, bg, xb = cvt[:, :CONV_WIDTH], cvt[:, CONV_WIDTH:2 * CONV_WIDTH], cvt[:, 2 * CONV_WIDTH:]
        u = c * xb
        row = lax.broadcasted_iota(jnp.int32, u.shape, 0)
        w0, w1, w2 = cw_ref[0:1, :], cw_ref[1:2, :], cw_ref[2:3, :]
        dcv_ref[:, CONV_WIDTH:2 * CONV_WIDTH] = (dcvo * cy_ref[...].astype(F32)).astype(BF16)
        dy = dcvo * bg
        n1 = next_dy[0:1, :]
        n2 = next_dy[1:2, :]
        dy1 = jnp.where(row < tm - 1, pltpu.roll(dy, tm - 1, 0), n1)
        dy2 = jnp.where(row < tm - 2, pltpu.roll(dy, tm - 2, 0), jnp.where(row == tm - 2, n1, n2))
        dcw_ref[0:1, :] += jnp.sum(dy2 * u, axis=0, keepdims=True)
        dcw_ref[1:2, :] += jnp.sum(dy1 * u, axis=0, keepdims=True)
        dcw_ref[2:3, :] += jnp.sum(dy * u, axis=0, keepdims=True)
        du = w2 * dy + w1 * dy1 + w0 * dy2
        dcv_ref[:, :CONV_WIDTH] = (du * xb).astype(BF16)
        dcv_ref[:, 2 * CONV_WIDTH:] = (du * c).astype(BF16)
        next_dy[...] = dy[:HALO, :]

    rt = lambda i: nt - 1 - i
    row = lambda n: pl.BlockSpec((tm, n), lambda i: (rt(i), 0))
    return pl.pallas_call(
        body, name="merge_bwd", grid=(nt,),
        in_specs=[HBM_SPEC] * len(tiles)
        + [_resident(wa.shape), _resident(wb.shape), _resident(wo.shape), _full((CONV_K, CONV_WIDTH))],
        out_specs=[row(2 * D_MODEL), row(D_MODEL), row(D_MODEL), row(HG_WIDTH), row(N_CV),
                   _full((CONV_K, CONV_WIDTH))],
        out_shape=[jax.ShapeDtypeStruct((T, 2 * D_MODEL), BF16), jax.ShapeDtypeStruct((T, D_MODEL), BF16),
                   jax.ShapeDtypeStruct((T, D_MODEL), BF16), jax.ShapeDtypeStruct((T, HG_WIDTH), F32),
                   jax.ShapeDtypeStruct((T, N_CV), BF16), jax.ShapeDtypeStruct((CONV_K, CONV_WIDTH), F32)],
        scratch_shapes=[pltpu.VMEM((HALO, CONV_WIDTH), F32)]
        + [pltpu.VMEM((3, tm, a.shape[1]), a.dtype) for a in tiles] + [pltpu.SemaphoreType.DMA((len(tiles), 3))],
        compiler_params=_params(("arbitrary",)),
    )(*tiles, wa, wb, wo, conv_w)


def _drop_operands(body, first, count):
    def wrapped(*refs):
        return body(*refs[:first], *refs[first + count:])
    return wrapped


def _hg_bwd(dog, hg, o, st, low, gn, after=()):
    T = hg.shape[0]
    tb = min(512, T)
    sb = min(256, tb)
    nb = T // tb
    nc = tb // CHUNK
    wid = HEADS_PER_STEP * HEAD_DIM

    def body(q_ref, f_ref, i_ref, g_ref, low_ref, gn_ref, o_ref, dog_ref, st_ref,
             dhg_ref, dlow_ref, dgn_ref,
             ds_scr, dqi_scr, dko_scr, dv_scr, dd_scr, dqh_scr, dkh_scr):
        h = pl.program_id(0)
        t = pl.program_id(1)
        dq_ref, df_ref, di_ref, dg_ref = (dhg_ref.at[:, p * HG_WIDTH:(p + 1) * HG_WIDTH] for p in range(4))

        @pl.when(t == 0)
        def _():
            ds_scr[...] = jnp.zeros_like(ds_scr)
            dlow_ref[...] = jnp.zeros_like(dlow_ref)

        @pl.when((t == 0) & (h == 0))
        def _():
            dgn_ref[...] = jnp.zeros_like(dgn_ref)

        pos = _chunk_pos((tb, HEAD_DIM))
        mask = _intra_mask(sb)
        gnv = gn_ref[...]
        lanes = [slice(hh * HEAD_DIM, (hh + 1) * HEAD_DIM) for hh in range(HEADS_PER_STEP)]
        heads = []
        for hh, ln in enumerate(lanes):
            lb, lb1 = _lower_bound(low_ref.at[:, ln])
            qr = q_ref[:, ln]
            sq, q, sg, f, k, e_qa, e_ka, e_b, e_ko, dec = _hg_gates(qr, f_ref[:, ln], lb, pos, tb)

            gr = g_ref[:, ln]
            o = o_ref[:, ln]
            dog_v = dog_ref[:, ln]
            sgr = _sigmoid(gr)
            r = lax.rsqrt(jnp.mean(o * o, axis=-1, keepdims=True) + EPS)
            oh = o * r
            dg_ref[:, ln] = (dog_v * (oh * gnv) * (sgr * (1.0 + gr * (1.0 - sgr)))).astype(BF16)
            don = dog_v * (gr * sgr)
            dgn_ref[...] += jnp.sum(don * oh, axis=0, keepdims=True)
            w = don * gnv
            do = (r * (w - oh * jnp.mean(w * oh, axis=-1, keepdims=True))).astype(BF16)

            qh = (q * e_qa).astype(BF16)
            kh = (k * e_ka).astype(BF16)
            qi = (q * e_b).astype(BF16)
            ko = (k * e_ko).astype(BF16)
            vb = i_ref[:, ln].astype(BF16)

            for s in range(tb // sb):
                sl = slice(s * sb, (s + 1) * sb)
                p = jnp.where(mask, _mm_nt(qh[sl], kh[sl]), 0.0).astype(BF16)
                dp = jnp.where(mask, _mm_nt(do[sl], vb[sl]), 0.0).astype(BF16)
                dv_scr[sl, ln] = _mm_tn(p, do[sl])
                dqh_scr[sl, ln] = _mm(dp, kh[sl])
                dkh_scr[sl, ln] = _mm_tn(dp, qh[sl])
            heads.append(dict(lb=lb, lb1=lb1, qr=qr, sq=sq, q=q, sg=sg, f=f, k=k, e_qa=e_qa, e_ka=e_ka, e_b=e_b,
                              e_ko=e_ko, dec=dec, do=do, qi=qi, ko=ko, vb=vb, ds=ds_scr[hh]))

        for c in reversed(range(nc)):
            sl = slice(c * CHUNK, (c + 1) * CHUNK)
            for hh, ln in enumerate(lanes):
                hd = heads[hh]
                ds = hd["ds"]
                st_c = st_ref[hh, c]
                dqi_scr[sl, ln] = _mm(hd["do"][sl], st_c)
                dko_scr[sl, ln] = _mm(hd["vb"][sl], ds)
                dv_scr[sl, ln] = dv_scr[sl, ln] + _mm_nt(hd["ko"][sl], ds)
                dec_c = hd["dec"][c]
                dd_scr[sl, ln] = jnp.broadcast_to(dec_c * jnp.sum(ds * st_c, axis=0, keepdims=True),
                                                  (CHUNK, HEAD_DIM))
                hd["ds"] = dec_c * ds + _mm_tn(hd["do"][sl], hd["qi"][sl])

        for hh, ln in enumerate(lanes):
            hd = heads[hh]
            ds_scr[hh] = hd["ds"]
            q, k, lb = hd["q"], hd["k"], hd["lb"]
            dko_e = dko_scr[:, ln] * hd["e_ko"]
            dq = dqh_scr[:, ln] * hd["e_qa"] + dqi_scr[:, ln] * hd["e_b"]
            dk = dkh_scr[:, ln] * hd["e_ka"] + dko_e
            kd3 = (k * dko_e).reshape(nc, CHUNK, HEAD_DIM)
            last = jnp.broadcast_to(jnp.sum(kd3, axis=1, keepdims=True), kd3.shape).reshape(tb, HEAD_DIM)
            db = q * dq - k * dk + jnp.where(pos == CHUNK - 1, dd_scr[:, ln] + last, 0.0)
            dlg = _chunk_rev_cumsum(db, pos)
            dfv = dlg / hd["f"] - dk
            s_low = jnp.sum(dfv * (1.0 - hd["sg"]), axis=0, keepdims=True)
            dlow_ref[0:1, ln] += s_low * lb * (1.0 - lb)
            dlow_ref[1:2, ln] += -s_low * lb * hd["lb1"]
            df_ref[:, ln] = (dfv * (1.0 - lb) * hd["sg"] * (1.0 - hd["sg"])).astype(BF16)
            dq_ref[:, ln] = (dq * Q_SCALE * (hd["sq"] * (1.0 + hd["qr"] * (1.0 - hd["sq"])))).astype(BF16)
            di_ref[:, ln] = dv_scr[:, ln].astype(BF16)

    rt = lambda t: nb - 1 - t
    col = lambda p: pl.BlockSpec((tb, wid), lambda h, t: (rt(t), p * HEAD_GROUPS + h))
    hcol = pl.BlockSpec((tb, wid), lambda h, t: (rt(t), h))
    assert HEAD_GROUPS == 1
    tile = pltpu.VMEM((tb, wid), F32)
    return pl.pallas_call(
        _drop_operands(body, 9, len(after)), name="hg_bwd", grid=(HEAD_GROUPS, nb),
        in_specs=[col(0), col(1), col(2), col(3), pl.BlockSpec((2, wid), lambda h, t: (0, h)),
                  pl.BlockSpec((1, HEAD_DIM), lambda h, t: (0, 0)), hcol, hcol,
                  pl.BlockSpec((HEADS_PER_STEP, nc, HEAD_DIM, HEAD_DIM), lambda h, t: (h, rt(t), 0, 0))]
                 + [HBM_SPEC] * len(after),
        out_specs=[pl.BlockSpec((tb, N_HG), lambda h, t: (rt(t), 0)), pl.BlockSpec((2, wid), lambda h, t: (0, h)),
                   pl.BlockSpec((1, HEAD_DIM), lambda h, t: (0, 0))],
        out_shape=[jax.ShapeDtypeStruct((T, N_HG), BF16), jax.ShapeDtypeStruct((2, HG_WIDTH), F32),
                   jax.ShapeDtypeStruct((1, HEAD_DIM), F32)],
        scratch_shapes=[pltpu.VMEM((HEADS_PER_STEP, HEAD_DIM, HEAD_DIM), F32), tile, tile, tile, tile, tile, tile],
        compiler_params=_params(("arbitrary", "arbitrary")),
    )(hg, hg, hg, hg, low, gn, o, dog, st, *after)


def _in_bwd(dparts, w_in, x, dx1, g, after=()):
    T = x.shape[0]
    tm = min(512, T)
    widths = [p.shape[1] for p in dparts]
    offs = [sum(widths[:i]) for i in range(len(widths))]
    n = len(dparts)

    def body(*refs):
        d_refs = refs[:n]
        w_ref, x_ref, dx1_ref, g_ref, dx_ref, dgn_ref = refs[n:]

        @pl.when(pl.program_id(0) == 0)
        def _():
            dgn_ref[...] = jnp.zeros_like(dgn_ref)

        dh = None
        for d_ref, off, wd in zip(d_refs, offs, widths):
            part = _mm(d_ref[...], w_ref[off:off + wd, :])
            dh = part if dh is None else dh + part
        xv = x_ref[...]
        r = lax.rsqrt(jnp.mean(xv * xv, axis=-1, keepdims=True) + EPS)
        xh = xv * r
        dgn_ref[...] += jnp.sum(dh * xh, axis=0, keepdims=True)
        w = dh * g_ref[...]
        dx_ref[...] = dx1_ref[...] + r * (w - xh * jnp.mean(w * xh, axis=-1, keepdims=True))

    row = lambda m: pl.BlockSpec((tm, m), lambda i: (i, 0))
    return pl.pallas_call(
        _drop_operands(body, n + 4, len(after)), name="in_bwd", grid=(T // tm,),
        in_specs=[row(wd) for wd in widths] + [_resident(w_in.shape), row(D_MODEL), row(D_MODEL), _full((1, D_MODEL))]
                 + [HBM_SPEC] * len(after),
        out_specs=[row(D_MODEL), _full((1, D_MODEL))],
        out_shape=[jax.ShapeDtypeStruct((T, D_MODEL), F32), jax.ShapeDtypeStruct((1, D_MODEL), F32)],
        compiler_params=_params(("arbitrary",)),
    )(*dparts, w_in, x, dx1, g, *after)


def _wgrad_ffn(h2t, dgate, dup, dx2t, act, tn=256):
    M, T = h2t.shape
    N = dgate.shape[1]

    def body(h_ref, x_ref, dg_ref, du_ref, act_ref, og_ref, ou_ref, od_ref):
        h = h_ref[...]
        og_ref[...] = _mm(h, dg_ref[...]).T.astype(BF16)
        ou_ref[...] = _mm(h, du_ref[...]).T.astype(BF16)
        od_ref[...] = _mm(x_ref[...], act_ref[...]).T.astype(BF16)

    rhs = pl.BlockSpec((T, tn), lambda j: (0, j))
    out_spec = pl.BlockSpec((tn, M), lambda j: (j, 0))
    out = jax.ShapeDtypeStruct((N, M), BF16)
    return pl.pallas_call(
        body, name="wgrad_ffn", grid=(N // tn,),
        in_specs=[_resident((M, T)), _resident((M, T)), rhs, rhs, rhs], out_specs=[out_spec] * 3,
        out_shape=[out, out, out],
        compiler_params=_params(("parallel",)),
    )(h2t, dx2t, dgate, dup, act)


def _wgrad_out_branches(ogt, dya, cvot, dyb, mgt, dx1, by_owner, got, core):
    M, T = ogt.shape
    N = dya.shape[1]
    c = N // N_DEV
    per = 2
    tn = per * c
    n = len(got)
    assert N // tn == got[0].shape[0]

    def body(core_ref, at_ref, da_ref, bt_ref, db_ref, mt_ref, dx_ref, *refs):
        mine, theirs = refs[:n], refs[n:2 * n]
        oa_ref, ob_ref, oo_ref = refs[2 * n:2 * n + 3]
        ga = _mm(at_ref[...], da_ref[...])
        gb = _mm(bt_ref[...], db_ref[...])
        for s in range(per):
            oa_ref[s] = ga[:, s * c:(s + 1) * c].astype(BF16)
            ob_ref[s] = gb[:, s * c:(s + 1) * c].astype(BF16)
        oo_ref[...] = _mm(mt_ref[...], dx_ref[...]).astype(BF16)
        for a_ref, b_ref, o_ref in zip(mine, theirs, refs[2 * n + 3:]):
            o_ref[...] = (a_ref[...].astype(F32) + b_ref[...].astype(F32)).astype(BF16)

    def blk(g):
        return pl.BlockSpec((None,) + g.shape[1:], lambda j, core_ref: (j, 0, 0))

    def own(g):
        return pl.BlockSpec((None,) + g.shape[1:], lambda j, core_ref: (2 * j + core_ref[0], 0, 0))

    rhs = pl.BlockSpec((T, tn), lambda j, core_ref: (0, j))
    owners = pl.BlockSpec((per, M, c), lambda j, core_ref: (j, 0, 0))
    out = jax.ShapeDtypeStruct((N_DEV, M, c), BF16)
    return pl.pallas_call(
        body, name="wgrad_out_branches",
        grid_spec=pltpu.PrefetchScalarGridSpec(
            num_scalar_prefetch=1, grid=(N // tn,),
            in_specs=[_resident((M, T)), rhs, _resident((M, T)), rhs, _resident(mgt.shape), rhs]
            + [own(g) for g in by_owner] + [blk(g) for g in got],
            out_specs=[owners, owners, pl.BlockSpec((mgt.shape[0], tn), lambda j, core_ref: (0, j))]
            + [blk(g) for g in got]),
        out_shape=[out, out, jax.ShapeDtypeStruct((mgt.shape[0], dx1.shape[1]), BF16)]
        + [jax.ShapeDtypeStruct(g.shape, BF16) for g in got],
        compiler_params=_params(("parallel",)),
    )(core, ogt, dya, cvot, dyb, mgt, dx1, *by_owner, *got)


def _wgrad_in(ht, dparts, after=(), riders=()):
    M, T = ht.shape
    tn = 512
    nblk = [p.shape[1] // tn for p in dparts]
    start = [sum(nblk[:i]) for i in range(len(nblk))]
    n = len(dparts)
    steps = sum(nblk)
    nr = len(riders)
    rows = [r[0].shape[0] // steps for r in riders]
    assert all(r[0].shape[0] == rr * steps and rr % 16 == 0 for r, rr in zip(riders, rows))

    def body(a_ref, *refs):
        d_refs = refs[:n]
        rider_in = refs[n:n + 4 * nr]
        o_ref = refs[n + 4 * nr]
        rider_out = refs[n + 4 * nr + 1:]
        j = pl.program_id(0)
        for d_ref, s, nb in zip(d_refs, start, nblk):
            @pl.when((j >= s) & (j < s + nb))
            def _():
                o_ref[...] = _mm(a_ref[...], d_ref[...]).T.astype(BF16)
        for i in range(nr):
            w_ref, p_ref, m_ref, v_ref = rider_in[4 * i:4 * i + 4]
            g = p_ref[0].astype(F32)
            for k in range(1, 4):
                g = g + p_ref[k].astype(F32)
            delta, m_new, v_new = _adamw_math(w_ref[...], g, m_ref[...], v_ref[...])
            for k, val in enumerate((g, delta, m_new, v_new)):
                rider_out[4 * i + k][...] = val

    def piece_spec(s, nb):
        return pl.BlockSpec((T, tn), lambda j: (0, jnp.clip(j - s, 0, nb - 1)))

    rider_specs, rider_out_specs, rider_out_shape, rider_args = [], [], [], []
    for (w, parts, m, v), rr in zip(riders, rows):
        blk = pl.BlockSpec((rr, w.shape[1]), lambda j: (j, 0))
        rider_specs += [blk, pl.BlockSpec((4, rr, w.shape[1]), lambda j: (0, j, 0)), blk, blk]
        rider_out_specs += [blk] * 4
        rider_out_shape += [jax.ShapeDtypeStruct(w.shape, F32)] * 4
        rider_args += [w, parts, m, v]
    outs = pl.pallas_call(
        _drop_operands(body, 1 + n + 4 * nr, len(after)), name="wgrad_in", grid=(steps,),
        in_specs=[_resident((M, T))] + [piece_spec(s, nb) for s, nb in zip(start, nblk)] + rider_specs
                 + [HBM_SPEC] * len(after),
        out_specs=[pl.BlockSpec((tn, M), lambda j: (j, 0))] + rider_out_specs,
        out_shape=[jax.ShapeDtypeStruct((steps * tn, M), BF16)] + rider_out_shape,
        compiler_params=_params(("parallel",)),
    )(ht, *dparts, *rider_args, *after)
    return outs[0], [outs[1 + 4 * i:5 + 4 * i] for i in range(nr)]


def _adamw_math(w, g, m, v):
    m = ADAM_B1 * m + (1.0 - ADAM_B1) * g
    v = ADAM_B2 * v + (1.0 - ADAM_B2) * (g * g)
    m_hat = m / (1.0 - ADAM_B1 ** ADAM_STEP)
    v_hat = v / (1.0 - ADAM_B2 ** ADAM_STEP)
    delta = -ADAM_LR * (m_hat / (jnp.sqrt(v_hat) + ADAM_EPS) + ADAM_WD * w)
    return delta, m, v


def _adamw_sum(name, ws, parts, ms, vs):
    n = len(ws)
    steps = min(_row_steps(w.shape[0]) for w in ws)
    rows = [w.shape[0] // steps for w in ws]

    def body(*refs):
        w_refs, p_refs, m_refs, v_refs = (refs[k * n:(k + 1) * n] for k in range(4))
        out_refs = refs[4 * n:]
        for i in range(n):
            g = p_refs[i][0].astype(F32)
            for k in range(1, 4):
                g = g + p_refs[i][k].astype(F32)
            delta, m_new, v_new = _adamw_math(w_refs[i][...], g, m_refs[i][...], v_refs[i][...])
            for k, val in enumerate((g, delta, m_new, v_new)):
                out_refs[4 * i + k][...] = val

    blk = [pl.BlockSpec((r, w.shape[1]), lambda s: (s, 0)) for r, w in zip(rows, ws)]
    pblk = [pl.BlockSpec((4, r, w.shape[1]), lambda s: (0, s, 0)) for r, w in zip(rows, ws)]
    out_specs, out_shape = [], []
    for b, w in zip(blk, ws):
        out_specs += [b] * 4
        out_shape += [jax.ShapeDtypeStruct(w.shape, F32)] * 4
    flat = pl.pallas_call(
        body, name=name, grid=(steps,),
        in_specs=blk + pblk + blk + blk, out_specs=out_specs, out_shape=out_shape,
        compiler_params=_params(("parallel",)),
    )(*ws, *parts, *ms, *vs)
    return [flat[4 * i:4 * i + 4] for i in range(n)]


_SMALL_SLOTS = (("norm_mix_g", 0, 1, 1024), ("norm_ffn_g", 1, 1, 1024), ("norm_final_g", 2, 1, 1024),
                ("lower_bounds", 3, 2, 512), ("hg_norm_g", 5, 1, 128), ("loss", 6, 1, 128), ("conv_w", 8, 3, 512))
_SMALL_PARAMS = tuple(s for s in _SMALL_SLOTS if s[0] != "loss")
CONV_SHARD = CONV_WIDTH // N_DEV


def _small_pack(small):
    def body(*refs):
        out = refs[-1]
        out[...] = jnp.zeros_like(out)
        for ref, (_, row, rows, lanes) in zip(refs[:-1], _SMALL_SLOTS):
            out[row:row + rows, 0:lanes] = ref[...]

    vmem = pl.BlockSpec(memory_space=pltpu.VMEM)
    return pl.pallas_call(
        body, name="small_pack", in_specs=[vmem] * len(_SMALL_SLOTS), out_specs=vmem,
        out_shape=jax.ShapeDtypeStruct((SMALL_ROWS, 1024), F32),
    )(*[small[name] for name, _, _, _ in _SMALL_SLOTS])


def _small_update(gathered, dev, w, m, v):
    n = len(_SMALL_PARAMS)

    def body(dev_ref, g_ref, *refs):
        w_refs, m_refs, v_refs = refs[:n], refs[n:2 * n], refs[2 * n:3 * n]
        loss_ref, out_refs, sum_scr = refs[3 * n], refs[3 * n + 1:-1], refs[-1]
        total = g_ref[0]
        for k in range(1, N_DEV):
            total = total + g_ref[k]
        sum_scr[...] = total
        loss_ref[...] = sum_scr[6:7, 0:128]
        for p, (name, row, rows, lanes) in enumerate(_SMALL_PARAMS):
            if name == "conv_w":
                for r in range(rows):
                    g = sum_scr[row + r:row + r + 1, 0:CONV_SHARD]
                    for s in range(1, N_DEV):
                        mine = sum_scr[row + r:row + r + 1, s * CONV_SHARD:(s + 1) * CONV_SHARD]
                        g = jnp.where(dev_ref[0] == s, mine, g)
                    delta, m_new, v_new = _adamw_math(w_refs[p][r], g, m_refs[p][r], v_refs[p][r])
                    out_refs[4 * p][r] = g
                    out_refs[4 * p + 1][r] = delta
                    out_refs[4 * p + 2][r] = m_new
                    out_refs[4 * p + 3][r] = v_new
                continue
            g = sum_scr[row:row + rows, 0:lanes]
            delta, m_new, v_new = _adamw_math(w_refs[p][...], g, m_refs[p][...], v_refs[p][...])
            out_refs[4 * p][...] = g
            out_refs[4 * p + 1][...] = delta
            out_refs[4 * p + 2][...] = m_new
            out_refs[4 * p + 3][...] = v_new

    vmem = pl.BlockSpec(memory_space=pltpu.VMEM)
    outs = [jax.ShapeDtypeStruct((1, 128), F32)]
    for a in w:
        outs += [jax.ShapeDtypeStruct(a.shape, F32)] * 4
    return pl.pallas_call(
        body, name="small_update",
        in_specs=[pl.BlockSpec(memory_space=pltpu.SMEM)] + [vmem] * (1 + 3 * n), out_specs=[vmem] * len(outs),
        out_shape=outs, scratch_shapes=[pltpu.VMEM((SMALL_ROWS, 1024), F32)],
    )(dev, gathered, *w, *m, *v)


def _row_steps(rows):
    for steps in (4, 2):
        if rows % (16 * steps) == 0:
            return steps
    return 1


def _pair_sum(name, by_owner, got, core, after=()):
    n = len(got)

    def body(core_ref, *refs):
        for a_ref, b_ref, o_ref in zip(refs[:n], refs[n:2 * n], refs[2 * n:]):
            o_ref[...] = (a_ref[...].astype(F32) + b_ref[...].astype(F32)).astype(BF16)

    def blk(g):
        return pl.BlockSpec((None,) + g.shape[1:], lambda k, core_ref: (k, 0, 0))

    def mine(g):
        return pl.BlockSpec((None,) + g.shape[1:], lambda k, core_ref: (2 * k + core_ref[0], 0, 0))

    return pl.pallas_call(
        _drop_operands(body, 1 + 2 * n, len(after)), name=name,
        grid_spec=pltpu.PrefetchScalarGridSpec(
            num_scalar_prefetch=1, grid=(4,),
            in_specs=[mine(g) for g in got] + [blk(g) for g in got] + [HBM_SPEC] * len(after),
            out_specs=[blk(g) for g in got]),
        out_shape=[jax.ShapeDtypeStruct(g.shape, BF16) for g in got],
        compiler_params=_params(("parallel",)),
    )(core, *by_owner, *got, *after)


MESH = pl.DeviceIdType.MESH
HBM_SPEC = pl.BlockSpec(memory_space=pl.ANY)


def _handshake(peers):
    barrier = pltpu.get_barrier_semaphore()
    for peer in peers:
        pl.semaphore_signal(barrier, inc=1, device_id=peer, device_id_type=MESH)
    pl.semaphore_wait(barrier, len(peers))


def _comm_call(body, name, operands, out_shape, scratch, collective_id):
    if collective_id is None:
        return pl.pallas_call(body, name=name, in_specs=[HBM_SPEC] * len(operands), out_specs=[HBM_SPEC] * len(out_shape),
                              out_shape=out_shape, scratch_shapes=scratch)(*operands)
    return pl.kernel(body, out_type=out_shape, mesh=plsc.ScalarSubcoreMesh(axis_name="sequencer", num_cores=1),
                     scratch_types=scratch, name=name,
                     compiler_params=pltpu.CompilerParams(collective_id=collective_id))(*operands)


def _all_gather(name, blocks, collective_id=None, after=(), pieces=None):
    n = len(blocks)
    na = len(after)
    pieces = pieces or [1] * n
    parts = []
    for i, (b, k) in enumerate(zip(blocks, pieces)):
        rows, rem = divmod(b.shape[0], k)
        assert rem == 0 and (k == 1 or rows % 16 == 0), (b.shape, k)
        parts += [(i, None, None)] if k == 1 else [(i, j * rows, rows) for j in range(k)]
    np_ = len(parts)

    def body(*refs):
        x_refs, out_refs = refs[:n], refs[n + na:2 * n + na]
        send_sems, recv_sems, local_sems = refs[2 * n + na:]
        x, y, c = lax.axis_index("x"), lax.axis_index("y"), lax.axis_index("c")
        me, sibling = (x, y, c), (x, y, 1 - c)
        chips = [(1 - x, y), (x, 1 - y), (1 - x, 1 - y)]
        if collective_id is not None:
            _handshake([sibling] + [(*chip, c) for chip in chips])

        def slot(p, px, py, pc):
            i, r0, rows = parts[p]
            whole = out_refs[i].at[4 * px + 2 * py + pc]
            return whole if r0 is None else whole.at[pl.ds(r0, rows)]

        def own(p):
            i, r0, rows = parts[p]
            return x_refs[i] if r0 is None else x_refs[i].at[pl.ds(r0, rows)]

        def copy(p, k, blk, to, src=None):
            return pltpu.make_async_remote_copy(
                src_ref=slot(p, *blk) if src is None else src, dst_ref=slot(p, *blk),
                send_sem=send_sems.at[7 * p + k], recv_sem=recv_sems.at[7 * p + k], device_id=to, device_id_type=MESH)

        mine = [pltpu.make_async_copy(own(p), slot(p, *me), local_sems.at[p]) for p in range(np_)]
        for cp in mine:
            cp.start()
        first = []
        for p in range(np_):
            first.append(copy(p, 0, me, sibling, src=own(p)))
            first += [copy(p, 1 + j, me, (*chip, c), src=own(p)) for j, chip in enumerate(chips)]
        for cp in first:
            cp.start()
        passed = []
        for p in range(np_):
            for j, chip in enumerate(chips):
                copy(p, 1 + j, (*chip, c), me).wait_recv()
                passed.append(copy(p, 4 + j, (*chip, c), sibling))
                passed[-1].start()
        for p in range(np_):
            copy(p, 0, sibling, me).wait_recv()
            for j, chip in enumerate(chips):
                copy(p, 4 + j, (*chip, 1 - c), me).wait_recv()
        for cp in first + passed:
            cp.wait_send()
        for cp in mine:
            cp.wait()

    return _comm_call(
        body, name, list(blocks) + list(after), [jax.ShapeDtypeStruct((N_DEV,) + b.shape, b.dtype) for b in blocks],
        [pltpu.SemaphoreType.DMA((7 * np_,)), pltpu.SemaphoreType.DMA((7 * np_,)), pltpu.SemaphoreType.DMA((np_,))],
        collective_id)


def _sibling_swap(name, by_owner, collective_id=None, after=()):
    n = len(by_owner)
    na = len(after)

    def body(*refs):
        x_refs, out_refs = refs[:n], refs[n + na:2 * n + na]
        send_sems, recv_sems = refs[2 * n + na:]
        x, y, c = lax.axis_index("x"), lax.axis_index("y"), lax.axis_index("c")
        if collective_id is not None:
            _handshake([(x, y, 1 - c)])
        copies = []
        for i in range(n):
            for k in range(4):
                copies.append(pltpu.make_async_remote_copy(
                    src_ref=x_refs[i].at[2 * k + 1 - c], dst_ref=out_refs[i].at[k],
                    send_sem=send_sems.at[4 * i + k], recv_sem=recv_sems.at[4 * i + k],
                    device_id=(x, y, 1 - c), device_id_type=MESH))
        for cp in copies:
            cp.start()
        for cp in copies:
            cp.wait()

    return _comm_call(
        body, name, list(by_owner) + list(after),
        [jax.ShapeDtypeStruct((4,) + b.shape[1:], b.dtype) for b in by_owner],
        [pltpu.SemaphoreType.DMA((4 * n,)), pltpu.SemaphoreType.DMA((4 * n,))], collective_id)


def _chip_exchange(name, sums, collective_id=None, after=()):
    n = len(sums)
    na = len(after)

    def body(*refs):
        x_refs, out_refs = refs[:n], refs[n + na:2 * n + na]
        send_sems, recv_sems, local_sems = refs[2 * n + na:]
        x, y, c = lax.axis_index("x"), lax.axis_index("y"), lax.axis_index("c")
        chips = [(1 - x, y), (x, 1 - y), (1 - x, 1 - y)]
        my_chip = 2 * x + y
        if collective_id is not None:
            _handshake([(cx, cy, c) for cx, cy in chips])
        mine = [pltpu.make_async_copy(x_refs[i].at[my_chip], out_refs[i].at[my_chip], local_sems.at[i])
                for i in range(n)]
        for cp in mine:
            cp.start()
        sends = []
        for i in range(n):
            for j, (cx, cy) in enumerate(chips):
                sends.append(pltpu.make_async_remote_copy(
                    src_ref=x_refs[i].at[2 * cx + cy], dst_ref=out_refs[i].at[my_chip],
                    send_sem=send_sems.at[3 * i + j], recv_sem=recv_sems.at[3 * i + j],
                    device_id=(cx, cy, c), device_id_type=MESH))
        for cp in sends:
            cp.start()
        for i in range(n):
            for j, (cx, cy) in enumerate(chips):
                pltpu.make_async_remote_copy(
                    src_ref=x_refs[i].at[my_chip], dst_ref=out_refs[i].at[2 * cx + cy],
                    send_sem=send_sems.at[3 * i + j], recv_sem=recv_sems.at[3 * i + j],
                    device_id=(cx, cy, c), device_id_type=MESH).wait_recv()
        for cp in sends:
            cp.wait_send()
        for cp in mine:
            cp.wait()

    return _comm_call(
        body, name, list(sums) + list(after), [jax.ShapeDtypeStruct(s.shape, s.dtype) for s in sums],
        [pltpu.SemaphoreType.DMA((3 * n,)), pltpu.SemaphoreType.DMA((3 * n,)), pltpu.SemaphoreType.DMA((n,))],
        collective_id)


def _cast_shards(name, shards):
    n = len(shards)
    steps = min(_row_steps(s.shape[0]) for s in shards)

    def body(*refs):
        for i in range(n):
            refs[n + i][...] = refs[i][...].astype(BF16)

    blocks = [pl.BlockSpec((s.shape[0] // steps, s.shape[1]), lambda i: (i, 0)) for s in shards]
    return pl.pallas_call(
        body, name=name, grid=(steps,), in_specs=blocks, out_specs=blocks,
        out_shape=[jax.ShapeDtypeStruct(s.shape, BF16) for s in shards],
        compiler_params=_params(("parallel",)),
    )(*shards)


BIG = ("w_in", "w_branch_a", "w_branch_b", "w_out", "w_ffn_gate", "w_ffn_up", "w_ffn_down")


def _local_step(x, target, gains, low, conv_w, wg8, reduce):
    g_mix, g_hg, g_ffn, g_fin = gains
    w_in = wg8["w_in"].reshape(N_IN, D_MODEL)
    wg = wg8["w_ffn_gate"].reshape(D_FF, D_MODEL)
    wu = wg8["w_ffn_up"].reshape(D_FF, D_MODEL)
    wa, wb = wg8["w_branch_a"], wg8["w_branch_b"]
    wo = wg8["w_out"].reshape(D_MODEL, D_MODEL)
    wd = wg8["w_ffn_down"].reshape(D_FF, D_MODEL)

    ht, hg, cv, gt, cvo, cvot, cy, o, og, ogt, st = _fwd_in(x, g_mix, w_in, conv_w, low, g_hg)
    x1, mgt, ya, yb = _merge_fwd(og, cvo, gt, x, wa, wb, wo)
    h2t, gate, up, act, loss, d_gfin, dx2, dx2t = _ffn_fwd_loss(x1, g_ffn, wg, wu, wd, target, g_fin)

    dgate, dup, dx1, d_gffn = _ffn_bwd(dx2, x1, gate, up, g_ffn, wg, wu, wd)
    d_wg, d_wu, d_wd = _wgrad_ffn(h2t, dgate, dup, dx2t, act)
    by_owner_ffn = lambda a: a.reshape(N_DEV, D_FF // N_DEV, D_MODEL)
    late = ("w_ffn_gate", "w_ffn_up")
    ffn = dict(w_ffn_gate=by_owner_ffn(d_wg), w_ffn_up=by_owner_ffn(d_wu), w_ffn_down=by_owner_ffn(d_wd))
    got_ffn = reduce.swap(ffn)
    dgt, dya, dyb, dog, dcv, d_conv = _merge_bwd(dx1, ya, yb, gt, cv, cy, wa, wb, wo, conv_w)
    grad_a, grad_b, grad_o, *sums_ffn = _wgrad_out_branches(ogt, dya, cvot, dyb, mgt, dx1, list(ffn.values()), got_ffn,
                                                           reduce.core)
    out = dict(w_out=grad_o.reshape(N_DEV, D_MODEL // N_DEV, D_MODEL), w_branch_a=grad_a, w_branch_b=grad_b)
    parts_ffn, _ = reduce.finish({n: ffn[n] for n in late}, sums_ffn[:2], defer=late)
    dhg, d_low, d_ghg = _hg_bwd(dog, hg, o, st, low, g_hg, after=list(sums_ffn))
    sums_out, got_out = reduce.begin(out, sum_after=[dhg])
    parts_out, updated_out = reduce.finish(dict(out, w_ffn_down=ffn["w_ffn_down"]), list(sums_out) + [sums_ffn[2]])
    dparts = [dhg, dcv, dgt]
    d_w_in_t, ridden = _wgrad_in(ht, dparts, after=sums_out[:1],
                                 riders=reduce.riders(late, dict(zip(late, parts_ffn))))
    reduce.record(late, ridden)
    w_in_grad = dict(w_in=d_w_in_t.reshape(N_DEV, N_IN // N_DEV, D_MODEL))
    sums_in, _ = reduce.begin(w_in_grad, after=parts_out[:1], sum_after=updated_out)
    parts_in, _ = reduce.finish(w_in_grad, sums_in)
    grad_x, d_gmix = _in_bwd(dparts, w_in, x, dx1, g_mix, after=list(parts_out[:1]) + list(sums_in))
    small = dict(norm_mix_g=d_gmix, norm_ffn_g=d_gffn, norm_final_g=d_gfin, lower_bounds=d_low, hg_norm_g=d_ghg,
                 conv_w=d_conv, loss=loss)
    return grad_x, small, parts_in


def kernel(x, norm_mix_g, w_in, lower_bounds, hg_norm_g, conv_w, w_branch_a, w_branch_b, w_out, norm_ffn_g, w_ffn_gate, w_ffn_up, w_ffn_down, norm_final_g, loss_target, m_norm_mix_g, m_w_in, m_lower_bounds, m_hg_norm_g, m_conv_w, m_w_branch_a, m_w_branch_b, m_w_out, m_norm_ffn_g, m_w_ffn_gate, m_w_ffn_up, m_w_ffn_down, m_norm_final_g, v_norm_mix_g, v_w_in, v_lower_bounds, v_hg_norm_g, v_conv_w, v_w_branch_a, v_w_branch_b, v_w_out, v_norm_ffn_g, v_w_ffn_gate, v_w_ffn_up, v_w_ffn_down, v_norm_final_g):
    cx, cy, cc = lax.axis_index("x"), lax.axis_index("y"), lax.axis_index("c")
    my_dev = 4 * cx + 2 * cy + cc

    def tr(a):
        return a[0].T

    big = dict(w_in=tr(w_in), w_branch_a=w_branch_a[0], w_branch_b=w_branch_b[0], w_out=w_out[0],
               w_ffn_gate=tr(w_ffn_gate), w_ffn_up=tr(w_ffn_up), w_ffn_down=w_ffn_down[0])
    big_m = dict(w_in=tr(m_w_in), w_branch_a=m_w_branch_a[0], w_branch_b=m_w_branch_b[0], w_out=m_w_out[0],
                 w_ffn_gate=tr(m_w_ffn_gate), w_ffn_up=tr(m_w_ffn_up), w_ffn_down=m_w_ffn_down[0])
    big_v = dict(w_in=tr(v_w_in), w_branch_a=v_w_branch_a[0], w_branch_b=v_w_branch_b[0], w_out=v_w_out[0],
                 w_ffn_gate=tr(v_w_ffn_gate), w_ffn_up=tr(v_w_ffn_up), w_ffn_down=v_w_ffn_down[0])
    transposed = ("w_in", "w_ffn_gate", "w_ffn_up")

    ids = iter(range(1, 16))
    shards = dict(zip(BIG[:1], _cast_shards("cast_w_in", [big["w_in"]])))
    first = _all_gather("gather_w_in", [shards["w_in"], conv_w.transpose(1, 0, 2)], collective_id=next(ids),
                        pieces=[4, 1])
    shards.update(zip(BIG[1:], _cast_shards("cast_shards", [big[n] for n in BIG[1:]])))
    mid = _all_gather("gather_mid", [shards[n] for n in BIG[1:4]], collective_id=next(ids))
    ffn = _all_gather("gather_ffn", [shards[n] for n in BIG[4:]], collective_id=next(ids), pieces=[2, 2, 2])
    wg8 = dict(zip(BIG, [first[0]] + list(mid) + list(ffn)))
    conv_full = first[1].transpose(1, 2, 0, 3).reshape(3, CONV_WIDTH)

    core = cc.reshape(1).astype(jnp.int32)
    outs = {}

    class Reduce:
        core = cc.reshape(1).astype(jnp.int32)

        @staticmethod
        def swap(grads, after=()):
            names = list(grads)
            return _sibling_swap("sibling_swap_" + names[0], [grads[n] for n in names], collective_id=next(ids),
                                 after=after)

        @staticmethod
        def begin(grads, after=(), sum_after=()):
            got = Reduce.swap(grads, after)
            sums = _pair_sum("pair_sum_" + list(grads)[0], list(grads.values()), got, core, after=sum_after)
            return sums, got

        @staticmethod
        def finish(grads, chip_sums, after=(), defer=()):
            names = list(grads)
            parts = _chip_exchange("chip_exchange_" + names[0], chip_sums, collective_id=next(ids), after=after)
            now = [n for n in names if n not in defer]
            if now:
                updated = _adamw_sum("adamw_" + now[0], [big[n] for n in now],
                                     [p for n, p in zip(names, parts) if n in now],
                                     [big_m[n] for n in now], [big_v[n] for n in now])
                outs.update(zip(now, updated))
            return parts, [outs[n][1] for n in now]

        @staticmethod
        def riders(names, parts):
            return [(big[n], parts[n], big_m[n], big_v[n]) for n in names]

        @staticmethod
        def record(names, updated):
            outs.update(zip(names, updated))

    gains = (norm_mix_g, hg_norm_g, norm_ffn_g, norm_final_g.reshape(1, D_MODEL))
    grad_x, small, last = _local_step(x[0], loss_target[0], gains, lower_bounds, conv_full, wg8, Reduce)

    small_all = _all_gather("gather_small", [_small_pack(small)], collective_id=next(ids), after=last[:1])

    def small_state(a):
        return [a[0], a[1], a[2].reshape(1, D_MODEL), a[3], a[4], a[5].transpose(1, 0, 2)]

    upd = _small_update(
        small_all[0], my_dev.reshape(1).astype(jnp.int32),
        small_state((norm_mix_g, norm_ffn_g, norm_final_g, lower_bounds, hg_norm_g, conv_w)),
        small_state((m_norm_mix_g, m_norm_ffn_g, m_norm_final_g, m_lower_bounds, m_hg_norm_g, m_conv_w)),
        small_state((v_norm_mix_g, v_norm_ffn_g, v_norm_final_g, v_lower_bounds, v_hg_norm_g, v_conv_w)))
    loss = upd[0][0, 0]
    for p, (name, _, _, _) in enumerate(_SMALL_PARAMS):
        outs[name] = upd[1 + 4 * p:5 + 4 * p]
    outs["norm_final_g"] = [a.reshape(D_MODEL) for a in outs["norm_final_g"]]
    outs["conv_w"] = [a.transpose(1, 0, 2) for a in outs["conv_w"]]

    order = ["norm_mix_g", "w_in", "lower_bounds", "hg_norm_g", "conv_w", "w_branch_a", "w_branch_b", "w_out",
             "norm_ffn_g", "w_ffn_gate", "w_ffn_up", "w_ffn_down", "norm_final_g"]
    result = [loss, grad_x[None]]
    for k in range(4):
        for n in order:
            if n in BIG:
                result.append((outs[n][k].T if n in transposed else outs[n][k])[None])
            else:
                result.append(outs[n][k])
    return tuple(result)
```
